```python
import math
import jax, jax.numpy as jnp
from jax import lax
import numpy as np

D_MODEL = 1024
BATCH = 8
SEQ = 4096
DEPTH = 1

CHUNK = 64
RET_HEADS = 4
RET_HEAD_DIM = D_MODEL // 8
RET_WIDTH = RET_HEADS * RET_HEAD_DIM
POOL_WINDOWS = (2, 4, 8, 16)
POOL_GROUPS = len(POOL_WINDOWS)
POOL_GROUP_DIM = D_MODEL // 8
POOL_WIDTH = POOL_GROUPS * POOL_GROUP_DIM
MIX_WIDTH = RET_WIDTH + POOL_WIDTH
IN_WIDTH = 4 * RET_WIDTH + POOL_WIDTH
D_FF = ((8 * D_MODEL // 3 + 127) // 128) * 128
CONV_WIDTH = 3
ROPE_BASE = 10000.0
LN_EPS = 1e-5
RMS_EPS = 1e-6
DEEPNORM_ALPHA = (2.0 * DEPTH) ** 0.25
DEEPNORM_BETA = (8.0 * DEPTH) ** -0.25

kernel_name = "hybrid_retention_pool_convffn_deepnorm"


def _layernorm(x, g, b):
    xf = x.astype(jnp.float32)
    mu = jnp.mean(xf, axis=-1, keepdims=True)
    var = jnp.mean(jnp.square(xf - mu), axis=-1, keepdims=True)
    y = (xf - mu) * lax.rsqrt(var + LN_EPS) * g.astype(jnp.float32) + b.astype(jnp.float32)
    return y.astype(x.dtype)


def _rope(t):
    s, dh = t.shape[1], t.shape[-1]
    inv_freq = ROPE_BASE ** (-jnp.arange(0, dh, 2, dtype=jnp.float32) / dh)
    ang = jnp.arange(s, dtype=jnp.float32)[:, None] * inv_freq[None, :]
    cos = jnp.cos(ang)[None, :, None, :]
    sin = jnp.sin(ang)[None, :, None, :]
    tf = t.astype(jnp.float32)
    t1, t2 = tf[..., : dh // 2], tf[..., dh // 2:]
    return jnp.concatenate([t1 * cos - t2 * sin, t1 * sin + t2 * cos], axis=-1)


def _retention(q, k, v):
    b, s, h, dh = q.shape
    nc = s // CHUNK
    log_gamma = jnp.log(1.0 - 2.0 ** (-5.0 - jnp.arange(h, dtype=jnp.float32)))
    idx = jnp.arange(CHUNK, dtype=jnp.float32)
    inner_decay = jnp.exp(log_gamma[:, None, None] * jnp.abs(idx[:, None] - idx[None, :]))
    q_decay = jnp.exp(log_gamma[None, :] * (idx[:, None] + 1.0))
    k_decay = jnp.exp(log_gamma[None, :] * (CHUNK - 1.0 - idx[:, None]))
    chunk_decay = jnp.exp(log_gamma * CHUNK)

    qc = q.reshape(b, nc, CHUNK, h, dh)
    kc = k.reshape(b, nc, CHUNK, h, dh)
    vc = v.reshape(b, nc, CHUNK, h, dh)

    scores = jnp.einsum('bnihd,bnjhd->bnhij', qc, kc) * inner_decay[None, None]
    inner = jnp.einsum('bnhij,bnjhe->bnihe', scores, vc)

    def step(state, inp):
        q_n, k_n, v_n = inp
        cross = jnp.einsum('bihd,bhde->bihe', q_n * q_decay[None, :, :, None], state)
        new_state = state * chunk_decay[None, :, None, None] + jnp.einsum(
            'bjhd,bjhe->bhde', k_n * k_decay[None, :, :, None], v_n)
        return new_state, cross

    state0 = jnp.zeros((b, h, dh, dh), jnp.float32)
    xs = (jnp.moveaxis(qc, 1, 0), jnp.moveaxis(kc, 1, 0), jnp.moveaxis(vc, 1, 0))
    _, cross = lax.scan(step, state0, xs)
    out = inner + jnp.moveaxis(cross, 0, 1)
    return out.reshape(b, s, h, dh)


def _pool_mixer(p, w_pool, pool_scale):
    b, s, _ = p.shape
    pf = p.astype(jnp.float32).reshape(b, s, POOL_GROUPS, POOL_GROUP_DIM)
    cs = jnp.cumsum(pf, axis=1)
    pos = jnp.arange(1, s + 1, dtype=jnp.float32)
    outs = []
    for gi, w in enumerate(POOL_WINDOWS):
        c = cs[:, :, gi]
        prev = jnp.pad(c, ((0, 0), (w, 0), (0, 0)))[:, :s]
        mean = (c - prev) / jnp.minimum(pos, float(w))[None, :, None]
        outs.append(mean - pf[:, :, gi])
    pooled = jnp.stack(outs, axis=2).astype(p.dtype)
    y = jnp.einsum('bsgc,gcd->bsgd', pooled, w_pool).reshape(b, s, POOL_WIDTH)
    return y * pool_scale


def _conv_ffn(x, w_up, conv_w, conv_b, w_down):
    s = x.shape[1]
    u = x @ w_up
    val, gate = u[..., :D_FF], u[..., D_FF:]
    gp = jnp.pad(gate, ((0, 0), (CONV_WIDTH - 1, 0), (0, 0)))
    h = conv_b + sum(gp[:, j:j + s] * conv_w[j] for j in range(CONV_WIDTH))
    return (jax.nn.silu(h) * val) @ w_down


def _fwd_setup_inputs(seed: int = 0) -> dict:
    key = jax.random.key(seed)
    ks = jax.random.split(key, 20)
    f32 = jnp.float32
    L = DEPTH
    x = jax.random.normal(ks[0], (BATCH, SEQ, D_MODEL), f32)
    sd = D_MODEL ** -0.5
    w_qk = jax.random.normal(ks[1], (L, D_MODEL, 2 * RET_WIDTH), f32) * sd
    w_v = jax.random.normal(ks[2], (L, D_MODEL, RET_WIDTH), f32) * sd * DEEPNORM_BETA
    w_g = jax.random.normal(ks[3], (L, D_MODEL, RET_WIDTH), f32) * sd
    w_p = jax.random.normal(ks[4], (L, D_MODEL, POOL_WIDTH), f32) * sd * DEEPNORM_BETA
    w_in = jnp.concatenate([w_qk, w_v, w_g, w_p], axis=-1)
    w_pool = jax.random.normal(ks[5], (L, POOL_GROUPS, POOL_GROUP_DIM, POOL_GROUP_DIM), f32) * POOL_GROUP_DIM ** -0.5
    pool_scale = 1.0 + 0.1 * jax.random.normal(ks[6], (L, POOL_WIDTH), f32)
    w_out = jax.random.normal(ks[7], (L, MIX_WIDTH, D_MODEL), f32) * MIX_WIDTH ** -0.5 * DEEPNORM_BETA
    ln1_g = 1.0 + 0.05 * jax.random.normal(ks[8], (L, D_MODEL), f32)
    ln1_b = 0.02 * jax.random.normal(ks[9], (L, D_MODEL), f32)
    w_up = jax.random.normal(ks[10], (L, D_MODEL, 2 * D_FF), f32) * sd * DEEPNORM_BETA
    conv_w = jax.random.normal(ks[11], (L, CONV_WIDTH, D_FF), f32) * CONV_WIDTH ** -0.5
    conv_b = 0.02 * jax.random.normal(ks[12], (L, D_FF), f32)
    w_down = jax.random.normal(ks[13], (L, D_FF, D_MODEL), f32) * D_FF ** -0.5 * DEEPNORM_BETA
    ln2_g = 1.0 + 0.05 * jax.random.normal(ks[14], (L, D_MODEL), f32)
    ln2_b = 0.02 * jax.random.normal(ks[15], (L, D_MODEL), f32)
    return {"x": x, "w_in": w_in, "w_pool": w_pool, "pool_scale": pool_scale, "w_out": w_out,
            "ln1_g": ln1_g, "ln1_b": ln1_b, "w_up": w_up, "conv_w": conv_w, "conv_b": conv_b,
            "w_down": w_down, "ln2_g": ln2_g, "ln2_b": ln2_b}


def _fwd_reference(x, w_in, w_pool, pool_scale, w_out, ln1_g, ln1_b, w_up, conv_w, conv_b,
              w_down, ln2_g, ln2_b):
    b, s, _ = x.shape
    for l in range(DEPTH):
        proj = x @ w_in[l]
        q, k, v, g, p = jnp.split(proj, [RET_WIDTH, 2 * RET_WIDTH, 3 * RET_WIDTH, 4 * RET_WIDTH], axis=-1)
        q = _rope(q.reshape(b, s, RET_HEADS, RET_HEAD_DIM))
        k = _rope(k.reshape(b, s, RET_HEADS, RET_HEAD_DIM)) * (RET_HEAD_DIM ** -0.5)
        v = v.reshape(b, s, RET_HEADS, RET_HEAD_DIM).astype(jnp.float32)
        ret = _retention(q, k, v)
        ret = ret * lax.rsqrt(jnp.mean(jnp.square(ret), axis=-1, keepdims=True) + RMS_EPS)
        ret = ret.reshape(b, s, RET_WIDTH).astype(x.dtype) * jax.nn.silu(g)
        pool = _pool_mixer(p, w_pool[l], pool_scale[l])
        mix = jnp.concatenate([ret, pool], axis=-1) @ w_out[l]
        x = _layernorm(DEEPNORM_ALPHA * x + mix, ln1_g[l], ln1_b[l])
        ffn = _conv_ffn(x, w_up[l], conv_w[l], conv_b[l], w_down[l])
        x = _layernorm(DEEPNORM_ALPHA * x + ffn, ln2_g[l], ln2_b[l])
    return x


import jax as _jax
import jax.numpy as _jnp

TWIN_FORMAT = 'train_step'
FWD_PARAMS = ['x', 'w_in', 'w_pool', 'pool_scale', 'w_out', 'ln1_g', 'ln1_b', 'w_up', 'conv_w', 'conv_b', 'w_down', 'ln2_g', 'ln2_b']
TWIN_WEIGHTS = ['w_in', 'w_pool', 'pool_scale', 'w_out', 'ln1_g', 'ln1_b', 'w_up', 'conv_w', 'conv_b', 'w_down', 'ln2_g', 'ln2_b']
TWIN_DIFF_INPUT = 'x'
TWIN_INPUTS = ['x', 'w_in', 'w_pool', 'pool_scale', 'w_out', 'ln1_g', 'ln1_b', 'w_up', 'conv_w', 'conv_b', 'w_down', 'ln2_g', 'ln2_b', 'loss_target', 'm_w_in', 'm_w_pool', 'm_pool_scale', 'm_w_out', 'm_ln1_g', 'm_ln1_b', 'm_w_up', 'm_conv_w', 'm_conv_b', 'm_w_down', 'm_ln2_g', 'm_ln2_b', 'v_w_in', 'v_w_pool', 'v_pool_scale', 'v_w_out', 'v_ln1_g', 'v_ln1_b', 'v_w_up', 'v_conv_w', 'v_conv_b', 'v_w_down', 'v_ln2_g', 'v_ln2_b']
TWIN_OUTPUTS = ['loss', 'grad_x', 'grad_w_in', 'grad_w_pool', 'grad_pool_scale', 'grad_w_out', 'grad_ln1_g', 'grad_ln1_b', 'grad_w_up', 'grad_conv_w', 'grad_conv_b', 'grad_w_down', 'grad_ln2_g', 'grad_ln2_b', 'delta_w_in', 'delta_w_pool', 'delta_pool_scale', 'delta_w_out', 'delta_ln1_g', 'delta_ln1_b', 'delta_w_up', 'delta_conv_w', 'delta_conv_b', 'delta_w_down', 'delta_ln2_g', 'delta_ln2_b', 'new_m_w_in', 'new_m_w_pool', 'new_m_pool_scale', 'new_m_w_out', 'new_m_ln1_g', 'new_m_ln1_b', 'new_m_w_up', 'new_m_conv_w', 'new_m_conv_b', 'new_m_w_down', 'new_m_ln2_g', 'new_m_ln2_b', 'new_v_w_in', 'new_v_w_pool', 'new_v_pool_scale', 'new_v_w_out', 'new_v_ln1_g', 'new_v_ln1_b', 'new_v_w_up', 'new_v_conv_w', 'new_v_conv_b', 'new_v_w_down', 'new_v_ln2_g', 'new_v_ln2_b']
TWIN_LEAF_KINDS = {'loss': 'loss', 'grad_x': 'grad_x', 'grad_w_in': 'grad_w', 'grad_w_pool': 'grad_w', 'grad_pool_scale': 'grad_w', 'grad_w_out': 'grad_w', 'grad_ln1_g': 'grad_w', 'grad_ln1_b': 'grad_w', 'grad_w_up': 'grad_w', 'grad_conv_w': 'grad_w', 'grad_conv_b': 'grad_w', 'grad_w_down': 'grad_w', 'grad_ln2_g': 'grad_w', 'grad_ln2_b': 'grad_w', 'delta_w_in': 'delta_w', 'delta_w_pool': 'delta_w', 'delta_pool_scale': 'delta_w', 'delta_w_out': 'delta_w', 'delta_ln1_g': 'delta_w', 'delta_ln1_b': 'delta_w', 'delta_w_up': 'delta_w', 'delta_conv_w': 'delta_w', 'delta_conv_b': 'delta_w', 'delta_w_down': 'delta_w', 'delta_ln2_g': 'delta_w', 'delta_ln2_b': 'delta_w', 'new_m_w_in': 'new_m', 'new_m_w_pool': 'new_m', 'new_m_pool_scale': 'new_m', 'new_m_w_out': 'new_m', 'new_m_ln1_g': 'new_m', 'new_m_ln1_b': 'new_m', 'new_m_w_up': 'new_m', 'new_m_conv_w': 'new_m', 'new_m_conv_b': 'new_m', 'new_m_w_down': 'new_m', 'new_m_ln2_g': 'new_m', 'new_m_ln2_b': 'new_m', 'new_v_w_in': 'new_v', 'new_v_w_pool': 'new_v', 'new_v_pool_scale': 'new_v', 'new_v_w_out': 'new_v', 'new_v_ln1_g': 'new_v', 'new_v_ln1_b': 'new_v', 'new_v_w_up': 'new_v', 'new_v_conv_w': 'new_v', 'new_v_conv_b': 'new_v', 'new_v_w_down': 'new_v', 'new_v_ln2_g': 'new_v', 'new_v_ln2_b': 'new_v'}


def _forward(args):
    return _fwd_reference(*[args[k] for k in FWD_PARAMS])


def _output_shape():
    out = _jax.eval_shape(lambda: _forward(_fwd_setup_inputs(0)))
    return out.shape, out.dtype

N_MICROBATCH = 1
ADAM_LR = 0.001
ADAM_B1 = 0.9
ADAM_B2 = 0.999
ADAM_EPS = 1e-08
ADAM_WD = 0.01
ADAM_STEP = 10
PER_EXAMPLE_BATCH_AXIS = {'x': 0, 'loss_target': 0}
SHARED_INPUTS = []
_WEIGHT_DTYPES = {'w_in': _jnp.float32, 'w_pool': _jnp.float32, 'pool_scale': _jnp.float32, 'w_out': _jnp.float32, 'ln1_g': _jnp.float32, 'ln1_b': _jnp.float32, 'w_up': _jnp.float32, 'conv_w': _jnp.float32, 'conv_b': _jnp.float32, 'w_down': _jnp.float32, 'ln2_g': _jnp.float32, 'ln2_b': _jnp.float32}
MOMENT_SCALE = {'w_in': 6.591695e-02, 'w_pool': 4.657884e-02, 'pool_scale': 4.676556e-02, 'w_out': 8.259978e-02, 'ln1_g': 2.970623e+00, 'ln1_b': 4.588398e-01, 'w_up': 1.831266e-02, 'conv_w': 1.093309e-02, 'conv_b': 1.849506e-02, 'w_down': 2.998194e-02, 'ln2_g': 3.217049e+01, 'ln2_b': 6.301350e-01}


def _to_microbatches(a, axis):
    t = _jnp.moveaxis(a, axis, 0)
    t = t.reshape((N_MICROBATCH, t.shape[0] // N_MICROBATCH) + t.shape[1:])
    return _jnp.moveaxis(t, 1, axis + 1)


def setup_inputs(seed: int = 0) -> dict:
    inp = _fwd_setup_inputs(seed)
    key = _jax.random.fold_in(_jax.random.key(seed), 7919)
    shape, _ = _output_shape()
    out = dict(inp)
    out["loss_target"] = _jax.random.normal(_jax.random.fold_in(key, 0), shape, _jnp.float32)
    for i, name in enumerate(TWIN_WEIGHTS):
        w = inp[name].astype(_jnp.float32)
        if MOMENT_SCALE is None:
            s = _jnp.sqrt(_jnp.mean(_jnp.square(w)) + 1e-30)
        else:
            s = MOMENT_SCALE[name]
        km, kv = _jax.random.split(_jax.random.fold_in(key, i + 1))
        out[name] = w
        out["m_" + name] = s * _jax.random.normal(km, w.shape, _jnp.float32)
        out["v_" + name] = (s * s) * _jax.random.uniform(kv, w.shape, _jnp.float32, 0.5, 1.5)
    if N_MICROBATCH > 1:
        for name, axis in PER_EXAMPLE_BATCH_AXIS.items():
            out[name] = _to_microbatches(out[name], axis)
    return {'x': out['x'], 'w_in': out['w_in'], 'w_pool': out['w_pool'], 'pool_scale': out['pool_scale'], 'w_out': out['w_out'], 'ln1_g': out['ln1_g'], 'ln1_b': out['ln1_b'], 'w_up': out['w_up'], 'conv_w': out['conv_w'], 'conv_b': out['conv_b'], 'w_down': out['w_down'], 'ln2_g': out['ln2_g'], 'ln2_b': out['ln2_b'], 'loss_target': out['loss_target'], 'm_w_in': out['m_w_in'], 'm_w_pool': out['m_w_pool'], 'm_pool_scale': out['m_pool_scale'], 'm_w_out': out['m_w_out'], 'm_ln1_g': out['m_ln1_g'], 'm_ln1_b': out['m_ln1_b'], 'm_w_up': out['m_w_up'], 'm_conv_w': out['m_conv_w'], 'm_conv_b': out['m_conv_b'], 'm_w_down': out['m_w_down'], 'm_ln2_g': out['m_ln2_g'], 'm_ln2_b': out['m_ln2_b'], 'v_w_in': out['v_w_in'], 'v_w_pool': out['v_w_pool'], 'v_pool_scale': out['v_pool_scale'], 'v_w_out': out['v_w_out'], 'v_ln1_g': out['v_ln1_g'], 'v_ln1_b': out['v_ln1_b'], 'v_w_up': out['v_w_up'], 'v_conv_w': out['v_conv_w'], 'v_conv_b': out['v_conv_b'], 'v_w_down': out['v_w_down'], 'v_ln2_g': out['v_ln2_g'], 'v_ln2_b': out['v_ln2_b']}


def _loss(weights, diff, rest, loss_target):
    with _jax.named_scope("forward"):
        args = {**rest, TWIN_DIFF_INPUT: diff, **{k: w.astype(_WEIGHT_DTYPES[k]) for k, w in weights.items()}}
        y = _forward(args)
    with _jax.named_scope("loss_head"):
        err = _jnp.square(y.astype(_jnp.float32) - loss_target)
        return 0.5 * _jnp.sum(_jnp.mean(err, axis=-1)) if err.ndim else 0.5 * err


def _adamw(w, g, m, v):
    m = ADAM_B1 * m + (1.0 - ADAM_B1) * g
    v = ADAM_B2 * v + (1.0 - ADAM_B2) * _jnp.square(g)
    m_hat = m / (1.0 - ADAM_B1 ** ADAM_STEP)
    v_hat = v / (1.0 - ADAM_B2 ** ADAM_STEP)
    delta = -ADAM_LR * (m_hat / (_jnp.sqrt(v_hat) + ADAM_EPS) + ADAM_WD * w)
    return delta, m, v


def reference(x, w_in, w_pool, pool_scale, w_out, ln1_g, ln1_b, w_up, conv_w, conv_b, w_down, ln2_g, ln2_b, loss_target, m_w_in, m_w_pool, m_pool_scale, m_w_out, m_ln1_g, m_ln1_b, m_w_up, m_conv_w, m_conv_b, m_w_down, m_ln2_g, m_ln2_b, v_w_in, v_w_pool, v_pool_scale, v_w_out, v_ln1_g, v_ln1_b, v_w_up, v_conv_w, v_conv_b, v_w_down, v_ln2_g, v_ln2_b):
    given = dict(x=x, w_in=w_in, w_pool=w_pool, pool_scale=pool_scale, w_out=w_out, ln1_g=ln1_g, ln1_b=ln1_b, w_up=w_up, conv_w=conv_w, conv_b=conv_b, w_down=w_down, ln2_g=ln2_g, ln2_b=ln2_b, loss_target=loss_target, m_w_in=m_w_in, m_w_pool=m_w_pool, m_pool_scale=m_pool_scale, m_w_out=m_w_out, m_ln1_g=m_ln1_g, m_ln1_b=m_ln1_b, m_w_up=m_w_up, m_conv_w=m_conv_w, m_conv_b=m_conv_b, m_w_down=m_w_down, m_ln2_g=m_ln2_g, m_ln2_b=m_ln2_b, v_w_in=v_w_in, v_w_pool=v_w_pool, v_pool_scale=v_pool_scale, v_w_out=v_w_out, v_ln1_g=v_ln1_g, v_ln1_b=v_ln1_b, v_w_up=v_w_up, v_conv_w=v_conv_w, v_conv_b=v_conv_b, v_w_down=v_w_down, v_ln2_g=v_ln2_g, v_ln2_b=v_ln2_b)
    weights = {n: given[n] for n in TWIN_WEIGHTS}
    shared = {n: given[n] for n in SHARED_INPUTS}
    per_example = {n: given[n] for n in ['x']}
    grad_fn = _jax.value_and_grad(_loss, argnums=(0, 1))

    def one_microbatch(ex, loss_target):
        ex = dict(ex)
        diff = ex.pop(TWIN_DIFF_INPUT)
        return grad_fn(weights, diff, {**shared, **ex}, loss_target)

    if N_MICROBATCH == 1:
        loss, (grad_w, grad_x) = one_microbatch(per_example, given["loss_target"])
    else:
        def body(carry, xs):
            loss_sum, grad_sum = carry
            l_k, (gw_k, gx_k) = one_microbatch(xs[0], xs[1])
            with _jax.named_scope("update"):
                return (loss_sum + l_k, _jax.tree.map(_jnp.add, grad_sum, gw_k)), gx_k

        init = (_jnp.zeros((), _jnp.float32), _jax.tree.map(_jnp.zeros_like, weights))
        (loss, grad_w), grad_x = _jax.lax.scan(body, init, (per_example, given["loss_target"]))
    with _jax.named_scope("update"):
        delta_w, new_m, new_v = {}, {}, {}
        for n in TWIN_WEIGHTS:
            delta_w[n], new_m[n], new_v[n] = _adamw(weights[n], grad_w[n], given["m_" + n], given["v_" + n])
    return (loss, grad_x, *[grad_w[n] for n in TWIN_WEIGHTS], *[delta_w[n] for n in TWIN_WEIGHTS],
            *[new_m[n] for n in TWIN_WEIGHTS], *[new_v[n] for n in TWIN_WEIGHTS])
```

```python
import functools
import math

import numpy as np
import jax
import jax.numpy as jnp
from jax import lax
from jax.experimental import pallas as pl
from jax.experimental.pallas import tpu as pltpu

F32 = jnp.float32
BF16 = jnp.bfloat16

D_MODEL = 1024
HEADS = 4
HEAD_DIM = 128
RET_W = HEADS * HEAD_DIM
POOL_WINDOWS = (2, 4, 8, 16)
POOL_W = 512
IN_W = 4 * RET_W + POOL_W
D_FF = 2816
N_SHARD = 4
IN_SH = IN_W // N_SHARD
UP_SH = 2 * D_FF // N_SHARD
DOWN_SH = D_FF // N_SHARD
OUT_SH = D_MODEL // N_SHARD
ROPE_BASE = 10000.0
LN_EPS = 1e-5
RMS_EPS = 1e-6
ALPHA = 2.0 ** 0.25
K_SCALE = HEAD_DIM ** -0.5
SUPER = 256
CHUNK = 64
POOL_HALO = 16
CONV_HALO = 8

ADAM_LR = 0.001
ADAM_B1 = 0.9
ADAM_B2 = 0.999
ADAM_EPS = 1e-08
ADAM_WD = 0.01
ADAM_STEP = 10

MESH = pl.DeviceIdType.MESH
VMEM_LIMIT = 56 * 1024 * 1024


def _dot(a, b):
    return jnp.dot(a, b, preferred_element_type=F32)


def _dot_nt(a, b):
    return lax.dot_general(a, b, (((1,), (1,)), ((), ())), preferred_element_type=F32)


def _dot_tn(a, b):
    return lax.dot_general(a, b, (((0,), (0,)), ((), ())), preferred_element_type=F32)


def _sigmoid(x):
    return 1.0 / (1.0 + jnp.exp(-x))


def _params(sem):
    return pltpu.CompilerParams(dimension_semantics=sem, vmem_limit_bytes=VMEM_LIMIT)


def _whole():
    return pl.BlockSpec(memory_space=pltpu.VMEM)


def _gammas():
    return [1.0 - 2.0 ** (-5.0 - h) for h in range(HEADS)]


def _decay_tables():
    idx = np.arange(SUPER)
    dist = np.abs(idx[:, None] - idx[None, :]).astype(np.float64)
    visible = (idx[None, :] // CHUNK) <= (idx[:, None] // CHUNK)
    mask = np.stack([np.where(visible, g ** dist, 0.0) for g in _gammas()])
    qd = np.concatenate([np.repeat((g ** (idx + 1.0))[:, None], HEAD_DIM, 1) for g in _gammas()], 1)
    kd = np.concatenate([np.repeat((g ** (SUPER - 1.0 - idx))[:, None], HEAD_DIM, 1) for g in _gammas()], 1)
    return (jnp.asarray(mask, F32), jnp.asarray(qd, F32), jnp.asarray(kd, F32))


def _rope_tables(s):
    inv_freq = ROPE_BASE ** (-np.arange(0, HEAD_DIM, 2, dtype=np.float64) / HEAD_DIM)
    ang = np.arange(s, dtype=np.float64)[:, None] * inv_freq[None, :]
    cos, sin = np.cos(ang), np.sin(ang)
    return (jnp.asarray(np.concatenate([cos, cos], 1), F32),
            jnp.asarray(np.concatenate([-sin, sin], 1), F32))


def _rope(t, cosf, sinf):
    return t * cosf + pltpu.roll(t, HEAD_DIM // 2, 1) * sinf


def _rope_t(t, cosf, sinf):
    return t * cosf - pltpu.roll(t, HEAD_DIM // 2, 1) * sinf


def _layernorm_fwd(z):
    mu = jnp.mean(z, axis=-1, keepdims=True)
    zc = z - mu
    var = jnp.mean(zc * zc, axis=-1, keepdims=True)
    rstd = lax.rsqrt(var + LN_EPS)
    return zc * rstd, rstd


def _layernorm_bwd(dy, xhat, rstd, gain):
    dxh = dy * gain
    m1 = jnp.mean(dxh, axis=-1, keepdims=True)
    m2 = jnp.mean(dxh * xhat, axis=-1, keepdims=True)
    return rstd * (dxh - m1 - xhat * m2)


def _proj_pool(x, win4, cosf, sinf, wpool, pscale, ts):
    s = x.shape[0]
    nt = s // ts

    def body(x_ref, w_ref, cos_ref, sin_ref, wp_ref, ps_ref,
             xb_ref, q_ref, k_ref, v_ref, g_ref, pooled_ref, cat_ref, proj_scr, pext_scr):
        i = pl.program_id(0)
        xb = x_ref[...].astype(BF16)
        xb_ref[...] = xb
        for j in range(N_SHARD):
            proj_scr[:, j * IN_SH:(j + 1) * IN_SH] = _dot(xb, w_ref[j])
        cosf_t = cos_ref[...]
        sinf_t = sin_ref[...]
        for h in range(HEADS):
            lo = h * HEAD_DIM
            q_ref[:, lo:lo + HEAD_DIM] = _rope(proj_scr[:, lo:lo + HEAD_DIM], cosf_t, sinf_t).astype(BF16)
            kk = _rope(proj_scr[:, RET_W + lo:RET_W + lo + HEAD_DIM], cosf_t, sinf_t) * K_SCALE
            k_ref[:, lo:lo + HEAD_DIM] = kk.astype(BF16)
        v_ref[...] = proj_scr[:, 2 * RET_W:3 * RET_W].astype(BF16)
        g_ref[...] = proj_scr[:, 3 * RET_W:4 * RET_W]

        @pl.when(i == 0)
        def _():
            pext_scr[0:POOL_HALO, :] = jnp.zeros((POOL_HALO, POOL_W), F32)

        pext_scr[POOL_HALO:POOL_HALO + ts, :] = proj_scr[:, 4 * RET_W:IN_W]
        pos = (i * ts + lax.broadcasted_iota(jnp.int32, (ts, 1), 0) + 1).astype(F32)
        for gi, w in enumerate(POOL_WINDOWS):
            lo = gi * HEAD_DIM
            ext = pext_scr[:, lo:lo + HEAD_DIM]
            acc = ext
            shift = 1
            while shift < w:
                acc = acc + pltpu.roll(acc, shift, 0)
                shift *= 2
            tok = ext[POOL_HALO:POOL_HALO + ts]
            pooled = acc[POOL_HALO:POOL_HALO + ts] / jnp.minimum(pos, float(w)) - tok
            pooled_b = pooled.astype(BF16)
            pooled_ref[:, lo:lo + HEAD_DIM] = pooled_b
            lin = _dot(pooled_b, wp_ref[gi])
            cat_ref[:, lo:lo + HEAD_DIM] = (lin * ps_ref[:, lo:lo + HEAD_DIM]).astype(BF16)
        pext_scr[0:POOL_HALO, :] = pext_scr[ts:ts + POOL_HALO, :]

    tile = lambda w: pl.BlockSpec((ts, w), lambda i: (i, 0))
    outs = pl.pallas_call(
        body, name="proj_pool", grid=(nt,),
        in_specs=[tile(D_MODEL), _whole(), tile(HEAD_DIM), tile(HEAD_DIM), _whole(), _whole()],
        out_specs=[tile(D_MODEL), tile(RET_W), tile(RET_W), tile(RET_W), tile(RET_W), tile(POOL_W),
                   pl.BlockSpec((ts, POOL_W), lambda i: (i, 1))],
        out_shape=[jax.ShapeDtypeStruct((s, D_MODEL), BF16), jax.ShapeDtypeStruct((s, RET_W), BF16),
                   jax.ShapeDtypeStruct((s, RET_W), BF16), jax.ShapeDtypeStruct((s, RET_W), BF16),
                   jax.ShapeDtypeStruct((s, RET_W), F32), jax.ShapeDtypeStruct((s, POOL_W), BF16),
                   jax.ShapeDtypeStruct((s, 2 * RET_W), BF16)],
        scratch_shapes=[pltpu.VMEM((ts, IN_W), F32), pltpu.VMEM((ts + POOL_HALO, POOL_W), F32)],
        compiler_params=_params(("arbitrary",)),
    )(x, win4, cosf, sinf, wpool, pscale)
    return outs


def _retention_fwd(q, k, v, g, cat, mask, qd, kd):
    s = q.shape[0]
    ns = s // SUPER
    cdec = [gm ** float(SUPER) for gm in _gammas()]

    def body(q_ref, k_ref, v_ref, g_ref, cat_in, mask_ref, qd_ref, kd_ref,
             ret_ref, cat_ref, st_ref, state_scr):
        del cat_in
        n = pl.program_id(0)

        @pl.when(n == 0)
        def _():
            state_scr[...] = jnp.zeros_like(state_scr)

        for h in range(HEADS):
            sl = slice(h * HEAD_DIM, (h + 1) * HEAD_DIM)
            qh, kh, vh = q_ref[:, sl], k_ref[:, sl], v_ref[:, sl]
            sc = _dot_nt(qh, kh) * mask_ref[h]
            st = state_scr[h]
            stb = st.astype(BF16)
            st_ref[0, h] = stb
            qdb = (qh.astype(F32) * qd_ref[:, sl]).astype(BF16)
            kdb = (kh.astype(F32) * kd_ref[:, sl]).astype(BF16)
            ret = _dot(sc.astype(BF16), vh) + _dot(qdb, stb)
            state_scr[h] = st * cdec[h] + _dot_tn(kdb, vh)
            ret_ref[:, sl] = ret
            r = lax.rsqrt(jnp.mean(ret * ret, axis=-1, keepdims=True) + RMS_EPS)
            gh = g_ref[:, sl]
            cat_ref[:, sl] = ((ret * r) * (gh * _sigmoid(gh))).astype(BF16)

    tile = pl.BlockSpec((SUPER, RET_W), lambda n: (n, 0))
    return pl.pallas_call(
        body, name="retention_fwd", grid=(ns,),
        in_specs=[tile, tile, tile, tile, pl.BlockSpec(memory_space=pl.ANY), _whole(), _whole(), _whole()],
        out_specs=[tile, tile, pl.BlockSpec((1, HEADS, HEAD_DIM, HEAD_DIM), lambda n: (n, 0, 0, 0))],
        out_shape=[jax.ShapeDtypeStruct((s, RET_W), F32), jax.ShapeDtypeStruct((s, 2 * RET_W), BF16),
                   jax.ShapeDtypeStruct((ns, HEADS, HEAD_DIM, HEAD_DIM), BF16)],
        scratch_shapes=[pltpu.VMEM((HEADS, HEAD_DIM, HEAD_DIM), F32)],
        input_output_aliases={4: 1},
        compiler_params=_params(("arbitrary",)),
    )(q, k, v, g, cat, mask, qd, kd)


def _outproj_ln1(x, cat, wout, g1, b1, ts):
    s = x.shape[0]

    def body(x_ref, cat_ref, w_ref, g_ref, b_ref, xhat_ref, rstd_ref, h1b_ref):
        z = ALPHA * x_ref[...] + _dot(cat_ref[...], w_ref[...])
        xhat, rstd = _layernorm_fwd(z)
        xhat_ref[...] = xhat
        rstd_ref[...] = rstd
        h1b_ref[...] = (xhat * g_ref[...] + b_ref[...]).astype(BF16)

    tile = lambda w: pl.BlockSpec((ts, w), lambda i: (i, 0))
    return pl.pallas_call(
        body, name="outproj_ln1", grid=(s // ts,),
        in_specs=[tile(D_MODEL), tile(D_MODEL), _whole(), _whole(), _whole()],
        out_specs=[tile(D_MODEL), tile(1), tile(D_MODEL)],
        out_shape=[jax.ShapeDtypeStruct((s, D_MODEL), F32), jax.ShapeDtypeStruct((s, 1), F32),
                   jax.ShapeDtypeStruct((s, D_MODEL), BF16)],
        compiler_params=_params(("arbitrary",)),
    )(x, cat, wout, g1, b1)


def _conv(cw_ref, cb_ref, g2s, g1s, gate):
    return cb_ref[...] + ((g2s * cw_ref[0:1, :] + g1s * cw_ref[1:2, :]) + gate * cw_ref[2:3, :])


def _ffn_fwd_loss(xhat1, h1b, target, wup4, wdown, cw, cb, g1, b1, g2, b2, ts):
    s = xhat1.shape[0]

    def body(xhat_ref, h1b_ref, tgt_ref, wup_ref, wdn_ref, cw_ref, cb_ref, g1_ref, b1_ref, g2_ref, b2_ref,
             ub_ref, dz2_ref, loss_ref, dg2_ref, db2_ref, u_scr, gext_scr):
        i = pl.program_id(0)

        @pl.when(i == 0)
        def _():
            gext_scr[0:CONV_HALO, :] = jnp.zeros((CONV_HALO, D_FF), F32)
            loss_ref[...] = jnp.zeros_like(loss_ref)
            dg2_ref[...] = jnp.zeros_like(dg2_ref)
            db2_ref[...] = jnp.zeros_like(db2_ref)

        hb = h1b_ref[...]
        for j in range(N_SHARD):
            u_scr[:, j * UP_SH:(j + 1) * UP_SH] = _dot(hb, wup_ref[j])
        ub_ref[...] = u_scr[...].astype(BF16)
        gate = u_scr[:, D_FF:2 * D_FF]
        gext_scr[CONV_HALO:CONV_HALO + ts, :] = gate
        hc = _conv(cw_ref, cb_ref, gext_scr[CONV_HALO - 2:CONV_HALO - 2 + ts, :],
                   gext_scr[CONV_HALO - 1:CONV_HALO - 1 + ts, :], gate)
        gext_scr[0:CONV_HALO, :] = gext_scr[ts:ts + CONV_HALO, :]
        act = (hc * _sigmoid(hc)) * u_scr[:, 0:D_FF]
        h1 = xhat_ref[...] * g1_ref[...] + b1_ref[...]
        z2 = ALPHA * h1 + _dot(act.astype(BF16), wdn_ref[...])
        xhat2, rstd2 = _layernorm_fwd(z2)
        diff = (xhat2 * g2_ref[...] + b2_ref[...]) - tgt_ref[...]
        row = jnp.mean(diff * diff, axis=-1, keepdims=True)
        loss_ref[...] += 0.5 * jnp.sum(row, axis=0, keepdims=True)
        dy = diff * (1.0 / D_MODEL)
        dg2_ref[...] += jnp.sum(dy * xhat2, axis=0, keepdims=True)
        db2_ref[...] += jnp.sum(dy, axis=0, keepdims=True)
        dz2_ref[...] = _layernorm_bwd(dy, xhat2, rstd2, g2_ref[...])

    tile = lambda w: pl.BlockSpec((ts, w), lambda i: (i, 0))
    acc = lambda w: pl.BlockSpec((1, w), lambda i: (0, 0))
    return pl.pallas_call(
        body, name="ffn_fwd_loss", grid=(s // ts,),
        in_specs=[tile(D_MODEL), tile(D_MODEL), tile(D_MODEL)] + [_whole()] * 8,
        out_specs=[tile(2 * D_FF), tile(D_MODEL), acc(1), acc(D_MODEL), acc(D_MODEL)],
        out_shape=[jax.ShapeDtypeStruct((s, 2 * D_FF), BF16), jax.ShapeDtypeStruct((s, D_MODEL), F32),
                   jax.ShapeDtypeStruct((1, 1), F32), jax.ShapeDtypeStruct((1, D_MODEL), F32),
                   jax.ShapeDtypeStruct((1, D_MODEL), F32)],
        scratch_shapes=[pltpu.VMEM((ts, 2 * D_FF), F32), pltpu.VMEM((ts + CONV_HALO, D_FF), F32)],
        compiler_params=_params(("arbitrary",)),
    )(xhat1, h1b, target, wup4, wdown, cw, cb, g1, b1, g2, b2)


def _ffn_bwd(dz2, ub, xhat1, rstd1, wup4, wdown, cw, cb, g1, ts):
    s = dz2.shape[0]
    nt = s // ts
    hb = 16

    def body(dz2_ref, ub_ref, prev_ref, xhat_ref, rstd_ref, wup_ref, wdn_ref, cw_ref, cb_ref, g1_ref,
             a_ref, dub_ref, dz1_ref, dg1_ref, db1_ref, dcw_ref, dcb_ref, gext_scr, dext_scr):
        i = pl.program_id(0)
        r = nt - 1 - i

        @pl.when(i == 0)
        def _():
            dext_scr[ts:ts + CONV_HALO, :] = jnp.zeros((CONV_HALO, D_FF), F32)
            dg1_ref[...] = jnp.zeros_like(dg1_ref)
            db1_ref[...] = jnp.zeros_like(db1_ref)
            dcw_ref[...] = jnp.zeros_like(dcw_ref)
            dcb_ref[...] = jnp.zeros_like(dcb_ref)

        dz2 = dz2_ref[...]
        da = _dot_nt(dz2.astype(BF16), wdn_ref[...])
        val = ub_ref[:, 0:D_FF].astype(F32)
        gate = ub_ref[:, D_FF:2 * D_FF].astype(F32)
        prev = prev_ref[...].astype(F32)[hb - CONV_HALO:hb]
        gext_scr[0:CONV_HALO, :] = jnp.where(r == 0, 0.0, prev)
        gext_scr[CONV_HALO:CONV_HALO + ts, :] = gate
        g2s = gext_scr[CONV_HALO - 2:CONV_HALO - 2 + ts, :]
        g1s = gext_scr[CONV_HALO - 1:CONV_HALO - 1 + ts, :]
        hc = _conv(cw_ref, cb_ref, g2s, g1s, gate)
        sg = _sigmoid(hc)
        si = hc * sg
        a_ref[...] = (si * val).astype(BF16)
        dhc = da * val * (sg * (1.0 + hc * (1.0 - sg)))
        dcb_ref[...] += jnp.sum(dhc, axis=0, keepdims=True)
        dcw_ref[0:1, :] += jnp.sum(dhc * g2s, axis=0, keepdims=True)
        dcw_ref[1:2, :] += jnp.sum(dhc * g1s, axis=0, keepdims=True)
        dcw_ref[2:3, :] += jnp.sum(dhc * gate, axis=0, keepdims=True)
        dext_scr[0:ts, :] = dhc
        dgate = (dhc * cw_ref[2:3, :] + dext_scr[1:1 + ts, :] * cw_ref[1:2, :]
                 + dext_scr[2:2 + ts, :] * cw_ref[0:1, :])
        dext_scr[ts:ts + CONV_HALO, :] = dext_scr[0:CONV_HALO, :]
        dub_ref[:, 0:D_FF] = (da * si).astype(BF16)
        dub_ref[:, D_FF:2 * D_FF] = dgate.astype(BF16)
        dh1 = ALPHA * dz2
        for j in range(N_SHARD):
            dh1 = dh1 + _dot_nt(dub_ref[:, j * UP_SH:(j + 1) * UP_SH], wup_ref[j])
        xhat = xhat_ref[...]
        dg1_ref[...] += jnp.sum(dh1 * xhat, axis=0, keepdims=True)
        db1_ref[...] += jnp.sum(dh1, axis=0, keepdims=True)
        dz1_ref[...] = _layernorm_bwd(dh1, xhat, rstd_ref[...], g1_ref[...])

    tile = lambda w: pl.BlockSpec((ts, w), lambda i: (nt - 1 - i, 0))
    acc = lambda rws, w: pl.BlockSpec((rws, w), lambda i: (0, 0))
    prev_spec = pl.BlockSpec((hb, D_FF), lambda i: (jnp.maximum((nt - 1 - i) * (ts // hb) - 1, 0), 1))
    return pl.pallas_call(
        body, name="ffn_bwd", grid=(nt,),
        in_specs=[tile(D_MODEL), tile(2 * D_FF), prev_spec, tile(D_MODEL), tile(1)] + [_whole()] * 5,
        out_specs=[tile(D_FF), tile(2 * D_FF), tile(D_MODEL), acc(1, D_MODEL), acc(1, D_MODEL),
                   acc(3, D_FF), acc(1, D_FF)],
        out_shape=[jax.ShapeDtypeStruct((s, D_FF), BF16), jax.ShapeDtypeStruct((s, 2 * D_FF), BF16),
                   jax.ShapeDtypeStruct((s, D_MODEL), F32), jax.ShapeDtypeStruct((1, D_MODEL), F32),
                   jax.ShapeDtypeStruct((1, D_MODEL), F32), jax.ShapeDtypeStruct((3, D_FF), F32),
                   jax.ShapeDtypeStruct((1, D_FF), F32)],
        scratch_shapes=[pltpu.VMEM((ts + CONV_HALO, D_FF), F32), pltpu.VMEM((ts + CONV_HALO, D_FF), F32)],
        compiler_params=_params(("arbitrary",)),
    )(dz2, ub, ub, xhat1, rstd1, wup4, wdown, cw, cb, g1)


def _mix_bwd(dz1, pooled, ret, g, wout, wpool, pscale, ts):
    s = dz1.shape[0]
    nt = s // ts

    def body(dz1_ref, pooled_ref, ret_ref, g_ref, wout_ref, wp_ref, ps_ref,
             dret_ref, dgp_ref, dwp_ref, dps_ref, eext_scr):
        i = pl.program_id(0)
        r = nt - 1 - i

        @pl.when(i == 0)
        def _():
            eext_scr[ts:ts + POOL_HALO, :] = jnp.zeros((POOL_HALO, POOL_W), F32)
            dwp_ref[...] = jnp.zeros_like(dwp_ref)
            dps_ref[...] = jnp.zeros_like(dps_ref)

        dzb = dz1_ref[...].astype(BF16)
        dcat_r = _dot_nt(dzb, wout_ref[0:RET_W, :])
        dcat_p = _dot_nt(dzb, wout_ref[RET_W:2 * RET_W, :])
        pos = (r * ts + lax.broadcasted_iota(jnp.int32, (ts, 1), 0) + 1).astype(F32)
        dpooled = []
        for gi, w in enumerate(POOL_WINDOWS):
            sl = slice(gi * HEAD_DIM, (gi + 1) * HEAD_DIM)
            pb = pooled_ref[:, sl]
            dy = dcat_p[:, sl]
            dps_ref[:, sl] += jnp.sum(dy * _dot(pb, wp_ref[gi]), axis=0, keepdims=True)
            dlin = (dy * ps_ref[:, sl]).astype(BF16)
            dwp_ref[gi] += _dot_tn(pb, dlin)
            dpg = _dot_nt(dlin, wp_ref[gi])
            dpooled.append(dpg)
            eext_scr[0:ts, sl] = dpg / jnp.minimum(pos, float(w))
        for gi, w in enumerate(POOL_WINDOWS):
            sl = slice(gi * HEAD_DIM, (gi + 1) * HEAD_DIM)
            acc = eext_scr[:, sl]
            shift = 1
            while shift < w:
                acc = acc + pltpu.roll(acc, ts + POOL_HALO - shift, 0)
                shift *= 2
            dgp_ref[:, RET_W + gi * HEAD_DIM:RET_W + (gi + 1) * HEAD_DIM] = (acc[0:ts] - dpooled[gi]).astype(BF16)
        eext_scr[ts:ts + POOL_HALO, :] = eext_scr[0:POOL_HALO, :]
        for h in range(HEADS):
            sl = slice(h * HEAD_DIM, (h + 1) * HEAD_DIM)
            rt = ret_ref[:, sl]
            rr = lax.rsqrt(jnp.mean(rt * rt, axis=-1, keepdims=True) + RMS_EPS)
            rn = rt * rr
            gh = g_ref[:, sl]
            sg = _sigmoid(gh)
            dy = dcat_r[:, sl]
            dgp_ref[:, sl] = (dy * rn * (sg * (1.0 + gh * (1.0 - sg)))).astype(BF16)
            drn = dy * (gh * sg)
            dret_ref[:, sl] = (rr * (drn - rn * jnp.mean(drn * rn, axis=-1, keepdims=True))).astype(BF16)

    tile = lambda w: pl.BlockSpec((ts, w), lambda i: (nt - 1 - i, 0))
    return pl.pallas_call(
        body, name="mix_bwd", grid=(nt,),
        in_specs=[tile(D_MODEL), tile(POOL_W), tile(RET_W), tile(RET_W), _whole(), _whole(), _whole()],
        out_specs=[tile(RET_W), tile(2 * RET_W),
                   pl.BlockSpec((len(POOL_WINDOWS), HEAD_DIM, HEAD_DIM), lambda i: (0, 0, 0)),
                   pl.BlockSpec((1, POOL_W), lambda i: (0, 0))],
        out_shape=[jax.ShapeDtypeStruct((s, RET_W), BF16), jax.ShapeDtypeStruct((s, 2 * RET_W), BF16),
                   jax.ShapeDtypeStruct((len(POOL_WINDOWS), HEAD_DIM, HEAD_DIM), F32),
                   jax.ShapeDtypeStruct((1, POOL_W), F32)],
        scratch_shapes=[pltpu.VMEM((ts + POOL_HALO, POOL_W), F32)],
        compiler_params=_params(("arbitrary",)),
    )(dz1, pooled, ret, g, wout, wpool, pscale)


def _retention_bwd(q, k, v, dret, dgp, states, mask, qd, kd, cosf, sinf):
    s = q.shape[0]
    ns = s // SUPER
    cdec = [gm ** float(SUPER) for gm in _gammas()]

    def body(q_ref, k_ref, v_ref, do_ref, dgp_ref, st_ref, mask_ref, qd_ref, kd_ref, cos_ref, sin_ref,
             dproj_ref, dstate_scr):
        i = pl.program_id(0)

        @pl.when(i == 0)
        def _():
            dstate_scr[...] = jnp.zeros_like(dstate_scr)

        cosf_t = cos_ref[...]
        sinf_t = sin_ref[...]
        for h in range(HEADS):
            sl = slice(h * HEAD_DIM, (h + 1) * HEAD_DIM)
            qh, kh, vh, doh = q_ref[:, sl], k_ref[:, sl], v_ref[:, sl], do_ref[:, sl]
            m = mask_ref[h]
            scb = (_dot_nt(qh, kh) * m).astype(BF16)
            dscb = (_dot_nt(doh, vh) * m).astype(BF16)
            stb = st_ref[0, h]
            dst = dstate_scr[h]
            dstb = dst.astype(BF16)
            qdb = (qh.astype(F32) * qd_ref[:, sl]).astype(BF16)
            kdb = (kh.astype(F32) * kd_ref[:, sl]).astype(BF16)
            dq = _dot(dscb, kh) + _dot_nt(doh, stb) * qd_ref[:, sl]
            dk = _dot_tn(dscb, qh) + _dot_nt(vh, dstb) * kd_ref[:, sl]
            dv = _dot_tn(scb, doh) + _dot(kdb, dstb)
            dstate_scr[h] = dst * cdec[h] + _dot_tn(qdb, doh)
            lo = h * HEAD_DIM
            dproj_ref[:, lo:lo + HEAD_DIM] = _rope_t(dq, cosf_t, sinf_t).astype(BF16)
            dproj_ref[:, RET_W + lo:RET_W + lo + HEAD_DIM] = _rope_t(dk * K_SCALE, cosf_t, sinf_t).astype(BF16)
            dproj_ref[:, 2 * RET_W + lo:2 * RET_W + lo + HEAD_DIM] = dv.astype(BF16)
        dproj_ref[:, 3 * RET_W:IN_W] = dgp_ref[...]

    tile = lambda w: pl.BlockSpec((SUPER, w), lambda i: (ns - 1 - i, 0))
    return pl.pallas_call(
        body, name="retention_bwd", grid=(ns,),
        in_specs=[tile(RET_W), tile(RET_W), tile(RET_W), tile(RET_W), tile(2 * RET_W),
                  pl.BlockSpec((1, HEADS, HEAD_DIM, HEAD_DIM), lambda i: (ns - 1 - i, 0, 0, 0)),
                  _whole(), _whole(), _whole(), tile(HEAD_DIM), tile(HEAD_DIM)],
        out_specs=tile(IN_W),
        out_shape=jax.ShapeDtypeStruct((s, IN_W), BF16),
        scratch_shapes=[pltpu.VMEM((HEADS, HEAD_DIM, HEAD_DIM), F32)],
        compiler_params=_params(("arbitrary",)),
    )(q, k, v, dret, dgp, states, mask, qd, kd, cosf, sinf)


def _dx(dz1, dproj, win4, ts):
    s = dz1.shape[0]

    def body(dz1_ref, dp_ref, w_ref, dx_ref):
        acc = ALPHA * dz1_ref[...]
        for j in range(N_SHARD):
            acc = acc + _dot_nt(dp_ref[:, j * IN_SH:(j + 1) * IN_SH], w_ref[j])
        dx_ref[...] = acc

    tile = lambda w: pl.BlockSpec((ts, w), lambda i: (i, 0))
    return pl.pallas_call(
        body, name="dx", grid=(s // ts,),
        in_specs=[tile(D_MODEL), tile(IN_W), _whole()],
        out_specs=tile(D_MODEL),
        out_shape=jax.ShapeDtypeStruct((s, D_MODEL), F32),
        compiler_params=_params(("arbitrary",)),
    )(dz1, dproj, win4)


def _wgrad(a, b, tm, tn, tk, name, stacked):
    s, m = a.shape
    n = b.shape[1]
    nk = s // tk

    def body(a_ref, b_ref, o32_ref, o16_ref, acc_scr):
        kk = pl.program_id(2)
        part = _dot_tn(a_ref[...], b_ref[...].astype(BF16))

        @pl.when(kk == 0)
        def _():
            acc_scr[...] = part

        @pl.when(kk > 0)
        def _():
            acc_scr[...] += part

        @pl.when(kk == nk - 1)
        def _():
            res = acc_scr[...]
            o32_ref[...] = res.reshape(o32_ref.shape)
            o16_ref[...] = res.astype(BF16).reshape(o16_ref.shape)

    if stacked:
        shape = (n // tn, m, tn)
        ospec = pl.BlockSpec((1, tm, tn), lambda mi, j, kk: (j, mi, 0))
    else:
        shape = (m, n)
        ospec = pl.BlockSpec((tm, tn), lambda mi, j, kk: (mi, j))
    return pl.pallas_call(
        body, name=name, grid=(m // tm, n // tn, nk),
        in_specs=[pl.BlockSpec((tk, tm), lambda mi, j, kk: (kk, mi)),
                  pl.BlockSpec((tk, tn), lambda mi, j, kk: (kk, j))],
        out_specs=[ospec, ospec],
        out_shape=[jax.ShapeDtypeStruct(shape, F32), jax.ShapeDtypeStruct(shape, BF16)],
        scratch_shapes=[pltpu.VMEM((tm, tn), F32)],
        compiler_params=_params(("arbitrary", "arbitrary", "arbitrary")),
    )(a, b)


def _local_step(x, target, win4, wout, wup4, wdown, cw, cb, wpool, pscale, g1, b1, g2, b2):
    s = x.shape[0]
    ts_a = min(512, s)
    ts_f = min(256, s)
    mask, qd, kd = _decay_tables()
    cosf, sinf = _rope_tables(s)
    wpool_b = wpool.astype(BF16)

    xb, q, k, v, g, pooled, cat = _proj_pool(x, win4, cosf, sinf, wpool_b, pscale, ts_a)
    ret, cat, states = _retention_fwd(q, k, v, g, cat, mask, qd, kd)
    xhat1, rstd1, h1b = _outproj_ln1(x, cat, wout, g1, b1, ts_a)
    ub, dz2, loss, dg2, db2 = _ffn_fwd_loss(xhat1, h1b, target, wup4, wdown, cw, cb, g1, b1, g2, b2, ts_f)

    act, dub, dz1, dg1, db1, dcw, dcb = _ffn_bwd(dz2, ub, xhat1, rstd1, wup4, wdown, cw, cb, g1, ts_f)
    dret, dgp, dwp, dps = _mix_bwd(dz1, pooled, ret, g, wout, wpool_b, pscale, ts_a)
    dproj = _retention_bwd(q, k, v, dret, dgp, states, mask, qd, kd, cosf, sinf)
    grad_x = _dx(dz1, dproj, win4, ts_a)

    tk = min(512, s)
    dwin = _wgrad(xb, dproj, D_MODEL, IN_SH, tk, "wgrad_in", True)
    dwout = _wgrad(cat, dz1, D_MODEL, D_MODEL, tk, "wgrad_out", False)
    dwup = _wgrad(h1b, dub, D_MODEL, UP_SH, tk, "wgrad_up", True)
    dwdown = _wgrad(act, dz2, D_FF // 2, D_MODEL, tk, "wgrad_down", False)
    small = dict(w_pool=dwp, pool_scale=dps, ln1_g=dg1, ln1_b=db1, conv_w=dcw, conv_b=dcb,
                 ln2_g=dg2, ln2_b=db2)
    return loss, grad_x, dict(w_in=dwin, w_out=dwout, w_up=dwup, w_down=dwdown), small


HBM_SPEC = pl.BlockSpec(memory_space=pl.ANY)
CAST_ROWS = 64
SHARD_SHAPES = ((D_MODEL, IN_SH), (OUT_SH, D_MODEL), (D_MODEL, UP_SH), (DOWN_SH, D_MODEL))
N_BIG = len(SHARD_SHAPES)
CW_PAD = (8, 768)


def _mesh_pos():
    return lax.axis_index("x"), lax.axis_index("y"), lax.axis_index("c")


def _other_chips(x, y):
    return [(1 - x, y), (x, 1 - y), (1 - x, 1 - y)]


def _half_rows(w, which):
    hr = SHARD_SHAPES[w][0] // 2
    return pl.ds(pl.multiple_of(which * hr, 16), hr)


def _gather_weights(shards, cw8):
    def body(*refs):
        in_refs = refs[:N_BIG]
        cw_ref = refs[N_BIG]
        out_refs = refs[N_BIG + 1:2 * N_BIG + 1]
        cwo_ref = refs[2 * N_BIG + 1]
        stage = refs[2 * N_BIG + 2:3 * N_BIG + 2]
        send_sems, recv_sems, fsend_sems, frecv_sems, cw_send, cw_recv, local_sems = refs[3 * N_BIG + 2:]
        x, y, c = _mesh_pos()
        j0 = 2 * x + y
        chips = _other_chips(x, y)

        for w in range(N_BIG):
            def cast(i, carry, w=w):
                rows = pl.ds(pl.multiple_of(i * CAST_ROWS, CAST_ROWS), CAST_ROWS)
                stage[w][rows, :] = in_refs[w][rows, :].astype(BF16)
                return carry
            lax.fori_loop(0, SHARD_SHAPES[w][0] // CAST_ROWS, cast, 0)

        local = [pltpu.make_async_copy(stage[w], out_refs[w].at[j0], local_sems.at[w]) for w in range(N_BIG)]
        local.append(pltpu.make_async_copy(cw_ref, cwo_ref.at[j0], local_sems.at[N_BIG]))
        for cp in local:
            cp.start()

        def ici(w, k, block):
            chip = chips[k]
            return pltpu.make_async_remote_copy(
                src_ref=stage[w].at[_half_rows(w, c), :], dst_ref=out_refs[w].at[block, _half_rows(w, c), :],
                send_sem=send_sems.at[w, k], recv_sem=recv_sems.at[w, k],
                device_id=(chip[0], chip[1], c), device_id_type=MESH)

        def d2d(w, k, block, half):
            return pltpu.make_async_remote_copy(
                src_ref=out_refs[w].at[block, _half_rows(w, half), :],
                dst_ref=out_refs[w].at[block, _half_rows(w, half), :],
                send_sem=fsend_sems.at[w, k], recv_sem=frecv_sems.at[w, k],
                device_id=(x, y, 1 - c), device_id_type=MESH)

        def conv(k, block):
            chip = chips[k]
            return pltpu.make_async_remote_copy(
                src_ref=cw_ref, dst_ref=cwo_ref.at[block], send_sem=cw_send.at[k], recv_sem=cw_recv.at[k],
                device_id=(chip[0], chip[1], c), device_id_type=MESH)

        sent = [ici(w, k, j0) for w in range(N_BIG) for k in range(3)] + [conv(k, j0) for k in range(3)]
        for cp in sent:
            cp.start()
        for k, chip in enumerate(chips):
            jk = 2 * chip[0] + chip[1]
            for w in range(N_BIG):
                ici(w, k, jk).wait_recv()
                fw = d2d(w, k, jk, c)
                fw.start()
                sent.append(fw)
        for k, chip in enumerate(chips):
            jk = 2 * chip[0] + chip[1]
            for w in range(N_BIG):
                d2d(w, k, jk, 1 - c).wait_recv()
            conv(k, jk).wait_recv()
        for cp in sent:
            cp.wait_send()
        for cp in local:
            cp.wait()

    out_shape = [jax.ShapeDtypeStruct((N_SHARD,) + shp, BF16) for shp in SHARD_SHAPES]
    out_shape.append(jax.ShapeDtypeStruct((N_SHARD,) + CW_PAD, F32))
    return pl.pallas_call(
        body, name="gather_weights",
        in_specs=[_whole()] * (N_BIG + 1),
        out_specs=[HBM_SPEC] * (N_BIG + 1),
        out_shape=out_shape,
        scratch_shapes=[pltpu.VMEM(shp, BF16) for shp in SHARD_SHAPES] + [
            pltpu.SemaphoreType.DMA((N_BIG, 3)), pltpu.SemaphoreType.DMA((N_BIG, 3)),
            pltpu.SemaphoreType.DMA((N_BIG, 3)), pltpu.SemaphoreType.DMA((N_BIG, 3)),
            pltpu.SemaphoreType.DMA((3,)), pltpu.SemaphoreType.DMA((3,)),
            pltpu.SemaphoreType.DMA((N_BIG + 1,))],
        compiler_params=pltpu.CompilerParams(vmem_limit_bytes=VMEM_LIMIT),
    )(*shards, cw8)


def _pair_exchange(srcs, out_shapes, dtype, name, take_other_half):
    n = len(srcs)

    def body(*refs):
        src_refs, out_refs = refs[:n], refs[n:2 * n]
        send_sems, recv_sems = refs[2 * n:]
        x, y, c = _mesh_pos()
        copies = []
        for w in range(n):
            src = src_refs[w].at[:, _half_rows(w, 1 - c), :] if take_other_half else src_refs[w]
            copies.append(pltpu.make_async_remote_copy(
                src_ref=src, dst_ref=out_refs[w], send_sem=send_sems.at[w], recv_sem=recv_sems.at[w],
                device_id=(x, y, 1 - c), device_id_type=MESH))
        for cp in copies:
            cp.start()
        for cp in copies:
            cp.wait()

    return pl.pallas_call(
        body, name=name,
        in_specs=[HBM_SPEC] * n, out_specs=[HBM_SPEC] * n,
        out_shape=[jax.ShapeDtypeStruct(shp, dtype) for shp in out_shapes],
        scratch_shapes=[pltpu.SemaphoreType.DMA((n,)), pltpu.SemaphoreType.DMA((n,))],
    )(*srcs)


def _chip_exchange(p16s):
    def body(*refs):
        src_refs, out_refs = refs[:N_BIG], refs[N_BIG:2 * N_BIG]
        send_sems, recv_sems = refs[2 * N_BIG:]
        x, y, c = _mesh_pos()
        copies = []
        for k, chip in enumerate(_other_chips(x, y)):
            jk = 2 * chip[0] + chip[1]
            for w in range(N_BIG):
                copies.append(pltpu.make_async_remote_copy(
                    src_ref=src_refs[w].at[jk], dst_ref=out_refs[w].at[k],
                    send_sem=send_sems.at[w, k], recv_sem=recv_sems.at[w, k],
                    device_id=(chip[0], chip[1], c), device_id_type=MESH))
        for cp in copies:
            cp.start()
        for cp in copies:
            cp.wait()

    return pl.pallas_call(
        body, name="chip_exchange",
        in_specs=[HBM_SPEC] * N_BIG, out_specs=[HBM_SPEC] * N_BIG,
        out_shape=[jax.ShapeDtypeStruct((3, r // 2, cc), BF16) for r, cc in SHARD_SHAPES],
        scratch_shapes=[pltpu.SemaphoreType.DMA((N_BIG, 3)), pltpu.SemaphoreType.DMA((N_BIG, 3))],
    )(*p16s)


def _pair_sum(pos, g32s, recvs):
    def body(pos_ref, *refs):
        del pos_ref
        g_refs, r_refs = refs[:N_BIG], refs[N_BIG:2 * N_BIG]
        p32_refs, p16_refs = refs[2 * N_BIG:3 * N_BIG], refs[3 * N_BIG:]
        for w in range(N_BIG):
            tot = g_refs[w][...] + r_refs[w][...].astype(F32)
            p32_refs[w][...] = tot
            p16_refs[w][...] = tot.astype(BF16)

    halves = [(r // 2, cc) for r, cc in SHARD_SHAPES]
    own = [pl.BlockSpec((None, None) + h, lambda j, pos_ref: (j, pos_ref[0], 0, 0)) for h in halves]
    blk = [pl.BlockSpec((None,) + h, lambda j, pos_ref: (j, 0, 0)) for h in halves]
    g4 = [g.reshape(N_SHARD, 2, r // 2, cc) for g, (r, cc) in zip(g32s, SHARD_SHAPES)]
    return pl.pallas_call(
        body, name="pair_sum",
        grid_spec=pltpu.PrefetchScalarGridSpec(
            num_scalar_prefetch=1, grid=(N_SHARD,), in_specs=own + blk, out_specs=blk + blk),
        out_shape=[jax.ShapeDtypeStruct((N_SHARD,) + h, F32) for h in halves]
        + [jax.ShapeDtypeStruct((N_SHARD,) + h, BF16) for h in halves],
        compiler_params=_params(("arbitrary",)),
    )(pos, *g4, *recvs)


def _chip_sum(pos, p32s, recvs):
    parts = 2

    def body(pos_ref, *refs):
        del pos_ref
        p_refs, r_refs, f_refs = refs[:N_BIG], refs[N_BIG:2 * N_BIG], refs[2 * N_BIG:]
        for w in range(N_BIG):
            f_refs[w][...] = ((p_refs[w][...] + r_refs[w][0].astype(F32)) + r_refs[w][1].astype(F32)) \
                + r_refs[w][2].astype(F32)

    quarters = [(r // 2 // parts, cc) for r, cc in SHARD_SHAPES]
    own = [pl.BlockSpec((None,) + qt, lambda i, pos_ref: (pos_ref[1], i, 0)) for qt in quarters]
    rcv = [pl.BlockSpec((3,) + qt, lambda i, pos_ref: (0, i, 0)) for qt in quarters]
    out = [pl.BlockSpec(qt, lambda i, pos_ref: (i, 0)) for qt in quarters]
    return pl.pallas_call(
        body, name="chip_sum",
        grid_spec=pltpu.PrefetchScalarGridSpec(
            num_scalar_prefetch=1, grid=(parts,), in_specs=own + rcv, out_specs=out),
        out_shape=[jax.ShapeDtypeStruct((r // 2, cc), F32) for r, cc in SHARD_SHAPES],
        compiler_params=_params(("arbitrary",)),
    )(pos, *p32s, *recvs)


def _adamw(w, g, m, v):
    m_new = ADAM_B1 * m + (1.0 - ADAM_B1) * g
    v_new = ADAM_B2 * v + (1.0 - ADAM_B2) * (g * g)
    m_hat = m_new / (1.0 - ADAM_B1 ** ADAM_STEP)
    v_hat = v_new / (1.0 - ADAM_B2 ** ADAM_STEP)
    delta = -ADAM_LR * (m_hat / (jnp.sqrt(v_hat) + ADAM_EPS) + ADAM_WD * w)
    return delta, m_new, v_new


def _adam_big(pos, mine, theirs, ws, ms, vs):
    nb = 4

    def body(pos_ref, *refs):
        hf = pl.program_id(0)
        groups = [refs[i * N_BIG:(i + 1) * N_BIG] for i in range(9)]
        f_refs, t_refs, w_refs, m_refs, v_refs, go_refs, do_refs, mo_refs, vo_refs = groups
        for w in range(N_BIG):
            g = jnp.where(hf == pos_ref[0], f_refs[w][...], t_refs[w][...])
            delta, m_new, v_new = _adamw(w_refs[w][...], g, m_refs[w][...], v_refs[w][...])
            go_refs[w][...] = g
            do_refs[w][...] = delta
            mo_refs[w][...] = m_new
            vo_refs[w][...] = v_new

    blocks = [(r // 2 // nb, cc) for r, cc in SHARD_SHAPES]
    half = [pl.BlockSpec(b, lambda hf, i, pos_ref: (i, 0)) for b in blocks]
    full = [pl.BlockSpec((None,) + b, lambda hf, i, pos_ref: (0, hf * nb + i, 0)) for b in blocks]
    shapes = [jax.ShapeDtypeStruct((1,) + shp, F32) for shp in SHARD_SHAPES]
    outs = pl.pallas_call(
        body, name="adam_big",
        grid_spec=pltpu.PrefetchScalarGridSpec(
            num_scalar_prefetch=1, grid=(2, nb), in_specs=half + half + full * 3, out_specs=full * 4),
        out_shape=shapes * 4,
        compiler_params=_params(("arbitrary", "arbitrary")),
    )(pos, *mine, *theirs, *ws, *ms, *vs)
    return [outs[i * N_BIG:(i + 1) * N_BIG] for i in range(4)]


SMALL_ROWS = 8
ROW_CONV_B, ROW_POOL_SCALE, ROW_LN1_G, ROW_LN1_B, ROW_LN2_G, ROW_LN2_B, ROW_LOSS = range(7)
SMALL_VECS = ((ROW_CONV_B, D_FF), (ROW_POOL_SCALE, POOL_W), (ROW_LN1_G, D_MODEL), (ROW_LN1_B, D_MODEL),
              (ROW_LN2_G, D_MODEL), (ROW_LN2_B, D_MODEL))


def _small_update(loss, dwp, dcw4, vec_grads, wp, cwp, vec_ws, m_wp, m_cwp, vec_ms, v_wp, v_cwp, vec_vs):
    nv = len(SMALL_VECS)

    def body(*refs):
        loss_ref, dwp_ref, dcw_ref = refs[0:3]
        gvec = refs[3:3 + nv]
        o = 3 + nv
        wp_ref, cw_ref = refs[o:o + 2]
        wvec = refs[o + 2:o + 2 + nv]
        o += 2 + nv
        mwp_ref, mcw_ref = refs[o:o + 2]
        mvec = refs[o + 2:o + 2 + nv]
        o += 2 + nv
        vwp_ref, vcw_ref = refs[o:o + 2]
        vvec = refs[o + 2:o + 2 + nv]
        o += 2 + nv
        loss_out = refs[o]
        outs = refs[o + 1:o + 1 + 4 * (2 + nv)]
        o += 1 + 4 * (2 + nv)
        (vec_scr, sib_a, sib_b, sib_c, all_a, all_b, all_c,
         send1, recv1, send2, recv2) = refs[o:]
        x, y, c = _mesh_pos()
        j0 = 2 * x + y
        chips = _other_chips(x, y)

        vec_scr[...] = jnp.zeros_like(vec_scr)
        for (row, n), ref in zip(SMALL_VECS, gvec):
            vec_scr[row:row + 1, 0:n] = ref[...]
        vec_scr[ROW_LOSS:ROW_LOSS + 1, 0:HEAD_DIM] = jnp.broadcast_to(loss_ref[...], (1, HEAD_DIM))

        mine = (dwp_ref, vec_scr, dcw_ref)
        sib = (sib_a, sib_b, sib_c)
        every = (all_a, all_b, all_c)
        first = [pltpu.make_async_remote_copy(
            src_ref=mine[b], dst_ref=sib[b], send_sem=send1.at[b], recv_sem=recv1.at[b],
            device_id=(x, y, 1 - c), device_id_type=MESH) for b in range(3)]
        for cp in first:
            cp.start()
        for cp in first:
            cp.wait()
        for b in range(3):
            every[b][j0] = mine[b][...] + sib[b][...]

        def ici(b, k, block):
            chip = chips[k]
            return pltpu.make_async_remote_copy(
                src_ref=every[b].at[block], dst_ref=every[b].at[block],
                send_sem=send2.at[b, k], recv_sem=recv2.at[b, k],
                device_id=(chip[0], chip[1], c), device_id_type=MESH)

        second = [ici(b, k, j0) for b in range(3) for k in range(3)]
        for cp in second:
            cp.start()
        for k, chip in enumerate(chips):
            for b in range(3):
                ici(b, k, 2 * chip[0] + chip[1]).wait_recv()
        for cp in second:
            cp.wait_send()

        tot_a = ((all_a[0] + all_a[1]) + all_a[2]) + all_a[3]
        tot_b = ((all_b[0] + all_b[1]) + all_b[2]) + all_b[3]
        all_c[0] = ((all_c[0] + all_c[1]) + all_c[2]) + all_c[3]
        tot_c = all_c[0, j0]
        loss_out[...] = tot_b[ROW_LOSS:ROW_LOSS + 1, 0:1]

        grads = [tot_a, tot_c] + [tot_b[row:row + 1, 0:n] for row, n in SMALL_VECS]
        w_all = [wp_ref, cw_ref] + list(wvec)
        m_all = [mwp_ref, mcw_ref] + list(mvec)
        v_all = [vwp_ref, vcw_ref] + list(vvec)
        np_ = 2 + nv
        for p in range(np_):
            g = grads[p]
            delta, m_new, v_new = _adamw(w_all[p][...], g, m_all[p][...], v_all[p][...])
            outs[p][...] = g
            outs[np_ + p][...] = delta
            outs[2 * np_ + p][...] = m_new
            outs[3 * np_ + p][...] = v_new

    pshapes = [wp.shape, CW_PAD] + [wv.shape for wv in vec_ws]
    out_shape = [jax.ShapeDtypeStruct((1, 1), F32)] + [jax.ShapeDtypeStruct(s, F32) for s in pshapes] * 4
    a_shape = dwp.shape
    b_shape = (SMALL_ROWS, D_FF)
    c_shape = dcw4.shape
    n_in = 3 + nv + 3 * (2 + nv)
    outs = pl.pallas_call(
        body, name="small_update",
        in_specs=[_whole()] * n_in, out_specs=[_whole()] * len(out_shape), out_shape=out_shape,
        scratch_shapes=[pltpu.VMEM(b_shape, F32),
                        pltpu.VMEM(a_shape, F32), pltpu.VMEM(b_shape, F32), pltpu.VMEM(c_shape, F32),
                        pltpu.VMEM((N_SHARD,) + a_shape, F32), pltpu.VMEM((N_SHARD,) + b_shape, F32),
                        pltpu.VMEM((N_SHARD,) + c_shape, F32),
                        pltpu.SemaphoreType.DMA((3,)), pltpu.SemaphoreType.DMA((3,)),
                        pltpu.SemaphoreType.DMA((3, 3)), pltpu.SemaphoreType.DMA((3, 3))],
        compiler_params=pltpu.CompilerParams(vmem_limit_bytes=VMEM_LIMIT),
    )(loss, dwp, dcw4, *vec_grads, wp, cwp, *vec_ws, m_wp, m_cwp, *vec_ms, v_wp, v_cwp, *vec_vs)
    np_ = 2 + nv
    return outs[0], [outs[1 + i * np_:1 + (i + 1) * np_] for i in range(4)]


def _pad_cw(a):
    pad = [(0, 0)] * (a.ndim - 2) + [(0, CW_PAD[0] - a.shape[-2]), (0, CW_PAD[1] - a.shape[-1])]
    return jnp.pad(a, pad)


def kernel(x, w_in, w_pool, pool_scale, w_out, ln1_g, ln1_b, w_up, conv_w, conv_b, w_down, ln2_g, ln2_b, loss_target, m_w_in, m_w_pool, m_pool_scale, m_w_out, m_ln1_g, m_ln1_b, m_w_up, m_conv_w, m_conv_b, m_w_down, m_ln2_g, m_ln2_b, v_w_in, v_w_pool, v_pool_scale, v_w_out, v_ln1_g, v_ln1_b, v_w_up, v_conv_w, v_conv_b, v_w_down, v_ln2_g, v_ln2_b):
    pos = jnp.stack([lax.axis_index("c"), 2 * lax.axis_index("x") + lax.axis_index("y")]).astype(jnp.int32)

    win4, wout4, wup4, wdown4, cw4 = _gather_weights(
        [w_in[0], w_out[0], w_up[0], w_down[0]], _pad_cw(conv_w[0]))
    cw_full = jnp.transpose(cw4[:, 0:3, 0:DOWN_SH], (1, 0, 2)).reshape(3, D_FF)

    loss, grad_x, big, small = _local_step(
        x[0], loss_target[0], win4, wout4.reshape(D_MODEL, D_MODEL), wup4, wdown4.reshape(D_FF, D_MODEL),
        cw_full, conv_b, w_pool[0], pool_scale, ln1_g, ln1_b, ln2_g, ln2_b)

    stacked = lambda g: g.reshape(N_SHARD, g.shape[0] // N_SHARD, g.shape[1]) if g.ndim == 2 else g
    order = ("w_in", "w_out", "w_up", "w_down")
    g32s = [stacked(big[n][0]) for n in order]
    g16s = [stacked(big[n][1]) for n in order]
    halves = [(N_SHARD, r // 2, cc) for r, cc in SHARD_SHAPES]
    recv_a = _pair_exchange(g16s, halves, BF16, "pair_exchange_bf16", True)
    outs = _pair_sum(pos, g32s, recv_a)
    p32s, p16s = outs[:N_BIG], outs[N_BIG:]
    recv_b = _chip_exchange(p16s)
    mine = _chip_sum(pos, p32s, recv_b)
    theirs = _pair_exchange(mine, [(r // 2, cc) for r, cc in SHARD_SHAPES], F32, "pair_exchange_f32", False)
    big_out = _adam_big(pos, mine, theirs, [w_in, w_out, w_up, w_down], [m_w_in, m_w_out, m_w_up, m_w_down],
                        [v_w_in, v_w_out, v_w_up, v_w_down])

    dcw4 = _pad_cw(jnp.transpose(small["conv_w"].reshape(3, N_SHARD, DOWN_SH), (1, 0, 2)))
    vec_names = ("conv_b", "pool_scale", "ln1_g", "ln1_b", "ln2_g", "ln2_b")
    given = dict(w_pool=w_pool, pool_scale=pool_scale, ln1_g=ln1_g, ln1_b=ln1_b, conv_w=conv_w, conv_b=conv_b,
                 ln2_g=ln2_g, ln2_b=ln2_b)
    given_m = dict(w_pool=m_w_pool, pool_scale=m_pool_scale, ln1_g=m_ln1_g, ln1_b=m_ln1_b, conv_w=m_conv_w,
                   conv_b=m_conv_b, ln2_g=m_ln2_g, ln2_b=m_ln2_b)
    given_v = dict(w_pool=v_w_pool, pool_scale=v_pool_scale, ln1_g=v_ln1_g, ln1_b=v_ln1_b, conv_w=v_conv_w,
                   conv_b=v_conv_b, ln2_g=v_ln2_g, ln2_b=v_ln2_b)
    args = []
    for src in (given, given_m, given_v):
        args += [src["w_pool"][0], _pad_cw(src["conv_w"][0]), [src[n] for n in vec_names]]
    loss_tot, small_out = _small_update(loss, small["w_pool"], dcw4, [small[n] for n in vec_names], *args)

    names = ("w_in", "w_pool", "pool_scale", "w_out", "ln1_g", "ln1_b", "w_up", "conv_w", "conv_b", "w_down",
             "ln2_g", "ln2_b")
    small_names = ("w_pool", "conv_w") + vec_names
    result = [loss_tot.reshape(()), grad_x[None]]
    for kind in range(4):
        for n in names:
            if n in order:
                result.append(big_out[kind][order.index(n)])
            else:
                val = small_out[kind][small_names.index(n)]
                if n == "conv_w":
                    val = val[0:3, 0:DOWN_SH][None]
                elif n == "w_pool":
                    val = val[None]
                result.append(val)
    return tuple(result)
```

```python
import functools
import math

import numpy as np
import jax
import jax.numpy as jnp
from jax import lax
from jax.experimental import pallas as pl
from jax.experimental.pallas import tpu as pltpu

F32 = jnp.float32
BF16 = jnp.bfloat16

D_MODEL = 1024
HEADS = 4
HEAD_DIM = 128
RET_W = HEADS * HEAD_DIM
POOL_WINDOWS = (2, 4, 8, 16)
POOL_W = 512
IN_W = 4 * RET_W + POOL_W
D_FF = 2816
N_SHARD = 4
IN_SH = IN_W // N_SHARD
UP_SH = 2 * D_FF // N_SHARD
DOWN_SH = D_FF // N_SHARD
OUT_SH = D_MODEL // N_SHARD
ROPE_BASE = 10000.0
LN_EPS = 1e-5
RMS_EPS = 1e-6
ALPHA = 2.0 ** 0.25
K_SCALE = HEAD_DIM ** -0.5
SUPER = 256
CHUNK = 64
POOL_HALO = 16
CONV_HALO = 8

ADAM_LR = 0.001
ADAM_B1 = 0.9
ADAM_B2 = 0.999
ADAM_EPS = 1e-08
ADAM_WD = 0.01
ADAM_STEP = 10

MESH = pl.DeviceIdType.MESH
VMEM_LIMIT = 56 * 1024 * 1024


def _dot(a, b):
    return jnp.dot(a, b, preferred_element_type=F32)


def _dot_nt(a, b):
    return lax.dot_general(a, b, (((1,), (1,)), ((), ())), preferred_element_type=F32)


def _dot_tn(a, b):
    return lax.dot_general(a, b, (((0,), (0,)), ((), ())), preferred_element_type=F32)


def _sigmoid(x):
    return 1.0 / (1.0 + jnp.exp(-x))


def _params(sem):
    return pltpu.CompilerParams(dimension_semantics=sem, vmem_limit_bytes=VMEM_LIMIT)


def _whole():
    return pl.BlockSpec(memory_space=pltpu.VMEM)


HBM_SPEC = pl.BlockSpec(memory_space=pl.ANY)


class _Rider:
    def __init__(self, inplace, srcs, lands, n_copies, make):
        self.inplace, self.srcs, self.lands, self.n_copies, self.make = list(inplace), list(srcs), list(lands), n_copies, make


def _call(body, *, name, grid, in_specs, out_specs, out_shape, operands, scratch_shapes=(), sem=(),
          aliases=None, riders=()):
    n_in, n_out, n_scr = len(in_specs), len(out_shape), len(scratch_shapes)
    in_specs, out_specs, out_shape = list(in_specs), list(out_specs), list(out_shape)
    operands, scratch_shapes, aliases = list(operands), list(scratch_shapes), dict(aliases or {})
    for r in riders:
        for a in r.inplace:
            aliases[len(in_specs)] = len(out_shape)
            in_specs.append(HBM_SPEC)
            operands.append(a)
            out_specs.append(HBM_SPEC)
            out_shape.append(jax.ShapeDtypeStruct(a.shape, a.dtype))
        for a in r.srcs:
            in_specs.append(HBM_SPEC)
            operands.append(a)
        for shp in r.lands:
            out_specs.append(HBM_SPEC)
            out_shape.append(shp)
        scratch_shapes += [pltpu.SemaphoreType.DMA((r.n_copies,)), pltpu.SemaphoreType.DMA((r.n_copies,))]

    def full(*refs):
        ins = refs[:n_in]
        at = n_in
        r_srcs = []
        for r in riders:
            at += len(r.inplace)
            r_srcs.append(refs[at:at + len(r.srcs)])
            at += len(r.srcs)
        outs = refs[at:at + n_out]
        at += n_out
        r_outs = []
        for r in riders:
            r_outs.append((refs[at:at + len(r.inplace)], refs[at + len(r.inplace):at + len(r.inplace) + len(r.lands)]))
            at += len(r.inplace) + len(r.lands)
        scr = refs[at:at + n_scr]
        at += n_scr
        r_sems = [refs[at + 2 * i:at + 2 * i + 2] for i in range(len(riders))]

        def copies():
            return [r.make(r_outs[i][0], r_srcs[i], r_outs[i][1], r_sems[i][0], r_sems[i][1])
                    for i, r in enumerate(riders)]

        def start():
            for starts, _ in copies():
                for cp in starts:
                    cp.start()

        def finish():
            for _, waits in copies():
                for wait in waits:
                    wait()

        if riders and grid:
            first = functools.reduce(jnp.logical_and, [pl.program_id(d) == 0 for d in range(len(grid))])
            last = functools.reduce(jnp.logical_and, [pl.program_id(d) == grid[d] - 1 for d in range(len(grid))])
            pl.when(first)(start)
            body(*ins, *outs, *scr)
            pl.when(last)(finish)
        else:
            if riders:
                start()
            body(*ins, *outs, *scr)
            if riders:
                finish()

    params = _params(sem) if grid else pltpu.CompilerParams(vmem_limit_bytes=VMEM_LIMIT)
    res = pl.pallas_call(
        full, name=name, grid=grid, in_specs=in_specs, out_specs=out_specs, out_shape=out_shape,
        scratch_shapes=scratch_shapes, input_output_aliases=aliases, compiler_params=params,
    )(*operands)
    outs, at, rider_res = res[:n_out], n_out, []
    for r in riders:
        rider_res.append((res[at:at + len(r.inplace)], res[at + len(r.inplace):at + len(r.inplace) + len(r.lands)]))
        at += len(r.inplace) + len(r.lands)
    return list(outs), rider_res


def _gammas():
    return [1.0 - 2.0 ** (-5.0 - h) for h in range(HEADS)]


def _decay_tables():
    idx = np.arange(SUPER)
    dist = np.abs(idx[:, None] - idx[None, :]).astype(np.float64)
    visible = (idx[None, :] // CHUNK) <= (idx[:, None] // CHUNK)
    mask = np.stack([np.where(visible, g ** dist, 0.0) for g in _gammas()])
    qd = np.concatenate([np.repeat((g ** (idx + 1.0))[:, None], HEAD_DIM, 1) for g in _gammas()], 1)
    kd = np.concatenate([np.repeat((g ** (SUPER - 1.0 - idx))[:, None], HEAD_DIM, 1) for g in _gammas()], 1)
    return (jnp.asarray(mask, F32), jnp.asarray(qd, F32), jnp.asarray(kd, F32))


def _rope_tables(s):
    inv_freq = ROPE_BASE ** (-np.arange(0, HEAD_DIM, 2, dtype=np.float64) / HEAD_DIM)
    ang = np.arange(s, dtype=np.float64)[:, None] * inv_freq[None, :]
    cos, sin = np.cos(ang), np.sin(ang)
    return (jnp.asarray(np.concatenate([cos, cos], 1), F32),
            jnp.asarray(np.concatenate([-sin, sin], 1), F32))


def _rope(t, cosf, sinf):
    return t * cosf + pltpu.roll(t, HEAD_DIM // 2, 1) * sinf


def _rope_t(t, cosf, sinf):
    return t * cosf - pltpu.roll(t, HEAD_DIM // 2, 1) * sinf


def _layernorm_fwd(z):
    mu = jnp.mean(z, axis=-1, keepdims=True)
    zc = z - mu
    var = jnp.mean(zc * zc, axis=-1, keepdims=True)
    rstd = lax.rsqrt(var + LN_EPS)
    return zc * rstd, rstd


def _layernorm_bwd(dy, xhat, rstd, gain):
    dxh = dy * gain
    m1 = jnp.mean(dxh, axis=-1, keepdims=True)
    m2 = jnp.mean(dxh * xhat, axis=-1, keepdims=True)
    return rstd * (dxh - m1 - xhat * m2)


def _proj_pool(x, win4, cosf, sinf, wpool, pscale, ts, riders=()):
    s = x.shape[0]
    nt = s // ts

    def body(x_ref, w_ref, cos_ref, sin_ref, wp_ref, ps_ref,
             xb_ref, q_ref, k_ref, v_ref, g_ref, pooled_ref, cat_ref, proj_scr, pext_scr):
        i = pl.program_id(0)
        xb = x_ref[...].astype(BF16)
        xb_ref[...] = xb
        for j in range(N_SHARD):
            proj_scr[:, j * IN_SH:(j + 1) * IN_SH] = _dot(xb, w_ref[j])
        cosf_t = cos_ref[...]
        sinf_t = sin_ref[...]
        for h in range(HEADS):
            lo = h * HEAD_DIM
            q_ref[:, lo:lo + HEAD_DIM] = _rope(proj_scr[:, lo:lo + HEAD_DIM], cosf_t, sinf_t).astype(BF16)
            kk = _rope(proj_scr[:, RET_W + lo:RET_W + lo + HEAD_DIM], cosf_t, sinf_t) * K_SCALE
            k_ref[:, lo:lo + HEAD_DIM] = kk.astype(BF16)
        v_ref[...] = proj_scr[:, 2 * RET_W:3 * RET_W].astype(BF16)
        g_ref[...] = proj_scr[:, 3 * RET_W:4 * RET_W]

        @pl.when(i == 0)
        def _():
            pext_scr[0:POOL_HALO, :] = jnp.zeros((POOL_HALO, POOL_W), F32)

        pext_scr[POOL_HALO:POOL_HALO + ts, :] = proj_scr[:, 4 * RET_W:IN_W]
        pos = (i * ts + lax.broadcasted_iota(jnp.int32, (ts, 1), 0) + 1).astype(F32)
        for gi, w in enumerate(POOL_WINDOWS):
            lo = gi * HEAD_DIM
            ext = pext_scr[:, lo:lo + HEAD_DIM]
            acc = ext
            shift = 1
            while shift < w:
                acc = acc + pltpu.roll(acc, shift, 0)
                shift *= 2
            tok = ext[POOL_HALO:POOL_HALO + ts]
            pooled = acc[POOL_HALO:POOL_HALO + ts] / jnp.minimum(pos, float(w)) - tok
            pooled_b = pooled.astype(BF16)
            pooled_ref[:, lo:lo + HEAD_DIM] = pooled_b
            lin = _dot(pooled_b, wp_ref[gi])
            cat_ref[:, lo:lo + HEAD_DIM] = (lin * ps_ref[:, lo:lo + HEAD_DIM]).astype(BF16)
        pext_scr[0:POOL_HALO, :] = pext_scr[ts:ts + POOL_HALO, :]

    tile = lambda w: pl.BlockSpec((ts, w), lambda i: (i, 0))
    return _call(
        body, name="proj_pool", grid=(nt,),
        in_specs=[tile(D_MODEL), _whole(), tile(HEAD_DIM), tile(HEAD_DIM), _whole(), _whole()],
        out_specs=[tile(D_MODEL), tile(RET_W), tile(RET_W), tile(RET_W), tile(RET_W), tile(POOL_W),
                   pl.BlockSpec((ts, POOL_W), lambda i: (i, 1))],
        out_shape=[jax.ShapeDtypeStruct((s, D_MODEL), BF16), jax.ShapeDtypeStruct((s, RET_W), BF16),
                   jax.ShapeDtypeStruct((s, RET_W), BF16), jax.ShapeDtypeStruct((s, RET_W), BF16),
                   jax.ShapeDtypeStruct((s, RET_W), F32), jax.ShapeDtypeStruct((s, POOL_W), BF16),
                   jax.ShapeDtypeStruct((s, 2 * RET_W), BF16)],
        scratch_shapes=[pltpu.VMEM((ts, IN_W), F32), pltpu.VMEM((ts + POOL_HALO, POOL_W), F32)],
        sem=("arbitrary",), operands=(x, win4, cosf, sinf, wpool, pscale), riders=riders,
    )


def _retention_fwd(q, k, v, g, cat, mask, qd, kd, riders=()):
    s = q.shape[0]
    ns = s // SUPER
    cdec = [gm ** float(SUPER) for gm in _gammas()]

    def body(q_ref, k_ref, v_ref, g_ref, cat_in, mask_ref, qd_ref, kd_ref,
             ret_ref, cat_ref, st_ref, state_scr):
        del cat_in
        n = pl.program_id(0)

        @pl.when(n == 0)
        def _():
            state_scr[...] = jnp.zeros_like(state_scr)

        for h in range(HEADS):
            sl = slice(h * HEAD_DIM, (h + 1) * HEAD_DIM)
            qh, kh, vh = q_ref[:, sl], k_ref[:, sl], v_ref[:, sl]
            sc = _dot_nt(qh, kh) * mask_ref[h]
            st = state_scr[h]
            stb = st.astype(BF16)
            st_ref[0, h] = stb
            qdb = (qh.astype(F32) * qd_ref[:, sl]).astype(BF16)
            kdb = (kh.astype(F32) * kd_ref[:, sl]).astype(BF16)
            ret = _dot(sc.astype(BF16), vh) + _dot(qdb, stb)
            state_scr[h] = st * cdec[h] + _dot_tn(kdb, vh)
            ret_ref[:, sl] = ret
            r = lax.rsqrt(jnp.mean(ret * ret, axis=-1, keepdims=True) + RMS_EPS)
            gh = g_ref[:, sl]
            cat_ref[:, sl] = ((ret * r) * (gh * _sigmoid(gh))).astype(BF16)

    tile = pl.BlockSpec((SUPER, RET_W), lambda n: (n, 0))
    return _call(
        body, name="retention_fwd", grid=(ns,),
        in_specs=[tile, tile, tile, tile, HBM_SPEC, _whole(), _whole(), _whole()],
        out_specs=[tile, tile, pl.BlockSpec((1, HEADS, HEAD_DIM, HEAD_DIM), lambda n: (n, 0, 0, 0))],
        out_shape=[jax.ShapeDtypeStruct((s, RET_W), F32), jax.ShapeDtypeStruct((s, 2 * RET_W), BF16),
                   jax.ShapeDtypeStruct((ns, HEADS, HEAD_DIM, HEAD_DIM), BF16)],
        scratch_shapes=[pltpu.VMEM((HEADS, HEAD_DIM, HEAD_DIM), F32)],
        aliases={4: 1}, sem=("arbitrary",), operands=(q, k, v, g, cat, mask, qd, kd), riders=riders,
    )


def _outproj_ln1(x, cat, wout, g1, b1, ts, riders=()):
    s = x.shape[0]

    def body(x_ref, cat_ref, w_ref, g_ref, b_ref, xhat_ref, rstd_ref, h1b_ref):
        z = ALPHA * x_ref[...] + _dot(cat_ref[...], w_ref[...])
        xhat, rstd = _layernorm_fwd(z)
        xhat_ref[...] = xhat
        rstd_ref[...] = rstd
        h1b_ref[...] = (xhat * g_ref[...] + b_ref[...]).astype(BF16)

    tile = lambda w: pl.BlockSpec((ts, w), lambda i: (i, 0))
    return _call(
        body, name="outproj_ln1", grid=(s // ts,),
        in_specs=[tile(D_MODEL), tile(D_MODEL), _whole(), _whole(), _whole()],
        out_specs=[tile(D_MODEL), tile(1), tile(D_MODEL)],
        out_shape=[jax.ShapeDtypeStruct((s, D_MODEL), F32), jax.ShapeDtypeStruct((s, 1), F32),
                   jax.ShapeDtypeStruct((s, D_MODEL), BF16)],
        sem=("arbitrary",), operands=(x, cat, wout, g1, b1), riders=riders,
    )


def _conv(cw_ref, cb_ref, g2s, g1s, gate):
    return cb_ref[...] + ((g2s * cw_ref[0:1, :] + g1s * cw_ref[1:2, :]) + gate * cw_ref[2:3, :])


def _ffn_fwd_loss(xhat1, h1b, target, wup4, wdown, cw, cb, g1, b1, g2, b2, ts):
    s = xhat1.shape[0]

    def body(xhat_ref, h1b_ref, tgt_ref, wup_ref, wdn_ref, cw_ref, cb_ref, g1_ref, b1_ref, g2_ref, b2_ref,
             ub_ref, dz2_ref, loss_ref, dg2_ref, db2_ref, u_scr, gext_scr):
        i = pl.program_id(0)

        @pl.when(i == 0)
        def _():
            gext_scr[0:CONV_HALO, :] = jnp.zeros((CONV_HALO, D_FF), F32)
            loss_ref[...] = jnp.zeros_like(loss_ref)
            dg2_ref[...] = jnp.zeros_like(dg2_ref)
            db2_ref[...] = jnp.zeros_like(db2_ref)

        hb = h1b_ref[...]
        for j in range(N_SHARD):
            u_scr[:, j * UP_SH:(j + 1) * UP_SH] = _dot(hb, wup_ref[j])
        ub_ref[...] = u_scr[...].astype(BF16)
        gate = u_scr[:, D_FF:2 * D_FF]
        gext_scr[CONV_HALO:CONV_HALO + ts, :] = gate
        hc = _conv(cw_ref, cb_ref, gext_scr[CONV_HALO - 2:CONV_HALO - 2 + ts, :],
                   gext_scr[CONV_HALO - 1:CONV_HALO - 1 + ts, :], gate)
        gext_scr[0:CONV_HALO, :] = gext_scr[ts:ts + CONV_HALO, :]
        act = (hc * _sigmoid(hc)) * u_scr[:, 0:D_FF]
        h1 = xhat_ref[...] * g1_ref[...] + b1_ref[...]
        z2 = ALPHA * h1 + _dot(act.astype(BF16), wdn_ref[...])
        xhat2, rstd2 = _layernorm_fwd(z2)
        diff = (xhat2 * g2_ref[...] + b2_ref[...]) - tgt_ref[...]
        row = jnp.mean(diff * diff, axis=-1, keepdims=True)
        loss_ref[...] += 0.5 * jnp.sum(row, axis=0, keepdims=True)
        dy = diff * (1.0 / D_MODEL)
        dg2_ref[...] += jnp.sum(dy * xhat2, axis=0, keepdims=True)
        db2_ref[...] += jnp.sum(dy, axis=0, keepdims=True)
        dz2_ref[...] = _layernorm_bwd(dy, xhat2, rstd2, g2_ref[...])

    tile = lambda w: pl.BlockSpec((ts, w), lambda i: (i, 0))
    acc = lambda w: pl.BlockSpec((1, w), lambda i: (0, 0))
    return pl.pallas_call(
        body, name="ffn_fwd_loss", grid=(s // ts,),
        in_specs=[tile(D_MODEL), tile(D_MODEL), tile(D_MODEL)] + [_whole()] * 8,
        out_specs=[tile(2 * D_FF), tile(D_MODEL), acc(1), acc(D_MODEL), acc(D_MODEL)],
        out_shape=[jax.ShapeDtypeStruct((s, 2 * D_FF), BF16), jax.ShapeDtypeStruct((s, D_MODEL), F32),
                   jax.ShapeDtypeStruct((1, 1), F32), jax.ShapeDtypeStruct((1, D_MODEL), F32),
                   jax.ShapeDtypeStruct((1, D_MODEL), F32)],
        scratch_shapes=[pltpu.VMEM((ts, 2 * D_FF), F32), pltpu.VMEM((ts + CONV_HALO, D_FF), F32)],
        compiler_params=_params(("arbitrary",)),
    )(xhat1, h1b, target, wup4, wdown, cw, cb, g1, b1, g2, b2)


def _ffn_bwd(dz2, ub, xhat1, rstd1, wup4, wdown, cw, cb, g1, ts):
    s = dz2.shape[0]
    nt = s // ts
    hb = 16

    def body(dz2_ref, ub_ref, prev_ref, xhat_ref, rstd_ref, wup_ref, wdn_ref, cw_ref, cb_ref, g1_ref,
             a_ref, dub_ref, dz1_ref, dg1_ref, db1_ref, dcw_ref, dcb_ref, gext_scr, dext_scr):
        i = pl.program_id(0)
        r = nt - 1 - i

        @pl.when(i == 0)
        def _():
            dext_scr[ts:ts + CONV_HALO, :] = jnp.zeros((CONV_HALO, D_FF), F32)
            dg1_ref[...] = jnp.zeros_like(dg1_ref)
            db1_ref[...] = jnp.zeros_like(db1_ref)
            dcw_ref[...] = jnp.zeros_like(dcw_ref)
            dcb_ref[...] = jnp.zeros_like(dcb_ref)

        dz2 = dz2_ref[...]
        da = _dot_nt(dz2.astype(BF16), wdn_ref[...])
        val = ub_ref[:, 0:D_FF].astype(F32)
        gate = ub_ref[:, D_FF:2 * D_FF].astype(F32)
        prev = prev_ref[...].astype(F32)[hb - CONV_HALO:hb]
        gext_scr[0:CONV_HALO, :] = jnp.where(r == 0, 0.0, prev)
        gext_scr[CONV_HALO:CONV_HALO + ts, :] = gate
        g2s = gext_scr[CONV_HALO - 2:CONV_HALO - 2 + ts, :]
        g1s = gext_scr[CONV_HALO - 1:CONV_HALO - 1 + ts, :]
        hc = _conv(cw_ref, cb_ref, g2s, g1s, gate)
        sg = _sigmoid(hc)
        si = hc * sg
        a_ref[...] = (si * val).astype(BF16)
        dhc = da * val * (sg * (1.0 + hc * (1.0 - sg)))
        dcb_ref[...] += jnp.sum(dhc, axis=0, keepdims=True)
        dcw_ref[0:1, :] += jnp.sum(dhc * g2s, axis=0, keepdims=True)
        dcw_ref[1:2, :] += jnp.sum(dhc * g1s, axis=0, keepdims=True)
        dcw_ref[2:3, :] += jnp.sum(dhc * gate, axis=0, keepdims=True)
        dext_scr[0:ts, :] = dhc
        dgate = (dhc * cw_ref[2:3, :] + dext_scr[1:1 + ts, :] * cw_ref[1:2, :]
                 + dext_scr[2:2 + ts, :] * cw_ref[0:1, :])
        dext_scr[ts:ts + CONV_HALO, :] = dext_scr[0:CONV_HALO, :]
        dub_ref[:, 0:D_FF] = (da * si).astype(BF16)
        dub_ref[:, D_FF:2 * D_FF] = dgate.astype(BF16)
        dh1 = ALPHA * dz2
        for j in range(N_SHARD):
            dh1 = dh1 + _dot_nt(dub_ref[:, j * UP_SH:(j + 1) * UP_SH], wup_ref[j])
        xhat = xhat_ref[...]
        dg1_ref[...] += jnp.sum(dh1 * xhat, axis=0, keepdims=True)
        db1_ref[...] += jnp.sum(dh1, axis=0, keepdims=True)
        dz1_ref[...] = _layernorm_bwd(dh1, xhat, rstd_ref[...], g1_ref[...])

    tile = lambda w: pl.BlockSpec((ts, w), lambda i: (nt - 1 - i, 0))
    acc = lambda rws, w: pl.BlockSpec((rws, w), lambda i: (0, 0))
    prev_spec = pl.BlockSpec((hb, D_FF), lambda i: (jnp.maximum((nt - 1 - i) * (ts // hb) - 1, 0), 1))
    return pl.pallas_call(
        body, name="ffn_bwd", grid=(nt,),
        in_specs=[tile(D_MODEL), tile(2 * D_FF), prev_spec, tile(D_MODEL), tile(1)] + [_whole()] * 5,
        out_specs=[tile(D_FF), tile(2 * D_FF), tile(D_MODEL), acc(1, D_MODEL), acc(1, D_MODEL),
                   acc(3, D_FF), acc(1, D_FF)],
        out_shape=[jax.ShapeDtypeStruct((s, D_FF), BF16), jax.ShapeDtypeStruct((s, 2 * D_FF), BF16),
                   jax.ShapeDtypeStruct((s, D_MODEL), F32), jax.ShapeDtypeStruct((1, D_MODEL), F32),
                   jax.ShapeDtypeStruct((1, D_MODEL), F32), jax.ShapeDtypeStruct((3, D_FF), F32),
                   jax.ShapeDtypeStruct((1, D_FF), F32)],
        scratch_shapes=[pltpu.VMEM((ts + CONV_HALO, D_FF), F32), pltpu.VMEM((ts + CONV_HALO, D_FF), F32)],
        compiler_params=_params(("arbitrary",)),
    )(dz2, ub, ub, xhat1, rstd1, wup4, wdown, cw, cb, g1)


def _mix_bwd(dz1, pooled, ret, g, wout, wpool, pscale, ts, riders=()):
    s = dz1.shape[0]
    nt = s // ts

    def body(dz1_ref, pooled_ref, ret_ref, g_ref, wout_ref, wp_ref, ps_ref,
             dret_ref, dgp_ref, dwp_ref, dps_ref, eext_scr):
        i = pl.program_id(0)
        r = nt - 1 - i

        @pl.when(i == 0)
        def _():
            eext_scr[ts:ts + POOL_HALO, :] = jnp.zeros((POOL_HALO, POOL_W), F32)
            dwp_ref[...] = jnp.zeros_like(dwp_ref)
            dps_ref[...] = jnp.zeros_like(dps_ref)

        dzb = dz1_ref[...].astype(BF16)
        dcat_r = _dot_nt(dzb, wout_ref[0:RET_W, :])
        dcat_p = _dot_nt(dzb, wout_ref[RET_W:2 * RET_W, :])
        pos = (r * ts + lax.broadcasted_iota(jnp.int32, (ts, 1), 0) + 1).astype(F32)
        dpooled = []
        for gi, w in enumerate(POOL_WINDOWS):
            sl = slice(gi * HEAD_DIM, (gi + 1) * HEAD_DIM)
            pb = pooled_ref[:, sl]
            dy = dcat_p[:, sl]
            dps_ref[:, sl] += jnp.sum(dy * _dot(pb, wp_ref[gi]), axis=0, keepdims=True)
            dlin = (dy * ps_ref[:, sl]).astype(BF16)
            dwp_ref[gi] += _dot_tn(pb, dlin)
            dpg = _dot_nt(dlin, wp_ref[gi])
            dpooled.append(dpg)
            eext_scr[0:ts, sl] = dpg / jnp.minimum(pos, float(w))
        for gi, w in enumerate(POOL_WINDOWS):
            sl = slice(gi * HEAD_DIM, (gi + 1) * HEAD_DIM)
            acc = eext_scr[:, sl]
            shift = 1
            while shift < w:
                acc = acc + pltpu.roll(acc, ts + POOL_HALO - shift, 0)
                shift *= 2
            dgp_ref[:, RET_W + gi * HEAD_DIM:RET_W + (gi + 1) * HEAD_DIM] = (acc[0:ts] - dpooled[gi]).astype(BF16)
        eext_scr[ts:ts + POOL_HALO, :] = eext_scr[0:POOL_HALO, :]
        for h in range(HEADS):
            sl = slice(h * HEAD_DIM, (h + 1) * HEAD_DIM)
            rt = ret_ref[:, sl]
            rr = lax.rsqrt(jnp.mean(rt * rt, axis=-1, keepdims=True) + RMS_EPS)
            rn = rt * rr
            gh = g_ref[:, sl]
            sg = _sigmoid(gh)
            dy = dcat_r[:, sl]
            dgp_ref[:, sl] = (dy * rn * (sg * (1.0 + gh * (1.0 - sg)))).astype(BF16)
            drn = dy * (gh * sg)
            dret_ref[:, sl] = (rr * (drn - rn * jnp.mean(drn * rn, axis=-1, keepdims=True))).astype(BF16)

    tile = lambda w: pl.BlockSpec((ts, w), lambda i: (nt - 1 - i, 0))
    return _call(
        body, name="mix_bwd", grid=(nt,),
        in_specs=[tile(D_MODEL), tile(POOL_W), tile(RET_W), tile(RET_W), _whole(), _whole(), _whole()],
        out_specs=[tile(RET_W), tile(2 * RET_W),
                   pl.BlockSpec((len(POOL_WINDOWS), HEAD_DIM, HEAD_DIM), lambda i: (0, 0, 0)),
                   pl.BlockSpec((1, POOL_W), lambda i: (0, 0))],
        out_shape=[jax.ShapeDtypeStruct((s, RET_W), BF16), jax.ShapeDtypeStruct((s, 2 * RET_W), BF16),
                   jax.ShapeDtypeStruct((len(POOL_WINDOWS), HEAD_DIM, HEAD_DIM), F32),
                   jax.ShapeDtypeStruct((1, POOL_W), F32)],
        scratch_shapes=[pltpu.VMEM((ts + POOL_HALO, POOL_W), F32)],
        sem=("arbitrary",), operands=(dz1, pooled, ret, g, wout, wpool, pscale), riders=riders,
    )


def _retention_bwd(q, k, v, dret, dgp, states, mask, qd, kd, cosf, sinf, riders=()):
    s = q.shape[0]
    ns = s // SUPER
    cdec = [gm ** float(SUPER) for gm in _gammas()]

    def body(q_ref, k_ref, v_ref, do_ref, dgp_ref, st_ref, mask_ref, qd_ref, kd_ref, cos_ref, sin_ref,
             dproj_ref, dstate_scr):
        i = pl.program_id(0)

        @pl.when(i == 0)
        def _():
            dstate_scr[...] = jnp.zeros_like(dstate_scr)

        cosf_t = cos_ref[...]
        sinf_t = sin_ref[...]
        for h in range(HEADS):
            sl = slice(h * HEAD_DIM, (h + 1) * HEAD_DIM)
            qh, kh, vh, doh = q_ref[:, sl], k_ref[:, sl], v_ref[:, sl], do_ref[:, sl]
            m = mask_ref[h]
            scb = (_dot_nt(qh, kh) * m).astype(BF16)
            dscb = (_dot_nt(doh, vh) * m).astype(BF16)
            stb = st_ref[0, h]
            dst = dstate_scr[h]
            dstb = dst.astype(BF16)
            qdb = (qh.astype(F32) * qd_ref[:, sl]).astype(BF16)
            kdb = (kh.astype(F32) * kd_ref[:, sl]).astype(BF16)
            dq = _dot(dscb, kh) + _dot_nt(doh, stb) * qd_ref[:, sl]
            dk = _dot_tn(dscb, qh) + _dot_nt(vh, dstb) * kd_ref[:, sl]
            dv = _dot_tn(scb, doh) + _dot(kdb, dstb)
            dstate_scr[h] = dst * cdec[h] + _dot_tn(qdb, doh)
            lo = h * HEAD_DIM
            dproj_ref[:, lo:lo + HEAD_DIM] = _rope_t(dq, cosf_t, sinf_t).astype(BF16)
            dproj_ref[:, RET_W + lo:RET_W + lo + HEAD_DIM] = _rope_t(dk * K_SCALE, cosf_t, sinf_t).astype(BF16)
            dproj_ref[:, 2 * RET_W + lo:2 * RET_W + lo + HEAD_DIM] = dv.astype(BF16)
        dproj_ref[:, 3 * RET_W:IN_W] = dgp_ref[...]

    tile = lambda w: pl.BlockSpec((SUPER, w), lambda i: (ns - 1 - i, 0))
    return _call(
        body, name="retention_bwd", grid=(ns,),
        in_specs=[tile(RET_W), tile(RET_W), tile(RET_W), tile(RET_W), tile(2 * RET_W),
                  pl.BlockSpec((1, HEADS, HEAD_DIM, HEAD_DIM), lambda i: (ns - 1 - i, 0, 0, 0)),
                  _whole(), _whole(), _whole(), tile(HEAD_DIM), tile(HEAD_DIM)],
        out_specs=[tile(IN_W)],
        out_shape=[jax.ShapeDtypeStruct((s, IN_W), BF16)],
        scratch_shapes=[pltpu.VMEM((HEADS, HEAD_DIM, HEAD_DIM), F32)],
        sem=("arbitrary",), operands=(q, k, v, dret, dgp, states, mask, qd, kd, cosf, sinf), riders=riders,
    )


def _dx(dz1, dproj, win4, ts, riders=()):
    s = dz1.shape[0]

    def body(dz1_ref, dp_ref, w_ref, dx_ref):
        acc = ALPHA * dz1_ref[...]
        for j in range(N_SHARD):
            acc = acc + _dot_nt(dp_ref[:, j * IN_SH:(j + 1) * IN_SH], w_ref[j])
        dx_ref[...] = acc

    tile = lambda w: pl.BlockSpec((ts, w), lambda i: (i, 0))
    return _call(
        body, name="dx", grid=(s // ts,),
        in_specs=[tile(D_MODEL), tile(IN_W), _whole()],
        out_specs=[tile(D_MODEL)],
        out_shape=[jax.ShapeDtypeStruct((s, D_MODEL), F32)],
        sem=("arbitrary",), operands=(dz1, dproj, win4), riders=riders,
    )


def _wgrad(a, b, tm, tn, tk, name, stacked, riders=()):
    s, m = a.shape
    n = b.shape[1]
    nk = s // tk

    def body(a_ref, b_ref, o32_ref, o16_ref, acc_scr):
        kk = pl.program_id(2)
        part = _dot_tn(a_ref[...], b_ref[...].astype(BF16))

        @pl.when(kk == 0)
        def _():
            acc_scr[...] = part

        @pl.when(kk > 0)
        def _():
            acc_scr[...] += part

        @pl.when(kk == nk - 1)
        def _():
            res = acc_scr[...]
            o32_ref[...] = res.reshape(o32_ref.shape)
            o16_ref[...] = res.astype(BF16).reshape(o16_ref.shape)

    if stacked:
        shape = (n // tn, m, tn)
        ospec = pl.BlockSpec((1, tm, tn), lambda mi, j, kk: (j, mi, 0))
    else:
        shape = (m, n)
        ospec = pl.BlockSpec((tm, tn), lambda mi, j, kk: (mi, j))
    return _call(
        body, name=name, grid=(m // tm, n // tn, nk),
        in_specs=[pl.BlockSpec((tk, tm), lambda mi, j, kk: (kk, mi)),
                  pl.BlockSpec((tk, tn), lambda mi, j, kk: (kk, j))],
        out_specs=[ospec, ospec],
        out_shape=[jax.ShapeDtypeStruct(shape, F32), jax.ShapeDtypeStruct(shape, BF16)],
        scratch_shapes=[pltpu.VMEM((tm, tn), F32)],
        sem=("arbitrary", "arbitrary", "arbitrary"), operands=(a, b), riders=riders,
    )


class _NoComm:
    def __init__(self, win4, wout, wup4, wdown):
        self.weights = dict(w_in=win4, w_out=wout, w_up=wup4, w_down=wdown)
        self.grads = {}

    def weight(self, name):
        return self.weights[name]

    def riders(self, call):
        return ()

    def landed(self, call, results):
        pass

    def gradient(self, name, g32, g16):
        self.grads[name] = (g32, g16)


def _local_step(x, target, cw, cb, wpool, pscale, g1, b1, g2, b2, comm):
    s = x.shape[0]
    ts_a = min(512, s)
    ts_f = min(256, s)
    tk = min(512, s)
    mask, qd, kd = _decay_tables()
    cosf, sinf = _rope_tables(s)
    wpool_b = wpool.astype(BF16)

    def run(call, fn, *args):
        outs, res = fn(*args, riders=comm.riders(call))
        comm.landed(call, res)
        return outs

    xb, q, k, v, g, pooled, cat = run("proj_pool", _proj_pool, x, comm.weight("w_in"), cosf, sinf, wpool_b,
                                      pscale, ts_a)
    ret, cat, states = run("retention_fwd", _retention_fwd, q, k, v, g, cat, mask, qd, kd)
    wout = comm.weight("w_out")
    xhat1, rstd1, h1b = run("outproj_ln1", _outproj_ln1, x, cat, wout, g1, b1, ts_a)
    wup4, wdown = comm.weight("w_up"), comm.weight("w_down")
    ub, dz2, loss, dg2, db2 = _ffn_fwd_loss(xhat1, h1b, target, wup4, wdown, cw, cb, g1, b1, g2, b2, ts_f)

    act, dub, dz1, dg1, db1, dcw, dcb = _ffn_bwd(dz2, ub, xhat1, rstd1, wup4, wdown, cw, cb, g1, ts_f)
    comm.gradient("w_down", *run("wgrad_down", _wgrad, act, dz2, D_FF // 2, D_MODEL, tk, "wgrad_down", False))
    comm.gradient("w_up", *run("wgrad_up", _wgrad, h1b, dub, D_MODEL, UP_SH, tk, "wgrad_up", True))
    dret, dgp, dwp, dps = run("mix_bwd", _mix_bwd, dz1, pooled, ret, g, wout, wpool_b, pscale, ts_a)
    dproj, = run("retention_bwd", _retention_bwd, q, k, v, dret, dgp, states, mask, qd, kd, cosf, sinf)
    comm.gradient("w_in", *run("wgrad_in", _wgrad, xb, dproj, D_MODEL, IN_SH, tk, "wgrad_in", True))
    comm.gradient("w_out", *run("wgrad_out", _wgrad, cat, dz1, D_MODEL, D_MODEL, tk, "wgrad_out", False))
    grad_x, = run("dx", _dx, dz1, dproj, comm.weight("w_in"), ts_a)
    small = dict(w_pool=dwp, pool_scale=dps, ln1_g=dg1, ln1_b=db1, conv_w=dcw, conv_b=dcb,
                 ln2_g=dg2, ln2_b=db2)
    return loss, grad_x, small


CAST_ROWS = 64
SHARD_SHAPES = ((D_MODEL, IN_SH), (OUT_SH, D_MODEL), (D_MODEL, UP_SH), (DOWN_SH, D_MODEL))
N_BIG = len(SHARD_SHAPES)
CW_PAD = (8, 768)


def _mesh_pos():
    return lax.axis_index("x"), lax.axis_index("y"), lax.axis_index("c")


def _other_chips(x, y):
    return [(1 - x, y), (x, 1 - y), (1 - x, 1 - y)]


def _half_rows(w, which):
    hr = SHARD_SHAPES[w][0] // 2
    return pl.ds(pl.multiple_of(which * hr, 16), hr)


def _gather_weights(shards, cw8, full):
    def body(*refs):
        in_refs = refs[:N_BIG]
        cw_ref = refs[N_BIG]
        out_refs = refs[N_BIG + 1:2 * N_BIG + 1]
        cwo_ref = refs[2 * N_BIG + 1]
        stage = refs[2 * N_BIG + 2:3 * N_BIG + 2]
        send_sems, recv_sems, fsend_sems, frecv_sems, cw_send, cw_recv, local_sems = refs[3 * N_BIG + 2:]
        x, y, c = _mesh_pos()
        j0 = 2 * x + y
        chips = _other_chips(x, y)

        for w in range(N_BIG):
            def cast(i, carry, w=w):
                rows = pl.ds(pl.multiple_of(i * CAST_ROWS, CAST_ROWS), CAST_ROWS)
                stage[w][rows, :] = in_refs[w][rows, :].astype(BF16)
                return carry
            lax.fori_loop(0, SHARD_SHAPES[w][0] // CAST_ROWS, cast, 0)

        local = [pltpu.make_async_copy(stage[w], out_refs[w].at[j0], local_sems.at[w]) for w in range(N_BIG)]
        local.append(pltpu.make_async_copy(cw_ref, cwo_ref.at[j0], local_sems.at[N_BIG]))
        for cp in local:
            cp.start()

        def ici(w, k, block):
            chip = chips[k]
            return pltpu.make_async_remote_copy(
                src_ref=stage[w].at[_half_rows(w, c), :], dst_ref=out_refs[w].at[block, _half_rows(w, c), :],
                send_sem=send_sems.at[w, k], recv_sem=recv_sems.at[w, k],
                device_id=(chip[0], chip[1], c), device_id_type=MESH)

        def d2d(w, k, block, half):
            return pltpu.make_async_remote_copy(
                src_ref=out_refs[w].at[block, _half_rows(w, half), :],
                dst_ref=out_refs[w].at[block, _half_rows(w, half), :],
                send_sem=fsend_sems.at[w, k], recv_sem=frecv_sems.at[w, k],
                device_id=(x, y, 1 - c), device_id_type=MESH)

        def conv(k, block):
            chip = chips[k]
            return pltpu.make_async_remote_copy(
                src_ref=cw_ref, dst_ref=cwo_ref.at[block], send_sem=cw_send.at[k], recv_sem=cw_recv.at[k],
                device_id=(chip[0], chip[1], c), device_id_type=MESH)

        sent = [ici(w, k, j0) for w in full for k in range(3)] + [conv(k, j0) for k in range(3)]
        for cp in sent:
            cp.start()
        for k, chip in enumerate(chips):
            jk = 2 * chip[0] + chip[1]
            for w in full:
                ici(w, k, jk).wait_recv()
                fw = d2d(w, k, jk, c)
                fw.start()
                sent.append(fw)
        for k, chip in enumerate(chips):
            jk = 2 * chip[0] + chip[1]
            for w in full:
                d2d(w, k, jk, 1 - c).wait_recv()
            conv(k, jk).wait_recv()
        for cp in sent:
            cp.wait_send()
        for cp in local:
            cp.wait()

    out_shape = [jax.ShapeDtypeStruct((N_SHARD,) + shp, BF16) for shp in SHARD_SHAPES]
    out_shape.append(jax.ShapeDtypeStruct((N_SHARD,) + CW_PAD, F32))
    return pl.pallas_call(
        body, name="gather_weights",
        in_specs=[_whole()] * (N_BIG + 1),
        out_specs=[HBM_SPEC] * (N_BIG + 1),
        out_shape=out_shape,
        scratch_shapes=[pltpu.VMEM(shp, BF16) for shp in SHARD_SHAPES] + [
            pltpu.SemaphoreType.DMA((N_BIG, 3)), pltpu.SemaphoreType.DMA((N_BIG, 3)),
            pltpu.SemaphoreType.DMA((N_BIG, 3)), pltpu.SemaphoreType.DMA((N_BIG, 3)),
            pltpu.SemaphoreType.DMA((3,)), pltpu.SemaphoreType.DMA((3,)),
            pltpu.SemaphoreType.DMA((N_BIG + 1,))],
        compiler_params=pltpu.CompilerParams(vmem_limit_bytes=VMEM_LIMIT),
    )(*shards, cw8)


def _gather_rider(arrays, ops):
    ws = sorted(arrays)

    def make(inplace, srcs, lands, send_sems, recv_sems):
        del srcs, lands
        x, y, c = _mesh_pos()
        j0 = 2 * x + y
        chips = _other_chips(x, y)
        starts, waits = [], []
        for n, (kind, w, (r0, nr)) in enumerate(ops):
            ref = inplace[ws.index(w)]
            hr = SHARD_SHAPES[w][0] // 2
            rows = lambda core: pl.ds(pl.multiple_of(core * hr + r0, 16), nr)
            for k, chip in enumerate(chips):
                jk = 2 * chip[0] + chip[1]
                if kind == "ici":
                    src, to, landing = ref.at[j0, rows(c), :], (chip[0], chip[1], c), ref.at[jk, rows(c), :]
                else:
                    src, to, landing = ref.at[jk, rows(c), :], (x, y, 1 - c), ref.at[jk, rows(1 - c), :]
                sems = dict(send_sem=send_sems.at[3 * n + k], recv_sem=recv_sems.at[3 * n + k],
                            device_id=to, device_id_type=MESH)
                send = pltpu.make_async_remote_copy(src_ref=src, dst_ref=src, **sems)
                arrival = pltpu.make_async_remote_copy(src_ref=src, dst_ref=landing, **sems)
                starts.append(send)
                waits += [arrival.wait_recv, send.wait_send]
        return starts, waits

    return _Rider([arrays[w] for w in ws], [], [], 3 * len(ops), make)


def _whole_half(w):
    return (0, SHARD_SHAPES[w][0] // 2)


def _pair_rider(ws, g16s):
    def make(inplace, srcs, lands, send_sems, recv_sems):
        del inplace
        x, y, c = _mesh_pos()
        copies = [pltpu.make_async_remote_copy(
            src_ref=srcs[i].at[:, _half_rows(w, 1 - c), :], dst_ref=lands[i],
            send_sem=send_sems.at[i], recv_sem=recv_sems.at[i], device_id=(x, y, 1 - c), device_id_type=MESH)
            for i, w in enumerate(ws)]
        return copies, [cp.wait for cp in copies]

    lands = [jax.ShapeDtypeStruct((N_SHARD, SHARD_SHAPES[w][0] // 2, SHARD_SHAPES[w][1]), BF16) for w in ws]
    return _Rider([], g16s, lands, len(ws), make)


def _chip_rider(ws, p16s):
    def make(inplace, srcs, lands, send_sems, recv_sems):
        del inplace
        x, y, c = _mesh_pos()
        copies = []
        for i in range(len(ws)):
            for k, chip in enumerate(_other_chips(x, y)):
                copies.append(pltpu.make_async_remote_copy(
                    src_ref=srcs[i].at[2 * chip[0] + chip[1]], dst_ref=lands[i].at[k],
                    send_sem=send_sems.at[3 * i + k], recv_sem=recv_sems.at[3 * i + k],
                    device_id=(chip[0], chip[1], c), device_id_type=MESH))
        return copies, [cp.wait for cp in copies]

    lands = [jax.ShapeDtypeStruct((3, SHARD_SHAPES[w][0] // 2, SHARD_SHAPES[w][1]), BF16) for w in ws]
    return _Rider([], p16s, lands, 3 * len(ws), make)


def _final_rider(halves):
    def make(inplace, srcs, lands, send_sems, recv_sems):
        del inplace
        x, y, c = _mesh_pos()
        copies = [pltpu.make_async_remote_copy(
            src_ref=srcs[i], dst_ref=lands[i], send_sem=send_sems.at[i], recv_sem=recv_sems.at[i],
            device_id=(x, y, 1 - c), device_id_type=MESH) for i in range(len(halves))]
        return copies, [cp.wait for cp in copies]

    return _Rider([], halves, [jax.ShapeDtypeStruct(h.shape, h.dtype) for h in halves], len(halves), make)


def _comm_only(name, riders):
    _, res = _call(lambda: None, name=name, grid=(), in_specs=[], out_specs=[], out_shape=[], operands=(),
                   riders=riders)
    return res


def _pair_sum(pos, ws, g32s, recvs):
    n = len(ws)

    def body(pos_ref, *refs):
        del pos_ref
        g_refs, r_refs = refs[:n], refs[n:2 * n]
        p32_refs, p16_refs = refs[2 * n:3 * n], refs[3 * n:]
        for i in range(n):
            tot = g_refs[i][...] + r_refs[i][...].astype(F32)
            p32_refs[i][...] = tot
            p16_refs[i][...] = tot.astype(BF16)

    halves = [(SHARD_SHAPES[w][0] // 2, SHARD_SHAPES[w][1]) for w in ws]
    own = [pl.BlockSpec((None, None) + h, lambda j, pos_ref: (j, pos_ref[0], 0, 0)) for h in halves]
    blk = [pl.BlockSpec((None,) + h, lambda j, pos_ref: (j, 0, 0)) for h in halves]
    g4 = [g.reshape((N_SHARD, 2) + h) for g, h in zip(g32s, halves)]
    outs = pl.pallas_call(
        body, name="pair_sum_" + "_".join(str(w) for w in ws),
        grid_spec=pltpu.PrefetchScalarGridSpec(
            num_scalar_prefetch=1, grid=(N_SHARD,), in_specs=own + blk, out_specs=blk + blk),
        out_shape=[jax.ShapeDtypeStruct((N_SHARD,) + h, F32) for h in halves]
        + [jax.ShapeDtypeStruct((N_SHARD,) + h, BF16) for h in halves],
        compiler_params=_params(("arbitrary",)),
    )(pos, *g4, *recvs)
    return outs[:n], outs[n:]


def _chip_sum(pos, p32s, recvs):
    parts = 2

    def body(pos_ref, *refs):
        del pos_ref
        p_refs, r_refs, f_refs = refs[:N_BIG], refs[N_BIG:2 * N_BIG], refs[2 * N_BIG:]
        for w in range(N_BIG):
            f_refs[w][...] = ((p_refs[w][...] + r_refs[w][0].astype(F32)) + r_refs[w][1].astype(F32)) \
                + r_refs[w][2].astype(F32)

    quarters = [(r // 2 // parts, cc) for r, cc in SHARD_SHAPES]
    own = [pl.BlockSpec((None,) + qt, lambda i, pos_ref: (pos_ref[1], i, 0)) for qt in quarters]
    rcv = [pl.BlockSpec((3,) + qt, lambda i, pos_ref: (0, i, 0)) for qt in quarters]
    out = [pl.BlockSpec(qt, lambda i, pos_ref: (i, 0)) for qt in quarters]
    return pl.pallas_call(
        body, name="chip_sum",
        grid_spec=pltpu.PrefetchScalarGridSpec(
            num_scalar_prefetch=1, grid=(parts,), in_specs=own + rcv, out_specs=out),
        out_shape=[jax.ShapeDtypeStruct((r // 2, cc), F32) for r, cc in SHARD_SHAPES],
        compiler_params=_params(("arbitrary",)),
    )(pos, *p32s, *recvs)


def _adamw(w, g, m, v):
    m_new = ADAM_B1 * m + (1.0 - ADAM_B1) * g
    v_new = ADAM_B2 * v + (1.0 - ADAM_B2) * (g * g)
    m_hat = m_new / (1.0 - ADAM_B1 ** ADAM_STEP)
    v_hat = v_new / (1.0 - ADAM_B2 ** ADAM_STEP)
    delta = -ADAM_LR * (m_hat / (jnp.sqrt(v_hat) + ADAM_EPS) + ADAM_WD * w)
    return delta, m_new, v_new


def _adam_big(pos, mine, theirs, ws, ms, vs):
    nb = 4

    def body(pos_ref, *refs):
        hf = pl.program_id(0)
        groups = [refs[i * N_BIG:(i + 1) * N_BIG] for i in range(9)]
        f_refs, t_refs, w_refs, m_refs, v_refs, go_refs, do_refs, mo_refs, vo_refs = groups
        for w in range(N_BIG):
            g = jnp.where(hf == pos_ref[0], f_refs[w][...], t_refs[w][...])
            delta, m_new, v_new = _adamw(w_refs[w][...], g, m_refs[w][...], v_refs[w][...])
            go_refs[w][...] = g
            do_refs[w][...] = delta
            mo_refs[w][...] = m_new
            vo_refs[w][...] = v_new

    blocks = [(r // 2 // nb, cc) for r, cc in SHARD_SHAPES]
    half = [pl.BlockSpec(b, lambda hf, i, pos_ref: (i, 0)) for b in blocks]
    full = [pl.BlockSpec((None,) + b, lambda hf, i, pos_ref: (0, hf * nb + i, 0)) for b in blocks]
    shapes = [jax.ShapeDtypeStruct((1,) + shp, F32) for shp in SHARD_SHAPES]
    outs = pl.pallas_call(
        body, name="adam_big",
        grid_spec=pltpu.PrefetchScalarGridSpec(
            num_scalar_prefetch=1, grid=(2, nb), in_specs=half + half + full * 3, out_specs=full * 4),
        out_shape=shapes * 4,
        compiler_params=_params(("arbitrary", "arbitrary")),
    )(pos, *mine, *theirs, *ws, *ms, *vs)
    return [outs[i * N_BIG:(i + 1) * N_BIG] for i in range(4)]


SMALL_ROWS = 8
ROW_CONV_B, ROW_POOL_SCALE, ROW_LN1_G, ROW_LN1_B, ROW_LN2_G, ROW_LN2_B, ROW_LOSS = range(7)
SMALL_VECS = ((ROW_CONV_B, D_FF), (ROW_POOL_SCALE, POOL_W), (ROW_LN1_G, D_MODEL), (ROW_LN1_B, D_MODEL),
              (ROW_LN2_G, D_MODEL), (ROW_LN2_B, D_MODEL))


def _small_update(loss, dwp, dcw4, vec_grads, wp, cwp, vec_ws, m_wp, m_cwp, vec_ms, v_wp, v_cwp, vec_vs):
    nv = len(SMALL_VECS)

    def body(*refs):
        loss_ref, dwp_ref, dcw_ref = refs[0:3]
        gvec = refs[3:3 + nv]
        o = 3 + nv
        wp_ref, cw_ref = refs[o:o + 2]
        wvec = refs[o + 2:o + 2 + nv]
        o += 2 + nv
        mwp_ref, mcw_ref = refs[o:o + 2]
        mvec = refs[o + 2:o + 2 + nv]
        o += 2 + nv
        vwp_ref, vcw_ref = refs[o:o + 2]
        vvec = refs[o + 2:o + 2 + nv]
        o += 2 + nv
        loss_out = refs[o]
        outs = refs[o + 1:o + 1 + 4 * (2 + nv)]
        o += 1 + 4 * (2 + nv)
        (vec_scr, sib_a, sib_b, sib_c, all_a, all_b, all_c,
         send1, recv1, send2, recv2) = refs[o:]
        x, y, c = _mesh_pos()
        j0 = 2 * x + y
        chips = _other_chips(x, y)

        vec_scr[...] = jnp.zeros_like(vec_scr)
        for (row, n), ref in zip(SMALL_VECS, gvec):
            vec_scr[row:row + 1, 0:n] = ref[...]
        vec_scr[ROW_LOSS:ROW_LOSS + 1, 0:HEAD_DIM] = jnp.broadcast_to(loss_ref[...], (1, HEAD_DIM))

        mine = (dwp_ref, vec_scr, dcw_ref)
        sib = (sib_a, sib_b, sib_c)
        every = (all_a, all_b, all_c)
        first = [pltpu.make_async_remote_copy(
            src_ref=mine[b], dst_ref=sib[b], send_sem=send1.at[b], recv_sem=recv1.at[b],
            device_id=(x, y, 1 - c), device_id_type=MESH) for b in range(3)]
        for cp in first:
            cp.start()
        for cp in first:
            cp.wait()
        for b in range(3):
            every[b][j0] = mine[b][...] + sib[b][...]

        def ici(b, k, block):
            chip = chips[k]
            return pltpu.make_async_remote_copy(
                src_ref=every[b].at[block], dst_ref=every[b].at[block],
                send_sem=send2.at[b, k], recv_sem=recv2.at[b, k],
                device_id=(chip[0], chip[1], c), device_id_type=MESH)

        second = [ici(b, k, j0) for b in range(3) for k in range(3)]
        for cp in second:
            cp.start()
        for k, chip in enumerate(chips):
            for b in range(3):
                ici(b, k, 2 * chip[0] + chip[1]).wait_recv()
        for cp in second:
            cp.wait_send()

        tot_a = ((all_a[0] + all_a[1]) + all_a[2]) + all_a[3]
        tot_b = ((all_b[0] + all_b[1]) + all_b[2]) + all_b[3]
        all_c[0] = ((all_c[0] + all_c[1]) + all_c[2]) + all_c[3]
        tot_c = all_c[0, j0]
        loss_out[...] = tot_b[ROW_LOSS:ROW_LOSS + 1, 0:1]

        grads = [tot_a, tot_c] + [tot_b[row:row + 1, 0:n] for row, n in SMALL_VECS]
        w_all = [wp_ref, cw_ref] + list(wvec)
        m_all = [mwp_ref, mcw_ref] + list(mvec)
        v_all = [vwp_ref, vcw_ref] + list(vvec)
        np_ = 2 + nv
        for p in range(np_):
            g = grads[p]
            delta, m_new, v_new = _adamw(w_all[p][...], g, m_all[p][...], v_all[p][...])
            outs[p][...] = g
            outs[np_ + p][...] = delta
            outs[2 * np_ + p][...] = m_new
            outs[3 * np_ + p][...] = v_new

    pshapes = [wp.shape, CW_PAD] + [wv.shape for wv in vec_ws]
    out_shape = [jax.ShapeDtypeStruct((1, 1), F32)] + [jax.ShapeDtypeStruct(s, F32) for s in pshapes] * 4
    a_shape = dwp.shape
    b_shape = (SMALL_ROWS, D_FF)
    c_shape = dcw4.shape
    n_in = 3 + nv + 3 * (2 + nv)
    outs = pl.pallas_call(
        body, name="small_update",
        in_specs=[_whole()] * n_in, out_specs=[_whole()] * len(out_shape), out_shape=out_shape,
        scratch_shapes=[pltpu.VMEM(b_shape, F32),
                        pltpu.VMEM(a_shape, F32), pltpu.VMEM(b_shape, F32), pltpu.VMEM(c_shape, F32),
                        pltpu.VMEM((N_SHARD,) + a_shape, F32), pltpu.VMEM((N_SHARD,) + b_shape, F32),
                        pltpu.VMEM((N_SHARD,) + c_shape, F32),
                        pltpu.SemaphoreType.DMA((3,)), pltpu.SemaphoreType.DMA((3,)),
                        pltpu.SemaphoreType.DMA((3, 3)), pltpu.SemaphoreType.DMA((3, 3))],
        compiler_params=pltpu.CompilerParams(vmem_limit_bytes=VMEM_LIMIT),
    )(loss, dwp, dcw4, *vec_grads, wp, cwp, *vec_ws, m_wp, m_cwp, *vec_ms, v_wp, v_cwp, *vec_vs)
    np_ = 2 + nv
    return outs[0], [outs[1 + i * np_:1 + (i + 1) * np_] for i in range(4)]


def _pad_cw(a):
    pad = [(0, 0)] * (a.ndim - 2) + [(0, CW_PAD[0] - a.shape[-2]), (0, CW_PAD[1] - a.shape[-1])]
    return jnp.pad(a, pad)


def kernel(x, w_in, w_pool, pool_scale, w_out, ln1_g, ln1_b, w_up, conv_w, conv_b, w_down, ln2_g, ln2_b, loss_target, m_w_in, m_w_pool, m_pool_scale, m_w_out, m_ln1_g, m_ln1_b, m_w_up, m_conv_w, m_conv_b, m_w_down, m_ln2_g, m_ln2_b, v_w_in, v_w_pool, v_pool_scale, v_w_out, v_ln1_g, v_ln1_b, v_w_up, v_conv_w, v_conv_b, v_w_down, v_ln2_g, v_ln2_b):
    pos = jnp.stack([lax.axis_index("c"), 2 * lax.axis_index("x") + lax.axis_index("y")]).astype(jnp.int32)
    order = ("w_in", "w_out", "w_up", "w_down")
    w_in_i, w_out_i, w_up_i, w_down_i = range(N_BIG)

    gathered = _gather_weights([w_in[0], w_out[0], w_up[0], w_down[0]], _pad_cw(conv_w[0]), (w_in_i,))
    cw_full = jnp.transpose(gathered[N_BIG][:, 0:3, 0:DOWN_SH], (1, 0, 2)).reshape(3, D_FF)
    up_half = SHARD_SHAPES[w_up_i][0] // 2
    up_a, up_b = (0, up_half // 2), (up_half // 2, up_half // 2)

    class MeshComm:
        def __init__(self):
            self.w = {i: gathered[i] for i in range(N_BIG)}
            self.g32, self.g16, self.recv_a, self.p32, self.p16, self.recv_b = {}, {}, {}, {}, {}, {}
            self.up_complete = False

        def weight(self, name):
            i = order.index(name)
            if name == "w_up" and not self.up_complete:
                (arrs, _), = _comm_only("gather_up_last", [_gather_rider({i: self.w[i]}, [("d2d", i, up_b)])])
                self.w[i], self.up_complete = arrs[0], True
            full = self.w[i]
            return full.reshape(-1, full.shape[-1]) if name in ("w_out", "w_down") else full

        def _gather(self, ws, ops):
            return _gather_rider({w: self.w[w] for w in ws}, ops), ("w", ws)

        def _pair(self, ws):
            return _pair_rider(ws, [self.g16[w] for w in ws]), ("recv_a", ws)

        def _chip(self, ws):
            return _chip_rider(ws, [self.p16[w] for w in ws]), ("recv_b", ws)

        def plan(self, call):
            if call == "proj_pool":
                return [self._gather([w_out_i, w_down_i], [("ici", w_out_i, _whole_half(w_out_i)),
                                                          ("ici", w_down_i, _whole_half(w_down_i))])]
            if call == "retention_fwd":
                return [self._gather([w_out_i, w_up_i, w_down_i],
                                     [("d2d", w_out_i, _whole_half(w_out_i)),
                                      ("d2d", w_down_i, _whole_half(w_down_i)), ("ici", w_up_i, up_a)])]
            if call == "outproj_ln1":
                return [self._gather([w_up_i], [("ici", w_up_i, up_b), ("d2d", w_up_i, up_a)])]
            if call == "mix_bwd":
                return [self._pair([w_up_i, w_down_i])]
            if call == "retention_bwd":
                return [self._chip([w_down_i])]
            if call == "wgrad_in":
                return [self._chip([w_up_i])]
            if call == "wgrad_out":
                return [self._pair([w_in_i])]
            if call == "dx":
                return [self._chip([w_in_i]), self._pair([w_out_i])]
            return []

        def riders(self, call):
            self.pending = self.plan(call)
            return [r for r, _ in self.pending]

        def landed(self, call, results):
            for (_, (slot, ws)), (inplace, lands) in zip(self.pending, results):
                for w, arr in zip(ws, inplace if slot == "w" else lands):
                    getattr(self, slot)[w] = arr
            if call == "mix_bwd":
                self._sum([w_up_i, w_down_i])
            if call == "wgrad_out":
                self._sum([w_in_i])
            if call == "dx":
                self._sum([w_out_i])
                (_, lands), = _comm_only("chip_exchange_out", [self._chip([w_out_i])[0]])
                self.recv_b[w_out_i] = lands[0]

        def _sum(self, ws):
            p32s, p16s = _pair_sum(pos, ws, [self.g32[w] for w in ws], [self.recv_a[w] for w in ws])
            for w, p32, p16 in zip(ws, p32s, p16s):
                self.p32[w], self.p16[w] = p32, p16

        def gradient(self, name, g32, g16):
            w = order.index(name)
            shape = (N_SHARD,) + SHARD_SHAPES[w]
            self.g32[w], self.g16[w] = g32.reshape(shape), g16.reshape(shape)

    comm = MeshComm()
    loss, grad_x, small = _local_step(x[0], loss_target[0], cw_full, conv_b, w_pool[0], pool_scale,
                                      ln1_g, ln1_b, ln2_g, ln2_b, comm)
    every = range(N_BIG)
    mine = _chip_sum(pos, [comm.p32[w] for w in every], [comm.recv_b[w] for w in every])
    (_, theirs), = _comm_only("pair_exchange_f32", [_final_rider(mine)])
    big_out = _adam_big(pos, mine, theirs, [w_in, w_out, w_up, w_down], [m_w_in, m_w_out, m_w_up, m_w_down],
                        [v_w_in, v_w_out, v_w_up, v_w_down])

    dcw4 = _pad_cw(jnp.transpose(small["conv_w"].reshape(3, N_SHARD, DOWN_SH), (1, 0, 2)))
    vec_names = ("conv_b", "pool_scale", "ln1_g", "ln1_b", "ln2_g", "ln2_b")
    given = dict(w_pool=w_pool, pool_scale=pool_scale, ln1_g=ln1_g, ln1_b=ln1_b, conv_w=conv_w, conv_b=conv_b,
                 ln2_g=ln2_g, ln2_b=ln2_b)
    given_m = dict(w_pool=m_w_pool, pool_scale=m_pool_scale, ln1_g=m_ln1_g, ln1_b=m_ln1_b, conv_w=m_conv_w,
                   conv_b=m_conv_b, ln2_g=m_ln2_g, ln2_b=m_ln2_b)
    given_v = dict(w_pool=v_w_pool, pool_scale=v_pool_scale, ln1_g=v_ln1_g, ln1_b=v_ln1_b, conv_w=v_conv_w,
                   conv_b=v_conv_b, ln2_g=v_ln2_g, ln2_b=v_ln2_b)
    args = []
    for src in (given, given_m, given_v):
        args += [src["w_pool"][0], _pad_cw(src["conv_w"][0]), [src[n] for n in vec_names]]
    loss_tot, small_out = _small_update(loss, small["w_pool"], dcw4, [small[n] for n in vec_names], *args)

    names = ("w_in", "w_pool", "pool_scale", "w_out", "ln1_g", "ln1_b", "w_up", "conv_w", "conv_b", "w_down",
             "ln2_g", "ln2_b")
    small_names = ("w_pool", "conv_w") + vec_names
    result = [loss_tot.reshape(()), grad_x[None]]
    for kind in range(4):
        for n in names:
            if n in order:
                result.append(big_out[kind][order.index(n)])
            else:
                val = small_out[kind][small_names.index(n)]
                if n == "conv_w":
                    val = val[0:3, 0:DOWN_SH][None]
                elif n == "w_pool":
                    val = val[None]
                result.append(val)
    return tuple(result)
```

```python
import functools
import math

import numpy as np
import jax
import jax.numpy as jnp
from jax import lax
from jax.experimental import pallas as pl
from jax.experimental.pallas import tpu as pltpu

F32 = jnp.float32
BF16 = jnp.bfloat16

D_MODEL = 1024
HEADS = 4
HEAD_DIM = 128
RET_W = HEADS * HEAD_DIM
POOL_WINDOWS = (2, 4, 8, 16)
POOL_W = 512
IN_W = 4 * RET_W + POOL_W
D_FF = 2816
N_SHARD = 4
IN_SH = IN_W // N_SHARD
UP_SH = 2 * D_FF // N_SHARD
DOWN_SH = D_FF // N_SHARD
OUT_SH = D_MODEL // N_SHARD
ROPE_BASE = 10000.0
LN_EPS = 1e-5
RMS_EPS = 1e-6
ALPHA = 2.0 ** 0.25
K_SCALE = HEAD_DIM ** -0.5
SUPER = 256
CHUNK = 64
POOL_HALO = 16
CONV_HALO = 8

ADAM_LR = 0.001
ADAM_B1 = 0.9
ADAM_B2 = 0.999
ADAM_EPS = 1e-08
ADAM_WD = 0.01
ADAM_STEP = 10

MESH = pl.DeviceIdType.MESH
VMEM_LIMIT = 56 * 1024 * 1024


def _dot(a, b):
    return jnp.dot(a, b, preferred_element_type=F32)


def _dot_nt(a, b):
    return lax.dot_general(a, b, (((1,), (1,)), ((), ())), preferred_element_type=F32)


def _dot_tn(a, b):
    return lax.dot_general(a, b, (((0,), (0,)), ((), ())), preferred_element_type=F32)


def _sigmoid(x):
    return 1.0 / (1.0 + jnp.exp(-x))


def _params(sem):
    return pltpu.CompilerParams(dimension_semantics=sem, vmem_limit_bytes=VMEM_LIMIT)


def _whole():
    return pl.BlockSpec(memory_space=pltpu.VMEM)


HBM_SPEC = pl.BlockSpec(memory_space=pl.ANY)


class _Rider:
    def __init__(self, inplace, srcs, lands, n_copies, make):
        self.inplace, self.srcs, self.lands, self.n_copies, self.make = list(inplace), list(srcs), list(lands), n_copies, make


def _call(body, *, name, grid, in_specs, out_specs, out_shape, operands, scratch_shapes=(), sem=(),
          aliases=None, riders=()):
    n_in, n_out, n_scr = len(in_specs), len(out_shape), len(scratch_shapes)
    in_specs, out_specs, out_shape = list(in_specs), list(out_specs), list(out_shape)
    operands, scratch_shapes, aliases = list(operands), list(scratch_shapes), dict(aliases or {})
    for r in riders:
        for a in r.inplace:
            aliases[len(in_specs)] = len(out_shape)
            in_specs.append(HBM_SPEC)
            operands.append(a)
            out_specs.append(HBM_SPEC)
            out_shape.append(jax.ShapeDtypeStruct(a.shape, a.dtype))
        for a in r.srcs:
            in_specs.append(HBM_SPEC)
            operands.append(a)
        for shp in r.lands:
            out_specs.append(HBM_SPEC)
            out_shape.append(shp)
        scratch_shapes += [pltpu.SemaphoreType.DMA((r.n_copies,)), pltpu.SemaphoreType.DMA((r.n_copies,))]

    def full(*refs):
        ins = refs[:n_in]
        at = n_in
        r_srcs = []
        for r in riders:
            at += len(r.inplace)
            r_srcs.append(refs[at:at + len(r.srcs)])
            at += len(r.srcs)
        outs = refs[at:at + n_out]
        at += n_out
        r_outs = []
        for r in riders:
            r_outs.append((refs[at:at + len(r.inplace)], refs[at + len(r.inplace):at + len(r.inplace) + len(r.lands)]))
            at += len(r.inplace) + len(r.lands)
        scr = refs[at:at + n_scr]
        at += n_scr
        r_sems = [refs[at + 2 * i:at + 2 * i + 2] for i in range(len(riders))]

        def copies():
            return [r.make(r_outs[i][0], r_srcs[i], r_outs[i][1], r_sems[i][0], r_sems[i][1])
                    for i, r in enumerate(riders)]

        def start():
            for starts, _ in copies():
                for cp in starts:
                    cp.start()

        def finish():
            for _, waits in copies():
                for wait in waits:
                    wait()

        if riders and grid:
            first = functools.reduce(jnp.logical_and, [pl.program_id(d) == 0 for d in range(len(grid))])
            last = functools.reduce(jnp.logical_and, [pl.program_id(d) == grid[d] - 1 for d in range(len(grid))])
            pl.when(first)(start)
            body(*ins, *outs, *scr)
            pl.when(last)(finish)
        else:
            if riders:
                start()
            body(*ins, *outs, *scr)
            if riders:
                finish()

    params = _params(sem) if grid else pltpu.CompilerParams(vmem_limit_bytes=VMEM_LIMIT)
    res = pl.pallas_call(
        full, name=name, grid=grid, in_specs=in_specs, out_specs=out_specs, out_shape=out_shape,
        scratch_shapes=scratch_shapes, input_output_aliases=aliases, compiler_params=params,
    )(*operands)
    outs, at, rider_res = res[:n_out], n_out, []
    for r in riders:
        rider_res.append((res[at:at + len(r.inplace)], res[at + len(r.inplace):at + len(r.inplace) + len(r.lands)]))
        at += len(r.inplace) + len(r.lands)
    return list(outs), rider_res


def _gammas():
    return [1.0 - 2.0 ** (-5.0 - h) for h in range(HEADS)]


def _decay_tables():
    idx = np.arange(SUPER)
    dist = np.abs(idx[:, None] - idx[None, :]).astype(np.float64)
    visible = (idx[None, :] // CHUNK) <= (idx[:, None] // CHUNK)
    mask = np.stack([np.where(visible, g ** dist, 0.0) for g in _gammas()])
    qd = np.concatenate([np.repeat((g ** (idx + 1.0))[:, None], HEAD_DIM, 1) for g in _gammas()], 1)
    kd = np.concatenate([np.repeat((g ** (SUPER - 1.0 - idx))[:, None], HEAD_DIM, 1) for g in _gammas()], 1)
    return (jnp.asarray(mask, F32), jnp.asarray(qd, F32), jnp.asarray(kd, F32))


def _rope_tables(s):
    inv_freq = ROPE_BASE ** (-np.arange(0, HEAD_DIM, 2, dtype=np.float64) / HEAD_DIM)
    ang = np.arange(s, dtype=np.float64)[:, None] * inv_freq[None, :]
    cos, sin = np.cos(ang), np.sin(ang)
    return (jnp.asarray(np.concatenate([cos, cos], 1), F32),
            jnp.asarray(np.concatenate([-sin, sin], 1), F32))


def _rope(t, cosf, sinf):
    return t * cosf + pltpu.roll(t, HEAD_DIM // 2, 1) * sinf


def _rope_t(t, cosf, sinf):
    return t * cosf - pltpu.roll(t, HEAD_DIM // 2, 1) * sinf


def _layernorm_fwd(z):
    mu = jnp.mean(z, axis=-1, keepdims=True)
    zc = z - mu
    var = jnp.mean(zc * zc, axis=-1, keepdims=True)
    rstd = lax.rsqrt(var + LN_EPS)
    return zc * rstd, rstd


def _layernorm_bwd(dy, xhat, rstd, gain):
    dxh = dy * gain
    m1 = jnp.mean(dxh, axis=-1, keepdims=True)
    m2 = jnp.mean(dxh * xhat, axis=-1, keepdims=True)
    return rstd * (dxh - m1 - xhat * m2)


def _proj_pool(x, win4, cosf, sinf, wpool, pscale, ts, riders=()):
    s = x.shape[0]
    nt = s // ts

    def body(x_ref, w_ref, cos_ref, sin_ref, wp_ref, ps_ref,
             xb_ref, q_ref, k_ref, v_ref, g_ref, pooled_ref, cat_ref, proj_scr, pext_scr):
        i = pl.program_id(0)
        xb = x_ref[...].astype(BF16)
        xb_ref[...] = xb
        for j in range(N_SHARD):
            proj_scr[:, j * IN_SH:(j + 1) * IN_SH] = _dot(xb, w_ref[j])
        cosf_t = cos_ref[...]
        sinf_t = sin_ref[...]
        for h in range(HEADS):
            lo = h * HEAD_DIM
            q_ref[:, lo:lo + HEAD_DIM] = _rope(proj_scr[:, lo:lo + HEAD_DIM], cosf_t, sinf_t).astype(BF16)
            kk = _rope(proj_scr[:, RET_W + lo:RET_W + lo + HEAD_DIM], cosf_t, sinf_t) * K_SCALE
            k_ref[:, lo:lo + HEAD_DIM] = kk.astype(BF16)
        v_ref[...] = proj_scr[:, 2 * RET_W:3 * RET_W].astype(BF16)
        g_ref[...] = proj_scr[:, 3 * RET_W:4 * RET_W]

        @pl.when(i == 0)
        def _():
            pext_scr[0:POOL_HALO, :] = jnp.zeros((POOL_HALO, POOL_W), F32)

        pext_scr[POOL_HALO:POOL_HALO + ts, :] = proj_scr[:, 4 * RET_W:IN_W]
        pos = (i * ts + lax.broadcasted_iota(jnp.int32, (ts, 1), 0) + 1).astype(F32)
        for gi, w in enumerate(POOL_WINDOWS):
            lo = gi * HEAD_DIM
            ext = pext_scr[:, lo:lo + HEAD_DIM]
            acc = ext
            shift = 1
            while shift < w:
                acc = acc + pltpu.roll(acc, shift, 0)
                shift *= 2
            tok = ext[POOL_HALO:POOL_HALO + ts]
            pooled = acc[POOL_HALO:POOL_HALO + ts] / jnp.minimum(pos, float(w)) - tok
            pooled_b = pooled.astype(BF16)
            pooled_ref[:, lo:lo + HEAD_DIM] = pooled_b
            lin = _dot(pooled_b, wp_ref[gi])
            cat_ref[:, lo:lo + HEAD_DIM] = (lin * ps_ref[:, lo:lo + HEAD_DIM]).astype(BF16)
        pext_scr[0:POOL_HALO, :] = pext_scr[ts:ts + POOL_HALO, :]

    tile = lambda w: pl.BlockSpec((ts, w), lambda i: (i, 0))
    return _call(
        body, name="proj_pool", grid=(nt,),
        in_specs=[tile(D_MODEL), _whole(), tile(HEAD_DIM), tile(HEAD_DIM), _whole(), _whole()],
        out_specs=[tile(D_MODEL), tile(RET_W), tile(RET_W), tile(RET_W), tile(RET_W), tile(POOL_W),
                   pl.BlockSpec((ts, POOL_W), lambda i: (i, 1))],
        out_shape=[jax.ShapeDtypeStruct((s, D_MODEL), BF16), jax.ShapeDtypeStruct((s, RET_W), BF16),
                   jax.ShapeDtypeStruct((s, RET_W), BF16), jax.ShapeDtypeStruct((s, RET_W), BF16),
                   jax.ShapeDtypeStruct((s, RET_W), F32), jax.ShapeDtypeStruct((s, POOL_W), BF16),
                   jax.ShapeDtypeStruct((s, 2 * RET_W), BF16)],
        scratch_shapes=[pltpu.VMEM((ts, IN_W), F32), pltpu.VMEM((ts + POOL_HALO, POOL_W), F32)],
        sem=("arbitrary",), operands=(x, win4, cosf, sinf, wpool, pscale), riders=riders,
    )


def _retention_fwd(q, k, v, g, cat, mask, qd, kd, riders=()):
    s = q.shape[0]
    ns = s // SUPER
    cdec = [gm ** float(SUPER) for gm in _gammas()]

    def body(q_ref, k_ref, v_ref, g_ref, cat_in, mask_ref, qd_ref, kd_ref,
             ret_ref, cat_ref, st_ref, state_scr):
        del cat_in
        n = pl.program_id(0)

        @pl.when(n == 0)
        def _():
            state_scr[...] = jnp.zeros_like(state_scr)

        for h in range(HEADS):
            sl = slice(h * HEAD_DIM, (h + 1) * HEAD_DIM)
            qh, kh, vh = q_ref[:, sl], k_ref[:, sl], v_ref[:, sl]
            sc = _dot_nt(qh, kh) * mask_ref[h]
            st = state_scr[h]
            stb = st.astype(BF16)
            st_ref[0, h] = stb
            qdb = (qh.astype(F32) * qd_ref[:, sl]).astype(BF16)
            kdb = (kh.astype(F32) * kd_ref[:, sl]).astype(BF16)
            ret = _dot(sc.astype(BF16), vh) + _dot(qdb, stb)
            state_scr[h] = st * cdec[h] + _dot_tn(kdb, vh)
            ret_ref[:, sl] = ret
            r = lax.rsqrt(jnp.mean(ret * ret, axis=-1, keepdims=True) + RMS_EPS)
            gh = g_ref[:, sl]
            cat_ref[:, sl] = ((ret * r) * (gh * _sigmoid(gh))).astype(BF16)

    tile = pl.BlockSpec((SUPER, RET_W), lambda n: (n, 0))
    return _call(
        body, name="retention_fwd", grid=(ns,),
        in_specs=[tile, tile, tile, tile, HBM_SPEC, _whole(), _whole(), _whole()],
        out_specs=[tile, tile, pl.BlockSpec((1, HEADS, HEAD_DIM, HEAD_DIM), lambda n: (n, 0, 0, 0))],
        out_shape=[jax.ShapeDtypeStruct((s, RET_W), F32), jax.ShapeDtypeStruct((s, 2 * RET_W), BF16),
                   jax.ShapeDtypeStruct((ns, HEADS, HEAD_DIM, HEAD_DIM), BF16)],
        scratch_shapes=[pltpu.VMEM((HEADS, HEAD_DIM, HEAD_DIM), F32)],
        aliases={4: 1}, sem=("arbitrary",), operands=(q, k, v, g, cat, mask, qd, kd), riders=riders,
    )


def _outproj_ln1(x, cat, wout, g1, b1, ts, riders=()):
    s = x.shape[0]

    def body(x_ref, cat_ref, w_ref, g_ref, b_ref, xhat_ref, rstd_ref, h1b_ref):
        z = ALPHA * x_ref[...] + _dot(cat_ref[...], w_ref[...])
        xhat, rstd = _layernorm_fwd(z)
        xhat_ref[...] = xhat
        rstd_ref[...] = rstd
        h1b_ref[...] = (xhat * g_ref[...] + b_ref[...]).astype(BF16)

    tile = lambda w: pl.BlockSpec((ts, w), lambda i: (i, 0))
    return _call(
        body, name="outproj_ln1", grid=(s // ts,),
        in_specs=[tile(D_MODEL), tile(D_MODEL), _whole(), _whole(), _whole()],
        out_specs=[tile(D_MODEL), tile(1), tile(D_MODEL)],
        out_shape=[jax.ShapeDtypeStruct((s, D_MODEL), F32), jax.ShapeDtypeStruct((s, 1), F32),
                   jax.ShapeDtypeStruct((s, D_MODEL), BF16)],
        sem=("arbitrary",), operands=(x, cat, wout, g1, b1), riders=riders,
    )


def _conv(cw_ref, cb_ref, g2s, g1s, gate):
    return cb_ref[...] + ((g2s * cw_ref[0:1, :] + g1s * cw_ref[1:2, :]) + gate * cw_ref[2:3, :])


def _ffn_fwd_loss(xhat1, h1b, target, wup4, wdown, cw, cb, g1, b1, g2, b2, ts):
    s = xhat1.shape[0]

    def body(xhat_ref, h1b_ref, tgt_ref, wup_ref, wdn_ref, cw_ref, cb_ref, g1_ref, b1_ref, g2_ref, b2_ref,
             ub_ref, dz2_ref, dz2b_ref, loss_ref, dg2_ref, db2_ref, u_scr, gext_scr):
        i = pl.program_id(0)

        @pl.when(i == 0)
        def _():
            gext_scr[0:CONV_HALO, :] = jnp.zeros((CONV_HALO, D_FF), F32)
            loss_ref[...] = jnp.zeros_like(loss_ref)
            dg2_ref[...] = jnp.zeros_like(dg2_ref)
            db2_ref[...] = jnp.zeros_like(db2_ref)

        hb = h1b_ref[...]
        for j in range(N_SHARD):
            u_scr[:, j * UP_SH:(j + 1) * UP_SH] = _dot(hb, wup_ref[j])
        ub_ref[...] = u_scr[...].astype(BF16)
        gate = u_scr[:, D_FF:2 * D_FF]
        gext_scr[CONV_HALO:CONV_HALO + ts, :] = gate
        ext = gext_scr[...]
        hc = _conv(cw_ref, cb_ref, pltpu.roll(ext, 2, 0)[CONV_HALO:], pltpu.roll(ext, 1, 0)[CONV_HALO:], gate)
        gext_scr[0:CONV_HALO, :] = gext_scr[ts:ts + CONV_HALO, :]
        act = (hc * _sigmoid(hc)) * u_scr[:, 0:D_FF]
        h1 = xhat_ref[...] * g1_ref[...] + b1_ref[...]
        z2 = ALPHA * h1 + _dot(act.astype(BF16), wdn_ref[...])
        xhat2, rstd2 = _layernorm_fwd(z2)
        diff = (xhat2 * g2_ref[...] + b2_ref[...]) - tgt_ref[...]
        row = jnp.mean(diff * diff, axis=-1, keepdims=True)
        loss_ref[...] += 0.5 * jnp.sum(row, axis=0, keepdims=True)
        dy = diff * (1.0 / D_MODEL)
        dg2_ref[...] += jnp.sum(dy * xhat2, axis=0, keepdims=True)
        db2_ref[...] += jnp.sum(dy, axis=0, keepdims=True)
        dz2 = _layernorm_bwd(dy, xhat2, rstd2, g2_ref[...])
        dz2_ref[...] = dz2
        dz2b_ref[...] = dz2.astype(BF16)

    tile = lambda w: pl.BlockSpec((ts, w), lambda i: (i, 0))
    acc = lambda w: pl.BlockSpec((1, w), lambda i: (0, 0))
    return pl.pallas_call(
        body, name="ffn_fwd_loss", grid=(s // ts,),
        in_specs=[tile(D_MODEL), tile(D_MODEL), tile(D_MODEL)] + [_whole()] * 8,
        out_specs=[tile(2 * D_FF), tile(D_MODEL), tile(D_MODEL), acc(1), acc(D_MODEL), acc(D_MODEL)],
        out_shape=[jax.ShapeDtypeStruct((s, 2 * D_FF), BF16), jax.ShapeDtypeStruct((s, D_MODEL), F32),
                   jax.ShapeDtypeStruct((s, D_MODEL), BF16),
                   jax.ShapeDtypeStruct((1, 1), F32), jax.ShapeDtypeStruct((1, D_MODEL), F32),
                   jax.ShapeDtypeStruct((1, D_MODEL), F32)],
        scratch_shapes=[pltpu.VMEM((ts, 2 * D_FF), F32), pltpu.VMEM((ts + CONV_HALO, D_FF), F32)],
        compiler_params=_params(("arbitrary",)),
    )(xhat1, h1b, target, wup4, wdown, cw, cb, g1, b1, g2, b2)


def _ffn_bwd(dz2, ub, xhat1, rstd1, wup4, wdown, cw, cb, g1, ts):
    s = dz2.shape[0]
    nt = s // ts
    hb = 16

    def body(dz2_ref, ub_ref, prev_ref, xhat_ref, rstd_ref, wup_ref, wdn_ref, cw_ref, cb_ref, g1_ref,
             a_ref, dub_ref, dz1_ref, dz1b_ref, dg1_ref, db1_ref, dcw_ref, dcb_ref, gext_scr, dext_scr):
        i = pl.program_id(0)
        r = nt - 1 - i

        @pl.when(i == 0)
        def _():
            dext_scr[ts:ts + CONV_HALO, :] = jnp.zeros((CONV_HALO, D_FF), F32)
            dg1_ref[...] = jnp.zeros_like(dg1_ref)
            db1_ref[...] = jnp.zeros_like(db1_ref)
            dcw_ref[...] = jnp.zeros_like(dcw_ref)
            dcb_ref[...] = jnp.zeros_like(dcb_ref)

        dz2 = dz2_ref[...]
        da = _dot_nt(dz2.astype(BF16), wdn_ref[...])
        val = ub_ref[:, 0:D_FF].astype(F32)
        gate = ub_ref[:, D_FF:2 * D_FF].astype(F32)
        prev = prev_ref[...].astype(F32)[hb - CONV_HALO:hb]
        gext_scr[0:CONV_HALO, :] = jnp.where(r == 0, 0.0, prev)
        gext_scr[CONV_HALO:CONV_HALO + ts, :] = gate
        ext = gext_scr[...]
        g2s = pltpu.roll(ext, 2, 0)[CONV_HALO:]
        g1s = pltpu.roll(ext, 1, 0)[CONV_HALO:]
        hc = _conv(cw_ref, cb_ref, g2s, g1s, gate)
        sg = _sigmoid(hc)
        si = hc * sg
        a_ref[...] = (si * val).astype(BF16)
        dhc = da * val * (sg * (1.0 + hc * (1.0 - sg)))
        dcb_ref[...] += jnp.sum(dhc, axis=0, keepdims=True)
        dcw_ref[0:1, :] += jnp.sum(dhc * g2s, axis=0, keepdims=True)
        dcw_ref[1:2, :] += jnp.sum(dhc * g1s, axis=0, keepdims=True)
        dcw_ref[2:3, :] += jnp.sum(dhc * gate, axis=0, keepdims=True)
        dext_scr[0:ts, :] = dhc
        dext = dext_scr[...]
        dgate = (dhc * cw_ref[2:3, :] + pltpu.roll(dext, ts + CONV_HALO - 1, 0)[0:ts] * cw_ref[1:2, :]
                 + pltpu.roll(dext, ts + CONV_HALO - 2, 0)[0:ts] * cw_ref[0:1, :])
        dext_scr[ts:ts + CONV_HALO, :] = dext_scr[0:CONV_HALO, :]
        dub_ref[:, 0:D_FF] = (da * si).astype(BF16)
        dub_ref[:, D_FF:2 * D_FF] = dgate.astype(BF16)
        dh1 = ALPHA * dz2
        for j in range(N_SHARD):
            dh1 = dh1 + _dot_nt(dub_ref[:, j * UP_SH:(j + 1) * UP_SH], wup_ref[j])
        xhat = xhat_ref[...]
        dg1_ref[...] += jnp.sum(dh1 * xhat, axis=0, keepdims=True)
        db1_ref[...] += jnp.sum(dh1, axis=0, keepdims=True)
        dz1 = _layernorm_bwd(dh1, xhat, rstd_ref[...], g1_ref[...])
        dz1_ref[...] = dz1
        dz1b_ref[...] = dz1.astype(BF16)

    tile = lambda w: pl.BlockSpec((ts, w), lambda i: (nt - 1 - i, 0))
    acc = lambda rws, w: pl.BlockSpec((rws, w), lambda i: (0, 0))
    prev_spec = pl.BlockSpec((hb, D_FF), lambda i: (jnp.maximum((nt - 1 - i) * (ts // hb) - 1, 0), 1))
    return pl.pallas_call(
        body, name="ffn_bwd", grid=(nt,),
        in_specs=[tile(D_MODEL), tile(2 * D_FF), prev_spec, tile(D_MODEL), tile(1)] + [_whole()] * 5,
        out_specs=[tile(D_FF), tile(2 * D_FF), tile(D_MODEL), tile(D_MODEL), acc(1, D_MODEL), acc(1, D_MODEL),
                   acc(3, D_FF), acc(1, D_FF)],
        out_shape=[jax.ShapeDtypeStruct((s, D_FF), BF16), jax.ShapeDtypeStruct((s, 2 * D_FF), BF16),
                   jax.ShapeDtypeStruct((s, D_MODEL), F32), jax.ShapeDtypeStruct((s, D_MODEL), BF16),
                   jax.ShapeDtypeStruct((1, D_MODEL), F32),
                   jax.ShapeDtypeStruct((1, D_MODEL), F32), jax.ShapeDtypeStruct((3, D_FF), F32),
                   jax.ShapeDtypeStruct((1, D_FF), F32)],
        scratch_shapes=[pltpu.VMEM((ts + CONV_HALO, D_FF), F32), pltpu.VMEM((ts + CONV_HALO, D_FF), F32)],
        compiler_params=_params(("arbitrary",)),
    )(dz2, ub, ub, xhat1, rstd1, wup4, wdown, cw, cb, g1)


def _mix_bwd(dz1, pooled, ret, g, wout, wpool, pscale, ts, riders=()):
    s = dz1.shape[0]
    nt = s // ts

    def body(dz1_ref, pooled_ref, ret_ref, g_ref, wout_ref, wp_ref, ps_ref,
             dret_ref, dgp_ref, dwp_ref, dps_ref, eext_scr):
        i = pl.program_id(0)
        r = nt - 1 - i

        @pl.when(i == 0)
        def _():
            eext_scr[ts:ts + POOL_HALO, :] = jnp.zeros((POOL_HALO, POOL_W), F32)
            dwp_ref[...] = jnp.zeros_like(dwp_ref)
            dps_ref[...] = jnp.zeros_like(dps_ref)

        dzb = dz1_ref[...].astype(BF16)
        dcat_r = _dot_nt(dzb, wout_ref[0:RET_W, :])
        dcat_p = _dot_nt(dzb, wout_ref[RET_W:2 * RET_W, :])
        pos = (r * ts + lax.broadcasted_iota(jnp.int32, (ts, 1), 0) + 1).astype(F32)
        dpooled = []
        for gi, w in enumerate(POOL_WINDOWS):
            sl = slice(gi * HEAD_DIM, (gi + 1) * HEAD_DIM)
            pb = pooled_ref[:, sl]
            dy = dcat_p[:, sl]
            dps_ref[:, sl] += jnp.sum(dy * _dot(pb, wp_ref[gi]), axis=0, keepdims=True)
            dlin = (dy * ps_ref[:, sl]).astype(BF16)
            dwp_ref[gi] += _dot_tn(pb, dlin)
            dpg = _dot_nt(dlin, wp_ref[gi])
            dpooled.append(dpg)
            eext_scr[0:ts, sl] = dpg / jnp.minimum(pos, float(w))
        for gi, w in enumerate(POOL_WINDOWS):
            sl = slice(gi * HEAD_DIM, (gi + 1) * HEAD_DIM)
            acc = eext_scr[:, sl]
            shift = 1
            while shift < w:
                acc = acc + pltpu.roll(acc, ts + POOL_HALO - shift, 0)
                shift *= 2
            dgp_ref[:, RET_W + gi * HEAD_DIM:RET_W + (gi + 1) * HEAD_DIM] = (acc[0:ts] - dpooled[gi]).astype(BF16)
        eext_scr[ts:ts + POOL_HALO, :] = eext_scr[0:POOL_HALO, :]
        for h in range(HEADS):
            sl = slice(h * HEAD_DIM, (h + 1) * HEAD_DIM)
            rt = ret_ref[:, sl]
            rr = lax.rsqrt(jnp.mean(rt * rt, axis=-1, keepdims=True) + RMS_EPS)
            rn = rt * rr
            gh = g_ref[:, sl]
            sg = _sigmoid(gh)
            dy = dcat_r[:, sl]
            dgp_ref[:, sl] = (dy * rn * (sg * (1.0 + gh * (1.0 - sg)))).astype(BF16)
            drn = dy * (gh * sg)
            dret_ref[:, sl] = (rr * (drn - rn * jnp.mean(drn * rn, axis=-1, keepdims=True))).astype(BF16)

    tile = lambda w: pl.BlockSpec((ts, w), lambda i: (nt - 1 - i, 0))
    return _call(
        body, name="mix_bwd", grid=(nt,),
        in_specs=[tile(D_MODEL), tile(POOL_W), tile(RET_W), tile(RET_W), _whole(), _whole(), _whole()],
        out_specs=[tile(RET_W), tile(2 * RET_W),
                   pl.BlockSpec((len(POOL_WINDOWS), HEAD_DIM, HEAD_DIM), lambda i: (0, 0, 0)),
                   pl.BlockSpec((1, POOL_W), lambda i: (0, 0))],
        out_shape=[jax.ShapeDtypeStruct((s, RET_W), BF16), jax.ShapeDtypeStruct((s, 2 * RET_W), BF16),
                   jax.ShapeDtypeStruct((len(POOL_WINDOWS), HEAD_DIM, HEAD_DIM), F32),
                   jax.ShapeDtypeStruct((1, POOL_W), F32)],
        scratch_shapes=[pltpu.VMEM((ts + POOL_HALO, POOL_W), F32)],
        sem=("arbitrary",), operands=(dz1, pooled, ret, g, wout, wpool, pscale), riders=riders,
    )


def _retention_bwd(q, k, v, dret, dgp, states, mask, qd, kd, cosf, sinf, riders=()):
    s = q.shape[0]
    ns = s // SUPER
    cdec = [gm ** float(SUPER) for gm in _gammas()]

    def body(q_ref, k_ref, v_ref, do_ref, dgp_ref, st_ref, mask_ref, qd_ref, kd_ref, cos_ref, sin_ref,
             dproj_ref, dstate_scr):
        i = pl.program_id(0)

        @pl.when(i == 0)
        def _():
            dstate_scr[...] = jnp.zeros_like(dstate_scr)

        cosf_t = cos_ref[...]
        sinf_t = sin_ref[...]
        for h in range(HEADS):
            sl = slice(h * HEAD_DIM, (h + 1) * HEAD_DIM)
            qh, kh, vh, doh = q_ref[:, sl], k_ref[:, sl], v_ref[:, sl], do_ref[:, sl]
            m = mask_ref[h]
            scb = (_dot_nt(qh, kh) * m).astype(BF16)
            dscb = (_dot_nt(doh, vh) * m).astype(BF16)
            stb = st_ref[0, h]
            dst = dstate_scr[h]
            dstb = dst.astype(BF16)
            qdb = (qh.astype(F32) * qd_ref[:, sl]).astype(BF16)
            kdb = (kh.astype(F32) * kd_ref[:, sl]).astype(BF16)
            dq = _dot(dscb, kh) + _dot_nt(doh, stb) * qd_ref[:, sl]
            dk = _dot_tn(dscb, qh) + _dot_nt(vh, dstb) * kd_ref[:, sl]
            dv = _dot_tn(scb, doh) + _dot(kdb, dstb)
            dstate_scr[h] = dst * cdec[h] + _dot_tn(qdb, doh)
            lo = h * HEAD_DIM
            dproj_ref[:, lo:lo + HEAD_DIM] = _rope_t(dq, cosf_t, sinf_t).astype(BF16)
            dproj_ref[:, RET_W + lo:RET_W + lo + HEAD_DIM] = _rope_t(dk * K_SCALE, cosf_t, sinf_t).astype(BF16)
            dproj_ref[:, 2 * RET_W + lo:2 * RET_W + lo + HEAD_DIM] = dv.astype(BF16)
        dproj_ref[:, 3 * RET_W:IN_W] = dgp_ref[...]

    tile = lambda w: pl.BlockSpec((SUPER, w), lambda i: (ns - 1 - i, 0))
    return _call(
        body, name="retention_bwd", grid=(ns,),
        in_specs=[tile(RET_W), tile(RET_W), tile(RET_W), tile(RET_W), tile(2 * RET_W),
                  pl.BlockSpec((1, HEADS, HEAD_DIM, HEAD_DIM), lambda i: (ns - 1 - i, 0, 0, 0)),
                  _whole(), _whole(), _whole(), tile(HEAD_DIM), tile(HEAD_DIM)],
        out_specs=[tile(IN_W)],
        out_shape=[jax.ShapeDtypeStruct((s, IN_W), BF16)],
        scratch_shapes=[pltpu.VMEM((HEADS, HEAD_DIM, HEAD_DIM), F32)],
        sem=("arbitrary",), operands=(q, k, v, dret, dgp, states, mask, qd, kd, cosf, sinf), riders=riders,
    )


def _dx(dz1, dproj, win4, ts, riders=()):
    s = dz1.shape[0]

    def body(dz1_ref, dp_ref, w_ref, dx_ref):
        acc = ALPHA * dz1_ref[...]
        for j in range(N_SHARD):
            acc = acc + _dot_nt(dp_ref[:, j * IN_SH:(j + 1) * IN_SH], w_ref[j])
        dx_ref[...] = acc

    tile = lambda w: pl.BlockSpec((ts, w), lambda i: (i, 0))
    return _call(
        body, name="dx", grid=(s // ts,),
        in_specs=[tile(D_MODEL), tile(IN_W), _whole()],
        out_specs=[tile(D_MODEL)],
        out_shape=[jax.ShapeDtypeStruct((s, D_MODEL), F32)],
        sem=("arbitrary",), operands=(dz1, dproj, win4), riders=riders,
    )


def _wgrad(a, b, tm, tn, name, stacked, m_outer, riders=()):
    s, m = a.shape
    n = b.shape[1]

    def body(a_ref, b_ref, o32_ref, o16_ref):
        res = _dot_tn(a_ref[...], b_ref[...])
        o32_ref[...] = res.reshape(o32_ref.shape)
        o16_ref[...] = res.astype(BF16).reshape(o16_ref.shape)

    if m_outer:
        grid, blocks = (m // tm, n // tn), (lambda g0, g1: (g0, g1))
    else:
        grid, blocks = (n // tn, m // tm), (lambda g0, g1: (g1, g0))
    if stacked:
        shape = (n // tn, m, tn)
        ospec = pl.BlockSpec((1, tm, tn), lambda g0, g1: (blocks(g0, g1)[1], blocks(g0, g1)[0], 0))
    else:
        shape = (m, n)
        ospec = pl.BlockSpec((tm, tn), lambda g0, g1: blocks(g0, g1))
    return _call(
        body, name=name, grid=grid,
        in_specs=[pl.BlockSpec((s, tm), lambda g0, g1: (0, blocks(g0, g1)[0])),
                  pl.BlockSpec((s, tn), lambda g0, g1: (0, blocks(g0, g1)[1]))],
        out_specs=[ospec, ospec],
        out_shape=[jax.ShapeDtypeStruct(shape, F32), jax.ShapeDtypeStruct(shape, BF16)],
        sem=("arbitrary", "arbitrary"), operands=(a, b), riders=riders,
    )


class _NoComm:
    def __init__(self, win4, wout, wup4, wdown):
        self.weights = dict(w_in=win4, w_out=wout, w_up=wup4, w_down=wdown)
        self.grads = {}

    def weight(self, name):
        return self.weights[name]

    def riders(self, call):
        return ()

    def landed(self, call, results):
        pass

    def gradient(self, name, g32, g16):
        self.grads[name] = (g32, g16)


def _local_step(x, target, cw, cb, wpool, pscale, g1, b1, g2, b2, comm):
    s = x.shape[0]
    ts_a = min(512, s)
    ts_f = min(256, s)
    mask, qd, kd = _decay_tables()
    cosf, sinf = _rope_tables(s)
    wpool_b = wpool.astype(BF16)

    def run(call, fn, *args):
        outs, res = fn(*args, riders=comm.riders(call))
        comm.landed(call, res)
        return outs

    xb, q, k, v, g, pooled, cat = run("proj_pool", _proj_pool, x, comm.weight("w_in"), cosf, sinf, wpool_b,
                                      pscale, ts_a)
    ret, cat, states = run("retention_fwd", _retention_fwd, q, k, v, g, cat, mask, qd, kd)
    wout = comm.weight("w_out")
    xhat1, rstd1, h1b = run("outproj_ln1", _outproj_ln1, x, cat, wout, g1, b1, ts_a)
    wup4, wdown = comm.weight("w_up"), comm.weight("w_down")
    ub, dz2, dz2b, loss, dg2, db2 = _ffn_fwd_loss(xhat1, h1b, target, wup4, wdown, cw, cb, g1, b1, g2, b2, ts_f)

    act, dub, dz1, dz1b, dg1, db1, dcw, dcb = _ffn_bwd(dz2, ub, xhat1, rstd1, wup4, wdown, cw, cb, g1, ts_f)
    half = D_MODEL // 2
    comm.gradient("w_down", *run("wgrad_down", _wgrad, act, dz2b, D_FF // 2, half, "wgrad_down", False, True))
    comm.gradient("w_up", *run("wgrad_up", _wgrad, h1b, dub, half, UP_SH, "wgrad_up", True, False))
    dret, dgp, dwp, dps = run("mix_bwd", _mix_bwd, dz1b, pooled, ret, g, wout, wpool_b, pscale, ts_a)
    dproj, = run("retention_bwd", _retention_bwd, q, k, v, dret, dgp, states, mask, qd, kd, cosf, sinf)
    comm.gradient("w_in", *run("wgrad_in", _wgrad, xb, dproj, D_MODEL, IN_SH, "wgrad_in", True, True))
    comm.gradient("w_out", *run("wgrad_out", _wgrad, cat, dz1b, D_MODEL, half, "wgrad_out", False, True))
    grad_x, = run("dx", _dx, dz1, dproj, comm.weight("w_in"), ts_a)
    small = dict(w_pool=dwp, pool_scale=dps, ln1_g=dg1, ln1_b=db1, conv_w=dcw, conv_b=dcb,
                 ln2_g=dg2, ln2_b=db2)
    return loss, grad_x, small


CAST_ROWS = 64
SHARD_SHAPES = ((D_MODEL, IN_SH), (OUT_SH, D_MODEL), (D_MODEL, UP_SH), (DOWN_SH, D_MODEL))
N_BIG = len(SHARD_SHAPES)
CW_PAD = (8, 768)


def _mesh_pos():
    return lax.axis_index("x"), lax.axis_index("y"), lax.axis_index("c")


def _other_chips(x, y):
    return [(1 - x, y), (x, 1 - y), (1 - x, 1 - y)]


def _half_rows(w, which):
    hr = SHARD_SHAPES[w][0] // 2
    return pl.ds(pl.multiple_of(which * hr, 16), hr)


def _gather_weights(shards, cw8, full):
    def body(*refs):
        in_refs = refs[:N_BIG]
        cw_ref = refs[N_BIG]
        out_refs = refs[N_BIG + 1:2 * N_BIG + 1]
        cwo_ref = refs[2 * N_BIG + 1]
        stage = refs[2 * N_BIG + 2:3 * N_BIG + 2]
        send_sems, recv_sems, fsend_sems, frecv_sems, cw_send, cw_recv, local_sems = refs[3 * N_BIG + 2:]
        x, y, c = _mesh_pos()
        j0 = 2 * x + y
        chips = _other_chips(x, y)

        for w in range(N_BIG):
            def cast(i, carry, w=w):
                rows = pl.ds(pl.multiple_of(i * CAST_ROWS, CAST_ROWS), CAST_ROWS)
                stage[w][rows, :] = in_refs[w][rows, :].astype(BF16)
                return carry
            lax.fori_loop(0, SHARD_SHAPES[w][0] // CAST_ROWS, cast, 0)

        local = [pltpu.make_async_copy(stage[w], out_refs[w].at[j0], local_sems.at[w]) for w in range(N_BIG)]
        local.append(pltpu.make_async_copy(cw_ref, cwo_ref.at[j0], local_sems.at[N_BIG]))
        for cp in local:
            cp.start()

        def ici(w, k, block):
            chip = chips[k]
            return pltpu.make_async_remote_copy(
                src_ref=stage[w].at[_half_rows(w, c), :], dst_ref=out_refs[w].at[block, _half_rows(w, c), :],
                send_sem=send_sems.at[w, k], recv_sem=recv_sems.at[w, k],
                device_id=(chip[0], chip[1], c), device_id_type=MESH)

        def d2d(w, k, block, half):
            return pltpu.make_async_remote_copy(
                src_ref=out_refs[w].at[block, _half_rows(w, half), :],
                dst_ref=out_refs[w].at[block, _half_rows(w, half), :],
                send_sem=fsend_sems.at[w, k], recv_sem=frecv_sems.at[w, k],
                device_id=(x, y, 1 - c), device_id_type=MESH)

        def conv(k, block):
            chip = chips[k]
            return pltpu.make_async_remote_copy(
                src_ref=cw_ref, dst_ref=cwo_ref.at[block], send_sem=cw_send.at[k], recv_sem=cw_recv.at[k],
                device_id=(chip[0], chip[1], c), device_id_type=MESH)

        sent = [ici(w, k, j0) for w in full for k in range(3)] + [conv(k, j0) for k in range(3)]
        for cp in sent:
            cp.start()
        for k, chip in enumerate(chips):
            jk = 2 * chip[0] + chip[1]
            for w in full:
                ici(w, k, jk).wait_recv()
                fw = d2d(w, k, jk, c)
                fw.start()
                sent.append(fw)
        for k, chip in enumerate(chips):
            jk = 2 * chip[0] + chip[1]
            for w in full:
                d2d(w, k, jk, 1 - c).wait_recv()
            conv(k, jk).wait_recv()
        for cp in sent:
            cp.wait_send()
        for cp in local:
            cp.wait()

    out_shape = [jax.ShapeDtypeStruct((N_SHARD,) + shp, BF16) for shp in SHARD_SHAPES]
    out_shape.append(jax.ShapeDtypeStruct((N_SHARD,) + CW_PAD, F32))
    return pl.pallas_call(
        body, name="gather_weights",
        in_specs=[_whole()] * (N_BIG + 1),
        out_specs=[HBM_SPEC] * (N_BIG + 1),
        out_shape=out_shape,
        scratch_shapes=[pltpu.VMEM(shp, BF16) for shp in SHARD_SHAPES] + [
            pltpu.SemaphoreType.DMA((N_BIG, 3)), pltpu.SemaphoreType.DMA((N_BIG, 3)),
            pltpu.SemaphoreType.DMA((N_BIG, 3)), pltpu.SemaphoreType.DMA((N_BIG, 3)),
            pltpu.SemaphoreType.DMA((3,)), pltpu.SemaphoreType.DMA((3,)),
            pltpu.SemaphoreType.DMA((N_BIG + 1,))],
        compiler_params=pltpu.CompilerParams(vmem_limit_bytes=VMEM_LIMIT),
    )(*shards, cw8)


def _gather_rider(arrays, ops):
    ws = sorted(arrays)

    def make(inplace, srcs, lands, send_sems, recv_sems):
        del srcs, lands
        x, y, c = _mesh_pos()
        j0 = 2 * x + y
        chips = _other_chips(x, y)
        starts, waits = [], []
        for n, (kind, w, (r0, nr)) in enumerate(ops):
            ref = inplace[ws.index(w)]
            hr = SHARD_SHAPES[w][0] // 2
            rows = lambda core: pl.ds(pl.multiple_of(core * hr + r0, 16), nr)
            for k, chip in enumerate(chips):
                jk = 2 * chip[0] + chip[1]
                if kind == "ici":
                    src, to, landing = ref.at[j0, rows(c), :], (chip[0], chip[1], c), ref.at[jk, rows(c), :]
                else:
                    src, to, landing = ref.at[jk, rows(c), :], (x, y, 1 - c), ref.at[jk, rows(1 - c), :]
                sems = dict(send_sem=send_sems.at[3 * n + k], recv_sem=recv_sems.at[3 * n + k],
                            device_id=to, device_id_type=MESH)
                send = pltpu.make_async_remote_copy(src_ref=src, dst_ref=src, **sems)
                arrival = pltpu.make_async_remote_copy(src_ref=src, dst_ref=landing, **sems)
                starts.append(send)
                waits += [arrival.wait_recv, send.wait_send]
        return starts, waits

    return _Rider([arrays[w] for w in ws], [], [], 3 * len(ops), make)


def _whole_half(w):
    return (0, SHARD_SHAPES[w][0] // 2)


def _pair_rider(ws, g16s):
    def make(inplace, srcs, lands, send_sems, recv_sems):
        del inplace
        x, y, c = _mesh_pos()
        copies = [pltpu.make_async_remote_copy(
            src_ref=srcs[i].at[:, _half_rows(w, 1 - c), :], dst_ref=lands[i],
            send_sem=send_sems.at[i], recv_sem=recv_sems.at[i], device_id=(x, y, 1 - c), device_id_type=MESH)
            for i, w in enumerate(ws)]
        return copies, [cp.wait for cp in copies]

    lands = [jax.ShapeDtypeStruct((N_SHARD, SHARD_SHAPES[w][0] // 2, SHARD_SHAPES[w][1]), BF16) for w in ws]
    return _Rider([], g16s, lands, len(ws), make)


def _chip_rider(ws, p16s):
    def make(inplace, srcs, lands, send_sems, recv_sems):
        del inplace
        x, y, c = _mesh_pos()
        copies = []
        for i in range(len(ws)):
            for k, chip in enumerate(_other_chips(x, y)):
                copies.append(pltpu.make_async_remote_copy(
                    src_ref=srcs[i].at[2 * chip[0] + chip[1]], dst_ref=lands[i].at[k],
                    send_sem=send_sems.at[3 * i + k], recv_sem=recv_sems.at[3 * i + k],
                    device_id=(chip[0], chip[1], c), device_id_type=MESH))
        return copies, [cp.wait for cp in copies]

    lands = [jax.ShapeDtypeStruct((3, SHARD_SHAPES[w][0] // 2, SHARD_SHAPES[w][1]), BF16) for w in ws]
    return _Rider([], p16s, lands, 3 * len(ws), make)


def _final_rider(halves):
    def make(inplace, srcs, lands, send_sems, recv_sems):
        del inplace
        x, y, c = _mesh_pos()
        copies = [pltpu.make_async_remote_copy(
            src_ref=srcs[i], dst_ref=lands[i], send_sem=send_sems.at[i], recv_sem=recv_sems.at[i],
            device_id=(x, y, 1 - c), device_id_type=MESH) for i in range(len(halves))]
        return copies, [cp.wait for cp in copies]

    return _Rider([], halves, [jax.ShapeDtypeStruct(h.shape, h.dtype) for h in halves], len(halves), make)


def _comm_only(name, riders):
    _, res = _call(lambda: None, name=name, grid=(), in_specs=[], out_specs=[], out_shape=[], operands=(),
                   riders=riders)
    return res


def _pair_sum(pos, ws, g32s, recvs):
    n = len(ws)

    def body(pos_ref, *refs):
        del pos_ref
        g_refs, r_refs = refs[:n], refs[n:2 * n]
        p32_refs, p16_refs = refs[2 * n:3 * n], refs[3 * n:]
        for i in range(n):
            tot = g_refs[i][...] + r_refs[i][...].astype(F32)
            p32_refs[i][...] = tot
            p16_refs[i][...] = tot.astype(BF16)

    halves = [(SHARD_SHAPES[w][0] // 2, SHARD_SHAPES[w][1]) for w in ws]
    own = [pl.BlockSpec((None, None) + h, lambda j, pos_ref: (j, pos_ref[0], 0, 0)) for h in halves]
    blk = [pl.BlockSpec((None,) + h, lambda j, pos_ref: (j, 0, 0)) for h in halves]
    g4 = [g.reshape((N_SHARD, 2) + h) for g, h in zip(g32s, halves)]
    outs = pl.pallas_call(
        body, name="pair_sum_" + "_".join(str(w) for w in ws),
        grid_spec=pltpu.PrefetchScalarGridSpec(
            num_scalar_prefetch=1, grid=(N_SHARD,), in_specs=own + blk, out_specs=blk + blk),
        out_shape=[jax.ShapeDtypeStruct((N_SHARD,) + h, F32) for h in halves]
        + [jax.ShapeDtypeStruct((N_SHARD,) + h, BF16) for h in halves],
        compiler_params=_params(("arbitrary",)),
    )(pos, *g4, *recvs)
    return outs[:n], outs[n:]


def _chip_sum(pos, p32s, recvs):
    parts = 2

    def body(pos_ref, *refs):
        del pos_ref
        p_refs, r_refs, f_refs = refs[:N_BIG], refs[N_BIG:2 * N_BIG], refs[2 * N_BIG:]
        for w in range(N_BIG):
            f_refs[w][...] = ((p_refs[w][...] + r_refs[w][0].astype(F32)) + r_refs[w][1].astype(F32)) \
                + r_refs[w][2].astype(F32)

    quarters = [(r // 2 // parts, cc) for r, cc in SHARD_SHAPES]
    own = [pl.BlockSpec((None,) + qt, lambda i, pos_ref: (pos_ref[1], i, 0)) for qt in quarters]
    rcv = [pl.BlockSpec((3,) + qt, lambda i, pos_ref: (0, i, 0)) for qt in quarters]
    out = [pl.BlockSpec(qt, lambda i, pos_ref: (i, 0)) for qt in quarters]
    return pl.pallas_call(
        body, name="chip_sum",
        grid_spec=pltpu.PrefetchScalarGridSpec(
            num_scalar_prefetch=1, grid=(parts,), in_specs=own + rcv, out_specs=out),
        out_shape=[jax.ShapeDtypeStruct((r // 2, cc), F32) for r, cc in SHARD_SHAPES],
        compiler_params=_params(("arbitrary",)),
    )(pos, *p32s, *recvs)


def _adamw(w, g, m, v):
    m_new = ADAM_B1 * m + (1.0 - ADAM_B1) * g
    v_new = ADAM_B2 * v + (1.0 - ADAM_B2) * (g * g)
    m_hat = m_new / (1.0 - ADAM_B1 ** ADAM_STEP)
    v_hat = v_new / (1.0 - ADAM_B2 ** ADAM_STEP)
    delta = -ADAM_LR * (m_hat / (jnp.sqrt(v_hat) + ADAM_EPS) + ADAM_WD * w)
    return delta, m_new, v_new


def _adam_big(pos, mine, theirs, ws, ms, vs):
    nb = 4

    def body(pos_ref, *refs):
        hf = pl.program_id(0)
        groups = [refs[i * N_BIG:(i + 1) * N_BIG] for i in range(9)]
        f_refs, t_refs, w_refs, m_refs, v_refs, go_refs, do_refs, mo_refs, vo_refs = groups
        for w in range(N_BIG):
            g = jnp.where(hf == pos_ref[0], f_refs[w][...], t_refs[w][...])
            delta, m_new, v_new = _adamw(w_refs[w][...], g, m_refs[w][...], v_refs[w][...])
            go_refs[w][...] = g
            do_refs[w][...] = delta
            mo_refs[w][...] = m_new
            vo_refs[w][...] = v_new

    blocks = [(r // 2 // nb, cc) for r, cc in SHARD_SHAPES]
    half = [pl.BlockSpec(b, lambda hf, i, pos_ref: (i, 0)) for b in blocks]
    full = [pl.BlockSpec((None,) + b, lambda hf, i, pos_ref: (0, hf * nb + i, 0)) for b in blocks]
    shapes = [jax.ShapeDtypeStruct((1,) + shp, F32) for shp in SHARD_SHAPES]
    outs = pl.pallas_call(
        body, name="adam_big",
        grid_spec=pltpu.PrefetchScalarGridSpec(
            num_scalar_prefetch=1, grid=(2, nb), in_specs=half + half + full * 3, out_specs=full * 4),
        out_shape=shapes * 4,
        compiler_params=_params(("arbitrary", "arbitrary")),
    )(pos, *mine, *theirs, *ws, *ms, *vs)
    return [outs[i * N_BIG:(i + 1) * N_BIG] for i in range(4)]


SMALL_ROWS = 8
ROW_CONV_B, ROW_POOL_SCALE, ROW_LN1_G, ROW_LN1_B, ROW_LN2_G, ROW_LN2_B, ROW_LOSS = range(7)
SMALL_VECS = ((ROW_CONV_B, D_FF), (ROW_POOL_SCALE, POOL_W), (ROW_LN1_G, D_MODEL), (ROW_LN1_B, D_MODEL),
              (ROW_LN2_G, D_MODEL), (ROW_LN2_B, D_MODEL))


def _small_update(loss, dwp, dcw4, vec_grads, wp, cwp, vec_ws, m_wp, m_cwp, vec_ms, v_wp, v_cwp, vec_vs):
    nv = len(SMALL_VECS)

    def body(*refs):
        loss_ref, dwp_ref, dcw_ref = refs[0:3]
        gvec = refs[3:3 + nv]
        o = 3 + nv
        wp_ref, cw_ref = refs[o:o + 2]
        wvec = refs[o + 2:o + 2 + nv]
        o += 2 + nv
        mwp_ref, mcw_ref = refs[o:o + 2]
        mvec = refs[o + 2:o + 2 + nv]
        o += 2 + nv
        vwp_ref, vcw_ref = refs[o:o + 2]
        vvec = refs[o + 2:o + 2 + nv]
        o += 2 + nv
        loss_out = refs[o]
        outs = refs[o + 1:o + 1 + 4 * (2 + nv)]
        o += 1 + 4 * (2 + nv)
        (vec_scr, sib_a, sib_b, sib_c, all_a, all_b, all_c,
         send1, recv1, send2, recv2) = refs[o:]
        x, y, c = _mesh_pos()
        j0 = 2 * x + y
        chips = _other_chips(x, y)

        vec_scr[...] = jnp.zeros_like(vec_scr)
        for (row, n), ref in zip(SMALL_VECS, gvec):
            vec_scr[row:row + 1, 0:n] = ref[...]
        vec_scr[ROW_LOSS:ROW_LOSS + 1, 0:HEAD_DIM] = jnp.broadcast_to(loss_ref[...], (1, HEAD_DIM))

        mine = (dwp_ref, vec_scr, dcw_ref)
        sib = (sib_a, sib_b, sib_c)
        every = (all_a, all_b, all_c)
        first = [pltpu.make_async_remote_copy(
            src_ref=mine[b], dst_ref=sib[b], send_sem=send1.at[b], recv_sem=recv1.at[b],
            device_id=(x, y, 1 - c), device_id_type=MESH) for b in range(3)]
        for cp in first:
            cp.start()
        for cp in first:
            cp.wait()
        for b in range(3):
            every[b][j0] = mine[b][...] + sib[b][...]

        def ici(b, k, block):
            chip = chips[k]
            return pltpu.make_async_remote_copy(
                src_ref=every[b].at[block], dst_ref=every[b].at[block],
                send_sem=send2.at[b, k], recv_sem=recv2.at[b, k],
                device_id=(chip[0], chip[1], c), device_id_type=MESH)

        second = [ici(b, k, j0) for b in range(3) for k in range(3)]
        for cp in second:
            cp.start()
        for k, chip in enumerate(chips):
            for b in range(3):
                ici(b, k, 2 * chip[0] + chip[1]).wait_recv()
        for cp in second:
            cp.wait_send()

        tot_a = ((all_a[0] + all_a[1]) + all_a[2]) + all_a[3]
        tot_b = ((all_b[0] + all_b[1]) + all_b[2]) + all_b[3]
        all_c[0] = ((all_c[0] + all_c[1]) + all_c[2]) + all_c[3]
        tot_c = all_c[0, j0]
        loss_out[...] = tot_b[ROW_LOSS:ROW_LOSS + 1, 0:1]

        grads = [tot_a, tot_c] + [tot_b[row:row + 1, 0:n] for row, n in SMALL_VECS]
        w_all = [wp_ref, cw_ref] + list(wvec)
        m_all = [mwp_ref, mcw_ref] + list(mvec)
        v_all = [vwp_ref, vcw_ref] + list(vvec)
        np_ = 2 + nv
        for p in range(np_):
            g = grads[p]
            delta, m_new, v_new = _adamw(w_all[p][...], g, m_all[p][...], v_all[p][...])
            outs[p][...] = g
            outs[np_ + p][...] = delta
            outs[2 * np_ + p][...] = m_new
            outs[3 * np_ + p][...] = v_new

    pshapes = [wp.shape, CW_PAD] + [wv.shape for wv in vec_ws]
    out_shape = [jax.ShapeDtypeStruct((1, 1), F32)] + [jax.ShapeDtypeStruct(s, F32) for s in pshapes] * 4
    a_shape = dwp.shape
    b_shape = (SMALL_ROWS, D_FF)
    c_shape = dcw4.shape
    n_in = 3 + nv + 3 * (2 + nv)
    outs = pl.pallas_call(
        body, name="small_update",
        in_specs=[_whole()] * n_in, out_specs=[_whole()] * len(out_shape), out_shape=out_shape,
        scratch_shapes=[pltpu.VMEM(b_shape, F32),
                        pltpu.VMEM(a_shape, F32), pltpu.VMEM(b_shape, F32), pltpu.VMEM(c_shape, F32),
                        pltpu.VMEM((N_SHARD,) + a_shape, F32), pltpu.VMEM((N_SHARD,) + b_shape, F32),
                        pltpu.VMEM((N_SHARD,) + c_shape, F32),
                        pltpu.SemaphoreType.DMA((3,)), pltpu.SemaphoreType.DMA((3,)),
                        pltpu.SemaphoreType.DMA((3, 3)), pltpu.SemaphoreType.DMA((3, 3))],
        compiler_params=pltpu.CompilerParams(vmem_limit_bytes=VMEM_LIMIT),
    )(loss, dwp, dcw4, *vec_grads, wp, cwp, *vec_ws, m_wp, m_cwp, *vec_ms, v_wp, v_cwp, *vec_vs)
    np_ = 2 + nv
    return outs[0], [outs[1 + i * np_:1 + (i + 1) * np_] for i in range(4)]


def _pad_cw(a):
    pad = [(0, 0)] * (a.ndim - 2) + [(0, CW_PAD[0] - a.shape[-2]), (0, CW_PAD[1] - a.shape[-1])]
    return jnp.pad(a, pad)


def kernel(x, w_in, w_pool, pool_scale, w_out, ln1_g, ln1_b, w_up, conv_w, conv_b, w_down, ln2_g, ln2_b, loss_target, m_w_in, m_w_pool, m_pool_scale, m_w_out, m_ln1_g, m_ln1_b, m_w_up, m_conv_w, m_conv_b, m_w_down, m_ln2_g, m_ln2_b, v_w_in, v_w_pool, v_pool_scale, v_w_out, v_ln1_g, v_ln1_b, v_w_up, v_conv_w, v_conv_b, v_w_down, v_ln2_g, v_ln2_b):
    pos = jnp.stack([lax.axis_index("c"), 2 * lax.axis_index("x") + lax.axis_index("y")]).astype(jnp.int32)
    order = ("w_in", "w_out", "w_up", "w_down")
    w_in_i, w_out_i, w_up_i, w_down_i = range(N_BIG)

    gathered = _gather_weights([w_in[0], w_out[0], w_up[0], w_down[0]], _pad_cw(conv_w[0]), (w_in_i,))
    cw_full = jnp.transpose(gathered[N_BIG][:, 0:3, 0:DOWN_SH], (1, 0, 2)).reshape(3, D_FF)
    up_half = SHARD_SHAPES[w_up_i][0] // 2
    up_a, up_b = (0, up_half // 2), (up_half // 2, up_half // 2)

    class MeshComm:
        def __init__(self):
            self.w = {i: gathered[i] for i in range(N_BIG)}
            self.g32, self.g16, self.recv_a, self.p32, self.p16, self.recv_b = {}, {}, {}, {}, {}, {}
            self.up_complete = False

        def weight(self, name):
            i = order.index(name)
            if name == "w_up" and not self.up_complete:
                (arrs, _), = _comm_only("gather_up_last", [_gather_rider({i: self.w[i]}, [("d2d", i, up_b)])])
                self.w[i], self.up_complete = arrs[0], True
            full = self.w[i]
            return full.reshape(-1, full.shape[-1]) if name in ("w_out", "w_down") else full

        def _gather(self, ws, ops):
            return _gather_rider({w: self.w[w] for w in ws}, ops), ("w", ws)

        def _pair(self, ws):
            return _pair_rider(ws, [self.g16[w] for w in ws]), ("recv_a", ws)

        def _chip(self, ws):
            return _chip_rider(ws, [self.p16[w] for w in ws]), ("recv_b", ws)

        def plan(self, call):
            if call == "proj_pool":
                return [self._gather([w_out_i, w_down_i], [("ici", w_out_i, _whole_half(w_out_i)),
                                                          ("ici", w_down_i, _whole_half(w_down_i))])]
            if call == "retention_fwd":
                return [self._gather([w_out_i, w_up_i, w_down_i],
                                     [("d2d", w_out_i, _whole_half(w_out_i)),
                                      ("d2d", w_down_i, _whole_half(w_down_i)), ("ici", w_up_i, up_a)])]
            if call == "outproj_ln1":
                return [self._gather([w_up_i], [("ici", w_up_i, up_b), ("d2d", w_up_i, up_a)])]
            if call == "mix_bwd":
                return [self._pair([w_up_i, w_down_i])]
            if call == "retention_bwd":
                return [self._chip([w_down_i])]
            if call == "wgrad_in":
                return [self._chip([w_up_i])]
            if call == "wgrad_out":
                return [self._pair([w_in_i])]
            if call == "dx":
                return [self._chip([w_in_i]), self._pair([w_out_i])]
            return []

        def riders(self, call):
            self.pending = self.plan(call)
            return [r for r, _ in self.pending]

        def landed(self, call, results):
            for (_, (slot, ws)), (inplace, lands) in zip(self.pending, results):
                for w, arr in zip(ws, inplace if slot == "w" else lands):
                    getattr(self, slot)[w] = arr
            if call == "mix_bwd":
                self._sum([w_up_i, w_down_i])
            if call == "wgrad_out":
                self._sum([w_in_i])
            if call == "dx":
                self._sum([w_out_i])
                (_, lands), = _comm_only("chip_exchange_out", [self._chip([w_out_i])[0]])
                self.recv_b[w_out_i] = lands[0]

        def _sum(self, ws):
            p32s, p16s = _pair_sum(pos, ws, [self.g32[w] for w in ws], [self.recv_a[w] for w in ws])
            for w, p32, p16 in zip(ws, p32s, p16s):
                self.p32[w], self.p16[w] = p32, p16

        def gradient(self, name, g32, g16):
            w = order.index(name)
            shape = (N_SHARD,) + SHARD_SHAPES[w]
            self.g32[w], self.g16[w] = g32.reshape(shape), g16.reshape(shape)

    comm = MeshComm()
    loss, grad_x, small = _local_step(x[0], loss_target[0], cw_full, conv_b, w_pool[0], pool_scale,
                                      ln1_g, ln1_b, ln2_g, ln2_b, comm)
    every = range(N_BIG)
    mine = _chip_sum(pos, [comm.p32[w] for w in every], [comm.recv_b[w] for w in every])
    (_, theirs), = _comm_only("pair_exchange_f32", [_final_rider(mine)])
    big_out = _adam_big(pos, mine, theirs, [w_in, w_out, w_up, w_down], [m_w_in, m_w_out, m_w_up, m_w_down],
                        [v_w_in, v_w_out, v_w_up, v_w_down])

    dcw4 = _pad_cw(jnp.transpose(small["conv_w"].reshape(3, N_SHARD, DOWN_SH), (1, 0, 2)))
    vec_names = ("conv_b", "pool_scale", "ln1_g", "ln1_b", "ln2_g", "ln2_b")
    given = dict(w_pool=w_pool, pool_scale=pool_scale, ln1_g=ln1_g, ln1_b=ln1_b, conv_w=conv_w, conv_b=conv_b,
                 ln2_g=ln2_g, ln2_b=ln2_b)
    given_m = dict(w_pool=m_w_pool, pool_scale=m_pool_scale, ln1_g=m_ln1_g, ln1_b=m_ln1_b, conv_w=m_conv_w,
                   conv_b=m_conv_b, ln2_g=m_ln2_g, ln2_b=m_ln2_b)
    given_v = dict(w_pool=v_w_pool, pool_scale=v_pool_scale, ln1_g=v_ln1_g, ln1_b=v_ln1_b, conv_w=v_conv_w,
                   conv_b=v_conv_b, ln2_g=v_ln2_g, ln2_b=v_ln2_b)
    args = []
    for src in (given, given_m, given_v):
        args += [src["w_pool"][0], _pad_cw(src["conv_w"][0]), [src[n] for n in vec_names]]
    loss_tot, small_out = _small_update(loss, small["w_pool"], dcw4, [small[n] for n in vec_names], *args)

    names = ("w_in", "w_pool", "pool_scale", "w_out", "ln1_g", "ln1_b", "w_up", "conv_w", "conv_b", "w_down",
             "ln2_g", "ln2_b")
    small_names = ("w_pool", "conv_w") + vec_names
    result = [loss_tot.reshape(()), grad_x[None]]
    for kind in range(4):
        for n in names:
            if n in order:
                result.append(big_out[kind][order.index(n)])
            else:
                val = small_out[kind][small_names.index(n)]
                if n == "conv_w":
                    val = val[0:3, 0:DOWN_SH][None]
                elif n == "w_pool":
                    val = val[None]
                result.append(val)
    return tuple(result)
```

```python
import functools
import math

import numpy as np
import jax
import jax.numpy as jnp
from jax import lax
from jax.experimental import pallas as pl
from jax.experimental.pallas import tpu as pltpu

F32 = jnp.float32
BF16 = jnp.bfloat16

D_MODEL = 1024
HEADS = 4
HEAD_DIM = 128
RET_W = HEADS * HEAD_DIM
POOL_WINDOWS = (2, 4, 8, 16)
POOL_W = 512
IN_W = 4 * RET_W + POOL_W
D_FF = 2816
N_SHARD = 4
IN_SH = IN_W // N_SHARD
UP_SH = 2 * D_FF // N_SHARD
DOWN_SH = D_FF // N_SHARD
OUT_SH = D_MODEL // N_SHARD
ROPE_BASE = 10000.0
LN_EPS = 1e-5
RMS_EPS = 1e-6
ALPHA = 2.0 ** 0.25
K_SCALE = HEAD_DIM ** -0.5
SUPER = 256
CHUNK = 64
POOL_HALO = 16
CONV_HALO = 8

ADAM_LR = 0.001
ADAM_B1 = 0.9
ADAM_B2 = 0.999
ADAM_EPS = 1e-08
ADAM_WD = 0.01
ADAM_STEP = 10

MESH = pl.DeviceIdType.MESH
VMEM_LIMIT = 56 * 1024 * 1024


def _dot(a, b):
    return jnp.dot(a, b, preferred_element_type=F32)


def _dot_nt(a, b):
    return lax.dot_general(a, b, (((1,), (1,)), ((), ())), preferred_element_type=F32)


def _dot_tn(a, b):
    return lax.dot_general(a, b, (((0,), (0,)), ((), ())), preferred_element_type=F32)


def _sigmoid(x):
    return 1.0 / (1.0 + jnp.exp(-x))


def _params(sem):
    return pltpu.CompilerParams(dimension_semantics=sem, vmem_limit_bytes=VMEM_LIMIT)


def _whole():
    return pl.BlockSpec(memory_space=pltpu.VMEM)


HBM_SPEC = pl.BlockSpec(memory_space=pl.ANY)


class _Rider:
    def __init__(self, inplace, srcs, lands, n_copies, make):
        self.inplace, self.srcs, self.lands, self.n_copies, self.make = list(inplace), list(srcs), list(lands), n_copies, make


def _call(body, *, name, grid, in_specs, out_specs, out_shape, operands, scratch_shapes=(), sem=(),
          aliases=None, riders=()):
    n_in, n_out, n_scr = len(in_specs), len(out_shape), len(scratch_shapes)
    in_specs, out_specs, out_shape = list(in_specs), list(out_specs), list(out_shape)
    operands, scratch_shapes, aliases = list(operands), list(scratch_shapes), dict(aliases or {})
    for r in riders:
        for a in r.inplace:
            aliases[len(in_specs)] = len(out_shape)
            in_specs.append(HBM_SPEC)
            operands.append(a)
            out_specs.append(HBM_SPEC)
            out_shape.append(jax.ShapeDtypeStruct(a.shape, a.dtype))
        for a in r.srcs:
            in_specs.append(HBM_SPEC)
            operands.append(a)
        for shp in r.lands:
            out_specs.append(HBM_SPEC)
            out_shape.append(shp)
        scratch_shapes += [pltpu.SemaphoreType.DMA((r.n_copies,)), pltpu.SemaphoreType.DMA((r.n_copies,))]

    def full(*refs):
        ins = refs[:n_in]
        at = n_in
        r_srcs = []
        for r in riders:
            at += len(r.inplace)
            r_srcs.append(refs[at:at + len(r.srcs)])
            at += len(r.srcs)
        outs = refs[at:at + n_out]
        at += n_out
        r_outs = []
        for r in riders:
            r_outs.append((refs[at:at + len(r.inplace)], refs[at + len(r.inplace):at + len(r.inplace) + len(r.lands)]))
            at += len(r.inplace) + len(r.lands)
        scr = refs[at:at + n_scr]
        at += n_scr
        r_sems = [refs[at + 2 * i:at + 2 * i + 2] for i in range(len(riders))]

        def copies():
            return [r.make(r_outs[i][0], r_srcs[i], r_outs[i][1], r_sems[i][0], r_sems[i][1])
                    for i, r in enumerate(riders)]

        def start():
            for starts, _ in copies():
                for cp in starts:
                    cp.start()

        def finish():
            for _, waits in copies():
                for wait in waits:
                    wait()

        if riders and grid:
            first = functools.reduce(jnp.logical_and, [pl.program_id(d) == 0 for d in range(len(grid))])
            last = functools.reduce(jnp.logical_and, [pl.program_id(d) == grid[d] - 1 for d in range(len(grid))])
            pl.when(first)(start)
            body(*ins, *outs, *scr)
            pl.when(last)(finish)
        else:
            if riders:
                start()
            body(*ins, *outs, *scr)
            if riders:
                finish()

    params = _params(sem) if grid else pltpu.CompilerParams(vmem_limit_bytes=VMEM_LIMIT)
    res = pl.pallas_call(
        full, name=name, grid=grid, in_specs=in_specs, out_specs=out_specs, out_shape=out_shape,
        scratch_shapes=scratch_shapes, input_output_aliases=aliases, compiler_params=params,
    )(*operands)
    outs, at, rider_res = res[:n_out], n_out, []
    for r in riders:
        rider_res.append((res[at:at + len(r.inplace)], res[at + len(r.inplace):at + len(r.inplace) + len(r.lands)]))
        at += len(r.inplace) + len(r.lands)
    return list(outs), rider_res


def _gammas():
    return [1.0 - 2.0 ** (-5.0 - h) for h in range(HEADS)]


def _decay_tables():
    idx = np.arange(SUPER)
    dist = np.abs(idx[:, None] - idx[None, :]).astype(np.float64)
    visible = (idx[None, :] // CHUNK) <= (idx[:, None] // CHUNK)
    mask = np.stack([np.where(visible, g ** dist, 0.0) for g in _gammas()])
    qd = np.concatenate([np.repeat((g ** (idx + 1.0))[:, None], HEAD_DIM, 1) for g in _gammas()], 1)
    kd = np.concatenate([np.repeat((g ** (SUPER - 1.0 - idx))[:, None], HEAD_DIM, 1) for g in _gammas()], 1)
    return (jnp.asarray(mask, F32), jnp.asarray(qd, F32), jnp.asarray(kd, F32))


def _rope_tables(s):
    inv_freq = ROPE_BASE ** (-np.arange(0, HEAD_DIM, 2, dtype=np.float64) / HEAD_DIM)
    ang = np.arange(s, dtype=np.float64)[:, None] * inv_freq[None, :]
    cos, sin = np.cos(ang), np.sin(ang)
    return (jnp.asarray(np.concatenate([cos, cos], 1), F32),
            jnp.asarray(np.concatenate([-sin, sin], 1), F32))


def _rope(t, cosf, sinf):
    return t * cosf + pltpu.roll(t, HEAD_DIM // 2, 1) * sinf


def _rope_t(t, cosf, sinf):
    return t * cosf - pltpu.roll(t, HEAD_DIM // 2, 1) * sinf


def _layernorm_fwd(z):
    mu = jnp.mean(z, axis=-1, keepdims=True)
    zc = z - mu
    var = jnp.mean(zc * zc, axis=-1, keepdims=True)
    rstd = lax.rsqrt(var + LN_EPS)
    return zc * rstd, rstd


def _layernorm_bwd(dy, xhat, rstd, gain):
    dxh = dy * gain
    m1 = jnp.mean(dxh, axis=-1, keepdims=True)
    m2 = jnp.mean(dxh * xhat, axis=-1, keepdims=True)
    return rstd * (dxh - m1 - xhat * m2)


def _proj_pool(x, win4, cosf, sinf, wpool, pscale, ts, riders=()):
    s = x.shape[0]
    nt = s // ts

    def body(x_ref, w_ref, cos_ref, sin_ref, wp_ref, ps_ref,
             xb_ref, q_ref, k_ref, v_ref, g_ref, pooled_ref, cat_ref, proj_scr, pext_scr):
        i = pl.program_id(0)
        xb = x_ref[...].astype(BF16)
        xb_ref[...] = xb
        for j in range(N_SHARD):
            proj_scr[:, j * IN_SH:(j + 1) * IN_SH] = _dot(xb, w_ref[j])
        cosf_t = cos_ref[...]
        sinf_t = sin_ref[...]
        for h in range(HEADS):
            lo = h * HEAD_DIM
            q_ref[:, lo:lo + HEAD_DIM] = _rope(proj_scr[:, lo:lo + HEAD_DIM], cosf_t, sinf_t).astype(BF16)
            kk = _rope(proj_scr[:, RET_W + lo:RET_W + lo + HEAD_DIM], cosf_t, sinf_t) * K_SCALE
            k_ref[:, lo:lo + HEAD_DIM] = kk.astype(BF16)
        v_ref[...] = proj_scr[:, 2 * RET_W:3 * RET_W].astype(BF16)
        g_ref[...] = proj_scr[:, 3 * RET_W:4 * RET_W]

        @pl.when(i == 0)
        def _():
            pext_scr[0:POOL_HALO, :] = jnp.zeros((POOL_HALO, POOL_W), F32)

        pext_scr[POOL_HALO:POOL_HALO + ts, :] = proj_scr[:, 4 * RET_W:IN_W]
        pos = (i * ts + lax.broadcasted_iota(jnp.int32, (ts, 1), 0) + 1).astype(F32)
        for gi, w in enumerate(POOL_WINDOWS):
            lo = gi * HEAD_DIM
            ext = pext_scr[:, lo:lo + HEAD_DIM]
            acc = ext
            shift = 1
            while shift < w:
                acc = acc + pltpu.roll(acc, shift, 0)
                shift *= 2
            tok = ext[POOL_HALO:POOL_HALO + ts]
            pooled = acc[POOL_HALO:POOL_HALO + ts] / jnp.minimum(pos, float(w)) - tok
            pooled_b = pooled.astype(BF16)
            pooled_ref[:, lo:lo + HEAD_DIM] = pooled_b
            lin = _dot(pooled_b, wp_ref[gi])
            cat_ref[:, lo:lo + HEAD_DIM] = (lin * ps_ref[:, lo:lo + HEAD_DIM]).astype(BF16)
        pext_scr[0:POOL_HALO, :] = pext_scr[ts:ts + POOL_HALO, :]

    tile = lambda w: pl.BlockSpec((ts, w), lambda i: (i, 0))
    return _call(
        body, name="proj_pool", grid=(nt,),
        in_specs=[tile(D_MODEL), _whole(), tile(HEAD_DIM), tile(HEAD_DIM), _whole(), _whole()],
        out_specs=[tile(D_MODEL), tile(RET_W), tile(RET_W), tile(RET_W), tile(RET_W), tile(POOL_W),
                   pl.BlockSpec((ts, POOL_W), lambda i: (i, 1))],
        out_shape=[jax.ShapeDtypeStruct((s, D_MODEL), BF16), jax.ShapeDtypeStruct((s, RET_W), BF16),
                   jax.ShapeDtypeStruct((s, RET_W), BF16), jax.ShapeDtypeStruct((s, RET_W), BF16),
                   jax.ShapeDtypeStruct((s, RET_W), F32), jax.ShapeDtypeStruct((s, POOL_W), BF16),
                   jax.ShapeDtypeStruct((s, 2 * RET_W), BF16)],
        scratch_shapes=[pltpu.VMEM((ts, IN_W), F32), pltpu.VMEM((ts + POOL_HALO, POOL_W), F32)],
        sem=("arbitrary",), operands=(x, win4, cosf, sinf, wpool, pscale), riders=riders,
    )


def _retention_fwd(q, k, v, g, cat, mask, qd, kd, riders=()):
    s = q.shape[0]
    ns = s // SUPER
    cdec = [gm ** float(SUPER) for gm in _gammas()]

    def body(q_ref, k_ref, v_ref, g_ref, cat_in, mask_ref, qd_ref, kd_ref,
             ret_ref, cat_ref, st_ref, state_scr):
        del cat_in
        n = pl.program_id(0)

        @pl.when(n == 0)
        def _():
            state_scr[...] = jnp.zeros_like(state_scr)

        for h in range(HEADS):
            sl = slice(h * HEAD_DIM, (h + 1) * HEAD_DIM)
            qh, kh, vh = q_ref[:, sl], k_ref[:, sl], v_ref[:, sl]
            sc = _dot_nt(qh, kh) * mask_ref[h]
            st = state_scr[h]
            stb = st.astype(BF16)
            st_ref[0, h] = stb
            qdb = (qh.astype(F32) * qd_ref[:, sl]).astype(BF16)
            kdb = (kh.astype(F32) * kd_ref[:, sl]).astype(BF16)
            ret = _dot(sc.astype(BF16), vh) + _dot(qdb, stb)
            state_scr[h] = st * cdec[h] + _dot_tn(kdb, vh)
            ret_ref[:, sl] = ret
            r = lax.rsqrt(jnp.mean(ret * ret, axis=-1, keepdims=True) + RMS_EPS)
            gh = g_ref[:, sl]
            cat_ref[:, sl] = ((ret * r) * (gh * _sigmoid(gh))).astype(BF16)

    tile = pl.BlockSpec((SUPER, RET_W), lambda n: (n, 0))
    return _call(
        body, name="retention_fwd", grid=(ns,),
        in_specs=[tile, tile, tile, tile, HBM_SPEC, _whole(), _whole(), _whole()],
        out_specs=[tile, tile, pl.BlockSpec((1, HEADS, HEAD_DIM, HEAD_DIM), lambda n: (n, 0, 0, 0))],
        out_shape=[jax.ShapeDtypeStruct((s, RET_W), F32), jax.ShapeDtypeStruct((s, 2 * RET_W), BF16),
                   jax.ShapeDtypeStruct((ns, HEADS, HEAD_DIM, HEAD_DIM), BF16)],
        scratch_shapes=[pltpu.VMEM((HEADS, HEAD_DIM, HEAD_DIM), F32)],
        aliases={4: 1}, sem=("arbitrary",), operands=(q, k, v, g, cat, mask, qd, kd), riders=riders,
    )


def _outproj_ln1(x, cat, wout, g1, b1, ts, riders=()):
    s = x.shape[0]

    def body(x_ref, cat_ref, w_ref, g_ref, b_ref, xhat_ref, rstd_ref, h1b_ref):
        z = ALPHA * x_ref[...] + _dot(cat_ref[...], w_ref[...])
        xhat, rstd = _layernorm_fwd(z)
        xhat_ref[...] = xhat
        rstd_ref[...] = rstd
        h1b_ref[...] = (xhat * g_ref[...] + b_ref[...]).astype(BF16)

    tile = lambda w: pl.BlockSpec((ts, w), lambda i: (i, 0))
    return _call(
        body, name="outproj_ln1", grid=(s // ts,),
        in_specs=[tile(D_MODEL), tile(D_MODEL), _whole(), _whole(), _whole()],
        out_specs=[tile(D_MODEL), tile(1), tile(D_MODEL)],
        out_shape=[jax.ShapeDtypeStruct((s, D_MODEL), F32), jax.ShapeDtypeStruct((s, 1), F32),
                   jax.ShapeDtypeStruct((s, D_MODEL), BF16)],
        sem=("arbitrary",), operands=(x, cat, wout, g1, b1), riders=riders,
    )


def _conv(cw_ref, cb_ref, g2s, g1s, gate):
    return cb_ref[...] + ((g2s * cw_ref[0:1, :] + g1s * cw_ref[1:2, :]) + gate * cw_ref[2:3, :])


def _ffn_fwd_loss(xhat1, h1b, target, wup4, wdown, cw, cb, g1, b1, g2, b2, ts):
    s = xhat1.shape[0]

    def body(xhat_ref, h1b_ref, tgt_ref, wup_ref, wdn_ref, cw_ref, cb_ref, g1_ref, b1_ref, g2_ref, b2_ref,
             ub_ref, dz2_ref, dz2b_ref, loss_ref, dg2_ref, db2_ref, u_scr, gext_scr):
        i = pl.program_id(0)

        @pl.when(i == 0)
        def _():
            gext_scr[0:CONV_HALO, :] = jnp.zeros((CONV_HALO, D_FF), F32)
            loss_ref[...] = jnp.zeros_like(loss_ref)
            dg2_ref[...] = jnp.zeros_like(dg2_ref)
            db2_ref[...] = jnp.zeros_like(db2_ref)

        hb = h1b_ref[...]
        for j in range(N_SHARD):
            u_scr[:, j * UP_SH:(j + 1) * UP_SH] = _dot(hb, wup_ref[j])
        ub_ref[...] = u_scr[...].astype(BF16)
        gate = u_scr[:, D_FF:2 * D_FF]
        gext_scr[CONV_HALO:CONV_HALO + ts, :] = gate
        ext = gext_scr[...]
        hc = _conv(cw_ref, cb_ref, pltpu.roll(ext, 2, 0)[CONV_HALO:], pltpu.roll(ext, 1, 0)[CONV_HALO:], gate)
        gext_scr[0:CONV_HALO, :] = gext_scr[ts:ts + CONV_HALO, :]
        act = (hc * _sigmoid(hc)) * u_scr[:, 0:D_FF]
        h1 = xhat_ref[...] * g1_ref[...] + b1_ref[...]
        z2 = ALPHA * h1 + _dot(act.astype(BF16), wdn_ref[...])
        xhat2, rstd2 = _layernorm_fwd(z2)
        diff = (xhat2 * g2_ref[...] + b2_ref[...]) - tgt_ref[...]
        row = jnp.mean(diff * diff, axis=-1, keepdims=True)
        loss_ref[...] += 0.5 * jnp.sum(row, axis=0, keepdims=True)
        dy = diff * (1.0 / D_MODEL)
        dg2_ref[...] += jnp.sum(dy * xhat2, axis=0, keepdims=True)
        db2_ref[...] += jnp.sum(dy, axis=0, keepdims=True)
        dz2 = _layernorm_bwd(dy, xhat2, rstd2, g2_ref[...])
        dz2_ref[...] = dz2
        dz2b_ref[...] = dz2.astype(BF16)

    tile = lambda w: pl.BlockSpec((ts, w), lambda i: (i, 0))
    acc = lambda w: pl.BlockSpec((1, w), lambda i: (0, 0))
    return pl.pallas_call(
        body, name="ffn_fwd_loss", grid=(s // ts,),
        in_specs=[tile(D_MODEL), tile(D_MODEL), tile(D_MODEL)] + [_whole()] * 8,
        out_specs=[tile(2 * D_FF), tile(D_MODEL), tile(D_MODEL), acc(1), acc(D_MODEL), acc(D_MODEL)],
        out_shape=[jax.ShapeDtypeStruct((s, 2 * D_FF), BF16), jax.ShapeDtypeStruct((s, D_MODEL), F32),
                   jax.ShapeDtypeStruct((s, D_MODEL), BF16),
                   jax.ShapeDtypeStruct((1, 1), F32), jax.ShapeDtypeStruct((1, D_MODEL), F32),
                   jax.ShapeDtypeStruct((1, D_MODEL), F32)],
        scratch_shapes=[pltpu.VMEM((ts, 2 * D_FF), F32), pltpu.VMEM((ts + CONV_HALO, D_FF), F32)],
        compiler_params=_params(("arbitrary",)),
    )(xhat1, h1b, target, wup4, wdown, cw, cb, g1, b1, g2, b2)


def _ffn_bwd(dz2, ub, xhat1, rstd1, wup4, wdown, cw, cb, g1, ts):
    s = dz2.shape[0]
    nt = s // ts
    hb = 16

    def body(dz2_ref, ub_ref, prev_ref, xhat_ref, rstd_ref, wup_ref, wdn_ref, cw_ref, cb_ref, g1_ref,
             a_ref, dub_ref, dz1_ref, dz1b_ref, dg1_ref, db1_ref, dcw_ref, dcb_ref, gext_scr, dext_scr):
        i = pl.program_id(0)
        r = nt - 1 - i

        @pl.when(i == 0)
        def _():
            dext_scr[ts:ts + CONV_HALO, :] = jnp.zeros((CONV_HALO, D_FF), F32)
            dg1_ref[...] = jnp.zeros_like(dg1_ref)
            db1_ref[...] = jnp.zeros_like(db1_ref)
            dcw_ref[...] = jnp.zeros_like(dcw_ref)
            dcb_ref[...] = jnp.zeros_like(dcb_ref)

        dz2 = dz2_ref[...]
        da = _dot_nt(dz2.astype(BF16), wdn_ref[...])
        val = ub_ref[:, 0:D_FF].astype(F32)
        gate = ub_ref[:, D_FF:2 * D_FF].astype(F32)
        prev = prev_ref[...].astype(F32)[hb - CONV_HALO:hb]
        gext_scr[0:CONV_HALO, :] = jnp.where(r == 0, 0.0, prev)
        gext_scr[CONV_HALO:CONV_HALO + ts, :] = gate
        ext = gext_scr[...]
        g2s = pltpu.roll(ext, 2, 0)[CONV_HALO:]
        g1s = pltpu.roll(ext, 1, 0)[CONV_HALO:]
        hc = _conv(cw_ref, cb_ref, g2s, g1s, gate)
        sg = _sigmoid(hc)
        si = hc * sg
        a_ref[...] = (si * val).astype(BF16)
        dhc = da * val * (sg * (1.0 + hc * (1.0 - sg)))
        dcb_ref[...] += jnp.sum(dhc, axis=0, keepdims=True)
        dcw_ref[0:1, :] += jnp.sum(dhc * g2s, axis=0, keepdims=True)
        dcw_ref[1:2, :] += jnp.sum(dhc * g1s, axis=0, keepdims=True)
        dcw_ref[2:3, :] += jnp.sum(dhc * gate, axis=0, keepdims=True)
        dext_scr[0:ts, :] = dhc
        dext = dext_scr[...]
        dgate = (dhc * cw_ref[2:3, :] + pltpu.roll(dext, ts + CONV_HALO - 1, 0)[0:ts] * cw_ref[1:2, :]
                 + pltpu.roll(dext, ts + CONV_HALO - 2, 0)[0:ts] * cw_ref[0:1, :])
        dext_scr[ts:ts + CONV_HALO, :] = dext_scr[0:CONV_HALO, :]
        dub_ref[:, 0:D_FF] = (da * si).astype(BF16)
        dub_ref[:, D_FF:2 * D_FF] = dgate.astype(BF16)
        dh1 = ALPHA * dz2
        for j in range(N_SHARD):
            dh1 = dh1 + _dot_nt(dub_ref[:, j * UP_SH:(j + 1) * UP_SH], wup_ref[j])
        xhat = xhat_ref[...]
        dg1_ref[...] += jnp.sum(dh1 * xhat, axis=0, keepdims=True)
        db1_ref[...] += jnp.sum(dh1, axis=0, keepdims=True)
        dz1 = _layernorm_bwd(dh1, xhat, rstd_ref[...], g1_ref[...])
        dz1_ref[...] = dz1
        dz1b_ref[...] = dz1.astype(BF16)

    tile = lambda w: pl.BlockSpec((ts, w), lambda i: (nt - 1 - i, 0))
    acc = lambda rws, w: pl.BlockSpec((rws, w), lambda i: (0, 0))
    prev_spec = pl.BlockSpec((hb, D_FF), lambda i: (jnp.maximum((nt - 1 - i) * (ts // hb) - 1, 0), 1))
    return pl.pallas_call(
        body, name="ffn_bwd", grid=(nt,),
        in_specs=[tile(D_MODEL), tile(2 * D_FF), prev_spec, tile(D_MODEL), tile(1)] + [_whole()] * 5,
        out_specs=[tile(D_FF), tile(2 * D_FF), tile(D_MODEL), tile(D_MODEL), acc(1, D_MODEL), acc(1, D_MODEL),
                   acc(3, D_FF), acc(1, D_FF)],
        out_shape=[jax.ShapeDtypeStruct((s, D_FF), BF16), jax.ShapeDtypeStruct((s, 2 * D_FF), BF16),
                   jax.ShapeDtypeStruct((s, D_MODEL), F32), jax.ShapeDtypeStruct((s, D_MODEL), BF16),
                   jax.ShapeDtypeStruct((1, D_MODEL), F32),
                   jax.ShapeDtypeStruct((1, D_MODEL), F32), jax.ShapeDtypeStruct((3, D_FF), F32),
                   jax.ShapeDtypeStruct((1, D_FF), F32)],
        scratch_shapes=[pltpu.VMEM((ts + CONV_HALO, D_FF), F32), pltpu.VMEM((ts + CONV_HALO, D_FF), F32)],
        compiler_params=_params(("arbitrary",)),
    )(dz2, ub, ub, xhat1, rstd1, wup4, wdown, cw, cb, g1)


def _mix_bwd(dz1, pooled, ret, g, wout, wpool, pscale, ts, riders=()):
    s = dz1.shape[0]
    nt = s // ts

    def body(dz1_ref, pooled_ref, ret_ref, g_ref, wout_ref, wp_ref, ps_ref,
             dret_ref, dgp_ref, dwp_ref, dps_ref, eext_scr):
        i = pl.program_id(0)
        r = nt - 1 - i

        @pl.when(i == 0)
        def _():
            eext_scr[ts:ts + POOL_HALO, :] = jnp.zeros((POOL_HALO, POOL_W), F32)
            dwp_ref[...] = jnp.zeros_like(dwp_ref)
            dps_ref[...] = jnp.zeros_like(dps_ref)

        dzb = dz1_ref[...].astype(BF16)
        dcat_r = _dot_nt(dzb, wout_ref[0:RET_W, :])
        dcat_p = _dot_nt(dzb, wout_ref[RET_W:2 * RET_W, :])
        pos = (r * ts + lax.broadcasted_iota(jnp.int32, (ts, 1), 0) + 1).astype(F32)
        dpooled = []
        for gi, w in enumerate(POOL_WINDOWS):
            sl = slice(gi * HEAD_DIM, (gi + 1) * HEAD_DIM)
            pb = pooled_ref[:, sl]
            dy = dcat_p[:, sl]
            dps_ref[:, sl] += jnp.sum(dy * _dot(pb, wp_ref[gi]), axis=0, keepdims=True)
            dlin = (dy * ps_ref[:, sl]).astype(BF16)
            dwp_ref[gi] += _dot_tn(pb, dlin)
            dpg = _dot_nt(dlin, wp_ref[gi])
            dpooled.append(dpg)
            eext_scr[0:ts, sl] = dpg / jnp.minimum(pos, float(w))
        for gi, w in enumerate(POOL_WINDOWS):
            sl = slice(gi * HEAD_DIM, (gi + 1) * HEAD_DIM)
            acc = eext_scr[:, sl]
            shift = 1
            while shift < w:
                acc = acc + pltpu.roll(acc, ts + POOL_HALO - shift, 0)
                shift *= 2
            dgp_ref[:, RET_W + gi * HEAD_DIM:RET_W + (gi + 1) * HEAD_DIM] = (acc[0:ts] - dpooled[gi]).astype(BF16)
        eext_scr[ts:ts + POOL_HALO, :] = eext_scr[0:POOL_HALO, :]
        for h in range(HEADS):
            sl = slice(h * HEAD_DIM, (h + 1) * HEAD_DIM)
            rt = ret_ref[:, sl]
            rr = lax.rsqrt(jnp.mean(rt * rt, axis=-1, keepdims=True) + RMS_EPS)
            rn = rt * rr
            gh = g_ref[:, sl]
            sg = _sigmoid(gh)
            dy = dcat_r[:, sl]
            dgp_ref[:, sl] = (dy * rn * (sg * (1.0 + gh * (1.0 - sg)))).astype(BF16)
            drn = dy * (gh * sg)
            dret_ref[:, sl] = (rr * (drn - rn * jnp.mean(drn * rn, axis=-1, keepdims=True))).astype(BF16)

    tile = lambda w: pl.BlockSpec((ts, w), lambda i: (nt - 1 - i, 0))
    return _call(
        body, name="mix_bwd", grid=(nt,),
        in_specs=[tile(D_MODEL), tile(POOL_W), tile(RET_W), tile(RET_W), _whole(), _whole(), _whole()],
        out_specs=[tile(RET_W), tile(2 * RET_W),
                   pl.BlockSpec((len(POOL_WINDOWS), HEAD_DIM, HEAD_DIM), lambda i: (0, 0, 0)),
                   pl.BlockSpec((1, POOL_W), lambda i: (0, 0))],
        out_shape=[jax.ShapeDtypeStruct((s, RET_W), BF16), jax.ShapeDtypeStruct((s, 2 * RET_W), BF16),
                   jax.ShapeDtypeStruct((len(POOL_WINDOWS), HEAD_DIM, HEAD_DIM), F32),
                   jax.ShapeDtypeStruct((1, POOL_W), F32)],
        scratch_shapes=[pltpu.VMEM((ts + POOL_HALO, POOL_W), F32)],
        sem=("arbitrary",), operands=(dz1, pooled, ret, g, wout, wpool, pscale), riders=riders,
    )


def _retention_bwd(q, k, v, dret, dgp, states, mask, qd, kd, cosf, sinf, riders=()):
    s = q.shape[0]
    ns = s // SUPER
    cdec = [gm ** float(SUPER) for gm in _gammas()]

    def body(q_ref, k_ref, v_ref, do_ref, dgp_ref, st_ref, mask_ref, qd_ref, kd_ref, cos_ref, sin_ref,
             dproj_ref, dstate_scr):
        i = pl.program_id(0)

        @pl.when(i == 0)
        def _():
            dstate_scr[...] = jnp.zeros_like(dstate_scr)

        cosf_t = cos_ref[...]
        sinf_t = sin_ref[...]
        for h in range(HEADS):
            sl = slice(h * HEAD_DIM, (h + 1) * HEAD_DIM)
            qh, kh, vh, doh = q_ref[:, sl], k_ref[:, sl], v_ref[:, sl], do_ref[:, sl]
            m = mask_ref[h]
            scb = (_dot_nt(qh, kh) * m).astype(BF16)
            dscb = (_dot_nt(doh, vh) * m).astype(BF16)
            stb = st_ref[0, h]
            dst = dstate_scr[h]
            dstb = dst.astype(BF16)
            qdb = (qh.astype(F32) * qd_ref[:, sl]).astype(BF16)
            kdb = (kh.astype(F32) * kd_ref[:, sl]).astype(BF16)
            dq = _dot(dscb, kh) + _dot_nt(doh, stb) * qd_ref[:, sl]
            dk = _dot_tn(dscb, qh) + _dot_nt(vh, dstb) * kd_ref[:, sl]
            dv = _dot_tn(scb, doh) + _dot(kdb, dstb)
            dstate_scr[h] = dst * cdec[h] + _dot_tn(qdb, doh)
            lo = h * HEAD_DIM
            dproj_ref[:, lo:lo + HEAD_DIM] = _rope_t(dq, cosf_t, sinf_t).astype(BF16)
            dproj_ref[:, RET_W + lo:RET_W + lo + HEAD_DIM] = _rope_t(dk * K_SCALE, cosf_t, sinf_t).astype(BF16)
            dproj_ref[:, 2 * RET_W + lo:2 * RET_W + lo + HEAD_DIM] = dv.astype(BF16)
        dproj_ref[:, 3 * RET_W:IN_W] = dgp_ref[...]

    tile = lambda w: pl.BlockSpec((SUPER, w), lambda i: (ns - 1 - i, 0))
    return _call(
        body, name="retention_bwd", grid=(ns,),
        in_specs=[tile(RET_W), tile(RET_W), tile(RET_W), tile(RET_W), tile(2 * RET_W),
                  pl.BlockSpec((1, HEADS, HEAD_DIM, HEAD_DIM), lambda i: (ns - 1 - i, 0, 0, 0)),
                  _whole(), _whole(), _whole(), tile(HEAD_DIM), tile(HEAD_DIM)],
        out_specs=[tile(IN_W)],
        out_shape=[jax.ShapeDtypeStruct((s, IN_W), BF16)],
        scratch_shapes=[pltpu.VMEM((HEADS, HEAD_DIM, HEAD_DIM), F32)],
        sem=("arbitrary",), operands=(q, k, v, dret, dgp, states, mask, qd, kd, cosf, sinf), riders=riders,
    )


def _dx(dz1, dproj, win4, ts, riders=()):
    s = dz1.shape[0]

    def body(dz1_ref, dp_ref, w_ref, dx_ref):
        acc = ALPHA * dz1_ref[...]
        for j in range(N_SHARD):
            acc = acc + _dot_nt(dp_ref[:, j * IN_SH:(j + 1) * IN_SH], w_ref[j])
        dx_ref[...] = acc

    tile = lambda w: pl.BlockSpec((ts, w), lambda i: (i, 0))
    return _call(
        body, name="dx", grid=(s // ts,),
        in_specs=[tile(D_MODEL), tile(IN_W), _whole()],
        out_specs=[tile(D_MODEL)],
        out_shape=[jax.ShapeDtypeStruct((s, D_MODEL), F32)],
        sem=("arbitrary",), operands=(dz1, dproj, win4), riders=riders,
    )


def _wgrad(a, b, tm, tn, name, stacked, m_outer, riders=()):
    s, m = a.shape
    n = b.shape[1]

    def body(a_ref, b_ref, o32_ref, o16_ref):
        res = _dot_tn(a_ref[...], b_ref[...])
        o32_ref[...] = res.reshape(o32_ref.shape)
        o16_ref[...] = res.astype(BF16).reshape(o16_ref.shape)

    if m_outer:
        grid, blocks = (m // tm, n // tn), (lambda g0, g1: (g0, g1))
    else:
        grid, blocks = (n // tn, m // tm), (lambda g0, g1: (g1, g0))
    if stacked:
        shape = (n // tn, m, tn)
        ospec = pl.BlockSpec((1, tm, tn), lambda g0, g1: (blocks(g0, g1)[1], blocks(g0, g1)[0], 0))
    else:
        shape = (m, n)
        ospec = pl.BlockSpec((tm, tn), lambda g0, g1: blocks(g0, g1))
    return _call(
        body, name=name, grid=grid,
        in_specs=[pl.BlockSpec((s, tm), lambda g0, g1: (0, blocks(g0, g1)[0])),
                  pl.BlockSpec((s, tn), lambda g0, g1: (0, blocks(g0, g1)[1]))],
        out_specs=[ospec, ospec],
        out_shape=[jax.ShapeDtypeStruct(shape, F32), jax.ShapeDtypeStruct(shape, BF16)],
        sem=("arbitrary", "arbitrary"), operands=(a, b), riders=riders,
    )


class _NoComm:
    def __init__(self, win4, wout, wup4, wdown):
        self.weights = dict(w_in=win4, w_out=wout, w_up=wup4, w_down=wdown)
        self.grads = {}

    def weight(self, name):
        return self.weights[name]

    def riders(self, call):
        return ()

    def landed(self, call, results):
        pass

    def gradient(self, name, g32, g16):
        self.grads[name] = (g32, g16)


def _local_step(x, target, cw, cb, wpool, pscale, g1, b1, g2, b2, comm):
    s = x.shape[0]
    ts_a = min(512, s)
    ts_f = min(256, s)
    mask, qd, kd = _decay_tables()
    cosf, sinf = _rope_tables(s)
    wpool_b = wpool.astype(BF16)

    def run(call, fn, *args):
        outs, res = fn(*args, riders=comm.riders(call))
        comm.landed(call, res)
        return outs

    xb, q, k, v, g, pooled, cat = run("proj_pool", _proj_pool, x, comm.weight("w_in"), cosf, sinf, wpool_b,
                                      pscale, ts_a)
    ret, cat, states = run("retention_fwd", _retention_fwd, q, k, v, g, cat, mask, qd, kd)
    wout = comm.weight("w_out")
    xhat1, rstd1, h1b = run("outproj_ln1", _outproj_ln1, x, cat, wout, g1, b1, ts_a)
    wup4, wdown = comm.weight("w_up"), comm.weight("w_down")
    ub, dz2, dz2b, loss, dg2, db2 = _ffn_fwd_loss(xhat1, h1b, target, wup4, wdown, cw, cb, g1, b1, g2, b2, ts_f)

    act, dub, dz1, dz1b, dg1, db1, dcw, dcb = _ffn_bwd(dz2, ub, xhat1, rstd1, wup4, wdown, cw, cb, g1, ts_f)
    half = D_MODEL // 2
    comm.gradient("w_up", *run("wgrad_up", _wgrad, h1b, dub, half, UP_SH, "wgrad_up", True, False))
    comm.gradient("w_out", *run("wgrad_out", _wgrad, cat, dz1b, D_MODEL, half, "wgrad_out", False, True))
    comm.gradient("w_down", *run("wgrad_down", _wgrad, act, dz2b, D_FF // 2, half, "wgrad_down", False, True))
    dret, dgp, dwp, dps = run("mix_bwd", _mix_bwd, dz1b, pooled, ret, g, wout, wpool_b, pscale, ts_a)
    dproj, = run("retention_bwd", _retention_bwd, q, k, v, dret, dgp, states, mask, qd, kd, cosf, sinf)
    comm.gradient("w_in", *run("wgrad_in", _wgrad, xb, dproj, D_MODEL, IN_SH, "wgrad_in", True, True))
    grad_x, = run("dx", _dx, dz1, dproj, comm.weight("w_in"), ts_a)
    small = dict(w_pool=dwp, pool_scale=dps, ln1_g=dg1, ln1_b=db1, conv_w=dcw, conv_b=dcb,
                 ln2_g=dg2, ln2_b=db2)
    return loss, grad_x, small


CAST_ROWS = 64
SHARD_SHAPES = ((D_MODEL, IN_SH), (OUT_SH, D_MODEL), (D_MODEL, UP_SH), (DOWN_SH, D_MODEL))
N_BIG = len(SHARD_SHAPES)
CW_PAD = (8, 768)


def _mesh_pos():
    return lax.axis_index("x"), lax.axis_index("y"), lax.axis_index("c")


def _other_chips(x, y):
    return [(1 - x, y), (x, 1 - y), (1 - x, 1 - y)]


def _half_rows(w, which):
    hr = SHARD_SHAPES[w][0] // 2
    return pl.ds(pl.multiple_of(which * hr, 16), hr)


def _gather_weights(shards, cw8, full):
    def body(*refs):
        in_refs = refs[:N_BIG]
        cw_ref = refs[N_BIG]
        out_refs = refs[N_BIG + 1:2 * N_BIG + 1]
        cwo_ref = refs[2 * N_BIG + 1]
        stage = refs[2 * N_BIG + 2:3 * N_BIG + 2]
        send_sems, recv_sems, fsend_sems, frecv_sems, cw_send, cw_recv, local_sems = refs[3 * N_BIG + 2:]
        x, y, c = _mesh_pos()
        j0 = 2 * x + y
        chips = _other_chips(x, y)

        for w in range(N_BIG):
            def cast(i, carry, w=w):
                rows = pl.ds(pl.multiple_of(i * CAST_ROWS, CAST_ROWS), CAST_ROWS)
                stage[w][rows, :] = in_refs[w][rows, :].astype(BF16)
                return carry
            lax.fori_loop(0, SHARD_SHAPES[w][0] // CAST_ROWS, cast, 0)

        local = [pltpu.make_async_copy(stage[w], out_refs[w].at[j0], local_sems.at[w]) for w in range(N_BIG)]
        local.append(pltpu.make_async_copy(cw_ref, cwo_ref.at[j0], local_sems.at[N_BIG]))
        for cp in local:
            cp.start()

        def ici(w, k, block):
            chip = chips[k]
            return pltpu.make_async_remote_copy(
                src_ref=stage[w].at[_half_rows(w, c), :], dst_ref=out_refs[w].at[block, _half_rows(w, c), :],
                send_sem=send_sems.at[w, k], recv_sem=recv_sems.at[w, k],
                device_id=(chip[0], chip[1], c), device_id_type=MESH)

        def d2d(w, k, block, half):
            return pltpu.make_async_remote_copy(
                src_ref=out_refs[w].at[block, _half_rows(w, half), :],
                dst_ref=out_refs[w].at[block, _half_rows(w, half), :],
                send_sem=fsend_sems.at[w, k], recv_sem=frecv_sems.at[w, k],
                device_id=(x, y, 1 - c), device_id_type=MESH)

        def conv(k, block):
            chip = chips[k]
            return pltpu.make_async_remote_copy(
                src_ref=cw_ref, dst_ref=cwo_ref.at[block], send_sem=cw_send.at[k], recv_sem=cw_recv.at[k],
                device_id=(chip[0], chip[1], c), device_id_type=MESH)

        sent = [ici(w, k, j0) for w in full for k in range(3)] + [conv(k, j0) for k in range(3)]
        for cp in sent:
            cp.start()
        for k, chip in enumerate(chips):
            jk = 2 * chip[0] + chip[1]
            for w in full:
                ici(w, k, jk).wait_recv()
                fw = d2d(w, k, jk, c)
                fw.start()
                sent.append(fw)
        for k, chip in enumerate(chips):
            jk = 2 * chip[0] + chip[1]
            for w in full:
                d2d(w, k, jk, 1 - c).wait_recv()
            conv(k, jk).wait_recv()
        for cp in sent:
            cp.wait_send()
        for cp in local:
            cp.wait()

    out_shape = [jax.ShapeDtypeStruct((N_SHARD,) + shp, BF16) for shp in SHARD_SHAPES]
    out_shape.append(jax.ShapeDtypeStruct((N_SHARD,) + CW_PAD, F32))
    return pl.pallas_call(
        body, name="gather_weights",
        in_specs=[_whole()] * (N_BIG + 1),
        out_specs=[HBM_SPEC] * (N_BIG + 1),
        out_shape=out_shape,
        scratch_shapes=[pltpu.VMEM(shp, BF16) for shp in SHARD_SHAPES] + [
            pltpu.SemaphoreType.DMA((N_BIG, 3)), pltpu.SemaphoreType.DMA((N_BIG, 3)),
            pltpu.SemaphoreType.DMA((N_BIG, 3)), pltpu.SemaphoreType.DMA((N_BIG, 3)),
            pltpu.SemaphoreType.DMA((3,)), pltpu.SemaphoreType.DMA((3,)),
            pltpu.SemaphoreType.DMA((N_BIG + 1,))],
        compiler_params=pltpu.CompilerParams(vmem_limit_bytes=VMEM_LIMIT),
    )(*shards, cw8)


def _gather_rider(arrays, ops):
    ws = sorted(arrays)

    def make(inplace, srcs, lands, send_sems, recv_sems):
        del srcs, lands
        x, y, c = _mesh_pos()
        j0 = 2 * x + y
        chips = _other_chips(x, y)
        starts, waits = [], []
        for n, (kind, w, (r0, nr)) in enumerate(ops):
            ref = inplace[ws.index(w)]
            hr = SHARD_SHAPES[w][0] // 2
            rows = lambda core: pl.ds(pl.multiple_of(core * hr + r0, 16), nr)
            for k, chip in enumerate(chips):
                jk = 2 * chip[0] + chip[1]
                if kind == "ici":
                    src, to, landing = ref.at[j0, rows(c), :], (chip[0], chip[1], c), ref.at[jk, rows(c), :]
                else:
                    src, to, landing = ref.at[jk, rows(c), :], (x, y, 1 - c), ref.at[jk, rows(1 - c), :]
                sems = dict(send_sem=send_sems.at[3 * n + k], recv_sem=recv_sems.at[3 * n + k],
                            device_id=to, device_id_type=MESH)
                send = pltpu.make_async_remote_copy(src_ref=src, dst_ref=src, **sems)
                arrival = pltpu.make_async_remote_copy(src_ref=src, dst_ref=landing, **sems)
                starts.append(send)
                waits += [arrival.wait_recv, send.wait_send]
        return starts, waits

    return _Rider([arrays[w] for w in ws], [], [], 3 * len(ops), make)


def _whole_half(w):
    return (0, SHARD_SHAPES[w][0] // 2)


def _pair_rider(ws, g16s):
    def make(inplace, srcs, lands, send_sems, recv_sems):
        del inplace
        x, y, c = _mesh_pos()
        copies = [pltpu.make_async_remote_copy(
            src_ref=srcs[i].at[:, _half_rows(w, 1 - c), :], dst_ref=lands[i],
            send_sem=send_sems.at[i], recv_sem=recv_sems.at[i], device_id=(x, y, 1 - c), device_id_type=MESH)
            for i, w in enumerate(ws)]
        return copies, [cp.wait for cp in copies]

    lands = [jax.ShapeDtypeStruct((N_SHARD, SHARD_SHAPES[w][0] // 2, SHARD_SHAPES[w][1]), BF16) for w in ws]
    return _Rider([], g16s, lands, len(ws), make)


def _chip_rider(ws, p16s, rows=None, landing=None):
    def make(inplace, srcs, lands, send_sems, recv_sems):
        x, y, c = _mesh_pos()
        dsts = inplace if landing is not None else lands
        copies = []
        for i, w in enumerate(ws):
            r0, nr = rows if rows is not None else _whole_half(w)
            for k, chip in enumerate(_other_chips(x, y)):
                copies.append(pltpu.make_async_remote_copy(
                    src_ref=srcs[i].at[2 * chip[0] + chip[1], pl.ds(r0, nr), :],
                    dst_ref=dsts[i].at[k, pl.ds(r0, nr), :],
                    send_sem=send_sems.at[3 * i + k], recv_sem=recv_sems.at[3 * i + k],
                    device_id=(chip[0], chip[1], c), device_id_type=MESH))
        return copies, [cp.wait for cp in copies]

    lands = [jax.ShapeDtypeStruct((3, SHARD_SHAPES[w][0] // 2, SHARD_SHAPES[w][1]), BF16) for w in ws]
    if landing is not None:
        return _Rider(landing, p16s, [], 3 * len(ws), make)
    return _Rider([], p16s, lands, 3 * len(ws), make)


def _final_rider(halves):
    def make(inplace, srcs, lands, send_sems, recv_sems):
        del inplace
        x, y, c = _mesh_pos()
        copies = [pltpu.make_async_remote_copy(
            src_ref=srcs[i], dst_ref=lands[i], send_sem=send_sems.at[i], recv_sem=recv_sems.at[i],
            device_id=(x, y, 1 - c), device_id_type=MESH) for i in range(len(halves))]
        return copies, [cp.wait for cp in copies]

    return _Rider([], halves, [jax.ShapeDtypeStruct(h.shape, h.dtype) for h in halves], len(halves), make)


def _comm_only(name, riders):
    _, res = _call(lambda: None, name=name, grid=(), in_specs=[], out_specs=[], out_shape=[], operands=(),
                   riders=riders)
    return res


def _pair_sum(pos, ws, g32s, recvs):
    n = len(ws)

    def body(pos_ref, *refs):
        del pos_ref
        g_refs, r_refs = refs[:n], refs[n:2 * n]
        p32_refs, p16_refs = refs[2 * n:3 * n], refs[3 * n:]
        for i in range(n):
            tot = g_refs[i][...] + r_refs[i][...].astype(F32)
            p32_refs[i][...] = tot
            p16_refs[i][...] = tot.astype(BF16)

    halves = [(SHARD_SHAPES[w][0] // 2, SHARD_SHAPES[w][1]) for w in ws]
    own = [pl.BlockSpec((None, None) + h, lambda j, pos_ref: (j, pos_ref[0], 0, 0)) for h in halves]
    blk = [pl.BlockSpec((None,) + h, lambda j, pos_ref: (j, 0, 0)) for h in halves]
    g4 = [g.reshape((N_SHARD, 2) + h) for g, h in zip(g32s, halves)]
    outs = pl.pallas_call(
        body, name="pair_sum_" + "_".join(str(w) for w in ws),
        grid_spec=pltpu.PrefetchScalarGridSpec(
            num_scalar_prefetch=1, grid=(N_SHARD,), in_specs=own + blk, out_specs=blk + blk),
        out_shape=[jax.ShapeDtypeStruct((N_SHARD,) + h, F32) for h in halves]
        + [jax.ShapeDtypeStruct((N_SHARD,) + h, BF16) for h in halves],
        compiler_params=_params(("arbitrary",)),
    )(pos, *g4, *recvs)
    return outs[:n], outs[n:]


def _chip_sum(pos, p32s, recvs):
    parts = 2

    def body(pos_ref, *refs):
        del pos_ref
        p_refs, r_refs, f_refs = refs[:N_BIG], refs[N_BIG:2 * N_BIG], refs[2 * N_BIG:]
        for w in range(N_BIG):
            f_refs[w][...] = ((p_refs[w][...] + r_refs[w][0].astype(F32)) + r_refs[w][1].astype(F32)) \
                + r_refs[w][2].astype(F32)

    quarters = [(r // 2 // parts, cc) for r, cc in SHARD_SHAPES]
    own = [pl.BlockSpec((None,) + qt, lambda i, pos_ref: (pos_ref[1], i, 0)) for qt in quarters]
    rcv = [pl.BlockSpec((3,) + qt, lambda i, pos_ref: (0, i, 0)) for qt in quarters]
    out = [pl.BlockSpec(qt, lambda i, pos_ref: (i, 0)) for qt in quarters]
    return pl.pallas_call(
        body, name="chip_sum",
        grid_spec=pltpu.PrefetchScalarGridSpec(
            num_scalar_prefetch=1, grid=(parts,), in_specs=own + rcv, out_specs=out),
        out_shape=[jax.ShapeDtypeStruct((r // 2, cc), F32) for r, cc in SHARD_SHAPES],
        compiler_params=_params(("arbitrary",)),
    )(pos, *p32s, *recvs)


def _adamw(w, g, m, v):
    m_new = ADAM_B1 * m + (1.0 - ADAM_B1) * g
    v_new = ADAM_B2 * v + (1.0 - ADAM_B2) * (g * g)
    m_hat = m_new / (1.0 - ADAM_B1 ** ADAM_STEP)
    v_hat = v_new / (1.0 - ADAM_B2 ** ADAM_STEP)
    delta = -ADAM_LR * (m_hat / (jnp.sqrt(v_hat) + ADAM_EPS) + ADAM_WD * w)
    return delta, m_new, v_new


def _adam_big(pos, mine, theirs, ws, ms, vs):
    nb = 4

    def body(pos_ref, *refs):
        hf = pl.program_id(0)
        groups = [refs[i * N_BIG:(i + 1) * N_BIG] for i in range(9)]
        f_refs, t_refs, w_refs, m_refs, v_refs, go_refs, do_refs, mo_refs, vo_refs = groups
        for w in range(N_BIG):
            g = jnp.where(hf == pos_ref[0], f_refs[w][...], t_refs[w][...])
            delta, m_new, v_new = _adamw(w_refs[w][...], g, m_refs[w][...], v_refs[w][...])
            go_refs[w][...] = g
            do_refs[w][...] = delta
            mo_refs[w][...] = m_new
            vo_refs[w][...] = v_new

    blocks = [(r // 2 // nb, cc) for r, cc in SHARD_SHAPES]
    half = [pl.BlockSpec(b, lambda hf, i, pos_ref: (i, 0)) for b in blocks]
    full = [pl.BlockSpec((None,) + b, lambda hf, i, pos_ref: (0, hf * nb + i, 0)) for b in blocks]
    shapes = [jax.ShapeDtypeStruct((1,) + shp, F32) for shp in SHARD_SHAPES]
    outs = pl.pallas_call(
        body, name="adam_big",
        grid_spec=pltpu.PrefetchScalarGridSpec(
            num_scalar_prefetch=1, grid=(2, nb), in_specs=half + half + full * 3, out_specs=full * 4),
        out_shape=shapes * 4,
        compiler_params=_params(("arbitrary", "arbitrary")),
    )(pos, *mine, *theirs, *ws, *ms, *vs)
    return [outs[i * N_BIG:(i + 1) * N_BIG] for i in range(4)]


SMALL_ROWS = 8
ROW_CONV_B, ROW_POOL_SCALE, ROW_LN1_G, ROW_LN1_B, ROW_LN2_G, ROW_LN2_B, ROW_LOSS = range(7)
SMALL_VECS = ((ROW_CONV_B, D_FF), (ROW_POOL_SCALE, POOL_W), (ROW_LN1_G, D_MODEL), (ROW_LN1_B, D_MODEL),
              (ROW_LN2_G, D_MODEL), (ROW_LN2_B, D_MODEL))


def _small_update(loss, dwp, dcw4, vec_grads, wp, cwp, vec_ws, m_wp, m_cwp, vec_ms, v_wp, v_cwp, vec_vs,
                  riders=()):
    nv = len(SMALL_VECS)

    def body(*refs):
        loss_ref, dwp_ref, dcw_ref = refs[0:3]
        gvec = refs[3:3 + nv]
        o = 3 + nv
        wp_ref, cw_ref = refs[o:o + 2]
        wvec = refs[o + 2:o + 2 + nv]
        o += 2 + nv
        mwp_ref, mcw_ref = refs[o:o + 2]
        mvec = refs[o + 2:o + 2 + nv]
        o += 2 + nv
        vwp_ref, vcw_ref = refs[o:o + 2]
        vvec = refs[o + 2:o + 2 + nv]
        o += 2 + nv
        loss_out = refs[o]
        outs = refs[o + 1:o + 1 + 4 * (2 + nv)]
        o += 1 + 4 * (2 + nv)
        (vec_scr, sib_a, sib_b, sib_c, all_a, all_b, all_c,
         send1, recv1, send2, recv2) = refs[o:]
        x, y, c = _mesh_pos()
        j0 = 2 * x + y
        chips = _other_chips(x, y)

        vec_scr[...] = jnp.zeros_like(vec_scr)
        for (row, n), ref in zip(SMALL_VECS, gvec):
            vec_scr[row:row + 1, 0:n] = ref[...]
        vec_scr[ROW_LOSS:ROW_LOSS + 1, 0:HEAD_DIM] = jnp.broadcast_to(loss_ref[...], (1, HEAD_DIM))

        mine = (dwp_ref, vec_scr, dcw_ref)
        sib = (sib_a, sib_b, sib_c)
        every = (all_a, all_b, all_c)
        first = [pltpu.make_async_remote_copy(
            src_ref=mine[b], dst_ref=sib[b], send_sem=send1.at[b], recv_sem=recv1.at[b],
            device_id=(x, y, 1 - c), device_id_type=MESH) for b in range(3)]
        for cp in first:
            cp.start()
        for cp in first:
            cp.wait()
        for b in range(3):
            every[b][j0] = mine[b][...] + sib[b][...]

        def ici(b, k, block):
            chip = chips[k]
            return pltpu.make_async_remote_copy(
                src_ref=every[b].at[block], dst_ref=every[b].at[block],
                send_sem=send2.at[b, k], recv_sem=recv2.at[b, k],
                device_id=(chip[0], chip[1], c), device_id_type=MESH)

        second = [ici(b, k, j0) for b in range(3) for k in range(3)]
        for cp in second:
            cp.start()
        for k, chip in enumerate(chips):
            for b in range(3):
                ici(b, k, 2 * chip[0] + chip[1]).wait_recv()
        for cp in second:
            cp.wait_send()

        tot_a = ((all_a[0] + all_a[1]) + all_a[2]) + all_a[3]
        tot_b = ((all_b[0] + all_b[1]) + all_b[2]) + all_b[3]
        all_c[0] = ((all_c[0] + all_c[1]) + all_c[2]) + all_c[3]
        tot_c = all_c[0, j0]
        loss_out[...] = tot_b[ROW_LOSS:ROW_LOSS + 1, 0:1]

        grads = [tot_a, tot_c] + [tot_b[row:row + 1, 0:n] for row, n in SMALL_VECS]
        w_all = [wp_ref, cw_ref] + list(wvec)
        m_all = [mwp_ref, mcw_ref] + list(mvec)
        v_all = [vwp_ref, vcw_ref] + list(vvec)
        np_ = 2 + nv
        for p in range(np_):
            g = grads[p]
            delta, m_new, v_new = _adamw(w_all[p][...], g, m_all[p][...], v_all[p][...])
            outs[p][...] = g
            outs[np_ + p][...] = delta
            outs[2 * np_ + p][...] = m_new
            outs[3 * np_ + p][...] = v_new

    pshapes = [wp.shape, CW_PAD] + [wv.shape for wv in vec_ws]
    out_shape = [jax.ShapeDtypeStruct((1, 1), F32)] + [jax.ShapeDtypeStruct(s, F32) for s in pshapes] * 4
    a_shape = dwp.shape
    b_shape = (SMALL_ROWS, D_FF)
    c_shape = dcw4.shape
    n_in = 3 + nv + 3 * (2 + nv)
    outs, rider_res = _call(
        body, name="small_update", grid=(),
        in_specs=[_whole()] * n_in, out_specs=[_whole()] * len(out_shape), out_shape=out_shape,
        scratch_shapes=[pltpu.VMEM(b_shape, F32),
                        pltpu.VMEM(a_shape, F32), pltpu.VMEM(b_shape, F32), pltpu.VMEM(c_shape, F32),
                        pltpu.VMEM((N_SHARD,) + a_shape, F32), pltpu.VMEM((N_SHARD,) + b_shape, F32),
                        pltpu.VMEM((N_SHARD,) + c_shape, F32),
                        pltpu.SemaphoreType.DMA((3,)), pltpu.SemaphoreType.DMA((3,)),
                        pltpu.SemaphoreType.DMA((3, 3)), pltpu.SemaphoreType.DMA((3, 3))],
        operands=(loss, dwp, dcw4, *vec_grads, wp, cwp, *vec_ws, m_wp, m_cwp, *vec_ms, v_wp, v_cwp, *vec_vs),
        riders=riders,
    )
    np_ = 2 + nv
    return outs[0], [outs[1 + i * np_:1 + (i + 1) * np_] for i in range(4)], rider_res


def _pad_cw(a):
    pad = [(0, 0)] * (a.ndim - 2) + [(0, CW_PAD[0] - a.shape[-2]), (0, CW_PAD[1] - a.shape[-1])]
    return jnp.pad(a, pad)


def kernel(x, w_in, w_pool, pool_scale, w_out, ln1_g, ln1_b, w_up, conv_w, conv_b, w_down, ln2_g, ln2_b, loss_target, m_w_in, m_w_pool, m_pool_scale, m_w_out, m_ln1_g, m_ln1_b, m_w_up, m_conv_w, m_conv_b, m_w_down, m_ln2_g, m_ln2_b, v_w_in, v_w_pool, v_pool_scale, v_w_out, v_ln1_g, v_ln1_b, v_w_up, v_conv_w, v_conv_b, v_w_down, v_ln2_g, v_ln2_b):
    pos = jnp.stack([lax.axis_index("c"), 2 * lax.axis_index("x") + lax.axis_index("y")]).astype(jnp.int32)
    order = ("w_in", "w_out", "w_up", "w_down")
    w_in_i, w_out_i, w_up_i, w_down_i = range(N_BIG)

    gathered = _gather_weights([w_in[0], w_out[0], w_up[0], w_down[0]], _pad_cw(conv_w[0]), (w_in_i,))
    cw_full = jnp.transpose(gathered[N_BIG][:, 0:3, 0:DOWN_SH], (1, 0, 2)).reshape(3, D_FF)
    up_half = SHARD_SHAPES[w_up_i][0] // 2
    up_a, up_b = (0, up_half // 2), (up_half // 2, up_half // 2)

    class MeshComm:
        def __init__(self):
            self.w = {i: gathered[i] for i in range(N_BIG)}
            self.g32, self.g16, self.recv_a, self.p32, self.p16, self.recv_b = {}, {}, {}, {}, {}, {}
            self.up_complete = False

        def weight(self, name):
            i = order.index(name)
            if name == "w_up" and not self.up_complete:
                (arrs, _), = _comm_only("gather_up_last", [_gather_rider({i: self.w[i]}, [("d2d", i, up_b)])])
                self.w[i], self.up_complete = arrs[0], True
            full = self.w[i]
            return full.reshape(-1, full.shape[-1]) if name in ("w_out", "w_down") else full

        def _gather(self, ws, ops):
            return _gather_rider({w: self.w[w] for w in ws}, ops), ("w", ws)

        def _pair(self, ws):
            return _pair_rider(ws, [self.g16[w] for w in ws]), ("recv_a", ws)

        def _chip(self, ws, rows=None, resume=False):
            landing = [self.recv_b[w] for w in ws] if resume else None
            return _chip_rider(ws, [self.p16[w] for w in ws], rows, landing), ("recv_b", ws)

        def plan(self, call):
            if call == "proj_pool":
                return [self._gather([w_out_i, w_down_i], [("ici", w_out_i, _whole_half(w_out_i)),
                                                          ("ici", w_down_i, _whole_half(w_down_i))])]
            if call == "retention_fwd":
                return [self._gather([w_out_i, w_up_i, w_down_i],
                                     [("d2d", w_out_i, _whole_half(w_out_i)),
                                      ("d2d", w_down_i, _whole_half(w_down_i)), ("ici", w_up_i, up_a)])]
            if call == "outproj_ln1":
                return [self._gather([w_up_i], [("ici", w_up_i, up_b), ("d2d", w_up_i, up_a)])]
            if call == "wgrad_down":
                return [self._pair([w_up_i, w_out_i])]
            if call == "mix_bwd":
                return [self._chip([w_up_i], up_a)]
            if call == "retention_bwd":
                return [self._chip([w_up_i], up_b, resume=True), self._pair([w_down_i])]
            if call == "wgrad_in":
                return [self._chip([w_down_i])]
            if call == "small_update":
                return [self._chip([w_in_i, w_out_i])]
            return []

        def riders(self, call):
            self.pending = self.plan(call)
            return [r for r, _ in self.pending]

        def landed(self, call, results):
            for (_, (slot, ws)), (inplace, lands) in zip(self.pending, results):
                for w, arr in zip(ws, inplace if len(inplace) else lands):
                    getattr(self, slot)[w] = arr
            if call == "wgrad_down":
                self._sum([w_up_i, w_out_i])
            if call == "retention_bwd":
                self._sum([w_down_i])

        def _sum(self, ws):
            p32s, p16s = _pair_sum(pos, ws, [self.g32[w] for w in ws], [self.recv_a[w] for w in ws])
            for w, p32, p16 in zip(ws, p32s, p16s):
                self.p32[w], self.p16[w] = p32, p16

        def gradient(self, name, g32, g16):
            w = order.index(name)
            shape = (N_SHARD,) + SHARD_SHAPES[w]
            self.g32[w], self.g16[w] = g32.reshape(shape), g16.reshape(shape)
            if name == "w_in":
                (_, lands), = _comm_only("pair_exchange_in", [self._pair([w])[0]])
                self.recv_a[w] = lands[0]
                self._sum([w])

    comm = MeshComm()
    loss, grad_x, small = _local_step(x[0], loss_target[0], cw_full, conv_b, w_pool[0], pool_scale,
                                      ln1_g, ln1_b, ln2_g, ln2_b, comm)

    dcw4 = _pad_cw(jnp.transpose(small["conv_w"].reshape(3, N_SHARD, DOWN_SH), (1, 0, 2)))
    vec_names = ("conv_b", "pool_scale", "ln1_g", "ln1_b", "ln2_g", "ln2_b")
    given = dict(w_pool=w_pool, pool_scale=pool_scale, ln1_g=ln1_g, ln1_b=ln1_b, conv_w=conv_w, conv_b=conv_b,
                 ln2_g=ln2_g, ln2_b=ln2_b)
    given_m = dict(w_pool=m_w_pool, pool_scale=m_pool_scale, ln1_g=m_ln1_g, ln1_b=m_ln1_b, conv_w=m_conv_w,
                   conv_b=m_conv_b, ln2_g=m_ln2_g, ln2_b=m_ln2_b)
    given_v = dict(w_pool=v_w_pool, pool_scale=v_pool_scale, ln1_g=v_ln1_g, ln1_b=v_ln1_b, conv_w=v_conv_w,
                   conv_b=v_conv_b, ln2_g=v_ln2_g, ln2_b=v_ln2_b)
    args = []
    for src in (given, given_m, given_v):
        args += [src["w_pool"][0], _pad_cw(src["conv_w"][0]), [src[n] for n in vec_names]]
    loss_tot, small_out, landed = _small_update(loss, small["w_pool"], dcw4, [small[n] for n in vec_names], *args,
                                                riders=comm.riders("small_update"))
    comm.landed("small_update", landed)

    every = range(N_BIG)
    mine = _chip_sum(pos, [comm.p32[w] for w in every], [comm.recv_b[w] for w in every])
    (_, theirs), = _comm_only("pair_exchange_f32", [_final_rider(mine)])
    big_out = _adam_big(pos, mine, theirs, [w_in, w_out, w_up, w_down], [m_w_in, m_w_out, m_w_up, m_w_down],
                        [v_w_in, v_w_out, v_w_up, v_w_down])

    names = ("w_in", "w_pool", "pool_scale", "w_out", "ln1_g", "ln1_b", "w_up", "conv_w", "conv_b", "w_down",
             "ln2_g", "ln2_b")
    small_names = ("w_pool", "conv_w") + vec_names
    result = [loss_tot.reshape(()), grad_x[None]]
    for kind in range(4):
        for n in names:
            if n in order:
                result.append(big_out[kind][order.index(n)])
            else:
                val = small_out[kind][small_names.index(n)]
                if n == "conv_w":
                    val = val[0:3, 0:DOWN_SH][None]
                elif n == "w_pool":
                    val = val[None]
                result.append(val)
    return tuple(result)
```

```python
import functools
import math

import numpy as np
import jax
import jax.numpy as jnp
from jax import lax
from jax.experimental import pallas as pl
from jax.experimental.pallas import tpu as pltpu

F32 = jnp.float32
BF16 = jnp.bfloat16

D_MODEL = 1024
HEADS = 4
HEAD_DIM = 128
RET_W = HEADS * HEAD_DIM
POOL_WINDOWS = (2, 4, 8, 16)
POOL_W = 512
IN_W = 4 * RET_W + POOL_W
D_FF = 2816
N_SHARD = 4
IN_SH = IN_W // N_SHARD
UP_SH = 2 * D_FF // N_SHARD
DOWN_SH = D_FF // N_SHARD
OUT_SH = D_MODEL // N_SHARD
ROPE_BASE = 10000.0
LN_EPS = 1e-5
RMS_EPS = 1e-6
ALPHA = 2.0 ** 0.25
K_SCALE = HEAD_DIM ** -0.5
SUPER = 256
CHUNK = 64
POOL_HALO = 16
CONV_HALO = 8
FFN_STRIP = 128
LN_ROWS = 32

ADAM_LR = 0.001
ADAM_B1 = 0.9
ADAM_B2 = 0.999
ADAM_EPS = 1e-08
ADAM_WD = 0.01
ADAM_STEP = 10

MESH = pl.DeviceIdType.MESH
VMEM_LIMIT = 56 * 1024 * 1024


def _dot(a, b):
    return jnp.dot(a, b, preferred_element_type=F32)


def _dot_nt(a, b):
    return lax.dot_general(a, b, (((1,), (1,)), ((), ())), preferred_element_type=F32)


def _dot_tn(a, b):
    return lax.dot_general(a, b, (((0,), (0,)), ((), ())), preferred_element_type=F32)


def _sigmoid(x):
    return 1.0 / (1.0 + jnp.exp(-x))


def _params(sem):
    return pltpu.CompilerParams(dimension_semantics=sem, vmem_limit_bytes=VMEM_LIMIT)


def _whole():
    return pl.BlockSpec(memory_space=pltpu.VMEM)


HBM_SPEC = pl.BlockSpec(memory_space=pl.ANY)


class _Rider:
    def __init__(self, inplace, srcs, lands, n_copies, make):
        self.inplace, self.srcs, self.lands, self.n_copies, self.make = list(inplace), list(srcs), list(lands), n_copies, make


def _call(body, *, name, grid, in_specs, out_specs, out_shape, operands, scratch_shapes=(), sem=(),
          aliases=None, riders=()):
    n_in, n_out, n_scr = len(in_specs), len(out_shape), len(scratch_shapes)
    in_specs, out_specs, out_shape = list(in_specs), list(out_specs), list(out_shape)
    operands, scratch_shapes, aliases = list(operands), list(scratch_shapes), dict(aliases or {})
    for r in riders:
        for a in r.inplace:
            aliases[len(in_specs)] = len(out_shape)
            in_specs.append(HBM_SPEC)
            operands.append(a)
            out_specs.append(HBM_SPEC)
            out_shape.append(jax.ShapeDtypeStruct(a.shape, a.dtype))
        for a in r.srcs:
            in_specs.append(HBM_SPEC)
            operands.append(a)
        for shp in r.lands:
            out_specs.append(HBM_SPEC)
            out_shape.append(shp)
        scratch_shapes += [pltpu.SemaphoreType.DMA((r.n_copies,)), pltpu.SemaphoreType.DMA((r.n_copies,))]

    def full(*refs):
        ins = refs[:n_in]
        at = n_in
        r_srcs = []
        for r in riders:
            at += len(r.inplace)
            r_srcs.append(refs[at:at + len(r.srcs)])
            at += len(r.srcs)
        outs = refs[at:at + n_out]
        at += n_out
        r_outs = []
        for r in riders:
            r_outs.append((refs[at:at + len(r.inplace)], refs[at + len(r.inplace):at + len(r.inplace) + len(r.lands)]))
            at += len(r.inplace) + len(r.lands)
        scr = refs[at:at + n_scr]
        at += n_scr
        r_sems = [refs[at + 2 * i:at + 2 * i + 2] for i in range(len(riders))]

        def copies():
            return [r.make(r_outs[i][0], r_srcs[i], r_outs[i][1], r_sems[i][0], r_sems[i][1])
                    for i, r in enumerate(riders)]

        def start():
            for starts, _ in copies():
                for cp in starts:
                    cp.start()

        def finish():
            for _, waits in copies():
                for wait in waits:
                    wait()

        if riders and grid:
            first = functools.reduce(jnp.logical_and, [pl.program_id(d) == 0 for d in range(len(grid))])
            last = functools.reduce(jnp.logical_and, [pl.program_id(d) == grid[d] - 1 for d in range(len(grid))])
            pl.when(first)(start)
            body(*ins, *outs, *scr)
            pl.when(last)(finish)
        else:
            if riders:
                start()
            body(*ins, *outs, *scr)
            if riders:
                finish()

    params = _params(sem) if grid else pltpu.CompilerParams(vmem_limit_bytes=VMEM_LIMIT)
    res = pl.pallas_call(
        full, name=name, grid=grid, in_specs=in_specs, out_specs=out_specs, out_shape=out_shape,
        scratch_shapes=scratch_shapes, input_output_aliases=aliases, compiler_params=params,
    )(*operands)
    outs, at, rider_res = res[:n_out], n_out, []
    for r in riders:
        rider_res.append((res[at:at + len(r.inplace)], res[at + len(r.inplace):at + len(r.inplace) + len(r.lands)]))
        at += len(r.inplace) + len(r.lands)
    return list(outs), rider_res


def _gammas():
    return [1.0 - 2.0 ** (-5.0 - h) for h in range(HEADS)]


def _decay_tables():
    idx = np.arange(SUPER)
    dist = np.abs(idx[:, None] - idx[None, :]).astype(np.float64)
    visible = (idx[None, :] // CHUNK) <= (idx[:, None] // CHUNK)
    mask = np.stack([np.where(visible, g ** dist, 0.0) for g in _gammas()])
    qd = np.concatenate([np.repeat((g ** (idx + 1.0))[:, None], HEAD_DIM, 1) for g in _gammas()], 1)
    kd = np.concatenate([np.repeat((g ** (SUPER - 1.0 - idx))[:, None], HEAD_DIM, 1) for g in _gammas()], 1)
    return (jnp.asarray(mask, F32), jnp.asarray(qd, F32), jnp.asarray(kd, F32))


def _rope_tables(s):
    inv_freq = ROPE_BASE ** (-np.arange(0, HEAD_DIM, 2, dtype=np.float64) / HEAD_DIM)
    ang = np.arange(s, dtype=np.float64)[:, None] * inv_freq[None, :]
    cos, sin = np.cos(ang), np.sin(ang)
    return (jnp.asarray(np.concatenate([cos, cos], 1), F32),
            jnp.asarray(np.concatenate([-sin, sin], 1), F32))


def _rope(t, cosf, sinf):
    return t * cosf + pltpu.roll(t, HEAD_DIM // 2, 1) * sinf


def _rope_t(t, cosf, sinf):
    return t * cosf - pltpu.roll(t, HEAD_DIM // 2, 1) * sinf


def _layernorm_fwd(z):
    mu = jnp.mean(z, axis=-1, keepdims=True)
    zc = z - mu
    var = jnp.mean(zc * zc, axis=-1, keepdims=True)
    rstd = lax.rsqrt(var + LN_EPS)
    return zc * rstd, rstd


def _layernorm_bwd(dy, xhat, rstd, gain):
    dxh = dy * gain
    m1 = jnp.mean(dxh, axis=-1, keepdims=True)
    m2 = jnp.mean(dxh * xhat, axis=-1, keepdims=True)
    return rstd * (dxh - m1 - xhat * m2)


def _proj_pool(x, win4, cosf, sinf, wpool, pscale, ts, riders=()):
    s = x.shape[0]
    nt = s // ts

    def body(x_ref, w_ref, cos_ref, sin_ref, wp_ref, ps_ref,
             xb_ref, q_ref, k_ref, v_ref, g_ref, pooled_ref, cat_ref, proj_scr, pext_scr):
        i = pl.program_id(0)
        xb = x_ref[...].astype(BF16)
        xb_ref[...] = xb
        for j in range(N_SHARD):
            proj_scr[:, j * IN_SH:(j + 1) * IN_SH] = _dot(xb, w_ref[j])
        cosf_t = cos_ref[...]
        sinf_t = sin_ref[...]
        for h in range(HEADS):
            lo = h * HEAD_DIM
            q_ref[:, lo:lo + HEAD_DIM] = _rope(proj_scr[:, lo:lo + HEAD_DIM], cosf_t, sinf_t).astype(BF16)
            kk = _rope(proj_scr[:, RET_W + lo:RET_W + lo + HEAD_DIM], cosf_t, sinf_t) * K_SCALE
            k_ref[:, lo:lo + HEAD_DIM] = kk.astype(BF16)
        v_ref[...] = proj_scr[:, 2 * RET_W:3 * RET_W].astype(BF16)
        g_ref[...] = proj_scr[:, 3 * RET_W:4 * RET_W]

        @pl.when(i == 0)
        def _():
            pext_scr[0:POOL_HALO, :] = jnp.zeros((POOL_HALO, POOL_W), F32)

        pext_scr[POOL_HALO:POOL_HALO + ts, :] = proj_scr[:, 4 * RET_W:IN_W]
        pos = (i * ts + lax.broadcasted_iota(jnp.int32, (ts, 1), 0) + 1).astype(F32)
        for gi, w in enumerate(POOL_WINDOWS):
            lo = gi * HEAD_DIM
            ext = pext_scr[:, lo:lo + HEAD_DIM]
            acc = ext
            shift = 1
            while shift < w:
                acc = acc + pltpu.roll(acc, shift, 0)
                shift *= 2
            tok = ext[POOL_HALO:POOL_HALO + ts]
            pooled = acc[POOL_HALO:POOL_HALO + ts] / jnp.minimum(pos, float(w)) - tok
            pooled_b = pooled.astype(BF16)
            pooled_ref[:, lo:lo + HEAD_DIM] = pooled_b
            lin = _dot(pooled_b, wp_ref[gi])
            cat_ref[:, lo:lo + HEAD_DIM] = (lin * ps_ref[:, lo:lo + HEAD_DIM]).astype(BF16)
        pext_scr[0:POOL_HALO, :] = pext_scr[ts:ts + POOL_HALO, :]

    tile = lambda w: pl.BlockSpec((ts, w), lambda i: (i, 0))
    return _call(
        body, name="proj_pool", grid=(nt,),
        in_specs=[tile(D_MODEL), _whole(), tile(HEAD_DIM), tile(HEAD_DIM), _whole(), _whole()],
        out_specs=[tile(D_MODEL), tile(RET_W), tile(RET_W), tile(RET_W), tile(RET_W), tile(POOL_W),
                   pl.BlockSpec((ts, POOL_W), lambda i: (i, 1))],
        out_shape=[jax.ShapeDtypeStruct((s, D_MODEL), BF16), jax.ShapeDtypeStruct((s, RET_W), BF16),
                   jax.ShapeDtypeStruct((s, RET_W), BF16), jax.ShapeDtypeStruct((s, RET_W), BF16),
                   jax.ShapeDtypeStruct((s, RET_W), F32), jax.ShapeDtypeStruct((s, POOL_W), BF16),
                   jax.ShapeDtypeStruct((s, 2 * RET_W), BF16)],
        scratch_shapes=[pltpu.VMEM((ts, IN_W), F32), pltpu.VMEM((ts + POOL_HALO, POOL_W), F32)],
        sem=("arbitrary",), operands=(x, win4, cosf, sinf, wpool, pscale), riders=riders,
    )


def _retention_fwd(q, k, v, g, cat, mask, qd, kd, riders=()):
    s = q.shape[0]
    ns = s // SUPER
    cdec = [gm ** float(SUPER) for gm in _gammas()]

    def body(q_ref, k_ref, v_ref, g_ref, cat_in, mask_ref, qd_ref, kd_ref,
             ret_ref, cat_ref, st_ref, state_scr):
        del cat_in
        n = pl.program_id(0)

        @pl.when(n == 0)
        def _():
            state_scr[...] = jnp.zeros_like(state_scr)

        for h in range(HEADS):
            sl = slice(h * HEAD_DIM, (h + 1) * HEAD_DIM)
            qh, kh, vh = q_ref[:, sl], k_ref[:, sl], v_ref[:, sl]
            sc = _dot_nt(qh, kh) * mask_ref[h]
            st = state_scr[h]
            stb = st.astype(BF16)
            st_ref[0, h] = stb
            qdb = (qh.astype(F32) * qd_ref[:, sl]).astype(BF16)
            kdb = (kh.astype(F32) * kd_ref[:, sl]).astype(BF16)
            ret = _dot(sc.astype(BF16), vh) + _dot(qdb, stb)
            state_scr[h] = st * cdec[h] + _dot_tn(kdb, vh)
            ret_ref[:, sl] = ret
            r = lax.rsqrt(jnp.mean(ret * ret, axis=-1, keepdims=True) + RMS_EPS)
            gh = g_ref[:, sl]
            cat_ref[:, sl] = ((ret * r) * (gh * _sigmoid(gh))).astype(BF16)

    tile = pl.BlockSpec((SUPER, RET_W), lambda n: (n, 0))
    return _call(
        body, name="retention_fwd", grid=(ns,),
        in_specs=[tile, tile, tile, tile, HBM_SPEC, _whole(), _whole(), _whole()],
        out_specs=[tile, tile, pl.BlockSpec((1, HEADS, HEAD_DIM, HEAD_DIM), lambda n: (n, 0, 0, 0))],
        out_shape=[jax.ShapeDtypeStruct((s, RET_W), F32), jax.ShapeDtypeStruct((s, 2 * RET_W), BF16),
                   jax.ShapeDtypeStruct((ns, HEADS, HEAD_DIM, HEAD_DIM), BF16)],
        scratch_shapes=[pltpu.VMEM((HEADS, HEAD_DIM, HEAD_DIM), F32)],
        aliases={4: 1}, sem=("arbitrary",), operands=(q, k, v, g, cat, mask, qd, kd), riders=riders,
    )


def _outproj_ln1(x, cat, wout, g1, b1, ts, riders=()):
    s = x.shape[0]

    def body(x_ref, cat_ref, w_ref, g_ref, b_ref, xhat_ref, rstd_ref, h1b_ref):
        z = ALPHA * x_ref[...] + _dot(cat_ref[...], w_ref[...])
        xhat, rstd = _layernorm_fwd(z)
        xhat_ref[...] = xhat
        rstd_ref[...] = rstd
        h1b_ref[...] = (xhat * g_ref[...] + b_ref[...]).astype(BF16)

    tile = lambda w: pl.BlockSpec((ts, w), lambda i: (i, 0))
    return _call(
        body, name="outproj_ln1", grid=(s // ts,),
        in_specs=[tile(D_MODEL), tile(D_MODEL), _whole(), _whole(), _whole()],
        out_specs=[tile(D_MODEL), tile(1), tile(D_MODEL)],
        out_shape=[jax.ShapeDtypeStruct((s, D_MODEL), F32), jax.ShapeDtypeStruct((s, 1), F32),
                   jax.ShapeDtypeStruct((s, D_MODEL), BF16)],
        sem=("arbitrary",), operands=(x, cat, wout, g1, b1), riders=riders,
    )


def _ffn_fwd_loss(xhat1, h1b, target, wup4, wdown, cw, cb, g1, b1, g2, b2, ts):
    s = xhat1.shape[0]

    def body(xhat_ref, h1b_ref, tgt_ref, wup_ref, wdn_ref, cw_ref, cb_ref, g1_ref, b1_ref, g2_ref, b2_ref,
             ub_ref, dz2_ref, dz2b_ref, loss_ref, dg2_ref, db2_ref, val_scr, gext_scr, act_scr, ffn_scr):
        i = pl.program_id(0)

        @pl.when(i == 0)
        def _():
            gext_scr[0:CONV_HALO, :] = jnp.zeros((CONV_HALO, D_FF), F32)
            loss_ref[...] = jnp.zeros_like(loss_ref)
            dg2_ref[...] = jnp.zeros_like(dg2_ref)
            db2_ref[...] = jnp.zeros_like(db2_ref)

        hb = h1b_ref[...]
        for half in range(2):
            lo = half * UP_SH
            gext_scr[CONV_HALO:CONV_HALO + ts, lo:lo + UP_SH] = _dot(hb, wup_ref[2 + half])
            val_scr[:, lo:lo + UP_SH] = _dot(hb, wup_ref[half])
            for c0 in range(lo, lo + UP_SH, FFN_STRIP):
                cols = slice(c0, c0 + FFN_STRIP)
                ext = gext_scr[:, cols]
                gate = ext[CONV_HALO:]
                hc = cb_ref[:, cols] + ((pltpu.roll(ext, 2, 0)[CONV_HALO:] * cw_ref[0:1, cols]
                                         + pltpu.roll(ext, 1, 0)[CONV_HALO:] * cw_ref[1:2, cols])
                                        + gate * cw_ref[2:3, cols])
                val = val_scr[:, cols]
                act_scr[:, cols] = ((hc * _sigmoid(hc)) * val).astype(BF16)
                ub_ref[:, cols] = val.astype(BF16)
                ub_ref[:, D_FF + c0:D_FF + c0 + FFN_STRIP] = gate.astype(BF16)
            part = _dot(act_scr[:, lo:lo + UP_SH], wdn_ref[lo:lo + UP_SH, :])
            if half == 0:
                ffn_scr[...] = part
            else:
                ffn_scr[...] += part
        gext_scr[0:CONV_HALO, :] = gext_scr[ts:ts + CONV_HALO, :]

        loss_acc = jnp.zeros((1, 1), F32)
        dg2_acc = jnp.zeros((1, D_MODEL), F32)
        db2_acc = jnp.zeros((1, D_MODEL), F32)
        for r0 in range(0, ts, LN_ROWS):
            rows = slice(r0, r0 + LN_ROWS)
            h1 = xhat_ref[rows, :] * g1_ref[...] + b1_ref[...]
            xhat2, rstd2 = _layernorm_fwd(ALPHA * h1 + ffn_scr[rows, :])
            diff = (xhat2 * g2_ref[...] + b2_ref[...]) - tgt_ref[rows, :]
            row = jnp.mean(diff * diff, axis=-1, keepdims=True)
            loss_acc = loss_acc + 0.5 * jnp.sum(row, axis=0, keepdims=True)
            dy = diff * (1.0 / D_MODEL)
            dg2_acc = dg2_acc + jnp.sum(dy * xhat2, axis=0, keepdims=True)
            db2_acc = db2_acc + jnp.sum(dy, axis=0, keepdims=True)
            dz2 = _layernorm_bwd(dy, xhat2, rstd2, g2_ref[...])
            dz2_ref[rows, :] = dz2
            dz2b_ref[rows, :] = dz2.astype(BF16)
        loss_ref[...] += loss_acc
        dg2_ref[...] += dg2_acc
        db2_ref[...] += db2_acc

    tile = lambda w: pl.BlockSpec((ts, w), lambda i: (i, 0))
    acc = lambda w: pl.BlockSpec((1, w), lambda i: (0, 0))
    return pl.pallas_call(
        body, name="ffn_fwd_loss", grid=(s // ts,),
        in_specs=[tile(D_MODEL), tile(D_MODEL), tile(D_MODEL)] + [_whole()] * 8,
        out_specs=[tile(2 * D_FF), tile(D_MODEL), tile(D_MODEL), acc(1), acc(D_MODEL), acc(D_MODEL)],
        out_shape=[jax.ShapeDtypeStruct((s, 2 * D_FF), BF16), jax.ShapeDtypeStruct((s, D_MODEL), F32),
                   jax.ShapeDtypeStruct((s, D_MODEL), BF16),
                   jax.ShapeDtypeStruct((1, 1), F32), jax.ShapeDtypeStruct((1, D_MODEL), F32),
                   jax.ShapeDtypeStruct((1, D_MODEL), F32)],
        scratch_shapes=[pltpu.VMEM((ts, D_FF), F32), pltpu.VMEM((ts + CONV_HALO, D_FF), F32),
                        pltpu.VMEM((ts, D_FF), BF16), pltpu.VMEM((ts, D_MODEL), F32)],
        compiler_params=_params(("arbitrary",)),
    )(xhat1, h1b, target, wup4, wdown, cw, cb, g1, b1, g2, b2)


def _ffn_bwd(dz2, dz2b, ub, xhat1, rstd1, wup4, wdown, cw, cb, g1, ts):
    s = dz2.shape[0]
    nt = s // ts
    hb = 16

    def body(dz2_ref, dz2b_ref, ub_ref, prev_ref, xhat_ref, rstd_ref, wup_ref, wdn_ref, cw_ref, cb_ref, g1_ref,
             a_ref, dub_ref, dz1_ref, dz1b_ref, dg1_ref, db1_ref, dcw_ref, dcb_ref, gext_scr, dext_scr):
        i = pl.program_id(0)
        r = nt - 1 - i

        @pl.when(i == 0)
        def _():
            dext_scr[ts:ts + CONV_HALO, :] = jnp.zeros((CONV_HALO, D_FF), F32)
            dg1_ref[...] = jnp.zeros_like(dg1_ref)
            db1_ref[...] = jnp.zeros_like(db1_ref)
            dcw_ref[...] = jnp.zeros_like(dcw_ref)
            dcb_ref[...] = jnp.zeros_like(dcb_ref)

        da = _dot_nt(dz2b_ref[...], wdn_ref[...])
        val = ub_ref[:, 0:D_FF].astype(F32)
        gate = ub_ref[:, D_FF:2 * D_FF].astype(F32)
        prev = prev_ref[...].astype(F32)[hb - CONV_HALO:hb]
        gext_scr[0:CONV_HALO, :] = jnp.where(r == 0, 0.0, prev)
        gext_scr[CONV_HALO:CONV_HALO + ts, :] = gate
        ext = gext_scr[...]
        g2s = pltpu.roll(ext, 2, 0)[CONV_HALO:]
        g1s = pltpu.roll(ext, 1, 0)[CONV_HALO:]
        hc = cb_ref[...] + ((g2s * cw_ref[0:1, :] + g1s * cw_ref[1:2, :]) + gate * cw_ref[2:3, :])
        sg = _sigmoid(hc)
        si = hc * sg
        a_ref[...] = (si * val).astype(BF16)
        dhc = da * val * (sg * (1.0 + hc * (1.0 - sg)))
        dcb_ref[...] += jnp.sum(dhc, axis=0, keepdims=True)
        dcw_ref[0:1, :] += jnp.sum(dhc * g2s, axis=0, keepdims=True)
        dcw_ref[1:2, :] += jnp.sum(dhc * g1s, axis=0, keepdims=True)
        dcw_ref[2:3, :] += jnp.sum(dhc * gate, axis=0, keepdims=True)
        dext_scr[0:ts, :] = dhc
        dext = dext_scr[...]
        dgate = (dhc * cw_ref[2:3, :] + pltpu.roll(dext, ts + CONV_HALO - 1, 0)[0:ts] * cw_ref[1:2, :]
                 + pltpu.roll(dext, ts + CONV_HALO - 2, 0)[0:ts] * cw_ref[0:1, :])
        dext_scr[ts:ts + CONV_HALO, :] = dext_scr[0:CONV_HALO, :]
        dub_ref[:, 0:D_FF] = (da * si).astype(BF16)
        dub_ref[:, D_FF:2 * D_FF] = dgate.astype(BF16)
        dh1 = ALPHA * dz2_ref[...]
        for j in range(N_SHARD):
            dh1 = dh1 + _dot_nt(dub_ref[:, j * UP_SH:(j + 1) * UP_SH], wup_ref[j])
        xhat = xhat_ref[...]
        dg1_ref[...] += jnp.sum(dh1 * xhat, axis=0, keepdims=True)
        db1_ref[...] += jnp.sum(dh1, axis=0, keepdims=True)
        dz1 = _layernorm_bwd(dh1, xhat, rstd_ref[...], g1_ref[...])
        dz1_ref[...] = dz1
        dz1b_ref[...] = dz1.astype(BF16)

    tile = lambda w: pl.BlockSpec((ts, w), lambda i: (nt - 1 - i, 0))
    acc = lambda rws, w: pl.BlockSpec((rws, w), lambda i: (0, 0))
    prev_spec = pl.BlockSpec((hb, D_FF), lambda i: (jnp.maximum((nt - 1 - i) * (ts // hb) - 1, 0), 1))
    return pl.pallas_call(
        body, name="ffn_bwd", grid=(nt,),
        in_specs=[tile(D_MODEL), tile(D_MODEL), tile(2 * D_FF), prev_spec, tile(D_MODEL), tile(1)] + [_whole()] * 5,
        out_specs=[tile(D_FF), tile(2 * D_FF), tile(D_MODEL), tile(D_MODEL), acc(1, D_MODEL), acc(1, D_MODEL),
                   acc(3, D_FF), acc(1, D_FF)],
        out_shape=[jax.ShapeDtypeStruct((s, D_FF), BF16), jax.ShapeDtypeStruct((s, 2 * D_FF), BF16),
                   jax.ShapeDtypeStruct((s, D_MODEL), F32), jax.ShapeDtypeStruct((s, D_MODEL), BF16),
                   jax.ShapeDtypeStruct((1, D_MODEL), F32),
                   jax.ShapeDtypeStruct((1, D_MODEL), F32), jax.ShapeDtypeStruct((3, D_FF), F32),
                   jax.ShapeDtypeStruct((1, D_FF), F32)],
        scratch_shapes=[pltpu.VMEM((ts + CONV_HALO, D_FF), F32), pltpu.VMEM((ts + CONV_HALO, D_FF), F32)],
        compiler_params=_params(("arbitrary",)),
    )(dz2, dz2b, ub, ub, xhat1, rstd1, wup4, wdown, cw, cb, g1)


def _mix_bwd(dz1, pooled, ret, g, wout, wpool, pscale, ts, riders=()):
    s = dz1.shape[0]
    nt = s // ts

    def body(dz1_ref, pooled_ref, ret_ref, g_ref, wout_ref, wp_ref, ps_ref,
             dret_ref, dgp_ref, dwp_ref, dps_ref, eext_scr):
        i = pl.program_id(0)
        r = nt - 1 - i

        @pl.when(i == 0)
        def _():
            eext_scr[ts:ts + POOL_HALO, :] = jnp.zeros((POOL_HALO, POOL_W), F32)
            dwp_ref[...] = jnp.zeros_like(dwp_ref)
            dps_ref[...] = jnp.zeros_like(dps_ref)

        dzb = dz1_ref[...].astype(BF16)
        dcat_r = _dot_nt(dzb, wout_ref[0:RET_W, :])
        dcat_p = _dot_nt(dzb, wout_ref[RET_W:2 * RET_W, :])
        pos = (r * ts + lax.broadcasted_iota(jnp.int32, (ts, 1), 0) + 1).astype(F32)
        dpooled = []
        for gi, w in enumerate(POOL_WINDOWS):
            sl = slice(gi * HEAD_DIM, (gi + 1) * HEAD_DIM)
            pb = pooled_ref[:, sl]
            dy = dcat_p[:, sl]
            dps_ref[:, sl] += jnp.sum(dy * _dot(pb, wp_ref[gi]), axis=0, keepdims=True)
            dlin = (dy * ps_ref[:, sl]).astype(BF16)
            dwp_ref[gi] += _dot_tn(pb, dlin)
            dpg = _dot_nt(dlin, wp_ref[gi])
            dpooled.append(dpg)
            eext_scr[0:ts, sl] = dpg / jnp.minimum(pos, float(w))
        for gi, w in enumerate(POOL_WINDOWS):
            sl = slice(gi * HEAD_DIM, (gi + 1) * HEAD_DIM)
            acc = eext_scr[:, sl]
            shift = 1
            while shift < w:
                acc = acc + pltpu.roll(acc, ts + POOL_HALO - shift, 0)
                shift *= 2
            dgp_ref[:, RET_W + gi * HEAD_DIM:RET_W + (gi + 1) * HEAD_DIM] = (acc[0:ts] - dpooled[gi]).astype(BF16)
        eext_scr[ts:ts + POOL_HALO, :] = eext_scr[0:POOL_HALO, :]
        for h in range(HEADS):
            sl = slice(h * HEAD_DIM, (h + 1) * HEAD_DIM)
            rt = ret_ref[:, sl]
            rr = lax.rsqrt(jnp.mean(rt * rt, axis=-1, keepdims=True) + RMS_EPS)
            rn = rt * rr
            gh = g_ref[:, sl]
            sg = _sigmoid(gh)
            dy = dcat_r[:, sl]
            dgp_ref[:, sl] = (dy * rn * (sg * (1.0 + gh * (1.0 - sg)))).astype(BF16)
            drn = dy * (gh * sg)
            dret_ref[:, sl] = (rr * (drn - rn * jnp.mean(drn * rn, axis=-1, keepdims=True))).astype(BF16)

    tile = lambda w: pl.BlockSpec((ts, w), lambda i: (nt - 1 - i, 0))
    return _call(
        body, name="mix_bwd", grid=(nt,),
        in_specs=[tile(D_MODEL), tile(POOL_W), tile(RET_W), tile(RET_W), _whole(), _whole(), _whole()],
        out_specs=[tile(RET_W), tile(2 * RET_W),
                   pl.BlockSpec((len(POOL_WINDOWS), HEAD_DIM, HEAD_DIM), lambda i: (0, 0, 0)),
                   pl.BlockSpec((1, POOL_W), lambda i: (0, 0))],
        out_shape=[jax.ShapeDtypeStruct((s, RET_W), BF16), jax.ShapeDtypeStruct((s, 2 * RET_W), BF16),
                   jax.ShapeDtypeStruct((len(POOL_WINDOWS), HEAD_DIM, HEAD_DIM), F32),
                   jax.ShapeDtypeStruct((1, POOL_W), F32)],
        scratch_shapes=[pltpu.VMEM((ts + POOL_HALO, POOL_W), F32)],
        sem=("arbitrary",), operands=(dz1, pooled, ret, g, wout, wpool, pscale), riders=riders,
    )


def _retention_bwd(q, k, v, dret, dgp, states, mask, qd, kd, cosf, sinf, riders=()):
    s = q.shape[0]
    ns = s // SUPER
    cdec = [gm ** float(SUPER) for gm in _gammas()]

    def body(q_ref, k_ref, v_ref, do_ref, dgp_ref, st_ref, mask_ref, qd_ref, kd_ref, cos_ref, sin_ref,
             dproj_ref, dstate_scr):
        i = pl.program_id(0)

        @pl.when(i == 0)
        def _():
            dstate_scr[...] = jnp.zeros_like(dstate_scr)

        cosf_t = cos_ref[...]
        sinf_t = sin_ref[...]
        for h in range(HEADS):
            sl = slice(h * HEAD_DIM, (h + 1) * HEAD_DIM)
            qh, kh, vh, doh = q_ref[:, sl], k_ref[:, sl], v_ref[:, sl], do_ref[:, sl]
            m = mask_ref[h]
            scb = (_dot_nt(qh, kh) * m).astype(BF16)
            dscb = (_dot_nt(doh, vh) * m).astype(BF16)
            stb = st_ref[0, h]
            dst = dstate_scr[h]
            dstb = dst.astype(BF16)
            qdb = (qh.astype(F32) * qd_ref[:, sl]).astype(BF16)
            kdb = (kh.astype(F32) * kd_ref[:, sl]).astype(BF16)
            dq = _dot(dscb, kh) + _dot_nt(doh, stb) * qd_ref[:, sl]
            dk = _dot_tn(dscb, qh) + _dot_nt(vh, dstb) * kd_ref[:, sl]
            dv = _dot_tn(scb, doh) + _dot(kdb, dstb)
            dstate_scr[h] = dst * cdec[h] + _dot_tn(qdb, doh)
            lo = h * HEAD_DIM
            dproj_ref[:, lo:lo + HEAD_DIM] = _rope_t(dq, cosf_t, sinf_t).astype(BF16)
            dproj_ref[:, RET_W + lo:RET_W + lo + HEAD_DIM] = _rope_t(dk * K_SCALE, cosf_t, sinf_t).astype(BF16)
            dproj_ref[:, 2 * RET_W + lo:2 * RET_W + lo + HEAD_DIM] = dv.astype(BF16)
        dproj_ref[:, 3 * RET_W:IN_W] = dgp_ref[...]

    tile = lambda w: pl.BlockSpec((SUPER, w), lambda i: (ns - 1 - i, 0))
    return _call(
        body, name="retention_bwd", grid=(ns,),
        in_specs=[tile(RET_W), tile(RET_W), tile(RET_W), tile(RET_W), tile(2 * RET_W),
                  pl.BlockSpec((1, HEADS, HEAD_DIM, HEAD_DIM), lambda i: (ns - 1 - i, 0, 0, 0)),
                  _whole(), _whole(), _whole(), tile(HEAD_DIM), tile(HEAD_DIM)],
        out_specs=[tile(IN_W)],
        out_shape=[jax.ShapeDtypeStruct((s, IN_W), BF16)],
        scratch_shapes=[pltpu.VMEM((HEADS, HEAD_DIM, HEAD_DIM), F32)],
        sem=("arbitrary",), operands=(q, k, v, dret, dgp, states, mask, qd, kd, cosf, sinf), riders=riders,
    )


def _dx(dz1, dproj, win4, ts, riders=()):
    s = dz1.shape[0]

    def body(dz1_ref, dp_ref, w_ref, dx_ref):
        acc = ALPHA * dz1_ref[...]
        for j in range(N_SHARD):
            acc = acc + _dot_nt(dp_ref[:, j * IN_SH:(j + 1) * IN_SH], w_ref[j])
        dx_ref[...] = acc

    tile = lambda w: pl.BlockSpec((ts, w), lambda i: (i, 0))
    return _call(
        body, name="dx", grid=(s // ts,),
        in_specs=[tile(D_MODEL), tile(IN_W), _whole()],
        out_specs=[tile(D_MODEL)],
        out_shape=[jax.ShapeDtypeStruct((s, D_MODEL), F32)],
        sem=("arbitrary",), operands=(dz1, dproj, win4), riders=riders,
    )


def _wgrad(a, b, tm, tn, name, stacked, m_outer, riders=()):
    s, m = a.shape
    n = b.shape[1]

    def body(a_ref, b_ref, o32_ref, o16_ref):
        res = _dot_tn(a_ref[...], b_ref[...])
        o32_ref[...] = res.reshape(o32_ref.shape)
        o16_ref[...] = res.astype(BF16).reshape(o16_ref.shape)

    if m_outer:
        grid, blocks = (m // tm, n // tn), (lambda g0, g1: (g0, g1))
    else:
        grid, blocks = (n // tn, m // tm), (lambda g0, g1: (g1, g0))
    if stacked:
        shape = (n // tn, m, tn)
        ospec = pl.BlockSpec((1, tm, tn), lambda g0, g1: (blocks(g0, g1)[1], blocks(g0, g1)[0], 0))
    else:
        shape = (m, n)
        ospec = pl.BlockSpec((tm, tn), lambda g0, g1: blocks(g0, g1))
    return _call(
        body, name=name, grid=grid,
        in_specs=[pl.BlockSpec((s, tm), lambda g0, g1: (0, blocks(g0, g1)[0])),
                  pl.BlockSpec((s, tn), lambda g0, g1: (0, blocks(g0, g1)[1]))],
        out_specs=[ospec, ospec],
        out_shape=[jax.ShapeDtypeStruct(shape, F32), jax.ShapeDtypeStruct(shape, BF16)],
        sem=("arbitrary", "arbitrary"), operands=(a, b), riders=riders,
    )


class _NoComm:
    def __init__(self, win4, wout, wup4, wdown):
        self.weights = dict(w_in=win4, w_out=wout, w_up=wup4, w_down=wdown)
        self.grads = {}

    def weight(self, name):
        return self.weights[name]

    def riders(self, call):
        return ()

    def landed(self, call, results):
        pass

    def gradient(self, name, g32, g16):
        self.grads[name] = (g32, g16)


def _local_step(x, target, cw, cb, wpool, pscale, g1, b1, g2, b2, comm):
    s = x.shape[0]
    ts_a = min(512, s)
    ts_f = min(256, s)
    mask, qd, kd = _decay_tables()
    cosf, sinf = _rope_tables(s)
    wpool_b = wpool.astype(BF16)

    def run(call, fn, *args):
        outs, res = fn(*args, riders=comm.riders(call))
        comm.landed(call, res)
        return outs

    xb, q, k, v, g, pooled, cat = run("proj_pool", _proj_pool, x, comm.weight("w_in"), cosf, sinf, wpool_b,
                                      pscale, ts_a)
    ret, cat, states = run("retention_fwd", _retention_fwd, q, k, v, g, cat, mask, qd, kd)
    wout = comm.weight("w_out")
    xhat1, rstd1, h1b = run("outproj_ln1", _outproj_ln1, x, cat, wout, g1, b1, ts_a)
    wup4, wdown = comm.weight("w_up"), comm.weight("w_down")
    ub, dz2, dz2b, loss, dg2, db2 = _ffn_fwd_loss(xhat1, h1b, target, wup4, wdown, cw, cb, g1, b1, g2, b2, ts_f)

    act, dub, dz1, dz1b, dg1, db1, dcw, dcb = _ffn_bwd(dz2, dz2b, ub, xhat1, rstd1, wup4, wdown, cw, cb, g1, ts_f)
    half = D_MODEL // 2
    comm.gradient("w_up", *run("wgrad_up", _wgrad, h1b, dub, half, UP_SH, "wgrad_up", True, False))
    comm.gradient("w_out", *run("wgrad_out", _wgrad, cat, dz1b, D_MODEL, half, "wgrad_out", False, True))
    comm.gradient("w_down", *run("wgrad_down", _wgrad, act, dz2b, D_FF // 2, half, "wgrad_down", False, True))
    dret, dgp, dwp, dps = run("mix_bwd", _mix_bwd, dz1b, pooled, ret, g, wout, wpool_b, pscale, ts_a)
    dproj, = run("retention_bwd", _retention_bwd, q, k, v, dret, dgp, states, mask, qd, kd, cosf, sinf)
    comm.gradient("w_in", *run("wgrad_in", _wgrad, xb, dproj, D_MODEL, IN_SH, "wgrad_in", True, True))
    grad_x, = run("dx", _dx, dz1, dproj, comm.weight("w_in"), ts_a)
    small = dict(w_pool=dwp, pool_scale=dps, ln1_g=dg1, ln1_b=db1, conv_w=dcw, conv_b=dcb,
                 ln2_g=dg2, ln2_b=db2)
    return loss, grad_x, small


CAST_ROWS = 64
SHARD_SHAPES = ((D_MODEL, IN_SH), (OUT_SH, D_MODEL), (D_MODEL, UP_SH), (DOWN_SH, D_MODEL))
N_BIG = len(SHARD_SHAPES)
CW_PAD = (8, 768)


def _mesh_pos():
    return lax.axis_index("x"), lax.axis_index("y"), lax.axis_index("c")


def _other_chips(x, y):
    return [(1 - x, y), (x, 1 - y), (1 - x, 1 - y)]


def _half_rows(w, which):
    hr = SHARD_SHAPES[w][0] // 2
    return pl.ds(pl.multiple_of(which * hr, 16), hr)


def _gather_weights(shards, cw8, full):
    def body(*refs):
        in_refs = refs[:N_BIG]
        cw_ref = refs[N_BIG]
        out_refs = refs[N_BIG + 1:2 * N_BIG + 1]
        cwo_ref = refs[2 * N_BIG + 1]
        stage = refs[2 * N_BIG + 2:3 * N_BIG + 2]
        send_sems, recv_sems, fsend_sems, frecv_sems, cw_send, cw_recv, local_sems = refs[3 * N_BIG + 2:]
        x, y, c = _mesh_pos()
        j0 = 2 * x + y
        chips = _other_chips(x, y)

        def cast_to_stage(w):
            def cast(i, carry):
                rows = pl.ds(pl.multiple_of(i * CAST_ROWS, CAST_ROWS), CAST_ROWS)
                stage[w][rows, :] = in_refs[w][rows, :].astype(BF16)
                return carry
            lax.fori_loop(0, SHARD_SHAPES[w][0] // CAST_ROWS, cast, 0)

        for w in full:
            cast_to_stage(w)

        def ici(w, k, block):
            chip = chips[k]
            return pltpu.make_async_remote_copy(
                src_ref=stage[w].at[_half_rows(w, c), :], dst_ref=out_refs[w].at[block, _half_rows(w, c), :],
                send_sem=send_sems.at[w, k], recv_sem=recv_sems.at[w, k],
                device_id=(chip[0], chip[1], c), device_id_type=MESH)

        def d2d(w, k, block, half):
            return pltpu.make_async_remote_copy(
                src_ref=out_refs[w].at[block, _half_rows(w, half), :],
                dst_ref=out_refs[w].at[block, _half_rows(w, half), :],
                send_sem=fsend_sems.at[w, k], recv_sem=frecv_sems.at[w, k],
                device_id=(x, y, 1 - c), device_id_type=MESH)

        def conv(k, block):
            chip = chips[k]
            return pltpu.make_async_remote_copy(
                src_ref=cw_ref, dst_ref=cwo_ref.at[block], send_sem=cw_send.at[k], recv_sem=cw_recv.at[k],
                device_id=(chip[0], chip[1], c), device_id_type=MESH)

        sent = [ici(w, k, j0) for w in full for k in range(3)] + [conv(k, j0) for k in range(3)]
        for cp in sent:
            cp.start()
        for w in range(N_BIG):
            if w not in full:
                cast_to_stage(w)
        local = [pltpu.make_async_copy(stage[w], out_refs[w].at[j0], local_sems.at[w]) for w in range(N_BIG)]
        local.append(pltpu.make_async_copy(cw_ref, cwo_ref.at[j0], local_sems.at[N_BIG]))
        for cp in local:
            cp.start()
        for k, chip in enumerate(chips):
            jk = 2 * chip[0] + chip[1]
            for w in full:
                ici(w, k, jk).wait_recv()
                fw = d2d(w, k, jk, c)
                fw.start()
                sent.append(fw)
        for k, chip in enumerate(chips):
            jk = 2 * chip[0] + chip[1]
            for w in full:
                d2d(w, k, jk, 1 - c).wait_recv()
            conv(k, jk).wait_recv()
        for cp in sent:
            cp.wait_send()
        for cp in local:
            cp.wait()

    out_shape = [jax.ShapeDtypeStruct((N_SHARD,) + shp, BF16) for shp in SHARD_SHAPES]
    out_shape.append(jax.ShapeDtypeStruct((N_SHARD,) + CW_PAD, F32))
    return pl.pallas_call(
        body, name="gather_weights",
        in_specs=[_whole()] * (N_BIG + 1),
        out_specs=[HBM_SPEC] * (N_BIG + 1),
        out_shape=out_shape,
        scratch_shapes=[pltpu.VMEM(shp, BF16) for shp in SHARD_SHAPES] + [
            pltpu.SemaphoreType.DMA((N_BIG, 3)), pltpu.SemaphoreType.DMA((N_BIG, 3)),
            pltpu.SemaphoreType.DMA((N_BIG, 3)), pltpu.SemaphoreType.DMA((N_BIG, 3)),
            pltpu.SemaphoreType.DMA((3,)), pltpu.SemaphoreType.DMA((3,)),
            pltpu.SemaphoreType.DMA((N_BIG + 1,))],
        compiler_params=pltpu.CompilerParams(vmem_limit_bytes=VMEM_LIMIT),
    )(*shards, cw8)


def _gather_rider(arrays, ops):
    ws = sorted(arrays)

    def make(inplace, srcs, lands, send_sems, recv_sems):
        del srcs, lands
        x, y, c = _mesh_pos()
        j0 = 2 * x + y
        chips = _other_chips(x, y)
        starts, waits = [], []
        for n, (kind, w, (r0, nr)) in enumerate(ops):
            ref = inplace[ws.index(w)]
            hr = SHARD_SHAPES[w][0] // 2
            rows = lambda core: pl.ds(pl.multiple_of(core * hr + r0, 16), nr)
            for k, chip in enumerate(chips):
                jk = 2 * chip[0] + chip[1]
                if kind == "ici":
                    src, to, landing = ref.at[j0, rows(c), :], (chip[0], chip[1], c), ref.at[jk, rows(c), :]
                else:
                    src, to, landing = ref.at[jk, rows(c), :], (x, y, 1 - c), ref.at[jk, rows(1 - c), :]
                sems = dict(send_sem=send_sems.at[3 * n + k], recv_sem=recv_sems.at[3 * n + k],
                            device_id=to, device_id_type=MESH)
                send = pltpu.make_async_remote_copy(src_ref=src, dst_ref=src, **sems)
                arrival = pltpu.make_async_remote_copy(src_ref=src, dst_ref=landing, **sems)
                starts.append(send)
                waits += [arrival.wait_recv, send.wait_send]
        return starts, waits

    return _Rider([arrays[w] for w in ws], [], [], 3 * len(ops), make)


def _whole_half(w):
    return (0, SHARD_SHAPES[w][0] // 2)


def _pair_rider(ws, g16s):
    def make(inplace, srcs, lands, send_sems, recv_sems):
        del inplace
        x, y, c = _mesh_pos()
        copies = [pltpu.make_async_remote_copy(
            src_ref=srcs[i].at[:, _half_rows(w, 1 - c), :], dst_ref=lands[i],
            send_sem=send_sems.at[i], recv_sem=recv_sems.at[i], device_id=(x, y, 1 - c), device_id_type=MESH)
            for i, w in enumerate(ws)]
        return copies, [cp.wait for cp in copies]

    lands = [jax.ShapeDtypeStruct((N_SHARD, SHARD_SHAPES[w][0] // 2, SHARD_SHAPES[w][1]), BF16) for w in ws]
    return _Rider([], g16s, lands, len(ws), make)


def _chip_rider(ws, p16s, rows=None, landing=None):
    def make(inplace, srcs, lands, send_sems, recv_sems):
        x, y, c = _mesh_pos()
        dsts = inplace if landing is not None else lands
        copies = []
        for i, w in enumerate(ws):
            r0, nr = rows if rows is not None else _whole_half(w)
            for k, chip in enumerate(_other_chips(x, y)):
                copies.append(pltpu.make_async_remote_copy(
                    src_ref=srcs[i].at[2 * chip[0] + chip[1], pl.ds(r0, nr), :],
                    dst_ref=dsts[i].at[k, pl.ds(r0, nr), :],
                    send_sem=send_sems.at[3 * i + k], recv_sem=recv_sems.at[3 * i + k],
                    device_id=(chip[0], chip[1], c), device_id_type=MESH))
        return copies, [cp.wait for cp in copies]

    lands = [jax.ShapeDtypeStruct((3, SHARD_SHAPES[w][0] // 2, SHARD_SHAPES[w][1]), BF16) for w in ws]
    if landing is not None:
        return _Rider(landing, p16s, [], 3 * len(ws), make)
    return _Rider([], p16s, lands, 3 * len(ws), make)


def _final_rider(halves):
    def make(inplace, srcs, lands, send_sems, recv_sems):
        del inplace
        x, y, c = _mesh_pos()
        copies = [pltpu.make_async_remote_copy(
            src_ref=srcs[i], dst_ref=lands[i], send_sem=send_sems.at[i], recv_sem=recv_sems.at[i],
            device_id=(x, y, 1 - c), device_id_type=MESH) for i in range(len(halves))]
        return copies, [cp.wait for cp in copies]

    return _Rider([], halves, [jax.ShapeDtypeStruct(h.shape, h.dtype) for h in halves], len(halves), make)


def _comm_only(name, riders):
    _, res = _call(lambda: None, name=name, grid=(), in_specs=[], out_specs=[], out_shape=[], operands=(),
                   riders=riders)
    return res


def _pair_sum(pos, ws, g32s, recvs):
    n = len(ws)

    def body(pos_ref, *refs):
        del pos_ref
        g_refs, r_refs = refs[:n], refs[n:2 * n]
        p32_refs, p16_refs = refs[2 * n:3 * n], refs[3 * n:]
        for i in range(n):
            tot = g_refs[i][...] + r_refs[i][...].astype(F32)
            p32_refs[i][...] = tot
            p16_refs[i][...] = tot.astype(BF16)

    halves = [(SHARD_SHAPES[w][0] // 2, SHARD_SHAPES[w][1]) for w in ws]
    own = [pl.BlockSpec((None, None) + h, lambda j, pos_ref: (j, pos_ref[0], 0, 0)) for h in halves]
    blk = [pl.BlockSpec((None,) + h, lambda j, pos_ref: (j, 0, 0)) for h in halves]
    g4 = [g.reshape((N_SHARD, 2) + h) for g, h in zip(g32s, halves)]
    outs = pl.pallas_call(
        body, name="pair_sum_" + "_".join(str(w) for w in ws),
        grid_spec=pltpu.PrefetchScalarGridSpec(
            num_scalar_prefetch=1, grid=(N_SHARD,), in_specs=own + blk, out_specs=blk + blk),
        out_shape=[jax.ShapeDtypeStruct((N_SHARD,) + h, F32) for h in halves]
        + [jax.ShapeDtypeStruct((N_SHARD,) + h, BF16) for h in halves],
        compiler_params=_params(("arbitrary",)),
    )(pos, *g4, *recvs)
    return outs[:n], outs[n:]


def _chip_sum(pos, p32s, recvs):
    parts = 2

    def body(pos_ref, *refs):
        del pos_ref
        p_refs, r_refs, f_refs = refs[:N_BIG], refs[N_BIG:2 * N_BIG], refs[2 * N_BIG:]
        for w in range(N_BIG):
            f_refs[w][...] = ((p_refs[w][...] + r_refs[w][0].astype(F32)) + r_refs[w][1].astype(F32)) \
                + r_refs[w][2].astype(F32)

    quarters = [(r // 2 // parts, cc) for r, cc in SHARD_SHAPES]
    own = [pl.BlockSpec((None,) + qt, lambda i, pos_ref: (pos_ref[1], i, 0)) for qt in quarters]
    rcv = [pl.BlockSpec((3,) + qt, lambda i, pos_ref: (0, i, 0)) for qt in quarters]
    out = [pl.BlockSpec(qt, lambda i, pos_ref: (i, 0)) for qt in quarters]
    return pl.pallas_call(
        body, name="chip_sum",
        grid_spec=pltpu.PrefetchScalarGridSpec(
            num_scalar_prefetch=1, grid=(parts,), in_specs=own + rcv, out_specs=out),
        out_shape=[jax.ShapeDtypeStruct((r // 2, cc), F32) for r, cc in SHARD_SHAPES],
        compiler_params=_params(("arbitrary",)),
    )(pos, *p32s, *recvs)


def _adamw(w, g, m, v):
    m_new = ADAM_B1 * m + (1.0 - ADAM_B1) * g
    v_new = ADAM_B2 * v + (1.0 - ADAM_B2) * (g * g)
    m_hat = m_new / (1.0 - ADAM_B1 ** ADAM_STEP)
    v_hat = v_new / (1.0 - ADAM_B2 ** ADAM_STEP)
    delta = -ADAM_LR * (m_hat / (jnp.sqrt(v_hat) + ADAM_EPS) + ADAM_WD * w)
    return delta, m_new, v_new


def _adam_big(pos, mine, theirs, ws, ms, vs):
    nb = 4

    def body(pos_ref, *refs):
        hf = pl.program_id(0)
        groups = [refs[i * N_BIG:(i + 1) * N_BIG] for i in range(9)]
        f_refs, t_refs, w_refs, m_refs, v_refs, go_refs, do_refs, mo_refs, vo_refs = groups
        for w in range(N_BIG):
            g = jnp.where(hf == pos_ref[0], f_refs[w][...], t_refs[w][...])
            delta, m_new, v_new = _adamw(w_refs[w][...], g, m_refs[w][...], v_refs[w][...])
            go_refs[w][...] = g
            do_refs[w][...] = delta
            mo_refs[w][...] = m_new
            vo_refs[w][...] = v_new

    blocks = [(r // 2 // nb, cc) for r, cc in SHARD_SHAPES]
    half = [pl.BlockSpec(b, lambda hf, i, pos_ref: (i, 0)) for b in blocks]
    full = [pl.BlockSpec((None,) + b, lambda hf, i, pos_ref: (0, hf * nb + i, 0)) for b in blocks]
    shapes = [jax.ShapeDtypeStruct((1,) + shp, F32) for shp in SHARD_SHAPES]
    outs = pl.pallas_call(
        body, name="adam_big",
        grid_spec=pltpu.PrefetchScalarGridSpec(
            num_scalar_prefetch=1, grid=(2, nb), in_specs=half + half + full * 3, out_specs=full * 4),
        out_shape=shapes * 4,
        compiler_params=_params(("arbitrary", "arbitrary")),
    )(pos, *mine, *theirs, *ws, *ms, *vs)
    return [outs[i * N_BIG:(i + 1) * N_BIG] for i in range(4)]


SMALL_ROWS = 8
ROW_CONV_B, ROW_POOL_SCALE, ROW_LN1_G, ROW_LN1_B, ROW_LN2_G, ROW_LN2_B, ROW_LOSS = range(7)
SMALL_VECS = ((ROW_CONV_B, D_FF), (ROW_POOL_SCALE, POOL_W), (ROW_LN1_G, D_MODEL), (ROW_LN1_B, D_MODEL),
              (ROW_LN2_G, D_MODEL), (ROW_LN2_B, D_MODEL))


def _small_update(loss, dwp, dcw4, vec_grads, wp, cwp, vec_ws, m_wp, m_cwp, vec_ms, v_wp, v_cwp, vec_vs,
                  riders=()):
    nv = len(SMALL_VECS)

    def body(*refs):
        loss_ref, dwp_ref, dcw_ref = refs[0:3]
        gvec = refs[3:3 + nv]
        o = 3 + nv
        wp_ref, cw_ref = refs[o:o + 2]
        wvec = refs[o + 2:o + 2 + nv]
        o += 2 + nv
        mwp_ref, mcw_ref = refs[o:o + 2]
        mvec = refs[o + 2:o + 2 + nv]
        o += 2 + nv
        vwp_ref, vcw_ref = refs[o:o + 2]
        vvec = refs[o + 2:o + 2 + nv]
        o += 2 + nv
        loss_out = refs[o]
        outs = refs[o + 1:o + 1 + 4 * (2 + nv)]
        o += 1 + 4 * (2 + nv)
        (vec_scr, sib_a, sib_b, sib_c, all_a, all_b, all_c,
         send1, recv1, send2, recv2) = refs[o:]
        x, y, c = _mesh_pos()
        j0 = 2 * x + y
        chips = _other_chips(x, y)

        vec_scr[...] = jnp.zeros_like(vec_scr)
        for (row, n), ref in zip(SMALL_VECS, gvec):
            vec_scr[row:row + 1, 0:n] = ref[...]
        vec_scr[ROW_LOSS:ROW_LOSS + 1, 0:HEAD_DIM] = jnp.broadcast_to(loss_ref[...], (1, HEAD_DIM))

        mine = (dwp_ref, vec_scr, dcw_ref)
        sib = (sib_a, sib_b, sib_c)
        every = (all_a, all_b, all_c)
        first = [pltpu.make_async_remote_copy(
            src_ref=mine[b], dst_ref=sib[b], send_sem=send1.at[b], recv_sem=recv1.at[b],
            device_id=(x, y, 1 - c), device_id_type=MESH) for b in range(3)]
        for cp in first:
            cp.start()
        for cp in first:
            cp.wait()
        for b in range(3):
            every[b][j0] = mine[b][...] + sib[b][...]

        def ici(b, k, block):
            chip = chips[k]
            return pltpu.make_async_remote_copy(
                src_ref=every[b].at[block], dst_ref=every[b].at[block],
                send_sem=send2.at[b, k], recv_sem=recv2.at[b, k],
                device_id=(chip[0], chip[1], c), device_id_type=MESH)

        second = [ici(b, k, j0) for b in range(3) for k in range(3)]
        for cp in second:
            cp.start()
        for k, chip in enumerate(chips):
            for b in range(3):
                ici(b, k, 2 * chip[0] + chip[1]).wait_recv()
        for cp in second:
            cp.wait_send()

        tot_a = ((all_a[0] + all_a[1]) + all_a[2]) + all_a[3]
        tot_b = ((all_b[0] + all_b[1]) + all_b[2]) + all_b[3]
        all_c[0] = ((all_c[0] + all_c[1]) + all_c[2]) + all_c[3]
        tot_c = all_c[0, j0]
        loss_out[...] = tot_b[ROW_LOSS:ROW_LOSS + 1, 0:1]

        grads = [tot_a, tot_c] + [tot_b[row:row + 1, 0:n] for row, n in SMALL_VECS]
        w_all = [wp_ref, cw_ref] + list(wvec)
        m_all = [mwp_ref, mcw_ref] + list(mvec)
        v_all = [vwp_ref, vcw_ref] + list(vvec)
        np_ = 2 + nv
        for p in range(np_):
            g = grads[p]
            delta, m_new, v_new = _adamw(w_all[p][...], g, m_all[p][...], v_all[p][...])
            outs[p][...] = g
            outs[np_ + p][...] = delta
            outs[2 * np_ + p][...] = m_new
            outs[3 * np_ + p][...] = v_new

    pshapes = [wp.shape, CW_PAD] + [wv.shape for wv in vec_ws]
    out_shape = [jax.ShapeDtypeStruct((1, 1), F32)] + [jax.ShapeDtypeStruct(s, F32) for s in pshapes] * 4
    a_shape = dwp.shape
    b_shape = (SMALL_ROWS, D_FF)
    c_shape = dcw4.shape
    n_in = 3 + nv + 3 * (2 + nv)
    outs, rider_res = _call(
        body, name="small_update", grid=(),
        in_specs=[_whole()] * n_in, out_specs=[_whole()] * len(out_shape), out_shape=out_shape,
        scratch_shapes=[pltpu.VMEM(b_shape, F32),
                        pltpu.VMEM(a_shape, F32), pltpu.VMEM(b_shape, F32), pltpu.VMEM(c_shape, F32),
                        pltpu.VMEM((N_SHARD,) + a_shape, F32), pltpu.VMEM((N_SHARD,) + b_shape, F32),
                        pltpu.VMEM((N_SHARD,) + c_shape, F32),
                        pltpu.SemaphoreType.DMA((3,)), pltpu.SemaphoreType.DMA((3,)),
                        pltpu.SemaphoreType.DMA((3, 3)), pltpu.SemaphoreType.DMA((3, 3))],
        operands=(loss, dwp, dcw4, *vec_grads, wp, cwp, *vec_ws, m_wp, m_cwp, *vec_ms, v_wp, v_cwp, *vec_vs),
        riders=riders,
    )
    np_ = 2 + nv
    return outs[0], [outs[1 + i * np_:1 + (i + 1) * np_] for i in range(4)], rider_res


def _pad_cw(a):
    pad = [(0, 0)] * (a.ndim - 2) + [(0, CW_PAD[0] - a.shape[-2]), (0, CW_PAD[1] - a.shape[-1])]
    return jnp.pad(a, pad)


def kernel(x, w_in, w_pool, pool_scale, w_out, ln1_g, ln1_b, w_up, conv_w, conv_b, w_down, ln2_g, ln2_b, loss_target, m_w_in, m_w_pool, m_pool_scale, m_w_out, m_ln1_g, m_ln1_b, m_w_up, m_conv_w, m_conv_b, m_w_down, m_ln2_g, m_ln2_b, v_w_in, v_w_pool, v_pool_scale, v_w_out, v_ln1_g, v_ln1_b, v_w_up, v_conv_w, v_conv_b, v_w_down, v_ln2_g, v_ln2_b):
    pos = jnp.stack([lax.axis_index("c"), 2 * lax.axis_index("x") + lax.axis_index("y")]).astype(jnp.int32)
    order = ("w_in", "w_out", "w_up", "w_down")
    w_in_i, w_out_i, w_up_i, w_down_i = range(N_BIG)

    gathered = _gather_weights([w_in[0], w_out[0], w_up[0], w_down[0]], _pad_cw(conv_w[0]), (w_in_i,))
    cw_full = jnp.transpose(gathered[N_BIG][:, 0:3, 0:DOWN_SH], (1, 0, 2)).reshape(3, D_FF)
    up_half = SHARD_SHAPES[w_up_i][0] // 2
    up_a, up_b = (0, up_half // 2), (up_half // 2, up_half // 2)

    class MeshComm:
        def __init__(self):
            self.w = {i: gathered[i] for i in range(N_BIG)}
            self.g32, self.g16, self.recv_a, self.p32, self.p16, self.recv_b = {}, {}, {}, {}, {}, {}
            self.up_complete = False

        def weight(self, name):
            i = order.index(name)
            if name == "w_up" and not self.up_complete:
                (arrs, _), = _comm_only("gather_up_last", [_gather_rider({i: self.w[i]}, [("d2d", i, up_b)])])
                self.w[i], self.up_complete = arrs[0], True
            full = self.w[i]
            return full.reshape(-1, full.shape[-1]) if name in ("w_out", "w_down") else full

        def _gather(self, ws, ops):
            return _gather_rider({w: self.w[w] for w in ws}, ops), ("w", ws)

        def _pair(self, ws):
            return _pair_rider(ws, [self.g16[w] for w in ws]), ("recv_a", ws)

        def _chip(self, ws, rows=None, resume=False):
            landing = [self.recv_b[w] for w in ws] if resume else None
            return _chip_rider(ws, [self.p16[w] for w in ws], rows, landing), ("recv_b", ws)

        def plan(self, call):
            if call == "proj_pool":
                return [self._gather([w_out_i, w_down_i], [("ici", w_out_i, _whole_half(w_out_i)),
                                                          ("ici", w_down_i, _whole_half(w_down_i))])]
            if call == "retention_fwd":
                return [self._gather([w_out_i, w_up_i, w_down_i],
                                     [("d2d", w_out_i, _whole_half(w_out_i)),
                                      ("d2d", w_down_i, _whole_half(w_down_i)), ("ici", w_up_i, up_a)])]
            if call == "outproj_ln1":
                return [self._gather([w_up_i], [("ici", w_up_i, up_b), ("d2d", w_up_i, up_a)])]
            if call == "wgrad_down":
                return [self._pair([w_up_i, w_out_i])]
            if call == "mix_bwd":
                return [self._chip([w_up_i], up_a)]
            if call == "retention_bwd":
                return [self._chip([w_up_i], up_b, resume=True), self._pair([w_down_i])]
            if call == "wgrad_in":
                return [self._chip([w_out_i, w_down_i])]
            if call == "dx":
                return [self._chip([w_in_i])]
            return []

        def riders(self, call):
            self.pending = self.plan(call)
            return [r for r, _ in self.pending]

        def landed(self, call, results):
            for (_, (slot, ws)), (inplace, lands) in zip(self.pending, results):
                for w, arr in zip(ws, inplace if len(inplace) else lands):
                    getattr(self, slot)[w] = arr
            if call == "wgrad_down":
                self._sum([w_up_i, w_out_i])
            if call == "retention_bwd":
                self._sum([w_down_i])

        def _sum(self, ws):
            p32s, p16s = _pair_sum(pos, ws, [self.g32[w] for w in ws], [self.recv_a[w] for w in ws])
            for w, p32, p16 in zip(ws, p32s, p16s):
                self.p32[w], self.p16[w] = p32, p16

        def gradient(self, name, g32, g16):
            w = order.index(name)
            shape = (N_SHARD,) + SHARD_SHAPES[w]
            self.g32[w], self.g16[w] = g32.reshape(shape), g16.reshape(shape)
            if name == "w_in":
                (_, lands), = _comm_only("pair_exchange_in", [self._pair([w])[0]])
                self.recv_a[w] = lands[0]
                self._sum([w])

    comm = MeshComm()
    loss, grad_x, small = _local_step(x[0], loss_target[0], cw_full, conv_b, w_pool[0], pool_scale,
                                      ln1_g, ln1_b, ln2_g, ln2_b, comm)

    dcw4 = _pad_cw(jnp.transpose(small["conv_w"].reshape(3, N_SHARD, DOWN_SH), (1, 0, 2)))
    vec_names = ("conv_b", "pool_scale", "ln1_g", "ln1_b", "ln2_g", "ln2_b")
    given = dict(w_pool=w_pool, pool_scale=pool_scale, ln1_g=ln1_g, ln1_b=ln1_b, conv_w=conv_w, conv_b=conv_b,
                 ln2_g=ln2_g, ln2_b=ln2_b)
    given_m = dict(w_pool=m_w_pool, pool_scale=m_pool_scale, ln1_g=m_ln1_g, ln1_b=m_ln1_b, conv_w=m_conv_w,
                   conv_b=m_conv_b, ln2_g=m_ln2_g, ln2_b=m_ln2_b)
    given_v = dict(w_pool=v_w_pool, pool_scale=v_pool_scale, ln1_g=v_ln1_g, ln1_b=v_ln1_b, conv_w=v_conv_w,
                   conv_b=v_conv_b, ln2_g=v_ln2_g, ln2_b=v_ln2_b)
    args = []
    for src in (given, given_m, given_v):
        args += [src["w_pool"][0], _pad_cw(src["conv_w"][0]), [src[n] for n in vec_names]]
    loss_tot, small_out, landed = _small_update(loss, small["w_pool"], dcw4, [small[n] for n in vec_names], *args,
                                                riders=comm.riders("small_update"))
    comm.landed("small_update", landed)

    every = range(N_BIG)
    mine = _chip_sum(pos, [comm.p32[w] for w in every], [comm.recv_b[w] for w in every])
    (_, theirs), = _comm_only("pair_exchange_f32", [_final_rider(mine)])
    big_out = _adam_big(pos, mine, theirs, [w_in, w_out, w_up, w_down], [m_w_in, m_w_out, m_w_up, m_w_down],
                        [v_w_in, v_w_out, v_w_up, v_w_down])

    names = ("w_in", "w_pool", "pool_scale", "w_out", "ln1_g", "ln1_b", "w_up", "conv_w", "conv_b", "w_down",
             "ln2_g", "ln2_b")
    small_names = ("w_pool", "conv_w") + vec_names
    result = [loss_tot.reshape(()), grad_x[None]]
    for kind in range(4):
        for n in names:
            if n in order:
                result.append(big_out[kind][order.index(n)])
            else:
                val = small_out[kind][small_names.index(n)]
                if n == "conv_w":
                    val = val[0:3, 0:DOWN_SH][None]
                elif n == "w_pool":
                    val = val[None]
                result.append(val)
    return tuple(result)
```

```python
import functools
import math

import numpy as np
import jax
import jax.numpy as jnp
from jax import lax
from jax.experimental import pallas as pl
from jax.experimental.pallas import tpu as pltpu

F32 = jnp.float32
BF16 = jnp.bfloat16

D_MODEL = 1024
HEADS = 4
HEAD_DIM = 128
RET_W = HEADS * HEAD_DIM
POOL_WINDOWS = (2, 4, 8, 16)
POOL_W = 512
IN_W = 4 * RET_W + POOL_W
D_FF = 2816
N_SHARD = 4
IN_SH = IN_W // N_SHARD
UP_SH = 2 * D_FF // N_SHARD
DOWN_SH = D_FF // N_SHARD
OUT_SH = D_MODEL // N_SHARD
ROPE_BASE = 10000.0
LN_EPS = 1e-5
RMS_EPS = 1e-6
ALPHA = 2.0 ** 0.25
K_SCALE = HEAD_DIM ** -0.5
SUPER = 256
CHUNK = 64
POOL_HALO = 16
CONV_HALO = 8
FFN_STRIP = 128
LN_ROWS = 32

ADAM_LR = 0.001
ADAM_B1 = 0.9
ADAM_B2 = 0.999
ADAM_EPS = 1e-08
ADAM_WD = 0.01
ADAM_STEP = 10

MESH = pl.DeviceIdType.MESH
VMEM_LIMIT = 56 * 1024 * 1024


def _dot(a, b):
    return jnp.dot(a, b, preferred_element_type=F32)


def _dot_nt(a, b):
    return lax.dot_general(a, b, (((1,), (1,)), ((), ())), preferred_element_type=F32)


def _dot_tn(a, b):
    return lax.dot_general(a, b, (((0,), (0,)), ((), ())), preferred_element_type=F32)


def _sigmoid(x):
    return 1.0 / (1.0 + jnp.exp(-x))


def _params(sem):
    return pltpu.CompilerParams(dimension_semantics=sem, vmem_limit_bytes=VMEM_LIMIT)


def _whole():
    return pl.BlockSpec(memory_space=pltpu.VMEM)


HBM_SPEC = pl.BlockSpec(memory_space=pl.ANY)


class _Rider:
    def __init__(self, inplace, srcs, lands, n_copies, make):
        self.inplace, self.srcs, self.lands, self.n_copies, self.make = list(inplace), list(srcs), list(lands), n_copies, make


def _call(body, *, name, grid, in_specs, out_specs, out_shape, operands, scratch_shapes=(), sem=(),
          aliases=None, riders=()):
    n_in, n_out, n_scr = len(in_specs), len(out_shape), len(scratch_shapes)
    in_specs, out_specs, out_shape = list(in_specs), list(out_specs), list(out_shape)
    operands, scratch_shapes, aliases = list(operands), list(scratch_shapes), dict(aliases or {})
    for r in riders:
        for a in r.inplace:
            aliases[len(in_specs)] = len(out_shape)
            in_specs.append(HBM_SPEC)
            operands.append(a)
            out_specs.append(HBM_SPEC)
            out_shape.append(jax.ShapeDtypeStruct(a.shape, a.dtype))
        for a in r.srcs:
            in_specs.append(HBM_SPEC)
            operands.append(a)
        for shp in r.lands:
            out_specs.append(HBM_SPEC)
            out_shape.append(shp)
        scratch_shapes += [pltpu.SemaphoreType.DMA((r.n_copies,)), pltpu.SemaphoreType.DMA((r.n_copies,))]

    def full(*refs):
        ins = refs[:n_in]
        at = n_in
        r_srcs = []
        for r in riders:
            at += len(r.inplace)
            r_srcs.append(refs[at:at + len(r.srcs)])
            at += len(r.srcs)
        outs = refs[at:at + n_out]
        at += n_out
        r_outs = []
        for r in riders:
            r_outs.append((refs[at:at + len(r.inplace)], refs[at + len(r.inplace):at + len(r.inplace) + len(r.lands)]))
            at += len(r.inplace) + len(r.lands)
        scr = refs[at:at + n_scr]
        at += n_scr
        r_sems = [refs[at + 2 * i:at + 2 * i + 2] for i in range(len(riders))]

        def copies():
            return [r.make(r_outs[i][0], r_srcs[i], r_outs[i][1], r_sems[i][0], r_sems[i][1])
                    for i, r in enumerate(riders)]

        def start():
            for starts, _ in copies():
                for cp in starts:
                    cp.start()

        def finish():
            for _, waits in copies():
                for wait in waits:
                    wait()

        if riders and grid:
            first = functools.reduce(jnp.logical_and, [pl.program_id(d) == 0 for d in range(len(grid))])
            last = functools.reduce(jnp.logical_and, [pl.program_id(d) == grid[d] - 1 for d in range(len(grid))])
            pl.when(first)(start)
            body(*ins, *outs, *scr)
            pl.when(last)(finish)
        else:
            if riders:
                start()
            body(*ins, *outs, *scr)
            if riders:
                finish()

    params = _params(sem) if grid else pltpu.CompilerParams(vmem_limit_bytes=VMEM_LIMIT)
    res = pl.pallas_call(
        full, name=name, grid=grid, in_specs=in_specs, out_specs=out_specs, out_shape=out_shape,
        scratch_shapes=scratch_shapes, input_output_aliases=aliases, compiler_params=params,
    )(*operands)
    outs, at, rider_res = res[:n_out], n_out, []
    for r in riders:
        rider_res.append((res[at:at + len(r.inplace)], res[at + len(r.inplace):at + len(r.inplace) + len(r.lands)]))
        at += len(r.inplace) + len(r.lands)
    return list(outs), rider_res


def _gammas():
    return [1.0 - 2.0 ** (-5.0 - h) for h in range(HEADS)]


def _decay_tables():
    idx = np.arange(SUPER)
    dist = np.abs(idx[:, None] - idx[None, :]).astype(np.float64)
    visible = (idx[None, :] // CHUNK) <= (idx[:, None] // CHUNK)
    mask = np.stack([np.where(visible, g ** dist, 0.0) for g in _gammas()])
    qd = np.concatenate([np.repeat((g ** (idx + 1.0))[:, None], HEAD_DIM, 1) for g in _gammas()], 1)
    kd = np.concatenate([np.repeat((g ** (SUPER - 1.0 - idx))[:, None], HEAD_DIM, 1) for g in _gammas()], 1)
    return (jnp.asarray(mask, F32), jnp.asarray(qd, F32), jnp.asarray(kd, F32))


def _rope_tables(s):
    inv_freq = ROPE_BASE ** (-np.arange(0, HEAD_DIM, 2, dtype=np.float64) / HEAD_DIM)
    ang = np.arange(s, dtype=np.float64)[:, None] * inv_freq[None, :]
    cos, sin = np.cos(ang), np.sin(ang)
    return (jnp.asarray(np.concatenate([cos, cos], 1), F32),
            jnp.asarray(np.concatenate([-sin, sin], 1), F32))


def _rope(t, cosf, sinf):
    return t * cosf + pltpu.roll(t, HEAD_DIM // 2, 1) * sinf


def _rope_t(t, cosf, sinf):
    return t * cosf - pltpu.roll(t, HEAD_DIM // 2, 1) * sinf


def _layernorm_fwd(z):
    mu = jnp.mean(z, axis=-1, keepdims=True)
    zc = z - mu
    var = jnp.mean(zc * zc, axis=-1, keepdims=True)
    rstd = lax.rsqrt(var + LN_EPS)
    return zc * rstd, rstd


def _layernorm_bwd(dy, xhat, rstd, gain):
    dxh = dy * gain
    m1 = jnp.mean(dxh, axis=-1, keepdims=True)
    m2 = jnp.mean(dxh * xhat, axis=-1, keepdims=True)
    return rstd * (dxh - m1 - xhat * m2)


def _proj_pool(x, win4, cosf, sinf, wpool, pscale, ts, riders=()):
    s = x.shape[0]
    nt = s // ts

    def body(x_ref, w_ref, cos_ref, sin_ref, wp_ref, ps_ref,
             xb_ref, q_ref, k_ref, v_ref, g_ref, pooled_ref, cat_ref, proj_scr, pext_scr):
        i = pl.program_id(0)
        xb = x_ref[...].astype(BF16)
        xb_ref[...] = xb
        for j in range(N_SHARD):
            proj_scr[:, j * IN_SH:(j + 1) * IN_SH] = _dot(xb, w_ref[j])
        cosf_t = cos_ref[...]
        sinf_t = sin_ref[...]
        for h in range(HEADS):
            lo = h * HEAD_DIM
            q_ref[:, lo:lo + HEAD_DIM] = _rope(proj_scr[:, lo:lo + HEAD_DIM], cosf_t, sinf_t).astype(BF16)
            kk = _rope(proj_scr[:, RET_W + lo:RET_W + lo + HEAD_DIM], cosf_t, sinf_t) * K_SCALE
            k_ref[:, lo:lo + HEAD_DIM] = kk.astype(BF16)
        v_ref[...] = proj_scr[:, 2 * RET_W:3 * RET_W].astype(BF16)
        g_ref[...] = proj_scr[:, 3 * RET_W:4 * RET_W]

        @pl.when(i == 0)
        def _():
            pext_scr[0:POOL_HALO, :] = jnp.zeros((POOL_HALO, POOL_W), F32)

        pext_scr[POOL_HALO:POOL_HALO + ts, :] = proj_scr[:, 4 * RET_W:IN_W]
        pos = (i * ts + lax.broadcasted_iota(jnp.int32, (ts, 1), 0) + 1).astype(F32)
        for gi, w in enumerate(POOL_WINDOWS):
            lo = gi * HEAD_DIM
            ext = pext_scr[:, lo:lo + HEAD_DIM]
            acc = ext
            shift = 1
            while shift < w:
                acc = acc + pltpu.roll(acc, shift, 0)
                shift *= 2
            tok = ext[POOL_HALO:POOL_HALO + ts]
            pooled = acc[POOL_HALO:POOL_HALO + ts] / jnp.minimum(pos, float(w)) - tok
            pooled_b = pooled.astype(BF16)
            pooled_ref[:, lo:lo + HEAD_DIM] = pooled_b
            lin = _dot(pooled_b, wp_ref[gi])
            cat_ref[:, lo:lo + HEAD_DIM] = (lin * ps_ref[:, lo:lo + HEAD_DIM]).astype(BF16)
        pext_scr[0:POOL_HALO, :] = pext_scr[ts:ts + POOL_HALO, :]

    tile = lambda w: pl.BlockSpec((ts, w), lambda i: (i, 0))
    return _call(
        body, name="proj_pool", grid=(nt,),
        in_specs=[tile(D_MODEL), _whole(), tile(HEAD_DIM), tile(HEAD_DIM), _whole(), _whole()],
        out_specs=[tile(D_MODEL), tile(RET_W), tile(RET_W), tile(RET_W), tile(RET_W), tile(POOL_W),
                   pl.BlockSpec((ts, POOL_W), lambda i: (i, 1))],
        out_shape=[jax.ShapeDtypeStruct((s, D_MODEL), BF16), jax.ShapeDtypeStruct((s, RET_W), BF16),
                   jax.ShapeDtypeStruct((s, RET_W), BF16), jax.ShapeDtypeStruct((s, RET_W), BF16),
                   jax.ShapeDtypeStruct((s, RET_W), F32), jax.ShapeDtypeStruct((s, POOL_W), BF16),
                   jax.ShapeDtypeStruct((s, 2 * RET_W), BF16)],
        scratch_shapes=[pltpu.VMEM((ts, IN_W), F32), pltpu.VMEM((ts + POOL_HALO, POOL_W), F32)],
        sem=("arbitrary",), operands=(x, win4, cosf, sinf, wpool, pscale), riders=riders,
    )


def _retention_fwd(q, k, v, g, cat, mask, qd, kd, riders=()):
    s = q.shape[0]
    ns = s // SUPER
    cdec = [gm ** float(SUPER) for gm in _gammas()]

    def body(q_ref, k_ref, v_ref, g_ref, cat_in, mask_ref, qd_ref, kd_ref,
             ret_ref, cat_ref, st_ref, state_scr):
        del cat_in
        n = pl.program_id(0)

        @pl.when(n == 0)
        def _():
            state_scr[...] = jnp.zeros_like(state_scr)

        for h in range(HEADS):
            sl = slice(h * HEAD_DIM, (h + 1) * HEAD_DIM)
            qh, kh, vh = q_ref[:, sl], k_ref[:, sl], v_ref[:, sl]
            sc = _dot_nt(qh, kh) * mask_ref[h]
            st = state_scr[h]
            stb = st.astype(BF16)
            st_ref[0, h] = stb
            qdb = (qh.astype(F32) * qd_ref[:, sl]).astype(BF16)
            kdb = (kh.astype(F32) * kd_ref[:, sl]).astype(BF16)
            ret = _dot(sc.astype(BF16), vh) + _dot(qdb, stb)
            state_scr[h] = st * cdec[h] + _dot_tn(kdb, vh)
            ret_ref[:, sl] = ret
            r = lax.rsqrt(jnp.mean(ret * ret, axis=-1, keepdims=True) + RMS_EPS)
            gh = g_ref[:, sl]
            cat_ref[:, sl] = ((ret * r) * (gh * _sigmoid(gh))).astype(BF16)

    tile = pl.BlockSpec((SUPER, RET_W), lambda n: (n, 0))
    return _call(
        body, name="retention_fwd", grid=(ns,),
        in_specs=[tile, tile, tile, tile, HBM_SPEC, _whole(), _whole(), _whole()],
        out_specs=[tile, tile, pl.BlockSpec((1, HEADS, HEAD_DIM, HEAD_DIM), lambda n: (n, 0, 0, 0))],
        out_shape=[jax.ShapeDtypeStruct((s, RET_W), F32), jax.ShapeDtypeStruct((s, 2 * RET_W), BF16),
                   jax.ShapeDtypeStruct((ns, HEADS, HEAD_DIM, HEAD_DIM), BF16)],
        scratch_shapes=[pltpu.VMEM((HEADS, HEAD_DIM, HEAD_DIM), F32)],
        aliases={4: 1}, sem=("arbitrary",), operands=(q, k, v, g, cat, mask, qd, kd), riders=riders,
    )


def _outproj_ln1(x, cat, wout, g1, b1, ts, riders=()):
    s = x.shape[0]

    def body(x_ref, cat_ref, w_ref, g_ref, b_ref, xhat_ref, rstd_ref, h1b_ref):
        z = ALPHA * x_ref[...] + _dot(cat_ref[...], w_ref[...])
        xhat, rstd = _layernorm_fwd(z)
        xhat_ref[...] = xhat
        rstd_ref[...] = rstd
        h1b_ref[...] = (xhat * g_ref[...] + b_ref[...]).astype(BF16)

    tile = lambda w: pl.BlockSpec((ts, w), lambda i: (i, 0))
    return _call(
        body, name="outproj_ln1", grid=(s // ts,),
        in_specs=[tile(D_MODEL), tile(D_MODEL), _whole(), _whole(), _whole()],
        out_specs=[tile(D_MODEL), tile(1), tile(D_MODEL)],
        out_shape=[jax.ShapeDtypeStruct((s, D_MODEL), F32), jax.ShapeDtypeStruct((s, 1), F32),
                   jax.ShapeDtypeStruct((s, D_MODEL), BF16)],
        sem=("arbitrary",), operands=(x, cat, wout, g1, b1), riders=riders,
    )


def _ffn_fwd_loss(xhat1, h1b, target, wup4, wdown, cw, cb, g1, b1, g2, b2, ts):
    s = xhat1.shape[0]

    def body(xhat_ref, h1b_ref, tgt_ref, wup_ref, wdn_ref, cw_ref, cb_ref, g1_ref, b1_ref, g2_ref, b2_ref,
             ub_ref, dz2_ref, dz2b_ref, loss_ref, dg2_ref, db2_ref, val_scr, gext_scr, act_scr, ffn_scr):
        i = pl.program_id(0)

        @pl.when(i == 0)
        def _():
            gext_scr[0:CONV_HALO, :] = jnp.zeros((CONV_HALO, D_FF), F32)
            loss_ref[...] = jnp.zeros_like(loss_ref)
            dg2_ref[...] = jnp.zeros_like(dg2_ref)
            db2_ref[...] = jnp.zeros_like(db2_ref)

        hb = h1b_ref[...]
        for half in range(2):
            lo = half * UP_SH
            gext_scr[CONV_HALO:CONV_HALO + ts, lo:lo + UP_SH] = _dot(hb, wup_ref[2 + half])
            val_scr[:, lo:lo + UP_SH] = _dot(hb, wup_ref[half])
            for c0 in range(lo, lo + UP_SH, FFN_STRIP):
                cols = slice(c0, c0 + FFN_STRIP)
                ext = gext_scr[:, cols]
                gate = ext[CONV_HALO:]
                hc = cb_ref[:, cols] + ((pltpu.roll(ext, 2, 0)[CONV_HALO:] * cw_ref[0:1, cols]
                                         + pltpu.roll(ext, 1, 0)[CONV_HALO:] * cw_ref[1:2, cols])
                                        + gate * cw_ref[2:3, cols])
                val = val_scr[:, cols]
                act_scr[:, cols] = ((hc * _sigmoid(hc)) * val).astype(BF16)
                ub_ref[:, cols] = val.astype(BF16)
                ub_ref[:, D_FF + c0:D_FF + c0 + FFN_STRIP] = gate.astype(BF16)
            part = _dot(act_scr[:, lo:lo + UP_SH], wdn_ref[lo:lo + UP_SH, :])
            if half == 0:
                ffn_scr[...] = part
            else:
                ffn_scr[...] += part
        gext_scr[0:CONV_HALO, :] = gext_scr[ts:ts + CONV_HALO, :]

        loss_acc = jnp.zeros((1, 1), F32)
        dg2_acc = jnp.zeros((1, D_MODEL), F32)
        db2_acc = jnp.zeros((1, D_MODEL), F32)
        for r0 in range(0, ts, LN_ROWS):
            rows = slice(r0, r0 + LN_ROWS)
            h1 = xhat_ref[rows, :] * g1_ref[...] + b1_ref[...]
            xhat2, rstd2 = _layernorm_fwd(ALPHA * h1 + ffn_scr[rows, :])
            diff = (xhat2 * g2_ref[...] + b2_ref[...]) - tgt_ref[rows, :]
            row = jnp.mean(diff * diff, axis=-1, keepdims=True)
            loss_acc = loss_acc + 0.5 * jnp.sum(row, axis=0, keepdims=True)
            dy = diff * (1.0 / D_MODEL)
            dg2_acc = dg2_acc + jnp.sum(dy * xhat2, axis=0, keepdims=True)
            db2_acc = db2_acc + jnp.sum(dy, axis=0, keepdims=True)
            dz2 = _layernorm_bwd(dy, xhat2, rstd2, g2_ref[...])
            dz2_ref[rows, :] = dz2
            dz2b_ref[rows, :] = dz2.astype(BF16)
        loss_ref[...] += loss_acc
        dg2_ref[...] += dg2_acc
        db2_ref[...] += db2_acc

    tile = lambda w: pl.BlockSpec((ts, w), lambda i: (i, 0))
    acc = lambda w: pl.BlockSpec((1, w), lambda i: (0, 0))
    return pl.pallas_call(
        body, name="ffn_fwd_loss", grid=(s // ts,),
        in_specs=[tile(D_MODEL), tile(D_MODEL), tile(D_MODEL)] + [_whole()] * 8,
        out_specs=[tile(2 * D_FF), tile(D_MODEL), tile(D_MODEL), acc(1), acc(D_MODEL), acc(D_MODEL)],
        out_shape=[jax.ShapeDtypeStruct((s, 2 * D_FF), BF16), jax.ShapeDtypeStruct((s, D_MODEL), F32),
                   jax.ShapeDtypeStruct((s, D_MODEL), BF16),
                   jax.ShapeDtypeStruct((1, 1), F32), jax.ShapeDtypeStruct((1, D_MODEL), F32),
                   jax.ShapeDtypeStruct((1, D_MODEL), F32)],
        scratch_shapes=[pltpu.VMEM((ts, D_FF), F32), pltpu.VMEM((ts + CONV_HALO, D_FF), F32),
                        pltpu.VMEM((ts, D_FF), BF16), pltpu.VMEM((ts, D_MODEL), F32)],
        compiler_params=_params(("arbitrary",)),
    )(xhat1, h1b, target, wup4, wdown, cw, cb, g1, b1, g2, b2)


def _ffn_bwd(dz2, dz2b, ub, xhat1, rstd1, wup4, wdown, cw, cb, g1, ts):
    s = dz2.shape[0]
    nt = s // ts
    hb = 16

    def body(dz2_ref, dz2b_ref, ub_ref, prev_ref, xhat_ref, rstd_ref, wup_ref, wdn_ref, cw_ref, cb_ref, g1_ref,
             a_ref, dub_ref, dz1_ref, dz1b_ref, dg1_ref, db1_ref, dcw_ref, dcb_ref, gext_scr, dext_scr):
        i = pl.program_id(0)
        r = nt - 1 - i

        @pl.when(i == 0)
        def _():
            dext_scr[ts:ts + CONV_HALO, :] = jnp.zeros((CONV_HALO, D_FF), F32)
            dg1_ref[...] = jnp.zeros_like(dg1_ref)
            db1_ref[...] = jnp.zeros_like(db1_ref)
            dcw_ref[...] = jnp.zeros_like(dcw_ref)
            dcb_ref[...] = jnp.zeros_like(dcb_ref)

        da = _dot_nt(dz2b_ref[...], wdn_ref[...])
        val = ub_ref[:, 0:D_FF].astype(F32)
        gate = ub_ref[:, D_FF:2 * D_FF].astype(F32)
        prev = prev_ref[...].astype(F32)[hb - CONV_HALO:hb]
        gext_scr[0:CONV_HALO, :] = jnp.where(r == 0, 0.0, prev)
        gext_scr[CONV_HALO:CONV_HALO + ts, :] = gate
        ext = gext_scr[...]
        g2s = pltpu.roll(ext, 2, 0)[CONV_HALO:]
        g1s = pltpu.roll(ext, 1, 0)[CONV_HALO:]
        hc = cb_ref[...] + ((g2s * cw_ref[0:1, :] + g1s * cw_ref[1:2, :]) + gate * cw_ref[2:3, :])
        sg = _sigmoid(hc)
        si = hc * sg
        a_ref[...] = (si * val).astype(BF16)
        dhc = da * val * (sg * (1.0 + hc * (1.0 - sg)))
        dcb_ref[...] += jnp.sum(dhc, axis=0, keepdims=True)
        dcw_ref[0:1, :] += jnp.sum(dhc * g2s, axis=0, keepdims=True)
        dcw_ref[1:2, :] += jnp.sum(dhc * g1s, axis=0, keepdims=True)
        dcw_ref[2:3, :] += jnp.sum(dhc * gate, axis=0, keepdims=True)
        dext_scr[0:ts, :] = dhc
        dext = dext_scr[...]
        dgate = (dhc * cw_ref[2:3, :] + pltpu.roll(dext, ts + CONV_HALO - 1, 0)[0:ts] * cw_ref[1:2, :]
                 + pltpu.roll(dext, ts + CONV_HALO - 2, 0)[0:ts] * cw_ref[0:1, :])
        dext_scr[ts:ts + CONV_HALO, :] = dext_scr[0:CONV_HALO, :]
        dub_ref[:, 0:D_FF] = (da * si).astype(BF16)
        dub_ref[:, D_FF:2 * D_FF] = dgate.astype(BF16)
        dh1 = ALPHA * dz2_ref[...]
        for j in range(N_SHARD):
            dh1 = dh1 + _dot_nt(dub_ref[:, j * UP_SH:(j + 1) * UP_SH], wup_ref[j])
        xhat = xhat_ref[...]
        dg1_ref[...] += jnp.sum(dh1 * xhat, axis=0, keepdims=True)
        db1_ref[...] += jnp.sum(dh1, axis=0, keepdims=True)
        dz1 = _layernorm_bwd(dh1, xhat, rstd_ref[...], g1_ref[...])
        dz1_ref[...] = dz1
        dz1b_ref[...] = dz1.astype(BF16)

    tile = lambda w: pl.BlockSpec((ts, w), lambda i: (nt - 1 - i, 0))
    acc = lambda rws, w: pl.BlockSpec((rws, w), lambda i: (0, 0))
    prev_spec = pl.BlockSpec((hb, D_FF), lambda i: (jnp.maximum((nt - 1 - i) * (ts // hb) - 1, 0), 1))
    return pl.pallas_call(
        body, name="ffn_bwd", grid=(nt,),
        in_specs=[tile(D_MODEL), tile(D_MODEL), tile(2 * D_FF), prev_spec, tile(D_MODEL), tile(1)] + [_whole()] * 5,
        out_specs=[tile(D_FF), tile(2 * D_FF), tile(D_MODEL), tile(D_MODEL), acc(1, D_MODEL), acc(1, D_MODEL),
                   acc(3, D_FF), acc(1, D_FF)],
        out_shape=[jax.ShapeDtypeStruct((s, D_FF), BF16), jax.ShapeDtypeStruct((s, 2 * D_FF), BF16),
                   jax.ShapeDtypeStruct((s, D_MODEL), F32), jax.ShapeDtypeStruct((s, D_MODEL), BF16),
                   jax.ShapeDtypeStruct((1, D_MODEL), F32),
                   jax.ShapeDtypeStruct((1, D_MODEL), F32), jax.ShapeDtypeStruct((3, D_FF), F32),
                   jax.ShapeDtypeStruct((1, D_FF), F32)],
        scratch_shapes=[pltpu.VMEM((ts + CONV_HALO, D_FF), F32), pltpu.VMEM((ts + CONV_HALO, D_FF), F32)],
        compiler_params=_params(("arbitrary",)),
    )(dz2, dz2b, ub, ub, xhat1, rstd1, wup4, wdown, cw, cb, g1)


def _mix_bwd(dz1, pooled, ret, g, wout, wpool, pscale, ts, riders=()):
    s = dz1.shape[0]
    nt = s // ts

    def body(dz1_ref, pooled_ref, ret_ref, g_ref, wout_ref, wp_ref, ps_ref,
             dret_ref, dgp_ref, dwp_ref, dps_ref, eext_scr):
        i = pl.program_id(0)
        r = nt - 1 - i

        @pl.when(i == 0)
        def _():
            eext_scr[ts:ts + POOL_HALO, :] = jnp.zeros((POOL_HALO, POOL_W), F32)
            dwp_ref[...] = jnp.zeros_like(dwp_ref)
            dps_ref[...] = jnp.zeros_like(dps_ref)

        dzb = dz1_ref[...].astype(BF16)
        dcat_r = _dot_nt(dzb, wout_ref[0:RET_W, :])
        dcat_p = _dot_nt(dzb, wout_ref[RET_W:2 * RET_W, :])
        pos = (r * ts + lax.broadcasted_iota(jnp.int32, (ts, 1), 0) + 1).astype(F32)
        dpooled = []
        for gi, w in enumerate(POOL_WINDOWS):
            sl = slice(gi * HEAD_DIM, (gi + 1) * HEAD_DIM)
            pb = pooled_ref[:, sl]
            dy = dcat_p[:, sl]
            dps_ref[:, sl] += jnp.sum(dy * _dot(pb, wp_ref[gi]), axis=0, keepdims=True)
            dlin = (dy * ps_ref[:, sl]).astype(BF16)
            dwp_ref[gi] += _dot_tn(pb, dlin)
            dpg = _dot_nt(dlin, wp_ref[gi])
            dpooled.append(dpg)
            eext_scr[0:ts, sl] = dpg / jnp.minimum(pos, float(w))
        for gi, w in enumerate(POOL_WINDOWS):
            sl = slice(gi * HEAD_DIM, (gi + 1) * HEAD_DIM)
            acc = eext_scr[:, sl]
            shift = 1
            while shift < w:
                acc = acc + pltpu.roll(acc, ts + POOL_HALO - shift, 0)
                shift *= 2
            dgp_ref[:, RET_W + gi * HEAD_DIM:RET_W + (gi + 1) * HEAD_DIM] = (acc[0:ts] - dpooled[gi]).astype(BF16)
        eext_scr[ts:ts + POOL_HALO, :] = eext_scr[0:POOL_HALO, :]
        for h in range(HEADS):
            sl = slice(h * HEAD_DIM, (h + 1) * HEAD_DIM)
            rt = ret_ref[:, sl]
            rr = lax.rsqrt(jnp.mean(rt * rt, axis=-1, keepdims=True) + RMS_EPS)
            rn = rt * rr
            gh = g_ref[:, sl]
            sg = _sigmoid(gh)
            dy = dcat_r[:, sl]
            dgp_ref[:, sl] = (dy * rn * (sg * (1.0 + gh * (1.0 - sg)))).astype(BF16)
            drn = dy * (gh * sg)
            dret_ref[:, sl] = (rr * (drn - rn * jnp.mean(drn * rn, axis=-1, keepdims=True))).astype(BF16)

    tile = lambda w: pl.BlockSpec((ts, w), lambda i: (nt - 1 - i, 0))
    return _call(
        body, name="mix_bwd", grid=(nt,),
        in_specs=[tile(D_MODEL), tile(POOL_W), tile(RET_W), tile(RET_W), _whole(), _whole(), _whole()],
        out_specs=[tile(RET_W), tile(2 * RET_W),
                   pl.BlockSpec((len(POOL_WINDOWS), HEAD_DIM, HEAD_DIM), lambda i: (0, 0, 0)),
                   pl.BlockSpec((1, POOL_W), lambda i: (0, 0))],
        out_shape=[jax.ShapeDtypeStruct((s, RET_W), BF16), jax.ShapeDtypeStruct((s, 2 * RET_W), BF16),
                   jax.ShapeDtypeStruct((len(POOL_WINDOWS), HEAD_DIM, HEAD_DIM), F32),
                   jax.ShapeDtypeStruct((1, POOL_W), F32)],
        scratch_shapes=[pltpu.VMEM((ts + POOL_HALO, POOL_W), F32)],
        sem=("arbitrary",), operands=(dz1, pooled, ret, g, wout, wpool, pscale), riders=riders,
    )


def _retention_bwd(q, k, v, dret, dgp, states, mask, qd, kd, cosf, sinf, riders=()):
    s = q.shape[0]
    ns = s // SUPER
    cdec = [gm ** float(SUPER) for gm in _gammas()]

    def body(q_ref, k_ref, v_ref, do_ref, dgp_ref, st_ref, mask_ref, qd_ref, kd_ref, cos_ref, sin_ref,
             dproj_ref, dstate_scr):
        i = pl.program_id(0)

        @pl.when(i == 0)
        def _():
            dstate_scr[...] = jnp.zeros_like(dstate_scr)

        cosf_t = cos_ref[...]
        sinf_t = sin_ref[...]
        for h in range(HEADS):
            sl = slice(h * HEAD_DIM, (h + 1) * HEAD_DIM)
            qh, kh, vh, doh = q_ref[:, sl], k_ref[:, sl], v_ref[:, sl], do_ref[:, sl]
            m = mask_ref[h]
            scb = (_dot_nt(qh, kh) * m).astype(BF16)
            dscb = (_dot_nt(doh, vh) * m).astype(BF16)
            stb = st_ref[0, h]
            dst = dstate_scr[h]
            dstb = dst.astype(BF16)
            qdb = (qh.astype(F32) * qd_ref[:, sl]).astype(BF16)
            kdb = (kh.astype(F32) * kd_ref[:, sl]).astype(BF16)
            dq = _dot(dscb, kh) + _dot_nt(doh, stb) * qd_ref[:, sl]
            dk = _dot_tn(dscb, qh) + _dot_nt(vh, dstb) * kd_ref[:, sl]
            dv = _dot_tn(scb, doh) + _dot(kdb, dstb)
            dstate_scr[h] = dst * cdec[h] + _dot_tn(qdb, doh)
            lo = h * HEAD_DIM
            dproj_ref[:, lo:lo + HEAD_DIM] = _rope_t(dq, cosf_t, sinf_t).astype(BF16)
            dproj_ref[:, RET_W + lo:RET_W + lo + HEAD_DIM] = _rope_t(dk * K_SCALE, cosf_t, sinf_t).astype(BF16)
            dproj_ref[:, 2 * RET_W + lo:2 * RET_W + lo + HEAD_DIM] = dv.astype(BF16)
        dproj_ref[:, 3 * RET_W:IN_W] = dgp_ref[...]

    tile = lambda w: pl.BlockSpec((SUPER, w), lambda i: (ns - 1 - i, 0))
    return _call(
        body, name="retention_bwd", grid=(ns,),
        in_specs=[tile(RET_W), tile(RET_W), tile(RET_W), tile(RET_W), tile(2 * RET_W),
                  pl.BlockSpec((1, HEADS, HEAD_DIM, HEAD_DIM), lambda i: (ns - 1 - i, 0, 0, 0)),
                  _whole(), _whole(), _whole(), tile(HEAD_DIM), tile(HEAD_DIM)],
        out_specs=[tile(IN_W)],
        out_shape=[jax.ShapeDtypeStruct((s, IN_W), BF16)],
        scratch_shapes=[pltpu.VMEM((HEADS, HEAD_DIM, HEAD_DIM), F32)],
        sem=("arbitrary",), operands=(q, k, v, dret, dgp, states, mask, qd, kd, cosf, sinf), riders=riders,
    )


def _dx(dz1, dproj, win4, ts, riders=()):
    s = dz1.shape[0]

    def body(dz1_ref, dp_ref, w_ref, dx_ref):
        acc = ALPHA * dz1_ref[...]
        for j in range(N_SHARD):
            acc = acc + _dot_nt(dp_ref[:, j * IN_SH:(j + 1) * IN_SH], w_ref[j])
        dx_ref[...] = acc

    tile = lambda w: pl.BlockSpec((ts, w), lambda i: (i, 0))
    return _call(
        body, name="dx", grid=(s // ts,),
        in_specs=[tile(D_MODEL), tile(IN_W), _whole()],
        out_specs=[tile(D_MODEL)],
        out_shape=[jax.ShapeDtypeStruct((s, D_MODEL), F32)],
        sem=("arbitrary",), operands=(dz1, dproj, win4), riders=riders,
    )


def _wgrad(a, b, tm, tn, name, stacked, m_outer, riders=()):
    s, m = a.shape
    n = b.shape[1]

    def body(a_ref, b_ref, o32_ref, o16_ref):
        res = _dot_tn(a_ref[...], b_ref[...])
        o32_ref[...] = res.reshape(o32_ref.shape)
        o16_ref[...] = res.astype(BF16).reshape(o16_ref.shape)

    if m_outer:
        grid, blocks = (m // tm, n // tn), (lambda g0, g1: (g0, g1))
    else:
        grid, blocks = (n // tn, m // tm), (lambda g0, g1: (g1, g0))
    if stacked:
        shape = (n // tn, m, tn)
        ospec = pl.BlockSpec((1, tm, tn), lambda g0, g1: (blocks(g0, g1)[1], blocks(g0, g1)[0], 0))
    else:
        shape = (m, n)
        ospec = pl.BlockSpec((tm, tn), lambda g0, g1: blocks(g0, g1))
    return _call(
        body, name=name, grid=grid,
        in_specs=[pl.BlockSpec((s, tm), lambda g0, g1: (0, blocks(g0, g1)[0])),
                  pl.BlockSpec((s, tn), lambda g0, g1: (0, blocks(g0, g1)[1]))],
        out_specs=[ospec, ospec],
        out_shape=[jax.ShapeDtypeStruct(shape, F32), jax.ShapeDtypeStruct(shape, BF16)],
        sem=("arbitrary", "arbitrary"), operands=(a, b), riders=riders,
    )


class _NoComm:
    def __init__(self, win4, wout, wup4, wdown):
        self.weights = dict(w_in=win4, w_out=wout, w_up=wup4, w_down=wdown)
        self.grads = {}

    def weight(self, name):
        return self.weights[name]

    def riders(self, call):
        return ()

    def landed(self, call, results):
        pass

    def gradient(self, name, g32, g16):
        self.grads[name] = (g32, g16)


def _local_step(x, target, cw, cb, wpool, pscale, g1, b1, g2, b2, comm):
    s = x.shape[0]
    ts_a = min(512, s)
    ts_f = min(256, s)
    mask, qd, kd = _decay_tables()
    cosf, sinf = _rope_tables(s)
    wpool_b = wpool.astype(BF16)

    def run(call, fn, *args):
        outs, res = fn(*args, riders=comm.riders(call))
        comm.landed(call, res)
        return outs

    xb, q, k, v, g, pooled, cat = run("proj_pool", _proj_pool, x, comm.weight("w_in"), cosf, sinf, wpool_b,
                                      pscale, ts_a)
    ret, cat, states = run("retention_fwd", _retention_fwd, q, k, v, g, cat, mask, qd, kd)
    wout = comm.weight("w_out")
    xhat1, rstd1, h1b = run("outproj_ln1", _outproj_ln1, x, cat, wout, g1, b1, ts_a)
    wup4, wdown = comm.weight("w_up"), comm.weight("w_down")
    ub, dz2, dz2b, loss, dg2, db2 = _ffn_fwd_loss(xhat1, h1b, target, wup4, wdown, cw, cb, g1, b1, g2, b2, ts_f)

    act, dub, dz1, dz1b, dg1, db1, dcw, dcb = _ffn_bwd(dz2, dz2b, ub, xhat1, rstd1, wup4, wdown, cw, cb, g1, ts_f)
    half = D_MODEL // 2
    comm.gradient("w_up", *run("wgrad_up", _wgrad, h1b, dub, half, UP_SH, "wgrad_up", True, False))
    comm.gradient("w_out", *run("wgrad_out", _wgrad, cat, dz1b, D_MODEL, half, "wgrad_out", False, True))
    comm.gradient("w_down", *run("wgrad_down", _wgrad, act, dz2b, D_FF // 2, half, "wgrad_down", False, True))
    dret, dgp, dwp, dps = run("mix_bwd", _mix_bwd, dz1b, pooled, ret, g, wout, wpool_b, pscale, ts_a)
    dproj, = run("retention_bwd", _retention_bwd, q, k, v, dret, dgp, states, mask, qd, kd, cosf, sinf)
    comm.gradient("w_in", *run("wgrad_in", _wgrad, xb, dproj, D_MODEL, IN_SH, "wgrad_in", True, True))
    grad_x, = run("dx", _dx, dz1, dproj, comm.weight("w_in"), ts_a)
    small = dict(w_pool=dwp, pool_scale=dps, ln1_g=dg1, ln1_b=db1, conv_w=dcw, conv_b=dcb,
                 ln2_g=dg2, ln2_b=db2)
    return loss, grad_x, small


CAST_ROWS = 64
SHARD_SHAPES = ((D_MODEL, IN_SH), (OUT_SH, D_MODEL), (D_MODEL, UP_SH), (DOWN_SH, D_MODEL))
N_BIG = len(SHARD_SHAPES)
CW_PAD = (8, 768)


def _mesh_pos():
    return lax.axis_index("x"), lax.axis_index("y"), lax.axis_index("c")


def _other_chips(x, y):
    return [(1 - x, y), (x, 1 - y), (1 - x, 1 - y)]


def _half_rows(w, which):
    hr = SHARD_SHAPES[w][0] // 2
    return pl.ds(pl.multiple_of(which * hr, 16), hr)


def _gather_weights(shards, cw8, full):
    def body(*refs):
        in_refs = refs[:N_BIG]
        cw_ref = refs[N_BIG]
        out_refs = refs[N_BIG + 1:2 * N_BIG + 1]
        cwo_ref = refs[2 * N_BIG + 1]
        stage = refs[2 * N_BIG + 2:3 * N_BIG + 2]
        send_sems, recv_sems, fsend_sems, frecv_sems, cw_send, cw_recv, local_sems = refs[3 * N_BIG + 2:]
        x, y, c = _mesh_pos()
        j0 = 2 * x + y
        chips = _other_chips(x, y)

        def cast_to_stage(w):
            def cast(i, carry):
                rows = pl.ds(pl.multiple_of(i * CAST_ROWS, CAST_ROWS), CAST_ROWS)
                stage[w][rows, :] = in_refs[w][rows, :].astype(BF16)
                return carry
            lax.fori_loop(0, SHARD_SHAPES[w][0] // CAST_ROWS, cast, 0)

        for w in full:
            cast_to_stage(w)

        jx, jy, jd = 2 * (1 - x) + y, 2 * x + (1 - y), 2 * (1 - x) + (1 - y)
        neighbours = [((1 - x, y, c), jx), ((x, 1 - y, c), jy)]
        passed = jnp.where(c == 0, jx, jy)
        pass_to = (jnp.where(c == 0, x, 1 - x), jnp.where(c == 0, 1 - y, y), c)

        def nbr(w, k, block):
            return pltpu.make_async_remote_copy(
                src_ref=stage[w].at[_half_rows(w, c), :], dst_ref=out_refs[w].at[block, _half_rows(w, c), :],
                send_sem=send_sems.at[w, k], recv_sem=recv_sems.at[w, k],
                device_id=neighbours[k][0], device_id_type=MESH)

        def relay(w, block):
            return pltpu.make_async_remote_copy(
                src_ref=out_refs[w].at[passed, _half_rows(w, c), :],
                dst_ref=out_refs[w].at[block, _half_rows(w, c), :],
                send_sem=send_sems.at[w, 2], recv_sem=recv_sems.at[w, 2],
                device_id=pass_to, device_id_type=MESH)

        def d2d(w, k, block, half):
            return pltpu.make_async_remote_copy(
                src_ref=out_refs[w].at[block, _half_rows(w, half), :],
                dst_ref=out_refs[w].at[block, _half_rows(w, half), :],
                send_sem=fsend_sems.at[w, k], recv_sem=frecv_sems.at[w, k],
                device_id=(x, y, 1 - c), device_id_type=MESH)

        def conv(k, block):
            chip = chips[k]
            return pltpu.make_async_remote_copy(
                src_ref=cw_ref, dst_ref=cwo_ref.at[block], send_sem=cw_send.at[k], recv_sem=cw_recv.at[k],
                device_id=(chip[0], chip[1], c), device_id_type=MESH)

        sent = [nbr(w, k, j0) for w in full for k in range(2)] + [conv(k, j0) for k in range(3)]
        for cp in sent:
            cp.start()
        for w in range(N_BIG):
            if w not in full:
                cast_to_stage(w)
        local = [pltpu.make_async_copy(stage[w], out_refs[w].at[j0], local_sems.at[w]) for w in range(N_BIG)]
        local.append(pltpu.make_async_copy(cw_ref, cwo_ref.at[j0], local_sems.at[N_BIG]))
        for cp in local:
            cp.start()
        for w in full:
            for k, (_, block) in enumerate(neighbours):
                nbr(w, k, block).wait_recv()
            later = [relay(w, passed)] + [d2d(w, k, block, c) for k, (_, block) in enumerate(neighbours)]
            for cp in later:
                cp.start()
            sent += later
        for w in full:
            relay(w, jd).wait_recv()
            fw = d2d(w, 2, jd, c)
            fw.start()
            sent.append(fw)
        for w in full:
            for k, block in enumerate([jx, jy, jd]):
                d2d(w, k, block, 1 - c).wait_recv()
        for k, chip in enumerate(chips):
            conv(k, 2 * chip[0] + chip[1]).wait_recv()
        for cp in sent:
            cp.wait_send()
        for cp in local:
            cp.wait()

    out_shape = [jax.ShapeDtypeStruct((N_SHARD,) + shp, BF16) for shp in SHARD_SHAPES]
    out_shape.append(jax.ShapeDtypeStruct((N_SHARD,) + CW_PAD, F32))
    return pl.pallas_call(
        body, name="gather_weights",
        in_specs=[_whole()] * (N_BIG + 1),
        out_specs=[HBM_SPEC] * (N_BIG + 1),
        out_shape=out_shape,
        scratch_shapes=[pltpu.VMEM(shp, BF16) for shp in SHARD_SHAPES] + [
            pltpu.SemaphoreType.DMA((N_BIG, 3)), pltpu.SemaphoreType.DMA((N_BIG, 3)),
            pltpu.SemaphoreType.DMA((N_BIG, 3)), pltpu.SemaphoreType.DMA((N_BIG, 3)),
            pltpu.SemaphoreType.DMA((3,)), pltpu.SemaphoreType.DMA((3,)),
            pltpu.SemaphoreType.DMA((N_BIG + 1,))],
        compiler_params=pltpu.CompilerParams(vmem_limit_bytes=VMEM_LIMIT),
    )(*shards, cw8)


def _gather_rider(arrays, ops):
    ws = sorted(arrays)

    def make(inplace, srcs, lands, send_sems, recv_sems):
        del srcs, lands
        x, y, c = _mesh_pos()
        j0, jx, jy, jd = 2 * x + y, 2 * (1 - x) + y, 2 * x + (1 - y), 2 * (1 - x) + (1 - y)
        x_nbr, y_nbr, sibling = (1 - x, y, c), (x, 1 - y, c), (x, y, 1 - c)
        starts, waits = [], []
        for n, (kind, w, (r0, nr)) in enumerate(ops):
            ref = inplace[ws.index(w)]
            hr = SHARD_SHAPES[w][0] // 2
            rows = lambda core: pl.ds(pl.multiple_of(core * hr + r0, 16), nr)
            mine, theirs = rows(c), rows(1 - c)
            if kind == "ici":
                moves = [(ref.at[j0, mine, :], x_nbr, ref.at[jx, mine, :]),
                         (ref.at[j0, mine, :], y_nbr, ref.at[jy, mine, :]),
                         (ref.at[j0, mine, :], (1 - x, 1 - y, c), ref.at[jd, mine, :])]
            elif kind == "nbr":
                moves = [(ref.at[j0, mine, :], x_nbr, ref.at[jx, mine, :]),
                         (ref.at[j0, mine, :], y_nbr, ref.at[jy, mine, :])]
            elif kind == "relay":
                passed = jnp.where(c == 0, jx, jy)
                to = (jnp.where(c == 0, x, 1 - x), jnp.where(c == 0, 1 - y, y), c)
                moves = [(ref.at[passed, mine, :], to, ref.at[jd, mine, :])]
            else:
                blocks = dict(d2d=[jx, jy, jd], d2d_nbr=[jx, jy], d2d_diag=[jd])[kind]
                moves = [(ref.at[b, mine, :], sibling, ref.at[b, theirs, :]) for b in blocks]
            for k, (src, to, landing) in enumerate(moves):
                sems = dict(send_sem=send_sems.at[3 * n + k], recv_sem=recv_sems.at[3 * n + k],
                            device_id=to, device_id_type=MESH)
                send = pltpu.make_async_remote_copy(src_ref=src, dst_ref=src, **sems)
                arrival = pltpu.make_async_remote_copy(src_ref=src, dst_ref=landing, **sems)
                starts.append(send)
                waits += [arrival.wait_recv, send.wait_send]
        return starts, waits

    return _Rider([arrays[w] for w in ws], [], [], 3 * len(ops), make)


def _whole_half(w):
    return (0, SHARD_SHAPES[w][0] // 2)


def _pair_rider(ws, g16s):
    def make(inplace, srcs, lands, send_sems, recv_sems):
        del inplace
        x, y, c = _mesh_pos()
        copies = [pltpu.make_async_remote_copy(
            src_ref=srcs[i].at[:, _half_rows(w, 1 - c), :], dst_ref=lands[i],
            send_sem=send_sems.at[i], recv_sem=recv_sems.at[i], device_id=(x, y, 1 - c), device_id_type=MESH)
            for i, w in enumerate(ws)]
        return copies, [cp.wait for cp in copies]

    lands = [jax.ShapeDtypeStruct((N_SHARD, SHARD_SHAPES[w][0] // 2, SHARD_SHAPES[w][1]), BF16) for w in ws]
    return _Rider([], g16s, lands, len(ws), make)


def _chip_rider(ws, p16s, rows=None, landing=None):
    def make(inplace, srcs, lands, send_sems, recv_sems):
        x, y, c = _mesh_pos()
        dsts = inplace if landing is not None else lands
        copies = []
        for i, w in enumerate(ws):
            r0, nr = rows if rows is not None else _whole_half(w)
            for k, chip in enumerate(_other_chips(x, y)):
                copies.append(pltpu.make_async_remote_copy(
                    src_ref=srcs[i].at[2 * chip[0] + chip[1], pl.ds(r0, nr), :],
                    dst_ref=dsts[i].at[k, pl.ds(r0, nr), :],
                    send_sem=send_sems.at[3 * i + k], recv_sem=recv_sems.at[3 * i + k],
                    device_id=(chip[0], chip[1], c), device_id_type=MESH))
        return copies, [cp.wait for cp in copies]

    lands = [jax.ShapeDtypeStruct((3, SHARD_SHAPES[w][0] // 2, SHARD_SHAPES[w][1]), BF16) for w in ws]
    if landing is not None:
        return _Rider(landing, p16s, [], 3 * len(ws), make)
    return _Rider([], p16s, lands, 3 * len(ws), make)


def _final_rider(halves):
    def make(inplace, srcs, lands, send_sems, recv_sems):
        del inplace
        x, y, c = _mesh_pos()
        copies = [pltpu.make_async_remote_copy(
            src_ref=srcs[i], dst_ref=lands[i], send_sem=send_sems.at[i], recv_sem=recv_sems.at[i],
            device_id=(x, y, 1 - c), device_id_type=MESH) for i in range(len(halves))]
        return copies, [cp.wait for cp in copies]

    return _Rider([], halves, [jax.ShapeDtypeStruct(h.shape, h.dtype) for h in halves], len(halves), make)


def _comm_only(name, riders):
    _, res = _call(lambda: None, name=name, grid=(), in_specs=[], out_specs=[], out_shape=[], operands=(),
                   riders=riders)
    return res


def _pair_sum(pos, ws, g32s, recvs):
    n = len(ws)

    def body(pos_ref, *refs):
        del pos_ref
        g_refs, r_refs = refs[:n], refs[n:2 * n]
        p32_refs, p16_refs = refs[2 * n:3 * n], refs[3 * n:]
        for i in range(n):
            tot = g_refs[i][...] + r_refs[i][...].astype(F32)
            p32_refs[i][...] = tot
            p16_refs[i][...] = tot.astype(BF16)

    halves = [(SHARD_SHAPES[w][0] // 2, SHARD_SHAPES[w][1]) for w in ws]
    own = [pl.BlockSpec((None, None) + h, lambda j, pos_ref: (j, pos_ref[0], 0, 0)) for h in halves]
    blk = [pl.BlockSpec((None,) + h, lambda j, pos_ref: (j, 0, 0)) for h in halves]
    g4 = [g.reshape((N_SHARD, 2) + h) for g, h in zip(g32s, halves)]
    outs = pl.pallas_call(
        body, name="pair_sum_" + "_".join(str(w) for w in ws),
        grid_spec=pltpu.PrefetchScalarGridSpec(
            num_scalar_prefetch=1, grid=(N_SHARD,), in_specs=own + blk, out_specs=blk + blk),
        out_shape=[jax.ShapeDtypeStruct((N_SHARD,) + h, F32) for h in halves]
        + [jax.ShapeDtypeStruct((N_SHARD,) + h, BF16) for h in halves],
        compiler_params=_params(("arbitrary",)),
    )(pos, *g4, *recvs)
    return outs[:n], outs[n:]


def _chip_sum(pos, p32s, recvs):
    parts = 2

    def body(pos_ref, *refs):
        del pos_ref
        p_refs, r_refs, f_refs = refs[:N_BIG], refs[N_BIG:2 * N_BIG], refs[2 * N_BIG:]
        for w in range(N_BIG):
            f_refs[w][...] = ((p_refs[w][...] + r_refs[w][0].astype(F32)) + r_refs[w][1].astype(F32)) \
                + r_refs[w][2].astype(F32)

    quarters = [(r // 2 // parts, cc) for r, cc in SHARD_SHAPES]
    own = [pl.BlockSpec((None,) + qt, lambda i, pos_ref: (pos_ref[1], i, 0)) for qt in quarters]
    rcv = [pl.BlockSpec((3,) + qt, lambda i, pos_ref: (0, i, 0)) for qt in quarters]
    out = [pl.BlockSpec(qt, lambda i, pos_ref: (i, 0)) for qt in quarters]
    return pl.pallas_call(
        body, name="chip_sum",
        grid_spec=pltpu.PrefetchScalarGridSpec(
            num_scalar_prefetch=1, grid=(parts,), in_specs=own + rcv, out_specs=out),
        out_shape=[jax.ShapeDtypeStruct((r // 2, cc), F32) for r, cc in SHARD_SHAPES],
        compiler_params=_params(("arbitrary",)),
    )(pos, *p32s, *recvs)


def _adamw(w, g, m, v):
    m_new = ADAM_B1 * m + (1.0 - ADAM_B1) * g
    v_new = ADAM_B2 * v + (1.0 - ADAM_B2) * (g * g)
    m_hat = m_new / (1.0 - ADAM_B1 ** ADAM_STEP)
    v_hat = v_new / (1.0 - ADAM_B2 ** ADAM_STEP)
    delta = -ADAM_LR * (m_hat / (jnp.sqrt(v_hat) + ADAM_EPS) + ADAM_WD * w)
    return delta, m_new, v_new


def _adam_big(pos, mine, theirs, ws, ms, vs):
    nb = 4

    def body(pos_ref, *refs):
        hf = pl.program_id(0)
        groups = [refs[i * N_BIG:(i + 1) * N_BIG] for i in range(9)]
        f_refs, t_refs, w_refs, m_refs, v_refs, go_refs, do_refs, mo_refs, vo_refs = groups
        for w in range(N_BIG):
            g = jnp.where(hf == pos_ref[0], f_refs[w][...], t_refs[w][...])
            delta, m_new, v_new = _adamw(w_refs[w][...], g, m_refs[w][...], v_refs[w][...])
            go_refs[w][...] = g
            do_refs[w][...] = delta
            mo_refs[w][...] = m_new
            vo_refs[w][...] = v_new

    blocks = [(r // 2 // nb, cc) for r, cc in SHARD_SHAPES]
    half = [pl.BlockSpec(b, lambda hf, i, pos_ref: (i, 0)) for b in blocks]
    full = [pl.BlockSpec((None,) + b, lambda hf, i, pos_ref: (0, hf * nb + i, 0)) for b in blocks]
    shapes = [jax.ShapeDtypeStruct((1,) + shp, F32) for shp in SHARD_SHAPES]
    outs = pl.pallas_call(
        body, name="adam_big",
        grid_spec=pltpu.PrefetchScalarGridSpec(
            num_scalar_prefetch=1, grid=(2, nb), in_specs=half + half + full * 3, out_specs=full * 4),
        out_shape=shapes * 4,
        compiler_params=_params(("arbitrary", "arbitrary")),
    )(pos, *mine, *theirs, *ws, *ms, *vs)
    return [outs[i * N_BIG:(i + 1) * N_BIG] for i in range(4)]


SMALL_ROWS = 8
ROW_CONV_B, ROW_POOL_SCALE, ROW_LN1_G, ROW_LN1_B, ROW_LN2_G, ROW_LN2_B, ROW_LOSS = range(7)
SMALL_VECS = ((ROW_CONV_B, D_FF), (ROW_POOL_SCALE, POOL_W), (ROW_LN1_G, D_MODEL), (ROW_LN1_B, D_MODEL),
              (ROW_LN2_G, D_MODEL), (ROW_LN2_B, D_MODEL))


def _small_update(loss, dwp, dcw4, vec_grads, wp, cwp, vec_ws, m_wp, m_cwp, vec_ms, v_wp, v_cwp, vec_vs,
                  riders=()):
    nv = len(SMALL_VECS)

    def body(*refs):
        loss_ref, dwp_ref, dcw_ref = refs[0:3]
        gvec = refs[3:3 + nv]
        o = 3 + nv
        wp_ref, cw_ref = refs[o:o + 2]
        wvec = refs[o + 2:o + 2 + nv]
        o += 2 + nv
        mwp_ref, mcw_ref = refs[o:o + 2]
        mvec = refs[o + 2:o + 2 + nv]
        o += 2 + nv
        vwp_ref, vcw_ref = refs[o:o + 2]
        vvec = refs[o + 2:o + 2 + nv]
        o += 2 + nv
        loss_out = refs[o]
        outs = refs[o + 1:o + 1 + 4 * (2 + nv)]
        o += 1 + 4 * (2 + nv)
        (vec_scr, sib_a, sib_b, sib_c, all_a, all_b, all_c,
         send1, recv1, send2, recv2) = refs[o:]
        x, y, c = _mesh_pos()
        j0 = 2 * x + y
        chips = _other_chips(x, y)

        vec_scr[...] = jnp.zeros_like(vec_scr)
        for (row, n), ref in zip(SMALL_VECS, gvec):
            vec_scr[row:row + 1, 0:n] = ref[...]
        vec_scr[ROW_LOSS:ROW_LOSS + 1, 0:HEAD_DIM] = jnp.broadcast_to(loss_ref[...], (1, HEAD_DIM))

        mine = (dwp_ref, vec_scr, dcw_ref)
        sib = (sib_a, sib_b, sib_c)
        every = (all_a, all_b, all_c)
        first = [pltpu.make_async_remote_copy(
            src_ref=mine[b], dst_ref=sib[b], send_sem=send1.at[b], recv_sem=recv1.at[b],
            device_id=(x, y, 1 - c), device_id_type=MESH) for b in range(3)]
        for cp in first:
            cp.start()
        for cp in first:
            cp.wait()
        for b in range(3):
            every[b][j0] = mine[b][...] + sib[b][...]

        def ici(b, k, block):
            chip = chips[k]
            return pltpu.make_async_remote_copy(
                src_ref=every[b].at[block], dst_ref=every[b].at[block],
                send_sem=send2.at[b, k], recv_sem=recv2.at[b, k],
                device_id=(chip[0], chip[1], c), device_id_type=MESH)

        second = [ici(b, k, j0) for b in range(3) for k in range(3)]
        for cp in second:
            cp.start()
        for k, chip in enumerate(chips):
            for b in range(3):
                ici(b, k, 2 * chip[0] + chip[1]).wait_recv()
        for cp in second:
            cp.wait_send()

        tot_a = ((all_a[0] + all_a[1]) + all_a[2]) + all_a[3]
        tot_b = ((all_b[0] + all_b[1]) + all_b[2]) + all_b[3]
        all_c[0] = ((all_c[0] + all_c[1]) + all_c[2]) + all_c[3]
        tot_c = all_c[0, j0]
        loss_out[...] = tot_b[ROW_LOSS:ROW_LOSS + 1, 0:1]

        grads = [tot_a, tot_c] + [tot_b[row:row + 1, 0:n] for row, n in SMALL_VECS]
        w_all = [wp_ref, cw_ref] + list(wvec)
        m_all = [mwp_ref, mcw_ref] + list(mvec)
        v_all = [vwp_ref, vcw_ref] + list(vvec)
        np_ = 2 + nv
        for p in range(np_):
            g = grads[p]
            delta, m_new, v_new = _adamw(w_all[p][...], g, m_all[p][...], v_all[p][...])
            outs[p][...] = g
            outs[np_ + p][...] = delta
            outs[2 * np_ + p][...] = m_new
            outs[3 * np_ + p][...] = v_new

    pshapes = [wp.shape, CW_PAD] + [wv.shape for wv in vec_ws]
    out_shape = [jax.ShapeDtypeStruct((1, 1), F32)] + [jax.ShapeDtypeStruct(s, F32) for s in pshapes] * 4
    a_shape = dwp.shape
    b_shape = (SMALL_ROWS, D_FF)
    c_shape = dcw4.shape
    n_in = 3 + nv + 3 * (2 + nv)
    outs, rider_res = _call(
        body, name="small_update", grid=(),
        in_specs=[_whole()] * n_in, out_specs=[_whole()] * len(out_shape), out_shape=out_shape,
        scratch_shapes=[pltpu.VMEM(b_shape, F32),
                        pltpu.VMEM(a_shape, F32), pltpu.VMEM(b_shape, F32), pltpu.VMEM(c_shape, F32),
                        pltpu.VMEM((N_SHARD,) + a_shape, F32), pltpu.VMEM((N_SHARD,) + b_shape, F32),
                        pltpu.VMEM((N_SHARD,) + c_shape, F32),
                        pltpu.SemaphoreType.DMA((3,)), pltpu.SemaphoreType.DMA((3,)),
                        pltpu.SemaphoreType.DMA((3, 3)), pltpu.SemaphoreType.DMA((3, 3))],
        operands=(loss, dwp, dcw4, *vec_grads, wp, cwp, *vec_ws, m_wp, m_cwp, *vec_ms, v_wp, v_cwp, *vec_vs),
        riders=riders,
    )
    np_ = 2 + nv
    return outs[0], [outs[1 + i * np_:1 + (i + 1) * np_] for i in range(4)], rider_res


def _pad_cw(a):
    pad = [(0, 0)] * (a.ndim - 2) + [(0, CW_PAD[0] - a.shape[-2]), (0, CW_PAD[1] - a.shape[-1])]
    return jnp.pad(a, pad)


def kernel(x, w_in, w_pool, pool_scale, w_out, ln1_g, ln1_b, w_up, conv_w, conv_b, w_down, ln2_g, ln2_b, loss_target, m_w_in, m_w_pool, m_pool_scale, m_w_out, m_ln1_g, m_ln1_b, m_w_up, m_conv_w, m_conv_b, m_w_down, m_ln2_g, m_ln2_b, v_w_in, v_w_pool, v_pool_scale, v_w_out, v_ln1_g, v_ln1_b, v_w_up, v_conv_w, v_conv_b, v_w_down, v_ln2_g, v_ln2_b):
    pos = jnp.stack([lax.axis_index("c"), 2 * lax.axis_index("x") + lax.axis_index("y")]).astype(jnp.int32)
    order = ("w_in", "w_out", "w_up", "w_down")
    w_in_i, w_out_i, w_up_i, w_down_i = range(N_BIG)

    gathered = _gather_weights([w_in[0], w_out[0], w_up[0], w_down[0]], _pad_cw(conv_w[0]), (w_in_i,))
    cw_full = jnp.transpose(gathered[N_BIG][:, 0:3, 0:DOWN_SH], (1, 0, 2)).reshape(3, D_FF)
    up_a, up_b, up_c = (0, 224), (224, 160), (384, 128)
    assert up_c[0] + up_c[1] == SHARD_SHAPES[w_up_i][0] // 2

    class MeshComm:
        def __init__(self):
            self.w = {i: gathered[i] for i in range(N_BIG)}
            self.g32, self.g16, self.recv_a, self.p32, self.p16, self.recv_b = {}, {}, {}, {}, {}, {}
            self.up_complete = False

        def weight(self, name):
            i = order.index(name)
            if name == "w_up" and not self.up_complete:
                (arrs, _), = _comm_only("gather_up_last", [_gather_rider(
                    {i: self.w[i]}, [("d2d_diag", i, up_b), ("d2d", i, up_c)])])
                self.w[i], self.up_complete = arrs[0], True
            full = self.w[i]
            return full.reshape(-1, full.shape[-1]) if name in ("w_out", "w_down") else full

        def _gather(self, ws, ops):
            return _gather_rider({w: self.w[w] for w in ws}, ops), ("w", ws)

        def _pair(self, ws):
            return _pair_rider(ws, [self.g16[w] for w in ws]), ("recv_a", ws)

        def _chip(self, ws, rows=None, resume=False):
            landing = [self.recv_b[w] for w in ws] if resume else None
            return _chip_rider(ws, [self.p16[w] for w in ws], rows, landing), ("recv_b", ws)

        def plan(self, call):
            out_all, down_all = _whole_half(w_out_i), _whole_half(w_down_i)
            if call == "proj_pool":
                return [self._gather([w_out_i, w_up_i, w_down_i],
                                     [("ici", w_out_i, out_all), ("nbr", w_down_i, down_all),
                                      ("nbr", w_up_i, up_a)])]
            if call == "retention_fwd":
                return [self._gather([w_out_i, w_up_i, w_down_i],
                                     [("d2d", w_out_i, out_all),
                                      ("relay", w_down_i, down_all), ("d2d_nbr", w_down_i, down_all),
                                      ("relay", w_up_i, up_a), ("d2d_nbr", w_up_i, up_a), ("nbr", w_up_i, up_b)])]
            if call == "outproj_ln1":
                return [self._gather([w_up_i, w_down_i],
                                     [("d2d_diag", w_down_i, down_all), ("d2d_diag", w_up_i, up_a),
                                      ("relay", w_up_i, up_b), ("d2d_nbr", w_up_i, up_b), ("ici", w_up_i, up_c)])]
            if call == "wgrad_down":
                return [self._pair([w_up_i, w_out_i])]
            if call == "mix_bwd":
                return [self._chip([w_up_i], (0, 256))]
            if call == "retention_bwd":
                return [self._chip([w_up_i], (256, 256), resume=True), self._pair([w_down_i])]
            if call == "wgrad_in":
                return [self._chip([w_out_i, w_down_i])]
            if call == "dx":
                return [self._chip([w_in_i])]
            return []

        def riders(self, call):
            self.pending = self.plan(call)
            return [r for r, _ in self.pending]

        def landed(self, call, results):
            for (_, (slot, ws)), (inplace, lands) in zip(self.pending, results):
                for w, arr in zip(ws, inplace if len(inplace) else lands):
                    getattr(self, slot)[w] = arr
            if call == "wgrad_down":
                self._sum([w_up_i, w_out_i])
            if call == "retention_bwd":
                self._sum([w_down_i])

        def _sum(self, ws):
            p32s, p16s = _pair_sum(pos, ws, [self.g32[w] for w in ws], [self.recv_a[w] for w in ws])
            for w, p32, p16 in zip(ws, p32s, p16s):
                self.p32[w], self.p16[w] = p32, p16

        def gradient(self, name, g32, g16):
            w = order.index(name)
            shape = (N_SHARD,) + SHARD_SHAPES[w]
            self.g32[w], self.g16[w] = g32.reshape(shape), g16.reshape(shape)
            if name == "w_in":
                (_, lands), = _comm_only("pair_exchange_in", [self._pair([w])[0]])
                self.recv_a[w] = lands[0]
                self._sum([w])

    comm = MeshComm()
    loss, grad_x, small = _local_step(x[0], loss_target[0], cw_full, conv_b, w_pool[0], pool_scale,
                                      ln1_g, ln1_b, ln2_g, ln2_b, comm)

    dcw4 = _pad_cw(jnp.transpose(small["conv_w"].reshape(3, N_SHARD, DOWN_SH), (1, 0, 2)))
    vec_names = ("conv_b", "pool_scale", "ln1_g", "ln1_b", "ln2_g", "ln2_b")
    given = dict(w_pool=w_pool, pool_scale=pool_scale, ln1_g=ln1_g, ln1_b=ln1_b, conv_w=conv_w, conv_b=conv_b,
                 ln2_g=ln2_g, ln2_b=ln2_b)
    given_m = dict(w_pool=m_w_pool, pool_scale=m_pool_scale, ln1_g=m_ln1_g, ln1_b=m_ln1_b, conv_w=m_conv_w,
                   conv_b=m_conv_b, ln2_g=m_ln2_g, ln2_b=m_ln2_b)
    given_v = dict(w_pool=v_w_pool, pool_scale=v_pool_scale, ln1_g=v_ln1_g, ln1_b=v_ln1_b, conv_w=v_conv_w,
                   conv_b=v_conv_b, ln2_g=v_ln2_g, ln2_b=v_ln2_b)
    args = []
    for src in (given, given_m, given_v):
        args += [src["w_pool"][0], _pad_cw(src["conv_w"][0]), [src[n] for n in vec_names]]
    loss_tot, small_out, landed = _small_update(loss, small["w_pool"], dcw4, [small[n] for n in vec_names], *args,
                                                riders=comm.riders("small_update"))
    comm.landed("small_update", landed)

    every = range(N_BIG)
    mine = _chip_sum(pos, [comm.p32[w] for w in every], [comm.recv_b[w] for w in every])
    (_, theirs), = _comm_only("pair_exchange_f32", [_final_rider(mine)])
    big_out = _adam_big(pos, mine, theirs, [w_in, w_out, w_up, w_down], [m_w_in, m_w_out, m_w_up, m_w_down],
                        [v_w_in, v_w_out, v_w_up, v_w_down])

    names = ("w_in", "w_pool", "pool_scale", "w_out", "ln1_g", "ln1_b", "w_up", "conv_w", "conv_b", "w_down",
             "ln2_g", "ln2_b")
    small_names = ("w_pool", "conv_w") + vec_names
    result = [loss_tot.reshape(()), grad_x[None]]
    for kind in range(4):
        for n in names:
            if n in order:
                result.append(big_out[kind][order.index(n)])
            else:
                val = small_out[kind][small_names.index(n)]
                if n == "conv_w":
                    val = val[0:3, 0:DOWN_SH][None]
                elif n == "w_pool":
                    val = val[None]
                result.append(val)
    return tuple(result)
```

```python
import functools
import math

import numpy as np
import jax
import jax.numpy as jnp
from jax import lax
from jax.experimental import pallas as pl
from jax.experimental.pallas import tpu as pltpu

F32 = jnp.float32
BF16 = jnp.bfloat16

D_MODEL = 1024
HEADS = 4
HEAD_DIM = 128
RET_W = HEADS * HEAD_DIM
POOL_WINDOWS = (2, 4, 8, 16)
POOL_W = 512
IN_W = 4 * RET_W + POOL_W
D_FF = 2816
N_SHARD = 4
IN_SH = IN_W // N_SHARD
UP_SH = 2 * D_FF // N_SHARD
DOWN_SH = D_FF // N_SHARD
OUT_SH = D_MODEL // N_SHARD
ROPE_BASE = 10000.0
LN_EPS = 1e-5
RMS_EPS = 1e-6
ALPHA = 2.0 ** 0.25
K_SCALE = HEAD_DIM ** -0.5
SUPER = 256
CHUNK = 64
POOL_HALO = 16
CONV_HALO = 8
FFN_STRIP = 128
LN_ROWS = 32

ADAM_LR = 0.001
ADAM_B1 = 0.9
ADAM_B2 = 0.999
ADAM_EPS = 1e-08
ADAM_WD = 0.01
ADAM_STEP = 10

MESH = pl.DeviceIdType.MESH
VMEM_LIMIT = 56 * 1024 * 1024


def _dot(a, b):
    return jnp.dot(a, b, preferred_element_type=F32)


def _dot_nt(a, b):
    return lax.dot_general(a, b, (((1,), (1,)), ((), ())), preferred_element_type=F32)


def _dot_tn(a, b):
    return lax.dot_general(a, b, (((0,), (0,)), ((), ())), preferred_element_type=F32)


def _sigmoid(x):
    return 1.0 / (1.0 + jnp.exp(-x))


def _params(sem):
    return pltpu.CompilerParams(dimension_semantics=sem, vmem_limit_bytes=VMEM_LIMIT)


def _whole():
    return pl.BlockSpec(memory_space=pltpu.VMEM)


HBM_SPEC = pl.BlockSpec(memory_space=pl.ANY)


class _Rider:
    def __init__(self, inplace, srcs, lands, n_copies, make):
        self.inplace, self.srcs, self.lands, self.n_copies, self.make = list(inplace), list(srcs), list(lands), n_copies, make


def _call(body, *, name, grid, in_specs, out_specs, out_shape, operands, scratch_shapes=(), sem=(),
          aliases=None, riders=(), after=()):
    n_in, n_out, n_scr = len(in_specs), len(out_shape), len(scratch_shapes)
    in_specs, out_specs, out_shape = list(in_specs), list(out_specs), list(out_shape)
    operands, scratch_shapes, aliases = list(operands), list(scratch_shapes), dict(aliases or {})
    in_specs += [_whole()] * len(after)
    operands += list(after)
    for r in riders:
        for a in r.inplace:
            aliases[len(in_specs)] = len(out_shape)
            in_specs.append(HBM_SPEC)
            operands.append(a)
            out_specs.append(HBM_SPEC)
            out_shape.append(jax.ShapeDtypeStruct(a.shape, a.dtype))
        for a in r.srcs:
            in_specs.append(HBM_SPEC)
            operands.append(a)
        for shp in r.lands:
            out_specs.append(HBM_SPEC)
            out_shape.append(shp)
        scratch_shapes += [pltpu.SemaphoreType.DMA((r.n_copies,)), pltpu.SemaphoreType.DMA((r.n_copies,))]

    def full(*refs):
        ins = refs[:n_in]
        at = n_in + len(after)
        r_srcs = []
        for r in riders:
            at += len(r.inplace)
            r_srcs.append(refs[at:at + len(r.srcs)])
            at += len(r.srcs)
        outs = refs[at:at + n_out]
        at += n_out
        r_outs = []
        for r in riders:
            r_outs.append((refs[at:at + len(r.inplace)], refs[at + len(r.inplace):at + len(r.inplace) + len(r.lands)]))
            at += len(r.inplace) + len(r.lands)
        scr = refs[at:at + n_scr]
        at += n_scr
        r_sems = [refs[at + 2 * i:at + 2 * i + 2] for i in range(len(riders))]

        def copies():
            return [r.make(r_outs[i][0], r_srcs[i], r_outs[i][1], r_sems[i][0], r_sems[i][1])
                    for i, r in enumerate(riders)]

        def start():
            for starts, _ in copies():
                for cp in starts:
                    cp.start()

        def finish():
            for _, waits in copies():
                for wait in waits:
                    wait()

        if riders and grid:
            first = functools.reduce(jnp.logical_and, [pl.program_id(d) == 0 for d in range(len(grid))])
            last = functools.reduce(jnp.logical_and, [pl.program_id(d) == grid[d] - 1 for d in range(len(grid))])
            pl.when(first)(start)
            body(*ins, *outs, *scr)
            pl.when(last)(finish)
        else:
            if riders:
                start()
            body(*ins, *outs, *scr)
            if riders:
                finish()

    params = _params(sem) if grid else pltpu.CompilerParams(vmem_limit_bytes=VMEM_LIMIT)
    res = pl.pallas_call(
        full, name=name, grid=grid, in_specs=in_specs, out_specs=out_specs, out_shape=out_shape,
        scratch_shapes=scratch_shapes, input_output_aliases=aliases, compiler_params=params,
    )(*operands)
    outs, at, rider_res = res[:n_out], n_out, []
    for r in riders:
        rider_res.append((res[at:at + len(r.inplace)], res[at + len(r.inplace):at + len(r.inplace) + len(r.lands)]))
        at += len(r.inplace) + len(r.lands)
    return list(outs), rider_res


def _gammas():
    return [1.0 - 2.0 ** (-5.0 - h) for h in range(HEADS)]


def _decay_tables():
    idx = np.arange(SUPER)
    dist = np.abs(idx[:, None] - idx[None, :]).astype(np.float64)
    visible = (idx[None, :] // CHUNK) <= (idx[:, None] // CHUNK)
    mask = np.stack([np.where(visible, g ** dist, 0.0) for g in _gammas()])
    qd = np.concatenate([np.repeat((g ** (idx + 1.0))[:, None], HEAD_DIM, 1) for g in _gammas()], 1)
    kd = np.concatenate([np.repeat((g ** (SUPER - 1.0 - idx))[:, None], HEAD_DIM, 1) for g in _gammas()], 1)
    return (jnp.asarray(mask, F32), jnp.asarray(qd, F32), jnp.asarray(kd, F32))


def _rope_tables(s):
    inv_freq = ROPE_BASE ** (-np.arange(0, HEAD_DIM, 2, dtype=np.float64) / HEAD_DIM)
    ang = np.arange(s, dtype=np.float64)[:, None] * inv_freq[None, :]
    cos, sin = np.cos(ang), np.sin(ang)
    return (jnp.asarray(np.concatenate([cos, cos], 1), F32),
            jnp.asarray(np.concatenate([-sin, sin], 1), F32))


def _rope(t, cosf, sinf):
    return t * cosf + pltpu.roll(t, HEAD_DIM // 2, 1) * sinf


def _rope_t(t, cosf, sinf):
    return t * cosf - pltpu.roll(t, HEAD_DIM // 2, 1) * sinf


def _layernorm_fwd(z):
    mu = jnp.mean(z, axis=-1, keepdims=True)
    zc = z - mu
    var = jnp.mean(zc * zc, axis=-1, keepdims=True)
    rstd = lax.rsqrt(var + LN_EPS)
    return zc * rstd, rstd


def _layernorm_bwd(dy, xhat, rstd, gain):
    dxh = dy * gain
    m1 = jnp.mean(dxh, axis=-1, keepdims=True)
    m2 = jnp.mean(dxh * xhat, axis=-1, keepdims=True)
    return rstd * (dxh - m1 - xhat * m2)


def _proj_pool(x, win4, cosf, sinf, wpool, pscale, ts, riders=()):
    s = x.shape[0]
    nt = s // ts

    def body(x_ref, w_ref, cos_ref, sin_ref, wp_ref, ps_ref,
             xb_ref, q_ref, k_ref, v_ref, g_ref, pooled_ref, cat_ref, proj_scr, pext_scr):
        i = pl.program_id(0)
        xb = x_ref[...].astype(BF16)
        xb_ref[...] = xb
        for j in range(N_SHARD):
            proj_scr[:, j * IN_SH:(j + 1) * IN_SH] = _dot(xb, w_ref[j])
        cosf_t = cos_ref[...]
        sinf_t = sin_ref[...]
        for h in range(HEADS):
            lo = h * HEAD_DIM
            q_ref[:, lo:lo + HEAD_DIM] = _rope(proj_scr[:, lo:lo + HEAD_DIM], cosf_t, sinf_t).astype(BF16)
            kk = _rope(proj_scr[:, RET_W + lo:RET_W + lo + HEAD_DIM], cosf_t, sinf_t) * K_SCALE
            k_ref[:, lo:lo + HEAD_DIM] = kk.astype(BF16)
        v_ref[...] = proj_scr[:, 2 * RET_W:3 * RET_W].astype(BF16)
        g_ref[...] = proj_scr[:, 3 * RET_W:4 * RET_W]

        @pl.when(i == 0)
        def _():
            pext_scr[0:POOL_HALO, :] = jnp.zeros((POOL_HALO, POOL_W), F32)

        pext_scr[POOL_HALO:POOL_HALO + ts, :] = proj_scr[:, 4 * RET_W:IN_W]
        pos = (i * ts + lax.broadcasted_iota(jnp.int32, (ts, 1), 0) + 1).astype(F32)
        for gi, w in enumerate(POOL_WINDOWS):
            lo = gi * HEAD_DIM
            ext = pext_scr[:, lo:lo + HEAD_DIM]
            acc = ext
            shift = 1
            while shift < w:
                acc = acc + pltpu.roll(acc, shift, 0)
                shift *= 2
            tok = ext[POOL_HALO:POOL_HALO + ts]
            pooled = acc[POOL_HALO:POOL_HALO + ts] / jnp.minimum(pos, float(w)) - tok
            pooled_b = pooled.astype(BF16)
            pooled_ref[:, lo:lo + HEAD_DIM] = pooled_b
            lin = _dot(pooled_b, wp_ref[gi])
            cat_ref[:, lo:lo + HEAD_DIM] = (lin * ps_ref[:, lo:lo + HEAD_DIM]).astype(BF16)
        pext_scr[0:POOL_HALO, :] = pext_scr[ts:ts + POOL_HALO, :]

    tile = lambda w: pl.BlockSpec((ts, w), lambda i: (i, 0))
    return _call(
        body, name="proj_pool", grid=(nt,),
        in_specs=[tile(D_MODEL), _whole(), tile(HEAD_DIM), tile(HEAD_DIM), _whole(), _whole()],
        out_specs=[tile(D_MODEL), tile(RET_W), tile(RET_W), tile(RET_W), tile(RET_W), tile(POOL_W),
                   pl.BlockSpec((ts, POOL_W), lambda i: (i, 1))],
        out_shape=[jax.ShapeDtypeStruct((s, D_MODEL), BF16), jax.ShapeDtypeStruct((s, RET_W), BF16),
                   jax.ShapeDtypeStruct((s, RET_W), BF16), jax.ShapeDtypeStruct((s, RET_W), BF16),
                   jax.ShapeDtypeStruct((s, RET_W), F32), jax.ShapeDtypeStruct((s, POOL_W), BF16),
                   jax.ShapeDtypeStruct((s, 2 * RET_W), BF16)],
        scratch_shapes=[pltpu.VMEM((ts, IN_W), F32), pltpu.VMEM((ts + POOL_HALO, POOL_W), F32)],
        sem=("arbitrary",), operands=(x, win4, cosf, sinf, wpool, pscale), riders=riders,
    )


def _retention_fwd(q, k, v, g, cat, mask, qd, kd, riders=()):
    s = q.shape[0]
    ns = s // SUPER
    cdec = [gm ** float(SUPER) for gm in _gammas()]

    def body(q_ref, k_ref, v_ref, g_ref, cat_in, mask_ref, qd_ref, kd_ref,
             ret_ref, cat_ref, st_ref, state_scr):
        del cat_in
        n = pl.program_id(0)

        @pl.when(n == 0)
        def _():
            state_scr[...] = jnp.zeros_like(state_scr)

        for h in range(HEADS):
            sl = slice(h * HEAD_DIM, (h + 1) * HEAD_DIM)
            qh, kh, vh = q_ref[:, sl], k_ref[:, sl], v_ref[:, sl]
            sc = _dot_nt(qh, kh) * mask_ref[h]
            st = state_scr[h]
            stb = st.astype(BF16)
            st_ref[0, h] = stb
            qdb = (qh.astype(F32) * qd_ref[:, sl]).astype(BF16)
            kdb = (kh.astype(F32) * kd_ref[:, sl]).astype(BF16)
            ret = _dot(sc.astype(BF16), vh) + _dot(qdb, stb)
            state_scr[h] = st * cdec[h] + _dot_tn(kdb, vh)
            ret_ref[:, sl] = ret
            r = lax.rsqrt(jnp.mean(ret * ret, axis=-1, keepdims=True) + RMS_EPS)
            gh = g_ref[:, sl]
            cat_ref[:, sl] = ((ret * r) * (gh * _sigmoid(gh))).astype(BF16)

    tile = pl.BlockSpec((SUPER, RET_W), lambda n: (n, 0))
    return _call(
        body, name="retention_fwd", grid=(ns,),
        in_specs=[tile, tile, tile, tile, HBM_SPEC, _whole(), _whole(), _whole()],
        out_specs=[tile, tile, pl.BlockSpec((1, HEADS, HEAD_DIM, HEAD_DIM), lambda n: (n, 0, 0, 0))],
        out_shape=[jax.ShapeDtypeStruct((s, RET_W), F32), jax.ShapeDtypeStruct((s, 2 * RET_W), BF16),
                   jax.ShapeDtypeStruct((ns, HEADS, HEAD_DIM, HEAD_DIM), BF16)],
        scratch_shapes=[pltpu.VMEM((HEADS, HEAD_DIM, HEAD_DIM), F32)],
        aliases={4: 1}, sem=("arbitrary",), operands=(q, k, v, g, cat, mask, qd, kd), riders=riders,
    )


def _outproj_ln1(x, cat, wout, g1, b1, ts, riders=()):
    s = x.shape[0]

    def body(x_ref, cat_ref, w_ref, g_ref, b_ref, xhat_ref, rstd_ref, h1b_ref):
        z = ALPHA * x_ref[...] + _dot(cat_ref[...], w_ref[...])
        xhat, rstd = _layernorm_fwd(z)
        xhat_ref[...] = xhat
        rstd_ref[...] = rstd
        h1b_ref[...] = (xhat * g_ref[...] + b_ref[...]).astype(BF16)

    tile = lambda w: pl.BlockSpec((ts, w), lambda i: (i, 0))
    return _call(
        body, name="outproj_ln1", grid=(s // ts,),
        in_specs=[tile(D_MODEL), tile(D_MODEL), _whole(), _whole(), _whole()],
        out_specs=[tile(D_MODEL), tile(1), tile(D_MODEL)],
        out_shape=[jax.ShapeDtypeStruct((s, D_MODEL), F32), jax.ShapeDtypeStruct((s, 1), F32),
                   jax.ShapeDtypeStruct((s, D_MODEL), BF16)],
        sem=("arbitrary",), operands=(x, cat, wout, g1, b1), riders=riders,
    )


def _ffn_fwd_loss(xhat1, h1b, target, wup4, wdown, cw, cb, g1, b1, g2, b2, ts):
    s = xhat1.shape[0]

    def body(xhat_ref, h1b_ref, tgt_ref, wup_ref, wdn_ref, cw_ref, cb_ref, g1_ref, b1_ref, g2_ref, b2_ref,
             ub_ref, dz2_ref, dz2b_ref, loss_ref, dg2_ref, db2_ref, val_scr, gext_scr, act_scr, ffn_scr):
        i = pl.program_id(0)

        @pl.when(i == 0)
        def _():
            gext_scr[0:CONV_HALO, :] = jnp.zeros((CONV_HALO, D_FF), F32)
            loss_ref[...] = jnp.zeros_like(loss_ref)
            dg2_ref[...] = jnp.zeros_like(dg2_ref)
            db2_ref[...] = jnp.zeros_like(db2_ref)

        hb = h1b_ref[...]
        for half in range(2):
            lo = half * UP_SH
            gext_scr[CONV_HALO:CONV_HALO + ts, lo:lo + UP_SH] = _dot(hb, wup_ref[2 + half])
            val_scr[:, lo:lo + UP_SH] = _dot(hb, wup_ref[half])
            for c0 in range(lo, lo + UP_SH, FFN_STRIP):
                cols = slice(c0, c0 + FFN_STRIP)
                ext = gext_scr[:, cols]
                gate = ext[CONV_HALO:]
                hc = cb_ref[:, cols] + ((pltpu.roll(ext, 2, 0)[CONV_HALO:] * cw_ref[0:1, cols]
                                         + pltpu.roll(ext, 1, 0)[CONV_HALO:] * cw_ref[1:2, cols])
                                        + gate * cw_ref[2:3, cols])
                val = val_scr[:, cols]
                act_scr[:, cols] = ((hc * _sigmoid(hc)) * val).astype(BF16)
                ub_ref[:, cols] = val.astype(BF16)
                ub_ref[:, D_FF + c0:D_FF + c0 + FFN_STRIP] = gate.astype(BF16)
            part = _dot(act_scr[:, lo:lo + UP_SH], wdn_ref[lo:lo + UP_SH, :])
            if half == 0:
                ffn_scr[...] = part
            else:
                ffn_scr[...] += part
        gext_scr[0:CONV_HALO, :] = gext_scr[ts:ts + CONV_HALO, :]

        loss_acc = jnp.zeros((1, 1), F32)
        dg2_acc = jnp.zeros((1, D_MODEL), F32)
        db2_acc = jnp.zeros((1, D_MODEL), F32)
        for r0 in range(0, ts, LN_ROWS):
            rows = slice(r0, r0 + LN_ROWS)
            h1 = xhat_ref[rows, :] * g1_ref[...] + b1_ref[...]
            xhat2, rstd2 = _layernorm_fwd(ALPHA * h1 + ffn_scr[rows, :])
            diff = (xhat2 * g2_ref[...] + b2_ref[...]) - tgt_ref[rows, :]
            row = jnp.mean(diff * diff, axis=-1, keepdims=True)
            loss_acc = loss_acc + 0.5 * jnp.sum(row, axis=0, keepdims=True)
            dy = diff * (1.0 / D_MODEL)
            dg2_acc = dg2_acc + jnp.sum(dy * xhat2, axis=0, keepdims=True)
            db2_acc = db2_acc + jnp.sum(dy, axis=0, keepdims=True)
            dz2 = _layernorm_bwd(dy, xhat2, rstd2, g2_ref[...])
            dz2_ref[rows, :] = dz2
            dz2b_ref[rows, :] = dz2.astype(BF16)
        loss_ref[...] += loss_acc
        dg2_ref[...] += dg2_acc
        db2_ref[...] += db2_acc

    tile = lambda w: pl.BlockSpec((ts, w), lambda i: (i, 0))
    acc = lambda w: pl.BlockSpec((1, w), lambda i: (0, 0))
    return pl.pallas_call(
        body, name="ffn_fwd_loss", grid=(s // ts,),
        in_specs=[tile(D_MODEL), tile(D_MODEL), tile(D_MODEL)] + [_whole()] * 8,
        out_specs=[tile(2 * D_FF), tile(D_MODEL), tile(D_MODEL), acc(1), acc(D_MODEL), acc(D_MODEL)],
        out_shape=[jax.ShapeDtypeStruct((s, 2 * D_FF), BF16), jax.ShapeDtypeStruct((s, D_MODEL), F32),
                   jax.ShapeDtypeStruct((s, D_MODEL), BF16),
                   jax.ShapeDtypeStruct((1, 1), F32), jax.ShapeDtypeStruct((1, D_MODEL), F32),
                   jax.ShapeDtypeStruct((1, D_MODEL), F32)],
        scratch_shapes=[pltpu.VMEM((ts, D_FF), F32), pltpu.VMEM((ts + CONV_HALO, D_FF), F32),
                        pltpu.VMEM((ts, D_FF), BF16), pltpu.VMEM((ts, D_MODEL), F32)],
        compiler_params=_params(("arbitrary",)),
    )(xhat1, h1b, target, wup4, wdown, cw, cb, g1, b1, g2, b2)


def _ffn_bwd(dz2, dz2b, ub, xhat1, rstd1, wup4, wdown, cw, cb, g1, ts):
    s = dz2.shape[0]
    nt = s // ts
    hb = 16

    def body(dz2_ref, dz2b_ref, ub_ref, prev_ref, xhat_ref, rstd_ref, wup_ref, wdn_ref, cw_ref, cb_ref, g1_ref,
             a_ref, dub_ref, dz1_ref, dz1b_ref, dg1_ref, db1_ref, dcw_ref, dcb_ref, gext_scr, dext_scr, da_scr):
        i = pl.program_id(0)
        r = nt - 1 - i

        @pl.when(i == 0)
        def _():
            dext_scr[ts:ts + CONV_HALO, :] = jnp.zeros((CONV_HALO, D_FF), F32)
            dg1_ref[...] = jnp.zeros_like(dg1_ref)
            db1_ref[...] = jnp.zeros_like(db1_ref)
            dcw_ref[...] = jnp.zeros_like(dcw_ref)
            dcb_ref[...] = jnp.zeros_like(dcb_ref)

        da_scr[...] = _dot_nt(dz2b_ref[...], wdn_ref[...])
        prev = prev_ref[...].astype(F32)[hb - CONV_HALO:hb]
        gext_scr[0:CONV_HALO, :] = jnp.where(r == 0, 0.0, prev)
        n_ext = ts + CONV_HALO
        for c0 in range(0, D_FF, FFN_STRIP):
            cols = slice(c0, c0 + FFN_STRIP)
            gcols = slice(D_FF + c0, D_FF + c0 + FFN_STRIP)
            val = ub_ref[:, cols].astype(F32)
            gate = ub_ref[:, gcols].astype(F32)
            gext_scr[CONV_HALO:n_ext, cols] = gate
            ext = gext_scr[:, cols]
            g2s = pltpu.roll(ext, 2, 0)[CONV_HALO:]
            g1s = pltpu.roll(ext, 1, 0)[CONV_HALO:]
            hc = cb_ref[:, cols] + ((g2s * cw_ref[0:1, cols] + g1s * cw_ref[1:2, cols]) + gate * cw_ref[2:3, cols])
            sg = _sigmoid(hc)
            si = hc * sg
            a_ref[:, cols] = (si * val).astype(BF16)
            da = da_scr[:, cols]
            dhc = da * val * (sg * (1.0 + hc * (1.0 - sg)))
            dcb_ref[:, cols] += jnp.sum(dhc, axis=0, keepdims=True)
            dcw_ref[0:1, cols] += jnp.sum(dhc * g2s, axis=0, keepdims=True)
            dcw_ref[1:2, cols] += jnp.sum(dhc * g1s, axis=0, keepdims=True)
            dcw_ref[2:3, cols] += jnp.sum(dhc * gate, axis=0, keepdims=True)
            dext_scr[0:ts, cols] = dhc
            dext = dext_scr[:, cols]
            dgate = (dhc * cw_ref[2:3, cols] + pltpu.roll(dext, n_ext - 1, 0)[0:ts] * cw_ref[1:2, cols]
                     + pltpu.roll(dext, n_ext - 2, 0)[0:ts] * cw_ref[0:1, cols])
            dub_ref[:, cols] = (da * si).astype(BF16)
            dub_ref[:, gcols] = dgate.astype(BF16)
        dext_scr[ts:n_ext, :] = dext_scr[0:CONV_HALO, :]
        dh1 = ALPHA * dz2_ref[...]
        for j in range(N_SHARD):
            dh1 = dh1 + _dot_nt(dub_ref[:, j * UP_SH:(j + 1) * UP_SH], wup_ref[j])
        xhat = xhat_ref[...]
        dg1_ref[...] += jnp.sum(dh1 * xhat, axis=0, keepdims=True)
        db1_ref[...] += jnp.sum(dh1, axis=0, keepdims=True)
        dz1 = _layernorm_bwd(dh1, xhat, rstd_ref[...], g1_ref[...])
        dz1_ref[...] = dz1
        dz1b_ref[...] = dz1.astype(BF16)

    tile = lambda w: pl.BlockSpec((ts, w), lambda i: (nt - 1 - i, 0))
    acc = lambda rws, w: pl.BlockSpec((rws, w), lambda i: (0, 0))
    prev_spec = pl.BlockSpec((hb, D_FF), lambda i: (jnp.maximum((nt - 1 - i) * (ts // hb) - 1, 0), 1))
    return pl.pallas_call(
        body, name="ffn_bwd", grid=(nt,),
        in_specs=[tile(D_MODEL), tile(D_MODEL), tile(2 * D_FF), prev_spec, tile(D_MODEL), tile(1)] + [_whole()] * 5,
        out_specs=[tile(D_FF), tile(2 * D_FF), tile(D_MODEL), tile(D_MODEL), acc(1, D_MODEL), acc(1, D_MODEL),
                   acc(3, D_FF), acc(1, D_FF)],
        out_shape=[jax.ShapeDtypeStruct((s, D_FF), BF16), jax.ShapeDtypeStruct((s, 2 * D_FF), BF16),
                   jax.ShapeDtypeStruct((s, D_MODEL), F32), jax.ShapeDtypeStruct((s, D_MODEL), BF16),
                   jax.ShapeDtypeStruct((1, D_MODEL), F32),
                   jax.ShapeDtypeStruct((1, D_MODEL), F32), jax.ShapeDtypeStruct((3, D_FF), F32),
                   jax.ShapeDtypeStruct((1, D_FF), F32)],
        scratch_shapes=[pltpu.VMEM((ts + CONV_HALO, D_FF), F32), pltpu.VMEM((ts + CONV_HALO, D_FF), F32),
                        pltpu.VMEM((ts, D_FF), F32)],
        compiler_params=_params(("arbitrary",)),
    )(dz2, dz2b, ub, ub, xhat1, rstd1, wup4, wdown, cw, cb, g1)


def _mix_bwd(dz1, pooled, ret, g, wout, wpool, pscale, ts, riders=()):
    s = dz1.shape[0]
    nt = s // ts

    def body(dz1_ref, pooled_ref, ret_ref, g_ref, wout_ref, wp_ref, ps_ref,
             dret_ref, dgp_ref, dwp_ref, dps_ref, eext_scr):
        i = pl.program_id(0)
        r = nt - 1 - i

        @pl.when(i == 0)
        def _():
            eext_scr[ts:ts + POOL_HALO, :] = jnp.zeros((POOL_HALO, POOL_W), F32)
            dwp_ref[...] = jnp.zeros_like(dwp_ref)
            dps_ref[...] = jnp.zeros_like(dps_ref)

        dzb = dz1_ref[...].astype(BF16)
        dcat_r = _dot_nt(dzb, wout_ref[0:RET_W, :])
        dcat_p = _dot_nt(dzb, wout_ref[RET_W:2 * RET_W, :])
        pos = (r * ts + lax.broadcasted_iota(jnp.int32, (ts, 1), 0) + 1).astype(F32)
        dpooled = []
        for gi, w in enumerate(POOL_WINDOWS):
            sl = slice(gi * HEAD_DIM, (gi + 1) * HEAD_DIM)
            pb = pooled_ref[:, sl]
            dy = dcat_p[:, sl]
            dps_ref[:, sl] += jnp.sum(dy * _dot(pb, wp_ref[gi]), axis=0, keepdims=True)
            dlin = (dy * ps_ref[:, sl]).astype(BF16)
            dwp_ref[gi] += _dot_tn(pb, dlin)
            dpg = _dot_nt(dlin, wp_ref[gi])
            dpooled.append(dpg)
            eext_scr[0:ts, sl] = dpg / jnp.minimum(pos, float(w))
        for gi, w in enumerate(POOL_WINDOWS):
            sl = slice(gi * HEAD_DIM, (gi + 1) * HEAD_DIM)
            acc = eext_scr[:, sl]
            shift = 1
            while shift < w:
                acc = acc + pltpu.roll(acc, ts + POOL_HALO - shift, 0)
                shift *= 2
            dgp_ref[:, RET_W + gi * HEAD_DIM:RET_W + (gi + 1) * HEAD_DIM] = (acc[0:ts] - dpooled[gi]).astype(BF16)
        eext_scr[ts:ts + POOL_HALO, :] = eext_scr[0:POOL_HALO, :]
        for h in range(HEADS):
            sl = slice(h * HEAD_DIM, (h + 1) * HEAD_DIM)
            rt = ret_ref[:, sl]
            rr = lax.rsqrt(jnp.mean(rt * rt, axis=-1, keepdims=True) + RMS_EPS)
            rn = rt * rr
            gh = g_ref[:, sl]
            sg = _sigmoid(gh)
            dy = dcat_r[:, sl]
            dgp_ref[:, sl] = (dy * rn * (sg * (1.0 + gh * (1.0 - sg)))).astype(BF16)
            drn = dy * (gh * sg)
            dret_ref[:, sl] = (rr * (drn - rn * jnp.mean(drn * rn, axis=-1, keepdims=True))).astype(BF16)

    tile = lambda w: pl.BlockSpec((ts, w), lambda i: (nt - 1 - i, 0))
    return _call(
        body, name="mix_bwd", grid=(nt,),
        in_specs=[tile(D_MODEL), tile(POOL_W), tile(RET_W), tile(RET_W), _whole(), _whole(), _whole()],
        out_specs=[tile(RET_W), tile(2 * RET_W),
                   pl.BlockSpec((len(POOL_WINDOWS), HEAD_DIM, HEAD_DIM), lambda i: (0, 0, 0)),
                   pl.BlockSpec((1, POOL_W), lambda i: (0, 0))],
        out_shape=[jax.ShapeDtypeStruct((s, RET_W), BF16), jax.ShapeDtypeStruct((s, 2 * RET_W), BF16),
                   jax.ShapeDtypeStruct((len(POOL_WINDOWS), HEAD_DIM, HEAD_DIM), F32),
                   jax.ShapeDtypeStruct((1, POOL_W), F32)],
        scratch_shapes=[pltpu.VMEM((ts + POOL_HALO, POOL_W), F32)],
        sem=("arbitrary",), operands=(dz1, pooled, ret, g, wout, wpool, pscale), riders=riders,
    )


def _retention_bwd(q, k, v, dret, dgp, states, mask, qd, kd, cosf, sinf, riders=()):
    s = q.shape[0]
    ns = s // SUPER
    cdec = [gm ** float(SUPER) for gm in _gammas()]

    def body(q_ref, k_ref, v_ref, do_ref, dgp_ref, st_ref, mask_ref, qd_ref, kd_ref, cos_ref, sin_ref,
             dproj_ref, dstate_scr):
        i = pl.program_id(0)

        @pl.when(i == 0)
        def _():
            dstate_scr[...] = jnp.zeros_like(dstate_scr)

        cosf_t = cos_ref[...]
        sinf_t = sin_ref[...]
        for h in range(HEADS):
            sl = slice(h * HEAD_DIM, (h + 1) * HEAD_DIM)
            qh, kh, vh, doh = q_ref[:, sl], k_ref[:, sl], v_ref[:, sl], do_ref[:, sl]
            m = mask_ref[h]
            scb = (_dot_nt(qh, kh) * m).astype(BF16)
            dscb = (_dot_nt(doh, vh) * m).astype(BF16)
            stb = st_ref[0, h]
            dst = dstate_scr[h]
            dstb = dst.astype(BF16)
            qdb = (qh.astype(F32) * qd_ref[:, sl]).astype(BF16)
            kdb = (kh.astype(F32) * kd_ref[:, sl]).astype(BF16)
            dq = _dot(dscb, kh) + _dot_nt(doh, stb) * qd_ref[:, sl]
            dk = _dot_tn(dscb, qh) + _dot_nt(vh, dstb) * kd_ref[:, sl]
            dv = _dot_tn(scb, doh) + _dot(kdb, dstb)
            dstate_scr[h] = dst * cdec[h] + _dot_tn(qdb, doh)
            lo = h * HEAD_DIM
            dproj_ref[:, lo:lo + HEAD_DIM] = _rope_t(dq, cosf_t, sinf_t).astype(BF16)
            dproj_ref[:, RET_W + lo:RET_W + lo + HEAD_DIM] = _rope_t(dk * K_SCALE, cosf_t, sinf_t).astype(BF16)
            dproj_ref[:, 2 * RET_W + lo:2 * RET_W + lo + HEAD_DIM] = dv.astype(BF16)
        dproj_ref[:, 3 * RET_W:IN_W] = dgp_ref[...]

    tile = lambda w: pl.BlockSpec((SUPER, w), lambda i: (ns - 1 - i, 0))
    return _call(
        body, name="retention_bwd", grid=(ns,),
        in_specs=[tile(RET_W), tile(RET_W), tile(RET_W), tile(RET_W), tile(2 * RET_W),
                  pl.BlockSpec((1, HEADS, HEAD_DIM, HEAD_DIM), lambda i: (ns - 1 - i, 0, 0, 0)),
                  _whole(), _whole(), _whole(), tile(HEAD_DIM), tile(HEAD_DIM)],
        out_specs=[tile(IN_W)],
        out_shape=[jax.ShapeDtypeStruct((s, IN_W), BF16)],
        scratch_shapes=[pltpu.VMEM((HEADS, HEAD_DIM, HEAD_DIM), F32)],
        sem=("arbitrary",), operands=(q, k, v, dret, dgp, states, mask, qd, kd, cosf, sinf), riders=riders,
    )


def _dx(dz1, dproj, win4, ts, riders=(), after=()):
    s = dz1.shape[0]

    def body(dz1_ref, dp_ref, w_ref, dx_ref):
        acc = ALPHA * dz1_ref[...]
        for j in range(N_SHARD):
            acc = acc + _dot_nt(dp_ref[:, j * IN_SH:(j + 1) * IN_SH], w_ref[j])
        dx_ref[...] = acc

    tile = lambda w: pl.BlockSpec((ts, w), lambda i: (i, 0))
    return _call(
        body, name="dx", grid=(s // ts,),
        in_specs=[tile(D_MODEL), tile(IN_W), _whole()],
        out_specs=[tile(D_MODEL)],
        out_shape=[jax.ShapeDtypeStruct((s, D_MODEL), F32)],
        sem=("arbitrary",), operands=(dz1, dproj, win4), riders=riders, after=after,
    )


def _wgrad(a, b, tm, tn, name, stacked, m_outer, riders=()):
    s, m = a.shape
    n = b.shape[1]

    def body(a_ref, b_ref, o32_ref, o16_ref):
        res = _dot_tn(a_ref[...], b_ref[...])
        o32_ref[...] = res.reshape(o32_ref.shape)
        o16_ref[...] = res.astype(BF16).reshape(o16_ref.shape)

    if m_outer:
        grid, blocks = (m // tm, n // tn), (lambda g0, g1: (g0, g1))
    else:
        grid, blocks = (n // tn, m // tm), (lambda g0, g1: (g1, g0))
    if stacked:
        shape = (n // tn, m, tn)
        ospec = pl.BlockSpec((1, tm, tn), lambda g0, g1: (blocks(g0, g1)[1], blocks(g0, g1)[0], 0))
    else:
        shape = (m, n)
        ospec = pl.BlockSpec((tm, tn), lambda g0, g1: blocks(g0, g1))
    return _call(
        body, name=name, grid=grid,
        in_specs=[pl.BlockSpec((s, tm), lambda g0, g1: (0, blocks(g0, g1)[0])),
                  pl.BlockSpec((s, tn), lambda g0, g1: (0, blocks(g0, g1)[1]))],
        out_specs=[ospec, ospec],
        out_shape=[jax.ShapeDtypeStruct(shape, F32), jax.ShapeDtypeStruct(shape, BF16)],
        sem=("arbitrary", "arbitrary"), operands=(a, b), riders=riders,
    )


class _NoComm:
    def __init__(self, win4, wout, wup4, wdown):
        self.weights = dict(w_in=win4, w_out=wout, w_up=wup4, w_down=wdown)
        self.grads = {}

    def weight(self, name):
        return self.weights[name]

    def riders(self, call):
        return ()

    def after(self, call):
        return ()

    def landed(self, call, results):
        pass

    def gradient(self, name, g32, g16):
        self.grads[name] = (g32, g16)


def _local_step(x, target, cw, cb, wpool, pscale, g1, b1, g2, b2, comm):
    s = x.shape[0]
    ts_a = min(512, s)
    ts_f = min(256, s)
    mask, qd, kd = _decay_tables()
    cosf, sinf = _rope_tables(s)
    wpool_b = wpool.astype(BF16)

    def run(call, fn, *args):
        outs, res = fn(*args, riders=comm.riders(call))
        comm.landed(call, res)
        return outs

    xb, q, k, v, g, pooled, cat = run("proj_pool", _proj_pool, x, comm.weight("w_in"), cosf, sinf, wpool_b,
                                      pscale, ts_a)
    ret, cat, states = run("retention_fwd", _retention_fwd, q, k, v, g, cat, mask, qd, kd)
    wout = comm.weight("w_out")
    xhat1, rstd1, h1b = run("outproj_ln1", _outproj_ln1, x, cat, wout, g1, b1, ts_a)
    wup4, wdown = comm.weight("w_up"), comm.weight("w_down")
    ub, dz2, dz2b, loss, dg2, db2 = _ffn_fwd_loss(xhat1, h1b, target, wup4, wdown, cw, cb, g1, b1, g2, b2, ts_f)

    act, dub, dz1, dz1b, dg1, db1, dcw, dcb = _ffn_bwd(dz2, dz2b, ub, xhat1, rstd1, wup4, wdown, cw, cb, g1, ts_f)
    half = D_MODEL // 2
    comm.gradient("w_up", *run("wgrad_up", _wgrad, h1b, dub, half, UP_SH, "wgrad_up", True, False))
    comm.gradient("w_out", *run("wgrad_out", _wgrad, cat, dz1b, D_MODEL, half, "wgrad_out", False, True))
    comm.gradient("w_down", *run("wgrad_down", _wgrad, act, dz2b, D_FF // 2, half, "wgrad_down", False, True))
    dret, dgp, dwp, dps = run("mix_bwd", _mix_bwd, dz1b, pooled, ret, g, wout, wpool_b, pscale, ts_a)
    dproj, = run("retention_bwd", _retention_bwd, q, k, v, dret, dgp, states, mask, qd, kd, cosf, sinf)
    comm.gradient("w_in", *run("wgrad_in", _wgrad, xb, dproj, D_MODEL, IN_SH, "wgrad_in", True, True))
    (grad_x,), _ = _dx(dz1, dproj, comm.weight("w_in"), ts_a, after=comm.after("dx"))
    small = dict(w_pool=dwp, pool_scale=dps, ln1_g=dg1, ln1_b=db1, conv_w=dcw, conv_b=dcb,
                 ln2_g=dg2, ln2_b=db2)
    return loss, grad_x, small


CAST_ROWS = 64
SHARD_SHAPES = ((D_MODEL, IN_SH), (OUT_SH, D_MODEL), (D_MODEL, UP_SH), (DOWN_SH, D_MODEL))
N_BIG = len(SHARD_SHAPES)
CW_PAD = (8, 768)


def _mesh_pos():
    return lax.axis_index("x"), lax.axis_index("y"), lax.axis_index("c")


def _other_chips(x, y):
    return [(1 - x, y), (x, 1 - y), (1 - x, 1 - y)]


def _half_rows(w, which):
    hr = SHARD_SHAPES[w][0] // 2
    return pl.ds(pl.multiple_of(which * hr, 16), hr)


def _gather_weights(shards, cw8, full):
    def body(*refs):
        in_refs = refs[:N_BIG]
        cw_ref = refs[N_BIG]
        out_refs = refs[N_BIG + 1:2 * N_BIG + 1]
        cwo_ref = refs[2 * N_BIG + 1]
        stage = refs[2 * N_BIG + 2:3 * N_BIG + 2]
        send_sems, recv_sems, fsend_sems, frecv_sems, cw_send, cw_recv, local_sems = refs[3 * N_BIG + 2:]
        x, y, c = _mesh_pos()
        j0 = 2 * x + y
        chips = _other_chips(x, y)

        def cast_to_stage(w):
            def cast(i, carry):
                rows = pl.ds(pl.multiple_of(i * CAST_ROWS, CAST_ROWS), CAST_ROWS)
                stage[w][rows, :] = in_refs[w][rows, :].astype(BF16)
                return carry
            lax.fori_loop(0, SHARD_SHAPES[w][0] // CAST_ROWS, cast, 0)

        for w in full:
            cast_to_stage(w)

        jx, jy, jd = 2 * (1 - x) + y, 2 * x + (1 - y), 2 * (1 - x) + (1 - y)
        neighbours = [((1 - x, y, c), jx), ((x, 1 - y, c), jy)]
        passed = jnp.where(c == 0, jx, jy)
        pass_to = (jnp.where(c == 0, x, 1 - x), jnp.where(c == 0, 1 - y, y), c)

        def nbr(w, k, block):
            return pltpu.make_async_remote_copy(
                src_ref=stage[w].at[_half_rows(w, c), :], dst_ref=out_refs[w].at[block, _half_rows(w, c), :],
                send_sem=send_sems.at[w, k], recv_sem=recv_sems.at[w, k],
                device_id=neighbours[k][0], device_id_type=MESH)

        def relay(w, block):
            return pltpu.make_async_remote_copy(
                src_ref=out_refs[w].at[passed, _half_rows(w, c), :],
                dst_ref=out_refs[w].at[block, _half_rows(w, c), :],
                send_sem=send_sems.at[w, 2], recv_sem=recv_sems.at[w, 2],
                device_id=pass_to, device_id_type=MESH)

        def d2d(w, k, block, half):
            return pltpu.make_async_remote_copy(
                src_ref=out_refs[w].at[block, _half_rows(w, half), :],
                dst_ref=out_refs[w].at[block, _half_rows(w, half), :],
                send_sem=fsend_sems.at[w, k], recv_sem=frecv_sems.at[w, k],
                device_id=(x, y, 1 - c), device_id_type=MESH)

        def conv(k, block):
            chip = chips[k]
            return pltpu.make_async_remote_copy(
                src_ref=cw_ref, dst_ref=cwo_ref.at[block], send_sem=cw_send.at[k], recv_sem=cw_recv.at[k],
                device_id=(chip[0], chip[1], c), device_id_type=MESH)

        sent = [nbr(w, k, j0) for w in full for k in range(2)] + [conv(k, j0) for k in range(3)]
        for cp in sent:
            cp.start()
        for w in range(N_BIG):
            if w not in full:
                cast_to_stage(w)
        local = [pltpu.make_async_copy(stage[w], out_refs[w].at[j0], local_sems.at[w]) for w in range(N_BIG)]
        local.append(pltpu.make_async_copy(cw_ref, cwo_ref.at[j0], local_sems.at[N_BIG]))
        for cp in local:
            cp.start()
        for w in full:
            for k, (_, block) in enumerate(neighbours):
                nbr(w, k, block).wait_recv()
            later = [relay(w, passed)] + [d2d(w, k, block, c) for k, (_, block) in enumerate(neighbours)]
            for cp in later:
                cp.start()
            sent += later
        for w in full:
            relay(w, jd).wait_recv()
            fw = d2d(w, 2, jd, c)
            fw.start()
            sent.append(fw)
        for w in full:
            for k, block in enumerate([jx, jy, jd]):
                d2d(w, k, block, 1 - c).wait_recv()
        for k, chip in enumerate(chips):
            conv(k, 2 * chip[0] + chip[1]).wait_recv()
        for cp in sent:
            cp.wait_send()
        for cp in local:
            cp.wait()

    out_shape = [jax.ShapeDtypeStruct((N_SHARD,) + shp, BF16) for shp in SHARD_SHAPES]
    out_shape.append(jax.ShapeDtypeStruct((N_SHARD,) + CW_PAD, F32))
    return pl.pallas_call(
        body, name="gather_weights",
        in_specs=[_whole()] * (N_BIG + 1),
        out_specs=[HBM_SPEC] * (N_BIG + 1),
        out_shape=out_shape,
        scratch_shapes=[pltpu.VMEM(shp, BF16) for shp in SHARD_SHAPES] + [
            pltpu.SemaphoreType.DMA((N_BIG, 3)), pltpu.SemaphoreType.DMA((N_BIG, 3)),
            pltpu.SemaphoreType.DMA((N_BIG, 3)), pltpu.SemaphoreType.DMA((N_BIG, 3)),
            pltpu.SemaphoreType.DMA((3,)), pltpu.SemaphoreType.DMA((3,)),
            pltpu.SemaphoreType.DMA((N_BIG + 1,))],
        compiler_params=pltpu.CompilerParams(vmem_limit_bytes=VMEM_LIMIT),
    )(*shards, cw8)


def _gather_rider(arrays, ops):
    ws = sorted(arrays)

    def make(inplace, srcs, lands, send_sems, recv_sems):
        del srcs, lands
        x, y, c = _mesh_pos()
        j0, jx, jy, jd = 2 * x + y, 2 * (1 - x) + y, 2 * x + (1 - y), 2 * (1 - x) + (1 - y)
        x_nbr, y_nbr, sibling = (1 - x, y, c), (x, 1 - y, c), (x, y, 1 - c)
        starts, waits = [], []
        for n, (kind, w, (r0, nr)) in enumerate(ops):
            ref = inplace[ws.index(w)]
            hr = SHARD_SHAPES[w][0] // 2
            rows = lambda core: pl.ds(pl.multiple_of(core * hr + r0, 16), nr)
            mine, theirs = rows(c), rows(1 - c)
            if kind == "ici":
                moves = [(ref.at[j0, mine, :], x_nbr, ref.at[jx, mine, :]),
                         (ref.at[j0, mine, :], y_nbr, ref.at[jy, mine, :]),
                         (ref.at[j0, mine, :], (1 - x, 1 - y, c), ref.at[jd, mine, :])]
            elif kind == "nbr":
                moves = [(ref.at[j0, mine, :], x_nbr, ref.at[jx, mine, :]),
                         (ref.at[j0, mine, :], y_nbr, ref.at[jy, mine, :])]
            elif kind == "relay":
                passed = jnp.where(c == 0, jx, jy)
                to = (jnp.where(c == 0, x, 1 - x), jnp.where(c == 0, 1 - y, y), c)
                moves = [(ref.at[passed, mine, :], to, ref.at[jd, mine, :])]
            else:
                blocks = dict(d2d=[jx, jy, jd], d2d_nbr=[jx, jy], d2d_diag=[jd])[kind]
                moves = [(ref.at[b, mine, :], sibling, ref.at[b, theirs, :]) for b in blocks]
            for k, (src, to, landing) in enumerate(moves):
                sems = dict(send_sem=send_sems.at[3 * n + k], recv_sem=recv_sems.at[3 * n + k],
                            device_id=to, device_id_type=MESH)
                send = pltpu.make_async_remote_copy(src_ref=src, dst_ref=src, **sems)
                arrival = pltpu.make_async_remote_copy(src_ref=src, dst_ref=landing, **sems)
                starts.append(send)
                waits += [arrival.wait_recv, send.wait_send]
        return starts, waits

    return _Rider([arrays[w] for w in ws], [], [], 3 * len(ops), make)


def _whole_half(w):
    return (0, SHARD_SHAPES[w][0] // 2)


def _pair_rider(ws, g16s):
    def make(inplace, srcs, lands, send_sems, recv_sems):
        del inplace
        x, y, c = _mesh_pos()
        copies = [pltpu.make_async_remote_copy(
            src_ref=srcs[i].at[:, _half_rows(w, 1 - c), :], dst_ref=lands[i],
            send_sem=send_sems.at[i], recv_sem=recv_sems.at[i], device_id=(x, y, 1 - c), device_id_type=MESH)
            for i, w in enumerate(ws)]
        return copies, [cp.wait for cp in copies]

    lands = [jax.ShapeDtypeStruct((N_SHARD, SHARD_SHAPES[w][0] // 2, SHARD_SHAPES[w][1]), BF16) for w in ws]
    return _Rider([], g16s, lands, len(ws), make)


def _chip_rider(ws, p16s, rows=None, landing=None):
    def make(inplace, srcs, lands, send_sems, recv_sems):
        x, y, c = _mesh_pos()
        dsts = inplace if landing is not None else lands
        copies = []
        for i, w in enumerate(ws):
            r0, nr = rows if rows is not None else _whole_half(w)
            for k, chip in enumerate(_other_chips(x, y)):
                copies.append(pltpu.make_async_remote_copy(
                    src_ref=srcs[i].at[2 * chip[0] + chip[1], pl.ds(r0, nr), :],
                    dst_ref=dsts[i].at[k, pl.ds(r0, nr), :],
                    send_sem=send_sems.at[3 * i + k], recv_sem=recv_sems.at[3 * i + k],
                    device_id=(chip[0], chip[1], c), device_id_type=MESH))
        return copies, [cp.wait for cp in copies]

    lands = [jax.ShapeDtypeStruct((3, SHARD_SHAPES[w][0] // 2, SHARD_SHAPES[w][1]), BF16) for w in ws]
    if landing is not None:
        return _Rider(landing, p16s, [], 3 * len(ws), make)
    return _Rider([], p16s, lands, 3 * len(ws), make)


def _final_rider(halves):
    def make(inplace, srcs, lands, send_sems, recv_sems):
        del inplace
        x, y, c = _mesh_pos()
        copies = [pltpu.make_async_remote_copy(
            src_ref=srcs[i], dst_ref=lands[i], send_sem=send_sems.at[i], recv_sem=recv_sems.at[i],
            device_id=(x, y, 1 - c), device_id_type=MESH) for i in range(len(halves))]
        return copies, [cp.wait for cp in copies]

    return _Rider([], halves, [jax.ShapeDtypeStruct(h.shape, h.dtype) for h in halves], len(halves), make)


def _comm_only(name, riders):
    _, res = _call(lambda: None, name=name, grid=(), in_specs=[], out_specs=[], out_shape=[], operands=(),
                   riders=riders)
    return res


class _SemList:
    def __init__(self, refs):
        self.at = list(refs)


def _split_start(name, rider):
    assert not rider.inplace
    ns, nl, n = len(rider.srcs), len(rider.lands), rider.n_copies

    def body(*refs):
        srcs, lands = refs[:ns], refs[ns:ns + nl]
        sems = refs[ns + nl:ns + nl + 2 * n]
        token = refs[-1]
        starts, _ = rider.make([], srcs, lands, _SemList(sems[:n]), _SemList(sems[n:]))
        for cp in starts:
            cp.start()
        token[...] = jnp.zeros_like(token)

    buffers = [pltpu.with_memory_space_constraint(a, pltpu.HBM) for a in rider.srcs]
    buffers += [pltpu.with_memory_space_constraint(lax.empty(s.shape, s.dtype), pltpu.HBM) for s in rider.lands]
    hbm = pl.BlockSpec(memory_space=pltpu.HBM)
    sem = pl.BlockSpec(memory_space=pltpu.SEMAPHORE)
    outs = pl.pallas_call(
        body, name=name,
        out_shape=tuple([pltpu.SemaphoreType.DMA(())] * (2 * n) + [pltpu.HBM(b.shape, b.dtype) for b in buffers]
                        + [jax.ShapeDtypeStruct((8, 128), F32)]),
        in_specs=[hbm] * (ns + nl),
        out_specs=tuple([sem] * (2 * n) + [hbm] * (ns + nl) + [_whole()]),
        input_output_aliases={i: 2 * n + i for i in range(ns + nl)},
        compiler_params=pltpu.CompilerParams(has_side_effects=pltpu.SideEffectType.DATAFLOW_SIDE_EFFECTING),
    )(*buffers)
    return (rider, outs[:2 * n], outs[2 * n:2 * n + ns + nl]), outs[-1]


def _split_wait(name, state, after):
    rider, sems, buffers = state
    ns, nl, n = len(rider.srcs), len(rider.lands), rider.n_copies

    def body(*refs):
        srcs, lands = refs[:ns], refs[ns:ns + nl]
        sem_refs = refs[ns + nl:ns + nl + 2 * n]
        _, waits = rider.make([], srcs, lands, _SemList(sem_refs[:n]), _SemList(sem_refs[n:]))
        for wait in waits:
            wait()

    hbm = pl.BlockSpec(memory_space=pltpu.HBM)
    sem = pl.BlockSpec(memory_space=pltpu.SEMAPHORE)
    outs = pl.pallas_call(
        body, name=name,
        out_shape=tuple(pltpu.HBM(b.shape, b.dtype) for b in buffers),
        in_specs=[hbm] * (ns + nl) + [sem] * (2 * n) + [HBM_SPEC],
        out_specs=tuple([hbm] * (ns + nl)),
        input_output_aliases={i: i for i in range(ns + nl)},
        compiler_params=pltpu.CompilerParams(has_side_effects=pltpu.SideEffectType.DATAFLOW_SIDE_EFFECTING),
    )(*buffers, *sems, after)
    return list(outs[:ns]), list(outs[ns:])


def _pair_sum(pos, ws, g32s, recvs):
    n = len(ws)

    def body(pos_ref, *refs):
        del pos_ref
        g_refs, r_refs = refs[:n], refs[n:2 * n]
        p32_refs, p16_refs = refs[2 * n:3 * n], refs[3 * n:]
        for i in range(n):
            tot = g_refs[i][...] + r_refs[i][...].astype(F32)
            p32_refs[i][...] = tot
            p16_refs[i][...] = tot.astype(BF16)

    halves = [(SHARD_SHAPES[w][0] // 2, SHARD_SHAPES[w][1]) for w in ws]
    own = [pl.BlockSpec((None, None) + h, lambda j, pos_ref: (j, pos_ref[0], 0, 0)) for h in halves]
    blk = [pl.BlockSpec((None,) + h, lambda j, pos_ref: (j, 0, 0)) for h in halves]
    g4 = [g.reshape((N_SHARD, 2) + h) for g, h in zip(g32s, halves)]
    outs = pl.pallas_call(
        body, name="pair_sum_" + "_".join(str(w) for w in ws),
        grid_spec=pltpu.PrefetchScalarGridSpec(
            num_scalar_prefetch=1, grid=(N_SHARD,), in_specs=own + blk, out_specs=blk + blk),
        out_shape=[jax.ShapeDtypeStruct((N_SHARD,) + h, F32) for h in halves]
        + [jax.ShapeDtypeStruct((N_SHARD,) + h, BF16) for h in halves],
        compiler_params=_params(("arbitrary",)),
    )(pos, *g4, *recvs)
    return outs[:n], outs[n:]


def _chip_sum(pos, p32s, recvs):
    parts = 2

    def body(pos_ref, *refs):
        del pos_ref
        p_refs, r_refs, f_refs = refs[:N_BIG], refs[N_BIG:2 * N_BIG], refs[2 * N_BIG:]
        for w in range(N_BIG):
            f_refs[w][...] = ((p_refs[w][...] + r_refs[w][0].astype(F32)) + r_refs[w][1].astype(F32)) \
                + r_refs[w][2].astype(F32)

    quarters = [(r // 2 // parts, cc) for r, cc in SHARD_SHAPES]
    own = [pl.BlockSpec((None,) + qt, lambda i, pos_ref: (pos_ref[1], i, 0)) for qt in quarters]
    rcv = [pl.BlockSpec((3,) + qt, lambda i, pos_ref: (0, i, 0)) for qt in quarters]
    out = [pl.BlockSpec(qt, lambda i, pos_ref: (i, 0)) for qt in quarters]
    return pl.pallas_call(
        body, name="chip_sum",
        grid_spec=pltpu.PrefetchScalarGridSpec(
            num_scalar_prefetch=1, grid=(parts,), in_specs=own + rcv, out_specs=out),
        out_shape=[jax.ShapeDtypeStruct((r // 2, cc), F32) for r, cc in SHARD_SHAPES],
        compiler_params=_params(("arbitrary",)),
    )(pos, *p32s, *recvs)


def _adamw(w, g, m, v):
    m_new = ADAM_B1 * m + (1.0 - ADAM_B1) * g
    v_new = ADAM_B2 * v + (1.0 - ADAM_B2) * (g * g)
    m_hat = m_new / (1.0 - ADAM_B1 ** ADAM_STEP)
    v_hat = v_new / (1.0 - ADAM_B2 ** ADAM_STEP)
    delta = -ADAM_LR * (m_hat / (jnp.sqrt(v_hat) + ADAM_EPS) + ADAM_WD * w)
    return delta, m_new, v_new


def _adam_big(pos, mine, theirs, ws, ms, vs):
    nb = 4

    def body(pos_ref, *refs):
        hf = pl.program_id(0)
        groups = [refs[i * N_BIG:(i + 1) * N_BIG] for i in range(9)]
        f_refs, t_refs, w_refs, m_refs, v_refs, go_refs, do_refs, mo_refs, vo_refs = groups
        for w in range(N_BIG):
            g = jnp.where(hf == pos_ref[0], f_refs[w][...], t_refs[w][...])
            delta, m_new, v_new = _adamw(w_refs[w][...], g, m_refs[w][...], v_refs[w][...])
            go_refs[w][...] = g
            do_refs[w][...] = delta
            mo_refs[w][...] = m_new
            vo_refs[w][...] = v_new

    blocks = [(r // 2 // nb, cc) for r, cc in SHARD_SHAPES]
    half = [pl.BlockSpec(b, lambda hf, i, pos_ref: (i, 0)) for b in blocks]
    full = [pl.BlockSpec((None,) + b, lambda hf, i, pos_ref: (0, hf * nb + i, 0)) for b in blocks]
    shapes = [jax.ShapeDtypeStruct((1,) + shp, F32) for shp in SHARD_SHAPES]
    outs = pl.pallas_call(
        body, name="adam_big",
        grid_spec=pltpu.PrefetchScalarGridSpec(
            num_scalar_prefetch=1, grid=(2, nb), in_specs=half + half + full * 3, out_specs=full * 4),
        out_shape=shapes * 4,
        compiler_params=_params(("arbitrary", "arbitrary")),
    )(pos, *mine, *theirs, *ws, *ms, *vs)
    return [outs[i * N_BIG:(i + 1) * N_BIG] for i in range(4)]


SMALL_ROWS = 8
ROW_CONV_B, ROW_POOL_SCALE, ROW_LN1_G, ROW_LN1_B, ROW_LN2_G, ROW_LN2_B, ROW_LOSS = range(7)
SMALL_VECS = ((ROW_CONV_B, D_FF), (ROW_POOL_SCALE, POOL_W), (ROW_LN1_G, D_MODEL), (ROW_LN1_B, D_MODEL),
              (ROW_LN2_G, D_MODEL), (ROW_LN2_B, D_MODEL))


def _small_update(loss, dwp, dcw4, vec_grads, wp, cwp, vec_ws, m_wp, m_cwp, vec_ms, v_wp, v_cwp, vec_vs,
                  riders=(), after=()):
    nv = len(SMALL_VECS)

    def body(*refs):
        loss_ref, dwp_ref, dcw_ref = refs[0:3]
        gvec = refs[3:3 + nv]
        o = 3 + nv
        wp_ref, cw_ref = refs[o:o + 2]
        wvec = refs[o + 2:o + 2 + nv]
        o += 2 + nv
        mwp_ref, mcw_ref = refs[o:o + 2]
        mvec = refs[o + 2:o + 2 + nv]
        o += 2 + nv
        vwp_ref, vcw_ref = refs[o:o + 2]
        vvec = refs[o + 2:o + 2 + nv]
        o += 2 + nv
        loss_out = refs[o]
        outs = refs[o + 1:o + 1 + 4 * (2 + nv)]
        o += 1 + 4 * (2 + nv)
        (vec_scr, sib_a, sib_b, sib_c, all_a, all_b, all_c,
         send1, recv1, send2, recv2) = refs[o:]
        x, y, c = _mesh_pos()
        j0 = 2 * x + y
        chips = _other_chips(x, y)

        vec_scr[...] = jnp.zeros_like(vec_scr)
        for (row, n), ref in zip(SMALL_VECS, gvec):
            vec_scr[row:row + 1, 0:n] = ref[...]
        vec_scr[ROW_LOSS:ROW_LOSS + 1, 0:HEAD_DIM] = jnp.broadcast_to(loss_ref[...], (1, HEAD_DIM))

        mine = (dwp_ref, vec_scr, dcw_ref)
        sib = (sib_a, sib_b, sib_c)
        every = (all_a, all_b, all_c)
        first = [pltpu.make_async_remote_copy(
            src_ref=mine[b], dst_ref=sib[b], send_sem=send1.at[b], recv_sem=recv1.at[b],
            device_id=(x, y, 1 - c), device_id_type=MESH) for b in range(3)]
        for cp in first:
            cp.start()
        for cp in first:
            cp.wait()
        for b in range(3):
            every[b][j0] = mine[b][...] + sib[b][...]

        def ici(b, k, block):
            chip = chips[k]
            return pltpu.make_async_remote_copy(
                src_ref=every[b].at[block], dst_ref=every[b].at[block],
                send_sem=send2.at[b, k], recv_sem=recv2.at[b, k],
                device_id=(chip[0], chip[1], c), device_id_type=MESH)

        second = [ici(b, k, j0) for b in range(3) for k in range(3)]
        for cp in second:
            cp.start()
        for k, chip in enumerate(chips):
            for b in range(3):
                ici(b, k, 2 * chip[0] + chip[1]).wait_recv()
        for cp in second:
            cp.wait_send()

        tot_a = ((all_a[0] + all_a[1]) + all_a[2]) + all_a[3]
        tot_b = ((all_b[0] + all_b[1]) + all_b[2]) + all_b[3]
        all_c[0] = ((all_c[0] + all_c[1]) + all_c[2]) + all_c[3]
        tot_c = all_c[0, j0]
        loss_out[...] = tot_b[ROW_LOSS:ROW_LOSS + 1, 0:1]

        grads = [tot_a, tot_c] + [tot_b[row:row + 1, 0:n] for row, n in SMALL_VECS]
        w_all = [wp_ref, cw_ref] + list(wvec)
        m_all = [mwp_ref, mcw_ref] + list(mvec)
        v_all = [vwp_ref, vcw_ref] + list(vvec)
        np_ = 2 + nv
        for p in range(np_):
            g = grads[p]
            delta, m_new, v_new = _adamw(w_all[p][...], g, m_all[p][...], v_all[p][...])
            outs[p][...] = g
            outs[np_ + p][...] = delta
            outs[2 * np_ + p][...] = m_new
            outs[3 * np_ + p][...] = v_new

    pshapes = [wp.shape, CW_PAD] + [wv.shape for wv in vec_ws]
    out_shape = [jax.ShapeDtypeStruct((1, 1), F32)] + [jax.ShapeDtypeStruct(s, F32) for s in pshapes] * 4
    a_shape = dwp.shape
    b_shape = (SMALL_ROWS, D_FF)
    c_shape = dcw4.shape
    n_in = 3 + nv + 3 * (2 + nv)
    outs, rider_res = _call(
        body, name="small_update", grid=(),
        in_specs=[_whole()] * n_in, out_specs=[_whole()] * len(out_shape), out_shape=out_shape,
        scratch_shapes=[pltpu.VMEM(b_shape, F32),
                        pltpu.VMEM(a_shape, F32), pltpu.VMEM(b_shape, F32), pltpu.VMEM(c_shape, F32),
                        pltpu.VMEM((N_SHARD,) + a_shape, F32), pltpu.VMEM((N_SHARD,) + b_shape, F32),
                        pltpu.VMEM((N_SHARD,) + c_shape, F32),
                        pltpu.SemaphoreType.DMA((3,)), pltpu.SemaphoreType.DMA((3,)),
                        pltpu.SemaphoreType.DMA((3, 3)), pltpu.SemaphoreType.DMA((3, 3))],
        operands=(loss, dwp, dcw4, *vec_grads, wp, cwp, *vec_ws, m_wp, m_cwp, *vec_ms, v_wp, v_cwp, *vec_vs),
        riders=riders, after=after,
    )
    np_ = 2 + nv
    return outs[0], [outs[1 + i * np_:1 + (i + 1) * np_] for i in range(4)], rider_res


def _pad_cw(a):
    pad = [(0, 0)] * (a.ndim - 2) + [(0, CW_PAD[0] - a.shape[-2]), (0, CW_PAD[1] - a.shape[-1])]
    return jnp.pad(a, pad)


def kernel(x, w_in, w_pool, pool_scale, w_out, ln1_g, ln1_b, w_up, conv_w, conv_b, w_down, ln2_g, ln2_b, loss_target, m_w_in, m_w_pool, m_pool_scale, m_w_out, m_ln1_g, m_ln1_b, m_w_up, m_conv_w, m_conv_b, m_w_down, m_ln2_g, m_ln2_b, v_w_in, v_w_pool, v_pool_scale, v_w_out, v_ln1_g, v_ln1_b, v_w_up, v_conv_w, v_conv_b, v_w_down, v_ln2_g, v_ln2_b):
    pos = jnp.stack([lax.axis_index("c"), 2 * lax.axis_index("x") + lax.axis_index("y")]).astype(jnp.int32)
    order = ("w_in", "w_out", "w_up", "w_down")
    w_in_i, w_out_i, w_up_i, w_down_i = range(N_BIG)

    gathered = _gather_weights([w_in[0], w_out[0], w_up[0], w_down[0]], _pad_cw(conv_w[0]), (w_in_i,))
    cw_full = jnp.transpose(gathered[N_BIG][:, 0:3, 0:DOWN_SH], (1, 0, 2)).reshape(3, D_FF)
    up_a, up_b, up_c = (0, 224), (224, 160), (384, 128)
    assert up_c[0] + up_c[1] == SHARD_SHAPES[w_up_i][0] // 2

    class MeshComm:
        def __init__(self):
            self.w = {i: gathered[i] for i in range(N_BIG)}
            self.g32, self.g16, self.recv_a, self.p32, self.p16, self.recv_b = {}, {}, {}, {}, {}, {}
            self.up_complete = False

        def weight(self, name):
            i = order.index(name)
            if name == "w_up" and not self.up_complete:
                (arrs, _), = _comm_only("gather_up_last", [_gather_rider(
                    {i: self.w[i]}, [("d2d_diag", i, up_b), ("d2d", i, up_c)])])
                self.w[i], self.up_complete = arrs[0], True
            full = self.w[i]
            return full.reshape(-1, full.shape[-1]) if name in ("w_out", "w_down") else full

        def _gather(self, ws, ops):
            return _gather_rider({w: self.w[w] for w in ws}, ops), ("w", ws)

        def _pair(self, ws):
            return _pair_rider(ws, [self.g16[w] for w in ws]), ("recv_a", ws)

        def _chip(self, ws, rows=None, resume=False):
            landing = [self.recv_b[w] for w in ws] if resume else None
            return _chip_rider(ws, [self.p16[w] for w in ws], rows, landing), ("recv_b", ws)

        def plan(self, call):
            out_all, down_all = _whole_half(w_out_i), _whole_half(w_down_i)
            if call == "proj_pool":
                return [self._gather([w_out_i, w_up_i, w_down_i],
                                     [("ici", w_out_i, out_all), ("nbr", w_down_i, down_all),
                                      ("nbr", w_up_i, up_a)])]
            if call == "retention_fwd":
                return [self._gather([w_out_i, w_up_i, w_down_i],
                                     [("d2d", w_out_i, out_all),
                                      ("relay", w_down_i, down_all), ("d2d_nbr", w_down_i, down_all),
                                      ("relay", w_up_i, up_a), ("d2d_nbr", w_up_i, up_a), ("nbr", w_up_i, up_b)])]
            if call == "outproj_ln1":
                return [self._gather([w_up_i, w_down_i],
                                     [("d2d_diag", w_down_i, down_all), ("d2d_diag", w_up_i, up_a),
                                      ("relay", w_up_i, up_b), ("d2d_nbr", w_up_i, up_b), ("ici", w_up_i, up_c)])]
            if call == "wgrad_down":
                return [self._pair([w_up_i, w_out_i])]
            if call == "mix_bwd":
                return [self._chip([w_up_i], (0, 256))]
            if call == "retention_bwd":
                return [self._chip([w_up_i], (256, 256), resume=True), self._pair([w_down_i])]
            if call == "wgrad_in":
                return [self._chip([w_out_i, w_down_i])]
            return []

        def after(self, call):
            return (self.in_token,) if call == "dx" else ()

        def riders(self, call):
            self.pending = self.plan(call)
            return [r for r, _ in self.pending]

        def landed(self, call, results):
            for (_, (slot, ws)), (inplace, lands) in zip(self.pending, results):
                for w, arr in zip(ws, inplace if len(inplace) else lands):
                    getattr(self, slot)[w] = arr
            if call == "wgrad_down":
                self._sum([w_up_i, w_out_i])
            if call == "retention_bwd":
                self._sum([w_down_i])

        def _sum(self, ws):
            p32s, p16s = _pair_sum(pos, ws, [self.g32[w] for w in ws], [self.recv_a[w] for w in ws])
            for w, p32, p16 in zip(ws, p32s, p16s):
                self.p32[w], self.p16[w] = p32, p16

        def gradient(self, name, g32, g16):
            w = order.index(name)
            shape = (N_SHARD,) + SHARD_SHAPES[w]
            self.g32[w], self.g16[w] = g32.reshape(shape), g16.reshape(shape)
            if name == "w_in":
                (_, lands), = _comm_only("pair_exchange_in", [self._pair([w])[0]])
                self.recv_a[w] = lands[0]
                self._sum([w])
                self.in_state, self.in_token = _split_start("chip_exchange_in_start", self._chip([w])[0])

    comm = MeshComm()
    loss, grad_x, small = _local_step(x[0], loss_target[0], cw_full, conv_b, w_pool[0], pool_scale,
                                      ln1_g, ln1_b, ln2_g, ln2_b, comm)

    dcw4 = _pad_cw(jnp.transpose(small["conv_w"].reshape(3, N_SHARD, DOWN_SH), (1, 0, 2)))
    vec_names = ("conv_b", "pool_scale", "ln1_g", "ln1_b", "ln2_g", "ln2_b")
    given = dict(w_pool=w_pool, pool_scale=pool_scale, ln1_g=ln1_g, ln1_b=ln1_b, conv_w=conv_w, conv_b=conv_b,
                 ln2_g=ln2_g, ln2_b=ln2_b)
    given_m = dict(w_pool=m_w_pool, pool_scale=m_pool_scale, ln1_g=m_ln1_g, ln1_b=m_ln1_b, conv_w=m_conv_w,
                   conv_b=m_conv_b, ln2_g=m_ln2_g, ln2_b=m_ln2_b)
    given_v = dict(w_pool=v_w_pool, pool_scale=v_pool_scale, ln1_g=v_ln1_g, ln1_b=v_ln1_b, conv_w=v_conv_w,
                   conv_b=v_conv_b, ln2_g=v_ln2_g, ln2_b=v_ln2_b)
    args = []
    for src in (given, given_m, given_v):
        args += [src["w_pool"][0], _pad_cw(src["conv_w"][0]), [src[n] for n in vec_names]]
    _, (comm.recv_b[w_in_i],) = _split_wait("chip_exchange_in_wait", comm.in_state, grad_x)
    every = range(N_BIG)
    mine = _chip_sum(pos, [comm.p32[w] for w in every], [comm.recv_b[w] for w in every])
    final_state, final_token = _split_start("pair_exchange_f32_start", _final_rider(mine))
    loss_tot, small_out, _ = _small_update(loss, small["w_pool"], dcw4, [small[n] for n in vec_names], *args,
                                           after=(final_token,))
    mine, theirs = _split_wait("pair_exchange_f32_wait", final_state, small_out[0][0])
    big_out = _adam_big(pos, mine, theirs, [w_in, w_out, w_up, w_down], [m_w_in, m_w_out, m_w_up, m_w_down],
                        [v_w_in, v_w_out, v_w_up, v_w_down])

    names = ("w_in", "w_pool", "pool_scale", "w_out", "ln1_g", "ln1_b", "w_up", "conv_w", "conv_b", "w_down",
             "ln2_g", "ln2_b")
    small_names = ("w_pool", "conv_w") + vec_names
    result = [loss_tot.reshape(()), grad_x[None]]
    for kind in range(4):
        for n in names:
            if n in order:
                result.append(big_out[kind][order.index(n)])
            else:
                val = small_out[kind][small_names.index(n)]
                if n == "conv_w":
                    val = val[0:3, 0:DOWN_SH][None]
                elif n == "w_pool":
                    val = val[None]
                result.append(val)
    return tuple(result)
```

```python
import functools
import math

import numpy as np
import jax
import jax.numpy as jnp
from jax import lax
from jax.experimental import pallas as pl
from jax.experimental.pallas import tpu as pltpu

F32 = jnp.float32
BF16 = jnp.bfloat16

D_MODEL = 1024
HEADS = 4
HEAD_DIM = 128
RET_W = HEADS * HEAD_DIM
POOL_WINDOWS = (2, 4, 8, 16)
POOL_W = 512
IN_W = 4 * RET_W + POOL_W
D_FF = 2816
N_SHARD = 4
IN_SH = IN_W // N_SHARD
UP_SH = 2 * D_FF // N_SHARD
DOWN_SH = D_FF // N_SHARD
OUT_SH = D_MODEL // N_SHARD
ROPE_BASE = 10000.0
LN_EPS = 1e-5
RMS_EPS = 1e-6
ALPHA = 2.0 ** 0.25
K_SCALE = HEAD_DIM ** -0.5
SUPER = 256
CHUNK = 64
POOL_HALO = 16
CONV_HALO = 8
FFN_STRIP = 128
LN_ROWS = 32

ADAM_LR = 0.001
ADAM_B1 = 0.9
ADAM_B2 = 0.999
ADAM_EPS = 1e-08
ADAM_WD = 0.01
ADAM_STEP = 10

MESH = pl.DeviceIdType.MESH
VMEM_LIMIT = 56 * 1024 * 1024


def _dot(a, b):
    return jnp.dot(a, b, preferred_element_type=F32)


def _dot_nt(a, b):
    return lax.dot_general(a, b, (((1,), (1,)), ((), ())), preferred_element_type=F32)


def _dot_tn(a, b):
    return lax.dot_general(a, b, (((0,), (0,)), ((), ())), preferred_element_type=F32)


def _sigmoid(x):
    return 1.0 / (1.0 + jnp.exp(-x))


def _params(sem):
    return pltpu.CompilerParams(dimension_semantics=sem, vmem_limit_bytes=VMEM_LIMIT)


def _whole():
    return pl.BlockSpec(memory_space=pltpu.VMEM)


HBM_SPEC = pl.BlockSpec(memory_space=pl.ANY)


class _Rider:
    def __init__(self, inplace, srcs, lands, n_copies, make):
        self.inplace, self.srcs, self.lands, self.n_copies, self.make = list(inplace), list(srcs), list(lands), n_copies, make


def _call(body, *, name, grid, in_specs, out_specs, out_shape, operands, scratch_shapes=(), sem=(),
          aliases=None, riders=(), after=()):
    n_in, n_out, n_scr = len(in_specs), len(out_shape), len(scratch_shapes)
    in_specs, out_specs, out_shape = list(in_specs), list(out_specs), list(out_shape)
    operands, scratch_shapes, aliases = list(operands), list(scratch_shapes), dict(aliases or {})
    in_specs += [_whole()] * len(after)
    operands += list(after)
    for r in riders:
        for a in r.inplace:
            aliases[len(in_specs)] = len(out_shape)
            in_specs.append(HBM_SPEC)
            operands.append(a)
            out_specs.append(HBM_SPEC)
            out_shape.append(jax.ShapeDtypeStruct(a.shape, a.dtype))
        for a in r.srcs:
            in_specs.append(HBM_SPEC)
            operands.append(a)
        for shp in r.lands:
            out_specs.append(HBM_SPEC)
            out_shape.append(shp)
        scratch_shapes += [pltpu.SemaphoreType.DMA((r.n_copies,)), pltpu.SemaphoreType.DMA((r.n_copies,))]

    def full(*refs):
        ins = refs[:n_in]
        at = n_in + len(after)
        r_srcs = []
        for r in riders:
            at += len(r.inplace)
            r_srcs.append(refs[at:at + len(r.srcs)])
            at += len(r.srcs)
        outs = refs[at:at + n_out]
        at += n_out
        r_outs = []
        for r in riders:
            r_outs.append((refs[at:at + len(r.inplace)], refs[at + len(r.inplace):at + len(r.inplace) + len(r.lands)]))
            at += len(r.inplace) + len(r.lands)
        scr = refs[at:at + n_scr]
        at += n_scr
        r_sems = [refs[at + 2 * i:at + 2 * i + 2] for i in range(len(riders))]

        def copies():
            return [r.make(r_outs[i][0], r_srcs[i], r_outs[i][1], r_sems[i][0], r_sems[i][1])
                    for i, r in enumerate(riders)]

        def start():
            for starts, _ in copies():
                for cp in starts:
                    cp.start()

        def finish():
            for _, waits in copies():
                for wait in waits:
                    wait()

        if riders and grid:
            first = functools.reduce(jnp.logical_and, [pl.program_id(d) == 0 for d in range(len(grid))])
            last = functools.reduce(jnp.logical_and, [pl.program_id(d) == grid[d] - 1 for d in range(len(grid))])
            pl.when(first)(start)
            body(*ins, *outs, *scr)
            pl.when(last)(finish)
        else:
            if riders:
                start()
            body(*ins, *outs, *scr)
            if riders:
                finish()

    params = _params(sem) if grid else pltpu.CompilerParams(vmem_limit_bytes=VMEM_LIMIT)
    res = pl.pallas_call(
        full, name=name, grid=grid, in_specs=in_specs, out_specs=out_specs, out_shape=out_shape,
        scratch_shapes=scratch_shapes, input_output_aliases=aliases, compiler_params=params,
    )(*operands)
    outs, at, rider_res = res[:n_out], n_out, []
    for r in riders:
        rider_res.append((res[at:at + len(r.inplace)], res[at + len(r.inplace):at + len(r.inplace) + len(r.lands)]))
        at += len(r.inplace) + len(r.lands)
    return list(outs), rider_res


def _gammas():
    return [1.0 - 2.0 ** (-5.0 - h) for h in range(HEADS)]


def _decay_tables():
    idx = np.arange(SUPER)
    dist = np.abs(idx[:, None] - idx[None, :]).astype(np.float64)
    visible = (idx[None, :] // CHUNK) <= (idx[:, None] // CHUNK)
    mask = np.stack([np.where(visible, g ** dist, 0.0) for g in _gammas()])
    qd = np.concatenate([np.repeat((g ** (idx + 1.0))[:, None], HEAD_DIM, 1) for g in _gammas()], 1)
    kd = np.concatenate([np.repeat((g ** (SUPER - 1.0 - idx))[:, None], HEAD_DIM, 1) for g in _gammas()], 1)
    return (jnp.asarray(mask, F32), jnp.asarray(qd, F32), jnp.asarray(kd, F32))


def _rope_tables(s):
    inv_freq = ROPE_BASE ** (-np.arange(0, HEAD_DIM, 2, dtype=np.float64) / HEAD_DIM)
    ang = np.arange(s, dtype=np.float64)[:, None] * inv_freq[None, :]
    cos, sin = np.cos(ang), np.sin(ang)
    return (jnp.asarray(np.concatenate([cos, cos], 1), F32),
            jnp.asarray(np.concatenate([-sin, sin], 1), F32))


def _rope(t, cosf, sinf):
    return t * cosf + pltpu.roll(t, HEAD_DIM // 2, 1) * sinf


def _rope_t(t, cosf, sinf):
    return t * cosf - pltpu.roll(t, HEAD_DIM // 2, 1) * sinf


def _layernorm_fwd(z):
    mu = jnp.mean(z, axis=-1, keepdims=True)
    zc = z - mu
    var = jnp.mean(zc * zc, axis=-1, keepdims=True)
    rstd = lax.rsqrt(var + LN_EPS)
    return zc * rstd, rstd


def _layernorm_bwd(dy, xhat, rstd, gain):
    dxh = dy * gain
    m1 = jnp.mean(dxh, axis=-1, keepdims=True)
    m2 = jnp.mean(dxh * xhat, axis=-1, keepdims=True)
    return rstd * (dxh - m1 - xhat * m2)


def _proj_pool(x, win4, cosf, sinf, wpool, pscale, ts, riders=(), after=()):
    s = x.shape[0]
    nt = s // ts

    def body(x_ref, w_ref, cos_ref, sin_ref, wp_ref, ps_ref,
             xb_ref, q_ref, k_ref, v_ref, g_ref, pooled_ref, cat_ref, proj_scr, pext_scr):
        i = pl.program_id(0)
        xb = x_ref[...].astype(BF16)
        xb_ref[...] = xb
        for j in range(N_SHARD):
            proj_scr[:, j * IN_SH:(j + 1) * IN_SH] = _dot(xb, w_ref[j])
        cosf_t = cos_ref[...]
        sinf_t = sin_ref[...]
        for h in range(HEADS):
            lo = h * HEAD_DIM
            q_ref[:, lo:lo + HEAD_DIM] = _rope(proj_scr[:, lo:lo + HEAD_DIM], cosf_t, sinf_t).astype(BF16)
            kk = _rope(proj_scr[:, RET_W + lo:RET_W + lo + HEAD_DIM], cosf_t, sinf_t) * K_SCALE
            k_ref[:, lo:lo + HEAD_DIM] = kk.astype(BF16)
        v_ref[...] = proj_scr[:, 2 * RET_W:3 * RET_W].astype(BF16)
        g_ref[...] = proj_scr[:, 3 * RET_W:4 * RET_W]

        @pl.when(i == 0)
        def _():
            pext_scr[0:POOL_HALO, :] = jnp.zeros((POOL_HALO, POOL_W), F32)

        pext_scr[POOL_HALO:POOL_HALO + ts, :] = proj_scr[:, 4 * RET_W:IN_W]
        pos = (i * ts + lax.broadcasted_iota(jnp.int32, (ts, 1), 0) + 1).astype(F32)
        for gi, w in enumerate(POOL_WINDOWS):
            lo = gi * HEAD_DIM
            ext = pext_scr[:, lo:lo + HEAD_DIM]
            acc = ext
            shift = 1
            while shift < w:
                acc = acc + pltpu.roll(acc, shift, 0)
                shift *= 2
            tok = ext[POOL_HALO:POOL_HALO + ts]
            pooled = acc[POOL_HALO:POOL_HALO + ts] / jnp.minimum(pos, float(w)) - tok
            pooled_b = pooled.astype(BF16)
            pooled_ref[:, lo:lo + HEAD_DIM] = pooled_b
            lin = _dot(pooled_b, wp_ref[gi])
            cat_ref[:, lo:lo + HEAD_DIM] = (lin * ps_ref[:, lo:lo + HEAD_DIM]).astype(BF16)
        pext_scr[0:POOL_HALO, :] = pext_scr[ts:ts + POOL_HALO, :]

    tile = lambda w: pl.BlockSpec((ts, w), lambda i: (i, 0))
    return _call(
        body, name="proj_pool", grid=(nt,),
        in_specs=[tile(D_MODEL), _whole(), tile(HEAD_DIM), tile(HEAD_DIM), _whole(), _whole()],
        out_specs=[tile(D_MODEL), tile(RET_W), tile(RET_W), tile(RET_W), tile(RET_W), tile(POOL_W),
                   pl.BlockSpec((ts, POOL_W), lambda i: (i, 1))],
        out_shape=[jax.ShapeDtypeStruct((s, D_MODEL), BF16), jax.ShapeDtypeStruct((s, RET_W), BF16),
                   jax.ShapeDtypeStruct((s, RET_W), BF16), jax.ShapeDtypeStruct((s, RET_W), BF16),
                   jax.ShapeDtypeStruct((s, RET_W), F32), jax.ShapeDtypeStruct((s, POOL_W), BF16),
                   jax.ShapeDtypeStruct((s, 2 * RET_W), BF16)],
        scratch_shapes=[pltpu.VMEM((ts, IN_W), F32), pltpu.VMEM((ts + POOL_HALO, POOL_W), F32)],
        sem=("arbitrary",), operands=(x, win4, cosf, sinf, wpool, pscale), riders=riders, after=after,
    )


def _retention_fwd(q, k, v, g, cat, mask, qd, kd, riders=(), after=()):
    s = q.shape[0]
    ns = s // SUPER
    cdec = [gm ** float(SUPER) for gm in _gammas()]

    def body(q_ref, k_ref, v_ref, g_ref, cat_in, mask_ref, qd_ref, kd_ref,
             ret_ref, cat_ref, st_ref, state_scr):
        del cat_in
        n = pl.program_id(0)

        @pl.when(n == 0)
        def _():
            state_scr[...] = jnp.zeros_like(state_scr)

        for h in range(HEADS):
            sl = slice(h * HEAD_DIM, (h + 1) * HEAD_DIM)
            qh, kh, vh = q_ref[:, sl], k_ref[:, sl], v_ref[:, sl]
            sc = _dot_nt(qh, kh) * mask_ref[h]
            st = state_scr[h]
            stb = st.astype(BF16)
            st_ref[0, h] = stb
            qdb = (qh.astype(F32) * qd_ref[:, sl]).astype(BF16)
            kdb = (kh.astype(F32) * kd_ref[:, sl]).astype(BF16)
            ret = _dot(sc.astype(BF16), vh) + _dot(qdb, stb)
            state_scr[h] = st * cdec[h] + _dot_tn(kdb, vh)
            ret_ref[:, sl] = ret
            r = lax.rsqrt(jnp.mean(ret * ret, axis=-1, keepdims=True) + RMS_EPS)
            gh = g_ref[:, sl]
            cat_ref[:, sl] = ((ret * r) * (gh * _sigmoid(gh))).astype(BF16)

    tile = pl.BlockSpec((SUPER, RET_W), lambda n: (n, 0))
    return _call(
        body, name="retention_fwd", grid=(ns,),
        in_specs=[tile, tile, tile, tile, HBM_SPEC, _whole(), _whole(), _whole()],
        out_specs=[tile, tile, pl.BlockSpec((1, HEADS, HEAD_DIM, HEAD_DIM), lambda n: (n, 0, 0, 0))],
        out_shape=[jax.ShapeDtypeStruct((s, RET_W), F32), jax.ShapeDtypeStruct((s, 2 * RET_W), BF16),
                   jax.ShapeDtypeStruct((ns, HEADS, HEAD_DIM, HEAD_DIM), BF16)],
        scratch_shapes=[pltpu.VMEM((HEADS, HEAD_DIM, HEAD_DIM), F32)],
        aliases={4: 1}, sem=("arbitrary",), operands=(q, k, v, g, cat, mask, qd, kd), riders=riders,
        after=after,
    )


def _outproj_ln1(x, cat, wout, g1, b1, ts, riders=(), after=()):
    s = x.shape[0]

    def body(x_ref, cat_ref, w_ref, g_ref, b_ref, xhat_ref, rstd_ref, h1b_ref):
        z = ALPHA * x_ref[...] + _dot(cat_ref[...], w_ref[...])
        xhat, rstd = _layernorm_fwd(z)
        xhat_ref[...] = xhat
        rstd_ref[...] = rstd
        h1b_ref[...] = (xhat * g_ref[...] + b_ref[...]).astype(BF16)

    tile = lambda w: pl.BlockSpec((ts, w), lambda i: (i, 0))
    return _call(
        body, name="outproj_ln1", grid=(s // ts,),
        in_specs=[tile(D_MODEL), tile(D_MODEL), _whole(), _whole(), _whole()],
        out_specs=[tile(D_MODEL), tile(1), tile(D_MODEL)],
        out_shape=[jax.ShapeDtypeStruct((s, D_MODEL), F32), jax.ShapeDtypeStruct((s, 1), F32),
                   jax.ShapeDtypeStruct((s, D_MODEL), BF16)],
        sem=("arbitrary",), operands=(x, cat, wout, g1, b1), riders=riders, after=after,
    )


def _ffn_fwd_loss(xhat1, h1b, target, wup4, wdown, cw, cb, g1, b1, g2, b2, ts):
    s = xhat1.shape[0]

    def body(xhat_ref, h1b_ref, tgt_ref, wup_ref, wdn_ref, cw_ref, cb_ref, g1_ref, b1_ref, g2_ref, b2_ref,
             ub_ref, dz2_ref, dz2b_ref, loss_ref, dg2_ref, db2_ref, val_scr, gext_scr, act_scr, ffn_scr):
        i = pl.program_id(0)

        @pl.when(i == 0)
        def _():
            gext_scr[0:CONV_HALO, :] = jnp.zeros((CONV_HALO, D_FF), F32)
            loss_ref[...] = jnp.zeros_like(loss_ref)
            dg2_ref[...] = jnp.zeros_like(dg2_ref)
            db2_ref[...] = jnp.zeros_like(db2_ref)

        hb = h1b_ref[...]
        for half in range(2):
            lo = half * UP_SH
            gext_scr[CONV_HALO:CONV_HALO + ts, lo:lo + UP_SH] = _dot(hb, wup_ref[2 + half])
            val_scr[:, lo:lo + UP_SH] = _dot(hb, wup_ref[half])
            for c0 in range(lo, lo + UP_SH, FFN_STRIP):
                cols = slice(c0, c0 + FFN_STRIP)
                ext = gext_scr[:, cols]
                gate = ext[CONV_HALO:]
                hc = cb_ref[:, cols] + ((pltpu.roll(ext, 2, 0)[CONV_HALO:] * cw_ref[0:1, cols]
                                         + pltpu.roll(ext, 1, 0)[CONV_HALO:] * cw_ref[1:2, cols])
                                        + gate * cw_ref[2:3, cols])
                val = val_scr[:, cols]
                act_scr[:, cols] = ((hc * _sigmoid(hc)) * val).astype(BF16)
                ub_ref[:, cols] = val.astype(BF16)
                ub_ref[:, D_FF + c0:D_FF + c0 + FFN_STRIP] = gate.astype(BF16)
            part = _dot(act_scr[:, lo:lo + UP_SH], wdn_ref[lo:lo + UP_SH, :])
            if half == 0:
                ffn_scr[...] = part
            else:
                ffn_scr[...] += part
        gext_scr[0:CONV_HALO, :] = gext_scr[ts:ts + CONV_HALO, :]

        loss_acc = jnp.zeros((1, 1), F32)
        dg2_acc = jnp.zeros((1, D_MODEL), F32)
        db2_acc = jnp.zeros((1, D_MODEL), F32)
        for r0 in range(0, ts, LN_ROWS):
            rows = slice(r0, r0 + LN_ROWS)
            h1 = xhat_ref[rows, :] * g1_ref[...] + b1_ref[...]
            xhat2, rstd2 = _layernorm_fwd(ALPHA * h1 + ffn_scr[rows, :])
            diff = (xhat2 * g2_ref[...] + b2_ref[...]) - tgt_ref[rows, :]
            row = jnp.mean(diff * diff, axis=-1, keepdims=True)
            loss_acc = loss_acc + 0.5 * jnp.sum(row, axis=0, keepdims=True)
            dy = diff * (1.0 / D_MODEL)
            dg2_acc = dg2_acc + jnp.sum(dy * xhat2, axis=0, keepdims=True)
            db2_acc = db2_acc + jnp.sum(dy, axis=0, keepdims=True)
            dz2 = _layernorm_bwd(dy, xhat2, rstd2, g2_ref[...])
            dz2_ref[rows, :] = dz2
            dz2b_ref[rows, :] = dz2.astype(BF16)
        loss_ref[...] += loss_acc
        dg2_ref[...] += dg2_acc
        db2_ref[...] += db2_acc

    tile = lambda w: pl.BlockSpec((ts, w), lambda i: (i, 0))
    acc = lambda w: pl.BlockSpec((1, w), lambda i: (0, 0))
    return pl.pallas_call(
        body, name="ffn_fwd_loss", grid=(s // ts,),
        in_specs=[tile(D_MODEL), tile(D_MODEL), tile(D_MODEL)] + [_whole()] * 8,
        out_specs=[tile(2 * D_FF), tile(D_MODEL), tile(D_MODEL), acc(1), acc(D_MODEL), acc(D_MODEL)],
        out_shape=[jax.ShapeDtypeStruct((s, 2 * D_FF), BF16), jax.ShapeDtypeStruct((s, D_MODEL), F32),
                   jax.ShapeDtypeStruct((s, D_MODEL), BF16),
                   jax.ShapeDtypeStruct((1, 1), F32), jax.ShapeDtypeStruct((1, D_MODEL), F32),
                   jax.ShapeDtypeStruct((1, D_MODEL), F32)],
        scratch_shapes=[pltpu.VMEM((ts, D_FF), F32), pltpu.VMEM((ts + CONV_HALO, D_FF), F32),
                        pltpu.VMEM((ts, D_FF), BF16), pltpu.VMEM((ts, D_MODEL), F32)],
        compiler_params=_params(("arbitrary",)),
    )(xhat1, h1b, target, wup4, wdown, cw, cb, g1, b1, g2, b2)


def _ffn_bwd(dz2, dz2b, ub, xhat1, rstd1, wup4, wdown, cw, cb, g1, ts):
    s = dz2.shape[0]
    nt = s // ts
    hb = 16

    def body(dz2_ref, dz2b_ref, ub_ref, prev_ref, xhat_ref, rstd_ref, wup_ref, wdn_ref, cw_ref, cb_ref, g1_ref,
             a_ref, dub_ref, dz1_ref, dz1b_ref, dg1_ref, db1_ref, dcw_ref, dcb_ref, gext_scr, dext_scr, da_scr):
        i = pl.program_id(0)
        r = nt - 1 - i

        @pl.when(i == 0)
        def _():
            dext_scr[ts:ts + CONV_HALO, :] = jnp.zeros((CONV_HALO, D_FF), F32)
            dg1_ref[...] = jnp.zeros_like(dg1_ref)
            db1_ref[...] = jnp.zeros_like(db1_ref)
            dcw_ref[...] = jnp.zeros_like(dcw_ref)
            dcb_ref[...] = jnp.zeros_like(dcb_ref)

        da_scr[...] = _dot_nt(dz2b_ref[...], wdn_ref[...])
        prev = prev_ref[...].astype(F32)[hb - CONV_HALO:hb]
        gext_scr[0:CONV_HALO, :] = jnp.where(r == 0, 0.0, prev)
        n_ext = ts + CONV_HALO
        for c0 in range(0, D_FF, FFN_STRIP):
            cols = slice(c0, c0 + FFN_STRIP)
            gcols = slice(D_FF + c0, D_FF + c0 + FFN_STRIP)
            val = ub_ref[:, cols].astype(F32)
            gate = ub_ref[:, gcols].astype(F32)
            gext_scr[CONV_HALO:n_ext, cols] = gate
            ext = gext_scr[:, cols]
            g2s = pltpu.roll(ext, 2, 0)[CONV_HALO:]
            g1s = pltpu.roll(ext, 1, 0)[CONV_HALO:]
            hc = cb_ref[:, cols] + ((g2s * cw_ref[0:1, cols] + g1s * cw_ref[1:2, cols]) + gate * cw_ref[2:3, cols])
            sg = _sigmoid(hc)
            si = hc * sg
            a_ref[:, cols] = (si * val).astype(BF16)
            da = da_scr[:, cols]
            dhc = da * val * (sg * (1.0 + hc * (1.0 - sg)))
            dcb_ref[:, cols] += jnp.sum(dhc, axis=0, keepdims=True)
            dcw_ref[0:1, cols] += jnp.sum(dhc * g2s, axis=0, keepdims=True)
            dcw_ref[1:2, cols] += jnp.sum(dhc * g1s, axis=0, keepdims=True)
            dcw_ref[2:3, cols] += jnp.sum(dhc * gate, axis=0, keepdims=True)
            dext_scr[0:ts, cols] = dhc
            dext = dext_scr[:, cols]
            dgate = (dhc * cw_ref[2:3, cols] + pltpu.roll(dext, n_ext - 1, 0)[0:ts] * cw_ref[1:2, cols]
                     + pltpu.roll(dext, n_ext - 2, 0)[0:ts] * cw_ref[0:1, cols])
            dub_ref[:, cols] = (da * si).astype(BF16)
            dub_ref[:, gcols] = dgate.astype(BF16)
        dext_scr[ts:n_ext, :] = dext_scr[0:CONV_HALO, :]
        dh1 = ALPHA * dz2_ref[...]
        for j in range(N_SHARD):
            dh1 = dh1 + _dot_nt(dub_ref[:, j * UP_SH:(j + 1) * UP_SH], wup_ref[j])
        xhat = xhat_ref[...]
        dg1_ref[...] += jnp.sum(dh1 * xhat, axis=0, keepdims=True)
        db1_ref[...] += jnp.sum(dh1, axis=0, keepdims=True)
        dz1 = _layernorm_bwd(dh1, xhat, rstd_ref[...], g1_ref[...])
        dz1_ref[...] = dz1
        dz1b_ref[...] = dz1.astype(BF16)

    tile = lambda w: pl.BlockSpec((ts, w), lambda i: (nt - 1 - i, 0))
    acc = lambda rws, w: pl.BlockSpec((rws, w), lambda i: (0, 0))
    prev_spec = pl.BlockSpec((hb, D_FF), lambda i: (jnp.maximum((nt - 1 - i) * (ts // hb) - 1, 0), 1))
    return pl.pallas_call(
        body, name="ffn_bwd", grid=(nt,),
        in_specs=[tile(D_MODEL), tile(D_MODEL), tile(2 * D_FF), prev_spec, tile(D_MODEL), tile(1)] + [_whole()] * 5,
        out_specs=[tile(D_FF), tile(2 * D_FF), tile(D_MODEL), tile(D_MODEL), acc(1, D_MODEL), acc(1, D_MODEL),
                   acc(3, D_FF), acc(1, D_FF)],
        out_shape=[jax.ShapeDtypeStruct((s, D_FF), BF16), jax.ShapeDtypeStruct((s, 2 * D_FF), BF16),
                   jax.ShapeDtypeStruct((s, D_MODEL), F32), jax.ShapeDtypeStruct((s, D_MODEL), BF16),
                   jax.ShapeDtypeStruct((1, D_MODEL), F32),
                   jax.ShapeDtypeStruct((1, D_MODEL), F32), jax.ShapeDtypeStruct((3, D_FF), F32),
                   jax.ShapeDtypeStruct((1, D_FF), F32)],
        scratch_shapes=[pltpu.VMEM((ts + CONV_HALO, D_FF), F32), pltpu.VMEM((ts + CONV_HALO, D_FF), F32),
                        pltpu.VMEM((ts, D_FF), F32)],
        compiler_params=_params(("arbitrary",)),
    )(dz2, dz2b, ub, ub, xhat1, rstd1, wup4, wdown, cw, cb, g1)


def _mix_bwd(dz1, pooled, ret, g, wout, wpool, pscale, ts, riders=(), after=()):
    s = dz1.shape[0]
    nt = s // ts

    def body(dz1_ref, pooled_ref, ret_ref, g_ref, wout_ref, wp_ref, ps_ref,
             dret_ref, dgp_ref, dwp_ref, dps_ref, eext_scr):
        i = pl.program_id(0)
        r = nt - 1 - i

        @pl.when(i == 0)
        def _():
            eext_scr[ts:ts + POOL_HALO, :] = jnp.zeros((POOL_HALO, POOL_W), F32)
            dwp_ref[...] = jnp.zeros_like(dwp_ref)
            dps_ref[...] = jnp.zeros_like(dps_ref)

        dzb = dz1_ref[...].astype(BF16)
        dcat_r = _dot_nt(dzb, wout_ref[0:RET_W, :])
        dcat_p = _dot_nt(dzb, wout_ref[RET_W:2 * RET_W, :])
        pos = (r * ts + lax.broadcasted_iota(jnp.int32, (ts, 1), 0) + 1).astype(F32)
        dpooled = []
        for gi, w in enumerate(POOL_WINDOWS):
            sl = slice(gi * HEAD_DIM, (gi + 1) * HEAD_DIM)
            pb = pooled_ref[:, sl]
            dy = dcat_p[:, sl]
            dps_ref[:, sl] += jnp.sum(dy * _dot(pb, wp_ref[gi]), axis=0, keepdims=True)
            dlin = (dy * ps_ref[:, sl]).astype(BF16)
            dwp_ref[gi] += _dot_tn(pb, dlin)
            dpg = _dot_nt(dlin, wp_ref[gi])
            dpooled.append(dpg)
            eext_scr[0:ts, sl] = dpg / jnp.minimum(pos, float(w))
        for gi, w in enumerate(POOL_WINDOWS):
            sl = slice(gi * HEAD_DIM, (gi + 1) * HEAD_DIM)
            acc = eext_scr[:, sl]
            shift = 1
            while shift < w:
                acc = acc + pltpu.roll(acc, ts + POOL_HALO - shift, 0)
                shift *= 2
            dgp_ref[:, RET_W + gi * HEAD_DIM:RET_W + (gi + 1) * HEAD_DIM] = (acc[0:ts] - dpooled[gi]).astype(BF16)
        eext_scr[ts:ts + POOL_HALO, :] = eext_scr[0:POOL_HALO, :]
        for h in range(HEADS):
            sl = slice(h * HEAD_DIM, (h + 1) * HEAD_DIM)
            rt = ret_ref[:, sl]
            rr = lax.rsqrt(jnp.mean(rt * rt, axis=-1, keepdims=True) + RMS_EPS)
            rn = rt * rr
            gh = g_ref[:, sl]
            sg = _sigmoid(gh)
            dy = dcat_r[:, sl]
            dgp_ref[:, sl] = (dy * rn * (sg * (1.0 + gh * (1.0 - sg)))).astype(BF16)
            drn = dy * (gh * sg)
            dret_ref[:, sl] = (rr * (drn - rn * jnp.mean(drn * rn, axis=-1, keepdims=True))).astype(BF16)

    tile = lambda w: pl.BlockSpec((ts, w), lambda i: (nt - 1 - i, 0))
    return _call(
        body, name="mix_bwd", grid=(nt,),
        in_specs=[tile(D_MODEL), tile(POOL_W), tile(RET_W), tile(RET_W), _whole(), _whole(), _whole()],
        out_specs=[tile(RET_W), tile(2 * RET_W),
                   pl.BlockSpec((len(POOL_WINDOWS), HEAD_DIM, HEAD_DIM), lambda i: (0, 0, 0)),
                   pl.BlockSpec((1, POOL_W), lambda i: (0, 0))],
        out_shape=[jax.ShapeDtypeStruct((s, RET_W), BF16), jax.ShapeDtypeStruct((s, 2 * RET_W), BF16),
                   jax.ShapeDtypeStruct((len(POOL_WINDOWS), HEAD_DIM, HEAD_DIM), F32),
                   jax.ShapeDtypeStruct((1, POOL_W), F32)],
        scratch_shapes=[pltpu.VMEM((ts + POOL_HALO, POOL_W), F32)],
        sem=("arbitrary",), operands=(dz1, pooled, ret, g, wout, wpool, pscale), riders=riders,
        after=after,
    )


def _retention_bwd(q, k, v, dret, dgp, states, mask, qd, kd, cosf, sinf, riders=(), after=()):
    s = q.shape[0]
    ns = s // SUPER
    cdec = [gm ** float(SUPER) for gm in _gammas()]

    def body(q_ref, k_ref, v_ref, do_ref, dgp_ref, st_ref, mask_ref, qd_ref, kd_ref, cos_ref, sin_ref,
             dproj_ref, dstate_scr):
        i = pl.program_id(0)

        @pl.when(i == 0)
        def _():
            dstate_scr[...] = jnp.zeros_like(dstate_scr)

        cosf_t = cos_ref[...]
        sinf_t = sin_ref[...]
        for h in range(HEADS):
            sl = slice(h * HEAD_DIM, (h + 1) * HEAD_DIM)
            qh, kh, vh, doh = q_ref[:, sl], k_ref[:, sl], v_ref[:, sl], do_ref[:, sl]
            m = mask_ref[h]
            scb = (_dot_nt(qh, kh) * m).astype(BF16)
            dscb = (_dot_nt(doh, vh) * m).astype(BF16)
            stb = st_ref[0, h]
            dst = dstate_scr[h]
            dstb = dst.astype(BF16)
            qdb = (qh.astype(F32) * qd_ref[:, sl]).astype(BF16)
            kdb = (kh.astype(F32) * kd_ref[:, sl]).astype(BF16)
            dq = _dot(dscb, kh) + _dot_nt(doh, stb) * qd_ref[:, sl]
            dk = _dot_tn(dscb, qh) + _dot_nt(vh, dstb) * kd_ref[:, sl]
            dv = _dot_tn(scb, doh) + _dot(kdb, dstb)
            dstate_scr[h] = dst * cdec[h] + _dot_tn(qdb, doh)
            lo = h * HEAD_DIM
            dproj_ref[:, lo:lo + HEAD_DIM] = _rope_t(dq, cosf_t, sinf_t).astype(BF16)
            dproj_ref[:, RET_W + lo:RET_W + lo + HEAD_DIM] = _rope_t(dk * K_SCALE, cosf_t, sinf_t).astype(BF16)
            dproj_ref[:, 2 * RET_W + lo:2 * RET_W + lo + HEAD_DIM] = dv.astype(BF16)
        dproj_ref[:, 3 * RET_W:IN_W] = dgp_ref[...]

    tile = lambda w: pl.BlockSpec((SUPER, w), lambda i: (ns - 1 - i, 0))
    return _call(
        body, name="retention_bwd", grid=(ns,),
        in_specs=[tile(RET_W), tile(RET_W), tile(RET_W), tile(RET_W), tile(2 * RET_W),
                  pl.BlockSpec((1, HEADS, HEAD_DIM, HEAD_DIM), lambda i: (ns - 1 - i, 0, 0, 0)),
                  _whole(), _whole(), _whole(), tile(HEAD_DIM), tile(HEAD_DIM)],
        out_specs=[tile(IN_W)],
        out_shape=[jax.ShapeDtypeStruct((s, IN_W), BF16)],
        scratch_shapes=[pltpu.VMEM((HEADS, HEAD_DIM, HEAD_DIM), F32)],
        sem=("arbitrary",), operands=(q, k, v, dret, dgp, states, mask, qd, kd, cosf, sinf), riders=riders,
        after=after,
    )


def _dx(dz1, dproj, win4, ts, riders=(), after=()):
    s = dz1.shape[0]

    def body(dz1_ref, dp_ref, w_ref, dx_ref):
        acc = ALPHA * dz1_ref[...]
        for j in range(N_SHARD):
            acc = acc + _dot_nt(dp_ref[:, j * IN_SH:(j + 1) * IN_SH], w_ref[j])
        dx_ref[...] = acc

    tile = lambda w: pl.BlockSpec((ts, w), lambda i: (i, 0))
    return _call(
        body, name="dx", grid=(s // ts,),
        in_specs=[tile(D_MODEL), tile(IN_W), _whole()],
        out_specs=[tile(D_MODEL)],
        out_shape=[jax.ShapeDtypeStruct((s, D_MODEL), F32)],
        sem=("arbitrary",), operands=(dz1, dproj, win4), riders=riders, after=after,
    )


def _wgrad(a, b, tm, tn, name, stacked, m_outer, riders=(), after=()):
    s, m = a.shape
    n = b.shape[1]

    def body(a_ref, b_ref, o32_ref, o16_ref):
        res = _dot_tn(a_ref[...], b_ref[...])
        o32_ref[...] = res.reshape(o32_ref.shape)
        o16_ref[...] = res.astype(BF16).reshape(o16_ref.shape)

    if m_outer:
        grid, blocks = (m // tm, n // tn), (lambda g0, g1: (g0, g1))
    else:
        grid, blocks = (n // tn, m // tm), (lambda g0, g1: (g1, g0))
    if stacked:
        shape = (n // tn, m, tn)
        ospec = pl.BlockSpec((1, tm, tn), lambda g0, g1: (blocks(g0, g1)[1], blocks(g0, g1)[0], 0))
    else:
        shape = (m, n)
        ospec = pl.BlockSpec((tm, tn), lambda g0, g1: blocks(g0, g1))
    return _call(
        body, name=name, grid=grid,
        in_specs=[pl.BlockSpec((s, tm), lambda g0, g1: (0, blocks(g0, g1)[0])),
                  pl.BlockSpec((s, tn), lambda g0, g1: (0, blocks(g0, g1)[1]))],
        out_specs=[ospec, ospec],
        out_shape=[jax.ShapeDtypeStruct(shape, F32), jax.ShapeDtypeStruct(shape, BF16)],
        sem=("arbitrary", "arbitrary"), operands=(a, b), riders=riders, after=after,
    )


class _NoComm:
    def __init__(self, win4, wout, wup4, wdown):
        self.weights = dict(w_in=win4, w_out=wout, w_up=wup4, w_down=wdown)
        self.grads = {}

    def weight(self, name):
        return self.weights[name]

    def riders(self, call):
        return ()

    def after(self, call):
        return ()

    def landed(self, call, results, outs):
        pass

    def gradient(self, name, g32, g16):
        self.grads[name] = (g32, g16)


def _local_step(x, target, cw, cb, wpool, pscale, g1, b1, g2, b2, comm):
    s = x.shape[0]
    ts_a = min(512, s)
    ts_f = min(256, s)
    mask, qd, kd = _decay_tables()
    cosf, sinf = _rope_tables(s)
    wpool_b = wpool.astype(BF16)

    def run(call, fn, *args):
        outs, res = fn(*args, riders=comm.riders(call), after=comm.after(call))
        comm.landed(call, res, outs)
        return outs

    xb, q, k, v, g, pooled, cat = run("proj_pool", _proj_pool, x, comm.weight("w_in"), cosf, sinf, wpool_b,
                                      pscale, ts_a)
    ret, cat, states = run("retention_fwd", _retention_fwd, q, k, v, g, cat, mask, qd, kd)
    wout = comm.weight("w_out")
    xhat1, rstd1, h1b = run("outproj_ln1", _outproj_ln1, x, cat, wout, g1, b1, ts_a)
    wup4, wdown = comm.weight("w_up"), comm.weight("w_down")
    ub, dz2, dz2b, loss, dg2, db2 = _ffn_fwd_loss(xhat1, h1b, target, wup4, wdown, cw, cb, g1, b1, g2, b2, ts_f)

    act, dub, dz1, dz1b, dg1, db1, dcw, dcb = _ffn_bwd(dz2, dz2b, ub, xhat1, rstd1, wup4, wdown, cw, cb, g1, ts_f)
    half = D_MODEL // 2
    comm.gradient("w_up", *run("wgrad_up", _wgrad, h1b, dub, half, UP_SH, "wgrad_up", True, False))
    comm.gradient("w_out", *run("wgrad_out", _wgrad, cat, dz1b, D_MODEL, half, "wgrad_out", False, True))
    comm.gradient("w_down", *run("wgrad_down", _wgrad, act, dz2b, D_FF // 2, half, "wgrad_down", False, True))
    dret, dgp, dwp, dps = run("mix_bwd", _mix_bwd, dz1b, pooled, ret, g, wout, wpool_b, pscale, ts_a)
    dproj, = run("retention_bwd", _retention_bwd, q, k, v, dret, dgp, states, mask, qd, kd, cosf, sinf)
    comm.gradient("w_in", *run("wgrad_in", _wgrad, xb, dproj, D_MODEL, IN_SH, "wgrad_in", True, True))
    (grad_x,), _ = _dx(dz1, dproj, comm.weight("w_in"), ts_a, after=comm.after("dx"))
    small = dict(w_pool=dwp, pool_scale=dps, ln1_g=dg1, ln1_b=db1, conv_w=dcw, conv_b=dcb,
                 ln2_g=dg2, ln2_b=db2)
    return loss, grad_x, small


CAST_ROWS = 64
SHARD_SHAPES = ((D_MODEL, IN_SH), (OUT_SH, D_MODEL), (D_MODEL, UP_SH), (DOWN_SH, D_MODEL))
N_BIG = len(SHARD_SHAPES)
CW_PAD = (8, 768)


def _mesh_pos():
    return lax.axis_index("x"), lax.axis_index("y"), lax.axis_index("c")


def _other_chips(x, y):
    return [(1 - x, y), (x, 1 - y), (1 - x, 1 - y)]


def _half_rows(w, which):
    hr = SHARD_SHAPES[w][0] // 2
    return pl.ds(pl.multiple_of(which * hr, 16), hr)


def _gather_weights(shards, cw8, full):
    def body(*refs):
        in_refs = refs[:N_BIG]
        cw_ref = refs[N_BIG]
        out_refs = refs[N_BIG + 1:2 * N_BIG + 1]
        cwo_ref = refs[2 * N_BIG + 1]
        stage = refs[2 * N_BIG + 2:3 * N_BIG + 2]
        send_sems, recv_sems, fsend_sems, frecv_sems, cw_send, cw_recv, local_sems = refs[3 * N_BIG + 2:]
        x, y, c = _mesh_pos()
        j0 = 2 * x + y
        chips = _other_chips(x, y)

        def cast_to_stage(w):
            def cast(i, carry):
                rows = pl.ds(pl.multiple_of(i * CAST_ROWS, CAST_ROWS), CAST_ROWS)
                stage[w][rows, :] = in_refs[w][rows, :].astype(BF16)
                return carry
            lax.fori_loop(0, SHARD_SHAPES[w][0] // CAST_ROWS, cast, 0)

        for w in full:
            cast_to_stage(w)

        jx, jy, jd = 2 * (1 - x) + y, 2 * x + (1 - y), 2 * (1 - x) + (1 - y)
        neighbours = [((1 - x, y, c), jx), ((x, 1 - y, c), jy)]
        passed = jnp.where(c == 0, jx, jy)
        pass_to = (jnp.where(c == 0, x, 1 - x), jnp.where(c == 0, 1 - y, y), c)

        def nbr(w, k, block):
            return pltpu.make_async_remote_copy(
                src_ref=stage[w].at[_half_rows(w, c), :], dst_ref=out_refs[w].at[block, _half_rows(w, c), :],
                send_sem=send_sems.at[w, k], recv_sem=recv_sems.at[w, k],
                device_id=neighbours[k][0], device_id_type=MESH)

        def relay(w, block):
            return pltpu.make_async_remote_copy(
                src_ref=out_refs[w].at[passed, _half_rows(w, c), :],
                dst_ref=out_refs[w].at[block, _half_rows(w, c), :],
                send_sem=send_sems.at[w, 2], recv_sem=recv_sems.at[w, 2],
                device_id=pass_to, device_id_type=MESH)

        def d2d(w, k, block, half):
            return pltpu.make_async_remote_copy(
                src_ref=out_refs[w].at[block, _half_rows(w, half), :],
                dst_ref=out_refs[w].at[block, _half_rows(w, half), :],
                send_sem=fsend_sems.at[w, k], recv_sem=frecv_sems.at[w, k],
                device_id=(x, y, 1 - c), device_id_type=MESH)

        def conv(k, block):
            chip = chips[k]
            return pltpu.make_async_remote_copy(
                src_ref=cw_ref, dst_ref=cwo_ref.at[block], send_sem=cw_send.at[k], recv_sem=cw_recv.at[k],
                device_id=(chip[0], chip[1], c), device_id_type=MESH)

        sent = [nbr(w, k, j0) for w in full for k in range(2)] + [conv(k, j0) for k in range(3)]
        for cp in sent:
            cp.start()
        for w in range(N_BIG):
            if w not in full:
                cast_to_stage(w)
        local = [pltpu.make_async_copy(stage[w], out_refs[w].at[j0], local_sems.at[w]) for w in range(N_BIG)]
        local.append(pltpu.make_async_copy(cw_ref, cwo_ref.at[j0], local_sems.at[N_BIG]))
        for cp in local:
            cp.start()
        for w in full:
            for k, (_, block) in enumerate(neighbours):
                nbr(w, k, block).wait_recv()
            later = [relay(w, passed)] + [d2d(w, k, block, c) for k, (_, block) in enumerate(neighbours)]
            for cp in later:
                cp.start()
            sent += later
        for w in full:
            relay(w, jd).wait_recv()
            fw = d2d(w, 2, jd, c)
            fw.start()
            sent.append(fw)
        for w in full:
            for k, block in enumerate([jx, jy, jd]):
                d2d(w, k, block, 1 - c).wait_recv()
        for k, chip in enumerate(chips):
            conv(k, 2 * chip[0] + chip[1]).wait_recv()
        for cp in sent:
            cp.wait_send()
        for cp in local:
            cp.wait()

    out_shape = [jax.ShapeDtypeStruct((N_SHARD,) + shp, BF16) for shp in SHARD_SHAPES]
    out_shape.append(jax.ShapeDtypeStruct((N_SHARD,) + CW_PAD, F32))
    return pl.pallas_call(
        body, name="gather_weights",
        in_specs=[_whole()] * (N_BIG + 1),
        out_specs=[HBM_SPEC] * (N_BIG + 1),
        out_shape=out_shape,
        scratch_shapes=[pltpu.VMEM(shp, BF16) for shp in SHARD_SHAPES] + [
            pltpu.SemaphoreType.DMA((N_BIG, 3)), pltpu.SemaphoreType.DMA((N_BIG, 3)),
            pltpu.SemaphoreType.DMA((N_BIG, 3)), pltpu.SemaphoreType.DMA((N_BIG, 3)),
            pltpu.SemaphoreType.DMA((3,)), pltpu.SemaphoreType.DMA((3,)),
            pltpu.SemaphoreType.DMA((N_BIG + 1,))],
        compiler_params=pltpu.CompilerParams(vmem_limit_bytes=VMEM_LIMIT),
    )(*shards, cw8)


def _gather_rider(arrays, ops):
    ws = sorted(arrays)

    def make(inplace, srcs, lands, send_sems, recv_sems):
        del srcs, lands
        x, y, c = _mesh_pos()
        j0, jx, jy, jd = 2 * x + y, 2 * (1 - x) + y, 2 * x + (1 - y), 2 * (1 - x) + (1 - y)
        x_nbr, y_nbr, sibling = (1 - x, y, c), (x, 1 - y, c), (x, y, 1 - c)
        starts, waits = [], []
        for n, (kind, w, (r0, nr)) in enumerate(ops):
            ref = inplace[ws.index(w)]
            hr = SHARD_SHAPES[w][0] // 2
            rows = lambda core: pl.ds(pl.multiple_of(core * hr + r0, 16), nr)
            mine, theirs = rows(c), rows(1 - c)
            if kind == "ici":
                moves = [(ref.at[j0, mine, :], x_nbr, ref.at[jx, mine, :]),
                         (ref.at[j0, mine, :], y_nbr, ref.at[jy, mine, :]),
                         (ref.at[j0, mine, :], (1 - x, 1 - y, c), ref.at[jd, mine, :])]
            elif kind == "nbr":
                moves = [(ref.at[j0, mine, :], x_nbr, ref.at[jx, mine, :]),
                         (ref.at[j0, mine, :], y_nbr, ref.at[jy, mine, :])]
            elif kind == "relay":
                passed = jnp.where(c == 0, jx, jy)
                to = (jnp.where(c == 0, x, 1 - x), jnp.where(c == 0, 1 - y, y), c)
                moves = [(ref.at[passed, mine, :], to, ref.at[jd, mine, :])]
            else:
                blocks = dict(d2d=[jx, jy, jd], d2d_nbr=[jx, jy], d2d_diag=[jd])[kind]
                moves = [(ref.at[b, mine, :], sibling, ref.at[b, theirs, :]) for b in blocks]
            for k, (src, to, landing) in enumerate(moves):
                sems = dict(send_sem=send_sems.at[3 * n + k], recv_sem=recv_sems.at[3 * n + k],
                            device_id=to, device_id_type=MESH)
                send = pltpu.make_async_remote_copy(src_ref=src, dst_ref=src, **sems)
                arrival = pltpu.make_async_remote_copy(src_ref=src, dst_ref=landing, **sems)
                starts.append(send)
                waits += [arrival.wait_recv, send.wait_send]
        return starts, waits

    return _Rider([arrays[w] for w in ws], [], [], 3 * len(ops), make)


def _whole_half(w):
    return (0, SHARD_SHAPES[w][0] // 2)


def _pair_rider(ws, g16s):
    def make(inplace, srcs, lands, send_sems, recv_sems):
        del inplace
        x, y, c = _mesh_pos()
        copies = [pltpu.make_async_remote_copy(
            src_ref=srcs[i].at[:, _half_rows(w, 1 - c), :], dst_ref=lands[i],
            send_sem=send_sems.at[i], recv_sem=recv_sems.at[i], device_id=(x, y, 1 - c), device_id_type=MESH)
            for i, w in enumerate(ws)]
        return copies, [cp.wait for cp in copies]

    lands = [jax.ShapeDtypeStruct((N_SHARD, SHARD_SHAPES[w][0] // 2, SHARD_SHAPES[w][1]), BF16) for w in ws]
    return _Rider([], g16s, lands, len(ws), make)


def _chip_rider(ws, p16s, rows=None, landing=None):
    def make(inplace, srcs, lands, send_sems, recv_sems):
        x, y, c = _mesh_pos()
        dsts = inplace if landing is not None else lands
        copies = []
        for i, w in enumerate(ws):
            r0, nr = rows if rows is not None else _whole_half(w)
            for k, chip in enumerate(_other_chips(x, y)):
                copies.append(pltpu.make_async_remote_copy(
                    src_ref=srcs[i].at[2 * chip[0] + chip[1], pl.ds(r0, nr), :],
                    dst_ref=dsts[i].at[k, pl.ds(r0, nr), :],
                    send_sem=send_sems.at[3 * i + k], recv_sem=recv_sems.at[3 * i + k],
                    device_id=(chip[0], chip[1], c), device_id_type=MESH))
        return copies, [cp.wait for cp in copies]

    lands = [jax.ShapeDtypeStruct((3, SHARD_SHAPES[w][0] // 2, SHARD_SHAPES[w][1]), BF16) for w in ws]
    if landing is not None:
        return _Rider(landing, p16s, [], 3 * len(ws), make)
    return _Rider([], p16s, lands, 3 * len(ws), make)


def _final_rider(halves):
    def make(inplace, srcs, lands, send_sems, recv_sems):
        del inplace
        x, y, c = _mesh_pos()
        copies = [pltpu.make_async_remote_copy(
            src_ref=srcs[i], dst_ref=lands[i], send_sem=send_sems.at[i], recv_sem=recv_sems.at[i],
            device_id=(x, y, 1 - c), device_id_type=MESH) for i in range(len(halves))]
        return copies, [cp.wait for cp in copies]

    return _Rider([], halves, [jax.ShapeDtypeStruct(h.shape, h.dtype) for h in halves], len(halves), make)


def _comm_only(name, riders):
    _, res = _call(lambda: None, name=name, grid=(), in_specs=[], out_specs=[], out_shape=[], operands=(),
                   riders=riders)
    return res


class _SemList:
    def __init__(self, refs):
        self.at = list(refs)


def _split_start(name, rider):
    assert not rider.inplace
    ns, nl, n = len(rider.srcs), len(rider.lands), rider.n_copies

    def body(*refs):
        srcs, lands = refs[:ns], refs[ns:ns + nl]
        sems = refs[ns + nl:ns + nl + 2 * n]
        token = refs[-1]
        starts, _ = rider.make([], srcs, lands, _SemList(sems[:n]), _SemList(sems[n:]))
        for cp in starts:
            cp.start()
        token[...] = jnp.zeros_like(token)

    buffers = [pltpu.with_memory_space_constraint(a, pltpu.HBM) for a in rider.srcs]
    buffers += [pltpu.with_memory_space_constraint(lax.empty(s.shape, s.dtype), pltpu.HBM) for s in rider.lands]
    hbm = pl.BlockSpec(memory_space=pltpu.HBM)
    sem = pl.BlockSpec(memory_space=pltpu.SEMAPHORE)
    outs = pl.pallas_call(
        body, name=name,
        out_shape=tuple([pltpu.SemaphoreType.DMA(())] * (2 * n) + [pltpu.HBM(b.shape, b.dtype) for b in buffers]
                        + [jax.ShapeDtypeStruct((8, 128), F32)]),
        in_specs=[hbm] * (ns + nl),
        out_specs=tuple([sem] * (2 * n) + [hbm] * (ns + nl) + [_whole()]),
        input_output_aliases={i: 2 * n + i for i in range(ns + nl)},
        compiler_params=pltpu.CompilerParams(has_side_effects=pltpu.SideEffectType.DATAFLOW_SIDE_EFFECTING),
    )(*buffers)
    return (rider, outs[:2 * n], outs[2 * n:2 * n + ns + nl]), outs[-1]


def _split_wait(name, state, after):
    rider, sems, buffers = state
    ns, nl, n = len(rider.srcs), len(rider.lands), rider.n_copies

    def body(*refs):
        srcs, lands = refs[:ns], refs[ns:ns + nl]
        sem_refs = refs[ns + nl:ns + nl + 2 * n]
        _, waits = rider.make([], srcs, lands, _SemList(sem_refs[:n]), _SemList(sem_refs[n:]))
        for wait in waits:
            wait()

    hbm = pl.BlockSpec(memory_space=pltpu.HBM)
    sem = pl.BlockSpec(memory_space=pltpu.SEMAPHORE)
    outs = pl.pallas_call(
        body, name=name,
        out_shape=tuple(pltpu.HBM(b.shape, b.dtype) for b in buffers),
        in_specs=[hbm] * (ns + nl) + [sem] * (2 * n) + [HBM_SPEC],
        out_specs=tuple([hbm] * (ns + nl)),
        input_output_aliases={i: i for i in range(ns + nl)},
        compiler_params=pltpu.CompilerParams(has_side_effects=pltpu.SideEffectType.DATAFLOW_SIDE_EFFECTING),
    )(*buffers, *sems, after)
    return list(outs[:ns]), list(outs[ns:])


def _pair_sum(pos, ws, g32s, recvs):
    n = len(ws)

    def body(pos_ref, *refs):
        del pos_ref
        g_refs, r_refs = refs[:n], refs[n:2 * n]
        p32_refs, p16_refs = refs[2 * n:3 * n], refs[3 * n:]
        for i in range(n):
            tot = g_refs[i][...] + r_refs[i][...].astype(F32)
            p32_refs[i][...] = tot
            p16_refs[i][...] = tot.astype(BF16)

    halves = [(SHARD_SHAPES[w][0] // 2, SHARD_SHAPES[w][1]) for w in ws]
    own = [pl.BlockSpec((None, None) + h, lambda j, pos_ref: (j, pos_ref[0], 0, 0)) for h in halves]
    blk = [pl.BlockSpec((None,) + h, lambda j, pos_ref: (j, 0, 0)) for h in halves]
    g4 = [g.reshape((N_SHARD, 2) + h) for g, h in zip(g32s, halves)]
    outs = pl.pallas_call(
        body, name="pair_sum_" + "_".join(str(w) for w in ws),
        grid_spec=pltpu.PrefetchScalarGridSpec(
            num_scalar_prefetch=1, grid=(N_SHARD,), in_specs=own + blk, out_specs=blk + blk),
        out_shape=[jax.ShapeDtypeStruct((N_SHARD,) + h, F32) for h in halves]
        + [jax.ShapeDtypeStruct((N_SHARD,) + h, BF16) for h in halves],
        compiler_params=_params(("arbitrary",)),
    )(pos, *g4, *recvs)
    return outs[:n], outs[n:]


def _chip_sum(pos, p32s, recvs):
    parts = 2

    def body(pos_ref, *refs):
        del pos_ref
        p_refs, r_refs, f_refs = refs[:N_BIG], refs[N_BIG:2 * N_BIG], refs[2 * N_BIG:]
        for w in range(N_BIG):
            f_refs[w][...] = ((p_refs[w][...] + r_refs[w][0].astype(F32)) + r_refs[w][1].astype(F32)) \
                + r_refs[w][2].astype(F32)

    quarters = [(r // 2 // parts, cc) for r, cc in SHARD_SHAPES]
    own = [pl.BlockSpec((None,) + qt, lambda i, pos_ref: (pos_ref[1], i, 0)) for qt in quarters]
    rcv = [pl.BlockSpec((3,) + qt, lambda i, pos_ref: (0, i, 0)) for qt in quarters]
    out = [pl.BlockSpec(qt, lambda i, pos_ref: (i, 0)) for qt in quarters]
    return pl.pallas_call(
        body, name="chip_sum",
        grid_spec=pltpu.PrefetchScalarGridSpec(
            num_scalar_prefetch=1, grid=(parts,), in_specs=own + rcv, out_specs=out),
        out_shape=[jax.ShapeDtypeStruct((r // 2, cc), F32) for r, cc in SHARD_SHAPES],
        compiler_params=_params(("arbitrary",)),
    )(pos, *p32s, *recvs)


def _adamw(w, g, m, v):
    m_new = ADAM_B1 * m + (1.0 - ADAM_B1) * g
    v_new = ADAM_B2 * v + (1.0 - ADAM_B2) * (g * g)
    m_hat = m_new / (1.0 - ADAM_B1 ** ADAM_STEP)
    v_hat = v_new / (1.0 - ADAM_B2 ** ADAM_STEP)
    delta = -ADAM_LR * (m_hat / (jnp.sqrt(v_hat) + ADAM_EPS) + ADAM_WD * w)
    return delta, m_new, v_new


def _adam_big(pos, mine, theirs, ws, ms, vs):
    nb = 4

    def body(pos_ref, *refs):
        hf = pl.program_id(0)
        groups = [refs[i * N_BIG:(i + 1) * N_BIG] for i in range(9)]
        f_refs, t_refs, w_refs, m_refs, v_refs, go_refs, do_refs, mo_refs, vo_refs = groups
        for w in range(N_BIG):
            g = jnp.where(hf == pos_ref[0], f_refs[w][...], t_refs[w][...])
            delta, m_new, v_new = _adamw(w_refs[w][...], g, m_refs[w][...], v_refs[w][...])
            go_refs[w][...] = g
            do_refs[w][...] = delta
            mo_refs[w][...] = m_new
            vo_refs[w][...] = v_new

    blocks = [(r // 2 // nb, cc) for r, cc in SHARD_SHAPES]
    half = [pl.BlockSpec(b, lambda hf, i, pos_ref: (i, 0)) for b in blocks]
    full = [pl.BlockSpec((None,) + b, lambda hf, i, pos_ref: (0, hf * nb + i, 0)) for b in blocks]
    shapes = [jax.ShapeDtypeStruct((1,) + shp, F32) for shp in SHARD_SHAPES]
    outs = pl.pallas_call(
        body, name="adam_big",
        grid_spec=pltpu.PrefetchScalarGridSpec(
            num_scalar_prefetch=1, grid=(2, nb), in_specs=half + half + full * 3, out_specs=full * 4),
        out_shape=shapes * 4,
        compiler_params=_params(("arbitrary", "arbitrary")),
    )(pos, *mine, *theirs, *ws, *ms, *vs)
    return [outs[i * N_BIG:(i + 1) * N_BIG] for i in range(4)]


SMALL_ROWS = 8
ROW_CONV_B, ROW_POOL_SCALE, ROW_LN1_G, ROW_LN1_B, ROW_LN2_G, ROW_LN2_B, ROW_LOSS = range(7)
SMALL_VECS = ((ROW_CONV_B, D_FF), (ROW_POOL_SCALE, POOL_W), (ROW_LN1_G, D_MODEL), (ROW_LN1_B, D_MODEL),
              (ROW_LN2_G, D_MODEL), (ROW_LN2_B, D_MODEL))


def _small_update(loss, dwp, dcw4, vec_grads, wp, cwp, vec_ws, m_wp, m_cwp, vec_ms, v_wp, v_cwp, vec_vs,
                  riders=(), after=()):
    nv = len(SMALL_VECS)

    def body(*refs):
        loss_ref, dwp_ref, dcw_ref = refs[0:3]
        gvec = refs[3:3 + nv]
        o = 3 + nv
        wp_ref, cw_ref = refs[o:o + 2]
        wvec = refs[o + 2:o + 2 + nv]
        o += 2 + nv
        mwp_ref, mcw_ref = refs[o:o + 2]
        mvec = refs[o + 2:o + 2 + nv]
        o += 2 + nv
        vwp_ref, vcw_ref = refs[o:o + 2]
        vvec = refs[o + 2:o + 2 + nv]
        o += 2 + nv
        loss_out = refs[o]
        outs = refs[o + 1:o + 1 + 4 * (2 + nv)]
        o += 1 + 4 * (2 + nv)
        (vec_scr, sib_a, sib_b, sib_c, all_a, all_b, all_c,
         send1, recv1, send2, recv2) = refs[o:]
        x, y, c = _mesh_pos()
        j0 = 2 * x + y
        chips = _other_chips(x, y)

        vec_scr[...] = jnp.zeros_like(vec_scr)
        for (row, n), ref in zip(SMALL_VECS, gvec):
            vec_scr[row:row + 1, 0:n] = ref[...]
        vec_scr[ROW_LOSS:ROW_LOSS + 1, 0:HEAD_DIM] = jnp.broadcast_to(loss_ref[...], (1, HEAD_DIM))

        mine = (dwp_ref, vec_scr, dcw_ref)
        sib = (sib_a, sib_b, sib_c)
        every = (all_a, all_b, all_c)
        first = [pltpu.make_async_remote_copy(
            src_ref=mine[b], dst_ref=sib[b], send_sem=send1.at[b], recv_sem=recv1.at[b],
            device_id=(x, y, 1 - c), device_id_type=MESH) for b in range(3)]
        for cp in first:
            cp.start()
        for cp in first:
            cp.wait()
        for b in range(3):
            every[b][j0] = mine[b][...] + sib[b][...]

        def ici(b, k, block):
            chip = chips[k]
            return pltpu.make_async_remote_copy(
                src_ref=every[b].at[block], dst_ref=every[b].at[block],
                send_sem=send2.at[b, k], recv_sem=recv2.at[b, k],
                device_id=(chip[0], chip[1], c), device_id_type=MESH)

        second = [ici(b, k, j0) for b in range(3) for k in range(3)]
        for cp in second:
            cp.start()
        for k, chip in enumerate(chips):
            for b in range(3):
                ici(b, k, 2 * chip[0] + chip[1]).wait_recv()
        for cp in second:
            cp.wait_send()

        tot_a = ((all_a[0] + all_a[1]) + all_a[2]) + all_a[3]
        tot_b = ((all_b[0] + all_b[1]) + all_b[2]) + all_b[3]
        all_c[0] = ((all_c[0] + all_c[1]) + all_c[2]) + all_c[3]
        tot_c = all_c[0, j0]
        loss_out[...] = tot_b[ROW_LOSS:ROW_LOSS + 1, 0:1]

        grads = [tot_a, tot_c] + [tot_b[row:row + 1, 0:n] for row, n in SMALL_VECS]
        w_all = [wp_ref, cw_ref] + list(wvec)
        m_all = [mwp_ref, mcw_ref] + list(mvec)
        v_all = [vwp_ref, vcw_ref] + list(vvec)
        np_ = 2 + nv
        for p in range(np_):
            g = grads[p]
            delta, m_new, v_new = _adamw(w_all[p][...], g, m_all[p][...], v_all[p][...])
            outs[p][...] = g
            outs[np_ + p][...] = delta
            outs[2 * np_ + p][...] = m_new
            outs[3 * np_ + p][...] = v_new

    pshapes = [wp.shape, CW_PAD] + [wv.shape for wv in vec_ws]
    out_shape = [jax.ShapeDtypeStruct((1, 1), F32)] + [jax.ShapeDtypeStruct(s, F32) for s in pshapes] * 4
    a_shape = dwp.shape
    b_shape = (SMALL_ROWS, D_FF)
    c_shape = dcw4.shape
    n_in = 3 + nv + 3 * (2 + nv)
    outs, rider_res = _call(
        body, name="small_update", grid=(),
        in_specs=[_whole()] * n_in, out_specs=[_whole()] * len(out_shape), out_shape=out_shape,
        scratch_shapes=[pltpu.VMEM(b_shape, F32),
                        pltpu.VMEM(a_shape, F32), pltpu.VMEM(b_shape, F32), pltpu.VMEM(c_shape, F32),
                        pltpu.VMEM((N_SHARD,) + a_shape, F32), pltpu.VMEM((N_SHARD,) + b_shape, F32),
                        pltpu.VMEM((N_SHARD,) + c_shape, F32),
                        pltpu.SemaphoreType.DMA((3,)), pltpu.SemaphoreType.DMA((3,)),
                        pltpu.SemaphoreType.DMA((3, 3)), pltpu.SemaphoreType.DMA((3, 3))],
        operands=(loss, dwp, dcw4, *vec_grads, wp, cwp, *vec_ws, m_wp, m_cwp, *vec_ms, v_wp, v_cwp, *vec_vs),
        riders=riders, after=after,
    )
    np_ = 2 + nv
    return outs[0], [outs[1 + i * np_:1 + (i + 1) * np_] for i in range(4)], rider_res


def _pad_cw(a):
    pad = [(0, 0)] * (a.ndim - 2) + [(0, CW_PAD[0] - a.shape[-2]), (0, CW_PAD[1] - a.shape[-1])]
    return jnp.pad(a, pad)


def kernel(x, w_in, w_pool, pool_scale, w_out, ln1_g, ln1_b, w_up, conv_w, conv_b, w_down, ln2_g, ln2_b, loss_target, m_w_in, m_w_pool, m_pool_scale, m_w_out, m_ln1_g, m_ln1_b, m_w_up, m_conv_w, m_conv_b, m_w_down, m_ln2_g, m_ln2_b, v_w_in, v_w_pool, v_pool_scale, v_w_out, v_ln1_g, v_ln1_b, v_w_up, v_conv_w, v_conv_b, v_w_down, v_ln2_g, v_ln2_b):
    pos = jnp.stack([lax.axis_index("c"), 2 * lax.axis_index("x") + lax.axis_index("y")]).astype(jnp.int32)
    order = ("w_in", "w_out", "w_up", "w_down")
    w_in_i, w_out_i, w_up_i, w_down_i = range(N_BIG)

    gathered = _gather_weights([w_in[0], w_out[0], w_up[0], w_down[0]], _pad_cw(conv_w[0]), (w_in_i,))
    cw_full = jnp.transpose(gathered[N_BIG][:, 0:3, 0:DOWN_SH], (1, 0, 2)).reshape(3, D_FF)
    up_a, up_b, up_c = (0, 224), (224, 160), (384, 128)
    assert up_c[0] + up_c[1] == SHARD_SHAPES[w_up_i][0] // 2

    class MeshComm:
        def __init__(self):
            self.w = {i: gathered[i] for i in range(N_BIG)}
            self.g32, self.g16, self.p32, self.p16, self.recv_b = {}, {}, {}, {}, {}
            self.up_complete = False
            self.tokens, self.chips = {}, []

        def weight(self, name):
            i = order.index(name)
            if name == "w_up" and not self.up_complete:
                (arrs, _), = _comm_only("gather_up_last", [_gather_rider(
                    {i: self.w[i]}, [("d2d_diag", i, up_b), ("d2d", i, up_c)])])
                self.w[i], self.up_complete = arrs[0], True
            full = self.w[i]
            return full.reshape(-1, full.shape[-1]) if name in ("w_out", "w_down") else full

        def _gather(self, ws, ops):
            return _gather_rider({w: self.w[w] for w in ws}, ops), ("w", ws)

        def _pair(self, ws):
            return _pair_rider(ws, [self.g16[w] for w in ws]), ("recv_a", ws)

        def _chip(self, ws, rows=None, resume=False):
            landing = [self.recv_b[w] for w in ws] if resume else None
            return _chip_rider(ws, [self.p16[w] for w in ws], rows, landing), ("recv_b", ws)

        def plan(self, call):
            out_all, down_all = _whole_half(w_out_i), _whole_half(w_down_i)
            if call == "proj_pool":
                return [self._gather([w_out_i, w_up_i, w_down_i],
                                     [("ici", w_out_i, out_all), ("nbr", w_down_i, down_all),
                                      ("nbr", w_up_i, up_a)])]
            if call == "retention_fwd":
                return [self._gather([w_out_i, w_up_i, w_down_i],
                                     [("d2d", w_out_i, out_all),
                                      ("relay", w_down_i, down_all), ("d2d_nbr", w_down_i, down_all),
                                      ("relay", w_up_i, up_a), ("d2d_nbr", w_up_i, up_a), ("nbr", w_up_i, up_b)])]
            if call == "outproj_ln1":
                return [self._gather([w_up_i, w_down_i],
                                     [("d2d_diag", w_down_i, down_all), ("d2d_diag", w_up_i, up_a),
                                      ("relay", w_up_i, up_b), ("d2d_nbr", w_up_i, up_b), ("ici", w_up_i, up_c)])]
            return []

        def after(self, call):
            return tuple(self.tokens.pop(call, ()))

        def riders(self, call):
            self.pending = self.plan(call)
            return [r for r, _ in self.pending]

        def _start(self, name, rider, before):
            state, token = _split_start(name, rider)
            self.tokens.setdefault(before, []).append(token)
            return state

        def _finish_pair(self, name, state, ws, after):
            _, lands = _split_wait(name, state, after)
            self._finish_sum(ws, lands)

        def landed(self, call, results, outs):
            for (_, (slot, ws)), (inplace, lands) in zip(self.pending, results):
                for w, arr in zip(ws, inplace if len(inplace) else lands):
                    getattr(self, slot)[w] = arr
            if call == "wgrad_out":
                self._finish_pair("pair_exchange_up_wait", self.pair_up, [w_up_i], outs[1])
                self.chips.append(([w_up_i], self._start(
                    "chip_exchange_up_start", self._chip([w_up_i])[0], "wgrad_down")))
            if call == "mix_bwd":
                ws = [w_out_i, w_down_i]
                self._finish_pair("pair_exchange_out_down_wait", self.pair_out_down, ws, outs[0])
                self.chips.append((ws, self._start(
                    "chip_exchange_out_down_start", self._chip(ws)[0], "retention_bwd")))

        def gradient(self, name, g32, g16):
            w = order.index(name)
            shape = (N_SHARD,) + SHARD_SHAPES[w]
            self.g32[w], self.g16[w] = g32.reshape(shape), g16.reshape(shape)
            if name == "w_up":
                self.pair_up = self._start("pair_exchange_up_start", self._pair([w])[0], "wgrad_out")
            if name == "w_down":
                self.pair_out_down = self._start("pair_exchange_out_down_start",
                                                 self._pair([w_out_i, w_down_i])[0], "mix_bwd")
            if name == "w_in":
                (_, lands), = _comm_only("pair_exchange_in", [self._pair([w])[0]])
                self._finish_sum([w], lands)
                self.chips.append(([w], self._start("chip_exchange_in_start", self._chip([w])[0], "dx")))

        def _finish_sum(self, ws, lands):
            p32s, p16s = _pair_sum(pos, ws, [self.g32[w] for w in ws], lands)
            for w, p32, p16 in zip(ws, p32s, p16s):
                self.p32[w], self.p16[w] = p32, p16

        def finish_chips(self, after):
            for n, (ws, state) in enumerate(self.chips):
                _, lands = _split_wait("chip_exchange_wait_%d" % n, state, after)
                for w, arr in zip(ws, lands):
                    self.recv_b[w] = arr

    comm = MeshComm()
    loss, grad_x, small = _local_step(x[0], loss_target[0], cw_full, conv_b, w_pool[0], pool_scale,
                                      ln1_g, ln1_b, ln2_g, ln2_b, comm)

    dcw4 = _pad_cw(jnp.transpose(small["conv_w"].reshape(3, N_SHARD, DOWN_SH), (1, 0, 2)))
    vec_names = ("conv_b", "pool_scale", "ln1_g", "ln1_b", "ln2_g", "ln2_b")
    given = dict(w_pool=w_pool, pool_scale=pool_scale, ln1_g=ln1_g, ln1_b=ln1_b, conv_w=conv_w, conv_b=conv_b,
                 ln2_g=ln2_g, ln2_b=ln2_b)
    given_m = dict(w_pool=m_w_pool, pool_scale=m_pool_scale, ln1_g=m_ln1_g, ln1_b=m_ln1_b, conv_w=m_conv_w,
                   conv_b=m_conv_b, ln2_g=m_ln2_g, ln2_b=m_ln2_b)
    given_v = dict(w_pool=v_w_pool, pool_scale=v_pool_scale, ln1_g=v_ln1_g, ln1_b=v_ln1_b, conv_w=v_conv_w,
                   conv_b=v_conv_b, ln2_g=v_ln2_g, ln2_b=v_ln2_b)
    args = []
    for src in (given, given_m, given_v):
        args += [src["w_pool"][0], _pad_cw(src["conv_w"][0]), [src[n] for n in vec_names]]
    comm.finish_chips(grad_x)
    every = range(N_BIG)
    mine = _chip_sum(pos, [comm.p32[w] for w in every], [comm.recv_b[w] for w in every])
    final_state, final_token = _split_start("pair_exchange_f32_start", _final_rider(mine))
    loss_tot, small_out, _ = _small_update(loss, small["w_pool"], dcw4, [small[n] for n in vec_names], *args,
                                           after=(final_token,))
    mine, theirs = _split_wait("pair_exchange_f32_wait", final_state, small_out[0][0])
    big_out = _adam_big(pos, mine, theirs, [w_in, w_out, w_up, w_down], [m_w_in, m_w_out, m_w_up, m_w_down],
                        [v_w_in, v_w_out, v_w_up, v_w_down])

    names = ("w_in", "w_pool", "pool_scale", "w_out", "ln1_g", "ln1_b", "w_up", "conv_w", "conv_b", "w_down",
             "ln2_g", "ln2_b")
    small_names = ("w_pool", "conv_w") + vec_names
    result = [loss_tot.reshape(()), grad_x[None]]
    for kind in range(4):
        for n in names:
            if n in order:
                result.append(big_out[kind][order.index(n)])
            else:
                val = small_out[kind][small_names.index(n)]
                if n == "conv_w":
                    val = val[0:3, 0:DOWN_SH][None]
                elif n == "w_pool":
                    val = val[None]
                result.append(val)
    return tuple(result)
```

```python
import functools
import math

import numpy as np
import jax
import jax.numpy as jnp
from jax import lax
from jax.experimental import pallas as pl
from jax.experimental.pallas import tpu as pltpu

F32 = jnp.float32
BF16 = jnp.bfloat16

D_MODEL = 1024
HEADS = 4
HEAD_DIM = 128
RET_W = HEADS * HEAD_DIM
POOL_WINDOWS = (2, 4, 8, 16)
POOL_W = 512
IN_W = 4 * RET_W + POOL_W
D_FF = 2816
N_SHARD = 4
IN_SH = IN_W // N_SHARD
UP_SH = 2 * D_FF // N_SHARD
DOWN_SH = D_FF // N_SHARD
OUT_SH = D_MODEL // N_SHARD
ROPE_BASE = 10000.0
LN_EPS = 1e-5
RMS_EPS = 1e-6
ALPHA = 2.0 ** 0.25
K_SCALE = HEAD_DIM ** -0.5
SUPER = 256
CHUNK = 64
POOL_HALO = 16
CONV_HALO = 8
FFN_STRIP = 128
LN_ROWS = 32

ADAM_LR = 0.001
ADAM_B1 = 0.9
ADAM_B2 = 0.999
ADAM_EPS = 1e-08
ADAM_WD = 0.01
ADAM_STEP = 10

MESH = pl.DeviceIdType.MESH
VMEM_LIMIT = 56 * 1024 * 1024


def _dot(a, b):
    return jnp.dot(a, b, preferred_element_type=F32)


def _dot_nt(a, b):
    return lax.dot_general(a, b, (((1,), (1,)), ((), ())), preferred_element_type=F32)


def _dot_tn(a, b):
    return lax.dot_general(a, b, (((0,), (0,)), ((), ())), preferred_element_type=F32)


def _sigmoid(x):
    return 1.0 / (1.0 + jnp.exp(-x))


def _params(sem):
    return pltpu.CompilerParams(dimension_semantics=sem, vmem_limit_bytes=VMEM_LIMIT)


def _whole():
    return pl.BlockSpec(memory_space=pltpu.VMEM)


HBM_SPEC = pl.BlockSpec(memory_space=pl.ANY)


class _Rider:
    def __init__(self, inplace, srcs, lands, n_copies, make):
        self.inplace, self.srcs, self.lands, self.n_copies, self.make = list(inplace), list(srcs), list(lands), n_copies, make


def _call(body, *, name, grid, in_specs, out_specs, out_shape, operands, scratch_shapes=(), sem=(),
          aliases=None, riders=(), after=()):
    n_in, n_out, n_scr = len(in_specs), len(out_shape), len(scratch_shapes)
    in_specs, out_specs, out_shape = list(in_specs), list(out_specs), list(out_shape)
    operands, scratch_shapes, aliases = list(operands), list(scratch_shapes), dict(aliases or {})
    in_specs += [_whole()] * len(after)
    operands += list(after)
    for r in riders:
        for a in r.inplace:
            aliases[len(in_specs)] = len(out_shape)
            in_specs.append(HBM_SPEC)
            operands.append(a)
            out_specs.append(HBM_SPEC)
            out_shape.append(jax.ShapeDtypeStruct(a.shape, a.dtype))
        for a in r.srcs:
            in_specs.append(HBM_SPEC)
            operands.append(a)
        for shp in r.lands:
            out_specs.append(HBM_SPEC)
            out_shape.append(shp)
        scratch_shapes += [pltpu.SemaphoreType.DMA((r.n_copies,)), pltpu.SemaphoreType.DMA((r.n_copies,))]

    def full(*refs):
        ins = refs[:n_in]
        at = n_in + len(after)
        r_srcs = []
        for r in riders:
            at += len(r.inplace)
            r_srcs.append(refs[at:at + len(r.srcs)])
            at += len(r.srcs)
        outs = refs[at:at + n_out]
        at += n_out
        r_outs = []
        for r in riders:
            r_outs.append((refs[at:at + len(r.inplace)], refs[at + len(r.inplace):at + len(r.inplace) + len(r.lands)]))
            at += len(r.inplace) + len(r.lands)
        scr = refs[at:at + n_scr]
        at += n_scr
        r_sems = [refs[at + 2 * i:at + 2 * i + 2] for i in range(len(riders))]

        def copies():
            return [r.make(r_outs[i][0], r_srcs[i], r_outs[i][1], r_sems[i][0], r_sems[i][1])
                    for i, r in enumerate(riders)]

        def start():
            for starts, _ in copies():
                for cp in starts:
                    cp.start()

        def finish():
            for _, waits in copies():
                for wait in waits:
                    wait()

        if riders and grid:
            first = functools.reduce(jnp.logical_and, [pl.program_id(d) == 0 for d in range(len(grid))])
            last = functools.reduce(jnp.logical_and, [pl.program_id(d) == grid[d] - 1 for d in range(len(grid))])
            pl.when(first)(start)
            body(*ins, *outs, *scr)
            pl.when(last)(finish)
        else:
            if riders:
                start()
            body(*ins, *outs, *scr)
            if riders:
                finish()

    params = _params(sem) if grid else pltpu.CompilerParams(vmem_limit_bytes=VMEM_LIMIT)
    res = pl.pallas_call(
        full, name=name, grid=grid, in_specs=in_specs, out_specs=out_specs, out_shape=out_shape,
        scratch_shapes=scratch_shapes, input_output_aliases=aliases, compiler_params=params,
    )(*operands)
    outs, at, rider_res = res[:n_out], n_out, []
    for r in riders:
        rider_res.append((res[at:at + len(r.inplace)], res[at + len(r.inplace):at + len(r.inplace) + len(r.lands)]))
        at += len(r.inplace) + len(r.lands)
    return list(outs), rider_res


def _gammas():
    return [1.0 - 2.0 ** (-5.0 - h) for h in range(HEADS)]


def _decay_tables():
    idx = np.arange(SUPER)
    dist = np.abs(idx[:, None] - idx[None, :]).astype(np.float64)
    visible = (idx[None, :] // CHUNK) <= (idx[:, None] // CHUNK)
    mask = np.stack([np.where(visible, g ** dist, 0.0) for g in _gammas()])
    qd = np.concatenate([np.repeat((g ** (idx + 1.0))[:, None], HEAD_DIM, 1) for g in _gammas()], 1)
    kd = np.concatenate([np.repeat((g ** (SUPER - 1.0 - idx))[:, None], HEAD_DIM, 1) for g in _gammas()], 1)
    return (jnp.asarray(mask, F32), jnp.asarray(qd, F32), jnp.asarray(kd, F32))


def _rope_tables(s):
    inv_freq = ROPE_BASE ** (-np.arange(0, HEAD_DIM, 2, dtype=np.float64) / HEAD_DIM)
    ang = np.arange(s, dtype=np.float64)[:, None] * inv_freq[None, :]
    cos, sin = np.cos(ang), np.sin(ang)
    return (jnp.asarray(np.concatenate([cos, cos], 1), F32),
            jnp.asarray(np.concatenate([-sin, sin], 1), F32))


def _rope(t, cosf, sinf):
    return t * cosf + pltpu.roll(t, HEAD_DIM // 2, 1) * sinf


def _rope_t(t, cosf, sinf):
    return t * cosf - pltpu.roll(t, HEAD_DIM // 2, 1) * sinf


def _layernorm_fwd(z):
    mu = jnp.mean(z, axis=-1, keepdims=True)
    zc = z - mu
    var = jnp.mean(zc * zc, axis=-1, keepdims=True)
    rstd = lax.rsqrt(var + LN_EPS)
    return zc * rstd, rstd


def _layernorm_bwd(dy, xhat, rstd, gain):
    dxh = dy * gain
    m1 = jnp.mean(dxh, axis=-1, keepdims=True)
    m2 = jnp.mean(dxh * xhat, axis=-1, keepdims=True)
    return rstd * (dxh - m1 - xhat * m2)


def _proj_pool(x, win4, cosf, sinf, wpool, pscale, ts, riders=(), after=()):
    s = x.shape[0]
    nt = s // ts

    def body(x_ref, w_ref, cos_ref, sin_ref, wp_ref, ps_ref,
             xb_ref, q_ref, k_ref, v_ref, g_ref, pooled_ref, cat_ref, proj_scr, pext_scr):
        i = pl.program_id(0)
        xb = x_ref[...].astype(BF16)
        xb_ref[...] = xb
        for j in range(N_SHARD):
            proj_scr[:, j * IN_SH:(j + 1) * IN_SH] = _dot(xb, w_ref[j])
        cosf_t = cos_ref[...]
        sinf_t = sin_ref[...]
        for h in range(HEADS):
            lo = h * HEAD_DIM
            q_ref[:, lo:lo + HEAD_DIM] = _rope(proj_scr[:, lo:lo + HEAD_DIM], cosf_t, sinf_t).astype(BF16)
            kk = _rope(proj_scr[:, RET_W + lo:RET_W + lo + HEAD_DIM], cosf_t, sinf_t) * K_SCALE
            k_ref[:, lo:lo + HEAD_DIM] = kk.astype(BF16)
        v_ref[...] = proj_scr[:, 2 * RET_W:3 * RET_W].astype(BF16)
        g_ref[...] = proj_scr[:, 3 * RET_W:4 * RET_W]

        @pl.when(i == 0)
        def _():
            pext_scr[0:POOL_HALO, :] = jnp.zeros((POOL_HALO, POOL_W), F32)

        pext_scr[POOL_HALO:POOL_HALO + ts, :] = proj_scr[:, 4 * RET_W:IN_W]
        pos = (i * ts + lax.broadcasted_iota(jnp.int32, (ts, 1), 0) + 1).astype(F32)
        for gi, w in enumerate(POOL_WINDOWS):
            lo = gi * HEAD_DIM
            ext = pext_scr[:, lo:lo + HEAD_DIM]
            acc = ext
            shift = 1
            while shift < w:
                acc = acc + pltpu.roll(acc, shift, 0)
                shift *= 2
            tok = ext[POOL_HALO:POOL_HALO + ts]
            pooled = acc[POOL_HALO:POOL_HALO + ts] / jnp.minimum(pos, float(w)) - tok
            pooled_b = pooled.astype(BF16)
            pooled_ref[:, lo:lo + HEAD_DIM] = pooled_b
            lin = _dot(pooled_b, wp_ref[gi])
            cat_ref[:, lo:lo + HEAD_DIM] = (lin * ps_ref[:, lo:lo + HEAD_DIM]).astype(BF16)
        pext_scr[0:POOL_HALO, :] = pext_scr[ts:ts + POOL_HALO, :]

    tile = lambda w: pl.BlockSpec((ts, w), lambda i: (i, 0))
    return _call(
        body, name="proj_pool", grid=(nt,),
        in_specs=[tile(D_MODEL), _whole(), tile(HEAD_DIM), tile(HEAD_DIM), _whole(), _whole()],
        out_specs=[tile(D_MODEL), tile(RET_W), tile(RET_W), tile(RET_W), tile(RET_W), tile(POOL_W),
                   pl.BlockSpec((ts, POOL_W), lambda i: (i, 1))],
        out_shape=[jax.ShapeDtypeStruct((s, D_MODEL), BF16), jax.ShapeDtypeStruct((s, RET_W), BF16),
                   jax.ShapeDtypeStruct((s, RET_W), BF16), jax.ShapeDtypeStruct((s, RET_W), BF16),
                   jax.ShapeDtypeStruct((s, RET_W), F32), jax.ShapeDtypeStruct((s, POOL_W), BF16),
                   jax.ShapeDtypeStruct((s, 2 * RET_W), BF16)],
        scratch_shapes=[pltpu.VMEM((ts, IN_W), F32), pltpu.VMEM((ts + POOL_HALO, POOL_W), F32)],
        sem=("arbitrary",), operands=(x, win4, cosf, sinf, wpool, pscale), riders=riders, after=after,
    )


def _retention_fwd(q, k, v, g, cat, mask, qd, kd, riders=(), after=()):
    s = q.shape[0]
    ns = s // SUPER
    cdec = [gm ** float(SUPER) for gm in _gammas()]

    def body(q_ref, k_ref, v_ref, g_ref, cat_in, mask_ref, qd_ref, kd_ref,
             ret_ref, cat_ref, st_ref, state_scr):
        del cat_in
        n = pl.program_id(0)

        @pl.when(n == 0)
        def _():
            state_scr[...] = jnp.zeros_like(state_scr)

        for h in range(HEADS):
            sl = slice(h * HEAD_DIM, (h + 1) * HEAD_DIM)
            qh, kh, vh = q_ref[:, sl], k_ref[:, sl], v_ref[:, sl]
            sc = _dot_nt(qh, kh) * mask_ref[h]
            st = state_scr[h]
            stb = st.astype(BF16)
            st_ref[0, h] = stb
            qdb = (qh.astype(F32) * qd_ref[:, sl]).astype(BF16)
            kdb = (kh.astype(F32) * kd_ref[:, sl]).astype(BF16)
            ret = _dot(sc.astype(BF16), vh) + _dot(qdb, stb)
            state_scr[h] = st * cdec[h] + _dot_tn(kdb, vh)
            ret_ref[:, sl] = ret
            r = lax.rsqrt(jnp.mean(ret * ret, axis=-1, keepdims=True) + RMS_EPS)
            gh = g_ref[:, sl]
            cat_ref[:, sl] = ((ret * r) * (gh * _sigmoid(gh))).astype(BF16)

    tile = pl.BlockSpec((SUPER, RET_W), lambda n: (n, 0))
    return _call(
        body, name="retention_fwd", grid=(ns,),
        in_specs=[tile, tile, tile, tile, HBM_SPEC, _whole(), _whole(), _whole()],
        out_specs=[tile, tile, pl.BlockSpec((1, HEADS, HEAD_DIM, HEAD_DIM), lambda n: (n, 0, 0, 0))],
        out_shape=[jax.ShapeDtypeStruct((s, RET_W), F32), jax.ShapeDtypeStruct((s, 2 * RET_W), BF16),
                   jax.ShapeDtypeStruct((ns, HEADS, HEAD_DIM, HEAD_DIM), BF16)],
        scratch_shapes=[pltpu.VMEM((HEADS, HEAD_DIM, HEAD_DIM), F32)],
        aliases={4: 1}, sem=("arbitrary",), operands=(q, k, v, g, cat, mask, qd, kd), riders=riders,
        after=after,
    )


def _outproj_ln1(x, cat, wout, g1, b1, ts, riders=(), after=()):
    s = x.shape[0]

    def body(x_ref, cat_ref, w_ref, g_ref, b_ref, xhat_ref, rstd_ref, h1b_ref):
        z = ALPHA * x_ref[...] + _dot(cat_ref[...], w_ref[...])
        xhat, rstd = _layernorm_fwd(z)
        xhat_ref[...] = xhat
        rstd_ref[...] = rstd
        h1b_ref[...] = (xhat * g_ref[...] + b_ref[...]).astype(BF16)

    tile = lambda w: pl.BlockSpec((ts, w), lambda i: (i, 0))
    return _call(
        body, name="outproj_ln1", grid=(s // ts,),
        in_specs=[tile(D_MODEL), tile(D_MODEL), _whole(), _whole(), _whole()],
        out_specs=[tile(D_MODEL), tile(1), tile(D_MODEL)],
        out_shape=[jax.ShapeDtypeStruct((s, D_MODEL), F32), jax.ShapeDtypeStruct((s, 1), F32),
                   jax.ShapeDtypeStruct((s, D_MODEL), BF16)],
        sem=("arbitrary",), operands=(x, cat, wout, g1, b1), riders=riders, after=after,
    )


def _ffn_fwd_loss(xhat1, h1b, target, wup4, wdown, cw, cb, g1, b1, g2, b2, ts):
    s = xhat1.shape[0]

    def body(xhat_ref, h1b_ref, tgt_ref, wup_ref, wdn_ref, cw_ref, cb_ref, g1_ref, b1_ref, g2_ref, b2_ref,
             ub_ref, dz2_ref, dz2b_ref, loss_ref, dg2_ref, db2_ref, val_scr, gext_scr, act_scr, ffn_scr):
        i = pl.program_id(0)

        @pl.when(i == 0)
        def _():
            gext_scr[0:CONV_HALO, :] = jnp.zeros((CONV_HALO, D_FF), F32)
            loss_ref[...] = jnp.zeros_like(loss_ref)
            dg2_ref[...] = jnp.zeros_like(dg2_ref)
            db2_ref[...] = jnp.zeros_like(db2_ref)

        hb = h1b_ref[...]
        for half in range(2):
            lo = half * UP_SH
            gext_scr[CONV_HALO:CONV_HALO + ts, lo:lo + UP_SH] = _dot(hb, wup_ref[2 + half])
            val_scr[:, lo:lo + UP_SH] = _dot(hb, wup_ref[half])
            for c0 in range(lo, lo + UP_SH, FFN_STRIP):
                cols = slice(c0, c0 + FFN_STRIP)
                ext = gext_scr[:, cols]
                gate = ext[CONV_HALO:]
                hc = cb_ref[:, cols] + ((pltpu.roll(ext, 2, 0)[CONV_HALO:] * cw_ref[0:1, cols]
                                         + pltpu.roll(ext, 1, 0)[CONV_HALO:] * cw_ref[1:2, cols])
                                        + gate * cw_ref[2:3, cols])
                val = val_scr[:, cols]
                act_scr[:, cols] = ((hc * _sigmoid(hc)) * val).astype(BF16)
                ub_ref[:, cols] = val.astype(BF16)
                ub_ref[:, D_FF + c0:D_FF + c0 + FFN_STRIP] = gate.astype(BF16)
            part = _dot(act_scr[:, lo:lo + UP_SH], wdn_ref[lo:lo + UP_SH, :])
            if half == 0:
                ffn_scr[...] = part
            else:
                ffn_scr[...] += part
        gext_scr[0:CONV_HALO, :] = gext_scr[ts:ts + CONV_HALO, :]

        loss_acc = jnp.zeros((1, 1), F32)
        dg2_acc = jnp.zeros((1, D_MODEL), F32)
        db2_acc = jnp.zeros((1, D_MODEL), F32)
        for r0 in range(0, ts, LN_ROWS):
            rows = slice(r0, r0 + LN_ROWS)
            h1 = xhat_ref[rows, :] * g1_ref[...] + b1_ref[...]
            xhat2, rstd2 = _layernorm_fwd(ALPHA * h1 + ffn_scr[rows, :])
            diff = (xhat2 * g2_ref[...] + b2_ref[...]) - tgt_ref[rows, :]
            row = jnp.mean(diff * diff, axis=-1, keepdims=True)
            loss_acc = loss_acc + 0.5 * jnp.sum(row, axis=0, keepdims=True)
            dy = diff * (1.0 / D_MODEL)
            dg2_acc = dg2_acc + jnp.sum(dy * xhat2, axis=0, keepdims=True)
            db2_acc = db2_acc + jnp.sum(dy, axis=0, keepdims=True)
            dz2 = _layernorm_bwd(dy, xhat2, rstd2, g2_ref[...])
            dz2_ref[rows, :] = dz2
            dz2b_ref[rows, :] = dz2.astype(BF16)
        loss_ref[...] += loss_acc
        dg2_ref[...] += dg2_acc
        db2_ref[...] += db2_acc

    tile = lambda w: pl.BlockSpec((ts, w), lambda i: (i, 0))
    acc = lambda w: pl.BlockSpec((1, w), lambda i: (0, 0))
    return pl.pallas_call(
        body, name="ffn_fwd_loss", grid=(s // ts,),
        in_specs=[tile(D_MODEL), tile(D_MODEL), tile(D_MODEL)] + [_whole()] * 8,
        out_specs=[tile(2 * D_FF), tile(D_MODEL), tile(D_MODEL), acc(1), acc(D_MODEL), acc(D_MODEL)],
        out_shape=[jax.ShapeDtypeStruct((s, 2 * D_FF), BF16), jax.ShapeDtypeStruct((s, D_MODEL), F32),
                   jax.ShapeDtypeStruct((s, D_MODEL), BF16),
                   jax.ShapeDtypeStruct((1, 1), F32), jax.ShapeDtypeStruct((1, D_MODEL), F32),
                   jax.ShapeDtypeStruct((1, D_MODEL), F32)],
        scratch_shapes=[pltpu.VMEM((ts, D_FF), F32), pltpu.VMEM((ts + CONV_HALO, D_FF), F32),
                        pltpu.VMEM((ts, D_FF), BF16), pltpu.VMEM((ts, D_MODEL), F32)],
        compiler_params=_params(("arbitrary",)),
    )(xhat1, h1b, target, wup4, wdown, cw, cb, g1, b1, g2, b2)


def _ffn_bwd(dz2, dz2b, ub, xhat1, rstd1, wup4, wdown, cw, cb, g1, ts):
    s = dz2.shape[0]
    nt = s // ts
    hb = 16

    def body(dz2_ref, dz2b_ref, ub_ref, prev_ref, xhat_ref, rstd_ref, wup_ref, wdn_ref, cw_ref, cb_ref, g1_ref,
             a_ref, dub_ref, dz1_ref, dz1b_ref, dg1_ref, db1_ref, dcw_ref, dcb_ref, gext_scr, dext_scr, da_scr):
        i = pl.program_id(0)
        r = nt - 1 - i

        @pl.when(i == 0)
        def _():
            dext_scr[ts:ts + CONV_HALO, :] = jnp.zeros((CONV_HALO, D_FF), F32)
            dg1_ref[...] = jnp.zeros_like(dg1_ref)
            db1_ref[...] = jnp.zeros_like(db1_ref)
            dcw_ref[...] = jnp.zeros_like(dcw_ref)
            dcb_ref[...] = jnp.zeros_like(dcb_ref)

        da_scr[...] = _dot_nt(dz2b_ref[...], wdn_ref[...])
        prev = prev_ref[...].astype(F32)[hb - CONV_HALO:hb]
        gext_scr[0:CONV_HALO, :] = jnp.where(r == 0, 0.0, prev)
        n_ext = ts + CONV_HALO
        for c0 in range(0, D_FF, FFN_STRIP):
            cols = slice(c0, c0 + FFN_STRIP)
            gcols = slice(D_FF + c0, D_FF + c0 + FFN_STRIP)
            val = ub_ref[:, cols].astype(F32)
            gate = ub_ref[:, gcols].astype(F32)
            gext_scr[CONV_HALO:n_ext, cols] = gate
            ext = gext_scr[:, cols]
            g2s = pltpu.roll(ext, 2, 0)[CONV_HALO:]
            g1s = pltpu.roll(ext, 1, 0)[CONV_HALO:]
            hc = cb_ref[:, cols] + ((g2s * cw_ref[0:1, cols] + g1s * cw_ref[1:2, cols]) + gate * cw_ref[2:3, cols])
            sg = _sigmoid(hc)
            si = hc * sg
            a_ref[:, cols] = (si * val).astype(BF16)
            da = da_scr[:, cols]
            dhc = da * val * (sg * (1.0 + hc * (1.0 - sg)))
            dcb_ref[:, cols] += jnp.sum(dhc, axis=0, keepdims=True)
            dcw_ref[0:1, cols] += jnp.sum(dhc * g2s, axis=0, keepdims=True)
            dcw_ref[1:2, cols] += jnp.sum(dhc * g1s, axis=0, keepdims=True)
            dcw_ref[2:3, cols] += jnp.sum(dhc * gate, axis=0, keepdims=True)
            dext_scr[0:ts, cols] = dhc
            dext = dext_scr[:, cols]
            dgate = (dhc * cw_ref[2:3, cols] + pltpu.roll(dext, n_ext - 1, 0)[0:ts] * cw_ref[1:2, cols]
                     + pltpu.roll(dext, n_ext - 2, 0)[0:ts] * cw_ref[0:1, cols])
            dub_ref[:, cols] = (da * si).astype(BF16)
            dub_ref[:, gcols] = dgate.astype(BF16)
        dext_scr[ts:n_ext, :] = dext_scr[0:CONV_HALO, :]
        dh1 = ALPHA * dz2_ref[...]
        for j in range(N_SHARD):
            dh1 = dh1 + _dot_nt(dub_ref[:, j * UP_SH:(j + 1) * UP_SH], wup_ref[j])
        xhat = xhat_ref[...]
        dg1_ref[...] += jnp.sum(dh1 * xhat, axis=0, keepdims=True)
        db1_ref[...] += jnp.sum(dh1, axis=0, keepdims=True)
        dz1 = _layernorm_bwd(dh1, xhat, rstd_ref[...], g1_ref[...])
        dz1_ref[...] = dz1
        dz1b_ref[...] = dz1.astype(BF16)

    tile = lambda w: pl.BlockSpec((ts, w), lambda i: (nt - 1 - i, 0))
    acc = lambda rws, w: pl.BlockSpec((rws, w), lambda i: (0, 0))
    prev_spec = pl.BlockSpec((hb, D_FF), lambda i: (jnp.maximum((nt - 1 - i) * (ts // hb) - 1, 0), 1))
    return pl.pallas_call(
        body, name="ffn_bwd", grid=(nt,),
        in_specs=[tile(D_MODEL), tile(D_MODEL), tile(2 * D_FF), prev_spec, tile(D_MODEL), tile(1)] + [_whole()] * 5,
        out_specs=[tile(D_FF), tile(2 * D_FF), tile(D_MODEL), tile(D_MODEL), acc(1, D_MODEL), acc(1, D_MODEL),
                   acc(3, D_FF), acc(1, D_FF)],
        out_shape=[jax.ShapeDtypeStruct((s, D_FF), BF16), jax.ShapeDtypeStruct((s, 2 * D_FF), BF16),
                   jax.ShapeDtypeStruct((s, D_MODEL), F32), jax.ShapeDtypeStruct((s, D_MODEL), BF16),
                   jax.ShapeDtypeStruct((1, D_MODEL), F32),
                   jax.ShapeDtypeStruct((1, D_MODEL), F32), jax.ShapeDtypeStruct((3, D_FF), F32),
                   jax.ShapeDtypeStruct((1, D_FF), F32)],
        scratch_shapes=[pltpu.VMEM((ts + CONV_HALO, D_FF), F32), pltpu.VMEM((ts + CONV_HALO, D_FF), F32),
                        pltpu.VMEM((ts, D_FF), F32)],
        compiler_params=_params(("arbitrary",)),
    )(dz2, dz2b, ub, ub, xhat1, rstd1, wup4, wdown, cw, cb, g1)


def _mix_bwd(dz1, pooled, ret, g, wout, wpool, pscale, ts, riders=(), after=()):
    s = dz1.shape[0]
    nt = s // ts

    def body(dz1_ref, pooled_ref, ret_ref, g_ref, wout_ref, wp_ref, ps_ref,
             dret_ref, dgp_ref, dwp_ref, dps_ref, eext_scr):
        i = pl.program_id(0)
        r = nt - 1 - i

        @pl.when(i == 0)
        def _():
            eext_scr[ts:ts + POOL_HALO, :] = jnp.zeros((POOL_HALO, POOL_W), F32)
            dwp_ref[...] = jnp.zeros_like(dwp_ref)
            dps_ref[...] = jnp.zeros_like(dps_ref)

        dzb = dz1_ref[...].astype(BF16)
        dcat_r = _dot_nt(dzb, wout_ref[0:RET_W, :])
        dcat_p = _dot_nt(dzb, wout_ref[RET_W:2 * RET_W, :])
        pos = (r * ts + lax.broadcasted_iota(jnp.int32, (ts, 1), 0) + 1).astype(F32)
        dpooled = []
        for gi, w in enumerate(POOL_WINDOWS):
            sl = slice(gi * HEAD_DIM, (gi + 1) * HEAD_DIM)
            pb = pooled_ref[:, sl]
            dy = dcat_p[:, sl]
            dps_ref[:, sl] += jnp.sum(dy * _dot(pb, wp_ref[gi]), axis=0, keepdims=True)
            dlin = (dy * ps_ref[:, sl]).astype(BF16)
            dwp_ref[gi] += _dot_tn(pb, dlin)
            dpg = _dot_nt(dlin, wp_ref[gi])
            dpooled.append(dpg)
            eext_scr[0:ts, sl] = dpg / jnp.minimum(pos, float(w))
        for gi, w in enumerate(POOL_WINDOWS):
            sl = slice(gi * HEAD_DIM, (gi + 1) * HEAD_DIM)
            acc = eext_scr[:, sl]
            shift = 1
            while shift < w:
                acc = acc + pltpu.roll(acc, ts + POOL_HALO - shift, 0)
                shift *= 2
            dgp_ref[:, RET_W + gi * HEAD_DIM:RET_W + (gi + 1) * HEAD_DIM] = (acc[0:ts] - dpooled[gi]).astype(BF16)
        eext_scr[ts:ts + POOL_HALO, :] = eext_scr[0:POOL_HALO, :]
        for h in range(HEADS):
            sl = slice(h * HEAD_DIM, (h + 1) * HEAD_DIM)
            rt = ret_ref[:, sl]
            rr = lax.rsqrt(jnp.mean(rt * rt, axis=-1, keepdims=True) + RMS_EPS)
            rn = rt * rr
            gh = g_ref[:, sl]
            sg = _sigmoid(gh)
            dy = dcat_r[:, sl]
            dgp_ref[:, sl] = (dy * rn * (sg * (1.0 + gh * (1.0 - sg)))).astype(BF16)
            drn = dy * (gh * sg)
            dret_ref[:, sl] = (rr * (drn - rn * jnp.mean(drn * rn, axis=-1, keepdims=True))).astype(BF16)

    tile = lambda w: pl.BlockSpec((ts, w), lambda i: (nt - 1 - i, 0))
    return _call(
        body, name="mix_bwd", grid=(nt,),
        in_specs=[tile(D_MODEL), tile(POOL_W), tile(RET_W), tile(RET_W), _whole(), _whole(), _whole()],
        out_specs=[tile(RET_W), tile(2 * RET_W),
                   pl.BlockSpec((len(POOL_WINDOWS), HEAD_DIM, HEAD_DIM), lambda i: (0, 0, 0)),
                   pl.BlockSpec((1, POOL_W), lambda i: (0, 0))],
        out_shape=[jax.ShapeDtypeStruct((s, RET_W), BF16), jax.ShapeDtypeStruct((s, 2 * RET_W), BF16),
                   jax.ShapeDtypeStruct((len(POOL_WINDOWS), HEAD_DIM, HEAD_DIM), F32),
                   jax.ShapeDtypeStruct((1, POOL_W), F32)],
        scratch_shapes=[pltpu.VMEM((ts + POOL_HALO, POOL_W), F32)],
        sem=("arbitrary",), operands=(dz1, pooled, ret, g, wout, wpool, pscale), riders=riders,
        after=after,
    )


def _retention_bwd(q, k, v, dret, dgp, states, mask, qd, kd, cosf, sinf, riders=(), after=()):
    s = q.shape[0]
    ns = s // SUPER
    cdec = [gm ** float(SUPER) for gm in _gammas()]

    def body(q_ref, k_ref, v_ref, do_ref, dgp_ref, st_ref, mask_ref, qd_ref, kd_ref, cos_ref, sin_ref,
             dproj_ref, dstate_scr):
        i = pl.program_id(0)

        @pl.when(i == 0)
        def _():
            dstate_scr[...] = jnp.zeros_like(dstate_scr)

        cosf_t = cos_ref[...]
        sinf_t = sin_ref[...]
        for h in range(HEADS):
            sl = slice(h * HEAD_DIM, (h + 1) * HEAD_DIM)
            qh, kh, vh, doh = q_ref[:, sl], k_ref[:, sl], v_ref[:, sl], do_ref[:, sl]
            m = mask_ref[h]
            scb = (_dot_nt(qh, kh) * m).astype(BF16)
            dscb = (_dot_nt(doh, vh) * m).astype(BF16)
            stb = st_ref[0, h]
            dst = dstate_scr[h]
            dstb = dst.astype(BF16)
            qdb = (qh.astype(F32) * qd_ref[:, sl]).astype(BF16)
            kdb = (kh.astype(F32) * kd_ref[:, sl]).astype(BF16)
            dq = _dot(dscb, kh) + _dot_nt(doh, stb) * qd_ref[:, sl]
            dk = _dot_tn(dscb, qh) + _dot_nt(vh, dstb) * kd_ref[:, sl]
            dv = _dot_tn(scb, doh) + _dot(kdb, dstb)
            dstate_scr[h] = dst * cdec[h] + _dot_tn(qdb, doh)
            lo = h * HEAD_DIM
            dproj_ref[:, lo:lo + HEAD_DIM] = _rope_t(dq, cosf_t, sinf_t).astype(BF16)
            dproj_ref[:, RET_W + lo:RET_W + lo + HEAD_DIM] = _rope_t(dk * K_SCALE, cosf_t, sinf_t).astype(BF16)
            dproj_ref[:, 2 * RET_W + lo:2 * RET_W + lo + HEAD_DIM] = dv.astype(BF16)
        dproj_ref[:, 3 * RET_W:IN_W] = dgp_ref[...]

    tile = lambda w: pl.BlockSpec((SUPER, w), lambda i: (ns - 1 - i, 0))
    return _call(
        body, name="retention_bwd", grid=(ns,),
        in_specs=[tile(RET_W), tile(RET_W), tile(RET_W), tile(RET_W), tile(2 * RET_W),
                  pl.BlockSpec((1, HEADS, HEAD_DIM, HEAD_DIM), lambda i: (ns - 1 - i, 0, 0, 0)),
                  _whole(), _whole(), _whole(), tile(HEAD_DIM), tile(HEAD_DIM)],
        out_specs=[tile(IN_W)],
        out_shape=[jax.ShapeDtypeStruct((s, IN_W), BF16)],
        scratch_shapes=[pltpu.VMEM((HEADS, HEAD_DIM, HEAD_DIM), F32)],
        sem=("arbitrary",), operands=(q, k, v, dret, dgp, states, mask, qd, kd, cosf, sinf), riders=riders,
        after=after,
    )


def _dx(dz1, dproj, win4, ts, riders=(), after=()):
    s = dz1.shape[0]

    def body(dz1_ref, dp_ref, w_ref, dx_ref):
        acc = ALPHA * dz1_ref[...]
        for j in range(N_SHARD):
            acc = acc + _dot_nt(dp_ref[:, j * IN_SH:(j + 1) * IN_SH], w_ref[j])
        dx_ref[...] = acc

    tile = lambda w: pl.BlockSpec((ts, w), lambda i: (i, 0))
    return _call(
        body, name="dx", grid=(s // ts,),
        in_specs=[tile(D_MODEL), tile(IN_W), _whole()],
        out_specs=[tile(D_MODEL)],
        out_shape=[jax.ShapeDtypeStruct((s, D_MODEL), F32)],
        sem=("arbitrary",), operands=(dz1, dproj, win4), riders=riders, after=after,
    )


def _wgrad(a, b, tm, tn, name, stacked, m_outer, riders=(), after=()):
    s, m = a.shape
    n = b.shape[1]

    def body(a_ref, b_ref, o32_ref, o16_ref):
        res = _dot_tn(a_ref[...], b_ref[...])
        o32_ref[...] = res.reshape(o32_ref.shape)
        o16_ref[...] = res.astype(BF16).reshape(o16_ref.shape)

    if m_outer:
        grid, blocks = (m // tm, n // tn), (lambda g0, g1: (g0, g1))
    else:
        grid, blocks = (n // tn, m // tm), (lambda g0, g1: (g1, g0))
    if stacked:
        shape = (n // tn, m, tn)
        ospec = pl.BlockSpec((1, tm, tn), lambda g0, g1: (blocks(g0, g1)[1], blocks(g0, g1)[0], 0))
    else:
        shape = (m, n)
        ospec = pl.BlockSpec((tm, tn), lambda g0, g1: blocks(g0, g1))
    return _call(
        body, name=name, grid=grid,
        in_specs=[pl.BlockSpec((s, tm), lambda g0, g1: (0, blocks(g0, g1)[0])),
                  pl.BlockSpec((s, tn), lambda g0, g1: (0, blocks(g0, g1)[1]))],
        out_specs=[ospec, ospec],
        out_shape=[jax.ShapeDtypeStruct(shape, F32), jax.ShapeDtypeStruct(shape, BF16)],
        sem=("arbitrary", "arbitrary"), operands=(a, b), riders=riders, after=after,
    )


class _NoComm:
    def __init__(self, win4, wout, wup4, wdown):
        self.weights = dict(w_in=win4, w_out=wout, w_up=wup4, w_down=wdown)
        self.grads = {}

    def weight(self, name):
        return self.weights[name]

    def riders(self, call):
        return ()

    def after(self, call):
        return ()

    def landed(self, call, results, outs):
        pass

    def gradient(self, name, g32, g16):
        self.grads[name] = (g32, g16)


def _local_step(x, target, cw, cb, wpool, pscale, g1, b1, g2, b2, comm):
    s = x.shape[0]
    ts_a = min(512, s)
    ts_f = min(256, s)
    mask, qd, kd = _decay_tables()
    cosf, sinf = _rope_tables(s)
    wpool_b = wpool.astype(BF16)

    def run(call, fn, *args):
        outs, res = fn(*args, riders=comm.riders(call), after=comm.after(call))
        comm.landed(call, res, outs)
        return outs

    xb, q, k, v, g, pooled, cat = run("proj_pool", _proj_pool, x, comm.weight("w_in"), cosf, sinf, wpool_b,
                                      pscale, ts_a)
    ret, cat, states = run("retention_fwd", _retention_fwd, q, k, v, g, cat, mask, qd, kd)
    wout = comm.weight("w_out")
    xhat1, rstd1, h1b = run("outproj_ln1", _outproj_ln1, x, cat, wout, g1, b1, ts_a)
    wup4, wdown = comm.weight("w_up"), comm.weight("w_down")
    ub, dz2, dz2b, loss, dg2, db2 = _ffn_fwd_loss(xhat1, h1b, target, wup4, wdown, cw, cb, g1, b1, g2, b2, ts_f)

    act, dub, dz1, dz1b, dg1, db1, dcw, dcb = _ffn_bwd(dz2, dz2b, ub, xhat1, rstd1, wup4, wdown, cw, cb, g1, ts_f)
    half = D_MODEL // 2
    comm.gradient("w_up", *run("wgrad_up", _wgrad, h1b, dub, half, UP_SH, "wgrad_up", True, False))
    comm.gradient("w_out", *run("wgrad_out", _wgrad, cat, dz1b, D_MODEL, half, "wgrad_out", False, True))
    comm.gradient("w_down", *run("wgrad_down", _wgrad, act, dz2b, D_FF // 2, half, "wgrad_down", False, True))
    dret, dgp, dwp, dps = run("mix_bwd", _mix_bwd, dz1b, pooled, ret, g, wout, wpool_b, pscale, ts_a)
    dproj, = run("retention_bwd", _retention_bwd, q, k, v, dret, dgp, states, mask, qd, kd, cosf, sinf)
    comm.gradient("w_in", *run("wgrad_in", _wgrad, xb, dproj, D_MODEL, IN_SH, "wgrad_in", True, True))
    (grad_x,), _ = _dx(dz1, dproj, comm.weight("w_in"), ts_a, after=comm.after("dx"))
    small = dict(w_pool=dwp, pool_scale=dps, ln1_g=dg1, ln1_b=db1, conv_w=dcw, conv_b=dcb,
                 ln2_g=dg2, ln2_b=db2)
    return loss, grad_x, small


CAST_ROWS = 64
SHARD_SHAPES = ((D_MODEL, IN_SH), (OUT_SH, D_MODEL), (D_MODEL, UP_SH), (DOWN_SH, D_MODEL))
N_BIG = len(SHARD_SHAPES)
CW_PAD = (8, 768)


def _mesh_pos():
    return lax.axis_index("x"), lax.axis_index("y"), lax.axis_index("c")


def _other_chips(x, y):
    return [(1 - x, y), (x, 1 - y), (1 - x, 1 - y)]


def _half_rows(w, which):
    hr = SHARD_SHAPES[w][0] // 2
    return pl.ds(pl.multiple_of(which * hr, 16), hr)


def _gather_weights(shards, cw8, full):
    def body(*refs):
        in_refs = refs[:N_BIG]
        cw_ref = refs[N_BIG]
        out_refs = refs[N_BIG + 1:2 * N_BIG + 1]
        cwo_ref = refs[2 * N_BIG + 1]
        stage = refs[2 * N_BIG + 2:3 * N_BIG + 2]
        send_sems, recv_sems, fsend_sems, frecv_sems, cw_send, cw_recv, local_sems = refs[3 * N_BIG + 2:]
        x, y, c = _mesh_pos()
        j0 = 2 * x + y
        chips = _other_chips(x, y)

        def cast_to_stage(w):
            def cast(i, carry):
                rows = pl.ds(pl.multiple_of(i * CAST_ROWS, CAST_ROWS), CAST_ROWS)
                stage[w][rows, :] = in_refs[w][rows, :].astype(BF16)
                return carry
            lax.fori_loop(0, SHARD_SHAPES[w][0] // CAST_ROWS, cast, 0)

        for w in full:
            cast_to_stage(w)

        jx, jy, jd = 2 * (1 - x) + y, 2 * x + (1 - y), 2 * (1 - x) + (1 - y)
        neighbours = [((1 - x, y, c), jx), ((x, 1 - y, c), jy)]
        passed = jnp.where(c == 0, jx, jy)
        pass_to = (jnp.where(c == 0, x, 1 - x), jnp.where(c == 0, 1 - y, y), c)

        def nbr(w, k, block):
            return pltpu.make_async_remote_copy(
                src_ref=stage[w].at[_half_rows(w, c), :], dst_ref=out_refs[w].at[block, _half_rows(w, c), :],
                send_sem=send_sems.at[w, k], recv_sem=recv_sems.at[w, k],
                device_id=neighbours[k][0], device_id_type=MESH)

        def relay(w, block):
            return pltpu.make_async_remote_copy(
                src_ref=out_refs[w].at[passed, _half_rows(w, c), :],
                dst_ref=out_refs[w].at[block, _half_rows(w, c), :],
                send_sem=send_sems.at[w, 2], recv_sem=recv_sems.at[w, 2],
                device_id=pass_to, device_id_type=MESH)

        def d2d(w, k, block, half):
            return pltpu.make_async_remote_copy(
                src_ref=out_refs[w].at[block, _half_rows(w, half), :],
                dst_ref=out_refs[w].at[block, _half_rows(w, half), :],
                send_sem=fsend_sems.at[w, k], recv_sem=frecv_sems.at[w, k],
                device_id=(x, y, 1 - c), device_id_type=MESH)

        def conv(k, block):
            chip = chips[k]
            return pltpu.make_async_remote_copy(
                src_ref=cw_ref, dst_ref=cwo_ref.at[block], send_sem=cw_send.at[k], recv_sem=cw_recv.at[k],
                device_id=(chip[0], chip[1], c), device_id_type=MESH)

        sent = [nbr(w, k, j0) for w in full for k in range(2)] + [conv(k, j0) for k in range(3)]
        for cp in sent:
            cp.start()
        for w in range(N_BIG):
            if w not in full:
                cast_to_stage(w)
        local = [pltpu.make_async_copy(stage[w], out_refs[w].at[j0], local_sems.at[w]) for w in range(N_BIG)]
        local.append(pltpu.make_async_copy(cw_ref, cwo_ref.at[j0], local_sems.at[N_BIG]))
        for cp in local:
            cp.start()
        for w in full:
            for k, (_, block) in enumerate(neighbours):
                nbr(w, k, block).wait_recv()
            later = [relay(w, passed)] + [d2d(w, k, block, c) for k, (_, block) in enumerate(neighbours)]
            for cp in later:
                cp.start()
            sent += later
        for w in full:
            relay(w, jd).wait_recv()
            fw = d2d(w, 2, jd, c)
            fw.start()
            sent.append(fw)
        for w in full:
            for k, block in enumerate([jx, jy, jd]):
                d2d(w, k, block, 1 - c).wait_recv()
        for k, chip in enumerate(chips):
            conv(k, 2 * chip[0] + chip[1]).wait_recv()
        for cp in sent:
            cp.wait_send()
        for cp in local:
            cp.wait()

    out_shape = [jax.ShapeDtypeStruct((N_SHARD,) + shp, BF16) for shp in SHARD_SHAPES]
    out_shape.append(jax.ShapeDtypeStruct((N_SHARD,) + CW_PAD, F32))
    return pl.pallas_call(
        body, name="gather_weights",
        in_specs=[_whole()] * (N_BIG + 1),
        out_specs=[HBM_SPEC] * (N_BIG + 1),
        out_shape=out_shape,
        scratch_shapes=[pltpu.VMEM(shp, BF16) for shp in SHARD_SHAPES] + [
            pltpu.SemaphoreType.DMA((N_BIG, 3)), pltpu.SemaphoreType.DMA((N_BIG, 3)),
            pltpu.SemaphoreType.DMA((N_BIG, 3)), pltpu.SemaphoreType.DMA((N_BIG, 3)),
            pltpu.SemaphoreType.DMA((3,)), pltpu.SemaphoreType.DMA((3,)),
            pltpu.SemaphoreType.DMA((N_BIG + 1,))],
        compiler_params=pltpu.CompilerParams(vmem_limit_bytes=VMEM_LIMIT),
    )(*shards, cw8)


def _gather_rider(arrays, ops):
    ws = sorted(arrays)

    def make(inplace, srcs, lands, send_sems, recv_sems):
        del srcs, lands
        x, y, c = _mesh_pos()
        j0, jx, jy, jd = 2 * x + y, 2 * (1 - x) + y, 2 * x + (1 - y), 2 * (1 - x) + (1 - y)
        x_nbr, y_nbr, sibling = (1 - x, y, c), (x, 1 - y, c), (x, y, 1 - c)
        starts, waits = [], []
        for n, (kind, w, (r0, nr)) in enumerate(ops):
            ref = inplace[ws.index(w)]
            hr = SHARD_SHAPES[w][0] // 2
            rows = lambda core: pl.ds(pl.multiple_of(core * hr + r0, 16), nr)
            mine, theirs = rows(c), rows(1 - c)
            if kind == "ici":
                moves = [(ref.at[j0, mine, :], x_nbr, ref.at[jx, mine, :]),
                         (ref.at[j0, mine, :], y_nbr, ref.at[jy, mine, :]),
                         (ref.at[j0, mine, :], (1 - x, 1 - y, c), ref.at[jd, mine, :])]
            elif kind == "nbr":
                moves = [(ref.at[j0, mine, :], x_nbr, ref.at[jx, mine, :]),
                         (ref.at[j0, mine, :], y_nbr, ref.at[jy, mine, :])]
            elif kind == "relay":
                passed = jnp.where(c == 0, jx, jy)
                to = (jnp.where(c == 0, x, 1 - x), jnp.where(c == 0, 1 - y, y), c)
                moves = [(ref.at[passed, mine, :], to, ref.at[jd, mine, :])]
            else:
                blocks = dict(d2d=[jx, jy, jd], d2d_nbr=[jx, jy], d2d_diag=[jd])[kind]
                moves = [(ref.at[b, mine, :], sibling, ref.at[b, theirs, :]) for b in blocks]
            for k, (src, to, landing) in enumerate(moves):
                sems = dict(send_sem=send_sems.at[3 * n + k], recv_sem=recv_sems.at[3 * n + k],
                            device_id=to, device_id_type=MESH)
                send = pltpu.make_async_remote_copy(src_ref=src, dst_ref=src, **sems)
                arrival = pltpu.make_async_remote_copy(src_ref=src, dst_ref=landing, **sems)
                starts.append(send)
                waits += [arrival.wait_recv, send.wait_send]
        return starts, waits

    return _Rider([arrays[w] for w in ws], [], [], 3 * len(ops), make)


def _whole_half(w):
    return (0, SHARD_SHAPES[w][0] // 2)


def _pair_rider(ws, g16s):
    def make(inplace, srcs, lands, send_sems, recv_sems):
        del inplace
        x, y, c = _mesh_pos()
        copies = [pltpu.make_async_remote_copy(
            src_ref=srcs[i].at[:, _half_rows(w, 1 - c), :], dst_ref=lands[i],
            send_sem=send_sems.at[i], recv_sem=recv_sems.at[i], device_id=(x, y, 1 - c), device_id_type=MESH)
            for i, w in enumerate(ws)]
        return copies, [cp.wait for cp in copies]

    lands = [jax.ShapeDtypeStruct((N_SHARD, SHARD_SHAPES[w][0] // 2, SHARD_SHAPES[w][1]), BF16) for w in ws]
    return _Rider([], g16s, lands, len(ws), make)


def _chip_rider(ws, p16s, rows=None, landing=None):
    def make(inplace, srcs, lands, send_sems, recv_sems):
        x, y, c = _mesh_pos()
        dsts = inplace if landing is not None else lands
        copies = []
        for i, w in enumerate(ws):
            r0, nr = rows if rows is not None else _whole_half(w)
            for k, chip in enumerate(_other_chips(x, y)):
                copies.append(pltpu.make_async_remote_copy(
                    src_ref=srcs[i].at[2 * chip[0] + chip[1], pl.ds(r0, nr), :],
                    dst_ref=dsts[i].at[k, pl.ds(r0, nr), :],
                    send_sem=send_sems.at[3 * i + k], recv_sem=recv_sems.at[3 * i + k],
                    device_id=(chip[0], chip[1], c), device_id_type=MESH))
        return copies, [cp.wait for cp in copies]

    lands = [jax.ShapeDtypeStruct((3, SHARD_SHAPES[w][0] // 2, SHARD_SHAPES[w][1]), BF16) for w in ws]
    if landing is not None:
        return _Rider(landing, p16s, [], 3 * len(ws), make)
    return _Rider([], p16s, lands, 3 * len(ws), make)


def _final_rider(halves):
    def make(inplace, srcs, lands, send_sems, recv_sems):
        del inplace
        x, y, c = _mesh_pos()
        copies = [pltpu.make_async_remote_copy(
            src_ref=srcs[i], dst_ref=lands[i], send_sem=send_sems.at[i], recv_sem=recv_sems.at[i],
            device_id=(x, y, 1 - c), device_id_type=MESH) for i in range(len(halves))]
        return copies, [cp.wait for cp in copies]

    return _Rider([], halves, [jax.ShapeDtypeStruct(h.shape, h.dtype) for h in halves], len(halves), make)


def _comm_only(name, riders):
    _, res = _call(lambda: None, name=name, grid=(), in_specs=[], out_specs=[], out_shape=[], operands=(),
                   riders=riders)
    return res


class _SemList:
    def __init__(self, refs):
        self.at = list(refs)


def _split_start(name, rider):
    assert not rider.inplace
    ns, nl, n = len(rider.srcs), len(rider.lands), rider.n_copies

    def body(*refs):
        srcs, lands = refs[:ns], refs[ns:ns + nl]
        sems = refs[ns + nl:ns + nl + 2 * n]
        token = refs[-1]
        starts, _ = rider.make([], srcs, lands, _SemList(sems[:n]), _SemList(sems[n:]))
        for cp in starts:
            cp.start()
        token[...] = jnp.zeros_like(token)

    buffers = [pltpu.with_memory_space_constraint(a, pltpu.HBM) for a in rider.srcs]
    buffers += [pltpu.with_memory_space_constraint(lax.empty(s.shape, s.dtype), pltpu.HBM) for s in rider.lands]
    hbm = pl.BlockSpec(memory_space=pltpu.HBM)
    sem = pl.BlockSpec(memory_space=pltpu.SEMAPHORE)
    outs = pl.pallas_call(
        body, name=name,
        out_shape=tuple([pltpu.SemaphoreType.DMA(())] * (2 * n) + [pltpu.HBM(b.shape, b.dtype) for b in buffers]
                        + [jax.ShapeDtypeStruct((8, 128), F32)]),
        in_specs=[hbm] * (ns + nl),
        out_specs=tuple([sem] * (2 * n) + [hbm] * (ns + nl) + [_whole()]),
        input_output_aliases={i: 2 * n + i for i in range(ns + nl)},
        compiler_params=pltpu.CompilerParams(has_side_effects=pltpu.SideEffectType.DATAFLOW_SIDE_EFFECTING),
    )(*buffers)
    return (rider, outs[:2 * n], outs[2 * n:2 * n + ns + nl]), outs[-1]


def _split_wait(name, state, after):
    rider, sems, buffers = state
    ns, nl, n = len(rider.srcs), len(rider.lands), rider.n_copies

    def body(*refs):
        srcs, lands = refs[:ns], refs[ns:ns + nl]
        sem_refs = refs[ns + nl:ns + nl + 2 * n]
        _, waits = rider.make([], srcs, lands, _SemList(sem_refs[:n]), _SemList(sem_refs[n:]))
        for wait in waits:
            wait()

    hbm = pl.BlockSpec(memory_space=pltpu.HBM)
    sem = pl.BlockSpec(memory_space=pltpu.SEMAPHORE)
    outs = pl.pallas_call(
        body, name=name,
        out_shape=tuple(pltpu.HBM(b.shape, b.dtype) for b in buffers),
        in_specs=[hbm] * (ns + nl) + [sem] * (2 * n) + [HBM_SPEC],
        out_specs=tuple([hbm] * (ns + nl)),
        input_output_aliases={i: i for i in range(ns + nl)},
        compiler_params=pltpu.CompilerParams(has_side_effects=pltpu.SideEffectType.DATAFLOW_SIDE_EFFECTING),
    )(*buffers, *sems, after)
    return list(outs[:ns]), list(outs[ns:])


def _pair_sum(pos, ws, g32s, recvs):
    n = len(ws)

    def body(pos_ref, *refs):
        del pos_ref
        g_refs, r_refs = refs[:n], refs[n:2 * n]
        p32_refs, p16_refs = refs[2 * n:3 * n], refs[3 * n:]
        for i in range(n):
            tot = g_refs[i][...] + r_refs[i][...].astype(F32)
            p32_refs[i][...] = tot
            p16_refs[i][...] = tot.astype(BF16)

    halves = [(SHARD_SHAPES[w][0] // 2, SHARD_SHAPES[w][1]) for w in ws]
    own = [pl.BlockSpec((None, None) + h, lambda j, pos_ref: (j, pos_ref[0], 0, 0)) for h in halves]
    blk = [pl.BlockSpec((None,) + h, lambda j, pos_ref: (j, 0, 0)) for h in halves]
    g4 = [g.reshape((N_SHARD, 2) + h) for g, h in zip(g32s, halves)]
    outs = pl.pallas_call(
        body, name="pair_sum_" + "_".join(str(w) for w in ws),
        grid_spec=pltpu.PrefetchScalarGridSpec(
            num_scalar_prefetch=1, grid=(N_SHARD,), in_specs=own + blk, out_specs=blk + blk),
        out_shape=[jax.ShapeDtypeStruct((N_SHARD,) + h, F32) for h in halves]
        + [jax.ShapeDtypeStruct((N_SHARD,) + h, BF16) for h in halves],
        compiler_params=_params(("arbitrary",)),
    )(pos, *g4, *recvs)
    return outs[:n], outs[n:]


def _chip_sum(pos, p32s, recvs):
    parts = 2

    def body(pos_ref, *refs):
        del pos_ref
        p_refs, r_refs, f_refs = refs[:N_BIG], refs[N_BIG:2 * N_BIG], refs[2 * N_BIG:]
        for w in range(N_BIG):
            f_refs[w][...] = ((p_refs[w][...] + r_refs[w][0].astype(F32)) + r_refs[w][1].astype(F32)) \
                + r_refs[w][2].astype(F32)

    quarters = [(r // 2 // parts, cc) for r, cc in SHARD_SHAPES]
    own = [pl.BlockSpec((None,) + qt, lambda i, pos_ref: (pos_ref[1], i, 0)) for qt in quarters]
    rcv = [pl.BlockSpec((3,) + qt, lambda i, pos_ref: (0, i, 0)) for qt in quarters]
    out = [pl.BlockSpec(qt, lambda i, pos_ref: (i, 0)) for qt in quarters]
    return pl.pallas_call(
        body, name="chip_sum",
        grid_spec=pltpu.PrefetchScalarGridSpec(
            num_scalar_prefetch=1, grid=(parts,), in_specs=own + rcv, out_specs=out),
        out_shape=[jax.ShapeDtypeStruct((r // 2, cc), F32) for r, cc in SHARD_SHAPES],
        compiler_params=_params(("arbitrary",)),
    )(pos, *p32s, *recvs)


def _adamw(w, g, m, v):
    m_new = ADAM_B1 * m + (1.0 - ADAM_B1) * g
    v_new = ADAM_B2 * v + (1.0 - ADAM_B2) * (g * g)
    m_hat = m_new / (1.0 - ADAM_B1 ** ADAM_STEP)
    v_hat = v_new / (1.0 - ADAM_B2 ** ADAM_STEP)
    delta = -ADAM_LR * (m_hat / (jnp.sqrt(v_hat) + ADAM_EPS) + ADAM_WD * w)
    return delta, m_new, v_new


def _adam_big(pos, mine, theirs, ws, ms, vs):
    nb = 4

    def body(pos_ref, *refs):
        hf = pl.program_id(0)
        groups = [refs[i * N_BIG:(i + 1) * N_BIG] for i in range(9)]
        f_refs, t_refs, w_refs, m_refs, v_refs, go_refs, do_refs, mo_refs, vo_refs = groups
        for w in range(N_BIG):
            g = jnp.where(hf == pos_ref[0], f_refs[w][...], t_refs[w][...])
            delta, m_new, v_new = _adamw(w_refs[w][...], g, m_refs[w][...], v_refs[w][...])
            go_refs[w][...] = g
            do_refs[w][...] = delta
            mo_refs[w][...] = m_new
            vo_refs[w][...] = v_new

    blocks = [(r // 2 // nb, cc) for r, cc in SHARD_SHAPES]
    half = [pl.BlockSpec(b, lambda hf, i, pos_ref: (i, 0)) for b in blocks]
    full = [pl.BlockSpec((None,) + b, lambda hf, i, pos_ref: (0, hf * nb + i, 0)) for b in blocks]
    shapes = [jax.ShapeDtypeStruct((1,) + shp, F32) for shp in SHARD_SHAPES]
    outs = pl.pallas_call(
        body, name="adam_big",
        grid_spec=pltpu.PrefetchScalarGridSpec(
            num_scalar_prefetch=1, grid=(2, nb), in_specs=half + half + full * 3, out_specs=full * 4),
        out_shape=shapes * 4,
        compiler_params=_params(("arbitrary", "arbitrary")),
    )(pos, *mine, *theirs, *ws, *ms, *vs)
    return [outs[i * N_BIG:(i + 1) * N_BIG] for i in range(4)]


SMALL_ROWS = 8
ROW_CONV_B, ROW_POOL_SCALE, ROW_LN1_G, ROW_LN1_B, ROW_LN2_G, ROW_LN2_B, ROW_LOSS = range(7)
SMALL_VECS = ((ROW_CONV_B, D_FF), (ROW_POOL_SCALE, POOL_W), (ROW_LN1_G, D_MODEL), (ROW_LN1_B, D_MODEL),
              (ROW_LN2_G, D_MODEL), (ROW_LN2_B, D_MODEL))


def _small_reduce(loss, dwp, dcw4, vec_grads, after=()):
    nv = len(SMALL_VECS)

    def body(*refs):
        loss_ref, dwp_ref, dcw_ref = refs[0:3]
        gvec = refs[3:3 + nv]
        o = 3 + nv
        tot_a_ref, tot_b_ref, tot_c_ref = refs[o:o + 3]
        (vec_scr, sib_a, sib_b, sib_c, all_a, all_b, all_c,
         send1, recv1, send2, recv2) = refs[o + 3:]
        x, y, c = _mesh_pos()
        j0 = 2 * x + y
        chips = _other_chips(x, y)

        vec_scr[...] = jnp.zeros_like(vec_scr)
        for (row, n), ref in zip(SMALL_VECS, gvec):
            vec_scr[row:row + 1, 0:n] = ref[...]
        vec_scr[ROW_LOSS:ROW_LOSS + 1, 0:HEAD_DIM] = jnp.broadcast_to(loss_ref[...], (1, HEAD_DIM))

        mine = (dwp_ref, vec_scr, dcw_ref)
        sib = (sib_a, sib_b, sib_c)
        every = (all_a, all_b, all_c)
        first = [pltpu.make_async_remote_copy(
            src_ref=mine[b], dst_ref=sib[b], send_sem=send1.at[b], recv_sem=recv1.at[b],
            device_id=(x, y, 1 - c), device_id_type=MESH) for b in range(3)]
        for cp in first:
            cp.start()
        for cp in first:
            cp.wait()
        for b in range(3):
            every[b][j0] = mine[b][...] + sib[b][...]

        def ici(b, k, block):
            chip = chips[k]
            return pltpu.make_async_remote_copy(
                src_ref=every[b].at[block], dst_ref=every[b].at[block],
                send_sem=send2.at[b, k], recv_sem=recv2.at[b, k],
                device_id=(chip[0], chip[1], c), device_id_type=MESH)

        second = [ici(b, k, j0) for b in range(3) for k in range(3)]
        for cp in second:
            cp.start()
        for k, chip in enumerate(chips):
            for b in range(3):
                ici(b, k, 2 * chip[0] + chip[1]).wait_recv()
        for cp in second:
            cp.wait_send()

        tot_a_ref[...] = ((all_a[0] + all_a[1]) + all_a[2]) + all_a[3]
        tot_b_ref[...] = ((all_b[0] + all_b[1]) + all_b[2]) + all_b[3]
        tot_c_ref[...] = ((all_c[0] + all_c[1]) + all_c[2]) + all_c[3]

    a_shape = dwp.shape
    b_shape = (SMALL_ROWS, D_FF)
    c_shape = dcw4.shape
    outs, _ = _call(
        body, name="small_reduce", grid=(),
        in_specs=[_whole()] * (3 + nv), out_specs=[_whole()] * 3,
        out_shape=[jax.ShapeDtypeStruct(shp, F32) for shp in (a_shape, b_shape, c_shape)],
        scratch_shapes=[pltpu.VMEM(b_shape, F32),
                        pltpu.VMEM(a_shape, F32), pltpu.VMEM(b_shape, F32), pltpu.VMEM(c_shape, F32),
                        pltpu.VMEM((N_SHARD,) + a_shape, F32), pltpu.VMEM((N_SHARD,) + b_shape, F32),
                        pltpu.VMEM((N_SHARD,) + c_shape, F32),
                        pltpu.SemaphoreType.DMA((3,)), pltpu.SemaphoreType.DMA((3,)),
                        pltpu.SemaphoreType.DMA((3, 3)), pltpu.SemaphoreType.DMA((3, 3))],
        operands=(loss, dwp, dcw4, *vec_grads), after=after,
    )
    return outs


def _small_adam(tot_a, tot_b, tot_c, wp, cwp, vec_ws, m_wp, m_cwp, vec_ms, v_wp, v_cwp, vec_vs):
    nv = len(SMALL_VECS)
    np_ = 2 + nv

    def body(*refs):
        tot_a_ref, tot_b_ref, tot_c_ref = refs[0:3]
        w_all, m_all, v_all = (refs[3 + i * np_:3 + (i + 1) * np_] for i in range(3))
        loss_out = refs[3 + 3 * np_]
        outs = refs[4 + 3 * np_:]
        x, y, _ = _mesh_pos()
        tot_b = tot_b_ref[...]
        loss_out[...] = tot_b[ROW_LOSS:ROW_LOSS + 1, 0:1]
        grads = [tot_a_ref[...], tot_c_ref[2 * x + y]] + [tot_b[row:row + 1, 0:n] for row, n in SMALL_VECS]
        for p in range(np_):
            delta, m_new, v_new = _adamw(w_all[p][...], grads[p], m_all[p][...], v_all[p][...])
            outs[p][...] = grads[p]
            outs[np_ + p][...] = delta
            outs[2 * np_ + p][...] = m_new
            outs[3 * np_ + p][...] = v_new

    pshapes = [wp.shape, CW_PAD] + [wv.shape for wv in vec_ws]
    out_shape = [jax.ShapeDtypeStruct((1, 1), F32)] + [jax.ShapeDtypeStruct(s, F32) for s in pshapes] * 4
    outs = pl.pallas_call(
        body, name="small_adam",
        in_specs=[_whole()] * (3 + 3 * np_), out_specs=[_whole()] * len(out_shape), out_shape=out_shape,
        compiler_params=pltpu.CompilerParams(vmem_limit_bytes=VMEM_LIMIT),
    )(tot_a, tot_b, tot_c, wp, cwp, *vec_ws, m_wp, m_cwp, *vec_ms, v_wp, v_cwp, *vec_vs)
    return outs[0], [outs[1 + i * np_:1 + (i + 1) * np_] for i in range(4)]


def _pad_cw(a):
    pad = [(0, 0)] * (a.ndim - 2) + [(0, CW_PAD[0] - a.shape[-2]), (0, CW_PAD[1] - a.shape[-1])]
    return jnp.pad(a, pad)


def kernel(x, w_in, w_pool, pool_scale, w_out, ln1_g, ln1_b, w_up, conv_w, conv_b, w_down, ln2_g, ln2_b, loss_target, m_w_in, m_w_pool, m_pool_scale, m_w_out, m_ln1_g, m_ln1_b, m_w_up, m_conv_w, m_conv_b, m_w_down, m_ln2_g, m_ln2_b, v_w_in, v_w_pool, v_pool_scale, v_w_out, v_ln1_g, v_ln1_b, v_w_up, v_conv_w, v_conv_b, v_w_down, v_ln2_g, v_ln2_b):
    pos = jnp.stack([lax.axis_index("c"), 2 * lax.axis_index("x") + lax.axis_index("y")]).astype(jnp.int32)
    order = ("w_in", "w_out", "w_up", "w_down")
    w_in_i, w_out_i, w_up_i, w_down_i = range(N_BIG)

    gathered = _gather_weights([w_in[0], w_out[0], w_up[0], w_down[0]], _pad_cw(conv_w[0]), (w_in_i,))
    cw_full = jnp.transpose(gathered[N_BIG][:, 0:3, 0:DOWN_SH], (1, 0, 2)).reshape(3, D_FF)
    up_a, up_b, up_c = (0, 224), (224, 160), (384, 128)
    assert up_c[0] + up_c[1] == SHARD_SHAPES[w_up_i][0] // 2

    class MeshComm:
        def __init__(self):
            self.w = {i: gathered[i] for i in range(N_BIG)}
            self.g32, self.g16, self.p32, self.p16, self.recv_b = {}, {}, {}, {}, {}
            self.up_complete = False
            self.tokens, self.chips = {}, []

        def weight(self, name):
            i = order.index(name)
            if name == "w_up" and not self.up_complete:
                (arrs, _), = _comm_only("gather_up_last", [_gather_rider(
                    {i: self.w[i]}, [("d2d_diag", i, up_b), ("d2d", i, up_c)])])
                self.w[i], self.up_complete = arrs[0], True
            full = self.w[i]
            return full.reshape(-1, full.shape[-1]) if name in ("w_out", "w_down") else full

        def _gather(self, ws, ops):
            return _gather_rider({w: self.w[w] for w in ws}, ops), ("w", ws)

        def _pair(self, ws):
            return _pair_rider(ws, [self.g16[w] for w in ws]), ("recv_a", ws)

        def _chip(self, ws, rows=None, resume=False):
            landing = [self.recv_b[w] for w in ws] if resume else None
            return _chip_rider(ws, [self.p16[w] for w in ws], rows, landing), ("recv_b", ws)

        def plan(self, call):
            out_all, down_all = _whole_half(w_out_i), _whole_half(w_down_i)
            if call == "proj_pool":
                return [self._gather([w_out_i, w_up_i, w_down_i],
                                     [("ici", w_out_i, out_all), ("nbr", w_down_i, down_all),
                                      ("nbr", w_up_i, up_a)])]
            if call == "retention_fwd":
                return [self._gather([w_out_i, w_up_i, w_down_i],
                                     [("d2d", w_out_i, out_all),
                                      ("relay", w_down_i, down_all), ("d2d_nbr", w_down_i, down_all),
                                      ("relay", w_up_i, up_a), ("d2d_nbr", w_up_i, up_a), ("nbr", w_up_i, up_b)])]
            if call == "outproj_ln1":
                return [self._gather([w_up_i, w_down_i],
                                     [("d2d_diag", w_down_i, down_all), ("d2d_diag", w_up_i, up_a),
                                      ("relay", w_up_i, up_b), ("d2d_nbr", w_up_i, up_b), ("ici", w_up_i, up_c)])]
            return []

        def after(self, call):
            return tuple(self.tokens.pop(call, ()))

        def riders(self, call):
            self.pending = self.plan(call)
            return [r for r, _ in self.pending]

        def _start(self, name, rider, before):
            state, token = _split_start(name, rider)
            self.tokens.setdefault(before, []).append(token)
            return state

        def _finish_pair(self, name, state, ws, after):
            _, lands = _split_wait(name, state, after)
            self._finish_sum(ws, lands)

        def landed(self, call, results, outs):
            for (_, (slot, ws)), (inplace, lands) in zip(self.pending, results):
                for w, arr in zip(ws, inplace if len(inplace) else lands):
                    getattr(self, slot)[w] = arr
            if call == "wgrad_out":
                self._finish_pair("pair_exchange_up_wait", self.pair_up, [w_up_i], outs[1])
                self.chips.append(([w_up_i], self._start(
                    "chip_exchange_up_start", self._chip([w_up_i])[0], "wgrad_down")))
            if call == "mix_bwd":
                ws = [w_out_i, w_down_i]
                self._finish_pair("pair_exchange_out_down_wait", self.pair_out_down, ws, outs[0])
                self.chips.append((ws, self._start(
                    "chip_exchange_out_down_start", self._chip(ws)[0], "retention_bwd")))

        def gradient(self, name, g32, g16):
            w = order.index(name)
            shape = (N_SHARD,) + SHARD_SHAPES[w]
            self.g32[w], self.g16[w] = g32.reshape(shape), g16.reshape(shape)
            if name == "w_up":
                self.pair_up = self._start("pair_exchange_up_start", self._pair([w])[0], "wgrad_out")
            if name == "w_down":
                self.pair_out_down = self._start("pair_exchange_out_down_start",
                                                 self._pair([w_out_i, w_down_i])[0], "mix_bwd")
            if name == "w_in":
                (_, lands), = _comm_only("pair_exchange_in", [self._pair([w])[0]])
                self._finish_sum([w], lands)
                self.chips.append(([w], self._start("chip_exchange_in_start", self._chip([w])[0], "dx")))

        def _finish_sum(self, ws, lands):
            p32s, p16s = _pair_sum(pos, ws, [self.g32[w] for w in ws], lands)
            for w, p32, p16 in zip(ws, p32s, p16s):
                self.p32[w], self.p16[w] = p32, p16

        def finish_chips(self, after):
            for n, (ws, state) in enumerate(self.chips):
                _, lands = _split_wait("chip_exchange_wait_%d" % n, state, after)
                for w, arr in zip(ws, lands):
                    self.recv_b[w] = arr

    comm = MeshComm()
    loss, grad_x, small = _local_step(x[0], loss_target[0], cw_full, conv_b, w_pool[0], pool_scale,
                                      ln1_g, ln1_b, ln2_g, ln2_b, comm)

    dcw4 = _pad_cw(jnp.transpose(small["conv_w"].reshape(3, N_SHARD, DOWN_SH), (1, 0, 2)))
    vec_names = ("conv_b", "pool_scale", "ln1_g", "ln1_b", "ln2_g", "ln2_b")
    given = dict(w_pool=w_pool, pool_scale=pool_scale, ln1_g=ln1_g, ln1_b=ln1_b, conv_w=conv_w, conv_b=conv_b,
                 ln2_g=ln2_g, ln2_b=ln2_b)
    given_m = dict(w_pool=m_w_pool, pool_scale=m_pool_scale, ln1_g=m_ln1_g, ln1_b=m_ln1_b, conv_w=m_conv_w,
                   conv_b=m_conv_b, ln2_g=m_ln2_g, ln2_b=m_ln2_b)
    given_v = dict(w_pool=v_w_pool, pool_scale=v_pool_scale, ln1_g=v_ln1_g, ln1_b=v_ln1_b, conv_w=v_conv_w,
                   conv_b=v_conv_b, ln2_g=v_ln2_g, ln2_b=v_ln2_b)
    args = []
    for src in (given, given_m, given_v):
        args += [src["w_pool"][0], _pad_cw(src["conv_w"][0]), [src[n] for n in vec_names]]
    comm.finish_chips(grad_x)
    every = range(N_BIG)
    mine = _chip_sum(pos, [comm.p32[w] for w in every], [comm.recv_b[w] for w in every])
    final_state, final_token = _split_start("pair_exchange_f32_start", _final_rider(mine))
    totals = _small_reduce(loss, small["w_pool"], dcw4, [small[n] for n in vec_names], after=(final_token,))
    loss_tot, small_out = _small_adam(*totals, *args)
    mine, theirs = _split_wait("pair_exchange_f32_wait", final_state, small_out[0][0])
    big_out = _adam_big(pos, mine, theirs, [w_in, w_out, w_up, w_down], [m_w_in, m_w_out, m_w_up, m_w_down],
                        [v_w_in, v_w_out, v_w_up, v_w_down])

    names = ("w_in", "w_pool", "pool_scale", "w_out", "ln1_g", "ln1_b", "w_up", "conv_w", "conv_b", "w_down",
             "ln2_g", "ln2_b")
    small_names = ("w_pool", "conv_w") + vec_names
    result = [loss_tot.reshape(()), grad_x[None]]
    for kind in range(4):
        for n in names:
            if n in order:
                result.append(big_out[kind][order.index(n)])
            else:
                val = small_out[kind][small_names.index(n)]
                if n == "conv_w":
                    val = val[0:3, 0:DOWN_SH][None]
                elif n == "w_pool":
                    val = val[None]
                result.append(val)
    return tuple(result)
```

```python
import functools
import math

import numpy as np
import jax
import jax.numpy as jnp
from jax import lax
from jax.experimental import pallas as pl
from jax.experimental.pallas import tpu as pltpu

F32 = jnp.float32
BF16 = jnp.bfloat16

D_MODEL = 1024
HEADS = 4
HEAD_DIM = 128
RET_W = HEADS * HEAD_DIM
POOL_WINDOWS = (2, 4, 8, 16)
POOL_W = 512
IN_W = 4 * RET_W + POOL_W
D_FF = 2816
N_SHARD = 4
IN_SH = IN_W // N_SHARD
UP_SH = 2 * D_FF // N_SHARD
DOWN_SH = D_FF // N_SHARD
OUT_SH = D_MODEL // N_SHARD
ROPE_BASE = 10000.0
LN_EPS = 1e-5
RMS_EPS = 1e-6
ALPHA = 2.0 ** 0.25
K_SCALE = HEAD_DIM ** -0.5
SUPER = 256
CHUNK = 64
POOL_HALO = 16
CONV_HALO = 8
FFN_STRIP = 128
LN_ROWS = 32

ADAM_LR = 0.001
ADAM_B1 = 0.9
ADAM_B2 = 0.999
ADAM_EPS = 1e-08
ADAM_WD = 0.01
ADAM_STEP = 10

MESH = pl.DeviceIdType.MESH
VMEM_LIMIT = 56 * 1024 * 1024


def _dot(a, b):
    return jnp.dot(a, b, preferred_element_type=F32)


def _dot_nt(a, b):
    return lax.dot_general(a, b, (((1,), (1,)), ((), ())), preferred_element_type=F32)


def _dot_tn(a, b):
    return lax.dot_general(a, b, (((0,), (0,)), ((), ())), preferred_element_type=F32)


def _sigmoid(x):
    return 1.0 / (1.0 + jnp.exp(-x))


def _params(sem):
    return pltpu.CompilerParams(dimension_semantics=sem, vmem_limit_bytes=VMEM_LIMIT)


def _whole():
    return pl.BlockSpec(memory_space=pltpu.VMEM)


HBM_SPEC = pl.BlockSpec(memory_space=pl.ANY)


class _Rider:
    def __init__(self, inplace, srcs, lands, n_copies, make):
        self.inplace, self.srcs, self.lands, self.n_copies, self.make = list(inplace), list(srcs), list(lands), n_copies, make


def _call(body, *, name, grid, in_specs, out_specs, out_shape, operands, scratch_shapes=(), sem=(),
          aliases=None, riders=(), after=()):
    n_in, n_out, n_scr = len(in_specs), len(out_shape), len(scratch_shapes)
    in_specs, out_specs, out_shape = list(in_specs), list(out_specs), list(out_shape)
    operands, scratch_shapes, aliases = list(operands), list(scratch_shapes), dict(aliases or {})
    in_specs += [_whole()] * len(after)
    operands += list(after)
    for r in riders:
        for a in r.inplace:
            aliases[len(in_specs)] = len(out_shape)
            in_specs.append(HBM_SPEC)
            operands.append(a)
            out_specs.append(HBM_SPEC)
            out_shape.append(jax.ShapeDtypeStruct(a.shape, a.dtype))
        for a in r.srcs:
            in_specs.append(HBM_SPEC)
            operands.append(a)
        for shp in r.lands:
            out_specs.append(HBM_SPEC)
            out_shape.append(shp)
        scratch_shapes += [pltpu.SemaphoreType.DMA((r.n_copies,)), pltpu.SemaphoreType.DMA((r.n_copies,))]

    def full(*refs):
        ins = refs[:n_in]
        at = n_in + len(after)
        r_srcs = []
        for r in riders:
            at += len(r.inplace)
            r_srcs.append(refs[at:at + len(r.srcs)])
            at += len(r.srcs)
        outs = refs[at:at + n_out]
        at += n_out
        r_outs = []
        for r in riders:
            r_outs.append((refs[at:at + len(r.inplace)], refs[at + len(r.inplace):at + len(r.inplace) + len(r.lands)]))
            at += len(r.inplace) + len(r.lands)
        scr = refs[at:at + n_scr]
        at += n_scr
        r_sems = [refs[at + 2 * i:at + 2 * i + 2] for i in range(len(riders))]

        def copies():
            return [r.make(r_outs[i][0], r_srcs[i], r_outs[i][1], r_sems[i][0], r_sems[i][1])
                    for i, r in enumerate(riders)]

        def start():
            for starts, _ in copies():
                for cp in starts:
                    cp.start()

        def finish():
            for _, waits in copies():
                for wait in waits:
                    wait()

        if riders and grid:
            first = functools.reduce(jnp.logical_and, [pl.program_id(d) == 0 for d in range(len(grid))])
            last = functools.reduce(jnp.logical_and, [pl.program_id(d) == grid[d] - 1 for d in range(len(grid))])
            pl.when(first)(start)
            body(*ins, *outs, *scr)
            pl.when(last)(finish)
        else:
            if riders:
                start()
            body(*ins, *outs, *scr)
            if riders:
                finish()

    params = _params(sem) if grid else pltpu.CompilerParams(vmem_limit_bytes=VMEM_LIMIT)
    res = pl.pallas_call(
        full, name=name, grid=grid, in_specs=in_specs, out_specs=out_specs, out_shape=out_shape,
        scratch_shapes=scratch_shapes, input_output_aliases=aliases, compiler_params=params,
    )(*operands)
    outs, at, rider_res = res[:n_out], n_out, []
    for r in riders:
        rider_res.append((res[at:at + len(r.inplace)], res[at + len(r.inplace):at + len(r.inplace) + len(r.lands)]))
        at += len(r.inplace) + len(r.lands)
    return list(outs), rider_res


def _gammas():
    return [1.0 - 2.0 ** (-5.0 - h) for h in range(HEADS)]


def _decay_tables():
    idx = np.arange(SUPER)
    dist = np.abs(idx[:, None] - idx[None, :]).astype(np.float64)
    visible = (idx[None, :] // CHUNK) <= (idx[:, None] // CHUNK)
    mask = np.stack([np.where(visible, g ** dist, 0.0) for g in _gammas()])
    qd = np.concatenate([np.repeat((g ** (idx + 1.0))[:, None], HEAD_DIM, 1) for g in _gammas()], 1)
    kd = np.concatenate([np.repeat((g ** (SUPER - 1.0 - idx))[:, None], HEAD_DIM, 1) for g in _gammas()], 1)
    return (jnp.asarray(mask, F32), jnp.asarray(qd, F32), jnp.asarray(kd, F32))


def _rope_tables(s):
    inv_freq = ROPE_BASE ** (-np.arange(0, HEAD_DIM, 2, dtype=np.float64) / HEAD_DIM)
    ang = np.arange(s, dtype=np.float64)[:, None] * inv_freq[None, :]
    cos, sin = np.cos(ang), np.sin(ang)
    return (jnp.asarray(np.concatenate([cos, cos], 1), F32),
            jnp.asarray(np.concatenate([-sin, sin], 1), F32))


def _rope(t, cosf, sinf):
    return t * cosf + pltpu.roll(t, HEAD_DIM // 2, 1) * sinf


def _rope_t(t, cosf, sinf):
    return t * cosf - pltpu.roll(t, HEAD_DIM // 2, 1) * sinf


def _layernorm_fwd(z):
    mu = jnp.mean(z, axis=-1, keepdims=True)
    zc = z - mu
    var = jnp.mean(zc * zc, axis=-1, keepdims=True)
    rstd = lax.rsqrt(var + LN_EPS)
    return zc * rstd, rstd


def _layernorm_bwd(dy, xhat, rstd, gain):
    dxh = dy * gain
    m1 = jnp.mean(dxh, axis=-1, keepdims=True)
    m2 = jnp.mean(dxh * xhat, axis=-1, keepdims=True)
    return rstd * (dxh - m1 - xhat * m2)


def _proj_pool(x, win4, cosf, sinf, wpool, pscale, ts, riders=(), after=()):
    s = x.shape[0]
    nt = s // ts

    def body(x_ref, w_ref, cos_ref, sin_ref, wp_ref, ps_ref,
             xb_ref, q_ref, k_ref, v_ref, g_ref, pooled_ref, cat_ref, proj_scr, pext_scr):
        i = pl.program_id(0)
        xb = x_ref[...].astype(BF16)
        xb_ref[...] = xb
        for j in range(N_SHARD):
            proj_scr[:, j * IN_SH:(j + 1) * IN_SH] = _dot(xb, w_ref[j])
        cosf_t = cos_ref[...]
        sinf_t = sin_ref[...]
        for h in range(HEADS):
            lo = h * HEAD_DIM
            q_ref[:, lo:lo + HEAD_DIM] = _rope(proj_scr[:, lo:lo + HEAD_DIM], cosf_t, sinf_t).astype(BF16)
            kk = _rope(proj_scr[:, RET_W + lo:RET_W + lo + HEAD_DIM], cosf_t, sinf_t) * K_SCALE
            k_ref[:, lo:lo + HEAD_DIM] = kk.astype(BF16)
        v_ref[...] = proj_scr[:, 2 * RET_W:3 * RET_W].astype(BF16)
        g_ref[...] = proj_scr[:, 3 * RET_W:4 * RET_W]

        @pl.when(i == 0)
        def _():
            pext_scr[0:POOL_HALO, :] = jnp.zeros((POOL_HALO, POOL_W), F32)

        pext_scr[POOL_HALO:POOL_HALO + ts, :] = proj_scr[:, 4 * RET_W:IN_W]
        pos = (i * ts + lax.broadcasted_iota(jnp.int32, (ts, 1), 0) + 1).astype(F32)
        for gi, w in enumerate(POOL_WINDOWS):
            lo = gi * HEAD_DIM
            ext = pext_scr[:, lo:lo + HEAD_DIM]
            acc = ext
            shift = 1
            while shift < w:
                acc = acc + pltpu.roll(acc, shift, 0)
                shift *= 2
            tok = ext[POOL_HALO:POOL_HALO + ts]
            pooled = acc[POOL_HALO:POOL_HALO + ts] / jnp.minimum(pos, float(w)) - tok
            pooled_b = pooled.astype(BF16)
            pooled_ref[:, lo:lo + HEAD_DIM] = pooled_b
            lin = _dot(pooled_b, wp_ref[gi])
            cat_ref[:, lo:lo + HEAD_DIM] = (lin * ps_ref[:, lo:lo + HEAD_DIM]).astype(BF16)
        pext_scr[0:POOL_HALO, :] = pext_scr[ts:ts + POOL_HALO, :]

    tile = lambda w: pl.BlockSpec((ts, w), lambda i: (i, 0))
    return _call(
        body, name="proj_pool", grid=(nt,),
        in_specs=[tile(D_MODEL), _whole(), tile(HEAD_DIM), tile(HEAD_DIM), _whole(), _whole()],
        out_specs=[tile(D_MODEL), tile(RET_W), tile(RET_W), tile(RET_W), tile(RET_W), tile(POOL_W),
                   pl.BlockSpec((ts, POOL_W), lambda i: (i, 1))],
        out_shape=[jax.ShapeDtypeStruct((s, D_MODEL), BF16), jax.ShapeDtypeStruct((s, RET_W), BF16),
                   jax.ShapeDtypeStruct((s, RET_W), BF16), jax.ShapeDtypeStruct((s, RET_W), BF16),
                   jax.ShapeDtypeStruct((s, RET_W), F32), jax.ShapeDtypeStruct((s, POOL_W), BF16),
                   jax.ShapeDtypeStruct((s, 2 * RET_W), BF16)],
        scratch_shapes=[pltpu.VMEM((ts, IN_W), F32), pltpu.VMEM((ts + POOL_HALO, POOL_W), F32)],
        sem=("arbitrary",), operands=(x, win4, cosf, sinf, wpool, pscale), riders=riders, after=after,
    )


def _retention_fwd(q, k, v, g, cat, mask, qd, kd, riders=(), after=()):
    s = q.shape[0]
    ns = s // SUPER
    cdec = [gm ** float(SUPER) for gm in _gammas()]

    def body(q_ref, k_ref, v_ref, g_ref, cat_in, mask_ref, qd_ref, kd_ref,
             ret_ref, cat_ref, st_ref, state_scr):
        del cat_in
        n = pl.program_id(0)

        @pl.when(n == 0)
        def _():
            state_scr[...] = jnp.zeros_like(state_scr)

        for h in range(HEADS):
            sl = slice(h * HEAD_DIM, (h + 1) * HEAD_DIM)
            qh, kh, vh = q_ref[:, sl], k_ref[:, sl], v_ref[:, sl]
            sc = _dot_nt(qh, kh) * mask_ref[h]
            st = state_scr[h]
            stb = st.astype(BF16)
            st_ref[0, h] = stb
            qdb = (qh.astype(F32) * qd_ref[:, sl]).astype(BF16)
            kdb = (kh.astype(F32) * kd_ref[:, sl]).astype(BF16)
            ret = _dot(sc.astype(BF16), vh) + _dot(qdb, stb)
            state_scr[h] = st * cdec[h] + _dot_tn(kdb, vh)
            ret_ref[:, sl] = ret
            r = lax.rsqrt(jnp.mean(ret * ret, axis=-1, keepdims=True) + RMS_EPS)
            gh = g_ref[:, sl]
            cat_ref[:, sl] = ((ret * r) * (gh * _sigmoid(gh))).astype(BF16)

    tile = pl.BlockSpec((SUPER, RET_W), lambda n: (n, 0))
    return _call(
        body, name="retention_fwd", grid=(ns,),
        in_specs=[tile, tile, tile, tile, HBM_SPEC, _whole(), _whole(), _whole()],
        out_specs=[tile, tile, pl.BlockSpec((1, HEADS, HEAD_DIM, HEAD_DIM), lambda n: (n, 0, 0, 0))],
        out_shape=[jax.ShapeDtypeStruct((s, RET_W), F32), jax.ShapeDtypeStruct((s, 2 * RET_W), BF16),
                   jax.ShapeDtypeStruct((ns, HEADS, HEAD_DIM, HEAD_DIM), BF16)],
        scratch_shapes=[pltpu.VMEM((HEADS, HEAD_DIM, HEAD_DIM), F32)],
        aliases={4: 1}, sem=("arbitrary",), operands=(q, k, v, g, cat, mask, qd, kd), riders=riders,
        after=after,
    )


def _outproj_ln1(x, cat, wout, g1, b1, ts, riders=(), after=()):
    s = x.shape[0]

    def body(x_ref, cat_ref, w_ref, g_ref, b_ref, xhat_ref, rstd_ref, h1b_ref):
        z = ALPHA * x_ref[...] + _dot(cat_ref[...], w_ref[...])
        xhat, rstd = _layernorm_fwd(z)
        xhat_ref[...] = xhat
        rstd_ref[...] = rstd
        h1b_ref[...] = (xhat * g_ref[...] + b_ref[...]).astype(BF16)

    tile = lambda w: pl.BlockSpec((ts, w), lambda i: (i, 0))
    return _call(
        body, name="outproj_ln1", grid=(s // ts,),
        in_specs=[tile(D_MODEL), tile(D_MODEL), _whole(), _whole(), _whole()],
        out_specs=[tile(D_MODEL), tile(1), tile(D_MODEL)],
        out_shape=[jax.ShapeDtypeStruct((s, D_MODEL), F32), jax.ShapeDtypeStruct((s, 1), F32),
                   jax.ShapeDtypeStruct((s, D_MODEL), BF16)],
        sem=("arbitrary",), operands=(x, cat, wout, g1, b1), riders=riders, after=after,
    )


def _ffn_fwd_loss(xhat1, h1b, target, wup4, wdown, cw, cb, g1, b1, g2, b2, ts):
    s = xhat1.shape[0]

    def body(xhat_ref, h1b_ref, tgt_ref, wup_ref, wdn_ref, cw_ref, cb_ref, g1_ref, b1_ref, g2_ref, b2_ref,
             ub_ref, dz2_ref, dz2b_ref, loss_ref, dg2_ref, db2_ref, val_scr, gext_scr, act_scr, ffn_scr):
        i = pl.program_id(0)

        @pl.when(i == 0)
        def _():
            gext_scr[0:CONV_HALO, :] = jnp.zeros((CONV_HALO, D_FF), F32)
            loss_ref[...] = jnp.zeros_like(loss_ref)
            dg2_ref[...] = jnp.zeros_like(dg2_ref)
            db2_ref[...] = jnp.zeros_like(db2_ref)

        hb = h1b_ref[...]
        for half in range(2):
            lo = half * UP_SH
            gext_scr[CONV_HALO:CONV_HALO + ts, lo:lo + UP_SH] = _dot(hb, wup_ref[2 + half])
            val_scr[:, lo:lo + UP_SH] = _dot(hb, wup_ref[half])
            for c0 in range(lo, lo + UP_SH, FFN_STRIP):
                cols = slice(c0, c0 + FFN_STRIP)
                ext = gext_scr[:, cols]
                gate = ext[CONV_HALO:]
                hc = cb_ref[:, cols] + ((pltpu.roll(ext, 2, 0)[CONV_HALO:] * cw_ref[0:1, cols]
                                         + pltpu.roll(ext, 1, 0)[CONV_HALO:] * cw_ref[1:2, cols])
                                        + gate * cw_ref[2:3, cols])
                val = val_scr[:, cols]
                act_scr[:, cols] = ((hc * _sigmoid(hc)) * val).astype(BF16)
                ub_ref[:, cols] = val.astype(BF16)
                ub_ref[:, D_FF + c0:D_FF + c0 + FFN_STRIP] = gate.astype(BF16)
            part = _dot(act_scr[:, lo:lo + UP_SH], wdn_ref[lo:lo + UP_SH, :])
            if half == 0:
                ffn_scr[...] = part
            else:
                ffn_scr[...] += part
        gext_scr[0:CONV_HALO, :] = gext_scr[ts:ts + CONV_HALO, :]

        loss_acc = jnp.zeros((1, 1), F32)
        dg2_acc = jnp.zeros((1, D_MODEL), F32)
        db2_acc = jnp.zeros((1, D_MODEL), F32)
        for r0 in range(0, ts, LN_ROWS):
            rows = slice(r0, r0 + LN_ROWS)
            h1 = xhat_ref[rows, :] * g1_ref[...] + b1_ref[...]
            xhat2, rstd2 = _layernorm_fwd(ALPHA * h1 + ffn_scr[rows, :])
            diff = (xhat2 * g2_ref[...] + b2_ref[...]) - tgt_ref[rows, :]
            row = jnp.mean(diff * diff, axis=-1, keepdims=True)
            loss_acc = loss_acc + 0.5 * jnp.sum(row, axis=0, keepdims=True)
            dy = diff * (1.0 / D_MODEL)
            dg2_acc = dg2_acc + jnp.sum(dy * xhat2, axis=0, keepdims=True)
            db2_acc = db2_acc + jnp.sum(dy, axis=0, keepdims=True)
            dz2 = _layernorm_bwd(dy, xhat2, rstd2, g2_ref[...])
            dz2_ref[rows, :] = dz2
            dz2b_ref[rows, :] = dz2.astype(BF16)
        loss_ref[...] += loss_acc
        dg2_ref[...] += dg2_acc
        db2_ref[...] += db2_acc

    tile = lambda w: pl.BlockSpec((ts, w), lambda i: (i, 0))
    acc = lambda w: pl.BlockSpec((1, w), lambda i: (0, 0))
    return pl.pallas_call(
        body, name="ffn_fwd_loss", grid=(s // ts,),
        in_specs=[tile(D_MODEL), tile(D_MODEL), tile(D_MODEL)] + [_whole()] * 8,
        out_specs=[tile(2 * D_FF), tile(D_MODEL), tile(D_MODEL), acc(1), acc(D_MODEL), acc(D_MODEL)],
        out_shape=[jax.ShapeDtypeStruct((s, 2 * D_FF), BF16), jax.ShapeDtypeStruct((s, D_MODEL), F32),
                   jax.ShapeDtypeStruct((s, D_MODEL), BF16),
                   jax.ShapeDtypeStruct((1, 1), F32), jax.ShapeDtypeStruct((1, D_MODEL), F32),
                   jax.ShapeDtypeStruct((1, D_MODEL), F32)],
        scratch_shapes=[pltpu.VMEM((ts, D_FF), F32), pltpu.VMEM((ts + CONV_HALO, D_FF), F32),
                        pltpu.VMEM((ts, D_FF), BF16), pltpu.VMEM((ts, D_MODEL), F32)],
        compiler_params=_params(("arbitrary",)),
    )(xhat1, h1b, target, wup4, wdown, cw, cb, g1, b1, g2, b2)


def _ffn_bwd(dz2, dz2b, ub, xhat1, rstd1, wup4, wdown, cw, cb, g1, ts):
    s = dz2.shape[0]
    nt = s // ts
    hb = 16

    def body(dz2_ref, dz2b_ref, ub_ref, prev_ref, xhat_ref, rstd_ref, wup_ref, wdn_ref, cw_ref, cb_ref, g1_ref,
             a_ref, dub_ref, dz1_ref, dz1b_ref, dg1_ref, db1_ref, dcw_ref, dcb_ref, gext_scr, dext_scr, da_scr):
        i = pl.program_id(0)
        r = nt - 1 - i

        @pl.when(i == 0)
        def _():
            dext_scr[ts:ts + CONV_HALO, :] = jnp.zeros((CONV_HALO, D_FF), F32)
            dg1_ref[...] = jnp.zeros_like(dg1_ref)
            db1_ref[...] = jnp.zeros_like(db1_ref)
            dcw_ref[...] = jnp.zeros_like(dcw_ref)
            dcb_ref[...] = jnp.zeros_like(dcb_ref)

        da_scr[...] = _dot_nt(dz2b_ref[...], wdn_ref[...])
        prev = prev_ref[...].astype(F32)[hb - CONV_HALO:hb]
        gext_scr[0:CONV_HALO, :] = jnp.where(r == 0, 0.0, prev)
        n_ext = ts + CONV_HALO
        for c0 in range(0, D_FF, FFN_STRIP):
            cols = slice(c0, c0 + FFN_STRIP)
            gcols = slice(D_FF + c0, D_FF + c0 + FFN_STRIP)
            val = ub_ref[:, cols].astype(F32)
            gate = ub_ref[:, gcols].astype(F32)
            gext_scr[CONV_HALO:n_ext, cols] = gate
            ext = gext_scr[:, cols]
            g2s = pltpu.roll(ext, 2, 0)[CONV_HALO:]
            g1s = pltpu.roll(ext, 1, 0)[CONV_HALO:]
            hc = cb_ref[:, cols] + ((g2s * cw_ref[0:1, cols] + g1s * cw_ref[1:2, cols]) + gate * cw_ref[2:3, cols])
            sg = _sigmoid(hc)
            si = hc * sg
            a_ref[:, cols] = (si * val).astype(BF16)
            da = da_scr[:, cols]
            dhc = da * val * (sg * (1.0 + hc * (1.0 - sg)))
            dcb_ref[:, cols] += jnp.sum(dhc, axis=0, keepdims=True)
            dcw_ref[0:1, cols] += jnp.sum(dhc * g2s, axis=0, keepdims=True)
            dcw_ref[1:2, cols] += jnp.sum(dhc * g1s, axis=0, keepdims=True)
            dcw_ref[2:3, cols] += jnp.sum(dhc * gate, axis=0, keepdims=True)
            dext_scr[0:ts, cols] = dhc
            dext = dext_scr[:, cols]
            dgate = (dhc * cw_ref[2:3, cols] + pltpu.roll(dext, n_ext - 1, 0)[0:ts] * cw_ref[1:2, cols]
                     + pltpu.roll(dext, n_ext - 2, 0)[0:ts] * cw_ref[0:1, cols])
            dub_ref[:, cols] = (da * si).astype(BF16)
            dub_ref[:, gcols] = dgate.astype(BF16)
        dext_scr[ts:n_ext, :] = dext_scr[0:CONV_HALO, :]
        dh1 = ALPHA * dz2_ref[...]
        for j in range(N_SHARD):
            dh1 = dh1 + _dot_nt(dub_ref[:, j * UP_SH:(j + 1) * UP_SH], wup_ref[j])
        xhat = xhat_ref[...]
        dg1_ref[...] += jnp.sum(dh1 * xhat, axis=0, keepdims=True)
        db1_ref[...] += jnp.sum(dh1, axis=0, keepdims=True)
        dz1 = _layernorm_bwd(dh1, xhat, rstd_ref[...], g1_ref[...])
        dz1_ref[...] = dz1
        dz1b_ref[...] = dz1.astype(BF16)

    tile = lambda w: pl.BlockSpec((ts, w), lambda i: (nt - 1 - i, 0))
    acc = lambda rws, w: pl.BlockSpec((rws, w), lambda i: (0, 0))
    prev_spec = pl.BlockSpec((hb, D_FF), lambda i: (jnp.maximum((nt - 1 - i) * (ts // hb) - 1, 0), 1))
    return pl.pallas_call(
        body, name="ffn_bwd", grid=(nt,),
        in_specs=[tile(D_MODEL), tile(D_MODEL), tile(2 * D_FF), prev_spec, tile(D_MODEL), tile(1)] + [_whole()] * 5,
        out_specs=[tile(D_FF), tile(2 * D_FF), tile(D_MODEL), tile(D_MODEL), acc(1, D_MODEL), acc(1, D_MODEL),
                   acc(3, D_FF), acc(1, D_FF)],
        out_shape=[jax.ShapeDtypeStruct((s, D_FF), BF16), jax.ShapeDtypeStruct((s, 2 * D_FF), BF16),
                   jax.ShapeDtypeStruct((s, D_MODEL), F32), jax.ShapeDtypeStruct((s, D_MODEL), BF16),
                   jax.ShapeDtypeStruct((1, D_MODEL), F32),
                   jax.ShapeDtypeStruct((1, D_MODEL), F32), jax.ShapeDtypeStruct((3, D_FF), F32),
                   jax.ShapeDtypeStruct((1, D_FF), F32)],
        scratch_shapes=[pltpu.VMEM((ts + CONV_HALO, D_FF), F32), pltpu.VMEM((ts + CONV_HALO, D_FF), F32),
                        pltpu.VMEM((ts, D_FF), F32)],
        compiler_params=_params(("arbitrary",)),
    )(dz2, dz2b, ub, ub, xhat1, rstd1, wup4, wdown, cw, cb, g1)


def _mix_bwd(dz1, pooled, ret, g, wout, wpool, pscale, ts, riders=(), after=()):
    s = dz1.shape[0]
    nt = s // ts

    def body(dz1_ref, pooled_ref, ret_ref, g_ref, wout_ref, wp_ref, ps_ref,
             dret_ref, dgp_ref, dwp_ref, dps_ref, eext_scr):
        i = pl.program_id(0)
        r = nt - 1 - i

        @pl.when(i == 0)
        def _():
            eext_scr[ts:ts + POOL_HALO, :] = jnp.zeros((POOL_HALO, POOL_W), F32)
            dwp_ref[...] = jnp.zeros_like(dwp_ref)
            dps_ref[...] = jnp.zeros_like(dps_ref)

        dzb = dz1_ref[...].astype(BF16)
        dcat_r = _dot_nt(dzb, wout_ref[0:RET_W, :])
        dcat_p = _dot_nt(dzb, wout_ref[RET_W:2 * RET_W, :])
        pos = (r * ts + lax.broadcasted_iota(jnp.int32, (ts, 1), 0) + 1).astype(F32)
        dpooled = []
        for gi, w in enumerate(POOL_WINDOWS):
            sl = slice(gi * HEAD_DIM, (gi + 1) * HEAD_DIM)
            pb = pooled_ref[:, sl]
            dy = dcat_p[:, sl]
            dps_ref[:, sl] += jnp.sum(dy * _dot(pb, wp_ref[gi]), axis=0, keepdims=True)
            dlin = (dy * ps_ref[:, sl]).astype(BF16)
            dwp_ref[gi] += _dot_tn(pb, dlin)
            dpg = _dot_nt(dlin, wp_ref[gi])
            dpooled.append(dpg)
            eext_scr[0:ts, sl] = dpg / jnp.minimum(pos, float(w))
        for gi, w in enumerate(POOL_WINDOWS):
            sl = slice(gi * HEAD_DIM, (gi + 1) * HEAD_DIM)
            acc = eext_scr[:, sl]
            shift = 1
            while shift < w:
                acc = acc + pltpu.roll(acc, ts + POOL_HALO - shift, 0)
                shift *= 2
            dgp_ref[:, RET_W + gi * HEAD_DIM:RET_W + (gi + 1) * HEAD_DIM] = (acc[0:ts] - dpooled[gi]).astype(BF16)
        eext_scr[ts:ts + POOL_HALO, :] = eext_scr[0:POOL_HALO, :]
        for h in range(HEADS):
            sl = slice(h * HEAD_DIM, (h + 1) * HEAD_DIM)
            rt = ret_ref[:, sl]
            rr = lax.rsqrt(jnp.mean(rt * rt, axis=-1, keepdims=True) + RMS_EPS)
            rn = rt * rr
            gh = g_ref[:, sl]
            sg = _sigmoid(gh)
            dy = dcat_r[:, sl]
            dgp_ref[:, sl] = (dy * rn * (sg * (1.0 + gh * (1.0 - sg)))).astype(BF16)
            drn = dy * (gh * sg)
            dret_ref[:, sl] = (rr * (drn - rn * jnp.mean(drn * rn, axis=-1, keepdims=True))).astype(BF16)

    tile = lambda w: pl.BlockSpec((ts, w), lambda i: (nt - 1 - i, 0))
    return _call(
        body, name="mix_bwd", grid=(nt,),
        in_specs=[tile(D_MODEL), tile(POOL_W), tile(RET_W), tile(RET_W), _whole(), _whole(), _whole()],
        out_specs=[tile(RET_W), tile(2 * RET_W),
                   pl.BlockSpec((len(POOL_WINDOWS), HEAD_DIM, HEAD_DIM), lambda i: (0, 0, 0)),
                   pl.BlockSpec((1, POOL_W), lambda i: (0, 0))],
        out_shape=[jax.ShapeDtypeStruct((s, RET_W), BF16), jax.ShapeDtypeStruct((s, 2 * RET_W), BF16),
                   jax.ShapeDtypeStruct((len(POOL_WINDOWS), HEAD_DIM, HEAD_DIM), F32),
                   jax.ShapeDtypeStruct((1, POOL_W), F32)],
        scratch_shapes=[pltpu.VMEM((ts + POOL_HALO, POOL_W), F32)],
        sem=("arbitrary",), operands=(dz1, pooled, ret, g, wout, wpool, pscale), riders=riders,
        after=after,
    )


def _retention_bwd(q, k, v, dret, dgp, states, mask, qd, kd, cosf, sinf, riders=(), after=()):
    s = q.shape[0]
    ns = s // SUPER
    cdec = [gm ** float(SUPER) for gm in _gammas()]

    def body(q_ref, k_ref, v_ref, do_ref, dgp_ref, st_ref, mask_ref, qd_ref, kd_ref, cos_ref, sin_ref,
             dproj_ref, dstate_scr):
        i = pl.program_id(0)

        @pl.when(i == 0)
        def _():
            dstate_scr[...] = jnp.zeros_like(dstate_scr)

        cosf_t = cos_ref[...]
        sinf_t = sin_ref[...]
        for h in range(HEADS):
            sl = slice(h * HEAD_DIM, (h + 1) * HEAD_DIM)
            qh, kh, vh, doh = q_ref[:, sl], k_ref[:, sl], v_ref[:, sl], do_ref[:, sl]
            m = mask_ref[h]
            scb = (_dot_nt(qh, kh) * m).astype(BF16)
            dscb = (_dot_nt(doh, vh) * m).astype(BF16)
            stb = st_ref[0, h]
            dst = dstate_scr[h]
            dstb = dst.astype(BF16)
            qdb = (qh.astype(F32) * qd_ref[:, sl]).astype(BF16)
            kdb = (kh.astype(F32) * kd_ref[:, sl]).astype(BF16)
            dq = _dot(dscb, kh) + _dot_nt(doh, stb) * qd_ref[:, sl]
            dk = _dot_tn(dscb, qh) + _dot_nt(vh, dstb) * kd_ref[:, sl]
            dv = _dot_tn(scb, doh) + _dot(kdb, dstb)
            dstate_scr[h] = dst * cdec[h] + _dot_tn(qdb, doh)
            lo = h * HEAD_DIM
            dproj_ref[:, lo:lo + HEAD_DIM] = _rope_t(dq, cosf_t, sinf_t).astype(BF16)
            dproj_ref[:, RET_W + lo:RET_W + lo + HEAD_DIM] = _rope_t(dk * K_SCALE, cosf_t, sinf_t).astype(BF16)
            dproj_ref[:, 2 * RET_W + lo:2 * RET_W + lo + HEAD_DIM] = dv.astype(BF16)
        dproj_ref[:, 3 * RET_W:IN_W] = dgp_ref[...]

    tile = lambda w: pl.BlockSpec((SUPER, w), lambda i: (ns - 1 - i, 0))
    return _call(
        body, name="retention_bwd", grid=(ns,),
        in_specs=[tile(RET_W), tile(RET_W), tile(RET_W), tile(RET_W), tile(2 * RET_W),
                  pl.BlockSpec((1, HEADS, HEAD_DIM, HEAD_DIM), lambda i: (ns - 1 - i, 0, 0, 0)),
                  _whole(), _whole(), _whole(), tile(HEAD_DIM), tile(HEAD_DIM)],
        out_specs=[tile(IN_W)],
        out_shape=[jax.ShapeDtypeStruct((s, IN_W), BF16)],
        scratch_shapes=[pltpu.VMEM((HEADS, HEAD_DIM, HEAD_DIM), F32)],
        sem=("arbitrary",), operands=(q, k, v, dret, dgp, states, mask, qd, kd, cosf, sinf), riders=riders,
        after=after,
    )


def _dx(dz1, dproj, win4, ts, riders=(), after=()):
    s = dz1.shape[0]

    def body(dz1_ref, dp_ref, w_ref, dx_ref):
        acc = ALPHA * dz1_ref[...]
        for j in range(N_SHARD):
            acc = acc + _dot_nt(dp_ref[:, j * IN_SH:(j + 1) * IN_SH], w_ref[j])
        dx_ref[...] = acc

    tile = lambda w: pl.BlockSpec((ts, w), lambda i: (i, 0))
    return _call(
        body, name="dx", grid=(s // ts,),
        in_specs=[tile(D_MODEL), tile(IN_W), _whole()],
        out_specs=[tile(D_MODEL)],
        out_shape=[jax.ShapeDtypeStruct((s, D_MODEL), F32)],
        sem=("arbitrary",), operands=(dz1, dproj, win4), riders=riders, after=after,
    )


def _wgrad(a, b, tm, tn, name, stacked, m_outer, riders=(), after=()):
    s, m = a.shape
    n = b.shape[1]

    def body(a_ref, b_ref, o32_ref, o16_ref):
        res = _dot_tn(a_ref[...], b_ref[...])
        o32_ref[...] = res.reshape(o32_ref.shape)
        o16_ref[...] = res.astype(BF16).reshape(o16_ref.shape)

    if m_outer:
        grid, blocks = (m // tm, n // tn), (lambda g0, g1: (g0, g1))
    else:
        grid, blocks = (n // tn, m // tm), (lambda g0, g1: (g1, g0))
    if stacked:
        shape = (n // tn, m, tn)
        ospec = pl.BlockSpec((1, tm, tn), lambda g0, g1: (blocks(g0, g1)[1], blocks(g0, g1)[0], 0))
    else:
        shape = (m, n)
        ospec = pl.BlockSpec((tm, tn), lambda g0, g1: blocks(g0, g1))
    return _call(
        body, name=name, grid=grid,
        in_specs=[pl.BlockSpec((s, tm), lambda g0, g1: (0, blocks(g0, g1)[0])),
                  pl.BlockSpec((s, tn), lambda g0, g1: (0, blocks(g0, g1)[1]))],
        out_specs=[ospec, ospec],
        out_shape=[jax.ShapeDtypeStruct(shape, F32), jax.ShapeDtypeStruct(shape, BF16)],
        sem=("arbitrary", "arbitrary"), operands=(a, b), riders=riders, after=after,
    )


class _NoComm:
    def __init__(self, win4, wout, wup4, wdown):
        self.weights = dict(w_in=win4, w_out=wout, w_up=wup4, w_down=wdown)
        self.grads = {}

    def weight(self, name):
        return self.weights[name]

    def riders(self, call):
        return ()

    def after(self, call):
        return ()

    def landed(self, call, results, outs):
        pass

    def small_gradients(self, loss, small):
        pass

    def gradient(self, name, g32, g16):
        self.grads[name] = (g32, g16)


def _local_step(x, target, cw, cb, wpool, pscale, g1, b1, g2, b2, comm):
    s = x.shape[0]
    ts_a = min(512, s)
    ts_f = min(256, s)
    mask, qd, kd = _decay_tables()
    cosf, sinf = _rope_tables(s)
    wpool_b = wpool.astype(BF16)

    def run(call, fn, *args):
        outs, res = fn(*args, riders=comm.riders(call), after=comm.after(call))
        comm.landed(call, res, outs)
        return outs

    xb, q, k, v, g, pooled, cat = run("proj_pool", _proj_pool, x, comm.weight("w_in"), cosf, sinf, wpool_b,
                                      pscale, ts_a)
    ret, cat, states = run("retention_fwd", _retention_fwd, q, k, v, g, cat, mask, qd, kd)
    wout = comm.weight("w_out")
    xhat1, rstd1, h1b = run("outproj_ln1", _outproj_ln1, x, cat, wout, g1, b1, ts_a)
    wup4, wdown = comm.weight("w_up"), comm.weight("w_down")
    ub, dz2, dz2b, loss, dg2, db2 = _ffn_fwd_loss(xhat1, h1b, target, wup4, wdown, cw, cb, g1, b1, g2, b2, ts_f)

    act, dub, dz1, dz1b, dg1, db1, dcw, dcb = _ffn_bwd(dz2, dz2b, ub, xhat1, rstd1, wup4, wdown, cw, cb, g1, ts_f)
    half = D_MODEL // 2
    comm.gradient("w_up", *run("wgrad_up", _wgrad, h1b, dub, half, UP_SH, "wgrad_up", True, False))
    comm.gradient("w_out", *run("wgrad_out", _wgrad, cat, dz1b, D_MODEL, half, "wgrad_out", False, True))
    comm.gradient("w_down", *run("wgrad_down", _wgrad, act, dz2b, D_FF // 2, half, "wgrad_down", False, True))
    dret, dgp, dwp, dps = run("mix_bwd", _mix_bwd, dz1b, pooled, ret, g, wout, wpool_b, pscale, ts_a)
    small = dict(w_pool=dwp, pool_scale=dps, ln1_g=dg1, ln1_b=db1, conv_w=dcw, conv_b=dcb,
                 ln2_g=dg2, ln2_b=db2)
    comm.small_gradients(loss, small)
    dproj, = run("retention_bwd", _retention_bwd, q, k, v, dret, dgp, states, mask, qd, kd, cosf, sinf)
    comm.gradient("w_in", *run("wgrad_in", _wgrad, xb, dproj, D_MODEL, IN_SH, "wgrad_in", True, True))
    (grad_x,), _ = _dx(dz1, dproj, comm.weight("w_in"), ts_a, after=comm.after("dx"))
    return loss, grad_x, small


CAST_ROWS = 64
SHARD_SHAPES = ((D_MODEL, IN_SH), (OUT_SH, D_MODEL), (D_MODEL, UP_SH), (DOWN_SH, D_MODEL))
N_BIG = len(SHARD_SHAPES)
CW_PAD = (8, 768)


def _mesh_pos():
    return lax.axis_index("x"), lax.axis_index("y"), lax.axis_index("c")


def _other_chips(x, y):
    return [(1 - x, y), (x, 1 - y), (1 - x, 1 - y)]


def _half_rows(w, which):
    hr = SHARD_SHAPES[w][0] // 2
    return pl.ds(pl.multiple_of(which * hr, 16), hr)


def _gather_weights(shards, cw8, full):
    def body(*refs):
        in_refs = refs[:N_BIG]
        cw_ref = refs[N_BIG]
        out_refs = refs[N_BIG + 1:2 * N_BIG + 1]
        cwo_ref = refs[2 * N_BIG + 1]
        stage = refs[2 * N_BIG + 2:3 * N_BIG + 2]
        send_sems, recv_sems, fsend_sems, frecv_sems, cw_send, cw_recv, local_sems = refs[3 * N_BIG + 2:]
        x, y, c = _mesh_pos()
        j0 = 2 * x + y
        chips = _other_chips(x, y)

        def cast_to_stage(w):
            def cast(i, carry):
                rows = pl.ds(pl.multiple_of(i * CAST_ROWS, CAST_ROWS), CAST_ROWS)
                stage[w][rows, :] = in_refs[w][rows, :].astype(BF16)
                return carry
            lax.fori_loop(0, SHARD_SHAPES[w][0] // CAST_ROWS, cast, 0)

        for w in full:
            cast_to_stage(w)

        jx, jy, jd = 2 * (1 - x) + y, 2 * x + (1 - y), 2 * (1 - x) + (1 - y)
        neighbours = [((1 - x, y, c), jx), ((x, 1 - y, c), jy)]
        passed = jnp.where(c == 0, jx, jy)
        pass_to = (jnp.where(c == 0, x, 1 - x), jnp.where(c == 0, 1 - y, y), c)

        def nbr(w, k, block):
            return pltpu.make_async_remote_copy(
                src_ref=stage[w].at[_half_rows(w, c), :], dst_ref=out_refs[w].at[block, _half_rows(w, c), :],
                send_sem=send_sems.at[w, k], recv_sem=recv_sems.at[w, k],
                device_id=neighbours[k][0], device_id_type=MESH)

        def relay(w, block):
            return pltpu.make_async_remote_copy(
                src_ref=out_refs[w].at[passed, _half_rows(w, c), :],
                dst_ref=out_refs[w].at[block, _half_rows(w, c), :],
                send_sem=send_sems.at[w, 2], recv_sem=recv_sems.at[w, 2],
                device_id=pass_to, device_id_type=MESH)

        def d2d(w, k, block, half):
            return pltpu.make_async_remote_copy(
                src_ref=out_refs[w].at[block, _half_rows(w, half), :],
                dst_ref=out_refs[w].at[block, _half_rows(w, half), :],
                send_sem=fsend_sems.at[w, k], recv_sem=frecv_sems.at[w, k],
                device_id=(x, y, 1 - c), device_id_type=MESH)

        def conv(k, block):
            chip = chips[k]
            return pltpu.make_async_remote_copy(
                src_ref=cw_ref, dst_ref=cwo_ref.at[block], send_sem=cw_send.at[k], recv_sem=cw_recv.at[k],
                device_id=(chip[0], chip[1], c), device_id_type=MESH)

        sent = [nbr(w, k, j0) for w in full for k in range(2)] + [conv(k, j0) for k in range(3)]
        for cp in sent:
            cp.start()
        for w in range(N_BIG):
            if w not in full:
                cast_to_stage(w)
        local = [pltpu.make_async_copy(stage[w], out_refs[w].at[j0], local_sems.at[w]) for w in range(N_BIG)]
        local.append(pltpu.make_async_copy(cw_ref, cwo_ref.at[j0], local_sems.at[N_BIG]))
        for cp in local:
            cp.start()
        for w in full:
            for k, (_, block) in enumerate(neighbours):
                nbr(w, k, block).wait_recv()
            later = [relay(w, passed)] + [d2d(w, k, block, c) for k, (_, block) in enumerate(neighbours)]
            for cp in later:
                cp.start()
            sent += later
        for w in full:
            relay(w, jd).wait_recv()
            fw = d2d(w, 2, jd, c)
            fw.start()
            sent.append(fw)
        for w in full:
            for k, block in enumerate([jx, jy, jd]):
                d2d(w, k, block, 1 - c).wait_recv()
        for k, chip in enumerate(chips):
            conv(k, 2 * chip[0] + chip[1]).wait_recv()
        for cp in sent:
            cp.wait_send()
        for cp in local:
            cp.wait()

    out_shape = [jax.ShapeDtypeStruct((N_SHARD,) + shp, BF16) for shp in SHARD_SHAPES]
    out_shape.append(jax.ShapeDtypeStruct((N_SHARD,) + CW_PAD, F32))
    return pl.pallas_call(
        body, name="gather_weights",
        in_specs=[_whole()] * (N_BIG + 1),
        out_specs=[HBM_SPEC] * (N_BIG + 1),
        out_shape=out_shape,
        scratch_shapes=[pltpu.VMEM(shp, BF16) for shp in SHARD_SHAPES] + [
            pltpu.SemaphoreType.DMA((N_BIG, 3)), pltpu.SemaphoreType.DMA((N_BIG, 3)),
            pltpu.SemaphoreType.DMA((N_BIG, 3)), pltpu.SemaphoreType.DMA((N_BIG, 3)),
            pltpu.SemaphoreType.DMA((3,)), pltpu.SemaphoreType.DMA((3,)),
            pltpu.SemaphoreType.DMA((N_BIG + 1,))],
        compiler_params=pltpu.CompilerParams(vmem_limit_bytes=VMEM_LIMIT),
    )(*shards, cw8)


def _gather_rider(arrays, ops):
    ws = sorted(arrays)

    def make(inplace, srcs, lands, send_sems, recv_sems):
        del srcs, lands
        x, y, c = _mesh_pos()
        j0, jx, jy, jd = 2 * x + y, 2 * (1 - x) + y, 2 * x + (1 - y), 2 * (1 - x) + (1 - y)
        x_nbr, y_nbr, sibling = (1 - x, y, c), (x, 1 - y, c), (x, y, 1 - c)
        starts, waits = [], []
        for n, (kind, w, (r0, nr)) in enumerate(ops):
            ref = inplace[ws.index(w)]
            hr = SHARD_SHAPES[w][0] // 2
            rows = lambda core: pl.ds(pl.multiple_of(core * hr + r0, 16), nr)
            mine, theirs = rows(c), rows(1 - c)
            if kind == "ici":
                moves = [(ref.at[j0, mine, :], x_nbr, ref.at[jx, mine, :]),
                         (ref.at[j0, mine, :], y_nbr, ref.at[jy, mine, :]),
                         (ref.at[j0, mine, :], (1 - x, 1 - y, c), ref.at[jd, mine, :])]
            elif kind == "nbr":
                moves = [(ref.at[j0, mine, :], x_nbr, ref.at[jx, mine, :]),
                         (ref.at[j0, mine, :], y_nbr, ref.at[jy, mine, :])]
            elif kind == "relay":
                passed = jnp.where(c == 0, jx, jy)
                to = (jnp.where(c == 0, x, 1 - x), jnp.where(c == 0, 1 - y, y), c)
                moves = [(ref.at[passed, mine, :], to, ref.at[jd, mine, :])]
            else:
                blocks = dict(d2d=[jx, jy, jd], d2d_nbr=[jx, jy], d2d_diag=[jd])[kind]
                moves = [(ref.at[b, mine, :], sibling, ref.at[b, theirs, :]) for b in blocks]
            for k, (src, to, landing) in enumerate(moves):
                sems = dict(send_sem=send_sems.at[3 * n + k], recv_sem=recv_sems.at[3 * n + k],
                            device_id=to, device_id_type=MESH)
                send = pltpu.make_async_remote_copy(src_ref=src, dst_ref=src, **sems)
                arrival = pltpu.make_async_remote_copy(src_ref=src, dst_ref=landing, **sems)
                starts.append(send)
                waits += [arrival.wait_recv, send.wait_send]
        return starts, waits

    return _Rider([arrays[w] for w in ws], [], [], 3 * len(ops), make)


def _whole_half(w):
    return (0, SHARD_SHAPES[w][0] // 2)


def _pair_rider(ws, g16s):
    def make(inplace, srcs, lands, send_sems, recv_sems):
        del inplace
        x, y, c = _mesh_pos()
        copies = [pltpu.make_async_remote_copy(
            src_ref=srcs[i].at[:, _half_rows(w, 1 - c), :], dst_ref=lands[i],
            send_sem=send_sems.at[i], recv_sem=recv_sems.at[i], device_id=(x, y, 1 - c), device_id_type=MESH)
            for i, w in enumerate(ws)]
        return copies, [cp.wait for cp in copies]

    lands = [jax.ShapeDtypeStruct((N_SHARD, SHARD_SHAPES[w][0] // 2, SHARD_SHAPES[w][1]), BF16) for w in ws]
    return _Rider([], g16s, lands, len(ws), make)


def _chip_rider(ws, p16s, rows=None, landing=None):
    def make(inplace, srcs, lands, send_sems, recv_sems):
        x, y, c = _mesh_pos()
        dsts = inplace if landing is not None else lands
        copies = []
        for i, w in enumerate(ws):
            r0, nr = rows if rows is not None else _whole_half(w)
            for k, chip in enumerate(_other_chips(x, y)):
                copies.append(pltpu.make_async_remote_copy(
                    src_ref=srcs[i].at[2 * chip[0] + chip[1], pl.ds(r0, nr), :],
                    dst_ref=dsts[i].at[k, pl.ds(r0, nr), :],
                    send_sem=send_sems.at[3 * i + k], recv_sem=recv_sems.at[3 * i + k],
                    device_id=(chip[0], chip[1], c), device_id_type=MESH))
        return copies, [cp.wait for cp in copies]

    lands = [jax.ShapeDtypeStruct((3, SHARD_SHAPES[w][0] // 2, SHARD_SHAPES[w][1]), BF16) for w in ws]
    if landing is not None:
        return _Rider(landing, p16s, [], 3 * len(ws), make)
    return _Rider([], p16s, lands, 3 * len(ws), make)


def _final_rider(halves):
    def make(inplace, srcs, lands, send_sems, recv_sems):
        del inplace
        x, y, c = _mesh_pos()
        copies = [pltpu.make_async_remote_copy(
            src_ref=srcs[i], dst_ref=lands[i], send_sem=send_sems.at[i], recv_sem=recv_sems.at[i],
            device_id=(x, y, 1 - c), device_id_type=MESH) for i in range(len(halves))]
        return copies, [cp.wait for cp in copies]

    return _Rider([], halves, [jax.ShapeDtypeStruct(h.shape, h.dtype) for h in halves], len(halves), make)


def _comm_only(name, riders):
    _, res = _call(lambda: None, name=name, grid=(), in_specs=[], out_specs=[], out_shape=[], operands=(),
                   riders=riders)
    return res


class _SemList:
    def __init__(self, refs):
        self.at = list(refs)


def _split_start(name, rider):
    assert not rider.inplace
    ns, nl, n = len(rider.srcs), len(rider.lands), rider.n_copies

    def body(*refs):
        srcs, lands = refs[:ns], refs[ns:ns + nl]
        sems = refs[ns + nl:ns + nl + 2 * n]
        token = refs[-1]
        starts, _ = rider.make([], srcs, lands, _SemList(sems[:n]), _SemList(sems[n:]))
        for cp in starts:
            cp.start()
        token[...] = jnp.zeros_like(token)

    buffers = [pltpu.with_memory_space_constraint(a, pltpu.HBM) for a in rider.srcs]
    buffers += [pltpu.with_memory_space_constraint(lax.empty(s.shape, s.dtype), pltpu.HBM) for s in rider.lands]
    hbm = pl.BlockSpec(memory_space=pltpu.HBM)
    sem = pl.BlockSpec(memory_space=pltpu.SEMAPHORE)
    outs = pl.pallas_call(
        body, name=name,
        out_shape=tuple([pltpu.SemaphoreType.DMA(())] * (2 * n) + [pltpu.HBM(b.shape, b.dtype) for b in buffers]
                        + [jax.ShapeDtypeStruct((8, 128), F32)]),
        in_specs=[hbm] * (ns + nl),
        out_specs=tuple([sem] * (2 * n) + [hbm] * (ns + nl) + [_whole()]),
        input_output_aliases={i: 2 * n + i for i in range(ns + nl)},
        compiler_params=pltpu.CompilerParams(has_side_effects=pltpu.SideEffectType.DATAFLOW_SIDE_EFFECTING),
    )(*buffers)
    return (rider, outs[:2 * n], outs[2 * n:2 * n + ns + nl]), outs[-1]


def _split_wait(name, state, after):
    rider, sems, buffers = state
    ns, nl, n = len(rider.srcs), len(rider.lands), rider.n_copies

    def body(*refs):
        srcs, lands = refs[:ns], refs[ns:ns + nl]
        sem_refs = refs[ns + nl:ns + nl + 2 * n]
        _, waits = rider.make([], srcs, lands, _SemList(sem_refs[:n]), _SemList(sem_refs[n:]))
        for wait in waits:
            wait()

    hbm = pl.BlockSpec(memory_space=pltpu.HBM)
    sem = pl.BlockSpec(memory_space=pltpu.SEMAPHORE)
    outs = pl.pallas_call(
        body, name=name,
        out_shape=tuple(pltpu.HBM(b.shape, b.dtype) for b in buffers),
        in_specs=[hbm] * (ns + nl) + [sem] * (2 * n) + [HBM_SPEC],
        out_specs=tuple([hbm] * (ns + nl)),
        input_output_aliases={i: i for i in range(ns + nl)},
        compiler_params=pltpu.CompilerParams(has_side_effects=pltpu.SideEffectType.DATAFLOW_SIDE_EFFECTING),
    )(*buffers, *sems, after)
    return list(outs[:ns]), list(outs[ns:])


def _pair_sum(pos, ws, g32s, recvs):
    n = len(ws)

    def body(pos_ref, *refs):
        del pos_ref
        g_refs, r_refs = refs[:n], refs[n:2 * n]
        p32_refs, p16_refs = refs[2 * n:3 * n], refs[3 * n:]
        for i in range(n):
            tot = g_refs[i][...] + r_refs[i][...].astype(F32)
            p32_refs[i][...] = tot
            p16_refs[i][...] = tot.astype(BF16)

    halves = [(SHARD_SHAPES[w][0] // 2, SHARD_SHAPES[w][1]) for w in ws]
    own = [pl.BlockSpec((None, None) + h, lambda j, pos_ref: (j, pos_ref[0], 0, 0)) for h in halves]
    blk = [pl.BlockSpec((None,) + h, lambda j, pos_ref: (j, 0, 0)) for h in halves]
    g4 = [g.reshape((N_SHARD, 2) + h) for g, h in zip(g32s, halves)]
    outs = pl.pallas_call(
        body, name="pair_sum_" + "_".join(str(w) for w in ws),
        grid_spec=pltpu.PrefetchScalarGridSpec(
            num_scalar_prefetch=1, grid=(N_SHARD,), in_specs=own + blk, out_specs=blk + blk),
        out_shape=[jax.ShapeDtypeStruct((N_SHARD,) + h, F32) for h in halves]
        + [jax.ShapeDtypeStruct((N_SHARD,) + h, BF16) for h in halves],
        compiler_params=_params(("arbitrary",)),
    )(pos, *g4, *recvs)
    return outs[:n], outs[n:]


def _chip_sum(pos, p32s, recvs):
    parts = 2

    def body(pos_ref, *refs):
        del pos_ref
        p_refs, r_refs, f_refs = refs[:N_BIG], refs[N_BIG:2 * N_BIG], refs[2 * N_BIG:]
        for w in range(N_BIG):
            f_refs[w][...] = ((p_refs[w][...] + r_refs[w][0].astype(F32)) + r_refs[w][1].astype(F32)) \
                + r_refs[w][2].astype(F32)

    quarters = [(r // 2 // parts, cc) for r, cc in SHARD_SHAPES]
    own = [pl.BlockSpec((None,) + qt, lambda i, pos_ref: (pos_ref[1], i, 0)) for qt in quarters]
    rcv = [pl.BlockSpec((3,) + qt, lambda i, pos_ref: (0, i, 0)) for qt in quarters]
    out = [pl.BlockSpec(qt, lambda i, pos_ref: (i, 0)) for qt in quarters]
    return pl.pallas_call(
        body, name="chip_sum",
        grid_spec=pltpu.PrefetchScalarGridSpec(
            num_scalar_prefetch=1, grid=(parts,), in_specs=own + rcv, out_specs=out),
        out_shape=[jax.ShapeDtypeStruct((r // 2, cc), F32) for r, cc in SHARD_SHAPES],
        compiler_params=_params(("arbitrary",)),
    )(pos, *p32s, *recvs)


def _adamw(w, g, m, v):
    m_new = ADAM_B1 * m + (1.0 - ADAM_B1) * g
    v_new = ADAM_B2 * v + (1.0 - ADAM_B2) * (g * g)
    m_hat = m_new / (1.0 - ADAM_B1 ** ADAM_STEP)
    v_hat = v_new / (1.0 - ADAM_B2 ** ADAM_STEP)
    delta = -ADAM_LR * (m_hat / (jnp.sqrt(v_hat) + ADAM_EPS) + ADAM_WD * w)
    return delta, m_new, v_new


def _adam_half(name, which, grads, ws, ms, vs, into=None):
    nb = 4

    def body(which_ref, *refs):
        del which_ref
        groups = [refs[i * N_BIG:(i + 1) * N_BIG] for i in range(4)]
        g_refs, w_refs, m_refs, v_refs = groups
        go_refs, do_refs, mo_refs, vo_refs = [refs[len(refs) - (4 - i) * N_BIG:len(refs) - (3 - i) * N_BIG]
                                              for i in range(4)]
        for w in range(N_BIG):
            g = g_refs[w][...]
            delta, m_new, v_new = _adamw(w_refs[w][...], g, m_refs[w][...], v_refs[w][...])
            go_refs[w][...] = g
            do_refs[w][...] = delta
            mo_refs[w][...] = m_new
            vo_refs[w][...] = v_new

    blocks = [(r // 2 // nb, cc) for r, cc in SHARD_SHAPES]
    half = [pl.BlockSpec(b, lambda i, which_ref: (i, 0)) for b in blocks]
    full = [pl.BlockSpec((None,) + b, lambda i, which_ref: (0, which_ref[0] * nb + i, 0)) for b in blocks]
    shapes = [jax.ShapeDtypeStruct((1,) + shp, F32) for shp in SHARD_SHAPES]
    carried = [] if into is None else [a for kind in into for a in kind]
    first = 1 + 4 * N_BIG
    outs = pl.pallas_call(
        body, name=name,
        grid_spec=pltpu.PrefetchScalarGridSpec(
            num_scalar_prefetch=1, grid=(nb,), in_specs=half + full * 3 + [HBM_SPEC] * len(carried),
            out_specs=full * 4),
        out_shape=shapes * 4,
        input_output_aliases={first + i: i for i in range(len(carried))},
        compiler_params=_params(("arbitrary",)),
    )(which, *grads, *ws, *ms, *vs, *carried)
    return [outs[i * N_BIG:(i + 1) * N_BIG] for i in range(4)]


SMALL_ROWS = 8
ROW_CONV_B, ROW_POOL_SCALE, ROW_LN1_G, ROW_LN1_B, ROW_LN2_G, ROW_LN2_B, ROW_LOSS = range(7)
SMALL_VECS = ((ROW_CONV_B, D_FF), (ROW_POOL_SCALE, POOL_W), (ROW_LN1_G, D_MODEL), (ROW_LN1_B, D_MODEL),
              (ROW_LN2_G, D_MODEL), (ROW_LN2_B, D_MODEL))


def _small_pack(loss, vec_grads):
    def body(*refs):
        loss_ref, gvec, out_ref = refs[0], refs[1:-1], refs[-1]
        out_ref[...] = jnp.zeros_like(out_ref)
        for (row, n), ref in zip(SMALL_VECS, gvec):
            out_ref[row:row + 1, 0:n] = ref[...]
        out_ref[ROW_LOSS:ROW_LOSS + 1, 0:HEAD_DIM] = jnp.broadcast_to(loss_ref[...], (1, HEAD_DIM))

    return pl.pallas_call(
        body, name="small_pack", in_specs=[_whole()] * (1 + len(vec_grads)), out_specs=_whole(),
        out_shape=jax.ShapeDtypeStruct((SMALL_ROWS, D_FF), F32),
    )(loss, *vec_grads)


def _small_pair_sum(own, sibling):
    n = len(own)

    def body(*refs):
        x, y, _ = _mesh_pos()
        for i in range(n):
            refs[2 * n + i][2 * x + y] = refs[i][...] + refs[n + i][...]

    return pl.pallas_call(
        body, name="small_pair_sum", in_specs=[_whole()] * (2 * n), out_specs=[_whole()] * n,
        out_shape=[jax.ShapeDtypeStruct((N_SHARD,) + a.shape, F32) for a in own],
        compiler_params=pltpu.CompilerParams(vmem_limit_bytes=VMEM_LIMIT),
    )(*own, *sibling)


def _small_chip_rider(gathered):
    n = len(gathered)

    def make(inplace, srcs, lands, send_sems, recv_sems):
        del inplace, lands
        x, y, c = _mesh_pos()
        j0 = 2 * x + y
        starts, waits = [], []
        for i in range(n):
            for k, chip in enumerate(_other_chips(x, y)):
                sems = dict(send_sem=send_sems.at[3 * i + k], recv_sem=recv_sems.at[3 * i + k],
                            device_id=(chip[0], chip[1], c), device_id_type=MESH)
                send = pltpu.make_async_remote_copy(src_ref=srcs[i].at[j0], dst_ref=srcs[i].at[j0], **sems)
                arrival = pltpu.make_async_remote_copy(
                    src_ref=srcs[i].at[j0], dst_ref=srcs[i].at[2 * chip[0] + chip[1]], **sems)
                starts.append(send)
                waits += [arrival.wait_recv, send.wait_send]
        return starts, waits

    return _Rider([], gathered, [], 3 * n, make)


def _small_adam(all_a, all_b, all_c, wp, cwp, vec_ws, m_wp, m_cwp, vec_ms, v_wp, v_cwp, vec_vs):
    nv = len(SMALL_VECS)
    np_ = 2 + nv

    def body(*refs):
        all_a_ref, all_b_ref, all_c_ref = refs[0:3]
        w_all, m_all, v_all = (refs[3 + i * np_:3 + (i + 1) * np_] for i in range(3))
        loss_out = refs[3 + 3 * np_]
        outs = refs[4 + 3 * np_:]
        x, y, _ = _mesh_pos()
        j0 = 2 * x + y
        tot_a = ((all_a_ref[0] + all_a_ref[1]) + all_a_ref[2]) + all_a_ref[3]
        tot_b = ((all_b_ref[0] + all_b_ref[1]) + all_b_ref[2]) + all_b_ref[3]
        tot_c = ((all_c_ref[0, j0] + all_c_ref[1, j0]) + all_c_ref[2, j0]) + all_c_ref[3, j0]
        loss_out[...] = tot_b[ROW_LOSS:ROW_LOSS + 1, 0:1]
        grads = [tot_a, tot_c] + [tot_b[row:row + 1, 0:n] for row, n in SMALL_VECS]
        for p in range(np_):
            delta, m_new, v_new = _adamw(w_all[p][...], grads[p], m_all[p][...], v_all[p][...])
            outs[p][...] = grads[p]
            outs[np_ + p][...] = delta
            outs[2 * np_ + p][...] = m_new
            outs[3 * np_ + p][...] = v_new

    pshapes = [wp.shape, CW_PAD] + [wv.shape for wv in vec_ws]
    out_shape = [jax.ShapeDtypeStruct((1, 1), F32)] + [jax.ShapeDtypeStruct(s, F32) for s in pshapes] * 4
    outs = pl.pallas_call(
        body, name="small_adam",
        in_specs=[_whole()] * (3 + 3 * np_), out_specs=[_whole()] * len(out_shape), out_shape=out_shape,
        compiler_params=pltpu.CompilerParams(vmem_limit_bytes=VMEM_LIMIT),
    )(all_a, all_b, all_c, wp, cwp, *vec_ws, m_wp, m_cwp, *vec_ms, v_wp, v_cwp, *vec_vs)
    return outs[0], [outs[1 + i * np_:1 + (i + 1) * np_] for i in range(4)]


def _pad_cw(a):
    pad = [(0, 0)] * (a.ndim - 2) + [(0, CW_PAD[0] - a.shape[-2]), (0, CW_PAD[1] - a.shape[-1])]
    return jnp.pad(a, pad)


def kernel(x, w_in, w_pool, pool_scale, w_out, ln1_g, ln1_b, w_up, conv_w, conv_b, w_down, ln2_g, ln2_b, loss_target, m_w_in, m_w_pool, m_pool_scale, m_w_out, m_ln1_g, m_ln1_b, m_w_up, m_conv_w, m_conv_b, m_w_down, m_ln2_g, m_ln2_b, v_w_in, v_w_pool, v_pool_scale, v_w_out, v_ln1_g, v_ln1_b, v_w_up, v_conv_w, v_conv_b, v_w_down, v_ln2_g, v_ln2_b):
    pos = jnp.stack([lax.axis_index("c"), 2 * lax.axis_index("x") + lax.axis_index("y")]).astype(jnp.int32)
    order = ("w_in", "w_out", "w_up", "w_down")
    w_in_i, w_out_i, w_up_i, w_down_i = range(N_BIG)
    vec_names = ("conv_b", "pool_scale", "ln1_g", "ln1_b", "ln2_g", "ln2_b")

    gathered = _gather_weights([w_in[0], w_out[0], w_up[0], w_down[0]], _pad_cw(conv_w[0]), (w_in_i,))
    cw_full = jnp.transpose(gathered[N_BIG][:, 0:3, 0:DOWN_SH], (1, 0, 2)).reshape(3, D_FF)
    up_a, up_b, up_c = (0, 224), (224, 160), (384, 128)
    assert up_c[0] + up_c[1] == SHARD_SHAPES[w_up_i][0] // 2

    class MeshComm:
        def __init__(self):
            self.w = {i: gathered[i] for i in range(N_BIG)}
            self.g32, self.g16, self.p32, self.p16, self.recv_b = {}, {}, {}, {}, {}
            self.up_complete = False
            self.tokens, self.chips = {}, []

        def weight(self, name):
            i = order.index(name)
            if name == "w_up" and not self.up_complete:
                (arrs, _), = _comm_only("gather_up_last", [_gather_rider(
                    {i: self.w[i]}, [("d2d_diag", i, up_b), ("d2d", i, up_c)])])
                self.w[i], self.up_complete = arrs[0], True
            full = self.w[i]
            return full.reshape(-1, full.shape[-1]) if name in ("w_out", "w_down") else full

        def _gather(self, ws, ops):
            return _gather_rider({w: self.w[w] for w in ws}, ops), ("w", ws)

        def _pair(self, ws):
            return _pair_rider(ws, [self.g16[w] for w in ws]), ("recv_a", ws)

        def _chip(self, ws, rows=None, resume=False):
            landing = [self.recv_b[w] for w in ws] if resume else None
            return _chip_rider(ws, [self.p16[w] for w in ws], rows, landing), ("recv_b", ws)

        def plan(self, call):
            out_all, down_all = _whole_half(w_out_i), _whole_half(w_down_i)
            if call == "proj_pool":
                return [self._gather([w_out_i, w_up_i, w_down_i],
                                     [("ici", w_out_i, out_all), ("nbr", w_down_i, down_all),
                                      ("nbr", w_up_i, up_a)])]
            if call == "retention_fwd":
                return [self._gather([w_out_i, w_up_i, w_down_i],
                                     [("d2d", w_out_i, out_all),
                                      ("relay", w_down_i, down_all), ("d2d_nbr", w_down_i, down_all),
                                      ("relay", w_up_i, up_a), ("d2d_nbr", w_up_i, up_a), ("nbr", w_up_i, up_b)])]
            if call == "outproj_ln1":
                return [self._gather([w_up_i, w_down_i],
                                     [("d2d_diag", w_down_i, down_all), ("d2d_diag", w_up_i, up_a),
                                      ("relay", w_up_i, up_b), ("d2d_nbr", w_up_i, up_b), ("ici", w_up_i, up_c)])]
            return []

        def after(self, call):
            return tuple(self.tokens.pop(call, ()))

        def riders(self, call):
            self.pending = self.plan(call)
            return [r for r, _ in self.pending]

        def _start(self, name, rider, before):
            state, token = _split_start(name, rider)
            self.tokens.setdefault(before, []).append(token)
            return state

        def _finish_pair(self, name, state, ws, after):
            _, lands = _split_wait(name, state, after)
            self._finish_sum(ws, lands)

        def landed(self, call, results, outs):
            for (_, (slot, ws)), (inplace, lands) in zip(self.pending, results):
                for w, arr in zip(ws, inplace if len(inplace) else lands):
                    getattr(self, slot)[w] = arr
            if call == "wgrad_out":
                self._finish_pair("pair_exchange_up_wait", self.pair_up, [w_up_i], outs[1])
                self.chips.append(([w_up_i], self._start(
                    "chip_exchange_up_start", self._chip([w_up_i])[0], "wgrad_down")))
            if call == "mix_bwd":
                ws = [w_out_i, w_down_i]
                self._finish_pair("pair_exchange_out_down_wait", self.pair_out_down, ws, outs[0])
                self.chips.append((ws, self._start(
                    "chip_exchange_out_down_start", self._chip(ws)[0], "retention_bwd")))
            if call == "retention_bwd":
                own, sibling = _split_wait("small_pair_wait", self.small_pair, outs[0])
                self.small_chip = self._start(
                    "small_chip_start", _small_chip_rider(_small_pair_sum(own, sibling)), "wgrad_in")

        def small_gradients(self, loss, small):
            dcw4 = _pad_cw(jnp.transpose(small["conv_w"].reshape(3, N_SHARD, DOWN_SH), (1, 0, 2)))
            own = [small["w_pool"], _small_pack(loss, [small[n] for n in vec_names]), dcw4]
            self.small_pair = self._start("small_pair_start", _final_rider(own), "retention_bwd")

        def gradient(self, name, g32, g16):
            w = order.index(name)
            shape = (N_SHARD,) + SHARD_SHAPES[w]
            self.g32[w], self.g16[w] = g32.reshape(shape), g16.reshape(shape)
            if name == "w_up":
                self.pair_up = self._start("pair_exchange_up_start", self._pair([w])[0], "wgrad_out")
            if name == "w_down":
                self.pair_out_down = self._start("pair_exchange_out_down_start",
                                                 self._pair([w_out_i, w_down_i])[0], "mix_bwd")
            if name == "w_in":
                (_, lands), = _comm_only("pair_exchange_in", [self._pair([w])[0]])
                self._finish_sum([w], lands)
                self.chips.append(([w], self._start("chip_exchange_in_start", self._chip([w])[0], "dx")))

        def _finish_sum(self, ws, lands):
            p32s, p16s = _pair_sum(pos, ws, [self.g32[w] for w in ws], lands)
            for w, p32, p16 in zip(ws, p32s, p16s):
                self.p32[w], self.p16[w] = p32, p16

        def finish(self, after):
            for n, (ws, state) in enumerate(self.chips):
                _, lands = _split_wait("chip_exchange_wait_%d" % n, state, after)
                for w, arr in zip(ws, lands):
                    self.recv_b[w] = arr
            return _split_wait("small_chip_wait", self.small_chip, after)[0]

    comm = MeshComm()
    loss, grad_x, small = _local_step(x[0], loss_target[0], cw_full, conv_b, w_pool[0], pool_scale,
                                      ln1_g, ln1_b, ln2_g, ln2_b, comm)

    given = dict(w_pool=w_pool, pool_scale=pool_scale, ln1_g=ln1_g, ln1_b=ln1_b, conv_w=conv_w, conv_b=conv_b,
                 ln2_g=ln2_g, ln2_b=ln2_b)
    given_m = dict(w_pool=m_w_pool, pool_scale=m_pool_scale, ln1_g=m_ln1_g, ln1_b=m_ln1_b, conv_w=m_conv_w,
                   conv_b=m_conv_b, ln2_g=m_ln2_g, ln2_b=m_ln2_b)
    given_v = dict(w_pool=v_w_pool, pool_scale=v_pool_scale, ln1_g=v_ln1_g, ln1_b=v_ln1_b, conv_w=v_conv_w,
                   conv_b=v_conv_b, ln2_g=v_ln2_g, ln2_b=v_ln2_b)
    args = []
    for src in (given, given_m, given_v):
        args += [src["w_pool"][0], _pad_cw(src["conv_w"][0]), [src[n] for n in vec_names]]
    small_sums = comm.finish(grad_x)
    loss_tot, small_out = _small_adam(*small_sums, *args)
    every = range(N_BIG)
    mine = _chip_sum(pos, [comm.p32[w] for w in every], [comm.recv_b[w] for w in every])
    final_state, _ = _split_start("pair_exchange_f32_start", _final_rider(mine))
    mine = final_state[2][:N_BIG]
    big = ([w_in, w_out, w_up, w_down], [m_w_in, m_w_out, m_w_up, m_w_down], [v_w_in, v_w_out, v_w_up, v_w_down])
    own_half = _adam_half("adam_own_half", pos[0:1], mine, *big)
    _, theirs = _split_wait("pair_exchange_f32_wait", final_state, own_half[0][0])
    big_out = _adam_half("adam_other_half", 1 - pos[0:1], theirs, *big, into=own_half)

    names = ("w_in", "w_pool", "pool_scale", "w_out", "ln1_g", "ln1_b", "w_up", "conv_w", "conv_b", "w_down",
             "ln2_g", "ln2_b")
    small_names = ("w_pool", "conv_w") + vec_names
    result = [loss_tot.reshape(()), grad_x[None]]
    for kind in range(4):
        for n in names:
            if n in order:
                result.append(big_out[kind][order.index(n)])
            else:
                val = small_out[kind][small_names.index(n)]
                if n == "conv_w":
                    val = val[0:3, 0:DOWN_SH][None]
                elif n == "w_pool":
                    val = val[None]
                result.append(val)
    return tuple(result)
```

```python
import functools
import math

import numpy as np
import jax
import jax.numpy as jnp
from jax import lax
from jax.experimental import pallas as pl
from jax.experimental.pallas import tpu as pltpu

F32 = jnp.float32
BF16 = jnp.bfloat16

D_MODEL = 1024
HEADS = 4
HEAD_DIM = 128
RET_W = HEADS * HEAD_DIM
POOL_WINDOWS = (2, 4, 8, 16)
POOL_W = 512
IN_W = 4 * RET_W + POOL_W
D_FF = 2816
N_SHARD = 4
IN_SH = IN_W // N_SHARD
UP_SH = 2 * D_FF // N_SHARD
DOWN_SH = D_FF // N_SHARD
OUT_SH = D_MODEL // N_SHARD
ROPE_BASE = 10000.0
LN_EPS = 1e-5
RMS_EPS = 1e-6
ALPHA = 2.0 ** 0.25
K_SCALE = HEAD_DIM ** -0.5
SUPER = 256
CHUNK = 64
POOL_HALO = 16
CONV_HALO = 8
FFN_STRIP = 128
LN_ROWS = 32

ADAM_LR = 0.001
ADAM_B1 = 0.9
ADAM_B2 = 0.999
ADAM_EPS = 1e-08
ADAM_WD = 0.01
ADAM_STEP = 10

MESH = pl.DeviceIdType.MESH
VMEM_LIMIT = 56 * 1024 * 1024


def _dot(a, b):
    return jnp.dot(a, b, preferred_element_type=F32)


def _dot_nt(a, b):
    return lax.dot_general(a, b, (((1,), (1,)), ((), ())), preferred_element_type=F32)


def _dot_tn(a, b):
    return lax.dot_general(a, b, (((0,), (0,)), ((), ())), preferred_element_type=F32)


def _sigmoid(x):
    return 1.0 / (1.0 + jnp.exp(-x))


def _params(sem):
    return pltpu.CompilerParams(dimension_semantics=sem, vmem_limit_bytes=VMEM_LIMIT)


def _whole():
    return pl.BlockSpec(memory_space=pltpu.VMEM)


HBM_SPEC = pl.BlockSpec(memory_space=pl.ANY)


class _Rider:
    def __init__(self, inplace, srcs, lands, n_copies, make):
        self.inplace, self.srcs, self.lands, self.n_copies, self.make = list(inplace), list(srcs), list(lands), n_copies, make


def _call(body, *, name, grid, in_specs, out_specs, out_shape, operands, scratch_shapes=(), sem=(),
          aliases=None, riders=(), after=()):
    n_in, n_out, n_scr = len(in_specs), len(out_shape), len(scratch_shapes)
    in_specs, out_specs, out_shape = list(in_specs), list(out_specs), list(out_shape)
    operands, scratch_shapes, aliases = list(operands), list(scratch_shapes), dict(aliases or {})
    in_specs += [_whole()] * len(after)
    operands += list(after)
    for r in riders:
        for a in r.inplace:
            aliases[len(in_specs)] = len(out_shape)
            in_specs.append(HBM_SPEC)
            operands.append(a)
            out_specs.append(HBM_SPEC)
            out_shape.append(jax.ShapeDtypeStruct(a.shape, a.dtype))
        for a in r.srcs:
            in_specs.append(HBM_SPEC)
            operands.append(a)
        for shp in r.lands:
            out_specs.append(HBM_SPEC)
            out_shape.append(shp)
        scratch_shapes += [pltpu.SemaphoreType.DMA((r.n_copies,)), pltpu.SemaphoreType.DMA((r.n_copies,))]

    def full(*refs):
        ins = refs[:n_in]
        at = n_in + len(after)
        r_srcs = []
        for r in riders:
            at += len(r.inplace)
            r_srcs.append(refs[at:at + len(r.srcs)])
            at += len(r.srcs)
        outs = refs[at:at + n_out]
        at += n_out
        r_outs = []
        for r in riders:
            r_outs.append((refs[at:at + len(r.inplace)], refs[at + len(r.inplace):at + len(r.inplace) + len(r.lands)]))
            at += len(r.inplace) + len(r.lands)
        scr = refs[at:at + n_scr]
        at += n_scr
        r_sems = [refs[at + 2 * i:at + 2 * i + 2] for i in range(len(riders))]

        def copies():
            return [r.make(r_outs[i][0], r_srcs[i], r_outs[i][1], r_sems[i][0], r_sems[i][1])
                    for i, r in enumerate(riders)]

        def start():
            for starts, _ in copies():
                for cp in starts:
                    cp.start()

        def finish():
            for _, waits in copies():
                for wait in waits:
                    wait()

        if riders and grid:
            first = functools.reduce(jnp.logical_and, [pl.program_id(d) == 0 for d in range(len(grid))])
            last = functools.reduce(jnp.logical_and, [pl.program_id(d) == grid[d] - 1 for d in range(len(grid))])
            pl.when(first)(start)
            body(*ins, *outs, *scr)
            pl.when(last)(finish)
        else:
            if riders:
                start()
            body(*ins, *outs, *scr)
            if riders:
                finish()

    params = _params(sem) if grid else pltpu.CompilerParams(vmem_limit_bytes=VMEM_LIMIT)
    res = pl.pallas_call(
        full, name=name, grid=grid, in_specs=in_specs, out_specs=out_specs, out_shape=out_shape,
        scratch_shapes=scratch_shapes, input_output_aliases=aliases, compiler_params=params,
    )(*operands)
    outs, at, rider_res = res[:n_out], n_out, []
    for r in riders:
        rider_res.append((res[at:at + len(r.inplace)], res[at + len(r.inplace):at + len(r.inplace) + len(r.lands)]))
        at += len(r.inplace) + len(r.lands)
    return list(outs), rider_res


def _gammas():
    return [1.0 - 2.0 ** (-5.0 - h) for h in range(HEADS)]


def _decay_tables():
    idx = np.arange(SUPER)
    dist = np.abs(idx[:, None] - idx[None, :]).astype(np.float64)
    visible = (idx[None, :] // CHUNK) <= (idx[:, None] // CHUNK)
    mask = np.stack([np.where(visible, g ** dist, 0.0) for g in _gammas()])
    qd = np.concatenate([np.repeat((g ** (idx + 1.0))[:, None], HEAD_DIM, 1) for g in _gammas()], 1)
    kd = np.concatenate([np.repeat((g ** (SUPER - 1.0 - idx))[:, None], HEAD_DIM, 1) for g in _gammas()], 1)
    return (jnp.asarray(mask, F32), jnp.asarray(qd, F32), jnp.asarray(kd, F32))


def _rope_tables(s):
    inv_freq = ROPE_BASE ** (-np.arange(0, HEAD_DIM, 2, dtype=np.float64) / HEAD_DIM)
    ang = np.arange(s, dtype=np.float64)[:, None] * inv_freq[None, :]
    cos, sin = np.cos(ang), np.sin(ang)
    return (jnp.asarray(np.concatenate([cos, cos], 1), F32),
            jnp.asarray(np.concatenate([-sin, sin], 1), F32))


def _rope(t, cosf, sinf):
    return t * cosf + pltpu.roll(t, HEAD_DIM // 2, 1) * sinf


def _rope_t(t, cosf, sinf):
    return t * cosf - pltpu.roll(t, HEAD_DIM // 2, 1) * sinf


def _layernorm_fwd(z):
    mu = jnp.mean(z, axis=-1, keepdims=True)
    zc = z - mu
    var = jnp.mean(zc * zc, axis=-1, keepdims=True)
    rstd = lax.rsqrt(var + LN_EPS)
    return zc * rstd, rstd


def _layernorm_bwd(dy, xhat, rstd, gain):
    dxh = dy * gain
    m1 = jnp.mean(dxh, axis=-1, keepdims=True)
    m2 = jnp.mean(dxh * xhat, axis=-1, keepdims=True)
    return rstd * (dxh - m1 - xhat * m2)


def _proj_pool(x, win4, cosf, sinf, wpool, pscale, ts, riders=(), after=()):
    s = x.shape[0]
    nt = s // ts

    def body(x_ref, w_ref, cos_ref, sin_ref, wp_ref, ps_ref,
             xb_ref, q_ref, k_ref, v_ref, g_ref, pooled_ref, cat_ref, proj_scr, pext_scr):
        i = pl.program_id(0)
        xb = x_ref[...].astype(BF16)
        xb_ref[...] = xb
        for j in range(N_SHARD):
            proj_scr[:, j * IN_SH:(j + 1) * IN_SH] = _dot(xb, w_ref[j])
        cosf_t = cos_ref[...]
        sinf_t = sin_ref[...]
        for h in range(HEADS):
            lo = h * HEAD_DIM
            q_ref[:, lo:lo + HEAD_DIM] = _rope(proj_scr[:, lo:lo + HEAD_DIM], cosf_t, sinf_t).astype(BF16)
            kk = _rope(proj_scr[:, RET_W + lo:RET_W + lo + HEAD_DIM], cosf_t, sinf_t) * K_SCALE
            k_ref[:, lo:lo + HEAD_DIM] = kk.astype(BF16)
        v_ref[...] = proj_scr[:, 2 * RET_W:3 * RET_W].astype(BF16)
        g_ref[...] = proj_scr[:, 3 * RET_W:4 * RET_W]

        @pl.when(i == 0)
        def _():
            pext_scr[0:POOL_HALO, :] = jnp.zeros((POOL_HALO, POOL_W), F32)

        pext_scr[POOL_HALO:POOL_HALO + ts, :] = proj_scr[:, 4 * RET_W:IN_W]
        pos = (i * ts + lax.broadcasted_iota(jnp.int32, (ts, 1), 0) + 1).astype(F32)
        for gi, w in enumerate(POOL_WINDOWS):
            lo = gi * HEAD_DIM
            ext = pext_scr[:, lo:lo + HEAD_DIM]
            acc = ext
            shift = 1
            while shift < w:
                acc = acc + pltpu.roll(acc, shift, 0)
                shift *= 2
            tok = ext[POOL_HALO:POOL_HALO + ts]
            pooled = acc[POOL_HALO:POOL_HALO + ts] / jnp.minimum(pos, float(w)) - tok
            pooled_b = pooled.astype(BF16)
            pooled_ref[:, lo:lo + HEAD_DIM] = pooled_b
            lin = _dot(pooled_b, wp_ref[gi])
            cat_ref[:, lo:lo + HEAD_DIM] = (lin * ps_ref[:, lo:lo + HEAD_DIM]).astype(BF16)
        pext_scr[0:POOL_HALO, :] = pext_scr[ts:ts + POOL_HALO, :]

    tile = lambda w: pl.BlockSpec((ts, w), lambda i: (i, 0))
    return _call(
        body, name="proj_pool", grid=(nt,),
        in_specs=[tile(D_MODEL), _whole(), tile(HEAD_DIM), tile(HEAD_DIM), _whole(), _whole()],
        out_specs=[tile(D_MODEL), tile(RET_W), tile(RET_W), tile(RET_W), tile(RET_W), tile(POOL_W),
                   pl.BlockSpec((ts, POOL_W), lambda i: (i, 1))],
        out_shape=[jax.ShapeDtypeStruct((s, D_MODEL), BF16), jax.ShapeDtypeStruct((s, RET_W), BF16),
                   jax.ShapeDtypeStruct((s, RET_W), BF16), jax.ShapeDtypeStruct((s, RET_W), BF16),
                   jax.ShapeDtypeStruct((s, RET_W), F32), jax.ShapeDtypeStruct((s, POOL_W), BF16),
                   jax.ShapeDtypeStruct((s, 2 * RET_W), BF16)],
        scratch_shapes=[pltpu.VMEM((ts, IN_W), F32), pltpu.VMEM((ts + POOL_HALO, POOL_W), F32)],
        sem=("arbitrary",), operands=(x, win4, cosf, sinf, wpool, pscale), riders=riders, after=after,
    )


def _retention_fwd(q, k, v, g, cat, mask, qd, kd, riders=(), after=()):
    s = q.shape[0]
    ns = s // SUPER
    cdec = [gm ** float(SUPER) for gm in _gammas()]

    def body(q_ref, k_ref, v_ref, g_ref, cat_in, mask_ref, qd_ref, kd_ref,
             ret_ref, cat_ref, st_ref, state_scr):
        del cat_in
        n = pl.program_id(0)

        @pl.when(n == 0)
        def _():
            state_scr[...] = jnp.zeros_like(state_scr)

        for h in range(HEADS):
            sl = slice(h * HEAD_DIM, (h + 1) * HEAD_DIM)
            qh, kh, vh = q_ref[:, sl], k_ref[:, sl], v_ref[:, sl]
            sc = _dot_nt(qh, kh) * mask_ref[h]
            st = state_scr[h]
            stb = st.astype(BF16)
            st_ref[0, h] = stb
            qdb = (qh.astype(F32) * qd_ref[:, sl]).astype(BF16)
            kdb = (kh.astype(F32) * kd_ref[:, sl]).astype(BF16)
            ret = _dot(sc.astype(BF16), vh) + _dot(qdb, stb)
            state_scr[h] = st * cdec[h] + _dot_tn(kdb, vh)
            ret_ref[:, sl] = ret
            r = lax.rsqrt(jnp.mean(ret * ret, axis=-1, keepdims=True) + RMS_EPS)
            gh = g_ref[:, sl]
            cat_ref[:, sl] = ((ret * r) * (gh * _sigmoid(gh))).astype(BF16)

    tile = pl.BlockSpec((SUPER, RET_W), lambda n: (n, 0))
    return _call(
        body, name="retention_fwd", grid=(ns,),
        in_specs=[tile, tile, tile, tile, HBM_SPEC, _whole(), _whole(), _whole()],
        out_specs=[tile, tile, pl.BlockSpec((1, HEADS, HEAD_DIM, HEAD_DIM), lambda n: (n, 0, 0, 0))],
        out_shape=[jax.ShapeDtypeStruct((s, RET_W), F32), jax.ShapeDtypeStruct((s, 2 * RET_W), BF16),
                   jax.ShapeDtypeStruct((ns, HEADS, HEAD_DIM, HEAD_DIM), BF16)],
        scratch_shapes=[pltpu.VMEM((HEADS, HEAD_DIM, HEAD_DIM), F32)],
        aliases={4: 1}, sem=("arbitrary",), operands=(q, k, v, g, cat, mask, qd, kd), riders=riders,
        after=after,
    )


def _outproj_ln1(x, cat, wout, g1, b1, ts, riders=(), after=()):
    s = x.shape[0]

    def body(x_ref, cat_ref, w_ref, g_ref, b_ref, xhat_ref, rstd_ref, h1b_ref):
        z = ALPHA * x_ref[...] + _dot(cat_ref[...], w_ref[...])
        xhat, rstd = _layernorm_fwd(z)
        xhat_ref[...] = xhat
        rstd_ref[...] = rstd
        h1b_ref[...] = (xhat * g_ref[...] + b_ref[...]).astype(BF16)

    tile = lambda w: pl.BlockSpec((ts, w), lambda i: (i, 0))
    return _call(
        body, name="outproj_ln1", grid=(s // ts,),
        in_specs=[tile(D_MODEL), tile(D_MODEL), _whole(), _whole(), _whole()],
        out_specs=[tile(D_MODEL), tile(1), tile(D_MODEL)],
        out_shape=[jax.ShapeDtypeStruct((s, D_MODEL), F32), jax.ShapeDtypeStruct((s, 1), F32),
                   jax.ShapeDtypeStruct((s, D_MODEL), BF16)],
        sem=("arbitrary",), operands=(x, cat, wout, g1, b1), riders=riders, after=after,
    )


def _ffn_fwd_loss(xhat1, h1b, target, wup4, wdown, cw, cb, g1, b1, g2, b2, ts):
    s = xhat1.shape[0]

    def body(xhat_ref, h1b_ref, tgt_ref, wup_ref, wdn_ref, cw_ref, cb_ref, g1_ref, b1_ref, g2_ref, b2_ref,
             ub_ref, act_ref, sd_ref, dz2_ref, dz2b_ref, loss_ref, dg2_ref, db2_ref, val_scr, gext_scr, ffn_scr):
        i = pl.program_id(0)

        @pl.when(i == 0)
        def _():
            gext_scr[0:CONV_HALO, :] = jnp.zeros((CONV_HALO, D_FF), F32)
            loss_ref[...] = jnp.zeros_like(loss_ref)
            dg2_ref[...] = jnp.zeros_like(dg2_ref)
            db2_ref[...] = jnp.zeros_like(db2_ref)

        hb = h1b_ref[...]
        for half in range(2):
            lo = half * UP_SH
            gext_scr[CONV_HALO:CONV_HALO + ts, lo:lo + UP_SH] = _dot(hb, wup_ref[2 + half])
            val_scr[:, lo:lo + UP_SH] = _dot(hb, wup_ref[half])
            for c0 in range(lo, lo + UP_SH, FFN_STRIP):
                cols = slice(c0, c0 + FFN_STRIP)
                ext = gext_scr[:, cols]
                gate = ext[CONV_HALO:]
                hc = cb_ref[:, cols] + ((pltpu.roll(ext, 2, 0)[CONV_HALO:] * cw_ref[0:1, cols]
                                         + pltpu.roll(ext, 1, 0)[CONV_HALO:] * cw_ref[1:2, cols])
                                        + gate * cw_ref[2:3, cols])
                val = val_scr[:, cols]
                sg = _sigmoid(hc)
                si = hc * sg
                act_ref[:, cols] = (si * val).astype(BF16)
                ub_ref[:, cols] = val.astype(BF16)
                ub_ref[:, D_FF + c0:D_FF + c0 + FFN_STRIP] = gate.astype(BF16)
                sd_ref[:, cols] = hc.astype(BF16)
            part = _dot(act_ref[:, lo:lo + UP_SH], wdn_ref[lo:lo + UP_SH, :])
            if half == 0:
                ffn_scr[...] = part
            else:
                ffn_scr[...] += part
        gext_scr[0:CONV_HALO, :] = gext_scr[ts:ts + CONV_HALO, :]

        loss_acc = jnp.zeros((1, 1), F32)
        dg2_acc = jnp.zeros((1, D_MODEL), F32)
        db2_acc = jnp.zeros((1, D_MODEL), F32)
        for r0 in range(0, ts, LN_ROWS):
            rows = slice(r0, r0 + LN_ROWS)
            h1 = xhat_ref[rows, :] * g1_ref[...] + b1_ref[...]
            xhat2, rstd2 = _layernorm_fwd(ALPHA * h1 + ffn_scr[rows, :])
            diff = (xhat2 * g2_ref[...] + b2_ref[...]) - tgt_ref[rows, :]
            row = jnp.mean(diff * diff, axis=-1, keepdims=True)
            loss_acc = loss_acc + 0.5 * jnp.sum(row, axis=0, keepdims=True)
            dy = diff * (1.0 / D_MODEL)
            dg2_acc = dg2_acc + jnp.sum(dy * xhat2, axis=0, keepdims=True)
            db2_acc = db2_acc + jnp.sum(dy, axis=0, keepdims=True)
            dz2 = _layernorm_bwd(dy, xhat2, rstd2, g2_ref[...])
            dz2_ref[rows, :] = dz2
            dz2b_ref[rows, :] = dz2.astype(BF16)
        loss_ref[...] += loss_acc
        dg2_ref[...] += dg2_acc
        db2_ref[...] += db2_acc

    tile = lambda w: pl.BlockSpec((ts, w), lambda i: (i, 0))
    acc = lambda w: pl.BlockSpec((1, w), lambda i: (0, 0))
    return pl.pallas_call(
        body, name="ffn_fwd_loss", grid=(s // ts,),
        in_specs=[tile(D_MODEL), tile(D_MODEL), tile(D_MODEL)] + [_whole()] * 8,
        out_specs=[tile(2 * D_FF), tile(D_FF), tile(D_FF), tile(D_MODEL), tile(D_MODEL),
                   acc(1), acc(D_MODEL), acc(D_MODEL)],
        out_shape=[jax.ShapeDtypeStruct((s, 2 * D_FF), BF16), jax.ShapeDtypeStruct((s, D_FF), BF16),
                   jax.ShapeDtypeStruct((s, D_FF), BF16), jax.ShapeDtypeStruct((s, D_MODEL), F32),
                   jax.ShapeDtypeStruct((s, D_MODEL), BF16),
                   jax.ShapeDtypeStruct((1, 1), F32), jax.ShapeDtypeStruct((1, D_MODEL), F32),
                   jax.ShapeDtypeStruct((1, D_MODEL), F32)],
        scratch_shapes=[pltpu.VMEM((ts, D_FF), F32), pltpu.VMEM((ts + CONV_HALO, D_FF), F32),
                        pltpu.VMEM((ts, D_MODEL), F32)],
        compiler_params=_params(("arbitrary",)),
    )(xhat1, h1b, target, wup4, wdown, cw, cb, g1, b1, g2, b2)


def _ffn_bwd(dz2, dz2b, ub, sd, xhat1, rstd1, wup4, wdown, cw, g1, ts):
    s = dz2.shape[0]
    nt = s // ts

    def body(dz2_ref, dz2b_ref, ub_ref, sd_ref, xhat_ref, rstd_ref, wup_ref, wdn_ref, cw_ref, g1_ref,
             dub_ref, dz1_ref, dz1b_ref, dg1_ref, db1_ref, dcw_ref, dcb_ref, dext_scr, da_scr):
        i = pl.program_id(0)

        @pl.when(i == 0)
        def _():
            dext_scr[ts:ts + CONV_HALO, :] = jnp.zeros((CONV_HALO, D_FF), F32)
            dg1_ref[...] = jnp.zeros_like(dg1_ref)
            db1_ref[...] = jnp.zeros_like(db1_ref)
            dcw_ref[...] = jnp.zeros_like(dcw_ref)
            dcb_ref[...] = jnp.zeros_like(dcb_ref)

        da_scr[...] = _dot_nt(dz2b_ref[...], wdn_ref[...])
        n_ext = ts + CONV_HALO
        for c0 in range(0, D_FF, FFN_STRIP):
            cols = slice(c0, c0 + FFN_STRIP)
            gcols = slice(D_FF + c0, D_FF + c0 + FFN_STRIP)
            val = ub_ref[:, cols].astype(F32)
            gate = ub_ref[:, gcols].astype(F32)
            da = da_scr[:, cols]
            hc = sd_ref[:, cols].astype(F32)
            sg = _sigmoid(hc)
            dhc = da * val * (sg * (1.0 + hc * (1.0 - sg)))
            dext_scr[0:ts, cols] = dhc
            dext = dext_scr[:, cols]
            dhc1 = pltpu.roll(dext, n_ext - 1, 0)[0:ts]
            dhc2 = pltpu.roll(dext, n_ext - 2, 0)[0:ts]
            dcb_ref[:, cols] += jnp.sum(dhc, axis=0, keepdims=True)
            dcw_ref[0:1, cols] += jnp.sum(dhc2 * gate, axis=0, keepdims=True)
            dcw_ref[1:2, cols] += jnp.sum(dhc1 * gate, axis=0, keepdims=True)
            dcw_ref[2:3, cols] += jnp.sum(dhc * gate, axis=0, keepdims=True)
            dgate = dhc * cw_ref[2:3, cols] + dhc1 * cw_ref[1:2, cols] + dhc2 * cw_ref[0:1, cols]
            dub_ref[:, cols] = (da * (hc * sg)).astype(BF16)
            dub_ref[:, gcols] = dgate.astype(BF16)
        dext_scr[ts:n_ext, :] = dext_scr[0:CONV_HALO, :]
        dh1 = ALPHA * dz2_ref[...]
        for j in range(N_SHARD):
            dh1 = dh1 + _dot_nt(dub_ref[:, j * UP_SH:(j + 1) * UP_SH], wup_ref[j])
        xhat = xhat_ref[...]
        dg1_ref[...] += jnp.sum(dh1 * xhat, axis=0, keepdims=True)
        db1_ref[...] += jnp.sum(dh1, axis=0, keepdims=True)
        dz1 = _layernorm_bwd(dh1, xhat, rstd_ref[...], g1_ref[...])
        dz1_ref[...] = dz1
        dz1b_ref[...] = dz1.astype(BF16)

    tile = lambda w: pl.BlockSpec((ts, w), lambda i: (nt - 1 - i, 0))
    acc = lambda rws, w: pl.BlockSpec((rws, w), lambda i: (0, 0))
    return pl.pallas_call(
        body, name="ffn_bwd", grid=(nt,),
        in_specs=[tile(D_MODEL), tile(D_MODEL), tile(2 * D_FF), tile(D_FF), tile(D_MODEL), tile(1)]
        + [_whole()] * 4,
        out_specs=[tile(2 * D_FF), tile(D_MODEL), tile(D_MODEL), acc(1, D_MODEL), acc(1, D_MODEL),
                   acc(3, D_FF), acc(1, D_FF)],
        out_shape=[jax.ShapeDtypeStruct((s, 2 * D_FF), BF16),
                   jax.ShapeDtypeStruct((s, D_MODEL), F32), jax.ShapeDtypeStruct((s, D_MODEL), BF16),
                   jax.ShapeDtypeStruct((1, D_MODEL), F32),
                   jax.ShapeDtypeStruct((1, D_MODEL), F32), jax.ShapeDtypeStruct((3, D_FF), F32),
                   jax.ShapeDtypeStruct((1, D_FF), F32)],
        scratch_shapes=[pltpu.VMEM((ts + CONV_HALO, D_FF), F32), pltpu.VMEM((ts, D_FF), F32)],
        compiler_params=_params(("arbitrary",)),
    )(dz2, dz2b, ub, sd, xhat1, rstd1, wup4, wdown, cw, g1)


def _mix_bwd(dz1, pooled, ret, g, wout, wpool, pscale, ts, riders=(), after=()):
    s = dz1.shape[0]
    nt = s // ts

    def body(dz1_ref, pooled_ref, ret_ref, g_ref, wout_ref, wp_ref, ps_ref,
             dret_ref, dgp_ref, dwp_ref, dps_ref, eext_scr):
        i = pl.program_id(0)
        r = nt - 1 - i

        @pl.when(i == 0)
        def _():
            eext_scr[ts:ts + POOL_HALO, :] = jnp.zeros((POOL_HALO, POOL_W), F32)
            dwp_ref[...] = jnp.zeros_like(dwp_ref)
            dps_ref[...] = jnp.zeros_like(dps_ref)

        dzb = dz1_ref[...].astype(BF16)
        dcat_r = _dot_nt(dzb, wout_ref[0:RET_W, :])
        dcat_p = _dot_nt(dzb, wout_ref[RET_W:2 * RET_W, :])
        pos = (r * ts + lax.broadcasted_iota(jnp.int32, (ts, 1), 0) + 1).astype(F32)
        dpooled = []
        for gi, w in enumerate(POOL_WINDOWS):
            sl = slice(gi * HEAD_DIM, (gi + 1) * HEAD_DIM)
            pb = pooled_ref[:, sl]
            dy = dcat_p[:, sl]
            dps_ref[:, sl] += jnp.sum(dy * _dot(pb, wp_ref[gi]), axis=0, keepdims=True)
            dlin = (dy * ps_ref[:, sl]).astype(BF16)
            dwp_ref[gi] += _dot_tn(pb, dlin)
            dpg = _dot_nt(dlin, wp_ref[gi])
            dpooled.append(dpg)
            eext_scr[0:ts, sl] = dpg / jnp.minimum(pos, float(w))
        for gi, w in enumerate(POOL_WINDOWS):
            sl = slice(gi * HEAD_DIM, (gi + 1) * HEAD_DIM)
            acc = eext_scr[:, sl]
            shift = 1
            while shift < w:
                acc = acc + pltpu.roll(acc, ts + POOL_HALO - shift, 0)
                shift *= 2
            dgp_ref[:, RET_W + gi * HEAD_DIM:RET_W + (gi + 1) * HEAD_DIM] = (acc[0:ts] - dpooled[gi]).astype(BF16)
        eext_scr[ts:ts + POOL_HALO, :] = eext_scr[0:POOL_HALO, :]
        for h in range(HEADS):
            sl = slice(h * HEAD_DIM, (h + 1) * HEAD_DIM)
            rt = ret_ref[:, sl]
            rr = lax.rsqrt(jnp.mean(rt * rt, axis=-1, keepdims=True) + RMS_EPS)
            rn = rt * rr
            gh = g_ref[:, sl]
            sg = _sigmoid(gh)
            dy = dcat_r[:, sl]
            dgp_ref[:, sl] = (dy * rn * (sg * (1.0 + gh * (1.0 - sg)))).astype(BF16)
            drn = dy * (gh * sg)
            dret_ref[:, sl] = (rr * (drn - rn * jnp.mean(drn * rn, axis=-1, keepdims=True))).astype(BF16)

    tile = lambda w: pl.BlockSpec((ts, w), lambda i: (nt - 1 - i, 0))
    return _call(
        body, name="mix_bwd", grid=(nt,),
        in_specs=[tile(D_MODEL), tile(POOL_W), tile(RET_W), tile(RET_W), _whole(), _whole(), _whole()],
        out_specs=[tile(RET_W), tile(2 * RET_W),
                   pl.BlockSpec((len(POOL_WINDOWS), HEAD_DIM, HEAD_DIM), lambda i: (0, 0, 0)),
                   pl.BlockSpec((1, POOL_W), lambda i: (0, 0))],
        out_shape=[jax.ShapeDtypeStruct((s, RET_W), BF16), jax.ShapeDtypeStruct((s, 2 * RET_W), BF16),
                   jax.ShapeDtypeStruct((len(POOL_WINDOWS), HEAD_DIM, HEAD_DIM), F32),
                   jax.ShapeDtypeStruct((1, POOL_W), F32)],
        scratch_shapes=[pltpu.VMEM((ts + POOL_HALO, POOL_W), F32)],
        sem=("arbitrary",), operands=(dz1, pooled, ret, g, wout, wpool, pscale), riders=riders,
        after=after,
    )


def _retention_bwd(q, k, v, dret, dgp, states, mask, qd, kd, cosf, sinf, riders=(), after=()):
    s = q.shape[0]
    ns = s // SUPER
    cdec = [gm ** float(SUPER) for gm in _gammas()]

    def body(q_ref, k_ref, v_ref, do_ref, dgp_ref, st_ref, mask_ref, qd_ref, kd_ref, cos_ref, sin_ref,
             dproj_ref, dstate_scr):
        i = pl.program_id(0)

        @pl.when(i == 0)
        def _():
            dstate_scr[...] = jnp.zeros_like(dstate_scr)

        cosf_t = cos_ref[...]
        sinf_t = sin_ref[...]
        for h in range(HEADS):
            sl = slice(h * HEAD_DIM, (h + 1) * HEAD_DIM)
            qh, kh, vh, doh = q_ref[:, sl], k_ref[:, sl], v_ref[:, sl], do_ref[:, sl]
            m = mask_ref[h]
            scb = (_dot_nt(qh, kh) * m).astype(BF16)
            dscb = (_dot_nt(doh, vh) * m).astype(BF16)
            stb = st_ref[0, h]
            dst = dstate_scr[h]
            dstb = dst.astype(BF16)
            qdb = (qh.astype(F32) * qd_ref[:, sl]).astype(BF16)
            kdb = (kh.astype(F32) * kd_ref[:, sl]).astype(BF16)
            dq = _dot(dscb, kh) + _dot_nt(doh, stb) * qd_ref[:, sl]
            dk = _dot_tn(dscb, qh) + _dot_nt(vh, dstb) * kd_ref[:, sl]
            dv = _dot_tn(scb, doh) + _dot(kdb, dstb)
            dstate_scr[h] = dst * cdec[h] + _dot_tn(qdb, doh)
            lo = h * HEAD_DIM
            dproj_ref[:, lo:lo + HEAD_DIM] = _rope_t(dq, cosf_t, sinf_t).astype(BF16)
            dproj_ref[:, RET_W + lo:RET_W + lo + HEAD_DIM] = _rope_t(dk * K_SCALE, cosf_t, sinf_t).astype(BF16)
            dproj_ref[:, 2 * RET_W + lo:2 * RET_W + lo + HEAD_DIM] = dv.astype(BF16)
        dproj_ref[:, 3 * RET_W:IN_W] = dgp_ref[...]

    tile = lambda w: pl.BlockSpec((SUPER, w), lambda i: (ns - 1 - i, 0))
    return _call(
        body, name="retention_bwd", grid=(ns,),
        in_specs=[tile(RET_W), tile(RET_W), tile(RET_W), tile(RET_W), tile(2 * RET_W),
                  pl.BlockSpec((1, HEADS, HEAD_DIM, HEAD_DIM), lambda i: (ns - 1 - i, 0, 0, 0)),
                  _whole(), _whole(), _whole(), tile(HEAD_DIM), tile(HEAD_DIM)],
        out_specs=[tile(IN_W)],
        out_shape=[jax.ShapeDtypeStruct((s, IN_W), BF16)],
        scratch_shapes=[pltpu.VMEM((HEADS, HEAD_DIM, HEAD_DIM), F32)],
        sem=("arbitrary",), operands=(q, k, v, dret, dgp, states, mask, qd, kd, cosf, sinf), riders=riders,
        after=after,
    )


def _dx(dz1, dproj, win4, ts, riders=(), after=()):
    s = dz1.shape[0]

    def body(dz1_ref, dp_ref, w_ref, dx_ref):
        acc = ALPHA * dz1_ref[...]
        for j in range(N_SHARD):
            acc = acc + _dot_nt(dp_ref[:, j * IN_SH:(j + 1) * IN_SH], w_ref[j])
        dx_ref[...] = acc

    tile = lambda w: pl.BlockSpec((ts, w), lambda i: (i, 0))
    return _call(
        body, name="dx", grid=(s // ts,),
        in_specs=[tile(D_MODEL), tile(IN_W), _whole()],
        out_specs=[tile(D_MODEL)],
        out_shape=[jax.ShapeDtypeStruct((s, D_MODEL), F32)],
        sem=("arbitrary",), operands=(dz1, dproj, win4), riders=riders, after=after,
    )


def _wgrad(a, b, tm, tn, name, stacked, m_outer, riders=(), after=()):
    s, m = a.shape
    n = b.shape[1]

    def body(a_ref, b_ref, o32_ref, o16_ref):
        res = _dot_tn(a_ref[...], b_ref[...])
        o32_ref[...] = res.reshape(o32_ref.shape)
        o16_ref[...] = res.astype(BF16).reshape(o16_ref.shape)

    if m_outer:
        grid, blocks = (m // tm, n // tn), (lambda g0, g1: (g0, g1))
    else:
        grid, blocks = (n // tn, m // tm), (lambda g0, g1: (g1, g0))
    if stacked:
        shape = (n // tn, m, tn)
        ospec = pl.BlockSpec((1, tm, tn), lambda g0, g1: (blocks(g0, g1)[1], blocks(g0, g1)[0], 0))
    else:
        shape = (m, n)
        ospec = pl.BlockSpec((tm, tn), lambda g0, g1: blocks(g0, g1))
    return _call(
        body, name=name, grid=grid,
        in_specs=[pl.BlockSpec((s, tm), lambda g0, g1: (0, blocks(g0, g1)[0])),
                  pl.BlockSpec((s, tn), lambda g0, g1: (0, blocks(g0, g1)[1]))],
        out_specs=[ospec, ospec],
        out_shape=[jax.ShapeDtypeStruct(shape, F32), jax.ShapeDtypeStruct(shape, BF16)],
        sem=("arbitrary", "arbitrary"), operands=(a, b), riders=riders, after=after,
    )


class _NoComm:
    def __init__(self, win4, wout, wup4, wdown):
        self.weights = dict(w_in=win4, w_out=wout, w_up=wup4, w_down=wdown)
        self.grads = {}

    def weight(self, name):
        return self.weights[name]

    def riders(self, call):
        return ()

    def after(self, call):
        return ()

    def landed(self, call, results, outs):
        pass

    def small_gradients(self, loss, small):
        pass

    def gradient(self, name, g32, g16):
        self.grads[name] = (g32, g16)


def _local_step(x, target, cw, cb, wpool, pscale, g1, b1, g2, b2, comm):
    s = x.shape[0]
    ts_a = min(512, s)
    ts_f = min(256, s)
    mask, qd, kd = _decay_tables()
    cosf, sinf = _rope_tables(s)
    wpool_b = wpool.astype(BF16)

    def run(call, fn, *args):
        outs, res = fn(*args, riders=comm.riders(call), after=comm.after(call))
        comm.landed(call, res, outs)
        return outs

    xb, q, k, v, g, pooled, cat = run("proj_pool", _proj_pool, x, comm.weight("w_in"), cosf, sinf, wpool_b,
                                      pscale, ts_a)
    ret, cat, states = run("retention_fwd", _retention_fwd, q, k, v, g, cat, mask, qd, kd)
    wout = comm.weight("w_out")
    xhat1, rstd1, h1b = run("outproj_ln1", _outproj_ln1, x, cat, wout, g1, b1, ts_a)
    wup4, wdown = comm.weight("w_up"), comm.weight("w_down")
    ub, act, sd, dz2, dz2b, loss, dg2, db2 = _ffn_fwd_loss(xhat1, h1b, target, wup4, wdown, cw, cb, g1, b1, g2, b2,
                                                           ts_f)

    dub, dz1, dz1b, dg1, db1, dcw, dcb = _ffn_bwd(dz2, dz2b, ub, sd, xhat1, rstd1, wup4, wdown, cw, g1, ts_f)
    half = D_MODEL // 2
    comm.gradient("w_up", *run("wgrad_up", _wgrad, h1b, dub, half, UP_SH, "wgrad_up", True, False))
    comm.gradient("w_out", *run("wgrad_out", _wgrad, cat, dz1b, D_MODEL, half, "wgrad_out", False, True))
    comm.gradient("w_down", *run("wgrad_down", _wgrad, act, dz2b, D_FF // 2, half, "wgrad_down", False, True))
    dret, dgp, dwp, dps = run("mix_bwd", _mix_bwd, dz1b, pooled, ret, g, wout, wpool_b, pscale, ts_a)
    small = dict(w_pool=dwp, pool_scale=dps, ln1_g=dg1, ln1_b=db1, conv_w=dcw, conv_b=dcb,
                 ln2_g=dg2, ln2_b=db2)
    comm.small_gradients(loss, small)
    dproj, = run("retention_bwd", _retention_bwd, q, k, v, dret, dgp, states, mask, qd, kd, cosf, sinf)
    comm.gradient("w_in", *run("wgrad_in", _wgrad, xb, dproj, D_MODEL, IN_SH, "wgrad_in", True, True))
    (grad_x,), _ = _dx(dz1, dproj, comm.weight("w_in"), ts_a, after=comm.after("dx"))
    return loss, grad_x, small


CAST_ROWS = 64
SHARD_SHAPES = ((D_MODEL, IN_SH), (OUT_SH, D_MODEL), (D_MODEL, UP_SH), (DOWN_SH, D_MODEL))
N_BIG = len(SHARD_SHAPES)
CW_PAD = (8, 768)


def _mesh_pos():
    return lax.axis_index("x"), lax.axis_index("y"), lax.axis_index("c")


def _other_chips(x, y):
    return [(1 - x, y), (x, 1 - y), (1 - x, 1 - y)]


def _half_rows(w, which):
    hr = SHARD_SHAPES[w][0] // 2
    return pl.ds(pl.multiple_of(which * hr, 16), hr)


def _gather_weights(shards, cw8, full):
    def body(*refs):
        in_refs = refs[:N_BIG]
        cw_ref = refs[N_BIG]
        out_refs = refs[N_BIG + 1:2 * N_BIG + 1]
        cwo_ref = refs[2 * N_BIG + 1]
        stage = refs[2 * N_BIG + 2:3 * N_BIG + 2]
        send_sems, recv_sems, fsend_sems, frecv_sems, cw_send, cw_recv, local_sems = refs[3 * N_BIG + 2:]
        x, y, c = _mesh_pos()
        j0 = 2 * x + y
        chips = _other_chips(x, y)

        def cast_to_stage(w):
            def cast(i, carry):
                rows = pl.ds(pl.multiple_of(i * CAST_ROWS, CAST_ROWS), CAST_ROWS)
                stage[w][rows, :] = in_refs[w][rows, :].astype(BF16)
                return carry
            lax.fori_loop(0, SHARD_SHAPES[w][0] // CAST_ROWS, cast, 0)

        for w in full:
            cast_to_stage(w)

        jx, jy, jd = 2 * (1 - x) + y, 2 * x + (1 - y), 2 * (1 - x) + (1 - y)
        neighbours = [((1 - x, y, c), jx), ((x, 1 - y, c), jy)]
        passed = jnp.where(c == 0, jx, jy)
        pass_to = (jnp.where(c == 0, x, 1 - x), jnp.where(c == 0, 1 - y, y), c)

        def nbr(w, k, block):
            return pltpu.make_async_remote_copy(
                src_ref=stage[w].at[_half_rows(w, c), :], dst_ref=out_refs[w].at[block, _half_rows(w, c), :],
                send_sem=send_sems.at[w, k], recv_sem=recv_sems.at[w, k],
                device_id=neighbours[k][0], device_id_type=MESH)

        def relay(w, block):
            return pltpu.make_async_remote_copy(
                src_ref=out_refs[w].at[passed, _half_rows(w, c), :],
                dst_ref=out_refs[w].at[block, _half_rows(w, c), :],
                send_sem=send_sems.at[w, 2], recv_sem=recv_sems.at[w, 2],
                device_id=pass_to, device_id_type=MESH)

        def d2d(w, k, block, half):
            return pltpu.make_async_remote_copy(
                src_ref=out_refs[w].at[block, _half_rows(w, half), :],
                dst_ref=out_refs[w].at[block, _half_rows(w, half), :],
                send_sem=fsend_sems.at[w, k], recv_sem=frecv_sems.at[w, k],
                device_id=(x, y, 1 - c), device_id_type=MESH)

        def conv(k, block):
            chip = chips[k]
            return pltpu.make_async_remote_copy(
                src_ref=cw_ref, dst_ref=cwo_ref.at[block], send_sem=cw_send.at[k], recv_sem=cw_recv.at[k],
                device_id=(chip[0], chip[1], c), device_id_type=MESH)

        sent = [nbr(w, k, j0) for w in full for k in range(2)] + [conv(k, j0) for k in range(3)]
        for cp in sent:
            cp.start()
        for w in range(N_BIG):
            if w not in full:
                cast_to_stage(w)
        local = [pltpu.make_async_copy(stage[w], out_refs[w].at[j0], local_sems.at[w]) for w in range(N_BIG)]
        local.append(pltpu.make_async_copy(cw_ref, cwo_ref.at[j0], local_sems.at[N_BIG]))
        for cp in local:
            cp.start()
        for w in full:
            for k, (_, block) in enumerate(neighbours):
                nbr(w, k, block).wait_recv()
            later = [relay(w, passed)] + [d2d(w, k, block, c) for k, (_, block) in enumerate(neighbours)]
            for cp in later:
                cp.start()
            sent += later
        for w in full:
            relay(w, jd).wait_recv()
            fw = d2d(w, 2, jd, c)
            fw.start()
            sent.append(fw)
        for w in full:
            for k, block in enumerate([jx, jy, jd]):
                d2d(w, k, block, 1 - c).wait_recv()
        for k, chip in enumerate(chips):
            conv(k, 2 * chip[0] + chip[1]).wait_recv()
        for cp in sent:
            cp.wait_send()
        for cp in local:
            cp.wait()

    out_shape = [jax.ShapeDtypeStruct((N_SHARD,) + shp, BF16) for shp in SHARD_SHAPES]
    out_shape.append(jax.ShapeDtypeStruct((N_SHARD,) + CW_PAD, F32))
    return pl.pallas_call(
        body, name="gather_weights",
        in_specs=[_whole()] * (N_BIG + 1),
        out_specs=[HBM_SPEC] * (N_BIG + 1),
        out_shape=out_shape,
        scratch_shapes=[pltpu.VMEM(shp, BF16) for shp in SHARD_SHAPES] + [
            pltpu.SemaphoreType.DMA((N_BIG, 3)), pltpu.SemaphoreType.DMA((N_BIG, 3)),
            pltpu.SemaphoreType.DMA((N_BIG, 3)), pltpu.SemaphoreType.DMA((N_BIG, 3)),
            pltpu.SemaphoreType.DMA((3,)), pltpu.SemaphoreType.DMA((3,)),
            pltpu.SemaphoreType.DMA((N_BIG + 1,))],
        compiler_params=pltpu.CompilerParams(vmem_limit_bytes=VMEM_LIMIT),
    )(*shards, cw8)


def _gather_rider(arrays, ops):
    ws = sorted(arrays)

    def make(inplace, srcs, lands, send_sems, recv_sems):
        del srcs, lands
        x, y, c = _mesh_pos()
        j0, jx, jy, jd = 2 * x + y, 2 * (1 - x) + y, 2 * x + (1 - y), 2 * (1 - x) + (1 - y)
        x_nbr, y_nbr, sibling = (1 - x, y, c), (x, 1 - y, c), (x, y, 1 - c)
        starts, waits = [], []
        for n, (kind, w, (r0, nr)) in enumerate(ops):
            ref = inplace[ws.index(w)]
            hr = SHARD_SHAPES[w][0] // 2
            rows = lambda core: pl.ds(pl.multiple_of(core * hr + r0, 16), nr)
            mine, theirs = rows(c), rows(1 - c)
            if kind == "ici":
                moves = [(ref.at[j0, mine, :], x_nbr, ref.at[jx, mine, :]),
                         (ref.at[j0, mine, :], y_nbr, ref.at[jy, mine, :]),
                         (ref.at[j0, mine, :], (1 - x, 1 - y, c), ref.at[jd, mine, :])]
            elif kind == "nbr":
                moves = [(ref.at[j0, mine, :], x_nbr, ref.at[jx, mine, :]),
                         (ref.at[j0, mine, :], y_nbr, ref.at[jy, mine, :])]
            elif kind == "relay":
                passed = jnp.where(c == 0, jx, jy)
                to = (jnp.where(c == 0, x, 1 - x), jnp.where(c == 0, 1 - y, y), c)
                moves = [(ref.at[passed, mine, :], to, ref.at[jd, mine, :])]
            else:
                blocks = dict(d2d=[jx, jy, jd], d2d_nbr=[jx, jy], d2d_diag=[jd])[kind]
                moves = [(ref.at[b, mine, :], sibling, ref.at[b, theirs, :]) for b in blocks]
            for k, (src, to, landing) in enumerate(moves):
                sems = dict(send_sem=send_sems.at[3 * n + k], recv_sem=recv_sems.at[3 * n + k],
                            device_id=to, device_id_type=MESH)
                send = pltpu.make_async_remote_copy(src_ref=src, dst_ref=src, **sems)
                arrival = pltpu.make_async_remote_copy(src_ref=src, dst_ref=landing, **sems)
                starts.append(send)
                waits += [arrival.wait_recv, send.wait_send]
        return starts, waits

    return _Rider([arrays[w] for w in ws], [], [], 3 * len(ops), make)


def _whole_half(w):
    return (0, SHARD_SHAPES[w][0] // 2)


def _pair_rider(ws, g16s):
    def make(inplace, srcs, lands, send_sems, recv_sems):
        del inplace
        x, y, c = _mesh_pos()
        copies = [pltpu.make_async_remote_copy(
            src_ref=srcs[i].at[:, _half_rows(w, 1 - c), :], dst_ref=lands[i],
            send_sem=send_sems.at[i], recv_sem=recv_sems.at[i], device_id=(x, y, 1 - c), device_id_type=MESH)
            for i, w in enumerate(ws)]
        return copies, [cp.wait for cp in copies]

    lands = [jax.ShapeDtypeStruct((N_SHARD, SHARD_SHAPES[w][0] // 2, SHARD_SHAPES[w][1]), BF16) for w in ws]
    return _Rider([], g16s, lands, len(ws), make)


def _chip_rider(ws, p16s, rows=None, landing=None):
    def make(inplace, srcs, lands, send_sems, recv_sems):
        x, y, c = _mesh_pos()
        dsts = inplace if landing is not None else lands
        copies = []
        for i, w in enumerate(ws):
            r0, nr = rows if rows is not None else _whole_half(w)
            for k, chip in enumerate(_other_chips(x, y)):
                copies.append(pltpu.make_async_remote_copy(
                    src_ref=srcs[i].at[2 * chip[0] + chip[1], pl.ds(r0, nr), :],
                    dst_ref=dsts[i].at[k, pl.ds(r0, nr), :],
                    send_sem=send_sems.at[3 * i + k], recv_sem=recv_sems.at[3 * i + k],
                    device_id=(chip[0], chip[1], c), device_id_type=MESH))
        return copies, [cp.wait for cp in copies]

    lands = [jax.ShapeDtypeStruct((3, SHARD_SHAPES[w][0] // 2, SHARD_SHAPES[w][1]), BF16) for w in ws]
    if landing is not None:
        return _Rider(landing, p16s, [], 3 * len(ws), make)
    return _Rider([], p16s, lands, 3 * len(ws), make)


def _final_rider(halves):
    def make(inplace, srcs, lands, send_sems, recv_sems):
        del inplace
        x, y, c = _mesh_pos()
        copies = [pltpu.make_async_remote_copy(
            src_ref=srcs[i], dst_ref=lands[i], send_sem=send_sems.at[i], recv_sem=recv_sems.at[i],
            device_id=(x, y, 1 - c), device_id_type=MESH) for i in range(len(halves))]
        return copies, [cp.wait for cp in copies]

    return _Rider([], halves, [jax.ShapeDtypeStruct(h.shape, h.dtype) for h in halves], len(halves), make)


def _comm_only(name, riders):
    _, res = _call(lambda: None, name=name, grid=(), in_specs=[], out_specs=[], out_shape=[], operands=(),
                   riders=riders)
    return res


class _SemList:
    def __init__(self, refs):
        self.at = list(refs)


def _merged_rider(riders):
    srcs = [a for r in riders for a in r.srcs]
    lands = [a for r in riders for a in r.lands]

    def make(inplace, src_refs, land_refs, send_sems, recv_sems):
        starts, waits = [], []
        s0 = l0 = c0 = 0
        for r in riders:
            part = r.make(inplace, src_refs[s0:s0 + len(r.srcs)], land_refs[l0:l0 + len(r.lands)],
                          _SemList(send_sems.at[c0:c0 + r.n_copies]), _SemList(recv_sems.at[c0:c0 + r.n_copies]))
            starts += part[0]
            waits += part[1]
            s0, l0, c0 = s0 + len(r.srcs), l0 + len(r.lands), c0 + r.n_copies
        return starts, waits

    return _Rider([], srcs, lands, sum(r.n_copies for r in riders), make)


def _split_start(name, rider):
    assert not rider.inplace
    ns, nl, n = len(rider.srcs), len(rider.lands), rider.n_copies

    def body(*refs):
        srcs, lands = refs[:ns], refs[ns:ns + nl]
        sems = refs[ns + nl:ns + nl + 2 * n]
        token = refs[-1]
        starts, _ = rider.make([], srcs, lands, _SemList(sems[:n]), _SemList(sems[n:]))
        for cp in starts:
            cp.start()
        token[...] = jnp.zeros_like(token)

    buffers = [pltpu.with_memory_space_constraint(a, pltpu.HBM) for a in rider.srcs]
    buffers += [pltpu.with_memory_space_constraint(lax.empty(s.shape, s.dtype), pltpu.HBM) for s in rider.lands]
    hbm = pl.BlockSpec(memory_space=pltpu.HBM)
    sem = pl.BlockSpec(memory_space=pltpu.SEMAPHORE)
    outs = pl.pallas_call(
        body, name=name,
        out_shape=tuple([pltpu.SemaphoreType.DMA(())] * (2 * n) + [pltpu.HBM(b.shape, b.dtype) for b in buffers]
                        + [jax.ShapeDtypeStruct((8, 128), F32)]),
        in_specs=[hbm] * (ns + nl),
        out_specs=tuple([sem] * (2 * n) + [hbm] * (ns + nl) + [_whole()]),
        input_output_aliases={i: 2 * n + i for i in range(ns + nl)},
        compiler_params=pltpu.CompilerParams(has_side_effects=pltpu.SideEffectType.DATAFLOW_SIDE_EFFECTING),
    )(*buffers)
    return (rider, outs[:2 * n], outs[2 * n:2 * n + ns + nl]), outs[-1]


def _split_parts(state, riders):
    merged, sems, buffers = state
    n, ns = merged.n_copies, len(merged.srcs)
    parts, s0, l0, c0 = [], 0, 0, 0
    for r in riders:
        parts.append((r, list(sems[c0:c0 + r.n_copies]) + list(sems[n + c0:n + c0 + r.n_copies]),
                      list(buffers[s0:s0 + len(r.srcs)]) + list(buffers[ns + l0:ns + l0 + len(r.lands)])))
        s0, l0, c0 = s0 + len(r.srcs), l0 + len(r.lands), c0 + r.n_copies
    return parts


def _split_wait(name, state, after):
    rider, sems, buffers = state
    ns, nl, n = len(rider.srcs), len(rider.lands), rider.n_copies

    def body(*refs):
        srcs, lands = refs[:ns], refs[ns:ns + nl]
        sem_refs = refs[ns + nl:ns + nl + 2 * n]
        _, waits = rider.make([], srcs, lands, _SemList(sem_refs[:n]), _SemList(sem_refs[n:]))
        for wait in waits:
            wait()

    hbm = pl.BlockSpec(memory_space=pltpu.HBM)
    sem = pl.BlockSpec(memory_space=pltpu.SEMAPHORE)
    outs = pl.pallas_call(
        body, name=name,
        out_shape=tuple(pltpu.HBM(b.shape, b.dtype) for b in buffers),
        in_specs=[hbm] * (ns + nl) + [sem] * (2 * n) + [HBM_SPEC],
        out_specs=tuple([hbm] * (ns + nl)),
        input_output_aliases={i: i for i in range(ns + nl)},
        compiler_params=pltpu.CompilerParams(has_side_effects=pltpu.SideEffectType.DATAFLOW_SIDE_EFFECTING),
    )(*buffers, *sems, after)
    return list(outs[:ns]), list(outs[ns:])


def _pair_sum(pos, ws, g32s, recvs):
    n = len(ws)

    def body(pos_ref, *refs):
        g_refs, r_refs = refs[:n], refs[n:2 * n]
        p32_refs, p16_refs = refs[2 * n:3 * n], refs[3 * n:]
        for i in range(n):
            tot = g_refs[i][...] + r_refs[i][...].astype(F32)
            p16_refs[i][...] = tot.astype(BF16)

            @pl.when(pl.program_id(0) == pos_ref[1])
            def _(i=i, tot=tot):
                p32_refs[i][...] = tot

    halves = [(SHARD_SHAPES[w][0] // 2, SHARD_SHAPES[w][1]) for w in ws]
    own = [pl.BlockSpec((None, None) + h, lambda j, pos_ref: (j, pos_ref[0], 0, 0)) for h in halves]
    blk = [pl.BlockSpec((None,) + h, lambda j, pos_ref: (j, 0, 0)) for h in halves]
    mine = [pl.BlockSpec(h, lambda j, pos_ref: (0, 0)) for h in halves]
    g4 = [g.reshape((N_SHARD, 2) + h) for g, h in zip(g32s, halves)]
    outs = pl.pallas_call(
        body, name="pair_sum_" + "_".join(str(w) for w in ws),
        grid_spec=pltpu.PrefetchScalarGridSpec(
            num_scalar_prefetch=1, grid=(N_SHARD,), in_specs=own + blk, out_specs=mine + blk),
        out_shape=[jax.ShapeDtypeStruct(h, F32) for h in halves]
        + [jax.ShapeDtypeStruct((N_SHARD,) + h, BF16) for h in halves],
        compiler_params=_params(("arbitrary",)),
    )(pos, *g4, *recvs)
    return outs[:n], outs[n:]


def _chip_sum(pos, p32s, recvs):
    parts = 2

    def body(pos_ref, *refs):
        del pos_ref
        p_refs, r_refs, f_refs = refs[:N_BIG], refs[N_BIG:2 * N_BIG], refs[2 * N_BIG:]
        for w in range(N_BIG):
            f_refs[w][...] = ((p_refs[w][...] + r_refs[w][0].astype(F32)) + r_refs[w][1].astype(F32)) \
                + r_refs[w][2].astype(F32)

    quarters = [(r // 2 // parts, cc) for r, cc in SHARD_SHAPES]
    own = [pl.BlockSpec(qt, lambda i, pos_ref: (i, 0)) for qt in quarters]
    rcv = [pl.BlockSpec((3,) + qt, lambda i, pos_ref: (0, i, 0)) for qt in quarters]
    out = [pl.BlockSpec(qt, lambda i, pos_ref: (i, 0)) for qt in quarters]
    return pl.pallas_call(
        body, name="chip_sum",
        grid_spec=pltpu.PrefetchScalarGridSpec(
            num_scalar_prefetch=1, grid=(parts,), in_specs=own + rcv, out_specs=out),
        out_shape=[jax.ShapeDtypeStruct((r // 2, cc), F32) for r, cc in SHARD_SHAPES],
        compiler_params=_params(("arbitrary",)),
    )(pos, *p32s, *recvs)


def _adamw(w, g, m, v):
    m_new = ADAM_B1 * m + (1.0 - ADAM_B1) * g
    v_new = ADAM_B2 * v + (1.0 - ADAM_B2) * (g * g)
    m_hat = m_new / (1.0 - ADAM_B1 ** ADAM_STEP)
    v_hat = v_new / (1.0 - ADAM_B2 ** ADAM_STEP)
    delta = -ADAM_LR * (m_hat / (jnp.sqrt(v_hat) + ADAM_EPS) + ADAM_WD * w)
    return delta, m_new, v_new


def _adam_half(name, which, grads, ws, ms, vs, into=None):
    nb = 4

    def body(which_ref, *refs):
        del which_ref
        groups = [refs[i * N_BIG:(i + 1) * N_BIG] for i in range(4)]
        g_refs, w_refs, m_refs, v_refs = groups
        go_refs, do_refs, mo_refs, vo_refs = [refs[len(refs) - (4 - i) * N_BIG:len(refs) - (3 - i) * N_BIG]
                                              for i in range(4)]
        for w in range(N_BIG):
            g = g_refs[w][...]
            delta, m_new, v_new = _adamw(w_refs[w][...], g, m_refs[w][...], v_refs[w][...])
            go_refs[w][...] = g
            do_refs[w][...] = delta
            mo_refs[w][...] = m_new
            vo_refs[w][...] = v_new

    blocks = [(r // 2 // nb, cc) for r, cc in SHARD_SHAPES]
    half = [pl.BlockSpec(b, lambda i, which_ref: (i, 0)) for b in blocks]
    full = [pl.BlockSpec((None,) + b, lambda i, which_ref: (0, which_ref[0] * nb + i, 0)) for b in blocks]
    shapes = [jax.ShapeDtypeStruct((1,) + shp, F32) for shp in SHARD_SHAPES]
    carried = [] if into is None else [a for kind in into for a in kind]
    first = 1 + 4 * N_BIG
    outs = pl.pallas_call(
        body, name=name,
        grid_spec=pltpu.PrefetchScalarGridSpec(
            num_scalar_prefetch=1, grid=(nb,), in_specs=half + full * 3 + [HBM_SPEC] * len(carried),
            out_specs=full * 4),
        out_shape=shapes * 4,
        input_output_aliases={first + i: i for i in range(len(carried))},
        compiler_params=_params(("arbitrary",)),
    )(which, *grads, *ws, *ms, *vs, *carried)
    return [outs[i * N_BIG:(i + 1) * N_BIG] for i in range(4)]


SMALL_ROWS = 8
ROW_CONV_B, ROW_POOL_SCALE, ROW_LN1_G, ROW_LN1_B, ROW_LN2_G, ROW_LN2_B, ROW_LOSS = range(7)
SMALL_VECS = ((ROW_CONV_B, D_FF), (ROW_POOL_SCALE, POOL_W), (ROW_LN1_G, D_MODEL), (ROW_LN1_B, D_MODEL),
              (ROW_LN2_G, D_MODEL), (ROW_LN2_B, D_MODEL))


def _small_pack(loss, vec_grads):
    def body(*refs):
        loss_ref, gvec, out_ref = refs[0], refs[1:-1], refs[-1]
        out_ref[...] = jnp.zeros_like(out_ref)
        for (row, n), ref in zip(SMALL_VECS, gvec):
            out_ref[row:row + 1, 0:n] = ref[...]
        out_ref[ROW_LOSS:ROW_LOSS + 1, 0:HEAD_DIM] = jnp.broadcast_to(loss_ref[...], (1, HEAD_DIM))

    return pl.pallas_call(
        body, name="small_pack", in_specs=[_whole()] * (1 + len(vec_grads)), out_specs=_whole(),
        out_shape=jax.ShapeDtypeStruct((SMALL_ROWS, D_FF), F32),
    )(loss, *vec_grads)


def _small_pair_sum(own, sibling):
    n = len(own)

    def body(*refs):
        x, y, _ = _mesh_pos()
        for i in range(n):
            refs[2 * n + i][2 * x + y] = refs[i][...] + refs[n + i][...]

    return pl.pallas_call(
        body, name="small_pair_sum", in_specs=[_whole()] * (2 * n), out_specs=[_whole()] * n,
        out_shape=[jax.ShapeDtypeStruct((N_SHARD,) + a.shape, F32) for a in own],
        compiler_params=pltpu.CompilerParams(vmem_limit_bytes=VMEM_LIMIT),
    )(*own, *sibling)


def _small_chip_rider(gathered):
    n = len(gathered)

    def make(inplace, srcs, lands, send_sems, recv_sems):
        del inplace, lands
        x, y, c = _mesh_pos()
        j0 = 2 * x + y
        starts, waits = [], []
        for i in range(n):
            for k, chip in enumerate(_other_chips(x, y)):
                sems = dict(send_sem=send_sems.at[3 * i + k], recv_sem=recv_sems.at[3 * i + k],
                            device_id=(chip[0], chip[1], c), device_id_type=MESH)
                send = pltpu.make_async_remote_copy(src_ref=srcs[i].at[j0], dst_ref=srcs[i].at[j0], **sems)
                arrival = pltpu.make_async_remote_copy(
                    src_ref=srcs[i].at[j0], dst_ref=srcs[i].at[2 * chip[0] + chip[1]], **sems)
                starts.append(send)
                waits += [arrival.wait_recv, send.wait_send]
        return starts, waits

    return _Rider([], gathered, [], 3 * n, make)


def _small_adam(all_a, all_b, all_c, wp, cwp, vec_ws, m_wp, m_cwp, vec_ms, v_wp, v_cwp, vec_vs):
    nv = len(SMALL_VECS)
    np_ = 2 + nv

    def body(*refs):
        all_a_ref, all_b_ref, all_c_ref = refs[0:3]
        w_all, m_all, v_all = (refs[3 + i * np_:3 + (i + 1) * np_] for i in range(3))
        loss_out = refs[3 + 3 * np_]
        outs = refs[4 + 3 * np_:]
        x, y, _ = _mesh_pos()
        j0 = 2 * x + y
        tot_a = ((all_a_ref[0] + all_a_ref[1]) + all_a_ref[2]) + all_a_ref[3]
        tot_b = ((all_b_ref[0] + all_b_ref[1]) + all_b_ref[2]) + all_b_ref[3]
        tot_c = ((all_c_ref[0, j0] + all_c_ref[1, j0]) + all_c_ref[2, j0]) + all_c_ref[3, j0]
        loss_out[...] = tot_b[ROW_LOSS:ROW_LOSS + 1, 0:1]
        grads = [tot_a, tot_c] + [tot_b[row:row + 1, 0:n] for row, n in SMALL_VECS]
        for p in range(np_):
            delta, m_new, v_new = _adamw(w_all[p][...], grads[p], m_all[p][...], v_all[p][...])
            outs[p][...] = grads[p]
            outs[np_ + p][...] = delta
            outs[2 * np_ + p][...] = m_new
            outs[3 * np_ + p][...] = v_new

    pshapes = [wp.shape, CW_PAD] + [wv.shape for wv in vec_ws]
    out_shape = [jax.ShapeDtypeStruct((1, 1), F32)] + [jax.ShapeDtypeStruct(s, F32) for s in pshapes] * 4
    outs = pl.pallas_call(
        body, name="small_adam",
        in_specs=[_whole()] * (3 + 3 * np_), out_specs=[_whole()] * len(out_shape), out_shape=out_shape,
        compiler_params=pltpu.CompilerParams(vmem_limit_bytes=VMEM_LIMIT),
    )(all_a, all_b, all_c, wp, cwp, *vec_ws, m_wp, m_cwp, *vec_ms, v_wp, v_cwp, *vec_vs)
    return outs[0], [outs[1 + i * np_:1 + (i + 1) * np_] for i in range(4)]


def _pad_cw(a):
    pad = [(0, 0)] * (a.ndim - 2) + [(0, CW_PAD[0] - a.shape[-2]), (0, CW_PAD[1] - a.shape[-1])]
    return jnp.pad(a, pad)


def kernel(x, w_in, w_pool, pool_scale, w_out, ln1_g, ln1_b, w_up, conv_w, conv_b, w_down, ln2_g, ln2_b, loss_target, m_w_in, m_w_pool, m_pool_scale, m_w_out, m_ln1_g, m_ln1_b, m_w_up, m_conv_w, m_conv_b, m_w_down, m_ln2_g, m_ln2_b, v_w_in, v_w_pool, v_pool_scale, v_w_out, v_ln1_g, v_ln1_b, v_w_up, v_conv_w, v_conv_b, v_w_down, v_ln2_g, v_ln2_b):
    pos = jnp.stack([lax.axis_index("c"), 2 * lax.axis_index("x") + lax.axis_index("y")]).astype(jnp.int32)
    order = ("w_in", "w_out", "w_up", "w_down")
    w_in_i, w_out_i, w_up_i, w_down_i = range(N_BIG)
    vec_names = ("conv_b", "pool_scale", "ln1_g", "ln1_b", "ln2_g", "ln2_b")

    gathered = _gather_weights([w_in[0], w_out[0], w_up[0], w_down[0]], _pad_cw(conv_w[0]), (w_in_i,))
    cw_full = jnp.transpose(gathered[N_BIG][:, 0:3, 0:DOWN_SH], (1, 0, 2)).reshape(3, D_FF)
    up_a, up_b, up_c = (0, 224), (224, 160), (384, 128)
    assert up_c[0] + up_c[1] == SHARD_SHAPES[w_up_i][0] // 2

    class MeshComm:
        def __init__(self):
            self.w = {i: gathered[i] for i in range(N_BIG)}
            self.g32, self.g16, self.p32, self.p16, self.recv_b = {}, {}, {}, {}, {}
            self.up_complete = False
            self.tokens, self.chips = {}, []

        def weight(self, name):
            i = order.index(name)
            if name == "w_up" and not self.up_complete:
                (arrs, _), = _comm_only("gather_up_last", [_gather_rider(
                    {i: self.w[i]}, [("d2d_diag", i, up_b), ("d2d", i, up_c)])])
                self.w[i], self.up_complete = arrs[0], True
            full = self.w[i]
            return full.reshape(-1, full.shape[-1]) if name in ("w_out", "w_down") else full

        def _gather(self, ws, ops):
            return _gather_rider({w: self.w[w] for w in ws}, ops), ("w", ws)

        def _pair(self, ws):
            return _pair_rider(ws, [self.g16[w] for w in ws]), ("recv_a", ws)

        def _chip(self, ws, rows=None, resume=False):
            landing = [self.recv_b[w] for w in ws] if resume else None
            return _chip_rider(ws, [self.p16[w] for w in ws], rows, landing), ("recv_b", ws)

        def plan(self, call):
            out_all, down_all = _whole_half(w_out_i), _whole_half(w_down_i)
            if call == "proj_pool":
                return [self._gather([w_out_i, w_up_i, w_down_i],
                                     [("ici", w_out_i, out_all), ("nbr", w_down_i, down_all),
                                      ("nbr", w_up_i, up_a)])]
            if call == "retention_fwd":
                return [self._gather([w_out_i, w_up_i, w_down_i],
                                     [("d2d", w_out_i, out_all),
                                      ("relay", w_down_i, down_all), ("d2d_nbr", w_down_i, down_all),
                                      ("relay", w_up_i, up_a), ("d2d_nbr", w_up_i, up_a), ("nbr", w_up_i, up_b)])]
            if call == "outproj_ln1":
                return [self._gather([w_up_i, w_down_i],
                                     [("d2d_diag", w_down_i, down_all), ("d2d_diag", w_up_i, up_a),
                                      ("relay", w_up_i, up_b), ("d2d_nbr", w_up_i, up_b), ("ici", w_up_i, up_c)])]
            return []

        def after(self, call):
            return tuple(self.tokens.pop(call, ()))

        def riders(self, call):
            self.pending = self.plan(call)
            return [r for r, _ in self.pending]

        def _start(self, name, rider, before):
            state, token = _split_start(name, rider)
            self.tokens.setdefault(before, []).append(token)
            return state

        def _finish_pair(self, name, state, ws, after):
            _, lands = _split_wait(name, state, after)
            self._finish_sum(ws, lands)

        def landed(self, call, results, outs):
            for (_, (slot, ws)), (inplace, lands) in zip(self.pending, results):
                for w, arr in zip(ws, inplace if len(inplace) else lands):
                    getattr(self, slot)[w] = arr
            if call == "wgrad_out":
                self._finish_pair("pair_exchange_up_wait", self.pair_up, [w_up_i], outs[1])
                self.chips.append(([w_up_i], self._start(
                    "chip_exchange_up_start", self._chip([w_up_i])[0], "wgrad_down")))
            if call == "mix_bwd":
                ws = [w_out_i, w_down_i]
                self._finish_pair("pair_exchange_out_down_wait", self.pair_out_down, ws, outs[0])
            if call == "retention_bwd":
                own, sibling = _split_wait("small_pair_wait", self.small_pair, outs[0])
                self.small_sums = _small_pair_sum(own, sibling)

        def small_gradients(self, loss, small):
            dcw4 = _pad_cw(jnp.transpose(small["conv_w"].reshape(3, N_SHARD, DOWN_SH), (1, 0, 2)))
            own = [small["w_pool"], _small_pack(loss, [small[n] for n in vec_names]), dcw4]
            ws = [w_out_i, w_down_i]
            parts = [self._chip(ws)[0], _final_rider(own)]
            chip, self.small_pair = _split_parts(
                self._start("chip_out_down_small_pair_start", _merged_rider(parts), "retention_bwd"), parts)
            self.chips.append((ws, chip))

        def gradient(self, name, g32, g16):
            w = order.index(name)
            shape = (N_SHARD,) + SHARD_SHAPES[w]
            self.g32[w], self.g16[w] = g32.reshape(shape), g16.reshape(shape)
            if name == "w_up":
                self.pair_up = self._start("pair_exchange_up_start", self._pair([w])[0], "wgrad_out")
            if name == "w_down":
                self.pair_out_down = self._start("pair_exchange_out_down_start",
                                                 self._pair([w_out_i, w_down_i])[0], "mix_bwd")
            if name == "w_in":
                (_, lands), = _comm_only("pair_exchange_in", [self._pair([w])[0]])
                self._finish_sum([w], lands)
                parts = [self._chip([w])[0], _small_chip_rider(self.small_sums)]
                chip, self.small_chip = _split_parts(
                    self._start("chip_in_small_chip_start", _merged_rider(parts), "dx"), parts)
                self.chips.append(([w], chip))

        def _finish_sum(self, ws, lands):
            p32s, p16s = _pair_sum(pos, ws, [self.g32[w] for w in ws], lands)
            for w, p32, p16 in zip(ws, p32s, p16s):
                self.p32[w], self.p16[w] = p32, p16

        def finish(self, after):
            for n, (ws, state) in enumerate(self.chips):
                _, lands = _split_wait("chip_exchange_wait_%d" % n, state, after)
                for w, arr in zip(ws, lands):
                    self.recv_b[w] = arr
            return _split_wait("small_chip_wait", self.small_chip, after)[0]

    comm = MeshComm()
    loss, grad_x, small = _local_step(x[0], loss_target[0], cw_full, conv_b, w_pool[0], pool_scale,
                                      ln1_g, ln1_b, ln2_g, ln2_b, comm)

    given = dict(w_pool=w_pool, pool_scale=pool_scale, ln1_g=ln1_g, ln1_b=ln1_b, conv_w=conv_w, conv_b=conv_b,
                 ln2_g=ln2_g, ln2_b=ln2_b)
    given_m = dict(w_pool=m_w_pool, pool_scale=m_pool_scale, ln1_g=m_ln1_g, ln1_b=m_ln1_b, conv_w=m_conv_w,
                   conv_b=m_conv_b, ln2_g=m_ln2_g, ln2_b=m_ln2_b)
    given_v = dict(w_pool=v_w_pool, pool_scale=v_pool_scale, ln1_g=v_ln1_g, ln1_b=v_ln1_b, conv_w=v_conv_w,
                   conv_b=v_conv_b, ln2_g=v_ln2_g, ln2_b=v_ln2_b)
    args = []
    for src in (given, given_m, given_v):
        args += [src["w_pool"][0], _pad_cw(src["conv_w"][0]), [src[n] for n in vec_names]]
    small_sums = comm.finish(grad_x)
    loss_tot, small_out = _small_adam(*small_sums, *args)
    every = range(N_BIG)
    mine = _chip_sum(pos, [comm.p32[w] for w in every], [comm.recv_b[w] for w in every])
    final_state, _ = _split_start("pair_exchange_f32_start", _final_rider(mine))
    mine = final_state[2][:N_BIG]
    big = ([w_in, w_out, w_up, w_down], [m_w_in, m_w_out, m_w_up, m_w_down], [v_w_in, v_w_out, v_w_up, v_w_down])
    own_half = _adam_half("adam_own_half", pos[0:1], mine, *big)
    _, theirs = _split_wait("pair_exchange_f32_wait", final_state, own_half[0][0])
    big_out = _adam_half("adam_other_half", 1 - pos[0:1], theirs, *big, into=own_half)

    names = ("w_in", "w_pool", "pool_scale", "w_out", "ln1_g", "ln1_b", "w_up", "conv_w", "conv_b", "w_down",
             "ln2_g", "ln2_b")
    small_names = ("w_pool", "conv_w") + vec_names
    result = [loss_tot.reshape(()), grad_x[None]]
    for kind in range(4):
        for n in names:
            if n in order:
                result.append(big_out[kind][order.index(n)])
            else:
                val = small_out[kind][small_names.index(n)]
                if n == "conv_w":
                    val = val[0:3, 0:DOWN_SH][None]
                elif n == "w_pool":
                    val = val[None]
                result.append(val)
    return tuple(result)
```

```python
import functools
import math

import numpy as np
import jax
import jax.numpy as jnp
from jax import lax
from jax.experimental import pallas as pl
from jax.experimental.pallas import tpu as pltpu

F32 = jnp.float32
BF16 = jnp.bfloat16

D_MODEL = 1024
HEADS = 4
HEAD_DIM = 128
RET_W = HEADS * HEAD_DIM
POOL_WINDOWS = (2, 4, 8, 16)
POOL_W = 512
IN_W = 4 * RET_W + POOL_W
D_FF = 2816
N_SHARD = 4
IN_SH = IN_W // N_SHARD
UP_SH = 2 * D_FF // N_SHARD
DOWN_SH = D_FF // N_SHARD
OUT_SH = D_MODEL // N_SHARD
ROPE_BASE = 10000.0
LN_EPS = 1e-5
RMS_EPS = 1e-6
ALPHA = 2.0 ** 0.25
K_SCALE = HEAD_DIM ** -0.5
SUPER = 256
CHUNK = 64
POOL_HALO = 16
CONV_HALO = 8
FFN_STRIP = 128
LN_ROWS = 32

ADAM_LR = 0.001
ADAM_B1 = 0.9
ADAM_B2 = 0.999
ADAM_EPS = 1e-08
ADAM_WD = 0.01
ADAM_STEP = 10

MESH = pl.DeviceIdType.MESH
VMEM_LIMIT = 56 * 1024 * 1024


def _dot(a, b):
    return jnp.dot(a, b, preferred_element_type=F32)


def _dot_nt(a, b):
    return lax.dot_general(a, b, (((1,), (1,)), ((), ())), preferred_element_type=F32)


def _dot_tn(a, b):
    return lax.dot_general(a, b, (((0,), (0,)), ((), ())), preferred_element_type=F32)


def _sigmoid(x):
    return 1.0 / (1.0 + jnp.exp(-x))


def _params(sem):
    return pltpu.CompilerParams(dimension_semantics=sem, vmem_limit_bytes=VMEM_LIMIT)


def _whole():
    return pl.BlockSpec(memory_space=pltpu.VMEM)


HBM_SPEC = pl.BlockSpec(memory_space=pl.ANY)


class _Rider:
    def __init__(self, inplace, srcs, lands, n_copies, make):
        self.inplace, self.srcs, self.lands, self.n_copies, self.make = list(inplace), list(srcs), list(lands), n_copies, make


def _call(body, *, name, grid, in_specs, out_specs, out_shape, operands, scratch_shapes=(), sem=(),
          aliases=None, riders=(), after=()):
    n_in, n_out, n_scr = len(in_specs), len(out_shape), len(scratch_shapes)
    in_specs, out_specs, out_shape = list(in_specs), list(out_specs), list(out_shape)
    operands, scratch_shapes, aliases = list(operands), list(scratch_shapes), dict(aliases or {})
    in_specs += [_whole()] * len(after)
    operands += list(after)
    for r in riders:
        for a in r.inplace:
            aliases[len(in_specs)] = len(out_shape)
            in_specs.append(HBM_SPEC)
            operands.append(a)
            out_specs.append(HBM_SPEC)
            out_shape.append(jax.ShapeDtypeStruct(a.shape, a.dtype))
        for a in r.srcs:
            in_specs.append(HBM_SPEC)
            operands.append(a)
        for shp in r.lands:
            out_specs.append(HBM_SPEC)
            out_shape.append(shp)
        scratch_shapes += [pltpu.SemaphoreType.DMA((r.n_copies,)), pltpu.SemaphoreType.DMA((r.n_copies,))]

    def full(*refs):
        ins = refs[:n_in]
        at = n_in + len(after)
        r_srcs = []
        for r in riders:
            at += len(r.inplace)
            r_srcs.append(refs[at:at + len(r.srcs)])
            at += len(r.srcs)
        outs = refs[at:at + n_out]
        at += n_out
        r_outs = []
        for r in riders:
            r_outs.append((refs[at:at + len(r.inplace)], refs[at + len(r.inplace):at + len(r.inplace) + len(r.lands)]))
            at += len(r.inplace) + len(r.lands)
        scr = refs[at:at + n_scr]
        at += n_scr
        r_sems = [refs[at + 2 * i:at + 2 * i + 2] for i in range(len(riders))]

        def copies():
            return [r.make(r_outs[i][0], r_srcs[i], r_outs[i][1], r_sems[i][0], r_sems[i][1])
                    for i, r in enumerate(riders)]

        def start():
            for starts, _ in copies():
                for cp in starts:
                    cp.start()

        def finish():
            for _, waits in copies():
                for wait in waits:
                    wait()

        if riders and grid:
            first = functools.reduce(jnp.logical_and, [pl.program_id(d) == 0 for d in range(len(grid))])
            last = functools.reduce(jnp.logical_and, [pl.program_id(d) == grid[d] - 1 for d in range(len(grid))])
            pl.when(first)(start)
            body(*ins, *outs, *scr)
            pl.when(last)(finish)
        else:
            if riders:
                start()
            body(*ins, *outs, *scr)
            if riders:
                finish()

    params = _params(sem) if grid else pltpu.CompilerParams(vmem_limit_bytes=VMEM_LIMIT)
    res = pl.pallas_call(
        full, name=name, grid=grid, in_specs=in_specs, out_specs=out_specs, out_shape=out_shape,
        scratch_shapes=scratch_shapes, input_output_aliases=aliases, compiler_params=params,
    )(*operands)
    outs, at, rider_res = res[:n_out], n_out, []
    for r in riders:
        rider_res.append((res[at:at + len(r.inplace)], res[at + len(r.inplace):at + len(r.inplace) + len(r.lands)]))
        at += len(r.inplace) + len(r.lands)
    return list(outs), rider_res


def _gammas():
    return [1.0 - 2.0 ** (-5.0 - h) for h in range(HEADS)]


def _decay_tables():
    idx = np.arange(SUPER)
    dist = np.abs(idx[:, None] - idx[None, :]).astype(np.float64)
    visible = (idx[None, :] // CHUNK) <= (idx[:, None] // CHUNK)
    mask = np.stack([np.where(visible, g ** dist, 0.0) for g in _gammas()])
    qd = np.concatenate([np.repeat((g ** (idx + 1.0))[:, None], HEAD_DIM, 1) for g in _gammas()], 1)
    kd = np.concatenate([np.repeat((g ** (SUPER - 1.0 - idx))[:, None], HEAD_DIM, 1) for g in _gammas()], 1)
    return (jnp.asarray(mask, F32), jnp.asarray(qd, F32), jnp.asarray(kd, F32))


def _rope_tables(s):
    inv_freq = ROPE_BASE ** (-np.arange(0, HEAD_DIM, 2, dtype=np.float64) / HEAD_DIM)
    ang = np.arange(s, dtype=np.float64)[:, None] * inv_freq[None, :]
    cos, sin = np.cos(ang), np.sin(ang)
    return (jnp.asarray(np.concatenate([cos, cos], 1), F32),
            jnp.asarray(np.concatenate([-sin, sin], 1), F32))


def _rope(t, cosf, sinf):
    return t * cosf + pltpu.roll(t, HEAD_DIM // 2, 1) * sinf


def _rope_t(t, cosf, sinf):
    return t * cosf - pltpu.roll(t, HEAD_DIM // 2, 1) * sinf


def _layernorm_fwd(z):
    mu = jnp.mean(z, axis=-1, keepdims=True)
    zc = z - mu
    var = jnp.mean(zc * zc, axis=-1, keepdims=True)
    rstd = lax.rsqrt(var + LN_EPS)
    return zc * rstd, rstd


def _layernorm_bwd(dy, xhat, rstd, gain):
    dxh = dy * gain
    m1 = jnp.mean(dxh, axis=-1, keepdims=True)
    m2 = jnp.mean(dxh * xhat, axis=-1, keepdims=True)
    return rstd * (dxh - m1 - xhat * m2)


def _proj_pool(x, win4, cosf, sinf, wpool, pscale, ts, riders=(), after=()):
    s = x.shape[0]
    nt = s // ts

    def body(x_ref, w_ref, cos_ref, sin_ref, wp_ref, ps_ref,
             xb_ref, q_ref, k_ref, v_ref, g_ref, pooled_ref, cat_ref, proj_scr, pext_scr):
        i = pl.program_id(0)
        xb = x_ref[...].astype(BF16)
        xb_ref[...] = xb
        for j in range(N_SHARD):
            proj_scr[:, j * IN_SH:(j + 1) * IN_SH] = _dot(xb, w_ref[j])
        cosf_t = cos_ref[...]
        sinf_t = sin_ref[...]
        for h in range(HEADS):
            lo = h * HEAD_DIM
            q_ref[:, lo:lo + HEAD_DIM] = _rope(proj_scr[:, lo:lo + HEAD_DIM], cosf_t, sinf_t).astype(BF16)
            kk = _rope(proj_scr[:, RET_W + lo:RET_W + lo + HEAD_DIM], cosf_t, sinf_t) * K_SCALE
            k_ref[:, lo:lo + HEAD_DIM] = kk.astype(BF16)
        v_ref[...] = proj_scr[:, 2 * RET_W:3 * RET_W].astype(BF16)
        g_ref[...] = proj_scr[:, 3 * RET_W:4 * RET_W]

        @pl.when(i == 0)
        def _():
            pext_scr[0:POOL_HALO, :] = jnp.zeros((POOL_HALO, POOL_W), F32)

        pext_scr[POOL_HALO:POOL_HALO + ts, :] = proj_scr[:, 4 * RET_W:IN_W]
        pos = (i * ts + lax.broadcasted_iota(jnp.int32, (ts, 1), 0) + 1).astype(F32)
        for gi, w in enumerate(POOL_WINDOWS):
            lo = gi * HEAD_DIM
            ext = pext_scr[:, lo:lo + HEAD_DIM]
            acc = ext
            shift = 1
            while shift < w:
                acc = acc + pltpu.roll(acc, shift, 0)
                shift *= 2
            tok = ext[POOL_HALO:POOL_HALO + ts]
            pooled = acc[POOL_HALO:POOL_HALO + ts] / jnp.minimum(pos, float(w)) - tok
            pooled_b = pooled.astype(BF16)
            pooled_ref[:, lo:lo + HEAD_DIM] = pooled_b
            lin = _dot(pooled_b, wp_ref[gi])
            cat_ref[:, lo:lo + HEAD_DIM] = (lin * ps_ref[:, lo:lo + HEAD_DIM]).astype(BF16)
        pext_scr[0:POOL_HALO, :] = pext_scr[ts:ts + POOL_HALO, :]

    tile = lambda w: pl.BlockSpec((ts, w), lambda i: (i, 0))
    return _call(
        body, name="proj_pool", grid=(nt,),
        in_specs=[tile(D_MODEL), _whole(), tile(HEAD_DIM), tile(HEAD_DIM), _whole(), _whole()],
        out_specs=[tile(D_MODEL), tile(RET_W), tile(RET_W), tile(RET_W), tile(RET_W), tile(POOL_W),
                   pl.BlockSpec((ts, POOL_W), lambda i: (i, 1))],
        out_shape=[jax.ShapeDtypeStruct((s, D_MODEL), BF16), jax.ShapeDtypeStruct((s, RET_W), BF16),
                   jax.ShapeDtypeStruct((s, RET_W), BF16), jax.ShapeDtypeStruct((s, RET_W), BF16),
                   jax.ShapeDtypeStruct((s, RET_W), F32), jax.ShapeDtypeStruct((s, POOL_W), BF16),
                   jax.ShapeDtypeStruct((s, 2 * RET_W), BF16)],
        scratch_shapes=[pltpu.VMEM((ts, IN_W), F32), pltpu.VMEM((ts + POOL_HALO, POOL_W), F32)],
        sem=("arbitrary",), operands=(x, win4, cosf, sinf, wpool, pscale), riders=riders, after=after,
    )


def _retention_fwd(q, k, v, g, cat, mask, qd, kd, riders=(), after=()):
    s = q.shape[0]
    ns = s // SUPER
    cdec = [gm ** float(SUPER) for gm in _gammas()]

    def body(q_ref, k_ref, v_ref, g_ref, cat_in, mask_ref, qd_ref, kd_ref,
             ret_ref, cat_ref, st_ref, state_scr):
        del cat_in
        n = pl.program_id(0)

        @pl.when(n == 0)
        def _():
            state_scr[...] = jnp.zeros_like(state_scr)

        for h in range(HEADS):
            sl = slice(h * HEAD_DIM, (h + 1) * HEAD_DIM)
            qh, kh, vh = q_ref[:, sl], k_ref[:, sl], v_ref[:, sl]
            sc = _dot_nt(qh, kh) * mask_ref[h]
            st = state_scr[h]
            stb = st.astype(BF16)
            st_ref[0, h] = stb
            qdb = (qh.astype(F32) * qd_ref[:, sl]).astype(BF16)
            kdb = (kh.astype(F32) * kd_ref[:, sl]).astype(BF16)
            ret = _dot(sc.astype(BF16), vh) + _dot(qdb, stb)
            state_scr[h] = st * cdec[h] + _dot_tn(kdb, vh)
            ret_ref[:, sl] = ret
            r = lax.rsqrt(jnp.mean(ret * ret, axis=-1, keepdims=True) + RMS_EPS)
            gh = g_ref[:, sl]
            cat_ref[:, sl] = ((ret * r) * (gh * _sigmoid(gh))).astype(BF16)

    tile = pl.BlockSpec((SUPER, RET_W), lambda n: (n, 0))
    return _call(
        body, name="retention_fwd", grid=(ns,),
        in_specs=[tile, tile, tile, tile, HBM_SPEC, _whole(), _whole(), _whole()],
        out_specs=[tile, tile, pl.BlockSpec((1, HEADS, HEAD_DIM, HEAD_DIM), lambda n: (n, 0, 0, 0))],
        out_shape=[jax.ShapeDtypeStruct((s, RET_W), F32), jax.ShapeDtypeStruct((s, 2 * RET_W), BF16),
                   jax.ShapeDtypeStruct((ns, HEADS, HEAD_DIM, HEAD_DIM), BF16)],
        scratch_shapes=[pltpu.VMEM((HEADS, HEAD_DIM, HEAD_DIM), F32)],
        aliases={4: 1}, sem=("arbitrary",), operands=(q, k, v, g, cat, mask, qd, kd), riders=riders,
        after=after,
    )


def _outproj_ln1(x, cat, wout, g1, b1, ts, riders=(), after=()):
    s = x.shape[0]

    def body(x_ref, cat_ref, w_ref, g_ref, b_ref, xhat_ref, rstd_ref, h1b_ref):
        z = ALPHA * x_ref[...] + _dot(cat_ref[...], w_ref[...])
        xhat, rstd = _layernorm_fwd(z)
        xhat_ref[...] = xhat
        rstd_ref[...] = rstd
        h1b_ref[...] = (xhat * g_ref[...] + b_ref[...]).astype(BF16)

    tile = lambda w: pl.BlockSpec((ts, w), lambda i: (i, 0))
    return _call(
        body, name="outproj_ln1", grid=(s // ts,),
        in_specs=[tile(D_MODEL), tile(D_MODEL), _whole(), _whole(), _whole()],
        out_specs=[tile(D_MODEL), tile(1), tile(D_MODEL)],
        out_shape=[jax.ShapeDtypeStruct((s, D_MODEL), F32), jax.ShapeDtypeStruct((s, 1), F32),
                   jax.ShapeDtypeStruct((s, D_MODEL), BF16)],
        sem=("arbitrary",), operands=(x, cat, wout, g1, b1), riders=riders, after=after,
    )


def _ffn_fwd_loss(xhat1, h1b, target, wup4, wdown, cw, cb, g1, b1, g2, b2, ts):
    s = xhat1.shape[0]

    def body(xhat_ref, h1b_ref, tgt_ref, wup_ref, wdn_ref, cw_ref, cb_ref, g1_ref, b1_ref, g2_ref, b2_ref,
             ub_ref, act_ref, sd_ref, dz2_ref, dz2b_ref, loss_ref, dg2_ref, db2_ref, val_scr, gext_scr, ffn_scr):
        i = pl.program_id(0)

        @pl.when(i == 0)
        def _():
            gext_scr[0:CONV_HALO, :] = jnp.zeros((CONV_HALO, D_FF), F32)
            loss_ref[...] = jnp.zeros_like(loss_ref)
            dg2_ref[...] = jnp.zeros_like(dg2_ref)
            db2_ref[...] = jnp.zeros_like(db2_ref)

        for half in range(2):
            lo = half * UP_SH
            gext_scr[CONV_HALO:CONV_HALO + ts, lo:lo + UP_SH] = _dot(h1b_ref[...], wup_ref[2 + half])
            val_scr[:, lo:lo + UP_SH] = _dot(h1b_ref[...], wup_ref[half])
            for c0 in range(lo, lo + UP_SH, FFN_STRIP):
                cols = slice(c0, c0 + FFN_STRIP)
                ext = gext_scr[:, cols]
                gate = ext[CONV_HALO:]
                hc = cb_ref[:, cols] + ((pltpu.roll(ext, 2, 0)[CONV_HALO:] * cw_ref[0:1, cols]
                                         + pltpu.roll(ext, 1, 0)[CONV_HALO:] * cw_ref[1:2, cols])
                                        + gate * cw_ref[2:3, cols])
                val = val_scr[:, cols]
                sg = _sigmoid(hc)
                si = hc * sg
                act_ref[:, cols] = (si * val).astype(BF16)
                ub_ref[:, cols] = val.astype(BF16)
                ub_ref[:, D_FF + c0:D_FF + c0 + FFN_STRIP] = gate.astype(BF16)
                sd_ref[:, cols] = hc.astype(BF16)
            part = _dot(act_ref[:, lo:lo + UP_SH], wdn_ref[lo:lo + UP_SH, :])
            if half == 0:
                ffn_scr[...] = part
            else:
                ffn_scr[...] += part

        gext_scr[0:CONV_HALO, :] = gext_scr[ts:ts + CONV_HALO, :]

        loss_acc = jnp.zeros((1, 1), F32)
        dg2_acc = jnp.zeros((1, D_MODEL), F32)
        db2_acc = jnp.zeros((1, D_MODEL), F32)
        for r0 in range(0, ts, LN_ROWS):
            rows = slice(r0, r0 + LN_ROWS)
            h1 = xhat_ref[rows, :] * g1_ref[...] + b1_ref[...]
            xhat2, rstd2 = _layernorm_fwd(ALPHA * h1 + ffn_scr[rows, :])
            diff = (xhat2 * g2_ref[...] + b2_ref[...]) - tgt_ref[rows, :]
            row = jnp.mean(diff * diff, axis=-1, keepdims=True)
            loss_acc = loss_acc + 0.5 * jnp.sum(row, axis=0, keepdims=True)
            dy = diff * (1.0 / D_MODEL)
            dg2_acc = dg2_acc + jnp.sum(dy * xhat2, axis=0, keepdims=True)
            db2_acc = db2_acc + jnp.sum(dy, axis=0, keepdims=True)
            dz2 = _layernorm_bwd(dy, xhat2, rstd2, g2_ref[...])
            dz2_ref[rows, :] = dz2
            dz2b_ref[rows, :] = dz2.astype(BF16)
        loss_ref[...] += loss_acc
        dg2_ref[...] += dg2_acc
        db2_ref[...] += db2_acc

    tile = lambda w: pl.BlockSpec((ts, w), lambda i: (i, 0))
    acc = lambda w: pl.BlockSpec((1, w), lambda i: (0, 0))
    return pl.pallas_call(
        body, name="ffn_fwd_loss", grid=(s // ts,),
        in_specs=[tile(D_MODEL), tile(D_MODEL), tile(D_MODEL)] + [_whole()] * 8,
        out_specs=[tile(2 * D_FF), tile(D_FF), tile(D_FF), tile(D_MODEL), tile(D_MODEL),
                   acc(1), acc(D_MODEL), acc(D_MODEL)],
        out_shape=[jax.ShapeDtypeStruct((s, 2 * D_FF), BF16), jax.ShapeDtypeStruct((s, D_FF), BF16),
                   jax.ShapeDtypeStruct((s, D_FF), BF16), jax.ShapeDtypeStruct((s, D_MODEL), F32),
                   jax.ShapeDtypeStruct((s, D_MODEL), BF16),
                   jax.ShapeDtypeStruct((1, 1), F32), jax.ShapeDtypeStruct((1, D_MODEL), F32),
                   jax.ShapeDtypeStruct((1, D_MODEL), F32)],
        scratch_shapes=[pltpu.VMEM((ts, D_FF), F32), pltpu.VMEM((ts + CONV_HALO, D_FF), F32),
                        pltpu.VMEM((ts, D_MODEL), F32)],
        compiler_params=_params(("arbitrary",)),
    )(xhat1, h1b, target, wup4, wdown, cw, cb, g1, b1, g2, b2)


def _ffn_bwd(dz2, dz2b, ub, sd, xhat1, rstd1, wup4, wdown, cw, g1, ts):
    s = dz2.shape[0]
    nt = s // ts

    def body(dz2_ref, dz2b_ref, ub_ref, sd_ref, xhat_ref, rstd_ref, wup_ref, wdn_ref, cw_ref, g1_ref,
             dub_ref, dz1_ref, dz1b_ref, dg1_ref, db1_ref, dcw_ref, dcb_ref, dext_scr, da_scr):
        i = pl.program_id(0)

        @pl.when(i == 0)
        def _():
            dext_scr[ts:ts + CONV_HALO, :] = jnp.zeros((CONV_HALO, D_FF), F32)
            dg1_ref[...] = jnp.zeros_like(dg1_ref)
            db1_ref[...] = jnp.zeros_like(db1_ref)
            dcw_ref[...] = jnp.zeros_like(dcw_ref)
            dcb_ref[...] = jnp.zeros_like(dcb_ref)

        da_scr[...] = _dot_nt(dz2b_ref[...], wdn_ref[...])
        n_ext = ts + CONV_HALO
        for c0 in range(0, D_FF, FFN_STRIP):
            cols = slice(c0, c0 + FFN_STRIP)
            gcols = slice(D_FF + c0, D_FF + c0 + FFN_STRIP)
            val = ub_ref[:, cols].astype(F32)
            gate = ub_ref[:, gcols].astype(F32)
            da = da_scr[:, cols]
            hc = sd_ref[:, cols].astype(F32)
            sg = _sigmoid(hc)
            dhc = da * val * (sg * (1.0 + hc * (1.0 - sg)))
            dext_scr[0:ts, cols] = dhc
            dext = dext_scr[:, cols]
            dhc1 = pltpu.roll(dext, n_ext - 1, 0)[0:ts]
            dhc2 = pltpu.roll(dext, n_ext - 2, 0)[0:ts]
            dcb_ref[:, cols] += jnp.sum(dhc, axis=0, keepdims=True)
            dcw_ref[0:1, cols] += jnp.sum(dhc2 * gate, axis=0, keepdims=True)
            dcw_ref[1:2, cols] += jnp.sum(dhc1 * gate, axis=0, keepdims=True)
            dcw_ref[2:3, cols] += jnp.sum(dhc * gate, axis=0, keepdims=True)
            dgate = dhc * cw_ref[2:3, cols] + dhc1 * cw_ref[1:2, cols] + dhc2 * cw_ref[0:1, cols]
            dub_ref[:, cols] = (da * (hc * sg)).astype(BF16)
            dub_ref[:, gcols] = dgate.astype(BF16)
        dext_scr[ts:n_ext, :] = dext_scr[0:CONV_HALO, :]
        dh1 = ALPHA * dz2_ref[...]
        for j in range(N_SHARD):
            dh1 = dh1 + _dot_nt(dub_ref[:, j * UP_SH:(j + 1) * UP_SH], wup_ref[j])
        xhat = xhat_ref[...]
        dg1_ref[...] += jnp.sum(dh1 * xhat, axis=0, keepdims=True)
        db1_ref[...] += jnp.sum(dh1, axis=0, keepdims=True)
        dz1 = _layernorm_bwd(dh1, xhat, rstd_ref[...], g1_ref[...])
        dz1_ref[...] = dz1
        dz1b_ref[...] = dz1.astype(BF16)

    tile = lambda w: pl.BlockSpec((ts, w), lambda i: (nt - 1 - i, 0))
    acc = lambda rws, w: pl.BlockSpec((rws, w), lambda i: (0, 0))
    return pl.pallas_call(
        body, name="ffn_bwd", grid=(nt,),
        in_specs=[tile(D_MODEL), tile(D_MODEL), tile(2 * D_FF), tile(D_FF), tile(D_MODEL), tile(1)]
        + [_whole()] * 4,
        out_specs=[tile(2 * D_FF), tile(D_MODEL), tile(D_MODEL), acc(1, D_MODEL), acc(1, D_MODEL),
                   acc(3, D_FF), acc(1, D_FF)],
        out_shape=[jax.ShapeDtypeStruct((s, 2 * D_FF), BF16),
                   jax.ShapeDtypeStruct((s, D_MODEL), F32), jax.ShapeDtypeStruct((s, D_MODEL), BF16),
                   jax.ShapeDtypeStruct((1, D_MODEL), F32),
                   jax.ShapeDtypeStruct((1, D_MODEL), F32), jax.ShapeDtypeStruct((3, D_FF), F32),
                   jax.ShapeDtypeStruct((1, D_FF), F32)],
        scratch_shapes=[pltpu.VMEM((ts + CONV_HALO, D_FF), F32), pltpu.VMEM((ts, D_FF), F32)],
        compiler_params=_params(("arbitrary",)),
    )(dz2, dz2b, ub, sd, xhat1, rstd1, wup4, wdown, cw, g1)


def _mix_bwd(dz1, pooled, ret, g, wout, wpool, pscale, ts, riders=(), after=()):
    s = dz1.shape[0]
    nt = s // ts

    def body(dz1_ref, pooled_ref, ret_ref, g_ref, wout_ref, wp_ref, ps_ref,
             dret_ref, dgp_ref, dwp_ref, dps_ref, eext_scr):
        i = pl.program_id(0)
        r = nt - 1 - i

        @pl.when(i == 0)
        def _():
            eext_scr[ts:ts + POOL_HALO, :] = jnp.zeros((POOL_HALO, POOL_W), F32)
            dwp_ref[...] = jnp.zeros_like(dwp_ref)
            dps_ref[...] = jnp.zeros_like(dps_ref)

        dzb = dz1_ref[...].astype(BF16)
        dcat_r = _dot_nt(dzb, wout_ref[0:RET_W, :])
        dcat_p = _dot_nt(dzb, wout_ref[RET_W:2 * RET_W, :])
        pos = (r * ts + lax.broadcasted_iota(jnp.int32, (ts, 1), 0) + 1).astype(F32)
        dpooled = []
        for gi, w in enumerate(POOL_WINDOWS):
            sl = slice(gi * HEAD_DIM, (gi + 1) * HEAD_DIM)
            pb = pooled_ref[:, sl]
            dy = dcat_p[:, sl]
            dps_ref[:, sl] += jnp.sum(dy * _dot(pb, wp_ref[gi]), axis=0, keepdims=True)
            dlin = (dy * ps_ref[:, sl]).astype(BF16)
            dwp_ref[gi] += _dot_tn(pb, dlin)
            dpg = _dot_nt(dlin, wp_ref[gi])
            dpooled.append(dpg)
            eext_scr[0:ts, sl] = dpg / jnp.minimum(pos, float(w))
        for gi, w in enumerate(POOL_WINDOWS):
            sl = slice(gi * HEAD_DIM, (gi + 1) * HEAD_DIM)
            acc = eext_scr[:, sl]
            shift = 1
            while shift < w:
                acc = acc + pltpu.roll(acc, ts + POOL_HALO - shift, 0)
                shift *= 2
            dgp_ref[:, RET_W + gi * HEAD_DIM:RET_W + (gi + 1) * HEAD_DIM] = (acc[0:ts] - dpooled[gi]).astype(BF16)
        eext_scr[ts:ts + POOL_HALO, :] = eext_scr[0:POOL_HALO, :]
        for h in range(HEADS):
            sl = slice(h * HEAD_DIM, (h + 1) * HEAD_DIM)
            rt = ret_ref[:, sl]
            rr = lax.rsqrt(jnp.mean(rt * rt, axis=-1, keepdims=True) + RMS_EPS)
            rn = rt * rr
            gh = g_ref[:, sl]
            sg = _sigmoid(gh)
            dy = dcat_r[:, sl]
            dgp_ref[:, sl] = (dy * rn * (sg * (1.0 + gh * (1.0 - sg)))).astype(BF16)
            drn = dy * (gh * sg)
            dret_ref[:, sl] = (rr * (drn - rn * jnp.mean(drn * rn, axis=-1, keepdims=True))).astype(BF16)

    tile = lambda w: pl.BlockSpec((ts, w), lambda i: (nt - 1 - i, 0))
    return _call(
        body, name="mix_bwd", grid=(nt,),
        in_specs=[tile(D_MODEL), tile(POOL_W), tile(RET_W), tile(RET_W), _whole(), _whole(), _whole()],
        out_specs=[tile(RET_W), tile(2 * RET_W),
                   pl.BlockSpec((len(POOL_WINDOWS), HEAD_DIM, HEAD_DIM), lambda i: (0, 0, 0)),
                   pl.BlockSpec((1, POOL_W), lambda i: (0, 0))],
        out_shape=[jax.ShapeDtypeStruct((s, RET_W), BF16), jax.ShapeDtypeStruct((s, 2 * RET_W), BF16),
                   jax.ShapeDtypeStruct((len(POOL_WINDOWS), HEAD_DIM, HEAD_DIM), F32),
                   jax.ShapeDtypeStruct((1, POOL_W), F32)],
        scratch_shapes=[pltpu.VMEM((ts + POOL_HALO, POOL_W), F32)],
        sem=("arbitrary",), operands=(dz1, pooled, ret, g, wout, wpool, pscale), riders=riders,
        after=after,
    )


def _retention_bwd(q, k, v, dret, dgp, states, mask, qd, kd, cosf, sinf, riders=(), after=()):
    s = q.shape[0]
    ns = s // SUPER
    cdec = [gm ** float(SUPER) for gm in _gammas()]

    def body(q_ref, k_ref, v_ref, do_ref, dgp_ref, st_ref, mask_ref, qd_ref, kd_ref, cos_ref, sin_ref,
             dproj_ref, dstate_scr):
        i = pl.program_id(0)

        @pl.when(i == 0)
        def _():
            dstate_scr[...] = jnp.zeros_like(dstate_scr)

        cosf_t = cos_ref[...]
        sinf_t = sin_ref[...]
        for h in range(HEADS):
            sl = slice(h * HEAD_DIM, (h + 1) * HEAD_DIM)
            qh, kh, vh, doh = q_ref[:, sl], k_ref[:, sl], v_ref[:, sl], do_ref[:, sl]
            m = mask_ref[h]
            scb = (_dot_nt(qh, kh) * m).astype(BF16)
            dscb = (_dot_nt(doh, vh) * m).astype(BF16)
            stb = st_ref[0, h]
            dst = dstate_scr[h]
            dstb = dst.astype(BF16)
            qdb = (qh.astype(F32) * qd_ref[:, sl]).astype(BF16)
            kdb = (kh.astype(F32) * kd_ref[:, sl]).astype(BF16)
            dq = _dot(dscb, kh) + _dot_nt(doh, stb) * qd_ref[:, sl]
            dk = _dot_tn(dscb, qh) + _dot_nt(vh, dstb) * kd_ref[:, sl]
            dv = _dot_tn(scb, doh) + _dot(kdb, dstb)
            dstate_scr[h] = dst * cdec[h] + _dot_tn(qdb, doh)
            lo = h * HEAD_DIM
            dproj_ref[:, lo:lo + HEAD_DIM] = _rope_t(dq, cosf_t, sinf_t).astype(BF16)
            dproj_ref[:, RET_W + lo:RET_W + lo + HEAD_DIM] = _rope_t(dk * K_SCALE, cosf_t, sinf_t).astype(BF16)
            dproj_ref[:, 2 * RET_W + lo:2 * RET_W + lo + HEAD_DIM] = dv.astype(BF16)
        dproj_ref[:, 3 * RET_W:IN_W] = dgp_ref[...]

    tile = lambda w: pl.BlockSpec((SUPER, w), lambda i: (ns - 1 - i, 0))
    return _call(
        body, name="retention_bwd", grid=(ns,),
        in_specs=[tile(RET_W), tile(RET_W), tile(RET_W), tile(RET_W), tile(2 * RET_W),
                  pl.BlockSpec((1, HEADS, HEAD_DIM, HEAD_DIM), lambda i: (ns - 1 - i, 0, 0, 0)),
                  _whole(), _whole(), _whole(), tile(HEAD_DIM), tile(HEAD_DIM)],
        out_specs=[tile(IN_W)],
        out_shape=[jax.ShapeDtypeStruct((s, IN_W), BF16)],
        scratch_shapes=[pltpu.VMEM((HEADS, HEAD_DIM, HEAD_DIM), F32)],
        sem=("arbitrary",), operands=(q, k, v, dret, dgp, states, mask, qd, kd, cosf, sinf), riders=riders,
        after=after,
    )


def _dx(dz1, dproj, win4, ts, riders=(), after=()):
    s = dz1.shape[0]

    def body(dz1_ref, dp_ref, w_ref, dx_ref):
        acc = ALPHA * dz1_ref[...]
        for j in range(N_SHARD):
            acc = acc + _dot_nt(dp_ref[:, j * IN_SH:(j + 1) * IN_SH], w_ref[j])
        dx_ref[...] = acc

    tile = lambda w: pl.BlockSpec((ts, w), lambda i: (i, 0))
    return _call(
        body, name="dx", grid=(s // ts,),
        in_specs=[tile(D_MODEL), tile(IN_W), _whole()],
        out_specs=[tile(D_MODEL)],
        out_shape=[jax.ShapeDtypeStruct((s, D_MODEL), F32)],
        sem=("arbitrary",), operands=(dz1, dproj, win4), riders=riders, after=after,
    )


def _wgrad(a, b, tm, tn, name, stacked, m_outer, riders=(), after=()):
    s, m = a.shape
    n = b.shape[1]

    def body(a_ref, b_ref, o32_ref, o16_ref):
        res = _dot_tn(a_ref[...], b_ref[...])
        o32_ref[...] = res.reshape(o32_ref.shape)
        o16_ref[...] = res.astype(BF16).reshape(o16_ref.shape)

    if m_outer:
        grid, blocks = (m // tm, n // tn), (lambda g0, g1: (g0, g1))
    else:
        grid, blocks = (n // tn, m // tm), (lambda g0, g1: (g1, g0))
    if stacked:
        shape = (n // tn, m, tn)
        ospec = pl.BlockSpec((1, tm, tn), lambda g0, g1: (blocks(g0, g1)[1], blocks(g0, g1)[0], 0))
    else:
        shape = (m, n)
        ospec = pl.BlockSpec((tm, tn), lambda g0, g1: blocks(g0, g1))
    return _call(
        body, name=name, grid=grid,
        in_specs=[pl.BlockSpec((s, tm), lambda g0, g1: (0, blocks(g0, g1)[0])),
                  pl.BlockSpec((s, tn), lambda g0, g1: (0, blocks(g0, g1)[1]))],
        out_specs=[ospec, ospec],
        out_shape=[jax.ShapeDtypeStruct(shape, F32), jax.ShapeDtypeStruct(shape, BF16)],
        sem=("arbitrary", "arbitrary"), operands=(a, b), riders=riders, after=after,
    )


class _NoComm:
    def __init__(self, win4, wout, wup4, wdown):
        self.weights = dict(w_in=win4, w_out=wout, w_up=wup4, w_down=wdown)
        self.grads = {}

    def weight(self, name):
        return self.weights[name]

    def riders(self, call):
        return ()

    def after(self, call):
        return ()

    def landed(self, call, results, outs):
        pass

    def small_gradients(self, loss, small):
        pass

    def gradient(self, name, g32, g16):
        self.grads[name] = (g32, g16)


def _local_step(x, target, cw, cb, wpool, pscale, g1, b1, g2, b2, comm):
    s = x.shape[0]
    ts_a = min(512, s)
    ts_f = min(256, s)
    mask, qd, kd = _decay_tables()
    cosf, sinf = _rope_tables(s)
    wpool_b = wpool.astype(BF16)

    def run(call, fn, *args):
        outs, res = fn(*args, riders=comm.riders(call), after=comm.after(call))
        comm.landed(call, res, outs)
        return outs

    xb, q, k, v, g, pooled, cat = run("proj_pool", _proj_pool, x, comm.weight("w_in"), cosf, sinf, wpool_b,
                                      pscale, ts_a)
    ret, cat, states = run("retention_fwd", _retention_fwd, q, k, v, g, cat, mask, qd, kd)
    wout = comm.weight("w_out")
    xhat1, rstd1, h1b = run("outproj_ln1", _outproj_ln1, x, cat, wout, g1, b1, ts_a)
    wup4, wdown = comm.weight("w_up"), comm.weight("w_down")
    ub, act, sd, dz2, dz2b, loss, dg2, db2 = _ffn_fwd_loss(xhat1, h1b, target, wup4, wdown, cw, cb, g1, b1, g2, b2,
                                                           ts_f)

    dub, dz1, dz1b, dg1, db1, dcw, dcb = _ffn_bwd(dz2, dz2b, ub, sd, xhat1, rstd1, wup4, wdown, cw, g1, ts_f)
    half = D_MODEL // 2
    comm.gradient("w_up", *run("wgrad_up", _wgrad, h1b, dub, half, UP_SH, "wgrad_up", True, False))
    comm.gradient("w_out", *run("wgrad_out", _wgrad, cat, dz1b, D_MODEL, half, "wgrad_out", False, True))
    comm.gradient("w_down", *run("wgrad_down", _wgrad, act, dz2b, D_FF // 2, half, "wgrad_down", False, True))
    dret, dgp, dwp, dps = run("mix_bwd", _mix_bwd, dz1b, pooled, ret, g, wout, wpool_b, pscale, ts_a)
    small = dict(w_pool=dwp, pool_scale=dps, ln1_g=dg1, ln1_b=db1, conv_w=dcw, conv_b=dcb,
                 ln2_g=dg2, ln2_b=db2)
    comm.small_gradients(loss, small)
    dproj, = run("retention_bwd", _retention_bwd, q, k, v, dret, dgp, states, mask, qd, kd, cosf, sinf)
    comm.gradient("w_in", *run("wgrad_in", _wgrad, xb, dproj, D_MODEL, IN_SH, "wgrad_in", True, True))
    (grad_x,), _ = _dx(dz1, dproj, comm.weight("w_in"), ts_a, after=comm.after("dx"))
    return loss, grad_x, small


CAST_ROWS = 64
SHARD_SHAPES = ((D_MODEL, IN_SH), (OUT_SH, D_MODEL), (D_MODEL, UP_SH), (DOWN_SH, D_MODEL))
N_BIG = len(SHARD_SHAPES)
CW_PAD = (8, 768)


def _mesh_pos():
    return lax.axis_index("x"), lax.axis_index("y"), lax.axis_index("c")


def _other_chips(x, y):
    return [(1 - x, y), (x, 1 - y), (1 - x, 1 - y)]


def _half_rows(w, which):
    hr = SHARD_SHAPES[w][0] // 2
    return pl.ds(pl.multiple_of(which * hr, 16), hr)


def _gather_weights(shards, cw8, full):
    def body(*refs):
        in_refs = refs[:N_BIG]
        cw_ref = refs[N_BIG]
        out_refs = refs[N_BIG + 1:2 * N_BIG + 1]
        cwo_ref = refs[2 * N_BIG + 1]
        stage = refs[2 * N_BIG + 2:3 * N_BIG + 2]
        send_sems, recv_sems, fsend_sems, frecv_sems, cw_send, cw_recv, local_sems = refs[3 * N_BIG + 2:]
        x, y, c = _mesh_pos()
        j0 = 2 * x + y
        chips = _other_chips(x, y)

        def cast_to_stage(w):
            def cast(i, carry):
                rows = pl.ds(pl.multiple_of(i * CAST_ROWS, CAST_ROWS), CAST_ROWS)
                stage[w][rows, :] = in_refs[w][rows, :].astype(BF16)
                return carry
            lax.fori_loop(0, SHARD_SHAPES[w][0] // CAST_ROWS, cast, 0)

        for w in full:
            cast_to_stage(w)

        jx, jy, jd = 2 * (1 - x) + y, 2 * x + (1 - y), 2 * (1 - x) + (1 - y)
        neighbours = [((1 - x, y, c), jx), ((x, 1 - y, c), jy)]
        passed = jnp.where(c == 0, jx, jy)
        pass_to = (jnp.where(c == 0, x, 1 - x), jnp.where(c == 0, 1 - y, y), c)

        def nbr(w, k, block):
            return pltpu.make_async_remote_copy(
                src_ref=stage[w].at[_half_rows(w, c), :], dst_ref=out_refs[w].at[block, _half_rows(w, c), :],
                send_sem=send_sems.at[w, k], recv_sem=recv_sems.at[w, k],
                device_id=neighbours[k][0], device_id_type=MESH)

        def relay(w, block):
            return pltpu.make_async_remote_copy(
                src_ref=out_refs[w].at[passed, _half_rows(w, c), :],
                dst_ref=out_refs[w].at[block, _half_rows(w, c), :],
                send_sem=send_sems.at[w, 2], recv_sem=recv_sems.at[w, 2],
                device_id=pass_to, device_id_type=MESH)

        def d2d(w, k, block, half):
            return pltpu.make_async_remote_copy(
                src_ref=out_refs[w].at[block, _half_rows(w, half), :],
                dst_ref=out_refs[w].at[block, _half_rows(w, half), :],
                send_sem=fsend_sems.at[w, k], recv_sem=frecv_sems.at[w, k],
                device_id=(x, y, 1 - c), device_id_type=MESH)

        def conv(k, block):
            chip = chips[k]
            return pltpu.make_async_remote_copy(
                src_ref=cw_ref, dst_ref=cwo_ref.at[block], send_sem=cw_send.at[k], recv_sem=cw_recv.at[k],
                device_id=(chip[0], chip[1], c), device_id_type=MESH)

        sent = [nbr(w, k, j0) for w in full for k in range(2)] + [conv(k, j0) for k in range(3)]
        for cp in sent:
            cp.start()
        for w in range(N_BIG):
            if w not in full:
                cast_to_stage(w)
        local = [pltpu.make_async_copy(stage[w], out_refs[w].at[j0], local_sems.at[w]) for w in range(N_BIG)]
        local.append(pltpu.make_async_copy(cw_ref, cwo_ref.at[j0], local_sems.at[N_BIG]))
        for cp in local:
            cp.start()
        for w in full:
            for k, (_, block) in enumerate(neighbours):
                nbr(w, k, block).wait_recv()
            later = [relay(w, passed)] + [d2d(w, k, block, c) for k, (_, block) in enumerate(neighbours)]
            for cp in later:
                cp.start()
            sent += later
        for w in full:
            relay(w, jd).wait_recv()
            fw = d2d(w, 2, jd, c)
            fw.start()
            sent.append(fw)
        for w in full:
            for k, block in enumerate([jx, jy, jd]):
                d2d(w, k, block, 1 - c).wait_recv()
        for k, chip in enumerate(chips):
            conv(k, 2 * chip[0] + chip[1]).wait_recv()
        for cp in sent:
            cp.wait_send()
        for cp in local:
            cp.wait()

    out_shape = [jax.ShapeDtypeStruct((N_SHARD,) + shp, BF16) for shp in SHARD_SHAPES]
    out_shape.append(jax.ShapeDtypeStruct((N_SHARD,) + CW_PAD, F32))
    return pl.pallas_call(
        body, name="gather_weights",
        in_specs=[_whole()] * (N_BIG + 1),
        out_specs=[HBM_SPEC] * (N_BIG + 1),
        out_shape=out_shape,
        scratch_shapes=[pltpu.VMEM(shp, BF16) for shp in SHARD_SHAPES] + [
            pltpu.SemaphoreType.DMA((N_BIG, 3)), pltpu.SemaphoreType.DMA((N_BIG, 3)),
            pltpu.SemaphoreType.DMA((N_BIG, 3)), pltpu.SemaphoreType.DMA((N_BIG, 3)),
            pltpu.SemaphoreType.DMA((3,)), pltpu.SemaphoreType.DMA((3,)),
            pltpu.SemaphoreType.DMA((N_BIG + 1,))],
        compiler_params=pltpu.CompilerParams(vmem_limit_bytes=VMEM_LIMIT),
    )(*shards, cw8)


def _gather_rider(arrays, ops):
    ws = sorted(arrays)

    def make(inplace, srcs, lands, send_sems, recv_sems):
        del srcs, lands
        x, y, c = _mesh_pos()
        j0, jx, jy, jd = 2 * x + y, 2 * (1 - x) + y, 2 * x + (1 - y), 2 * (1 - x) + (1 - y)
        x_nbr, y_nbr, sibling = (1 - x, y, c), (x, 1 - y, c), (x, y, 1 - c)
        starts, waits = [], []
        for n, (kind, w, (r0, nr)) in enumerate(ops):
            ref = inplace[ws.index(w)]
            hr = SHARD_SHAPES[w][0] // 2
            rows = lambda core: pl.ds(pl.multiple_of(core * hr + r0, 16), nr)
            mine, theirs = rows(c), rows(1 - c)
            if kind == "ici":
                moves = [(ref.at[j0, mine, :], x_nbr, ref.at[jx, mine, :]),
                         (ref.at[j0, mine, :], y_nbr, ref.at[jy, mine, :]),
                         (ref.at[j0, mine, :], (1 - x, 1 - y, c), ref.at[jd, mine, :])]
            elif kind == "nbr":
                moves = [(ref.at[j0, mine, :], x_nbr, ref.at[jx, mine, :]),
                         (ref.at[j0, mine, :], y_nbr, ref.at[jy, mine, :])]
            elif kind == "relay":
                passed = jnp.where(c == 0, jx, jy)
                to = (jnp.where(c == 0, x, 1 - x), jnp.where(c == 0, 1 - y, y), c)
                moves = [(ref.at[passed, mine, :], to, ref.at[jd, mine, :])]
            else:
                blocks = dict(d2d=[jx, jy, jd], d2d_nbr=[jx, jy], d2d_diag=[jd])[kind]
                moves = [(ref.at[b, mine, :], sibling, ref.at[b, theirs, :]) for b in blocks]
            for k, (src, to, landing) in enumerate(moves):
                sems = dict(send_sem=send_sems.at[3 * n + k], recv_sem=recv_sems.at[3 * n + k],
                            device_id=to, device_id_type=MESH)
                send = pltpu.make_async_remote_copy(src_ref=src, dst_ref=src, **sems)
                arrival = pltpu.make_async_remote_copy(src_ref=src, dst_ref=landing, **sems)
                starts.append(send)
                waits += [arrival.wait_recv, send.wait_send]
        return starts, waits

    return _Rider([arrays[w] for w in ws], [], [], 3 * len(ops), make)


def _whole_half(w):
    return (0, SHARD_SHAPES[w][0] // 2)


def _pair_rider(ws, g16s):
    def make(inplace, srcs, lands, send_sems, recv_sems):
        del inplace
        x, y, c = _mesh_pos()
        copies = [pltpu.make_async_remote_copy(
            src_ref=srcs[i].at[:, _half_rows(w, 1 - c), :], dst_ref=lands[i],
            send_sem=send_sems.at[i], recv_sem=recv_sems.at[i], device_id=(x, y, 1 - c), device_id_type=MESH)
            for i, w in enumerate(ws)]
        return copies, [cp.wait for cp in copies]

    lands = [jax.ShapeDtypeStruct((N_SHARD, SHARD_SHAPES[w][0] // 2, SHARD_SHAPES[w][1]), BF16) for w in ws]
    return _Rider([], g16s, lands, len(ws), make)


def _chip_rider(ws, p16s, rows=None, landing=None):
    def make(inplace, srcs, lands, send_sems, recv_sems):
        x, y, c = _mesh_pos()
        dsts = inplace if landing is not None else lands
        copies = []
        for i, w in enumerate(ws):
            r0, nr = rows if rows is not None else _whole_half(w)
            for k, chip in enumerate(_other_chips(x, y)):
                copies.append(pltpu.make_async_remote_copy(
                    src_ref=srcs[i].at[2 * chip[0] + chip[1], pl.ds(r0, nr), :],
                    dst_ref=dsts[i].at[k, pl.ds(r0, nr), :],
                    send_sem=send_sems.at[3 * i + k], recv_sem=recv_sems.at[3 * i + k],
                    device_id=(chip[0], chip[1], c), device_id_type=MESH))
        return copies, [cp.wait for cp in copies]

    lands = [jax.ShapeDtypeStruct((3, SHARD_SHAPES[w][0] // 2, SHARD_SHAPES[w][1]), BF16) for w in ws]
    if landing is not None:
        return _Rider(landing, p16s, [], 3 * len(ws), make)
    return _Rider([], p16s, lands, 3 * len(ws), make)


def _final_rider(halves):
    def make(inplace, srcs, lands, send_sems, recv_sems):
        del inplace
        x, y, c = _mesh_pos()
        copies = [pltpu.make_async_remote_copy(
            src_ref=srcs[i], dst_ref=lands[i], send_sem=send_sems.at[i], recv_sem=recv_sems.at[i],
            device_id=(x, y, 1 - c), device_id_type=MESH) for i in range(len(halves))]
        return copies, [cp.wait for cp in copies]

    return _Rider([], halves, [jax.ShapeDtypeStruct(h.shape, h.dtype) for h in halves], len(halves), make)


def _comm_only(name, riders):
    _, res = _call(lambda: None, name=name, grid=(), in_specs=[], out_specs=[], out_shape=[], operands=(),
                   riders=riders)
    return res


class _SemList:
    def __init__(self, refs):
        self.at = list(refs)


def _merged_rider(riders):
    srcs = [a for r in riders for a in r.srcs]
    lands = [a for r in riders for a in r.lands]

    def make(inplace, src_refs, land_refs, send_sems, recv_sems):
        starts, waits = [], []
        s0 = l0 = c0 = 0
        for r in riders:
            part = r.make(inplace, src_refs[s0:s0 + len(r.srcs)], land_refs[l0:l0 + len(r.lands)],
                          _SemList(send_sems.at[c0:c0 + r.n_copies]), _SemList(recv_sems.at[c0:c0 + r.n_copies]))
            starts += part[0]
            waits += part[1]
            s0, l0, c0 = s0 + len(r.srcs), l0 + len(r.lands), c0 + r.n_copies
        return starts, waits

    return _Rider([], srcs, lands, sum(r.n_copies for r in riders), make)


def _split_start(name, rider):
    assert not rider.inplace
    ns, nl, n = len(rider.srcs), len(rider.lands), rider.n_copies

    def body(*refs):
        srcs, lands = refs[:ns], refs[ns:ns + nl]
        sems = refs[ns + nl:ns + nl + 2 * n]
        token = refs[-1]
        starts, _ = rider.make([], srcs, lands, _SemList(sems[:n]), _SemList(sems[n:]))
        for cp in starts:
            cp.start()
        token[...] = jnp.zeros_like(token)

    buffers = [pltpu.with_memory_space_constraint(a, pltpu.HBM) for a in rider.srcs]
    buffers += [pltpu.with_memory_space_constraint(lax.empty(s.shape, s.dtype), pltpu.HBM) for s in rider.lands]
    hbm = pl.BlockSpec(memory_space=pltpu.HBM)
    sem = pl.BlockSpec(memory_space=pltpu.SEMAPHORE)
    outs = pl.pallas_call(
        body, name=name,
        out_shape=tuple([pltpu.SemaphoreType.DMA(())] * (2 * n) + [pltpu.HBM(b.shape, b.dtype) for b in buffers]
                        + [jax.ShapeDtypeStruct((8, 128), F32)]),
        in_specs=[hbm] * (ns + nl),
        out_specs=tuple([sem] * (2 * n) + [hbm] * (ns + nl) + [_whole()]),
        input_output_aliases={i: 2 * n + i for i in range(ns + nl)},
        compiler_params=pltpu.CompilerParams(has_side_effects=pltpu.SideEffectType.DATAFLOW_SIDE_EFFECTING),
    )(*buffers)
    return (rider, outs[:2 * n], outs[2 * n:2 * n + ns + nl]), outs[-1]


def _split_parts(state, riders):
    merged, sems, buffers = state
    n, ns = merged.n_copies, len(merged.srcs)
    parts, s0, l0, c0 = [], 0, 0, 0
    for r in riders:
        parts.append((r, list(sems[c0:c0 + r.n_copies]) + list(sems[n + c0:n + c0 + r.n_copies]),
                      list(buffers[s0:s0 + len(r.srcs)]) + list(buffers[ns + l0:ns + l0 + len(r.lands)])))
        s0, l0, c0 = s0 + len(r.srcs), l0 + len(r.lands), c0 + r.n_copies
    return parts


def _split_wait(name, state, after):
    rider, sems, buffers = state
    ns, nl, n = len(rider.srcs), len(rider.lands), rider.n_copies

    def body(*refs):
        srcs, lands = refs[:ns], refs[ns:ns + nl]
        sem_refs = refs[ns + nl:ns + nl + 2 * n]
        _, waits = rider.make([], srcs, lands, _SemList(sem_refs[:n]), _SemList(sem_refs[n:]))
        for wait in waits:
            wait()

    hbm = pl.BlockSpec(memory_space=pltpu.HBM)
    sem = pl.BlockSpec(memory_space=pltpu.SEMAPHORE)
    outs = pl.pallas_call(
        body, name=name,
        out_shape=tuple(pltpu.HBM(b.shape, b.dtype) for b in buffers),
        in_specs=[hbm] * (ns + nl) + [sem] * (2 * n) + [HBM_SPEC],
        out_specs=tuple([hbm] * (ns + nl)),
        input_output_aliases={i: i for i in range(ns + nl)},
        compiler_params=pltpu.CompilerParams(has_side_effects=pltpu.SideEffectType.DATAFLOW_SIDE_EFFECTING),
    )(*buffers, *sems, after)
    return list(outs[:ns]), list(outs[ns:])


def _pair_sum(pos, ws, g32s, recvs):
    n = len(ws)

    def body(pos_ref, *refs):
        g_refs, r_refs = refs[:n], refs[n:2 * n]
        p32_refs, p16_refs = refs[2 * n:3 * n], refs[3 * n:]
        for i in range(n):
            tot = g_refs[i][...] + r_refs[i][...].astype(F32)
            p16_refs[i][...] = tot.astype(BF16)

            @pl.when(pl.program_id(0) == pos_ref[1])
            def _(i=i, tot=tot):
                p32_refs[i][...] = tot

    halves = [(SHARD_SHAPES[w][0] // 2, SHARD_SHAPES[w][1]) for w in ws]
    own = [pl.BlockSpec((None, None) + h, lambda j, pos_ref: (j, pos_ref[0], 0, 0)) for h in halves]
    blk = [pl.BlockSpec((None,) + h, lambda j, pos_ref: (j, 0, 0)) for h in halves]
    mine = [pl.BlockSpec(h, lambda j, pos_ref: (0, 0)) for h in halves]
    g4 = [g.reshape((N_SHARD, 2) + h) for g, h in zip(g32s, halves)]
    outs = pl.pallas_call(
        body, name="pair_sum_" + "_".join(str(w) for w in ws),
        grid_spec=pltpu.PrefetchScalarGridSpec(
            num_scalar_prefetch=1, grid=(N_SHARD,), in_specs=own + blk, out_specs=mine + blk),
        out_shape=[jax.ShapeDtypeStruct(h, F32) for h in halves]
        + [jax.ShapeDtypeStruct((N_SHARD,) + h, BF16) for h in halves],
        compiler_params=_params(("arbitrary",)),
    )(pos, *g4, *recvs)
    return outs[:n], outs[n:]


def _chip_sum(pos, p32s, recvs):
    parts = 2

    def body(pos_ref, *refs):
        del pos_ref
        p_refs, r_refs, f_refs = refs[:N_BIG], refs[N_BIG:2 * N_BIG], refs[2 * N_BIG:]
        for w in range(N_BIG):
            f_refs[w][...] = ((p_refs[w][...] + r_refs[w][0].astype(F32)) + r_refs[w][1].astype(F32)) \
                + r_refs[w][2].astype(F32)

    quarters = [(r // 2 // parts, cc) for r, cc in SHARD_SHAPES]
    own = [pl.BlockSpec(qt, lambda i, pos_ref: (i, 0)) for qt in quarters]
    rcv = [pl.BlockSpec((3,) + qt, lambda i, pos_ref: (0, i, 0)) for qt in quarters]
    out = [pl.BlockSpec(qt, lambda i, pos_ref: (i, 0)) for qt in quarters]
    return pl.pallas_call(
        body, name="chip_sum",
        grid_spec=pltpu.PrefetchScalarGridSpec(
            num_scalar_prefetch=1, grid=(parts,), in_specs=own + rcv, out_specs=out),
        out_shape=[jax.ShapeDtypeStruct((r // 2, cc), F32) for r, cc in SHARD_SHAPES],
        compiler_params=_params(("arbitrary",)),
    )(pos, *p32s, *recvs)


def _adamw(w, g, m, v):
    m_new = ADAM_B1 * m + (1.0 - ADAM_B1) * g
    v_new = ADAM_B2 * v + (1.0 - ADAM_B2) * (g * g)
    m_hat = m_new / (1.0 - ADAM_B1 ** ADAM_STEP)
    v_hat = v_new / (1.0 - ADAM_B2 ** ADAM_STEP)
    delta = -ADAM_LR * (m_hat / (jnp.sqrt(v_hat) + ADAM_EPS) + ADAM_WD * w)
    return delta, m_new, v_new


def _adam_half(name, which, grads, ws, ms, vs, into=None):
    nb = 4

    def body(which_ref, *refs):
        del which_ref
        groups = [refs[i * N_BIG:(i + 1) * N_BIG] for i in range(4)]
        g_refs, w_refs, m_refs, v_refs = groups
        go_refs, do_refs, mo_refs, vo_refs = [refs[len(refs) - (4 - i) * N_BIG:len(refs) - (3 - i) * N_BIG]
                                              for i in range(4)]
        for w in range(N_BIG):
            g = g_refs[w][...]
            delta, m_new, v_new = _adamw(w_refs[w][...], g, m_refs[w][...], v_refs[w][...])
            go_refs[w][...] = g
            do_refs[w][...] = delta
            mo_refs[w][...] = m_new
            vo_refs[w][...] = v_new

    blocks = [(r // 2 // nb, cc) for r, cc in SHARD_SHAPES]
    half = [pl.BlockSpec(b, lambda i, which_ref: (i, 0)) for b in blocks]
    full = [pl.BlockSpec((None,) + b, lambda i, which_ref: (0, which_ref[0] * nb + i, 0)) for b in blocks]
    shapes = [jax.ShapeDtypeStruct((1,) + shp, F32) for shp in SHARD_SHAPES]
    carried = [] if into is None else [a for kind in into for a in kind]
    first = 1 + 4 * N_BIG
    outs = pl.pallas_call(
        body, name=name,
        grid_spec=pltpu.PrefetchScalarGridSpec(
            num_scalar_prefetch=1, grid=(nb,), in_specs=half + full * 3 + [HBM_SPEC] * len(carried),
            out_specs=full * 4),
        out_shape=shapes * 4,
        input_output_aliases={first + i: i for i in range(len(carried))},
        compiler_params=_params(("arbitrary",)),
    )(which, *grads, *ws, *ms, *vs, *carried)
    return [outs[i * N_BIG:(i + 1) * N_BIG] for i in range(4)]


SMALL_ROWS = 8
ROW_CONV_B, ROW_POOL_SCALE, ROW_LN1_G, ROW_LN1_B, ROW_LN2_G, ROW_LN2_B, ROW_LOSS = range(7)
SMALL_VECS = ((ROW_CONV_B, D_FF), (ROW_POOL_SCALE, POOL_W), (ROW_LN1_G, D_MODEL), (ROW_LN1_B, D_MODEL),
              (ROW_LN2_G, D_MODEL), (ROW_LN2_B, D_MODEL))


def _small_pack(loss, vec_grads):
    def body(*refs):
        loss_ref, gvec, out_ref = refs[0], refs[1:-1], refs[-1]
        out_ref[...] = jnp.zeros_like(out_ref)
        for (row, n), ref in zip(SMALL_VECS, gvec):
            out_ref[row:row + 1, 0:n] = ref[...]
        out_ref[ROW_LOSS:ROW_LOSS + 1, 0:HEAD_DIM] = jnp.broadcast_to(loss_ref[...], (1, HEAD_DIM))

    return pl.pallas_call(
        body, name="small_pack", in_specs=[_whole()] * (1 + len(vec_grads)), out_specs=_whole(),
        out_shape=jax.ShapeDtypeStruct((SMALL_ROWS, D_FF), F32),
    )(loss, *vec_grads)


def _small_pair_sum(own, sibling):
    n = len(own)

    def body(*refs):
        x, y, _ = _mesh_pos()
        for i in range(n):
            refs[2 * n + i][2 * x + y] = refs[i][...] + refs[n + i][...]

    return pl.pallas_call(
        body, name="small_pair_sum", in_specs=[_whole()] * (2 * n), out_specs=[_whole()] * n,
        out_shape=[jax.ShapeDtypeStruct((N_SHARD,) + a.shape, F32) for a in own],
        compiler_params=pltpu.CompilerParams(vmem_limit_bytes=VMEM_LIMIT),
    )(*own, *sibling)


def _small_chip_rider(gathered):
    n = len(gathered)

    def make(inplace, srcs, lands, send_sems, recv_sems):
        del inplace, lands
        x, y, c = _mesh_pos()
        j0 = 2 * x + y
        starts, waits = [], []
        for i in range(n):
            for k, chip in enumerate(_other_chips(x, y)):
                sems = dict(send_sem=send_sems.at[3 * i + k], recv_sem=recv_sems.at[3 * i + k],
                            device_id=(chip[0], chip[1], c), device_id_type=MESH)
                send = pltpu.make_async_remote_copy(src_ref=srcs[i].at[j0], dst_ref=srcs[i].at[j0], **sems)
                arrival = pltpu.make_async_remote_copy(
                    src_ref=srcs[i].at[j0], dst_ref=srcs[i].at[2 * chip[0] + chip[1]], **sems)
                starts.append(send)
                waits += [arrival.wait_recv, send.wait_send]
        return starts, waits

    return _Rider([], gathered, [], 3 * n, make)


def _small_adam(all_a, all_b, all_c, wp, cwp, vec_ws, m_wp, m_cwp, vec_ms, v_wp, v_cwp, vec_vs):
    nv = len(SMALL_VECS)
    np_ = 2 + nv

    def body(*refs):
        all_a_ref, all_b_ref, all_c_ref = refs[0:3]
        w_all, m_all, v_all = (refs[3 + i * np_:3 + (i + 1) * np_] for i in range(3))
        loss_out = refs[3 + 3 * np_]
        outs = refs[4 + 3 * np_:]
        x, y, _ = _mesh_pos()
        j0 = 2 * x + y
        tot_a = ((all_a_ref[0] + all_a_ref[1]) + all_a_ref[2]) + all_a_ref[3]
        tot_b = ((all_b_ref[0] + all_b_ref[1]) + all_b_ref[2]) + all_b_ref[3]
        tot_c = ((all_c_ref[0, j0] + all_c_ref[1, j0]) + all_c_ref[2, j0]) + all_c_ref[3, j0]
        loss_out[...] = tot_b[ROW_LOSS:ROW_LOSS + 1, 0:1]
        grads = [tot_a, tot_c] + [tot_b[row:row + 1, 0:n] for row, n in SMALL_VECS]
        for p in range(np_):
            delta, m_new, v_new = _adamw(w_all[p][...], grads[p], m_all[p][...], v_all[p][...])
            outs[p][...] = grads[p]
            outs[np_ + p][...] = delta
            outs[2 * np_ + p][...] = m_new
            outs[3 * np_ + p][...] = v_new

    pshapes = [wp.shape, CW_PAD] + [wv.shape for wv in vec_ws]
    out_shape = [jax.ShapeDtypeStruct((1, 1), F32)] + [jax.ShapeDtypeStruct(s, F32) for s in pshapes] * 4
    outs = pl.pallas_call(
        body, name="small_adam",
        in_specs=[_whole()] * (3 + 3 * np_), out_specs=[_whole()] * len(out_shape), out_shape=out_shape,
        compiler_params=pltpu.CompilerParams(vmem_limit_bytes=VMEM_LIMIT),
    )(all_a, all_b, all_c, wp, cwp, *vec_ws, m_wp, m_cwp, *vec_ms, v_wp, v_cwp, *vec_vs)
    return outs[0], [outs[1 + i * np_:1 + (i + 1) * np_] for i in range(4)]


def _pad_cw(a):
    pad = [(0, 0)] * (a.ndim - 2) + [(0, CW_PAD[0] - a.shape[-2]), (0, CW_PAD[1] - a.shape[-1])]
    return jnp.pad(a, pad)


def kernel(x, w_in, w_pool, pool_scale, w_out, ln1_g, ln1_b, w_up, conv_w, conv_b, w_down, ln2_g, ln2_b, loss_target, m_w_in, m_w_pool, m_pool_scale, m_w_out, m_ln1_g, m_ln1_b, m_w_up, m_conv_w, m_conv_b, m_w_down, m_ln2_g, m_ln2_b, v_w_in, v_w_pool, v_pool_scale, v_w_out, v_ln1_g, v_ln1_b, v_w_up, v_conv_w, v_conv_b, v_w_down, v_ln2_g, v_ln2_b):
    pos = jnp.stack([lax.axis_index("c"), 2 * lax.axis_index("x") + lax.axis_index("y")]).astype(jnp.int32)
    order = ("w_in", "w_out", "w_up", "w_down")
    w_in_i, w_out_i, w_up_i, w_down_i = range(N_BIG)
    vec_names = ("conv_b", "pool_scale", "ln1_g", "ln1_b", "ln2_g", "ln2_b")

    gathered = _gather_weights([w_in[0], w_out[0], w_up[0], w_down[0]], _pad_cw(conv_w[0]), (w_in_i,))
    cw_full = jnp.transpose(gathered[N_BIG][:, 0:3, 0:DOWN_SH], (1, 0, 2)).reshape(3, D_FF)
    up_a, up_b, up_c = (0, 224), (224, 160), (384, 128)
    assert up_c[0] + up_c[1] == SHARD_SHAPES[w_up_i][0] // 2

    class MeshComm:
        def __init__(self):
            self.w = {i: gathered[i] for i in range(N_BIG)}
            self.g32, self.g16, self.p32, self.p16, self.recv_b = {}, {}, {}, {}, {}
            self.up_complete = False
            self.tokens, self.chips = {}, []

        def weight(self, name):
            i = order.index(name)
            if name == "w_up" and not self.up_complete:
                (arrs, _), = _comm_only("gather_up_last", [_gather_rider(
                    {i: self.w[i]}, [("d2d_diag", i, up_b), ("d2d", i, up_c)])])
                self.w[i], self.up_complete = arrs[0], True
            full = self.w[i]
            return full.reshape(-1, full.shape[-1]) if name in ("w_out", "w_down") else full

        def _gather(self, ws, ops):
            return _gather_rider({w: self.w[w] for w in ws}, ops), ("w", ws)

        def _pair(self, ws):
            return _pair_rider(ws, [self.g16[w] for w in ws]), ("recv_a", ws)

        def _chip(self, ws, rows=None, resume=False):
            landing = [self.recv_b[w] for w in ws] if resume else None
            return _chip_rider(ws, [self.p16[w] for w in ws], rows, landing), ("recv_b", ws)

        def plan(self, call):
            out_all, down_all = _whole_half(w_out_i), _whole_half(w_down_i)
            if call == "proj_pool":
                return [self._gather([w_out_i, w_up_i, w_down_i],
                                     [("ici", w_out_i, out_all), ("nbr", w_down_i, down_all),
                                      ("nbr", w_up_i, up_a)])]
            if call == "retention_fwd":
                return [self._gather([w_out_i, w_up_i, w_down_i],
                                     [("d2d", w_out_i, out_all),
                                      ("relay", w_down_i, down_all), ("d2d_nbr", w_down_i, down_all),
                                      ("relay", w_up_i, up_a), ("d2d_nbr", w_up_i, up_a), ("nbr", w_up_i, up_b)])]
            if call == "outproj_ln1":
                return [self._gather([w_up_i, w_down_i],
                                     [("d2d_diag", w_down_i, down_all), ("d2d_diag", w_up_i, up_a),
                                      ("relay", w_up_i, up_b), ("d2d_nbr", w_up_i, up_b), ("ici", w_up_i, up_c)])]
            return []

        def after(self, call):
            return tuple(self.tokens.pop(call, ()))

        def riders(self, call):
            self.pending = self.plan(call)
            return [r for r, _ in self.pending]

        def _start(self, name, rider, before):
            state, token = _split_start(name, rider)
            self.tokens.setdefault(before, []).append(token)
            return state

        def _finish_pair(self, name, state, ws, after):
            _, lands = _split_wait(name, state, after)
            self._finish_sum(ws, lands)

        def landed(self, call, results, outs):
            for (_, (slot, ws)), (inplace, lands) in zip(self.pending, results):
                for w, arr in zip(ws, inplace if len(inplace) else lands):
                    getattr(self, slot)[w] = arr
            if call == "wgrad_out":
                self._finish_pair("pair_exchange_up_wait", self.pair_up, [w_up_i], outs[1])
                self.chips.append(([w_up_i], self._start(
                    "chip_exchange_up_start", self._chip([w_up_i])[0], "wgrad_down")))
            if call == "mix_bwd":
                ws = [w_out_i, w_down_i]
                self._finish_pair("pair_exchange_out_down_wait", self.pair_out_down, ws, outs[0])
            if call == "retention_bwd":
                own, sibling = _split_wait("small_pair_wait", self.small_pair, outs[0])
                self.small_chip = self._start(
                    "small_chip_start", _small_chip_rider(_small_pair_sum(own, sibling)), "wgrad_in")

        def small_gradients(self, loss, small):
            dcw4 = _pad_cw(jnp.transpose(small["conv_w"].reshape(3, N_SHARD, DOWN_SH), (1, 0, 2)))
            own = [small["w_pool"], _small_pack(loss, [small[n] for n in vec_names]), dcw4]
            ws = [w_out_i, w_down_i]
            parts = [self._chip(ws)[0], _final_rider(own)]
            chip, self.small_pair = _split_parts(
                self._start("chip_out_down_small_pair_start", _merged_rider(parts), "retention_bwd"), parts)
            self.chips.append((ws, chip))

        def gradient(self, name, g32, g16):
            w = order.index(name)
            shape = (N_SHARD,) + SHARD_SHAPES[w]
            self.g32[w], self.g16[w] = g32.reshape(shape), g16.reshape(shape)
            if name == "w_up":
                self.pair_up = self._start("pair_exchange_up_start", self._pair([w])[0], "wgrad_out")
            if name == "w_down":
                self.pair_out_down = self._start("pair_exchange_out_down_start",
                                                 self._pair([w_out_i, w_down_i])[0], "mix_bwd")
            if name == "w_in":
                (_, lands), = _comm_only("pair_exchange_in", [self._pair([w])[0]])
                self._finish_sum([w], lands)
                self.chips.append(([w], self._start("chip_exchange_in_start", self._chip([w])[0], "dx")))

        def _finish_sum(self, ws, lands):
            p32s, p16s = _pair_sum(pos, ws, [self.g32[w] for w in ws], lands)
            for w, p32, p16 in zip(ws, p32s, p16s):
                self.p32[w], self.p16[w] = p32, p16

        def finish(self, after):
            for n, (ws, state) in enumerate(self.chips):
                _, lands = _split_wait("chip_exchange_wait_%d" % n, state, after)
                for w, arr in zip(ws, lands):
                    self.recv_b[w] = arr
            return _split_wait("small_chip_wait", self.small_chip, after)[0]

    comm = MeshComm()
    loss, grad_x, small = _local_step(x[0], loss_target[0], cw_full, conv_b, w_pool[0], pool_scale,
                                      ln1_g, ln1_b, ln2_g, ln2_b, comm)

    given = dict(w_pool=w_pool, pool_scale=pool_scale, ln1_g=ln1_g, ln1_b=ln1_b, conv_w=conv_w, conv_b=conv_b,
                 ln2_g=ln2_g, ln2_b=ln2_b)
    given_m = dict(w_pool=m_w_pool, pool_scale=m_pool_scale, ln1_g=m_ln1_g, ln1_b=m_ln1_b, conv_w=m_conv_w,
                   conv_b=m_conv_b, ln2_g=m_ln2_g, ln2_b=m_ln2_b)
    given_v = dict(w_pool=v_w_pool, pool_scale=v_pool_scale, ln1_g=v_ln1_g, ln1_b=v_ln1_b, conv_w=v_conv_w,
                   conv_b=v_conv_b, ln2_g=v_ln2_g, ln2_b=v_ln2_b)
    args = []
    for src in (given, given_m, given_v):
        args += [src["w_pool"][0], _pad_cw(src["conv_w"][0]), [src[n] for n in vec_names]]
    small_sums = comm.finish(grad_x)
    loss_tot, small_out = _small_adam(*small_sums, *args)
    every = range(N_BIG)
    mine = _chip_sum(pos, [comm.p32[w] for w in every], [comm.recv_b[w] for w in every])
    final_state, _ = _split_start("pair_exchange_f32_start", _final_rider(mine))
    mine = final_state[2][:N_BIG]
    big = ([w_in, w_out, w_up, w_down], [m_w_in, m_w_out, m_w_up, m_w_down], [v_w_in, v_w_out, v_w_up, v_w_down])
    own_half = _adam_half("adam_own_half", pos[0:1], mine, *big)
    _, theirs = _split_wait("pair_exchange_f32_wait", final_state, own_half[0][0])
    big_out = _adam_half("adam_other_half", 1 - pos[0:1], theirs, *big, into=own_half)

    names = ("w_in", "w_pool", "pool_scale", "w_out", "ln1_g", "ln1_b", "w_up", "conv_w", "conv_b", "w_down",
             "ln2_g", "ln2_b")
    small_names = ("w_pool", "conv_w") + vec_names
    result = [loss_tot.reshape(()), grad_x[None]]
    for kind in range(4):
        for n in names:
            if n in order:
                result.append(big_out[kind][order.index(n)])
            else:
                val = small_out[kind][small_names.index(n)]
                if n == "conv_w":
                    val = val[0:3, 0:DOWN_SH][None]
                elif n == "w_pool":
                    val = val[None]
                result.append(val)
    return tuple(result)
```

```python
import functools

import numpy as np
import jax
import jax.numpy as jnp
from jax import lax
from jax.experimental import pallas as pl
from jax.experimental.pallas import tpu as pltpu

F32 = jnp.float32
BF16 = jnp.bfloat16

D_MODEL = 1024
HEADS = 4
HEAD_DIM = 128
RET_W = HEADS * HEAD_DIM
POOL_WINDOWS = (2, 4, 8, 16)
POOL_W = 512
IN_W = 4 * RET_W + POOL_W
D_FF = 2816
N_SHARD = 4
IN_SH = IN_W // N_SHARD
UP_SH = 2 * D_FF // N_SHARD
DOWN_SH = D_FF // N_SHARD
OUT_SH = D_MODEL // N_SHARD
ROPE_BASE = 10000.0
LN_EPS = 1e-5
RMS_EPS = 1e-6
ALPHA = 2.0 ** 0.25
K_SCALE = HEAD_DIM ** -0.5
SUPER = 256
CHUNK = 64
POOL_HALO = 16
CONV_HALO = 8
FFN_STRIP = 128
LN_ROWS = 32

ADAM_LR = 0.001
ADAM_B1 = 0.9
ADAM_B2 = 0.999
ADAM_EPS = 1e-08
ADAM_WD = 0.01
ADAM_STEP = 10

MESH = pl.DeviceIdType.MESH
VMEM_LIMIT = 56 * 1024 * 1024


def _dot(a, b):
    return jnp.dot(a, b, preferred_element_type=F32)


def _dot_nt(a, b):
    return lax.dot_general(a, b, (((1,), (1,)), ((), ())), preferred_element_type=F32)


def _dot_tn(a, b):
    return lax.dot_general(a, b, (((0,), (0,)), ((), ())), preferred_element_type=F32)


def _sigmoid(x):
    return 1.0 / (1.0 + jnp.exp(-x))


def _params(sem):
    return pltpu.CompilerParams(dimension_semantics=sem, vmem_limit_bytes=VMEM_LIMIT)


def _whole():
    return pl.BlockSpec(memory_space=pltpu.VMEM)


HBM_SPEC = pl.BlockSpec(memory_space=pl.ANY)


class _Rider:
    def __init__(self, inplace, srcs, lands, n_copies, make):
        self.inplace, self.srcs, self.lands, self.n_copies, self.make = list(inplace), list(srcs), list(lands), n_copies, make


def _call(body, *, name, grid, in_specs, out_specs, out_shape, operands, scratch_shapes=(), sem=(),
          aliases=None, riders=(), after=()):
    n_in, n_out, n_scr = len(in_specs), len(out_shape), len(scratch_shapes)
    in_specs, out_specs, out_shape = list(in_specs), list(out_specs), list(out_shape)
    operands, scratch_shapes, aliases = list(operands), list(scratch_shapes), dict(aliases or {})
    in_specs += [_whole()] * len(after)
    operands += list(after)
    for r in riders:
        for a in r.inplace:
            aliases[len(in_specs)] = len(out_shape)
            in_specs.append(HBM_SPEC)
            operands.append(a)
            out_specs.append(HBM_SPEC)
            out_shape.append(jax.ShapeDtypeStruct(a.shape, a.dtype))
        for a in r.srcs:
            in_specs.append(HBM_SPEC)
            operands.append(a)
        for shp in r.lands:
            out_specs.append(HBM_SPEC)
            out_shape.append(shp)
        scratch_shapes += [pltpu.SemaphoreType.DMA((r.n_copies,)), pltpu.SemaphoreType.DMA((r.n_copies,))]

    def full(*refs):
        ins = refs[:n_in]
        at = n_in + len(after)
        r_srcs = []
        for r in riders:
            at += len(r.inplace)
            r_srcs.append(refs[at:at + len(r.srcs)])
            at += len(r.srcs)
        outs = refs[at:at + n_out]
        at += n_out
        r_outs = []
        for r in riders:
            r_outs.append((refs[at:at + len(r.inplace)], refs[at + len(r.inplace):at + len(r.inplace) + len(r.lands)]))
            at += len(r.inplace) + len(r.lands)
        scr = refs[at:at + n_scr]
        at += n_scr
        r_sems = [refs[at + 2 * i:at + 2 * i + 2] for i in range(len(riders))]

        def copies():
            return [r.make(r_outs[i][0], r_srcs[i], r_outs[i][1], r_sems[i][0], r_sems[i][1])
                    for i, r in enumerate(riders)]

        def start():
            for starts, _ in copies():
                for cp in starts:
                    cp.start()

        def finish():
            for _, waits in copies():
                for wait in waits:
                    wait()

        if riders and grid:
            first = functools.reduce(jnp.logical_and, [pl.program_id(d) == 0 for d in range(len(grid))])
            last = functools.reduce(jnp.logical_and, [pl.program_id(d) == grid[d] - 1 for d in range(len(grid))])
            pl.when(first)(start)
            body(*ins, *outs, *scr)
            pl.when(last)(finish)
        else:
            if riders:
                start()
            body(*ins, *outs, *scr)
            if riders:
                finish()

    params = _params(sem) if grid else pltpu.CompilerParams(vmem_limit_bytes=VMEM_LIMIT)
    res = pl.pallas_call(
        full, name=name, grid=grid, in_specs=in_specs, out_specs=out_specs, out_shape=out_shape,
        scratch_shapes=scratch_shapes, input_output_aliases=aliases, compiler_params=params,
    )(*operands)
    outs, at, rider_res = res[:n_out], n_out, []
    for r in riders:
        rider_res.append((res[at:at + len(r.inplace)], res[at + len(r.inplace):at + len(r.inplace) + len(r.lands)]))
        at += len(r.inplace) + len(r.lands)
    return list(outs), rider_res


def _gammas():
    return [1.0 - 2.0 ** (-5.0 - h) for h in range(HEADS)]


def _decay_tables():
    idx = np.arange(SUPER)
    dist = np.abs(idx[:, None] - idx[None, :]).astype(np.float64)
    visible = (idx[None, :] // CHUNK) <= (idx[:, None] // CHUNK)
    mask = np.stack([np.where(visible, g ** dist, 0.0) for g in _gammas()])
    qd = np.concatenate([np.repeat((g ** (idx + 1.0))[:, None], HEAD_DIM, 1) for g in _gammas()], 1)
    kd = np.concatenate([np.repeat((g ** (SUPER - 1.0 - idx))[:, None], HEAD_DIM, 1) for g in _gammas()], 1)
    return (jnp.asarray(mask, F32), jnp.asarray(qd, F32), jnp.asarray(kd, F32))


def _rope_tables(s):
    inv_freq = ROPE_BASE ** (-np.arange(0, HEAD_DIM, 2, dtype=np.float64) / HEAD_DIM)
    ang = np.arange(s, dtype=np.float64)[:, None] * inv_freq[None, :]
    cos, sin = np.cos(ang), np.sin(ang)
    return (jnp.asarray(np.concatenate([cos, cos], 1), F32),
            jnp.asarray(np.concatenate([-sin, sin], 1), F32))


def _rope(t, cosf, sinf):
    return t * cosf + pltpu.roll(t, HEAD_DIM // 2, 1) * sinf


def _rope_t(t, cosf, sinf):
    return t * cosf - pltpu.roll(t, HEAD_DIM // 2, 1) * sinf


def _layernorm_fwd(z):
    mu = jnp.mean(z, axis=-1, keepdims=True)
    zc = z - mu
    var = jnp.mean(zc * zc, axis=-1, keepdims=True)
    rstd = lax.rsqrt(var + LN_EPS)
    return zc * rstd, rstd


def _layernorm_bwd(dy, xhat, rstd, gain):
    dxh = dy * gain
    m1 = jnp.mean(dxh, axis=-1, keepdims=True)
    m2 = jnp.mean(dxh * xhat, axis=-1, keepdims=True)
    return rstd * (dxh - m1 - xhat * m2)


def _proj_pool(x, win4, cosf, sinf, wpool, pscale, ts, riders=(), after=()):
    s = x.shape[0]
    nt = s // ts

    def body(x_ref, w_ref, cos_ref, sin_ref, wp_ref, ps_ref,
             xb_ref, q_ref, k_ref, v_ref, g_ref, pooled_ref, cat_ref, proj_scr, pext_scr):
        i = pl.program_id(0)
        xb = x_ref[...].astype(BF16)
        xb_ref[...] = xb
        for j in range(N_SHARD):
            proj_scr[:, j * IN_SH:(j + 1) * IN_SH] = _dot(xb, w_ref[j])
        cosf_t = cos_ref[...]
        sinf_t = sin_ref[...]
        for h in range(HEADS):
            lo = h * HEAD_DIM
            q_ref[:, lo:lo + HEAD_DIM] = _rope(proj_scr[:, lo:lo + HEAD_DIM], cosf_t, sinf_t).astype(BF16)
            kk = _rope(proj_scr[:, RET_W + lo:RET_W + lo + HEAD_DIM], cosf_t, sinf_t) * K_SCALE
            k_ref[:, lo:lo + HEAD_DIM] = kk.astype(BF16)
        v_ref[...] = proj_scr[:, 2 * RET_W:3 * RET_W].astype(BF16)
        g_ref[...] = proj_scr[:, 3 * RET_W:4 * RET_W]

        @pl.when(i == 0)
        def _():
            pext_scr[0:POOL_HALO, :] = jnp.zeros((POOL_HALO, POOL_W), F32)

        pext_scr[POOL_HALO:POOL_HALO + ts, :] = proj_scr[:, 4 * RET_W:IN_W]
        pos = (i * ts + lax.broadcasted_iota(jnp.int32, (ts, 1), 0) + 1).astype(F32)
        for gi, w in enumerate(POOL_WINDOWS):
            lo = gi * HEAD_DIM
            ext = pext_scr[:, lo:lo + HEAD_DIM]
            acc = ext
            shift = 1
            while shift < w:
                acc = acc + pltpu.roll(acc, shift, 0)
                shift *= 2
            tok = ext[POOL_HALO:POOL_HALO + ts]
            pooled = acc[POOL_HALO:POOL_HALO + ts] / jnp.minimum(pos, float(w)) - tok
            pooled_b = pooled.astype(BF16)
            pooled_ref[:, lo:lo + HEAD_DIM] = pooled_b
            lin = _dot(pooled_b, wp_ref[gi])
            cat_ref[:, lo:lo + HEAD_DIM] = (lin * ps_ref[:, lo:lo + HEAD_DIM]).astype(BF16)
        pext_scr[0:POOL_HALO, :] = pext_scr[ts:ts + POOL_HALO, :]

    tile = lambda w: pl.BlockSpec((ts, w), lambda i: (i, 0))
    return _call(
        body, name="proj_pool", grid=(nt,),
        in_specs=[tile(D_MODEL), _whole(), tile(HEAD_DIM), tile(HEAD_DIM), _whole(), _whole()],
        out_specs=[tile(D_MODEL), tile(RET_W), tile(RET_W), tile(RET_W), tile(RET_W), tile(POOL_W),
                   pl.BlockSpec((ts, POOL_W), lambda i: (i, 1))],
        out_shape=[jax.ShapeDtypeStruct((s, D_MODEL), BF16), jax.ShapeDtypeStruct((s, RET_W), BF16),
                   jax.ShapeDtypeStruct((s, RET_W), BF16), jax.ShapeDtypeStruct((s, RET_W), BF16),
                   jax.ShapeDtypeStruct((s, RET_W), F32), jax.ShapeDtypeStruct((s, POOL_W), BF16),
                   jax.ShapeDtypeStruct((s, 2 * RET_W), BF16)],
        scratch_shapes=[pltpu.VMEM((ts, IN_W), F32), pltpu.VMEM((ts + POOL_HALO, POOL_W), F32)],
        sem=("arbitrary",), operands=(x, win4, cosf, sinf, wpool, pscale), riders=riders, after=after,
    )


def _retention_fwd(q, k, v, g, cat, mask, qd, kd, riders=(), after=()):
    s = q.shape[0]
    ns = s // SUPER
    cdec = [gm ** float(SUPER) for gm in _gammas()]

    def body(q_ref, k_ref, v_ref, g_ref, cat_in, mask_ref, qd_ref, kd_ref,
             ret_ref, cat_ref, st_ref, state_scr):
        del cat_in
        n = pl.program_id(0)

        @pl.when(n == 0)
        def _():
            state_scr[...] = jnp.zeros_like(state_scr)

        for h in range(HEADS):
            sl = slice(h * HEAD_DIM, (h + 1) * HEAD_DIM)
            qh, kh, vh = q_ref[:, sl], k_ref[:, sl], v_ref[:, sl]
            sc = _dot_nt(qh, kh) * mask_ref[h]
            st = state_scr[h]
            stb = st.astype(BF16)
            st_ref[0, h] = stb
            qdb = (qh.astype(F32) * qd_ref[:, sl]).astype(BF16)
            kdb = (kh.astype(F32) * kd_ref[:, sl]).astype(BF16)
            ret = _dot(sc.astype(BF16), vh) + _dot(qdb, stb)
            state_scr[h] = st * cdec[h] + _dot_tn(kdb, vh)
            ret_ref[:, sl] = ret
            r = lax.rsqrt(jnp.mean(ret * ret, axis=-1, keepdims=True) + RMS_EPS)
            gh = g_ref[:, sl]
            cat_ref[:, sl] = ((ret * r) * (gh * _sigmoid(gh))).astype(BF16)

    tile = pl.BlockSpec((SUPER, RET_W), lambda n: (n, 0))
    return _call(
        body, name="retention_fwd", grid=(ns,),
        in_specs=[tile, tile, tile, tile, HBM_SPEC, _whole(), _whole(), _whole()],
        out_specs=[tile, tile, pl.BlockSpec((1, HEADS, HEAD_DIM, HEAD_DIM), lambda n: (n, 0, 0, 0))],
        out_shape=[jax.ShapeDtypeStruct((s, RET_W), F32), jax.ShapeDtypeStruct((s, 2 * RET_W), BF16),
                   jax.ShapeDtypeStruct((ns, HEADS, HEAD_DIM, HEAD_DIM), BF16)],
        scratch_shapes=[pltpu.VMEM((HEADS, HEAD_DIM, HEAD_DIM), F32)],
        aliases={4: 1}, sem=("arbitrary",), operands=(q, k, v, g, cat, mask, qd, kd), riders=riders,
        after=after,
    )


def _outproj_ln1(x, cat, wout, g1, b1, ts, riders=(), after=()):
    s = x.shape[0]

    def body(x_ref, cat_ref, w_ref, g_ref, b_ref, xhat_ref, rstd_ref, h1b_ref):
        z = ALPHA * x_ref[...] + _dot(cat_ref[...], w_ref[...])
        xhat, rstd = _layernorm_fwd(z)
        xhat_ref[...] = xhat
        rstd_ref[...] = rstd
        h1b_ref[...] = (xhat * g_ref[...] + b_ref[...]).astype(BF16)

    tile = lambda w: pl.BlockSpec((ts, w), lambda i: (i, 0))
    return _call(
        body, name="outproj_ln1", grid=(s // ts,),
        in_specs=[tile(D_MODEL), tile(D_MODEL), _whole(), _whole(), _whole()],
        out_specs=[tile(D_MODEL), tile(1), tile(D_MODEL)],
        out_shape=[jax.ShapeDtypeStruct((s, D_MODEL), F32), jax.ShapeDtypeStruct((s, 1), F32),
                   jax.ShapeDtypeStruct((s, D_MODEL), BF16)],
        sem=("arbitrary",), operands=(x, cat, wout, g1, b1), riders=riders, after=after,
    )


def _ffn_fwd_loss(xhat1, h1b, target, wup4, wdown, cw, cb, g1, b1, g2, b2, ts):
    s = xhat1.shape[0]

    def body(xhat_ref, h1b_ref, tgt_ref, wup_ref, wdn_ref, cw_ref, cb_ref, g1_ref, b1_ref, g2_ref, b2_ref,
             ub_ref, act_ref, sd_ref, dz2_ref, dz2b_ref, loss_ref, dg2_ref, db2_ref, val_scr, gext_scr, ffn_scr):
        i = pl.program_id(0)

        @pl.when(i == 0)
        def _():
            gext_scr[0:CONV_HALO, :] = jnp.zeros((CONV_HALO, D_FF), F32)
            loss_ref[...] = jnp.zeros_like(loss_ref)
            dg2_ref[...] = jnp.zeros_like(dg2_ref)
            db2_ref[...] = jnp.zeros_like(db2_ref)

        for half in range(2):
            lo = half * UP_SH
            gext_scr[CONV_HALO:CONV_HALO + ts, lo:lo + UP_SH] = _dot(h1b_ref[...], wup_ref[2 + half])
            val_scr[:, lo:lo + UP_SH] = _dot(h1b_ref[...], wup_ref[half])
            for c0 in range(lo, lo + UP_SH, FFN_STRIP):
                cols = slice(c0, c0 + FFN_STRIP)
                ext = gext_scr[:, cols]
                gate = ext[CONV_HALO:]
                hc = cb_ref[:, cols] + ((pltpu.roll(ext, 2, 0)[CONV_HALO:] * cw_ref[0:1, cols]
                                         + pltpu.roll(ext, 1, 0)[CONV_HALO:] * cw_ref[1:2, cols])
                                        + gate * cw_ref[2:3, cols])
                val = val_scr[:, cols]
                sg = _sigmoid(hc)
                si = hc * sg
                act_ref[:, cols] = (si * val).astype(BF16)
                ub_ref[:, cols] = val.astype(BF16)
                ub_ref[:, D_FF + c0:D_FF + c0 + FFN_STRIP] = gate.astype(BF16)
                sd_ref[:, cols] = hc.astype(BF16)
            part = _dot(act_ref[:, lo:lo + UP_SH], wdn_ref[lo:lo + UP_SH, :])
            if half == 0:
                ffn_scr[...] = part
            else:
                ffn_scr[...] += part

        gext_scr[0:CONV_HALO, :] = gext_scr[ts:ts + CONV_HALO, :]

        loss_acc = jnp.zeros((1, 1), F32)
        dg2_acc = jnp.zeros((1, D_MODEL), F32)
        db2_acc = jnp.zeros((1, D_MODEL), F32)
        for r0 in range(0, ts, LN_ROWS):
            rows = slice(r0, r0 + LN_ROWS)
            h1 = xhat_ref[rows, :] * g1_ref[...] + b1_ref[...]
            xhat2, rstd2 = _layernorm_fwd(ALPHA * h1 + ffn_scr[rows, :])
            diff = (xhat2 * g2_ref[...] + b2_ref[...]) - tgt_ref[rows, :]
            row = jnp.mean(diff * diff, axis=-1, keepdims=True)
            loss_acc = loss_acc + 0.5 * jnp.sum(row, axis=0, keepdims=True)
            dy = diff * (1.0 / D_MODEL)
            dg2_acc = dg2_acc + jnp.sum(dy * xhat2, axis=0, keepdims=True)
            db2_acc = db2_acc + jnp.sum(dy, axis=0, keepdims=True)
            dz2 = _layernorm_bwd(dy, xhat2, rstd2, g2_ref[...])
            dz2_ref[rows, :] = dz2
            dz2b_ref[rows, :] = dz2.astype(BF16)
        loss_ref[...] += loss_acc
        dg2_ref[...] += dg2_acc
        db2_ref[...] += db2_acc

    tile = lambda w: pl.BlockSpec((ts, w), lambda i: (i, 0))
    acc = lambda w: pl.BlockSpec((1, w), lambda i: (0, 0))
    return pl.pallas_call(
        body, name="ffn_fwd_loss", grid=(s // ts,),
        in_specs=[tile(D_MODEL), tile(D_MODEL), tile(D_MODEL)] + [_whole()] * 8,
        out_specs=[tile(2 * D_FF), tile(D_FF), tile(D_FF), tile(D_MODEL), tile(D_MODEL),
                   acc(1), acc(D_MODEL), acc(D_MODEL)],
        out_shape=[jax.ShapeDtypeStruct((s, 2 * D_FF), BF16), jax.ShapeDtypeStruct((s, D_FF), BF16),
                   jax.ShapeDtypeStruct((s, D_FF), BF16), jax.ShapeDtypeStruct((s, D_MODEL), F32),
                   jax.ShapeDtypeStruct((s, D_MODEL), BF16),
                   jax.ShapeDtypeStruct((1, 1), F32), jax.ShapeDtypeStruct((1, D_MODEL), F32),
                   jax.ShapeDtypeStruct((1, D_MODEL), F32)],
        scratch_shapes=[pltpu.VMEM((ts, D_FF), F32), pltpu.VMEM((ts + CONV_HALO, D_FF), F32),
                        pltpu.VMEM((ts, D_MODEL), F32)],
        compiler_params=_params(("arbitrary",)),
    )(xhat1, h1b, target, wup4, wdown, cw, cb, g1, b1, g2, b2)


def _ffn_bwd(dz2, dz2b, ub, sd, xhat1, rstd1, wup4, wdown, cw, g1, ts):
    s = dz2.shape[0]
    nt = s // ts

    def body(dz2_ref, dz2b_ref, ub_ref, sd_ref, xhat_ref, rstd_ref, wup_ref, wdn_ref, cw_ref, g1_ref,
             dub_ref, dz1_ref, dz1b_ref, dg1_ref, db1_ref, dcw_ref, dcb_ref, dext_scr, da_scr):
        i = pl.program_id(0)

        @pl.when(i == 0)
        def _():
            dext_scr[ts:ts + CONV_HALO, :] = jnp.zeros((CONV_HALO, D_FF), F32)
            dg1_ref[...] = jnp.zeros_like(dg1_ref)
            db1_ref[...] = jnp.zeros_like(db1_ref)
            dcw_ref[...] = jnp.zeros_like(dcw_ref)
            dcb_ref[...] = jnp.zeros_like(dcb_ref)

        da_scr[...] = _dot_nt(dz2b_ref[...], wdn_ref[...])
        n_ext = ts + CONV_HALO
        for c0 in range(0, D_FF, FFN_STRIP):
            cols = slice(c0, c0 + FFN_STRIP)
            gcols = slice(D_FF + c0, D_FF + c0 + FFN_STRIP)
            val = ub_ref[:, cols].astype(F32)
            gate = ub_ref[:, gcols].astype(F32)
            da = da_scr[:, cols]
            hc = sd_ref[:, cols].astype(F32)
            sg = _sigmoid(hc)
            dhc = da * val * (sg * (1.0 + hc * (1.0 - sg)))
            dext_scr[0:ts, cols] = dhc
            dext = dext_scr[:, cols]
            dhc1 = pltpu.roll(dext, n_ext - 1, 0)[0:ts]
            dhc2 = pltpu.roll(dext, n_ext - 2, 0)[0:ts]
            dcb_ref[:, cols] += jnp.sum(dhc, axis=0, keepdims=True)
            dcw_ref[0:1, cols] += jnp.sum(dhc2 * gate, axis=0, keepdims=True)
            dcw_ref[1:2, cols] += jnp.sum(dhc1 * gate, axis=0, keepdims=True)
            dcw_ref[2:3, cols] += jnp.sum(dhc * gate, axis=0, keepdims=True)
            dgate = dhc * cw_ref[2:3, cols] + dhc1 * cw_ref[1:2, cols] + dhc2 * cw_ref[0:1, cols]
            dub_ref[:, cols] = (da * (hc * sg)).astype(BF16)
            dub_ref[:, gcols] = dgate.astype(BF16)
        dext_scr[ts:n_ext, :] = dext_scr[0:CONV_HALO, :]
        dh1 = ALPHA * dz2_ref[...]
        for j in range(N_SHARD):
            dh1 = dh1 + _dot_nt(dub_ref[:, j * UP_SH:(j + 1) * UP_SH], wup_ref[j])
        xhat = xhat_ref[...]
        dg1_ref[...] += jnp.sum(dh1 * xhat, axis=0, keepdims=True)
        db1_ref[...] += jnp.sum(dh1, axis=0, keepdims=True)
        dz1 = _layernorm_bwd(dh1, xhat, rstd_ref[...], g1_ref[...])
        dz1_ref[...] = dz1
        dz1b_ref[...] = dz1.astype(BF16)

    tile = lambda w: pl.BlockSpec((ts, w), lambda i: (nt - 1 - i, 0))
    acc = lambda rws, w: pl.BlockSpec((rws, w), lambda i: (0, 0))
    return pl.pallas_call(
        body, name="ffn_bwd", grid=(nt,),
        in_specs=[tile(D_MODEL), tile(D_MODEL), tile(2 * D_FF), tile(D_FF), tile(D_MODEL), tile(1)]
        + [_whole()] * 4,
        out_specs=[tile(2 * D_FF), tile(D_MODEL), tile(D_MODEL), acc(1, D_MODEL), acc(1, D_MODEL),
                   acc(3, D_FF), acc(1, D_FF)],
        out_shape=[jax.ShapeDtypeStruct((s, 2 * D_FF), BF16),
                   jax.ShapeDtypeStruct((s, D_MODEL), F32), jax.ShapeDtypeStruct((s, D_MODEL), BF16),
                   jax.ShapeDtypeStruct((1, D_MODEL), F32),
                   jax.ShapeDtypeStruct((1, D_MODEL), F32), jax.ShapeDtypeStruct((3, D_FF), F32),
                   jax.ShapeDtypeStruct((1, D_FF), F32)],
        scratch_shapes=[pltpu.VMEM((ts + CONV_HALO, D_FF), F32), pltpu.VMEM((ts, D_FF), F32)],
        compiler_params=_params(("arbitrary",)),
    )(dz2, dz2b, ub, sd, xhat1, rstd1, wup4, wdown, cw, g1)


def _mix_bwd(dz1, pooled, ret, g, wout, wpool, pscale, ts, riders=(), after=()):
    s = dz1.shape[0]
    nt = s // ts

    def body(dz1_ref, pooled_ref, ret_ref, g_ref, wout_ref, wp_ref, ps_ref,
             dret_ref, dgp_ref, dwp_ref, dps_ref, eext_scr):
        i = pl.program_id(0)
        r = nt - 1 - i

        @pl.when(i == 0)
        def _():
            eext_scr[ts:ts + POOL_HALO, :] = jnp.zeros((POOL_HALO, POOL_W), F32)
            dwp_ref[...] = jnp.zeros_like(dwp_ref)
            dps_ref[...] = jnp.zeros_like(dps_ref)

        dzb = dz1_ref[...].astype(BF16)
        dcat_r = _dot_nt(dzb, wout_ref[0:RET_W, :])
        dcat_p = _dot_nt(dzb, wout_ref[RET_W:2 * RET_W, :])
        pos = (r * ts + lax.broadcasted_iota(jnp.int32, (ts, 1), 0) + 1).astype(F32)
        dpooled = []
        for gi, w in enumerate(POOL_WINDOWS):
            sl = slice(gi * HEAD_DIM, (gi + 1) * HEAD_DIM)
            pb = pooled_ref[:, sl]
            dy = dcat_p[:, sl]
            dps_ref[:, sl] += jnp.sum(dy * _dot(pb, wp_ref[gi]), axis=0, keepdims=True)
            dlin = (dy * ps_ref[:, sl]).astype(BF16)
            dwp_ref[gi] += _dot_tn(pb, dlin)
            dpg = _dot_nt(dlin, wp_ref[gi])
            dpooled.append(dpg)
            eext_scr[0:ts, sl] = dpg / jnp.minimum(pos, float(w))
        for gi, w in enumerate(POOL_WINDOWS):
            sl = slice(gi * HEAD_DIM, (gi + 1) * HEAD_DIM)
            acc = eext_scr[:, sl]
            shift = 1
            while shift < w:
                acc = acc + pltpu.roll(acc, ts + POOL_HALO - shift, 0)
                shift *= 2
            dgp_ref[:, RET_W + gi * HEAD_DIM:RET_W + (gi + 1) * HEAD_DIM] = (acc[0:ts] - dpooled[gi]).astype(BF16)
        eext_scr[ts:ts + POOL_HALO, :] = eext_scr[0:POOL_HALO, :]
        for h in range(HEADS):
            sl = slice(h * HEAD_DIM, (h + 1) * HEAD_DIM)
            rt = ret_ref[:, sl]
            rr = lax.rsqrt(jnp.mean(rt * rt, axis=-1, keepdims=True) + RMS_EPS)
            rn = rt * rr
            gh = g_ref[:, sl]
            sg = _sigmoid(gh)
            dy = dcat_r[:, sl]
            dgp_ref[:, sl] = (dy * rn * (sg * (1.0 + gh * (1.0 - sg)))).astype(BF16)
            drn = dy * (gh * sg)
            dret_ref[:, sl] = (rr * (drn - rn * jnp.mean(drn * rn, axis=-1, keepdims=True))).astype(BF16)

    tile = lambda w: pl.BlockSpec((ts, w), lambda i: (nt - 1 - i, 0))
    return _call(
        body, name="mix_bwd", grid=(nt,),
        in_specs=[tile(D_MODEL), tile(POOL_W), tile(RET_W), tile(RET_W), _whole(), _whole(), _whole()],
        out_specs=[tile(RET_W), tile(2 * RET_W),
                   pl.BlockSpec((len(POOL_WINDOWS), HEAD_DIM, HEAD_DIM), lambda i: (0, 0, 0)),
                   pl.BlockSpec((1, POOL_W), lambda i: (0, 0))],
        out_shape=[jax.ShapeDtypeStruct((s, RET_W), BF16), jax.ShapeDtypeStruct((s, 2 * RET_W), BF16),
                   jax.ShapeDtypeStruct((len(POOL_WINDOWS), HEAD_DIM, HEAD_DIM), F32),
                   jax.ShapeDtypeStruct((1, POOL_W), F32)],
        scratch_shapes=[pltpu.VMEM((ts + POOL_HALO, POOL_W), F32)],
        sem=("arbitrary",), operands=(dz1, pooled, ret, g, wout, wpool, pscale), riders=riders,
        after=after,
    )


def _retention_bwd(q, k, v, dret, dgp, states, mask, qd, kd, cosf, sinf, riders=(), after=()):
    s = q.shape[0]
    ns = s // SUPER
    cdec = [gm ** float(SUPER) for gm in _gammas()]

    def body(q_ref, k_ref, v_ref, do_ref, dgp_ref, st_ref, mask_ref, qd_ref, kd_ref, cos_ref, sin_ref,
             dproj_ref, dstate_scr):
        i = pl.program_id(0)

        @pl.when(i == 0)
        def _():
            dstate_scr[...] = jnp.zeros_like(dstate_scr)

        cosf_t = cos_ref[...]
        sinf_t = sin_ref[...]
        for h in range(HEADS):
            sl = slice(h * HEAD_DIM, (h + 1) * HEAD_DIM)
            qh, kh, vh, doh = q_ref[:, sl], k_ref[:, sl], v_ref[:, sl], do_ref[:, sl]
            m = mask_ref[h]
            scb = (_dot_nt(qh, kh) * m).astype(BF16)
            dscb = (_dot_nt(doh, vh) * m).astype(BF16)
            stb = st_ref[0, h]
            dst = dstate_scr[h]
            dstb = dst.astype(BF16)
            qdb = (qh.astype(F32) * qd_ref[:, sl]).astype(BF16)
            kdb = (kh.astype(F32) * kd_ref[:, sl]).astype(BF16)
            dq = _dot(dscb, kh) + _dot_nt(doh, stb) * qd_ref[:, sl]
            dk = _dot_tn(dscb, qh) + _dot_nt(vh, dstb) * kd_ref[:, sl]
            dv = _dot_tn(scb, doh) + _dot(kdb, dstb)
            dstate_scr[h] = dst * cdec[h] + _dot_tn(qdb, doh)
            lo = h * HEAD_DIM
            dproj_ref[:, lo:lo + HEAD_DIM] = _rope_t(dq, cosf_t, sinf_t).astype(BF16)
            dproj_ref[:, RET_W + lo:RET_W + lo + HEAD_DIM] = _rope_t(dk * K_SCALE, cosf_t, sinf_t).astype(BF16)
            dproj_ref[:, 2 * RET_W + lo:2 * RET_W + lo + HEAD_DIM] = dv.astype(BF16)
        dproj_ref[:, 3 * RET_W:IN_W] = dgp_ref[...]

    tile = lambda w: pl.BlockSpec((SUPER, w), lambda i: (ns - 1 - i, 0))
    return _call(
        body, name="retention_bwd", grid=(ns,),
        in_specs=[tile(RET_W), tile(RET_W), tile(RET_W), tile(RET_W), tile(2 * RET_W),
                  pl.BlockSpec((1, HEADS, HEAD_DIM, HEAD_DIM), lambda i: (ns - 1 - i, 0, 0, 0)),
                  _whole(), _whole(), _whole(), tile(HEAD_DIM), tile(HEAD_DIM)],
        out_specs=[tile(IN_W)],
        out_shape=[jax.ShapeDtypeStruct((s, IN_W), BF16)],
        scratch_shapes=[pltpu.VMEM((HEADS, HEAD_DIM, HEAD_DIM), F32)],
        sem=("arbitrary",), operands=(q, k, v, dret, dgp, states, mask, qd, kd, cosf, sinf), riders=riders,
        after=after,
    )


def _dx(dz1, dproj, win4, ts, riders=(), after=()):
    s = dz1.shape[0]

    def body(dz1_ref, dp_ref, w_ref, dx_ref):
        acc = ALPHA * dz1_ref[...]
        for j in range(N_SHARD):
            acc = acc + _dot_nt(dp_ref[:, j * IN_SH:(j + 1) * IN_SH], w_ref[j])
        dx_ref[...] = acc

    tile = lambda w: pl.BlockSpec((ts, w), lambda i: (i, 0))
    return _call(
        body, name="dx", grid=(s // ts,),
        in_specs=[tile(D_MODEL), tile(IN_W), _whole()],
        out_specs=[tile(D_MODEL)],
        out_shape=[jax.ShapeDtypeStruct((s, D_MODEL), F32)],
        sem=("arbitrary",), operands=(dz1, dproj, win4), riders=riders, after=after,
    )


def _wgrad(a, b, tm, tn, name, stacked, m_outer, riders=(), after=()):
    s, m = a.shape
    n = b.shape[1]

    def body(a_ref, b_ref, o32_ref, o16_ref):
        res = _dot_tn(a_ref[...], b_ref[...])
        o32_ref[...] = res.reshape(o32_ref.shape)
        o16_ref[...] = res.astype(BF16).reshape(o16_ref.shape)

    if m_outer:
        grid, blocks = (m // tm, n // tn), (lambda g0, g1: (g0, g1))
    else:
        grid, blocks = (n // tn, m // tm), (lambda g0, g1: (g1, g0))
    if stacked:
        shape = (n // tn, m, tn)
        ospec = pl.BlockSpec((1, tm, tn), lambda g0, g1: (blocks(g0, g1)[1], blocks(g0, g1)[0], 0))
    else:
        shape = (m, n)
        ospec = pl.BlockSpec((tm, tn), lambda g0, g1: blocks(g0, g1))
    return _call(
        body, name=name, grid=grid,
        in_specs=[pl.BlockSpec((s, tm), lambda g0, g1: (0, blocks(g0, g1)[0])),
                  pl.BlockSpec((s, tn), lambda g0, g1: (0, blocks(g0, g1)[1]))],
        out_specs=[ospec, ospec],
        out_shape=[jax.ShapeDtypeStruct(shape, F32), jax.ShapeDtypeStruct(shape, BF16)],
        sem=("arbitrary", "arbitrary"), operands=(a, b), riders=riders, after=after,
    )


class _NoComm:
    def __init__(self, win4, wout, wup4, wdown):
        self.weights = dict(w_in=win4, w_out=wout, w_up=wup4, w_down=wdown)
        self.grads = {}

    def weight(self, name):
        return self.weights[name]

    def riders(self, call):
        return ()

    def after(self, call):
        return ()

    def landed(self, call, results, outs):
        pass

    def small_gradients(self, loss, small):
        pass

    def gradient(self, name, g32, g16):
        self.grads[name] = (g32, g16)


def _local_step(x, target, cw, cb, wpool, pscale, g1, b1, g2, b2, comm):
    s = x.shape[0]
    ts_a = min(512, s)
    ts_f = min(256, s)
    mask, qd, kd = _decay_tables()
    cosf, sinf = _rope_tables(s)
    wpool_b = wpool.astype(BF16)

    def run(call, fn, *args):
        outs, res = fn(*args, riders=comm.riders(call), after=comm.after(call))
        comm.landed(call, res, outs)
        return outs

    xb, q, k, v, g, pooled, cat = run("proj_pool", _proj_pool, x, comm.weight("w_in"), cosf, sinf, wpool_b,
                                      pscale, ts_a)
    ret, cat, states = run("retention_fwd", _retention_fwd, q, k, v, g, cat, mask, qd, kd)
    wout = comm.weight("w_out")
    xhat1, rstd1, h1b = run("outproj_ln1", _outproj_ln1, x, cat, wout, g1, b1, ts_a)
    wup4, wdown = comm.weight("w_up"), comm.weight("w_down")
    ub, act, sd, dz2, dz2b, loss, dg2, db2 = _ffn_fwd_loss(xhat1, h1b, target, wup4, wdown, cw, cb, g1, b1, g2, b2,
                                                           ts_f)

    dub, dz1, dz1b, dg1, db1, dcw, dcb = _ffn_bwd(dz2, dz2b, ub, sd, xhat1, rstd1, wup4, wdown, cw, g1, ts_f)
    half = D_MODEL // 2
    comm.gradient("w_up", *run("wgrad_up", _wgrad, h1b, dub, half, UP_SH, "wgrad_up", True, False))
    comm.gradient("w_out", *run("wgrad_out", _wgrad, cat, dz1b, D_MODEL, half, "wgrad_out", False, True))
    comm.gradient("w_down", *run("wgrad_down", _wgrad, act, dz2b, D_FF // 2, half, "wgrad_down", False, True))
    dret, dgp, dwp, dps = run("mix_bwd", _mix_bwd, dz1b, pooled, ret, g, wout, wpool_b, pscale, ts_a)
    small = dict(w_pool=dwp, pool_scale=dps, ln1_g=dg1, ln1_b=db1, conv_w=dcw, conv_b=dcb,
                 ln2_g=dg2, ln2_b=db2)
    comm.small_gradients(loss, small)
    dproj, = run("retention_bwd", _retention_bwd, q, k, v, dret, dgp, states, mask, qd, kd, cosf, sinf)
    comm.gradient("w_in", *run("wgrad_in", _wgrad, xb, dproj, D_MODEL, IN_SH, "wgrad_in", True, True))
    (grad_x,), _ = _dx(dz1, dproj, comm.weight("w_in"), ts_a, after=comm.after("dx"))
    return loss, grad_x, small


CAST_ROWS = 64
SHARD_SHAPES = ((D_MODEL, IN_SH), (OUT_SH, D_MODEL), (D_MODEL, UP_SH), (DOWN_SH, D_MODEL))
N_BIG = len(SHARD_SHAPES)
CW_PAD = (8, 768)


def _mesh_pos():
    return lax.axis_index("x"), lax.axis_index("y"), lax.axis_index("c")


def _other_chips(x, y):
    return [(1 - x, y), (x, 1 - y), (1 - x, 1 - y)]


def _half_rows(w, which):
    hr = SHARD_SHAPES[w][0] // 2
    return pl.ds(pl.multiple_of(which * hr, 16), hr)


def _gather_weights(shards, cw8, full):
    def body(*refs):
        in_refs = refs[:N_BIG]
        cw_ref = refs[N_BIG]
        out_refs = refs[N_BIG + 1:2 * N_BIG + 1]
        cwo_ref = refs[2 * N_BIG + 1]
        stage = refs[2 * N_BIG + 2:3 * N_BIG + 2]
        raw = refs[3 * N_BIG + 2:4 * N_BIG + 2 - len(full)]
        send_sems, recv_sems, fsend_sems, frecv_sems, cw_send, cw_recv, local_sems, load_sems = \
            refs[4 * N_BIG + 2 - len(full):]
        x, y, c = _mesh_pos()
        j0 = 2 * x + y
        chips = _other_chips(x, y)

        fetched = [w for w in range(N_BIG) if w not in full]
        f32 = {w: in_refs[w] for w in full}
        loads = []
        for n, w in enumerate(fetched):
            f32[w] = raw[n]
            loads.append(pltpu.make_async_copy(in_refs[w], raw[n], load_sems.at[n]))
            loads[-1].start()

        def cast_to_stage(w):
            def cast(i, carry):
                rows = pl.ds(pl.multiple_of(i * CAST_ROWS, CAST_ROWS), CAST_ROWS)
                stage[w][rows, :] = f32[w][rows, :].astype(BF16)
                return carry
            lax.fori_loop(0, SHARD_SHAPES[w][0] // CAST_ROWS, cast, 0)

        for w in full:
            cast_to_stage(w)

        jx, jy, jd = 2 * (1 - x) + y, 2 * x + (1 - y), 2 * (1 - x) + (1 - y)
        neighbours = [((1 - x, y, c), jx), ((x, 1 - y, c), jy)]
        passed = jnp.where(c == 0, jx, jy)
        pass_to = (jnp.where(c == 0, x, 1 - x), jnp.where(c == 0, 1 - y, y), c)

        def nbr(w, k, block):
            return pltpu.make_async_remote_copy(
                src_ref=stage[w].at[_half_rows(w, c), :], dst_ref=out_refs[w].at[block, _half_rows(w, c), :],
                send_sem=send_sems.at[w, k], recv_sem=recv_sems.at[w, k],
                device_id=neighbours[k][0], device_id_type=MESH)

        def relay(w, block):
            return pltpu.make_async_remote_copy(
                src_ref=out_refs[w].at[passed, _half_rows(w, c), :],
                dst_ref=out_refs[w].at[block, _half_rows(w, c), :],
                send_sem=send_sems.at[w, 2], recv_sem=recv_sems.at[w, 2],
                device_id=pass_to, device_id_type=MESH)

        def d2d(w, k, block, half):
            return pltpu.make_async_remote_copy(
                src_ref=out_refs[w].at[block, _half_rows(w, half), :],
                dst_ref=out_refs[w].at[block, _half_rows(w, half), :],
                send_sem=fsend_sems.at[w, k], recv_sem=frecv_sems.at[w, k],
                device_id=(x, y, 1 - c), device_id_type=MESH)

        def conv(k, block):
            chip = chips[k]
            return pltpu.make_async_remote_copy(
                src_ref=cw_ref, dst_ref=cwo_ref.at[block], send_sem=cw_send.at[k], recv_sem=cw_recv.at[k],
                device_id=(chip[0], chip[1], c), device_id_type=MESH)

        sent = [nbr(w, k, j0) for w in full for k in range(2)] + [conv(k, j0) for k in range(3)]
        for cp in sent:
            cp.start()
        for n, w in enumerate(fetched):
            loads[n].wait()
            cast_to_stage(w)
        local = [pltpu.make_async_copy(stage[w], out_refs[w].at[j0], local_sems.at[w]) for w in range(N_BIG)]
        local.append(pltpu.make_async_copy(cw_ref, cwo_ref.at[j0], local_sems.at[N_BIG]))
        for cp in local:
            cp.start()
        for w in full:
            for k, (_, block) in enumerate(neighbours):
                nbr(w, k, block).wait_recv()
            later = [relay(w, passed)] + [d2d(w, k, block, c) for k, (_, block) in enumerate(neighbours)]
            for cp in later:
                cp.start()
            sent += later
        for w in full:
            relay(w, jd).wait_recv()
            fw = d2d(w, 2, jd, c)
            fw.start()
            sent.append(fw)
        for w in full:
            for k, block in enumerate([jx, jy, jd]):
                d2d(w, k, block, 1 - c).wait_recv()
        for k, chip in enumerate(chips):
            conv(k, 2 * chip[0] + chip[1]).wait_recv()
        for cp in sent:
            cp.wait_send()
        for cp in local:
            cp.wait()

    out_shape = [jax.ShapeDtypeStruct((N_SHARD,) + shp, BF16) for shp in SHARD_SHAPES]
    out_shape.append(jax.ShapeDtypeStruct((N_SHARD,) + CW_PAD, F32))
    return pl.pallas_call(
        body, name="gather_weights",
        in_specs=[_whole() if w in full else HBM_SPEC for w in range(N_BIG)] + [_whole()],
        out_specs=[HBM_SPEC] * (N_BIG + 1),
        out_shape=out_shape,
        scratch_shapes=[pltpu.VMEM(shp, BF16) for shp in SHARD_SHAPES]
        + [pltpu.VMEM(shp, F32) for w, shp in enumerate(SHARD_SHAPES) if w not in full] + [
            pltpu.SemaphoreType.DMA((N_BIG, 3)), pltpu.SemaphoreType.DMA((N_BIG, 3)),
            pltpu.SemaphoreType.DMA((N_BIG, 3)), pltpu.SemaphoreType.DMA((N_BIG, 3)),
            pltpu.SemaphoreType.DMA((3,)), pltpu.SemaphoreType.DMA((3,)),
            pltpu.SemaphoreType.DMA((N_BIG + 1,)), pltpu.SemaphoreType.DMA((N_BIG - len(full),))],
        compiler_params=pltpu.CompilerParams(vmem_limit_bytes=VMEM_LIMIT),
    )(*shards, cw8)


def _gather_rider(arrays, ops):
    ws = sorted(arrays)

    def make(inplace, srcs, lands, send_sems, recv_sems):
        del srcs, lands
        x, y, c = _mesh_pos()
        j0, jx, jy, jd = 2 * x + y, 2 * (1 - x) + y, 2 * x + (1 - y), 2 * (1 - x) + (1 - y)
        x_nbr, y_nbr, sibling = (1 - x, y, c), (x, 1 - y, c), (x, y, 1 - c)
        starts, waits = [], []
        for n, (kind, w, (r0, nr)) in enumerate(ops):
            ref = inplace[ws.index(w)]
            hr = SHARD_SHAPES[w][0] // 2
            rows = lambda core: pl.ds(pl.multiple_of(core * hr + r0, 16), nr)
            mine, theirs = rows(c), rows(1 - c)
            if kind == "ici":
                moves = [(ref.at[j0, mine, :], x_nbr, ref.at[jx, mine, :]),
                         (ref.at[j0, mine, :], y_nbr, ref.at[jy, mine, :]),
                         (ref.at[j0, mine, :], (1 - x, 1 - y, c), ref.at[jd, mine, :])]
            elif kind == "nbr":
                moves = [(ref.at[j0, mine, :], x_nbr, ref.at[jx, mine, :]),
                         (ref.at[j0, mine, :], y_nbr, ref.at[jy, mine, :])]
            elif kind == "relay":
                passed = jnp.where(c == 0, jx, jy)
                to = (jnp.where(c == 0, x, 1 - x), jnp.where(c == 0, 1 - y, y), c)
                moves = [(ref.at[passed, mine, :], to, ref.at[jd, mine, :])]
            else:
                blocks = dict(d2d=[jx, jy, jd], d2d_nbr=[jx, jy], d2d_diag=[jd])[kind]
                moves = [(ref.at[b, mine, :], sibling, ref.at[b, theirs, :]) for b in blocks]
            for k, (src, to, landing) in enumerate(moves):
                sems = dict(send_sem=send_sems.at[3 * n + k], recv_sem=recv_sems.at[3 * n + k],
                            device_id=to, device_id_type=MESH)
                send = pltpu.make_async_remote_copy(src_ref=src, dst_ref=src, **sems)
                arrival = pltpu.make_async_remote_copy(src_ref=src, dst_ref=landing, **sems)
                starts.append(send)
                waits += [arrival.wait_recv, send.wait_send]
        return starts, waits

    return _Rider([arrays[w] for w in ws], [], [], 3 * len(ops), make)


def _whole_half(w):
    return (0, SHARD_SHAPES[w][0] // 2)


def _pair_rider(ws, g16s):
    def make(inplace, srcs, lands, send_sems, recv_sems):
        del inplace
        x, y, c = _mesh_pos()
        copies = [pltpu.make_async_remote_copy(
            src_ref=srcs[i].at[:, _half_rows(w, 1 - c), :], dst_ref=lands[i],
            send_sem=send_sems.at[i], recv_sem=recv_sems.at[i], device_id=(x, y, 1 - c), device_id_type=MESH)
            for i, w in enumerate(ws)]
        return copies, [cp.wait for cp in copies]

    lands = [jax.ShapeDtypeStruct((N_SHARD, SHARD_SHAPES[w][0] // 2, SHARD_SHAPES[w][1]), BF16) for w in ws]
    return _Rider([], g16s, lands, len(ws), make)


def _chip_rider(ws, p16s):
    def make(inplace, srcs, lands, send_sems, recv_sems):
        del inplace
        x, y, c = _mesh_pos()
        copies = []
        for i in range(len(ws)):
            for k, chip in enumerate(_other_chips(x, y)):
                copies.append(pltpu.make_async_remote_copy(
                    src_ref=srcs[i].at[2 * chip[0] + chip[1]], dst_ref=lands[i].at[k],
                    send_sem=send_sems.at[3 * i + k], recv_sem=recv_sems.at[3 * i + k],
                    device_id=(chip[0], chip[1], c), device_id_type=MESH))
        return copies, [cp.wait for cp in copies]

    lands = [jax.ShapeDtypeStruct((3, SHARD_SHAPES[w][0] // 2, SHARD_SHAPES[w][1]), BF16) for w in ws]
    return _Rider([], p16s, lands, 3 * len(ws), make)


def _final_rider(halves):
    def make(inplace, srcs, lands, send_sems, recv_sems):
        del inplace
        x, y, c = _mesh_pos()
        copies = [pltpu.make_async_remote_copy(
            src_ref=srcs[i], dst_ref=lands[i], send_sem=send_sems.at[i], recv_sem=recv_sems.at[i],
            device_id=(x, y, 1 - c), device_id_type=MESH) for i in range(len(halves))]
        return copies, [cp.wait for cp in copies]

    return _Rider([], halves, [jax.ShapeDtypeStruct(h.shape, h.dtype) for h in halves], len(halves), make)


def _comm_only(name, riders):
    _, res = _call(lambda: None, name=name, grid=(), in_specs=[], out_specs=[], out_shape=[], operands=(),
                   riders=riders)
    return res


class _SemList:
    def __init__(self, refs):
        self.at = list(refs)


def _merged_rider(riders):
    srcs = [a for r in riders for a in r.srcs]
    lands = [a for r in riders for a in r.lands]

    def make(inplace, src_refs, land_refs, send_sems, recv_sems):
        starts, waits = [], []
        s0 = l0 = c0 = 0
        for r in riders:
            part = r.make(inplace, src_refs[s0:s0 + len(r.srcs)], land_refs[l0:l0 + len(r.lands)],
                          _SemList(send_sems.at[c0:c0 + r.n_copies]), _SemList(recv_sems.at[c0:c0 + r.n_copies]))
            starts += part[0]
            waits += part[1]
            s0, l0, c0 = s0 + len(r.srcs), l0 + len(r.lands), c0 + r.n_copies
        return starts, waits

    return _Rider([], srcs, lands, sum(r.n_copies for r in riders), make)


def _split_start(name, rider):
    assert not rider.inplace
    ns, nl, n = len(rider.srcs), len(rider.lands), rider.n_copies

    def body(*refs):
        srcs, lands = refs[:ns], refs[ns:ns + nl]
        sems = refs[ns + nl:ns + nl + 2 * n]
        token = refs[-1]
        starts, _ = rider.make([], srcs, lands, _SemList(sems[:n]), _SemList(sems[n:]))
        for cp in starts:
            cp.start()
        token[...] = jnp.zeros_like(token)

    buffers = [pltpu.with_memory_space_constraint(a, pltpu.HBM) for a in rider.srcs]
    buffers += [pltpu.with_memory_space_constraint(lax.empty(s.shape, s.dtype), pltpu.HBM) for s in rider.lands]
    hbm = pl.BlockSpec(memory_space=pltpu.HBM)
    sem = pl.BlockSpec(memory_space=pltpu.SEMAPHORE)
    outs = pl.pallas_call(
        body, name=name,
        out_shape=tuple([pltpu.SemaphoreType.DMA(())] * (2 * n) + [pltpu.HBM(b.shape, b.dtype) for b in buffers]
                        + [jax.ShapeDtypeStruct((8, 128), F32)]),
        in_specs=[hbm] * (ns + nl),
        out_specs=tuple([sem] * (2 * n) + [hbm] * (ns + nl) + [_whole()]),
        input_output_aliases={i: 2 * n + i for i in range(ns + nl)},
        compiler_params=pltpu.CompilerParams(has_side_effects=pltpu.SideEffectType.DATAFLOW_SIDE_EFFECTING),
    )(*buffers)
    return (rider, outs[:2 * n], outs[2 * n:2 * n + ns + nl]), outs[-1]


def _split_parts(state, riders):
    merged, sems, buffers = state
    n, ns = merged.n_copies, len(merged.srcs)
    parts, s0, l0, c0 = [], 0, 0, 0
    for r in riders:
        parts.append((r, list(sems[c0:c0 + r.n_copies]) + list(sems[n + c0:n + c0 + r.n_copies]),
                      list(buffers[s0:s0 + len(r.srcs)]) + list(buffers[ns + l0:ns + l0 + len(r.lands)])))
        s0, l0, c0 = s0 + len(r.srcs), l0 + len(r.lands), c0 + r.n_copies
    return parts


def _split_wait(name, state, after):
    rider, sems, buffers = state
    ns, nl, n = len(rider.srcs), len(rider.lands), rider.n_copies

    def body(*refs):
        srcs, lands = refs[:ns], refs[ns:ns + nl]
        sem_refs = refs[ns + nl:ns + nl + 2 * n]
        _, waits = rider.make([], srcs, lands, _SemList(sem_refs[:n]), _SemList(sem_refs[n:]))
        for wait in waits:
            wait()

    hbm = pl.BlockSpec(memory_space=pltpu.HBM)
    sem = pl.BlockSpec(memory_space=pltpu.SEMAPHORE)
    outs = pl.pallas_call(
        body, name=name,
        out_shape=tuple(pltpu.HBM(b.shape, b.dtype) for b in buffers),
        in_specs=[hbm] * (ns + nl) + [sem] * (2 * n) + [HBM_SPEC],
        out_specs=tuple([hbm] * (ns + nl)),
        input_output_aliases={i: i for i in range(ns + nl)},
        compiler_params=pltpu.CompilerParams(has_side_effects=pltpu.SideEffectType.DATAFLOW_SIDE_EFFECTING),
    )(*buffers, *sems, after)
    return list(outs[:ns]), list(outs[ns:])


def _pair_sum(pos, ws, g32s, recvs):
    n = len(ws)

    def body(pos_ref, *refs):
        g_refs, r_refs = refs[:n], refs[n:2 * n]
        p32_refs, p16_refs = refs[2 * n:3 * n], refs[3 * n:]
        for i in range(n):
            tot = g_refs[i][...] + r_refs[i][...].astype(F32)
            p16_refs[i][...] = tot.astype(BF16)

            @pl.when(pl.program_id(0) == pos_ref[1])
            def _(i=i, tot=tot):
                p32_refs[i][...] = tot

    halves = [(SHARD_SHAPES[w][0] // 2, SHARD_SHAPES[w][1]) for w in ws]
    own = [pl.BlockSpec((None, None) + h, lambda j, pos_ref: (j, pos_ref[0], 0, 0)) for h in halves]
    blk = [pl.BlockSpec((None,) + h, lambda j, pos_ref: (j, 0, 0)) for h in halves]
    mine = [pl.BlockSpec(h, lambda j, pos_ref: (0, 0)) for h in halves]
    g4 = [g.reshape((N_SHARD, 2) + h) for g, h in zip(g32s, halves)]
    outs = pl.pallas_call(
        body, name="pair_sum_" + "_".join(str(w) for w in ws),
        grid_spec=pltpu.PrefetchScalarGridSpec(
            num_scalar_prefetch=1, grid=(N_SHARD,), in_specs=own + blk, out_specs=mine + blk),
        out_shape=[jax.ShapeDtypeStruct(h, F32) for h in halves]
        + [jax.ShapeDtypeStruct((N_SHARD,) + h, BF16) for h in halves],
        compiler_params=_params(("arbitrary",)),
    )(pos, *g4, *recvs)
    return outs[:n], outs[n:]


def _chip_sum(pos, p32s, recvs):
    parts = 2

    def body(pos_ref, *refs):
        del pos_ref
        p_refs, r_refs, f_refs = refs[:N_BIG], refs[N_BIG:2 * N_BIG], refs[2 * N_BIG:]
        for w in range(N_BIG):
            f_refs[w][...] = ((p_refs[w][...] + r_refs[w][0].astype(F32)) + r_refs[w][1].astype(F32)) \
                + r_refs[w][2].astype(F32)

    quarters = [(r // 2 // parts, cc) for r, cc in SHARD_SHAPES]
    own = [pl.BlockSpec(qt, lambda i, pos_ref: (i, 0)) for qt in quarters]
    rcv = [pl.BlockSpec((3,) + qt, lambda i, pos_ref: (0, i, 0)) for qt in quarters]
    out = [pl.BlockSpec(qt, lambda i, pos_ref: (i, 0)) for qt in quarters]
    return pl.pallas_call(
        body, name="chip_sum",
        grid_spec=pltpu.PrefetchScalarGridSpec(
            num_scalar_prefetch=1, grid=(parts,), in_specs=own + rcv, out_specs=out),
        out_shape=[jax.ShapeDtypeStruct((r // 2, cc), F32) for r, cc in SHARD_SHAPES],
        compiler_params=_params(("arbitrary",)),
    )(pos, *p32s, *recvs)


def _adamw(w, g, m, v):
    m_new = ADAM_B1 * m + (1.0 - ADAM_B1) * g
    v_new = ADAM_B2 * v + (1.0 - ADAM_B2) * (g * g)
    m_hat = m_new / (1.0 - ADAM_B1 ** ADAM_STEP)
    v_hat = v_new / (1.0 - ADAM_B2 ** ADAM_STEP)
    delta = -ADAM_LR * (m_hat / (jnp.sqrt(v_hat) + ADAM_EPS) + ADAM_WD * w)
    return delta, m_new, v_new


def _adam_half(name, which, grads, ws, ms, vs, into=None):
    nb = 4

    def body(which_ref, *refs):
        del which_ref
        groups = [refs[i * N_BIG:(i + 1) * N_BIG] for i in range(4)]
        g_refs, w_refs, m_refs, v_refs = groups
        go_refs, do_refs, mo_refs, vo_refs = [refs[len(refs) - (4 - i) * N_BIG:len(refs) - (3 - i) * N_BIG]
                                              for i in range(4)]
        for w in range(N_BIG):
            g = g_refs[w][...]
            delta, m_new, v_new = _adamw(w_refs[w][...], g, m_refs[w][...], v_refs[w][...])
            go_refs[w][...] = g
            do_refs[w][...] = delta
            mo_refs[w][...] = m_new
            vo_refs[w][...] = v_new

    blocks = [(r // 2 // nb, cc) for r, cc in SHARD_SHAPES]
    half = [pl.BlockSpec(b, lambda i, which_ref: (i, 0)) for b in blocks]
    full = [pl.BlockSpec((None,) + b, lambda i, which_ref: (0, which_ref[0] * nb + i, 0)) for b in blocks]
    shapes = [jax.ShapeDtypeStruct((1,) + shp, F32) for shp in SHARD_SHAPES]
    carried = [] if into is None else [a for kind in into for a in kind]
    first = 1 + 4 * N_BIG
    outs = pl.pallas_call(
        body, name=name,
        grid_spec=pltpu.PrefetchScalarGridSpec(
            num_scalar_prefetch=1, grid=(nb,), in_specs=half + full * 3 + [HBM_SPEC] * len(carried),
            out_specs=full * 4),
        out_shape=shapes * 4,
        input_output_aliases={first + i: i for i in range(len(carried))},
        compiler_params=_params(("arbitrary",)),
    )(which, *grads, *ws, *ms, *vs, *carried)
    return [outs[i * N_BIG:(i + 1) * N_BIG] for i in range(4)]


SMALL_ROWS = 8
ROW_CONV_B, ROW_POOL_SCALE, ROW_LN1_G, ROW_LN1_B, ROW_LN2_G, ROW_LN2_B, ROW_LOSS = range(7)
SMALL_VECS = ((ROW_CONV_B, D_FF), (ROW_POOL_SCALE, POOL_W), (ROW_LN1_G, D_MODEL), (ROW_LN1_B, D_MODEL),
              (ROW_LN2_G, D_MODEL), (ROW_LN2_B, D_MODEL))


def _small_pack(loss, vec_grads):
    def body(*refs):
        loss_ref, gvec, out_ref = refs[0], refs[1:-1], refs[-1]
        out_ref[...] = jnp.zeros_like(out_ref)
        for (row, n), ref in zip(SMALL_VECS, gvec):
            out_ref[row:row + 1, 0:n] = ref[...]
        out_ref[ROW_LOSS:ROW_LOSS + 1, 0:HEAD_DIM] = jnp.broadcast_to(loss_ref[...], (1, HEAD_DIM))

    return pl.pallas_call(
        body, name="small_pack", in_specs=[_whole()] * (1 + len(vec_grads)), out_specs=_whole(),
        out_shape=jax.ShapeDtypeStruct((SMALL_ROWS, D_FF), F32),
    )(loss, *vec_grads)


def _small_pair_sum(own, sibling):
    n = len(own)

    def body(*refs):
        x, y, _ = _mesh_pos()
        for i in range(n):
            refs[2 * n + i][2 * x + y] = refs[i][...] + refs[n + i][...]

    return pl.pallas_call(
        body, name="small_pair_sum", in_specs=[_whole()] * (2 * n), out_specs=[_whole()] * n,
        out_shape=[jax.ShapeDtypeStruct((N_SHARD,) + a.shape, F32) for a in own],
        compiler_params=pltpu.CompilerParams(vmem_limit_bytes=VMEM_LIMIT),
    )(*own, *sibling)


def _small_chip_rider(gathered):
    n = len(gathered)

    def make(inplace, srcs, lands, send_sems, recv_sems):
        del inplace, lands
        x, y, c = _mesh_pos()
        j0 = 2 * x + y
        starts, waits = [], []
        for i in range(n):
            for k, chip in enumerate(_other_chips(x, y)):
                sems = dict(send_sem=send_sems.at[3 * i + k], recv_sem=recv_sems.at[3 * i + k],
                            device_id=(chip[0], chip[1], c), device_id_type=MESH)
                send = pltpu.make_async_remote_copy(src_ref=srcs[i].at[j0], dst_ref=srcs[i].at[j0], **sems)
                arrival = pltpu.make_async_remote_copy(
                    src_ref=srcs[i].at[j0], dst_ref=srcs[i].at[2 * chip[0] + chip[1]], **sems)
                starts.append(send)
                waits += [arrival.wait_recv, send.wait_send]
        return starts, waits

    return _Rider([], gathered, [], 3 * n, make)


def _small_adam(all_a, all_b, all_c, wp, cwp, vec_ws, m_wp, m_cwp, vec_ms, v_wp, v_cwp, vec_vs):
    nv = len(SMALL_VECS)
    np_ = 2 + nv

    def body(*refs):
        all_a_ref, all_b_ref, all_c_ref = refs[0:3]
        w_all, m_all, v_all = (refs[3 + i * np_:3 + (i + 1) * np_] for i in range(3))
        loss_out = refs[3 + 3 * np_]
        outs = refs[4 + 3 * np_:]
        x, y, _ = _mesh_pos()
        j0 = 2 * x + y
        tot_a = ((all_a_ref[0] + all_a_ref[1]) + all_a_ref[2]) + all_a_ref[3]
        tot_b = ((all_b_ref[0] + all_b_ref[1]) + all_b_ref[2]) + all_b_ref[3]
        tot_c = ((all_c_ref[0, j0] + all_c_ref[1, j0]) + all_c_ref[2, j0]) + all_c_ref[3, j0]
        loss_out[...] = tot_b[ROW_LOSS:ROW_LOSS + 1, 0:1]
        grads = [tot_a, tot_c] + [tot_b[row:row + 1, 0:n] for row, n in SMALL_VECS]
        for p in range(np_):
            delta, m_new, v_new = _adamw(w_all[p][...], grads[p], m_all[p][...], v_all[p][...])
            outs[p][...] = grads[p]
            outs[np_ + p][...] = delta
            outs[2 * np_ + p][...] = m_new
            outs[3 * np_ + p][...] = v_new

    pshapes = [wp.shape, CW_PAD] + [wv.shape for wv in vec_ws]
    out_shape = [jax.ShapeDtypeStruct((1, 1), F32)] + [jax.ShapeDtypeStruct(s, F32) for s in pshapes] * 4
    outs = pl.pallas_call(
        body, name="small_adam",
        in_specs=[_whole()] * (3 + 3 * np_), out_specs=[_whole()] * len(out_shape), out_shape=out_shape,
        compiler_params=pltpu.CompilerParams(vmem_limit_bytes=VMEM_LIMIT),
    )(all_a, all_b, all_c, wp, cwp, *vec_ws, m_wp, m_cwp, *vec_ms, v_wp, v_cwp, *vec_vs)
    return outs[0], [outs[1 + i * np_:1 + (i + 1) * np_] for i in range(4)]


def _pad_cw(a):
    pad = [(0, 0)] * (a.ndim - 2) + [(0, CW_PAD[0] - a.shape[-2]), (0, CW_PAD[1] - a.shape[-1])]
    return jnp.pad(a, pad)


def kernel(x, w_in, w_pool, pool_scale, w_out, ln1_g, ln1_b, w_up, conv_w, conv_b, w_down, ln2_g, ln2_b, loss_target, m_w_in, m_w_pool, m_pool_scale, m_w_out, m_ln1_g, m_ln1_b, m_w_up, m_conv_w, m_conv_b, m_w_down, m_ln2_g, m_ln2_b, v_w_in, v_w_pool, v_pool_scale, v_w_out, v_ln1_g, v_ln1_b, v_w_up, v_conv_w, v_conv_b, v_w_down, v_ln2_g, v_ln2_b):
    pos = jnp.stack([lax.axis_index("c"), 2 * lax.axis_index("x") + lax.axis_index("y")]).astype(jnp.int32)
    order = ("w_in", "w_out", "w_up", "w_down")
    w_in_i, w_out_i, w_up_i, w_down_i = range(N_BIG)
    vec_names = ("conv_b", "pool_scale", "ln1_g", "ln1_b", "ln2_g", "ln2_b")

    gathered = _gather_weights([w_in[0], w_out[0], w_up[0], w_down[0]], _pad_cw(conv_w[0]), (w_in_i,))
    cw_full = jnp.transpose(gathered[N_BIG][:, 0:3, 0:DOWN_SH], (1, 0, 2)).reshape(3, D_FF)
    up_a, up_b, up_c = (0, 128), (128, 192), (320, 192)
    assert up_c[0] + up_c[1] == SHARD_SHAPES[w_up_i][0] // 2

    class MeshComm:
        def __init__(self):
            self.w = {i: gathered[i] for i in range(N_BIG)}
            self.g32, self.g16, self.p32, self.p16, self.recv_b = {}, {}, {}, {}, {}
            self.up_complete = False
            self.tokens, self.chips = {}, []

        def weight(self, name):
            i = order.index(name)
            if name == "w_up" and not self.up_complete:
                (arrs, _), = _comm_only("gather_up_last", [_gather_rider(
                    {i: self.w[i]}, [("d2d_diag", i, up_b), ("d2d", i, up_c)])])
                self.w[i], self.up_complete = arrs[0], True
            full = self.w[i]
            return full.reshape(-1, full.shape[-1]) if name in ("w_out", "w_down") else full

        def _gather(self, ws, ops):
            return _gather_rider({w: self.w[w] for w in ws}, ops), ("w", ws)

        def _pair(self, ws):
            return _pair_rider(ws, [self.g16[w] for w in ws]), ("recv_a", ws)

        def _chip(self, ws):
            return _chip_rider(ws, [self.p16[w] for w in ws]), ("recv_b", ws)

        def plan(self, call):
            out_all, down_all = _whole_half(w_out_i), _whole_half(w_down_i)
            if call == "proj_pool":
                return [self._gather([w_out_i, w_up_i, w_down_i],
                                     [("ici", w_out_i, out_all), ("nbr", w_down_i, down_all),
                                      ("nbr", w_up_i, up_a)])]
            if call == "retention_fwd":
                return [self._gather([w_out_i, w_up_i, w_down_i],
                                     [("d2d", w_out_i, out_all),
                                      ("relay", w_down_i, down_all), ("d2d_nbr", w_down_i, down_all),
                                      ("relay", w_up_i, up_a), ("d2d_nbr", w_up_i, up_a), ("nbr", w_up_i, up_b)])]
            if call == "outproj_ln1":
                return [self._gather([w_up_i, w_down_i],
                                     [("d2d_diag", w_down_i, down_all), ("d2d_diag", w_up_i, up_a),
                                      ("relay", w_up_i, up_b), ("d2d_nbr", w_up_i, up_b), ("ici", w_up_i, up_c)])]
            return []

        def after(self, call):
            return tuple(self.tokens.pop(call, ()))

        def riders(self, call):
            self.pending = self.plan(call)
            return [r for r, _ in self.pending]

        def _start(self, name, rider, before):
            state, token = _split_start(name, rider)
            self.tokens.setdefault(before, []).append(token)
            return state

        def _finish_pair(self, name, state, ws, after):
            _, lands = _split_wait(name, state, after)
            self._finish_sum(ws, lands)

        def landed(self, call, results, outs):
            for (_, (slot, ws)), (inplace, lands) in zip(self.pending, results):
                for w, arr in zip(ws, inplace if len(inplace) else lands):
                    getattr(self, slot)[w] = arr
            if call == "wgrad_out":
                self._finish_pair("pair_exchange_up_wait", self.pair_up, [w_up_i], outs[1])
                self.chips.append(([w_up_i], self._start(
                    "chip_exchange_up_start", self._chip([w_up_i])[0], "wgrad_down")))
            if call == "mix_bwd":
                ws = [w_out_i, w_down_i]
                self._finish_pair("pair_exchange_out_down_wait", self.pair_out_down, ws, outs[0])
            if call == "retention_bwd":
                own, sibling = _split_wait("small_pair_wait", self.small_pair, outs[0])
                self.small_chip = self._start(
                    "small_chip_start", _small_chip_rider(_small_pair_sum(own, sibling)), "wgrad_in")

        def small_gradients(self, loss, small):
            dcw4 = _pad_cw(jnp.transpose(small["conv_w"].reshape(3, N_SHARD, DOWN_SH), (1, 0, 2)))
            own = [small["w_pool"], _small_pack(loss, [small[n] for n in vec_names]), dcw4]
            ws = [w_out_i, w_down_i]
            parts = [self._chip(ws)[0], _final_rider(own)]
            chip, self.small_pair = _split_parts(
                self._start("chip_out_down_small_pair_start", _merged_rider(parts), "retention_bwd"), parts)
            self.chips.append((ws, chip))

        def gradient(self, name, g32, g16):
            w = order.index(name)
            shape = (N_SHARD,) + SHARD_SHAPES[w]
            self.g32[w], self.g16[w] = g32.reshape(shape), g16.reshape(shape)
            if name == "w_up":
                self.pair_up = self._start("pair_exchange_up_start", self._pair([w])[0], "wgrad_out")
            if name == "w_down":
                self.pair_out_down = self._start("pair_exchange_out_down_start",
                                                 self._pair([w_out_i, w_down_i])[0], "mix_bwd")
            if name == "w_in":
                (_, lands), = _comm_only("pair_exchange_in", [self._pair([w])[0]])
                self._finish_sum([w], lands)
                self.chips.append(([w], self._start("chip_exchange_in_start", self._chip([w])[0], "dx")))

        def _finish_sum(self, ws, lands):
            p32s, p16s = _pair_sum(pos, ws, [self.g32[w] for w in ws], lands)
            for w, p32, p16 in zip(ws, p32s, p16s):
                self.p32[w], self.p16[w] = p32, p16

        def finish(self, after):
            for n, (ws, state) in enumerate(self.chips):
                _, lands = _split_wait("chip_exchange_wait_%d" % n, state, after)
                for w, arr in zip(ws, lands):
                    self.recv_b[w] = arr
            return _split_wait("small_chip_wait", self.small_chip, after)[0]

    comm = MeshComm()
    loss, grad_x, small = _local_step(x[0], loss_target[0], cw_full, conv_b, w_pool[0], pool_scale,
                                      ln1_g, ln1_b, ln2_g, ln2_b, comm)

    given = dict(w_pool=w_pool, pool_scale=pool_scale, ln1_g=ln1_g, ln1_b=ln1_b, conv_w=conv_w, conv_b=conv_b,
                 ln2_g=ln2_g, ln2_b=ln2_b)
    given_m = dict(w_pool=m_w_pool, pool_scale=m_pool_scale, ln1_g=m_ln1_g, ln1_b=m_ln1_b, conv_w=m_conv_w,
                   conv_b=m_conv_b, ln2_g=m_ln2_g, ln2_b=m_ln2_b)
    given_v = dict(w_pool=v_w_pool, pool_scale=v_pool_scale, ln1_g=v_ln1_g, ln1_b=v_ln1_b, conv_w=v_conv_w,
                   conv_b=v_conv_b, ln2_g=v_ln2_g, ln2_b=v_ln2_b)
    args = []
    for src in (given, given_m, given_v):
        args += [src["w_pool"][0], _pad_cw(src["conv_w"][0]), [src[n] for n in vec_names]]
    small_sums = comm.finish(grad_x)
    loss_tot, small_out = _small_adam(*small_sums, *args)
    every = range(N_BIG)
    mine = _chip_sum(pos, [comm.p32[w] for w in every], [comm.recv_b[w] for w in every])
    final_state, _ = _split_start("pair_exchange_f32_start", _final_rider(mine))
    mine = final_state[2][:N_BIG]
    big = ([w_in, w_out, w_up, w_down], [m_w_in, m_w_out, m_w_up, m_w_down], [v_w_in, v_w_out, v_w_up, v_w_down])
    own_half = _adam_half("adam_own_half", pos[0:1], mine, *big)
    _, theirs = _split_wait("pair_exchange_f32_wait", final_state, own_half[0][0])
    big_out = _adam_half("adam_other_half", 1 - pos[0:1], theirs, *big, into=own_half)

    names = ("w_in", "w_pool", "pool_scale", "w_out", "ln1_g", "ln1_b", "w_up", "conv_w", "conv_b", "w_down",
             "ln2_g", "ln2_b")
    small_names = ("w_pool", "conv_w") + vec_names
    result = [loss_tot.reshape(()), grad_x[None]]
    for kind in range(4):
        for n in names:
            if n in order:
                result.append(big_out[kind][order.index(n)])
            else:
                val = small_out[kind][small_names.index(n)]
                if n == "conv_w":
                    val = val[0:3, 0:DOWN_SH][None]
                elif n == "w_pool":
                    val = val[None]
                result.append(val)
    return tuple(result)
```

```python
import functools

import numpy as np
import jax
import jax.numpy as jnp
from jax import lax
from jax.experimental import pallas as pl
from jax.experimental.pallas import tpu as pltpu

F32 = jnp.float32
BF16 = jnp.bfloat16

D_MODEL = 1024
HEADS = 4
HEAD_DIM = 128
RET_W = HEADS * HEAD_DIM
POOL_WINDOWS = (2, 4, 8, 16)
POOL_W = 512
IN_W = 4 * RET_W + POOL_W
D_FF = 2816
N_SHARD = 4
IN_SH = IN_W // N_SHARD
UP_SH = 2 * D_FF // N_SHARD
DOWN_SH = D_FF // N_SHARD
OUT_SH = D_MODEL // N_SHARD
ROPE_BASE = 10000.0
LN_EPS = 1e-5
RMS_EPS = 1e-6
ALPHA = 2.0 ** 0.25
K_SCALE = HEAD_DIM ** -0.5
SUPER = 256
CHUNK = 64
POOL_HALO = 16
CONV_HALO = 8
FFN_STRIP = 128
LN_ROWS = 32

ADAM_LR = 0.001
ADAM_B1 = 0.9
ADAM_B2 = 0.999
ADAM_EPS = 1e-08
ADAM_WD = 0.01
ADAM_STEP = 10

MESH = pl.DeviceIdType.MESH
VMEM_LIMIT = 56 * 1024 * 1024


def _dot(a, b):
    return jnp.dot(a, b, preferred_element_type=F32)


def _dot_nt(a, b):
    return lax.dot_general(a, b, (((1,), (1,)), ((), ())), preferred_element_type=F32)


def _dot_tn(a, b):
    return lax.dot_general(a, b, (((0,), (0,)), ((), ())), preferred_element_type=F32)


def _sigmoid(x):
    return 1.0 / (1.0 + jnp.exp(-x))


def _params(sem):
    return pltpu.CompilerParams(dimension_semantics=sem, vmem_limit_bytes=VMEM_LIMIT)


def _whole():
    return pl.BlockSpec(memory_space=pltpu.VMEM)


HBM_SPEC = pl.BlockSpec(memory_space=pl.ANY)


class _Rider:
    def __init__(self, inplace, srcs, lands, n_copies, make):
        self.inplace, self.srcs, self.lands, self.n_copies, self.make = list(inplace), list(srcs), list(lands), n_copies, make


def _call(body, *, name, grid, in_specs, out_specs, out_shape, operands, scratch_shapes=(), sem=(),
          aliases=None, riders=(), after=()):
    n_in, n_out, n_scr = len(in_specs), len(out_shape), len(scratch_shapes)
    in_specs, out_specs, out_shape = list(in_specs), list(out_specs), list(out_shape)
    operands, scratch_shapes, aliases = list(operands), list(scratch_shapes), dict(aliases or {})
    in_specs += [_whole()] * len(after)
    operands += list(after)
    for r in riders:
        for a in r.inplace:
            aliases[len(in_specs)] = len(out_shape)
            in_specs.append(HBM_SPEC)
            operands.append(a)
            out_specs.append(HBM_SPEC)
            out_shape.append(jax.ShapeDtypeStruct(a.shape, a.dtype))
        for a in r.srcs:
            in_specs.append(HBM_SPEC)
            operands.append(a)
        for shp in r.lands:
            out_specs.append(HBM_SPEC)
            out_shape.append(shp)
        scratch_shapes += [pltpu.SemaphoreType.DMA((r.n_copies,)), pltpu.SemaphoreType.DMA((r.n_copies,))]

    def full(*refs):
        ins = refs[:n_in]
        at = n_in + len(after)
        r_srcs = []
        for r in riders:
            at += len(r.inplace)
            r_srcs.append(refs[at:at + len(r.srcs)])
            at += len(r.srcs)
        outs = refs[at:at + n_out]
        at += n_out
        r_outs = []
        for r in riders:
            r_outs.append((refs[at:at + len(r.inplace)], refs[at + len(r.inplace):at + len(r.inplace) + len(r.lands)]))
            at += len(r.inplace) + len(r.lands)
        scr = refs[at:at + n_scr]
        at += n_scr
        r_sems = [refs[at + 2 * i:at + 2 * i + 2] for i in range(len(riders))]

        def copies():
            return [r.make(r_outs[i][0], r_srcs[i], r_outs[i][1], r_sems[i][0], r_sems[i][1])
                    for i, r in enumerate(riders)]

        def start():
            for starts, _ in copies():
                for cp in starts:
                    cp.start()

        def finish():
            for _, waits in copies():
                for wait in waits:
                    wait()

        if riders and grid:
            first = functools.reduce(jnp.logical_and, [pl.program_id(d) == 0 for d in range(len(grid))])
            last = functools.reduce(jnp.logical_and, [pl.program_id(d) == grid[d] - 1 for d in range(len(grid))])
            pl.when(first)(start)
            body(*ins, *outs, *scr)
            pl.when(last)(finish)
        else:
            if riders:
                start()
            body(*ins, *outs, *scr)
            if riders:
                finish()

    params = _params(sem) if grid else pltpu.CompilerParams(vmem_limit_bytes=VMEM_LIMIT)
    res = pl.pallas_call(
        full, name=name, grid=grid, in_specs=in_specs, out_specs=out_specs, out_shape=out_shape,
        scratch_shapes=scratch_shapes, input_output_aliases=aliases, compiler_params=params,
    )(*operands)
    outs, at, rider_res = res[:n_out], n_out, []
    for r in riders:
        rider_res.append((res[at:at + len(r.inplace)], res[at + len(r.inplace):at + len(r.inplace) + len(r.lands)]))
        at += len(r.inplace) + len(r.lands)
    return list(outs), rider_res


def _gammas():
    return [1.0 - 2.0 ** (-5.0 - h) for h in range(HEADS)]


def _decay_tables():
    idx = np.arange(SUPER)
    dist = np.abs(idx[:, None] - idx[None, :]).astype(np.float64)
    visible = (idx[None, :] // CHUNK) <= (idx[:, None] // CHUNK)
    mask = np.stack([np.where(visible, g ** dist, 0.0) for g in _gammas()])
    qd = np.concatenate([np.repeat((g ** (idx + 1.0))[:, None], HEAD_DIM, 1) for g in _gammas()], 1)
    kd = np.concatenate([np.repeat((g ** (SUPER - 1.0 - idx))[:, None], HEAD_DIM, 1) for g in _gammas()], 1)
    return (jnp.asarray(mask, F32), jnp.asarray(qd, F32), jnp.asarray(kd, F32))


def _rope_tables(s):
    inv_freq = ROPE_BASE ** (-np.arange(0, HEAD_DIM, 2, dtype=np.float64) / HEAD_DIM)
    ang = np.arange(s, dtype=np.float64)[:, None] * inv_freq[None, :]
    cos, sin = np.cos(ang), np.sin(ang)
    return (jnp.asarray(np.concatenate([cos, cos], 1), F32),
            jnp.asarray(np.concatenate([-sin, sin], 1), F32))


def _rope(t, cosf, sinf):
    return t * cosf + pltpu.roll(t, HEAD_DIM // 2, 1) * sinf


def _rope_t(t, cosf, sinf):
    return t * cosf - pltpu.roll(t, HEAD_DIM // 2, 1) * sinf


def _layernorm_fwd(z):
    mu = jnp.mean(z, axis=-1, keepdims=True)
    zc = z - mu
    var = jnp.mean(zc * zc, axis=-1, keepdims=True)
    rstd = lax.rsqrt(var + LN_EPS)
    return zc * rstd, rstd


def _layernorm_bwd(dy, xhat, rstd, gain):
    dxh = dy * gain
    m1 = jnp.mean(dxh, axis=-1, keepdims=True)
    m2 = jnp.mean(dxh * xhat, axis=-1, keepdims=True)
    return rstd * (dxh - m1 - xhat * m2)


def _proj_pool(x, win4, cosf, sinf, wpool, pscale, ts, riders=(), after=()):
    s = x.shape[0]
    nt = s // ts

    def body(x_ref, w_ref, cos_ref, sin_ref, wp_ref, ps_ref,
             xb_ref, q_ref, k_ref, v_ref, g_ref, pooled_ref, cat_ref, proj_scr, pext_scr):
        i = pl.program_id(0)
        xb = x_ref[...].astype(BF16)
        xb_ref[...] = xb
        for j in range(N_SHARD):
            proj_scr[:, j * IN_SH:(j + 1) * IN_SH] = _dot(xb, w_ref[j])
        cosf_t = cos_ref[...]
        sinf_t = sin_ref[...]
        for h in range(HEADS):
            lo = h * HEAD_DIM
            q_ref[:, lo:lo + HEAD_DIM] = _rope(proj_scr[:, lo:lo + HEAD_DIM], cosf_t, sinf_t).astype(BF16)
            kk = _rope(proj_scr[:, RET_W + lo:RET_W + lo + HEAD_DIM], cosf_t, sinf_t) * K_SCALE
            k_ref[:, lo:lo + HEAD_DIM] = kk.astype(BF16)
        v_ref[...] = proj_scr[:, 2 * RET_W:3 * RET_W].astype(BF16)
        g_ref[...] = proj_scr[:, 3 * RET_W:4 * RET_W]

        @pl.when(i == 0)
        def _():
            pext_scr[0:POOL_HALO, :] = jnp.zeros((POOL_HALO, POOL_W), F32)

        pext_scr[POOL_HALO:POOL_HALO + ts, :] = proj_scr[:, 4 * RET_W:IN_W]
        pos = (i * ts + lax.broadcasted_iota(jnp.int32, (ts, 1), 0) + 1).astype(F32)
        for gi, w in enumerate(POOL_WINDOWS):
            lo = gi * HEAD_DIM
            ext = pext_scr[:, lo:lo + HEAD_DIM]
            acc = ext
            shift = 1
            while shift < w:
                acc = acc + pltpu.roll(acc, shift, 0)
                shift *= 2
            tok = ext[POOL_HALO:POOL_HALO + ts]
            pooled = acc[POOL_HALO:POOL_HALO + ts] / jnp.minimum(pos, float(w)) - tok
            pooled_b = pooled.astype(BF16)
            pooled_ref[:, lo:lo + HEAD_DIM] = pooled_b
            lin = _dot(pooled_b, wp_ref[gi])
            cat_ref[:, lo:lo + HEAD_DIM] = (lin * ps_ref[:, lo:lo + HEAD_DIM]).astype(BF16)
        pext_scr[0:POOL_HALO, :] = pext_scr[ts:ts + POOL_HALO, :]

    tile = lambda w: pl.BlockSpec((ts, w), lambda i: (i, 0))
    return _call(
        body, name="proj_pool", grid=(nt,),
        in_specs=[tile(D_MODEL), _whole(), tile(HEAD_DIM), tile(HEAD_DIM), _whole(), _whole()],
        out_specs=[tile(D_MODEL), tile(RET_W), tile(RET_W), tile(RET_W), tile(RET_W), tile(POOL_W),
                   pl.BlockSpec((ts, POOL_W), lambda i: (i, 1))],
        out_shape=[jax.ShapeDtypeStruct((s, D_MODEL), BF16), jax.ShapeDtypeStruct((s, RET_W), BF16),
                   jax.ShapeDtypeStruct((s, RET_W), BF16), jax.ShapeDtypeStruct((s, RET_W), BF16),
                   jax.ShapeDtypeStruct((s, RET_W), F32), jax.ShapeDtypeStruct((s, POOL_W), BF16),
                   jax.ShapeDtypeStruct((s, 2 * RET_W), BF16)],
        scratch_shapes=[pltpu.VMEM((ts, IN_W), F32), pltpu.VMEM((ts + POOL_HALO, POOL_W), F32)],
        sem=("arbitrary",), operands=(x, win4, cosf, sinf, wpool, pscale), riders=riders, after=after,
    )


def _retention_fwd(q, k, v, g, cat, mask, qd, kd, riders=(), after=()):
    s = q.shape[0]
    ns = s // SUPER
    cdec = [gm ** float(SUPER) for gm in _gammas()]

    def body(q_ref, k_ref, v_ref, g_ref, cat_in, mask_ref, qd_ref, kd_ref,
             ret_ref, cat_ref, st_ref, state_scr):
        del cat_in
        n = pl.program_id(0)

        @pl.when(n == 0)
        def _():
            state_scr[...] = jnp.zeros_like(state_scr)

        for h in range(HEADS):
            sl = slice(h * HEAD_DIM, (h + 1) * HEAD_DIM)
            qh, kh, vh = q_ref[:, sl], k_ref[:, sl], v_ref[:, sl]
            sc = _dot_nt(qh, kh) * mask_ref[h]
            st = state_scr[h]
            stb = st.astype(BF16)
            st_ref[0, h] = stb
            qdb = (qh.astype(F32) * qd_ref[:, sl]).astype(BF16)
            kdb = (kh.astype(F32) * kd_ref[:, sl]).astype(BF16)
            ret = _dot(sc.astype(BF16), vh) + _dot(qdb, stb)
            state_scr[h] = st * cdec[h] + _dot_tn(kdb, vh)
            ret_ref[:, sl] = ret
            r = lax.rsqrt(jnp.mean(ret * ret, axis=-1, keepdims=True) + RMS_EPS)
            gh = g_ref[:, sl]
            cat_ref[:, sl] = ((ret * r) * (gh * _sigmoid(gh))).astype(BF16)

    tile = pl.BlockSpec((SUPER, RET_W), lambda n: (n, 0))
    return _call(
        body, name="retention_fwd", grid=(ns,),
        in_specs=[tile, tile, tile, tile, HBM_SPEC, _whole(), _whole(), _whole()],
        out_specs=[tile, tile, pl.BlockSpec((1, HEADS, HEAD_DIM, HEAD_DIM), lambda n: (n, 0, 0, 0))],
        out_shape=[jax.ShapeDtypeStruct((s, RET_W), F32), jax.ShapeDtypeStruct((s, 2 * RET_W), BF16),
                   jax.ShapeDtypeStruct((ns, HEADS, HEAD_DIM, HEAD_DIM), BF16)],
        scratch_shapes=[pltpu.VMEM((HEADS, HEAD_DIM, HEAD_DIM), F32)],
        aliases={4: 1}, sem=("arbitrary",), operands=(q, k, v, g, cat, mask, qd, kd), riders=riders,
        after=after,
    )


def _outproj_ln1(x, cat, wout, g1, b1, ts, riders=(), after=()):
    s = x.shape[0]

    def body(x_ref, cat_ref, w_ref, g_ref, b_ref, xhat_ref, rstd_ref, h1b_ref):
        z = ALPHA * x_ref[...] + _dot(cat_ref[...], w_ref[...])
        xhat, rstd = _layernorm_fwd(z)
        xhat_ref[...] = xhat
        rstd_ref[...] = rstd
        h1b_ref[...] = (xhat * g_ref[...] + b_ref[...]).astype(BF16)

    tile = lambda w: pl.BlockSpec((ts, w), lambda i: (i, 0))
    return _call(
        body, name="outproj_ln1", grid=(s // ts,),
        in_specs=[tile(D_MODEL), tile(D_MODEL), _whole(), _whole(), _whole()],
        out_specs=[tile(D_MODEL), tile(1), tile(D_MODEL)],
        out_shape=[jax.ShapeDtypeStruct((s, D_MODEL), F32), jax.ShapeDtypeStruct((s, 1), F32),
                   jax.ShapeDtypeStruct((s, D_MODEL), BF16)],
        sem=("arbitrary",), operands=(x, cat, wout, g1, b1), riders=riders, after=after,
    )


def _ffn_fwd_loss(xhat1, h1b, target, wup4, wdown, cw, cb, g1, b1, g2, b2, ts):
    s = xhat1.shape[0]

    def body(xhat_ref, h1b_ref, tgt_ref, wup_ref, wdn_ref, cw_ref, cb_ref, g1_ref, b1_ref, g2_ref, b2_ref,
             ub_ref, act_ref, sd_ref, dz2_ref, dz2b_ref, loss_ref, dg2_ref, db2_ref, val_scr, gext_scr, ffn_scr):
        i = pl.program_id(0)

        @pl.when(i == 0)
        def _():
            gext_scr[0:CONV_HALO, :] = jnp.zeros((CONV_HALO, D_FF), F32)
            loss_ref[...] = jnp.zeros_like(loss_ref)
            dg2_ref[...] = jnp.zeros_like(dg2_ref)
            db2_ref[...] = jnp.zeros_like(db2_ref)

        for half in range(2):
            lo = half * UP_SH
            gext_scr[CONV_HALO:CONV_HALO + ts, lo:lo + UP_SH] = _dot(h1b_ref[...], wup_ref[2 + half])
            val_scr[:, lo:lo + UP_SH] = _dot(h1b_ref[...], wup_ref[half])
            for c0 in range(lo, lo + UP_SH, FFN_STRIP):
                cols = slice(c0, c0 + FFN_STRIP)
                ext = gext_scr[:, cols]
                gate = ext[CONV_HALO:]
                hc = cb_ref[:, cols] + ((pltpu.roll(ext, 2, 0)[CONV_HALO:] * cw_ref[0:1, cols]
                                         + pltpu.roll(ext, 1, 0)[CONV_HALO:] * cw_ref[1:2, cols])
                                        + gate * cw_ref[2:3, cols])
                val = val_scr[:, cols]
                sg = _sigmoid(hc)
                si = hc * sg
                act_ref[:, cols] = (si * val).astype(BF16)
                ub_ref[:, cols] = val.astype(BF16)
                ub_ref[:, D_FF + c0:D_FF + c0 + FFN_STRIP] = gate.astype(BF16)
                sd_ref[:, cols] = hc.astype(BF16)
            part = _dot(act_ref[:, lo:lo + UP_SH], wdn_ref[lo:lo + UP_SH, :])
            if half == 0:
                ffn_scr[...] = part
            else:
                ffn_scr[...] += part

        gext_scr[0:CONV_HALO, :] = gext_scr[ts:ts + CONV_HALO, :]

        loss_acc = jnp.zeros((1, 1), F32)
        dg2_acc = jnp.zeros((1, D_MODEL), F32)
        db2_acc = jnp.zeros((1, D_MODEL), F32)
        for r0 in range(0, ts, LN_ROWS):
            rows = slice(r0, r0 + LN_ROWS)
            h1 = xhat_ref[rows, :] * g1_ref[...] + b1_ref[...]
            xhat2, rstd2 = _layernorm_fwd(ALPHA * h1 + ffn_scr[rows, :])
            diff = (xhat2 * g2_ref[...] + b2_ref[...]) - tgt_ref[rows, :]
            row = jnp.mean(diff * diff, axis=-1, keepdims=True)
            loss_acc = loss_acc + 0.5 * jnp.sum(row, axis=0, keepdims=True)
            dy = diff * (1.0 / D_MODEL)
            dg2_acc = dg2_acc + jnp.sum(dy * xhat2, axis=0, keepdims=True)
            db2_acc = db2_acc + jnp.sum(dy, axis=0, keepdims=True)
            dz2 = _layernorm_bwd(dy, xhat2, rstd2, g2_ref[...])
            dz2_ref[rows, :] = dz2
            dz2b_ref[rows, :] = dz2.astype(BF16)
        loss_ref[...] += loss_acc
        dg2_ref[...] += dg2_acc
        db2_ref[...] += db2_acc

    tile = lambda w: pl.BlockSpec((ts, w), lambda i: (i, 0))
    acc = lambda w: pl.BlockSpec((1, w), lambda i: (0, 0))
    return pl.pallas_call(
        body, name="ffn_fwd_loss", grid=(s // ts,),
        in_specs=[tile(D_MODEL), tile(D_MODEL), tile(D_MODEL)] + [_whole()] * 8,
        out_specs=[tile(2 * D_FF), tile(D_FF), tile(D_FF), tile(D_MODEL), tile(D_MODEL),
                   acc(1), acc(D_MODEL), acc(D_MODEL)],
        out_shape=[jax.ShapeDtypeStruct((s, 2 * D_FF), BF16), jax.ShapeDtypeStruct((s, D_FF), BF16),
                   jax.ShapeDtypeStruct((s, D_FF), BF16), jax.ShapeDtypeStruct((s, D_MODEL), F32),
                   jax.ShapeDtypeStruct((s, D_MODEL), BF16),
                   jax.ShapeDtypeStruct((1, 1), F32), jax.ShapeDtypeStruct((1, D_MODEL), F32),
                   jax.ShapeDtypeStruct((1, D_MODEL), F32)],
        scratch_shapes=[pltpu.VMEM((ts, D_FF), F32), pltpu.VMEM((ts + CONV_HALO, D_FF), F32),
                        pltpu.VMEM((ts, D_MODEL), F32)],
        compiler_params=_params(("arbitrary",)),
    )(xhat1, h1b, target, wup4, wdown, cw, cb, g1, b1, g2, b2)


def _ffn_bwd(dz2, dz2b, ub, sd, xhat1, rstd1, wup4, wdown, cw, g1, ts):
    s = dz2.shape[0]
    nt = s // ts

    def body(dz2_ref, dz2b_ref, ub_ref, sd_ref, xhat_ref, rstd_ref, wup_ref, wdn_ref, cw_ref, g1_ref,
             dub_ref, dz1_ref, dz1b_ref, dg1_ref, db1_ref, dcw_ref, dcb_ref, dext_scr, da_scr):
        i = pl.program_id(0)

        @pl.when(i == 0)
        def _():
            dext_scr[ts:ts + CONV_HALO, :] = jnp.zeros((CONV_HALO, D_FF), F32)
            dg1_ref[...] = jnp.zeros_like(dg1_ref)
            db1_ref[...] = jnp.zeros_like(db1_ref)
            dcw_ref[...] = jnp.zeros_like(dcw_ref)
            dcb_ref[...] = jnp.zeros_like(dcb_ref)

        da_scr[...] = _dot_nt(dz2b_ref[...], wdn_ref[...])
        n_ext = ts + CONV_HALO
        for c0 in range(0, D_FF, FFN_STRIP):
            cols = slice(c0, c0 + FFN_STRIP)
            gcols = slice(D_FF + c0, D_FF + c0 + FFN_STRIP)
            val = ub_ref[:, cols].astype(F32)
            gate = ub_ref[:, gcols].astype(F32)
            da = da_scr[:, cols]
            hc = sd_ref[:, cols].astype(F32)
            sg = _sigmoid(hc)
            dhc = da * val * (sg * (1.0 + hc * (1.0 - sg)))
            dext_scr[0:ts, cols] = dhc
            dext = dext_scr[:, cols]
            dhc1 = pltpu.roll(dext, n_ext - 1, 0)[0:ts]
            dhc2 = pltpu.roll(dext, n_ext - 2, 0)[0:ts]
            dcb_ref[:, cols] += jnp.sum(dhc, axis=0, keepdims=True)
            dcw_ref[0:1, cols] += jnp.sum(dhc2 * gate, axis=0, keepdims=True)
            dcw_ref[1:2, cols] += jnp.sum(dhc1 * gate, axis=0, keepdims=True)
            dcw_ref[2:3, cols] += jnp.sum(dhc * gate, axis=0, keepdims=True)
            dgate = dhc * cw_ref[2:3, cols] + dhc1 * cw_ref[1:2, cols] + dhc2 * cw_ref[0:1, cols]
            dub_ref[:, cols] = (da * (hc * sg)).astype(BF16)
            dub_ref[:, gcols] = dgate.astype(BF16)
        dext_scr[ts:n_ext, :] = dext_scr[0:CONV_HALO, :]
        dh1 = ALPHA * dz2_ref[...]
        for j in range(N_SHARD):
            dh1 = dh1 + _dot_nt(dub_ref[:, j * UP_SH:(j + 1) * UP_SH], wup_ref[j])
        xhat = xhat_ref[...]
        dg1_ref[...] += jnp.sum(dh1 * xhat, axis=0, keepdims=True)
        db1_ref[...] += jnp.sum(dh1, axis=0, keepdims=True)
        dz1 = _layernorm_bwd(dh1, xhat, rstd_ref[...], g1_ref[...])
        dz1_ref[...] = dz1
        dz1b_ref[...] = dz1.astype(BF16)

    tile = lambda w: pl.BlockSpec((ts, w), lambda i: (nt - 1 - i, 0))
    acc = lambda rws, w: pl.BlockSpec((rws, w), lambda i: (0, 0))
    return pl.pallas_call(
        body, name="ffn_bwd", grid=(nt,),
        in_specs=[tile(D_MODEL), tile(D_MODEL), tile(2 * D_FF), tile(D_FF), tile(D_MODEL), tile(1)]
        + [_whole()] * 4,
        out_specs=[tile(2 * D_FF), tile(D_MODEL), tile(D_MODEL), acc(1, D_MODEL), acc(1, D_MODEL),
                   acc(3, D_FF), acc(1, D_FF)],
        out_shape=[jax.ShapeDtypeStruct((s, 2 * D_FF), BF16),
                   jax.ShapeDtypeStruct((s, D_MODEL), F32), jax.ShapeDtypeStruct((s, D_MODEL), BF16),
                   jax.ShapeDtypeStruct((1, D_MODEL), F32),
                   jax.ShapeDtypeStruct((1, D_MODEL), F32), jax.ShapeDtypeStruct((3, D_FF), F32),
                   jax.ShapeDtypeStruct((1, D_FF), F32)],
        scratch_shapes=[pltpu.VMEM((ts + CONV_HALO, D_FF), F32), pltpu.VMEM((ts, D_FF), F32)],
        compiler_params=_params(("arbitrary",)),
    )(dz2, dz2b, ub, sd, xhat1, rstd1, wup4, wdown, cw, g1)


def _mix_bwd(dz1, pooled, ret, g, wout, wpool, pscale, ts, riders=(), after=()):
    s = dz1.shape[0]
    nt = s // ts

    def body(dz1_ref, pooled_ref, ret_ref, g_ref, wout_ref, wp_ref, ps_ref,
             dret_ref, dgp_ref, dwp_ref, dps_ref, eext_scr):
        i = pl.program_id(0)
        r = nt - 1 - i

        @pl.when(i == 0)
        def _():
            eext_scr[ts:ts + POOL_HALO, :] = jnp.zeros((POOL_HALO, POOL_W), F32)
            dwp_ref[...] = jnp.zeros_like(dwp_ref)
            dps_ref[...] = jnp.zeros_like(dps_ref)

        dzb = dz1_ref[...].astype(BF16)
        dcat_r = _dot_nt(dzb, wout_ref[0:RET_W, :])
        dcat_p = _dot_nt(dzb, wout_ref[RET_W:2 * RET_W, :])
        pos = (r * ts + lax.broadcasted_iota(jnp.int32, (ts, 1), 0) + 1).astype(F32)
        dpooled = []
        for gi, w in enumerate(POOL_WINDOWS):
            sl = slice(gi * HEAD_DIM, (gi + 1) * HEAD_DIM)
            pb = pooled_ref[:, sl]
            dy = dcat_p[:, sl]
            dps_ref[:, sl] += jnp.sum(dy * _dot(pb, wp_ref[gi]), axis=0, keepdims=True)
            dlin = (dy * ps_ref[:, sl]).astype(BF16)
            dwp_ref[gi] += _dot_tn(pb, dlin)
            dpg = _dot_nt(dlin, wp_ref[gi])
            dpooled.append(dpg)
            eext_scr[0:ts, sl] = dpg / jnp.minimum(pos, float(w))
        for gi, w in enumerate(POOL_WINDOWS):
            sl = slice(gi * HEAD_DIM, (gi + 1) * HEAD_DIM)
            acc = eext_scr[:, sl]
            shift = 1
            while shift < w:
                acc = acc + pltpu.roll(acc, ts + POOL_HALO - shift, 0)
                shift *= 2
            dgp_ref[:, RET_W + gi * HEAD_DIM:RET_W + (gi + 1) * HEAD_DIM] = (acc[0:ts] - dpooled[gi]).astype(BF16)
        eext_scr[ts:ts + POOL_HALO, :] = eext_scr[0:POOL_HALO, :]
        for h in range(HEADS):
            sl = slice(h * HEAD_DIM, (h + 1) * HEAD_DIM)
            rt = ret_ref[:, sl]
            rr = lax.rsqrt(jnp.mean(rt * rt, axis=-1, keepdims=True) + RMS_EPS)
            rn = rt * rr
            gh = g_ref[:, sl]
            sg = _sigmoid(gh)
            dy = dcat_r[:, sl]
            dgp_ref[:, sl] = (dy * rn * (sg * (1.0 + gh * (1.0 - sg)))).astype(BF16)
            drn = dy * (gh * sg)
            dret_ref[:, sl] = (rr * (drn - rn * jnp.mean(drn * rn, axis=-1, keepdims=True))).astype(BF16)

    tile = lambda w: pl.BlockSpec((ts, w), lambda i: (nt - 1 - i, 0))
    return _call(
        body, name="mix_bwd", grid=(nt,),
        in_specs=[tile(D_MODEL), tile(POOL_W), tile(RET_W), tile(RET_W), _whole(), _whole(), _whole()],
        out_specs=[tile(RET_W), tile(2 * RET_W),
                   pl.BlockSpec((len(POOL_WINDOWS), HEAD_DIM, HEAD_DIM), lambda i: (0, 0, 0)),
                   pl.BlockSpec((1, POOL_W), lambda i: (0, 0))],
        out_shape=[jax.ShapeDtypeStruct((s, RET_W), BF16), jax.ShapeDtypeStruct((s, 2 * RET_W), BF16),
                   jax.ShapeDtypeStruct((len(POOL_WINDOWS), HEAD_DIM, HEAD_DIM), F32),
                   jax.ShapeDtypeStruct((1, POOL_W), F32)],
        scratch_shapes=[pltpu.VMEM((ts + POOL_HALO, POOL_W), F32)],
        sem=("arbitrary",), operands=(dz1, pooled, ret, g, wout, wpool, pscale), riders=riders,
        after=after,
    )


def _retention_bwd(q, k, v, dret, dgp, states, mask, qd, kd, cosf, sinf, riders=(), after=()):
    s = q.shape[0]
    ns = s // SUPER
    cdec = [gm ** float(SUPER) for gm in _gammas()]

    def body(q_ref, k_ref, v_ref, do_ref, dgp_ref, st_ref, mask_ref, qd_ref, kd_ref, cos_ref, sin_ref,
             dproj_ref, dstate_scr):
        i = pl.program_id(0)

        @pl.when(i == 0)
        def _():
            dstate_scr[...] = jnp.zeros_like(dstate_scr)

        cosf_t = cos_ref[...]
        sinf_t = sin_ref[...]
        for h in range(HEADS):
            sl = slice(h * HEAD_DIM, (h + 1) * HEAD_DIM)
            qh, kh, vh, doh = q_ref[:, sl], k_ref[:, sl], v_ref[:, sl], do_ref[:, sl]
            m = mask_ref[h]
            scb = (_dot_nt(qh, kh) * m).astype(BF16)
            dscb = (_dot_nt(doh, vh) * m).astype(BF16)
            stb = st_ref[0, h]
            dst = dstate_scr[h]
            dstb = dst.astype(BF16)
            qdb = (qh.astype(F32) * qd_ref[:, sl]).astype(BF16)
            kdb = (kh.astype(F32) * kd_ref[:, sl]).astype(BF16)
            dq = _dot(dscb, kh) + _dot_nt(doh, stb) * qd_ref[:, sl]
            dk = _dot_tn(dscb, qh) + _dot_nt(vh, dstb) * kd_ref[:, sl]
            dv = _dot_tn(scb, doh) + _dot(kdb, dstb)
            dstate_scr[h] = dst * cdec[h] + _dot_tn(qdb, doh)
            lo = h * HEAD_DIM
            dproj_ref[:, lo:lo + HEAD_DIM] = _rope_t(dq, cosf_t, sinf_t).astype(BF16)
            dproj_ref[:, RET_W + lo:RET_W + lo + HEAD_DIM] = _rope_t(dk * K_SCALE, cosf_t, sinf_t).astype(BF16)
            dproj_ref[:, 2 * RET_W + lo:2 * RET_W + lo + HEAD_DIM] = dv.astype(BF16)
        dproj_ref[:, 3 * RET_W:IN_W] = dgp_ref[...]

    tile = lambda w: pl.BlockSpec((SUPER, w), lambda i: (ns - 1 - i, 0))
    return _call(
        body, name="retention_bwd", grid=(ns,),
        in_specs=[tile(RET_W), tile(RET_W), tile(RET_W), tile(RET_W), tile(2 * RET_W),
                  pl.BlockSpec((1, HEADS, HEAD_DIM, HEAD_DIM), lambda i: (ns - 1 - i, 0, 0, 0)),
                  _whole(), _whole(), _whole(), tile(HEAD_DIM), tile(HEAD_DIM)],
        out_specs=[tile(IN_W)],
        out_shape=[jax.ShapeDtypeStruct((s, IN_W), BF16)],
        scratch_shapes=[pltpu.VMEM((HEADS, HEAD_DIM, HEAD_DIM), F32)],
        sem=("arbitrary",), operands=(q, k, v, dret, dgp, states, mask, qd, kd, cosf, sinf), riders=riders,
        after=after,
    )


def _dx(dz1, dproj, win4, ts, riders=(), after=()):
    s = dz1.shape[0]

    def body(dz1_ref, dp_ref, w_ref, dx_ref):
        acc = ALPHA * dz1_ref[...]
        for j in range(N_SHARD):
            acc = acc + _dot_nt(dp_ref[:, j * IN_SH:(j + 1) * IN_SH], w_ref[j])
        dx_ref[...] = acc

    tile = lambda w: pl.BlockSpec((ts, w), lambda i: (i, 0))
    return _call(
        body, name="dx", grid=(s // ts,),
        in_specs=[tile(D_MODEL), tile(IN_W), _whole()],
        out_specs=[tile(D_MODEL)],
        out_shape=[jax.ShapeDtypeStruct((s, D_MODEL), F32)],
        sem=("arbitrary",), operands=(dz1, dproj, win4), riders=riders, after=after,
    )


def _wgrad(a, b, tm, tn, name, stacked, m_outer, riders=(), after=()):
    s, m = a.shape
    n = b.shape[1]

    def body(a_ref, b_ref, o32_ref, o16_ref):
        res = _dot_tn(a_ref[...], b_ref[...])
        o32_ref[...] = res.reshape(o32_ref.shape)
        o16_ref[...] = res.astype(BF16).reshape(o16_ref.shape)

    if m_outer:
        grid, blocks = (m // tm, n // tn), (lambda g0, g1: (g0, g1))
    else:
        grid, blocks = (n // tn, m // tm), (lambda g0, g1: (g1, g0))
    if stacked:
        shape = (n // tn, m, tn)
        ospec = pl.BlockSpec((1, tm, tn), lambda g0, g1: (blocks(g0, g1)[1], blocks(g0, g1)[0], 0))
    else:
        shape = (m, n)
        ospec = pl.BlockSpec((tm, tn), lambda g0, g1: blocks(g0, g1))
    return _call(
        body, name=name, grid=grid,
        in_specs=[pl.BlockSpec((s, tm), lambda g0, g1: (0, blocks(g0, g1)[0])),
                  pl.BlockSpec((s, tn), lambda g0, g1: (0, blocks(g0, g1)[1]))],
        out_specs=[ospec, ospec],
        out_shape=[jax.ShapeDtypeStruct(shape, F32), jax.ShapeDtypeStruct(shape, BF16)],
        sem=("arbitrary", "arbitrary"), operands=(a, b), riders=riders, after=after,
    )


class _NoComm:
    def __init__(self, win4, wout, wup4, wdown):
        self.weights = dict(w_in=win4, w_out=wout, w_up=wup4, w_down=wdown)
        self.grads = {}

    def weight(self, name):
        return self.weights[name]

    def riders(self, call):
        return ()

    def after(self, call):
        return ()

    def landed(self, call, results, outs):
        pass

    def small_gradients(self, loss, small):
        pass

    def gradient(self, name, g32, g16):
        self.grads[name] = (g32, g16)


def _local_step(x, target, cw, cb, wpool, pscale, g1, b1, g2, b2, comm):
    s = x.shape[0]
    ts_a = min(512, s)
    ts_f = min(256, s)
    mask, qd, kd = _decay_tables()
    cosf, sinf = _rope_tables(s)
    wpool_b = wpool.astype(BF16)

    def run(call, fn, *args):
        outs, res = fn(*args, riders=comm.riders(call), after=comm.after(call))
        comm.landed(call, res, outs)
        return outs

    xb, q, k, v, g, pooled, cat = run("proj_pool", _proj_pool, x, comm.weight("w_in"), cosf, sinf, wpool_b,
                                      pscale, ts_a)
    ret, cat, states = run("retention_fwd", _retention_fwd, q, k, v, g, cat, mask, qd, kd)
    wout = comm.weight("w_out")
    xhat1, rstd1, h1b = run("outproj_ln1", _outproj_ln1, x, cat, wout, g1, b1, ts_a)
    wup4, wdown = comm.weight("w_up"), comm.weight("w_down")
    ub, act, sd, dz2, dz2b, loss, dg2, db2 = _ffn_fwd_loss(xhat1, h1b, target, wup4, wdown, cw, cb, g1, b1, g2, b2,
                                                           ts_f)

    dub, dz1, dz1b, dg1, db1, dcw, dcb = _ffn_bwd(dz2, dz2b, ub, sd, xhat1, rstd1, wup4, wdown, cw, g1, ts_f)
    half = D_MODEL // 2
    comm.gradient("w_up", *run("wgrad_up", _wgrad, h1b, dub, half, UP_SH, "wgrad_up", True, False))
    comm.gradient("w_out", *run("wgrad_out", _wgrad, cat, dz1b, D_MODEL, half, "wgrad_out", False, True))
    comm.gradient("w_down", *run("wgrad_down", _wgrad, act, dz2b, D_FF // 2, half, "wgrad_down", False, True))
    dret, dgp, dwp, dps = run("mix_bwd", _mix_bwd, dz1b, pooled, ret, g, wout, wpool_b, pscale, ts_a)
    small = dict(w_pool=dwp, pool_scale=dps, ln1_g=dg1, ln1_b=db1, conv_w=dcw, conv_b=dcb,
                 ln2_g=dg2, ln2_b=db2)
    comm.small_gradients(loss, small)
    dproj, = run("retention_bwd", _retention_bwd, q, k, v, dret, dgp, states, mask, qd, kd, cosf, sinf)
    comm.gradient("w_in", *run("wgrad_in", _wgrad, xb, dproj, D_MODEL, IN_SH, "wgrad_in", True, True))
    (grad_x,), _ = _dx(dz1, dproj, comm.weight("w_in"), ts_a, after=comm.after("dx"))
    return loss, grad_x, small


CAST_ROWS = 64
SHARD_SHAPES = ((D_MODEL, IN_SH), (OUT_SH, D_MODEL), (D_MODEL, UP_SH), (DOWN_SH, D_MODEL))
N_BIG = len(SHARD_SHAPES)
CW_PAD = (8, 768)


def _mesh_pos():
    return lax.axis_index("x"), lax.axis_index("y"), lax.axis_index("c")


def _other_chips(x, y):
    return [(1 - x, y), (x, 1 - y), (1 - x, 1 - y)]


def _half_rows(w, which):
    hr = SHARD_SHAPES[w][0] // 2
    return pl.ds(pl.multiple_of(which * hr, 16), hr)


def _gather_weights(shards, cw8, full):
    def body(*refs):
        in_refs = refs[:N_BIG]
        cw_ref = refs[N_BIG]
        out_refs = refs[N_BIG + 1:2 * N_BIG + 1]
        cwo_ref = refs[2 * N_BIG + 1]
        stage = refs[2 * N_BIG + 2:3 * N_BIG + 2]
        raw = refs[3 * N_BIG + 2:4 * N_BIG + 2 - len(full)]
        send_sems, recv_sems, fsend_sems, frecv_sems, cw_send, cw_recv, local_sems, load_sems = \
            refs[4 * N_BIG + 2 - len(full):]
        x, y, c = _mesh_pos()
        j0 = 2 * x + y
        chips = _other_chips(x, y)

        fetched = [w for w in range(N_BIG) if w not in full]
        f32 = {w: in_refs[w] for w in full}
        loads = []
        for n, w in enumerate(fetched):
            f32[w] = raw[n]
            loads.append(pltpu.make_async_copy(in_refs[w], raw[n], load_sems.at[n]))
            loads[-1].start()

        def cast_to_stage(w):
            def cast(i, carry):
                rows = pl.ds(pl.multiple_of(i * CAST_ROWS, CAST_ROWS), CAST_ROWS)
                stage[w][rows, :] = f32[w][rows, :].astype(BF16)
                return carry
            lax.fori_loop(0, SHARD_SHAPES[w][0] // CAST_ROWS, cast, 0)

        for w in full:
            cast_to_stage(w)

        jx, jy, jd = 2 * (1 - x) + y, 2 * x + (1 - y), 2 * (1 - x) + (1 - y)
        neighbours = [((1 - x, y, c), jx), ((x, 1 - y, c), jy)]
        passed = jnp.where(c == 0, jx, jy)
        pass_to = (jnp.where(c == 0, x, 1 - x), jnp.where(c == 0, 1 - y, y), c)

        def nbr(w, k, block):
            return pltpu.make_async_remote_copy(
                src_ref=stage[w].at[_half_rows(w, c), :], dst_ref=out_refs[w].at[block, _half_rows(w, c), :],
                send_sem=send_sems.at[w, k], recv_sem=recv_sems.at[w, k],
                device_id=neighbours[k][0], device_id_type=MESH)

        def relay(w, block):
            return pltpu.make_async_remote_copy(
                src_ref=out_refs[w].at[passed, _half_rows(w, c), :],
                dst_ref=out_refs[w].at[block, _half_rows(w, c), :],
                send_sem=send_sems.at[w, 2], recv_sem=recv_sems.at[w, 2],
                device_id=pass_to, device_id_type=MESH)

        def d2d(w, k, block, half):
            return pltpu.make_async_remote_copy(
                src_ref=out_refs[w].at[block, _half_rows(w, half), :],
                dst_ref=out_refs[w].at[block, _half_rows(w, half), :],
                send_sem=fsend_sems.at[w, k], recv_sem=frecv_sems.at[w, k],
                device_id=(x, y, 1 - c), device_id_type=MESH)

        def conv(k, block):
            chip = chips[k]
            return pltpu.make_async_remote_copy(
                src_ref=cw_ref, dst_ref=cwo_ref.at[block], send_sem=cw_send.at[k], recv_sem=cw_recv.at[k],
                device_id=(chip[0], chip[1], c), device_id_type=MESH)

        sent = [nbr(w, k, j0) for w in full for k in range(2)] + [conv(k, j0) for k in range(3)]
        for cp in sent:
            cp.start()
        for n, w in enumerate(fetched):
            loads[n].wait()
            cast_to_stage(w)
        local = [pltpu.make_async_copy(stage[w], out_refs[w].at[j0], local_sems.at[w]) for w in range(N_BIG)]
        local.append(pltpu.make_async_copy(cw_ref, cwo_ref.at[j0], local_sems.at[N_BIG]))
        for cp in local:
            cp.start()
        for w in full:
            for k, (_, block) in enumerate(neighbours):
                nbr(w, k, block).wait_recv()
            later = [relay(w, passed)] + [d2d(w, k, block, c) for k, (_, block) in enumerate(neighbours)]
            for cp in later:
                cp.start()
            sent += later
        for w in full:
            relay(w, jd).wait_recv()
            fw = d2d(w, 2, jd, c)
            fw.start()
            sent.append(fw)
        for w in full:
            for k, block in enumerate([jx, jy, jd]):
                d2d(w, k, block, 1 - c).wait_recv()
        for k, chip in enumerate(chips):
            conv(k, 2 * chip[0] + chip[1]).wait_recv()
        for cp in sent:
            cp.wait_send()
        for cp in local:
            cp.wait()

    out_shape = [jax.ShapeDtypeStruct((N_SHARD,) + shp, BF16) for shp in SHARD_SHAPES]
    out_shape.append(jax.ShapeDtypeStruct((N_SHARD,) + CW_PAD, F32))
    return pl.pallas_call(
        body, name="gather_weights",
        in_specs=[_whole() if w in full else HBM_SPEC for w in range(N_BIG)] + [_whole()],
        out_specs=[HBM_SPEC] * (N_BIG + 1),
        out_shape=out_shape,
        scratch_shapes=[pltpu.VMEM(shp, BF16) for shp in SHARD_SHAPES]
        + [pltpu.VMEM(shp, F32) for w, shp in enumerate(SHARD_SHAPES) if w not in full] + [
            pltpu.SemaphoreType.DMA((N_BIG, 3)), pltpu.SemaphoreType.DMA((N_BIG, 3)),
            pltpu.SemaphoreType.DMA((N_BIG, 3)), pltpu.SemaphoreType.DMA((N_BIG, 3)),
            pltpu.SemaphoreType.DMA((3,)), pltpu.SemaphoreType.DMA((3,)),
            pltpu.SemaphoreType.DMA((N_BIG + 1,)), pltpu.SemaphoreType.DMA((N_BIG - len(full),))],
        compiler_params=pltpu.CompilerParams(vmem_limit_bytes=VMEM_LIMIT),
    )(*shards, cw8)


def _gather_rider(arrays, ops):
    ws = sorted(arrays)

    def make(inplace, srcs, lands, send_sems, recv_sems):
        del srcs, lands
        x, y, c = _mesh_pos()
        j0, jx, jy, jd = 2 * x + y, 2 * (1 - x) + y, 2 * x + (1 - y), 2 * (1 - x) + (1 - y)
        x_nbr, y_nbr, sibling = (1 - x, y, c), (x, 1 - y, c), (x, y, 1 - c)
        starts, waits = [], []
        for n, (kind, w, (r0, nr)) in enumerate(ops):
            ref = inplace[ws.index(w)]
            hr = SHARD_SHAPES[w][0] // 2
            rows = lambda core: pl.ds(pl.multiple_of(core * hr + r0, 16), nr)
            mine, theirs = rows(c), rows(1 - c)
            if kind == "ici":
                moves = [(ref.at[j0, mine, :], x_nbr, ref.at[jx, mine, :]),
                         (ref.at[j0, mine, :], y_nbr, ref.at[jy, mine, :]),
                         (ref.at[j0, mine, :], (1 - x, 1 - y, c), ref.at[jd, mine, :])]
            elif kind == "nbr":
                moves = [(ref.at[j0, mine, :], x_nbr, ref.at[jx, mine, :]),
                         (ref.at[j0, mine, :], y_nbr, ref.at[jy, mine, :])]
            elif kind == "relay":
                passed = jnp.where(c == 0, jx, jy)
                to = (jnp.where(c == 0, x, 1 - x), jnp.where(c == 0, 1 - y, y), c)
                moves = [(ref.at[passed, mine, :], to, ref.at[jd, mine, :])]
            else:
                blocks = dict(d2d=[jx, jy, jd], d2d_nbr=[jx, jy], d2d_diag=[jd])[kind]
                moves = [(ref.at[b, mine, :], sibling, ref.at[b, theirs, :]) for b in blocks]
            for k, (src, to, landing) in enumerate(moves):
                sems = dict(send_sem=send_sems.at[3 * n + k], recv_sem=recv_sems.at[3 * n + k],
                            device_id=to, device_id_type=MESH)
                send = pltpu.make_async_remote_copy(src_ref=src, dst_ref=src, **sems)
                arrival = pltpu.make_async_remote_copy(src_ref=src, dst_ref=landing, **sems)
                starts.append(send)
                waits += [arrival.wait_recv, send.wait_send]
        return starts, waits

    return _Rider([arrays[w] for w in ws], [], [], 3 * len(ops), make)


def _whole_half(w):
    return (0, SHARD_SHAPES[w][0] // 2)


def _pair_rider(ws, g16s):
    def make(inplace, srcs, lands, send_sems, recv_sems):
        del inplace
        x, y, c = _mesh_pos()
        copies = [pltpu.make_async_remote_copy(
            src_ref=srcs[i].at[:, _half_rows(w, 1 - c), :], dst_ref=lands[i],
            send_sem=send_sems.at[i], recv_sem=recv_sems.at[i], device_id=(x, y, 1 - c), device_id_type=MESH)
            for i, w in enumerate(ws)]
        return copies, [cp.wait for cp in copies]

    lands = [jax.ShapeDtypeStruct((N_SHARD, SHARD_SHAPES[w][0] // 2, SHARD_SHAPES[w][1]), BF16) for w in ws]
    return _Rider([], g16s, lands, len(ws), make)


def _chip_rider(ws, p16s):
    def make(inplace, srcs, lands, send_sems, recv_sems):
        del inplace
        x, y, c = _mesh_pos()
        copies = []
        for i in range(len(ws)):
            for k, chip in enumerate(_other_chips(x, y)):
                copies.append(pltpu.make_async_remote_copy(
                    src_ref=srcs[i].at[2 * chip[0] + chip[1]], dst_ref=lands[i].at[k],
                    send_sem=send_sems.at[3 * i + k], recv_sem=recv_sems.at[3 * i + k],
                    device_id=(chip[0], chip[1], c), device_id_type=MESH))
        return copies, [cp.wait for cp in copies]

    lands = [jax.ShapeDtypeStruct((3, SHARD_SHAPES[w][0] // 2, SHARD_SHAPES[w][1]), BF16) for w in ws]
    return _Rider([], p16s, lands, 3 * len(ws), make)


def _final_rider(halves):
    def make(inplace, srcs, lands, send_sems, recv_sems):
        del inplace
        x, y, c = _mesh_pos()
        copies = [pltpu.make_async_remote_copy(
            src_ref=srcs[i], dst_ref=lands[i], send_sem=send_sems.at[i], recv_sem=recv_sems.at[i],
            device_id=(x, y, 1 - c), device_id_type=MESH) for i in range(len(halves))]
        return copies, [cp.wait for cp in copies]

    return _Rider([], halves, [jax.ShapeDtypeStruct(h.shape, h.dtype) for h in halves], len(halves), make)


def _comm_only(name, riders):
    _, res = _call(lambda: None, name=name, grid=(), in_specs=[], out_specs=[], out_shape=[], operands=(),
                   riders=riders)
    return res


class _SemList:
    def __init__(self, refs):
        self.at = list(refs)


def _merged_rider(riders):
    srcs = [a for r in riders for a in r.srcs]
    lands = [a for r in riders for a in r.lands]

    def make(inplace, src_refs, land_refs, send_sems, recv_sems):
        starts, waits = [], []
        s0 = l0 = c0 = 0
        for r in riders:
            part = r.make(inplace, src_refs[s0:s0 + len(r.srcs)], land_refs[l0:l0 + len(r.lands)],
                          _SemList(send_sems.at[c0:c0 + r.n_copies]), _SemList(recv_sems.at[c0:c0 + r.n_copies]))
            starts += part[0]
            waits += part[1]
            s0, l0, c0 = s0 + len(r.srcs), l0 + len(r.lands), c0 + r.n_copies
        return starts, waits

    return _Rider([], srcs, lands, sum(r.n_copies for r in riders), make)


def _split_start(name, rider):
    assert not rider.inplace
    ns, nl, n = len(rider.srcs), len(rider.lands), rider.n_copies

    def body(*refs):
        srcs, lands = refs[:ns], refs[ns:ns + nl]
        sems = refs[ns + nl:ns + nl + 2 * n]
        token = refs[-1]
        starts, _ = rider.make([], srcs, lands, _SemList(sems[:n]), _SemList(sems[n:]))
        for cp in starts:
            cp.start()
        token[...] = jnp.zeros_like(token)

    buffers = [pltpu.with_memory_space_constraint(a, pltpu.HBM) for a in rider.srcs]
    buffers += [pltpu.with_memory_space_constraint(lax.empty(s.shape, s.dtype), pltpu.HBM) for s in rider.lands]
    hbm = pl.BlockSpec(memory_space=pltpu.HBM)
    sem = pl.BlockSpec(memory_space=pltpu.SEMAPHORE)
    outs = pl.pallas_call(
        body, name=name,
        out_shape=tuple([pltpu.SemaphoreType.DMA(())] * (2 * n) + [pltpu.HBM(b.shape, b.dtype) for b in buffers]
                        + [jax.ShapeDtypeStruct((8, 128), F32)]),
        in_specs=[hbm] * (ns + nl),
        out_specs=tuple([sem] * (2 * n) + [hbm] * (ns + nl) + [_whole()]),
        input_output_aliases={i: 2 * n + i for i in range(ns + nl)},
        compiler_params=pltpu.CompilerParams(has_side_effects=pltpu.SideEffectType.DATAFLOW_SIDE_EFFECTING),
    )(*buffers)
    return (rider, outs[:2 * n], outs[2 * n:2 * n + ns + nl]), outs[-1]


def _split_parts(state, riders):
    merged, sems, buffers = state
    n, ns = merged.n_copies, len(merged.srcs)
    parts, s0, l0, c0 = [], 0, 0, 0
    for r in riders:
        parts.append((r, list(sems[c0:c0 + r.n_copies]) + list(sems[n + c0:n + c0 + r.n_copies]),
                      list(buffers[s0:s0 + len(r.srcs)]) + list(buffers[ns + l0:ns + l0 + len(r.lands)])))
        s0, l0, c0 = s0 + len(r.srcs), l0 + len(r.lands), c0 + r.n_copies
    return parts


def _split_wait(name, state, after):
    rider, sems, buffers = state
    ns, nl, n = len(rider.srcs), len(rider.lands), rider.n_copies

    def body(*refs):
        srcs, lands = refs[:ns], refs[ns:ns + nl]
        sem_refs = refs[ns + nl:ns + nl + 2 * n]
        _, waits = rider.make([], srcs, lands, _SemList(sem_refs[:n]), _SemList(sem_refs[n:]))
        for wait in waits:
            wait()

    hbm = pl.BlockSpec(memory_space=pltpu.HBM)
    sem = pl.BlockSpec(memory_space=pltpu.SEMAPHORE)
    outs = pl.pallas_call(
        body, name=name,
        out_shape=tuple(pltpu.HBM(b.shape, b.dtype) for b in buffers),
        in_specs=[hbm] * (ns + nl) + [sem] * (2 * n) + [HBM_SPEC],
        out_specs=tuple([hbm] * (ns + nl)),
        input_output_aliases={i: i for i in range(ns + nl)},
        compiler_params=pltpu.CompilerParams(has_side_effects=pltpu.SideEffectType.DATAFLOW_SIDE_EFFECTING),
    )(*buffers, *sems, after)
    return list(outs[:ns]), list(outs[ns:])


def _pair_sum(pos, ws, g32s, recvs):
    n = len(ws)

    def body(pos_ref, *refs):
        g_refs, r_refs = refs[:n], refs[n:2 * n]
        p32_refs, p16_refs = refs[2 * n:3 * n], refs[3 * n:]
        for i in range(n):
            tot = g_refs[i][...] + r_refs[i][...].astype(F32)
            p16_refs[i][...] = tot.astype(BF16)

            @pl.when(pl.program_id(0) == pos_ref[1])
            def _(i=i, tot=tot):
                p32_refs[i][...] = tot

    halves = [(SHARD_SHAPES[w][0] // 2, SHARD_SHAPES[w][1]) for w in ws]
    own = [pl.BlockSpec((None, None) + h, lambda j, pos_ref: (j, pos_ref[0], 0, 0)) for h in halves]
    blk = [pl.BlockSpec((None,) + h, lambda j, pos_ref: (j, 0, 0)) for h in halves]
    mine = [pl.BlockSpec(h, lambda j, pos_ref: (0, 0)) for h in halves]
    g4 = [g.reshape((N_SHARD, 2) + h) for g, h in zip(g32s, halves)]
    outs = pl.pallas_call(
        body, name="pair_sum_" + "_".join(str(w) for w in ws),
        grid_spec=pltpu.PrefetchScalarGridSpec(
            num_scalar_prefetch=1, grid=(N_SHARD,), in_specs=own + blk, out_specs=mine + blk),
        out_shape=[jax.ShapeDtypeStruct(h, F32) for h in halves]
        + [jax.ShapeDtypeStruct((N_SHARD,) + h, BF16) for h in halves],
        compiler_params=_params(("arbitrary",)),
    )(pos, *g4, *recvs)
    return outs[:n], outs[n:]


def _chip_sum(pos, p32s, recvs):
    parts = 2

    def body(pos_ref, *refs):
        del pos_ref
        p_refs, r_refs, f_refs = refs[:N_BIG], refs[N_BIG:2 * N_BIG], refs[2 * N_BIG:]
        for w in range(N_BIG):
            f_refs[w][...] = ((p_refs[w][...] + r_refs[w][0].astype(F32)) + r_refs[w][1].astype(F32)) \
                + r_refs[w][2].astype(F32)

    quarters = [(r // 2 // parts, cc) for r, cc in SHARD_SHAPES]
    own = [pl.BlockSpec(qt, lambda i, pos_ref: (i, 0)) for qt in quarters]
    rcv = [pl.BlockSpec((3,) + qt, lambda i, pos_ref: (0, i, 0)) for qt in quarters]
    out = [pl.BlockSpec(qt, lambda i, pos_ref: (i, 0)) for qt in quarters]
    return pl.pallas_call(
        body, name="chip_sum",
        grid_spec=pltpu.PrefetchScalarGridSpec(
            num_scalar_prefetch=1, grid=(parts,), in_specs=own + rcv, out_specs=out),
        out_shape=[jax.ShapeDtypeStruct((r // 2, cc), F32) for r, cc in SHARD_SHAPES],
        compiler_params=_params(("arbitrary",)),
    )(pos, *p32s, *recvs)


def _adamw(w, g, m, v):
    m_new = ADAM_B1 * m + (1.0 - ADAM_B1) * g
    v_new = ADAM_B2 * v + (1.0 - ADAM_B2) * (g * g)
    m_hat = m_new / (1.0 - ADAM_B1 ** ADAM_STEP)
    v_hat = v_new / (1.0 - ADAM_B2 ** ADAM_STEP)
    delta = -ADAM_LR * (m_hat / (jnp.sqrt(v_hat) + ADAM_EPS) + ADAM_WD * w)
    return delta, m_new, v_new


def _adam_half(name, which, grads, ws, ms, vs, into=None):
    nb = 4

    def body(which_ref, *refs):
        del which_ref
        groups = [refs[i * N_BIG:(i + 1) * N_BIG] for i in range(4)]
        g_refs, w_refs, m_refs, v_refs = groups
        go_refs, do_refs, mo_refs, vo_refs = [refs[len(refs) - (4 - i) * N_BIG:len(refs) - (3 - i) * N_BIG]
                                              for i in range(4)]
        for w in range(N_BIG):
            g = g_refs[w][...]
            delta, m_new, v_new = _adamw(w_refs[w][...], g, m_refs[w][...], v_refs[w][...])
            go_refs[w][...] = g
            do_refs[w][...] = delta
            mo_refs[w][...] = m_new
            vo_refs[w][...] = v_new

    blocks = [(r // 2 // nb, cc) for r, cc in SHARD_SHAPES]
    half = [pl.BlockSpec(b, lambda i, which_ref: (i, 0)) for b in blocks]
    full = [pl.BlockSpec((None,) + b, lambda i, which_ref: (0, which_ref[0] * nb + i, 0)) for b in blocks]
    shapes = [jax.ShapeDtypeStruct((1,) + shp, F32) for shp in SHARD_SHAPES]
    carried = [] if into is None else [a for kind in into for a in kind]
    first = 1 + 4 * N_BIG
    outs = pl.pallas_call(
        body, name=name,
        grid_spec=pltpu.PrefetchScalarGridSpec(
            num_scalar_prefetch=1, grid=(nb,), in_specs=half + full * 3 + [HBM_SPEC] * len(carried),
            out_specs=full * 4),
        out_shape=shapes * 4,
        input_output_aliases={first + i: i for i in range(len(carried))},
        compiler_params=_params(("arbitrary",)),
    )(which, *grads, *ws, *ms, *vs, *carried)
    return [outs[i * N_BIG:(i + 1) * N_BIG] for i in range(4)]


SMALL_ROWS = 8
ROW_CONV_B, ROW_POOL_SCALE, ROW_LN1_G, ROW_LN1_B, ROW_LN2_G, ROW_LN2_B, ROW_LOSS = range(7)
SMALL_VECS = ((ROW_CONV_B, D_FF), (ROW_POOL_SCALE, POOL_W), (ROW_LN1_G, D_MODEL), (ROW_LN1_B, D_MODEL),
              (ROW_LN2_G, D_MODEL), (ROW_LN2_B, D_MODEL))


def _small_pack(loss, vec_grads):
    def body(*refs):
        loss_ref, gvec, out_ref = refs[0], refs[1:-1], refs[-1]
        out_ref[...] = jnp.zeros_like(out_ref)
        for (row, n), ref in zip(SMALL_VECS, gvec):
            out_ref[row:row + 1, 0:n] = ref[...]
        out_ref[ROW_LOSS:ROW_LOSS + 1, 0:HEAD_DIM] = jnp.broadcast_to(loss_ref[...], (1, HEAD_DIM))

    return pl.pallas_call(
        body, name="small_pack", in_specs=[_whole()] * (1 + len(vec_grads)), out_specs=_whole(),
        out_shape=jax.ShapeDtypeStruct((SMALL_ROWS, D_FF), F32),
    )(loss, *vec_grads)


def _small_pair_sum(own, sibling):
    n = len(own)

    def body(*refs):
        x, y, _ = _mesh_pos()
        for i in range(n):
            refs[2 * n + i][2 * x + y] = refs[i][...] + refs[n + i][...]

    return pl.pallas_call(
        body, name="small_pair_sum", in_specs=[_whole()] * (2 * n), out_specs=[_whole()] * n,
        out_shape=[jax.ShapeDtypeStruct((N_SHARD,) + a.shape, F32) for a in own],
        compiler_params=pltpu.CompilerParams(vmem_limit_bytes=VMEM_LIMIT),
    )(*own, *sibling)


def _small_chip_rider(gathered):
    n = len(gathered)

    def make(inplace, srcs, lands, send_sems, recv_sems):
        del inplace, lands
        x, y, c = _mesh_pos()
        j0 = 2 * x + y
        starts, waits = [], []
        for i in range(n):
            for k, chip in enumerate(_other_chips(x, y)):
                sems = dict(send_sem=send_sems.at[3 * i + k], recv_sem=recv_sems.at[3 * i + k],
                            device_id=(chip[0], chip[1], c), device_id_type=MESH)
                send = pltpu.make_async_remote_copy(src_ref=srcs[i].at[j0], dst_ref=srcs[i].at[j0], **sems)
                arrival = pltpu.make_async_remote_copy(
                    src_ref=srcs[i].at[j0], dst_ref=srcs[i].at[2 * chip[0] + chip[1]], **sems)
                starts.append(send)
                waits += [arrival.wait_recv, send.wait_send]
        return starts, waits

    return _Rider([], gathered, [], 3 * n, make)


def _small_adam(all_a, all_b, all_c, wp, cwp, vec_ws, m_wp, m_cwp, vec_ms, v_wp, v_cwp, vec_vs):
    nv = len(SMALL_VECS)
    np_ = 2 + nv

    def body(*refs):
        all_a_ref, all_b_ref, all_c_ref = refs[0:3]
        w_all, m_all, v_all = (refs[3 + i * np_:3 + (i + 1) * np_] for i in range(3))
        loss_out = refs[3 + 3 * np_]
        outs = refs[4 + 3 * np_:]
        x, y, _ = _mesh_pos()
        j0 = 2 * x + y
        tot_a = ((all_a_ref[0] + all_a_ref[1]) + all_a_ref[2]) + all_a_ref[3]
        tot_b = ((all_b_ref[0] + all_b_ref[1]) + all_b_ref[2]) + all_b_ref[3]
        tot_c = ((all_c_ref[0, j0] + all_c_ref[1, j0]) + all_c_ref[2, j0]) + all_c_ref[3, j0]
        loss_out[...] = tot_b[ROW_LOSS:ROW_LOSS + 1, 0:1]
        grads = [tot_a, tot_c] + [tot_b[row:row + 1, 0:n] for row, n in SMALL_VECS]
        for p in range(np_):
            delta, m_new, v_new = _adamw(w_all[p][...], grads[p], m_all[p][...], v_all[p][...])
            outs[p][...] = grads[p]
            outs[np_ + p][...] = delta
            outs[2 * np_ + p][...] = m_new
            outs[3 * np_ + p][...] = v_new

    pshapes = [wp.shape, CW_PAD] + [wv.shape for wv in vec_ws]
    out_shape = [jax.ShapeDtypeStruct((1, 1), F32)] + [jax.ShapeDtypeStruct(s, F32) for s in pshapes] * 4
    outs = pl.pallas_call(
        body, name="small_adam",
        in_specs=[_whole()] * (3 + 3 * np_), out_specs=[_whole()] * len(out_shape), out_shape=out_shape,
        compiler_params=pltpu.CompilerParams(vmem_limit_bytes=VMEM_LIMIT),
    )(all_a, all_b, all_c, wp, cwp, *vec_ws, m_wp, m_cwp, *vec_ms, v_wp, v_cwp, *vec_vs)
    return outs[0], [outs[1 + i * np_:1 + (i + 1) * np_] for i in range(4)]


def _pad_cw(a):
    pad = [(0, 0)] * (a.ndim - 2) + [(0, CW_PAD[0] - a.shape[-2]), (0, CW_PAD[1] - a.shape[-1])]
    return jnp.pad(a, pad)


def kernel(x, w_in, w_pool, pool_scale, w_out, ln1_g, ln1_b, w_up, conv_w, conv_b, w_down, ln2_g, ln2_b, loss_target, m_w_in, m_w_pool, m_pool_scale, m_w_out, m_ln1_g, m_ln1_b, m_w_up, m_conv_w, m_conv_b, m_w_down, m_ln2_g, m_ln2_b, v_w_in, v_w_pool, v_pool_scale, v_w_out, v_ln1_g, v_ln1_b, v_w_up, v_conv_w, v_conv_b, v_w_down, v_ln2_g, v_ln2_b):
    pos = jnp.stack([lax.axis_index("c"), 2 * lax.axis_index("x") + lax.axis_index("y")]).astype(jnp.int32)
    order = ("w_in", "w_out", "w_up", "w_down")
    w_in_i, w_out_i, w_up_i, w_down_i = range(N_BIG)
    vec_names = ("conv_b", "pool_scale", "ln1_g", "ln1_b", "ln2_g", "ln2_b")

    gathered = _gather_weights([w_in[0], w_out[0], w_up[0], w_down[0]], _pad_cw(conv_w[0]), (w_in_i,))
    cw_full = jnp.transpose(gathered[N_BIG][:, 0:3, 0:DOWN_SH], (1, 0, 2)).reshape(3, D_FF)
    up_a, up_b, up_c = (0, 176), (176, 176), (352, 160)
    assert up_c[0] + up_c[1] == SHARD_SHAPES[w_up_i][0] // 2

    class MeshComm:
        def __init__(self):
            self.w = {i: gathered[i] for i in range(N_BIG)}
            self.g32, self.g16, self.p32, self.p16, self.recv_b = {}, {}, {}, {}, {}
            self.up_complete = False
            self.tokens, self.chips = {}, []

        def weight(self, name):
            i = order.index(name)
            if name == "w_up" and not self.up_complete:
                (arrs, _), = _comm_only("gather_up_last", [_gather_rider(
                    {i: self.w[i]}, [("d2d_diag", i, up_b), ("d2d", i, up_c)])])
                self.w[i], self.up_complete = arrs[0], True
            full = self.w[i]
            return full.reshape(-1, full.shape[-1]) if name in ("w_out", "w_down") else full

        def _gather(self, ws, ops):
            return _gather_rider({w: self.w[w] for w in ws}, ops), ("w", ws)

        def _pair(self, ws):
            return _pair_rider(ws, [self.g16[w] for w in ws]), ("recv_a", ws)

        def _chip(self, ws):
            return _chip_rider(ws, [self.p16[w] for w in ws]), ("recv_b", ws)

        def plan(self, call):
            out_all, down_all = _whole_half(w_out_i), _whole_half(w_down_i)
            if call == "proj_pool":
                return [self._gather([w_out_i, w_up_i, w_down_i],
                                     [("ici", w_out_i, out_all), ("nbr", w_down_i, down_all),
                                      ("nbr", w_up_i, up_a)])]
            if call == "retention_fwd":
                return [self._gather([w_out_i, w_up_i, w_down_i],
                                     [("d2d", w_out_i, out_all),
                                      ("relay", w_down_i, down_all), ("d2d_nbr", w_down_i, down_all),
                                      ("relay", w_up_i, up_a), ("d2d_nbr", w_up_i, up_a), ("nbr", w_up_i, up_b)])]
            if call == "outproj_ln1":
                return [self._gather([w_up_i, w_down_i],
                                     [("d2d_diag", w_down_i, down_all), ("d2d_diag", w_up_i, up_a),
                                      ("relay", w_up_i, up_b), ("d2d_nbr", w_up_i, up_b), ("ici", w_up_i, up_c)])]
            return []

        def after(self, call):
            return tuple(self.tokens.pop(call, ()))

        def riders(self, call):
            self.pending = self.plan(call)
            return [r for r, _ in self.pending]

        def _start(self, name, rider, before):
            state, token = _split_start(name, rider)
            self.tokens.setdefault(before, []).append(token)
            return state

        def _finish_pair(self, name, state, ws, after):
            _, lands = _split_wait(name, state, after)
            self._finish_sum(ws, lands)

        def landed(self, call, results, outs):
            for (_, (slot, ws)), (inplace, lands) in zip(self.pending, results):
                for w, arr in zip(ws, inplace if len(inplace) else lands):
                    getattr(self, slot)[w] = arr
            if call == "wgrad_out":
                self._finish_pair("pair_exchange_up_wait", self.pair_up, [w_up_i], outs[1])
                self.chips.append(([w_up_i], self._start(
                    "chip_exchange_up_start", self._chip([w_up_i])[0], "wgrad_down")))
            if call == "mix_bwd":
                ws = [w_out_i, w_down_i]
                self._finish_pair("pair_exchange_out_down_wait", self.pair_out_down, ws, outs[0])
            if call == "retention_bwd":
                own, sibling = _split_wait("small_pair_wait", self.small_pair, outs[0])
                self.small_chip = self._start(
                    "small_chip_start", _small_chip_rider(_small_pair_sum(own, sibling)), "wgrad_in")

        def small_gradients(self, loss, small):
            dcw4 = _pad_cw(jnp.transpose(small["conv_w"].reshape(3, N_SHARD, DOWN_SH), (1, 0, 2)))
            own = [small["w_pool"], _small_pack(loss, [small[n] for n in vec_names]), dcw4]
            ws = [w_out_i, w_down_i]
            parts = [self._chip(ws)[0], _final_rider(own)]
            chip, self.small_pair = _split_parts(
                self._start("chip_out_down_small_pair_start", _merged_rider(parts), "retention_bwd"), parts)
            self.chips.append((ws, chip))

        def gradient(self, name, g32, g16):
            w = order.index(name)
            shape = (N_SHARD,) + SHARD_SHAPES[w]
            self.g32[w], self.g16[w] = g32.reshape(shape), g16.reshape(shape)
            if name == "w_up":
                self.pair_up = self._start("pair_exchange_up_start", self._pair([w])[0], "wgrad_out")
            if name == "w_down":
                self.pair_out_down = self._start("pair_exchange_out_down_start",
                                                 self._pair([w_out_i, w_down_i])[0], "mix_bwd")
            if name == "w_in":
                (_, lands), = _comm_only("pair_exchange_in", [self._pair([w])[0]])
                self._finish_sum([w], lands)
                self.chips.append(([w], self._start("chip_exchange_in_start", self._chip([w])[0], "dx")))

        def _finish_sum(self, ws, lands):
            p32s, p16s = _pair_sum(pos, ws, [self.g32[w] for w in ws], lands)
            for w, p32, p16 in zip(ws, p32s, p16s):
                self.p32[w], self.p16[w] = p32, p16

        def finish(self, after):
            for n, (ws, state) in enumerate(self.chips):
                _, lands = _split_wait("chip_exchange_wait_%d" % n, state, after)
                for w, arr in zip(ws, lands):
                    self.recv_b[w] = arr
            return _split_wait("small_chip_wait", self.small_chip, after)[0]

    comm = MeshComm()
    loss, grad_x, small = _local_step(x[0], loss_target[0], cw_full, conv_b, w_pool[0], pool_scale,
                                      ln1_g, ln1_b, ln2_g, ln2_b, comm)

    given = dict(w_pool=w_pool, pool_scale=pool_scale, ln1_g=ln1_g, ln1_b=ln1_b, conv_w=conv_w, conv_b=conv_b,
                 ln2_g=ln2_g, ln2_b=ln2_b)
    given_m = dict(w_pool=m_w_pool, pool_scale=m_pool_scale, ln1_g=m_ln1_g, ln1_b=m_ln1_b, conv_w=m_conv_w,
                   conv_b=m_conv_b, ln2_g=m_ln2_g, ln2_b=m_ln2_b)
    given_v = dict(w_pool=v_w_pool, pool_scale=v_pool_scale, ln1_g=v_ln1_g, ln1_b=v_ln1_b, conv_w=v_conv_w,
                   conv_b=v_conv_b, ln2_g=v_ln2_g, ln2_b=v_ln2_b)
    args = []
    for src in (given, given_m, given_v):
        args += [src["w_pool"][0], _pad_cw(src["conv_w"][0]), [src[n] for n in vec_names]]
    small_sums = comm.finish(grad_x)
    loss_tot, small_out = _small_adam(*small_sums, *args)
    every = range(N_BIG)
    mine = _chip_sum(pos, [comm.p32[w] for w in every], [comm.recv_b[w] for w in every])
    final_state, _ = _split_start("pair_exchange_f32_start", _final_rider(mine))
    mine = final_state[2][:N_BIG]
    big = ([w_in, w_out, w_up, w_down], [m_w_in, m_w_out, m_w_up, m_w_down], [v_w_in, v_w_out, v_w_up, v_w_down])
    own_half = _adam_half("adam_own_half", pos[0:1], mine, *big)
    _, theirs = _split_wait("pair_exchange_f32_wait", final_state, own_half[0][0])
    big_out = _adam_half("adam_other_half", 1 - pos[0:1], theirs, *big, into=own_half)

    names = ("w_in", "w_pool", "pool_scale", "w_out", "ln1_g", "ln1_b", "w_up", "conv_w", "conv_b", "w_down",
             "ln2_g", "ln2_b")
    small_names = ("w_pool", "conv_w") + vec_names
    result = [loss_tot.reshape(()), grad_x[None]]
    for kind in range(4):
        for n in names:
            if n in order:
                result.append(big_out[kind][order.index(n)])
            else:
                val = small_out[kind][small_names.index(n)]
                if n == "conv_w":
                    val = val[0:3, 0:DOWN_SH][None]
                elif n == "w_pool":
                    val = val[None]
                result.append(val)
    return tuple(result)
```

```python
import functools

import numpy as np
import jax
import jax.numpy as jnp
from jax import lax
from jax.experimental import pallas as pl
from jax.experimental.pallas import tpu as pltpu

F32 = jnp.float32
BF16 = jnp.bfloat16

D_MODEL = 1024
HEADS = 4
HEAD_DIM = 128
RET_W = HEADS * HEAD_DIM
POOL_WINDOWS = (2, 4, 8, 16)
POOL_W = 512
IN_W = 4 * RET_W + POOL_W
D_FF = 2816
N_SHARD = 4
IN_SH = IN_W // N_SHARD
UP_SH = 2 * D_FF // N_SHARD
DOWN_SH = D_FF // N_SHARD
OUT_SH = D_MODEL // N_SHARD
ROPE_BASE = 10000.0
LN_EPS = 1e-5
RMS_EPS = 1e-6
ALPHA = 2.0 ** 0.25
K_SCALE = HEAD_DIM ** -0.5
SUPER = 256
CHUNK = 64
POOL_HALO = 16
CONV_HALO = 8
FFN_STRIP = 128
LN_ROWS = 32

ADAM_LR = 0.001
ADAM_B1 = 0.9
ADAM_B2 = 0.999
ADAM_EPS = 1e-08
ADAM_WD = 0.01
ADAM_STEP = 10

MESH = pl.DeviceIdType.MESH
VMEM_LIMIT = 56 * 1024 * 1024


def _dot(a, b):
    return jnp.dot(a, b, preferred_element_type=F32)


def _dot_nt(a, b):
    return lax.dot_general(a, b, (((1,), (1,)), ((), ())), preferred_element_type=F32)


def _dot_tn(a, b):
    return lax.dot_general(a, b, (((0,), (0,)), ((), ())), preferred_element_type=F32)


def _sigmoid(x):
    return 1.0 / (1.0 + jnp.exp(-x))


def _params(sem):
    return pltpu.CompilerParams(dimension_semantics=sem, vmem_limit_bytes=VMEM_LIMIT)


def _whole():
    return pl.BlockSpec(memory_space=pltpu.VMEM)


HBM_SPEC = pl.BlockSpec(memory_space=pl.ANY)


class _Rider:
    def __init__(self, inplace, srcs, lands, n_copies, make):
        self.inplace, self.srcs, self.lands, self.n_copies, self.make = list(inplace), list(srcs), list(lands), n_copies, make


def _call(body, *, name, grid, in_specs, out_specs, out_shape, operands, scratch_shapes=(), sem=(),
          aliases=None, riders=(), after=()):
    n_in, n_out, n_scr = len(in_specs), len(out_shape), len(scratch_shapes)
    in_specs, out_specs, out_shape = list(in_specs), list(out_specs), list(out_shape)
    operands, scratch_shapes, aliases = list(operands), list(scratch_shapes), dict(aliases or {})
    in_specs += [_whole()] * len(after)
    operands += list(after)
    for r in riders:
        for a in r.inplace:
            aliases[len(in_specs)] = len(out_shape)
            in_specs.append(HBM_SPEC)
            operands.append(a)
            out_specs.append(HBM_SPEC)
            out_shape.append(jax.ShapeDtypeStruct(a.shape, a.dtype))
        for a in r.srcs:
            in_specs.append(HBM_SPEC)
            operands.append(a)
        for shp in r.lands:
            out_specs.append(HBM_SPEC)
            out_shape.append(shp)
        scratch_shapes += [pltpu.SemaphoreType.DMA((r.n_copies,)), pltpu.SemaphoreType.DMA((r.n_copies,))]

    def full(*refs):
        ins = refs[:n_in]
        at = n_in + len(after)
        r_srcs = []
        for r in riders:
            at += len(r.inplace)
            r_srcs.append(refs[at:at + len(r.srcs)])
            at += len(r.srcs)
        outs = refs[at:at + n_out]
        at += n_out
        r_outs = []
        for r in riders:
            r_outs.append((refs[at:at + len(r.inplace)], refs[at + len(r.inplace):at + len(r.inplace) + len(r.lands)]))
            at += len(r.inplace) + len(r.lands)
        scr = refs[at:at + n_scr]
        at += n_scr
        r_sems = [refs[at + 2 * i:at + 2 * i + 2] for i in range(len(riders))]

        def copies():
            return [r.make(r_outs[i][0], r_srcs[i], r_outs[i][1], r_sems[i][0], r_sems[i][1])
                    for i, r in enumerate(riders)]

        def start():
            for starts, _ in copies():
                for cp in starts:
                    cp.start()

        def finish():
            for _, waits in copies():
                for wait in waits:
                    wait()

        if riders and grid:
            first = functools.reduce(jnp.logical_and, [pl.program_id(d) == 0 for d in range(len(grid))])
            last = functools.reduce(jnp.logical_and, [pl.program_id(d) == grid[d] - 1 for d in range(len(grid))])
            pl.when(first)(start)
            body(*ins, *outs, *scr)
            pl.when(last)(finish)
        else:
            if riders:
                start()
            body(*ins, *outs, *scr)
            if riders:
                finish()

    params = _params(sem) if grid else pltpu.CompilerParams(vmem_limit_bytes=VMEM_LIMIT)
    res = pl.pallas_call(
        full, name=name, grid=grid, in_specs=in_specs, out_specs=out_specs, out_shape=out_shape,
        scratch_shapes=scratch_shapes, input_output_aliases=aliases, compiler_params=params,
    )(*operands)
    outs, at, rider_res = res[:n_out], n_out, []
    for r in riders:
        rider_res.append((res[at:at + len(r.inplace)], res[at + len(r.inplace):at + len(r.inplace) + len(r.lands)]))
        at += len(r.inplace) + len(r.lands)
    return list(outs), rider_res


def _gammas():
    return [1.0 - 2.0 ** (-5.0 - h) for h in range(HEADS)]


def _decay_tables():
    idx = np.arange(SUPER)
    dist = np.abs(idx[:, None] - idx[None, :]).astype(np.float64)
    visible = (idx[None, :] // CHUNK) <= (idx[:, None] // CHUNK)
    mask = np.stack([np.where(visible, g ** dist, 0.0) for g in _gammas()])
    qd = np.concatenate([np.repeat((g ** (idx + 1.0))[:, None], HEAD_DIM, 1) for g in _gammas()], 1)
    kd = np.concatenate([np.repeat((g ** (SUPER - 1.0 - idx))[:, None], HEAD_DIM, 1) for g in _gammas()], 1)
    return (jnp.asarray(mask, F32), jnp.asarray(qd, F32), jnp.asarray(kd, F32))


def _rope_tables(s):
    inv_freq = ROPE_BASE ** (-np.arange(0, HEAD_DIM, 2, dtype=np.float64) / HEAD_DIM)
    ang = np.arange(s, dtype=np.float64)[:, None] * inv_freq[None, :]
    cos, sin = np.cos(ang), np.sin(ang)
    return (jnp.asarray(np.concatenate([cos, cos], 1), F32),
            jnp.asarray(np.concatenate([-sin, sin], 1), F32))


def _rope(t, cosf, sinf):
    return t * cosf + pltpu.roll(t, HEAD_DIM // 2, 1) * sinf


def _rope_t(t, cosf, sinf):
    return t * cosf - pltpu.roll(t, HEAD_DIM // 2, 1) * sinf


def _layernorm_fwd(z):
    mu = jnp.mean(z, axis=-1, keepdims=True)
    zc = z - mu
    var = jnp.mean(zc * zc, axis=-1, keepdims=True)
    rstd = lax.rsqrt(var + LN_EPS)
    return zc * rstd, rstd


def _layernorm_bwd(dy, xhat, rstd, gain):
    dxh = dy * gain
    m1 = jnp.mean(dxh, axis=-1, keepdims=True)
    m2 = jnp.mean(dxh * xhat, axis=-1, keepdims=True)
    return rstd * (dxh - m1 - xhat * m2)


def _proj_pool(x, win4, cosf, sinf, wpool, pscale, ts, riders=(), after=()):
    s = x.shape[0]
    nt = s // ts

    def body(x_ref, w_ref, cos_ref, sin_ref, wp_ref, ps_ref,
             xb_ref, q_ref, k_ref, v_ref, g_ref, pooled_ref, cat_ref, proj_scr, pext_scr):
        i = pl.program_id(0)
        xb = x_ref[...].astype(BF16)
        xb_ref[...] = xb
        for j in range(N_SHARD):
            proj_scr[:, j * IN_SH:(j + 1) * IN_SH] = _dot(xb, w_ref[j])
        cosf_t = cos_ref[...]
        sinf_t = sin_ref[...]
        for h in range(HEADS):
            lo = h * HEAD_DIM
            q_ref[:, lo:lo + HEAD_DIM] = _rope(proj_scr[:, lo:lo + HEAD_DIM], cosf_t, sinf_t).astype(BF16)
            kk = _rope(proj_scr[:, RET_W + lo:RET_W + lo + HEAD_DIM], cosf_t, sinf_t) * K_SCALE
            k_ref[:, lo:lo + HEAD_DIM] = kk.astype(BF16)
        v_ref[...] = proj_scr[:, 2 * RET_W:3 * RET_W].astype(BF16)
        g_ref[...] = proj_scr[:, 3 * RET_W:4 * RET_W]

        @pl.when(i == 0)
        def _():
            pext_scr[0:POOL_HALO, :] = jnp.zeros((POOL_HALO, POOL_W), F32)

        pext_scr[POOL_HALO:POOL_HALO + ts, :] = proj_scr[:, 4 * RET_W:IN_W]
        pos = (i * ts + lax.broadcasted_iota(jnp.int32, (ts, 1), 0) + 1).astype(F32)
        for gi, w in enumerate(POOL_WINDOWS):
            lo = gi * HEAD_DIM
            ext = pext_scr[:, lo:lo + HEAD_DIM]
            acc = ext
            shift = 1
            while shift < w:
                acc = acc + pltpu.roll(acc, shift, 0)
                shift *= 2
            tok = ext[POOL_HALO:POOL_HALO + ts]
            pooled = acc[POOL_HALO:POOL_HALO + ts] / jnp.minimum(pos, float(w)) - tok
            pooled_b = pooled.astype(BF16)
            pooled_ref[:, lo:lo + HEAD_DIM] = pooled_b
            lin = _dot(pooled_b, wp_ref[gi])
            cat_ref[:, lo:lo + HEAD_DIM] = (lin * ps_ref[:, lo:lo + HEAD_DIM]).astype(BF16)
        pext_scr[0:POOL_HALO, :] = pext_scr[ts:ts + POOL_HALO, :]

    tile = lambda w: pl.BlockSpec((ts, w), lambda i: (i, 0))
    return _call(
        body, name="proj_pool", grid=(nt,),
        in_specs=[tile(D_MODEL), _whole(), tile(HEAD_DIM), tile(HEAD_DIM), _whole(), _whole()],
        out_specs=[tile(D_MODEL), tile(RET_W), tile(RET_W), tile(RET_W), tile(RET_W), tile(POOL_W),
                   pl.BlockSpec((ts, POOL_W), lambda i: (i, 1))],
        out_shape=[jax.ShapeDtypeStruct((s, D_MODEL), BF16), jax.ShapeDtypeStruct((s, RET_W), BF16),
                   jax.ShapeDtypeStruct((s, RET_W), BF16), jax.ShapeDtypeStruct((s, RET_W), BF16),
                   jax.ShapeDtypeStruct((s, RET_W), F32), jax.ShapeDtypeStruct((s, POOL_W), BF16),
                   jax.ShapeDtypeStruct((s, 2 * RET_W), BF16)],
        scratch_shapes=[pltpu.VMEM((ts, IN_W), F32), pltpu.VMEM((ts + POOL_HALO, POOL_W), F32)],
        sem=("arbitrary",), operands=(x, win4, cosf, sinf, wpool, pscale), riders=riders, after=after,
    )


def _retention_fwd(q, k, v, g, cat, mask, qd, kd, riders=(), after=()):
    s = q.shape[0]
    ns = s // SUPER
    cdec = [gm ** float(SUPER) for gm in _gammas()]

    def body(q_ref, k_ref, v_ref, g_ref, cat_in, mask_ref, qd_ref, kd_ref,
             ret_ref, cat_ref, st_ref, state_scr):
        del cat_in
        n = pl.program_id(0)

        @pl.when(n == 0)
        def _():
            state_scr[...] = jnp.zeros_like(state_scr)

        for h in range(HEADS):
            sl = slice(h * HEAD_DIM, (h + 1) * HEAD_DIM)
            qh, kh, vh = q_ref[:, sl], k_ref[:, sl], v_ref[:, sl]
            sc = _dot_nt(qh, kh) * mask_ref[h]
            st = state_scr[h]
            stb = st.astype(BF16)
            st_ref[0, h] = stb
            qdb = (qh.astype(F32) * qd_ref[:, sl]).astype(BF16)
            kdb = (kh.astype(F32) * kd_ref[:, sl]).astype(BF16)
            ret = _dot(sc.astype(BF16), vh) + _dot(qdb, stb)
            state_scr[h] = st * cdec[h] + _dot_tn(kdb, vh)
            ret_ref[:, sl] = ret
            r = lax.rsqrt(jnp.mean(ret * ret, axis=-1, keepdims=True) + RMS_EPS)
            gh = g_ref[:, sl]
            cat_ref[:, sl] = ((ret * r) * (gh * _sigmoid(gh))).astype(BF16)

    tile = pl.BlockSpec((SUPER, RET_W), lambda n: (n, 0))
    return _call(
        body, name="retention_fwd", grid=(ns,),
        in_specs=[tile, tile, tile, tile, HBM_SPEC, _whole(), _whole(), _whole()],
        out_specs=[tile, tile, pl.BlockSpec((1, HEADS, HEAD_DIM, HEAD_DIM), lambda n: (n, 0, 0, 0))],
        out_shape=[jax.ShapeDtypeStruct((s, RET_W), F32), jax.ShapeDtypeStruct((s, 2 * RET_W), BF16),
                   jax.ShapeDtypeStruct((ns, HEADS, HEAD_DIM, HEAD_DIM), BF16)],
        scratch_shapes=[pltpu.VMEM((HEADS, HEAD_DIM, HEAD_DIM), F32)],
        aliases={4: 1}, sem=("arbitrary",), operands=(q, k, v, g, cat, mask, qd, kd), riders=riders,
        after=after,
    )


def _outproj_ln1(x, cat, wout, g1, b1, ts, riders=(), after=()):
    s = x.shape[0]

    def body(x_ref, cat_ref, w_ref, g_ref, b_ref, xhat_ref, rstd_ref, h1b_ref):
        z = ALPHA * x_ref[...] + _dot(cat_ref[...], w_ref[...])
        xhat, rstd = _layernorm_fwd(z)
        xhat_ref[...] = xhat
        rstd_ref[...] = rstd
        h1b_ref[...] = (xhat * g_ref[...] + b_ref[...]).astype(BF16)

    tile = lambda w: pl.BlockSpec((ts, w), lambda i: (i, 0))
    return _call(
        body, name="outproj_ln1", grid=(s // ts,),
        in_specs=[tile(D_MODEL), tile(D_MODEL), _whole(), _whole(), _whole()],
        out_specs=[tile(D_MODEL), tile(1), tile(D_MODEL)],
        out_shape=[jax.ShapeDtypeStruct((s, D_MODEL), F32), jax.ShapeDtypeStruct((s, 1), F32),
                   jax.ShapeDtypeStruct((s, D_MODEL), BF16)],
        sem=("arbitrary",), operands=(x, cat, wout, g1, b1), riders=riders, after=after,
    )


def _ffn_fwd_loss(xhat1, h1b, target, wup4, wdown, cw, cb, g1, b1, g2, b2, ts):
    s = xhat1.shape[0]

    def body(xhat_ref, h1b_ref, tgt_ref, wup_ref, wdn_ref, cw_ref, cb_ref, g1_ref, b1_ref, g2_ref, b2_ref,
             ub_ref, act_ref, sd_ref, dz2_ref, dz2b_ref, loss_ref, dg2_ref, db2_ref, val_scr, gext_scr, ffn_scr):
        i = pl.program_id(0)

        @pl.when(i == 0)
        def _():
            gext_scr[0:CONV_HALO, :] = jnp.zeros((CONV_HALO, D_FF), F32)
            loss_ref[...] = jnp.zeros_like(loss_ref)
            dg2_ref[...] = jnp.zeros_like(dg2_ref)
            db2_ref[...] = jnp.zeros_like(db2_ref)

        for half in range(2):
            lo = half * UP_SH
            gext_scr[CONV_HALO:CONV_HALO + ts, lo:lo + UP_SH] = _dot(h1b_ref[...], wup_ref[2 + half])
            val_scr[:, lo:lo + UP_SH] = _dot(h1b_ref[...], wup_ref[half])
            for c0 in range(lo, lo + UP_SH, FFN_STRIP):
                cols = slice(c0, c0 + FFN_STRIP)
                ext = gext_scr[:, cols]
                gate = ext[CONV_HALO:]
                hc = cb_ref[:, cols] + ((pltpu.roll(ext, 2, 0)[CONV_HALO:] * cw_ref[0:1, cols]
                                         + pltpu.roll(ext, 1, 0)[CONV_HALO:] * cw_ref[1:2, cols])
                                        + gate * cw_ref[2:3, cols])
                val = val_scr[:, cols]
                sg = _sigmoid(hc)
                si = hc * sg
                act_ref[:, cols] = (si * val).astype(BF16)
                ub_ref[:, cols] = val.astype(BF16)
                ub_ref[:, D_FF + c0:D_FF + c0 + FFN_STRIP] = gate.astype(BF16)
                sd_ref[:, cols] = hc.astype(BF16)
            part = _dot(act_ref[:, lo:lo + UP_SH], wdn_ref[lo:lo + UP_SH, :])
            if half == 0:
                ffn_scr[...] = part
            else:
                ffn_scr[...] += part

        gext_scr[0:CONV_HALO, :] = gext_scr[ts:ts + CONV_HALO, :]

        loss_acc = jnp.zeros((1, 1), F32)
        dg2_acc = jnp.zeros((1, D_MODEL), F32)
        db2_acc = jnp.zeros((1, D_MODEL), F32)
        for r0 in range(0, ts, LN_ROWS):
            rows = slice(r0, r0 + LN_ROWS)
            h1 = xhat_ref[rows, :] * g1_ref[...] + b1_ref[...]
            xhat2, rstd2 = _layernorm_fwd(ALPHA * h1 + ffn_scr[rows, :])
            diff = (xhat2 * g2_ref[...] + b2_ref[...]) - tgt_ref[rows, :]
            row = jnp.mean(diff * diff, axis=-1, keepdims=True)
            loss_acc = loss_acc + 0.5 * jnp.sum(row, axis=0, keepdims=True)
            dy = diff * (1.0 / D_MODEL)
            dg2_acc = dg2_acc + jnp.sum(dy * xhat2, axis=0, keepdims=True)
            db2_acc = db2_acc + jnp.sum(dy, axis=0, keepdims=True)
            dz2 = _layernorm_bwd(dy, xhat2, rstd2, g2_ref[...])
            dz2_ref[rows, :] = dz2
            dz2b_ref[rows, :] = dz2.astype(BF16)
        loss_ref[...] += loss_acc
        dg2_ref[...] += dg2_acc
        db2_ref[...] += db2_acc

    tile = lambda w: pl.BlockSpec((ts, w), lambda i: (i, 0))
    acc = lambda w: pl.BlockSpec((1, w), lambda i: (0, 0))
    return pl.pallas_call(
        body, name="ffn_fwd_loss", grid=(s // ts,),
        in_specs=[tile(D_MODEL), tile(D_MODEL), tile(D_MODEL)] + [_whole()] * 8,
        out_specs=[tile(2 * D_FF), tile(D_FF), tile(D_FF), tile(D_MODEL), tile(D_MODEL),
                   acc(1), acc(D_MODEL), acc(D_MODEL)],
        out_shape=[jax.ShapeDtypeStruct((s, 2 * D_FF), BF16), jax.ShapeDtypeStruct((s, D_FF), BF16),
                   jax.ShapeDtypeStruct((s, D_FF), BF16), jax.ShapeDtypeStruct((s, D_MODEL), F32),
                   jax.ShapeDtypeStruct((s, D_MODEL), BF16),
                   jax.ShapeDtypeStruct((1, 1), F32), jax.ShapeDtypeStruct((1, D_MODEL), F32),
                   jax.ShapeDtypeStruct((1, D_MODEL), F32)],
        scratch_shapes=[pltpu.VMEM((ts, D_FF), F32), pltpu.VMEM((ts + CONV_HALO, D_FF), F32),
                        pltpu.VMEM((ts, D_MODEL), F32)],
        compiler_params=_params(("arbitrary",)),
    )(xhat1, h1b, target, wup4, wdown, cw, cb, g1, b1, g2, b2)


def _ffn_bwd(dz2, dz2b, ub, sd, xhat1, rstd1, wup4, wdown, cw, g1, ts):
    s = dz2.shape[0]
    nt = s // ts

    def body(dz2_ref, dz2b_ref, ub_ref, sd_ref, xhat_ref, rstd_ref, wup_ref, wdn_ref, cw_ref, g1_ref,
             dub_ref, dz1_ref, dz1b_ref, dg1_ref, db1_ref, dcw_ref, dcb_ref, dext_scr, da_scr):
        i = pl.program_id(0)

        @pl.when(i == 0)
        def _():
            dext_scr[ts:ts + CONV_HALO, :] = jnp.zeros((CONV_HALO, D_FF), F32)
            dg1_ref[...] = jnp.zeros_like(dg1_ref)
            db1_ref[...] = jnp.zeros_like(db1_ref)
            dcw_ref[...] = jnp.zeros_like(dcw_ref)
            dcb_ref[...] = jnp.zeros_like(dcb_ref)

        da_scr[...] = _dot_nt(dz2b_ref[...], wdn_ref[...])
        n_ext = ts + CONV_HALO
        for c0 in range(0, D_FF, FFN_STRIP):
            cols = slice(c0, c0 + FFN_STRIP)
            gcols = slice(D_FF + c0, D_FF + c0 + FFN_STRIP)
            val = ub_ref[:, cols].astype(F32)
            gate = ub_ref[:, gcols].astype(F32)
            da = da_scr[:, cols]
            hc = sd_ref[:, cols].astype(F32)
            sg = _sigmoid(hc)
            dhc = da * val * (sg * (1.0 + hc * (1.0 - sg)))
            dext_scr[0:ts, cols] = dhc
            dext = dext_scr[:, cols]
            dhc1 = pltpu.roll(dext, n_ext - 1, 0)[0:ts]
            dhc2 = pltpu.roll(dext, n_ext - 2, 0)[0:ts]
            dcb_ref[:, cols] += jnp.sum(dhc, axis=0, keepdims=True)
            dcw_ref[0:1, cols] += jnp.sum(dhc2 * gate, axis=0, keepdims=True)
            dcw_ref[1:2, cols] += jnp.sum(dhc1 * gate, axis=0, keepdims=True)
            dcw_ref[2:3, cols] += jnp.sum(dhc * gate, axis=0, keepdims=True)
            dgate = dhc * cw_ref[2:3, cols] + dhc1 * cw_ref[1:2, cols] + dhc2 * cw_ref[0:1, cols]
            dub_ref[:, cols] = (da * (hc * sg)).astype(BF16)
            dub_ref[:, gcols] = dgate.astype(BF16)
        dext_scr[ts:n_ext, :] = dext_scr[0:CONV_HALO, :]
        dh1 = ALPHA * dz2_ref[...]
        for j in range(N_SHARD):
            dh1 = dh1 + _dot_nt(dub_ref[:, j * UP_SH:(j + 1) * UP_SH], wup_ref[j])
        xhat = xhat_ref[...]
        dg1_ref[...] += jnp.sum(dh1 * xhat, axis=0, keepdims=True)
        db1_ref[...] += jnp.sum(dh1, axis=0, keepdims=True)
        dz1 = _layernorm_bwd(dh1, xhat, rstd_ref[...], g1_ref[...])
        dz1_ref[...] = dz1
        dz1b_ref[...] = dz1.astype(BF16)

    tile = lambda w: pl.BlockSpec((ts, w), lambda i: (nt - 1 - i, 0))
    acc = lambda rws, w: pl.BlockSpec((rws, w), lambda i: (0, 0))
    return pl.pallas_call(
        body, name="ffn_bwd", grid=(nt,),
        in_specs=[tile(D_MODEL), tile(D_MODEL), tile(2 * D_FF), tile(D_FF), tile(D_MODEL), tile(1)]
        + [_whole()] * 4,
        out_specs=[tile(2 * D_FF), tile(D_MODEL), tile(D_MODEL), acc(1, D_MODEL), acc(1, D_MODEL),
                   acc(3, D_FF), acc(1, D_FF)],
        out_shape=[jax.ShapeDtypeStruct((s, 2 * D_FF), BF16),
                   jax.ShapeDtypeStruct((s, D_MODEL), F32), jax.ShapeDtypeStruct((s, D_MODEL), BF16),
                   jax.ShapeDtypeStruct((1, D_MODEL), F32),
                   jax.ShapeDtypeStruct((1, D_MODEL), F32), jax.ShapeDtypeStruct((3, D_FF), F32),
                   jax.ShapeDtypeStruct((1, D_FF), F32)],
        scratch_shapes=[pltpu.VMEM((ts + CONV_HALO, D_FF), F32), pltpu.VMEM((ts, D_FF), F32)],
        compiler_params=_params(("arbitrary",)),
    )(dz2, dz2b, ub, sd, xhat1, rstd1, wup4, wdown, cw, g1)


def _mix_bwd(dz1, pooled, ret, g, wout, wpool, pscale, ts, riders=(), after=()):
    s = dz1.shape[0]
    nt = s // ts

    def body(dz1_ref, pooled_ref, ret_ref, g_ref, wout_ref, wp_ref, ps_ref,
             dret_ref, dgp_ref, dwp_ref, dps_ref, eext_scr):
        i = pl.program_id(0)
        r = nt - 1 - i

        @pl.when(i == 0)
        def _():
            eext_scr[ts:ts + POOL_HALO, :] = jnp.zeros((POOL_HALO, POOL_W), F32)
            dwp_ref[...] = jnp.zeros_like(dwp_ref)
            dps_ref[...] = jnp.zeros_like(dps_ref)

        dzb = dz1_ref[...].astype(BF16)
        dcat_r = _dot_nt(dzb, wout_ref[0:RET_W, :])
        dcat_p = _dot_nt(dzb, wout_ref[RET_W:2 * RET_W, :])
        pos = (r * ts + lax.broadcasted_iota(jnp.int32, (ts, 1), 0) + 1).astype(F32)
        dpooled = []
        for gi, w in enumerate(POOL_WINDOWS):
            sl = slice(gi * HEAD_DIM, (gi + 1) * HEAD_DIM)
            pb = pooled_ref[:, sl]
            dy = dcat_p[:, sl]
            dps_ref[:, sl] += jnp.sum(dy * _dot(pb, wp_ref[gi]), axis=0, keepdims=True)
            dlin = (dy * ps_ref[:, sl]).astype(BF16)
            dwp_ref[gi] += _dot_tn(pb, dlin)
            dpg = _dot_nt(dlin, wp_ref[gi])
            dpooled.append(dpg)
            eext_scr[0:ts, sl] = dpg / jnp.minimum(pos, float(w))
        for gi, w in enumerate(POOL_WINDOWS):
            sl = slice(gi * HEAD_DIM, (gi + 1) * HEAD_DIM)
            acc = eext_scr[:, sl]
            shift = 1
            while shift < w:
                acc = acc + pltpu.roll(acc, ts + POOL_HALO - shift, 0)
                shift *= 2
            dgp_ref[:, RET_W + gi * HEAD_DIM:RET_W + (gi + 1) * HEAD_DIM] = (acc[0:ts] - dpooled[gi]).astype(BF16)
        eext_scr[ts:ts + POOL_HALO, :] = eext_scr[0:POOL_HALO, :]
        for h in range(HEADS):
            sl = slice(h * HEAD_DIM, (h + 1) * HEAD_DIM)
            rt = ret_ref[:, sl]
            rr = lax.rsqrt(jnp.mean(rt * rt, axis=-1, keepdims=True) + RMS_EPS)
            rn = rt * rr
            gh = g_ref[:, sl]
            sg = _sigmoid(gh)
            dy = dcat_r[:, sl]
            dgp_ref[:, sl] = (dy * rn * (sg * (1.0 + gh * (1.0 - sg)))).astype(BF16)
            drn = dy * (gh * sg)
            dret_ref[:, sl] = (rr * (drn - rn * jnp.mean(drn * rn, axis=-1, keepdims=True))).astype(BF16)

    tile = lambda w: pl.BlockSpec((ts, w), lambda i: (nt - 1 - i, 0))
    return _call(
        body, name="mix_bwd", grid=(nt,),
        in_specs=[tile(D_MODEL), tile(POOL_W), tile(RET_W), tile(RET_W), _whole(), _whole(), _whole()],
        out_specs=[tile(RET_W), tile(2 * RET_W),
                   pl.BlockSpec((len(POOL_WINDOWS), HEAD_DIM, HEAD_DIM), lambda i: (0, 0, 0)),
                   pl.BlockSpec((1, POOL_W), lambda i: (0, 0))],
        out_shape=[jax.ShapeDtypeStruct((s, RET_W), BF16), jax.ShapeDtypeStruct((s, 2 * RET_W), BF16),
                   jax.ShapeDtypeStruct((len(POOL_WINDOWS), HEAD_DIM, HEAD_DIM), F32),
                   jax.ShapeDtypeStruct((1, POOL_W), F32)],
        scratch_shapes=[pltpu.VMEM((ts + POOL_HALO, POOL_W), F32)],
        sem=("arbitrary",), operands=(dz1, pooled, ret, g, wout, wpool, pscale), riders=riders,
        after=after,
    )


def _retention_bwd(q, k, v, dret, dgp, states, mask, qd, kd, cosf, sinf, riders=(), after=()):
    s = q.shape[0]
    ns = s // SUPER
    cdec = [gm ** float(SUPER) for gm in _gammas()]

    def body(q_ref, k_ref, v_ref, do_ref, dgp_ref, st_ref, mask_ref, qd_ref, kd_ref, cos_ref, sin_ref,
             dproj_ref, dstate_scr):
        i = pl.program_id(0)

        @pl.when(i == 0)
        def _():
            dstate_scr[...] = jnp.zeros_like(dstate_scr)

        cosf_t = cos_ref[...]
        sinf_t = sin_ref[...]
        for h in range(HEADS):
            sl = slice(h * HEAD_DIM, (h + 1) * HEAD_DIM)
            qh, kh, vh, doh = q_ref[:, sl], k_ref[:, sl], v_ref[:, sl], do_ref[:, sl]
            m = mask_ref[h]
            scb = (_dot_nt(qh, kh) * m).astype(BF16)
            dscb = (_dot_nt(doh, vh) * m).astype(BF16)
            stb = st_ref[0, h]
            dst = dstate_scr[h]
            dstb = dst.astype(BF16)
            qdb = (qh.astype(F32) * qd_ref[:, sl]).astype(BF16)
            kdb = (kh.astype(F32) * kd_ref[:, sl]).astype(BF16)
            dq = _dot(dscb, kh) + _dot_nt(doh, stb) * qd_ref[:, sl]
            dk = _dot_tn(dscb, qh) + _dot_nt(vh, dstb) * kd_ref[:, sl]
            dv = _dot_tn(scb, doh) + _dot(kdb, dstb)
            dstate_scr[h] = dst * cdec[h] + _dot_tn(qdb, doh)
            lo = h * HEAD_DIM
            dproj_ref[:, lo:lo + HEAD_DIM] = _rope_t(dq, cosf_t, sinf_t).astype(BF16)
            dproj_ref[:, RET_W + lo:RET_W + lo + HEAD_DIM] = _rope_t(dk * K_SCALE, cosf_t, sinf_t).astype(BF16)
            dproj_ref[:, 2 * RET_W + lo:2 * RET_W + lo + HEAD_DIM] = dv.astype(BF16)
        dproj_ref[:, 3 * RET_W:IN_W] = dgp_ref[...]

    tile = lambda w: pl.BlockSpec((SUPER, w), lambda i: (ns - 1 - i, 0))
    return _call(
        body, name="retention_bwd", grid=(ns,),
        in_specs=[tile(RET_W), tile(RET_W), tile(RET_W), tile(RET_W), tile(2 * RET_W),
                  pl.BlockSpec((1, HEADS, HEAD_DIM, HEAD_DIM), lambda i: (ns - 1 - i, 0, 0, 0)),
                  _whole(), _whole(), _whole(), tile(HEAD_DIM), tile(HEAD_DIM)],
        out_specs=[tile(IN_W)],
        out_shape=[jax.ShapeDtypeStruct((s, IN_W), BF16)],
        scratch_shapes=[pltpu.VMEM((HEADS, HEAD_DIM, HEAD_DIM), F32)],
        sem=("arbitrary",), operands=(q, k, v, dret, dgp, states, mask, qd, kd, cosf, sinf), riders=riders,
        after=after,
    )


def _dx(dz1, dproj, win4, ts, riders=(), after=()):
    s = dz1.shape[0]

    def body(dz1_ref, dp_ref, w_ref, dx_ref):
        acc = ALPHA * dz1_ref[...]
        for j in range(N_SHARD):
            acc = acc + _dot_nt(dp_ref[:, j * IN_SH:(j + 1) * IN_SH], w_ref[j])
        dx_ref[...] = acc

    tile = lambda w: pl.BlockSpec((ts, w), lambda i: (i, 0))
    return _call(
        body, name="dx", grid=(s // ts,),
        in_specs=[tile(D_MODEL), tile(IN_W), _whole()],
        out_specs=[tile(D_MODEL)],
        out_shape=[jax.ShapeDtypeStruct((s, D_MODEL), F32)],
        sem=("arbitrary",), operands=(dz1, dproj, win4), riders=riders, after=after,
    )


def _wgrad(a, b, tm, tn, name, stacked, m_outer, riders=(), after=()):
    s, m = a.shape
    n = b.shape[1]

    def body(a_ref, b_ref, o32_ref, o16_ref):
        res = _dot_tn(a_ref[...], b_ref[...])
        o32_ref[...] = res.reshape(o32_ref.shape)
        o16_ref[...] = res.astype(BF16).reshape(o16_ref.shape)

    if m_outer:
        grid, blocks = (m // tm, n // tn), (lambda g0, g1: (g0, g1))
    else:
        grid, blocks = (n // tn, m // tm), (lambda g0, g1: (g1, g0))
    if stacked:
        shape = (n // tn, m, tn)
        ospec = pl.BlockSpec((1, tm, tn), lambda g0, g1: (blocks(g0, g1)[1], blocks(g0, g1)[0], 0))
    else:
        shape = (m, n)
        ospec = pl.BlockSpec((tm, tn), lambda g0, g1: blocks(g0, g1))
    return _call(
        body, name=name, grid=grid,
        in_specs=[pl.BlockSpec((s, tm), lambda g0, g1: (0, blocks(g0, g1)[0])),
                  pl.BlockSpec((s, tn), lambda g0, g1: (0, blocks(g0, g1)[1]))],
        out_specs=[ospec, ospec],
        out_shape=[jax.ShapeDtypeStruct(shape, F32), jax.ShapeDtypeStruct(shape, BF16)],
        sem=("arbitrary", "arbitrary"), operands=(a, b), riders=riders, after=after,
    )


class _NoComm:
    def __init__(self, win4, wout, wup4, wdown):
        self.weights = dict(w_in=win4, w_out=wout, w_up=wup4, w_down=wdown)
        self.grads = {}

    def weight(self, name):
        return self.weights[name]

    def riders(self, call):
        return ()

    def after(self, call):
        return ()

    def landed(self, call, results, outs):
        pass

    def small_gradients(self, loss, small):
        pass

    def gradient(self, name, g32, g16):
        self.grads[name] = (g32, g16)


def _local_step(x, target, cw, cb, wpool, pscale, g1, b1, g2, b2, comm):
    s = x.shape[0]
    ts_a = min(512, s)
    ts_f = min(256, s)
    mask, qd, kd = _decay_tables()
    cosf, sinf = _rope_tables(s)
    wpool_b = wpool.astype(BF16)

    def run(call, fn, *args):
        outs, res = fn(*args, riders=comm.riders(call), after=comm.after(call))
        comm.landed(call, res, outs)
        return outs

    xb, q, k, v, g, pooled, cat = run("proj_pool", _proj_pool, x, comm.weight("w_in"), cosf, sinf, wpool_b,
                                      pscale, ts_a)
    ret, cat, states = run("retention_fwd", _retention_fwd, q, k, v, g, cat, mask, qd, kd)
    wout = comm.weight("w_out")
    xhat1, rstd1, h1b = run("outproj_ln1", _outproj_ln1, x, cat, wout, g1, b1, ts_a)
    wup4, wdown = comm.weight("w_up"), comm.weight("w_down")
    ub, act, sd, dz2, dz2b, loss, dg2, db2 = _ffn_fwd_loss(xhat1, h1b, target, wup4, wdown, cw, cb, g1, b1, g2, b2,
                                                           ts_f)

    dub, dz1, dz1b, dg1, db1, dcw, dcb = _ffn_bwd(dz2, dz2b, ub, sd, xhat1, rstd1, wup4, wdown, cw, g1, ts_f)
    half = D_MODEL // 2
    comm.gradient("w_up", *run("wgrad_up", _wgrad, h1b, dub, half, UP_SH, "wgrad_up", True, False))
    comm.gradient("w_out", *run("wgrad_out", _wgrad, cat, dz1b, D_MODEL, half, "wgrad_out", False, True))
    comm.gradient("w_down", *run("wgrad_down", _wgrad, act, dz2b, D_FF // 2, half, "wgrad_down", False, True))
    dret, dgp, dwp, dps = run("mix_bwd", _mix_bwd, dz1b, pooled, ret, g, wout, wpool_b, pscale, ts_a)
    small = dict(w_pool=dwp, pool_scale=dps, ln1_g=dg1, ln1_b=db1, conv_w=dcw, conv_b=dcb,
                 ln2_g=dg2, ln2_b=db2)
    comm.small_gradients(loss, small)
    dproj, = run("retention_bwd", _retention_bwd, q, k, v, dret, dgp, states, mask, qd, kd, cosf, sinf)
    comm.gradient("w_in", *run("wgrad_in", _wgrad, xb, dproj, D_MODEL, IN_SH, "wgrad_in", True, True))
    (grad_x,), _ = _dx(dz1, dproj, comm.weight("w_in"), ts_a, after=comm.after("dx"))
    return loss, grad_x, small


CAST_ROWS = 64
SHARD_SHAPES = ((D_MODEL, IN_SH), (OUT_SH, D_MODEL), (D_MODEL, UP_SH), (DOWN_SH, D_MODEL))
N_BIG = len(SHARD_SHAPES)
CW_SHARD = (3, DOWN_SH)


def _mesh_pos():
    return lax.axis_index("x"), lax.axis_index("y"), lax.axis_index("c")


def _other_chips(x, y):
    return [(1 - x, y), (x, 1 - y), (1 - x, 1 - y)]


def _half_rows(w, which):
    hr = SHARD_SHAPES[w][0] // 2
    return pl.ds(pl.multiple_of(which * hr, 16), hr)


def _gather_weights(shards, cw_shard, full):
    def body(*refs):
        in_refs = refs[:N_BIG]
        cw_ref = refs[N_BIG]
        out_refs = refs[N_BIG + 1:2 * N_BIG + 1]
        cwo_ref = refs[2 * N_BIG + 1]
        stage = refs[2 * N_BIG + 2:3 * N_BIG + 2]
        raw = refs[3 * N_BIG + 2:4 * N_BIG + 2 - len(full)]
        send_sems, recv_sems, fsend_sems, frecv_sems, cw_send, cw_recv, local_sems, load_sems = \
            refs[4 * N_BIG + 2 - len(full):]
        x, y, c = _mesh_pos()
        j0 = 2 * x + y
        chips = _other_chips(x, y)

        fetched = [w for w in range(N_BIG) if w not in full]
        f32 = {w: in_refs[w] for w in full}
        loads = []
        for n, w in enumerate(fetched):
            f32[w] = raw[n]
            loads.append(pltpu.make_async_copy(in_refs[w], raw[n], load_sems.at[n]))
            loads[-1].start()

        def cast_to_stage(w):
            def cast(i, carry):
                rows = pl.ds(pl.multiple_of(i * CAST_ROWS, CAST_ROWS), CAST_ROWS)
                stage[w][rows, :] = f32[w][rows, :].astype(BF16)
                return carry
            lax.fori_loop(0, SHARD_SHAPES[w][0] // CAST_ROWS, cast, 0)

        for w in full:
            cast_to_stage(w)

        jx, jy, jd = 2 * (1 - x) + y, 2 * x + (1 - y), 2 * (1 - x) + (1 - y)
        neighbours = [((1 - x, y, c), jx), ((x, 1 - y, c), jy)]
        passed = jnp.where(c == 0, jx, jy)
        pass_to = (jnp.where(c == 0, x, 1 - x), jnp.where(c == 0, 1 - y, y), c)

        def nbr(w, k, block):
            return pltpu.make_async_remote_copy(
                src_ref=stage[w].at[_half_rows(w, c), :], dst_ref=out_refs[w].at[block, _half_rows(w, c), :],
                send_sem=send_sems.at[w, k], recv_sem=recv_sems.at[w, k],
                device_id=neighbours[k][0], device_id_type=MESH)

        def relay(w, block):
            return pltpu.make_async_remote_copy(
                src_ref=out_refs[w].at[passed, _half_rows(w, c), :],
                dst_ref=out_refs[w].at[block, _half_rows(w, c), :],
                send_sem=send_sems.at[w, 2], recv_sem=recv_sems.at[w, 2],
                device_id=pass_to, device_id_type=MESH)

        def d2d(w, k, block, half):
            return pltpu.make_async_remote_copy(
                src_ref=out_refs[w].at[block, _half_rows(w, half), :],
                dst_ref=out_refs[w].at[block, _half_rows(w, half), :],
                send_sem=fsend_sems.at[w, k], recv_sem=frecv_sems.at[w, k],
                device_id=(x, y, 1 - c), device_id_type=MESH)

        def conv(k, block):
            chip = chips[k]
            return pltpu.make_async_remote_copy(
                src_ref=cw_ref, dst_ref=cwo_ref.at[block], send_sem=cw_send.at[k], recv_sem=cw_recv.at[k],
                device_id=(chip[0], chip[1], c), device_id_type=MESH)

        sent = [nbr(w, k, j0) for w in full for k in range(2)] + [conv(k, j0) for k in range(3)]
        for cp in sent:
            cp.start()
        for n, w in enumerate(fetched):
            loads[n].wait()
            cast_to_stage(w)
        local = [pltpu.make_async_copy(stage[w], out_refs[w].at[j0], local_sems.at[w]) for w in range(N_BIG)]
        local.append(pltpu.make_async_copy(cw_ref, cwo_ref.at[j0], local_sems.at[N_BIG]))
        for cp in local:
            cp.start()
        for w in full:
            for k, (_, block) in enumerate(neighbours):
                nbr(w, k, block).wait_recv()
            later = [relay(w, passed)] + [d2d(w, k, block, c) for k, (_, block) in enumerate(neighbours)]
            for cp in later:
                cp.start()
            sent += later
        for w in full:
            relay(w, jd).wait_recv()
            fw = d2d(w, 2, jd, c)
            fw.start()
            sent.append(fw)
        for w in full:
            for k, block in enumerate([jx, jy, jd]):
                d2d(w, k, block, 1 - c).wait_recv()
        for k, chip in enumerate(chips):
            conv(k, 2 * chip[0] + chip[1]).wait_recv()
        for cp in sent:
            cp.wait_send()
        for cp in local:
            cp.wait()

    out_shape = [jax.ShapeDtypeStruct((N_SHARD,) + shp, BF16) for shp in SHARD_SHAPES]
    out_shape.append(jax.ShapeDtypeStruct((N_SHARD,) + CW_SHARD, F32))
    return pl.pallas_call(
        body, name="gather_weights",
        in_specs=[_whole() if w in full else HBM_SPEC for w in range(N_BIG)] + [_whole()],
        out_specs=[HBM_SPEC] * (N_BIG + 1),
        out_shape=out_shape,
        scratch_shapes=[pltpu.VMEM(shp, BF16) for shp in SHARD_SHAPES]
        + [pltpu.VMEM(shp, F32) for w, shp in enumerate(SHARD_SHAPES) if w not in full] + [
            pltpu.SemaphoreType.DMA((N_BIG, 3)), pltpu.SemaphoreType.DMA((N_BIG, 3)),
            pltpu.SemaphoreType.DMA((N_BIG, 3)), pltpu.SemaphoreType.DMA((N_BIG, 3)),
            pltpu.SemaphoreType.DMA((3,)), pltpu.SemaphoreType.DMA((3,)),
            pltpu.SemaphoreType.DMA((N_BIG + 1,)), pltpu.SemaphoreType.DMA((N_BIG - len(full),))],
        compiler_params=pltpu.CompilerParams(vmem_limit_bytes=VMEM_LIMIT),
    )(*shards, cw_shard)


def _gather_rider(arrays, ops):
    ws = sorted(arrays)

    def make(inplace, srcs, lands, send_sems, recv_sems):
        del srcs, lands
        x, y, c = _mesh_pos()
        j0, jx, jy, jd = 2 * x + y, 2 * (1 - x) + y, 2 * x + (1 - y), 2 * (1 - x) + (1 - y)
        x_nbr, y_nbr, sibling = (1 - x, y, c), (x, 1 - y, c), (x, y, 1 - c)
        starts, waits = [], []
        for n, (kind, w, (r0, nr)) in enumerate(ops):
            ref = inplace[ws.index(w)]
            hr = SHARD_SHAPES[w][0] // 2
            rows = lambda core: pl.ds(pl.multiple_of(core * hr + r0, 16), nr)
            mine, theirs = rows(c), rows(1 - c)
            if kind == "ici":
                moves = [(ref.at[j0, mine, :], x_nbr, ref.at[jx, mine, :]),
                         (ref.at[j0, mine, :], y_nbr, ref.at[jy, mine, :]),
                         (ref.at[j0, mine, :], (1 - x, 1 - y, c), ref.at[jd, mine, :])]
            elif kind == "nbr":
                moves = [(ref.at[j0, mine, :], x_nbr, ref.at[jx, mine, :]),
                         (ref.at[j0, mine, :], y_nbr, ref.at[jy, mine, :])]
            elif kind == "relay":
                passed = jnp.where(c == 0, jx, jy)
                to = (jnp.where(c == 0, x, 1 - x), jnp.where(c == 0, 1 - y, y), c)
                moves = [(ref.at[passed, mine, :], to, ref.at[jd, mine, :])]
            else:
                blocks = dict(d2d=[jx, jy, jd], d2d_nbr=[jx, jy], d2d_diag=[jd])[kind]
                moves = [(ref.at[b, mine, :], sibling, ref.at[b, theirs, :]) for b in blocks]
            for k, (src, to, landing) in enumerate(moves):
                sems = dict(send_sem=send_sems.at[3 * n + k], recv_sem=recv_sems.at[3 * n + k],
                            device_id=to, device_id_type=MESH)
                send = pltpu.make_async_remote_copy(src_ref=src, dst_ref=src, **sems)
                arrival = pltpu.make_async_remote_copy(src_ref=src, dst_ref=landing, **sems)
                starts.append(send)
                waits += [arrival.wait_recv, send.wait_send]
        return starts, waits

    return _Rider([arrays[w] for w in ws], [], [], 3 * len(ops), make)


def _whole_half(w):
    return (0, SHARD_SHAPES[w][0] // 2)


def _pair_rider(ws, g16s):
    def make(inplace, srcs, lands, send_sems, recv_sems):
        del inplace
        x, y, c = _mesh_pos()
        copies = [pltpu.make_async_remote_copy(
            src_ref=srcs[i].at[:, _half_rows(w, 1 - c), :], dst_ref=lands[i],
            send_sem=send_sems.at[i], recv_sem=recv_sems.at[i], device_id=(x, y, 1 - c), device_id_type=MESH)
            for i, w in enumerate(ws)]
        return copies, [cp.wait for cp in copies]

    lands = [jax.ShapeDtypeStruct((N_SHARD, SHARD_SHAPES[w][0] // 2, SHARD_SHAPES[w][1]), BF16) for w in ws]
    return _Rider([], g16s, lands, len(ws), make)


def _chip_rider(ws, p16s):
    def make(inplace, srcs, lands, send_sems, recv_sems):
        del inplace
        x, y, c = _mesh_pos()
        copies = []
        for i in range(len(ws)):
            for k, chip in enumerate(_other_chips(x, y)):
                copies.append(pltpu.make_async_remote_copy(
                    src_ref=srcs[i].at[2 * chip[0] + chip[1]], dst_ref=lands[i].at[k],
                    send_sem=send_sems.at[3 * i + k], recv_sem=recv_sems.at[3 * i + k],
                    device_id=(chip[0], chip[1], c), device_id_type=MESH))
        return copies, [cp.wait for cp in copies]

    lands = [jax.ShapeDtypeStruct((3, SHARD_SHAPES[w][0] // 2, SHARD_SHAPES[w][1]), BF16) for w in ws]
    return _Rider([], p16s, lands, 3 * len(ws), make)


def _final_rider(halves):
    def make(inplace, srcs, lands, send_sems, recv_sems):
        del inplace
        x, y, c = _mesh_pos()
        copies = [pltpu.make_async_remote_copy(
            src_ref=srcs[i], dst_ref=lands[i], send_sem=send_sems.at[i], recv_sem=recv_sems.at[i],
            device_id=(x, y, 1 - c), device_id_type=MESH) for i in range(len(halves))]
        return copies, [cp.wait for cp in copies]

    return _Rider([], halves, [jax.ShapeDtypeStruct(h.shape, h.dtype) for h in halves], len(halves), make)


def _comm_only(name, riders):
    _, res = _call(lambda: None, name=name, grid=(), in_specs=[], out_specs=[], out_shape=[], operands=(),
                   riders=riders)
    return res


class _SemList:
    def __init__(self, refs):
        self.at = list(refs)


def _merged_rider(riders):
    srcs = [a for r in riders for a in r.srcs]
    lands = [a for r in riders for a in r.lands]

    def make(inplace, src_refs, land_refs, send_sems, recv_sems):
        starts, waits = [], []
        s0 = l0 = c0 = 0
        for r in riders:
            part = r.make(inplace, src_refs[s0:s0 + len(r.srcs)], land_refs[l0:l0 + len(r.lands)],
                          _SemList(send_sems.at[c0:c0 + r.n_copies]), _SemList(recv_sems.at[c0:c0 + r.n_copies]))
            starts += part[0]
            waits += part[1]
            s0, l0, c0 = s0 + len(r.srcs), l0 + len(r.lands), c0 + r.n_copies
        return starts, waits

    return _Rider([], srcs, lands, sum(r.n_copies for r in riders), make)


def _split_start(name, rider):
    assert not rider.inplace
    ns, nl, n = len(rider.srcs), len(rider.lands), rider.n_copies

    def body(*refs):
        srcs, lands = refs[:ns], refs[ns:ns + nl]
        sems = refs[ns + nl:ns + nl + 2 * n]
        token = refs[-1]
        starts, _ = rider.make([], srcs, lands, _SemList(sems[:n]), _SemList(sems[n:]))
        for cp in starts:
            cp.start()
        token[...] = jnp.zeros_like(token)

    buffers = [pltpu.with_memory_space_constraint(a, pltpu.HBM) for a in rider.srcs]
    buffers += [pltpu.with_memory_space_constraint(lax.empty(s.shape, s.dtype), pltpu.HBM) for s in rider.lands]
    hbm = pl.BlockSpec(memory_space=pltpu.HBM)
    sem = pl.BlockSpec(memory_space=pltpu.SEMAPHORE)
    outs = pl.pallas_call(
        body, name=name,
        out_shape=tuple([pltpu.SemaphoreType.DMA(())] * (2 * n) + [pltpu.HBM(b.shape, b.dtype) for b in buffers]
                        + [jax.ShapeDtypeStruct((8, 128), F32)]),
        in_specs=[hbm] * (ns + nl),
        out_specs=tuple([sem] * (2 * n) + [hbm] * (ns + nl) + [_whole()]),
        input_output_aliases={i: 2 * n + i for i in range(ns + nl)},
        compiler_params=pltpu.CompilerParams(has_side_effects=pltpu.SideEffectType.DATAFLOW_SIDE_EFFECTING),
    )(*buffers)
    return (rider, outs[:2 * n], outs[2 * n:2 * n + ns + nl]), outs[-1]


def _split_parts(state, riders):
    merged, sems, buffers = state
    n, ns = merged.n_copies, len(merged.srcs)
    parts, s0, l0, c0 = [], 0, 0, 0
    for r in riders:
        parts.append((r, list(sems[c0:c0 + r.n_copies]) + list(sems[n + c0:n + c0 + r.n_copies]),
                      list(buffers[s0:s0 + len(r.srcs)]) + list(buffers[ns + l0:ns + l0 + len(r.lands)])))
        s0, l0, c0 = s0 + len(r.srcs), l0 + len(r.lands), c0 + r.n_copies
    return parts


def _split_wait(name, state, after):
    rider, sems, buffers = state
    ns, nl, n = len(rider.srcs), len(rider.lands), rider.n_copies

    def body(*refs):
        srcs, lands = refs[:ns], refs[ns:ns + nl]
        sem_refs = refs[ns + nl:ns + nl + 2 * n]
        _, waits = rider.make([], srcs, lands, _SemList(sem_refs[:n]), _SemList(sem_refs[n:]))
        for wait in waits:
            wait()

    hbm = pl.BlockSpec(memory_space=pltpu.HBM)
    sem = pl.BlockSpec(memory_space=pltpu.SEMAPHORE)
    outs = pl.pallas_call(
        body, name=name,
        out_shape=tuple(pltpu.HBM(b.shape, b.dtype) for b in buffers),
        in_specs=[hbm] * (ns + nl) + [sem] * (2 * n) + [HBM_SPEC],
        out_specs=tuple([hbm] * (ns + nl)),
        input_output_aliases={i: i for i in range(ns + nl)},
        compiler_params=pltpu.CompilerParams(has_side_effects=pltpu.SideEffectType.DATAFLOW_SIDE_EFFECTING),
    )(*buffers, *sems, after)
    return list(outs[:ns]), list(outs[ns:])


def _pair_sum(pos, ws, g32s, recvs):
    n = len(ws)

    def body(pos_ref, *refs):
        g_refs, r_refs = refs[:n], refs[n:2 * n]
        p32_refs, p16_refs = refs[2 * n:3 * n], refs[3 * n:]
        for i in range(n):
            tot = g_refs[i][...] + r_refs[i][...].astype(F32)
            p16_refs[i][...] = tot.astype(BF16)

            @pl.when(pl.program_id(0) == pos_ref[1])
            def _(i=i, tot=tot):
                p32_refs[i][...] = tot

    halves = [(SHARD_SHAPES[w][0] // 2, SHARD_SHAPES[w][1]) for w in ws]
    own = [pl.BlockSpec((None, None) + h, lambda j, pos_ref: (j, pos_ref[0], 0, 0)) for h in halves]
    blk = [pl.BlockSpec((None,) + h, lambda j, pos_ref: (j, 0, 0)) for h in halves]
    mine = [pl.BlockSpec(h, lambda j, pos_ref: (0, 0)) for h in halves]
    g4 = [g.reshape((N_SHARD, 2) + h) for g, h in zip(g32s, halves)]
    outs = pl.pallas_call(
        body, name="pair_sum_" + "_".join(str(w) for w in ws),
        grid_spec=pltpu.PrefetchScalarGridSpec(
            num_scalar_prefetch=1, grid=(N_SHARD,), in_specs=own + blk, out_specs=mine + blk),
        out_shape=[jax.ShapeDtypeStruct(h, F32) for h in halves]
        + [jax.ShapeDtypeStruct((N_SHARD,) + h, BF16) for h in halves],
        compiler_params=_params(("arbitrary",)),
    )(pos, *g4, *recvs)
    return outs[:n], outs[n:]


def _chip_sum(pos, p32s, recvs):
    parts = 2

    def body(pos_ref, *refs):
        del pos_ref
        p_refs, r_refs, f_refs = refs[:N_BIG], refs[N_BIG:2 * N_BIG], refs[2 * N_BIG:]
        for w in range(N_BIG):
            f_refs[w][...] = ((p_refs[w][...] + r_refs[w][0].astype(F32)) + r_refs[w][1].astype(F32)) \
                + r_refs[w][2].astype(F32)

    quarters = [(r // 2 // parts, cc) for r, cc in SHARD_SHAPES]
    own = [pl.BlockSpec(qt, lambda i, pos_ref: (i, 0)) for qt in quarters]
    rcv = [pl.BlockSpec((3,) + qt, lambda i, pos_ref: (0, i, 0)) for qt in quarters]
    out = [pl.BlockSpec(qt, lambda i, pos_ref: (i, 0)) for qt in quarters]
    return pl.pallas_call(
        body, name="chip_sum",
        grid_spec=pltpu.PrefetchScalarGridSpec(
            num_scalar_prefetch=1, grid=(parts,), in_specs=own + rcv, out_specs=out),
        out_shape=[jax.ShapeDtypeStruct((r // 2, cc), F32) for r, cc in SHARD_SHAPES],
        compiler_params=_params(("arbitrary",)),
    )(pos, *p32s, *recvs)


def _adamw(w, g, m, v):
    m_new = ADAM_B1 * m + (1.0 - ADAM_B1) * g
    v_new = ADAM_B2 * v + (1.0 - ADAM_B2) * (g * g)
    m_hat = m_new / (1.0 - ADAM_B1 ** ADAM_STEP)
    v_hat = v_new / (1.0 - ADAM_B2 ** ADAM_STEP)
    delta = -ADAM_LR * (m_hat / (jnp.sqrt(v_hat) + ADAM_EPS) + ADAM_WD * w)
    return delta, m_new, v_new


def _adam_half(name, which, grads, ws, ms, vs, into=None):
    nb = 4

    def body(which_ref, *refs):
        del which_ref
        groups = [refs[i * N_BIG:(i + 1) * N_BIG] for i in range(4)]
        g_refs, w_refs, m_refs, v_refs = groups
        go_refs, do_refs, mo_refs, vo_refs = [refs[len(refs) - (4 - i) * N_BIG:len(refs) - (3 - i) * N_BIG]
                                              for i in range(4)]
        for w in range(N_BIG):
            g = g_refs[w][...]
            delta, m_new, v_new = _adamw(w_refs[w][...], g, m_refs[w][...], v_refs[w][...])
            go_refs[w][...] = g
            do_refs[w][...] = delta
            mo_refs[w][...] = m_new
            vo_refs[w][...] = v_new

    blocks = [(r // 2 // nb, cc) for r, cc in SHARD_SHAPES]
    half = [pl.BlockSpec(b, lambda i, which_ref: (i, 0)) for b in blocks]
    full = [pl.BlockSpec((None,) + b, lambda i, which_ref: (0, which_ref[0] * nb + i, 0)) for b in blocks]
    shapes = [jax.ShapeDtypeStruct((1,) + shp, F32) for shp in SHARD_SHAPES]
    carried = [] if into is None else [a for kind in into for a in kind]
    first = 1 + 4 * N_BIG
    outs = pl.pallas_call(
        body, name=name,
        grid_spec=pltpu.PrefetchScalarGridSpec(
            num_scalar_prefetch=1, grid=(nb,), in_specs=half + full * 3 + [HBM_SPEC] * len(carried),
            out_specs=full * 4),
        out_shape=shapes * 4,
        input_output_aliases={first + i: i for i in range(len(carried))},
        compiler_params=_params(("arbitrary",)),
    )(which, *grads, *ws, *ms, *vs, *carried)
    return [outs[i * N_BIG:(i + 1) * N_BIG] for i in range(4)]


SMALL_ROWS = 8
ROW_CONV_B, ROW_POOL_SCALE, ROW_LN1_G, ROW_LN1_B, ROW_LN2_G, ROW_LN2_B, ROW_LOSS = range(7)
SMALL_VECS = ((ROW_CONV_B, D_FF), (ROW_POOL_SCALE, POOL_W), (ROW_LN1_G, D_MODEL), (ROW_LN1_B, D_MODEL),
              (ROW_LN2_G, D_MODEL), (ROW_LN2_B, D_MODEL))


def _small_pack(loss, vec_grads):
    def body(*refs):
        loss_ref, gvec, out_ref = refs[0], refs[1:-1], refs[-1]
        out_ref[...] = jnp.zeros_like(out_ref)
        for (row, n), ref in zip(SMALL_VECS, gvec):
            out_ref[row:row + 1, 0:n] = ref[...]
        out_ref[ROW_LOSS:ROW_LOSS + 1, 0:HEAD_DIM] = jnp.broadcast_to(loss_ref[...], (1, HEAD_DIM))

    return pl.pallas_call(
        body, name="small_pack", in_specs=[_whole()] * (1 + len(vec_grads)), out_specs=_whole(),
        out_shape=jax.ShapeDtypeStruct((SMALL_ROWS, D_FF), F32),
    )(loss, *vec_grads)


def _small_pair_sum(own, sibling):
    n = len(own)

    def body(*refs):
        x, y, _ = _mesh_pos()
        for i in range(n):
            refs[2 * n + i][2 * x + y] = refs[i][...] + refs[n + i][...]

    return pl.pallas_call(
        body, name="small_pair_sum", in_specs=[_whole()] * (2 * n), out_specs=[_whole()] * n,
        out_shape=[jax.ShapeDtypeStruct((N_SHARD,) + a.shape, F32) for a in own],
        compiler_params=pltpu.CompilerParams(vmem_limit_bytes=VMEM_LIMIT),
    )(*own, *sibling)


def _small_chip_rider(gathered):
    n = len(gathered)

    def make(inplace, srcs, lands, send_sems, recv_sems):
        del inplace, lands
        x, y, c = _mesh_pos()
        j0 = 2 * x + y
        starts, waits = [], []
        for i in range(n):
            for k, chip in enumerate(_other_chips(x, y)):
                sems = dict(send_sem=send_sems.at[3 * i + k], recv_sem=recv_sems.at[3 * i + k],
                            device_id=(chip[0], chip[1], c), device_id_type=MESH)
                send = pltpu.make_async_remote_copy(src_ref=srcs[i].at[j0], dst_ref=srcs[i].at[j0], **sems)
                arrival = pltpu.make_async_remote_copy(
                    src_ref=srcs[i].at[j0], dst_ref=srcs[i].at[2 * chip[0] + chip[1]], **sems)
                starts.append(send)
                waits += [arrival.wait_recv, send.wait_send]
        return starts, waits

    return _Rider([], gathered, [], 3 * n, make)


def _small_adam(all_a, all_b, all_c, wp, cwp, vec_ws, m_wp, m_cwp, vec_ms, v_wp, v_cwp, vec_vs):
    nv = len(SMALL_VECS)
    np_ = 2 + nv

    def body(*refs):
        all_a_ref, all_b_ref, all_c_ref = refs[0:3]
        w_all, m_all, v_all = (refs[3 + i * np_:3 + (i + 1) * np_] for i in range(3))
        loss_out = refs[3 + 3 * np_]
        outs = refs[4 + 3 * np_:]
        x, y, _ = _mesh_pos()
        j0 = 2 * x + y
        tot_a = ((all_a_ref[0] + all_a_ref[1]) + all_a_ref[2]) + all_a_ref[3]
        tot_b = ((all_b_ref[0] + all_b_ref[1]) + all_b_ref[2]) + all_b_ref[3]
        tot_c = ((all_c_ref[0, j0] + all_c_ref[1, j0]) + all_c_ref[2, j0]) + all_c_ref[3, j0]
        loss_out[...] = tot_b[ROW_LOSS:ROW_LOSS + 1, 0:1]
        grads = [tot_a, tot_c] + [tot_b[row:row + 1, 0:n] for row, n in SMALL_VECS]
        for p in range(np_):
            delta, m_new, v_new = _adamw(w_all[p][...], grads[p], m_all[p][...], v_all[p][...])
            outs[p][...] = grads[p]
            outs[np_ + p][...] = delta
            outs[2 * np_ + p][...] = m_new
            outs[3 * np_ + p][...] = v_new

    pshapes = [wp.shape, CW_SHARD] + [wv.shape for wv in vec_ws]
    out_shape = [jax.ShapeDtypeStruct((1, 1), F32)] + [jax.ShapeDtypeStruct(s, F32) for s in pshapes] * 4
    outs = pl.pallas_call(
        body, name="small_adam",
        in_specs=[_whole()] * (3 + 3 * np_), out_specs=[_whole()] * len(out_shape), out_shape=out_shape,
        compiler_params=pltpu.CompilerParams(vmem_limit_bytes=VMEM_LIMIT),
    )(all_a, all_b, all_c, wp, cwp, *vec_ws, m_wp, m_cwp, *vec_ms, v_wp, v_cwp, *vec_vs)
    return outs[0], [outs[1 + i * np_:1 + (i + 1) * np_] for i in range(4)]


def kernel(x, w_in, w_pool, pool_scale, w_out, ln1_g, ln1_b, w_up, conv_w, conv_b, w_down, ln2_g, ln2_b, loss_target, m_w_in, m_w_pool, m_pool_scale, m_w_out, m_ln1_g, m_ln1_b, m_w_up, m_conv_w, m_conv_b, m_w_down, m_ln2_g, m_ln2_b, v_w_in, v_w_pool, v_pool_scale, v_w_out, v_ln1_g, v_ln1_b, v_w_up, v_conv_w, v_conv_b, v_w_down, v_ln2_g, v_ln2_b):
    pos = jnp.stack([lax.axis_index("c"), 2 * lax.axis_index("x") + lax.axis_index("y")]).astype(jnp.int32)
    order = ("w_in", "w_out", "w_up", "w_down")
    w_in_i, w_out_i, w_up_i, w_down_i = range(N_BIG)
    vec_names = ("conv_b", "pool_scale", "ln1_g", "ln1_b", "ln2_g", "ln2_b")

    gathered = _gather_weights([w_in[0], w_out[0], w_up[0], w_down[0]], conv_w[0], (w_in_i,))
    cw_full = jnp.transpose(gathered[N_BIG], (1, 0, 2)).reshape(3, D_FF)
    up_a, up_b, up_c = (0, 176), (176, 176), (352, 160)
    assert up_c[0] + up_c[1] == SHARD_SHAPES[w_up_i][0] // 2

    class MeshComm:
        def __init__(self):
            self.w = {i: gathered[i] for i in range(N_BIG)}
            self.g32, self.g16, self.p32, self.p16, self.recv_b = {}, {}, {}, {}, {}
            self.up_complete = False
            self.tokens, self.chips = {}, []

        def weight(self, name):
            i = order.index(name)
            if name == "w_up" and not self.up_complete:
                (arrs, _), = _comm_only("gather_up_last", [_gather_rider(
                    {i: self.w[i]}, [("d2d_diag", i, up_b), ("d2d", i, up_c)])])
                self.w[i], self.up_complete = arrs[0], True
            full = self.w[i]
            return full.reshape(-1, full.shape[-1]) if name in ("w_out", "w_down") else full

        def _gather(self, ws, ops):
            return _gather_rider({w: self.w[w] for w in ws}, ops), ("w", ws)

        def _pair(self, ws):
            return _pair_rider(ws, [self.g16[w] for w in ws]), ("recv_a", ws)

        def _chip(self, ws):
            return _chip_rider(ws, [self.p16[w] for w in ws]), ("recv_b", ws)

        def plan(self, call):
            out_all, down_all = _whole_half(w_out_i), _whole_half(w_down_i)
            if call == "proj_pool":
                return [self._gather([w_out_i, w_up_i, w_down_i],
                                     [("ici", w_out_i, out_all), ("nbr", w_down_i, down_all),
                                      ("nbr", w_up_i, up_a)])]
            if call == "retention_fwd":
                return [self._gather([w_out_i, w_up_i, w_down_i],
                                     [("d2d", w_out_i, out_all),
                                      ("relay", w_down_i, down_all), ("d2d_nbr", w_down_i, down_all),
                                      ("relay", w_up_i, up_a), ("d2d_nbr", w_up_i, up_a), ("nbr", w_up_i, up_b)])]
            if call == "outproj_ln1":
                return [self._gather([w_up_i, w_down_i],
                                     [("d2d_diag", w_down_i, down_all), ("d2d_diag", w_up_i, up_a),
                                      ("relay", w_up_i, up_b), ("d2d_nbr", w_up_i, up_b), ("ici", w_up_i, up_c)])]
            return []

        def after(self, call):
            return tuple(self.tokens.pop(call, ()))

        def riders(self, call):
            self.pending = self.plan(call)
            return [r for r, _ in self.pending]

        def _start(self, name, rider, before):
            state, token = _split_start(name, rider)
            self.tokens.setdefault(before, []).append(token)
            return state

        def _finish_pair(self, name, state, ws, after):
            _, lands = _split_wait(name, state, after)
            self._finish_sum(ws, lands)

        def landed(self, call, results, outs):
            for (_, (slot, ws)), (inplace, lands) in zip(self.pending, results):
                for w, arr in zip(ws, inplace if len(inplace) else lands):
                    getattr(self, slot)[w] = arr
            if call == "wgrad_out":
                self._finish_pair("pair_exchange_up_wait", self.pair_up, [w_up_i], outs[1])
                self.chips.append(([w_up_i], self._start(
                    "chip_exchange_up_start", self._chip([w_up_i])[0], "wgrad_down")))
            if call == "mix_bwd":
                ws = [w_out_i, w_down_i]
                self._finish_pair("pair_exchange_out_down_wait", self.pair_out_down, ws, outs[0])
            if call == "retention_bwd":
                own, sibling = _split_wait("small_pair_wait", self.small_pair, outs[0])
                self.small_chip = self._start(
                    "small_chip_start", _small_chip_rider(_small_pair_sum(own, sibling)), "wgrad_in")

        def small_gradients(self, loss, small):
            dcw4 = jnp.transpose(small["conv_w"].reshape(3, N_SHARD, DOWN_SH), (1, 0, 2))
            own = [small["w_pool"], _small_pack(loss, [small[n] for n in vec_names]), dcw4]
            ws = [w_out_i, w_down_i]
            parts = [self._chip(ws)[0], _final_rider(own)]
            chip, self.small_pair = _split_parts(
                self._start("chip_out_down_small_pair_start", _merged_rider(parts), "retention_bwd"), parts)
            self.chips.append((ws, chip))

        def gradient(self, name, g32, g16):
            w = order.index(name)
            shape = (N_SHARD,) + SHARD_SHAPES[w]
            self.g32[w], self.g16[w] = g32.reshape(shape), g16.reshape(shape)
            if name == "w_up":
                self.pair_up = self._start("pair_exchange_up_start", self._pair([w])[0], "wgrad_out")
            if name == "w_down":
                self.pair_out_down = self._start("pair_exchange_out_down_start",
                                                 self._pair([w_out_i, w_down_i])[0], "mix_bwd")
            if name == "w_in":
                (_, lands), = _comm_only("pair_exchange_in", [self._pair([w])[0]])
                self._finish_sum([w], lands)
                self.chips.append(([w], self._start("chip_exchange_in_start", self._chip([w])[0], "dx")))

        def _finish_sum(self, ws, lands):
            p32s, p16s = _pair_sum(pos, ws, [self.g32[w] for w in ws], lands)
            for w, p32, p16 in zip(ws, p32s, p16s):
                self.p32[w], self.p16[w] = p32, p16

        def finish(self, after):
            for n, (ws, state) in enumerate(self.chips):
                _, lands = _split_wait("chip_exchange_wait_%d" % n, state, after)
                for w, arr in zip(ws, lands):
                    self.recv_b[w] = arr
            return _split_wait("small_chip_wait", self.small_chip, after)[0]

    comm = MeshComm()
    loss, grad_x, small = _local_step(x[0], loss_target[0], cw_full, conv_b, w_pool[0], pool_scale,
                                      ln1_g, ln1_b, ln2_g, ln2_b, comm)

    given = dict(w_pool=w_pool, pool_scale=pool_scale, ln1_g=ln1_g, ln1_b=ln1_b, conv_w=conv_w, conv_b=conv_b,
                 ln2_g=ln2_g, ln2_b=ln2_b)
    given_m = dict(w_pool=m_w_pool, pool_scale=m_pool_scale, ln1_g=m_ln1_g, ln1_b=m_ln1_b, conv_w=m_conv_w,
                   conv_b=m_conv_b, ln2_g=m_ln2_g, ln2_b=m_ln2_b)
    given_v = dict(w_pool=v_w_pool, pool_scale=v_pool_scale, ln1_g=v_ln1_g, ln1_b=v_ln1_b, conv_w=v_conv_w,
                   conv_b=v_conv_b, ln2_g=v_ln2_g, ln2_b=v_ln2_b)
    args = []
    for src in (given, given_m, given_v):
        args += [src["w_pool"][0], src["conv_w"][0], [src[n] for n in vec_names]]
    small_sums = comm.finish(grad_x)
    loss_tot, small_out = _small_adam(*small_sums, *args)
    every = range(N_BIG)
    mine = _chip_sum(pos, [comm.p32[w] for w in every], [comm.recv_b[w] for w in every])
    final_state, _ = _split_start("pair_exchange_f32_start", _final_rider(mine))
    mine = final_state[2][:N_BIG]
    big = ([w_in, w_out, w_up, w_down], [m_w_in, m_w_out, m_w_up, m_w_down], [v_w_in, v_w_out, v_w_up, v_w_down])
    own_half = _adam_half("adam_own_half", pos[0:1], mine, *big)
    _, theirs = _split_wait("pair_exchange_f32_wait", final_state, own_half[0][0])
    big_out = _adam_half("adam_other_half", 1 - pos[0:1], theirs, *big, into=own_half)

    names = ("w_in", "w_pool", "pool_scale", "w_out", "ln1_g", "ln1_b", "w_up", "conv_w", "conv_b", "w_down",
             "ln2_g", "ln2_b")
    small_names = ("w_pool", "conv_w") + vec_names
    result = [loss_tot.reshape(()), grad_x[None]]
    for kind in range(4):
        for n in names:
            if n in order:
                result.append(big_out[kind][order.index(n)])
            else:
                val = small_out[kind][small_names.index(n)]
                if n == "conv_w":
                    val = val[None]
                elif n == "w_pool":
                    val = val[None]
                result.append(val)
    return tuple(result)
```

```python
import functools

import numpy as np
import jax
import jax.numpy as jnp
from jax import lax
from jax.experimental import pallas as pl
from jax.experimental.pallas import tpu as pltpu

F32 = jnp.float32
BF16 = jnp.bfloat16

D_MODEL = 1024
HEADS = 4
HEAD_DIM = 128
RET_W = HEADS * HEAD_DIM
POOL_WINDOWS = (2, 4, 8, 16)
POOL_W = 512
IN_W = 4 * RET_W + POOL_W
D_FF = 2816
N_SHARD = 4
IN_SH = IN_W // N_SHARD
UP_SH = 2 * D_FF // N_SHARD
DOWN_SH = D_FF // N_SHARD
OUT_SH = D_MODEL // N_SHARD
ROPE_BASE = 10000.0
LN_EPS = 1e-5
RMS_EPS = 1e-6
ALPHA = 2.0 ** 0.25
K_SCALE = HEAD_DIM ** -0.5
SUPER = 256
CHUNK = 64
POOL_HALO = 16
CONV_HALO = 8
FFN_STRIP = 128
LN_ROWS = 32

ADAM_LR = 0.001
ADAM_B1 = 0.9
ADAM_B2 = 0.999
ADAM_EPS = 1e-08
ADAM_WD = 0.01
ADAM_STEP = 10

MESH = pl.DeviceIdType.MESH
VMEM_LIMIT = 56 * 1024 * 1024


def _dot(a, b):
    return jnp.dot(a, b, preferred_element_type=F32)


def _dot_nt(a, b):
    return lax.dot_general(a, b, (((1,), (1,)), ((), ())), preferred_element_type=F32)


def _dot_tn(a, b):
    return lax.dot_general(a, b, (((0,), (0,)), ((), ())), preferred_element_type=F32)


def _sigmoid(x):
    return 1.0 / (1.0 + jnp.exp(-x))


def _params(sem):
    return pltpu.CompilerParams(dimension_semantics=sem, vmem_limit_bytes=VMEM_LIMIT)


def _whole():
    return pl.BlockSpec(memory_space=pltpu.VMEM)


HBM_SPEC = pl.BlockSpec(memory_space=pl.ANY)


class _Rider:
    def __init__(self, inplace, srcs, lands, n_copies, make):
        self.inplace, self.srcs, self.lands, self.n_copies, self.make = list(inplace), list(srcs), list(lands), n_copies, make


def _call(body, *, name, grid, in_specs, out_specs, out_shape, operands, scratch_shapes=(), sem=(),
          aliases=None, riders=(), after=()):
    n_in, n_out, n_scr = len(in_specs), len(out_shape), len(scratch_shapes)
    in_specs, out_specs, out_shape = list(in_specs), list(out_specs), list(out_shape)
    operands, scratch_shapes, aliases = list(operands), list(scratch_shapes), dict(aliases or {})
    in_specs += [_whole()] * len(after)
    operands += list(after)
    for r in riders:
        for a in r.inplace:
            aliases[len(in_specs)] = len(out_shape)
            in_specs.append(HBM_SPEC)
            operands.append(a)
            out_specs.append(HBM_SPEC)
            out_shape.append(jax.ShapeDtypeStruct(a.shape, a.dtype))
        for a in r.srcs:
            in_specs.append(HBM_SPEC)
            operands.append(a)
        for shp in r.lands:
            out_specs.append(HBM_SPEC)
            out_shape.append(shp)
        scratch_shapes += [pltpu.SemaphoreType.DMA((r.n_copies,)), pltpu.SemaphoreType.DMA((r.n_copies,))]

    def full(*refs):
        ins = refs[:n_in]
        at = n_in + len(after)
        r_srcs = []
        for r in riders:
            at += len(r.inplace)
            r_srcs.append(refs[at:at + len(r.srcs)])
            at += len(r.srcs)
        outs = refs[at:at + n_out]
        at += n_out
        r_outs = []
        for r in riders:
            r_outs.append((refs[at:at + len(r.inplace)], refs[at + len(r.inplace):at + len(r.inplace) + len(r.lands)]))
            at += len(r.inplace) + len(r.lands)
        scr = refs[at:at + n_scr]
        at += n_scr
        r_sems = [refs[at + 2 * i:at + 2 * i + 2] for i in range(len(riders))]

        def copies():
            return [r.make(r_outs[i][0], r_srcs[i], r_outs[i][1], r_sems[i][0], r_sems[i][1])
                    for i, r in enumerate(riders)]

        def start():
            for starts, _ in copies():
                for cp in starts:
                    cp.start()

        def finish():
            for _, waits in copies():
                for wait in waits:
                    wait()

        if riders and grid:
            first = functools.reduce(jnp.logical_and, [pl.program_id(d) == 0 for d in range(len(grid))])
            last = functools.reduce(jnp.logical_and, [pl.program_id(d) == grid[d] - 1 for d in range(len(grid))])
            pl.when(first)(start)
            body(*ins, *outs, *scr)
            pl.when(last)(finish)
        else:
            if riders:
                start()
            body(*ins, *outs, *scr)
            if riders:
                finish()

    params = _params(sem) if grid else pltpu.CompilerParams(vmem_limit_bytes=VMEM_LIMIT)
    res = pl.pallas_call(
        full, name=name, grid=grid, in_specs=in_specs, out_specs=out_specs, out_shape=out_shape,
        scratch_shapes=scratch_shapes, input_output_aliases=aliases, compiler_params=params,
    )(*operands)
    outs, at, rider_res = res[:n_out], n_out, []
    for r in riders:
        rider_res.append((res[at:at + len(r.inplace)], res[at + len(r.inplace):at + len(r.inplace) + len(r.lands)]))
        at += len(r.inplace) + len(r.lands)
    return list(outs), rider_res


def _gammas():
    return [1.0 - 2.0 ** (-5.0 - h) for h in range(HEADS)]


def _decay_tables():
    idx = np.arange(SUPER)
    dist = np.abs(idx[:, None] - idx[None, :]).astype(np.float64)
    visible = (idx[None, :] // CHUNK) <= (idx[:, None] // CHUNK)
    mask = np.stack([np.where(visible, g ** dist, 0.0) for g in _gammas()])
    qd = np.concatenate([np.repeat((g ** (idx + 1.0))[:, None], HEAD_DIM, 1) for g in _gammas()], 1)
    kd = np.concatenate([np.repeat((g ** (SUPER - 1.0 - idx))[:, None], HEAD_DIM, 1) for g in _gammas()], 1)
    return (jnp.asarray(mask, F32), jnp.asarray(qd, F32), jnp.asarray(kd, F32))


def _rope_tables(s):
    inv_freq = ROPE_BASE ** (-np.arange(0, HEAD_DIM, 2, dtype=np.float64) / HEAD_DIM)
    ang = np.arange(s, dtype=np.float64)[:, None] * inv_freq[None, :]
    cos, sin = np.cos(ang), np.sin(ang)
    return (jnp.asarray(np.concatenate([cos, cos], 1), F32),
            jnp.asarray(np.concatenate([-sin, sin], 1), F32))


def _rope(t, cosf, sinf):
    return t * cosf + pltpu.roll(t, HEAD_DIM // 2, 1) * sinf


def _rope_t(t, cosf, sinf):
    return t * cosf - pltpu.roll(t, HEAD_DIM // 2, 1) * sinf


def _layernorm_fwd(z):
    mu = jnp.mean(z, axis=-1, keepdims=True)
    zc = z - mu
    var = jnp.mean(zc * zc, axis=-1, keepdims=True)
    rstd = lax.rsqrt(var + LN_EPS)
    return zc * rstd, rstd


def _layernorm_bwd(dy, xhat, rstd, gain):
    dxh = dy * gain
    m1 = jnp.mean(dxh, axis=-1, keepdims=True)
    m2 = jnp.mean(dxh * xhat, axis=-1, keepdims=True)
    return rstd * (dxh - m1 - xhat * m2)


def _proj_pool(x, win4, cosf, sinf, wpool, pscale, ts, riders=(), after=()):
    s = x.shape[0]
    nt = s // ts

    def body(x_ref, w_ref, cos_ref, sin_ref, wp_ref, ps_ref,
             xb_ref, q_ref, k_ref, v_ref, g_ref, pooled_ref, cat_ref, proj_scr, pext_scr):
        i = pl.program_id(0)
        xb = x_ref[...].astype(BF16)
        xb_ref[...] = xb
        for j in range(N_SHARD):
            proj_scr[:, j * IN_SH:(j + 1) * IN_SH] = _dot(xb, w_ref[j])
        cosf_t = cos_ref[...]
        sinf_t = sin_ref[...]
        for h in range(HEADS):
            lo = h * HEAD_DIM
            q_ref[:, lo:lo + HEAD_DIM] = _rope(proj_scr[:, lo:lo + HEAD_DIM], cosf_t, sinf_t).astype(BF16)
            kk = _rope(proj_scr[:, RET_W + lo:RET_W + lo + HEAD_DIM], cosf_t, sinf_t) * K_SCALE
            k_ref[:, lo:lo + HEAD_DIM] = kk.astype(BF16)
        v_ref[...] = proj_scr[:, 2 * RET_W:3 * RET_W].astype(BF16)
        g_ref[...] = proj_scr[:, 3 * RET_W:4 * RET_W]

        @pl.when(i == 0)
        def _():
            pext_scr[0:POOL_HALO, :] = jnp.zeros((POOL_HALO, POOL_W), F32)

        pext_scr[POOL_HALO:POOL_HALO + ts, :] = proj_scr[:, 4 * RET_W:IN_W]
        pos = (i * ts + lax.broadcasted_iota(jnp.int32, (ts, 1), 0) + 1).astype(F32)
        for gi, w in enumerate(POOL_WINDOWS):
            lo = gi * HEAD_DIM
            ext = pext_scr[:, lo:lo + HEAD_DIM]
            acc = ext
            shift = 1
            while shift < w:
                acc = acc + pltpu.roll(acc, shift, 0)
                shift *= 2
            tok = ext[POOL_HALO:POOL_HALO + ts]
            pooled = acc[POOL_HALO:POOL_HALO + ts] / jnp.minimum(pos, float(w)) - tok
            pooled_b = pooled.astype(BF16)
            pooled_ref[:, lo:lo + HEAD_DIM] = pooled_b
            lin = _dot(pooled_b, wp_ref[gi])
            cat_ref[:, lo:lo + HEAD_DIM] = (lin * ps_ref[:, lo:lo + HEAD_DIM]).astype(BF16)
        pext_scr[0:POOL_HALO, :] = pext_scr[ts:ts + POOL_HALO, :]

    tile = lambda w: pl.BlockSpec((ts, w), lambda i: (i, 0))
    return _call(
        body, name="proj_pool", grid=(nt,),
        in_specs=[tile(D_MODEL), _whole(), tile(HEAD_DIM), tile(HEAD_DIM), _whole(), _whole()],
        out_specs=[tile(D_MODEL), tile(RET_W), tile(RET_W), tile(RET_W), tile(RET_W), tile(POOL_W),
                   pl.BlockSpec((ts, POOL_W), lambda i: (i, 1))],
        out_shape=[jax.ShapeDtypeStruct((s, D_MODEL), BF16), jax.ShapeDtypeStruct((s, RET_W), BF16),
                   jax.ShapeDtypeStruct((s, RET_W), BF16), jax.ShapeDtypeStruct((s, RET_W), BF16),
                   jax.ShapeDtypeStruct((s, RET_W), F32), jax.ShapeDtypeStruct((s, POOL_W), BF16),
                   jax.ShapeDtypeStruct((s, 2 * RET_W), BF16)],
        scratch_shapes=[pltpu.VMEM((ts, IN_W), F32), pltpu.VMEM((ts + POOL_HALO, POOL_W), F32)],
        sem=("arbitrary",), operands=(x, win4, cosf, sinf, wpool, pscale), riders=riders, after=after,
    )


def _retention_fwd(q, k, v, g, cat, mask, qd, kd, riders=(), after=()):
    s = q.shape[0]
    ns = s // SUPER
    cdec = [gm ** float(SUPER) for gm in _gammas()]

    def body(q_ref, k_ref, v_ref, g_ref, cat_in, mask_ref, qd_ref, kd_ref,
             ret_ref, cat_ref, st_ref, state_scr):
        del cat_in
        n = pl.program_id(0)

        @pl.when(n == 0)
        def _():
            state_scr[...] = jnp.zeros_like(state_scr)

        for h in range(HEADS):
            sl = slice(h * HEAD_DIM, (h + 1) * HEAD_DIM)
            qh, kh, vh = q_ref[:, sl], k_ref[:, sl], v_ref[:, sl]
            sc = _dot_nt(qh, kh) * mask_ref[h]
            st = state_scr[h]
            stb = st.astype(BF16)
            st_ref[0, h] = stb
            qdb = (qh.astype(F32) * qd_ref[:, sl]).astype(BF16)
            kdb = (kh.astype(F32) * kd_ref[:, sl]).astype(BF16)
            ret = _dot(sc.astype(BF16), vh) + _dot(qdb, stb)
            state_scr[h] = st * cdec[h] + _dot_tn(kdb, vh)
            ret_ref[:, sl] = ret
            r = lax.rsqrt(jnp.mean(ret * ret, axis=-1, keepdims=True) + RMS_EPS)
            gh = g_ref[:, sl]
            cat_ref[:, sl] = ((ret * r) * (gh * _sigmoid(gh))).astype(BF16)

    tile = pl.BlockSpec((SUPER, RET_W), lambda n: (n, 0))
    return _call(
        body, name="retention_fwd", grid=(ns,),
        in_specs=[tile, tile, tile, tile, HBM_SPEC, _whole(), _whole(), _whole()],
        out_specs=[tile, tile, pl.BlockSpec((1, HEADS, HEAD_DIM, HEAD_DIM), lambda n: (n, 0, 0, 0))],
        out_shape=[jax.ShapeDtypeStruct((s, RET_W), F32), jax.ShapeDtypeStruct((s, 2 * RET_W), BF16),
                   jax.ShapeDtypeStruct((ns, HEADS, HEAD_DIM, HEAD_DIM), BF16)],
        scratch_shapes=[pltpu.VMEM((HEADS, HEAD_DIM, HEAD_DIM), F32)],
        aliases={4: 1}, sem=("arbitrary",), operands=(q, k, v, g, cat, mask, qd, kd), riders=riders,
        after=after,
    )


def _outproj_ln1(x, cat, wout, g1, b1, ts, riders=(), after=()):
    s = x.shape[0]

    def body(x_ref, cat_ref, w_ref, g_ref, b_ref, xhat_ref, rstd_ref, h1b_ref):
        z = ALPHA * x_ref[...] + _dot(cat_ref[...], w_ref[...])
        xhat, rstd = _layernorm_fwd(z)
        xhat_ref[...] = xhat
        rstd_ref[...] = rstd
        h1b_ref[...] = (xhat * g_ref[...] + b_ref[...]).astype(BF16)

    tile = lambda w: pl.BlockSpec((ts, w), lambda i: (i, 0))
    return _call(
        body, name="outproj_ln1", grid=(s // ts,),
        in_specs=[tile(D_MODEL), tile(D_MODEL), _whole(), _whole(), _whole()],
        out_specs=[tile(D_MODEL), tile(1), tile(D_MODEL)],
        out_shape=[jax.ShapeDtypeStruct((s, D_MODEL), F32), jax.ShapeDtypeStruct((s, 1), F32),
                   jax.ShapeDtypeStruct((s, D_MODEL), BF16)],
        sem=("arbitrary",), operands=(x, cat, wout, g1, b1), riders=riders, after=after,
    )


def _ffn_fwd_loss(xhat1, h1b, target, wup4, wdown, cw, cb, g1, b1, g2, b2, ts):
    s = xhat1.shape[0]

    def body(xhat_ref, h1b_ref, tgt_ref, wup_ref, wdn_ref, cw_ref, cb_ref, g1_ref, b1_ref, g2_ref, b2_ref,
             ub_ref, act_ref, sd_ref, dz2_ref, dz2b_ref, loss_ref, dg2_ref, db2_ref, val_scr, gext_scr, ffn_scr):
        i = pl.program_id(0)

        @pl.when(i == 0)
        def _():
            gext_scr[0:CONV_HALO, :] = jnp.zeros((CONV_HALO, D_FF), F32)
            loss_ref[...] = jnp.zeros_like(loss_ref)
            dg2_ref[...] = jnp.zeros_like(dg2_ref)
            db2_ref[...] = jnp.zeros_like(db2_ref)

        for half in range(2):
            lo = half * UP_SH
            gext_scr[CONV_HALO:CONV_HALO + ts, lo:lo + UP_SH] = _dot(h1b_ref[...], wup_ref[2 + half])
            val_scr[:, lo:lo + UP_SH] = _dot(h1b_ref[...], wup_ref[half])
            for c0 in range(lo, lo + UP_SH, FFN_STRIP):
                cols = slice(c0, c0 + FFN_STRIP)
                ext = gext_scr[:, cols]
                gate = ext[CONV_HALO:]
                hc = cb_ref[:, cols] + ((pltpu.roll(ext, 2, 0)[CONV_HALO:] * cw_ref[0:1, cols]
                                         + pltpu.roll(ext, 1, 0)[CONV_HALO:] * cw_ref[1:2, cols])
                                        + gate * cw_ref[2:3, cols])
                val = val_scr[:, cols]
                sg = _sigmoid(hc)
                si = hc * sg
                act_ref[:, cols] = (si * val).astype(BF16)
                ub_ref[:, cols] = val.astype(BF16)
                ub_ref[:, D_FF + c0:D_FF + c0 + FFN_STRIP] = gate.astype(BF16)
                sd_ref[:, cols] = hc.astype(BF16)
            part = _dot(act_ref[:, lo:lo + UP_SH], wdn_ref[lo:lo + UP_SH, :])
            if half == 0:
                ffn_scr[...] = part
            else:
                ffn_scr[...] += part

        gext_scr[0:CONV_HALO, :] = gext_scr[ts:ts + CONV_HALO, :]

        loss_acc = jnp.zeros((1, 1), F32)
        dg2_acc = jnp.zeros((1, D_MODEL), F32)
        db2_acc = jnp.zeros((1, D_MODEL), F32)
        for r0 in range(0, ts, LN_ROWS):
            rows = slice(r0, r0 + LN_ROWS)
            h1 = xhat_ref[rows, :] * g1_ref[...] + b1_ref[...]
            xhat2, rstd2 = _layernorm_fwd(ALPHA * h1 + ffn_scr[rows, :])
            diff = (xhat2 * g2_ref[...] + b2_ref[...]) - tgt_ref[rows, :]
            row = jnp.mean(diff * diff, axis=-1, keepdims=True)
            loss_acc = loss_acc + 0.5 * jnp.sum(row, axis=0, keepdims=True)
            dy = diff * (1.0 / D_MODEL)
            dg2_acc = dg2_acc + jnp.sum(dy * xhat2, axis=0, keepdims=True)
            db2_acc = db2_acc + jnp.sum(dy, axis=0, keepdims=True)
            dz2 = _layernorm_bwd(dy, xhat2, rstd2, g2_ref[...])
            dz2_ref[rows, :] = dz2
            dz2b_ref[rows, :] = dz2.astype(BF16)
        loss_ref[...] += loss_acc
        dg2_ref[...] += dg2_acc
        db2_ref[...] += db2_acc

    tile = lambda w: pl.BlockSpec((ts, w), lambda i: (i, 0))
    acc = lambda w: pl.BlockSpec((1, w), lambda i: (0, 0))
    return pl.pallas_call(
        body, name="ffn_fwd_loss", grid=(s // ts,),
        in_specs=[tile(D_MODEL), tile(D_MODEL), tile(D_MODEL)] + [_whole()] * 8,
        out_specs=[tile(2 * D_FF), tile(D_FF), tile(D_FF), tile(D_MODEL), tile(D_MODEL),
                   acc(1), acc(D_MODEL), acc(D_MODEL)],
        out_shape=[jax.ShapeDtypeStruct((s, 2 * D_FF), BF16), jax.ShapeDtypeStruct((s, D_FF), BF16),
                   jax.ShapeDtypeStruct((s, D_FF), BF16), jax.ShapeDtypeStruct((s, D_MODEL), F32),
                   jax.ShapeDtypeStruct((s, D_MODEL), BF16),
                   jax.ShapeDtypeStruct((1, 1), F32), jax.ShapeDtypeStruct((1, D_MODEL), F32),
                   jax.ShapeDtypeStruct((1, D_MODEL), F32)],
        scratch_shapes=[pltpu.VMEM((ts, D_FF), F32), pltpu.VMEM((ts + CONV_HALO, D_FF), F32),
                        pltpu.VMEM((ts, D_MODEL), F32)],
        compiler_params=_params(("arbitrary",)),
    )(xhat1, h1b, target, wup4, wdown, cw, cb, g1, b1, g2, b2)


def _ffn_bwd(dz2, dz2b, ub, sd, xhat1, rstd1, wup4, wdown, cw, g1, ts):
    s = dz2.shape[0]
    nt = s // ts

    def body(dz2_ref, dz2b_ref, ub_ref, sd_ref, xhat_ref, rstd_ref, wup_ref, wdn_ref, cw_ref, g1_ref,
             dub_ref, dz1_ref, dz1b_ref, dg1_ref, db1_ref, dcw_ref, dcb_ref, dext_scr, da_scr):
        i = pl.program_id(0)

        @pl.when(i == 0)
        def _():
            dext_scr[ts:ts + CONV_HALO, :] = jnp.zeros((CONV_HALO, D_FF), F32)
            dg1_ref[...] = jnp.zeros_like(dg1_ref)
            db1_ref[...] = jnp.zeros_like(db1_ref)
            dcw_ref[...] = jnp.zeros_like(dcw_ref)
            dcb_ref[...] = jnp.zeros_like(dcb_ref)

        da_scr[...] = _dot_nt(dz2b_ref[...], wdn_ref[...])
        n_ext = ts + CONV_HALO
        for c0 in range(0, D_FF, FFN_STRIP):
            cols = slice(c0, c0 + FFN_STRIP)
            gcols = slice(D_FF + c0, D_FF + c0 + FFN_STRIP)
            val = ub_ref[:, cols].astype(F32)
            gate = ub_ref[:, gcols].astype(F32)
            da = da_scr[:, cols]
            hc = sd_ref[:, cols].astype(F32)
            sg = _sigmoid(hc)
            dhc = da * val * (sg * (1.0 + hc * (1.0 - sg)))
            dext_scr[0:ts, cols] = dhc
            dext = dext_scr[:, cols]
            dhc1 = pltpu.roll(dext, n_ext - 1, 0)[0:ts]
            dhc2 = pltpu.roll(dext, n_ext - 2, 0)[0:ts]
            dcb_ref[:, cols] += jnp.sum(dhc, axis=0, keepdims=True)
            dcw_ref[0:1, cols] += jnp.sum(dhc2 * gate, axis=0, keepdims=True)
            dcw_ref[1:2, cols] += jnp.sum(dhc1 * gate, axis=0, keepdims=True)
            dcw_ref[2:3, cols] += jnp.sum(dhc * gate, axis=0, keepdims=True)
            dgate = dhc * cw_ref[2:3, cols] + dhc1 * cw_ref[1:2, cols] + dhc2 * cw_ref[0:1, cols]
            dub_ref[:, cols] = (da * (hc * sg)).astype(BF16)
            dub_ref[:, gcols] = dgate.astype(BF16)
        dext_scr[ts:n_ext, :] = dext_scr[0:CONV_HALO, :]
        dh1 = ALPHA * dz2_ref[...]
        for j in range(N_SHARD):
            dh1 = dh1 + _dot_nt(dub_ref[:, j * UP_SH:(j + 1) * UP_SH], wup_ref[j])
        xhat = xhat_ref[...]
        dg1_ref[...] += jnp.sum(dh1 * xhat, axis=0, keepdims=True)
        db1_ref[...] += jnp.sum(dh1, axis=0, keepdims=True)
        dz1 = _layernorm_bwd(dh1, xhat, rstd_ref[...], g1_ref[...])
        dz1_ref[...] = dz1
        dz1b_ref[...] = dz1.astype(BF16)

    tile = lambda w: pl.BlockSpec((ts, w), lambda i: (nt - 1 - i, 0))
    acc = lambda rws, w: pl.BlockSpec((rws, w), lambda i: (0, 0))
    return pl.pallas_call(
        body, name="ffn_bwd", grid=(nt,),
        in_specs=[tile(D_MODEL), tile(D_MODEL), tile(2 * D_FF), tile(D_FF), tile(D_MODEL), tile(1)]
        + [_whole()] * 4,
        out_specs=[tile(2 * D_FF), tile(D_MODEL), tile(D_MODEL), acc(1, D_MODEL), acc(1, D_MODEL),
                   acc(3, D_FF), acc(1, D_FF)],
        out_shape=[jax.ShapeDtypeStruct((s, 2 * D_FF), BF16),
                   jax.ShapeDtypeStruct((s, D_MODEL), F32), jax.ShapeDtypeStruct((s, D_MODEL), BF16),
                   jax.ShapeDtypeStruct((1, D_MODEL), F32),
                   jax.ShapeDtypeStruct((1, D_MODEL), F32), jax.ShapeDtypeStruct((3, D_FF), F32),
                   jax.ShapeDtypeStruct((1, D_FF), F32)],
        scratch_shapes=[pltpu.VMEM((ts + CONV_HALO, D_FF), F32), pltpu.VMEM((ts, D_FF), F32)],
        compiler_params=_params(("arbitrary",)),
    )(dz2, dz2b, ub, sd, xhat1, rstd1, wup4, wdown, cw, g1)


def _mix_bwd(dz1, pooled, ret, g, wout, wpool, pscale, ts, riders=(), after=()):
    s = dz1.shape[0]
    nt = s // ts

    def body(dz1_ref, pooled_ref, ret_ref, g_ref, wout_ref, wp_ref, ps_ref,
             dret_ref, dgp_ref, dwp_ref, dps_ref, eext_scr):
        i = pl.program_id(0)
        r = nt - 1 - i

        @pl.when(i == 0)
        def _():
            eext_scr[ts:ts + POOL_HALO, :] = jnp.zeros((POOL_HALO, POOL_W), F32)
            dwp_ref[...] = jnp.zeros_like(dwp_ref)
            dps_ref[...] = jnp.zeros_like(dps_ref)

        dzb = dz1_ref[...].astype(BF16)
        dcat_r = _dot_nt(dzb, wout_ref[0:RET_W, :])
        dcat_p = _dot_nt(dzb, wout_ref[RET_W:2 * RET_W, :])
        pos = (r * ts + lax.broadcasted_iota(jnp.int32, (ts, 1), 0) + 1).astype(F32)
        dpooled = []
        for gi, w in enumerate(POOL_WINDOWS):
            sl = slice(gi * HEAD_DIM, (gi + 1) * HEAD_DIM)
            pb = pooled_ref[:, sl]
            dy = dcat_p[:, sl]
            dps_ref[:, sl] += jnp.sum(dy * _dot(pb, wp_ref[gi]), axis=0, keepdims=True)
            dlin = (dy * ps_ref[:, sl]).astype(BF16)
            dwp_ref[gi] += _dot_tn(pb, dlin)
            dpg = _dot_nt(dlin, wp_ref[gi])
            dpooled.append(dpg)
            eext_scr[0:ts, sl] = dpg / jnp.minimum(pos, float(w))
        for gi, w in enumerate(POOL_WINDOWS):
            sl = slice(gi * HEAD_DIM, (gi + 1) * HEAD_DIM)
            acc = eext_scr[:, sl]
            shift = 1
            while shift < w:
                acc = acc + pltpu.roll(acc, ts + POOL_HALO - shift, 0)
                shift *= 2
            dgp_ref[:, RET_W + gi * HEAD_DIM:RET_W + (gi + 1) * HEAD_DIM] = (acc[0:ts] - dpooled[gi]).astype(BF16)
        eext_scr[ts:ts + POOL_HALO, :] = eext_scr[0:POOL_HALO, :]
        for h in range(HEADS):
            sl = slice(h * HEAD_DIM, (h + 1) * HEAD_DIM)
            rt = ret_ref[:, sl]
            rr = lax.rsqrt(jnp.mean(rt * rt, axis=-1, keepdims=True) + RMS_EPS)
            rn = rt * rr
            gh = g_ref[:, sl]
            sg = _sigmoid(gh)
            dy = dcat_r[:, sl]
            dgp_ref[:, sl] = (dy * rn * (sg * (1.0 + gh * (1.0 - sg)))).astype(BF16)
            drn = dy * (gh * sg)
            dret_ref[:, sl] = (rr * (drn - rn * jnp.mean(drn * rn, axis=-1, keepdims=True))).astype(BF16)

    tile = lambda w: pl.BlockSpec((ts, w), lambda i: (nt - 1 - i, 0))
    return _call(
        body, name="mix_bwd", grid=(nt,),
        in_specs=[tile(D_MODEL), tile(POOL_W), tile(RET_W), tile(RET_W), _whole(), _whole(), _whole()],
        out_specs=[tile(RET_W), tile(2 * RET_W),
                   pl.BlockSpec((len(POOL_WINDOWS), HEAD_DIM, HEAD_DIM), lambda i: (0, 0, 0)),
                   pl.BlockSpec((1, POOL_W), lambda i: (0, 0))],
        out_shape=[jax.ShapeDtypeStruct((s, RET_W), BF16), jax.ShapeDtypeStruct((s, 2 * RET_W), BF16),
                   jax.ShapeDtypeStruct((len(POOL_WINDOWS), HEAD_DIM, HEAD_DIM), F32),
                   jax.ShapeDtypeStruct((1, POOL_W), F32)],
        scratch_shapes=[pltpu.VMEM((ts + POOL_HALO, POOL_W), F32)],
        sem=("arbitrary",), operands=(dz1, pooled, ret, g, wout, wpool, pscale), riders=riders,
        after=after,
    )


def _retention_bwd(q, k, v, dret, dgp, states, mask, qd, kd, cosf, sinf, riders=(), after=()):
    s = q.shape[0]
    ns = s // SUPER
    cdec = [gm ** float(SUPER) for gm in _gammas()]

    def body(q_ref, k_ref, v_ref, do_ref, dgp_ref, st_ref, mask_ref, qd_ref, kd_ref, cos_ref, sin_ref,
             dproj_ref, dstate_scr):
        i = pl.program_id(0)

        @pl.when(i == 0)
        def _():
            dstate_scr[...] = jnp.zeros_like(dstate_scr)

        cosf_t = cos_ref[...]
        sinf_t = sin_ref[...]
        for h in range(HEADS):
            sl = slice(h * HEAD_DIM, (h + 1) * HEAD_DIM)
            qh, kh, vh, doh = q_ref[:, sl], k_ref[:, sl], v_ref[:, sl], do_ref[:, sl]
            m = mask_ref[h]
            scb = (_dot_nt(qh, kh) * m).astype(BF16)
            dscb = (_dot_nt(doh, vh) * m).astype(BF16)
            stb = st_ref[0, h]
            dst = dstate_scr[h]
            dstb = dst.astype(BF16)
            qdb = (qh.astype(F32) * qd_ref[:, sl]).astype(BF16)
            kdb = (kh.astype(F32) * kd_ref[:, sl]).astype(BF16)
            dq = _dot(dscb, kh) + _dot_nt(doh, stb) * qd_ref[:, sl]
            dk = _dot_tn(dscb, qh) + _dot_nt(vh, dstb) * kd_ref[:, sl]
            dv = _dot_tn(scb, doh) + _dot(kdb, dstb)
            dstate_scr[h] = dst * cdec[h] + _dot_tn(qdb, doh)
            lo = h * HEAD_DIM
            dproj_ref[:, lo:lo + HEAD_DIM] = _rope_t(dq, cosf_t, sinf_t).astype(BF16)
            dproj_ref[:, RET_W + lo:RET_W + lo + HEAD_DIM] = _rope_t(dk * K_SCALE, cosf_t, sinf_t).astype(BF16)
            dproj_ref[:, 2 * RET_W + lo:2 * RET_W + lo + HEAD_DIM] = dv.astype(BF16)
        dproj_ref[:, 3 * RET_W:IN_W] = dgp_ref[...]

    tile = lambda w: pl.BlockSpec((SUPER, w), lambda i: (ns - 1 - i, 0))
    return _call(
        body, name="retention_bwd", grid=(ns,),
        in_specs=[tile(RET_W), tile(RET_W), tile(RET_W), tile(RET_W), tile(2 * RET_W),
                  pl.BlockSpec((1, HEADS, HEAD_DIM, HEAD_DIM), lambda i: (ns - 1 - i, 0, 0, 0)),
                  _whole(), _whole(), _whole(), tile(HEAD_DIM), tile(HEAD_DIM)],
        out_specs=[tile(IN_W)],
        out_shape=[jax.ShapeDtypeStruct((s, IN_W), BF16)],
        scratch_shapes=[pltpu.VMEM((HEADS, HEAD_DIM, HEAD_DIM), F32)],
        sem=("arbitrary",), operands=(q, k, v, dret, dgp, states, mask, qd, kd, cosf, sinf), riders=riders,
        after=after,
    )


def _dx(dz1, dproj, win4, ts, riders=(), after=()):
    s = dz1.shape[0]

    def body(dz1_ref, dp_ref, w_ref, dx_ref):
        acc = ALPHA * dz1_ref[...]
        for j in range(N_SHARD):
            acc = acc + _dot_nt(dp_ref[:, j * IN_SH:(j + 1) * IN_SH], w_ref[j])
        dx_ref[...] = acc

    tile = lambda w: pl.BlockSpec((ts, w), lambda i: (i, 0))
    return _call(
        body, name="dx", grid=(s // ts,),
        in_specs=[tile(D_MODEL), tile(IN_W), _whole()],
        out_specs=[tile(D_MODEL)],
        out_shape=[jax.ShapeDtypeStruct((s, D_MODEL), F32)],
        sem=("arbitrary",), operands=(dz1, dproj, win4), riders=riders, after=after,
    )


def _wgrad(a, b, tm, tn, name, stacked, m_outer, riders=(), after=()):
    s, m = a.shape
    n = b.shape[1]

    def body(a_ref, b_ref, o32_ref, o16_ref):
        res = _dot_tn(a_ref[...], b_ref[...])
        o32_ref[...] = res.reshape(o32_ref.shape)
        o16_ref[...] = res.astype(BF16).reshape(o16_ref.shape)

    if m_outer:
        grid, blocks = (m // tm, n // tn), (lambda g0, g1: (g0, g1))
    else:
        grid, blocks = (n // tn, m // tm), (lambda g0, g1: (g1, g0))
    if stacked:
        shape = (n // tn, m, tn)
        ospec = pl.BlockSpec((1, tm, tn), lambda g0, g1: (blocks(g0, g1)[1], blocks(g0, g1)[0], 0))
    else:
        shape = (m, n)
        ospec = pl.BlockSpec((tm, tn), lambda g0, g1: blocks(g0, g1))
    return _call(
        body, name=name, grid=grid,
        in_specs=[pl.BlockSpec((s, tm), lambda g0, g1: (0, blocks(g0, g1)[0])),
                  pl.BlockSpec((s, tn), lambda g0, g1: (0, blocks(g0, g1)[1]))],
        out_specs=[ospec, ospec],
        out_shape=[jax.ShapeDtypeStruct(shape, F32), jax.ShapeDtypeStruct(shape, BF16)],
        sem=("arbitrary", "arbitrary"), operands=(a, b), riders=riders, after=after,
    )


class _NoComm:
    def __init__(self, win4, wout, wup4, wdown):
        self.weights = dict(w_in=win4, w_out=wout, w_up=wup4, w_down=wdown)
        self.grads = {}

    def weight(self, name):
        return self.weights[name]

    def riders(self, call):
        return ()

    def after(self, call):
        return ()

    def landed(self, call, results, outs):
        pass

    def small_gradients(self, loss, small):
        pass

    def gradient(self, name, g32, g16):
        self.grads[name] = (g32, g16)


def _local_step(x, target, cw, cb, wpool, pscale, g1, b1, g2, b2, comm):
    s = x.shape[0]
    ts_a = min(512, s)
    ts_f = min(256, s)
    mask, qd, kd = _decay_tables()
    cosf, sinf = _rope_tables(s)
    wpool_b = wpool.astype(BF16)

    def run(call, fn, *args):
        outs, res = fn(*args, riders=comm.riders(call), after=comm.after(call))
        comm.landed(call, res, outs)
        return outs

    xb, q, k, v, g, pooled, cat = run("proj_pool", _proj_pool, x, comm.weight("w_in"), cosf, sinf, wpool_b,
                                      pscale, ts_a)
    ret, cat, states = run("retention_fwd", _retention_fwd, q, k, v, g, cat, mask, qd, kd)
    wout = comm.weight("w_out")
    xhat1, rstd1, h1b = run("outproj_ln1", _outproj_ln1, x, cat, wout, g1, b1, ts_a)
    wup4, wdown = comm.weight("w_up"), comm.weight("w_down")
    ub, act, sd, dz2, dz2b, loss, dg2, db2 = _ffn_fwd_loss(xhat1, h1b, target, wup4, wdown, cw, cb, g1, b1, g2, b2,
                                                           ts_f)

    dub, dz1, dz1b, dg1, db1, dcw, dcb = _ffn_bwd(dz2, dz2b, ub, sd, xhat1, rstd1, wup4, wdown, cw, g1, ts_f)
    half = D_MODEL // 2
    comm.gradient("w_up", *run("wgrad_up", _wgrad, h1b, dub, half, UP_SH, "wgrad_up", True, False))
    comm.gradient("w_out", *run("wgrad_out", _wgrad, cat, dz1b, D_MODEL, half, "wgrad_out", False, True))
    comm.gradient("w_down", *run("wgrad_down", _wgrad, act, dz2b, D_FF // 2, half, "wgrad_down", False, True))
    dret, dgp, dwp, dps = run("mix_bwd", _mix_bwd, dz1b, pooled, ret, g, wout, wpool_b, pscale, ts_a)
    small = dict(w_pool=dwp, pool_scale=dps, ln1_g=dg1, ln1_b=db1, conv_w=dcw, conv_b=dcb,
                 ln2_g=dg2, ln2_b=db2)
    comm.small_gradients(loss, small)
    dproj, = run("retention_bwd", _retention_bwd, q, k, v, dret, dgp, states, mask, qd, kd, cosf, sinf)
    comm.gradient("w_in", *run("wgrad_in", _wgrad, xb, dproj, D_MODEL, IN_SH, "wgrad_in", True, True))
    (grad_x,), _ = _dx(dz1, dproj, comm.weight("w_in"), ts_a, after=comm.after("dx"))
    return loss, grad_x, small


CAST_ROWS = 64
SHARD_SHAPES = ((D_MODEL, IN_SH), (OUT_SH, D_MODEL), (D_MODEL, UP_SH), (DOWN_SH, D_MODEL))
N_BIG = len(SHARD_SHAPES)
CW_SHARD = (3, DOWN_SH)


def _mesh_pos():
    return lax.axis_index("x"), lax.axis_index("y"), lax.axis_index("c")


def _other_chips(x, y):
    return [(1 - x, y), (x, 1 - y), (1 - x, 1 - y)]


def _half_rows(w, which):
    hr = SHARD_SHAPES[w][0] // 2
    return pl.ds(pl.multiple_of(which * hr, 16), hr)


def _gather_weights(shards, cw_shard, full):
    def body(*refs):
        in_refs = refs[:N_BIG]
        cw_ref = refs[N_BIG]
        out_refs = refs[N_BIG + 1:2 * N_BIG + 1]
        cwo_ref = refs[2 * N_BIG + 1]
        stage = refs[2 * N_BIG + 2:3 * N_BIG + 2]
        raw = refs[3 * N_BIG + 2:4 * N_BIG + 2 - len(full)]
        send_sems, recv_sems, fsend_sems, frecv_sems, cw_send, cw_recv, local_sems, load_sems = \
            refs[4 * N_BIG + 2 - len(full):]
        x, y, c = _mesh_pos()
        j0 = 2 * x + y
        chips = _other_chips(x, y)

        fetched = [w for w in range(N_BIG) if w not in full]
        f32 = {w: in_refs[w] for w in full}
        loads = []
        for n, w in enumerate(fetched):
            f32[w] = raw[n]
            loads.append(pltpu.make_async_copy(in_refs[w], raw[n], load_sems.at[n]))
            loads[-1].start()

        def cast_to_stage(w):
            def cast(i, carry):
                rows = pl.ds(pl.multiple_of(i * CAST_ROWS, CAST_ROWS), CAST_ROWS)
                stage[w][rows, :] = f32[w][rows, :].astype(BF16)
                return carry
            lax.fori_loop(0, SHARD_SHAPES[w][0] // CAST_ROWS, cast, 0)

        for w in full:
            cast_to_stage(w)

        jx, jy, jd = 2 * (1 - x) + y, 2 * x + (1 - y), 2 * (1 - x) + (1 - y)
        neighbours = [((1 - x, y, c), jx), ((x, 1 - y, c), jy)]
        passed = jnp.where(c == 0, jx, jy)
        pass_to = (jnp.where(c == 0, x, 1 - x), jnp.where(c == 0, 1 - y, y), c)

        def nbr(w, k, block):
            return pltpu.make_async_remote_copy(
                src_ref=stage[w].at[_half_rows(w, c), :], dst_ref=out_refs[w].at[block, _half_rows(w, c), :],
                send_sem=send_sems.at[w, k], recv_sem=recv_sems.at[w, k],
                device_id=neighbours[k][0], device_id_type=MESH)

        def relay(w, block):
            return pltpu.make_async_remote_copy(
                src_ref=out_refs[w].at[passed, _half_rows(w, c), :],
                dst_ref=out_refs[w].at[block, _half_rows(w, c), :],
                send_sem=send_sems.at[w, 2], recv_sem=recv_sems.at[w, 2],
                device_id=pass_to, device_id_type=MESH)

        def d2d(w, k, block, half):
            return pltpu.make_async_remote_copy(
                src_ref=out_refs[w].at[block, _half_rows(w, half), :],
                dst_ref=out_refs[w].at[block, _half_rows(w, half), :],
                send_sem=fsend_sems.at[w, k], recv_sem=frecv_sems.at[w, k],
                device_id=(x, y, 1 - c), device_id_type=MESH)

        def conv(k, block):
            chip = chips[k]
            return pltpu.make_async_remote_copy(
                src_ref=cw_ref, dst_ref=cwo_ref.at[block], send_sem=cw_send.at[k], recv_sem=cw_recv.at[k],
                device_id=(chip[0], chip[1], c), device_id_type=MESH)

        sent = [nbr(w, k, j0) for w in full for k in range(2)] + [conv(k, j0) for k in range(3)]
        for cp in sent:
            cp.start()
        for n, w in enumerate(fetched):
            loads[n].wait()
            cast_to_stage(w)
        local = [pltpu.make_async_copy(stage[w], out_refs[w].at[j0], local_sems.at[w]) for w in range(N_BIG)]
        local.append(pltpu.make_async_copy(cw_ref, cwo_ref.at[j0], local_sems.at[N_BIG]))
        for cp in local:
            cp.start()
        for w in full:
            for k, (_, block) in enumerate(neighbours):
                nbr(w, k, block).wait_recv()
            later = [relay(w, passed)] + [d2d(w, k, block, c) for k, (_, block) in enumerate(neighbours)]
            for cp in later:
                cp.start()
            sent += later
        for w in full:
            relay(w, jd).wait_recv()
            fw = d2d(w, 2, jd, c)
            fw.start()
            sent.append(fw)
        for w in full:
            for k, block in enumerate([jx, jy, jd]):
                d2d(w, k, block, 1 - c).wait_recv()
        for k, chip in enumerate(chips):
            conv(k, 2 * chip[0] + chip[1]).wait_recv()
        for cp in sent:
            cp.wait_send()
        for cp in local:
            cp.wait()

    out_shape = [jax.ShapeDtypeStruct((N_SHARD,) + shp, BF16) for shp in SHARD_SHAPES]
    out_shape.append(jax.ShapeDtypeStruct((N_SHARD,) + CW_SHARD, F32))
    return pl.pallas_call(
        body, name="gather_weights",
        in_specs=[_whole() if w in full else HBM_SPEC for w in range(N_BIG)] + [_whole()],
        out_specs=[HBM_SPEC] * (N_BIG + 1),
        out_shape=out_shape,
        scratch_shapes=[pltpu.VMEM(shp, BF16) for shp in SHARD_SHAPES]
        + [pltpu.VMEM(shp, F32) for w, shp in enumerate(SHARD_SHAPES) if w not in full] + [
            pltpu.SemaphoreType.DMA((N_BIG, 3)), pltpu.SemaphoreType.DMA((N_BIG, 3)),
            pltpu.SemaphoreType.DMA((N_BIG, 3)), pltpu.SemaphoreType.DMA((N_BIG, 3)),
            pltpu.SemaphoreType.DMA((3,)), pltpu.SemaphoreType.DMA((3,)),
            pltpu.SemaphoreType.DMA((N_BIG + 1,)), pltpu.SemaphoreType.DMA((N_BIG - len(full),))],
        compiler_params=pltpu.CompilerParams(vmem_limit_bytes=VMEM_LIMIT),
    )(*shards, cw_shard)


def _gather_rider(arrays, ops):
    ws = sorted(arrays)

    def make(inplace, srcs, lands, send_sems, recv_sems):
        del srcs, lands
        x, y, c = _mesh_pos()
        j0, jx, jy, jd = 2 * x + y, 2 * (1 - x) + y, 2 * x + (1 - y), 2 * (1 - x) + (1 - y)
        x_nbr, y_nbr, sibling = (1 - x, y, c), (x, 1 - y, c), (x, y, 1 - c)
        starts, waits = [], []
        for n, (kind, w, (r0, nr)) in enumerate(ops):
            ref = inplace[ws.index(w)]
            hr = SHARD_SHAPES[w][0] // 2
            rows = lambda core: pl.ds(pl.multiple_of(core * hr + r0, 16), nr)
            mine, theirs = rows(c), rows(1 - c)
            if kind == "ici":
                moves = [(ref.at[j0, mine, :], x_nbr, ref.at[jx, mine, :]),
                         (ref.at[j0, mine, :], y_nbr, ref.at[jy, mine, :]),
                         (ref.at[j0, mine, :], (1 - x, 1 - y, c), ref.at[jd, mine, :])]
            elif kind == "nbr":
                moves = [(ref.at[j0, mine, :], x_nbr, ref.at[jx, mine, :]),
                         (ref.at[j0, mine, :], y_nbr, ref.at[jy, mine, :])]
            elif kind == "relay":
                passed = jnp.where(c == 0, jx, jy)
                to = (jnp.where(c == 0, x, 1 - x), jnp.where(c == 0, 1 - y, y), c)
                moves = [(ref.at[passed, mine, :], to, ref.at[jd, mine, :])]
            else:
                blocks = dict(d2d=[jx, jy, jd], d2d_nbr=[jx, jy], d2d_diag=[jd])[kind]
                moves = [(ref.at[b, mine, :], sibling, ref.at[b, theirs, :]) for b in blocks]
            for k, (src, to, landing) in enumerate(moves):
                sems = dict(send_sem=send_sems.at[3 * n + k], recv_sem=recv_sems.at[3 * n + k],
                            device_id=to, device_id_type=MESH)
                send = pltpu.make_async_remote_copy(src_ref=src, dst_ref=src, **sems)
                arrival = pltpu.make_async_remote_copy(src_ref=src, dst_ref=landing, **sems)
                starts.append(send)
                waits += [arrival.wait_recv, send.wait_send]
        return starts, waits

    return _Rider([arrays[w] for w in ws], [], [], 3 * len(ops), make)


def _whole_half(w):
    return (0, SHARD_SHAPES[w][0] // 2)


def _pair_rider(ws, g16s):
    def make(inplace, srcs, lands, send_sems, recv_sems):
        del inplace
        x, y, c = _mesh_pos()
        copies = [pltpu.make_async_remote_copy(
            src_ref=srcs[i].at[:, _half_rows(w, 1 - c), :], dst_ref=lands[i],
            send_sem=send_sems.at[i], recv_sem=recv_sems.at[i], device_id=(x, y, 1 - c), device_id_type=MESH)
            for i, w in enumerate(ws)]
        return copies, [cp.wait for cp in copies]

    lands = [jax.ShapeDtypeStruct((N_SHARD, SHARD_SHAPES[w][0] // 2, SHARD_SHAPES[w][1]), BF16) for w in ws]
    return _Rider([], g16s, lands, len(ws), make)


def _chip_rider(ws, p16s):
    def make(inplace, srcs, lands, send_sems, recv_sems):
        del inplace
        x, y, c = _mesh_pos()
        copies = []
        for i in range(len(ws)):
            for k, chip in enumerate(_other_chips(x, y)):
                copies.append(pltpu.make_async_remote_copy(
                    src_ref=srcs[i].at[2 * chip[0] + chip[1]], dst_ref=lands[i].at[k],
                    send_sem=send_sems.at[3 * i + k], recv_sem=recv_sems.at[3 * i + k],
                    device_id=(chip[0], chip[1], c), device_id_type=MESH))
        return copies, [cp.wait for cp in copies]

    lands = [jax.ShapeDtypeStruct((3, SHARD_SHAPES[w][0] // 2, SHARD_SHAPES[w][1]), BF16) for w in ws]
    return _Rider([], p16s, lands, 3 * len(ws), make)


def _final_rider(halves):
    def make(inplace, srcs, lands, send_sems, recv_sems):
        del inplace
        x, y, c = _mesh_pos()
        copies = [pltpu.make_async_remote_copy(
            src_ref=srcs[i], dst_ref=lands[i], send_sem=send_sems.at[i], recv_sem=recv_sems.at[i],
            device_id=(x, y, 1 - c), device_id_type=MESH) for i in range(len(halves))]
        return copies, [cp.wait for cp in copies]

    return _Rider([], halves, [jax.ShapeDtypeStruct(h.shape, h.dtype) for h in halves], len(halves), make)


def _comm_only(name, riders):
    _, res = _call(lambda: None, name=name, grid=(), in_specs=[], out_specs=[], out_shape=[], operands=(),
                   riders=riders)
    return res


class _SemList:
    def __init__(self, refs):
        self.at = list(refs)


def _merged_rider(riders):
    srcs = [a for r in riders for a in r.srcs]
    lands = [a for r in riders for a in r.lands]

    def make(inplace, src_refs, land_refs, send_sems, recv_sems):
        starts, waits = [], []
        s0 = l0 = c0 = 0
        for r in riders:
            part = r.make(inplace, src_refs[s0:s0 + len(r.srcs)], land_refs[l0:l0 + len(r.lands)],
                          _SemList(send_sems.at[c0:c0 + r.n_copies]), _SemList(recv_sems.at[c0:c0 + r.n_copies]))
            starts += part[0]
            waits += part[1]
            s0, l0, c0 = s0 + len(r.srcs), l0 + len(r.lands), c0 + r.n_copies
        return starts, waits

    return _Rider([], srcs, lands, sum(r.n_copies for r in riders), make)


def _split_start(name, rider):
    assert not rider.inplace
    ns, nl, n = len(rider.srcs), len(rider.lands), rider.n_copies

    def body(*refs):
        srcs, lands = refs[:ns], refs[ns:ns + nl]
        sems = refs[ns + nl:ns + nl + 2 * n]
        token = refs[-1]
        starts, _ = rider.make([], srcs, lands, _SemList(sems[:n]), _SemList(sems[n:]))
        for cp in starts:
            cp.start()
        token[...] = jnp.zeros_like(token)

    buffers = [pltpu.with_memory_space_constraint(a, pltpu.HBM) for a in rider.srcs]
    buffers += [pltpu.with_memory_space_constraint(lax.empty(s.shape, s.dtype), pltpu.HBM) for s in rider.lands]
    hbm = pl.BlockSpec(memory_space=pltpu.HBM)
    sem = pl.BlockSpec(memory_space=pltpu.SEMAPHORE)
    outs = pl.pallas_call(
        body, name=name,
        out_shape=tuple([pltpu.SemaphoreType.DMA(())] * (2 * n) + [pltpu.HBM(b.shape, b.dtype) for b in buffers]
                        + [jax.ShapeDtypeStruct((8, 128), F32)]),
        in_specs=[hbm] * (ns + nl),
        out_specs=tuple([sem] * (2 * n) + [hbm] * (ns + nl) + [_whole()]),
        input_output_aliases={i: 2 * n + i for i in range(ns + nl)},
        compiler_params=pltpu.CompilerParams(has_side_effects=pltpu.SideEffectType.DATAFLOW_SIDE_EFFECTING),
    )(*buffers)
    return (rider, outs[:2 * n], outs[2 * n:2 * n + ns + nl]), outs[-1]


def _split_parts(state, riders):
    merged, sems, buffers = state
    n, ns = merged.n_copies, len(merged.srcs)
    parts, s0, l0, c0 = [], 0, 0, 0
    for r in riders:
        parts.append((r, list(sems[c0:c0 + r.n_copies]) + list(sems[n + c0:n + c0 + r.n_copies]),
                      list(buffers[s0:s0 + len(r.srcs)]) + list(buffers[ns + l0:ns + l0 + len(r.lands)])))
        s0, l0, c0 = s0 + len(r.srcs), l0 + len(r.lands), c0 + r.n_copies
    return parts


def _split_wait(name, state, after):
    rider, sems, buffers = state
    ns, nl, n = len(rider.srcs), len(rider.lands), rider.n_copies

    def body(*refs):
        srcs, lands = refs[:ns], refs[ns:ns + nl]
        sem_refs = refs[ns + nl:ns + nl + 2 * n]
        _, waits = rider.make([], srcs, lands, _SemList(sem_refs[:n]), _SemList(sem_refs[n:]))
        for wait in waits:
            wait()

    hbm = pl.BlockSpec(memory_space=pltpu.HBM)
    sem = pl.BlockSpec(memory_space=pltpu.SEMAPHORE)
    outs = pl.pallas_call(
        body, name=name,
        out_shape=tuple(pltpu.HBM(b.shape, b.dtype) for b in buffers),
        in_specs=[hbm] * (ns + nl) + [sem] * (2 * n) + [HBM_SPEC],
        out_specs=tuple([hbm] * (ns + nl)),
        input_output_aliases={i: i for i in range(ns + nl)},
        compiler_params=pltpu.CompilerParams(has_side_effects=pltpu.SideEffectType.DATAFLOW_SIDE_EFFECTING),
    )(*buffers, *sems, after)
    return list(outs[:ns]), list(outs[ns:])


def _pair_sum(pos, ws, g32s, recvs):
    n = len(ws)

    def body(pos_ref, *refs):
        g_refs, r_refs = refs[:n], refs[n:2 * n]
        p32_refs, p16_refs = refs[2 * n:3 * n], refs[3 * n:]
        for i in range(n):
            tot = g_refs[i][...] + r_refs[i][...].astype(F32)
            p16_refs[i][...] = tot.astype(BF16)

            @pl.when(pl.program_id(0) == pos_ref[1])
            def _(i=i, tot=tot):
                p32_refs[i][...] = tot

    halves = [(SHARD_SHAPES[w][0] // 2, SHARD_SHAPES[w][1]) for w in ws]
    own = [pl.BlockSpec((None, None) + h, lambda j, pos_ref: (j, pos_ref[0], 0, 0)) for h in halves]
    blk = [pl.BlockSpec((None,) + h, lambda j, pos_ref: (j, 0, 0)) for h in halves]
    mine = [pl.BlockSpec(h, lambda j, pos_ref: (0, 0)) for h in halves]
    g4 = [g.reshape((N_SHARD, 2) + h) for g, h in zip(g32s, halves)]
    outs = pl.pallas_call(
        body, name="pair_sum_" + "_".join(str(w) for w in ws),
        grid_spec=pltpu.PrefetchScalarGridSpec(
            num_scalar_prefetch=1, grid=(N_SHARD,), in_specs=own + blk, out_specs=mine + blk),
        out_shape=[jax.ShapeDtypeStruct(h, F32) for h in halves]
        + [jax.ShapeDtypeStruct((N_SHARD,) + h, BF16) for h in halves],
        compiler_params=_params(("arbitrary",)),
    )(pos, *g4, *recvs)
    return outs[:n], outs[n:]


def _chip_sum(pos, p32s, recvs):
    parts = 2

    def body(pos_ref, *refs):
        del pos_ref
        p_refs, r_refs, f_refs = refs[:N_BIG], refs[N_BIG:2 * N_BIG], refs[2 * N_BIG:]
        for w in range(N_BIG):
            f_refs[w][...] = ((p_refs[w][...] + r_refs[w][0].astype(F32)) + r_refs[w][1].astype(F32)) \
                + r_refs[w][2].astype(F32)

    quarters = [(r // 2 // parts, cc) for r, cc in SHARD_SHAPES]
    own = [pl.BlockSpec(qt, lambda i, pos_ref: (i, 0)) for qt in quarters]
    rcv = [pl.BlockSpec((3,) + qt, lambda i, pos_ref: (0, i, 0)) for qt in quarters]
    out = [pl.BlockSpec(qt, lambda i, pos_ref: (i, 0)) for qt in quarters]
    return pl.pallas_call(
        body, name="chip_sum",
        grid_spec=pltpu.PrefetchScalarGridSpec(
            num_scalar_prefetch=1, grid=(parts,), in_specs=own + rcv, out_specs=out),
        out_shape=[jax.ShapeDtypeStruct((r // 2, cc), F32) for r, cc in SHARD_SHAPES],
        compiler_params=_params(("arbitrary",)),
    )(pos, *p32s, *recvs)


def _adamw(w, g, m, v):
    m_new = ADAM_B1 * m + (1.0 - ADAM_B1) * g
    v_new = ADAM_B2 * v + (1.0 - ADAM_B2) * (g * g)
    m_hat = m_new / (1.0 - ADAM_B1 ** ADAM_STEP)
    v_hat = v_new / (1.0 - ADAM_B2 ** ADAM_STEP)
    delta = -ADAM_LR * (m_hat / (jnp.sqrt(v_hat) + ADAM_EPS) + ADAM_WD * w)
    return delta, m_new, v_new


def _adam_half(name, which, grads, ws, ms, vs, into=None):
    nb = 4

    def body(which_ref, *refs):
        del which_ref
        groups = [refs[i * N_BIG:(i + 1) * N_BIG] for i in range(4)]
        g_refs, w_refs, m_refs, v_refs = groups
        go_refs, do_refs, mo_refs, vo_refs = [refs[len(refs) - (4 - i) * N_BIG:len(refs) - (3 - i) * N_BIG]
                                              for i in range(4)]
        for w in range(N_BIG):
            g = g_refs[w][...]
            delta, m_new, v_new = _adamw(w_refs[w][...], g, m_refs[w][...], v_refs[w][...])
            go_refs[w][...] = g
            do_refs[w][...] = delta
            mo_refs[w][...] = m_new
            vo_refs[w][...] = v_new

    blocks = [(r // 2 // nb, cc) for r, cc in SHARD_SHAPES]
    half = [pl.BlockSpec(b, lambda i, which_ref: (i, 0)) for b in blocks]
    full = [pl.BlockSpec((None,) + b, lambda i, which_ref: (0, which_ref[0] * nb + i, 0)) for b in blocks]
    shapes = [jax.ShapeDtypeStruct((1,) + shp, F32) for shp in SHARD_SHAPES]
    carried = [] if into is None else [a for kind in into for a in kind]
    first = 1 + 4 * N_BIG
    outs = pl.pallas_call(
        body, name=name,
        grid_spec=pltpu.PrefetchScalarGridSpec(
            num_scalar_prefetch=1, grid=(nb,), in_specs=half + full * 3 + [HBM_SPEC] * len(carried),
            out_specs=full * 4),
        out_shape=shapes * 4,
        input_output_aliases={first + i: i for i in range(len(carried))},
        compiler_params=_params(("arbitrary",)),
    )(which, *grads, *ws, *ms, *vs, *carried)
    return [outs[i * N_BIG:(i + 1) * N_BIG] for i in range(4)]


SMALL_ROWS = 8
ROW_CONV_B, ROW_POOL_SCALE, ROW_LN1_G, ROW_LN1_B, ROW_LN2_G, ROW_LN2_B, ROW_LOSS = range(7)
SMALL_VECS = ((ROW_CONV_B, D_FF), (ROW_POOL_SCALE, POOL_W), (ROW_LN1_G, D_MODEL), (ROW_LN1_B, D_MODEL),
              (ROW_LN2_G, D_MODEL), (ROW_LN2_B, D_MODEL))


def _small_pack(loss, vec_grads):
    def body(*refs):
        loss_ref, gvec, out_ref = refs[0], refs[1:-1], refs[-1]
        out_ref[...] = jnp.zeros_like(out_ref)
        for (row, n), ref in zip(SMALL_VECS, gvec):
            out_ref[row:row + 1, 0:n] = ref[...]
        out_ref[ROW_LOSS:ROW_LOSS + 1, 0:HEAD_DIM] = jnp.broadcast_to(loss_ref[...], (1, HEAD_DIM))

    return pl.pallas_call(
        body, name="small_pack", in_specs=[_whole()] * (1 + len(vec_grads)), out_specs=_whole(),
        out_shape=jax.ShapeDtypeStruct((SMALL_ROWS, D_FF), F32),
    )(loss, *vec_grads)


def _small_pair_sum(own, sibling):
    n = len(own)

    def body(*refs):
        x, y, _ = _mesh_pos()
        for i in range(n):
            refs[2 * n + i][2 * x + y] = refs[i][...] + refs[n + i][...]

    return pl.pallas_call(
        body, name="small_pair_sum", in_specs=[_whole()] * (2 * n), out_specs=[_whole()] * n,
        out_shape=[jax.ShapeDtypeStruct((N_SHARD,) + a.shape, F32) for a in own],
        compiler_params=pltpu.CompilerParams(vmem_limit_bytes=VMEM_LIMIT),
    )(*own, *sibling)


def _small_chip_rider(gathered):
    n = len(gathered)

    def make(inplace, srcs, lands, send_sems, recv_sems):
        del inplace, lands
        x, y, c = _mesh_pos()
        j0 = 2 * x + y
        starts, waits = [], []
        for i in range(n):
            for k, chip in enumerate(_other_chips(x, y)):
                sems = dict(send_sem=send_sems.at[3 * i + k], recv_sem=recv_sems.at[3 * i + k],
                            device_id=(chip[0], chip[1], c), device_id_type=MESH)
                send = pltpu.make_async_remote_copy(src_ref=srcs[i].at[j0], dst_ref=srcs[i].at[j0], **sems)
                arrival = pltpu.make_async_remote_copy(
                    src_ref=srcs[i].at[j0], dst_ref=srcs[i].at[2 * chip[0] + chip[1]], **sems)
                starts.append(send)
                waits += [arrival.wait_recv, send.wait_send]
        return starts, waits

    return _Rider([], gathered, [], 3 * n, make)


def _small_adam(all_a, all_b, all_c, wp, cwp, vec_ws, m_wp, m_cwp, vec_ms, v_wp, v_cwp, vec_vs):
    nv = len(SMALL_VECS)
    np_ = 2 + nv

    def body(*refs):
        all_a_ref, all_b_ref, all_c_ref = refs[0:3]
        w_all, m_all, v_all = (refs[3 + i * np_:3 + (i + 1) * np_] for i in range(3))
        loss_out = refs[3 + 3 * np_]
        outs = refs[4 + 3 * np_:]
        x, y, _ = _mesh_pos()
        j0 = 2 * x + y
        tot_a = ((all_a_ref[0] + all_a_ref[1]) + all_a_ref[2]) + all_a_ref[3]
        tot_b = ((all_b_ref[0] + all_b_ref[1]) + all_b_ref[2]) + all_b_ref[3]
        tot_c = ((all_c_ref[0, j0] + all_c_ref[1, j0]) + all_c_ref[2, j0]) + all_c_ref[3, j0]
        loss_out[...] = tot_b[ROW_LOSS:ROW_LOSS + 1, 0:1]
        grads = [tot_a, tot_c] + [tot_b[row:row + 1, 0:n] for row, n in SMALL_VECS]
        for p in range(np_):
            g = grads[p].reshape(w_all[p].shape)
            delta, m_new, v_new = _adamw(w_all[p][...], g, m_all[p][...], v_all[p][...])
            outs[p][...] = g
            outs[np_ + p][...] = delta
            outs[2 * np_ + p][...] = m_new
            outs[3 * np_ + p][...] = v_new

    pshapes = [wp.shape, cwp.shape] + [wv.shape for wv in vec_ws]
    out_shape = [jax.ShapeDtypeStruct((1, 1), F32)] + [jax.ShapeDtypeStruct(s, F32) for s in pshapes] * 4
    outs = pl.pallas_call(
        body, name="small_adam",
        in_specs=[_whole()] * (3 + 3 * np_), out_specs=[_whole()] * len(out_shape), out_shape=out_shape,
        compiler_params=pltpu.CompilerParams(vmem_limit_bytes=VMEM_LIMIT),
    )(all_a, all_b, all_c, wp, cwp, *vec_ws, m_wp, m_cwp, *vec_ms, v_wp, v_cwp, *vec_vs)
    return outs[0], [outs[1 + i * np_:1 + (i + 1) * np_] for i in range(4)]


def kernel(x, w_in, w_pool, pool_scale, w_out, ln1_g, ln1_b, w_up, conv_w, conv_b, w_down, ln2_g, ln2_b, loss_target, m_w_in, m_w_pool, m_pool_scale, m_w_out, m_ln1_g, m_ln1_b, m_w_up, m_conv_w, m_conv_b, m_w_down, m_ln2_g, m_ln2_b, v_w_in, v_w_pool, v_pool_scale, v_w_out, v_ln1_g, v_ln1_b, v_w_up, v_conv_w, v_conv_b, v_w_down, v_ln2_g, v_ln2_b):
    pos = jnp.stack([lax.axis_index("c"), 2 * lax.axis_index("x") + lax.axis_index("y")]).astype(jnp.int32)
    order = ("w_in", "w_out", "w_up", "w_down")
    w_in_i, w_out_i, w_up_i, w_down_i = range(N_BIG)
    vec_names = ("conv_b", "pool_scale", "ln1_g", "ln1_b", "ln2_g", "ln2_b")

    gathered = _gather_weights([w_in[0], w_out[0], w_up[0], w_down[0]], conv_w[0], (w_in_i,))
    cw_full = jnp.transpose(gathered[N_BIG], (1, 0, 2)).reshape(3, D_FF)
    up_a, up_b, up_c = (0, 176), (176, 176), (352, 160)
    assert up_c[0] + up_c[1] == SHARD_SHAPES[w_up_i][0] // 2

    class MeshComm:
        def __init__(self):
            self.w = {i: gathered[i] for i in range(N_BIG)}
            self.g32, self.g16, self.p32, self.p16, self.recv_b = {}, {}, {}, {}, {}
            self.up_complete = False
            self.tokens, self.chips = {}, []

        def weight(self, name):
            i = order.index(name)
            if name == "w_up" and not self.up_complete:
                (arrs, _), = _comm_only("gather_up_last", [_gather_rider(
                    {i: self.w[i]}, [("d2d_diag", i, up_b), ("d2d", i, up_c)])])
                self.w[i], self.up_complete = arrs[0], True
            full = self.w[i]
            return full.reshape(-1, full.shape[-1]) if name in ("w_out", "w_down") else full

        def _gather(self, ws, ops):
            return _gather_rider({w: self.w[w] for w in ws}, ops), ("w", ws)

        def _pair(self, ws):
            return _pair_rider(ws, [self.g16[w] for w in ws]), ("recv_a", ws)

        def _chip(self, ws):
            return _chip_rider(ws, [self.p16[w] for w in ws]), ("recv_b", ws)

        def plan(self, call):
            out_all, down_all = _whole_half(w_out_i), _whole_half(w_down_i)
            if call == "proj_pool":
                return [self._gather([w_out_i, w_up_i, w_down_i],
                                     [("ici", w_out_i, out_all), ("nbr", w_down_i, down_all),
                                      ("nbr", w_up_i, up_a)])]
            if call == "retention_fwd":
                return [self._gather([w_out_i, w_up_i, w_down_i],
                                     [("d2d", w_out_i, out_all),
                                      ("relay", w_down_i, down_all), ("d2d_nbr", w_down_i, down_all),
                                      ("relay", w_up_i, up_a), ("d2d_nbr", w_up_i, up_a), ("nbr", w_up_i, up_b)])]
            if call == "outproj_ln1":
                return [self._gather([w_up_i, w_down_i],
                                     [("d2d_diag", w_down_i, down_all), ("d2d_diag", w_up_i, up_a),
                                      ("relay", w_up_i, up_b), ("d2d_nbr", w_up_i, up_b), ("ici", w_up_i, up_c)])]
            return []

        def after(self, call):
            return tuple(self.tokens.pop(call, ()))

        def riders(self, call):
            self.pending = self.plan(call)
            return [r for r, _ in self.pending]

        def _start(self, name, rider, before):
            state, token = _split_start(name, rider)
            self.tokens.setdefault(before, []).append(token)
            return state

        def _finish_pair(self, name, state, ws, after):
            _, lands = _split_wait(name, state, after)
            self._finish_sum(ws, lands)

        def landed(self, call, results, outs):
            for (_, (slot, ws)), (inplace, lands) in zip(self.pending, results):
                for w, arr in zip(ws, inplace if len(inplace) else lands):
                    getattr(self, slot)[w] = arr
            if call == "wgrad_out":
                self._finish_pair("pair_exchange_up_wait", self.pair_up, [w_up_i], outs[1])
                self.chips.append(([w_up_i], self._start(
                    "chip_exchange_up_start", self._chip([w_up_i])[0], "wgrad_down")))
            if call == "mix_bwd":
                ws = [w_out_i, w_down_i]
                self._finish_pair("pair_exchange_out_down_wait", self.pair_out_down, ws, outs[0])
            if call == "retention_bwd":
                own, sibling = _split_wait("small_pair_wait", self.small_pair, outs[0])
                self.small_chip = self._start(
                    "small_chip_start", _small_chip_rider(_small_pair_sum(own, sibling)), "wgrad_in")

        def small_gradients(self, loss, small):
            dcw4 = jnp.transpose(small["conv_w"].reshape(3, N_SHARD, DOWN_SH), (1, 0, 2))
            own = [small["w_pool"], _small_pack(loss, [small[n] for n in vec_names]), dcw4]
            ws = [w_out_i, w_down_i]
            parts = [self._chip(ws)[0], _final_rider(own)]
            chip, self.small_pair = _split_parts(
                self._start("chip_out_down_small_pair_start", _merged_rider(parts), "retention_bwd"), parts)
            self.chips.append((ws, chip))

        def gradient(self, name, g32, g16):
            w = order.index(name)
            shape = (N_SHARD,) + SHARD_SHAPES[w]
            self.g32[w], self.g16[w] = g32.reshape(shape), g16.reshape(shape)
            if name == "w_up":
                self.pair_up = self._start("pair_exchange_up_start", self._pair([w])[0], "wgrad_out")
            if name == "w_down":
                self.pair_out_down = self._start("pair_exchange_out_down_start",
                                                 self._pair([w_out_i, w_down_i])[0], "mix_bwd")
            if name == "w_in":
                (_, lands), = _comm_only("pair_exchange_in", [self._pair([w])[0]])
                self._finish_sum([w], lands)
                self.chips.append(([w], self._start("chip_exchange_in_start", self._chip([w])[0], "dx")))

        def _finish_sum(self, ws, lands):
            p32s, p16s = _pair_sum(pos, ws, [self.g32[w] for w in ws], lands)
            for w, p32, p16 in zip(ws, p32s, p16s):
                self.p32[w], self.p16[w] = p32, p16

        def finish(self, after):
            for n, (ws, state) in enumerate(self.chips):
                _, lands = _split_wait("chip_exchange_wait_%d" % n, state, after)
                for w, arr in zip(ws, lands):
                    self.recv_b[w] = arr
            return _split_wait("small_chip_wait", self.small_chip, after)[0]

    comm = MeshComm()
    loss, grad_x, small = _local_step(x[0], loss_target[0], cw_full, conv_b, w_pool[0], pool_scale,
                                      ln1_g, ln1_b, ln2_g, ln2_b, comm)

    given = dict(w_pool=w_pool, pool_scale=pool_scale, ln1_g=ln1_g, ln1_b=ln1_b, conv_w=conv_w, conv_b=conv_b,
                 ln2_g=ln2_g, ln2_b=ln2_b)
    given_m = dict(w_pool=m_w_pool, pool_scale=m_pool_scale, ln1_g=m_ln1_g, ln1_b=m_ln1_b, conv_w=m_conv_w,
                   conv_b=m_conv_b, ln2_g=m_ln2_g, ln2_b=m_ln2_b)
    given_v = dict(w_pool=v_w_pool, pool_scale=v_pool_scale, ln1_g=v_ln1_g, ln1_b=v_ln1_b, conv_w=v_conv_w,
                   conv_b=v_conv_b, ln2_g=v_ln2_g, ln2_b=v_ln2_b)
    args = []
    for src in (given, given_m, given_v):
        args += [src["w_pool"], src["conv_w"], [src[n] for n in vec_names]]
    small_sums = comm.finish(grad_x)
    loss_tot, small_out = _small_adam(*small_sums, *args)
    every = range(N_BIG)
    mine = _chip_sum(pos, [comm.p32[w] for w in every], [comm.recv_b[w] for w in every])
    final_state, _ = _split_start("pair_exchange_f32_start", _final_rider(mine))
    mine = final_state[2][:N_BIG]
    big = ([w_in, w_out, w_up, w_down], [m_w_in, m_w_out, m_w_up, m_w_down], [v_w_in, v_w_out, v_w_up, v_w_down])
    own_half = _adam_half("adam_own_half", pos[0:1], mine, *big)
    _, theirs = _split_wait("pair_exchange_f32_wait", final_state, own_half[0][0])
    big_out = _adam_half("adam_other_half", 1 - pos[0:1], theirs, *big, into=own_half)

    names = ("w_in", "w_pool", "pool_scale", "w_out", "ln1_g", "ln1_b", "w_up", "conv_w", "conv_b", "w_down",
             "ln2_g", "ln2_b")
    small_names = ("w_pool", "conv_w") + vec_names
    result = [loss_tot.reshape(()), grad_x[None]]
    for kind in range(4):
        for n in names:
            if n in order:
                result.append(big_out[kind][order.index(n)])
            else:
                result.append(small_out[kind][small_names.index(n)])
    return tuple(result)
```

```python
import functools

import numpy as np
import jax
import jax.numpy as jnp
from jax import lax
from jax.experimental import pallas as pl
from jax.experimental.pallas import tpu as pltpu

F32 = jnp.float32
BF16 = jnp.bfloat16

D_MODEL = 1024
HEADS = 4
HEAD_DIM = 128
RET_W = HEADS * HEAD_DIM
POOL_WINDOWS = (2, 4, 8, 16)
POOL_W = 512
IN_W = 4 * RET_W + POOL_W
D_FF = 2816
N_SHARD = 4
IN_SH = IN_W // N_SHARD
UP_SH = 2 * D_FF // N_SHARD
DOWN_SH = D_FF // N_SHARD
OUT_SH = D_MODEL // N_SHARD
ROPE_BASE = 10000.0
LN_EPS = 1e-5
RMS_EPS = 1e-6
ALPHA = 2.0 ** 0.25
K_SCALE = HEAD_DIM ** -0.5
SUPER = 256
CHUNK = 64
POOL_HALO = 16
CONV_HALO = 8
FFN_STRIP = 128
LN_ROWS = 32

ADAM_LR = 0.001
ADAM_B1 = 0.9
ADAM_B2 = 0.999
ADAM_EPS = 1e-08
ADAM_WD = 0.01
ADAM_STEP = 10

MESH = pl.DeviceIdType.MESH
VMEM_LIMIT = 56 * 1024 * 1024


def _dot(a, b):
    return jnp.dot(a, b, preferred_element_type=F32)


def _dot_nt(a, b):
    return lax.dot_general(a, b, (((1,), (1,)), ((), ())), preferred_element_type=F32)


def _dot_tn(a, b):
    return lax.dot_general(a, b, (((0,), (0,)), ((), ())), preferred_element_type=F32)


def _sigmoid(x):
    return 1.0 / (1.0 + jnp.exp(-x))


def _params(sem):
    return pltpu.CompilerParams(dimension_semantics=sem, vmem_limit_bytes=VMEM_LIMIT)


def _whole():
    return pl.BlockSpec(memory_space=pltpu.VMEM)


HBM_SPEC = pl.BlockSpec(memory_space=pl.ANY)


class _Rider:
    def __init__(self, inplace, srcs, lands, n_copies, make):
        self.inplace, self.srcs, self.lands, self.n_copies, self.make = list(inplace), list(srcs), list(lands), n_copies, make


def _call(body, *, name, grid, in_specs, out_specs, out_shape, operands, scratch_shapes=(), sem=(),
          aliases=None, riders=(), after=()):
    n_in, n_out, n_scr = len(in_specs), len(out_shape), len(scratch_shapes)
    in_specs, out_specs, out_shape = list(in_specs), list(out_specs), list(out_shape)
    operands, scratch_shapes, aliases = list(operands), list(scratch_shapes), dict(aliases or {})
    in_specs += [_whole()] * len(after)
    operands += list(after)
    for r in riders:
        for a in r.inplace:
            aliases[len(in_specs)] = len(out_shape)
            in_specs.append(HBM_SPEC)
            operands.append(a)
            out_specs.append(HBM_SPEC)
            out_shape.append(jax.ShapeDtypeStruct(a.shape, a.dtype))
        for a in r.srcs:
            in_specs.append(HBM_SPEC)
            operands.append(a)
        for shp in r.lands:
            out_specs.append(HBM_SPEC)
            out_shape.append(shp)
        scratch_shapes += [pltpu.SemaphoreType.DMA((r.n_copies,)), pltpu.SemaphoreType.DMA((r.n_copies,))]

    def full(*refs):
        ins = refs[:n_in]
        at = n_in + len(after)
        r_srcs = []
        for r in riders:
            at += len(r.inplace)
            r_srcs.append(refs[at:at + len(r.srcs)])
            at += len(r.srcs)
        outs = refs[at:at + n_out]
        at += n_out
        r_outs = []
        for r in riders:
            r_outs.append((refs[at:at + len(r.inplace)], refs[at + len(r.inplace):at + len(r.inplace) + len(r.lands)]))
            at += len(r.inplace) + len(r.lands)
        scr = refs[at:at + n_scr]
        at += n_scr
        r_sems = [refs[at + 2 * i:at + 2 * i + 2] for i in range(len(riders))]

        def copies():
            return [r.make(r_outs[i][0], r_srcs[i], r_outs[i][1], r_sems[i][0], r_sems[i][1])
                    for i, r in enumerate(riders)]

        def start():
            for starts, _ in copies():
                for cp in starts:
                    cp.start()

        def finish():
            for _, waits in copies():
                for wait in waits:
                    wait()

        if riders and grid:
            first = functools.reduce(jnp.logical_and, [pl.program_id(d) == 0 for d in range(len(grid))])
            last = functools.reduce(jnp.logical_and, [pl.program_id(d) == grid[d] - 1 for d in range(len(grid))])
            pl.when(first)(start)
            body(*ins, *outs, *scr)
            pl.when(last)(finish)
        else:
            if riders:
                start()
            body(*ins, *outs, *scr)
            if riders:
                finish()

    params = _params(sem) if grid else pltpu.CompilerParams(vmem_limit_bytes=VMEM_LIMIT)
    res = pl.pallas_call(
        full, name=name, grid=grid, in_specs=in_specs, out_specs=out_specs, out_shape=out_shape,
        scratch_shapes=scratch_shapes, input_output_aliases=aliases, compiler_params=params,
    )(*operands)
    outs, at, rider_res = res[:n_out], n_out, []
    for r in riders:
        rider_res.append((res[at:at + len(r.inplace)], res[at + len(r.inplace):at + len(r.inplace) + len(r.lands)]))
        at += len(r.inplace) + len(r.lands)
    return list(outs), rider_res


def _gammas():
    return [1.0 - 2.0 ** (-5.0 - h) for h in range(HEADS)]


def _decay_tables():
    idx = np.arange(SUPER)
    dist = np.abs(idx[:, None] - idx[None, :]).astype(np.float64)
    visible = (idx[None, :] // CHUNK) <= (idx[:, None] // CHUNK)
    mask = np.stack([np.where(visible, g ** dist, 0.0) for g in _gammas()])
    qd = np.concatenate([np.repeat((g ** (idx + 1.0))[:, None], HEAD_DIM, 1) for g in _gammas()], 1)
    kd = np.concatenate([np.repeat((g ** (SUPER - 1.0 - idx))[:, None], HEAD_DIM, 1) for g in _gammas()], 1)
    return (jnp.asarray(mask, F32), jnp.asarray(qd, F32), jnp.asarray(kd, F32))


def _rope_tables(s):
    inv_freq = ROPE_BASE ** (-np.arange(0, HEAD_DIM, 2, dtype=np.float64) / HEAD_DIM)
    ang = np.arange(s, dtype=np.float64)[:, None] * inv_freq[None, :]
    cos, sin = np.cos(ang), np.sin(ang)
    return (jnp.asarray(np.concatenate([cos, cos], 1), F32),
            jnp.asarray(np.concatenate([-sin, sin], 1), F32))


def _rope(t, cosf, sinf):
    return t * cosf + pltpu.roll(t, HEAD_DIM // 2, 1) * sinf


def _rope_t(t, cosf, sinf):
    return t * cosf - pltpu.roll(t, HEAD_DIM // 2, 1) * sinf


def _layernorm_fwd(z):
    mu = jnp.mean(z, axis=-1, keepdims=True)
    zc = z - mu
    var = jnp.mean(zc * zc, axis=-1, keepdims=True)
    rstd = lax.rsqrt(var + LN_EPS)
    return zc * rstd, rstd


def _layernorm_bwd(dy, xhat, rstd, gain):
    dxh = dy * gain
    m1 = jnp.mean(dxh, axis=-1, keepdims=True)
    m2 = jnp.mean(dxh * xhat, axis=-1, keepdims=True)
    return rstd * (dxh - m1 - xhat * m2)


def _proj_pool(x, win4, cosf, sinf, wpool, pscale, ts, riders=(), after=()):
    s = x.shape[0]
    nt = s // ts

    def body(x_ref, w_ref, cos_ref, sin_ref, wp_ref, ps_ref,
             xb_ref, q_ref, k_ref, v_ref, g_ref, pooled_ref, cat_ref, proj_scr, pext_scr):
        i = pl.program_id(0)
        xb = x_ref[...].astype(BF16)
        xb_ref[...] = xb
        for j in range(N_SHARD):
            proj_scr[:, j * IN_SH:(j + 1) * IN_SH] = _dot(xb, w_ref[j])
        cosf_t = cos_ref[...]
        sinf_t = sin_ref[...]
        for h in range(HEADS):
            lo = h * HEAD_DIM
            q_ref[:, lo:lo + HEAD_DIM] = _rope(proj_scr[:, lo:lo + HEAD_DIM], cosf_t, sinf_t).astype(BF16)
            kk = _rope(proj_scr[:, RET_W + lo:RET_W + lo + HEAD_DIM], cosf_t, sinf_t) * K_SCALE
            k_ref[:, lo:lo + HEAD_DIM] = kk.astype(BF16)
        v_ref[...] = proj_scr[:, 2 * RET_W:3 * RET_W].astype(BF16)
        g_ref[...] = proj_scr[:, 3 * RET_W:4 * RET_W]

        @pl.when(i == 0)
        def _():
            pext_scr[0:POOL_HALO, :] = jnp.zeros((POOL_HALO, POOL_W), F32)

        pext_scr[POOL_HALO:POOL_HALO + ts, :] = proj_scr[:, 4 * RET_W:IN_W]
        pos = (i * ts + lax.broadcasted_iota(jnp.int32, (ts, 1), 0) + 1).astype(F32)
        for gi, w in enumerate(POOL_WINDOWS):
            lo = gi * HEAD_DIM
            ext = pext_scr[:, lo:lo + HEAD_DIM]
            acc = ext
            shift = 1
            while shift < w:
                acc = acc + pltpu.roll(acc, shift, 0)
                shift *= 2
            tok = ext[POOL_HALO:POOL_HALO + ts]
            pooled = acc[POOL_HALO:POOL_HALO + ts] / jnp.minimum(pos, float(w)) - tok
            pooled_b = pooled.astype(BF16)
            pooled_ref[:, lo:lo + HEAD_DIM] = pooled_b
            lin = _dot(pooled_b, wp_ref[gi])
            cat_ref[:, lo:lo + HEAD_DIM] = (lin * ps_ref[:, lo:lo + HEAD_DIM]).astype(BF16)
        pext_scr[0:POOL_HALO, :] = pext_scr[ts:ts + POOL_HALO, :]

    tile = lambda w: pl.BlockSpec((ts, w), lambda i: (i, 0))
    return _call(
        body, name="proj_pool", grid=(nt,),
        in_specs=[tile(D_MODEL), _whole(), tile(HEAD_DIM), tile(HEAD_DIM), _whole(), _whole()],
        out_specs=[tile(D_MODEL), tile(RET_W), tile(RET_W), tile(RET_W), tile(RET_W), tile(POOL_W),
                   pl.BlockSpec((ts, POOL_W), lambda i: (i, 1))],
        out_shape=[jax.ShapeDtypeStruct((s, D_MODEL), BF16), jax.ShapeDtypeStruct((s, RET_W), BF16),
                   jax.ShapeDtypeStruct((s, RET_W), BF16), jax.ShapeDtypeStruct((s, RET_W), BF16),
                   jax.ShapeDtypeStruct((s, RET_W), F32), jax.ShapeDtypeStruct((s, POOL_W), BF16),
                   jax.ShapeDtypeStruct((s, 2 * RET_W), BF16)],
        scratch_shapes=[pltpu.VMEM((ts, IN_W), F32), pltpu.VMEM((ts + POOL_HALO, POOL_W), F32)],
        sem=("arbitrary",), operands=(x, win4, cosf, sinf, wpool, pscale), riders=riders, after=after,
    )


def _retention_fwd(q, k, v, g, cat, mask, qd, kd, riders=(), after=()):
    s = q.shape[0]
    ns = s // SUPER
    cdec = [gm ** float(SUPER) for gm in _gammas()]

    def body(q_ref, k_ref, v_ref, g_ref, cat_in, mask_ref, qd_ref, kd_ref,
             ret_ref, cat_ref, st_ref, state_scr):
        del cat_in
        n = pl.program_id(0)

        @pl.when(n == 0)
        def _():
            state_scr[...] = jnp.zeros_like(state_scr)

        for h in range(HEADS):
            sl = slice(h * HEAD_DIM, (h + 1) * HEAD_DIM)
            qh, kh, vh = q_ref[:, sl], k_ref[:, sl], v_ref[:, sl]
            sc = _dot_nt(qh, kh) * mask_ref[h]
            st = state_scr[h]
            stb = st.astype(BF16)
            st_ref[0, h] = stb
            qdb = (qh.astype(F32) * qd_ref[:, sl]).astype(BF16)
            kdb = (kh.astype(F32) * kd_ref[:, sl]).astype(BF16)
            ret = _dot(sc.astype(BF16), vh) + _dot(qdb, stb)
            state_scr[h] = st * cdec[h] + _dot_tn(kdb, vh)
            ret_ref[:, sl] = ret
            r = lax.rsqrt(jnp.mean(ret * ret, axis=-1, keepdims=True) + RMS_EPS)
            gh = g_ref[:, sl]
            cat_ref[:, sl] = ((ret * r) * (gh * _sigmoid(gh))).astype(BF16)

    tile = pl.BlockSpec((SUPER, RET_W), lambda n: (n, 0))
    return _call(
        body, name="retention_fwd", grid=(ns,),
        in_specs=[tile, tile, tile, tile, HBM_SPEC, _whole(), _whole(), _whole()],
        out_specs=[tile, tile, pl.BlockSpec((1, HEADS, HEAD_DIM, HEAD_DIM), lambda n: (n, 0, 0, 0))],
        out_shape=[jax.ShapeDtypeStruct((s, RET_W), F32), jax.ShapeDtypeStruct((s, 2 * RET_W), BF16),
                   jax.ShapeDtypeStruct((ns, HEADS, HEAD_DIM, HEAD_DIM), BF16)],
        scratch_shapes=[pltpu.VMEM((HEADS, HEAD_DIM, HEAD_DIM), F32)],
        aliases={4: 1}, sem=("arbitrary",), operands=(q, k, v, g, cat, mask, qd, kd), riders=riders,
        after=after,
    )


def _outproj_ln1(x, cat, wout, g1, b1, ts, riders=(), after=()):
    s = x.shape[0]

    def body(x_ref, cat_ref, w_ref, g_ref, b_ref, xhat_ref, rstd_ref, h1b_ref):
        z = ALPHA * x_ref[...] + _dot(cat_ref[...], w_ref[...])
        xhat, rstd = _layernorm_fwd(z)
        xhat_ref[...] = xhat
        rstd_ref[...] = rstd
        h1b_ref[...] = (xhat * g_ref[...] + b_ref[...]).astype(BF16)

    tile = lambda w: pl.BlockSpec((ts, w), lambda i: (i, 0))
    return _call(
        body, name="outproj_ln1", grid=(s // ts,),
        in_specs=[tile(D_MODEL), tile(D_MODEL), _whole(), _whole(), _whole()],
        out_specs=[tile(D_MODEL), tile(1), tile(D_MODEL)],
        out_shape=[jax.ShapeDtypeStruct((s, D_MODEL), F32), jax.ShapeDtypeStruct((s, 1), F32),
                   jax.ShapeDtypeStruct((s, D_MODEL), BF16)],
        sem=("arbitrary",), operands=(x, cat, wout, g1, b1), riders=riders, after=after,
    )


def _ffn_fwd_loss(xhat1, h1b, target, wup4, wdown, cw, cb, g1, b1, g2, b2, ts, last_gather=None):
    s = xhat1.shape[0]

    def body(xhat_ref, h1b_ref, tgt_ref, wup_in, wdn_ref, cw_ref, cb_ref, g1_ref, b1_ref, g2_ref, b2_ref,
             ub_ref, act_ref, sd_ref, dz2_ref, dz2b_ref, loss_ref, dg2_ref, db2_ref, *rest):
        i = pl.program_id(0)
        if last_gather is None:
            wup_ref = wup_in
            val_scr, gext_scr, ffn_scr = rest
        else:
            wup_hbm, val_scr, gext_scr, ffn_scr, wup_ref, send_sems, recv_sems, load_sem = rest

        @pl.when(i == 0)
        def _():
            gext_scr[0:CONV_HALO, :] = jnp.zeros((CONV_HALO, D_FF), F32)
            loss_ref[...] = jnp.zeros_like(loss_ref)
            dg2_ref[...] = jnp.zeros_like(dg2_ref)
            db2_ref[...] = jnp.zeros_like(db2_ref)
            if last_gather is not None:
                starts, waits = last_gather.make([wup_hbm], [], [], send_sems, recv_sems)
                for cp in starts:
                    cp.start()
                for wait in waits:
                    wait()
                load = pltpu.make_async_copy(wup_hbm, wup_ref, load_sem.at[0])
                load.start()
                load.wait()

        for half in range(2):
            lo = half * UP_SH
            gext_scr[CONV_HALO:CONV_HALO + ts, lo:lo + UP_SH] = _dot(h1b_ref[...], wup_ref[2 + half])
            val_scr[:, lo:lo + UP_SH] = _dot(h1b_ref[...], wup_ref[half])
            for c0 in range(lo, lo + UP_SH, FFN_STRIP):
                cols = slice(c0, c0 + FFN_STRIP)
                ext = gext_scr[:, cols]
                gate = ext[CONV_HALO:]
                hc = cb_ref[:, cols] + ((pltpu.roll(ext, 2, 0)[CONV_HALO:] * cw_ref[0:1, cols]
                                         + pltpu.roll(ext, 1, 0)[CONV_HALO:] * cw_ref[1:2, cols])
                                        + gate * cw_ref[2:3, cols])
                val = val_scr[:, cols]
                sg = _sigmoid(hc)
                si = hc * sg
                act_ref[:, cols] = (si * val).astype(BF16)
                ub_ref[:, cols] = val.astype(BF16)
                ub_ref[:, D_FF + c0:D_FF + c0 + FFN_STRIP] = gate.astype(BF16)
                sd_ref[:, cols] = hc.astype(BF16)
            part = _dot(act_ref[:, lo:lo + UP_SH], wdn_ref[lo:lo + UP_SH, :])
            if half == 0:
                ffn_scr[...] = part
            else:
                ffn_scr[...] += part

        gext_scr[0:CONV_HALO, :] = gext_scr[ts:ts + CONV_HALO, :]

        loss_acc = jnp.zeros((1, 1), F32)
        dg2_acc = jnp.zeros((1, D_MODEL), F32)
        db2_acc = jnp.zeros((1, D_MODEL), F32)
        for r0 in range(0, ts, LN_ROWS):
            rows = slice(r0, r0 + LN_ROWS)
            h1 = xhat_ref[rows, :] * g1_ref[...] + b1_ref[...]
            xhat2, rstd2 = _layernorm_fwd(ALPHA * h1 + ffn_scr[rows, :])
            diff = (xhat2 * g2_ref[...] + b2_ref[...]) - tgt_ref[rows, :]
            row = jnp.mean(diff * diff, axis=-1, keepdims=True)
            loss_acc = loss_acc + 0.5 * jnp.sum(row, axis=0, keepdims=True)
            dy = diff * (1.0 / D_MODEL)
            dg2_acc = dg2_acc + jnp.sum(dy * xhat2, axis=0, keepdims=True)
            db2_acc = db2_acc + jnp.sum(dy, axis=0, keepdims=True)
            dz2 = _layernorm_bwd(dy, xhat2, rstd2, g2_ref[...])
            dz2_ref[rows, :] = dz2
            dz2b_ref[rows, :] = dz2.astype(BF16)
        loss_ref[...] += loss_acc
        dg2_ref[...] += dg2_acc
        db2_ref[...] += db2_acc

    tile = lambda w: pl.BlockSpec((ts, w), lambda i: (i, 0))
    acc = lambda w: pl.BlockSpec((1, w), lambda i: (0, 0))
    in_specs = [tile(D_MODEL), tile(D_MODEL), tile(D_MODEL)] + [_whole()] * 8
    out_specs = [tile(2 * D_FF), tile(D_FF), tile(D_FF), tile(D_MODEL), tile(D_MODEL),
                 acc(1), acc(D_MODEL), acc(D_MODEL)]
    out_shape = [jax.ShapeDtypeStruct((s, 2 * D_FF), BF16), jax.ShapeDtypeStruct((s, D_FF), BF16),
                 jax.ShapeDtypeStruct((s, D_FF), BF16), jax.ShapeDtypeStruct((s, D_MODEL), F32),
                 jax.ShapeDtypeStruct((s, D_MODEL), BF16),
                 jax.ShapeDtypeStruct((1, 1), F32), jax.ShapeDtypeStruct((1, D_MODEL), F32),
                 jax.ShapeDtypeStruct((1, D_MODEL), F32)]
    scratch = [pltpu.VMEM((ts, D_FF), F32), pltpu.VMEM((ts + CONV_HALO, D_FF), F32), pltpu.VMEM((ts, D_MODEL), F32)]
    aliases = {}
    if last_gather is not None:
        w_up_at = 3
        in_specs[w_up_at] = HBM_SPEC
        aliases[w_up_at] = len(out_shape)
        out_specs.append(HBM_SPEC)
        out_shape.append(jax.ShapeDtypeStruct(wup4.shape, wup4.dtype))
        n = last_gather.n_copies
        scratch += [pltpu.VMEM(wup4.shape, wup4.dtype), pltpu.SemaphoreType.DMA((n,)),
                    pltpu.SemaphoreType.DMA((n,)), pltpu.SemaphoreType.DMA((1,))]
    return pl.pallas_call(
        body, name="ffn_fwd_loss", grid=(s // ts,), in_specs=in_specs, out_specs=out_specs, out_shape=out_shape,
        scratch_shapes=scratch, input_output_aliases=aliases, compiler_params=_params(("arbitrary",)),
    )(xhat1, h1b, target, wup4, wdown, cw, cb, g1, b1, g2, b2)


def _ffn_bwd(dz2, dz2b, ub, sd, xhat1, rstd1, wup4, wdown, cw, g1, ts):
    s = dz2.shape[0]
    nt = s // ts

    def body(dz2_ref, dz2b_ref, ub_ref, sd_ref, xhat_ref, rstd_ref, wup_ref, wdn_ref, cw_ref, g1_ref,
             dub_ref, dz1_ref, dz1b_ref, dg1_ref, db1_ref, dcw_ref, dcb_ref, dext_scr, da_scr):
        i = pl.program_id(0)

        @pl.when(i == 0)
        def _():
            dext_scr[ts:ts + CONV_HALO, :] = jnp.zeros((CONV_HALO, D_FF), F32)
            dg1_ref[...] = jnp.zeros_like(dg1_ref)
            db1_ref[...] = jnp.zeros_like(db1_ref)
            dcw_ref[...] = jnp.zeros_like(dcw_ref)
            dcb_ref[...] = jnp.zeros_like(dcb_ref)

        da_scr[...] = _dot_nt(dz2b_ref[...], wdn_ref[...])
        n_ext = ts + CONV_HALO
        for c0 in range(0, D_FF, FFN_STRIP):
            cols = slice(c0, c0 + FFN_STRIP)
            gcols = slice(D_FF + c0, D_FF + c0 + FFN_STRIP)
            val = ub_ref[:, cols].astype(F32)
            gate = ub_ref[:, gcols].astype(F32)
            da = da_scr[:, cols]
            hc = sd_ref[:, cols].astype(F32)
            sg = _sigmoid(hc)
            dhc = da * val * (sg * (1.0 + hc * (1.0 - sg)))
            dext_scr[0:ts, cols] = dhc
            dext = dext_scr[:, cols]
            dhc1 = pltpu.roll(dext, n_ext - 1, 0)[0:ts]
            dhc2 = pltpu.roll(dext, n_ext - 2, 0)[0:ts]
            dcb_ref[:, cols] += jnp.sum(dhc, axis=0, keepdims=True)
            dcw_ref[0:1, cols] += jnp.sum(dhc2 * gate, axis=0, keepdims=True)
            dcw_ref[1:2, cols] += jnp.sum(dhc1 * gate, axis=0, keepdims=True)
            dcw_ref[2:3, cols] += jnp.sum(dhc * gate, axis=0, keepdims=True)
            dgate = dhc * cw_ref[2:3, cols] + dhc1 * cw_ref[1:2, cols] + dhc2 * cw_ref[0:1, cols]
            dub_ref[:, cols] = (da * (hc * sg)).astype(BF16)
            dub_ref[:, gcols] = dgate.astype(BF16)
        dext_scr[ts:n_ext, :] = dext_scr[0:CONV_HALO, :]
        dh1 = ALPHA * dz2_ref[...]
        for j in range(N_SHARD):
            dh1 = dh1 + _dot_nt(dub_ref[:, j * UP_SH:(j + 1) * UP_SH], wup_ref[j])
        xhat = xhat_ref[...]
        dg1_ref[...] += jnp.sum(dh1 * xhat, axis=0, keepdims=True)
        db1_ref[...] += jnp.sum(dh1, axis=0, keepdims=True)
        dz1 = _layernorm_bwd(dh1, xhat, rstd_ref[...], g1_ref[...])
        dz1_ref[...] = dz1
        dz1b_ref[...] = dz1.astype(BF16)

    tile = lambda w: pl.BlockSpec((ts, w), lambda i: (nt - 1 - i, 0))
    acc = lambda rws, w: pl.BlockSpec((rws, w), lambda i: (0, 0))
    return pl.pallas_call(
        body, name="ffn_bwd", grid=(nt,),
        in_specs=[tile(D_MODEL), tile(D_MODEL), tile(2 * D_FF), tile(D_FF), tile(D_MODEL), tile(1)]
        + [_whole()] * 4,
        out_specs=[tile(2 * D_FF), tile(D_MODEL), tile(D_MODEL), acc(1, D_MODEL), acc(1, D_MODEL),
                   acc(3, D_FF), acc(1, D_FF)],
        out_shape=[jax.ShapeDtypeStruct((s, 2 * D_FF), BF16),
                   jax.ShapeDtypeStruct((s, D_MODEL), F32), jax.ShapeDtypeStruct((s, D_MODEL), BF16),
                   jax.ShapeDtypeStruct((1, D_MODEL), F32),
                   jax.ShapeDtypeStruct((1, D_MODEL), F32), jax.ShapeDtypeStruct((3, D_FF), F32),
                   jax.ShapeDtypeStruct((1, D_FF), F32)],
        scratch_shapes=[pltpu.VMEM((ts + CONV_HALO, D_FF), F32), pltpu.VMEM((ts, D_FF), F32)],
        compiler_params=_params(("arbitrary",)),
    )(dz2, dz2b, ub, sd, xhat1, rstd1, wup4, wdown, cw, g1)


def _mix_bwd(dz1, pooled, ret, g, wout, wpool, pscale, ts, riders=(), after=()):
    s = dz1.shape[0]
    nt = s // ts

    def body(dz1_ref, pooled_ref, ret_ref, g_ref, wout_ref, wp_ref, ps_ref,
             dret_ref, dgp_ref, dwp_ref, dps_ref, eext_scr):
        i = pl.program_id(0)
        r = nt - 1 - i

        @pl.when(i == 0)
        def _():
            eext_scr[ts:ts + POOL_HALO, :] = jnp.zeros((POOL_HALO, POOL_W), F32)
            dwp_ref[...] = jnp.zeros_like(dwp_ref)
            dps_ref[...] = jnp.zeros_like(dps_ref)

        dzb = dz1_ref[...].astype(BF16)
        dcat_r = _dot_nt(dzb, wout_ref[0:RET_W, :])
        dcat_p = _dot_nt(dzb, wout_ref[RET_W:2 * RET_W, :])
        pos = (r * ts + lax.broadcasted_iota(jnp.int32, (ts, 1), 0) + 1).astype(F32)
        dpooled = []
        for gi, w in enumerate(POOL_WINDOWS):
            sl = slice(gi * HEAD_DIM, (gi + 1) * HEAD_DIM)
            pb = pooled_ref[:, sl]
            dy = dcat_p[:, sl]
            dps_ref[:, sl] += jnp.sum(dy * _dot(pb, wp_ref[gi]), axis=0, keepdims=True)
            dlin = (dy * ps_ref[:, sl]).astype(BF16)
            dwp_ref[gi] += _dot_tn(pb, dlin)
            dpg = _dot_nt(dlin, wp_ref[gi])
            dpooled.append(dpg)
            eext_scr[0:ts, sl] = dpg / jnp.minimum(pos, float(w))
        for gi, w in enumerate(POOL_WINDOWS):
            sl = slice(gi * HEAD_DIM, (gi + 1) * HEAD_DIM)
            acc = eext_scr[:, sl]
            shift = 1
            while shift < w:
                acc = acc + pltpu.roll(acc, ts + POOL_HALO - shift, 0)
                shift *= 2
            dgp_ref[:, RET_W + gi * HEAD_DIM:RET_W + (gi + 1) * HEAD_DIM] = (acc[0:ts] - dpooled[gi]).astype(BF16)
        eext_scr[ts:ts + POOL_HALO, :] = eext_scr[0:POOL_HALO, :]
        for h in range(HEADS):
            sl = slice(h * HEAD_DIM, (h + 1) * HEAD_DIM)
            rt = ret_ref[:, sl]
            rr = lax.rsqrt(jnp.mean(rt * rt, axis=-1, keepdims=True) + RMS_EPS)
            rn = rt * rr
            gh = g_ref[:, sl]
            sg = _sigmoid(gh)
            dy = dcat_r[:, sl]
            dgp_ref[:, sl] = (dy * rn * (sg * (1.0 + gh * (1.0 - sg)))).astype(BF16)
            drn = dy * (gh * sg)
            dret_ref[:, sl] = (rr * (drn - rn * jnp.mean(drn * rn, axis=-1, keepdims=True))).astype(BF16)

    tile = lambda w: pl.BlockSpec((ts, w), lambda i: (nt - 1 - i, 0))
    return _call(
        body, name="mix_bwd", grid=(nt,),
        in_specs=[tile(D_MODEL), tile(POOL_W), tile(RET_W), tile(RET_W), _whole(), _whole(), _whole()],
        out_specs=[tile(RET_W), tile(2 * RET_W),
                   pl.BlockSpec((len(POOL_WINDOWS), HEAD_DIM, HEAD_DIM), lambda i: (0, 0, 0)),
                   pl.BlockSpec((1, POOL_W), lambda i: (0, 0))],
        out_shape=[jax.ShapeDtypeStruct((s, RET_W), BF16), jax.ShapeDtypeStruct((s, 2 * RET_W), BF16),
                   jax.ShapeDtypeStruct((len(POOL_WINDOWS), HEAD_DIM, HEAD_DIM), F32),
                   jax.ShapeDtypeStruct((1, POOL_W), F32)],
        scratch_shapes=[pltpu.VMEM((ts + POOL_HALO, POOL_W), F32)],
        sem=("arbitrary",), operands=(dz1, pooled, ret, g, wout, wpool, pscale), riders=riders,
        after=after,
    )


def _retention_bwd(q, k, v, dret, dgp, states, mask, qd, kd, cosf, sinf, riders=(), after=()):
    s = q.shape[0]
    ns = s // SUPER
    cdec = [gm ** float(SUPER) for gm in _gammas()]

    def body(q_ref, k_ref, v_ref, do_ref, dgp_ref, st_ref, mask_ref, qd_ref, kd_ref, cos_ref, sin_ref,
             dproj_ref, dstate_scr):
        i = pl.program_id(0)

        @pl.when(i == 0)
        def _():
            dstate_scr[...] = jnp.zeros_like(dstate_scr)

        cosf_t = cos_ref[...]
        sinf_t = sin_ref[...]
        for h in range(HEADS):
            sl = slice(h * HEAD_DIM, (h + 1) * HEAD_DIM)
            qh, kh, vh, doh = q_ref[:, sl], k_ref[:, sl], v_ref[:, sl], do_ref[:, sl]
            m = mask_ref[h]
            scb = (_dot_nt(qh, kh) * m).astype(BF16)
            dscb = (_dot_nt(doh, vh) * m).astype(BF16)
            stb = st_ref[0, h]
            dst = dstate_scr[h]
            dstb = dst.astype(BF16)
            qdb = (qh.astype(F32) * qd_ref[:, sl]).astype(BF16)
            kdb = (kh.astype(F32) * kd_ref[:, sl]).astype(BF16)
            dq = _dot(dscb, kh) + _dot_nt(doh, stb) * qd_ref[:, sl]
            dk = _dot_tn(dscb, qh) + _dot_nt(vh, dstb) * kd_ref[:, sl]
            dv = _dot_tn(scb, doh) + _dot(kdb, dstb)
            dstate_scr[h] = dst * cdec[h] + _dot_tn(qdb, doh)
            lo = h * HEAD_DIM
            dproj_ref[:, lo:lo + HEAD_DIM] = _rope_t(dq, cosf_t, sinf_t).astype(BF16)
            dproj_ref[:, RET_W + lo:RET_W + lo + HEAD_DIM] = _rope_t(dk * K_SCALE, cosf_t, sinf_t).astype(BF16)
            dproj_ref[:, 2 * RET_W + lo:2 * RET_W + lo + HEAD_DIM] = dv.astype(BF16)
        dproj_ref[:, 3 * RET_W:IN_W] = dgp_ref[...]

    tile = lambda w: pl.BlockSpec((SUPER, w), lambda i: (ns - 1 - i, 0))
    return _call(
        body, name="retention_bwd", grid=(ns,),
        in_specs=[tile(RET_W), tile(RET_W), tile(RET_W), tile(RET_W), tile(2 * RET_W),
                  pl.BlockSpec((1, HEADS, HEAD_DIM, HEAD_DIM), lambda i: (ns - 1 - i, 0, 0, 0)),
                  _whole(), _whole(), _whole(), tile(HEAD_DIM), tile(HEAD_DIM)],
        out_specs=[tile(IN_W)],
        out_shape=[jax.ShapeDtypeStruct((s, IN_W), BF16)],
        scratch_shapes=[pltpu.VMEM((HEADS, HEAD_DIM, HEAD_DIM), F32)],
        sem=("arbitrary",), operands=(q, k, v, dret, dgp, states, mask, qd, kd, cosf, sinf), riders=riders,
        after=after,
    )


def _dx(dz1, dproj, win4, ts, riders=(), after=()):
    s = dz1.shape[0]

    def body(dz1_ref, dp_ref, w_ref, dx_ref):
        acc = ALPHA * dz1_ref[...]
        for j in range(N_SHARD):
            acc = acc + _dot_nt(dp_ref[:, j * IN_SH:(j + 1) * IN_SH], w_ref[j])
        dx_ref[...] = acc

    tile = lambda w: pl.BlockSpec((ts, w), lambda i: (i, 0))
    return _call(
        body, name="dx", grid=(s // ts,),
        in_specs=[tile(D_MODEL), tile(IN_W), _whole()],
        out_specs=[tile(D_MODEL)],
        out_shape=[jax.ShapeDtypeStruct((s, D_MODEL), F32)],
        sem=("arbitrary",), operands=(dz1, dproj, win4), riders=riders, after=after,
    )


def _wgrad(a, b, tm, tn, name, stacked, m_outer, riders=(), after=()):
    s, m = a.shape
    n = b.shape[1]

    def body(a_ref, b_ref, o32_ref, o16_ref):
        res = _dot_tn(a_ref[...], b_ref[...])
        o32_ref[...] = res.reshape(o32_ref.shape)
        o16_ref[...] = res.astype(BF16).reshape(o16_ref.shape)

    if m_outer:
        grid, blocks = (m // tm, n // tn), (lambda g0, g1: (g0, g1))
    else:
        grid, blocks = (n // tn, m // tm), (lambda g0, g1: (g1, g0))
    if stacked:
        shape = (n // tn, m, tn)
        ospec = pl.BlockSpec((1, tm, tn), lambda g0, g1: (blocks(g0, g1)[1], blocks(g0, g1)[0], 0))
    else:
        shape = (m, n)
        ospec = pl.BlockSpec((tm, tn), lambda g0, g1: blocks(g0, g1))
    return _call(
        body, name=name, grid=grid,
        in_specs=[pl.BlockSpec((s, tm), lambda g0, g1: (0, blocks(g0, g1)[0])),
                  pl.BlockSpec((s, tn), lambda g0, g1: (0, blocks(g0, g1)[1]))],
        out_specs=[ospec, ospec],
        out_shape=[jax.ShapeDtypeStruct(shape, F32), jax.ShapeDtypeStruct(shape, BF16)],
        sem=("arbitrary", "arbitrary"), operands=(a, b), riders=riders, after=after,
    )


class _NoComm:
    def __init__(self, win4, wout, wup4, wdown):
        self.weights = dict(w_in=win4, w_out=wout, w_up=wup4, w_down=wdown)
        self.grads = {}

    def weight(self, name):
        return self.weights[name]

    def riders(self, call):
        return ()

    def after(self, call):
        return ()

    def last_gather(self):
        return None

    def landed(self, call, results, outs):
        pass

    def small_gradients(self, loss, small):
        pass

    def gradient(self, name, g32, g16):
        self.grads[name] = (g32, g16)


def _local_step(x, target, cw, cb, wpool, pscale, g1, b1, g2, b2, comm):
    s = x.shape[0]
    ts_a = min(512, s)
    ts_f = min(256, s)
    mask, qd, kd = _decay_tables()
    cosf, sinf = _rope_tables(s)
    wpool_b = wpool.astype(BF16)

    def run(call, fn, *args):
        outs, res = fn(*args, riders=comm.riders(call), after=comm.after(call))
        comm.landed(call, res, outs)
        return outs

    xb, q, k, v, g, pooled, cat = run("proj_pool", _proj_pool, x, comm.weight("w_in"), cosf, sinf, wpool_b,
                                      pscale, ts_a)
    ret, cat, states = run("retention_fwd", _retention_fwd, q, k, v, g, cat, mask, qd, kd)
    wout = comm.weight("w_out")
    xhat1, rstd1, h1b = run("outproj_ln1", _outproj_ln1, x, cat, wout, g1, b1, ts_a)
    wup4, wdown = comm.weight("w_up"), comm.weight("w_down")
    last_gather = comm.last_gather()
    ub, act, sd, dz2, dz2b, loss, dg2, db2, *rest = _ffn_fwd_loss(
        xhat1, h1b, target, wup4, wdown, cw, cb, g1, b1, g2, b2, ts_f, last_gather)
    if last_gather is not None:
        wup4, = rest

    dub, dz1, dz1b, dg1, db1, dcw, dcb = _ffn_bwd(dz2, dz2b, ub, sd, xhat1, rstd1, wup4, wdown, cw, g1, ts_f)
    half = D_MODEL // 2
    comm.gradient("w_up", *run("wgrad_up", _wgrad, h1b, dub, half, UP_SH, "wgrad_up", True, False))
    comm.gradient("w_out", *run("wgrad_out", _wgrad, cat, dz1b, D_MODEL, half, "wgrad_out", False, True))
    comm.gradient("w_down", *run("wgrad_down", _wgrad, act, dz2b, D_FF // 2, half, "wgrad_down", False, True))
    dret, dgp, dwp, dps = run("mix_bwd", _mix_bwd, dz1b, pooled, ret, g, wout, wpool_b, pscale, ts_a)
    small = dict(w_pool=dwp, pool_scale=dps, ln1_g=dg1, ln1_b=db1, conv_w=dcw, conv_b=dcb,
                 ln2_g=dg2, ln2_b=db2)
    comm.small_gradients(loss, small)
    dproj, = run("retention_bwd", _retention_bwd, q, k, v, dret, dgp, states, mask, qd, kd, cosf, sinf)
    comm.gradient("w_in", *run("wgrad_in", _wgrad, xb, dproj, D_MODEL, IN_SH, "wgrad_in", True, True))
    (grad_x,), _ = _dx(dz1, dproj, comm.weight("w_in"), ts_a, after=comm.after("dx"))
    return loss, grad_x, small


CAST_ROWS = 64
SHARD_SHAPES = ((D_MODEL, IN_SH), (OUT_SH, D_MODEL), (D_MODEL, UP_SH), (DOWN_SH, D_MODEL))
N_BIG = len(SHARD_SHAPES)
CW_SHARD = (3, DOWN_SH)


def _mesh_pos():
    return lax.axis_index("x"), lax.axis_index("y"), lax.axis_index("c")


def _other_chips(x, y):
    return [(1 - x, y), (x, 1 - y), (1 - x, 1 - y)]


def _half_rows(w, which):
    hr = SHARD_SHAPES[w][0] // 2
    return pl.ds(pl.multiple_of(which * hr, 16), hr)


def _gather_weights(shards, cw_shard, full):
    def body(*refs):
        in_refs = refs[:N_BIG]
        cw_ref = refs[N_BIG]
        out_refs = refs[N_BIG + 1:2 * N_BIG + 1]
        cwo_ref = refs[2 * N_BIG + 1]
        stage = refs[2 * N_BIG + 2:3 * N_BIG + 2]
        raw = refs[3 * N_BIG + 2:4 * N_BIG + 2 - len(full)]
        send_sems, recv_sems, fsend_sems, frecv_sems, cw_send, cw_recv, local_sems, load_sems = \
            refs[4 * N_BIG + 2 - len(full):]
        x, y, c = _mesh_pos()
        j0 = 2 * x + y
        chips = _other_chips(x, y)

        fetched = [w for w in range(N_BIG) if w not in full]
        f32 = {w: in_refs[w] for w in full}
        loads = []
        for n, w in enumerate(fetched):
            f32[w] = raw[n]
            loads.append(pltpu.make_async_copy(in_refs[w], raw[n], load_sems.at[n]))
            loads[-1].start()

        def cast_to_stage(w):
            def cast(i, carry):
                rows = pl.ds(pl.multiple_of(i * CAST_ROWS, CAST_ROWS), CAST_ROWS)
                stage[w][rows, :] = f32[w][rows, :].astype(BF16)
                return carry
            lax.fori_loop(0, SHARD_SHAPES[w][0] // CAST_ROWS, cast, 0)

        for w in full:
            cast_to_stage(w)

        jx, jy, jd = 2 * (1 - x) + y, 2 * x + (1 - y), 2 * (1 - x) + (1 - y)
        neighbours = [((1 - x, y, c), jx), ((x, 1 - y, c), jy)]
        passed = jnp.where(c == 0, jx, jy)
        pass_to = (jnp.where(c == 0, x, 1 - x), jnp.where(c == 0, 1 - y, y), c)

        def nbr(w, k, block):
            return pltpu.make_async_remote_copy(
                src_ref=stage[w].at[_half_rows(w, c), :], dst_ref=out_refs[w].at[block, _half_rows(w, c), :],
                send_sem=send_sems.at[w, k], recv_sem=recv_sems.at[w, k],
                device_id=neighbours[k][0], device_id_type=MESH)

        def relay(w, block):
            return pltpu.make_async_remote_copy(
                src_ref=out_refs[w].at[passed, _half_rows(w, c), :],
                dst_ref=out_refs[w].at[block, _half_rows(w, c), :],
                send_sem=send_sems.at[w, 2], recv_sem=recv_sems.at[w, 2],
                device_id=pass_to, device_id_type=MESH)

        def d2d(w, k, block, half):
            return pltpu.make_async_remote_copy(
                src_ref=out_refs[w].at[block, _half_rows(w, half), :],
                dst_ref=out_refs[w].at[block, _half_rows(w, half), :],
                send_sem=fsend_sems.at[w, k], recv_sem=frecv_sems.at[w, k],
                device_id=(x, y, 1 - c), device_id_type=MESH)

        def conv(k, block):
            chip = chips[k]
            return pltpu.make_async_remote_copy(
                src_ref=cw_ref, dst_ref=cwo_ref.at[block], send_sem=cw_send.at[k], recv_sem=cw_recv.at[k],
                device_id=(chip[0], chip[1], c), device_id_type=MESH)

        sent = [nbr(w, k, j0) for w in full for k in range(2)] + [conv(k, j0) for k in range(3)]
        for cp in sent:
            cp.start()
        for n, w in enumerate(fetched):
            loads[n].wait()
            cast_to_stage(w)
        local = [pltpu.make_async_copy(stage[w], out_refs[w].at[j0], local_sems.at[w]) for w in range(N_BIG)]
        local.append(pltpu.make_async_copy(cw_ref, cwo_ref.at[j0], local_sems.at[N_BIG]))
        for cp in local:
            cp.start()
        for w in full:
            for k, (_, block) in enumerate(neighbours):
                nbr(w, k, block).wait_recv()
            later = [relay(w, passed)] + [d2d(w, k, block, c) for k, (_, block) in enumerate(neighbours)]
            for cp in later:
                cp.start()
            sent += later
        for w in full:
            relay(w, jd).wait_recv()
            fw = d2d(w, 2, jd, c)
            fw.start()
            sent.append(fw)
        for w in full:
            for k, block in enumerate([jx, jy, jd]):
                d2d(w, k, block, 1 - c).wait_recv()
        for k, chip in enumerate(chips):
            conv(k, 2 * chip[0] + chip[1]).wait_recv()
        for cp in sent:
            cp.wait_send()
        for cp in local:
            cp.wait()

    out_shape = [jax.ShapeDtypeStruct((N_SHARD,) + shp, BF16) for shp in SHARD_SHAPES]
    out_shape.append(jax.ShapeDtypeStruct((N_SHARD,) + CW_SHARD, F32))
    return pl.pallas_call(
        body, name="gather_weights",
        in_specs=[_whole() if w in full else HBM_SPEC for w in range(N_BIG)] + [_whole()],
        out_specs=[HBM_SPEC] * (N_BIG + 1),
        out_shape=out_shape,
        scratch_shapes=[pltpu.VMEM(shp, BF16) for shp in SHARD_SHAPES]
        + [pltpu.VMEM(shp, F32) for w, shp in enumerate(SHARD_SHAPES) if w not in full] + [
            pltpu.SemaphoreType.DMA((N_BIG, 3)), pltpu.SemaphoreType.DMA((N_BIG, 3)),
            pltpu.SemaphoreType.DMA((N_BIG, 3)), pltpu.SemaphoreType.DMA((N_BIG, 3)),
            pltpu.SemaphoreType.DMA((3,)), pltpu.SemaphoreType.DMA((3,)),
            pltpu.SemaphoreType.DMA((N_BIG + 1,)), pltpu.SemaphoreType.DMA((N_BIG - len(full),))],
        compiler_params=pltpu.CompilerParams(vmem_limit_bytes=VMEM_LIMIT),
    )(*shards, cw_shard)


def _gather_rider(arrays, ops):
    ws = sorted(arrays)

    def make(inplace, srcs, lands, send_sems, recv_sems):
        del srcs, lands
        x, y, c = _mesh_pos()
        j0, jx, jy, jd = 2 * x + y, 2 * (1 - x) + y, 2 * x + (1 - y), 2 * (1 - x) + (1 - y)
        x_nbr, y_nbr, sibling = (1 - x, y, c), (x, 1 - y, c), (x, y, 1 - c)
        starts, waits = [], []
        for n, (kind, w, (r0, nr)) in enumerate(ops):
            ref = inplace[ws.index(w)]
            hr = SHARD_SHAPES[w][0] // 2
            rows = lambda core: pl.ds(pl.multiple_of(core * hr + r0, 16), nr)
            mine, theirs = rows(c), rows(1 - c)
            if kind == "ici":
                moves = [(ref.at[j0, mine, :], x_nbr, ref.at[jx, mine, :]),
                         (ref.at[j0, mine, :], y_nbr, ref.at[jy, mine, :]),
                         (ref.at[j0, mine, :], (1 - x, 1 - y, c), ref.at[jd, mine, :])]
            elif kind == "nbr":
                moves = [(ref.at[j0, mine, :], x_nbr, ref.at[jx, mine, :]),
                         (ref.at[j0, mine, :], y_nbr, ref.at[jy, mine, :])]
            elif kind == "relay":
                passed = jnp.where(c == 0, jx, jy)
                to = (jnp.where(c == 0, x, 1 - x), jnp.where(c == 0, 1 - y, y), c)
                moves = [(ref.at[passed, mine, :], to, ref.at[jd, mine, :])]
            else:
                blocks = dict(d2d=[jx, jy, jd], d2d_nbr=[jx, jy], d2d_diag=[jd])[kind]
                moves = [(ref.at[b, mine, :], sibling, ref.at[b, theirs, :]) for b in blocks]
            for k, (src, to, landing) in enumerate(moves):
                sems = dict(send_sem=send_sems.at[3 * n + k], recv_sem=recv_sems.at[3 * n + k],
                            device_id=to, device_id_type=MESH)
                send = pltpu.make_async_remote_copy(src_ref=src, dst_ref=src, **sems)
                arrival = pltpu.make_async_remote_copy(src_ref=src, dst_ref=landing, **sems)
                starts.append(send)
                waits += [arrival.wait_recv, send.wait_send]
        return starts, waits

    return _Rider([arrays[w] for w in ws], [], [], 3 * len(ops), make)


def _whole_half(w):
    return (0, SHARD_SHAPES[w][0] // 2)


def _pair_rider(ws, g16s):
    def make(inplace, srcs, lands, send_sems, recv_sems):
        del inplace
        x, y, c = _mesh_pos()
        copies = [pltpu.make_async_remote_copy(
            src_ref=srcs[i].at[:, _half_rows(w, 1 - c), :], dst_ref=lands[i],
            send_sem=send_sems.at[i], recv_sem=recv_sems.at[i], device_id=(x, y, 1 - c), device_id_type=MESH)
            for i, w in enumerate(ws)]
        return copies, [cp.wait for cp in copies]

    lands = [jax.ShapeDtypeStruct((N_SHARD, SHARD_SHAPES[w][0] // 2, SHARD_SHAPES[w][1]), BF16) for w in ws]
    return _Rider([], g16s, lands, len(ws), make)


def _chip_rider(ws, p16s):
    def make(inplace, srcs, lands, send_sems, recv_sems):
        del inplace
        x, y, c = _mesh_pos()
        copies = []
        for i in range(len(ws)):
            for k, chip in enumerate(_other_chips(x, y)):
                copies.append(pltpu.make_async_remote_copy(
                    src_ref=srcs[i].at[2 * chip[0] + chip[1]], dst_ref=lands[i].at[k],
                    send_sem=send_sems.at[3 * i + k], recv_sem=recv_sems.at[3 * i + k],
                    device_id=(chip[0], chip[1], c), device_id_type=MESH))
        return copies, [cp.wait for cp in copies]

    lands = [jax.ShapeDtypeStruct((3, SHARD_SHAPES[w][0] // 2, SHARD_SHAPES[w][1]), BF16) for w in ws]
    return _Rider([], p16s, lands, 3 * len(ws), make)


def _final_rider(halves):
    def make(inplace, srcs, lands, send_sems, recv_sems):
        del inplace
        x, y, c = _mesh_pos()
        copies = [pltpu.make_async_remote_copy(
            src_ref=srcs[i], dst_ref=lands[i], send_sem=send_sems.at[i], recv_sem=recv_sems.at[i],
            device_id=(x, y, 1 - c), device_id_type=MESH) for i in range(len(halves))]
        return copies, [cp.wait for cp in copies]

    return _Rider([], halves, [jax.ShapeDtypeStruct(h.shape, h.dtype) for h in halves], len(halves), make)


def _comm_only(name, riders):
    _, res = _call(lambda: None, name=name, grid=(), in_specs=[], out_specs=[], out_shape=[], operands=(),
                   riders=riders)
    return res


class _SemList:
    def __init__(self, refs):
        self.at = list(refs)


def _merged_rider(riders):
    srcs = [a for r in riders for a in r.srcs]
    lands = [a for r in riders for a in r.lands]

    def make(inplace, src_refs, land_refs, send_sems, recv_sems):
        starts, waits = [], []
        s0 = l0 = c0 = 0
        for r in riders:
            part = r.make(inplace, src_refs[s0:s0 + len(r.srcs)], land_refs[l0:l0 + len(r.lands)],
                          _SemList(send_sems.at[c0:c0 + r.n_copies]), _SemList(recv_sems.at[c0:c0 + r.n_copies]))
            starts += part[0]
            waits += part[1]
            s0, l0, c0 = s0 + len(r.srcs), l0 + len(r.lands), c0 + r.n_copies
        return starts, waits

    return _Rider([], srcs, lands, sum(r.n_copies for r in riders), make)


def _split_start(name, rider):
    assert not rider.inplace
    ns, nl, n = len(rider.srcs), len(rider.lands), rider.n_copies

    def body(*refs):
        srcs, lands = refs[:ns], refs[ns:ns + nl]
        sems = refs[ns + nl:ns + nl + 2 * n]
        token = refs[-1]
        starts, _ = rider.make([], srcs, lands, _SemList(sems[:n]), _SemList(sems[n:]))
        for cp in starts:
            cp.start()
        token[...] = jnp.zeros_like(token)

    buffers = [pltpu.with_memory_space_constraint(a, pltpu.HBM) for a in rider.srcs]
    buffers += [pltpu.with_memory_space_constraint(lax.empty(s.shape, s.dtype), pltpu.HBM) for s in rider.lands]
    hbm = pl.BlockSpec(memory_space=pltpu.HBM)
    sem = pl.BlockSpec(memory_space=pltpu.SEMAPHORE)
    outs = pl.pallas_call(
        body, name=name,
        out_shape=tuple([pltpu.SemaphoreType.DMA(())] * (2 * n) + [pltpu.HBM(b.shape, b.dtype) for b in buffers]
                        + [jax.ShapeDtypeStruct((8, 128), F32)]),
        in_specs=[hbm] * (ns + nl),
        out_specs=tuple([sem] * (2 * n) + [hbm] * (ns + nl) + [_whole()]),
        input_output_aliases={i: 2 * n + i for i in range(ns + nl)},
        compiler_params=pltpu.CompilerParams(has_side_effects=pltpu.SideEffectType.DATAFLOW_SIDE_EFFECTING),
    )(*buffers)
    return (rider, outs[:2 * n], outs[2 * n:2 * n + ns + nl]), outs[-1]


def _split_parts(state, riders):
    merged, sems, buffers = state
    n, ns = merged.n_copies, len(merged.srcs)
    parts, s0, l0, c0 = [], 0, 0, 0
    for r in riders:
        parts.append((r, list(sems[c0:c0 + r.n_copies]) + list(sems[n + c0:n + c0 + r.n_copies]),
                      list(buffers[s0:s0 + len(r.srcs)]) + list(buffers[ns + l0:ns + l0 + len(r.lands)])))
        s0, l0, c0 = s0 + len(r.srcs), l0 + len(r.lands), c0 + r.n_copies
    return parts


def _split_wait(name, state, after):
    rider, sems, buffers = state
    ns, nl, n = len(rider.srcs), len(rider.lands), rider.n_copies

    def body(*refs):
        srcs, lands = refs[:ns], refs[ns:ns + nl]
        sem_refs = refs[ns + nl:ns + nl + 2 * n]
        _, waits = rider.make([], srcs, lands, _SemList(sem_refs[:n]), _SemList(sem_refs[n:]))
        for wait in waits:
            wait()

    hbm = pl.BlockSpec(memory_space=pltpu.HBM)
    sem = pl.BlockSpec(memory_space=pltpu.SEMAPHORE)
    outs = pl.pallas_call(
        body, name=name,
        out_shape=tuple(pltpu.HBM(b.shape, b.dtype) for b in buffers),
        in_specs=[hbm] * (ns + nl) + [sem] * (2 * n) + [HBM_SPEC],
        out_specs=tuple([hbm] * (ns + nl)),
        input_output_aliases={i: i for i in range(ns + nl)},
        compiler_params=pltpu.CompilerParams(has_side_effects=pltpu.SideEffectType.DATAFLOW_SIDE_EFFECTING),
    )(*buffers, *sems, after)
    return list(outs[:ns]), list(outs[ns:])


def _pair_sum(pos, ws, g32s, recvs):
    n = len(ws)

    def body(pos_ref, *refs):
        g_refs, r_refs = refs[:n], refs[n:2 * n]
        p32_refs, p16_refs = refs[2 * n:3 * n], refs[3 * n:]
        for i in range(n):
            tot = g_refs[i][...] + r_refs[i][...].astype(F32)
            p16_refs[i][...] = tot.astype(BF16)

            @pl.when(pl.program_id(0) == pos_ref[1])
            def _(i=i, tot=tot):
                p32_refs[i][...] = tot

    halves = [(SHARD_SHAPES[w][0] // 2, SHARD_SHAPES[w][1]) for w in ws]
    own = [pl.BlockSpec((None, None) + h, lambda j, pos_ref: (j, pos_ref[0], 0, 0)) for h in halves]
    blk = [pl.BlockSpec((None,) + h, lambda j, pos_ref: (j, 0, 0)) for h in halves]
    mine = [pl.BlockSpec(h, lambda j, pos_ref: (0, 0)) for h in halves]
    g4 = [g.reshape((N_SHARD, 2) + h) for g, h in zip(g32s, halves)]
    outs = pl.pallas_call(
        body, name="pair_sum_" + "_".join(str(w) for w in ws),
        grid_spec=pltpu.PrefetchScalarGridSpec(
            num_scalar_prefetch=1, grid=(N_SHARD,), in_specs=own + blk, out_specs=mine + blk),
        out_shape=[jax.ShapeDtypeStruct(h, F32) for h in halves]
        + [jax.ShapeDtypeStruct((N_SHARD,) + h, BF16) for h in halves],
        compiler_params=_params(("arbitrary",)),
    )(pos, *g4, *recvs)
    return outs[:n], outs[n:]


def _chip_sum(pos, p32s, recvs):
    parts = 2

    def body(pos_ref, *refs):
        del pos_ref
        p_refs, r_refs, f_refs = refs[:N_BIG], refs[N_BIG:2 * N_BIG], refs[2 * N_BIG:]
        for w in range(N_BIG):
            f_refs[w][...] = ((p_refs[w][...] + r_refs[w][0].astype(F32)) + r_refs[w][1].astype(F32)) \
                + r_refs[w][2].astype(F32)

    quarters = [(r // 2 // parts, cc) for r, cc in SHARD_SHAPES]
    own = [pl.BlockSpec(qt, lambda i, pos_ref: (i, 0)) for qt in quarters]
    rcv = [pl.BlockSpec((3,) + qt, lambda i, pos_ref: (0, i, 0)) for qt in quarters]
    out = [pl.BlockSpec(qt, lambda i, pos_ref: (i, 0)) for qt in quarters]
    return pl.pallas_call(
        body, name="chip_sum",
        grid_spec=pltpu.PrefetchScalarGridSpec(
            num_scalar_prefetch=1, grid=(parts,), in_specs=own + rcv, out_specs=out),
        out_shape=[jax.ShapeDtypeStruct((r // 2, cc), F32) for r, cc in SHARD_SHAPES],
        compiler_params=_params(("arbitrary",)),
    )(pos, *p32s, *recvs)


def _adamw(w, g, m, v):
    m_new = ADAM_B1 * m + (1.0 - ADAM_B1) * g
    v_new = ADAM_B2 * v + (1.0 - ADAM_B2) * (g * g)
    m_hat = m_new / (1.0 - ADAM_B1 ** ADAM_STEP)
    v_hat = v_new / (1.0 - ADAM_B2 ** ADAM_STEP)
    delta = -ADAM_LR * (m_hat / (jnp.sqrt(v_hat) + ADAM_EPS) + ADAM_WD * w)
    return delta, m_new, v_new


def _adam_half(name, which, grads, ws, ms, vs, into=None):
    nb = 4

    def body(which_ref, *refs):
        del which_ref
        groups = [refs[i * N_BIG:(i + 1) * N_BIG] for i in range(4)]
        g_refs, w_refs, m_refs, v_refs = groups
        go_refs, do_refs, mo_refs, vo_refs = [refs[len(refs) - (4 - i) * N_BIG:len(refs) - (3 - i) * N_BIG]
                                              for i in range(4)]
        for w in range(N_BIG):
            g = g_refs[w][...]
            delta, m_new, v_new = _adamw(w_refs[w][...], g, m_refs[w][...], v_refs[w][...])
            go_refs[w][...] = g
            do_refs[w][...] = delta
            mo_refs[w][...] = m_new
            vo_refs[w][...] = v_new

    blocks = [(r // 2 // nb, cc) for r, cc in SHARD_SHAPES]
    half = [pl.BlockSpec(b, lambda i, which_ref: (i, 0)) for b in blocks]
    full = [pl.BlockSpec((None,) + b, lambda i, which_ref: (0, which_ref[0] * nb + i, 0)) for b in blocks]
    shapes = [jax.ShapeDtypeStruct((1,) + shp, F32) for shp in SHARD_SHAPES]
    carried = [] if into is None else [a for kind in into for a in kind]
    first = 1 + 4 * N_BIG
    outs = pl.pallas_call(
        body, name=name,
        grid_spec=pltpu.PrefetchScalarGridSpec(
            num_scalar_prefetch=1, grid=(nb,), in_specs=half + full * 3 + [HBM_SPEC] * len(carried),
            out_specs=full * 4),
        out_shape=shapes * 4,
        input_output_aliases={first + i: i for i in range(len(carried))},
        compiler_params=_params(("arbitrary",)),
    )(which, *grads, *ws, *ms, *vs, *carried)
    return [outs[i * N_BIG:(i + 1) * N_BIG] for i in range(4)]


SMALL_ROWS = 8
ROW_CONV_B, ROW_POOL_SCALE, ROW_LN1_G, ROW_LN1_B, ROW_LN2_G, ROW_LN2_B, ROW_LOSS = range(7)
SMALL_VECS = ((ROW_CONV_B, D_FF), (ROW_POOL_SCALE, POOL_W), (ROW_LN1_G, D_MODEL), (ROW_LN1_B, D_MODEL),
              (ROW_LN2_G, D_MODEL), (ROW_LN2_B, D_MODEL))


def _small_pack(loss, vec_grads):
    def body(*refs):
        loss_ref, gvec, out_ref = refs[0], refs[1:-1], refs[-1]
        out_ref[...] = jnp.zeros_like(out_ref)
        for (row, n), ref in zip(SMALL_VECS, gvec):
            out_ref[row:row + 1, 0:n] = ref[...]
        out_ref[ROW_LOSS:ROW_LOSS + 1, 0:HEAD_DIM] = jnp.broadcast_to(loss_ref[...], (1, HEAD_DIM))

    return pl.pallas_call(
        body, name="small_pack", in_specs=[_whole()] * (1 + len(vec_grads)), out_specs=_whole(),
        out_shape=jax.ShapeDtypeStruct((SMALL_ROWS, D_FF), F32),
    )(loss, *vec_grads)


def _small_pair_sum(own, sibling):
    n = len(own)

    def body(*refs):
        x, y, _ = _mesh_pos()
        for i in range(n):
            refs[2 * n + i][2 * x + y] = refs[i][...] + refs[n + i][...]

    return pl.pallas_call(
        body, name="small_pair_sum", in_specs=[_whole()] * (2 * n), out_specs=[_whole()] * n,
        out_shape=[jax.ShapeDtypeStruct((N_SHARD,) + a.shape, F32) for a in own],
        compiler_params=pltpu.CompilerParams(vmem_limit_bytes=VMEM_LIMIT),
    )(*own, *sibling)


def _small_chip_rider(gathered):
    n = len(gathered)

    def make(inplace, srcs, lands, send_sems, recv_sems):
        del inplace, lands
        x, y, c = _mesh_pos()
        j0 = 2 * x + y
        starts, waits = [], []
        for i in range(n):
            for k, chip in enumerate(_other_chips(x, y)):
                sems = dict(send_sem=send_sems.at[3 * i + k], recv_sem=recv_sems.at[3 * i + k],
                            device_id=(chip[0], chip[1], c), device_id_type=MESH)
                send = pltpu.make_async_remote_copy(src_ref=srcs[i].at[j0], dst_ref=srcs[i].at[j0], **sems)
                arrival = pltpu.make_async_remote_copy(
                    src_ref=srcs[i].at[j0], dst_ref=srcs[i].at[2 * chip[0] + chip[1]], **sems)
                starts.append(send)
                waits += [arrival.wait_recv, send.wait_send]
        return starts, waits

    return _Rider([], gathered, [], 3 * n, make)


def _small_adam(all_a, all_b, all_c, wp, cwp, vec_ws, m_wp, m_cwp, vec_ms, v_wp, v_cwp, vec_vs):
    nv = len(SMALL_VECS)
    np_ = 2 + nv

    def body(*refs):
        all_a_ref, all_b_ref, all_c_ref = refs[0:3]
        w_all, m_all, v_all = (refs[3 + i * np_:3 + (i + 1) * np_] for i in range(3))
        loss_out = refs[3 + 3 * np_]
        outs = refs[4 + 3 * np_:]
        x, y, _ = _mesh_pos()
        j0 = 2 * x + y
        tot_a = ((all_a_ref[0] + all_a_ref[1]) + all_a_ref[2]) + all_a_ref[3]
        tot_b = ((all_b_ref[0] + all_b_ref[1]) + all_b_ref[2]) + all_b_ref[3]
        tot_c = ((all_c_ref[0, j0] + all_c_ref[1, j0]) + all_c_ref[2, j0]) + all_c_ref[3, j0]
        loss_out[...] = tot_b[ROW_LOSS:ROW_LOSS + 1, 0:1]
        grads = [tot_a, tot_c] + [tot_b[row:row + 1, 0:n] for row, n in SMALL_VECS]
        for p in range(np_):
            delta, m_new, v_new = _adamw(w_all[p][...], grads[p], m_all[p][...], v_all[p][...])
            outs[p][...] = grads[p]
            outs[np_ + p][...] = delta
            outs[2 * np_ + p][...] = m_new
            outs[3 * np_ + p][...] = v_new

    pshapes = [wp.shape, CW_SHARD] + [wv.shape for wv in vec_ws]
    out_shape = [jax.ShapeDtypeStruct((1, 1), F32)] + [jax.ShapeDtypeStruct(s, F32) for s in pshapes] * 4
    outs = pl.pallas_call(
        body, name="small_adam",
        in_specs=[_whole()] * (3 + 3 * np_), out_specs=[_whole()] * len(out_shape), out_shape=out_shape,
        compiler_params=pltpu.CompilerParams(vmem_limit_bytes=VMEM_LIMIT),
    )(all_a, all_b, all_c, wp, cwp, *vec_ws, m_wp, m_cwp, *vec_ms, v_wp, v_cwp, *vec_vs)
    return outs[0], [outs[1 + i * np_:1 + (i + 1) * np_] for i in range(4)]


def kernel(x, w_in, w_pool, pool_scale, w_out, ln1_g, ln1_b, w_up, conv_w, conv_b, w_down, ln2_g, ln2_b, loss_target, m_w_in, m_w_pool, m_pool_scale, m_w_out, m_ln1_g, m_ln1_b, m_w_up, m_conv_w, m_conv_b, m_w_down, m_ln2_g, m_ln2_b, v_w_in, v_w_pool, v_pool_scale, v_w_out, v_ln1_g, v_ln1_b, v_w_up, v_conv_w, v_conv_b, v_w_down, v_ln2_g, v_ln2_b):
    pos = jnp.stack([lax.axis_index("c"), 2 * lax.axis_index("x") + lax.axis_index("y")]).astype(jnp.int32)
    order = ("w_in", "w_out", "w_up", "w_down")
    w_in_i, w_out_i, w_up_i, w_down_i = range(N_BIG)
    vec_names = ("conv_b", "pool_scale", "ln1_g", "ln1_b", "ln2_g", "ln2_b")

    gathered = _gather_weights([w_in[0], w_out[0], w_up[0], w_down[0]], conv_w[0], (w_in_i,))
    cw_full = jnp.transpose(gathered[N_BIG], (1, 0, 2)).reshape(3, D_FF)
    up_a, up_b, up_c = (0, 176), (176, 176), (352, 160)
    assert up_c[0] + up_c[1] == SHARD_SHAPES[w_up_i][0] // 2

    class MeshComm:
        def __init__(self):
            self.w = {i: gathered[i] for i in range(N_BIG)}
            self.g32, self.g16, self.p32, self.p16, self.recv_b = {}, {}, {}, {}, {}
            self.tokens, self.chips = {}, []

        def weight(self, name):
            full = self.w[order.index(name)]
            return full.reshape(-1, full.shape[-1]) if name in ("w_out", "w_down") else full

        def last_gather(self):
            return _gather_rider({w_up_i: self.w[w_up_i]}, [("d2d_diag", w_up_i, up_b), ("d2d", w_up_i, up_c)])

        def _gather(self, ws, ops):
            return _gather_rider({w: self.w[w] for w in ws}, ops), ("w", ws)

        def _pair(self, ws):
            return _pair_rider(ws, [self.g16[w] for w in ws]), ("recv_a", ws)

        def _chip(self, ws):
            return _chip_rider(ws, [self.p16[w] for w in ws]), ("recv_b", ws)

        def plan(self, call):
            out_all, down_all = _whole_half(w_out_i), _whole_half(w_down_i)
            if call == "proj_pool":
                return [self._gather([w_out_i, w_up_i, w_down_i],
                                     [("ici", w_out_i, out_all), ("nbr", w_down_i, down_all),
                                      ("nbr", w_up_i, up_a)])]
            if call == "retention_fwd":
                return [self._gather([w_out_i, w_up_i, w_down_i],
                                     [("d2d", w_out_i, out_all),
                                      ("relay", w_down_i, down_all), ("d2d_nbr", w_down_i, down_all),
                                      ("relay", w_up_i, up_a), ("d2d_nbr", w_up_i, up_a), ("nbr", w_up_i, up_b)])]
            if call == "outproj_ln1":
                return [self._gather([w_up_i, w_down_i],
                                     [("d2d_diag", w_down_i, down_all), ("d2d_diag", w_up_i, up_a),
                                      ("relay", w_up_i, up_b), ("d2d_nbr", w_up_i, up_b), ("ici", w_up_i, up_c)])]
            return []

        def after(self, call):
            return tuple(self.tokens.pop(call, ()))

        def riders(self, call):
            self.pending = self.plan(call)
            return [r for r, _ in self.pending]

        def _start(self, name, rider, before):
            state, token = _split_start(name, rider)
            self.tokens.setdefault(before, []).append(token)
            return state

        def _finish_pair(self, name, state, ws, after):
            _, lands = _split_wait(name, state, after)
            self._finish_sum(ws, lands)

        def landed(self, call, results, outs):
            for (_, (slot, ws)), (inplace, lands) in zip(self.pending, results):
                for w, arr in zip(ws, inplace if len(inplace) else lands):
                    getattr(self, slot)[w] = arr
            if call == "wgrad_out":
                self._finish_pair("pair_exchange_up_wait", self.pair_up, [w_up_i], outs[1])
                self.chips.append(([w_up_i], self._start(
                    "chip_exchange_up_start", self._chip([w_up_i])[0], "wgrad_down")))
            if call == "mix_bwd":
                ws = [w_out_i, w_down_i]
                self._finish_pair("pair_exchange_out_down_wait", self.pair_out_down, ws, outs[0])
            if call == "retention_bwd":
                own, sibling = _split_wait("small_pair_wait", self.small_pair, outs[0])
                self.small_chip = self._start(
                    "small_chip_start", _small_chip_rider(_small_pair_sum(own, sibling)), "wgrad_in")

        def small_gradients(self, loss, small):
            dcw4 = jnp.transpose(small["conv_w"].reshape(3, N_SHARD, DOWN_SH), (1, 0, 2))
            own = [small["w_pool"], _small_pack(loss, [small[n] for n in vec_names]), dcw4]
            ws = [w_out_i, w_down_i]
            parts = [self._chip(ws)[0], _final_rider(own)]
            chip, self.small_pair = _split_parts(
                self._start("chip_out_down_small_pair_start", _merged_rider(parts), "retention_bwd"), parts)
            self.chips.append((ws, chip))

        def gradient(self, name, g32, g16):
            w = order.index(name)
            shape = (N_SHARD,) + SHARD_SHAPES[w]
            self.g32[w], self.g16[w] = g32.reshape(shape), g16.reshape(shape)
            if name == "w_up":
                self.pair_up = self._start("pair_exchange_up_start", self._pair([w])[0], "wgrad_out")
            if name == "w_down":
                self.pair_out_down = self._start("pair_exchange_out_down_start",
                                                 self._pair([w_out_i, w_down_i])[0], "mix_bwd")
            if name == "w_in":
                (_, lands), = _comm_only("pair_exchange_in", [self._pair([w])[0]])
                self._finish_sum([w], lands)
                self.chips.append(([w], self._start("chip_exchange_in_start", self._chip([w])[0], "dx")))

        def _finish_sum(self, ws, lands):
            p32s, p16s = _pair_sum(pos, ws, [self.g32[w] for w in ws], lands)
            for w, p32, p16 in zip(ws, p32s, p16s):
                self.p32[w], self.p16[w] = p32, p16

        def finish(self, after):
            for n, (ws, state) in enumerate(self.chips):
                _, lands = _split_wait("chip_exchange_wait_%d" % n, state, after)
                for w, arr in zip(ws, lands):
                    self.recv_b[w] = arr
            return _split_wait("small_chip_wait", self.small_chip, after)[0]

    comm = MeshComm()
    loss, grad_x, small = _local_step(x[0], loss_target[0], cw_full, conv_b, w_pool[0], pool_scale,
                                      ln1_g, ln1_b, ln2_g, ln2_b, comm)

    given = dict(w_pool=w_pool, pool_scale=pool_scale, ln1_g=ln1_g, ln1_b=ln1_b, conv_w=conv_w, conv_b=conv_b,
                 ln2_g=ln2_g, ln2_b=ln2_b)
    given_m = dict(w_pool=m_w_pool, pool_scale=m_pool_scale, ln1_g=m_ln1_g, ln1_b=m_ln1_b, conv_w=m_conv_w,
                   conv_b=m_conv_b, ln2_g=m_ln2_g, ln2_b=m_ln2_b)
    given_v = dict(w_pool=v_w_pool, pool_scale=v_pool_scale, ln1_g=v_ln1_g, ln1_b=v_ln1_b, conv_w=v_conv_w,
                   conv_b=v_conv_b, ln2_g=v_ln2_g, ln2_b=v_ln2_b)
    args = []
    for src in (given, given_m, given_v):
        args += [src["w_pool"][0], src["conv_w"][0], [src[n] for n in vec_names]]
    small_sums = comm.finish(grad_x)
    loss_tot, small_out = _small_adam(*small_sums, *args)
    every = range(N_BIG)
    mine = _chip_sum(pos, [comm.p32[w] for w in every], [comm.recv_b[w] for w in every])
    final_state, _ = _split_start("pair_exchange_f32_start", _final_rider(mine))
    mine = final_state[2][:N_BIG]
    big = ([w_in, w_out, w_up, w_down], [m_w_in, m_w_out, m_w_up, m_w_down], [v_w_in, v_w_out, v_w_up, v_w_down])
    own_half = _adam_half("adam_own_half", pos[0:1], mine, *big)
    _, theirs = _split_wait("pair_exchange_f32_wait", final_state, own_half[0][0])
    big_out = _adam_half("adam_other_half", 1 - pos[0:1], theirs, *big, into=own_half)

    names = ("w_in", "w_pool", "pool_scale", "w_out", "ln1_g", "ln1_b", "w_up", "conv_w", "conv_b", "w_down",
             "ln2_g", "ln2_b")
    small_names = ("w_pool", "conv_w") + vec_names
    result = [loss_tot.reshape(()), grad_x[None]]
    for kind in range(4):
        for n in names:
            if n in order:
                result.append(big_out[kind][order.index(n)])
            else:
                val = small_out[kind][small_names.index(n)]
                if n == "conv_w":
                    val = val[None]
                elif n == "w_pool":
                    val = val[None]
                result.append(val)
    return tuple(result)
```

```python
import functools

import numpy as np
import jax
import jax.numpy as jnp
from jax import lax
from jax.experimental import pallas as pl
from jax.experimental.pallas import tpu as pltpu

F32 = jnp.float32
BF16 = jnp.bfloat16

D_MODEL = 1024
HEADS = 4
HEAD_DIM = 128
RET_W = HEADS * HEAD_DIM
POOL_WINDOWS = (2, 4, 8, 16)
POOL_W = 512
IN_W = 4 * RET_W + POOL_W
D_FF = 2816
N_SHARD = 4
IN_SH = IN_W // N_SHARD
UP_SH = 2 * D_FF // N_SHARD
DOWN_SH = D_FF // N_SHARD
OUT_SH = D_MODEL // N_SHARD
ROPE_BASE = 10000.0
LN_EPS = 1e-5
RMS_EPS = 1e-6
ALPHA = 2.0 ** 0.25
K_SCALE = HEAD_DIM ** -0.5
SUPER = 256
CHUNK = 64
POOL_HALO = 16
CONV_HALO = 8
FFN_STRIP = 128
LN_ROWS = 32

ADAM_LR = 0.001
ADAM_B1 = 0.9
ADAM_B2 = 0.999
ADAM_EPS = 1e-08
ADAM_WD = 0.01
ADAM_STEP = 10

MESH = pl.DeviceIdType.MESH
VMEM_LIMIT = 56 * 1024 * 1024


def _dot(a, b):
    return jnp.dot(a, b, preferred_element_type=F32)


def _dot_nt(a, b):
    return lax.dot_general(a, b, (((1,), (1,)), ((), ())), preferred_element_type=F32)


def _dot_tn(a, b):
    return lax.dot_general(a, b, (((0,), (0,)), ((), ())), preferred_element_type=F32)


def _sigmoid(x):
    return 1.0 / (1.0 + jnp.exp(-x))


def _params(sem):
    return pltpu.CompilerParams(dimension_semantics=sem, vmem_limit_bytes=VMEM_LIMIT)


def _whole():
    return pl.BlockSpec(memory_space=pltpu.VMEM)


HBM_SPEC = pl.BlockSpec(memory_space=pl.ANY)


class _Rider:
    def __init__(self, inplace, srcs, lands, n_copies, make):
        self.inplace, self.srcs, self.lands, self.n_copies, self.make = list(inplace), list(srcs), list(lands), n_copies, make


def _call(body, *, name, grid, in_specs, out_specs, out_shape, operands, scratch_shapes=(), sem=(),
          aliases=None, riders=(), after=()):
    n_in, n_out, n_scr = len(in_specs), len(out_shape), len(scratch_shapes)
    in_specs, out_specs, out_shape = list(in_specs), list(out_specs), list(out_shape)
    operands, scratch_shapes, aliases = list(operands), list(scratch_shapes), dict(aliases or {})
    in_specs += [_whole()] * len(after)
    operands += list(after)
    for r in riders:
        for a in r.inplace:
            aliases[len(in_specs)] = len(out_shape)
            in_specs.append(HBM_SPEC)
            operands.append(a)
            out_specs.append(HBM_SPEC)
            out_shape.append(jax.ShapeDtypeStruct(a.shape, a.dtype))
        for a in r.srcs:
            in_specs.append(HBM_SPEC)
            operands.append(a)
        for shp in r.lands:
            out_specs.append(HBM_SPEC)
            out_shape.append(shp)
        scratch_shapes += [pltpu.SemaphoreType.DMA((r.n_copies,)), pltpu.SemaphoreType.DMA((r.n_copies,))]

    def full(*refs):
        ins = refs[:n_in]
        at = n_in + len(after)
        r_srcs = []
        for r in riders:
            at += len(r.inplace)
            r_srcs.append(refs[at:at + len(r.srcs)])
            at += len(r.srcs)
        outs = refs[at:at + n_out]
        at += n_out
        r_outs = []
        for r in riders:
            r_outs.append((refs[at:at + len(r.inplace)], refs[at + len(r.inplace):at + len(r.inplace) + len(r.lands)]))
            at += len(r.inplace) + len(r.lands)
        scr = refs[at:at + n_scr]
        at += n_scr
        r_sems = [refs[at + 2 * i:at + 2 * i + 2] for i in range(len(riders))]

        def copies():
            return [r.make(r_outs[i][0], r_srcs[i], r_outs[i][1], r_sems[i][0], r_sems[i][1])
                    for i, r in enumerate(riders)]

        def start():
            for starts, _ in copies():
                for cp in starts:
                    cp.start()

        def finish():
            for _, waits in copies():
                for wait in waits:
                    wait()

        if riders and grid:
            first = functools.reduce(jnp.logical_and, [pl.program_id(d) == 0 for d in range(len(grid))])
            last = functools.reduce(jnp.logical_and, [pl.program_id(d) == grid[d] - 1 for d in range(len(grid))])
            pl.when(first)(start)
            body(*ins, *outs, *scr)
            pl.when(last)(finish)
        else:
            if riders:
                start()
            body(*ins, *outs, *scr)
            if riders:
                finish()

    params = _params(sem) if grid else pltpu.CompilerParams(vmem_limit_bytes=VMEM_LIMIT)
    res = pl.pallas_call(
        full, name=name, grid=grid, in_specs=in_specs, out_specs=out_specs, out_shape=out_shape,
        scratch_shapes=scratch_shapes, input_output_aliases=aliases, compiler_params=params,
    )(*operands)
    outs, at, rider_res = res[:n_out], n_out, []
    for r in riders:
        rider_res.append((res[at:at + len(r.inplace)], res[at + len(r.inplace):at + len(r.inplace) + len(r.lands)]))
        at += len(r.inplace) + len(r.lands)
    return list(outs), rider_res


def _gammas():
    return [1.0 - 2.0 ** (-5.0 - h) for h in range(HEADS)]


def _decay_tables():
    idx = np.arange(SUPER)
    dist = np.abs(idx[:, None] - idx[None, :]).astype(np.float64)
    visible = (idx[None, :] // CHUNK) <= (idx[:, None] // CHUNK)
    mask = np.stack([np.where(visible, g ** dist, 0.0) for g in _gammas()])
    qd = np.concatenate([np.repeat((g ** (idx + 1.0))[:, None], HEAD_DIM, 1) for g in _gammas()], 1)
    kd = np.concatenate([np.repeat((g ** (SUPER - 1.0 - idx))[:, None], HEAD_DIM, 1) for g in _gammas()], 1)
    return (jnp.asarray(mask, F32), jnp.asarray(qd, F32), jnp.asarray(kd, F32))


def _rope_tables(s):
    inv_freq = ROPE_BASE ** (-np.arange(0, HEAD_DIM, 2, dtype=np.float64) / HEAD_DIM)
    ang = np.arange(s, dtype=np.float64)[:, None] * inv_freq[None, :]
    cos, sin = np.cos(ang), np.sin(ang)
    return (jnp.asarray(np.concatenate([cos, cos], 1), F32),
            jnp.asarray(np.concatenate([-sin, sin], 1), F32))


def _rope(t, cosf, sinf):
    return t * cosf + pltpu.roll(t, HEAD_DIM // 2, 1) * sinf


def _rope_t(t, cosf, sinf):
    return t * cosf - pltpu.roll(t, HEAD_DIM // 2, 1) * sinf


def _layernorm_fwd(z):
    mu = jnp.mean(z, axis=-1, keepdims=True)
    zc = z - mu
    var = jnp.mean(zc * zc, axis=-1, keepdims=True)
    rstd = lax.rsqrt(var + LN_EPS)
    return zc * rstd, rstd


def _layernorm_bwd(dy, xhat, rstd, gain):
    dxh = dy * gain
    m1 = jnp.mean(dxh, axis=-1, keepdims=True)
    m2 = jnp.mean(dxh * xhat, axis=-1, keepdims=True)
    return rstd * (dxh - m1 - xhat * m2)


def _proj_pool(x, win4, cosf, sinf, wpool, pscale, ts, riders=(), after=()):
    s = x.shape[0]
    nt = s // ts

    def body(x_ref, w_ref, cos_ref, sin_ref, wp_ref, ps_ref,
             xb_ref, q_ref, k_ref, v_ref, g_ref, pooled_ref, cat_ref, proj_scr, pext_scr):
        i = pl.program_id(0)
        xb = x_ref[...].astype(BF16)
        xb_ref[...] = xb
        for j in range(N_SHARD):
            proj_scr[:, j * IN_SH:(j + 1) * IN_SH] = _dot(xb, w_ref[j])
        cosf_t = cos_ref[...]
        sinf_t = sin_ref[...]
        for h in range(HEADS):
            lo = h * HEAD_DIM
            q_ref[:, lo:lo + HEAD_DIM] = _rope(proj_scr[:, lo:lo + HEAD_DIM], cosf_t, sinf_t).astype(BF16)
            kk = _rope(proj_scr[:, RET_W + lo:RET_W + lo + HEAD_DIM], cosf_t, sinf_t) * K_SCALE
            k_ref[:, lo:lo + HEAD_DIM] = kk.astype(BF16)
        v_ref[...] = proj_scr[:, 2 * RET_W:3 * RET_W].astype(BF16)
        g_ref[...] = proj_scr[:, 3 * RET_W:4 * RET_W]

        @pl.when(i == 0)
        def _():
            pext_scr[0:POOL_HALO, :] = jnp.zeros((POOL_HALO, POOL_W), F32)

        pext_scr[POOL_HALO:POOL_HALO + ts, :] = proj_scr[:, 4 * RET_W:IN_W]
        pos = (i * ts + lax.broadcasted_iota(jnp.int32, (ts, 1), 0) + 1).astype(F32)
        for gi, w in enumerate(POOL_WINDOWS):
            lo = gi * HEAD_DIM
            ext = pext_scr[:, lo:lo + HEAD_DIM]
            acc = ext
            shift = 1
            while shift < w:
                acc = acc + pltpu.roll(acc, shift, 0)
                shift *= 2
            tok = ext[POOL_HALO:POOL_HALO + ts]
            pooled = acc[POOL_HALO:POOL_HALO + ts] / jnp.minimum(pos, float(w)) - tok
            pooled_b = pooled.astype(BF16)
            pooled_ref[:, lo:lo + HEAD_DIM] = pooled_b
            lin = _dot(pooled_b, wp_ref[gi])
            cat_ref[:, lo:lo + HEAD_DIM] = (lin * ps_ref[:, lo:lo + HEAD_DIM]).astype(BF16)
        pext_scr[0:POOL_HALO, :] = pext_scr[ts:ts + POOL_HALO, :]

    tile = lambda w: pl.BlockSpec((ts, w), lambda i: (i, 0))
    return _call(
        body, name="proj_pool", grid=(nt,),
        in_specs=[tile(D_MODEL), _whole(), tile(HEAD_DIM), tile(HEAD_DIM), _whole(), _whole()],
        out_specs=[tile(D_MODEL), tile(RET_W), tile(RET_W), tile(RET_W), tile(RET_W), tile(POOL_W),
                   pl.BlockSpec((ts, POOL_W), lambda i: (i, 1))],
        out_shape=[jax.ShapeDtypeStruct((s, D_MODEL), BF16), jax.ShapeDtypeStruct((s, RET_W), BF16),
                   jax.ShapeDtypeStruct((s, RET_W), BF16), jax.ShapeDtypeStruct((s, RET_W), BF16),
                   jax.ShapeDtypeStruct((s, RET_W), F32), jax.ShapeDtypeStruct((s, POOL_W), BF16),
                   jax.ShapeDtypeStruct((s, 2 * RET_W), BF16)],
        scratch_shapes=[pltpu.VMEM((ts, IN_W), F32), pltpu.VMEM((ts + POOL_HALO, POOL_W), F32)],
        sem=("arbitrary",), operands=(x, win4, cosf, sinf, wpool, pscale), riders=riders, after=after,
    )


def _retention_fwd(q, k, v, g, cat, mask, qd, kd, riders=(), after=()):
    s = q.shape[0]
    ns = s // SUPER
    cdec = [gm ** float(SUPER) for gm in _gammas()]

    def body(q_ref, k_ref, v_ref, g_ref, cat_in, mask_ref, qd_ref, kd_ref,
             ret_ref, cat_ref, st_ref, state_scr):
        del cat_in
        n = pl.program_id(0)

        @pl.when(n == 0)
        def _():
            state_scr[...] = jnp.zeros_like(state_scr)

        for h in range(HEADS):
            sl = slice(h * HEAD_DIM, (h + 1) * HEAD_DIM)
            qh, kh, vh = q_ref[:, sl], k_ref[:, sl], v_ref[:, sl]
            sc = _dot_nt(qh, kh) * mask_ref[h]
            st = state_scr[h]
            stb = st.astype(BF16)
            st_ref[0, h] = stb
            qdb = (qh.astype(F32) * qd_ref[:, sl]).astype(BF16)
            kdb = (kh.astype(F32) * kd_ref[:, sl]).astype(BF16)
            ret = _dot(sc.astype(BF16), vh) + _dot(qdb, stb)
            state_scr[h] = st * cdec[h] + _dot_tn(kdb, vh)
            ret_ref[:, sl] = ret
            r = lax.rsqrt(jnp.mean(ret * ret, axis=-1, keepdims=True) + RMS_EPS)
            gh = g_ref[:, sl]
            cat_ref[:, sl] = ((ret * r) * (gh * _sigmoid(gh))).astype(BF16)

    tile = pl.BlockSpec((SUPER, RET_W), lambda n: (n, 0))
    return _call(
        body, name="retention_fwd", grid=(ns,),
        in_specs=[tile, tile, tile, tile, HBM_SPEC, _whole(), _whole(), _whole()],
        out_specs=[tile, tile, pl.BlockSpec((1, HEADS, HEAD_DIM, HEAD_DIM), lambda n: (n, 0, 0, 0))],
        out_shape=[jax.ShapeDtypeStruct((s, RET_W), F32), jax.ShapeDtypeStruct((s, 2 * RET_W), BF16),
                   jax.ShapeDtypeStruct((ns, HEADS, HEAD_DIM, HEAD_DIM), BF16)],
        scratch_shapes=[pltpu.VMEM((HEADS, HEAD_DIM, HEAD_DIM), F32)],
        aliases={4: 1}, sem=("arbitrary",), operands=(q, k, v, g, cat, mask, qd, kd), riders=riders,
        after=after,
    )


def _outproj_ln1(x, cat, wout, g1, b1, ts, riders=(), after=()):
    s = x.shape[0]

    def body(x_ref, cat_ref, w_ref, g_ref, b_ref, xhat_ref, rstd_ref, h1b_ref):
        z = ALPHA * x_ref[...] + _dot(cat_ref[...], w_ref[...])
        xhat, rstd = _layernorm_fwd(z)
        xhat_ref[...] = xhat
        rstd_ref[...] = rstd
        h1b_ref[...] = (xhat * g_ref[...] + b_ref[...]).astype(BF16)

    tile = lambda w: pl.BlockSpec((ts, w), lambda i: (i, 0))
    return _call(
        body, name="outproj_ln1", grid=(s // ts,),
        in_specs=[tile(D_MODEL), tile(D_MODEL), _whole(), _whole(), _whole()],
        out_specs=[tile(D_MODEL), tile(1), tile(D_MODEL)],
        out_shape=[jax.ShapeDtypeStruct((s, D_MODEL), F32), jax.ShapeDtypeStruct((s, 1), F32),
                   jax.ShapeDtypeStruct((s, D_MODEL), BF16)],
        sem=("arbitrary",), operands=(x, cat, wout, g1, b1), riders=riders, after=after,
    )


def _ffn_fwd_loss(xhat1, h1b, target, wup4, wdown, cw, cb, g1, b1, g2, b2, ts):
    s = xhat1.shape[0]

    def body(xhat_ref, h1b_ref, tgt_ref, wup_ref, wdn_ref, cw_ref, cb_ref, g1_ref, b1_ref, g2_ref, b2_ref,
             ub_ref, act_ref, sd_ref, dz2_ref, dz2b_ref, loss_ref, dg2_ref, db2_ref, val_scr, gext_scr, ffn_scr):
        i = pl.program_id(0)

        @pl.when(i == 0)
        def _():
            gext_scr[0:CONV_HALO, :] = jnp.zeros((CONV_HALO, D_FF), F32)
            loss_ref[...] = jnp.zeros_like(loss_ref)
            dg2_ref[...] = jnp.zeros_like(dg2_ref)
            db2_ref[...] = jnp.zeros_like(db2_ref)

        for half in range(2):
            lo = half * UP_SH
            gext_scr[CONV_HALO:CONV_HALO + ts, lo:lo + UP_SH] = _dot(h1b_ref[...], wup_ref[2 + half])
            val_scr[:, lo:lo + UP_SH] = _dot(h1b_ref[...], wup_ref[half])
            for c0 in range(lo, lo + UP_SH, FFN_STRIP):
                cols = slice(c0, c0 + FFN_STRIP)
                ext = gext_scr[:, cols]
                gate = ext[CONV_HALO:]
                hc = cb_ref[:, cols] + ((pltpu.roll(ext, 2, 0)[CONV_HALO:] * cw_ref[0:1, cols]
                                         + pltpu.roll(ext, 1, 0)[CONV_HALO:] * cw_ref[1:2, cols])
                                        + gate * cw_ref[2:3, cols])
                val = val_scr[:, cols]
                sg = _sigmoid(hc)
                si = hc * sg
                act_ref[:, cols] = (si * val).astype(BF16)
                ub_ref[:, cols] = val.astype(BF16)
                ub_ref[:, D_FF + c0:D_FF + c0 + FFN_STRIP] = gate.astype(BF16)
                sd_ref[:, cols] = hc.astype(BF16)
            part = _dot(act_ref[:, lo:lo + UP_SH], wdn_ref[lo:lo + UP_SH, :])
            if half == 0:
                ffn_scr[...] = part
            else:
                ffn_scr[...] += part

        gext_scr[0:CONV_HALO, :] = gext_scr[ts:ts + CONV_HALO, :]

        loss_acc = jnp.zeros((1, 1), F32)
        dg2_acc = jnp.zeros((1, D_MODEL), F32)
        db2_acc = jnp.zeros((1, D_MODEL), F32)
        for r0 in range(0, ts, LN_ROWS):
            rows = slice(r0, r0 + LN_ROWS)
            h1 = xhat_ref[rows, :] * g1_ref[...] + b1_ref[...]
            xhat2, rstd2 = _layernorm_fwd(ALPHA * h1 + ffn_scr[rows, :])
            diff = (xhat2 * g2_ref[...] + b2_ref[...]) - tgt_ref[rows, :]
            row = jnp.mean(diff * diff, axis=-1, keepdims=True)
            loss_acc = loss_acc + 0.5 * jnp.sum(row, axis=0, keepdims=True)
            dy = diff * (1.0 / D_MODEL)
            dg2_acc = dg2_acc + jnp.sum(dy * xhat2, axis=0, keepdims=True)
            db2_acc = db2_acc + jnp.sum(dy, axis=0, keepdims=True)
            dz2 = _layernorm_bwd(dy, xhat2, rstd2, g2_ref[...])
            dz2_ref[rows, :] = dz2
            dz2b_ref[rows, :] = dz2.astype(BF16)
        loss_ref[...] += loss_acc
        dg2_ref[...] += dg2_acc
        db2_ref[...] += db2_acc

    tile = lambda w: pl.BlockSpec((ts, w), lambda i: (i, 0))
    acc = lambda w: pl.BlockSpec((1, w), lambda i: (0, 0))
    return pl.pallas_call(
        body, name="ffn_fwd_loss", grid=(s // ts,),
        in_specs=[tile(D_MODEL), tile(D_MODEL), tile(D_MODEL)] + [_whole()] * 8,
        out_specs=[tile(2 * D_FF), tile(D_FF), tile(D_FF), tile(D_MODEL), tile(D_MODEL),
                   acc(1), acc(D_MODEL), acc(D_MODEL)],
        out_shape=[jax.ShapeDtypeStruct((s, 2 * D_FF), BF16), jax.ShapeDtypeStruct((s, D_FF), BF16),
                   jax.ShapeDtypeStruct((s, D_FF), BF16), jax.ShapeDtypeStruct((s, D_MODEL), F32),
                   jax.ShapeDtypeStruct((s, D_MODEL), BF16),
                   jax.ShapeDtypeStruct((1, 1), F32), jax.ShapeDtypeStruct((1, D_MODEL), F32),
                   jax.ShapeDtypeStruct((1, D_MODEL), F32)],
        scratch_shapes=[pltpu.VMEM((ts, D_FF), F32), pltpu.VMEM((ts + CONV_HALO, D_FF), F32),
                        pltpu.VMEM((ts, D_MODEL), F32)],
        compiler_params=_params(("arbitrary",)),
    )(xhat1, h1b, target, wup4, wdown, cw, cb, g1, b1, g2, b2)


def _ffn_bwd(dz2, dz2b, ub, sd, xhat1, rstd1, wup4, wdown, cw, g1, ts):
    s = dz2.shape[0]
    nt = s // ts

    def body(dz2_ref, dz2b_ref, ub_ref, sd_ref, xhat_ref, rstd_ref, wup_ref, wdn_ref, cw_ref, g1_ref,
             dub_ref, dz1_ref, dz1b_ref, dg1_ref, db1_ref, dcw_ref, dcb_ref, dext_scr, da_scr):
        i = pl.program_id(0)

        @pl.when(i == 0)
        def _():
            dext_scr[ts:ts + CONV_HALO, :] = jnp.zeros((CONV_HALO, D_FF), F32)
            dg1_ref[...] = jnp.zeros_like(dg1_ref)
            db1_ref[...] = jnp.zeros_like(db1_ref)
            dcw_ref[...] = jnp.zeros_like(dcw_ref)
            dcb_ref[...] = jnp.zeros_like(dcb_ref)

        da_scr[...] = _dot_nt(dz2b_ref[...], wdn_ref[...])
        n_ext = ts + CONV_HALO
        for c0 in range(0, D_FF, FFN_STRIP):
            cols = slice(c0, c0 + FFN_STRIP)
            gcols = slice(D_FF + c0, D_FF + c0 + FFN_STRIP)
            val = ub_ref[:, cols].astype(F32)
            gate = ub_ref[:, gcols].astype(F32)
            da = da_scr[:, cols]
            hc = sd_ref[:, cols].astype(F32)
            sg = _sigmoid(hc)
            dhc = da * val * (sg * (1.0 + hc * (1.0 - sg)))
            dext_scr[0:ts, cols] = dhc
            dext = dext_scr[:, cols]
            dhc1 = pltpu.roll(dext, n_ext - 1, 0)[0:ts]
            dhc2 = pltpu.roll(dext, n_ext - 2, 0)[0:ts]
            dcb_ref[:, cols] += jnp.sum(dhc, axis=0, keepdims=True)
            dcw_ref[0:1, cols] += jnp.sum(dhc2 * gate, axis=0, keepdims=True)
            dcw_ref[1:2, cols] += jnp.sum(dhc1 * gate, axis=0, keepdims=True)
            dcw_ref[2:3, cols] += jnp.sum(dhc * gate, axis=0, keepdims=True)
            dgate = dhc * cw_ref[2:3, cols] + dhc1 * cw_ref[1:2, cols] + dhc2 * cw_ref[0:1, cols]
            dub_ref[:, cols] = (da * (hc * sg)).astype(BF16)
            dub_ref[:, gcols] = dgate.astype(BF16)
        dext_scr[ts:n_ext, :] = dext_scr[0:CONV_HALO, :]
        dh1 = ALPHA * dz2_ref[...]
        for j in range(N_SHARD):
            dh1 = dh1 + _dot_nt(dub_ref[:, j * UP_SH:(j + 1) * UP_SH], wup_ref[j])
        xhat = xhat_ref[...]
        dg1_ref[...] += jnp.sum(dh1 * xhat, axis=0, keepdims=True)
        db1_ref[...] += jnp.sum(dh1, axis=0, keepdims=True)
        dz1 = _layernorm_bwd(dh1, xhat, rstd_ref[...], g1_ref[...])
        dz1_ref[...] = dz1
        dz1b_ref[...] = dz1.astype(BF16)

    tile = lambda w: pl.BlockSpec((ts, w), lambda i: (nt - 1 - i, 0))
    acc = lambda rws, w: pl.BlockSpec((rws, w), lambda i: (0, 0))
    return pl.pallas_call(
        body, name="ffn_bwd", grid=(nt,),
        in_specs=[tile(D_MODEL), tile(D_MODEL), tile(2 * D_FF), tile(D_FF), tile(D_MODEL), tile(1)]
        + [_whole()] * 4,
        out_specs=[tile(2 * D_FF), tile(D_MODEL), tile(D_MODEL), acc(1, D_MODEL), acc(1, D_MODEL),
                   acc(3, D_FF), acc(1, D_FF)],
        out_shape=[jax.ShapeDtypeStruct((s, 2 * D_FF), BF16),
                   jax.ShapeDtypeStruct((s, D_MODEL), F32), jax.ShapeDtypeStruct((s, D_MODEL), BF16),
                   jax.ShapeDtypeStruct((1, D_MODEL), F32),
                   jax.ShapeDtypeStruct((1, D_MODEL), F32), jax.ShapeDtypeStruct((3, D_FF), F32),
                   jax.ShapeDtypeStruct((1, D_FF), F32)],
        scratch_shapes=[pltpu.VMEM((ts + CONV_HALO, D_FF), F32), pltpu.VMEM((ts, D_FF), F32)],
        compiler_params=_params(("arbitrary",)),
    )(dz2, dz2b, ub, sd, xhat1, rstd1, wup4, wdown, cw, g1)


def _mix_bwd(dz1, pooled, ret, g, wout, wpool, pscale, ts, riders=(), after=()):
    s = dz1.shape[0]
    nt = s // ts

    def body(dz1_ref, pooled_ref, ret_ref, g_ref, wout_ref, wp_ref, ps_ref,
             dret_ref, dgp_ref, dwp_ref, dps_ref, eext_scr):
        i = pl.program_id(0)
        r = nt - 1 - i

        @pl.when(i == 0)
        def _():
            eext_scr[ts:ts + POOL_HALO, :] = jnp.zeros((POOL_HALO, POOL_W), F32)
            dwp_ref[...] = jnp.zeros_like(dwp_ref)
            dps_ref[...] = jnp.zeros_like(dps_ref)

        dzb = dz1_ref[...].astype(BF16)
        dcat_r = _dot_nt(dzb, wout_ref[0:RET_W, :])
        dcat_p = _dot_nt(dzb, wout_ref[RET_W:2 * RET_W, :])
        pos = (r * ts + lax.broadcasted_iota(jnp.int32, (ts, 1), 0) + 1).astype(F32)
        dpooled = []
        for gi, w in enumerate(POOL_WINDOWS):
            sl = slice(gi * HEAD_DIM, (gi + 1) * HEAD_DIM)
            pb = pooled_ref[:, sl]
            dy = dcat_p[:, sl]
            dps_ref[:, sl] += jnp.sum(dy * _dot(pb, wp_ref[gi]), axis=0, keepdims=True)
            dlin = (dy * ps_ref[:, sl]).astype(BF16)
            dwp_ref[gi] += _dot_tn(pb, dlin)
            dpg = _dot_nt(dlin, wp_ref[gi])
            dpooled.append(dpg)
            eext_scr[0:ts, sl] = dpg / jnp.minimum(pos, float(w))
        for gi, w in enumerate(POOL_WINDOWS):
            sl = slice(gi * HEAD_DIM, (gi + 1) * HEAD_DIM)
            acc = eext_scr[:, sl]
            shift = 1
            while shift < w:
                acc = acc + pltpu.roll(acc, ts + POOL_HALO - shift, 0)
                shift *= 2
            dgp_ref[:, RET_W + gi * HEAD_DIM:RET_W + (gi + 1) * HEAD_DIM] = (acc[0:ts] - dpooled[gi]).astype(BF16)
        eext_scr[ts:ts + POOL_HALO, :] = eext_scr[0:POOL_HALO, :]
        for h in range(HEADS):
            sl = slice(h * HEAD_DIM, (h + 1) * HEAD_DIM)
            rt = ret_ref[:, sl]
            rr = lax.rsqrt(jnp.mean(rt * rt, axis=-1, keepdims=True) + RMS_EPS)
            rn = rt * rr
            gh = g_ref[:, sl]
            sg = _sigmoid(gh)
            dy = dcat_r[:, sl]
            dgp_ref[:, sl] = (dy * rn * (sg * (1.0 + gh * (1.0 - sg)))).astype(BF16)
            drn = dy * (gh * sg)
            dret_ref[:, sl] = (rr * (drn - rn * jnp.mean(drn * rn, axis=-1, keepdims=True))).astype(BF16)

    tile = lambda w: pl.BlockSpec((ts, w), lambda i: (nt - 1 - i, 0))
    return _call(
        body, name="mix_bwd", grid=(nt,),
        in_specs=[tile(D_MODEL), tile(POOL_W), tile(RET_W), tile(RET_W), _whole(), _whole(), _whole()],
        out_specs=[tile(RET_W), tile(2 * RET_W),
                   pl.BlockSpec((len(POOL_WINDOWS), HEAD_DIM, HEAD_DIM), lambda i: (0, 0, 0)),
                   pl.BlockSpec((1, POOL_W), lambda i: (0, 0))],
        out_shape=[jax.ShapeDtypeStruct((s, RET_W), BF16), jax.ShapeDtypeStruct((s, 2 * RET_W), BF16),
                   jax.ShapeDtypeStruct((len(POOL_WINDOWS), HEAD_DIM, HEAD_DIM), F32),
                   jax.ShapeDtypeStruct((1, POOL_W), F32)],
        scratch_shapes=[pltpu.VMEM((ts + POOL_HALO, POOL_W), F32)],
        sem=("arbitrary",), operands=(dz1, pooled, ret, g, wout, wpool, pscale), riders=riders,
        after=after,
    )


def _retention_bwd(q, k, v, dret, dgp, states, mask, qd, kd, cosf, sinf, riders=(), after=()):
    s = q.shape[0]
    ns = s // SUPER
    cdec = [gm ** float(SUPER) for gm in _gammas()]

    def body(q_ref, k_ref, v_ref, do_ref, dgp_ref, st_ref, mask_ref, qd_ref, kd_ref, cos_ref, sin_ref,
             dproj_ref, dstate_scr):
        i = pl.program_id(0)

        @pl.when(i == 0)
        def _():
            dstate_scr[...] = jnp.zeros_like(dstate_scr)

        cosf_t = cos_ref[...]
        sinf_t = sin_ref[...]
        for h in range(HEADS):
            sl = slice(h * HEAD_DIM, (h + 1) * HEAD_DIM)
            qh, kh, vh, doh = q_ref[:, sl], k_ref[:, sl], v_ref[:, sl], do_ref[:, sl]
            dscb = (_dot_nt(doh, vh) * mask_ref[0, h]).astype(BF16)
            dsctb = (_dot_nt(vh, doh) * mask_ref[1, h]).astype(BF16)
            sctb = (_dot_nt(kh, qh) * mask_ref[1, h]).astype(BF16)
            stb = st_ref[0, h]
            dst = dstate_scr[h]
            dstb = dst.astype(BF16)
            qdb = (qh.astype(F32) * qd_ref[:, sl]).astype(BF16)
            kdb = (kh.astype(F32) * kd_ref[:, sl]).astype(BF16)
            dq = _dot(dscb, kh) + _dot_nt(doh, stb) * qd_ref[:, sl]
            dk = _dot(dsctb, qh) + _dot_nt(vh, dstb) * kd_ref[:, sl]
            dv = _dot(sctb, doh) + _dot(kdb, dstb)
            dstate_scr[h] = dst * cdec[h] + _dot_tn(qdb, doh)
            lo = h * HEAD_DIM
            dproj_ref[:, lo:lo + HEAD_DIM] = _rope_t(dq, cosf_t, sinf_t).astype(BF16)
            dproj_ref[:, RET_W + lo:RET_W + lo + HEAD_DIM] = _rope_t(dk * K_SCALE, cosf_t, sinf_t).astype(BF16)
            dproj_ref[:, 2 * RET_W + lo:2 * RET_W + lo + HEAD_DIM] = dv.astype(BF16)
        dproj_ref[:, 3 * RET_W:IN_W] = dgp_ref[...]

    tile = lambda w: pl.BlockSpec((SUPER, w), lambda i: (ns - 1 - i, 0))
    return _call(
        body, name="retention_bwd", grid=(ns,),
        in_specs=[tile(RET_W), tile(RET_W), tile(RET_W), tile(RET_W), tile(2 * RET_W),
                  pl.BlockSpec((1, HEADS, HEAD_DIM, HEAD_DIM), lambda i: (ns - 1 - i, 0, 0, 0)),
                  _whole(), _whole(), _whole(), tile(HEAD_DIM), tile(HEAD_DIM)],
        out_specs=[tile(IN_W)],
        out_shape=[jax.ShapeDtypeStruct((s, IN_W), BF16)],
        scratch_shapes=[pltpu.VMEM((HEADS, HEAD_DIM, HEAD_DIM), F32)],
        sem=("arbitrary",), operands=(q, k, v, dret, dgp, states, mask, qd, kd, cosf, sinf), riders=riders,
        after=after,
    )


def _dx(dz1, dproj, win4, ts, riders=(), after=()):
    s = dz1.shape[0]

    def body(dz1_ref, dp_ref, w_ref, dx_ref):
        acc = ALPHA * dz1_ref[...]
        for j in range(N_SHARD):
            acc = acc + _dot_nt(dp_ref[:, j * IN_SH:(j + 1) * IN_SH], w_ref[j])
        dx_ref[...] = acc

    tile = lambda w: pl.BlockSpec((ts, w), lambda i: (i, 0))
    return _call(
        body, name="dx", grid=(s // ts,),
        in_specs=[tile(D_MODEL), tile(IN_W), _whole()],
        out_specs=[tile(D_MODEL)],
        out_shape=[jax.ShapeDtypeStruct((s, D_MODEL), F32)],
        sem=("arbitrary",), operands=(dz1, dproj, win4), riders=riders, after=after,
    )


def _wgrad(a, b, tm, tn, name, stacked, m_outer, riders=(), after=()):
    s, m = a.shape
    n = b.shape[1]

    def body(a_ref, b_ref, o32_ref, o16_ref):
        res = _dot_tn(a_ref[...], b_ref[...])
        o32_ref[...] = res.reshape(o32_ref.shape)
        o16_ref[...] = res.astype(BF16).reshape(o16_ref.shape)

    if m_outer:
        grid, blocks = (m // tm, n // tn), (lambda g0, g1: (g0, g1))
    else:
        grid, blocks = (n // tn, m // tm), (lambda g0, g1: (g1, g0))
    if stacked:
        shape = (n // tn, m, tn)
        ospec = pl.BlockSpec((1, tm, tn), lambda g0, g1: (blocks(g0, g1)[1], blocks(g0, g1)[0], 0))
    else:
        shape = (m, n)
        ospec = pl.BlockSpec((tm, tn), lambda g0, g1: blocks(g0, g1))
    return _call(
        body, name=name, grid=grid,
        in_specs=[pl.BlockSpec((s, tm), lambda g0, g1: (0, blocks(g0, g1)[0])),
                  pl.BlockSpec((s, tn), lambda g0, g1: (0, blocks(g0, g1)[1]))],
        out_specs=[ospec, ospec],
        out_shape=[jax.ShapeDtypeStruct(shape, F32), jax.ShapeDtypeStruct(shape, BF16)],
        sem=("arbitrary", "arbitrary"), operands=(a, b), riders=riders, after=after,
    )


class _NoComm:
    def __init__(self, win4, wout, wup4, wdown):
        self.weights = dict(w_in=win4, w_out=wout, w_up=wup4, w_down=wdown)
        self.grads = {}

    def weight(self, name):
        return self.weights[name]

    def riders(self, call):
        return ()

    def after(self, call):
        return ()

    def landed(self, call, results, outs):
        pass

    def small_gradients(self, loss, small):
        pass

    def gradient(self, name, g32, g16):
        self.grads[name] = (g32, g16)


def _local_step(x, target, cw, cb, wpool, pscale, g1, b1, g2, b2, comm):
    s = x.shape[0]
    ts_a = min(512, s)
    ts_f = min(256, s)
    mask, qd, kd = _decay_tables()
    cosf, sinf = _rope_tables(s)
    wpool_b = wpool.astype(BF16)

    def run(call, fn, *args):
        outs, res = fn(*args, riders=comm.riders(call), after=comm.after(call))
        comm.landed(call, res, outs)
        return outs

    xb, q, k, v, g, pooled, cat = run("proj_pool", _proj_pool, x, comm.weight("w_in"), cosf, sinf, wpool_b,
                                      pscale, ts_a)
    ret, cat, states = run("retention_fwd", _retention_fwd, q, k, v, g, cat, mask, qd, kd)
    wout = comm.weight("w_out")
    xhat1, rstd1, h1b = run("outproj_ln1", _outproj_ln1, x, cat, wout, g1, b1, ts_a)
    wup4, wdown = comm.weight("w_up"), comm.weight("w_down")
    ub, act, sd, dz2, dz2b, loss, dg2, db2 = _ffn_fwd_loss(xhat1, h1b, target, wup4, wdown, cw, cb, g1, b1, g2, b2,
                                                           ts_f)

    dub, dz1, dz1b, dg1, db1, dcw, dcb = _ffn_bwd(dz2, dz2b, ub, sd, xhat1, rstd1, wup4, wdown, cw, g1, ts_f)
    half = D_MODEL // 2
    comm.gradient("w_up", *run("wgrad_up", _wgrad, h1b, dub, half, UP_SH, "wgrad_up", True, False))
    comm.gradient("w_out", *run("wgrad_out", _wgrad, cat, dz1b, D_MODEL, half, "wgrad_out", False, True))
    comm.gradient("w_down", *run("wgrad_down", _wgrad, act, dz2b, D_FF // 2, half, "wgrad_down", False, True))
    dret, dgp, dwp, dps = run("mix_bwd", _mix_bwd, dz1b, pooled, ret, g, wout, wpool_b, pscale, ts_a)
    small = dict(w_pool=dwp, pool_scale=dps, ln1_g=dg1, ln1_b=db1, conv_w=dcw, conv_b=dcb,
                 ln2_g=dg2, ln2_b=db2)
    comm.small_gradients(loss, small)
    mask_both = jnp.stack([mask, jnp.swapaxes(mask, 1, 2)])
    dproj, = run("retention_bwd", _retention_bwd, q, k, v, dret, dgp, states, mask_both, qd, kd, cosf, sinf)
    comm.gradient("w_in", *run("wgrad_in", _wgrad, xb, dproj, D_MODEL, IN_SH, "wgrad_in", True, True))
    (grad_x,), _ = _dx(dz1, dproj, comm.weight("w_in"), ts_a, after=comm.after("dx"))
    return loss, grad_x, small


CAST_ROWS = 64
SHARD_SHAPES = ((D_MODEL, IN_SH), (OUT_SH, D_MODEL), (D_MODEL, UP_SH), (DOWN_SH, D_MODEL))
N_BIG = len(SHARD_SHAPES)
CW_SHARD = (3, DOWN_SH)


def _mesh_pos():
    return lax.axis_index("x"), lax.axis_index("y"), lax.axis_index("c")


def _other_chips(x, y):
    return [(1 - x, y), (x, 1 - y), (1 - x, 1 - y)]


def _half_rows(w, which):
    hr = SHARD_SHAPES[w][0] // 2
    return pl.ds(pl.multiple_of(which * hr, 16), hr)


def _gather_weights(shards, cw_shard, full):
    def body(*refs):
        in_refs = refs[:N_BIG]
        cw_ref = refs[N_BIG]
        out_refs = refs[N_BIG + 1:2 * N_BIG + 1]
        cwo_ref = refs[2 * N_BIG + 1]
        stage = refs[2 * N_BIG + 2:3 * N_BIG + 2]
        raw = refs[3 * N_BIG + 2:4 * N_BIG + 2 - len(full)]
        send_sems, recv_sems, fsend_sems, frecv_sems, cw_send, cw_recv, local_sems, load_sems = \
            refs[4 * N_BIG + 2 - len(full):]
        x, y, c = _mesh_pos()
        j0 = 2 * x + y
        chips = _other_chips(x, y)

        fetched = [w for w in range(N_BIG) if w not in full]
        f32 = {w: in_refs[w] for w in full}
        loads = []
        for n, w in enumerate(fetched):
            f32[w] = raw[n]
            loads.append(pltpu.make_async_copy(in_refs[w], raw[n], load_sems.at[n]))
            loads[-1].start()

        def cast_to_stage(w):
            def cast(i, carry):
                rows = pl.ds(pl.multiple_of(i * CAST_ROWS, CAST_ROWS), CAST_ROWS)
                stage[w][rows, :] = f32[w][rows, :].astype(BF16)
                return carry
            lax.fori_loop(0, SHARD_SHAPES[w][0] // CAST_ROWS, cast, 0)

        for w in full:
            cast_to_stage(w)

        jx, jy, jd = 2 * (1 - x) + y, 2 * x + (1 - y), 2 * (1 - x) + (1 - y)
        neighbours = [((1 - x, y, c), jx), ((x, 1 - y, c), jy)]
        passed = jnp.where(c == 0, jx, jy)
        pass_to = (jnp.where(c == 0, x, 1 - x), jnp.where(c == 0, 1 - y, y), c)

        def nbr(w, k, block):
            return pltpu.make_async_remote_copy(
                src_ref=stage[w].at[_half_rows(w, c), :], dst_ref=out_refs[w].at[block, _half_rows(w, c), :],
                send_sem=send_sems.at[w, k], recv_sem=recv_sems.at[w, k],
                device_id=neighbours[k][0], device_id_type=MESH)

        def relay(w, block):
            return pltpu.make_async_remote_copy(
                src_ref=out_refs[w].at[passed, _half_rows(w, c), :],
                dst_ref=out_refs[w].at[block, _half_rows(w, c), :],
                send_sem=send_sems.at[w, 2], recv_sem=recv_sems.at[w, 2],
                device_id=pass_to, device_id_type=MESH)

        def d2d(w, k, block, half):
            return pltpu.make_async_remote_copy(
                src_ref=out_refs[w].at[block, _half_rows(w, half), :],
                dst_ref=out_refs[w].at[block, _half_rows(w, half), :],
                send_sem=fsend_sems.at[w, k], recv_sem=frecv_sems.at[w, k],
                device_id=(x, y, 1 - c), device_id_type=MESH)

        def conv(k, block):
            chip = chips[k]
            return pltpu.make_async_remote_copy(
                src_ref=cw_ref, dst_ref=cwo_ref.at[block], send_sem=cw_send.at[k], recv_sem=cw_recv.at[k],
                device_id=(chip[0], chip[1], c), device_id_type=MESH)

        sent = [nbr(w, k, j0) for w in full for k in range(2)] + [conv(k, j0) for k in range(3)]
        for cp in sent:
            cp.start()
        for n, w in enumerate(fetched):
            loads[n].wait()
            cast_to_stage(w)
        local = [pltpu.make_async_copy(stage[w], out_refs[w].at[j0], local_sems.at[w]) for w in range(N_BIG)]
        local.append(pltpu.make_async_copy(cw_ref, cwo_ref.at[j0], local_sems.at[N_BIG]))
        for cp in local:
            cp.start()
        for w in full:
            for k, (_, block) in enumerate(neighbours):
                nbr(w, k, block).wait_recv()
            later = [relay(w, passed)] + [d2d(w, k, block, c) for k, (_, block) in enumerate(neighbours)]
            for cp in later:
                cp.start()
            sent += later
        for w in full:
            relay(w, jd).wait_recv()
            fw = d2d(w, 2, jd, c)
            fw.start()
            sent.append(fw)
        for w in full:
            for k, block in enumerate([jx, jy, jd]):
                d2d(w, k, block, 1 - c).wait_recv()
        for k, chip in enumerate(chips):
            conv(k, 2 * chip[0] + chip[1]).wait_recv()
        for cp in sent:
            cp.wait_send()
        for cp in local:
            cp.wait()

    out_shape = [jax.ShapeDtypeStruct((N_SHARD,) + shp, BF16) for shp in SHARD_SHAPES]
    out_shape.append(jax.ShapeDtypeStruct((N_SHARD,) + CW_SHARD, F32))
    return pl.pallas_call(
        body, name="gather_weights",
        in_specs=[_whole() if w in full else HBM_SPEC for w in range(N_BIG)] + [_whole()],
        out_specs=[HBM_SPEC] * (N_BIG + 1),
        out_shape=out_shape,
        scratch_shapes=[pltpu.VMEM(shp, BF16) for shp in SHARD_SHAPES]
        + [pltpu.VMEM(shp, F32) for w, shp in enumerate(SHARD_SHAPES) if w not in full] + [
            pltpu.SemaphoreType.DMA((N_BIG, 3)), pltpu.SemaphoreType.DMA((N_BIG, 3)),
            pltpu.SemaphoreType.DMA((N_BIG, 3)), pltpu.SemaphoreType.DMA((N_BIG, 3)),
            pltpu.SemaphoreType.DMA((3,)), pltpu.SemaphoreType.DMA((3,)),
            pltpu.SemaphoreType.DMA((N_BIG + 1,)), pltpu.SemaphoreType.DMA((N_BIG - len(full),))],
        compiler_params=pltpu.CompilerParams(vmem_limit_bytes=VMEM_LIMIT),
    )(*shards, cw_shard)


def _gather_rider(arrays, ops):
    ws = sorted(arrays)

    def make(inplace, srcs, lands, send_sems, recv_sems):
        del srcs, lands
        x, y, c = _mesh_pos()
        j0, jx, jy, jd = 2 * x + y, 2 * (1 - x) + y, 2 * x + (1 - y), 2 * (1 - x) + (1 - y)
        x_nbr, y_nbr, sibling = (1 - x, y, c), (x, 1 - y, c), (x, y, 1 - c)
        starts, waits = [], []
        for n, (kind, w, (r0, nr)) in enumerate(ops):
            ref = inplace[ws.index(w)]
            hr = SHARD_SHAPES[w][0] // 2
            rows = lambda core: pl.ds(pl.multiple_of(core * hr + r0, 16), nr)
            mine, theirs = rows(c), rows(1 - c)
            if kind == "ici":
                moves = [(ref.at[j0, mine, :], x_nbr, ref.at[jx, mine, :]),
                         (ref.at[j0, mine, :], y_nbr, ref.at[jy, mine, :]),
                         (ref.at[j0, mine, :], (1 - x, 1 - y, c), ref.at[jd, mine, :])]
            elif kind == "nbr":
                moves = [(ref.at[j0, mine, :], x_nbr, ref.at[jx, mine, :]),
                         (ref.at[j0, mine, :], y_nbr, ref.at[jy, mine, :])]
            elif kind == "relay":
                passed = jnp.where(c == 0, jx, jy)
                to = (jnp.where(c == 0, x, 1 - x), jnp.where(c == 0, 1 - y, y), c)
                moves = [(ref.at[passed, mine, :], to, ref.at[jd, mine, :])]
            else:
                blocks = dict(d2d=[jx, jy, jd], d2d_nbr=[jx, jy], d2d_diag=[jd])[kind]
                moves = [(ref.at[b, mine, :], sibling, ref.at[b, theirs, :]) for b in blocks]
            for k, (src, to, landing) in enumerate(moves):
                sems = dict(send_sem=send_sems.at[3 * n + k], recv_sem=recv_sems.at[3 * n + k],
                            device_id=to, device_id_type=MESH)
                send = pltpu.make_async_remote_copy(src_ref=src, dst_ref=src, **sems)
                arrival = pltpu.make_async_remote_copy(src_ref=src, dst_ref=landing, **sems)
                starts.append(send)
                waits += [arrival.wait_recv, send.wait_send]
        return starts, waits

    return _Rider([arrays[w] for w in ws], [], [], 3 * len(ops), make)


def _whole_half(w):
    return (0, SHARD_SHAPES[w][0] // 2)


def _pair_rider(ws, g16s):
    def make(inplace, srcs, lands, send_sems, recv_sems):
        del inplace
        x, y, c = _mesh_pos()
        copies = [pltpu.make_async_remote_copy(
            src_ref=srcs[i].at[:, _half_rows(w, 1 - c), :], dst_ref=lands[i],
            send_sem=send_sems.at[i], recv_sem=recv_sems.at[i], device_id=(x, y, 1 - c), device_id_type=MESH)
            for i, w in enumerate(ws)]
        return copies, [cp.wait for cp in copies]

    lands = [jax.ShapeDtypeStruct((N_SHARD, SHARD_SHAPES[w][0] // 2, SHARD_SHAPES[w][1]), BF16) for w in ws]
    return _Rider([], g16s, lands, len(ws), make)


def _chip_rider(ws, p16s):
    def make(inplace, srcs, lands, send_sems, recv_sems):
        del inplace
        x, y, c = _mesh_pos()
        copies = []
        for i in range(len(ws)):
            for k, chip in enumerate(_other_chips(x, y)):
                copies.append(pltpu.make_async_remote_copy(
                    src_ref=srcs[i].at[2 * chip[0] + chip[1]], dst_ref=lands[i].at[k],
                    send_sem=send_sems.at[3 * i + k], recv_sem=recv_sems.at[3 * i + k],
                    device_id=(chip[0], chip[1], c), device_id_type=MESH))
        return copies, [cp.wait for cp in copies]

    lands = [jax.ShapeDtypeStruct((3, SHARD_SHAPES[w][0] // 2, SHARD_SHAPES[w][1]), BF16) for w in ws]
    return _Rider([], p16s, lands, 3 * len(ws), make)


def _final_rider(halves):
    def make(inplace, srcs, lands, send_sems, recv_sems):
        del inplace
        x, y, c = _mesh_pos()
        copies = [pltpu.make_async_remote_copy(
            src_ref=srcs[i], dst_ref=lands[i], send_sem=send_sems.at[i], recv_sem=recv_sems.at[i],
            device_id=(x, y, 1 - c), device_id_type=MESH) for i in range(len(halves))]
        return copies, [cp.wait for cp in copies]

    return _Rider([], halves, [jax.ShapeDtypeStruct(h.shape, h.dtype) for h in halves], len(halves), make)


def _comm_only(name, riders):
    _, res = _call(lambda: None, name=name, grid=(), in_specs=[], out_specs=[], out_shape=[], operands=(),
                   riders=riders)
    return res


class _SemList:
    def __init__(self, refs):
        self.at = list(refs)


def _merged_rider(riders):
    srcs = [a for r in riders for a in r.srcs]
    lands = [a for r in riders for a in r.lands]

    def make(inplace, src_refs, land_refs, send_sems, recv_sems):
        starts, waits = [], []
        s0 = l0 = c0 = 0
        for r in riders:
            part = r.make(inplace, src_refs[s0:s0 + len(r.srcs)], land_refs[l0:l0 + len(r.lands)],
                          _SemList(send_sems.at[c0:c0 + r.n_copies]), _SemList(recv_sems.at[c0:c0 + r.n_copies]))
            starts += part[0]
            waits += part[1]
            s0, l0, c0 = s0 + len(r.srcs), l0 + len(r.lands), c0 + r.n_copies
        return starts, waits

    return _Rider([], srcs, lands, sum(r.n_copies for r in riders), make)


def _split_start(name, rider):
    assert not rider.inplace
    ns, nl, n = len(rider.srcs), len(rider.lands), rider.n_copies

    def body(*refs):
        srcs, lands = refs[:ns], refs[ns:ns + nl]
        sems = refs[ns + nl:ns + nl + 2 * n]
        token = refs[-1]
        starts, _ = rider.make([], srcs, lands, _SemList(sems[:n]), _SemList(sems[n:]))
        for cp in starts:
            cp.start()
        token[...] = jnp.zeros_like(token)

    buffers = [pltpu.with_memory_space_constraint(a, pltpu.HBM) for a in rider.srcs]
    buffers += [pltpu.with_memory_space_constraint(lax.empty(s.shape, s.dtype), pltpu.HBM) for s in rider.lands]
    hbm = pl.BlockSpec(memory_space=pltpu.HBM)
    sem = pl.BlockSpec(memory_space=pltpu.SEMAPHORE)
    outs = pl.pallas_call(
        body, name=name,
        out_shape=tuple([pltpu.SemaphoreType.DMA(())] * (2 * n) + [pltpu.HBM(b.shape, b.dtype) for b in buffers]
                        + [jax.ShapeDtypeStruct((8, 128), F32)]),
        in_specs=[hbm] * (ns + nl),
        out_specs=tuple([sem] * (2 * n) + [hbm] * (ns + nl) + [_whole()]),
        input_output_aliases={i: 2 * n + i for i in range(ns + nl)},
        compiler_params=pltpu.CompilerParams(has_side_effects=pltpu.SideEffectType.DATAFLOW_SIDE_EFFECTING),
    )(*buffers)
    return (rider, outs[:2 * n], outs[2 * n:2 * n + ns + nl]), outs[-1]


def _split_parts(state, riders):
    merged, sems, buffers = state
    n, ns = merged.n_copies, len(merged.srcs)
    parts, s0, l0, c0 = [], 0, 0, 0
    for r in riders:
        parts.append((r, list(sems[c0:c0 + r.n_copies]) + list(sems[n + c0:n + c0 + r.n_copies]),
                      list(buffers[s0:s0 + len(r.srcs)]) + list(buffers[ns + l0:ns + l0 + len(r.lands)])))
        s0, l0, c0 = s0 + len(r.srcs), l0 + len(r.lands), c0 + r.n_copies
    return parts


def _split_wait(name, state, after):
    rider, sems, buffers = state
    ns, nl, n = len(rider.srcs), len(rider.lands), rider.n_copies

    def body(*refs):
        srcs, lands = refs[:ns], refs[ns:ns + nl]
        sem_refs = refs[ns + nl:ns + nl + 2 * n]
        _, waits = rider.make([], srcs, lands, _SemList(sem_refs[:n]), _SemList(sem_refs[n:]))
        for wait in waits:
            wait()

    hbm = pl.BlockSpec(memory_space=pltpu.HBM)
    sem = pl.BlockSpec(memory_space=pltpu.SEMAPHORE)
    outs = pl.pallas_call(
        body, name=name,
        out_shape=tuple(pltpu.HBM(b.shape, b.dtype) for b in buffers),
        in_specs=[hbm] * (ns + nl) + [sem] * (2 * n) + [HBM_SPEC],
        out_specs=tuple([hbm] * (ns + nl)),
        input_output_aliases={i: i for i in range(ns + nl)},
        compiler_params=pltpu.CompilerParams(has_side_effects=pltpu.SideEffectType.DATAFLOW_SIDE_EFFECTING),
    )(*buffers, *sems, after)
    return list(outs[:ns]), list(outs[ns:])


def _pair_sum(pos, ws, g32s, recvs):
    n = len(ws)

    def body(pos_ref, *refs):
        g_refs, r_refs = refs[:n], refs[n:2 * n]
        p32_refs, p16_refs = refs[2 * n:3 * n], refs[3 * n:]
        for i in range(n):
            tot = g_refs[i][...] + r_refs[i][...].astype(F32)
            p16_refs[i][...] = tot.astype(BF16)

            @pl.when(pl.program_id(0) == pos_ref[1])
            def _(i=i, tot=tot):
                p32_refs[i][...] = tot

    halves = [(SHARD_SHAPES[w][0] // 2, SHARD_SHAPES[w][1]) for w in ws]
    own = [pl.BlockSpec((None, None) + h, lambda j, pos_ref: (j, pos_ref[0], 0, 0)) for h in halves]
    blk = [pl.BlockSpec((None,) + h, lambda j, pos_ref: (j, 0, 0)) for h in halves]
    mine = [pl.BlockSpec(h, lambda j, pos_ref: (0, 0)) for h in halves]
    g4 = [g.reshape((N_SHARD, 2) + h) for g, h in zip(g32s, halves)]
    outs = pl.pallas_call(
        body, name="pair_sum_" + "_".join(str(w) for w in ws),
        grid_spec=pltpu.PrefetchScalarGridSpec(
            num_scalar_prefetch=1, grid=(N_SHARD,), in_specs=own + blk, out_specs=mine + blk),
        out_shape=[jax.ShapeDtypeStruct(h, F32) for h in halves]
        + [jax.ShapeDtypeStruct((N_SHARD,) + h, BF16) for h in halves],
        compiler_params=_params(("arbitrary",)),
    )(pos, *g4, *recvs)
    return outs[:n], outs[n:]


def _chip_sum(pos, p32s, recvs):
    parts = 2

    def body(pos_ref, *refs):
        del pos_ref
        p_refs, r_refs, f_refs = refs[:N_BIG], refs[N_BIG:2 * N_BIG], refs[2 * N_BIG:]
        for w in range(N_BIG):
            f_refs[w][...] = ((p_refs[w][...] + r_refs[w][0].astype(F32)) + r_refs[w][1].astype(F32)) \
                + r_refs[w][2].astype(F32)

    quarters = [(r // 2 // parts, cc) for r, cc in SHARD_SHAPES]
    own = [pl.BlockSpec(qt, lambda i, pos_ref: (i, 0)) for qt in quarters]
    rcv = [pl.BlockSpec((3,) + qt, lambda i, pos_ref: (0, i, 0)) for qt in quarters]
    out = [pl.BlockSpec(qt, lambda i, pos_ref: (i, 0)) for qt in quarters]
    return pl.pallas_call(
        body, name="chip_sum",
        grid_spec=pltpu.PrefetchScalarGridSpec(
            num_scalar_prefetch=1, grid=(parts,), in_specs=own + rcv, out_specs=out),
        out_shape=[jax.ShapeDtypeStruct((r // 2, cc), F32) for r, cc in SHARD_SHAPES],
        compiler_params=_params(("arbitrary",)),
    )(pos, *p32s, *recvs)


def _adamw(w, g, m, v):
    m_new = ADAM_B1 * m + (1.0 - ADAM_B1) * g
    v_new = ADAM_B2 * v + (1.0 - ADAM_B2) * (g * g)
    m_hat = m_new / (1.0 - ADAM_B1 ** ADAM_STEP)
    v_hat = v_new / (1.0 - ADAM_B2 ** ADAM_STEP)
    delta = -ADAM_LR * (m_hat / (jnp.sqrt(v_hat) + ADAM_EPS) + ADAM_WD * w)
    return delta, m_new, v_new


def _adam_half(name, which, grads, ws, ms, vs, into=None):
    nb = 4

    def body(which_ref, *refs):
        del which_ref
        groups = [refs[i * N_BIG:(i + 1) * N_BIG] for i in range(4)]
        g_refs, w_refs, m_refs, v_refs = groups
        go_refs, do_refs, mo_refs, vo_refs = [refs[len(refs) - (4 - i) * N_BIG:len(refs) - (3 - i) * N_BIG]
                                              for i in range(4)]
        for w in range(N_BIG):
            g = g_refs[w][...]
            delta, m_new, v_new = _adamw(w_refs[w][...], g, m_refs[w][...], v_refs[w][...])
            go_refs[w][...] = g
            do_refs[w][...] = delta
            mo_refs[w][...] = m_new
            vo_refs[w][...] = v_new

    blocks = [(r // 2 // nb, cc) for r, cc in SHARD_SHAPES]
    half = [pl.BlockSpec(b, lambda i, which_ref: (i, 0)) for b in blocks]
    full = [pl.BlockSpec((None,) + b, lambda i, which_ref: (0, which_ref[0] * nb + i, 0)) for b in blocks]
    shapes = [jax.ShapeDtypeStruct((1,) + shp, F32) for shp in SHARD_SHAPES]
    carried = [] if into is None else [a for kind in into for a in kind]
    first = 1 + 4 * N_BIG
    outs = pl.pallas_call(
        body, name=name,
        grid_spec=pltpu.PrefetchScalarGridSpec(
            num_scalar_prefetch=1, grid=(nb,), in_specs=half + full * 3 + [HBM_SPEC] * len(carried),
            out_specs=full * 4),
        out_shape=shapes * 4,
        input_output_aliases={first + i: i for i in range(len(carried))},
        compiler_params=_params(("arbitrary",)),
    )(which, *grads, *ws, *ms, *vs, *carried)
    return [outs[i * N_BIG:(i + 1) * N_BIG] for i in range(4)]


SMALL_ROWS = 8
ROW_CONV_B, ROW_POOL_SCALE, ROW_LN1_G, ROW_LN1_B, ROW_LN2_G, ROW_LN2_B, ROW_LOSS = range(7)
SMALL_VECS = ((ROW_CONV_B, D_FF), (ROW_POOL_SCALE, POOL_W), (ROW_LN1_G, D_MODEL), (ROW_LN1_B, D_MODEL),
              (ROW_LN2_G, D_MODEL), (ROW_LN2_B, D_MODEL))


def _small_pack(loss, vec_grads):
    def body(*refs):
        loss_ref, gvec, out_ref = refs[0], refs[1:-1], refs[-1]
        out_ref[...] = jnp.zeros_like(out_ref)
        for (row, n), ref in zip(SMALL_VECS, gvec):
            out_ref[row:row + 1, 0:n] = ref[...]
        out_ref[ROW_LOSS:ROW_LOSS + 1, 0:HEAD_DIM] = jnp.broadcast_to(loss_ref[...], (1, HEAD_DIM))

    return pl.pallas_call(
        body, name="small_pack", in_specs=[_whole()] * (1 + len(vec_grads)), out_specs=_whole(),
        out_shape=jax.ShapeDtypeStruct((SMALL_ROWS, D_FF), F32),
    )(loss, *vec_grads)


def _small_pair_sum(own, sibling):
    n = len(own)

    def body(*refs):
        x, y, _ = _mesh_pos()
        for i in range(n):
            refs[2 * n + i][2 * x + y] = refs[i][...] + refs[n + i][...]

    return pl.pallas_call(
        body, name="small_pair_sum", in_specs=[_whole()] * (2 * n), out_specs=[_whole()] * n,
        out_shape=[jax.ShapeDtypeStruct((N_SHARD,) + a.shape, F32) for a in own],
        compiler_params=pltpu.CompilerParams(vmem_limit_bytes=VMEM_LIMIT),
    )(*own, *sibling)


def _small_chip_rider(gathered):
    n = len(gathered)

    def make(inplace, srcs, lands, send_sems, recv_sems):
        del inplace, lands
        x, y, c = _mesh_pos()
        j0 = 2 * x + y
        starts, waits = [], []
        for i in range(n):
            for k, chip in enumerate(_other_chips(x, y)):
                sems = dict(send_sem=send_sems.at[3 * i + k], recv_sem=recv_sems.at[3 * i + k],
                            device_id=(chip[0], chip[1], c), device_id_type=MESH)
                send = pltpu.make_async_remote_copy(src_ref=srcs[i].at[j0], dst_ref=srcs[i].at[j0], **sems)
                arrival = pltpu.make_async_remote_copy(
                    src_ref=srcs[i].at[j0], dst_ref=srcs[i].at[2 * chip[0] + chip[1]], **sems)
                starts.append(send)
                waits += [arrival.wait_recv, send.wait_send]
        return starts, waits

    return _Rider([], gathered, [], 3 * n, make)


def _small_adam(all_a, all_b, all_c, wp, cwp, vec_ws, m_wp, m_cwp, vec_ms, v_wp, v_cwp, vec_vs):
    nv = len(SMALL_VECS)
    np_ = 2 + nv

    def body(*refs):
        all_a_ref, all_b_ref, all_c_ref = refs[0:3]
        w_all, m_all, v_all = (refs[3 + i * np_:3 + (i + 1) * np_] for i in range(3))
        loss_out = refs[3 + 3 * np_]
        outs = refs[4 + 3 * np_:]
        x, y, _ = _mesh_pos()
        j0 = 2 * x + y
        tot_a = ((all_a_ref[0] + all_a_ref[1]) + all_a_ref[2]) + all_a_ref[3]
        tot_b = ((all_b_ref[0] + all_b_ref[1]) + all_b_ref[2]) + all_b_ref[3]
        tot_c = ((all_c_ref[0, j0] + all_c_ref[1, j0]) + all_c_ref[2, j0]) + all_c_ref[3, j0]
        loss_out[...] = tot_b[ROW_LOSS:ROW_LOSS + 1, 0:1]
        grads = [tot_a, tot_c] + [tot_b[row:row + 1, 0:n] for row, n in SMALL_VECS]
        for p in range(np_):
            delta, m_new, v_new = _adamw(w_all[p][...], grads[p], m_all[p][...], v_all[p][...])
            outs[p][...] = grads[p]
            outs[np_ + p][...] = delta
            outs[2 * np_ + p][...] = m_new
            outs[3 * np_ + p][...] = v_new

    pshapes = [wp.shape, CW_SHARD] + [wv.shape for wv in vec_ws]
    out_shape = [jax.ShapeDtypeStruct((1, 1), F32)] + [jax.ShapeDtypeStruct(s, F32) for s in pshapes] * 4
    outs = pl.pallas_call(
        body, name="small_adam",
        in_specs=[_whole()] * (3 + 3 * np_), out_specs=[_whole()] * len(out_shape), out_shape=out_shape,
        compiler_params=pltpu.CompilerParams(vmem_limit_bytes=VMEM_LIMIT),
    )(all_a, all_b, all_c, wp, cwp, *vec_ws, m_wp, m_cwp, *vec_ms, v_wp, v_cwp, *vec_vs)
    return outs[0], [outs[1 + i * np_:1 + (i + 1) * np_] for i in range(4)]


def kernel(x, w_in, w_pool, pool_scale, w_out, ln1_g, ln1_b, w_up, conv_w, conv_b, w_down, ln2_g, ln2_b, loss_target, m_w_in, m_w_pool, m_pool_scale, m_w_out, m_ln1_g, m_ln1_b, m_w_up, m_conv_w, m_conv_b, m_w_down, m_ln2_g, m_ln2_b, v_w_in, v_w_pool, v_pool_scale, v_w_out, v_ln1_g, v_ln1_b, v_w_up, v_conv_w, v_conv_b, v_w_down, v_ln2_g, v_ln2_b):
    pos = jnp.stack([lax.axis_index("c"), 2 * lax.axis_index("x") + lax.axis_index("y")]).astype(jnp.int32)
    order = ("w_in", "w_out", "w_up", "w_down")
    w_in_i, w_out_i, w_up_i, w_down_i = range(N_BIG)
    vec_names = ("conv_b", "pool_scale", "ln1_g", "ln1_b", "ln2_g", "ln2_b")

    gathered = _gather_weights([w_in[0], w_out[0], w_up[0], w_down[0]], conv_w[0], (w_in_i,))
    cw_full = jnp.transpose(gathered[N_BIG], (1, 0, 2)).reshape(3, D_FF)
    up_a, up_b, up_c = (0, 176), (176, 176), (352, 160)
    assert up_c[0] + up_c[1] == SHARD_SHAPES[w_up_i][0] // 2

    class MeshComm:
        def __init__(self):
            self.w = {i: gathered[i] for i in range(N_BIG)}
            self.g32, self.g16, self.p32, self.p16, self.recv_b = {}, {}, {}, {}, {}
            self.up_complete = False
            self.tokens, self.chips = {}, []

        def weight(self, name):
            i = order.index(name)
            if name == "w_up" and not self.up_complete:
                (arrs, _), = _comm_only("gather_up_last", [_gather_rider(
                    {i: self.w[i]}, [("d2d_diag", i, up_b), ("d2d", i, up_c)])])
                self.w[i], self.up_complete = arrs[0], True
            full = self.w[i]
            return full.reshape(-1, full.shape[-1]) if name in ("w_out", "w_down") else full

        def _gather(self, ws, ops):
            return _gather_rider({w: self.w[w] for w in ws}, ops), ("w", ws)

        def _pair(self, ws):
            return _pair_rider(ws, [self.g16[w] for w in ws]), ("recv_a", ws)

        def _chip(self, ws):
            return _chip_rider(ws, [self.p16[w] for w in ws]), ("recv_b", ws)

        def plan(self, call):
            out_all, down_all = _whole_half(w_out_i), _whole_half(w_down_i)
            if call == "proj_pool":
                return [self._gather([w_out_i, w_up_i, w_down_i],
                                     [("ici", w_out_i, out_all), ("nbr", w_down_i, down_all),
                                      ("nbr", w_up_i, up_a)])]
            if call == "retention_fwd":
                return [self._gather([w_out_i, w_up_i, w_down_i],
                                     [("d2d", w_out_i, out_all),
                                      ("relay", w_down_i, down_all), ("d2d_nbr", w_down_i, down_all),
                                      ("relay", w_up_i, up_a), ("d2d_nbr", w_up_i, up_a), ("nbr", w_up_i, up_b)])]
            if call == "outproj_ln1":
                return [self._gather([w_up_i, w_down_i],
                                     [("d2d_diag", w_down_i, down_all), ("d2d_diag", w_up_i, up_a),
                                      ("relay", w_up_i, up_b), ("d2d_nbr", w_up_i, up_b), ("ici", w_up_i, up_c)])]
            return []

        def after(self, call):
            return tuple(self.tokens.pop(call, ()))

        def riders(self, call):
            self.pending = self.plan(call)
            return [r for r, _ in self.pending]

        def _start(self, name, rider, before):
            state, token = _split_start(name, rider)
            self.tokens.setdefault(before, []).append(token)
            return state

        def _finish_pair(self, name, state, ws, after):
            _, lands = _split_wait(name, state, after)
            self._finish_sum(ws, lands)

        def landed(self, call, results, outs):
            for (_, (slot, ws)), (inplace, lands) in zip(self.pending, results):
                for w, arr in zip(ws, inplace if len(inplace) else lands):
                    getattr(self, slot)[w] = arr
            if call == "wgrad_out":
                self._finish_pair("pair_exchange_up_wait", self.pair_up, [w_up_i], outs[1])
                self.chips.append(([w_up_i], self._start(
                    "chip_exchange_up_start", self._chip([w_up_i])[0], "wgrad_down")))
            if call == "mix_bwd":
                ws = [w_out_i, w_down_i]
                self._finish_pair("pair_exchange_out_down_wait", self.pair_out_down, ws, outs[0])
            if call == "retention_bwd":
                own, sibling = _split_wait("small_pair_wait", self.small_pair, outs[0])
                self.small_chip = self._start(
                    "small_chip_start", _small_chip_rider(_small_pair_sum(own, sibling)), "wgrad_in")

        def small_gradients(self, loss, small):
            dcw4 = jnp.transpose(small["conv_w"].reshape(3, N_SHARD, DOWN_SH), (1, 0, 2))
            own = [small["w_pool"], _small_pack(loss, [small[n] for n in vec_names]), dcw4]
            ws = [w_out_i, w_down_i]
            parts = [self._chip(ws)[0], _final_rider(own)]
            chip, self.small_pair = _split_parts(
                self._start("chip_out_down_small_pair_start", _merged_rider(parts), "retention_bwd"), parts)
            self.chips.append((ws, chip))

        def gradient(self, name, g32, g16):
            w = order.index(name)
            shape = (N_SHARD,) + SHARD_SHAPES[w]
            self.g32[w], self.g16[w] = g32.reshape(shape), g16.reshape(shape)
            if name == "w_up":
                self.pair_up = self._start("pair_exchange_up_start", self._pair([w])[0], "wgrad_out")
            if name == "w_down":
                self.pair_out_down = self._start("pair_exchange_out_down_start",
                                                 self._pair([w_out_i, w_down_i])[0], "mix_bwd")
            if name == "w_in":
                (_, lands), = _comm_only("pair_exchange_in", [self._pair([w])[0]])
                self._finish_sum([w], lands)
                self.chips.append(([w], self._start("chip_exchange_in_start", self._chip([w])[0], "dx")))

        def _finish_sum(self, ws, lands):
            p32s, p16s = _pair_sum(pos, ws, [self.g32[w] for w in ws], lands)
            for w, p32, p16 in zip(ws, p32s, p16s):
                self.p32[w], self.p16[w] = p32, p16

        def finish(self, after):
            for n, (ws, state) in enumerate(self.chips):
                _, lands = _split_wait("chip_exchange_wait_%d" % n, state, after)
                for w, arr in zip(ws, lands):
                    self.recv_b[w] = arr
            return _split_wait("small_chip_wait", self.small_chip, after)[0]

    comm = MeshComm()
    loss, grad_x, small = _local_step(x[0], loss_target[0], cw_full, conv_b, w_pool[0], pool_scale,
                                      ln1_g, ln1_b, ln2_g, ln2_b, comm)

    given = dict(w_pool=w_pool, pool_scale=pool_scale, ln1_g=ln1_g, ln1_b=ln1_b, conv_w=conv_w, conv_b=conv_b,
                 ln2_g=ln2_g, ln2_b=ln2_b)
    given_m = dict(w_pool=m_w_pool, pool_scale=m_pool_scale, ln1_g=m_ln1_g, ln1_b=m_ln1_b, conv_w=m_conv_w,
                   conv_b=m_conv_b, ln2_g=m_ln2_g, ln2_b=m_ln2_b)
    given_v = dict(w_pool=v_w_pool, pool_scale=v_pool_scale, ln1_g=v_ln1_g, ln1_b=v_ln1_b, conv_w=v_conv_w,
                   conv_b=v_conv_b, ln2_g=v_ln2_g, ln2_b=v_ln2_b)
    args = []
    for src in (given, given_m, given_v):
        args += [src["w_pool"][0], src["conv_w"][0], [src[n] for n in vec_names]]
    small_sums = comm.finish(grad_x)
    loss_tot, small_out = _small_adam(*small_sums, *args)
    every = range(N_BIG)
    mine = _chip_sum(pos, [comm.p32[w] for w in every], [comm.recv_b[w] for w in every])
    final_state, _ = _split_start("pair_exchange_f32_start", _final_rider(mine))
    mine = final_state[2][:N_BIG]
    big = ([w_in, w_out, w_up, w_down], [m_w_in, m_w_out, m_w_up, m_w_down], [v_w_in, v_w_out, v_w_up, v_w_down])
    own_half = _adam_half("adam_own_half", pos[0:1], mine, *big)
    _, theirs = _split_wait("pair_exchange_f32_wait", final_state, own_half[0][0])
    big_out = _adam_half("adam_other_half", 1 - pos[0:1], theirs, *big, into=own_half)

    names = ("w_in", "w_pool", "pool_scale", "w_out", "ln1_g", "ln1_b", "w_up", "conv_w", "conv_b", "w_down",
             "ln2_g", "ln2_b")
    small_names = ("w_pool", "conv_w") + vec_names
    result = [loss_tot.reshape(()), grad_x[None]]
    for kind in range(4):
        for n in names:
            if n in order:
                result.append(big_out[kind][order.index(n)])
            else:
                val = small_out[kind][small_names.index(n)]
                if n == "conv_w":
                    val = val[None]
                elif n == "w_pool":
                    val = val[None]
                result.append(val)
    return tuple(result)
```

```python
import functools

import numpy as np
import jax
import jax.numpy as jnp
from jax import lax
from jax.experimental import pallas as pl
from jax.experimental.pallas import tpu as pltpu

F32 = jnp.float32
BF16 = jnp.bfloat16

D_MODEL = 1024
HEADS = 4
HEAD_DIM = 128
RET_W = HEADS * HEAD_DIM
POOL_WINDOWS = (2, 4, 8, 16)
POOL_W = 512
IN_W = 4 * RET_W + POOL_W
D_FF = 2816
N_SHARD = 4
IN_SH = IN_W // N_SHARD
UP_SH = 2 * D_FF // N_SHARD
DOWN_SH = D_FF // N_SHARD
OUT_SH = D_MODEL // N_SHARD
ROPE_BASE = 10000.0
LN_EPS = 1e-5
RMS_EPS = 1e-6
ALPHA = 2.0 ** 0.25
K_SCALE = HEAD_DIM ** -0.5
SUPER = 256
CHUNK = 64
POOL_HALO = 16
CONV_HALO = 8
FFN_STRIP = 128
LN_ROWS = 32

ADAM_LR = 0.001
ADAM_B1 = 0.9
ADAM_B2 = 0.999
ADAM_EPS = 1e-08
ADAM_WD = 0.01
ADAM_STEP = 10

MESH = pl.DeviceIdType.MESH
VMEM_LIMIT = 56 * 1024 * 1024


def _dot(a, b):
    return jnp.dot(a, b, preferred_element_type=F32)


def _dot_nt(a, b):
    return lax.dot_general(a, b, (((1,), (1,)), ((), ())), preferred_element_type=F32)


def _dot_tn(a, b):
    return lax.dot_general(a, b, (((0,), (0,)), ((), ())), preferred_element_type=F32)


def _sigmoid(x):
    return 1.0 / (1.0 + jnp.exp(-x))


def _params(sem):
    return pltpu.CompilerParams(dimension_semantics=sem, vmem_limit_bytes=VMEM_LIMIT)


def _whole():
    return pl.BlockSpec(memory_space=pltpu.VMEM)


HBM_SPEC = pl.BlockSpec(memory_space=pl.ANY)


class _Rider:
    def __init__(self, inplace, srcs, lands, n_copies, make):
        self.inplace, self.srcs, self.lands, self.n_copies, self.make = list(inplace), list(srcs), list(lands), n_copies, make


def _call(body, *, name, grid, in_specs, out_specs, out_shape, operands, scratch_shapes=(), sem=(),
          aliases=None, riders=(), after=()):
    n_in, n_out, n_scr = len(in_specs), len(out_shape), len(scratch_shapes)
    in_specs, out_specs, out_shape = list(in_specs), list(out_specs), list(out_shape)
    operands, scratch_shapes, aliases = list(operands), list(scratch_shapes), dict(aliases or {})
    in_specs += [_whole()] * len(after)
    operands += list(after)
    for r in riders:
        for a in r.inplace:
            aliases[len(in_specs)] = len(out_shape)
            in_specs.append(HBM_SPEC)
            operands.append(a)
            out_specs.append(HBM_SPEC)
            out_shape.append(jax.ShapeDtypeStruct(a.shape, a.dtype))
        for a in r.srcs:
            in_specs.append(HBM_SPEC)
            operands.append(a)
        for shp in r.lands:
            out_specs.append(HBM_SPEC)
            out_shape.append(shp)
        scratch_shapes += [pltpu.SemaphoreType.DMA((r.n_copies,)), pltpu.SemaphoreType.DMA((r.n_copies,))]

    def full(*refs):
        ins = refs[:n_in]
        at = n_in + len(after)
        r_srcs = []
        for r in riders:
            at += len(r.inplace)
            r_srcs.append(refs[at:at + len(r.srcs)])
            at += len(r.srcs)
        outs = refs[at:at + n_out]
        at += n_out
        r_outs = []
        for r in riders:
            r_outs.append((refs[at:at + len(r.inplace)], refs[at + len(r.inplace):at + len(r.inplace) + len(r.lands)]))
            at += len(r.inplace) + len(r.lands)
        scr = refs[at:at + n_scr]
        at += n_scr
        r_sems = [refs[at + 2 * i:at + 2 * i + 2] for i in range(len(riders))]

        def copies():
            return [r.make(r_outs[i][0], r_srcs[i], r_outs[i][1], r_sems[i][0], r_sems[i][1])
                    for i, r in enumerate(riders)]

        def start():
            for starts, _ in copies():
                for cp in starts:
                    cp.start()

        def finish():
            for _, waits in copies():
                for wait in waits:
                    wait()

        if riders and grid:
            first = functools.reduce(jnp.logical_and, [pl.program_id(d) == 0 for d in range(len(grid))])
            last = functools.reduce(jnp.logical_and, [pl.program_id(d) == grid[d] - 1 for d in range(len(grid))])
            pl.when(first)(start)
            body(*ins, *outs, *scr)
            pl.when(last)(finish)
        else:
            if riders:
                start()
            body(*ins, *outs, *scr)
            if riders:
                finish()

    params = _params(sem) if grid else pltpu.CompilerParams(vmem_limit_bytes=VMEM_LIMIT)
    res = pl.pallas_call(
        full, name=name, grid=grid, in_specs=in_specs, out_specs=out_specs, out_shape=out_shape,
        scratch_shapes=scratch_shapes, input_output_aliases=aliases, compiler_params=params,
    )(*operands)
    outs, at, rider_res = res[:n_out], n_out, []
    for r in riders:
        rider_res.append((res[at:at + len(r.inplace)], res[at + len(r.inplace):at + len(r.inplace) + len(r.lands)]))
        at += len(r.inplace) + len(r.lands)
    return list(outs), rider_res


def _gammas():
    return [1.0 - 2.0 ** (-5.0 - h) for h in range(HEADS)]


def _decay_tables():
    idx = np.arange(SUPER)
    dist = np.abs(idx[:, None] - idx[None, :]).astype(np.float64)
    visible = (idx[None, :] // CHUNK) <= (idx[:, None] // CHUNK)
    mask = np.stack([np.where(visible, g ** dist, 0.0) for g in _gammas()])
    qd = np.concatenate([np.repeat((g ** (idx + 1.0))[:, None], HEAD_DIM, 1) for g in _gammas()], 1)
    kd = np.concatenate([np.repeat((g ** (SUPER - 1.0 - idx))[:, None], HEAD_DIM, 1) for g in _gammas()], 1)
    return (jnp.asarray(mask, F32), jnp.asarray(qd, F32), jnp.asarray(kd, F32))


def _rope_tables(s):
    inv_freq = ROPE_BASE ** (-np.arange(0, HEAD_DIM, 2, dtype=np.float64) / HEAD_DIM)
    ang = np.arange(s, dtype=np.float64)[:, None] * inv_freq[None, :]
    cos, sin = np.cos(ang), np.sin(ang)
    return (jnp.asarray(np.concatenate([cos, cos], 1), F32),
            jnp.asarray(np.concatenate([-sin, sin], 1), F32))


def _rope(t, cosf, sinf):
    return t * cosf + pltpu.roll(t, HEAD_DIM // 2, 1) * sinf


def _rope_t(t, cosf, sinf):
    return t * cosf - pltpu.roll(t, HEAD_DIM // 2, 1) * sinf


def _layernorm_fwd(z):
    mu = jnp.mean(z, axis=-1, keepdims=True)
    zc = z - mu
    var = jnp.mean(zc * zc, axis=-1, keepdims=True)
    rstd = lax.rsqrt(var + LN_EPS)
    return zc * rstd, rstd


def _layernorm_bwd(dy, xhat, rstd, gain):
    dxh = dy * gain
    m1 = jnp.mean(dxh, axis=-1, keepdims=True)
    m2 = jnp.mean(dxh * xhat, axis=-1, keepdims=True)
    return rstd * (dxh - m1 - xhat * m2)


def _proj_pool(x, win4, cosf, sinf, wpool, pscale, ts, riders=(), after=()):
    s = x.shape[0]
    nt = s // ts

    def body(x_ref, w_ref, cos_ref, sin_ref, wp_ref, ps_ref,
             xb_ref, q_ref, k_ref, v_ref, g_ref, pooled_ref, cat_ref, proj_scr, pext_scr):
        i = pl.program_id(0)
        xb = x_ref[...].astype(BF16)
        xb_ref[...] = xb
        for j in range(N_SHARD):
            proj_scr[:, j * IN_SH:(j + 1) * IN_SH] = _dot(xb, w_ref[j])
        cosf_t = cos_ref[...]
        sinf_t = sin_ref[...]
        for h in range(HEADS):
            lo = h * HEAD_DIM
            q_ref[:, lo:lo + HEAD_DIM] = _rope(proj_scr[:, lo:lo + HEAD_DIM], cosf_t, sinf_t).astype(BF16)
            kk = _rope(proj_scr[:, RET_W + lo:RET_W + lo + HEAD_DIM], cosf_t, sinf_t) * K_SCALE
            k_ref[:, lo:lo + HEAD_DIM] = kk.astype(BF16)
        v_ref[...] = proj_scr[:, 2 * RET_W:3 * RET_W].astype(BF16)
        g_ref[...] = proj_scr[:, 3 * RET_W:4 * RET_W]

        @pl.when(i == 0)
        def _():
            pext_scr[0:POOL_HALO, :] = jnp.zeros((POOL_HALO, POOL_W), F32)

        pext_scr[POOL_HALO:POOL_HALO + ts, :] = proj_scr[:, 4 * RET_W:IN_W]
        pos = (i * ts + lax.broadcasted_iota(jnp.int32, (ts, 1), 0) + 1).astype(F32)
        for gi, w in enumerate(POOL_WINDOWS):
            lo = gi * HEAD_DIM
            ext = pext_scr[:, lo:lo + HEAD_DIM]
            acc = ext
            shift = 1
            while shift < w:
                acc = acc + pltpu.roll(acc, shift, 0)
                shift *= 2
            tok = ext[POOL_HALO:POOL_HALO + ts]
            pooled = acc[POOL_HALO:POOL_HALO + ts] / jnp.minimum(pos, float(w)) - tok
            pooled_b = pooled.astype(BF16)
            pooled_ref[:, lo:lo + HEAD_DIM] = pooled_b
            lin = _dot(pooled_b, wp_ref[gi])
            cat_ref[:, lo:lo + HEAD_DIM] = (lin * ps_ref[:, lo:lo + HEAD_DIM]).astype(BF16)
        pext_scr[0:POOL_HALO, :] = pext_scr[ts:ts + POOL_HALO, :]

    tile = lambda w: pl.BlockSpec((ts, w), lambda i: (i, 0))
    return _call(
        body, name="proj_pool", grid=(nt,),
        in_specs=[tile(D_MODEL), _whole(), tile(HEAD_DIM), tile(HEAD_DIM), _whole(), _whole()],
        out_specs=[tile(D_MODEL), tile(RET_W), tile(RET_W), tile(RET_W), tile(RET_W), tile(POOL_W),
                   pl.BlockSpec((ts, POOL_W), lambda i: (i, 1))],
        out_shape=[jax.ShapeDtypeStruct((s, D_MODEL), BF16), jax.ShapeDtypeStruct((s, RET_W), BF16),
                   jax.ShapeDtypeStruct((s, RET_W), BF16), jax.ShapeDtypeStruct((s, RET_W), BF16),
                   jax.ShapeDtypeStruct((s, RET_W), F32), jax.ShapeDtypeStruct((s, POOL_W), BF16),
                   jax.ShapeDtypeStruct((s, 2 * RET_W), BF16)],
        scratch_shapes=[pltpu.VMEM((ts, IN_W), F32), pltpu.VMEM((ts + POOL_HALO, POOL_W), F32)],
        sem=("arbitrary",), operands=(x, win4, cosf, sinf, wpool, pscale), riders=riders, after=after,
    )


def _retention_fwd(q, k, v, g, cat, mask, qd, kd, riders=(), after=()):
    s = q.shape[0]
    ns = s // SUPER
    cdec = [gm ** float(SUPER) for gm in _gammas()]

    def body(q_ref, k_ref, v_ref, g_ref, cat_in, mask_ref, qd_ref, kd_ref,
             ret_ref, cat_ref, st_ref, state_scr):
        del cat_in
        n = pl.program_id(0)

        @pl.when(n == 0)
        def _():
            state_scr[...] = jnp.zeros_like(state_scr)

        for h in range(HEADS):
            sl = slice(h * HEAD_DIM, (h + 1) * HEAD_DIM)
            qh, kh, vh = q_ref[:, sl], k_ref[:, sl], v_ref[:, sl]
            sc = _dot_nt(qh, kh) * mask_ref[h]
            st = state_scr[h]
            stb = st.astype(BF16)
            st_ref[0, h] = stb
            qdb = (qh.astype(F32) * qd_ref[:, sl]).astype(BF16)
            kdb = (kh.astype(F32) * kd_ref[:, sl]).astype(BF16)
            ret = _dot(sc.astype(BF16), vh) + _dot(qdb, stb)
            state_scr[h] = st * cdec[h] + _dot_tn(kdb, vh)
            ret_ref[:, sl] = ret
            r = lax.rsqrt(jnp.mean(ret * ret, axis=-1, keepdims=True) + RMS_EPS)
            gh = g_ref[:, sl]
            cat_ref[:, sl] = ((ret * r) * (gh * _sigmoid(gh))).astype(BF16)

    tile = pl.BlockSpec((SUPER, RET_W), lambda n: (n, 0))
    return _call(
        body, name="retention_fwd", grid=(ns,),
        in_specs=[tile, tile, tile, tile, HBM_SPEC, _whole(), _whole(), _whole()],
        out_specs=[tile, tile, pl.BlockSpec((1, HEADS, HEAD_DIM, HEAD_DIM), lambda n: (n, 0, 0, 0))],
        out_shape=[jax.ShapeDtypeStruct((s, RET_W), F32), jax.ShapeDtypeStruct((s, 2 * RET_W), BF16),
                   jax.ShapeDtypeStruct((ns, HEADS, HEAD_DIM, HEAD_DIM), BF16)],
        scratch_shapes=[pltpu.VMEM((HEADS, HEAD_DIM, HEAD_DIM), F32)],
        aliases={4: 1}, sem=("arbitrary",), operands=(q, k, v, g, cat, mask, qd, kd), riders=riders,
        after=after,
    )


def _outproj_ln1(x, cat, wout, g1, b1, ts, riders=(), after=()):
    s = x.shape[0]

    def body(x_ref, cat_ref, w_ref, g_ref, b_ref, xhat_ref, rstd_ref, h1b_ref):
        z = ALPHA * x_ref[...] + _dot(cat_ref[...], w_ref[...])
        xhat, rstd = _layernorm_fwd(z)
        xhat_ref[...] = xhat
        rstd_ref[...] = rstd
        h1b_ref[...] = (xhat * g_ref[...] + b_ref[...]).astype(BF16)

    tile = lambda w: pl.BlockSpec((ts, w), lambda i: (i, 0))
    return _call(
        body, name="outproj_ln1", grid=(s // ts,),
        in_specs=[tile(D_MODEL), tile(D_MODEL), _whole(), _whole(), _whole()],
        out_specs=[tile(D_MODEL), tile(1), tile(D_MODEL)],
        out_shape=[jax.ShapeDtypeStruct((s, D_MODEL), F32), jax.ShapeDtypeStruct((s, 1), F32),
                   jax.ShapeDtypeStruct((s, D_MODEL), BF16)],
        sem=("arbitrary",), operands=(x, cat, wout, g1, b1), riders=riders, after=after,
    )


def _ffn_fwd_loss(xhat1, h1b, target, wup4, wdown, cw, cb, g1, b1, g2, b2, ts):
    s = xhat1.shape[0]

    def body(xhat_ref, h1b_ref, tgt_ref, wup_ref, wdn_ref, cw_ref, cb_ref, g1_ref, b1_ref, g2_ref, b2_ref,
             ub_ref, act_ref, sd_ref, dz2_ref, dz2b_ref, loss_ref, dg2_ref, db2_ref, val_scr, gext_scr, ffn_scr):
        i = pl.program_id(0)

        @pl.when(i == 0)
        def _():
            gext_scr[0:CONV_HALO, :] = jnp.zeros((CONV_HALO, D_FF), F32)
            loss_ref[...] = jnp.zeros_like(loss_ref)
            dg2_ref[...] = jnp.zeros_like(dg2_ref)
            db2_ref[...] = jnp.zeros_like(db2_ref)

        for half in range(2):
            lo = half * UP_SH
            gext_scr[CONV_HALO:CONV_HALO + ts, lo:lo + UP_SH] = _dot(h1b_ref[...], wup_ref[2 + half])
            val_scr[:, lo:lo + UP_SH] = _dot(h1b_ref[...], wup_ref[half])
            for c0 in range(lo, lo + UP_SH, FFN_STRIP):
                cols = slice(c0, c0 + FFN_STRIP)
                ext = gext_scr[:, cols]
                gate = ext[CONV_HALO:]
                hc = cb_ref[:, cols] + ((pltpu.roll(ext, 2, 0)[CONV_HALO:] * cw_ref[0:1, cols]
                                         + pltpu.roll(ext, 1, 0)[CONV_HALO:] * cw_ref[1:2, cols])
                                        + gate * cw_ref[2:3, cols])
                val = val_scr[:, cols]
                sg = _sigmoid(hc)
                si = hc * sg
                act_ref[:, cols] = (si * val).astype(BF16)
                ub_ref[:, cols] = val.astype(BF16)
                ub_ref[:, D_FF + c0:D_FF + c0 + FFN_STRIP] = gate.astype(BF16)
                sd_ref[:, cols] = hc.astype(BF16)
            part = _dot(act_ref[:, lo:lo + UP_SH], wdn_ref[lo:lo + UP_SH, :])
            if half == 0:
                ffn_scr[...] = part
            else:
                ffn_scr[...] += part

        gext_scr[0:CONV_HALO, :] = gext_scr[ts:ts + CONV_HALO, :]

        loss_acc = jnp.zeros((1, 1), F32)
        dg2_acc = jnp.zeros((1, D_MODEL), F32)
        db2_acc = jnp.zeros((1, D_MODEL), F32)
        for r0 in range(0, ts, LN_ROWS):
            rows = slice(r0, r0 + LN_ROWS)
            h1 = xhat_ref[rows, :] * g1_ref[...] + b1_ref[...]
            xhat2, rstd2 = _layernorm_fwd(ALPHA * h1 + ffn_scr[rows, :])
            diff = (xhat2 * g2_ref[...] + b2_ref[...]) - tgt_ref[rows, :]
            row = jnp.mean(diff * diff, axis=-1, keepdims=True)
            loss_acc = loss_acc + 0.5 * jnp.sum(row, axis=0, keepdims=True)
            dy = diff * (1.0 / D_MODEL)
            dg2_acc = dg2_acc + jnp.sum(dy * xhat2, axis=0, keepdims=True)
            db2_acc = db2_acc + jnp.sum(dy, axis=0, keepdims=True)
            dz2 = _layernorm_bwd(dy, xhat2, rstd2, g2_ref[...])
            dz2_ref[rows, :] = dz2
            dz2b_ref[rows, :] = dz2.astype(BF16)
        loss_ref[...] += loss_acc
        dg2_ref[...] += dg2_acc
        db2_ref[...] += db2_acc

    tile = lambda w: pl.BlockSpec((ts, w), lambda i: (i, 0))
    acc = lambda w: pl.BlockSpec((1, w), lambda i: (0, 0))
    return pl.pallas_call(
        body, name="ffn_fwd_loss", grid=(s // ts,),
        in_specs=[tile(D_MODEL), tile(D_MODEL), tile(D_MODEL)] + [_whole()] * 8,
        out_specs=[tile(2 * D_FF), tile(D_FF), tile(D_FF), tile(D_MODEL), tile(D_MODEL),
                   acc(1), acc(D_MODEL), acc(D_MODEL)],
        out_shape=[jax.ShapeDtypeStruct((s, 2 * D_FF), BF16), jax.ShapeDtypeStruct((s, D_FF), BF16),
                   jax.ShapeDtypeStruct((s, D_FF), BF16), jax.ShapeDtypeStruct((s, D_MODEL), F32),
                   jax.ShapeDtypeStruct((s, D_MODEL), BF16),
                   jax.ShapeDtypeStruct((1, 1), F32), jax.ShapeDtypeStruct((1, D_MODEL), F32),
                   jax.ShapeDtypeStruct((1, D_MODEL), F32)],
        scratch_shapes=[pltpu.VMEM((ts, D_FF), F32), pltpu.VMEM((ts + CONV_HALO, D_FF), F32),
                        pltpu.VMEM((ts, D_MODEL), F32)],
        compiler_params=_params(("arbitrary",)),
    )(xhat1, h1b, target, wup4, wdown, cw, cb, g1, b1, g2, b2)


def _ffn_bwd(dz2, dz2b, ub, sd, xhat1, rstd1, wup4, wdown, cw, g1, ts):
    s = dz2.shape[0]
    nt = s // ts

    def body(dz2_ref, dz2b_ref, ub_ref, sd_ref, xhat_ref, rstd_ref, wup_ref, wdn_ref, cw_ref, g1_ref,
             dub_ref, dz1_ref, dz1b_ref, dg1_ref, db1_ref, dcw_ref, dcb_ref, dext_scr, da_scr):
        i = pl.program_id(0)

        @pl.when(i == 0)
        def _():
            dext_scr[ts:ts + CONV_HALO, :] = jnp.zeros((CONV_HALO, D_FF), F32)
            dg1_ref[...] = jnp.zeros_like(dg1_ref)
            db1_ref[...] = jnp.zeros_like(db1_ref)
            dcw_ref[...] = jnp.zeros_like(dcw_ref)
            dcb_ref[...] = jnp.zeros_like(dcb_ref)

        da_scr[...] = _dot_nt(dz2b_ref[...], wdn_ref[...])
        n_ext = ts + CONV_HALO
        for c0 in range(0, D_FF, FFN_STRIP):
            cols = slice(c0, c0 + FFN_STRIP)
            gcols = slice(D_FF + c0, D_FF + c0 + FFN_STRIP)
            val = ub_ref[:, cols].astype(F32)
            gate = ub_ref[:, gcols].astype(F32)
            da = da_scr[:, cols]
            hc = sd_ref[:, cols].astype(F32)
            sg = _sigmoid(hc)
            dhc = da * val * (sg * (1.0 + hc * (1.0 - sg)))
            dext_scr[0:ts, cols] = dhc
            dext = dext_scr[:, cols]
            dhc1 = pltpu.roll(dext, n_ext - 1, 0)[0:ts]
            dhc2 = pltpu.roll(dext, n_ext - 2, 0)[0:ts]
            dcb_ref[:, cols] += jnp.sum(dhc, axis=0, keepdims=True)
            dcw_ref[0:1, cols] += jnp.sum(dhc2 * gate, axis=0, keepdims=True)
            dcw_ref[1:2, cols] += jnp.sum(dhc1 * gate, axis=0, keepdims=True)
            dcw_ref[2:3, cols] += jnp.sum(dhc * gate, axis=0, keepdims=True)
            dgate = dhc * cw_ref[2:3, cols] + dhc1 * cw_ref[1:2, cols] + dhc2 * cw_ref[0:1, cols]
            dub_ref[:, cols] = (da * (hc * sg)).astype(BF16)
            dub_ref[:, gcols] = dgate.astype(BF16)
        dext_scr[ts:n_ext, :] = dext_scr[0:CONV_HALO, :]
        dh1 = ALPHA * dz2_ref[...]
        for j in range(N_SHARD):
            dh1 = dh1 + _dot_nt(dub_ref[:, j * UP_SH:(j + 1) * UP_SH], wup_ref[j])
        xhat = xhat_ref[...]
        dg1_ref[...] += jnp.sum(dh1 * xhat, axis=0, keepdims=True)
        db1_ref[...] += jnp.sum(dh1, axis=0, keepdims=True)
        dz1 = _layernorm_bwd(dh1, xhat, rstd_ref[...], g1_ref[...])
        dz1_ref[...] = dz1
        dz1b_ref[...] = dz1.astype(BF16)

    tile = lambda w: pl.BlockSpec((ts, w), lambda i: (nt - 1 - i, 0))
    acc = lambda rws, w: pl.BlockSpec((rws, w), lambda i: (0, 0))
    return pl.pallas_call(
        body, name="ffn_bwd", grid=(nt,),
        in_specs=[tile(D_MODEL), tile(D_MODEL), tile(2 * D_FF), tile(D_FF), tile(D_MODEL), tile(1)]
        + [_whole()] * 4,
        out_specs=[tile(2 * D_FF), tile(D_MODEL), tile(D_MODEL), acc(1, D_MODEL), acc(1, D_MODEL),
                   acc(3, D_FF), acc(1, D_FF)],
        out_shape=[jax.ShapeDtypeStruct((s, 2 * D_FF), BF16),
                   jax.ShapeDtypeStruct((s, D_MODEL), F32), jax.ShapeDtypeStruct((s, D_MODEL), BF16),
                   jax.ShapeDtypeStruct((1, D_MODEL), F32),
                   jax.ShapeDtypeStruct((1, D_MODEL), F32), jax.ShapeDtypeStruct((3, D_FF), F32),
                   jax.ShapeDtypeStruct((1, D_FF), F32)],
        scratch_shapes=[pltpu.VMEM((ts + CONV_HALO, D_FF), F32), pltpu.VMEM((ts, D_FF), F32)],
        compiler_params=_params(("arbitrary",)),
    )(dz2, dz2b, ub, sd, xhat1, rstd1, wup4, wdown, cw, g1)


def _mix_bwd(dz1, pooled, ret, g, wout, wpool, pscale, ts, riders=(), after=()):
    s = dz1.shape[0]
    nt = s // ts

    def body(dz1_ref, pooled_ref, ret_ref, g_ref, wout_ref, wp_ref, ps_ref,
             dret_ref, dgp_ref, dwp_ref, dps_ref, eext_scr):
        i = pl.program_id(0)
        r = nt - 1 - i

        @pl.when(i == 0)
        def _():
            eext_scr[ts:ts + POOL_HALO, :] = jnp.zeros((POOL_HALO, POOL_W), F32)
            dwp_ref[...] = jnp.zeros_like(dwp_ref)
            dps_ref[...] = jnp.zeros_like(dps_ref)

        dzb = dz1_ref[...].astype(BF16)
        dcat_r = _dot_nt(dzb, wout_ref[0:RET_W, :])
        dcat_p = _dot_nt(dzb, wout_ref[RET_W:2 * RET_W, :])
        pos = (r * ts + lax.broadcasted_iota(jnp.int32, (ts, 1), 0) + 1).astype(F32)
        dpooled = []
        for gi, w in enumerate(POOL_WINDOWS):
            sl = slice(gi * HEAD_DIM, (gi + 1) * HEAD_DIM)
            pb = pooled_ref[:, sl]
            dy = dcat_p[:, sl]
            dps_ref[:, sl] += jnp.sum(dy * _dot(pb, wp_ref[gi]), axis=0, keepdims=True)
            dlin = (dy * ps_ref[:, sl]).astype(BF16)
            dwp_ref[gi] += _dot_tn(pb, dlin)
            dpg = _dot_nt(dlin, wp_ref[gi])
            dpooled.append(dpg)
            eext_scr[0:ts, sl] = dpg / jnp.minimum(pos, float(w))
        for gi, w in enumerate(POOL_WINDOWS):
            sl = slice(gi * HEAD_DIM, (gi + 1) * HEAD_DIM)
            acc = eext_scr[:, sl]
            shift = 1
            while shift < w:
                acc = acc + pltpu.roll(acc, ts + POOL_HALO - shift, 0)
                shift *= 2
            dgp_ref[:, RET_W + gi * HEAD_DIM:RET_W + (gi + 1) * HEAD_DIM] = (acc[0:ts] - dpooled[gi]).astype(BF16)
        eext_scr[ts:ts + POOL_HALO, :] = eext_scr[0:POOL_HALO, :]
        for h in range(HEADS):
            sl = slice(h * HEAD_DIM, (h + 1) * HEAD_DIM)
            rt = ret_ref[:, sl]
            rr = lax.rsqrt(jnp.mean(rt * rt, axis=-1, keepdims=True) + RMS_EPS)
            rn = rt * rr
            gh = g_ref[:, sl]
            sg = _sigmoid(gh)
            dy = dcat_r[:, sl]
            dgp_ref[:, sl] = (dy * rn * (sg * (1.0 + gh * (1.0 - sg)))).astype(BF16)
            drn = dy * (gh * sg)
            dret_ref[:, sl] = (rr * (drn - rn * jnp.mean(drn * rn, axis=-1, keepdims=True))).astype(BF16)

    tile = lambda w: pl.BlockSpec((ts, w), lambda i: (nt - 1 - i, 0))
    return _call(
        body, name="mix_bwd", grid=(nt,),
        in_specs=[tile(D_MODEL), tile(POOL_W), tile(RET_W), tile(RET_W), _whole(), _whole(), _whole()],
        out_specs=[tile(RET_W), tile(2 * RET_W),
                   pl.BlockSpec((len(POOL_WINDOWS), HEAD_DIM, HEAD_DIM), lambda i: (0, 0, 0)),
                   pl.BlockSpec((1, POOL_W), lambda i: (0, 0))],
        out_shape=[jax.ShapeDtypeStruct((s, RET_W), BF16), jax.ShapeDtypeStruct((s, 2 * RET_W), BF16),
                   jax.ShapeDtypeStruct((len(POOL_WINDOWS), HEAD_DIM, HEAD_DIM), F32),
                   jax.ShapeDtypeStruct((1, POOL_W), F32)],
        scratch_shapes=[pltpu.VMEM((ts + POOL_HALO, POOL_W), F32)],
        sem=("arbitrary",), operands=(dz1, pooled, ret, g, wout, wpool, pscale), riders=riders,
        after=after,
    )


def _retention_bwd(q, k, v, dret, dgp, states, mask, qd, kd, cosf, sinf, riders=(), after=()):
    s = q.shape[0]
    ns = s // SUPER
    cdec = [gm ** float(SUPER) for gm in _gammas()]

    def body(q_ref, k_ref, v_ref, do_ref, dgp_ref, st_ref, mask_ref, qd_ref, kd_ref, cos_ref, sin_ref,
             dproj_ref, dstate_scr):
        i = pl.program_id(0)

        @pl.when(i == 0)
        def _():
            dstate_scr[...] = jnp.zeros_like(dstate_scr)

        cosf_t = cos_ref[...]
        sinf_t = sin_ref[...]
        for h in range(HEADS):
            sl = slice(h * HEAD_DIM, (h + 1) * HEAD_DIM)
            qh, kh, vh, doh = q_ref[:, sl], k_ref[:, sl], v_ref[:, sl], do_ref[:, sl]
            dscb = (_dot_nt(doh, vh) * mask_ref[0, h]).astype(BF16)
            dsctb = (_dot_nt(vh, doh) * mask_ref[1, h]).astype(BF16)
            sctb = (_dot_nt(kh, qh) * mask_ref[1, h]).astype(BF16)
            stb = st_ref[0, h]
            dst = dstate_scr[h]
            dstb = dst.astype(BF16)
            qdb = (qh.astype(F32) * qd_ref[:, sl]).astype(BF16)
            kdb = (kh.astype(F32) * kd_ref[:, sl]).astype(BF16)
            dq = _dot(dscb, kh) + _dot_nt(doh, stb) * qd_ref[:, sl]
            dk = _dot(dsctb, qh) + _dot_nt(vh, dstb) * kd_ref[:, sl]
            dv = _dot(sctb, doh) + _dot(kdb, dstb)
            dstate_scr[h] = dst * cdec[h] + _dot_tn(qdb, doh)
            lo = h * HEAD_DIM
            dproj_ref[:, lo:lo + HEAD_DIM] = _rope_t(dq, cosf_t, sinf_t).astype(BF16)
            dproj_ref[:, RET_W + lo:RET_W + lo + HEAD_DIM] = _rope_t(dk * K_SCALE, cosf_t, sinf_t).astype(BF16)
            dproj_ref[:, 2 * RET_W + lo:2 * RET_W + lo + HEAD_DIM] = dv.astype(BF16)
        dproj_ref[:, 3 * RET_W:IN_W] = dgp_ref[...]

    tile = lambda w: pl.BlockSpec((SUPER, w), lambda i: (ns - 1 - i, 0))
    return _call(
        body, name="retention_bwd", grid=(ns,),
        in_specs=[tile(RET_W), tile(RET_W), tile(RET_W), tile(RET_W), tile(2 * RET_W),
                  pl.BlockSpec((1, HEADS, HEAD_DIM, HEAD_DIM), lambda i: (ns - 1 - i, 0, 0, 0)),
                  _whole(), _whole(), _whole(), tile(HEAD_DIM), tile(HEAD_DIM)],
        out_specs=[tile(IN_W)],
        out_shape=[jax.ShapeDtypeStruct((s, IN_W), BF16)],
        scratch_shapes=[pltpu.VMEM((HEADS, HEAD_DIM, HEAD_DIM), F32)],
        sem=("arbitrary",), operands=(q, k, v, dret, dgp, states, mask, qd, kd, cosf, sinf), riders=riders,
        after=after,
    )


def _dx(dz1, dproj, win4, ts, riders=(), after=()):
    s = dz1.shape[0]

    def body(dz1_ref, dp_ref, w_ref, dx_ref):
        acc = ALPHA * dz1_ref[...]
        for j in range(N_SHARD):
            acc = acc + _dot_nt(dp_ref[:, j * IN_SH:(j + 1) * IN_SH], w_ref[j])
        dx_ref[...] = acc

    tile = lambda w: pl.BlockSpec((ts, w), lambda i: (i, 0))
    return _call(
        body, name="dx", grid=(s // ts,),
        in_specs=[tile(D_MODEL), tile(IN_W), _whole()],
        out_specs=[tile(D_MODEL)],
        out_shape=[jax.ShapeDtypeStruct((s, D_MODEL), F32)],
        sem=("arbitrary",), operands=(dz1, dproj, win4), riders=riders, after=after,
    )


def _wgrad(a, b, tm, tn, name, stacked, m_outer, riders=(), after=()):
    s, m = a.shape
    n = b.shape[1]

    def body(a_ref, b_ref, o32_ref, o16_ref):
        res = _dot_tn(a_ref[...], b_ref[...])
        o32_ref[...] = res.reshape(o32_ref.shape)
        o16_ref[...] = res.astype(BF16).reshape(o16_ref.shape)

    if m_outer:
        grid, blocks = (m // tm, n // tn), (lambda g0, g1: (g0, g1))
    else:
        grid, blocks = (n // tn, m // tm), (lambda g0, g1: (g1, g0))
    if stacked:
        shape = (n // tn, m, tn)
        ospec = pl.BlockSpec((1, tm, tn), lambda g0, g1: (blocks(g0, g1)[1], blocks(g0, g1)[0], 0))
    else:
        shape = (m, n)
        ospec = pl.BlockSpec((tm, tn), lambda g0, g1: blocks(g0, g1))
    return _call(
        body, name=name, grid=grid,
        in_specs=[pl.BlockSpec((s, tm), lambda g0, g1: (0, blocks(g0, g1)[0])),
                  pl.BlockSpec((s, tn), lambda g0, g1: (0, blocks(g0, g1)[1]))],
        out_specs=[ospec, ospec],
        out_shape=[jax.ShapeDtypeStruct(shape, F32), jax.ShapeDtypeStruct(shape, BF16)],
        sem=("arbitrary", "arbitrary"), operands=(a, b), riders=riders, after=after,
    )


class _NoComm:
    def __init__(self, win4, wout, wup4, wdown):
        self.weights = dict(w_in=win4, w_out=wout, w_up=wup4, w_down=wdown)
        self.grads = {}

    def weight(self, name):
        return self.weights[name]

    def riders(self, call):
        return ()

    def after(self, call):
        return ()

    def landed(self, call, results, outs):
        pass

    def small_gradients(self, loss, small):
        pass

    def gradient(self, name, g32, g16):
        self.grads[name] = (g32, g16)


def _local_step(x, target, cw, cb, wpool, pscale, g1, b1, g2, b2, comm):
    s = x.shape[0]
    ts_a = min(512, s)
    ts_f = min(256, s)
    mask, qd, kd = _decay_tables()
    cosf, sinf = _rope_tables(s)
    wpool_b = wpool.astype(BF16)

    def run(call, fn, *args):
        outs, res = fn(*args, riders=comm.riders(call), after=comm.after(call))
        comm.landed(call, res, outs)
        return outs

    xb, q, k, v, g, pooled, cat = run("proj_pool", _proj_pool, x, comm.weight("w_in"), cosf, sinf, wpool_b,
                                      pscale, ts_a)
    ret, cat, states = run("retention_fwd", _retention_fwd, q, k, v, g, cat, mask, qd, kd)
    wout = comm.weight("w_out")
    xhat1, rstd1, h1b = run("outproj_ln1", _outproj_ln1, x, cat, wout, g1, b1, ts_a)
    wup4, wdown = comm.weight("w_up"), comm.weight("w_down")
    ub, act, sd, dz2, dz2b, loss, dg2, db2 = _ffn_fwd_loss(xhat1, h1b, target, wup4, wdown, cw, cb, g1, b1, g2, b2,
                                                           ts_f)

    dub, dz1, dz1b, dg1, db1, dcw, dcb = _ffn_bwd(dz2, dz2b, ub, sd, xhat1, rstd1, wup4, wdown, cw, g1, ts_f)
    half = D_MODEL // 2
    comm.gradient("w_up", *run("wgrad_up", _wgrad, h1b, dub, half, UP_SH, "wgrad_up", True, False))
    comm.gradient("w_out", *run("wgrad_out", _wgrad, cat, dz1b, D_MODEL, half, "wgrad_out", False, True))
    comm.gradient("w_down", *run("wgrad_down", _wgrad, act, dz2b, D_FF // 2, half, "wgrad_down", False, True))
    dret, dgp, dwp, dps = run("mix_bwd", _mix_bwd, dz1b, pooled, ret, g, wout, wpool_b, pscale, ts_a)
    small = dict(w_pool=dwp, pool_scale=dps, ln1_g=dg1, ln1_b=db1, conv_w=dcw, conv_b=dcb,
                 ln2_g=dg2, ln2_b=db2)
    comm.small_gradients(loss, small)
    mask_both = jnp.stack([mask, jnp.swapaxes(mask, 1, 2)])
    dproj, = run("retention_bwd", _retention_bwd, q, k, v, dret, dgp, states, mask_both, qd, kd, cosf, sinf)
    comm.gradient("w_in", *run("wgrad_in", _wgrad, xb, dproj, D_MODEL, IN_SH, "wgrad_in", True, True))
    (grad_x,), _ = _dx(dz1, dproj, comm.weight("w_in"), ts_a, after=comm.after("dx"))
    return loss, grad_x, small


CAST_ROWS = 64
SHARD_SHAPES = ((D_MODEL, IN_SH), (OUT_SH, D_MODEL), (D_MODEL, UP_SH), (DOWN_SH, D_MODEL))
N_BIG = len(SHARD_SHAPES)
CW_SHARD = (3, DOWN_SH)


def _mesh_pos():
    return lax.axis_index("x"), lax.axis_index("y"), lax.axis_index("c")


def _other_chips(x, y):
    return [(1 - x, y), (x, 1 - y), (1 - x, 1 - y)]


def _half_rows(w, which):
    hr = SHARD_SHAPES[w][0] // 2
    return pl.ds(pl.multiple_of(which * hr, 16), hr)


def _gather_weights(shards, cw_shard, full):
    def body(*refs):
        in_refs = refs[:N_BIG]
        cw_ref = refs[N_BIG]
        out_refs = refs[N_BIG + 1:2 * N_BIG + 1]
        cwo_ref = refs[2 * N_BIG + 1]
        stage = refs[2 * N_BIG + 2:3 * N_BIG + 2]
        raw = refs[3 * N_BIG + 2:4 * N_BIG + 2 - len(full)]
        send_sems, recv_sems, fsend_sems, frecv_sems, cw_send, cw_recv, local_sems, load_sems = \
            refs[4 * N_BIG + 2 - len(full):]
        x, y, c = _mesh_pos()
        j0 = 2 * x + y
        chips = _other_chips(x, y)

        fetched = [w for w in range(N_BIG) if w not in full]
        f32 = {w: in_refs[w] for w in full}
        loads = []
        for n, w in enumerate(fetched):
            f32[w] = raw[n]
            loads.append(pltpu.make_async_copy(in_refs[w], raw[n], load_sems.at[n]))
            loads[-1].start()

        def cast_to_stage(w):
            def cast(i, carry):
                rows = pl.ds(pl.multiple_of(i * CAST_ROWS, CAST_ROWS), CAST_ROWS)
                stage[w][rows, :] = f32[w][rows, :].astype(BF16)
                return carry
            lax.fori_loop(0, SHARD_SHAPES[w][0] // CAST_ROWS, cast, 0)

        for w in full:
            cast_to_stage(w)

        jx, jy, jd = 2 * (1 - x) + y, 2 * x + (1 - y), 2 * (1 - x) + (1 - y)
        neighbours = [((1 - x, y, c), jx), ((x, 1 - y, c), jy)]
        passed = jnp.where(c == 0, jx, jy)
        pass_to = (jnp.where(c == 0, x, 1 - x), jnp.where(c == 0, 1 - y, y), c)

        def nbr(w, k, block):
            return pltpu.make_async_remote_copy(
                src_ref=stage[w].at[_half_rows(w, c), :], dst_ref=out_refs[w].at[block, _half_rows(w, c), :],
                send_sem=send_sems.at[w, k], recv_sem=recv_sems.at[w, k],
                device_id=neighbours[k][0], device_id_type=MESH)

        def relay(w, block):
            return pltpu.make_async_remote_copy(
                src_ref=out_refs[w].at[passed, _half_rows(w, c), :],
                dst_ref=out_refs[w].at[block, _half_rows(w, c), :],
                send_sem=send_sems.at[w, 2], recv_sem=recv_sems.at[w, 2],
                device_id=pass_to, device_id_type=MESH)

        def d2d(w, k, block, half):
            return pltpu.make_async_remote_copy(
                src_ref=out_refs[w].at[block, _half_rows(w, half), :],
                dst_ref=out_refs[w].at[block, _half_rows(w, half), :],
                send_sem=fsend_sems.at[w, k], recv_sem=frecv_sems.at[w, k],
                device_id=(x, y, 1 - c), device_id_type=MESH)

        def conv(k, block):
            chip = chips[k]
            return pltpu.make_async_remote_copy(
                src_ref=cw_ref, dst_ref=cwo_ref.at[block], send_sem=cw_send.at[k], recv_sem=cw_recv.at[k],
                device_id=(chip[0], chip[1], c), device_id_type=MESH)

        sent = [nbr(w, k, j0) for w in full for k in range(2)] + [conv(k, j0) for k in range(3)]
        for cp in sent:
            cp.start()
        for n, w in enumerate(fetched):
            loads[n].wait()
            cast_to_stage(w)
        local = [pltpu.make_async_copy(stage[w], out_refs[w].at[j0], local_sems.at[w]) for w in range(N_BIG)]
        local.append(pltpu.make_async_copy(cw_ref, cwo_ref.at[j0], local_sems.at[N_BIG]))
        for cp in local:
            cp.start()
        for w in full:
            for k, (_, block) in enumerate(neighbours):
                nbr(w, k, block).wait_recv()
            later = [relay(w, passed)] + [d2d(w, k, block, c) for k, (_, block) in enumerate(neighbours)]
            for cp in later:
                cp.start()
            sent += later
        for w in full:
            relay(w, jd).wait_recv()
            fw = d2d(w, 2, jd, c)
            fw.start()
            sent.append(fw)
        for w in full:
            for k, block in enumerate([jx, jy, jd]):
                d2d(w, k, block, 1 - c).wait_recv()
        for k, chip in enumerate(chips):
            conv(k, 2 * chip[0] + chip[1]).wait_recv()
        for cp in sent:
            cp.wait_send()
        for cp in local:
            cp.wait()

    out_shape = [jax.ShapeDtypeStruct((N_SHARD,) + shp, BF16) for shp in SHARD_SHAPES]
    out_shape.append(jax.ShapeDtypeStruct((N_SHARD,) + CW_SHARD, F32))
    return pl.pallas_call(
        body, name="gather_weights",
        in_specs=[_whole() if w in full else HBM_SPEC for w in range(N_BIG)] + [_whole()],
        out_specs=[HBM_SPEC] * (N_BIG + 1),
        out_shape=out_shape,
        scratch_shapes=[pltpu.VMEM(shp, BF16) for shp in SHARD_SHAPES]
        + [pltpu.VMEM(shp, F32) for w, shp in enumerate(SHARD_SHAPES) if w not in full] + [
            pltpu.SemaphoreType.DMA((N_BIG, 3)), pltpu.SemaphoreType.DMA((N_BIG, 3)),
            pltpu.SemaphoreType.DMA((N_BIG, 3)), pltpu.SemaphoreType.DMA((N_BIG, 3)),
            pltpu.SemaphoreType.DMA((3,)), pltpu.SemaphoreType.DMA((3,)),
            pltpu.SemaphoreType.DMA((N_BIG + 1,)), pltpu.SemaphoreType.DMA((N_BIG - len(full),))],
        compiler_params=pltpu.CompilerParams(vmem_limit_bytes=VMEM_LIMIT),
    )(*shards, cw_shard)


def _gather_rider(arrays, ops):
    ws = sorted(arrays)

    def make(inplace, srcs, lands, send_sems, recv_sems):
        del srcs, lands
        x, y, c = _mesh_pos()
        j0, jx, jy, jd = 2 * x + y, 2 * (1 - x) + y, 2 * x + (1 - y), 2 * (1 - x) + (1 - y)
        x_nbr, y_nbr, sibling = (1 - x, y, c), (x, 1 - y, c), (x, y, 1 - c)
        starts, waits = [], []
        for n, (kind, w, (r0, nr)) in enumerate(ops):
            ref = inplace[ws.index(w)]
            hr = SHARD_SHAPES[w][0] // 2
            rows = lambda core: pl.ds(pl.multiple_of(core * hr + r0, 16), nr)
            mine, theirs = rows(c), rows(1 - c)
            if kind == "ici":
                moves = [(ref.at[j0, mine, :], x_nbr, ref.at[jx, mine, :]),
                         (ref.at[j0, mine, :], y_nbr, ref.at[jy, mine, :]),
                         (ref.at[j0, mine, :], (1 - x, 1 - y, c), ref.at[jd, mine, :])]
            elif kind == "nbr":
                moves = [(ref.at[j0, mine, :], x_nbr, ref.at[jx, mine, :]),
                         (ref.at[j0, mine, :], y_nbr, ref.at[jy, mine, :])]
            elif kind == "relay":
                passed = jnp.where(c == 0, jx, jy)
                to = (jnp.where(c == 0, x, 1 - x), jnp.where(c == 0, 1 - y, y), c)
                moves = [(ref.at[passed, mine, :], to, ref.at[jd, mine, :])]
            else:
                blocks = dict(d2d=[jx, jy, jd], d2d_nbr=[jx, jy], d2d_diag=[jd])[kind]
                moves = [(ref.at[b, mine, :], sibling, ref.at[b, theirs, :]) for b in blocks]
            for k, (src, to, landing) in enumerate(moves):
                sems = dict(send_sem=send_sems.at[3 * n + k], recv_sem=recv_sems.at[3 * n + k],
                            device_id=to, device_id_type=MESH)
                send = pltpu.make_async_remote_copy(src_ref=src, dst_ref=src, **sems)
                arrival = pltpu.make_async_remote_copy(src_ref=src, dst_ref=landing, **sems)
                starts.append(send)
                waits += [arrival.wait_recv, send.wait_send]
        return starts, waits

    return _Rider([arrays[w] for w in ws], [], [], 3 * len(ops), make)


def _whole_half(w):
    return (0, SHARD_SHAPES[w][0] // 2)


def _pair_rider(ws, g16s):
    def make(inplace, srcs, lands, send_sems, recv_sems):
        del inplace
        x, y, c = _mesh_pos()
        copies = [pltpu.make_async_remote_copy(
            src_ref=srcs[i].at[:, _half_rows(w, 1 - c), :], dst_ref=lands[i],
            send_sem=send_sems.at[i], recv_sem=recv_sems.at[i], device_id=(x, y, 1 - c), device_id_type=MESH)
            for i, w in enumerate(ws)]
        return copies, [cp.wait for cp in copies]

    lands = [jax.ShapeDtypeStruct((N_SHARD, SHARD_SHAPES[w][0] // 2, SHARD_SHAPES[w][1]), BF16) for w in ws]
    return _Rider([], g16s, lands, len(ws), make)


def _chip_rider(ws, p16s):
    def make(inplace, srcs, lands, send_sems, recv_sems):
        del inplace
        x, y, c = _mesh_pos()
        copies = []
        for i in range(len(ws)):
            for k, chip in enumerate(_other_chips(x, y)):
                copies.append(pltpu.make_async_remote_copy(
                    src_ref=srcs[i].at[2 * chip[0] + chip[1]], dst_ref=lands[i].at[k],
                    send_sem=send_sems.at[3 * i + k], recv_sem=recv_sems.at[3 * i + k],
                    device_id=(chip[0], chip[1], c), device_id_type=MESH))
        return copies, [cp.wait for cp in copies]

    lands = [jax.ShapeDtypeStruct((3, SHARD_SHAPES[w][0] // 2, SHARD_SHAPES[w][1]), BF16) for w in ws]
    return _Rider([], p16s, lands, 3 * len(ws), make)


def _final_rider(halves):
    def make(inplace, srcs, lands, send_sems, recv_sems):
        del inplace
        x, y, c = _mesh_pos()
        copies = [pltpu.make_async_remote_copy(
            src_ref=srcs[i], dst_ref=lands[i], send_sem=send_sems.at[i], recv_sem=recv_sems.at[i],
            device_id=(x, y, 1 - c), device_id_type=MESH) for i in range(len(halves))]
        return copies, [cp.wait for cp in copies]

    return _Rider([], halves, [jax.ShapeDtypeStruct(h.shape, h.dtype) for h in halves], len(halves), make)


def _comm_only(name, riders):
    _, res = _call(lambda: None, name=name, grid=(), in_specs=[], out_specs=[], out_shape=[], operands=(),
                   riders=riders)
    return res


class _SemList:
    def __init__(self, refs):
        self.at = list(refs)


def _merged_rider(riders):
    srcs = [a for r in riders for a in r.srcs]
    lands = [a for r in riders for a in r.lands]

    def make(inplace, src_refs, land_refs, send_sems, recv_sems):
        starts, waits = [], []
        s0 = l0 = c0 = 0
        for r in riders:
            part = r.make(inplace, src_refs[s0:s0 + len(r.srcs)], land_refs[l0:l0 + len(r.lands)],
                          _SemList(send_sems.at[c0:c0 + r.n_copies]), _SemList(recv_sems.at[c0:c0 + r.n_copies]))
            starts += part[0]
            waits += part[1]
            s0, l0, c0 = s0 + len(r.srcs), l0 + len(r.lands), c0 + r.n_copies
        return starts, waits

    return _Rider([], srcs, lands, sum(r.n_copies for r in riders), make)


def _split_start(name, rider, sibling_barrier=None):
    assert not rider.inplace
    ns, nl, n = len(rider.srcs), len(rider.lands), rider.n_copies

    def body(*refs):
        if sibling_barrier is not None:
            x, y, c = _mesh_pos()
            barrier = pltpu.get_barrier_semaphore()
            pl.semaphore_signal(barrier, inc=1, device_id=(x, y, 1 - c), device_id_type=MESH)
            pl.semaphore_wait(barrier, 1)
        srcs, lands = refs[:ns], refs[ns:ns + nl]
        sems = refs[ns + nl:ns + nl + 2 * n]
        token = refs[-1]
        starts, _ = rider.make([], srcs, lands, _SemList(sems[:n]), _SemList(sems[n:]))
        for cp in starts:
            cp.start()
        token[...] = jnp.zeros_like(token)

    buffers = [pltpu.with_memory_space_constraint(a, pltpu.HBM) for a in rider.srcs]
    buffers += [pltpu.with_memory_space_constraint(lax.empty(s.shape, s.dtype), pltpu.HBM) for s in rider.lands]
    hbm = pl.BlockSpec(memory_space=pltpu.HBM)
    sem = pl.BlockSpec(memory_space=pltpu.SEMAPHORE)
    outs = pl.pallas_call(
        body, name=name,
        out_shape=tuple([pltpu.SemaphoreType.DMA(())] * (2 * n) + [pltpu.HBM(b.shape, b.dtype) for b in buffers]
                        + [jax.ShapeDtypeStruct((8, 128), F32)]),
        in_specs=[hbm] * (ns + nl),
        out_specs=tuple([sem] * (2 * n) + [hbm] * (ns + nl) + [_whole()]),
        input_output_aliases={i: 2 * n + i for i in range(ns + nl)},
        compiler_params=pltpu.CompilerParams(has_side_effects=pltpu.SideEffectType.DATAFLOW_SIDE_EFFECTING,
                                             collective_id=sibling_barrier),
    )(*buffers)
    return (rider, outs[:2 * n], outs[2 * n:2 * n + ns + nl]), outs[-1]


def _split_parts(state, riders):
    merged, sems, buffers = state
    n, ns = merged.n_copies, len(merged.srcs)
    parts, s0, l0, c0 = [], 0, 0, 0
    for r in riders:
        parts.append((r, list(sems[c0:c0 + r.n_copies]) + list(sems[n + c0:n + c0 + r.n_copies]),
                      list(buffers[s0:s0 + len(r.srcs)]) + list(buffers[ns + l0:ns + l0 + len(r.lands)])))
        s0, l0, c0 = s0 + len(r.srcs), l0 + len(r.lands), c0 + r.n_copies
    return parts


def _split_wait(name, state, after):
    rider, sems, buffers = state
    ns, nl, n = len(rider.srcs), len(rider.lands), rider.n_copies

    def body(*refs):
        srcs, lands = refs[:ns], refs[ns:ns + nl]
        sem_refs = refs[ns + nl:ns + nl + 2 * n]
        _, waits = rider.make([], srcs, lands, _SemList(sem_refs[:n]), _SemList(sem_refs[n:]))
        for wait in waits:
            wait()

    hbm = pl.BlockSpec(memory_space=pltpu.HBM)
    sem = pl.BlockSpec(memory_space=pltpu.SEMAPHORE)
    outs = pl.pallas_call(
        body, name=name,
        out_shape=tuple(pltpu.HBM(b.shape, b.dtype) for b in buffers),
        in_specs=[hbm] * (ns + nl) + [sem] * (2 * n) + [HBM_SPEC],
        out_specs=tuple([hbm] * (ns + nl)),
        input_output_aliases={i: i for i in range(ns + nl)},
        compiler_params=pltpu.CompilerParams(has_side_effects=pltpu.SideEffectType.DATAFLOW_SIDE_EFFECTING),
    )(*buffers, *sems, after)
    return list(outs[:ns]), list(outs[ns:])


def _pair_sum(pos, ws, g32s, recvs):
    n = len(ws)

    def body(pos_ref, *refs):
        g_refs, r_refs = refs[:n], refs[n:2 * n]
        p32_refs, p16_refs = refs[2 * n:3 * n], refs[3 * n:]
        for i in range(n):
            tot = g_refs[i][...] + r_refs[i][...].astype(F32)
            p16_refs[i][...] = tot.astype(BF16)

            @pl.when(pl.program_id(0) == pos_ref[1])
            def _(i=i, tot=tot):
                p32_refs[i][...] = tot

    halves = [(SHARD_SHAPES[w][0] // 2, SHARD_SHAPES[w][1]) for w in ws]
    own = [pl.BlockSpec((None, None) + h, lambda j, pos_ref: (j, pos_ref[0], 0, 0)) for h in halves]
    blk = [pl.BlockSpec((None,) + h, lambda j, pos_ref: (j, 0, 0)) for h in halves]
    mine = [pl.BlockSpec(h, lambda j, pos_ref: (0, 0)) for h in halves]
    g4 = [g.reshape((N_SHARD, 2) + h) for g, h in zip(g32s, halves)]
    outs = pl.pallas_call(
        body, name="pair_sum_" + "_".join(str(w) for w in ws),
        grid_spec=pltpu.PrefetchScalarGridSpec(
            num_scalar_prefetch=1, grid=(N_SHARD,), in_specs=own + blk, out_specs=mine + blk),
        out_shape=[jax.ShapeDtypeStruct(h, F32) for h in halves]
        + [jax.ShapeDtypeStruct((N_SHARD,) + h, BF16) for h in halves],
        compiler_params=_params(("arbitrary",)),
    )(pos, *g4, *recvs)
    return outs[:n], outs[n:]


def _chip_sum(pos, p32s, recvs):
    parts = 2

    def body(pos_ref, *refs):
        del pos_ref
        p_refs, r_refs, f_refs = refs[:N_BIG], refs[N_BIG:2 * N_BIG], refs[2 * N_BIG:]
        for w in range(N_BIG):
            f_refs[w][...] = ((p_refs[w][...] + r_refs[w][0].astype(F32)) + r_refs[w][1].astype(F32)) \
                + r_refs[w][2].astype(F32)

    quarters = [(r // 2 // parts, cc) for r, cc in SHARD_SHAPES]
    own = [pl.BlockSpec(qt, lambda i, pos_ref: (i, 0)) for qt in quarters]
    rcv = [pl.BlockSpec((3,) + qt, lambda i, pos_ref: (0, i, 0)) for qt in quarters]
    out = [pl.BlockSpec(qt, lambda i, pos_ref: (i, 0)) for qt in quarters]
    return pl.pallas_call(
        body, name="chip_sum",
        grid_spec=pltpu.PrefetchScalarGridSpec(
            num_scalar_prefetch=1, grid=(parts,), in_specs=own + rcv, out_specs=out),
        out_shape=[jax.ShapeDtypeStruct((r // 2, cc), F32) for r, cc in SHARD_SHAPES],
        compiler_params=_params(("arbitrary",)),
    )(pos, *p32s, *recvs)


def _adamw(w, g, m, v):
    m_new = ADAM_B1 * m + (1.0 - ADAM_B1) * g
    v_new = ADAM_B2 * v + (1.0 - ADAM_B2) * (g * g)
    m_hat = m_new / (1.0 - ADAM_B1 ** ADAM_STEP)
    v_hat = v_new / (1.0 - ADAM_B2 ** ADAM_STEP)
    delta = -ADAM_LR * (m_hat / (jnp.sqrt(v_hat) + ADAM_EPS) + ADAM_WD * w)
    return delta, m_new, v_new


def _adam_half(name, which, grads, ws, ms, vs, into=None):
    nb = 4

    def body(which_ref, *refs):
        del which_ref
        groups = [refs[i * N_BIG:(i + 1) * N_BIG] for i in range(4)]
        g_refs, w_refs, m_refs, v_refs = groups
        go_refs, do_refs, mo_refs, vo_refs = [refs[len(refs) - (4 - i) * N_BIG:len(refs) - (3 - i) * N_BIG]
                                              for i in range(4)]
        for w in range(N_BIG):
            g = g_refs[w][...]
            delta, m_new, v_new = _adamw(w_refs[w][...], g, m_refs[w][...], v_refs[w][...])
            go_refs[w][...] = g
            do_refs[w][...] = delta
            mo_refs[w][...] = m_new
            vo_refs[w][...] = v_new

    blocks = [(r // 2 // nb, cc) for r, cc in SHARD_SHAPES]
    half = [pl.BlockSpec(b, lambda i, which_ref: (i, 0)) for b in blocks]
    full = [pl.BlockSpec((None,) + b, lambda i, which_ref: (0, which_ref[0] * nb + i, 0)) for b in blocks]
    shapes = [jax.ShapeDtypeStruct((1,) + shp, F32) for shp in SHARD_SHAPES]
    carried = [] if into is None else [a for kind in into for a in kind]
    first = 1 + 4 * N_BIG
    outs = pl.pallas_call(
        body, name=name,
        grid_spec=pltpu.PrefetchScalarGridSpec(
            num_scalar_prefetch=1, grid=(nb,), in_specs=half + full * 3 + [HBM_SPEC] * len(carried),
            out_specs=full * 4),
        out_shape=shapes * 4,
        input_output_aliases={first + i: i for i in range(len(carried))},
        compiler_params=_params(("arbitrary",)),
    )(which, *grads, *ws, *ms, *vs, *carried)
    return [outs[i * N_BIG:(i + 1) * N_BIG] for i in range(4)]


SMALL_ROWS = 8
ROW_CONV_B, ROW_POOL_SCALE, ROW_LN1_G, ROW_LN1_B, ROW_LN2_G, ROW_LN2_B, ROW_LOSS = range(7)
SMALL_VECS = ((ROW_CONV_B, D_FF), (ROW_POOL_SCALE, POOL_W), (ROW_LN1_G, D_MODEL), (ROW_LN1_B, D_MODEL),
              (ROW_LN2_G, D_MODEL), (ROW_LN2_B, D_MODEL))


def _small_pack(loss, vec_grads):
    def body(*refs):
        loss_ref, gvec, out_ref = refs[0], refs[1:-1], refs[-1]
        out_ref[...] = jnp.zeros_like(out_ref)
        for (row, n), ref in zip(SMALL_VECS, gvec):
            out_ref[row:row + 1, 0:n] = ref[...]
        out_ref[ROW_LOSS:ROW_LOSS + 1, 0:HEAD_DIM] = jnp.broadcast_to(loss_ref[...], (1, HEAD_DIM))

    return pl.pallas_call(
        body, name="small_pack", in_specs=[_whole()] * (1 + len(vec_grads)), out_specs=_whole(),
        out_shape=jax.ShapeDtypeStruct((SMALL_ROWS, D_FF), F32),
    )(loss, *vec_grads)


def _small_pair_sum(own, sibling):
    n = len(own)

    def body(*refs):
        x, y, _ = _mesh_pos()
        for i in range(n):
            refs[2 * n + i][2 * x + y] = refs[i][...] + refs[n + i][...]

    return pl.pallas_call(
        body, name="small_pair_sum", in_specs=[_whole()] * (2 * n), out_specs=[_whole()] * n,
        out_shape=[jax.ShapeDtypeStruct((N_SHARD,) + a.shape, F32) for a in own],
        compiler_params=pltpu.CompilerParams(vmem_limit_bytes=VMEM_LIMIT),
    )(*own, *sibling)


def _small_chip_rider(gathered):
    n = len(gathered)

    def make(inplace, srcs, lands, send_sems, recv_sems):
        del inplace, lands
        x, y, c = _mesh_pos()
        j0 = 2 * x + y
        starts, waits = [], []
        for i in range(n):
            for k, chip in enumerate(_other_chips(x, y)):
                sems = dict(send_sem=send_sems.at[3 * i + k], recv_sem=recv_sems.at[3 * i + k],
                            device_id=(chip[0], chip[1], c), device_id_type=MESH)
                send = pltpu.make_async_remote_copy(src_ref=srcs[i].at[j0], dst_ref=srcs[i].at[j0], **sems)
                arrival = pltpu.make_async_remote_copy(
                    src_ref=srcs[i].at[j0], dst_ref=srcs[i].at[2 * chip[0] + chip[1]], **sems)
                starts.append(send)
                waits += [arrival.wait_recv, send.wait_send]
        return starts, waits

    return _Rider([], gathered, [], 3 * n, make)


def _small_adam(all_a, all_b, all_c, wp, cwp, vec_ws, m_wp, m_cwp, vec_ms, v_wp, v_cwp, vec_vs):
    nv = len(SMALL_VECS)
    np_ = 2 + nv

    def body(*refs):
        all_a_ref, all_b_ref, all_c_ref = refs[0:3]
        w_all, m_all, v_all = (refs[3 + i * np_:3 + (i + 1) * np_] for i in range(3))
        loss_out = refs[3 + 3 * np_]
        outs = refs[4 + 3 * np_:]
        x, y, _ = _mesh_pos()
        j0 = 2 * x + y
        tot_a = ((all_a_ref[0] + all_a_ref[1]) + all_a_ref[2]) + all_a_ref[3]
        tot_b = ((all_b_ref[0] + all_b_ref[1]) + all_b_ref[2]) + all_b_ref[3]
        tot_c = ((all_c_ref[0, j0] + all_c_ref[1, j0]) + all_c_ref[2, j0]) + all_c_ref[3, j0]
        loss_out[...] = tot_b[ROW_LOSS:ROW_LOSS + 1, 0:1]
        grads = [tot_a, tot_c] + [tot_b[row:row + 1, 0:n] for row, n in SMALL_VECS]
        for p in range(np_):
            delta, m_new, v_new = _adamw(w_all[p][...], grads[p], m_all[p][...], v_all[p][...])
            outs[p][...] = grads[p]
            outs[np_ + p][...] = delta
            outs[2 * np_ + p][...] = m_new
            outs[3 * np_ + p][...] = v_new

    pshapes = [wp.shape, CW_SHARD] + [wv.shape for wv in vec_ws]
    out_shape = [jax.ShapeDtypeStruct((1, 1), F32)] + [jax.ShapeDtypeStruct(s, F32) for s in pshapes] * 4
    outs = pl.pallas_call(
        body, name="small_adam",
        in_specs=[_whole()] * (3 + 3 * np_), out_specs=[_whole()] * len(out_shape), out_shape=out_shape,
        compiler_params=pltpu.CompilerParams(vmem_limit_bytes=VMEM_LIMIT),
    )(all_a, all_b, all_c, wp, cwp, *vec_ws, m_wp, m_cwp, *vec_ms, v_wp, v_cwp, *vec_vs)
    return outs[0], [outs[1 + i * np_:1 + (i + 1) * np_] for i in range(4)]


def kernel(x, w_in, w_pool, pool_scale, w_out, ln1_g, ln1_b, w_up, conv_w, conv_b, w_down, ln2_g, ln2_b, loss_target, m_w_in, m_w_pool, m_pool_scale, m_w_out, m_ln1_g, m_ln1_b, m_w_up, m_conv_w, m_conv_b, m_w_down, m_ln2_g, m_ln2_b, v_w_in, v_w_pool, v_pool_scale, v_w_out, v_ln1_g, v_ln1_b, v_w_up, v_conv_w, v_conv_b, v_w_down, v_ln2_g, v_ln2_b):
    pos = jnp.stack([lax.axis_index("c"), 2 * lax.axis_index("x") + lax.axis_index("y")]).astype(jnp.int32)
    order = ("w_in", "w_out", "w_up", "w_down")
    w_in_i, w_out_i, w_up_i, w_down_i = range(N_BIG)
    vec_names = ("conv_b", "pool_scale", "ln1_g", "ln1_b", "ln2_g", "ln2_b")

    gathered = _gather_weights([w_in[0], w_out[0], w_up[0], w_down[0]], conv_w[0], (w_in_i,))
    cw_full = jnp.transpose(gathered[N_BIG], (1, 0, 2)).reshape(3, D_FF)
    up_a, up_b, up_c = (0, 176), (176, 176), (352, 160)
    assert up_c[0] + up_c[1] == SHARD_SHAPES[w_up_i][0] // 2

    class MeshComm:
        def __init__(self):
            self.w = {i: gathered[i] for i in range(N_BIG)}
            self.g32, self.g16, self.p32, self.p16, self.recv_b = {}, {}, {}, {}, {}
            self.up_complete = False
            self.tokens, self.chips = {}, []

        def weight(self, name):
            i = order.index(name)
            if name == "w_up" and not self.up_complete:
                (arrs, _), = _comm_only("gather_up_last", [_gather_rider(
                    {i: self.w[i]}, [("d2d_diag", i, up_b), ("d2d", i, up_c)])])
                self.w[i], self.up_complete = arrs[0], True
            full = self.w[i]
            return full.reshape(-1, full.shape[-1]) if name in ("w_out", "w_down") else full

        def _gather(self, ws, ops):
            return _gather_rider({w: self.w[w] for w in ws}, ops), ("w", ws)

        def _pair(self, ws):
            return _pair_rider(ws, [self.g16[w] for w in ws]), ("recv_a", ws)

        def _chip(self, ws):
            return _chip_rider(ws, [self.p16[w] for w in ws]), ("recv_b", ws)

        def plan(self, call):
            out_all, down_all = _whole_half(w_out_i), _whole_half(w_down_i)
            if call == "proj_pool":
                return [self._gather([w_out_i, w_up_i, w_down_i],
                                     [("ici", w_out_i, out_all), ("nbr", w_down_i, down_all),
                                      ("nbr", w_up_i, up_a)])]
            if call == "retention_fwd":
                return [self._gather([w_out_i, w_up_i, w_down_i],
                                     [("d2d", w_out_i, out_all),
                                      ("relay", w_down_i, down_all), ("d2d_nbr", w_down_i, down_all),
                                      ("relay", w_up_i, up_a), ("d2d_nbr", w_up_i, up_a), ("nbr", w_up_i, up_b)])]
            if call == "outproj_ln1":
                return [self._gather([w_up_i, w_down_i],
                                     [("d2d_diag", w_down_i, down_all), ("d2d_diag", w_up_i, up_a),
                                      ("relay", w_up_i, up_b), ("d2d_nbr", w_up_i, up_b), ("ici", w_up_i, up_c)])]
            return []

        def after(self, call):
            return tuple(self.tokens.pop(call, ()))

        def riders(self, call):
            self.pending = self.plan(call)
            return [r for r, _ in self.pending]

        def _start(self, name, rider, before, sibling_barrier=None):
            state, token = _split_start(name, rider, sibling_barrier)
            self.tokens.setdefault(before, []).append(token)
            return state

        def _finish_pair(self, name, state, ws, after):
            _, lands = _split_wait(name, state, after)
            self._finish_sum(ws, lands)

        def landed(self, call, results, outs):
            for (_, (slot, ws)), (inplace, lands) in zip(self.pending, results):
                for w, arr in zip(ws, inplace if len(inplace) else lands):
                    getattr(self, slot)[w] = arr
            if call == "wgrad_out":
                self._finish_pair("pair_exchange_up_wait", self.pair_up, [w_up_i], outs[1])
                self.chips.append(([w_up_i], self._start(
                    "chip_exchange_up_start", self._chip([w_up_i])[0], "wgrad_down")))
            if call == "mix_bwd":
                ws = [w_out_i, w_down_i]
                self._finish_pair("pair_exchange_out_down_wait", self.pair_out_down, ws, outs[0])
            if call == "retention_bwd":
                own, sibling = _split_wait("small_pair_wait", self.small_pair, outs[0])
                self.small_chip = self._start(
                    "small_chip_start", _small_chip_rider(_small_pair_sum(own, sibling)), "wgrad_in")

        def small_gradients(self, loss, small):
            dcw4 = jnp.transpose(small["conv_w"].reshape(3, N_SHARD, DOWN_SH), (1, 0, 2))
            own = [small["w_pool"], _small_pack(loss, [small[n] for n in vec_names]), dcw4]
            ws = [w_out_i, w_down_i]
            parts = [self._chip(ws)[0], _final_rider(own)]
            chip, self.small_pair = _split_parts(
                self._start("chip_out_down_small_pair_start", _merged_rider(parts), "retention_bwd"), parts)
            self.chips.append((ws, chip))

        def gradient(self, name, g32, g16):
            w = order.index(name)
            shape = (N_SHARD,) + SHARD_SHAPES[w]
            self.g32[w], self.g16[w] = g32.reshape(shape), g16.reshape(shape)
            if name == "w_up":
                self.pair_up = self._start("pair_exchange_up_start", self._pair([w])[0], "wgrad_out", 1)
            if name == "w_down":
                self.pair_out_down = self._start("pair_exchange_out_down_start",
                                                 self._pair([w_out_i, w_down_i])[0], "mix_bwd", 2)
            if name == "w_in":
                (_, lands), = _comm_only("pair_exchange_in", [self._pair([w])[0]])
                self._finish_sum([w], lands)
                self.chips.append(([w], self._start("chip_exchange_in_start", self._chip([w])[0], "dx")))

        def _finish_sum(self, ws, lands):
            p32s, p16s = _pair_sum(pos, ws, [self.g32[w] for w in ws], lands)
            for w, p32, p16 in zip(ws, p32s, p16s):
                self.p32[w], self.p16[w] = p32, p16

        def finish(self, after):
            for n, (ws, state) in enumerate(self.chips):
                _, lands = _split_wait("chip_exchange_wait_%d" % n, state, after)
                for w, arr in zip(ws, lands):
                    self.recv_b[w] = arr
            return _split_wait("small_chip_wait", self.small_chip, after)[0]

    comm = MeshComm()
    loss, grad_x, small = _local_step(x[0], loss_target[0], cw_full, conv_b, w_pool[0], pool_scale,
                                      ln1_g, ln1_b, ln2_g, ln2_b, comm)

    given = dict(w_pool=w_pool, pool_scale=pool_scale, ln1_g=ln1_g, ln1_b=ln1_b, conv_w=conv_w, conv_b=conv_b,
                 ln2_g=ln2_g, ln2_b=ln2_b)
    given_m = dict(w_pool=m_w_pool, pool_scale=m_pool_scale, ln1_g=m_ln1_g, ln1_b=m_ln1_b, conv_w=m_conv_w,
                   conv_b=m_conv_b, ln2_g=m_ln2_g, ln2_b=m_ln2_b)
    given_v = dict(w_pool=v_w_pool, pool_scale=v_pool_scale, ln1_g=v_ln1_g, ln1_b=v_ln1_b, conv_w=v_conv_w,
                   conv_b=v_conv_b, ln2_g=v_ln2_g, ln2_b=v_ln2_b)
    args = []
    for src in (given, given_m, given_v):
        args += [src["w_pool"][0], src["conv_w"][0], [src[n] for n in vec_names]]
    small_sums = comm.finish(grad_x)
    loss_tot, small_out = _small_adam(*small_sums, *args)
    every = range(N_BIG)
    mine = _chip_sum(pos, [comm.p32[w] for w in every], [comm.recv_b[w] for w in every])
    final_state, _ = _split_start("pair_exchange_f32_start", _final_rider(mine), 3)
    mine = final_state[2][:N_BIG]
    big = ([w_in, w_out, w_up, w_down], [m_w_in, m_w_out, m_w_up, m_w_down], [v_w_in, v_w_out, v_w_up, v_w_down])
    own_half = _adam_half("adam_own_half", pos[0:1], mine, *big)
    _, theirs = _split_wait("pair_exchange_f32_wait", final_state, own_half[0][0])
    big_out = _adam_half("adam_other_half", 1 - pos[0:1], theirs, *big, into=own_half)

    names = ("w_in", "w_pool", "pool_scale", "w_out", "ln1_g", "ln1_b", "w_up", "conv_w", "conv_b", "w_down",
             "ln2_g", "ln2_b")
    small_names = ("w_pool", "conv_w") + vec_names
    result = [loss_tot.reshape(()), grad_x[None]]
    for kind in range(4):
        for n in names:
            if n in order:
                result.append(big_out[kind][order.index(n)])
            else:
                val = small_out[kind][small_names.index(n)]
                if n == "conv_w":
                    val = val[None]
                elif n == "w_pool":
                    val = val[None]
                result.append(val)
    return tuple(result)
```

```python
import functools

import numpy as np
import jax
import jax.numpy as jnp
from jax import lax
from jax.experimental import pallas as pl
from jax.experimental.pallas import tpu as pltpu

F32 = jnp.float32
BF16 = jnp.bfloat16

D_MODEL = 1024
HEADS = 4
HEAD_DIM = 128
RET_W = HEADS * HEAD_DIM
POOL_WINDOWS = (2, 4, 8, 16)
POOL_W = 512
IN_W = 4 * RET_W + POOL_W
D_FF = 2816
N_SHARD = 4
IN_SH = IN_W // N_SHARD
UP_SH = 2 * D_FF // N_SHARD
DOWN_SH = D_FF // N_SHARD
OUT_SH = D_MODEL // N_SHARD
ROPE_BASE = 10000.0
LN_EPS = 1e-5
RMS_EPS = 1e-6
ALPHA = 2.0 ** 0.25
K_SCALE = HEAD_DIM ** -0.5
SUPER = 256
CHUNK = 64
POOL_HALO = 16
CONV_HALO = 8
FFN_STRIP = 128
LN_ROWS = 32

ADAM_LR = 0.001
ADAM_B1 = 0.9
ADAM_B2 = 0.999
ADAM_EPS = 1e-08
ADAM_WD = 0.01
ADAM_STEP = 10

MESH = pl.DeviceIdType.MESH
VMEM_LIMIT = 56 * 1024 * 1024


def _dot(a, b):
    return jnp.dot(a, b, preferred_element_type=F32)


def _dot_nt(a, b):
    return lax.dot_general(a, b, (((1,), (1,)), ((), ())), preferred_element_type=F32)


def _dot_tn(a, b):
    return lax.dot_general(a, b, (((0,), (0,)), ((), ())), preferred_element_type=F32)


def _sigmoid(x):
    return 1.0 / (1.0 + jnp.exp(-x))


def _params(sem):
    return pltpu.CompilerParams(dimension_semantics=sem, vmem_limit_bytes=VMEM_LIMIT)


def _whole():
    return pl.BlockSpec(memory_space=pltpu.VMEM)


HBM_SPEC = pl.BlockSpec(memory_space=pl.ANY)


class _Rider:
    def __init__(self, inplace, srcs, lands, n_copies, make):
        self.inplace, self.srcs, self.lands, self.n_copies, self.make = list(inplace), list(srcs), list(lands), n_copies, make


def _call(body, *, name, grid, in_specs, out_specs, out_shape, operands, scratch_shapes=(), sem=(),
          aliases=None, riders=(), after=()):
    n_in, n_out, n_scr = len(in_specs), len(out_shape), len(scratch_shapes)
    in_specs, out_specs, out_shape = list(in_specs), list(out_specs), list(out_shape)
    operands, scratch_shapes, aliases = list(operands), list(scratch_shapes), dict(aliases or {})
    in_specs += [_whole()] * len(after)
    operands += list(after)
    for r in riders:
        for a in r.inplace:
            aliases[len(in_specs)] = len(out_shape)
            in_specs.append(HBM_SPEC)
            operands.append(a)
            out_specs.append(HBM_SPEC)
            out_shape.append(jax.ShapeDtypeStruct(a.shape, a.dtype))
        for a in r.srcs:
            in_specs.append(HBM_SPEC)
            operands.append(a)
        for shp in r.lands:
            out_specs.append(HBM_SPEC)
            out_shape.append(shp)
        scratch_shapes += [pltpu.SemaphoreType.DMA((r.n_copies,)), pltpu.SemaphoreType.DMA((r.n_copies,))]

    def full(*refs):
        ins = refs[:n_in]
        at = n_in + len(after)
        r_srcs = []
        for r in riders:
            at += len(r.inplace)
            r_srcs.append(refs[at:at + len(r.srcs)])
            at += len(r.srcs)
        outs = refs[at:at + n_out]
        at += n_out
        r_outs = []
        for r in riders:
            r_outs.append((refs[at:at + len(r.inplace)], refs[at + len(r.inplace):at + len(r.inplace) + len(r.lands)]))
            at += len(r.inplace) + len(r.lands)
        scr = refs[at:at + n_scr]
        at += n_scr
        r_sems = [refs[at + 2 * i:at + 2 * i + 2] for i in range(len(riders))]

        def copies():
            return [r.make(r_outs[i][0], r_srcs[i], r_outs[i][1], r_sems[i][0], r_sems[i][1])
                    for i, r in enumerate(riders)]

        def start():
            for starts, _ in copies():
                for cp in starts:
                    cp.start()

        def finish():
            for _, waits in copies():
                for wait in waits:
                    wait()

        if riders and grid:
            first = functools.reduce(jnp.logical_and, [pl.program_id(d) == 0 for d in range(len(grid))])
            last = functools.reduce(jnp.logical_and, [pl.program_id(d) == grid[d] - 1 for d in range(len(grid))])
            pl.when(first)(start)
            body(*ins, *outs, *scr)
            pl.when(last)(finish)
        else:
            if riders:
                start()
            body(*ins, *outs, *scr)
            if riders:
                finish()

    params = _params(sem) if grid else pltpu.CompilerParams(vmem_limit_bytes=VMEM_LIMIT)
    res = pl.pallas_call(
        full, name=name, grid=grid, in_specs=in_specs, out_specs=out_specs, out_shape=out_shape,
        scratch_shapes=scratch_shapes, input_output_aliases=aliases, compiler_params=params,
    )(*operands)
    outs, at, rider_res = res[:n_out], n_out, []
    for r in riders:
        rider_res.append((res[at:at + len(r.inplace)], res[at + len(r.inplace):at + len(r.inplace) + len(r.lands)]))
        at += len(r.inplace) + len(r.lands)
    return list(outs), rider_res


def _gammas():
    return [1.0 - 2.0 ** (-5.0 - h) for h in range(HEADS)]


def _decay_tables():
    idx = np.arange(SUPER)
    dist = np.abs(idx[:, None] - idx[None, :]).astype(np.float64)
    visible = (idx[None, :] // CHUNK) <= (idx[:, None] // CHUNK)
    mask = np.stack([np.where(visible, g ** dist, 0.0) for g in _gammas()])
    qd = np.concatenate([np.repeat((g ** (idx + 1.0))[:, None], HEAD_DIM, 1) for g in _gammas()], 1)
    kd = np.concatenate([np.repeat((g ** (SUPER - 1.0 - idx))[:, None], HEAD_DIM, 1) for g in _gammas()], 1)
    return (jnp.asarray(mask, F32), jnp.asarray(qd, F32), jnp.asarray(kd, F32))


def _rope_tables(s):
    inv_freq = ROPE_BASE ** (-np.arange(0, HEAD_DIM, 2, dtype=np.float64) / HEAD_DIM)
    ang = np.arange(s, dtype=np.float64)[:, None] * inv_freq[None, :]
    cos, sin = np.cos(ang), np.sin(ang)
    return (jnp.asarray(np.concatenate([cos, cos], 1), F32),
            jnp.asarray(np.concatenate([-sin, sin], 1), F32))


def _rope(t, cosf, sinf):
    return t * cosf + pltpu.roll(t, HEAD_DIM // 2, 1) * sinf


def _rope_t(t, cosf, sinf):
    return t * cosf - pltpu.roll(t, HEAD_DIM // 2, 1) * sinf


def _layernorm_fwd(z):
    mu = jnp.mean(z, axis=-1, keepdims=True)
    zc = z - mu
    var = jnp.mean(zc * zc, axis=-1, keepdims=True)
    rstd = lax.rsqrt(var + LN_EPS)
    return zc * rstd, rstd


def _layernorm_bwd(dy, xhat, rstd, gain):
    dxh = dy * gain
    m1 = jnp.mean(dxh, axis=-1, keepdims=True)
    m2 = jnp.mean(dxh * xhat, axis=-1, keepdims=True)
    return rstd * (dxh - m1 - xhat * m2)


def _proj_pool(x, win4, cosf, sinf, wpool, pscale, ts, riders=(), after=()):
    s = x.shape[0]
    nt = s // ts

    def body(x_ref, w_ref, cos_ref, sin_ref, wp_ref, ps_ref,
             xb_ref, q_ref, k_ref, v_ref, g_ref, pooled_ref, cat_ref, proj_scr, pext_scr):
        i = pl.program_id(0)
        xb = x_ref[...].astype(BF16)
        xb_ref[...] = xb
        for j in range(N_SHARD):
            proj_scr[:, j * IN_SH:(j + 1) * IN_SH] = _dot(xb, w_ref[j])
        cosf_t = cos_ref[...]
        sinf_t = sin_ref[...]
        for h in range(HEADS):
            lo = h * HEAD_DIM
            q_ref[:, lo:lo + HEAD_DIM] = _rope(proj_scr[:, lo:lo + HEAD_DIM], cosf_t, sinf_t).astype(BF16)
            kk = _rope(proj_scr[:, RET_W + lo:RET_W + lo + HEAD_DIM], cosf_t, sinf_t) * K_SCALE
            k_ref[:, lo:lo + HEAD_DIM] = kk.astype(BF16)
        v_ref[...] = proj_scr[:, 2 * RET_W:3 * RET_W].astype(BF16)
        g_ref[...] = proj_scr[:, 3 * RET_W:4 * RET_W]

        @pl.when(i == 0)
        def _():
            pext_scr[0:POOL_HALO, :] = jnp.zeros((POOL_HALO, POOL_W), F32)

        pext_scr[POOL_HALO:POOL_HALO + ts, :] = proj_scr[:, 4 * RET_W:IN_W]
        pos = (i * ts + lax.broadcasted_iota(jnp.int32, (ts, 1), 0) + 1).astype(F32)
        for gi, w in enumerate(POOL_WINDOWS):
            lo = gi * HEAD_DIM
            ext = pext_scr[:, lo:lo + HEAD_DIM]
            acc = ext
            shift = 1
            while shift < w:
                acc = acc + pltpu.roll(acc, shift, 0)
                shift *= 2
            tok = ext[POOL_HALO:POOL_HALO + ts]
            pooled = acc[POOL_HALO:POOL_HALO + ts] / jnp.minimum(pos, float(w)) - tok
            pooled_b = pooled.astype(BF16)
            pooled_ref[:, lo:lo + HEAD_DIM] = pooled_b
            lin = _dot(pooled_b, wp_ref[gi])
            cat_ref[:, lo:lo + HEAD_DIM] = (lin * ps_ref[:, lo:lo + HEAD_DIM]).astype(BF16)
        pext_scr[0:POOL_HALO, :] = pext_scr[ts:ts + POOL_HALO, :]

    tile = lambda w: pl.BlockSpec((ts, w), lambda i: (i, 0))
    return _call(
        body, name="proj_pool", grid=(nt,),
        in_specs=[tile(D_MODEL), _whole(), tile(HEAD_DIM), tile(HEAD_DIM), _whole(), _whole()],
        out_specs=[tile(D_MODEL), tile(RET_W), tile(RET_W), tile(RET_W), tile(RET_W), tile(POOL_W),
                   pl.BlockSpec((ts, POOL_W), lambda i: (i, 1))],
        out_shape=[jax.ShapeDtypeStruct((s, D_MODEL), BF16), jax.ShapeDtypeStruct((s, RET_W), BF16),
                   jax.ShapeDtypeStruct((s, RET_W), BF16), jax.ShapeDtypeStruct((s, RET_W), BF16),
                   jax.ShapeDtypeStruct((s, RET_W), F32), jax.ShapeDtypeStruct((s, POOL_W), BF16),
                   jax.ShapeDtypeStruct((s, 2 * RET_W), BF16)],
        scratch_shapes=[pltpu.VMEM((ts, IN_W), F32), pltpu.VMEM((ts + POOL_HALO, POOL_W), F32)],
        sem=("arbitrary",), operands=(x, win4, cosf, sinf, wpool, pscale), riders=riders, after=after,
    )


def _retention_fwd(q, k, v, g, cat, mask, qd, kd, riders=(), after=()):
    s = q.shape[0]
    ns = s // SUPER
    cdec = [gm ** float(SUPER) for gm in _gammas()]

    def body(q_ref, k_ref, v_ref, g_ref, cat_in, mask_ref, qd_ref, kd_ref,
             ret_ref, cat_ref, st_ref, state_scr):
        del cat_in
        n = pl.program_id(0)

        @pl.when(n == 0)
        def _():
            state_scr[...] = jnp.zeros_like(state_scr)

        for h in range(HEADS):
            sl = slice(h * HEAD_DIM, (h + 1) * HEAD_DIM)
            qh, kh, vh = q_ref[:, sl], k_ref[:, sl], v_ref[:, sl]
            sc = _dot_nt(qh, kh) * mask_ref[h]
            st = state_scr[h]
            stb = st.astype(BF16)
            st_ref[0, h] = stb
            qdb = (qh.astype(F32) * qd_ref[:, sl]).astype(BF16)
            kdb = (kh.astype(F32) * kd_ref[:, sl]).astype(BF16)
            ret = _dot(sc.astype(BF16), vh) + _dot(qdb, stb)
            state_scr[h] = st * cdec[h] + _dot_tn(kdb, vh)
            ret_ref[:, sl] = ret
            r = lax.rsqrt(jnp.mean(ret * ret, axis=-1, keepdims=True) + RMS_EPS)
            gh = g_ref[:, sl]
            cat_ref[:, sl] = ((ret * r) * (gh * _sigmoid(gh))).astype(BF16)

    tile = pl.BlockSpec((SUPER, RET_W), lambda n: (n, 0))
    return _call(
        body, name="retention_fwd", grid=(ns,),
        in_specs=[tile, tile, tile, tile, HBM_SPEC, _whole(), _whole(), _whole()],
        out_specs=[tile, tile, pl.BlockSpec((1, HEADS, HEAD_DIM, HEAD_DIM), lambda n: (n, 0, 0, 0))],
        out_shape=[jax.ShapeDtypeStruct((s, RET_W), F32), jax.ShapeDtypeStruct((s, 2 * RET_W), BF16),
                   jax.ShapeDtypeStruct((ns, HEADS, HEAD_DIM, HEAD_DIM), BF16)],
        scratch_shapes=[pltpu.VMEM((HEADS, HEAD_DIM, HEAD_DIM), F32)],
        aliases={4: 1}, sem=("arbitrary",), operands=(q, k, v, g, cat, mask, qd, kd), riders=riders,
        after=after,
    )


def _outproj_ln1(x, cat, wout, g1, b1, ts, riders=(), after=()):
    s = x.shape[0]

    def body(x_ref, cat_ref, w_ref, g_ref, b_ref, xhat_ref, rstd_ref, h1b_ref):
        z = ALPHA * x_ref[...] + _dot(cat_ref[...], w_ref[...])
        xhat, rstd = _layernorm_fwd(z)
        xhat_ref[...] = xhat
        rstd_ref[...] = rstd
        h1b_ref[...] = (xhat * g_ref[...] + b_ref[...]).astype(BF16)

    tile = lambda w: pl.BlockSpec((ts, w), lambda i: (i, 0))
    return _call(
        body, name="outproj_ln1", grid=(s // ts,),
        in_specs=[tile(D_MODEL), tile(D_MODEL), _whole(), _whole(), _whole()],
        out_specs=[tile(D_MODEL), tile(1), tile(D_MODEL)],
        out_shape=[jax.ShapeDtypeStruct((s, D_MODEL), F32), jax.ShapeDtypeStruct((s, 1), F32),
                   jax.ShapeDtypeStruct((s, D_MODEL), BF16)],
        sem=("arbitrary",), operands=(x, cat, wout, g1, b1), riders=riders, after=after,
    )


def _ffn_fwd_loss(xhat1, h1b, target, wup4, wdown, cw, cb, g1, b1, g2, b2, ts):
    s = xhat1.shape[0]

    def body(xhat_ref, h1b_ref, tgt_ref, wup_ref, wdn_ref, cw_ref, cb_ref, g1_ref, b1_ref, g2_ref, b2_ref,
             ub_ref, act_ref, sd_ref, dz2_ref, dz2b_ref, loss_ref, dg2_ref, db2_ref, val_scr, gext_scr, ffn_scr):
        i = pl.program_id(0)

        @pl.when(i == 0)
        def _():
            gext_scr[0:CONV_HALO, :] = jnp.zeros((CONV_HALO, D_FF), F32)
            loss_ref[...] = jnp.zeros_like(loss_ref)
            dg2_ref[...] = jnp.zeros_like(dg2_ref)
            db2_ref[...] = jnp.zeros_like(db2_ref)

        for half in range(2):
            lo = half * UP_SH
            gext_scr[CONV_HALO:CONV_HALO + ts, lo:lo + UP_SH] = _dot(h1b_ref[...], wup_ref[2 + half])
            val_scr[:, lo:lo + UP_SH] = _dot(h1b_ref[...], wup_ref[half])
            for c0 in range(lo, lo + UP_SH, FFN_STRIP):
                cols = slice(c0, c0 + FFN_STRIP)
                ext = gext_scr[:, cols]
                gate = ext[CONV_HALO:]
                hc = cb_ref[:, cols] + ((pltpu.roll(ext, 2, 0)[CONV_HALO:] * cw_ref[0:1, cols]
                                         + pltpu.roll(ext, 1, 0)[CONV_HALO:] * cw_ref[1:2, cols])
                                        + gate * cw_ref[2:3, cols])
                val = val_scr[:, cols]
                sg = _sigmoid(hc)
                si = hc * sg
                act_ref[:, cols] = (si * val).astype(BF16)
                ub_ref[:, cols] = val.astype(BF16)
                ub_ref[:, D_FF + c0:D_FF + c0 + FFN_STRIP] = gate.astype(BF16)
                sd_ref[:, cols] = hc.astype(BF16)
            part = _dot(act_ref[:, lo:lo + UP_SH], wdn_ref[lo:lo + UP_SH, :])
            if half == 0:
                ffn_scr[...] = part
            else:
                ffn_scr[...] += part

        gext_scr[0:CONV_HALO, :] = gext_scr[ts:ts + CONV_HALO, :]

        loss_acc = jnp.zeros((1, 1), F32)
        dg2_acc = jnp.zeros((1, D_MODEL), F32)
        db2_acc = jnp.zeros((1, D_MODEL), F32)
        for r0 in range(0, ts, LN_ROWS):
            rows = slice(r0, r0 + LN_ROWS)
            h1 = xhat_ref[rows, :] * g1_ref[...] + b1_ref[...]
            xhat2, rstd2 = _layernorm_fwd(ALPHA * h1 + ffn_scr[rows, :])
            diff = (xhat2 * g2_ref[...] + b2_ref[...]) - tgt_ref[rows, :]
            row = jnp.mean(diff * diff, axis=-1, keepdims=True)
            loss_acc = loss_acc + 0.5 * jnp.sum(row, axis=0, keepdims=True)
            dy = diff * (1.0 / D_MODEL)
            dg2_acc = dg2_acc + jnp.sum(dy * xhat2, axis=0, keepdims=True)
            db2_acc = db2_acc + jnp.sum(dy, axis=0, keepdims=True)
            dz2 = _layernorm_bwd(dy, xhat2, rstd2, g2_ref[...])
            dz2_ref[rows, :] = dz2
            dz2b_ref[rows, :] = dz2.astype(BF16)
        loss_ref[...] += loss_acc
        dg2_ref[...] += dg2_acc
        db2_ref[...] += db2_acc

    tile = lambda w: pl.BlockSpec((ts, w), lambda i: (i, 0))
    acc = lambda w: pl.BlockSpec((1, w), lambda i: (0, 0))
    return pl.pallas_call(
        body, name="ffn_fwd_loss", grid=(s // ts,),
        in_specs=[tile(D_MODEL), tile(D_MODEL), tile(D_MODEL)] + [_whole()] * 8,
        out_specs=[tile(2 * D_FF), tile(D_FF), tile(D_FF), tile(D_MODEL), tile(D_MODEL),
                   acc(1), acc(D_MODEL), acc(D_MODEL)],
        out_shape=[jax.ShapeDtypeStruct((s, 2 * D_FF), BF16), jax.ShapeDtypeStruct((s, D_FF), BF16),
                   jax.ShapeDtypeStruct((s, D_FF), BF16), jax.ShapeDtypeStruct((s, D_MODEL), F32),
                   jax.ShapeDtypeStruct((s, D_MODEL), BF16),
                   jax.ShapeDtypeStruct((1, 1), F32), jax.ShapeDtypeStruct((1, D_MODEL), F32),
                   jax.ShapeDtypeStruct((1, D_MODEL), F32)],
        scratch_shapes=[pltpu.VMEM((ts, D_FF), F32), pltpu.VMEM((ts + CONV_HALO, D_FF), F32),
                        pltpu.VMEM((ts, D_MODEL), F32)],
        compiler_params=_params(("arbitrary",)),
    )(xhat1, h1b, target, wup4, wdown, cw, cb, g1, b1, g2, b2)


def _ffn_bwd(dz2, dz2b, ub, sd, xhat1, rstd1, wup4, wdown, cw, g1, ts):
    s = dz2.shape[0]
    nt = s // ts

    def body(dz2_ref, dz2b_ref, ub_ref, sd_ref, xhat_ref, rstd_ref, wup_ref, wdn_ref, cw_ref, g1_ref,
             dub_ref, dz1_ref, dz1b_ref, dg1_ref, db1_ref, dcw_ref, dcb_ref, dext_scr, da_scr):
        i = pl.program_id(0)

        @pl.when(i == 0)
        def _():
            dext_scr[ts:ts + CONV_HALO, :] = jnp.zeros((CONV_HALO, D_FF), F32)
            dg1_ref[...] = jnp.zeros_like(dg1_ref)
            db1_ref[...] = jnp.zeros_like(db1_ref)
            dcw_ref[...] = jnp.zeros_like(dcw_ref)
            dcb_ref[...] = jnp.zeros_like(dcb_ref)

        da_scr[...] = _dot_nt(dz2b_ref[...], wdn_ref[...])
        n_ext = ts + CONV_HALO
        for c0 in range(0, D_FF, FFN_STRIP):
            cols = slice(c0, c0 + FFN_STRIP)
            gcols = slice(D_FF + c0, D_FF + c0 + FFN_STRIP)
            val = ub_ref[:, cols].astype(F32)
            gate = ub_ref[:, gcols].astype(F32)
            da = da_scr[:, cols]
            hc = sd_ref[:, cols].astype(F32)
            sg = _sigmoid(hc)
            dhc = da * val * (sg * (1.0 + hc * (1.0 - sg)))
            dext_scr[0:ts, cols] = dhc
            dext = dext_scr[:, cols]
            dhc1 = pltpu.roll(dext, n_ext - 1, 0)[0:ts]
            dhc2 = pltpu.roll(dext, n_ext - 2, 0)[0:ts]
            dcb_ref[:, cols] += jnp.sum(dhc, axis=0, keepdims=True)
            dcw_ref[0:1, cols] += jnp.sum(dhc2 * gate, axis=0, keepdims=True)
            dcw_ref[1:2, cols] += jnp.sum(dhc1 * gate, axis=0, keepdims=True)
            dcw_ref[2:3, cols] += jnp.sum(dhc * gate, axis=0, keepdims=True)
            dgate = dhc * cw_ref[2:3, cols] + dhc1 * cw_ref[1:2, cols] + dhc2 * cw_ref[0:1, cols]
            dub_ref[:, cols] = (da * (hc * sg)).astype(BF16)
            dub_ref[:, gcols] = dgate.astype(BF16)
        dext_scr[ts:n_ext, :] = dext_scr[0:CONV_HALO, :]
        dh1 = ALPHA * dz2_ref[...]
        for j in range(N_SHARD):
            dh1 = dh1 + _dot_nt(dub_ref[:, j * UP_SH:(j + 1) * UP_SH], wup_ref[j])
        xhat = xhat_ref[...]
        dg1_ref[...] += jnp.sum(dh1 * xhat, axis=0, keepdims=True)
        db1_ref[...] += jnp.sum(dh1, axis=0, keepdims=True)
        dz1 = _layernorm_bwd(dh1, xhat, rstd_ref[...], g1_ref[...])
        dz1_ref[...] = dz1
        dz1b_ref[...] = dz1.astype(BF16)

    tile = lambda w: pl.BlockSpec((ts, w), lambda i: (nt - 1 - i, 0))
    acc = lambda rws, w: pl.BlockSpec((rws, w), lambda i: (0, 0))
    return pl.pallas_call(
        body, name="ffn_bwd", grid=(nt,),
        in_specs=[tile(D_MODEL), tile(D_MODEL), tile(2 * D_FF), tile(D_FF), tile(D_MODEL), tile(1)]
        + [_whole()] * 4,
        out_specs=[tile(2 * D_FF), tile(D_MODEL), tile(D_MODEL), acc(1, D_MODEL), acc(1, D_MODEL),
                   acc(3, D_FF), acc(1, D_FF)],
        out_shape=[jax.ShapeDtypeStruct((s, 2 * D_FF), BF16),
                   jax.ShapeDtypeStruct((s, D_MODEL), F32), jax.ShapeDtypeStruct((s, D_MODEL), BF16),
                   jax.ShapeDtypeStruct((1, D_MODEL), F32),
                   jax.ShapeDtypeStruct((1, D_MODEL), F32), jax.ShapeDtypeStruct((3, D_FF), F32),
                   jax.ShapeDtypeStruct((1, D_FF), F32)],
        scratch_shapes=[pltpu.VMEM((ts + CONV_HALO, D_FF), F32), pltpu.VMEM((ts, D_FF), F32)],
        compiler_params=_params(("arbitrary",)),
    )(dz2, dz2b, ub, sd, xhat1, rstd1, wup4, wdown, cw, g1)


def _mix_bwd(dz1, pooled, ret, g, wout, wpool, pscale, ts, riders=(), after=()):
    s = dz1.shape[0]
    nt = s // ts

    def body(dz1_ref, pooled_ref, ret_ref, g_ref, wout_ref, wp_ref, ps_ref,
             dret_ref, dgp_ref, dwp_ref, dps_ref, eext_scr):
        i = pl.program_id(0)
        r = nt - 1 - i

        @pl.when(i == 0)
        def _():
            eext_scr[ts:ts + POOL_HALO, :] = jnp.zeros((POOL_HALO, POOL_W), F32)
            dwp_ref[...] = jnp.zeros_like(dwp_ref)
            dps_ref[...] = jnp.zeros_like(dps_ref)

        dzb = dz1_ref[...].astype(BF16)
        dcat_r = _dot_nt(dzb, wout_ref[0:RET_W, :])
        dcat_p = _dot_nt(dzb, wout_ref[RET_W:2 * RET_W, :])
        pos = (r * ts + lax.broadcasted_iota(jnp.int32, (ts, 1), 0) + 1).astype(F32)
        dpooled = []
        for gi, w in enumerate(POOL_WINDOWS):
            sl = slice(gi * HEAD_DIM, (gi + 1) * HEAD_DIM)
            pb = pooled_ref[:, sl]
            dy = dcat_p[:, sl]
            dps_ref[:, sl] += jnp.sum(dy * _dot(pb, wp_ref[gi]), axis=0, keepdims=True)
            dlin = (dy * ps_ref[:, sl]).astype(BF16)
            dwp_ref[gi] += _dot_tn(pb, dlin)
            dpg = _dot_nt(dlin, wp_ref[gi])
            dpooled.append(dpg)
            eext_scr[0:ts, sl] = dpg / jnp.minimum(pos, float(w))
        for gi, w in enumerate(POOL_WINDOWS):
            sl = slice(gi * HEAD_DIM, (gi + 1) * HEAD_DIM)
            acc = eext_scr[:, sl]
            shift = 1
            while shift < w:
                acc = acc + pltpu.roll(acc, ts + POOL_HALO - shift, 0)
                shift *= 2
            dgp_ref[:, RET_W + gi * HEAD_DIM:RET_W + (gi + 1) * HEAD_DIM] = (acc[0:ts] - dpooled[gi]).astype(BF16)
        eext_scr[ts:ts + POOL_HALO, :] = eext_scr[0:POOL_HALO, :]
        for h in range(HEADS):
            sl = slice(h * HEAD_DIM, (h + 1) * HEAD_DIM)
            rt = ret_ref[:, sl]
            rr = lax.rsqrt(jnp.mean(rt * rt, axis=-1, keepdims=True) + RMS_EPS)
            rn = rt * rr
            gh = g_ref[:, sl]
            sg = _sigmoid(gh)
            dy = dcat_r[:, sl]
            dgp_ref[:, sl] = (dy * rn * (sg * (1.0 + gh * (1.0 - sg)))).astype(BF16)
            drn = dy * (gh * sg)
            dret_ref[:, sl] = (rr * (drn - rn * jnp.mean(drn * rn, axis=-1, keepdims=True))).astype(BF16)

    tile = lambda w: pl.BlockSpec((ts, w), lambda i: (nt - 1 - i, 0))
    return _call(
        body, name="mix_bwd", grid=(nt,),
        in_specs=[tile(D_MODEL), tile(POOL_W), tile(RET_W), tile(RET_W), _whole(), _whole(), _whole()],
        out_specs=[tile(RET_W), tile(2 * RET_W),
                   pl.BlockSpec((len(POOL_WINDOWS), HEAD_DIM, HEAD_DIM), lambda i: (0, 0, 0)),
                   pl.BlockSpec((1, POOL_W), lambda i: (0, 0))],
        out_shape=[jax.ShapeDtypeStruct((s, RET_W), BF16), jax.ShapeDtypeStruct((s, 2 * RET_W), BF16),
                   jax.ShapeDtypeStruct((len(POOL_WINDOWS), HEAD_DIM, HEAD_DIM), F32),
                   jax.ShapeDtypeStruct((1, POOL_W), F32)],
        scratch_shapes=[pltpu.VMEM((ts + POOL_HALO, POOL_W), F32)],
        sem=("arbitrary",), operands=(dz1, pooled, ret, g, wout, wpool, pscale), riders=riders,
        after=after,
    )


def _retention_bwd(q, k, v, dret, dgp, states, mask, qd, kd, cosf, sinf, riders=(), after=()):
    s = q.shape[0]
    ns = s // SUPER
    cdec = [gm ** float(SUPER) for gm in _gammas()]

    def body(q_ref, k_ref, v_ref, do_ref, dgp_ref, st_ref, mask_ref, qd_ref, kd_ref, cos_ref, sin_ref,
             dproj_ref, dstate_scr):
        i = pl.program_id(0)

        @pl.when(i == 0)
        def _():
            dstate_scr[...] = jnp.zeros_like(dstate_scr)

        cosf_t = cos_ref[...]
        sinf_t = sin_ref[...]
        for h in range(HEADS):
            sl = slice(h * HEAD_DIM, (h + 1) * HEAD_DIM)
            qh, kh, vh, doh = q_ref[:, sl], k_ref[:, sl], v_ref[:, sl], do_ref[:, sl]
            dscb = (_dot_nt(doh, vh) * mask_ref[0, h]).astype(BF16)
            dsctb = (_dot_nt(vh, doh) * mask_ref[1, h]).astype(BF16)
            sctb = (_dot_nt(kh, qh) * mask_ref[1, h]).astype(BF16)
            stb = st_ref[0, h]
            dst = dstate_scr[h]
            dstb = dst.astype(BF16)
            qdb = (qh.astype(F32) * qd_ref[:, sl]).astype(BF16)
            kdb = (kh.astype(F32) * kd_ref[:, sl]).astype(BF16)
            dq = _dot(dscb, kh) + _dot_nt(doh, stb) * qd_ref[:, sl]
            dk = _dot(dsctb, qh) + _dot_nt(vh, dstb) * kd_ref[:, sl]
            dv = _dot(sctb, doh) + _dot(kdb, dstb)
            dstate_scr[h] = dst * cdec[h] + _dot_tn(qdb, doh)
            lo = h * HEAD_DIM
            dproj_ref[:, lo:lo + HEAD_DIM] = _rope_t(dq, cosf_t, sinf_t).astype(BF16)
            dproj_ref[:, RET_W + lo:RET_W + lo + HEAD_DIM] = _rope_t(dk * K_SCALE, cosf_t, sinf_t).astype(BF16)
            dproj_ref[:, 2 * RET_W + lo:2 * RET_W + lo + HEAD_DIM] = dv.astype(BF16)
        dproj_ref[:, 3 * RET_W:IN_W] = dgp_ref[...]

    tile = lambda w: pl.BlockSpec((SUPER, w), lambda i: (ns - 1 - i, 0))
    return _call(
        body, name="retention_bwd", grid=(ns,),
        in_specs=[tile(RET_W), tile(RET_W), tile(RET_W), tile(RET_W), tile(2 * RET_W),
                  pl.BlockSpec((1, HEADS, HEAD_DIM, HEAD_DIM), lambda i: (ns - 1 - i, 0, 0, 0)),
                  _whole(), _whole(), _whole(), tile(HEAD_DIM), tile(HEAD_DIM)],
        out_specs=[tile(IN_W)],
        out_shape=[jax.ShapeDtypeStruct((s, IN_W), BF16)],
        scratch_shapes=[pltpu.VMEM((HEADS, HEAD_DIM, HEAD_DIM), F32)],
        sem=("arbitrary",), operands=(q, k, v, dret, dgp, states, mask, qd, kd, cosf, sinf), riders=riders,
        after=after,
    )


def _dx(dz1, dproj, win4, ts, riders=(), after=()):
    s = dz1.shape[0]

    def body(dz1_ref, dp_ref, w_ref, dx_ref):
        acc = ALPHA * dz1_ref[...]
        for j in range(N_SHARD):
            acc = acc + _dot_nt(dp_ref[:, j * IN_SH:(j + 1) * IN_SH], w_ref[j])
        dx_ref[...] = acc

    tile = lambda w: pl.BlockSpec((ts, w), lambda i: (i, 0))
    return _call(
        body, name="dx", grid=(s // ts,),
        in_specs=[tile(D_MODEL), tile(IN_W), _whole()],
        out_specs=[tile(D_MODEL)],
        out_shape=[jax.ShapeDtypeStruct((s, D_MODEL), F32)],
        sem=("arbitrary",), operands=(dz1, dproj, win4), riders=riders, after=after,
    )


def _wgrad(a, b, tm, tn, name, stacked, m_outer, riders=(), after=()):
    s, m = a.shape
    n = b.shape[1]

    def body(a_ref, b_ref, o32_ref, o16_ref):
        res = _dot_tn(a_ref[...], b_ref[...])
        o32_ref[...] = res.reshape(o32_ref.shape)
        o16_ref[...] = res.astype(BF16).reshape(o16_ref.shape)

    if m_outer:
        grid, blocks = (m // tm, n // tn), (lambda g0, g1: (g0, g1))
    else:
        grid, blocks = (n // tn, m // tm), (lambda g0, g1: (g1, g0))
    if stacked:
        shape = (n // tn, m, tn)
        ospec = pl.BlockSpec((1, tm, tn), lambda g0, g1: (blocks(g0, g1)[1], blocks(g0, g1)[0], 0))
    else:
        shape = (m, n)
        ospec = pl.BlockSpec((tm, tn), lambda g0, g1: blocks(g0, g1))
    return _call(
        body, name=name, grid=grid,
        in_specs=[pl.BlockSpec((s, tm), lambda g0, g1: (0, blocks(g0, g1)[0])),
                  pl.BlockSpec((s, tn), lambda g0, g1: (0, blocks(g0, g1)[1]))],
        out_specs=[ospec, ospec],
        out_shape=[jax.ShapeDtypeStruct(shape, F32), jax.ShapeDtypeStruct(shape, BF16)],
        sem=("arbitrary", "arbitrary"), operands=(a, b), riders=riders, after=after,
    )


class _NoComm:
    def __init__(self, win4, wout, wup4, wdown):
        self.weights = dict(w_in=win4, w_out=wout, w_up=wup4, w_down=wdown)
        self.grads = {}

    def weight(self, name):
        return self.weights[name]

    def riders(self, call):
        return ()

    def after(self, call):
        return ()

    def landed(self, call, results, outs):
        pass

    def small_gradients(self, loss, small):
        pass

    def gradient(self, name, g32, g16):
        self.grads[name] = (g32, g16)


def _local_step(x, target, cw, cb, wpool_b, pscale, g1, b1, g2, b2, comm):
    s = x.shape[0]
    ts_a = min(512, s)
    ts_f = min(256, s)
    mask, qd, kd = _decay_tables()
    cosf, sinf = _rope_tables(s)

    def run(call, fn, *args):
        outs, res = fn(*args, riders=comm.riders(call), after=comm.after(call))
        comm.landed(call, res, outs)
        return outs

    xb, q, k, v, g, pooled, cat = run("proj_pool", _proj_pool, x, comm.weight("w_in"), cosf, sinf, wpool_b,
                                      pscale, ts_a)
    ret, cat, states = run("retention_fwd", _retention_fwd, q, k, v, g, cat, mask, qd, kd)
    wout = comm.weight("w_out")
    xhat1, rstd1, h1b = run("outproj_ln1", _outproj_ln1, x, cat, wout, g1, b1, ts_a)
    wup4, wdown = comm.weight("w_up"), comm.weight("w_down")
    ub, act, sd, dz2, dz2b, loss, dg2, db2 = _ffn_fwd_loss(xhat1, h1b, target, wup4, wdown, cw, cb, g1, b1, g2, b2,
                                                           ts_f)

    dub, dz1, dz1b, dg1, db1, dcw, dcb = _ffn_bwd(dz2, dz2b, ub, sd, xhat1, rstd1, wup4, wdown, cw, g1, ts_f)
    half = D_MODEL // 2
    comm.gradient("w_up", *run("wgrad_up", _wgrad, h1b, dub, half, UP_SH, "wgrad_up", True, False))
    comm.gradient("w_out", *run("wgrad_out", _wgrad, cat, dz1b, D_MODEL, half, "wgrad_out", False, True))
    comm.gradient("w_down", *run("wgrad_down", _wgrad, act, dz2b, D_FF // 2, half, "wgrad_down", False, True))
    dret, dgp, dwp, dps = run("mix_bwd", _mix_bwd, dz1b, pooled, ret, g, wout, wpool_b, pscale, ts_a)
    small = dict(w_pool=dwp, pool_scale=dps, ln1_g=dg1, ln1_b=db1, conv_w=dcw, conv_b=dcb,
                 ln2_g=dg2, ln2_b=db2)
    comm.small_gradients(loss, small)
    mask_both = jnp.stack([mask, jnp.swapaxes(mask, 1, 2)])
    dproj, = run("retention_bwd", _retention_bwd, q, k, v, dret, dgp, states, mask_both, qd, kd, cosf, sinf)
    comm.gradient("w_in", *run("wgrad_in", _wgrad, xb, dproj, D_MODEL, IN_SH, "wgrad_in", True, True))
    (grad_x,), _ = _dx(dz1, dproj, comm.weight("w_in"), ts_a, after=comm.after("dx"))
    return loss, grad_x, small


CAST_ROWS = 64
SHARD_SHAPES = ((D_MODEL, IN_SH), (OUT_SH, D_MODEL), (D_MODEL, UP_SH), (DOWN_SH, D_MODEL))
N_BIG = len(SHARD_SHAPES)
CW_SHARD = (3, 1, DOWN_SH)


def _mesh_pos():
    return lax.axis_index("x"), lax.axis_index("y"), lax.axis_index("c")


def _other_chips(x, y):
    return [(1 - x, y), (x, 1 - y), (1 - x, 1 - y)]


def _half_rows(w, which):
    hr = SHARD_SHAPES[w][0] // 2
    return pl.ds(pl.multiple_of(which * hr, 16), hr)


def _gather_weights(shards, cw_shard, wpool, full):
    def body(*refs):
        in_refs = refs[:N_BIG]
        cw_ref, wpool_ref = refs[N_BIG:N_BIG + 2]
        out_refs = refs[N_BIG + 2:2 * N_BIG + 2]
        cwo_ref, wpool_b_ref = refs[2 * N_BIG + 2:2 * N_BIG + 4]
        stage = refs[2 * N_BIG + 4:3 * N_BIG + 4]
        raw = refs[3 * N_BIG + 4:4 * N_BIG + 4 - len(full)]
        send_sems, recv_sems, fsend_sems, frecv_sems, cw_send, cw_recv, local_sems, load_sems = \
            refs[4 * N_BIG + 4 - len(full):]
        x, y, c = _mesh_pos()
        j0 = 2 * x + y
        chips = _other_chips(x, y)

        fetched = [w for w in range(N_BIG) if w not in full]
        f32 = {w: in_refs[w] for w in full}
        loads = []
        for n, w in enumerate(fetched):
            f32[w] = raw[n]
            loads.append(pltpu.make_async_copy(in_refs[w], raw[n], load_sems.at[n]))
            loads[-1].start()

        def cast_to_stage(w):
            def cast(i, carry):
                rows = pl.ds(pl.multiple_of(i * CAST_ROWS, CAST_ROWS), CAST_ROWS)
                stage[w][rows, :] = f32[w][rows, :].astype(BF16)
                return carry
            lax.fori_loop(0, SHARD_SHAPES[w][0] // CAST_ROWS, cast, 0)

        for w in full:
            cast_to_stage(w)

        jx, jy, jd = 2 * (1 - x) + y, 2 * x + (1 - y), 2 * (1 - x) + (1 - y)
        neighbours = [((1 - x, y, c), jx), ((x, 1 - y, c), jy)]
        passed = jnp.where(c == 0, jx, jy)
        pass_to = (jnp.where(c == 0, x, 1 - x), jnp.where(c == 0, 1 - y, y), c)

        def nbr(w, k, block):
            return pltpu.make_async_remote_copy(
                src_ref=stage[w].at[_half_rows(w, c), :], dst_ref=out_refs[w].at[block, _half_rows(w, c), :],
                send_sem=send_sems.at[w, k], recv_sem=recv_sems.at[w, k],
                device_id=neighbours[k][0], device_id_type=MESH)

        def relay(w, block):
            return pltpu.make_async_remote_copy(
                src_ref=out_refs[w].at[passed, _half_rows(w, c), :],
                dst_ref=out_refs[w].at[block, _half_rows(w, c), :],
                send_sem=send_sems.at[w, 2], recv_sem=recv_sems.at[w, 2],
                device_id=pass_to, device_id_type=MESH)

        def d2d(w, k, block, half):
            return pltpu.make_async_remote_copy(
                src_ref=out_refs[w].at[block, _half_rows(w, half), :],
                dst_ref=out_refs[w].at[block, _half_rows(w, half), :],
                send_sem=fsend_sems.at[w, k], recv_sem=frecv_sems.at[w, k],
                device_id=(x, y, 1 - c), device_id_type=MESH)

        def conv(k, block):
            chip = chips[k]
            return pltpu.make_async_remote_copy(
                src_ref=cw_ref, dst_ref=cwo_ref.at[block], send_sem=cw_send.at[k], recv_sem=cw_recv.at[k],
                device_id=(chip[0], chip[1], c), device_id_type=MESH)

        sent = [nbr(w, k, j0) for w in full for k in range(2)] + [conv(k, j0) for k in range(3)]
        for cp in sent:
            cp.start()
        for n, w in enumerate(fetched):
            loads[n].wait()
            cast_to_stage(w)
        local = [pltpu.make_async_copy(stage[w], out_refs[w].at[j0], local_sems.at[w]) for w in range(N_BIG)]
        local.append(pltpu.make_async_copy(cw_ref, cwo_ref.at[j0], local_sems.at[N_BIG]))
        for cp in local:
            cp.start()
        wpool_b_ref[...] = wpool_ref[...].astype(BF16)
        for w in full:
            for k, (_, block) in enumerate(neighbours):
                nbr(w, k, block).wait_recv()
            later = [relay(w, passed)] + [d2d(w, k, block, c) for k, (_, block) in enumerate(neighbours)]
            for cp in later:
                cp.start()
            sent += later
        for w in full:
            relay(w, jd).wait_recv()
            fw = d2d(w, 2, jd, c)
            fw.start()
            sent.append(fw)
        for w in full:
            for k, block in enumerate([jx, jy, jd]):
                d2d(w, k, block, 1 - c).wait_recv()
        for k, chip in enumerate(chips):
            conv(k, 2 * chip[0] + chip[1]).wait_recv()
        for cp in sent:
            cp.wait_send()
        for cp in local:
            cp.wait()

    out_shape = [jax.ShapeDtypeStruct((N_SHARD,) + shp, BF16) for shp in SHARD_SHAPES]
    out_shape.append(jax.ShapeDtypeStruct((N_SHARD,) + CW_SHARD, F32))
    out_shape.append(jax.ShapeDtypeStruct(wpool.shape, BF16))
    return pl.pallas_call(
        body, name="gather_weights",
        in_specs=[_whole() if w in full else HBM_SPEC for w in range(N_BIG)] + [_whole()] * 2,
        out_specs=[HBM_SPEC] * (N_BIG + 1) + [_whole()],
        out_shape=out_shape,
        scratch_shapes=[pltpu.VMEM(shp, BF16) for shp in SHARD_SHAPES]
        + [pltpu.VMEM(shp, F32) for w, shp in enumerate(SHARD_SHAPES) if w not in full] + [
            pltpu.SemaphoreType.DMA((N_BIG, 3)), pltpu.SemaphoreType.DMA((N_BIG, 3)),
            pltpu.SemaphoreType.DMA((N_BIG, 3)), pltpu.SemaphoreType.DMA((N_BIG, 3)),
            pltpu.SemaphoreType.DMA((3,)), pltpu.SemaphoreType.DMA((3,)),
            pltpu.SemaphoreType.DMA((N_BIG + 1,)), pltpu.SemaphoreType.DMA((N_BIG - len(full),))],
        compiler_params=pltpu.CompilerParams(vmem_limit_bytes=VMEM_LIMIT),
    )(*shards, cw_shard, wpool)


def _gather_rider(arrays, ops):
    ws = sorted(arrays)

    def make(inplace, srcs, lands, send_sems, recv_sems):
        del srcs, lands
        x, y, c = _mesh_pos()
        j0, jx, jy, jd = 2 * x + y, 2 * (1 - x) + y, 2 * x + (1 - y), 2 * (1 - x) + (1 - y)
        x_nbr, y_nbr, sibling = (1 - x, y, c), (x, 1 - y, c), (x, y, 1 - c)
        starts, waits = [], []
        for n, (kind, w, (r0, nr)) in enumerate(ops):
            ref = inplace[ws.index(w)]
            hr = SHARD_SHAPES[w][0] // 2
            rows = lambda core: pl.ds(pl.multiple_of(core * hr + r0, 16), nr)
            mine, theirs = rows(c), rows(1 - c)
            if kind == "ici":
                moves = [(ref.at[j0, mine, :], x_nbr, ref.at[jx, mine, :]),
                         (ref.at[j0, mine, :], y_nbr, ref.at[jy, mine, :]),
                         (ref.at[j0, mine, :], (1 - x, 1 - y, c), ref.at[jd, mine, :])]
            elif kind == "nbr":
                moves = [(ref.at[j0, mine, :], x_nbr, ref.at[jx, mine, :]),
                         (ref.at[j0, mine, :], y_nbr, ref.at[jy, mine, :])]
            elif kind == "relay":
                passed = jnp.where(c == 0, jx, jy)
                to = (jnp.where(c == 0, x, 1 - x), jnp.where(c == 0, 1 - y, y), c)
                moves = [(ref.at[passed, mine, :], to, ref.at[jd, mine, :])]
            else:
                blocks = dict(d2d=[jx, jy, jd], d2d_nbr=[jx, jy], d2d_diag=[jd])[kind]
                moves = [(ref.at[b, mine, :], sibling, ref.at[b, theirs, :]) for b in blocks]
            for k, (src, to, landing) in enumerate(moves):
                sems = dict(send_sem=send_sems.at[3 * n + k], recv_sem=recv_sems.at[3 * n + k],
                            device_id=to, device_id_type=MESH)
                send = pltpu.make_async_remote_copy(src_ref=src, dst_ref=src, **sems)
                arrival = pltpu.make_async_remote_copy(src_ref=src, dst_ref=landing, **sems)
                starts.append(send)
                waits += [arrival.wait_recv, send.wait_send]
        return starts, waits

    return _Rider([arrays[w] for w in ws], [], [], 3 * len(ops), make)


def _whole_half(w):
    return (0, SHARD_SHAPES[w][0] // 2)


def _pair_rider(ws, g16s):
    def make(inplace, srcs, lands, send_sems, recv_sems):
        del inplace
        x, y, c = _mesh_pos()
        copies = [pltpu.make_async_remote_copy(
            src_ref=srcs[i].at[:, _half_rows(w, 1 - c), :], dst_ref=lands[i],
            send_sem=send_sems.at[i], recv_sem=recv_sems.at[i], device_id=(x, y, 1 - c), device_id_type=MESH)
            for i, w in enumerate(ws)]
        return copies, [cp.wait for cp in copies]

    lands = [jax.ShapeDtypeStruct((N_SHARD, SHARD_SHAPES[w][0] // 2, SHARD_SHAPES[w][1]), BF16) for w in ws]
    return _Rider([], g16s, lands, len(ws), make)


def _chip_rider(ws, p16s):
    def make(inplace, srcs, lands, send_sems, recv_sems):
        del inplace
        x, y, c = _mesh_pos()
        copies = []
        for i in range(len(ws)):
            for k, chip in enumerate(_other_chips(x, y)):
                copies.append(pltpu.make_async_remote_copy(
                    src_ref=srcs[i].at[2 * chip[0] + chip[1]], dst_ref=lands[i].at[k],
                    send_sem=send_sems.at[3 * i + k], recv_sem=recv_sems.at[3 * i + k],
                    device_id=(chip[0], chip[1], c), device_id_type=MESH))
        return copies, [cp.wait for cp in copies]

    lands = [jax.ShapeDtypeStruct((3, SHARD_SHAPES[w][0] // 2, SHARD_SHAPES[w][1]), BF16) for w in ws]
    return _Rider([], p16s, lands, 3 * len(ws), make)


def _final_rider(halves):
    def make(inplace, srcs, lands, send_sems, recv_sems):
        del inplace
        x, y, c = _mesh_pos()
        copies = [pltpu.make_async_remote_copy(
            src_ref=srcs[i], dst_ref=lands[i], send_sem=send_sems.at[i], recv_sem=recv_sems.at[i],
            device_id=(x, y, 1 - c), device_id_type=MESH) for i in range(len(halves))]
        return copies, [cp.wait for cp in copies]

    return _Rider([], halves, [jax.ShapeDtypeStruct(h.shape, h.dtype) for h in halves], len(halves), make)


def _comm_only(name, riders):
    _, res = _call(lambda: None, name=name, grid=(), in_specs=[], out_specs=[], out_shape=[], operands=(),
                   riders=riders)
    return res


class _SemList:
    def __init__(self, refs):
        self.at = list(refs)


def _merged_rider(riders):
    srcs = [a for r in riders for a in r.srcs]
    lands = [a for r in riders for a in r.lands]

    def make(inplace, src_refs, land_refs, send_sems, recv_sems):
        starts, waits = [], []
        s0 = l0 = c0 = 0
        for r in riders:
            part = r.make(inplace, src_refs[s0:s0 + len(r.srcs)], land_refs[l0:l0 + len(r.lands)],
                          _SemList(send_sems.at[c0:c0 + r.n_copies]), _SemList(recv_sems.at[c0:c0 + r.n_copies]))
            starts += part[0]
            waits += part[1]
            s0, l0, c0 = s0 + len(r.srcs), l0 + len(r.lands), c0 + r.n_copies
        return starts, waits

    return _Rider([], srcs, lands, sum(r.n_copies for r in riders), make)


def _split_start(name, rider, sibling_barrier=None):
    assert not rider.inplace
    ns, nl, n = len(rider.srcs), len(rider.lands), rider.n_copies

    def body(*refs):
        if sibling_barrier is not None:
            x, y, c = _mesh_pos()
            barrier = pltpu.get_barrier_semaphore()
            pl.semaphore_signal(barrier, inc=1, device_id=(x, y, 1 - c), device_id_type=MESH)
            pl.semaphore_wait(barrier, 1)
        srcs, lands = refs[:ns], refs[ns:ns + nl]
        sems = refs[ns + nl:ns + nl + 2 * n]
        token = refs[-1]
        starts, _ = rider.make([], srcs, lands, _SemList(sems[:n]), _SemList(sems[n:]))
        for cp in starts:
            cp.start()
        token[...] = jnp.zeros_like(token)

    buffers = [pltpu.with_memory_space_constraint(a, pltpu.HBM) for a in rider.srcs]
    buffers += [pltpu.with_memory_space_constraint(lax.empty(s.shape, s.dtype), pltpu.HBM) for s in rider.lands]
    hbm = pl.BlockSpec(memory_space=pltpu.HBM)
    sem = pl.BlockSpec(memory_space=pltpu.SEMAPHORE)
    outs = pl.pallas_call(
        body, name=name,
        out_shape=tuple([pltpu.SemaphoreType.DMA(())] * (2 * n) + [pltpu.HBM(b.shape, b.dtype) for b in buffers]
                        + [jax.ShapeDtypeStruct((8, 128), F32)]),
        in_specs=[hbm] * (ns + nl),
        out_specs=tuple([sem] * (2 * n) + [hbm] * (ns + nl) + [_whole()]),
        input_output_aliases={i: 2 * n + i for i in range(ns + nl)},
        compiler_params=pltpu.CompilerParams(has_side_effects=pltpu.SideEffectType.DATAFLOW_SIDE_EFFECTING,
                                             collective_id=sibling_barrier),
    )(*buffers)
    return (rider, outs[:2 * n], outs[2 * n:2 * n + ns + nl]), outs[-1]


def _split_parts(state, riders):
    merged, sems, buffers = state
    n, ns = merged.n_copies, len(merged.srcs)
    parts, s0, l0, c0 = [], 0, 0, 0
    for r in riders:
        parts.append((r, list(sems[c0:c0 + r.n_copies]) + list(sems[n + c0:n + c0 + r.n_copies]),
                      list(buffers[s0:s0 + len(r.srcs)]) + list(buffers[ns + l0:ns + l0 + len(r.lands)])))
        s0, l0, c0 = s0 + len(r.srcs), l0 + len(r.lands), c0 + r.n_copies
    return parts


def _split_wait(name, state, after):
    rider, sems, buffers = state
    ns, nl, n = len(rider.srcs), len(rider.lands), rider.n_copies

    def body(*refs):
        srcs, lands = refs[:ns], refs[ns:ns + nl]
        sem_refs = refs[ns + nl:ns + nl + 2 * n]
        _, waits = rider.make([], srcs, lands, _SemList(sem_refs[:n]), _SemList(sem_refs[n:]))
        for wait in waits:
            wait()

    hbm = pl.BlockSpec(memory_space=pltpu.HBM)
    sem = pl.BlockSpec(memory_space=pltpu.SEMAPHORE)
    outs = pl.pallas_call(
        body, name=name,
        out_shape=tuple(pltpu.HBM(b.shape, b.dtype) for b in buffers),
        in_specs=[hbm] * (ns + nl) + [sem] * (2 * n) + [HBM_SPEC],
        out_specs=tuple([hbm] * (ns + nl)),
        input_output_aliases={i: i for i in range(ns + nl)},
        compiler_params=pltpu.CompilerParams(has_side_effects=pltpu.SideEffectType.DATAFLOW_SIDE_EFFECTING),
    )(*buffers, *sems, after)
    return list(outs[:ns]), list(outs[ns:])


def _pair_sum(pos, ws, g32s, recvs):
    n = len(ws)

    def body(pos_ref, *refs):
        del pos_ref
        g_refs, r_refs = refs[:n], refs[n:2 * n]
        p32_refs, p16_refs = refs[2 * n:3 * n], refs[3 * n:]
        x, y, _ = _mesh_pos()
        for i in range(n):
            tot = g_refs[i][...] + r_refs[i][...].astype(F32)
            p16_refs[i][...] = tot.astype(BF16)

            @pl.when(pl.program_id(0) == 2 * x + y)
            def _(i=i, tot=tot):
                p32_refs[i][...] = tot

    halves = [(SHARD_SHAPES[w][0] // 2, SHARD_SHAPES[w][1]) for w in ws]
    own = [pl.BlockSpec((None, None) + h, lambda j, pos_ref: (j, pos_ref[0], 0, 0)) for h in halves]
    blk = [pl.BlockSpec((None,) + h, lambda j, pos_ref: (j, 0, 0)) for h in halves]
    mine = [pl.BlockSpec(h, lambda j, pos_ref: (0, 0)) for h in halves]
    g4 = [g.reshape((N_SHARD, 2) + h) for g, h in zip(g32s, halves)]
    outs = pl.pallas_call(
        body, name="pair_sum_" + "_".join(str(w) for w in ws),
        grid_spec=pltpu.PrefetchScalarGridSpec(
            num_scalar_prefetch=1, grid=(N_SHARD,), in_specs=own + blk, out_specs=mine + blk),
        out_shape=[jax.ShapeDtypeStruct(h, F32) for h in halves]
        + [jax.ShapeDtypeStruct((N_SHARD,) + h, BF16) for h in halves],
        compiler_params=_params(("arbitrary",)),
    )(pos, *g4, *recvs)
    return outs[:n], outs[n:]


def _chip_sum(p32s, recvs):
    parts = 2

    def body(*refs):
        p_refs, r_refs, f_refs = refs[:N_BIG], refs[N_BIG:2 * N_BIG], refs[2 * N_BIG:]
        for w in range(N_BIG):
            f_refs[w][...] = ((p_refs[w][...] + r_refs[w][0].astype(F32)) + r_refs[w][1].astype(F32)) \
                + r_refs[w][2].astype(F32)

    quarters = [(r // 2 // parts, cc) for r, cc in SHARD_SHAPES]
    own = [pl.BlockSpec(qt, lambda i: (i, 0)) for qt in quarters]
    rcv = [pl.BlockSpec((3,) + qt, lambda i: (0, i, 0)) for qt in quarters]
    out = [pl.BlockSpec(qt, lambda i: (i, 0)) for qt in quarters]
    return pl.pallas_call(
        body, name="chip_sum", grid=(parts,), in_specs=own + rcv, out_specs=out,
        out_shape=[jax.ShapeDtypeStruct((r // 2, cc), F32) for r, cc in SHARD_SHAPES],
        compiler_params=_params(("arbitrary",)),
    )(*p32s, *recvs)


def _adamw(w, g, m, v):
    m_new = ADAM_B1 * m + (1.0 - ADAM_B1) * g
    v_new = ADAM_B2 * v + (1.0 - ADAM_B2) * (g * g)
    m_hat = m_new / (1.0 - ADAM_B1 ** ADAM_STEP)
    v_hat = v_new / (1.0 - ADAM_B2 ** ADAM_STEP)
    delta = -ADAM_LR * (m_hat / (jnp.sqrt(v_hat) + ADAM_EPS) + ADAM_WD * w)
    return delta, m_new, v_new


def _adam_half(name, pos, grads, ws, ms, vs, into=None):
    nb = 4
    which = (lambda ref: ref[0]) if into is None else (lambda ref: 1 - ref[0])

    def body(which_ref, *refs):
        del which_ref
        groups = [refs[i * N_BIG:(i + 1) * N_BIG] for i in range(4)]
        g_refs, w_refs, m_refs, v_refs = groups
        go_refs, do_refs, mo_refs, vo_refs = [refs[len(refs) - (4 - i) * N_BIG:len(refs) - (3 - i) * N_BIG]
                                              for i in range(4)]
        for w in range(N_BIG):
            g = g_refs[w][...]
            delta, m_new, v_new = _adamw(w_refs[w][...], g, m_refs[w][...], v_refs[w][...])
            go_refs[w][...] = g
            do_refs[w][...] = delta
            mo_refs[w][...] = m_new
            vo_refs[w][...] = v_new

    blocks = [(r // 2 // nb, cc) for r, cc in SHARD_SHAPES]
    half = [pl.BlockSpec(b, lambda i, which_ref: (i, 0)) for b in blocks]
    full = [pl.BlockSpec((None,) + b, lambda i, which_ref: (0, which(which_ref) * nb + i, 0)) for b in blocks]
    shapes = [jax.ShapeDtypeStruct((1,) + shp, F32) for shp in SHARD_SHAPES]
    carried = [] if into is None else [a for kind in into for a in kind]
    first = 1 + 4 * N_BIG
    outs = pl.pallas_call(
        body, name=name,
        grid_spec=pltpu.PrefetchScalarGridSpec(
            num_scalar_prefetch=1, grid=(nb,), in_specs=half + full * 3 + [HBM_SPEC] * len(carried),
            out_specs=full * 4),
        out_shape=shapes * 4,
        input_output_aliases={first + i: i for i in range(len(carried))},
        compiler_params=_params(("arbitrary",)),
    )(pos, *grads, *ws, *ms, *vs, *carried)
    return [outs[i * N_BIG:(i + 1) * N_BIG] for i in range(4)]


SMALL_ROWS = 8
ROW_CONV_B, ROW_POOL_SCALE, ROW_LN1_G, ROW_LN1_B, ROW_LN2_G, ROW_LN2_B, ROW_LOSS = range(7)
SMALL_VECS = ((ROW_CONV_B, D_FF), (ROW_POOL_SCALE, POOL_W), (ROW_LN1_G, D_MODEL), (ROW_LN1_B, D_MODEL),
              (ROW_LN2_G, D_MODEL), (ROW_LN2_B, D_MODEL))


def _small_pack(loss, vec_grads):
    def body(*refs):
        loss_ref, gvec, out_ref = refs[0], refs[1:-1], refs[-1]
        out_ref[...] = jnp.zeros_like(out_ref)
        for (row, n), ref in zip(SMALL_VECS, gvec):
            out_ref[row:row + 1, 0:n] = ref[...]
        out_ref[ROW_LOSS:ROW_LOSS + 1, 0:HEAD_DIM] = jnp.broadcast_to(loss_ref[...], (1, HEAD_DIM))

    return pl.pallas_call(
        body, name="small_pack", in_specs=[_whole()] * (1 + len(vec_grads)), out_specs=_whole(),
        out_shape=jax.ShapeDtypeStruct((SMALL_ROWS, D_FF), F32),
    )(loss, *vec_grads)


def _small_pair_sum(own, sibling):
    n = len(own)

    def body(*refs):
        x, y, _ = _mesh_pos()
        for i in range(n):
            refs[2 * n + i][2 * x + y] = refs[i][...] + refs[n + i][...]

    return pl.pallas_call(
        body, name="small_pair_sum", in_specs=[_whole()] * (2 * n), out_specs=[_whole()] * n,
        out_shape=[jax.ShapeDtypeStruct((N_SHARD,) + a.shape, F32) for a in own],
        compiler_params=pltpu.CompilerParams(vmem_limit_bytes=VMEM_LIMIT),
    )(*own, *sibling)


def _small_chip_rider(gathered):
    n = len(gathered)

    def make(inplace, srcs, lands, send_sems, recv_sems):
        del inplace, lands
        x, y, c = _mesh_pos()
        j0 = 2 * x + y
        starts, waits = [], []
        for i in range(n):
            for k, chip in enumerate(_other_chips(x, y)):
                sems = dict(send_sem=send_sems.at[3 * i + k], recv_sem=recv_sems.at[3 * i + k],
                            device_id=(chip[0], chip[1], c), device_id_type=MESH)
                send = pltpu.make_async_remote_copy(src_ref=srcs[i].at[j0], dst_ref=srcs[i].at[j0], **sems)
                arrival = pltpu.make_async_remote_copy(
                    src_ref=srcs[i].at[j0], dst_ref=srcs[i].at[2 * chip[0] + chip[1]], **sems)
                starts.append(send)
                waits += [arrival.wait_recv, send.wait_send]
        return starts, waits

    return _Rider([], gathered, [], 3 * n, make)


def _small_adam(all_a, all_b, all_c, wp, cwp, vec_ws, m_wp, m_cwp, vec_ms, v_wp, v_cwp, vec_vs):
    nv = len(SMALL_VECS)
    np_ = 2 + nv

    def body(*refs):
        all_a_ref, all_b_ref, all_c_ref = refs[0:3]
        w_all, m_all, v_all = (refs[3 + i * np_:3 + (i + 1) * np_] for i in range(3))
        loss_out = refs[3 + 3 * np_]
        outs = refs[4 + 3 * np_:]
        x, y, _ = _mesh_pos()
        j0 = 2 * x + y
        tot_a = ((all_a_ref[0] + all_a_ref[1]) + all_a_ref[2]) + all_a_ref[3]
        tot_b = ((all_b_ref[0] + all_b_ref[1]) + all_b_ref[2]) + all_b_ref[3]
        tot_c = ((all_c_ref[0, j0] + all_c_ref[1, j0]) + all_c_ref[2, j0]) + all_c_ref[3, j0]
        loss_out[...] = tot_b[ROW_LOSS:ROW_LOSS + 1, 0:1]
        grads = [tot_a, tot_c] + [tot_b[row:row + 1, 0:n] for row, n in SMALL_VECS]
        for p in range(np_):
            for at, g in ([(j, tot_c[j:j + 1]) for j in range(3)] if p == 1 else [(Ellipsis, grads[p])]):
                delta, m_new, v_new = _adamw(w_all[p][at], g, m_all[p][at], v_all[p][at])
                outs[p][at] = g
                outs[np_ + p][at] = delta
                outs[2 * np_ + p][at] = m_new
                outs[3 * np_ + p][at] = v_new

    pshapes = [wp.shape, CW_SHARD] + [wv.shape for wv in vec_ws]
    out_shape = [jax.ShapeDtypeStruct((1, 1), F32)] + [jax.ShapeDtypeStruct(s, F32) for s in pshapes] * 4
    outs = pl.pallas_call(
        body, name="small_adam",
        in_specs=[_whole()] * (3 + 3 * np_), out_specs=[_whole()] * len(out_shape), out_shape=out_shape,
        compiler_params=pltpu.CompilerParams(vmem_limit_bytes=VMEM_LIMIT),
    )(all_a, all_b, all_c, wp, cwp, *vec_ws, m_wp, m_cwp, *vec_ms, v_wp, v_cwp, *vec_vs)
    return outs[0], [outs[1 + i * np_:1 + (i + 1) * np_] for i in range(4)]


def kernel(x, w_in, w_pool, pool_scale, w_out, ln1_g, ln1_b, w_up, conv_w, conv_b, w_down, ln2_g, ln2_b, loss_target, m_w_in, m_w_pool, m_pool_scale, m_w_out, m_ln1_g, m_ln1_b, m_w_up, m_conv_w, m_conv_b, m_w_down, m_ln2_g, m_ln2_b, v_w_in, v_w_pool, v_pool_scale, v_w_out, v_ln1_g, v_ln1_b, v_w_up, v_conv_w, v_conv_b, v_w_down, v_ln2_g, v_ln2_b):
    pos = lax.axis_index("c").astype(jnp.int32).reshape(1)
    order = ("w_in", "w_out", "w_up", "w_down")
    w_in_i, w_out_i, w_up_i, w_down_i = range(N_BIG)
    vec_names = ("conv_b", "pool_scale", "ln1_g", "ln1_b", "ln2_g", "ln2_b")

    taps_first = lambda a: jnp.transpose(a, (1, 0, 2))
    gathered = _gather_weights([w_in[0], w_out[0], w_up[0], w_down[0]], taps_first(conv_w), w_pool[0], (w_in_i,))
    cw_full = jnp.transpose(gathered[N_BIG].reshape(N_SHARD, 3, DOWN_SH), (1, 0, 2)).reshape(3, D_FF)
    up_a, up_b, up_c = (0, 176), (176, 176), (352, 160)
    assert up_c[0] + up_c[1] == SHARD_SHAPES[w_up_i][0] // 2

    class MeshComm:
        def __init__(self):
            self.w = {i: gathered[i] for i in range(N_BIG)}
            self.g32, self.g16, self.p32, self.p16, self.recv_b = {}, {}, {}, {}, {}
            self.up_complete = False
            self.tokens, self.chips = {}, []

        def weight(self, name):
            i = order.index(name)
            if name == "w_up" and not self.up_complete:
                (arrs, _), = _comm_only("gather_up_last", [_gather_rider(
                    {i: self.w[i]}, [("d2d_diag", i, up_b), ("d2d", i, up_c)])])
                self.w[i], self.up_complete = arrs[0], True
            full = self.w[i]
            return full.reshape(-1, full.shape[-1]) if name in ("w_out", "w_down") else full

        def _gather(self, ws, ops):
            return _gather_rider({w: self.w[w] for w in ws}, ops), ("w", ws)

        def _pair(self, ws):
            return _pair_rider(ws, [self.g16[w] for w in ws]), ("recv_a", ws)

        def _chip(self, ws):
            return _chip_rider(ws, [self.p16[w] for w in ws]), ("recv_b", ws)

        def plan(self, call):
            out_all, down_all = _whole_half(w_out_i), _whole_half(w_down_i)
            if call == "proj_pool":
                return [self._gather([w_out_i, w_up_i, w_down_i],
                                     [("ici", w_out_i, out_all), ("nbr", w_down_i, down_all),
                                      ("nbr", w_up_i, up_a)])]
            if call == "retention_fwd":
                return [self._gather([w_out_i, w_up_i, w_down_i],
                                     [("d2d", w_out_i, out_all),
                                      ("relay", w_down_i, down_all), ("d2d_nbr", w_down_i, down_all),
                                      ("relay", w_up_i, up_a), ("d2d_nbr", w_up_i, up_a), ("nbr", w_up_i, up_b)])]
            if call == "outproj_ln1":
                return [self._gather([w_up_i, w_down_i],
                                     [("d2d_diag", w_down_i, down_all), ("d2d_diag", w_up_i, up_a),
                                      ("relay", w_up_i, up_b), ("d2d_nbr", w_up_i, up_b), ("ici", w_up_i, up_c)])]
            return []

        def after(self, call):
            return tuple(self.tokens.pop(call, ()))

        def riders(self, call):
            self.pending = self.plan(call)
            return [r for r, _ in self.pending]

        def _start(self, name, rider, before, sibling_barrier=None):
            state, token = _split_start(name, rider, sibling_barrier)
            self.tokens.setdefault(before, []).append(token)
            return state

        def _finish_pair(self, name, state, ws, after):
            _, lands = _split_wait(name, state, after)
            self._finish_sum(ws, lands)

        def landed(self, call, results, outs):
            for (_, (slot, ws)), (inplace, lands) in zip(self.pending, results):
                for w, arr in zip(ws, inplace if len(inplace) else lands):
                    getattr(self, slot)[w] = arr
            if call == "wgrad_out":
                self._finish_pair("pair_exchange_up_wait", self.pair_up, [w_up_i], outs[1])
                self.chips.append(([w_up_i], self._start(
                    "chip_exchange_up_start", self._chip([w_up_i])[0], "wgrad_down")))
            if call == "mix_bwd":
                ws = [w_out_i, w_down_i]
                self._finish_pair("pair_exchange_out_down_wait", self.pair_out_down, ws, outs[0])
            if call == "retention_bwd":
                own, sibling = _split_wait("small_pair_wait", self.small_pair, outs[0])
                self.small_chip = self._start(
                    "small_chip_start", _small_chip_rider(_small_pair_sum(own, sibling)), "wgrad_in")

        def small_gradients(self, loss, small):
            dcw4 = jnp.transpose(small["conv_w"].reshape(3, N_SHARD, DOWN_SH), (1, 0, 2))
            own = [small["w_pool"], _small_pack(loss, [small[n] for n in vec_names]), dcw4]
            ws = [w_out_i, w_down_i]
            parts = [self._chip(ws)[0], _final_rider(own)]
            chip, self.small_pair = _split_parts(
                self._start("chip_out_down_small_pair_start", _merged_rider(parts), "retention_bwd"), parts)
            self.chips.append((ws, chip))

        def gradient(self, name, g32, g16):
            w = order.index(name)
            shape = (N_SHARD,) + SHARD_SHAPES[w]
            self.g32[w], self.g16[w] = g32.reshape(shape), g16.reshape(shape)
            if name == "w_up":
                self.pair_up = self._start("pair_exchange_up_start", self._pair([w])[0], "wgrad_out", 1)
            if name == "w_down":
                self.pair_out_down = self._start("pair_exchange_out_down_start",
                                                 self._pair([w_out_i, w_down_i])[0], "mix_bwd", 2)
            if name == "w_in":
                (_, lands), = _comm_only("pair_exchange_in", [self._pair([w])[0]])
                self._finish_sum([w], lands)
                self.chips.append(([w], self._start("chip_exchange_in_start", self._chip([w])[0], "dx")))

        def _finish_sum(self, ws, lands):
            p32s, p16s = _pair_sum(pos, ws, [self.g32[w] for w in ws], lands)
            for w, p32, p16 in zip(ws, p32s, p16s):
                self.p32[w], self.p16[w] = p32, p16

        def finish(self, after):
            for n, (ws, state) in enumerate(self.chips):
                _, lands = _split_wait("chip_exchange_wait_%d" % n, state, after)
                for w, arr in zip(ws, lands):
                    self.recv_b[w] = arr
            return _split_wait("small_chip_wait", self.small_chip, after)[0]

    comm = MeshComm()
    loss, grad_x, small = _local_step(x[0], loss_target[0], cw_full, conv_b, gathered[N_BIG + 1], pool_scale,
                                      ln1_g, ln1_b, ln2_g, ln2_b, comm)

    given = dict(w_pool=w_pool, pool_scale=pool_scale, ln1_g=ln1_g, ln1_b=ln1_b, conv_w=conv_w, conv_b=conv_b,
                 ln2_g=ln2_g, ln2_b=ln2_b)
    given_m = dict(w_pool=m_w_pool, pool_scale=m_pool_scale, ln1_g=m_ln1_g, ln1_b=m_ln1_b, conv_w=m_conv_w,
                   conv_b=m_conv_b, ln2_g=m_ln2_g, ln2_b=m_ln2_b)
    given_v = dict(w_pool=v_w_pool, pool_scale=v_pool_scale, ln1_g=v_ln1_g, ln1_b=v_ln1_b, conv_w=v_conv_w,
                   conv_b=v_conv_b, ln2_g=v_ln2_g, ln2_b=v_ln2_b)
    args = []
    for src in (given, given_m, given_v):
        args += [src["w_pool"][0], taps_first(src["conv_w"]), [src[n] for n in vec_names]]
    small_sums = comm.finish(grad_x)
    loss_tot, small_out = _small_adam(*small_sums, *args)
    every = range(N_BIG)
    mine = _chip_sum([comm.p32[w] for w in every], [comm.recv_b[w] for w in every])
    final_state, _ = _split_start("pair_exchange_f32_start", _final_rider(mine), 3)
    mine = final_state[2][:N_BIG]
    big = ([w_in, w_out, w_up, w_down], [m_w_in, m_w_out, m_w_up, m_w_down], [v_w_in, v_w_out, v_w_up, v_w_down])
    own_half = _adam_half("adam_own_half", pos, mine, *big)
    _, theirs = _split_wait("pair_exchange_f32_wait", final_state, own_half[0][0])
    big_out = _adam_half("adam_other_half", pos, theirs, *big, into=own_half)

    names = ("w_in", "w_pool", "pool_scale", "w_out", "ln1_g", "ln1_b", "w_up", "conv_w", "conv_b", "w_down",
             "ln2_g", "ln2_b")
    small_names = ("w_pool", "conv_w") + vec_names
    result = [loss_tot.reshape(()), grad_x[None]]
    for kind in range(4):
        for n in names:
            if n in order:
                result.append(big_out[kind][order.index(n)])
            else:
                val = small_out[kind][small_names.index(n)]
                if n == "conv_w":
                    val = taps_first(val)
                elif n == "w_pool":
                    val = val[None]
                result.append(val)
    return tuple(result)
```

```python
import functools

import numpy as np
import jax
import jax.numpy as jnp
from jax import lax
from jax.experimental import pallas as pl
from jax.experimental.pallas import tpu as pltpu

F32 = jnp.float32
BF16 = jnp.bfloat16

D_MODEL = 1024
HEADS = 4
HEAD_DIM = 128
RET_W = HEADS * HEAD_DIM
POOL_WINDOWS = (2, 4, 8, 16)
POOL_W = 512
IN_W = 4 * RET_W + POOL_W
D_FF = 2816
N_SHARD = 4
IN_SH = IN_W // N_SHARD
UP_SH = 2 * D_FF // N_SHARD
DOWN_SH = D_FF // N_SHARD
OUT_SH = D_MODEL // N_SHARD
ROPE_BASE = 10000.0
LN_EPS = 1e-5
RMS_EPS = 1e-6
ALPHA = 2.0 ** 0.25
K_SCALE = HEAD_DIM ** -0.5
SUPER = 256
CHUNK = 64
POOL_HALO = 16
CONV_HALO = 8
FFN_STRIP = 128
LN_ROWS = 32

ADAM_LR = 0.001
ADAM_B1 = 0.9
ADAM_B2 = 0.999
ADAM_EPS = 1e-08
ADAM_WD = 0.01
ADAM_STEP = 10

MESH = pl.DeviceIdType.MESH
VMEM_LIMIT = 56 * 1024 * 1024


def _dot(a, b):
    return jnp.dot(a, b, preferred_element_type=F32)


def _dot_nt(a, b):
    return lax.dot_general(a, b, (((1,), (1,)), ((), ())), preferred_element_type=F32)


def _dot_tn(a, b):
    return lax.dot_general(a, b, (((0,), (0,)), ((), ())), preferred_element_type=F32)


def _sigmoid(x):
    return 1.0 / (1.0 + jnp.exp(-x))


def _params(sem):
    return pltpu.CompilerParams(dimension_semantics=sem, vmem_limit_bytes=VMEM_LIMIT)


def _whole():
    return pl.BlockSpec(memory_space=pltpu.VMEM)


HBM_SPEC = pl.BlockSpec(memory_space=pl.ANY)


class _Rider:
    def __init__(self, inplace, srcs, lands, n_copies, make):
        self.inplace, self.srcs, self.lands, self.n_copies, self.make = list(inplace), list(srcs), list(lands), n_copies, make


def _call(body, *, name, grid, in_specs, out_specs, out_shape, operands, scratch_shapes=(), sem=(),
          aliases=None, riders=(), after=()):
    n_in, n_out, n_scr = len(in_specs), len(out_shape), len(scratch_shapes)
    in_specs, out_specs, out_shape = list(in_specs), list(out_specs), list(out_shape)
    operands, scratch_shapes, aliases = list(operands), list(scratch_shapes), dict(aliases or {})
    in_specs += [_whole()] * len(after)
    operands += list(after)
    for r in riders:
        for a in r.inplace:
            aliases[len(in_specs)] = len(out_shape)
            in_specs.append(HBM_SPEC)
            operands.append(a)
            out_specs.append(HBM_SPEC)
            out_shape.append(jax.ShapeDtypeStruct(a.shape, a.dtype))
        for a in r.srcs:
            in_specs.append(HBM_SPEC)
            operands.append(a)
        for shp in r.lands:
            out_specs.append(HBM_SPEC)
            out_shape.append(shp)
        scratch_shapes += [pltpu.SemaphoreType.DMA((r.n_copies,)), pltpu.SemaphoreType.DMA((r.n_copies,))]

    def full(*refs):
        ins = refs[:n_in]
        at = n_in + len(after)
        r_srcs = []
        for r in riders:
            at += len(r.inplace)
            r_srcs.append(refs[at:at + len(r.srcs)])
            at += len(r.srcs)
        outs = refs[at:at + n_out]
        at += n_out
        r_outs = []
        for r in riders:
            r_outs.append((refs[at:at + len(r.inplace)], refs[at + len(r.inplace):at + len(r.inplace) + len(r.lands)]))
            at += len(r.inplace) + len(r.lands)
        scr = refs[at:at + n_scr]
        at += n_scr
        r_sems = [refs[at + 2 * i:at + 2 * i + 2] for i in range(len(riders))]

        def copies():
            return [r.make(r_outs[i][0], r_srcs[i], r_outs[i][1], r_sems[i][0], r_sems[i][1])
                    for i, r in enumerate(riders)]

        def start():
            for starts, _ in copies():
                for cp in starts:
                    cp.start()

        def finish():
            for _, waits in copies():
                for wait in waits:
                    wait()

        if riders and grid:
            first = functools.reduce(jnp.logical_and, [pl.program_id(d) == 0 for d in range(len(grid))])
            last = functools.reduce(jnp.logical_and, [pl.program_id(d) == grid[d] - 1 for d in range(len(grid))])
            pl.when(first)(start)
            body(*ins, *outs, *scr)
            pl.when(last)(finish)
        else:
            if riders:
                start()
            body(*ins, *outs, *scr)
            if riders:
                finish()

    params = _params(sem) if grid else pltpu.CompilerParams(vmem_limit_bytes=VMEM_LIMIT)
    res = pl.pallas_call(
        full, name=name, grid=grid, in_specs=in_specs, out_specs=out_specs, out_shape=out_shape,
        scratch_shapes=scratch_shapes, input_output_aliases=aliases, compiler_params=params,
    )(*operands)
    outs, at, rider_res = res[:n_out], n_out, []
    for r in riders:
        rider_res.append((res[at:at + len(r.inplace)], res[at + len(r.inplace):at + len(r.inplace) + len(r.lands)]))
        at += len(r.inplace) + len(r.lands)
    return list(outs), rider_res


def _gammas():
    return [1.0 - 2.0 ** (-5.0 - h) for h in range(HEADS)]


def _decay_tables():
    idx = np.arange(SUPER)
    dist = np.abs(idx[:, None] - idx[None, :]).astype(np.float64)
    visible = (idx[None, :] // CHUNK) <= (idx[:, None] // CHUNK)
    mask = np.stack([np.where(visible, g ** dist, 0.0) for g in _gammas()])
    qd = np.concatenate([np.repeat((g ** (idx + 1.0))[:, None], HEAD_DIM, 1) for g in _gammas()], 1)
    kd = np.concatenate([np.repeat((g ** (SUPER - 1.0 - idx))[:, None], HEAD_DIM, 1) for g in _gammas()], 1)
    return (jnp.asarray(mask, F32), jnp.asarray(qd, F32), jnp.asarray(kd, F32))


def _rope_tables(s):
    inv_freq = ROPE_BASE ** (-np.arange(0, HEAD_DIM, 2, dtype=np.float64) / HEAD_DIM)
    ang = np.arange(s, dtype=np.float64)[:, None] * inv_freq[None, :]
    cos, sin = np.cos(ang), np.sin(ang)
    return (jnp.asarray(np.concatenate([cos, cos], 1), F32),
            jnp.asarray(np.concatenate([-sin, sin], 1), F32))


def _rope(t, cosf, sinf):
    return t * cosf + pltpu.roll(t, HEAD_DIM // 2, 1) * sinf


def _rope_t(t, cosf, sinf):
    return t * cosf - pltpu.roll(t, HEAD_DIM // 2, 1) * sinf


def _layernorm_fwd(z):
    mu = jnp.mean(z, axis=-1, keepdims=True)
    zc = z - mu
    var = jnp.mean(zc * zc, axis=-1, keepdims=True)
    rstd = lax.rsqrt(var + LN_EPS)
    return zc * rstd, rstd


def _layernorm_bwd(dy, xhat, rstd, gain):
    dxh = dy * gain
    m1 = jnp.mean(dxh, axis=-1, keepdims=True)
    m2 = jnp.mean(dxh * xhat, axis=-1, keepdims=True)
    return rstd * (dxh - m1 - xhat * m2)


def _proj_pool(x, win4, cosf, sinf, wpool, pscale, ts, riders=(), after=()):
    s = x.shape[0]
    nt = s // ts

    def body(x_ref, w_ref, cos_ref, sin_ref, wp_ref, ps_ref,
             xb_ref, q_ref, k_ref, v_ref, g_ref, pooled_ref, cat_ref, proj_scr, pext_scr):
        i = pl.program_id(0)
        xb = x_ref[...].astype(BF16)
        xb_ref[...] = xb
        for j in range(N_SHARD):
            proj_scr[:, j * IN_SH:(j + 1) * IN_SH] = _dot(xb, w_ref[j])
        cosf_t = cos_ref[...]
        sinf_t = sin_ref[...]
        for h in range(HEADS):
            lo = h * HEAD_DIM
            q_ref[:, lo:lo + HEAD_DIM] = _rope(proj_scr[:, lo:lo + HEAD_DIM], cosf_t, sinf_t).astype(BF16)
            kk = _rope(proj_scr[:, RET_W + lo:RET_W + lo + HEAD_DIM], cosf_t, sinf_t) * K_SCALE
            k_ref[:, lo:lo + HEAD_DIM] = kk.astype(BF16)
        v_ref[...] = proj_scr[:, 2 * RET_W:3 * RET_W].astype(BF16)
        g_ref[...] = proj_scr[:, 3 * RET_W:4 * RET_W]

        @pl.when(i == 0)
        def _():
            pext_scr[0:POOL_HALO, :] = jnp.zeros((POOL_HALO, POOL_W), F32)

        pext_scr[POOL_HALO:POOL_HALO + ts, :] = proj_scr[:, 4 * RET_W:IN_W]
        pos = (i * ts + lax.broadcasted_iota(jnp.int32, (ts, 1), 0) + 1).astype(F32)
        for gi, w in enumerate(POOL_WINDOWS):
            lo = gi * HEAD_DIM
            ext = pext_scr[:, lo:lo + HEAD_DIM]
            acc = ext
            shift = 1
            while shift < w:
                acc = acc + pltpu.roll(acc, shift, 0)
                shift *= 2
            tok = ext[POOL_HALO:POOL_HALO + ts]
            pooled = acc[POOL_HALO:POOL_HALO + ts] / jnp.minimum(pos, float(w)) - tok
            pooled_b = pooled.astype(BF16)
            pooled_ref[:, lo:lo + HEAD_DIM] = pooled_b
            lin = _dot(pooled_b, wp_ref[gi])
            cat_ref[:, lo:lo + HEAD_DIM] = (lin * ps_ref[:, lo:lo + HEAD_DIM]).astype(BF16)
        pext_scr[0:POOL_HALO, :] = pext_scr[ts:ts + POOL_HALO, :]

    tile = lambda w: pl.BlockSpec((ts, w), lambda i: (i, 0))
    return _call(
        body, name="proj_pool", grid=(nt,),
        in_specs=[tile(D_MODEL), _whole(), tile(HEAD_DIM), tile(HEAD_DIM), _whole(), _whole()],
        out_specs=[tile(D_MODEL), tile(RET_W), tile(RET_W), tile(RET_W), tile(RET_W), tile(POOL_W),
                   pl.BlockSpec((ts, POOL_W), lambda i: (i, 1))],
        out_shape=[jax.ShapeDtypeStruct((s, D_MODEL), BF16), jax.ShapeDtypeStruct((s, RET_W), BF16),
                   jax.ShapeDtypeStruct((s, RET_W), BF16), jax.ShapeDtypeStruct((s, RET_W), BF16),
                   jax.ShapeDtypeStruct((s, RET_W), F32), jax.ShapeDtypeStruct((s, POOL_W), BF16),
                   jax.ShapeDtypeStruct((s, 2 * RET_W), BF16)],
        scratch_shapes=[pltpu.VMEM((ts, IN_W), F32), pltpu.VMEM((ts + POOL_HALO, POOL_W), F32)],
        sem=("arbitrary",), operands=(x, win4, cosf, sinf, wpool, pscale), riders=riders, after=after,
    )


def _retention_fwd(q, k, v, g, cat, mask, qd, kd, riders=(), after=()):
    s = q.shape[0]
    ns = s // SUPER
    cdec = [gm ** float(SUPER) for gm in _gammas()]

    def body(q_ref, k_ref, v_ref, g_ref, cat_in, mask_ref, qd_ref, kd_ref,
             ret_ref, cat_ref, st_ref, state_scr):
        del cat_in
        n = pl.program_id(0)

        @pl.when(n == 0)
        def _():
            state_scr[...] = jnp.zeros_like(state_scr)

        for h in range(HEADS):
            sl = slice(h * HEAD_DIM, (h + 1) * HEAD_DIM)
            qh, kh, vh = q_ref[:, sl], k_ref[:, sl], v_ref[:, sl]
            sc = _dot_nt(qh, kh) * mask_ref[h]
            st = state_scr[h]
            stb = st.astype(BF16)
            st_ref[0, h] = stb
            qdb = (qh.astype(F32) * qd_ref[:, sl]).astype(BF16)
            kdb = (kh.astype(F32) * kd_ref[:, sl]).astype(BF16)
            ret = _dot(sc.astype(BF16), vh) + _dot(qdb, stb)
            state_scr[h] = st * cdec[h] + _dot_tn(kdb, vh)
            ret_ref[:, sl] = ret
            r = lax.rsqrt(jnp.mean(ret * ret, axis=-1, keepdims=True) + RMS_EPS)
            gh = g_ref[:, sl]
            cat_ref[:, sl] = ((ret * r) * (gh * _sigmoid(gh))).astype(BF16)

    tile = pl.BlockSpec((SUPER, RET_W), lambda n: (n, 0))
    return _call(
        body, name="retention_fwd", grid=(ns,),
        in_specs=[tile, tile, tile, tile, HBM_SPEC, _whole(), _whole(), _whole()],
        out_specs=[tile, tile, pl.BlockSpec((1, HEADS, HEAD_DIM, HEAD_DIM), lambda n: (n, 0, 0, 0))],
        out_shape=[jax.ShapeDtypeStruct((s, RET_W), F32), jax.ShapeDtypeStruct((s, 2 * RET_W), BF16),
                   jax.ShapeDtypeStruct((ns, HEADS, HEAD_DIM, HEAD_DIM), BF16)],
        scratch_shapes=[pltpu.VMEM((HEADS, HEAD_DIM, HEAD_DIM), F32)],
        aliases={4: 1}, sem=("arbitrary",), operands=(q, k, v, g, cat, mask, qd, kd), riders=riders,
        after=after,
    )


def _outproj_ln1(x, cat, wout, g1, b1, ts, riders=(), after=()):
    s = x.shape[0]

    def body(x_ref, cat_ref, w_ref, g_ref, b_ref, xhat_ref, rstd_ref, h1b_ref):
        z = ALPHA * x_ref[...] + _dot(cat_ref[...], w_ref[...])
        xhat, rstd = _layernorm_fwd(z)
        xhat_ref[...] = xhat
        rstd_ref[...] = rstd
        h1b_ref[...] = (xhat * g_ref[...] + b_ref[...]).astype(BF16)

    tile = lambda w: pl.BlockSpec((ts, w), lambda i: (i, 0))
    return _call(
        body, name="outproj_ln1", grid=(s // ts,),
        in_specs=[tile(D_MODEL), tile(D_MODEL), _whole(), _whole(), _whole()],
        out_specs=[tile(D_MODEL), tile(1), tile(D_MODEL)],
        out_shape=[jax.ShapeDtypeStruct((s, D_MODEL), F32), jax.ShapeDtypeStruct((s, 1), F32),
                   jax.ShapeDtypeStruct((s, D_MODEL), BF16)],
        sem=("arbitrary",), operands=(x, cat, wout, g1, b1), riders=riders, after=after,
    )


def _ffn_fwd_loss(xhat1, h1b, target, wup4, wdown, cw, cb, g1, b1, g2, b2, ts):
    s = xhat1.shape[0]

    def body(xhat_ref, h1b_ref, tgt_ref, wup_ref, wdn_ref, cw_ref, cb_ref, g1_ref, b1_ref, g2_ref, b2_ref,
             ub_ref, act_ref, sd_ref, dz2_ref, dz2b_ref, loss_ref, dg2_ref, db2_ref, val_scr, gext_scr, ffn_scr):
        i = pl.program_id(0)

        @pl.when(i == 0)
        def _():
            gext_scr[0:CONV_HALO, :] = jnp.zeros((CONV_HALO, D_FF), F32)
            loss_ref[...] = jnp.zeros_like(loss_ref)
            dg2_ref[...] = jnp.zeros_like(dg2_ref)
            db2_ref[...] = jnp.zeros_like(db2_ref)

        for half in range(2):
            lo = half * UP_SH
            gext_scr[CONV_HALO:CONV_HALO + ts, lo:lo + UP_SH] = _dot(h1b_ref[...], wup_ref[2 + half])
            val_scr[:, lo:lo + UP_SH] = _dot(h1b_ref[...], wup_ref[half])
            for c0 in range(lo, lo + UP_SH, FFN_STRIP):
                cols = slice(c0, c0 + FFN_STRIP)
                ext = gext_scr[:, cols]
                gate = ext[CONV_HALO:]
                hc = cb_ref[:, cols] + ((pltpu.roll(ext, 2, 0)[CONV_HALO:] * cw_ref[0:1, cols]
                                         + pltpu.roll(ext, 1, 0)[CONV_HALO:] * cw_ref[1:2, cols])
                                        + gate * cw_ref[2:3, cols])
                val = val_scr[:, cols]
                sg = _sigmoid(hc)
                si = hc * sg
                act_ref[:, cols] = (si * val).astype(BF16)
                ub_ref[:, cols] = val.astype(BF16)
                ub_ref[:, D_FF + c0:D_FF + c0 + FFN_STRIP] = gate.astype(BF16)
                sd_ref[:, cols] = hc.astype(BF16)
            part = _dot(act_ref[:, lo:lo + UP_SH], wdn_ref[lo:lo + UP_SH, :])
            if half == 0:
                ffn_scr[...] = part
            else:
                ffn_scr[...] += part

        gext_scr[0:CONV_HALO, :] = gext_scr[ts:ts + CONV_HALO, :]

        loss_acc = jnp.zeros((1, 1), F32)
        dg2_acc = jnp.zeros((1, D_MODEL), F32)
        db2_acc = jnp.zeros((1, D_MODEL), F32)
        for r0 in range(0, ts, LN_ROWS):
            rows = slice(r0, r0 + LN_ROWS)
            h1 = xhat_ref[rows, :] * g1_ref[...] + b1_ref[...]
            xhat2, rstd2 = _layernorm_fwd(ALPHA * h1 + ffn_scr[rows, :])
            diff = (xhat2 * g2_ref[...] + b2_ref[...]) - tgt_ref[rows, :]
            row = jnp.mean(diff * diff, axis=-1, keepdims=True)
            loss_acc = loss_acc + 0.5 * jnp.sum(row, axis=0, keepdims=True)
            dy = diff * (1.0 / D_MODEL)
            dg2_acc = dg2_acc + jnp.sum(dy * xhat2, axis=0, keepdims=True)
            db2_acc = db2_acc + jnp.sum(dy, axis=0, keepdims=True)
            dz2 = _layernorm_bwd(dy, xhat2, rstd2, g2_ref[...])
            dz2_ref[rows, :] = dz2
            dz2b_ref[rows, :] = dz2.astype(BF16)
        loss_ref[...] += loss_acc
        dg2_ref[...] += dg2_acc
        db2_ref[...] += db2_acc

    tile = lambda w: pl.BlockSpec((ts, w), lambda i: (i, 0))
    acc = lambda w: pl.BlockSpec((1, w), lambda i: (0, 0))
    return pl.pallas_call(
        body, name="ffn_fwd_loss", grid=(s // ts,),
        in_specs=[tile(D_MODEL), tile(D_MODEL), tile(D_MODEL)] + [_whole()] * 8,
        out_specs=[tile(2 * D_FF), tile(D_FF), tile(D_FF), tile(D_MODEL), tile(D_MODEL),
                   acc(1), acc(D_MODEL), acc(D_MODEL)],
        out_shape=[jax.ShapeDtypeStruct((s, 2 * D_FF), BF16), jax.ShapeDtypeStruct((s, D_FF), BF16),
                   jax.ShapeDtypeStruct((s, D_FF), BF16), jax.ShapeDtypeStruct((s, D_MODEL), F32),
                   jax.ShapeDtypeStruct((s, D_MODEL), BF16),
                   jax.ShapeDtypeStruct((1, 1), F32), jax.ShapeDtypeStruct((1, D_MODEL), F32),
                   jax.ShapeDtypeStruct((1, D_MODEL), F32)],
        scratch_shapes=[pltpu.VMEM((ts, D_FF), F32), pltpu.VMEM((ts + CONV_HALO, D_FF), F32),
                        pltpu.VMEM((ts, D_MODEL), F32)],
        compiler_params=_params(("arbitrary",)),
    )(xhat1, h1b, target, wup4, wdown, cw, cb, g1, b1, g2, b2)


def _ffn_bwd(dz2, dz2b, ub, sd, xhat1, rstd1, wup4, wdown, cw, g1, ts):
    s = dz2.shape[0]
    nt = s // ts

    def body(dz2_ref, dz2b_ref, ub_ref, sd_ref, xhat_ref, rstd_ref, wup_ref, wdn_ref, cw_ref, g1_ref,
             dub_ref, dz1_ref, dz1b_ref, dg1_ref, db1_ref, dcw_ref, dcb_ref, dext_scr, da_scr):
        i = pl.program_id(0)

        @pl.when(i == 0)
        def _():
            dext_scr[ts:ts + CONV_HALO, :] = jnp.zeros((CONV_HALO, D_FF), F32)
            dg1_ref[...] = jnp.zeros_like(dg1_ref)
            db1_ref[...] = jnp.zeros_like(db1_ref)
            dcw_ref[...] = jnp.zeros_like(dcw_ref)
            dcb_ref[...] = jnp.zeros_like(dcb_ref)

        da_scr[...] = _dot_nt(dz2b_ref[...], wdn_ref[...])
        n_ext = ts + CONV_HALO
        for c0 in range(0, D_FF, FFN_STRIP):
            cols = slice(c0, c0 + FFN_STRIP)
            gcols = slice(D_FF + c0, D_FF + c0 + FFN_STRIP)
            val = ub_ref[:, cols].astype(F32)
            gate = ub_ref[:, gcols].astype(F32)
            da = da_scr[:, cols]
            hc = sd_ref[:, cols].astype(F32)
            sg = _sigmoid(hc)
            dhc = da * val * (sg * (1.0 + hc * (1.0 - sg)))
            dext_scr[0:ts, cols] = dhc
            dext = dext_scr[:, cols]
            dhc1 = pltpu.roll(dext, n_ext - 1, 0)[0:ts]
            dhc2 = pltpu.roll(dext, n_ext - 2, 0)[0:ts]
            dcb_ref[:, cols] += jnp.sum(dhc, axis=0, keepdims=True)
            dcw_ref[0:1, cols] += jnp.sum(dhc2 * gate, axis=0, keepdims=True)
            dcw_ref[1:2, cols] += jnp.sum(dhc1 * gate, axis=0, keepdims=True)
            dcw_ref[2:3, cols] += jnp.sum(dhc * gate, axis=0, keepdims=True)
            dgate = dhc * cw_ref[2:3, cols] + dhc1 * cw_ref[1:2, cols] + dhc2 * cw_ref[0:1, cols]
            dub_ref[:, cols] = (da * (hc * sg)).astype(BF16)
            dub_ref[:, gcols] = dgate.astype(BF16)
        dext_scr[ts:n_ext, :] = dext_scr[0:CONV_HALO, :]
        dh1 = ALPHA * dz2_ref[...]
        for j in range(N_SHARD):
            dh1 = dh1 + _dot_nt(dub_ref[:, j * UP_SH:(j + 1) * UP_SH], wup_ref[j])
        xhat = xhat_ref[...]
        dg1_ref[...] += jnp.sum(dh1 * xhat, axis=0, keepdims=True)
        db1_ref[...] += jnp.sum(dh1, axis=0, keepdims=True)
        dz1 = _layernorm_bwd(dh1, xhat, rstd_ref[...], g1_ref[...])
        dz1_ref[...] = dz1
        dz1b_ref[...] = dz1.astype(BF16)

    tile = lambda w: pl.BlockSpec((ts, w), lambda i: (nt - 1 - i, 0))
    acc = lambda rws, w: pl.BlockSpec((rws, w), lambda i: (0, 0))
    return pl.pallas_call(
        body, name="ffn_bwd", grid=(nt,),
        in_specs=[tile(D_MODEL), tile(D_MODEL), tile(2 * D_FF), tile(D_FF), tile(D_MODEL), tile(1)]
        + [_whole()] * 4,
        out_specs=[tile(2 * D_FF), tile(D_MODEL), tile(D_MODEL), acc(1, D_MODEL), acc(1, D_MODEL),
                   acc(3, D_FF), acc(1, D_FF)],
        out_shape=[jax.ShapeDtypeStruct((s, 2 * D_FF), BF16),
                   jax.ShapeDtypeStruct((s, D_MODEL), F32), jax.ShapeDtypeStruct((s, D_MODEL), BF16),
                   jax.ShapeDtypeStruct((1, D_MODEL), F32),
                   jax.ShapeDtypeStruct((1, D_MODEL), F32), jax.ShapeDtypeStruct((3, D_FF), F32),
                   jax.ShapeDtypeStruct((1, D_FF), F32)],
        scratch_shapes=[pltpu.VMEM((ts + CONV_HALO, D_FF), F32), pltpu.VMEM((ts, D_FF), F32)],
        compiler_params=_params(("arbitrary",)),
    )(dz2, dz2b, ub, sd, xhat1, rstd1, wup4, wdown, cw, g1)


def _mix_bwd(dz1, pooled, ret, g, wout, wpool, pscale, ts, riders=(), after=()):
    s = dz1.shape[0]
    nt = s // ts

    def body(dz1_ref, pooled_ref, ret_ref, g_ref, wout_ref, wp_ref, ps_ref,
             dret_ref, dgp_ref, dwp_ref, dps_ref, eext_scr):
        i = pl.program_id(0)
        r = nt - 1 - i

        @pl.when(i == 0)
        def _():
            eext_scr[ts:ts + POOL_HALO, :] = jnp.zeros((POOL_HALO, POOL_W), F32)
            dwp_ref[...] = jnp.zeros_like(dwp_ref)
            dps_ref[...] = jnp.zeros_like(dps_ref)

        dzb = dz1_ref[...].astype(BF16)
        dcat_r = _dot_nt(dzb, wout_ref[0:RET_W, :])
        dcat_p = _dot_nt(dzb, wout_ref[RET_W:2 * RET_W, :])
        pos = (r * ts + lax.broadcasted_iota(jnp.int32, (ts, 1), 0) + 1).astype(F32)
        dpooled = []
        for gi, w in enumerate(POOL_WINDOWS):
            sl = slice(gi * HEAD_DIM, (gi + 1) * HEAD_DIM)
            pb = pooled_ref[:, sl]
            dy = dcat_p[:, sl]
            dps_ref[:, sl] += jnp.sum(dy * _dot(pb, wp_ref[gi]), axis=0, keepdims=True)
            dlin = (dy * ps_ref[:, sl]).astype(BF16)
            dwp_ref[gi] += _dot_tn(pb, dlin)
            dpg = _dot_nt(dlin, wp_ref[gi])
            dpooled.append(dpg)
            eext_scr[0:ts, sl] = dpg / jnp.minimum(pos, float(w))
        for gi, w in enumerate(POOL_WINDOWS):
            sl = slice(gi * HEAD_DIM, (gi + 1) * HEAD_DIM)
            acc = eext_scr[:, sl]
            shift = 1
            while shift < w:
                acc = acc + pltpu.roll(acc, ts + POOL_HALO - shift, 0)
                shift *= 2
            dgp_ref[:, RET_W + gi * HEAD_DIM:RET_W + (gi + 1) * HEAD_DIM] = (acc[0:ts] - dpooled[gi]).astype(BF16)
        eext_scr[ts:ts + POOL_HALO, :] = eext_scr[0:POOL_HALO, :]
        for h in range(HEADS):
            sl = slice(h * HEAD_DIM, (h + 1) * HEAD_DIM)
            rt = ret_ref[:, sl]
            rr = lax.rsqrt(jnp.mean(rt * rt, axis=-1, keepdims=True) + RMS_EPS)
            rn = rt * rr
            gh = g_ref[:, sl]
            sg = _sigmoid(gh)
            dy = dcat_r[:, sl]
            dgp_ref[:, sl] = (dy * rn * (sg * (1.0 + gh * (1.0 - sg)))).astype(BF16)
            drn = dy * (gh * sg)
            dret_ref[:, sl] = (rr * (drn - rn * jnp.mean(drn * rn, axis=-1, keepdims=True))).astype(BF16)

    tile = lambda w: pl.BlockSpec((ts, w), lambda i: (nt - 1 - i, 0))
    return _call(
        body, name="mix_bwd", grid=(nt,),
        in_specs=[tile(D_MODEL), tile(POOL_W), tile(RET_W), tile(RET_W), _whole(), _whole(), _whole()],
        out_specs=[tile(RET_W), tile(2 * RET_W),
                   pl.BlockSpec((len(POOL_WINDOWS), HEAD_DIM, HEAD_DIM), lambda i: (0, 0, 0)),
                   pl.BlockSpec((1, POOL_W), lambda i: (0, 0))],
        out_shape=[jax.ShapeDtypeStruct((s, RET_W), BF16), jax.ShapeDtypeStruct((s, 2 * RET_W), BF16),
                   jax.ShapeDtypeStruct((len(POOL_WINDOWS), HEAD_DIM, HEAD_DIM), F32),
                   jax.ShapeDtypeStruct((1, POOL_W), F32)],
        scratch_shapes=[pltpu.VMEM((ts + POOL_HALO, POOL_W), F32)],
        sem=("arbitrary",), operands=(dz1, pooled, ret, g, wout, wpool, pscale), riders=riders,
        after=after,
    )


def _retention_bwd(q, k, v, dret, dgp, states, mask, qd, kd, cosf, sinf, riders=(), after=()):
    s = q.shape[0]
    ns = s // SUPER
    cdec = [gm ** float(SUPER) for gm in _gammas()]

    def body(q_ref, k_ref, v_ref, do_ref, dgp_ref, st_ref, mask_ref, qd_ref, kd_ref, cos_ref, sin_ref,
             dproj_ref, dstate_scr):
        i = pl.program_id(0)

        @pl.when(i == 0)
        def _():
            dstate_scr[...] = jnp.zeros_like(dstate_scr)

        cosf_t = cos_ref[...]
        sinf_t = sin_ref[...]
        for h in range(HEADS):
            sl = slice(h * HEAD_DIM, (h + 1) * HEAD_DIM)
            qh, kh, vh, doh = q_ref[:, sl], k_ref[:, sl], v_ref[:, sl], do_ref[:, sl]
            dscb = (_dot_nt(doh, vh) * mask_ref[0, h]).astype(BF16)
            dsctb = (_dot_nt(vh, doh) * mask_ref[1, h]).astype(BF16)
            sctb = (_dot_nt(kh, qh) * mask_ref[1, h]).astype(BF16)
            stb = st_ref[0, h]
            dst = dstate_scr[h]
            dstb = dst.astype(BF16)
            qdb = (qh.astype(F32) * qd_ref[:, sl]).astype(BF16)
            kdb = (kh.astype(F32) * kd_ref[:, sl]).astype(BF16)
            dq = _dot(dscb, kh) + _dot_nt(doh, stb) * qd_ref[:, sl]
            dk = _dot(dsctb, qh) + _dot_nt(vh, dstb) * kd_ref[:, sl]
            dv = _dot(sctb, doh) + _dot(kdb, dstb)
            dstate_scr[h] = dst * cdec[h] + _dot_tn(qdb, doh)
            lo = h * HEAD_DIM
            dproj_ref[:, lo:lo + HEAD_DIM] = _rope_t(dq, cosf_t, sinf_t).astype(BF16)
            dproj_ref[:, RET_W + lo:RET_W + lo + HEAD_DIM] = _rope_t(dk * K_SCALE, cosf_t, sinf_t).astype(BF16)
            dproj_ref[:, 2 * RET_W + lo:2 * RET_W + lo + HEAD_DIM] = dv.astype(BF16)
        dproj_ref[:, 3 * RET_W:IN_W] = dgp_ref[...]

    tile = lambda w: pl.BlockSpec((SUPER, w), lambda i: (ns - 1 - i, 0))
    return _call(
        body, name="retention_bwd", grid=(ns,),
        in_specs=[tile(RET_W), tile(RET_W), tile(RET_W), tile(RET_W), tile(2 * RET_W),
                  pl.BlockSpec((1, HEADS, HEAD_DIM, HEAD_DIM), lambda i: (ns - 1 - i, 0, 0, 0)),
                  _whole(), _whole(), _whole(), tile(HEAD_DIM), tile(HEAD_DIM)],
        out_specs=[tile(IN_W)],
        out_shape=[jax.ShapeDtypeStruct((s, IN_W), BF16)],
        scratch_shapes=[pltpu.VMEM((HEADS, HEAD_DIM, HEAD_DIM), F32)],
        sem=("arbitrary",), operands=(q, k, v, dret, dgp, states, mask, qd, kd, cosf, sinf), riders=riders,
        after=after,
    )


def _dx(dz1, dproj, win4, ts, riders=(), after=()):
    s = dz1.shape[0]

    def body(dz1_ref, dp_ref, w_ref, dx_ref):
        acc = ALPHA * dz1_ref[...]
        for j in range(N_SHARD):
            acc = acc + _dot_nt(dp_ref[:, j * IN_SH:(j + 1) * IN_SH], w_ref[j])
        dx_ref[...] = acc

    tile = lambda w: pl.BlockSpec((ts, w), lambda i: (i, 0))
    return _call(
        body, name="dx", grid=(s // ts,),
        in_specs=[tile(D_MODEL), tile(IN_W), _whole()],
        out_specs=[tile(D_MODEL)],
        out_shape=[jax.ShapeDtypeStruct((s, D_MODEL), F32)],
        sem=("arbitrary",), operands=(dz1, dproj, win4), riders=riders, after=after,
    )


def _wgrad(a, b, tm, tn, name, stacked, m_outer, riders=(), after=()):
    s, m = a.shape
    n = b.shape[1]

    def body(a_ref, b_ref, o32_ref, o16_ref):
        res = _dot_tn(a_ref[...], b_ref[...])
        o32_ref[...] = res.reshape(o32_ref.shape)
        o16_ref[...] = res.astype(BF16).reshape(o16_ref.shape)

    if m_outer:
        grid, blocks = (m // tm, n // tn), (lambda g0, g1: (g0, g1))
    else:
        grid, blocks = (n // tn, m // tm), (lambda g0, g1: (g1, g0))
    if stacked:
        shape = (n // tn, m, tn)
        ospec = pl.BlockSpec((1, tm, tn), lambda g0, g1: (blocks(g0, g1)[1], blocks(g0, g1)[0], 0))
    else:
        shape = (m, n)
        ospec = pl.BlockSpec((tm, tn), lambda g0, g1: blocks(g0, g1))
    return _call(
        body, name=name, grid=grid,
        in_specs=[pl.BlockSpec((s, tm), lambda g0, g1: (0, blocks(g0, g1)[0])),
                  pl.BlockSpec((s, tn), lambda g0, g1: (0, blocks(g0, g1)[1]))],
        out_specs=[ospec, ospec],
        out_shape=[jax.ShapeDtypeStruct(shape, F32), jax.ShapeDtypeStruct(shape, BF16)],
        sem=("arbitrary", "arbitrary"), operands=(a, b), riders=riders, after=after,
    )


def _wgrad_send(a, b, tn, name, barrier_id, after=()):
    s, m = a.shape
    n = b.shape[1]
    nb, hm = n // tn, m // 2

    def body(*refs):
        a_ref, b_ref = refs[:2]
        o32_ref, land_ref, send_scr, send_sems, recv_sems = refs[2 + len(after):]
        j = pl.program_id(0)
        x, y, c = _mesh_pos()

        @pl.when(j == 0)
        def _():
            barrier = pltpu.get_barrier_semaphore()
            pl.semaphore_signal(barrier, inc=1, device_id=(x, y, 1 - c), device_id_type=MESH)
            pl.semaphore_wait(barrier, 1)

        o32_ref[0] = _dot_tn(a_ref[...], b_ref[...])
        theirs = pl.ds(pl.multiple_of((1 - c) * hm, 16), hm)
        copies = [pltpu.make_async_remote_copy(
            src_ref=send_scr.at[blk], dst_ref=land_ref.at[blk], send_sem=send_sems.at[blk],
            recv_sem=recv_sems.at[blk], device_id=(x, y, 1 - c), device_id_type=MESH) for blk in range(nb)]
        for blk in range(nb):
            @pl.when(j == blk)
            def _(blk=blk):
                send_scr[blk] = o32_ref[0, theirs, :].astype(BF16)
                copies[blk].start()

        @pl.when(j == nb - 1)
        def _():
            for cp in copies:
                cp.wait()

    return pl.pallas_call(
        body, name=name, grid=(nb,),
        in_specs=[pl.BlockSpec((s, m), lambda j: (0, 0)), pl.BlockSpec((s, tn), lambda j: (0, j))]
        + [_whole()] * len(after),
        out_specs=[pl.BlockSpec((1, m, tn), lambda j: (j, 0, 0)), HBM_SPEC],
        out_shape=[jax.ShapeDtypeStruct((nb, m, tn), F32), jax.ShapeDtypeStruct((nb, hm, tn), BF16)],
        scratch_shapes=[pltpu.VMEM((nb, hm, tn), BF16), pltpu.SemaphoreType.DMA((nb,)),
                        pltpu.SemaphoreType.DMA((nb,))],
        compiler_params=pltpu.CompilerParams(dimension_semantics=("arbitrary",), vmem_limit_bytes=VMEM_LIMIT,
                                             collective_id=barrier_id),
    )(a, b, *after)


class _NoComm:
    def __init__(self, win4, wout, wup4, wdown):
        self.weights = dict(w_in=win4, w_out=wout, w_up=wup4, w_down=wdown)
        self.grads = {}

    def weight(self, name):
        return self.weights[name]

    def riders(self, call):
        return ()

    def after(self, call):
        return ()

    def landed(self, call, results, outs):
        pass

    def small_gradients(self, loss, small):
        pass

    def gradient(self, name, g32, g16):
        self.grads[name] = (g32, g16)

    def wgrad_in(self, xb, dproj):
        (g32, g16), _ = _wgrad(xb, dproj, D_MODEL, IN_SH, "wgrad_in", True, True)
        self.gradient("w_in", g32, g16)


def _local_step(x, target, cw, cb, wpool_b, pscale, g1, b1, g2, b2, comm):
    s = x.shape[0]
    ts_a = min(512, s)
    ts_f = min(256, s)
    mask, qd, kd = _decay_tables()
    cosf, sinf = _rope_tables(s)

    def run(call, fn, *args):
        outs, res = fn(*args, riders=comm.riders(call), after=comm.after(call))
        comm.landed(call, res, outs)
        return outs

    xb, q, k, v, g, pooled, cat = run("proj_pool", _proj_pool, x, comm.weight("w_in"), cosf, sinf, wpool_b,
                                      pscale, ts_a)
    ret, cat, states = run("retention_fwd", _retention_fwd, q, k, v, g, cat, mask, qd, kd)
    wout = comm.weight("w_out")
    xhat1, rstd1, h1b = run("outproj_ln1", _outproj_ln1, x, cat, wout, g1, b1, ts_a)
    wup4, wdown = comm.weight("w_up"), comm.weight("w_down")
    ub, act, sd, dz2, dz2b, loss, dg2, db2 = _ffn_fwd_loss(xhat1, h1b, target, wup4, wdown, cw, cb, g1, b1, g2, b2,
                                                           ts_f)

    dub, dz1, dz1b, dg1, db1, dcw, dcb = _ffn_bwd(dz2, dz2b, ub, sd, xhat1, rstd1, wup4, wdown, cw, g1, ts_f)
    half = D_MODEL // 2
    comm.gradient("w_up", *run("wgrad_up", _wgrad, h1b, dub, half, UP_SH, "wgrad_up", True, False))
    comm.gradient("w_out", *run("wgrad_out", _wgrad, cat, dz1b, D_MODEL, half, "wgrad_out", False, True))
    comm.gradient("w_down", *run("wgrad_down", _wgrad, act, dz2b, D_FF // 2, half, "wgrad_down", False, True))
    dret, dgp, dwp, dps = run("mix_bwd", _mix_bwd, dz1b, pooled, ret, g, wout, wpool_b, pscale, ts_a)
    small = dict(w_pool=dwp, pool_scale=dps, ln1_g=dg1, ln1_b=db1, conv_w=dcw, conv_b=dcb,
                 ln2_g=dg2, ln2_b=db2)
    comm.small_gradients(loss, small)
    mask_both = jnp.stack([mask, jnp.swapaxes(mask, 1, 2)])
    dproj, = run("retention_bwd", _retention_bwd, q, k, v, dret, dgp, states, mask_both, qd, kd, cosf, sinf)
    comm.wgrad_in(xb, dproj)
    (grad_x,), _ = _dx(dz1, dproj, comm.weight("w_in"), ts_a, after=comm.after("dx"))
    return loss, grad_x, small


CAST_ROWS = 64
SHARD_SHAPES = ((D_MODEL, IN_SH), (OUT_SH, D_MODEL), (D_MODEL, UP_SH), (DOWN_SH, D_MODEL))
N_BIG = len(SHARD_SHAPES)
CW_SHARD = (3, 1, DOWN_SH)


def _mesh_pos():
    return lax.axis_index("x"), lax.axis_index("y"), lax.axis_index("c")


def _other_chips(x, y):
    return [(1 - x, y), (x, 1 - y), (1 - x, 1 - y)]


def _half_rows(w, which):
    hr = SHARD_SHAPES[w][0] // 2
    return pl.ds(pl.multiple_of(which * hr, 16), hr)


def _gather_weights(shards, cw_shard, wpool, full):
    def body(*refs):
        in_refs = refs[:N_BIG]
        cw_ref, wpool_ref = refs[N_BIG:N_BIG + 2]
        out_refs = refs[N_BIG + 2:2 * N_BIG + 2]
        cwo_ref, wpool_b_ref = refs[2 * N_BIG + 2:2 * N_BIG + 4]
        stage = refs[2 * N_BIG + 4:3 * N_BIG + 4]
        raw = refs[3 * N_BIG + 4:4 * N_BIG + 4 - len(full)]
        send_sems, recv_sems, fsend_sems, frecv_sems, cw_send, cw_recv, local_sems, load_sems = \
            refs[4 * N_BIG + 4 - len(full):]
        x, y, c = _mesh_pos()
        j0 = 2 * x + y
        chips = _other_chips(x, y)

        fetched = [w for w in range(N_BIG) if w not in full]
        f32 = {w: in_refs[w] for w in full}
        loads = []
        for n, w in enumerate(fetched):
            f32[w] = raw[n]
            loads.append(pltpu.make_async_copy(in_refs[w], raw[n], load_sems.at[n]))
            loads[-1].start()

        def cast_to_stage(w):
            def cast(i, carry):
                rows = pl.ds(pl.multiple_of(i * CAST_ROWS, CAST_ROWS), CAST_ROWS)
                stage[w][rows, :] = f32[w][rows, :].astype(BF16)
                return carry
            lax.fori_loop(0, SHARD_SHAPES[w][0] // CAST_ROWS, cast, 0)

        for w in full:
            cast_to_stage(w)

        jx, jy, jd = 2 * (1 - x) + y, 2 * x + (1 - y), 2 * (1 - x) + (1 - y)
        neighbours = [((1 - x, y, c), jx), ((x, 1 - y, c), jy)]
        passed = jnp.where(c == 0, jx, jy)
        pass_to = (jnp.where(c == 0, x, 1 - x), jnp.where(c == 0, 1 - y, y), c)

        def nbr(w, k, block):
            return pltpu.make_async_remote_copy(
                src_ref=stage[w].at[_half_rows(w, c), :], dst_ref=out_refs[w].at[block, _half_rows(w, c), :],
                send_sem=send_sems.at[w, k], recv_sem=recv_sems.at[w, k],
                device_id=neighbours[k][0], device_id_type=MESH)

        def relay(w, block):
            return pltpu.make_async_remote_copy(
                src_ref=out_refs[w].at[passed, _half_rows(w, c), :],
                dst_ref=out_refs[w].at[block, _half_rows(w, c), :],
                send_sem=send_sems.at[w, 2], recv_sem=recv_sems.at[w, 2],
                device_id=pass_to, device_id_type=MESH)

        def d2d(w, k, block, half):
            return pltpu.make_async_remote_copy(
                src_ref=out_refs[w].at[block, _half_rows(w, half), :],
                dst_ref=out_refs[w].at[block, _half_rows(w, half), :],
                send_sem=fsend_sems.at[w, k], recv_sem=frecv_sems.at[w, k],
                device_id=(x, y, 1 - c), device_id_type=MESH)

        def conv(k, block):
            chip = chips[k]
            return pltpu.make_async_remote_copy(
                src_ref=cw_ref, dst_ref=cwo_ref.at[block], send_sem=cw_send.at[k], recv_sem=cw_recv.at[k],
                device_id=(chip[0], chip[1], c), device_id_type=MESH)

        sent = [nbr(w, k, j0) for w in full for k in range(2)] + [conv(k, j0) for k in range(3)]
        for cp in sent:
            cp.start()
        for n, w in enumerate(fetched):
            loads[n].wait()
            cast_to_stage(w)
        local = [pltpu.make_async_copy(stage[w], out_refs[w].at[j0], local_sems.at[w]) for w in range(N_BIG)]
        local.append(pltpu.make_async_copy(cw_ref, cwo_ref.at[j0], local_sems.at[N_BIG]))
        for cp in local:
            cp.start()
        wpool_b_ref[...] = wpool_ref[...].astype(BF16)
        for w in full:
            for k, (_, block) in enumerate(neighbours):
                nbr(w, k, block).wait_recv()
            later = [relay(w, passed)] + [d2d(w, k, block, c) for k, (_, block) in enumerate(neighbours)]
            for cp in later:
                cp.start()
            sent += later
        for w in full:
            relay(w, jd).wait_recv()
            fw = d2d(w, 2, jd, c)
            fw.start()
            sent.append(fw)
        for w in full:
            for k, block in enumerate([jx, jy, jd]):
                d2d(w, k, block, 1 - c).wait_recv()
        for k, chip in enumerate(chips):
            conv(k, 2 * chip[0] + chip[1]).wait_recv()
        for cp in sent:
            cp.wait_send()
        for cp in local:
            cp.wait()

    out_shape = [jax.ShapeDtypeStruct((N_SHARD,) + shp, BF16) for shp in SHARD_SHAPES]
    out_shape.append(jax.ShapeDtypeStruct((N_SHARD,) + CW_SHARD, F32))
    out_shape.append(jax.ShapeDtypeStruct(wpool.shape, BF16))
    return pl.pallas_call(
        body, name="gather_weights",
        in_specs=[_whole() if w in full else HBM_SPEC for w in range(N_BIG)] + [_whole()] * 2,
        out_specs=[HBM_SPEC] * (N_BIG + 1) + [_whole()],
        out_shape=out_shape,
        scratch_shapes=[pltpu.VMEM(shp, BF16) for shp in SHARD_SHAPES]
        + [pltpu.VMEM(shp, F32) for w, shp in enumerate(SHARD_SHAPES) if w not in full] + [
            pltpu.SemaphoreType.DMA((N_BIG, 3)), pltpu.SemaphoreType.DMA((N_BIG, 3)),
            pltpu.SemaphoreType.DMA((N_BIG, 3)), pltpu.SemaphoreType.DMA((N_BIG, 3)),
            pltpu.SemaphoreType.DMA((3,)), pltpu.SemaphoreType.DMA((3,)),
            pltpu.SemaphoreType.DMA((N_BIG + 1,)), pltpu.SemaphoreType.DMA((N_BIG - len(full),))],
        compiler_params=pltpu.CompilerParams(vmem_limit_bytes=VMEM_LIMIT),
    )(*shards, cw_shard, wpool)


def _gather_rider(arrays, ops):
    ws = sorted(arrays)

    def make(inplace, srcs, lands, send_sems, recv_sems):
        del srcs, lands
        x, y, c = _mesh_pos()
        j0, jx, jy, jd = 2 * x + y, 2 * (1 - x) + y, 2 * x + (1 - y), 2 * (1 - x) + (1 - y)
        x_nbr, y_nbr, sibling = (1 - x, y, c), (x, 1 - y, c), (x, y, 1 - c)
        starts, waits = [], []
        for n, (kind, w, (r0, nr)) in enumerate(ops):
            ref = inplace[ws.index(w)]
            hr = SHARD_SHAPES[w][0] // 2
            rows = lambda core: pl.ds(pl.multiple_of(core * hr + r0, 16), nr)
            mine, theirs = rows(c), rows(1 - c)
            if kind == "ici":
                moves = [(ref.at[j0, mine, :], x_nbr, ref.at[jx, mine, :]),
                         (ref.at[j0, mine, :], y_nbr, ref.at[jy, mine, :]),
                         (ref.at[j0, mine, :], (1 - x, 1 - y, c), ref.at[jd, mine, :])]
            elif kind == "nbr":
                moves = [(ref.at[j0, mine, :], x_nbr, ref.at[jx, mine, :]),
                         (ref.at[j0, mine, :], y_nbr, ref.at[jy, mine, :])]
            elif kind == "relay":
                passed = jnp.where(c == 0, jx, jy)
                to = (jnp.where(c == 0, x, 1 - x), jnp.where(c == 0, 1 - y, y), c)
                moves = [(ref.at[passed, mine, :], to, ref.at[jd, mine, :])]
            else:
                blocks = dict(d2d=[jx, jy, jd], d2d_nbr=[jx, jy], d2d_diag=[jd])[kind]
                moves = [(ref.at[b, mine, :], sibling, ref.at[b, theirs, :]) for b in blocks]
            for k, (src, to, landing) in enumerate(moves):
                sems = dict(send_sem=send_sems.at[3 * n + k], recv_sem=recv_sems.at[3 * n + k],
                            device_id=to, device_id_type=MESH)
                send = pltpu.make_async_remote_copy(src_ref=src, dst_ref=src, **sems)
                arrival = pltpu.make_async_remote_copy(src_ref=src, dst_ref=landing, **sems)
                starts.append(send)
                waits += [arrival.wait_recv, send.wait_send]
        return starts, waits

    return _Rider([arrays[w] for w in ws], [], [], 3 * len(ops), make)


def _whole_half(w):
    return (0, SHARD_SHAPES[w][0] // 2)


def _pair_rider(ws, g16s):
    def make(inplace, srcs, lands, send_sems, recv_sems):
        del inplace
        x, y, c = _mesh_pos()
        copies = [pltpu.make_async_remote_copy(
            src_ref=srcs[i].at[:, _half_rows(w, 1 - c), :], dst_ref=lands[i],
            send_sem=send_sems.at[i], recv_sem=recv_sems.at[i], device_id=(x, y, 1 - c), device_id_type=MESH)
            for i, w in enumerate(ws)]
        return copies, [cp.wait for cp in copies]

    lands = [jax.ShapeDtypeStruct((N_SHARD, SHARD_SHAPES[w][0] // 2, SHARD_SHAPES[w][1]), BF16) for w in ws]
    return _Rider([], g16s, lands, len(ws), make)


def _chip_rider(ws, p16s):
    def make(inplace, srcs, lands, send_sems, recv_sems):
        del inplace
        x, y, c = _mesh_pos()
        copies = []
        for i in range(len(ws)):
            for k, chip in enumerate(_other_chips(x, y)):
                copies.append(pltpu.make_async_remote_copy(
                    src_ref=srcs[i].at[2 * chip[0] + chip[1]], dst_ref=lands[i].at[k],
                    send_sem=send_sems.at[3 * i + k], recv_sem=recv_sems.at[3 * i + k],
                    device_id=(chip[0], chip[1], c), device_id_type=MESH))
        return copies, [cp.wait for cp in copies]

    lands = [jax.ShapeDtypeStruct((3, SHARD_SHAPES[w][0] // 2, SHARD_SHAPES[w][1]), BF16) for w in ws]
    return _Rider([], p16s, lands, 3 * len(ws), make)


def _final_rider(halves):
    def make(inplace, srcs, lands, send_sems, recv_sems):
        del inplace
        x, y, c = _mesh_pos()
        copies = [pltpu.make_async_remote_copy(
            src_ref=srcs[i], dst_ref=lands[i], send_sem=send_sems.at[i], recv_sem=recv_sems.at[i],
            device_id=(x, y, 1 - c), device_id_type=MESH) for i in range(len(halves))]
        return copies, [cp.wait for cp in copies]

    return _Rider([], halves, [jax.ShapeDtypeStruct(h.shape, h.dtype) for h in halves], len(halves), make)


def _comm_only(name, riders):
    _, res = _call(lambda: None, name=name, grid=(), in_specs=[], out_specs=[], out_shape=[], operands=(),
                   riders=riders)
    return res


class _SemList:
    def __init__(self, refs):
        self.at = list(refs)


def _merged_rider(riders):
    srcs = [a for r in riders for a in r.srcs]
    lands = [a for r in riders for a in r.lands]

    def make(inplace, src_refs, land_refs, send_sems, recv_sems):
        starts, waits = [], []
        s0 = l0 = c0 = 0
        for r in riders:
            part = r.make(inplace, src_refs[s0:s0 + len(r.srcs)], land_refs[l0:l0 + len(r.lands)],
                          _SemList(send_sems.at[c0:c0 + r.n_copies]), _SemList(recv_sems.at[c0:c0 + r.n_copies]))
            starts += part[0]
            waits += part[1]
            s0, l0, c0 = s0 + len(r.srcs), l0 + len(r.lands), c0 + r.n_copies
        return starts, waits

    return _Rider([], srcs, lands, sum(r.n_copies for r in riders), make)


def _split_start(name, rider, sibling_barrier=None):
    assert not rider.inplace
    ns, nl, n = len(rider.srcs), len(rider.lands), rider.n_copies

    def body(*refs):
        if sibling_barrier is not None:
            x, y, c = _mesh_pos()
            barrier = pltpu.get_barrier_semaphore()
            pl.semaphore_signal(barrier, inc=1, device_id=(x, y, 1 - c), device_id_type=MESH)
            pl.semaphore_wait(barrier, 1)
        srcs, lands = refs[:ns], refs[ns:ns + nl]
        sems = refs[ns + nl:ns + nl + 2 * n]
        token = refs[-1]
        starts, _ = rider.make([], srcs, lands, _SemList(sems[:n]), _SemList(sems[n:]))
        for cp in starts:
            cp.start()
        token[...] = jnp.zeros_like(token)

    buffers = [pltpu.with_memory_space_constraint(a, pltpu.HBM) for a in rider.srcs]
    buffers += [pltpu.with_memory_space_constraint(lax.empty(s.shape, s.dtype), pltpu.HBM) for s in rider.lands]
    hbm = pl.BlockSpec(memory_space=pltpu.HBM)
    sem = pl.BlockSpec(memory_space=pltpu.SEMAPHORE)
    outs = pl.pallas_call(
        body, name=name,
        out_shape=tuple([pltpu.SemaphoreType.DMA(())] * (2 * n) + [pltpu.HBM(b.shape, b.dtype) for b in buffers]
                        + [jax.ShapeDtypeStruct((8, 128), F32)]),
        in_specs=[hbm] * (ns + nl),
        out_specs=tuple([sem] * (2 * n) + [hbm] * (ns + nl) + [_whole()]),
        input_output_aliases={i: 2 * n + i for i in range(ns + nl)},
        compiler_params=pltpu.CompilerParams(has_side_effects=pltpu.SideEffectType.DATAFLOW_SIDE_EFFECTING,
                                             collective_id=sibling_barrier),
    )(*buffers)
    return (rider, outs[:2 * n], outs[2 * n:2 * n + ns + nl]), outs[-1]


def _split_parts(state, riders):
    merged, sems, buffers = state
    n, ns = merged.n_copies, len(merged.srcs)
    parts, s0, l0, c0 = [], 0, 0, 0
    for r in riders:
        parts.append((r, list(sems[c0:c0 + r.n_copies]) + list(sems[n + c0:n + c0 + r.n_copies]),
                      list(buffers[s0:s0 + len(r.srcs)]) + list(buffers[ns + l0:ns + l0 + len(r.lands)])))
        s0, l0, c0 = s0 + len(r.srcs), l0 + len(r.lands), c0 + r.n_copies
    return parts


def _split_wait(name, state, after):
    rider, sems, buffers = state
    ns, nl, n = len(rider.srcs), len(rider.lands), rider.n_copies

    def body(*refs):
        srcs, lands = refs[:ns], refs[ns:ns + nl]
        sem_refs = refs[ns + nl:ns + nl + 2 * n]
        _, waits = rider.make([], srcs, lands, _SemList(sem_refs[:n]), _SemList(sem_refs[n:]))
        for wait in waits:
            wait()

    hbm = pl.BlockSpec(memory_space=pltpu.HBM)
    sem = pl.BlockSpec(memory_space=pltpu.SEMAPHORE)
    outs = pl.pallas_call(
        body, name=name,
        out_shape=tuple(pltpu.HBM(b.shape, b.dtype) for b in buffers),
        in_specs=[hbm] * (ns + nl) + [sem] * (2 * n) + [HBM_SPEC],
        out_specs=tuple([hbm] * (ns + nl)),
        input_output_aliases={i: i for i in range(ns + nl)},
        compiler_params=pltpu.CompilerParams(has_side_effects=pltpu.SideEffectType.DATAFLOW_SIDE_EFFECTING),
    )(*buffers, *sems, after)
    return list(outs[:ns]), list(outs[ns:])


def _pair_sum(pos, ws, g32s, recvs):
    n = len(ws)

    def body(pos_ref, *refs):
        del pos_ref
        g_refs, r_refs = refs[:n], refs[n:2 * n]
        p32_refs, p16_refs = refs[2 * n:3 * n], refs[3 * n:]
        x, y, _ = _mesh_pos()
        for i in range(n):
            tot = g_refs[i][...] + r_refs[i][...].astype(F32)
            p16_refs[i][...] = tot.astype(BF16)

            @pl.when(pl.program_id(0) == 2 * x + y)
            def _(i=i, tot=tot):
                p32_refs[i][...] = tot

    halves = [(SHARD_SHAPES[w][0] // 2, SHARD_SHAPES[w][1]) for w in ws]
    own = [pl.BlockSpec((None, None) + h, lambda j, pos_ref: (j, pos_ref[0], 0, 0)) for h in halves]
    blk = [pl.BlockSpec((None,) + h, lambda j, pos_ref: (j, 0, 0)) for h in halves]
    mine = [pl.BlockSpec(h, lambda j, pos_ref: (0, 0)) for h in halves]
    g4 = [g.reshape((N_SHARD, 2) + h) for g, h in zip(g32s, halves)]
    outs = pl.pallas_call(
        body, name="pair_sum_" + "_".join(str(w) for w in ws),
        grid_spec=pltpu.PrefetchScalarGridSpec(
            num_scalar_prefetch=1, grid=(N_SHARD,), in_specs=own + blk, out_specs=mine + blk),
        out_shape=[jax.ShapeDtypeStruct(h, F32) for h in halves]
        + [jax.ShapeDtypeStruct((N_SHARD,) + h, BF16) for h in halves],
        compiler_params=_params(("arbitrary",)),
    )(pos, *g4, *recvs)
    return outs[:n], outs[n:]


def _chip_sum(p32s, recvs):
    parts = 2

    def body(*refs):
        p_refs, r_refs, f_refs = refs[:N_BIG], refs[N_BIG:2 * N_BIG], refs[2 * N_BIG:]
        for w in range(N_BIG):
            f_refs[w][...] = ((p_refs[w][...] + r_refs[w][0].astype(F32)) + r_refs[w][1].astype(F32)) \
                + r_refs[w][2].astype(F32)

    quarters = [(r // 2 // parts, cc) for r, cc in SHARD_SHAPES]
    own = [pl.BlockSpec(qt, lambda i: (i, 0)) for qt in quarters]
    rcv = [pl.BlockSpec((3,) + qt, lambda i: (0, i, 0)) for qt in quarters]
    out = [pl.BlockSpec(qt, lambda i: (i, 0)) for qt in quarters]
    return pl.pallas_call(
        body, name="chip_sum", grid=(parts,), in_specs=own + rcv, out_specs=out,
        out_shape=[jax.ShapeDtypeStruct((r // 2, cc), F32) for r, cc in SHARD_SHAPES],
        compiler_params=_params(("arbitrary",)),
    )(*p32s, *recvs)


def _adamw(w, g, m, v):
    m_new = ADAM_B1 * m + (1.0 - ADAM_B1) * g
    v_new = ADAM_B2 * v + (1.0 - ADAM_B2) * (g * g)
    m_hat = m_new / (1.0 - ADAM_B1 ** ADAM_STEP)
    v_hat = v_new / (1.0 - ADAM_B2 ** ADAM_STEP)
    delta = -ADAM_LR * (m_hat / (jnp.sqrt(v_hat) + ADAM_EPS) + ADAM_WD * w)
    return delta, m_new, v_new


def _adam_half(name, pos, grads, ws, ms, vs, into=None):
    nb = 4
    which = (lambda ref: ref[0]) if into is None else (lambda ref: 1 - ref[0])

    def body(which_ref, *refs):
        del which_ref
        groups = [refs[i * N_BIG:(i + 1) * N_BIG] for i in range(4)]
        g_refs, w_refs, m_refs, v_refs = groups
        go_refs, do_refs, mo_refs, vo_refs = [refs[len(refs) - (4 - i) * N_BIG:len(refs) - (3 - i) * N_BIG]
                                              for i in range(4)]
        for w in range(N_BIG):
            g = g_refs[w][...]
            delta, m_new, v_new = _adamw(w_refs[w][...], g, m_refs[w][...], v_refs[w][...])
            go_refs[w][...] = g
            do_refs[w][...] = delta
            mo_refs[w][...] = m_new
            vo_refs[w][...] = v_new

    blocks = [(r // 2 // nb, cc) for r, cc in SHARD_SHAPES]
    half = [pl.BlockSpec(b, lambda i, which_ref: (i, 0)) for b in blocks]
    full = [pl.BlockSpec((None,) + b, lambda i, which_ref: (0, which(which_ref) * nb + i, 0)) for b in blocks]
    shapes = [jax.ShapeDtypeStruct((1,) + shp, F32) for shp in SHARD_SHAPES]
    carried = [] if into is None else [a for kind in into for a in kind]
    first = 1 + 4 * N_BIG
    outs = pl.pallas_call(
        body, name=name,
        grid_spec=pltpu.PrefetchScalarGridSpec(
            num_scalar_prefetch=1, grid=(nb,), in_specs=half + full * 3 + [HBM_SPEC] * len(carried),
            out_specs=full * 4),
        out_shape=shapes * 4,
        input_output_aliases={first + i: i for i in range(len(carried))},
        compiler_params=_params(("arbitrary",)),
    )(pos, *grads, *ws, *ms, *vs, *carried)
    return [outs[i * N_BIG:(i + 1) * N_BIG] for i in range(4)]


SMALL_ROWS = 8
ROW_CONV_B, ROW_POOL_SCALE, ROW_LN1_G, ROW_LN1_B, ROW_LN2_G, ROW_LN2_B, ROW_LOSS = range(7)
SMALL_VECS = ((ROW_CONV_B, D_FF), (ROW_POOL_SCALE, POOL_W), (ROW_LN1_G, D_MODEL), (ROW_LN1_B, D_MODEL),
              (ROW_LN2_G, D_MODEL), (ROW_LN2_B, D_MODEL))


def _small_pack(loss, vec_grads):
    def body(*refs):
        loss_ref, gvec, out_ref = refs[0], refs[1:-1], refs[-1]
        out_ref[...] = jnp.zeros_like(out_ref)
        for (row, n), ref in zip(SMALL_VECS, gvec):
            out_ref[row:row + 1, 0:n] = ref[...]
        out_ref[ROW_LOSS:ROW_LOSS + 1, 0:HEAD_DIM] = jnp.broadcast_to(loss_ref[...], (1, HEAD_DIM))

    return pl.pallas_call(
        body, name="small_pack", in_specs=[_whole()] * (1 + len(vec_grads)), out_specs=_whole(),
        out_shape=jax.ShapeDtypeStruct((SMALL_ROWS, D_FF), F32),
    )(loss, *vec_grads)


def _small_pair_sum(own, sibling):
    n = len(own)

    def body(*refs):
        x, y, _ = _mesh_pos()
        for i in range(n):
            refs[2 * n + i][2 * x + y] = refs[i][...] + refs[n + i][...]

    return pl.pallas_call(
        body, name="small_pair_sum", in_specs=[_whole()] * (2 * n), out_specs=[_whole()] * n,
        out_shape=[jax.ShapeDtypeStruct((N_SHARD,) + a.shape, F32) for a in own],
        compiler_params=pltpu.CompilerParams(vmem_limit_bytes=VMEM_LIMIT),
    )(*own, *sibling)


def _small_chip_rider(gathered):
    n = len(gathered)

    def make(inplace, srcs, lands, send_sems, recv_sems):
        del inplace, lands
        x, y, c = _mesh_pos()
        j0 = 2 * x + y
        starts, waits = [], []
        for i in range(n):
            for k, chip in enumerate(_other_chips(x, y)):
                sems = dict(send_sem=send_sems.at[3 * i + k], recv_sem=recv_sems.at[3 * i + k],
                            device_id=(chip[0], chip[1], c), device_id_type=MESH)
                send = pltpu.make_async_remote_copy(src_ref=srcs[i].at[j0], dst_ref=srcs[i].at[j0], **sems)
                arrival = pltpu.make_async_remote_copy(
                    src_ref=srcs[i].at[j0], dst_ref=srcs[i].at[2 * chip[0] + chip[1]], **sems)
                starts.append(send)
                waits += [arrival.wait_recv, send.wait_send]
        return starts, waits

    return _Rider([], gathered, [], 3 * n, make)


def _small_adam(all_a, all_b, all_c, wp, cwp, vec_ws, m_wp, m_cwp, vec_ms, v_wp, v_cwp, vec_vs):
    nv = len(SMALL_VECS)
    np_ = 2 + nv

    def body(*refs):
        all_a_ref, all_b_ref, all_c_ref = refs[0:3]
        w_all, m_all, v_all = (refs[3 + i * np_:3 + (i + 1) * np_] for i in range(3))
        loss_out = refs[3 + 3 * np_]
        outs = refs[4 + 3 * np_:]
        x, y, _ = _mesh_pos()
        j0 = 2 * x + y
        tot_a = ((all_a_ref[0] + all_a_ref[1]) + all_a_ref[2]) + all_a_ref[3]
        tot_b = ((all_b_ref[0] + all_b_ref[1]) + all_b_ref[2]) + all_b_ref[3]
        tot_c = ((all_c_ref[0, j0] + all_c_ref[1, j0]) + all_c_ref[2, j0]) + all_c_ref[3, j0]
        loss_out[...] = tot_b[ROW_LOSS:ROW_LOSS + 1, 0:1]
        grads = [tot_a, tot_c] + [tot_b[row:row + 1, 0:n] for row, n in SMALL_VECS]
        for p in range(np_):
            for at, g in ([(j, tot_c[j:j + 1]) for j in range(3)] if p == 1 else [(Ellipsis, grads[p])]):
                delta, m_new, v_new = _adamw(w_all[p][at], g, m_all[p][at], v_all[p][at])
                outs[p][at] = g
                outs[np_ + p][at] = delta
                outs[2 * np_ + p][at] = m_new
                outs[3 * np_ + p][at] = v_new

    pshapes = [wp.shape, CW_SHARD] + [wv.shape for wv in vec_ws]
    out_shape = [jax.ShapeDtypeStruct((1, 1), F32)] + [jax.ShapeDtypeStruct(s, F32) for s in pshapes] * 4
    outs = pl.pallas_call(
        body, name="small_adam",
        in_specs=[_whole()] * (3 + 3 * np_), out_specs=[_whole()] * len(out_shape), out_shape=out_shape,
        compiler_params=pltpu.CompilerParams(vmem_limit_bytes=VMEM_LIMIT),
    )(all_a, all_b, all_c, wp, cwp, *vec_ws, m_wp, m_cwp, *vec_ms, v_wp, v_cwp, *vec_vs)
    return outs[0], [outs[1 + i * np_:1 + (i + 1) * np_] for i in range(4)]


def kernel(x, w_in, w_pool, pool_scale, w_out, ln1_g, ln1_b, w_up, conv_w, conv_b, w_down, ln2_g, ln2_b, loss_target, m_w_in, m_w_pool, m_pool_scale, m_w_out, m_ln1_g, m_ln1_b, m_w_up, m_conv_w, m_conv_b, m_w_down, m_ln2_g, m_ln2_b, v_w_in, v_w_pool, v_pool_scale, v_w_out, v_ln1_g, v_ln1_b, v_w_up, v_conv_w, v_conv_b, v_w_down, v_ln2_g, v_ln2_b):
    pos = lax.axis_index("c").astype(jnp.int32).reshape(1)
    order = ("w_in", "w_out", "w_up", "w_down")
    w_in_i, w_out_i, w_up_i, w_down_i = range(N_BIG)
    vec_names = ("conv_b", "pool_scale", "ln1_g", "ln1_b", "ln2_g", "ln2_b")

    taps_first = lambda a: jnp.transpose(a, (1, 0, 2))
    gathered = _gather_weights([w_in[0], w_out[0], w_up[0], w_down[0]], taps_first(conv_w), w_pool[0], (w_in_i,))
    cw_full = jnp.transpose(gathered[N_BIG].reshape(N_SHARD, 3, DOWN_SH), (1, 0, 2)).reshape(3, D_FF)
    up_a, up_b, up_c = (0, 176), (176, 176), (352, 160)
    assert up_c[0] + up_c[1] == SHARD_SHAPES[w_up_i][0] // 2

    class MeshComm:
        def __init__(self):
            self.w = {i: gathered[i] for i in range(N_BIG)}
            self.g32, self.g16, self.p32, self.p16, self.recv_b = {}, {}, {}, {}, {}
            self.up_complete = False
            self.tokens, self.chips = {}, []

        def weight(self, name):
            i = order.index(name)
            if name == "w_up" and not self.up_complete:
                (arrs, _), = _comm_only("gather_up_last", [_gather_rider(
                    {i: self.w[i]}, [("d2d_diag", i, up_b), ("d2d", i, up_c)])])
                self.w[i], self.up_complete = arrs[0], True
            full = self.w[i]
            return full.reshape(-1, full.shape[-1]) if name in ("w_out", "w_down") else full

        def _gather(self, ws, ops):
            return _gather_rider({w: self.w[w] for w in ws}, ops), ("w", ws)

        def _pair(self, ws):
            return _pair_rider(ws, [self.g16[w] for w in ws]), ("recv_a", ws)

        def _chip(self, ws):
            return _chip_rider(ws, [self.p16[w] for w in ws]), ("recv_b", ws)

        def plan(self, call):
            out_all, down_all = _whole_half(w_out_i), _whole_half(w_down_i)
            if call == "proj_pool":
                return [self._gather([w_out_i, w_up_i, w_down_i],
                                     [("ici", w_out_i, out_all), ("nbr", w_down_i, down_all),
                                      ("nbr", w_up_i, up_a)])]
            if call == "retention_fwd":
                return [self._gather([w_out_i, w_up_i, w_down_i],
                                     [("d2d", w_out_i, out_all),
                                      ("relay", w_down_i, down_all), ("d2d_nbr", w_down_i, down_all),
                                      ("relay", w_up_i, up_a), ("d2d_nbr", w_up_i, up_a), ("nbr", w_up_i, up_b)])]
            if call == "outproj_ln1":
                return [self._gather([w_up_i, w_down_i],
                                     [("d2d_diag", w_down_i, down_all), ("d2d_diag", w_up_i, up_a),
                                      ("relay", w_up_i, up_b), ("d2d_nbr", w_up_i, up_b), ("ici", w_up_i, up_c)])]
            return []

        def after(self, call):
            return tuple(self.tokens.pop(call, ()))

        def riders(self, call):
            self.pending = self.plan(call)
            return [r for r, _ in self.pending]

        def _start(self, name, rider, before, sibling_barrier=None):
            state, token = _split_start(name, rider, sibling_barrier)
            self.tokens.setdefault(before, []).append(token)
            return state

        def _finish_pair(self, name, state, ws, after):
            _, lands = _split_wait(name, state, after)
            self._finish_sum(ws, lands)

        def landed(self, call, results, outs):
            for (_, (slot, ws)), (inplace, lands) in zip(self.pending, results):
                for w, arr in zip(ws, inplace if len(inplace) else lands):
                    getattr(self, slot)[w] = arr
            if call == "wgrad_out":
                self._finish_pair("pair_exchange_up_wait", self.pair_up, [w_up_i], outs[1])
                self.chips.append(([w_up_i], self._start(
                    "chip_exchange_up_start", self._chip([w_up_i])[0], "wgrad_down")))
            if call == "mix_bwd":
                ws = [w_out_i, w_down_i]
                self._finish_pair("pair_exchange_out_down_wait", self.pair_out_down, ws, outs[0])
            if call == "retention_bwd":
                own, sibling = _split_wait("small_pair_wait", self.small_pair, outs[0])
                self.small_chip = self._start(
                    "small_chip_start", _small_chip_rider(_small_pair_sum(own, sibling)), "wgrad_in")

        def small_gradients(self, loss, small):
            dcw4 = jnp.transpose(small["conv_w"].reshape(3, N_SHARD, DOWN_SH), (1, 0, 2))
            own = [small["w_pool"], _small_pack(loss, [small[n] for n in vec_names]), dcw4]
            ws = [w_out_i, w_down_i]
            parts = [self._chip(ws)[0], _final_rider(own)]
            chip, self.small_pair = _split_parts(
                self._start("chip_out_down_small_pair_start", _merged_rider(parts), "retention_bwd"), parts)
            self.chips.append((ws, chip))

        def gradient(self, name, g32, g16):
            w = order.index(name)
            shape = (N_SHARD,) + SHARD_SHAPES[w]
            self.g32[w], self.g16[w] = g32.reshape(shape), g16.reshape(shape)
            if name == "w_up":
                self.pair_up = self._start("pair_exchange_up_start", self._pair([w])[0], "wgrad_out", 1)
            if name == "w_down":
                self.pair_out_down = self._start("pair_exchange_out_down_start",
                                                 self._pair([w_out_i, w_down_i])[0], "mix_bwd", 2)

        def wgrad_in(self, xb, dproj):
            w = w_in_i
            g32, landed = _wgrad_send(xb, dproj, IN_SH, "wgrad_in", 4, after=self.after("wgrad_in"))
            self.g32[w] = g32
            self._finish_sum([w], [landed])
            self.chips.append(([w], self._start("chip_exchange_in_start", self._chip([w])[0], "dx")))

        def _finish_sum(self, ws, lands):
            p32s, p16s = _pair_sum(pos, ws, [self.g32[w] for w in ws], lands)
            for w, p32, p16 in zip(ws, p32s, p16s):
                self.p32[w], self.p16[w] = p32, p16

        def finish(self, after):
            for n, (ws, state) in enumerate(self.chips):
                _, lands = _split_wait("chip_exchange_wait_%d" % n, state, after)
                for w, arr in zip(ws, lands):
                    self.recv_b[w] = arr
            return _split_wait("small_chip_wait", self.small_chip, after)[0]

    comm = MeshComm()
    loss, grad_x, small = _local_step(x[0], loss_target[0], cw_full, conv_b, gathered[N_BIG + 1], pool_scale,
                                      ln1_g, ln1_b, ln2_g, ln2_b, comm)

    given = dict(w_pool=w_pool, pool_scale=pool_scale, ln1_g=ln1_g, ln1_b=ln1_b, conv_w=conv_w, conv_b=conv_b,
                 ln2_g=ln2_g, ln2_b=ln2_b)
    given_m = dict(w_pool=m_w_pool, pool_scale=m_pool_scale, ln1_g=m_ln1_g, ln1_b=m_ln1_b, conv_w=m_conv_w,
                   conv_b=m_conv_b, ln2_g=m_ln2_g, ln2_b=m_ln2_b)
    given_v = dict(w_pool=v_w_pool, pool_scale=v_pool_scale, ln1_g=v_ln1_g, ln1_b=v_ln1_b, conv_w=v_conv_w,
                   conv_b=v_conv_b, ln2_g=v_ln2_g, ln2_b=v_ln2_b)
    args = []
    for src in (given, given_m, given_v):
        args += [src["w_pool"][0], taps_first(src["conv_w"]), [src[n] for n in vec_names]]
    small_sums = comm.finish(grad_x)
    loss_tot, small_out = _small_adam(*small_sums, *args)
    every = range(N_BIG)
    mine = _chip_sum([comm.p32[w] for w in every], [comm.recv_b[w] for w in every])
    final_state, _ = _split_start("pair_exchange_f32_start", _final_rider(mine), 3)
    mine = final_state[2][:N_BIG]
    big = ([w_in, w_out, w_up, w_down], [m_w_in, m_w_out, m_w_up, m_w_down], [v_w_in, v_w_out, v_w_up, v_w_down])
    own_half = _adam_half("adam_own_half", pos, mine, *big)
    _, theirs = _split_wait("pair_exchange_f32_wait", final_state, own_half[0][0])
    big_out = _adam_half("adam_other_half", pos, theirs, *big, into=own_half)

    names = ("w_in", "w_pool", "pool_scale", "w_out", "ln1_g", "ln1_b", "w_up", "conv_w", "conv_b", "w_down",
             "ln2_g", "ln2_b")
    small_names = ("w_pool", "conv_w") + vec_names
    result = [loss_tot.reshape(()), grad_x[None]]
    for kind in range(4):
        for n in names:
            if n in order:
                result.append(big_out[kind][order.index(n)])
            else:
                val = small_out[kind][small_names.index(n)]
                if n == "conv_w":
                    val = taps_first(val)
                elif n == "w_pool":
                    val = val[None]
                result.append(val)
    return tuple(result)
```

```python
import functools

import numpy as np
import jax
import jax.numpy as jnp
from jax import lax
from jax.experimental import pallas as pl
from jax.experimental.pallas import tpu as pltpu

F32 = jnp.float32
BF16 = jnp.bfloat16

D_MODEL = 1024
HEADS = 4
HEAD_DIM = 128
RET_W = HEADS * HEAD_DIM
POOL_WINDOWS = (2, 4, 8, 16)
POOL_W = 512
IN_W = 4 * RET_W + POOL_W
D_FF = 2816
N_SHARD = 4
IN_SH = IN_W // N_SHARD
UP_SH = 2 * D_FF // N_SHARD
DOWN_SH = D_FF // N_SHARD
OUT_SH = D_MODEL // N_SHARD
ROPE_BASE = 10000.0
LN_EPS = 1e-5
RMS_EPS = 1e-6
ALPHA = 2.0 ** 0.25
K_SCALE = HEAD_DIM ** -0.5
SUPER = 256
CHUNK = 64
POOL_HALO = 16
CONV_HALO = 8
FFN_STRIP = 128
LN_ROWS = 32

ADAM_LR = 0.001
ADAM_B1 = 0.9
ADAM_B2 = 0.999
ADAM_EPS = 1e-08
ADAM_WD = 0.01
ADAM_STEP = 10

MESH = pl.DeviceIdType.MESH
VMEM_LIMIT = 56 * 1024 * 1024


def _dot(a, b):
    return jnp.dot(a, b, preferred_element_type=F32)


def _dot_nt(a, b):
    return lax.dot_general(a, b, (((1,), (1,)), ((), ())), preferred_element_type=F32)


def _dot_tn(a, b):
    return lax.dot_general(a, b, (((0,), (0,)), ((), ())), preferred_element_type=F32)


def _sigmoid(x):
    return 1.0 / (1.0 + jnp.exp(-x))


def _params(sem):
    return pltpu.CompilerParams(dimension_semantics=sem, vmem_limit_bytes=VMEM_LIMIT)


def _whole():
    return pl.BlockSpec(memory_space=pltpu.VMEM)


HBM_SPEC = pl.BlockSpec(memory_space=pl.ANY)


class _Rider:
    def __init__(self, inplace, srcs, lands, n_copies, make):
        self.inplace, self.srcs, self.lands, self.n_copies, self.make = list(inplace), list(srcs), list(lands), n_copies, make


def _call(body, *, name, grid, in_specs, out_specs, out_shape, operands, scratch_shapes=(), sem=(),
          aliases=None, riders=(), after=()):
    n_in, n_out, n_scr = len(in_specs), len(out_shape), len(scratch_shapes)
    in_specs, out_specs, out_shape = list(in_specs), list(out_specs), list(out_shape)
    operands, scratch_shapes, aliases = list(operands), list(scratch_shapes), dict(aliases or {})
    in_specs += [_whole()] * len(after)
    operands += list(after)
    for r in riders:
        for a in r.inplace:
            aliases[len(in_specs)] = len(out_shape)
            in_specs.append(HBM_SPEC)
            operands.append(a)
            out_specs.append(HBM_SPEC)
            out_shape.append(jax.ShapeDtypeStruct(a.shape, a.dtype))
        for a in r.srcs:
            in_specs.append(HBM_SPEC)
            operands.append(a)
        for shp in r.lands:
            out_specs.append(HBM_SPEC)
            out_shape.append(shp)
        scratch_shapes += [pltpu.SemaphoreType.DMA((r.n_copies,)), pltpu.SemaphoreType.DMA((r.n_copies,))]

    def full(*refs):
        ins = refs[:n_in]
        at = n_in + len(after)
        r_srcs = []
        for r in riders:
            at += len(r.inplace)
            r_srcs.append(refs[at:at + len(r.srcs)])
            at += len(r.srcs)
        outs = refs[at:at + n_out]
        at += n_out
        r_outs = []
        for r in riders:
            r_outs.append((refs[at:at + len(r.inplace)], refs[at + len(r.inplace):at + len(r.inplace) + len(r.lands)]))
            at += len(r.inplace) + len(r.lands)
        scr = refs[at:at + n_scr]
        at += n_scr
        r_sems = [refs[at + 2 * i:at + 2 * i + 2] for i in range(len(riders))]

        def copies():
            return [r.make(r_outs[i][0], r_srcs[i], r_outs[i][1], r_sems[i][0], r_sems[i][1])
                    for i, r in enumerate(riders)]

        def start():
            for starts, _ in copies():
                for cp in starts:
                    cp.start()

        def finish():
            for _, waits in copies():
                for wait in waits:
                    wait()

        if riders and grid:
            first = functools.reduce(jnp.logical_and, [pl.program_id(d) == 0 for d in range(len(grid))])
            last = functools.reduce(jnp.logical_and, [pl.program_id(d) == grid[d] - 1 for d in range(len(grid))])
            pl.when(first)(start)
            body(*ins, *outs, *scr)
            pl.when(last)(finish)
        else:
            if riders:
                start()
            body(*ins, *outs, *scr)
            if riders:
                finish()

    params = _params(sem) if grid else pltpu.CompilerParams(vmem_limit_bytes=VMEM_LIMIT)
    res = pl.pallas_call(
        full, name=name, grid=grid, in_specs=in_specs, out_specs=out_specs, out_shape=out_shape,
        scratch_shapes=scratch_shapes, input_output_aliases=aliases, compiler_params=params,
    )(*operands)
    outs, at, rider_res = res[:n_out], n_out, []
    for r in riders:
        rider_res.append((res[at:at + len(r.inplace)], res[at + len(r.inplace):at + len(r.inplace) + len(r.lands)]))
        at += len(r.inplace) + len(r.lands)
    return list(outs), rider_res


def _gammas():
    return [1.0 - 2.0 ** (-5.0 - h) for h in range(HEADS)]


def _decay_tables():
    idx = np.arange(SUPER)
    dist = np.abs(idx[:, None] - idx[None, :]).astype(np.float64)
    visible = (idx[None, :] // CHUNK) <= (idx[:, None] // CHUNK)
    mask = np.stack([np.where(visible, g ** dist, 0.0) for g in _gammas()])
    qd = np.concatenate([np.repeat((g ** (idx + 1.0))[:, None], HEAD_DIM, 1) for g in _gammas()], 1)
    kd = np.concatenate([np.repeat((g ** (SUPER - 1.0 - idx))[:, None], HEAD_DIM, 1) for g in _gammas()], 1)
    return (jnp.asarray(mask, F32), jnp.asarray(qd, F32), jnp.asarray(kd, F32))


def _rope_tables(s):
    inv_freq = ROPE_BASE ** (-np.arange(0, HEAD_DIM, 2, dtype=np.float64) / HEAD_DIM)
    ang = np.arange(s, dtype=np.float64)[:, None] * inv_freq[None, :]
    cos, sin = np.cos(ang), np.sin(ang)
    return (jnp.asarray(np.concatenate([cos, cos], 1), F32),
            jnp.asarray(np.concatenate([-sin, sin], 1), F32))


def _rope(t, cosf, sinf):
    return t * cosf + pltpu.roll(t, HEAD_DIM // 2, 1) * sinf


def _rope_t(t, cosf, sinf):
    return t * cosf - pltpu.roll(t, HEAD_DIM // 2, 1) * sinf


def _layernorm_fwd(z):
    mu = jnp.mean(z, axis=-1, keepdims=True)
    zc = z - mu
    var = jnp.mean(zc * zc, axis=-1, keepdims=True)
    rstd = lax.rsqrt(var + LN_EPS)
    return zc * rstd, rstd


def _layernorm_bwd(dy, xhat, rstd, gain):
    dxh = dy * gain
    m1 = jnp.mean(dxh, axis=-1, keepdims=True)
    m2 = jnp.mean(dxh * xhat, axis=-1, keepdims=True)
    return rstd * (dxh - m1 - xhat * m2)


def _proj_pool(x, win4, cosf, sinf, wpool, pscale, ts, riders=(), after=()):
    s = x.shape[0]
    nt = s // ts

    def body(x_ref, w_ref, cos_ref, sin_ref, wp_ref, ps_ref,
             xb_ref, q_ref, k_ref, v_ref, g_ref, pooled_ref, cat_ref, proj_scr, pext_scr):
        i = pl.program_id(0)
        xb = x_ref[...].astype(BF16)
        xb_ref[...] = xb
        for j in range(N_SHARD):
            proj_scr[:, j * IN_SH:(j + 1) * IN_SH] = _dot(xb, w_ref[j])
        cosf_t = cos_ref[...]
        sinf_t = sin_ref[...]
        for h in range(HEADS):
            lo = h * HEAD_DIM
            q_ref[:, lo:lo + HEAD_DIM] = _rope(proj_scr[:, lo:lo + HEAD_DIM], cosf_t, sinf_t).astype(BF16)
            kk = _rope(proj_scr[:, RET_W + lo:RET_W + lo + HEAD_DIM], cosf_t, sinf_t) * K_SCALE
            k_ref[:, lo:lo + HEAD_DIM] = kk.astype(BF16)
        v_ref[...] = proj_scr[:, 2 * RET_W:3 * RET_W].astype(BF16)
        g_ref[...] = proj_scr[:, 3 * RET_W:4 * RET_W]

        @pl.when(i == 0)
        def _():
            pext_scr[0:POOL_HALO, :] = jnp.zeros((POOL_HALO, POOL_W), F32)

        pext_scr[POOL_HALO:POOL_HALO + ts, :] = proj_scr[:, 4 * RET_W:IN_W]
        pos = (i * ts + lax.broadcasted_iota(jnp.int32, (ts, 1), 0) + 1).astype(F32)
        for gi, w in enumerate(POOL_WINDOWS):
            lo = gi * HEAD_DIM
            ext = pext_scr[:, lo:lo + HEAD_DIM]
            acc = ext
            shift = 1
            while shift < w:
                acc = acc + pltpu.roll(acc, shift, 0)
                shift *= 2
            tok = ext[POOL_HALO:POOL_HALO + ts]
            pooled = acc[POOL_HALO:POOL_HALO + ts] / jnp.minimum(pos, float(w)) - tok
            pooled_b = pooled.astype(BF16)
            pooled_ref[:, lo:lo + HEAD_DIM] = pooled_b
            lin = _dot(pooled_b, wp_ref[gi])
            cat_ref[:, lo:lo + HEAD_DIM] = (lin * ps_ref[:, lo:lo + HEAD_DIM]).astype(BF16)
        pext_scr[0:POOL_HALO, :] = pext_scr[ts:ts + POOL_HALO, :]

    tile = lambda w: pl.BlockSpec((ts, w), lambda i: (i, 0))
    return _call(
        body, name="proj_pool", grid=(nt,),
        in_specs=[tile(D_MODEL), _whole(), tile(HEAD_DIM), tile(HEAD_DIM), _whole(), _whole()],
        out_specs=[tile(D_MODEL), tile(RET_W), tile(RET_W), tile(RET_W), tile(RET_W), tile(POOL_W),
                   pl.BlockSpec((ts, POOL_W), lambda i: (i, 1))],
        out_shape=[jax.ShapeDtypeStruct((s, D_MODEL), BF16), jax.ShapeDtypeStruct((s, RET_W), BF16),
                   jax.ShapeDtypeStruct((s, RET_W), BF16), jax.ShapeDtypeStruct((s, RET_W), BF16),
                   jax.ShapeDtypeStruct((s, RET_W), F32), jax.ShapeDtypeStruct((s, POOL_W), BF16),
                   jax.ShapeDtypeStruct((s, 2 * RET_W), BF16)],
        scratch_shapes=[pltpu.VMEM((ts, IN_W), F32), pltpu.VMEM((ts + POOL_HALO, POOL_W), F32)],
        sem=("arbitrary",), operands=(x, win4, cosf, sinf, wpool, pscale), riders=riders, after=after,
    )


def _retention_fwd(q, k, v, g, cat, mask, qd, kd, riders=(), after=()):
    s = q.shape[0]
    ns = s // SUPER
    cdec = [gm ** float(SUPER) for gm in _gammas()]

    def body(q_ref, k_ref, v_ref, g_ref, cat_in, mask_ref, qd_ref, kd_ref,
             ret_ref, cat_ref, st_ref, state_scr):
        del cat_in
        n = pl.program_id(0)

        @pl.when(n == 0)
        def _():
            state_scr[...] = jnp.zeros_like(state_scr)

        for h in range(HEADS):
            sl = slice(h * HEAD_DIM, (h + 1) * HEAD_DIM)
            qh, kh, vh = q_ref[:, sl], k_ref[:, sl], v_ref[:, sl]
            sc = _dot_nt(qh, kh) * mask_ref[h]
            st = state_scr[h]
            stb = st.astype(BF16)
            st_ref[0, h] = stb
            qdb = (qh.astype(F32) * qd_ref[:, sl]).astype(BF16)
            kdb = (kh.astype(F32) * kd_ref[:, sl]).astype(BF16)
            ret = _dot(sc.astype(BF16), vh) + _dot(qdb, stb)
            state_scr[h] = st * cdec[h] + _dot_tn(kdb, vh)
            ret_ref[:, sl] = ret
            r = lax.rsqrt(jnp.mean(ret * ret, axis=-1, keepdims=True) + RMS_EPS)
            gh = g_ref[:, sl]
            cat_ref[:, sl] = ((ret * r) * (gh * _sigmoid(gh))).astype(BF16)

    tile = pl.BlockSpec((SUPER, RET_W), lambda n: (n, 0))
    return _call(
        body, name="retention_fwd", grid=(ns,),
        in_specs=[tile, tile, tile, tile, HBM_SPEC, _whole(), _whole(), _whole()],
        out_specs=[tile, tile, pl.BlockSpec((1, HEADS, HEAD_DIM, HEAD_DIM), lambda n: (n, 0, 0, 0))],
        out_shape=[jax.ShapeDtypeStruct((s, RET_W), F32), jax.ShapeDtypeStruct((s, 2 * RET_W), BF16),
                   jax.ShapeDtypeStruct((ns, HEADS, HEAD_DIM, HEAD_DIM), BF16)],
        scratch_shapes=[pltpu.VMEM((HEADS, HEAD_DIM, HEAD_DIM), F32)],
        aliases={4: 1}, sem=("arbitrary",), operands=(q, k, v, g, cat, mask, qd, kd), riders=riders,
        after=after,
    )


def _outproj_ln1(x, cat, wout, g1, b1, ts, riders=(), after=()):
    s = x.shape[0]

    def body(x_ref, cat_ref, w_ref, g_ref, b_ref, xhat_ref, rstd_ref, h1b_ref):
        z = ALPHA * x_ref[...] + _dot(cat_ref[...], w_ref[...])
        xhat, rstd = _layernorm_fwd(z)
        xhat_ref[...] = xhat
        rstd_ref[...] = rstd
        h1b_ref[...] = (xhat * g_ref[...] + b_ref[...]).astype(BF16)

    tile = lambda w: pl.BlockSpec((ts, w), lambda i: (i, 0))
    return _call(
        body, name="outproj_ln1", grid=(s // ts,),
        in_specs=[tile(D_MODEL), tile(D_MODEL), _whole(), _whole(), _whole()],
        out_specs=[tile(D_MODEL), tile(1), tile(D_MODEL)],
        out_shape=[jax.ShapeDtypeStruct((s, D_MODEL), F32), jax.ShapeDtypeStruct((s, 1), F32),
                   jax.ShapeDtypeStruct((s, D_MODEL), BF16)],
        sem=("arbitrary",), operands=(x, cat, wout, g1, b1), riders=riders, after=after,
    )


def _ffn_fwd_loss(xhat1, h1b, target, wup4, wdown, cw, cb, g1, b1, g2, b2, ts):
    s = xhat1.shape[0]

    def body(xhat_ref, h1b_ref, tgt_ref, wup_ref, wdn_ref, cw_ref, cb_ref, g1_ref, b1_ref, g2_ref, b2_ref,
             ub_ref, act_ref, sd_ref, dz2_ref, dz2b_ref, loss_ref, dg2_ref, db2_ref, val_scr, gext_scr, ffn_scr):
        i = pl.program_id(0)

        @pl.when(i == 0)
        def _():
            gext_scr[0:CONV_HALO, :] = jnp.zeros((CONV_HALO, D_FF), F32)
            loss_ref[...] = jnp.zeros_like(loss_ref)
            dg2_ref[...] = jnp.zeros_like(dg2_ref)
            db2_ref[...] = jnp.zeros_like(db2_ref)

        for half in range(2):
            lo = half * UP_SH
            gext_scr[CONV_HALO:CONV_HALO + ts, lo:lo + UP_SH] = _dot(h1b_ref[...], wup_ref[2 + half])
            val_scr[:, lo:lo + UP_SH] = _dot(h1b_ref[...], wup_ref[half])
            for c0 in range(lo, lo + UP_SH, FFN_STRIP):
                cols = slice(c0, c0 + FFN_STRIP)
                ext = gext_scr[:, cols]
                gate = ext[CONV_HALO:]
                hc = cb_ref[:, cols] + ((pltpu.roll(ext, 2, 0)[CONV_HALO:] * cw_ref[0:1, cols]
                                         + pltpu.roll(ext, 1, 0)[CONV_HALO:] * cw_ref[1:2, cols])
                                        + gate * cw_ref[2:3, cols])
                val = val_scr[:, cols]
                sg = _sigmoid(hc)
                si = hc * sg
                act_ref[:, cols] = (si * val).astype(BF16)
                ub_ref[:, cols] = val.astype(BF16)
                ub_ref[:, D_FF + c0:D_FF + c0 + FFN_STRIP] = gate.astype(BF16)
                sd_ref[:, cols] = hc.astype(BF16)
            part = _dot(act_ref[:, lo:lo + UP_SH], wdn_ref[lo:lo + UP_SH, :])
            if half == 0:
                ffn_scr[...] = part
            else:
                ffn_scr[...] += part

        gext_scr[0:CONV_HALO, :] = gext_scr[ts:ts + CONV_HALO, :]

        loss_acc = jnp.zeros((1, 1), F32)
        dg2_acc = jnp.zeros((1, D_MODEL), F32)
        db2_acc = jnp.zeros((1, D_MODEL), F32)
        for r0 in range(0, ts, LN_ROWS):
            rows = slice(r0, r0 + LN_ROWS)
            h1 = xhat_ref[rows, :] * g1_ref[...] + b1_ref[...]
            xhat2, rstd2 = _layernorm_fwd(ALPHA * h1 + ffn_scr[rows, :])
            diff = (xhat2 * g2_ref[...] + b2_ref[...]) - tgt_ref[rows, :]
            row = jnp.mean(diff * diff, axis=-1, keepdims=True)
            loss_acc = loss_acc + 0.5 * jnp.sum(row, axis=0, keepdims=True)
            dy = diff * (1.0 / D_MODEL)
            dg2_acc = dg2_acc + jnp.sum(dy * xhat2, axis=0, keepdims=True)
            db2_acc = db2_acc + jnp.sum(dy, axis=0, keepdims=True)
            dz2 = _layernorm_bwd(dy, xhat2, rstd2, g2_ref[...])
            dz2_ref[rows, :] = dz2
            dz2b_ref[rows, :] = dz2.astype(BF16)
        loss_ref[...] += loss_acc
        dg2_ref[...] += dg2_acc
        db2_ref[...] += db2_acc

    tile = lambda w: pl.BlockSpec((ts, w), lambda i: (i, 0))
    acc = lambda w: pl.BlockSpec((1, w), lambda i: (0, 0))
    return pl.pallas_call(
        body, name="ffn_fwd_loss", grid=(s // ts,),
        in_specs=[tile(D_MODEL), tile(D_MODEL), tile(D_MODEL)] + [_whole()] * 8,
        out_specs=[tile(2 * D_FF), tile(D_FF), tile(D_FF), tile(D_MODEL), tile(D_MODEL),
                   acc(1), acc(D_MODEL), acc(D_MODEL)],
        out_shape=[jax.ShapeDtypeStruct((s, 2 * D_FF), BF16), jax.ShapeDtypeStruct((s, D_FF), BF16),
                   jax.ShapeDtypeStruct((s, D_FF), BF16), jax.ShapeDtypeStruct((s, D_MODEL), F32),
                   jax.ShapeDtypeStruct((s, D_MODEL), BF16),
                   jax.ShapeDtypeStruct((1, 1), F32), jax.ShapeDtypeStruct((1, D_MODEL), F32),
                   jax.ShapeDtypeStruct((1, D_MODEL), F32)],
        scratch_shapes=[pltpu.VMEM((ts, D_FF), F32), pltpu.VMEM((ts + CONV_HALO, D_FF), F32),
                        pltpu.VMEM((ts, D_MODEL), F32)],
        compiler_params=_params(("arbitrary",)),
    )(xhat1, h1b, target, wup4, wdown, cw, cb, g1, b1, g2, b2)


def _ffn_bwd(dz2, dz2b, ub, sd, xhat1, rstd1, wup4, wdown, cw, g1, ts):
    s = dz2.shape[0]
    nt = s // ts

    def body(dz2_ref, dz2b_ref, ub_ref, sd_ref, xhat_ref, rstd_ref, wup_ref, wdn_ref, cw_ref, g1_ref,
             dub_ref, dz1_ref, dz1b_ref, dg1_ref, db1_ref, dcw_ref, dcb_ref, dext_scr, da_scr):
        i = pl.program_id(0)

        @pl.when(i == 0)
        def _():
            dext_scr[ts:ts + CONV_HALO, :] = jnp.zeros((CONV_HALO, D_FF), F32)
            dg1_ref[...] = jnp.zeros_like(dg1_ref)
            db1_ref[...] = jnp.zeros_like(db1_ref)
            dcw_ref[...] = jnp.zeros_like(dcw_ref)
            dcb_ref[...] = jnp.zeros_like(dcb_ref)

        da_scr[...] = _dot_nt(dz2b_ref[...], wdn_ref[...])
        n_ext = ts + CONV_HALO
        for c0 in range(0, D_FF, FFN_STRIP):
            cols = slice(c0, c0 + FFN_STRIP)
            gcols = slice(D_FF + c0, D_FF + c0 + FFN_STRIP)
            val = ub_ref[:, cols].astype(F32)
            gate = ub_ref[:, gcols].astype(F32)
            da = da_scr[:, cols]
            hc = sd_ref[:, cols].astype(F32)
            sg = _sigmoid(hc)
            dhc = da * val * (sg * (1.0 + hc * (1.0 - sg)))
            dext_scr[0:ts, cols] = dhc
            dext = dext_scr[:, cols]
            dhc1 = pltpu.roll(dext, n_ext - 1, 0)[0:ts]
            dhc2 = pltpu.roll(dext, n_ext - 2, 0)[0:ts]
            dcb_ref[:, cols] += jnp.sum(dhc, axis=0, keepdims=True)
            dcw_ref[0:1, cols] += jnp.sum(dhc2 * gate, axis=0, keepdims=True)
            dcw_ref[1:2, cols] += jnp.sum(dhc1 * gate, axis=0, keepdims=True)
            dcw_ref[2:3, cols] += jnp.sum(dhc * gate, axis=0, keepdims=True)
            dgate = dhc * cw_ref[2:3, cols] + dhc1 * cw_ref[1:2, cols] + dhc2 * cw_ref[0:1, cols]
            dub_ref[:, cols] = (da * (hc * sg)).astype(BF16)
            dub_ref[:, gcols] = dgate.astype(BF16)
        dext_scr[ts:n_ext, :] = dext_scr[0:CONV_HALO, :]
        dh1 = ALPHA * dz2_ref[...]
        for j in range(N_SHARD):
            dh1 = dh1 + _dot_nt(dub_ref[:, j * UP_SH:(j + 1) * UP_SH], wup_ref[j])
        xhat = xhat_ref[...]
        dg1_ref[...] += jnp.sum(dh1 * xhat, axis=0, keepdims=True)
        db1_ref[...] += jnp.sum(dh1, axis=0, keepdims=True)
        dz1 = _layernorm_bwd(dh1, xhat, rstd_ref[...], g1_ref[...])
        dz1_ref[...] = dz1
        dz1b_ref[...] = dz1.astype(BF16)

    tile = lambda w: pl.BlockSpec((ts, w), lambda i: (nt - 1 - i, 0))
    acc = lambda rws, w: pl.BlockSpec((rws, w), lambda i: (0, 0))
    return pl.pallas_call(
        body, name="ffn_bwd", grid=(nt,),
        in_specs=[tile(D_MODEL), tile(D_MODEL), tile(2 * D_FF), tile(D_FF), tile(D_MODEL), tile(1)]
        + [_whole()] * 4,
        out_specs=[tile(2 * D_FF), tile(D_MODEL), tile(D_MODEL), acc(1, D_MODEL), acc(1, D_MODEL),
                   acc(3, D_FF), acc(1, D_FF)],
        out_shape=[jax.ShapeDtypeStruct((s, 2 * D_FF), BF16),
                   jax.ShapeDtypeStruct((s, D_MODEL), F32), jax.ShapeDtypeStruct((s, D_MODEL), BF16),
                   jax.ShapeDtypeStruct((1, D_MODEL), F32),
                   jax.ShapeDtypeStruct((1, D_MODEL), F32), jax.ShapeDtypeStruct((3, D_FF), F32),
                   jax.ShapeDtypeStruct((1, D_FF), F32)],
        scratch_shapes=[pltpu.VMEM((ts + CONV_HALO, D_FF), F32), pltpu.VMEM((ts, D_FF), F32)],
        compiler_params=_params(("arbitrary",)),
    )(dz2, dz2b, ub, sd, xhat1, rstd1, wup4, wdown, cw, g1)


def _mix_bwd(dz1, pooled, ret, g, wout, wpool, pscale, ts, riders=(), after=()):
    s = dz1.shape[0]
    nt = s // ts

    def body(dz1_ref, pooled_ref, ret_ref, g_ref, wout_ref, wp_ref, ps_ref,
             dret_ref, dgp_ref, dwp_ref, dps_ref, eext_scr):
        i = pl.program_id(0)
        r = nt - 1 - i

        @pl.when(i == 0)
        def _():
            eext_scr[ts:ts + POOL_HALO, :] = jnp.zeros((POOL_HALO, POOL_W), F32)
            dwp_ref[...] = jnp.zeros_like(dwp_ref)
            dps_ref[...] = jnp.zeros_like(dps_ref)

        dzb = dz1_ref[...].astype(BF16)
        dcat_r = _dot_nt(dzb, wout_ref[0:RET_W, :])
        dcat_p = _dot_nt(dzb, wout_ref[RET_W:2 * RET_W, :])
        pos = (r * ts + lax.broadcasted_iota(jnp.int32, (ts, 1), 0) + 1).astype(F32)
        dpooled = []
        for gi, w in enumerate(POOL_WINDOWS):
            sl = slice(gi * HEAD_DIM, (gi + 1) * HEAD_DIM)
            pb = pooled_ref[:, sl]
            dy = dcat_p[:, sl]
            dps_ref[:, sl] += jnp.sum(dy * _dot(pb, wp_ref[gi]), axis=0, keepdims=True)
            dlin = (dy * ps_ref[:, sl]).astype(BF16)
            dwp_ref[gi] += _dot_tn(pb, dlin)
            dpg = _dot_nt(dlin, wp_ref[gi])
            dpooled.append(dpg)
            eext_scr[0:ts, sl] = dpg / jnp.minimum(pos, float(w))
        for gi, w in enumerate(POOL_WINDOWS):
            sl = slice(gi * HEAD_DIM, (gi + 1) * HEAD_DIM)
            acc = eext_scr[:, sl]
            shift = 1
            while shift < w:
                acc = acc + pltpu.roll(acc, ts + POOL_HALO - shift, 0)
                shift *= 2
            dgp_ref[:, RET_W + gi * HEAD_DIM:RET_W + (gi + 1) * HEAD_DIM] = (acc[0:ts] - dpooled[gi]).astype(BF16)
        eext_scr[ts:ts + POOL_HALO, :] = eext_scr[0:POOL_HALO, :]
        for h in range(HEADS):
            sl = slice(h * HEAD_DIM, (h + 1) * HEAD_DIM)
            rt = ret_ref[:, sl]
            rr = lax.rsqrt(jnp.mean(rt * rt, axis=-1, keepdims=True) + RMS_EPS)
            rn = rt * rr
            gh = g_ref[:, sl]
            sg = _sigmoid(gh)
            dy = dcat_r[:, sl]
            dgp_ref[:, sl] = (dy * rn * (sg * (1.0 + gh * (1.0 - sg)))).astype(BF16)
            drn = dy * (gh * sg)
            dret_ref[:, sl] = (rr * (drn - rn * jnp.mean(drn * rn, axis=-1, keepdims=True))).astype(BF16)

    tile = lambda w: pl.BlockSpec((ts, w), lambda i: (nt - 1 - i, 0))
    return _call(
        body, name="mix_bwd", grid=(nt,),
        in_specs=[tile(D_MODEL), tile(POOL_W), tile(RET_W), tile(RET_W), _whole(), _whole(), _whole()],
        out_specs=[tile(RET_W), tile(2 * RET_W),
                   pl.BlockSpec((len(POOL_WINDOWS), HEAD_DIM, HEAD_DIM), lambda i: (0, 0, 0)),
                   pl.BlockSpec((1, POOL_W), lambda i: (0, 0))],
        out_shape=[jax.ShapeDtypeStruct((s, RET_W), BF16), jax.ShapeDtypeStruct((s, 2 * RET_W), BF16),
                   jax.ShapeDtypeStruct((len(POOL_WINDOWS), HEAD_DIM, HEAD_DIM), F32),
                   jax.ShapeDtypeStruct((1, POOL_W), F32)],
        scratch_shapes=[pltpu.VMEM((ts + POOL_HALO, POOL_W), F32)],
        sem=("arbitrary",), operands=(dz1, pooled, ret, g, wout, wpool, pscale), riders=riders,
        after=after,
    )


def _retention_bwd(q, k, v, dret, dgp, states, mask, qd, kd, cosf, sinf, riders=(), after=()):
    s = q.shape[0]
    ns = s // SUPER
    cdec = [gm ** float(SUPER) for gm in _gammas()]

    def body(q_ref, k_ref, v_ref, do_ref, dgp_ref, st_ref, mask_ref, qd_ref, kd_ref, cos_ref, sin_ref,
             dproj_ref, dstate_scr):
        i = pl.program_id(0)

        @pl.when(i == 0)
        def _():
            dstate_scr[...] = jnp.zeros_like(dstate_scr)

        cosf_t = cos_ref[...]
        sinf_t = sin_ref[...]
        for h in range(HEADS):
            sl = slice(h * HEAD_DIM, (h + 1) * HEAD_DIM)
            qh, kh, vh, doh = q_ref[:, sl], k_ref[:, sl], v_ref[:, sl], do_ref[:, sl]
            dscb = (_dot_nt(doh, vh) * mask_ref[0, h]).astype(BF16)
            dsctb = (_dot_nt(vh, doh) * mask_ref[1, h]).astype(BF16)
            sctb = (_dot_nt(kh, qh) * mask_ref[1, h]).astype(BF16)
            stb = st_ref[0, h]
            dst = dstate_scr[h]
            dstb = dst.astype(BF16)
            qdb = (qh.astype(F32) * qd_ref[:, sl]).astype(BF16)
            kdb = (kh.astype(F32) * kd_ref[:, sl]).astype(BF16)
            dq = _dot(dscb, kh) + _dot_nt(doh, stb) * qd_ref[:, sl]
            dk = _dot(dsctb, qh) + _dot_nt(vh, dstb) * kd_ref[:, sl]
            dv = _dot(sctb, doh) + _dot(kdb, dstb)
            dstate_scr[h] = dst * cdec[h] + _dot_tn(qdb, doh)
            lo = h * HEAD_DIM
            dproj_ref[:, lo:lo + HEAD_DIM] = _rope_t(dq, cosf_t, sinf_t).astype(BF16)
            dproj_ref[:, RET_W + lo:RET_W + lo + HEAD_DIM] = _rope_t(dk * K_SCALE, cosf_t, sinf_t).astype(BF16)
            dproj_ref[:, 2 * RET_W + lo:2 * RET_W + lo + HEAD_DIM] = dv.astype(BF16)
        dproj_ref[:, 3 * RET_W:IN_W] = dgp_ref[...]

    tile = lambda w: pl.BlockSpec((SUPER, w), lambda i: (ns - 1 - i, 0))
    return _call(
        body, name="retention_bwd", grid=(ns,),
        in_specs=[tile(RET_W), tile(RET_W), tile(RET_W), tile(RET_W), tile(2 * RET_W),
                  pl.BlockSpec((1, HEADS, HEAD_DIM, HEAD_DIM), lambda i: (ns - 1 - i, 0, 0, 0)),
                  _whole(), _whole(), _whole(), tile(HEAD_DIM), tile(HEAD_DIM)],
        out_specs=[tile(IN_W)],
        out_shape=[jax.ShapeDtypeStruct((s, IN_W), BF16)],
        scratch_shapes=[pltpu.VMEM((HEADS, HEAD_DIM, HEAD_DIM), F32)],
        sem=("arbitrary",), operands=(q, k, v, dret, dgp, states, mask, qd, kd, cosf, sinf), riders=riders,
        after=after,
    )


def _dx(dz1, dproj, win4, ts, riders=(), after=()):
    s = dz1.shape[0]

    def body(dz1_ref, dp_ref, w_ref, dx_ref):
        acc = ALPHA * dz1_ref[...]
        for j in range(N_SHARD):
            acc = acc + _dot_nt(dp_ref[:, j * IN_SH:(j + 1) * IN_SH], w_ref[j])
        dx_ref[...] = acc

    tile = lambda w: pl.BlockSpec((ts, w), lambda i: (i, 0))
    return _call(
        body, name="dx", grid=(s // ts,),
        in_specs=[tile(D_MODEL), tile(IN_W), _whole()],
        out_specs=[tile(D_MODEL)],
        out_shape=[jax.ShapeDtypeStruct((s, D_MODEL), F32)],
        sem=("arbitrary",), operands=(dz1, dproj, win4), riders=riders, after=after,
    )


def _wgrad(a, b, tm, tn, name, stacked, m_outer, riders=(), after=()):
    s, m = a.shape
    n = b.shape[1]

    def body(a_ref, b_ref, o32_ref, o16_ref):
        res = _dot_tn(a_ref[...], b_ref[...])
        o32_ref[...] = res.reshape(o32_ref.shape)
        o16_ref[...] = res.astype(BF16).reshape(o16_ref.shape)

    if m_outer:
        grid, blocks = (m // tm, n // tn), (lambda g0, g1: (g0, g1))
    else:
        grid, blocks = (n // tn, m // tm), (lambda g0, g1: (g1, g0))
    if stacked:
        shape = (n // tn, m, tn)
        ospec = pl.BlockSpec((1, tm, tn), lambda g0, g1: (blocks(g0, g1)[1], blocks(g0, g1)[0], 0))
    else:
        shape = (m, n)
        ospec = pl.BlockSpec((tm, tn), lambda g0, g1: blocks(g0, g1))
    return _call(
        body, name=name, grid=grid,
        in_specs=[pl.BlockSpec((s, tm), lambda g0, g1: (0, blocks(g0, g1)[0])),
                  pl.BlockSpec((s, tn), lambda g0, g1: (0, blocks(g0, g1)[1]))],
        out_specs=[ospec, ospec],
        out_shape=[jax.ShapeDtypeStruct(shape, F32), jax.ShapeDtypeStruct(shape, BF16)],
        sem=("arbitrary", "arbitrary"), operands=(a, b), riders=riders, after=after,
    )


def _wgrad_send(a, b, tn, name, barrier_id, after=()):
    s, m = a.shape
    n = b.shape[1]
    nb, hm = n // tn, m // 2

    def body(*refs):
        a_ref, b_ref = refs[:2]
        o32_ref, land_ref, send_scr, send_sems, recv_sems = refs[2 + len(after):]
        j = pl.program_id(0)
        x, y, c = _mesh_pos()

        @pl.when(j == 0)
        def _():
            barrier = pltpu.get_barrier_semaphore()
            pl.semaphore_signal(barrier, inc=1, device_id=(x, y, 1 - c), device_id_type=MESH)
            pl.semaphore_wait(barrier, 1)

        o32_ref[0] = _dot_tn(a_ref[...], b_ref[...])
        theirs = pl.ds(pl.multiple_of((1 - c) * hm, 16), hm)
        copies = [pltpu.make_async_remote_copy(
            src_ref=send_scr.at[blk], dst_ref=land_ref.at[blk], send_sem=send_sems.at[blk],
            recv_sem=recv_sems.at[blk], device_id=(x, y, 1 - c), device_id_type=MESH) for blk in range(nb)]
        for blk in range(nb):
            @pl.when(j == blk)
            def _(blk=blk):
                send_scr[blk] = o32_ref[0, theirs, :].astype(BF16)
                copies[blk].start()

        @pl.when(j == nb - 1)
        def _():
            for cp in copies:
                cp.wait()

    return pl.pallas_call(
        body, name=name, grid=(nb,),
        in_specs=[pl.BlockSpec((s, m), lambda j: (0, 0)), pl.BlockSpec((s, tn), lambda j: (0, j))]
        + [_whole()] * len(after),
        out_specs=[pl.BlockSpec((1, m, tn), lambda j: (j, 0, 0)), HBM_SPEC],
        out_shape=[jax.ShapeDtypeStruct((nb, m, tn), F32), jax.ShapeDtypeStruct((nb, hm, tn), BF16)],
        scratch_shapes=[pltpu.VMEM((nb, hm, tn), BF16), pltpu.SemaphoreType.DMA((nb,)),
                        pltpu.SemaphoreType.DMA((nb,))],
        compiler_params=pltpu.CompilerParams(dimension_semantics=("arbitrary",), vmem_limit_bytes=VMEM_LIMIT,
                                             collective_id=barrier_id),
    )(a, b, *after)


class _NoComm:
    def __init__(self, win4, wout, wup4, wdown):
        self.weights = dict(w_in=win4, w_out=wout, w_up=wup4, w_down=wdown)
        self.grads = {}

    def weight(self, name):
        return self.weights[name]

    def riders(self, call):
        return ()

    def after(self, call):
        return ()

    def landed(self, call, results, outs):
        pass

    def small_gradients(self, loss, small):
        pass

    def gradient(self, name, g32, g16):
        self.grads[name] = (g32, g16)

    def wgrad_in(self, xb, dproj):
        (g32, g16), _ = _wgrad(xb, dproj, D_MODEL, IN_SH, "wgrad_in", True, True)
        self.gradient("w_in", g32, g16)


def _local_step(x, target, cw, cb, wpool_b, pscale, g1, b1, g2, b2, comm):
    s = x.shape[0]
    ts_a = min(512, s)
    ts_f = min(256, s)
    mask, qd, kd = _decay_tables()
    cosf, sinf = _rope_tables(s)

    def run(call, fn, *args):
        outs, res = fn(*args, riders=comm.riders(call), after=comm.after(call))
        comm.landed(call, res, outs)
        return outs

    xb, q, k, v, g, pooled, cat = run("proj_pool", _proj_pool, x, comm.weight("w_in"), cosf, sinf, wpool_b,
                                      pscale, ts_a)
    ret, cat, states = run("retention_fwd", _retention_fwd, q, k, v, g, cat, mask, qd, kd)
    wout = comm.weight("w_out")
    xhat1, rstd1, h1b = run("outproj_ln1", _outproj_ln1, x, cat, wout, g1, b1, ts_a)
    wup4, wdown = comm.weight("w_up"), comm.weight("w_down")
    ub, act, sd, dz2, dz2b, loss, dg2, db2 = _ffn_fwd_loss(xhat1, h1b, target, wup4, wdown, cw, cb, g1, b1, g2, b2,
                                                           ts_f)

    dub, dz1, dz1b, dg1, db1, dcw, dcb = _ffn_bwd(dz2, dz2b, ub, sd, xhat1, rstd1, wup4, wdown, cw, g1, ts_f)
    half = D_MODEL // 2
    comm.gradient("w_up", *run("wgrad_up", _wgrad, h1b, dub, half, UP_SH, "wgrad_up", True, False))
    comm.gradient("w_out", *run("wgrad_out", _wgrad, cat, dz1b, D_MODEL, half, "wgrad_out", False, True))
    comm.gradient("w_down", *run("wgrad_down", _wgrad, act, dz2b, D_FF // 2, half, "wgrad_down", False, True))
    dret, dgp, dwp, dps = run("mix_bwd", _mix_bwd, dz1b, pooled, ret, g, wout, wpool_b, pscale, ts_a)
    small = dict(w_pool=dwp, pool_scale=dps, ln1_g=dg1, ln1_b=db1, conv_w=dcw, conv_b=dcb,
                 ln2_g=dg2, ln2_b=db2)
    comm.small_gradients(loss, small)
    mask_both = jnp.stack([mask, jnp.swapaxes(mask, 1, 2)])
    dproj, = run("retention_bwd", _retention_bwd, q, k, v, dret, dgp, states, mask_both, qd, kd, cosf, sinf)
    comm.wgrad_in(xb, dproj)
    (grad_x,), _ = _dx(dz1, dproj, comm.weight("w_in"), ts_a, after=comm.after("dx"))
    return loss, grad_x, small


CAST_ROWS = 64
SHARD_SHAPES = ((D_MODEL, IN_SH), (OUT_SH, D_MODEL), (D_MODEL, UP_SH), (DOWN_SH, D_MODEL))
N_BIG = len(SHARD_SHAPES)
CW_SHARD = (3, 1, DOWN_SH)


def _mesh_pos():
    return lax.axis_index("x"), lax.axis_index("y"), lax.axis_index("c")


def _other_chips(x, y):
    return [(1 - x, y), (x, 1 - y), (1 - x, 1 - y)]


def _half_rows(w, which):
    hr = SHARD_SHAPES[w][0] // 2
    return pl.ds(pl.multiple_of(which * hr, 16), hr)


def _gather_weights(shards, cw_shard, wpool, full):
    def body(*refs):
        in_refs = refs[:N_BIG]
        cw_ref, wpool_ref = refs[N_BIG:N_BIG + 2]
        out_refs = refs[N_BIG + 2:2 * N_BIG + 2]
        cwo_ref, wpool_b_ref = refs[2 * N_BIG + 2:2 * N_BIG + 4]
        stage = refs[2 * N_BIG + 4:3 * N_BIG + 4]
        raw = refs[3 * N_BIG + 4:4 * N_BIG + 4 - len(full)]
        send_sems, recv_sems, fsend_sems, frecv_sems, cw_send, cw_recv, local_sems, load_sems = \
            refs[4 * N_BIG + 4 - len(full):]
        x, y, c = _mesh_pos()
        j0 = 2 * x + y
        chips = _other_chips(x, y)

        fetched = [w for w in range(N_BIG) if w not in full]
        f32 = {w: in_refs[w] for w in full}
        loads = []
        for n, w in enumerate(fetched):
            f32[w] = raw[n]
            loads.append(pltpu.make_async_copy(in_refs[w], raw[n], load_sems.at[n]))
            loads[-1].start()

        def cast_to_stage(w):
            def cast(i, carry):
                rows = pl.ds(pl.multiple_of(i * CAST_ROWS, CAST_ROWS), CAST_ROWS)
                stage[w][rows, :] = f32[w][rows, :].astype(BF16)
                return carry
            lax.fori_loop(0, SHARD_SHAPES[w][0] // CAST_ROWS, cast, 0)

        for w in full:
            cast_to_stage(w)

        jx, jy, jd = 2 * (1 - x) + y, 2 * x + (1 - y), 2 * (1 - x) + (1 - y)
        neighbours = [((1 - x, y, c), jx), ((x, 1 - y, c), jy)]
        passed = jnp.where(c == 0, jx, jy)
        pass_to = (jnp.where(c == 0, x, 1 - x), jnp.where(c == 0, 1 - y, y), c)

        def nbr(w, k, block):
            return pltpu.make_async_remote_copy(
                src_ref=stage[w].at[_half_rows(w, c), :], dst_ref=out_refs[w].at[block, _half_rows(w, c), :],
                send_sem=send_sems.at[w, k], recv_sem=recv_sems.at[w, k],
                device_id=neighbours[k][0], device_id_type=MESH)

        def relay(w, block):
            return pltpu.make_async_remote_copy(
                src_ref=out_refs[w].at[passed, _half_rows(w, c), :],
                dst_ref=out_refs[w].at[block, _half_rows(w, c), :],
                send_sem=send_sems.at[w, 2], recv_sem=recv_sems.at[w, 2],
                device_id=pass_to, device_id_type=MESH)

        def d2d(w, k, block, half):
            return pltpu.make_async_remote_copy(
                src_ref=out_refs[w].at[block, _half_rows(w, half), :],
                dst_ref=out_refs[w].at[block, _half_rows(w, half), :],
                send_sem=fsend_sems.at[w, k], recv_sem=frecv_sems.at[w, k],
                device_id=(x, y, 1 - c), device_id_type=MESH)

        def conv(k, block):
            chip = chips[k]
            return pltpu.make_async_remote_copy(
                src_ref=cw_ref, dst_ref=cwo_ref.at[block], send_sem=cw_send.at[k], recv_sem=cw_recv.at[k],
                device_id=(chip[0], chip[1], c), device_id_type=MESH)

        sent = [nbr(w, k, j0) for w in full for k in range(2)] + [conv(k, j0) for k in range(3)]
        for cp in sent:
            cp.start()
        for n, w in enumerate(fetched):
            loads[n].wait()
            cast_to_stage(w)
        local = [pltpu.make_async_copy(stage[w], out_refs[w].at[j0], local_sems.at[w]) for w in range(N_BIG)]
        local.append(pltpu.make_async_copy(cw_ref, cwo_ref.at[j0], local_sems.at[N_BIG]))
        for cp in local:
            cp.start()
        wpool_b_ref[...] = wpool_ref[...].astype(BF16)
        for w in full:
            for k, (_, block) in enumerate(neighbours):
                nbr(w, k, block).wait_recv()
            later = [relay(w, passed)] + [d2d(w, k, block, c) for k, (_, block) in enumerate(neighbours)]
            for cp in later:
                cp.start()
            sent += later
        for w in full:
            relay(w, jd).wait_recv()
            fw = d2d(w, 2, jd, c)
            fw.start()
            sent.append(fw)
        for w in full:
            for k, block in enumerate([jx, jy, jd]):
                d2d(w, k, block, 1 - c).wait_recv()
        for k, chip in enumerate(chips):
            conv(k, 2 * chip[0] + chip[1]).wait_recv()
        for cp in sent:
            cp.wait_send()
        for cp in local:
            cp.wait()

    out_shape = [jax.ShapeDtypeStruct((N_SHARD,) + shp, BF16) for shp in SHARD_SHAPES]
    out_shape.append(jax.ShapeDtypeStruct((N_SHARD,) + CW_SHARD, F32))
    out_shape.append(jax.ShapeDtypeStruct(wpool.shape, BF16))
    return pl.pallas_call(
        body, name="gather_weights",
        in_specs=[_whole() if w in full else HBM_SPEC for w in range(N_BIG)] + [_whole()] * 2,
        out_specs=[HBM_SPEC] * (N_BIG + 1) + [_whole()],
        out_shape=out_shape,
        scratch_shapes=[pltpu.VMEM(shp, BF16) for shp in SHARD_SHAPES]
        + [pltpu.VMEM(shp, F32) for w, shp in enumerate(SHARD_SHAPES) if w not in full] + [
            pltpu.SemaphoreType.DMA((N_BIG, 3)), pltpu.SemaphoreType.DMA((N_BIG, 3)),
            pltpu.SemaphoreType.DMA((N_BIG, 3)), pltpu.SemaphoreType.DMA((N_BIG, 3)),
            pltpu.SemaphoreType.DMA((3,)), pltpu.SemaphoreType.DMA((3,)),
            pltpu.SemaphoreType.DMA((N_BIG + 1,)), pltpu.SemaphoreType.DMA((N_BIG - len(full),))],
        compiler_params=pltpu.CompilerParams(vmem_limit_bytes=VMEM_LIMIT),
    )(*shards, cw_shard, wpool)


def _gather_rider(arrays, ops):
    ws = sorted(arrays)

    def make(inplace, srcs, lands, send_sems, recv_sems):
        del srcs, lands
        x, y, c = _mesh_pos()
        j0, jx, jy, jd = 2 * x + y, 2 * (1 - x) + y, 2 * x + (1 - y), 2 * (1 - x) + (1 - y)
        x_nbr, y_nbr, sibling = (1 - x, y, c), (x, 1 - y, c), (x, y, 1 - c)
        starts, waits = [], []
        for n, (kind, w, (r0, nr)) in enumerate(ops):
            ref = inplace[ws.index(w)]
            hr = SHARD_SHAPES[w][0] // 2
            rows = lambda core: pl.ds(pl.multiple_of(core * hr + r0, 16), nr)
            mine, theirs = rows(c), rows(1 - c)
            if kind == "ici":
                moves = [(ref.at[j0, mine, :], x_nbr, ref.at[jx, mine, :]),
                         (ref.at[j0, mine, :], y_nbr, ref.at[jy, mine, :]),
                         (ref.at[j0, mine, :], (1 - x, 1 - y, c), ref.at[jd, mine, :])]
            elif kind == "nbr":
                moves = [(ref.at[j0, mine, :], x_nbr, ref.at[jx, mine, :]),
                         (ref.at[j0, mine, :], y_nbr, ref.at[jy, mine, :])]
            elif kind == "relay":
                passed = jnp.where(c == 0, jx, jy)
                to = (jnp.where(c == 0, x, 1 - x), jnp.where(c == 0, 1 - y, y), c)
                moves = [(ref.at[passed, mine, :], to, ref.at[jd, mine, :])]
            else:
                blocks = dict(d2d=[jx, jy, jd], d2d_nbr=[jx, jy], d2d_diag=[jd])[kind]
                moves = [(ref.at[b, mine, :], sibling, ref.at[b, theirs, :]) for b in blocks]
            for k, (src, to, landing) in enumerate(moves):
                sems = dict(send_sem=send_sems.at[3 * n + k], recv_sem=recv_sems.at[3 * n + k],
                            device_id=to, device_id_type=MESH)
                send = pltpu.make_async_remote_copy(src_ref=src, dst_ref=src, **sems)
                arrival = pltpu.make_async_remote_copy(src_ref=src, dst_ref=landing, **sems)
                starts.append(send)
                waits += [arrival.wait_recv, send.wait_send]
        return starts, waits

    return _Rider([arrays[w] for w in ws], [], [], 3 * len(ops), make)


def _whole_half(w):
    return (0, SHARD_SHAPES[w][0] // 2)


def _pair_rider(ws, g16s):
    def make(inplace, srcs, lands, send_sems, recv_sems):
        del inplace
        x, y, c = _mesh_pos()
        copies = [pltpu.make_async_remote_copy(
            src_ref=srcs[i].at[:, _half_rows(w, 1 - c), :], dst_ref=lands[i],
            send_sem=send_sems.at[i], recv_sem=recv_sems.at[i], device_id=(x, y, 1 - c), device_id_type=MESH)
            for i, w in enumerate(ws)]
        return copies, [cp.wait for cp in copies]

    lands = [jax.ShapeDtypeStruct((N_SHARD, SHARD_SHAPES[w][0] // 2, SHARD_SHAPES[w][1]), BF16) for w in ws]
    return _Rider([], g16s, lands, len(ws), make)


def _chip_rider(ws, p16s):
    def make(inplace, srcs, lands, send_sems, recv_sems):
        del inplace
        x, y, c = _mesh_pos()
        copies = []
        for i in range(len(ws)):
            for k, chip in enumerate(_other_chips(x, y)):
                copies.append(pltpu.make_async_remote_copy(
                    src_ref=srcs[i].at[2 * chip[0] + chip[1]], dst_ref=lands[i].at[k],
                    send_sem=send_sems.at[3 * i + k], recv_sem=recv_sems.at[3 * i + k],
                    device_id=(chip[0], chip[1], c), device_id_type=MESH))
        return copies, [cp.wait for cp in copies]

    lands = [jax.ShapeDtypeStruct((3, SHARD_SHAPES[w][0] // 2, SHARD_SHAPES[w][1]), BF16) for w in ws]
    return _Rider([], p16s, lands, 3 * len(ws), make)


def _final_rider(halves):
    def make(inplace, srcs, lands, send_sems, recv_sems):
        del inplace
        x, y, c = _mesh_pos()
        copies = [pltpu.make_async_remote_copy(
            src_ref=srcs[i], dst_ref=lands[i], send_sem=send_sems.at[i], recv_sem=recv_sems.at[i],
            device_id=(x, y, 1 - c), device_id_type=MESH) for i in range(len(halves))]
        return copies, [cp.wait for cp in copies]

    return _Rider([], halves, [jax.ShapeDtypeStruct(h.shape, h.dtype) for h in halves], len(halves), make)


def _comm_only(name, riders):
    _, res = _call(lambda: None, name=name, grid=(), in_specs=[], out_specs=[], out_shape=[], operands=(),
                   riders=riders)
    return res


class _SemList:
    def __init__(self, refs):
        self.at = list(refs)


def _merged_rider(riders):
    srcs = [a for r in riders for a in r.srcs]
    lands = [a for r in riders for a in r.lands]

    def make(inplace, src_refs, land_refs, send_sems, recv_sems):
        starts, waits = [], []
        s0 = l0 = c0 = 0
        for r in riders:
            part = r.make(inplace, src_refs[s0:s0 + len(r.srcs)], land_refs[l0:l0 + len(r.lands)],
                          _SemList(send_sems.at[c0:c0 + r.n_copies]), _SemList(recv_sems.at[c0:c0 + r.n_copies]))
            starts += part[0]
            waits += part[1]
            s0, l0, c0 = s0 + len(r.srcs), l0 + len(r.lands), c0 + r.n_copies
        return starts, waits

    return _Rider([], srcs, lands, sum(r.n_copies for r in riders), make)


def _split_start(name, rider, handshake=None):
    assert not rider.inplace
    ns, nl, n = len(rider.srcs), len(rider.lands), rider.n_copies
    barrier_id, peers = handshake if handshake is not None else (None, None)

    def body(*refs):
        if handshake is not None:
            x, y, c = _mesh_pos()
            others = [(x, y, 1 - c)] if peers in ("sibling", "both") else []
            if peers in ("chips", "both"):
                others += [(chip[0], chip[1], c) for chip in _other_chips(x, y)]
            barrier = pltpu.get_barrier_semaphore()
            for peer in others:
                pl.semaphore_signal(barrier, inc=1, device_id=peer, device_id_type=MESH)
            pl.semaphore_wait(barrier, len(others))
        srcs, lands = refs[:ns], refs[ns:ns + nl]
        sems = refs[ns + nl:ns + nl + 2 * n]
        token = refs[-1]
        starts, _ = rider.make([], srcs, lands, _SemList(sems[:n]), _SemList(sems[n:]))
        for cp in starts:
            cp.start()
        token[...] = jnp.zeros_like(token)

    buffers = [pltpu.with_memory_space_constraint(a, pltpu.HBM) for a in rider.srcs]
    buffers += [pltpu.with_memory_space_constraint(lax.empty(s.shape, s.dtype), pltpu.HBM) for s in rider.lands]
    hbm = pl.BlockSpec(memory_space=pltpu.HBM)
    sem = pl.BlockSpec(memory_space=pltpu.SEMAPHORE)
    outs = pl.pallas_call(
        body, name=name,
        out_shape=tuple([pltpu.SemaphoreType.DMA(())] * (2 * n) + [pltpu.HBM(b.shape, b.dtype) for b in buffers]
                        + [jax.ShapeDtypeStruct((8, 128), F32)]),
        in_specs=[hbm] * (ns + nl),
        out_specs=tuple([sem] * (2 * n) + [hbm] * (ns + nl) + [_whole()]),
        input_output_aliases={i: 2 * n + i for i in range(ns + nl)},
        compiler_params=pltpu.CompilerParams(has_side_effects=pltpu.SideEffectType.DATAFLOW_SIDE_EFFECTING,
                                             collective_id=barrier_id),
    )(*buffers)
    return (rider, outs[:2 * n], outs[2 * n:2 * n + ns + nl]), outs[-1]


def _split_parts(state, riders):
    merged, sems, buffers = state
    n, ns = merged.n_copies, len(merged.srcs)
    parts, s0, l0, c0 = [], 0, 0, 0
    for r in riders:
        parts.append((r, list(sems[c0:c0 + r.n_copies]) + list(sems[n + c0:n + c0 + r.n_copies]),
                      list(buffers[s0:s0 + len(r.srcs)]) + list(buffers[ns + l0:ns + l0 + len(r.lands)])))
        s0, l0, c0 = s0 + len(r.srcs), l0 + len(r.lands), c0 + r.n_copies
    return parts


def _split_wait(name, state, after):
    rider, sems, buffers = state
    ns, nl, n = len(rider.srcs), len(rider.lands), rider.n_copies

    def body(*refs):
        srcs, lands = refs[:ns], refs[ns:ns + nl]
        sem_refs = refs[ns + nl:ns + nl + 2 * n]
        _, waits = rider.make([], srcs, lands, _SemList(sem_refs[:n]), _SemList(sem_refs[n:]))
        for wait in waits:
            wait()

    hbm = pl.BlockSpec(memory_space=pltpu.HBM)
    sem = pl.BlockSpec(memory_space=pltpu.SEMAPHORE)
    outs = pl.pallas_call(
        body, name=name,
        out_shape=tuple(pltpu.HBM(b.shape, b.dtype) for b in buffers),
        in_specs=[hbm] * (ns + nl) + [sem] * (2 * n) + [HBM_SPEC],
        out_specs=tuple([hbm] * (ns + nl)),
        input_output_aliases={i: i for i in range(ns + nl)},
        compiler_params=pltpu.CompilerParams(has_side_effects=pltpu.SideEffectType.DATAFLOW_SIDE_EFFECTING),
    )(*buffers, *sems, after)
    return list(outs[:ns]), list(outs[ns:])


def _pair_sum(pos, ws, g32s, recvs):
    n = len(ws)

    def body(pos_ref, *refs):
        del pos_ref
        g_refs, r_refs = refs[:n], refs[n:2 * n]
        p32_refs, p16_refs = refs[2 * n:3 * n], refs[3 * n:]
        x, y, _ = _mesh_pos()
        for i in range(n):
            tot = g_refs[i][...] + r_refs[i][...].astype(F32)
            p16_refs[i][...] = tot.astype(BF16)

            @pl.when(pl.program_id(0) == 2 * x + y)
            def _(i=i, tot=tot):
                p32_refs[i][...] = tot

    halves = [(SHARD_SHAPES[w][0] // 2, SHARD_SHAPES[w][1]) for w in ws]
    own = [pl.BlockSpec((None, None) + h, lambda j, pos_ref: (j, pos_ref[0], 0, 0)) for h in halves]
    blk = [pl.BlockSpec((None,) + h, lambda j, pos_ref: (j, 0, 0)) for h in halves]
    mine = [pl.BlockSpec(h, lambda j, pos_ref: (0, 0)) for h in halves]
    g4 = [g.reshape((N_SHARD, 2) + h) for g, h in zip(g32s, halves)]
    outs = pl.pallas_call(
        body, name="pair_sum_" + "_".join(str(w) for w in ws),
        grid_spec=pltpu.PrefetchScalarGridSpec(
            num_scalar_prefetch=1, grid=(N_SHARD,), in_specs=own + blk, out_specs=mine + blk),
        out_shape=[jax.ShapeDtypeStruct(h, F32) for h in halves]
        + [jax.ShapeDtypeStruct((N_SHARD,) + h, BF16) for h in halves],
        compiler_params=_params(("arbitrary",)),
    )(pos, *g4, *recvs)
    return outs[:n], outs[n:]


def _chip_sum(p32s, recvs):
    parts = 2

    def body(*refs):
        p_refs, r_refs, f_refs = refs[:N_BIG], refs[N_BIG:2 * N_BIG], refs[2 * N_BIG:]
        for w in range(N_BIG):
            f_refs[w][...] = ((p_refs[w][...] + r_refs[w][0].astype(F32)) + r_refs[w][1].astype(F32)) \
                + r_refs[w][2].astype(F32)

    quarters = [(r // 2 // parts, cc) for r, cc in SHARD_SHAPES]
    own = [pl.BlockSpec(qt, lambda i: (i, 0)) for qt in quarters]
    rcv = [pl.BlockSpec((3,) + qt, lambda i: (0, i, 0)) for qt in quarters]
    out = [pl.BlockSpec(qt, lambda i: (i, 0)) for qt in quarters]
    return pl.pallas_call(
        body, name="chip_sum", grid=(parts,), in_specs=own + rcv, out_specs=out,
        out_shape=[jax.ShapeDtypeStruct((r // 2, cc), F32) for r, cc in SHARD_SHAPES],
        compiler_params=_params(("arbitrary",)),
    )(*p32s, *recvs)


def _adamw(w, g, m, v):
    m_new = ADAM_B1 * m + (1.0 - ADAM_B1) * g
    v_new = ADAM_B2 * v + (1.0 - ADAM_B2) * (g * g)
    m_hat = m_new / (1.0 - ADAM_B1 ** ADAM_STEP)
    v_hat = v_new / (1.0 - ADAM_B2 ** ADAM_STEP)
    delta = -ADAM_LR * (m_hat / (jnp.sqrt(v_hat) + ADAM_EPS) + ADAM_WD * w)
    return delta, m_new, v_new


def _adam_half(name, pos, grads, ws, ms, vs, into=None):
    nb = 4
    which = (lambda ref: ref[0]) if into is None else (lambda ref: 1 - ref[0])

    def body(which_ref, *refs):
        del which_ref
        groups = [refs[i * N_BIG:(i + 1) * N_BIG] for i in range(4)]
        g_refs, w_refs, m_refs, v_refs = groups
        go_refs, do_refs, mo_refs, vo_refs = [refs[len(refs) - (4 - i) * N_BIG:len(refs) - (3 - i) * N_BIG]
                                              for i in range(4)]
        for w in range(N_BIG):
            g = g_refs[w][...]
            delta, m_new, v_new = _adamw(w_refs[w][...], g, m_refs[w][...], v_refs[w][...])
            go_refs[w][...] = g
            do_refs[w][...] = delta
            mo_refs[w][...] = m_new
            vo_refs[w][...] = v_new

    blocks = [(r // 2 // nb, cc) for r, cc in SHARD_SHAPES]
    half = [pl.BlockSpec(b, lambda i, which_ref: (i, 0)) for b in blocks]
    full = [pl.BlockSpec((None,) + b, lambda i, which_ref: (0, which(which_ref) * nb + i, 0)) for b in blocks]
    shapes = [jax.ShapeDtypeStruct((1,) + shp, F32) for shp in SHARD_SHAPES]
    carried = [] if into is None else [a for kind in into for a in kind]
    first = 1 + 4 * N_BIG
    outs = pl.pallas_call(
        body, name=name,
        grid_spec=pltpu.PrefetchScalarGridSpec(
            num_scalar_prefetch=1, grid=(nb,), in_specs=half + full * 3 + [HBM_SPEC] * len(carried),
            out_specs=full * 4),
        out_shape=shapes * 4,
        input_output_aliases={first + i: i for i in range(len(carried))},
        compiler_params=_params(("arbitrary",)),
    )(pos, *grads, *ws, *ms, *vs, *carried)
    return [outs[i * N_BIG:(i + 1) * N_BIG] for i in range(4)]


SMALL_ROWS = 8
ROW_CONV_B, ROW_POOL_SCALE, ROW_LN1_G, ROW_LN1_B, ROW_LN2_G, ROW_LN2_B, ROW_LOSS = range(7)
SMALL_VECS = ((ROW_CONV_B, D_FF), (ROW_POOL_SCALE, POOL_W), (ROW_LN1_G, D_MODEL), (ROW_LN1_B, D_MODEL),
              (ROW_LN2_G, D_MODEL), (ROW_LN2_B, D_MODEL))


def _small_pack(loss, vec_grads):
    def body(*refs):
        loss_ref, gvec, out_ref = refs[0], refs[1:-1], refs[-1]
        out_ref[...] = jnp.zeros_like(out_ref)
        for (row, n), ref in zip(SMALL_VECS, gvec):
            out_ref[row:row + 1, 0:n] = ref[...]
        out_ref[ROW_LOSS:ROW_LOSS + 1, 0:HEAD_DIM] = jnp.broadcast_to(loss_ref[...], (1, HEAD_DIM))

    return pl.pallas_call(
        body, name="small_pack", in_specs=[_whole()] * (1 + len(vec_grads)), out_specs=_whole(),
        out_shape=jax.ShapeDtypeStruct((SMALL_ROWS, D_FF), F32),
    )(loss, *vec_grads)


def _small_pair_sum(own, sibling):
    n = len(own)

    def body(*refs):
        x, y, _ = _mesh_pos()
        for i in range(n):
            refs[2 * n + i][2 * x + y] = refs[i][...] + refs[n + i][...]

    return pl.pallas_call(
        body, name="small_pair_sum", in_specs=[_whole()] * (2 * n), out_specs=[_whole()] * n,
        out_shape=[jax.ShapeDtypeStruct((N_SHARD,) + a.shape, F32) for a in own],
        compiler_params=pltpu.CompilerParams(vmem_limit_bytes=VMEM_LIMIT),
    )(*own, *sibling)


def _small_chip_rider(gathered):
    n = len(gathered)

    def make(inplace, srcs, lands, send_sems, recv_sems):
        del inplace, lands
        x, y, c = _mesh_pos()
        j0 = 2 * x + y
        starts, waits = [], []
        for i in range(n):
            for k, chip in enumerate(_other_chips(x, y)):
                sems = dict(send_sem=send_sems.at[3 * i + k], recv_sem=recv_sems.at[3 * i + k],
                            device_id=(chip[0], chip[1], c), device_id_type=MESH)
                send = pltpu.make_async_remote_copy(src_ref=srcs[i].at[j0], dst_ref=srcs[i].at[j0], **sems)
                arrival = pltpu.make_async_remote_copy(
                    src_ref=srcs[i].at[j0], dst_ref=srcs[i].at[2 * chip[0] + chip[1]], **sems)
                starts.append(send)
                waits += [arrival.wait_recv, send.wait_send]
        return starts, waits

    return _Rider([], gathered, [], 3 * n, make)


def _small_adam(all_a, all_b, all_c, wp, cwp, vec_ws, m_wp, m_cwp, vec_ms, v_wp, v_cwp, vec_vs):
    nv = len(SMALL_VECS)
    np_ = 2 + nv

    def body(*refs):
        all_a_ref, all_b_ref, all_c_ref = refs[0:3]
        w_all, m_all, v_all = (refs[3 + i * np_:3 + (i + 1) * np_] for i in range(3))
        loss_out = refs[3 + 3 * np_]
        outs = refs[4 + 3 * np_:]
        x, y, _ = _mesh_pos()
        j0 = 2 * x + y
        tot_a = ((all_a_ref[0] + all_a_ref[1]) + all_a_ref[2]) + all_a_ref[3]
        tot_b = ((all_b_ref[0] + all_b_ref[1]) + all_b_ref[2]) + all_b_ref[3]
        tot_c = ((all_c_ref[0, j0] + all_c_ref[1, j0]) + all_c_ref[2, j0]) + all_c_ref[3, j0]
        loss_out[...] = tot_b[ROW_LOSS:ROW_LOSS + 1, 0:1]
        grads = [tot_a, tot_c] + [tot_b[row:row + 1, 0:n] for row, n in SMALL_VECS]
        for p in range(np_):
            for at, g in ([(j, tot_c[j:j + 1]) for j in range(3)] if p == 1 else [(Ellipsis, grads[p])]):
                delta, m_new, v_new = _adamw(w_all[p][at], g, m_all[p][at], v_all[p][at])
                outs[p][at] = g
                outs[np_ + p][at] = delta
                outs[2 * np_ + p][at] = m_new
                outs[3 * np_ + p][at] = v_new

    pshapes = [wp.shape, CW_SHARD] + [wv.shape for wv in vec_ws]
    out_shape = [jax.ShapeDtypeStruct((1, 1), F32)] + [jax.ShapeDtypeStruct(s, F32) for s in pshapes] * 4
    outs = pl.pallas_call(
        body, name="small_adam",
        in_specs=[_whole()] * (3 + 3 * np_), out_specs=[_whole()] * len(out_shape), out_shape=out_shape,
        compiler_params=pltpu.CompilerParams(vmem_limit_bytes=VMEM_LIMIT),
    )(all_a, all_b, all_c, wp, cwp, *vec_ws, m_wp, m_cwp, *vec_ms, v_wp, v_cwp, *vec_vs)
    return outs[0], [outs[1 + i * np_:1 + (i + 1) * np_] for i in range(4)]


def kernel(x, w_in, w_pool, pool_scale, w_out, ln1_g, ln1_b, w_up, conv_w, conv_b, w_down, ln2_g, ln2_b, loss_target, m_w_in, m_w_pool, m_pool_scale, m_w_out, m_ln1_g, m_ln1_b, m_w_up, m_conv_w, m_conv_b, m_w_down, m_ln2_g, m_ln2_b, v_w_in, v_w_pool, v_pool_scale, v_w_out, v_ln1_g, v_ln1_b, v_w_up, v_conv_w, v_conv_b, v_w_down, v_ln2_g, v_ln2_b):
    pos = lax.axis_index("c").astype(jnp.int32).reshape(1)
    order = ("w_in", "w_out", "w_up", "w_down")
    w_in_i, w_out_i, w_up_i, w_down_i = range(N_BIG)
    vec_names = ("conv_b", "pool_scale", "ln1_g", "ln1_b", "ln2_g", "ln2_b")

    taps_first = lambda a: jnp.transpose(a, (1, 0, 2))
    gathered = _gather_weights([w_in[0], w_out[0], w_up[0], w_down[0]], taps_first(conv_w), w_pool[0], (w_in_i,))
    cw_full = jnp.transpose(gathered[N_BIG].reshape(N_SHARD, 3, DOWN_SH), (1, 0, 2)).reshape(3, D_FF)
    up_a, up_b, up_c = (0, 176), (176, 176), (352, 160)
    assert up_c[0] + up_c[1] == SHARD_SHAPES[w_up_i][0] // 2

    class MeshComm:
        def __init__(self):
            self.w = {i: gathered[i] for i in range(N_BIG)}
            self.g32, self.g16, self.p32, self.p16, self.recv_b = {}, {}, {}, {}, {}
            self.up_complete = False
            self.tokens, self.chips = {}, []

        def weight(self, name):
            i = order.index(name)
            if name == "w_up" and not self.up_complete:
                (arrs, _), = _comm_only("gather_up_last", [_gather_rider(
                    {i: self.w[i]}, [("d2d_diag", i, up_b), ("d2d", i, up_c)])])
                self.w[i], self.up_complete = arrs[0], True
            full = self.w[i]
            return full.reshape(-1, full.shape[-1]) if name in ("w_out", "w_down") else full

        def _gather(self, ws, ops):
            return _gather_rider({w: self.w[w] for w in ws}, ops), ("w", ws)

        def _pair(self, ws):
            return _pair_rider(ws, [self.g16[w] for w in ws]), ("recv_a", ws)

        def _chip(self, ws):
            return _chip_rider(ws, [self.p16[w] for w in ws]), ("recv_b", ws)

        def plan(self, call):
            out_all, down_all = _whole_half(w_out_i), _whole_half(w_down_i)
            if call == "proj_pool":
                return [self._gather([w_out_i, w_up_i, w_down_i],
                                     [("ici", w_out_i, out_all), ("nbr", w_down_i, down_all),
                                      ("nbr", w_up_i, up_a)])]
            if call == "retention_fwd":
                return [self._gather([w_out_i, w_up_i, w_down_i],
                                     [("d2d", w_out_i, out_all),
                                      ("relay", w_down_i, down_all), ("d2d_nbr", w_down_i, down_all),
                                      ("relay", w_up_i, up_a), ("d2d_nbr", w_up_i, up_a), ("nbr", w_up_i, up_b)])]
            if call == "outproj_ln1":
                return [self._gather([w_up_i, w_down_i],
                                     [("d2d_diag", w_down_i, down_all), ("d2d_diag", w_up_i, up_a),
                                      ("relay", w_up_i, up_b), ("d2d_nbr", w_up_i, up_b), ("ici", w_up_i, up_c)])]
            return []

        def after(self, call):
            return tuple(self.tokens.pop(call, ()))

        def riders(self, call):
            self.pending = self.plan(call)
            return [r for r, _ in self.pending]

        def _start(self, name, rider, before, handshake):
            state, token = _split_start(name, rider, handshake)
            self.tokens.setdefault(before, []).append(token)
            return state

        def _finish_pair(self, name, state, ws, after):
            _, lands = _split_wait(name, state, after)
            self._finish_sum(ws, lands)

        def landed(self, call, results, outs):
            for (_, (slot, ws)), (inplace, lands) in zip(self.pending, results):
                for w, arr in zip(ws, inplace if len(inplace) else lands):
                    getattr(self, slot)[w] = arr
            if call == "wgrad_out":
                self._finish_pair("pair_exchange_up_wait", self.pair_up, [w_up_i], outs[1])
                self.chips.append(([w_up_i], self._start(
                    "chip_exchange_up_start", self._chip([w_up_i])[0], "wgrad_down", (5, "chips"))))
            if call == "mix_bwd":
                ws = [w_out_i, w_down_i]
                self._finish_pair("pair_exchange_out_down_wait", self.pair_out_down, ws, outs[0])
            if call == "retention_bwd":
                own, sibling = _split_wait("small_pair_wait", self.small_pair, outs[0])
                self.small_chip = self._start(
                    "small_chip_start", _small_chip_rider(_small_pair_sum(own, sibling)), "wgrad_in",
                    (6, "chips"))

        def small_gradients(self, loss, small):
            dcw4 = jnp.transpose(small["conv_w"].reshape(3, N_SHARD, DOWN_SH), (1, 0, 2))
            own = [small["w_pool"], _small_pack(loss, [small[n] for n in vec_names]), dcw4]
            ws = [w_out_i, w_down_i]
            parts = [self._chip(ws)[0], _final_rider(own)]
            chip, self.small_pair = _split_parts(
                self._start("chip_out_down_small_pair_start", _merged_rider(parts), "retention_bwd",
                            (7, "both")), parts)
            self.chips.append((ws, chip))

        def gradient(self, name, g32, g16):
            w = order.index(name)
            shape = (N_SHARD,) + SHARD_SHAPES[w]
            self.g32[w], self.g16[w] = g32.reshape(shape), g16.reshape(shape)
            if name == "w_up":
                self.pair_up = self._start("pair_exchange_up_start", self._pair([w])[0], "wgrad_out",
                                           (1, "sibling"))
            if name == "w_down":
                self.pair_out_down = self._start("pair_exchange_out_down_start",
                                                 self._pair([w_out_i, w_down_i])[0], "mix_bwd", (2, "sibling"))

        def wgrad_in(self, xb, dproj):
            w = w_in_i
            g32, landed = _wgrad_send(xb, dproj, IN_SH, "wgrad_in", 4, after=self.after("wgrad_in"))
            self.g32[w] = g32
            self._finish_sum([w], [landed])
            self.chips.append(([w], self._start("chip_exchange_in_start", self._chip([w])[0], "dx", (8, "chips"))))

        def _finish_sum(self, ws, lands):
            p32s, p16s = _pair_sum(pos, ws, [self.g32[w] for w in ws], lands)
            for w, p32, p16 in zip(ws, p32s, p16s):
                self.p32[w], self.p16[w] = p32, p16

        def finish(self, after):
            for n, (ws, state) in enumerate(self.chips):
                _, lands = _split_wait("chip_exchange_wait_%d" % n, state, after)
                for w, arr in zip(ws, lands):
                    self.recv_b[w] = arr
            return _split_wait("small_chip_wait", self.small_chip, after)[0]

    comm = MeshComm()
    loss, grad_x, small = _local_step(x[0], loss_target[0], cw_full, conv_b, gathered[N_BIG + 1], pool_scale,
                                      ln1_g, ln1_b, ln2_g, ln2_b, comm)

    given = dict(w_pool=w_pool, pool_scale=pool_scale, ln1_g=ln1_g, ln1_b=ln1_b, conv_w=conv_w, conv_b=conv_b,
                 ln2_g=ln2_g, ln2_b=ln2_b)
    given_m = dict(w_pool=m_w_pool, pool_scale=m_pool_scale, ln1_g=m_ln1_g, ln1_b=m_ln1_b, conv_w=m_conv_w,
                   conv_b=m_conv_b, ln2_g=m_ln2_g, ln2_b=m_ln2_b)
    given_v = dict(w_pool=v_w_pool, pool_scale=v_pool_scale, ln1_g=v_ln1_g, ln1_b=v_ln1_b, conv_w=v_conv_w,
                   conv_b=v_conv_b, ln2_g=v_ln2_g, ln2_b=v_ln2_b)
    args = []
    for src in (given, given_m, given_v):
        args += [src["w_pool"][0], taps_first(src["conv_w"]), [src[n] for n in vec_names]]
    small_sums = comm.finish(grad_x)
    loss_tot, small_out = _small_adam(*small_sums, *args)
    every = range(N_BIG)
    mine = _chip_sum([comm.p32[w] for w in every], [comm.recv_b[w] for w in every])
    final_state, _ = _split_start("pair_exchange_f32_start", _final_rider(mine), (3, "sibling"))
    mine = final_state[2][:N_BIG]
    big = ([w_in, w_out, w_up, w_down], [m_w_in, m_w_out, m_w_up, m_w_down], [v_w_in, v_w_out, v_w_up, v_w_down])
    own_half = _adam_half("adam_own_half", pos, mine, *big)
    _, theirs = _split_wait("pair_exchange_f32_wait", final_state, own_half[0][0])
    big_out = _adam_half("adam_other_half", pos, theirs, *big, into=own_half)

    names = ("w_in", "w_pool", "pool_scale", "w_out", "ln1_g", "ln1_b", "w_up", "conv_w", "conv_b", "w_down",
             "ln2_g", "ln2_b")
    small_names = ("w_pool", "conv_w") + vec_names
    result = [loss_tot.reshape(()), grad_x[None]]
    for kind in range(4):
        for n in names:
            if n in order:
                result.append(big_out[kind][order.index(n)])
            else:
                val = small_out[kind][small_names.index(n)]
                if n == "conv_w":
                    val = taps_first(val)
                elif n == "w_pool":
                    val = val[None]
                result.append(val)
    return tuple(result)
```

```python
import functools

import numpy as np
import jax
import jax.numpy as jnp
from jax import lax
from jax.experimental import pallas as pl
from jax.experimental.pallas import tpu as pltpu

F32 = jnp.float32
BF16 = jnp.bfloat16

D_MODEL = 1024
HEADS = 4
HEAD_DIM = 128
RET_W = HEADS * HEAD_DIM
POOL_WINDOWS = (2, 4, 8, 16)
POOL_W = 512
IN_W = 4 * RET_W + POOL_W
D_FF = 2816
N_SHARD = 4
IN_SH = IN_W // N_SHARD
UP_SH = 2 * D_FF // N_SHARD
DOWN_SH = D_FF // N_SHARD
OUT_SH = D_MODEL // N_SHARD
ROPE_BASE = 10000.0
LN_EPS = 1e-5
RMS_EPS = 1e-6
ALPHA = 2.0 ** 0.25
K_SCALE = HEAD_DIM ** -0.5
SUPER = 256
CHUNK = 64
POOL_HALO = 16
CONV_HALO = 8
FFN_STRIP = 128
LN_ROWS = 32

ADAM_LR = 0.001
ADAM_B1 = 0.9
ADAM_B2 = 0.999
ADAM_EPS = 1e-08
ADAM_WD = 0.01
ADAM_STEP = 10

MESH = pl.DeviceIdType.MESH
VMEM_LIMIT = 56 * 1024 * 1024


def _dot(a, b):
    return jnp.dot(a, b, preferred_element_type=F32)


def _dot_nt(a, b):
    return lax.dot_general(a, b, (((1,), (1,)), ((), ())), preferred_element_type=F32)


def _dot_tn(a, b):
    return lax.dot_general(a, b, (((0,), (0,)), ((), ())), preferred_element_type=F32)


def _sigmoid(x):
    return 1.0 / (1.0 + jnp.exp(-x))


def _params(sem):
    return pltpu.CompilerParams(dimension_semantics=sem, vmem_limit_bytes=VMEM_LIMIT)


def _whole():
    return pl.BlockSpec(memory_space=pltpu.VMEM)


HBM_SPEC = pl.BlockSpec(memory_space=pl.ANY)


class _Rider:
    def __init__(self, inplace, srcs, lands, n_copies, make, handshake=None):
        self.inplace, self.srcs, self.lands, self.n_copies, self.make = list(inplace), list(srcs), list(lands), n_copies, make
        self.handshake = handshake


def _call(body, *, name, grid, in_specs, out_specs, out_shape, operands, scratch_shapes=(), sem=(),
          aliases=None, riders=(), after=()):
    n_in, n_out, n_scr = len(in_specs), len(out_shape), len(scratch_shapes)
    in_specs, out_specs, out_shape = list(in_specs), list(out_specs), list(out_shape)
    operands, scratch_shapes, aliases = list(operands), list(scratch_shapes), dict(aliases or {})
    in_specs += [_whole()] * len(after)
    operands += list(after)
    shakes = [r.handshake for r in riders if r.handshake is not None]
    assert len(shakes) <= 1
    for r in riders:
        for a in r.inplace:
            aliases[len(in_specs)] = len(out_shape)
            in_specs.append(HBM_SPEC)
            operands.append(a)
            out_specs.append(HBM_SPEC)
            out_shape.append(jax.ShapeDtypeStruct(a.shape, a.dtype))
        for a in r.srcs:
            in_specs.append(HBM_SPEC)
            operands.append(a)
        for shp in r.lands:
            out_specs.append(HBM_SPEC)
            out_shape.append(shp)
        scratch_shapes += [pltpu.SemaphoreType.DMA((r.n_copies,)), pltpu.SemaphoreType.DMA((r.n_copies,))]

    def full(*refs):
        ins = refs[:n_in]
        at = n_in + len(after)
        r_srcs = []
        for r in riders:
            at += len(r.inplace)
            r_srcs.append(refs[at:at + len(r.srcs)])
            at += len(r.srcs)
        outs = refs[at:at + n_out]
        at += n_out
        r_outs = []
        for r in riders:
            r_outs.append((refs[at:at + len(r.inplace)], refs[at + len(r.inplace):at + len(r.inplace) + len(r.lands)]))
            at += len(r.inplace) + len(r.lands)
        scr = refs[at:at + n_scr]
        at += n_scr
        r_sems = [refs[at + 2 * i:at + 2 * i + 2] for i in range(len(riders))]

        def copies():
            return [r.make(r_outs[i][0], r_srcs[i], r_outs[i][1], r_sems[i][0], r_sems[i][1])
                    for i, r in enumerate(riders)]

        def start():
            if shakes:
                _shake_hands(shakes[0][1])
            for starts, _ in copies():
                for cp in starts:
                    cp.start()

        def finish():
            for _, waits in copies():
                for wait in waits:
                    wait()

        if riders and grid:
            first = functools.reduce(jnp.logical_and, [pl.program_id(d) == 0 for d in range(len(grid))])
            last = functools.reduce(jnp.logical_and, [pl.program_id(d) == grid[d] - 1 for d in range(len(grid))])
            pl.when(first)(start)
            body(*ins, *outs, *scr)
            pl.when(last)(finish)
        else:
            if riders:
                start()
            body(*ins, *outs, *scr)
            if riders:
                finish()

    barrier_id = shakes[0][0] if shakes else None
    params = pltpu.CompilerParams(vmem_limit_bytes=VMEM_LIMIT, collective_id=barrier_id,
                                  **(dict(dimension_semantics=sem) if grid else {}))
    res = pl.pallas_call(
        full, name=name, grid=grid, in_specs=in_specs, out_specs=out_specs, out_shape=out_shape,
        scratch_shapes=scratch_shapes, input_output_aliases=aliases, compiler_params=params,
    )(*operands)
    outs, at, rider_res = res[:n_out], n_out, []
    for r in riders:
        rider_res.append((res[at:at + len(r.inplace)], res[at + len(r.inplace):at + len(r.inplace) + len(r.lands)]))
        at += len(r.inplace) + len(r.lands)
    return list(outs), rider_res


def _gammas():
    return [1.0 - 2.0 ** (-5.0 - h) for h in range(HEADS)]


def _decay_tables():
    idx = np.arange(SUPER)
    dist = np.abs(idx[:, None] - idx[None, :]).astype(np.float64)
    visible = (idx[None, :] // CHUNK) <= (idx[:, None] // CHUNK)
    mask = np.stack([np.where(visible, g ** dist, 0.0) for g in _gammas()])
    qd = np.concatenate([np.repeat((g ** (idx + 1.0))[:, None], HEAD_DIM, 1) for g in _gammas()], 1)
    kd = np.concatenate([np.repeat((g ** (SUPER - 1.0 - idx))[:, None], HEAD_DIM, 1) for g in _gammas()], 1)
    return (jnp.asarray(mask, F32), jnp.asarray(qd, F32), jnp.asarray(kd, F32))


def _rope_tables(s):
    inv_freq = ROPE_BASE ** (-np.arange(0, HEAD_DIM, 2, dtype=np.float64) / HEAD_DIM)
    ang = np.arange(s, dtype=np.float64)[:, None] * inv_freq[None, :]
    cos, sin = np.cos(ang), np.sin(ang)
    return (jnp.asarray(np.concatenate([cos, cos], 1), F32),
            jnp.asarray(np.concatenate([-sin, sin], 1), F32))


def _rope(t, cosf, sinf):
    return t * cosf + pltpu.roll(t, HEAD_DIM // 2, 1) * sinf


def _rope_t(t, cosf, sinf):
    return t * cosf - pltpu.roll(t, HEAD_DIM // 2, 1) * sinf


def _layernorm_fwd(z):
    mu = jnp.mean(z, axis=-1, keepdims=True)
    zc = z - mu
    var = jnp.mean(zc * zc, axis=-1, keepdims=True)
    rstd = lax.rsqrt(var + LN_EPS)
    return zc * rstd, rstd


def _layernorm_bwd(dy, xhat, rstd, gain):
    dxh = dy * gain
    m1 = jnp.mean(dxh, axis=-1, keepdims=True)
    m2 = jnp.mean(dxh * xhat, axis=-1, keepdims=True)
    return rstd * (dxh - m1 - xhat * m2)


def _proj_pool(x, win4, cosf, sinf, wpool, pscale, ts, riders=(), after=()):
    s = x.shape[0]
    nt = s // ts

    def body(x_ref, w_ref, cos_ref, sin_ref, wp_ref, ps_ref,
             xb_ref, q_ref, k_ref, v_ref, g_ref, pooled_ref, cat_ref, proj_scr, pext_scr):
        i = pl.program_id(0)
        xb = x_ref[...].astype(BF16)
        xb_ref[...] = xb
        for j in range(N_SHARD):
            proj_scr[:, j * IN_SH:(j + 1) * IN_SH] = _dot(xb, w_ref[j])
        cosf_t = cos_ref[...]
        sinf_t = sin_ref[...]
        for h in range(HEADS):
            lo = h * HEAD_DIM
            q_ref[:, lo:lo + HEAD_DIM] = _rope(proj_scr[:, lo:lo + HEAD_DIM], cosf_t, sinf_t).astype(BF16)
            kk = _rope(proj_scr[:, RET_W + lo:RET_W + lo + HEAD_DIM], cosf_t, sinf_t) * K_SCALE
            k_ref[:, lo:lo + HEAD_DIM] = kk.astype(BF16)
        v_ref[...] = proj_scr[:, 2 * RET_W:3 * RET_W].astype(BF16)
        g_ref[...] = proj_scr[:, 3 * RET_W:4 * RET_W]

        @pl.when(i == 0)
        def _():
            pext_scr[0:POOL_HALO, :] = jnp.zeros((POOL_HALO, POOL_W), F32)

        pext_scr[POOL_HALO:POOL_HALO + ts, :] = proj_scr[:, 4 * RET_W:IN_W]
        pos = (i * ts + lax.broadcasted_iota(jnp.int32, (ts, 1), 0) + 1).astype(F32)
        for gi, w in enumerate(POOL_WINDOWS):
            lo = gi * HEAD_DIM
            ext = pext_scr[:, lo:lo + HEAD_DIM]
            acc = ext
            shift = 1
            while shift < w:
                acc = acc + pltpu.roll(acc, shift, 0)
                shift *= 2
            tok = ext[POOL_HALO:POOL_HALO + ts]
            pooled = acc[POOL_HALO:POOL_HALO + ts] / jnp.minimum(pos, float(w)) - tok
            pooled_b = pooled.astype(BF16)
            pooled_ref[:, lo:lo + HEAD_DIM] = pooled_b
            lin = _dot(pooled_b, wp_ref[gi])
            cat_ref[:, lo:lo + HEAD_DIM] = (lin * ps_ref[:, lo:lo + HEAD_DIM]).astype(BF16)
        pext_scr[0:POOL_HALO, :] = pext_scr[ts:ts + POOL_HALO, :]

    tile = lambda w: pl.BlockSpec((ts, w), lambda i: (i, 0))
    return _call(
        body, name="proj_pool", grid=(nt,),
        in_specs=[tile(D_MODEL), _whole(), tile(HEAD_DIM), tile(HEAD_DIM), _whole(), _whole()],
        out_specs=[tile(D_MODEL), tile(RET_W), tile(RET_W), tile(RET_W), tile(RET_W), tile(POOL_W),
                   pl.BlockSpec((ts, POOL_W), lambda i: (i, 1))],
        out_shape=[jax.ShapeDtypeStruct((s, D_MODEL), BF16), jax.ShapeDtypeStruct((s, RET_W), BF16),
                   jax.ShapeDtypeStruct((s, RET_W), BF16), jax.ShapeDtypeStruct((s, RET_W), BF16),
                   jax.ShapeDtypeStruct((s, RET_W), F32), jax.ShapeDtypeStruct((s, POOL_W), BF16),
                   jax.ShapeDtypeStruct((s, 2 * RET_W), BF16)],
        scratch_shapes=[pltpu.VMEM((ts, IN_W), F32), pltpu.VMEM((ts + POOL_HALO, POOL_W), F32)],
        sem=("arbitrary",), operands=(x, win4, cosf, sinf, wpool, pscale), riders=riders, after=after,
    )


def _retention_fwd(q, k, v, g, cat, mask, qd, kd, riders=(), after=()):
    s = q.shape[0]
    ns = s // SUPER
    cdec = [gm ** float(SUPER) for gm in _gammas()]

    def body(q_ref, k_ref, v_ref, g_ref, cat_in, mask_ref, qd_ref, kd_ref,
             ret_ref, cat_ref, st_ref, state_scr):
        del cat_in
        n = pl.program_id(0)

        @pl.when(n == 0)
        def _():
            state_scr[...] = jnp.zeros_like(state_scr)

        for h in range(HEADS):
            sl = slice(h * HEAD_DIM, (h + 1) * HEAD_DIM)
            qh, kh, vh = q_ref[:, sl], k_ref[:, sl], v_ref[:, sl]
            sc = _dot_nt(qh, kh) * mask_ref[h]
            st = state_scr[h]
            stb = st.astype(BF16)
            st_ref[0, h] = stb
            qdb = (qh.astype(F32) * qd_ref[:, sl]).astype(BF16)
            kdb = (kh.astype(F32) * kd_ref[:, sl]).astype(BF16)
            ret = _dot(sc.astype(BF16), vh) + _dot(qdb, stb)
            state_scr[h] = st * cdec[h] + _dot_tn(kdb, vh)
            ret_ref[:, sl] = ret
            r = lax.rsqrt(jnp.mean(ret * ret, axis=-1, keepdims=True) + RMS_EPS)
            gh = g_ref[:, sl]
            cat_ref[:, sl] = ((ret * r) * (gh * _sigmoid(gh))).astype(BF16)

    tile = pl.BlockSpec((SUPER, RET_W), lambda n: (n, 0))
    return _call(
        body, name="retention_fwd", grid=(ns,),
        in_specs=[tile, tile, tile, tile, HBM_SPEC, _whole(), _whole(), _whole()],
        out_specs=[tile, tile, pl.BlockSpec((1, HEADS, HEAD_DIM, HEAD_DIM), lambda n: (n, 0, 0, 0))],
        out_shape=[jax.ShapeDtypeStruct((s, RET_W), F32), jax.ShapeDtypeStruct((s, 2 * RET_W), BF16),
                   jax.ShapeDtypeStruct((ns, HEADS, HEAD_DIM, HEAD_DIM), BF16)],
        scratch_shapes=[pltpu.VMEM((HEADS, HEAD_DIM, HEAD_DIM), F32)],
        aliases={4: 1}, sem=("arbitrary",), operands=(q, k, v, g, cat, mask, qd, kd), riders=riders,
        after=after,
    )


def _outproj_ln1(x, cat, wout, g1, b1, ts, riders=(), after=()):
    s = x.shape[0]

    def body(x_ref, cat_ref, w_ref, g_ref, b_ref, xhat_ref, rstd_ref, h1b_ref):
        z = ALPHA * x_ref[...] + _dot(cat_ref[...], w_ref[...])
        xhat, rstd = _layernorm_fwd(z)
        xhat_ref[...] = xhat
        rstd_ref[...] = rstd
        h1b_ref[...] = (xhat * g_ref[...] + b_ref[...]).astype(BF16)

    tile = lambda w: pl.BlockSpec((ts, w), lambda i: (i, 0))
    return _call(
        body, name="outproj_ln1", grid=(s // ts,),
        in_specs=[tile(D_MODEL), tile(D_MODEL), _whole(), _whole(), _whole()],
        out_specs=[tile(D_MODEL), tile(1), tile(D_MODEL)],
        out_shape=[jax.ShapeDtypeStruct((s, D_MODEL), F32), jax.ShapeDtypeStruct((s, 1), F32),
                   jax.ShapeDtypeStruct((s, D_MODEL), BF16)],
        sem=("arbitrary",), operands=(x, cat, wout, g1, b1), riders=riders, after=after,
    )


def _ffn_fwd_loss(xhat1, h1b, target, wup4, wdown, cw, cb, g1, b1, g2, b2, ts):
    s = xhat1.shape[0]

    def body(xhat_ref, h1b_ref, tgt_ref, wup_ref, wdn_ref, cw_ref, cb_ref, g1_ref, b1_ref, g2_ref, b2_ref,
             ub_ref, act_ref, sd_ref, dz2_ref, dz2b_ref, loss_ref, dg2_ref, db2_ref, val_scr, gext_scr, ffn_scr):
        i = pl.program_id(0)

        @pl.when(i == 0)
        def _():
            gext_scr[0:CONV_HALO, :] = jnp.zeros((CONV_HALO, D_FF), F32)
            loss_ref[...] = jnp.zeros_like(loss_ref)
            dg2_ref[...] = jnp.zeros_like(dg2_ref)
            db2_ref[...] = jnp.zeros_like(db2_ref)

        for half in range(2):
            lo = half * UP_SH
            gext_scr[CONV_HALO:CONV_HALO + ts, lo:lo + UP_SH] = _dot(h1b_ref[...], wup_ref[2 + half])
            val_scr[:, lo:lo + UP_SH] = _dot(h1b_ref[...], wup_ref[half])
            for c0 in range(lo, lo + UP_SH, FFN_STRIP):
                cols = slice(c0, c0 + FFN_STRIP)
                ext = gext_scr[:, cols]
                gate = ext[CONV_HALO:]
                hc = cb_ref[:, cols] + ((pltpu.roll(ext, 2, 0)[CONV_HALO:] * cw_ref[0:1, cols]
                                         + pltpu.roll(ext, 1, 0)[CONV_HALO:] * cw_ref[1:2, cols])
                                        + gate * cw_ref[2:3, cols])
                val = val_scr[:, cols]
                sg = _sigmoid(hc)
                si = hc * sg
                act_ref[:, cols] = (si * val).astype(BF16)
                ub_ref[:, cols] = val.astype(BF16)
                ub_ref[:, D_FF + c0:D_FF + c0 + FFN_STRIP] = gate.astype(BF16)
                sd_ref[:, cols] = hc.astype(BF16)
            part = _dot(act_ref[:, lo:lo + UP_SH], wdn_ref[lo:lo + UP_SH, :])
            if half == 0:
                ffn_scr[...] = part
            else:
                ffn_scr[...] += part

        gext_scr[0:CONV_HALO, :] = gext_scr[ts:ts + CONV_HALO, :]

        loss_acc = jnp.zeros((1, 1), F32)
        dg2_acc = jnp.zeros((1, D_MODEL), F32)
        db2_acc = jnp.zeros((1, D_MODEL), F32)
        for r0 in range(0, ts, LN_ROWS):
            rows = slice(r0, r0 + LN_ROWS)
            h1 = xhat_ref[rows, :] * g1_ref[...] + b1_ref[...]
            xhat2, rstd2 = _layernorm_fwd(ALPHA * h1 + ffn_scr[rows, :])
            diff = (xhat2 * g2_ref[...] + b2_ref[...]) - tgt_ref[rows, :]
            row = jnp.mean(diff * diff, axis=-1, keepdims=True)
            loss_acc = loss_acc + 0.5 * jnp.sum(row, axis=0, keepdims=True)
            dy = diff * (1.0 / D_MODEL)
            dg2_acc = dg2_acc + jnp.sum(dy * xhat2, axis=0, keepdims=True)
            db2_acc = db2_acc + jnp.sum(dy, axis=0, keepdims=True)
            dz2 = _layernorm_bwd(dy, xhat2, rstd2, g2_ref[...])
            dz2_ref[rows, :] = dz2
            dz2b_ref[rows, :] = dz2.astype(BF16)
        loss_ref[...] += loss_acc
        dg2_ref[...] += dg2_acc
        db2_ref[...] += db2_acc

    tile = lambda w: pl.BlockSpec((ts, w), lambda i: (i, 0))
    acc = lambda w: pl.BlockSpec((1, w), lambda i: (0, 0))
    return pl.pallas_call(
        body, name="ffn_fwd_loss", grid=(s // ts,),
        in_specs=[tile(D_MODEL), tile(D_MODEL), tile(D_MODEL)] + [_whole()] * 8,
        out_specs=[tile(2 * D_FF), tile(D_FF), tile(D_FF), tile(D_MODEL), tile(D_MODEL),
                   acc(1), acc(D_MODEL), acc(D_MODEL)],
        out_shape=[jax.ShapeDtypeStruct((s, 2 * D_FF), BF16), jax.ShapeDtypeStruct((s, D_FF), BF16),
                   jax.ShapeDtypeStruct((s, D_FF), BF16), jax.ShapeDtypeStruct((s, D_MODEL), F32),
                   jax.ShapeDtypeStruct((s, D_MODEL), BF16),
                   jax.ShapeDtypeStruct((1, 1), F32), jax.ShapeDtypeStruct((1, D_MODEL), F32),
                   jax.ShapeDtypeStruct((1, D_MODEL), F32)],
        scratch_shapes=[pltpu.VMEM((ts, D_FF), F32), pltpu.VMEM((ts + CONV_HALO, D_FF), F32),
                        pltpu.VMEM((ts, D_MODEL), F32)],
        compiler_params=_params(("arbitrary",)),
    )(xhat1, h1b, target, wup4, wdown, cw, cb, g1, b1, g2, b2)


def _ffn_bwd(dz2, dz2b, ub, sd, xhat1, rstd1, wup4, wdown, cw, g1, ts):
    s = dz2.shape[0]
    nt = s // ts

    def body(dz2_ref, dz2b_ref, ub_ref, sd_ref, xhat_ref, rstd_ref, wup_ref, wdn_ref, cw_ref, g1_ref,
             dub_ref, dz1_ref, dz1b_ref, dg1_ref, db1_ref, dcw_ref, dcb_ref, dext_scr, da_scr):
        i = pl.program_id(0)

        @pl.when(i == 0)
        def _():
            dext_scr[ts:ts + CONV_HALO, :] = jnp.zeros((CONV_HALO, D_FF), F32)
            dg1_ref[...] = jnp.zeros_like(dg1_ref)
            db1_ref[...] = jnp.zeros_like(db1_ref)
            dcw_ref[...] = jnp.zeros_like(dcw_ref)
            dcb_ref[...] = jnp.zeros_like(dcb_ref)

        da_scr[...] = _dot_nt(dz2b_ref[...], wdn_ref[...])
        n_ext = ts + CONV_HALO
        for c0 in range(0, D_FF, FFN_STRIP):
            cols = slice(c0, c0 + FFN_STRIP)
            gcols = slice(D_FF + c0, D_FF + c0 + FFN_STRIP)
            val = ub_ref[:, cols].astype(F32)
            gate = ub_ref[:, gcols].astype(F32)
            da = da_scr[:, cols]
            hc = sd_ref[:, cols].astype(F32)
            sg = _sigmoid(hc)
            dhc = da * val * (sg * (1.0 + hc * (1.0 - sg)))
            dext_scr[0:ts, cols] = dhc
            dext = dext_scr[:, cols]
            dhc1 = pltpu.roll(dext, n_ext - 1, 0)[0:ts]
            dhc2 = pltpu.roll(dext, n_ext - 2, 0)[0:ts]
            dcb_ref[:, cols] += jnp.sum(dhc, axis=0, keepdims=True)
            dcw_ref[0:1, cols] += jnp.sum(dhc2 * gate, axis=0, keepdims=True)
            dcw_ref[1:2, cols] += jnp.sum(dhc1 * gate, axis=0, keepdims=True)
            dcw_ref[2:3, cols] += jnp.sum(dhc * gate, axis=0, keepdims=True)
            dgate = dhc * cw_ref[2:3, cols] + dhc1 * cw_ref[1:2, cols] + dhc2 * cw_ref[0:1, cols]
            dub_ref[:, cols] = (da * (hc * sg)).astype(BF16)
            dub_ref[:, gcols] = dgate.astype(BF16)
        dext_scr[ts:n_ext, :] = dext_scr[0:CONV_HALO, :]
        dh1 = ALPHA * dz2_ref[...]
        for j in range(N_SHARD):
            dh1 = dh1 + _dot_nt(dub_ref[:, j * UP_SH:(j + 1) * UP_SH], wup_ref[j])
        xhat = xhat_ref[...]
        dg1_ref[...] += jnp.sum(dh1 * xhat, axis=0, keepdims=True)
        db1_ref[...] += jnp.sum(dh1, axis=0, keepdims=True)
        dz1 = _layernorm_bwd(dh1, xhat, rstd_ref[...], g1_ref[...])
        dz1_ref[...] = dz1
        dz1b_ref[...] = dz1.astype(BF16)

    tile = lambda w: pl.BlockSpec((ts, w), lambda i: (nt - 1 - i, 0))
    acc = lambda rws, w: pl.BlockSpec((rws, w), lambda i: (0, 0))
    return pl.pallas_call(
        body, name="ffn_bwd", grid=(nt,),
        in_specs=[tile(D_MODEL), tile(D_MODEL), tile(2 * D_FF), tile(D_FF), tile(D_MODEL), tile(1)]
        + [_whole()] * 4,
        out_specs=[tile(2 * D_FF), tile(D_MODEL), tile(D_MODEL), acc(1, D_MODEL), acc(1, D_MODEL),
                   acc(3, D_FF), acc(1, D_FF)],
        out_shape=[jax.ShapeDtypeStruct((s, 2 * D_FF), BF16),
                   jax.ShapeDtypeStruct((s, D_MODEL), F32), jax.ShapeDtypeStruct((s, D_MODEL), BF16),
                   jax.ShapeDtypeStruct((1, D_MODEL), F32),
                   jax.ShapeDtypeStruct((1, D_MODEL), F32), jax.ShapeDtypeStruct((3, D_FF), F32),
                   jax.ShapeDtypeStruct((1, D_FF), F32)],
        scratch_shapes=[pltpu.VMEM((ts + CONV_HALO, D_FF), F32), pltpu.VMEM((ts, D_FF), F32)],
        compiler_params=_params(("arbitrary",)),
    )(dz2, dz2b, ub, sd, xhat1, rstd1, wup4, wdown, cw, g1)


def _mix_bwd(dz1, pooled, ret, g, wout, wpool, pscale, ts, riders=(), after=()):
    s = dz1.shape[0]
    nt = s // ts

    def body(dz1_ref, pooled_ref, ret_ref, g_ref, wout_ref, wp_ref, ps_ref,
             dret_ref, dgp_ref, dwp_ref, dps_ref, eext_scr):
        i = pl.program_id(0)
        r = nt - 1 - i

        @pl.when(i == 0)
        def _():
            eext_scr[ts:ts + POOL_HALO, :] = jnp.zeros((POOL_HALO, POOL_W), F32)
            dwp_ref[...] = jnp.zeros_like(dwp_ref)
            dps_ref[...] = jnp.zeros_like(dps_ref)

        dzb = dz1_ref[...].astype(BF16)
        dcat_r = _dot_nt(dzb, wout_ref[0:RET_W, :])
        dcat_p = _dot_nt(dzb, wout_ref[RET_W:2 * RET_W, :])
        pos = (r * ts + lax.broadcasted_iota(jnp.int32, (ts, 1), 0) + 1).astype(F32)
        dpooled = []
        for gi, w in enumerate(POOL_WINDOWS):
            sl = slice(gi * HEAD_DIM, (gi + 1) * HEAD_DIM)
            pb = pooled_ref[:, sl]
            dy = dcat_p[:, sl]
            dps_ref[:, sl] += jnp.sum(dy * _dot(pb, wp_ref[gi]), axis=0, keepdims=True)
            dlin = (dy * ps_ref[:, sl]).astype(BF16)
            dwp_ref[gi] += _dot_tn(pb, dlin)
            dpg = _dot_nt(dlin, wp_ref[gi])
            dpooled.append(dpg)
            eext_scr[0:ts, sl] = dpg / jnp.minimum(pos, float(w))
        for gi, w in enumerate(POOL_WINDOWS):
            sl = slice(gi * HEAD_DIM, (gi + 1) * HEAD_DIM)
            acc = eext_scr[:, sl]
            shift = 1
            while shift < w:
                acc = acc + pltpu.roll(acc, ts + POOL_HALO - shift, 0)
                shift *= 2
            dgp_ref[:, RET_W + gi * HEAD_DIM:RET_W + (gi + 1) * HEAD_DIM] = (acc[0:ts] - dpooled[gi]).astype(BF16)
        eext_scr[ts:ts + POOL_HALO, :] = eext_scr[0:POOL_HALO, :]
        for h in range(HEADS):
            sl = slice(h * HEAD_DIM, (h + 1) * HEAD_DIM)
            rt = ret_ref[:, sl]
            rr = lax.rsqrt(jnp.mean(rt * rt, axis=-1, keepdims=True) + RMS_EPS)
            rn = rt * rr
            gh = g_ref[:, sl]
            sg = _sigmoid(gh)
            dy = dcat_r[:, sl]
            dgp_ref[:, sl] = (dy * rn * (sg * (1.0 + gh * (1.0 - sg)))).astype(BF16)
            drn = dy * (gh * sg)
            dret_ref[:, sl] = (rr * (drn - rn * jnp.mean(drn * rn, axis=-1, keepdims=True))).astype(BF16)

    tile = lambda w: pl.BlockSpec((ts, w), lambda i: (nt - 1 - i, 0))
    return _call(
        body, name="mix_bwd", grid=(nt,),
        in_specs=[tile(D_MODEL), tile(POOL_W), tile(RET_W), tile(RET_W), _whole(), _whole(), _whole()],
        out_specs=[tile(RET_W), tile(2 * RET_W),
                   pl.BlockSpec((len(POOL_WINDOWS), HEAD_DIM, HEAD_DIM), lambda i: (0, 0, 0)),
                   pl.BlockSpec((1, POOL_W), lambda i: (0, 0))],
        out_shape=[jax.ShapeDtypeStruct((s, RET_W), BF16), jax.ShapeDtypeStruct((s, 2 * RET_W), BF16),
                   jax.ShapeDtypeStruct((len(POOL_WINDOWS), HEAD_DIM, HEAD_DIM), F32),
                   jax.ShapeDtypeStruct((1, POOL_W), F32)],
        scratch_shapes=[pltpu.VMEM((ts + POOL_HALO, POOL_W), F32)],
        sem=("arbitrary",), operands=(dz1, pooled, ret, g, wout, wpool, pscale), riders=riders,
        after=after,
    )


def _retention_bwd(q, k, v, dret, dgp, states, mask, qd, kd, cosf, sinf, riders=(), after=()):
    s = q.shape[0]
    ns = s // SUPER
    cdec = [gm ** float(SUPER) for gm in _gammas()]

    def body(q_ref, k_ref, v_ref, do_ref, dgp_ref, st_ref, mask_ref, qd_ref, kd_ref, cos_ref, sin_ref,
             dproj_ref, dstate_scr):
        i = pl.program_id(0)

        @pl.when(i == 0)
        def _():
            dstate_scr[...] = jnp.zeros_like(dstate_scr)

        cosf_t = cos_ref[...]
        sinf_t = sin_ref[...]
        for h in range(HEADS):
            sl = slice(h * HEAD_DIM, (h + 1) * HEAD_DIM)
            qh, kh, vh, doh = q_ref[:, sl], k_ref[:, sl], v_ref[:, sl], do_ref[:, sl]
            dscb = (_dot_nt(doh, vh) * mask_ref[0, h]).astype(BF16)
            dsctb = (_dot_nt(vh, doh) * mask_ref[1, h]).astype(BF16)
            sctb = (_dot_nt(kh, qh) * mask_ref[1, h]).astype(BF16)
            stb = st_ref[0, h]
            dst = dstate_scr[h]
            dstb = dst.astype(BF16)
            qdb = (qh.astype(F32) * qd_ref[:, sl]).astype(BF16)
            kdb = (kh.astype(F32) * kd_ref[:, sl]).astype(BF16)
            dq = _dot(dscb, kh) + _dot_nt(doh, stb) * qd_ref[:, sl]
            dk = _dot(dsctb, qh) + _dot_nt(vh, dstb) * kd_ref[:, sl]
            dv = _dot(sctb, doh) + _dot(kdb, dstb)
            dstate_scr[h] = dst * cdec[h] + _dot_tn(qdb, doh)
            lo = h * HEAD_DIM
            dproj_ref[:, lo:lo + HEAD_DIM] = _rope_t(dq, cosf_t, sinf_t).astype(BF16)
            dproj_ref[:, RET_W + lo:RET_W + lo + HEAD_DIM] = _rope_t(dk * K_SCALE, cosf_t, sinf_t).astype(BF16)
            dproj_ref[:, 2 * RET_W + lo:2 * RET_W + lo + HEAD_DIM] = dv.astype(BF16)
        dproj_ref[:, 3 * RET_W:IN_W] = dgp_ref[...]

    tile = lambda w: pl.BlockSpec((SUPER, w), lambda i: (ns - 1 - i, 0))
    return _call(
        body, name="retention_bwd", grid=(ns,),
        in_specs=[tile(RET_W), tile(RET_W), tile(RET_W), tile(RET_W), tile(2 * RET_W),
                  pl.BlockSpec((1, HEADS, HEAD_DIM, HEAD_DIM), lambda i: (ns - 1 - i, 0, 0, 0)),
                  _whole(), _whole(), _whole(), tile(HEAD_DIM), tile(HEAD_DIM)],
        out_specs=[tile(IN_W)],
        out_shape=[jax.ShapeDtypeStruct((s, IN_W), BF16)],
        scratch_shapes=[pltpu.VMEM((HEADS, HEAD_DIM, HEAD_DIM), F32)],
        sem=("arbitrary",), operands=(q, k, v, dret, dgp, states, mask, qd, kd, cosf, sinf), riders=riders,
        after=after,
    )


def _dx(dz1, dproj, win4, ts, riders=(), after=()):
    s = dz1.shape[0]

    def body(dz1_ref, dp_ref, w_ref, dx_ref):
        acc = ALPHA * dz1_ref[...]
        for j in range(N_SHARD):
            acc = acc + _dot_nt(dp_ref[:, j * IN_SH:(j + 1) * IN_SH], w_ref[j])
        dx_ref[...] = acc

    tile = lambda w: pl.BlockSpec((ts, w), lambda i: (i, 0))
    return _call(
        body, name="dx", grid=(s // ts,),
        in_specs=[tile(D_MODEL), tile(IN_W), _whole()],
        out_specs=[tile(D_MODEL)],
        out_shape=[jax.ShapeDtypeStruct((s, D_MODEL), F32)],
        sem=("arbitrary",), operands=(dz1, dproj, win4), riders=riders, after=after,
    )


def _wgrad(a, b, tm, tn, name, stacked, m_outer, riders=(), after=()):
    s, m = a.shape
    n = b.shape[1]

    def body(a_ref, b_ref, o32_ref, o16_ref):
        res = _dot_tn(a_ref[...], b_ref[...])
        o32_ref[...] = res.reshape(o32_ref.shape)
        o16_ref[...] = res.astype(BF16).reshape(o16_ref.shape)

    if m_outer:
        grid, blocks = (m // tm, n // tn), (lambda g0, g1: (g0, g1))
    else:
        grid, blocks = (n // tn, m // tm), (lambda g0, g1: (g1, g0))
    if stacked:
        shape = (n // tn, m, tn)
        ospec = pl.BlockSpec((1, tm, tn), lambda g0, g1: (blocks(g0, g1)[1], blocks(g0, g1)[0], 0))
    else:
        shape = (m, n)
        ospec = pl.BlockSpec((tm, tn), lambda g0, g1: blocks(g0, g1))
    return _call(
        body, name=name, grid=grid,
        in_specs=[pl.BlockSpec((s, tm), lambda g0, g1: (0, blocks(g0, g1)[0])),
                  pl.BlockSpec((s, tn), lambda g0, g1: (0, blocks(g0, g1)[1]))],
        out_specs=[ospec, ospec],
        out_shape=[jax.ShapeDtypeStruct(shape, F32), jax.ShapeDtypeStruct(shape, BF16)],
        sem=("arbitrary", "arbitrary"), operands=(a, b), riders=riders, after=after,
    )


def _wgrad_send(a, b, tn, name, barrier_id, after=()):
    s, m = a.shape
    n = b.shape[1]
    nb, hm = n // tn, m // 2

    def body(*refs):
        a_ref, b_ref = refs[:2]
        o32_ref, land_ref, send_scr, send_sems, recv_sems = refs[2 + len(after):]
        j = pl.program_id(0)
        x, y, c = _mesh_pos()

        @pl.when(j == 0)
        def _():
            _shake_hands("sibling")

        o32_ref[0] = _dot_tn(a_ref[...], b_ref[...])
        theirs = pl.ds(pl.multiple_of((1 - c) * hm, 16), hm)
        copies = [pltpu.make_async_remote_copy(
            src_ref=send_scr.at[blk], dst_ref=land_ref.at[blk], send_sem=send_sems.at[blk],
            recv_sem=recv_sems.at[blk], device_id=(x, y, 1 - c), device_id_type=MESH) for blk in range(nb)]
        for blk in range(nb):
            @pl.when(j == blk)
            def _(blk=blk):
                send_scr[blk] = o32_ref[0, theirs, :].astype(BF16)
                copies[blk].start()

        @pl.when(j == nb - 1)
        def _():
            for cp in copies:
                cp.wait()

    return pl.pallas_call(
        body, name=name, grid=(nb,),
        in_specs=[pl.BlockSpec((s, m), lambda j: (0, 0)), pl.BlockSpec((s, tn), lambda j: (0, j))]
        + [_whole()] * len(after),
        out_specs=[pl.BlockSpec((1, m, tn), lambda j: (j, 0, 0)), HBM_SPEC],
        out_shape=[jax.ShapeDtypeStruct((nb, m, tn), F32), jax.ShapeDtypeStruct((nb, hm, tn), BF16)],
        scratch_shapes=[pltpu.VMEM((nb, hm, tn), BF16), pltpu.SemaphoreType.DMA((nb,)),
                        pltpu.SemaphoreType.DMA((nb,))],
        compiler_params=pltpu.CompilerParams(dimension_semantics=("arbitrary",), vmem_limit_bytes=VMEM_LIMIT,
                                             collective_id=barrier_id),
    )(a, b, *after)


class _NoComm:
    def __init__(self, win4, wout, wup4, wdown):
        self.weights = dict(w_in=win4, w_out=wout, w_up=wup4, w_down=wdown)
        self.grads = {}

    def weight(self, name):
        return self.weights[name]

    def riders(self, call):
        return ()

    def after(self, call):
        return ()

    def landed(self, call, results, outs):
        pass

    def small_gradients(self, loss, small):
        pass

    def gradient(self, name, g32, g16):
        self.grads[name] = (g32, g16)

    def wgrad_in(self, xb, dproj):
        (g32, g16), _ = _wgrad(xb, dproj, D_MODEL, IN_SH, "wgrad_in", True, True)
        self.gradient("w_in", g32, g16)


def _local_step(x, target, cw, cb, wpool_b, pscale, g1, b1, g2, b2, comm):
    s = x.shape[0]
    ts_a = min(512, s)
    ts_f = min(256, s)
    mask, qd, kd = _decay_tables()
    cosf, sinf = _rope_tables(s)

    def run(call, fn, *args):
        outs, res = fn(*args, riders=comm.riders(call), after=comm.after(call))
        comm.landed(call, res, outs)
        return outs

    xb, q, k, v, g, pooled, cat = run("proj_pool", _proj_pool, x, comm.weight("w_in"), cosf, sinf, wpool_b,
                                      pscale, ts_a)
    ret, cat, states = run("retention_fwd", _retention_fwd, q, k, v, g, cat, mask, qd, kd)
    wout = comm.weight("w_out")
    xhat1, rstd1, h1b = run("outproj_ln1", _outproj_ln1, x, cat, wout, g1, b1, ts_a)
    wup4, wdown = comm.weight("w_up"), comm.weight("w_down")
    ub, act, sd, dz2, dz2b, loss, dg2, db2 = _ffn_fwd_loss(xhat1, h1b, target, wup4, wdown, cw, cb, g1, b1, g2, b2,
                                                           ts_f)

    dub, dz1, dz1b, dg1, db1, dcw, dcb = _ffn_bwd(dz2, dz2b, ub, sd, xhat1, rstd1, wup4, wdown, cw, g1, ts_f)
    half = D_MODEL // 2
    comm.gradient("w_up", *run("wgrad_up", _wgrad, h1b, dub, half, UP_SH, "wgrad_up", True, False))
    comm.gradient("w_out", *run("wgrad_out", _wgrad, cat, dz1b, D_MODEL, half, "wgrad_out", False, True))
    comm.gradient("w_down", *run("wgrad_down", _wgrad, act, dz2b, D_FF // 2, half, "wgrad_down", False, True))
    dret, dgp, dwp, dps = run("mix_bwd", _mix_bwd, dz1b, pooled, ret, g, wout, wpool_b, pscale, ts_a)
    small = dict(w_pool=dwp, pool_scale=dps, ln1_g=dg1, ln1_b=db1, conv_w=dcw, conv_b=dcb,
                 ln2_g=dg2, ln2_b=db2)
    comm.small_gradients(loss, small)
    mask_both = jnp.stack([mask, jnp.swapaxes(mask, 1, 2)])
    dproj, = run("retention_bwd", _retention_bwd, q, k, v, dret, dgp, states, mask_both, qd, kd, cosf, sinf)
    comm.wgrad_in(xb, dproj)
    (grad_x,), _ = _dx(dz1, dproj, comm.weight("w_in"), ts_a, after=comm.after("dx"))
    return loss, grad_x, small


CAST_ROWS = 64
SHARD_SHAPES = ((D_MODEL, IN_SH), (OUT_SH, D_MODEL), (D_MODEL, UP_SH), (DOWN_SH, D_MODEL))
N_BIG = len(SHARD_SHAPES)
CW_SHARD = (3, 1, DOWN_SH)


def _mesh_pos():
    return lax.axis_index("x"), lax.axis_index("y"), lax.axis_index("c")


def _other_chips(x, y):
    return [(1 - x, y), (x, 1 - y), (1 - x, 1 - y)]


def _shake_hands(peers):
    x, y, c = _mesh_pos()
    others = [(x, y, 1 - c)] if peers in ("sibling", "both") else []
    if peers in ("chips", "both"):
        others += [(chip[0], chip[1], c) for chip in _other_chips(x, y)]
    barrier = pltpu.get_barrier_semaphore()
    for peer in others:
        pl.semaphore_signal(barrier, inc=1, device_id=peer, device_id_type=MESH)
    pl.semaphore_wait(barrier, len(others))


def _half_rows(w, which):
    hr = SHARD_SHAPES[w][0] // 2
    return pl.ds(pl.multiple_of(which * hr, 16), hr)


def _gather_weights(shards, cw_shard, wpool, full):
    def body(*refs):
        in_refs = refs[:N_BIG]
        cw_ref, wpool_ref = refs[N_BIG:N_BIG + 2]
        out_refs = refs[N_BIG + 2:2 * N_BIG + 2]
        cwo_ref, wpool_b_ref = refs[2 * N_BIG + 2:2 * N_BIG + 4]
        stage = refs[2 * N_BIG + 4:3 * N_BIG + 4]
        raw = refs[3 * N_BIG + 4:4 * N_BIG + 4 - len(full)]
        send_sems, recv_sems, fsend_sems, frecv_sems, cw_send, cw_recv, local_sems, load_sems = \
            refs[4 * N_BIG + 4 - len(full):]
        x, y, c = _mesh_pos()
        j0 = 2 * x + y
        chips = _other_chips(x, y)

        fetched = [w for w in range(N_BIG) if w not in full]
        f32 = {w: in_refs[w] for w in full}
        loads = []
        for n, w in enumerate(fetched):
            f32[w] = raw[n]
            loads.append(pltpu.make_async_copy(in_refs[w], raw[n], load_sems.at[n]))
            loads[-1].start()

        def cast_to_stage(w):
            def cast(i, carry):
                rows = pl.ds(pl.multiple_of(i * CAST_ROWS, CAST_ROWS), CAST_ROWS)
                stage[w][rows, :] = f32[w][rows, :].astype(BF16)
                return carry
            lax.fori_loop(0, SHARD_SHAPES[w][0] // CAST_ROWS, cast, 0)

        for w in full:
            cast_to_stage(w)

        jx, jy, jd = 2 * (1 - x) + y, 2 * x + (1 - y), 2 * (1 - x) + (1 - y)
        neighbours = [((1 - x, y, c), jx), ((x, 1 - y, c), jy)]
        passed = jnp.where(c == 0, jx, jy)
        pass_to = (jnp.where(c == 0, x, 1 - x), jnp.where(c == 0, 1 - y, y), c)

        def nbr(w, k, block):
            return pltpu.make_async_remote_copy(
                src_ref=stage[w].at[_half_rows(w, c), :], dst_ref=out_refs[w].at[block, _half_rows(w, c), :],
                send_sem=send_sems.at[w, k], recv_sem=recv_sems.at[w, k],
                device_id=neighbours[k][0], device_id_type=MESH)

        def relay(w, block):
            return pltpu.make_async_remote_copy(
                src_ref=out_refs[w].at[passed, _half_rows(w, c), :],
                dst_ref=out_refs[w].at[block, _half_rows(w, c), :],
                send_sem=send_sems.at[w, 2], recv_sem=recv_sems.at[w, 2],
                device_id=pass_to, device_id_type=MESH)

        def d2d(w, k, block, half):
            return pltpu.make_async_remote_copy(
                src_ref=out_refs[w].at[block, _half_rows(w, half), :],
                dst_ref=out_refs[w].at[block, _half_rows(w, half), :],
                send_sem=fsend_sems.at[w, k], recv_sem=frecv_sems.at[w, k],
                device_id=(x, y, 1 - c), device_id_type=MESH)

        def conv(k, block):
            chip = chips[k]
            return pltpu.make_async_remote_copy(
                src_ref=cw_ref, dst_ref=cwo_ref.at[block], send_sem=cw_send.at[k], recv_sem=cw_recv.at[k],
                device_id=(chip[0], chip[1], c), device_id_type=MESH)

        sent = [nbr(w, k, j0) for w in full for k in range(2)] + [conv(k, j0) for k in range(3)]
        for cp in sent:
            cp.start()
        for n, w in enumerate(fetched):
            loads[n].wait()
            cast_to_stage(w)
        local = [pltpu.make_async_copy(stage[w], out_refs[w].at[j0], local_sems.at[w]) for w in range(N_BIG)]
        local.append(pltpu.make_async_copy(cw_ref, cwo_ref.at[j0], local_sems.at[N_BIG]))
        for cp in local:
            cp.start()
        wpool_b_ref[...] = wpool_ref[...].astype(BF16)
        for w in full:
            for k, (_, block) in enumerate(neighbours):
                nbr(w, k, block).wait_recv()
            later = [relay(w, passed)] + [d2d(w, k, block, c) for k, (_, block) in enumerate(neighbours)]
            for cp in later:
                cp.start()
            sent += later
        for w in full:
            relay(w, jd).wait_recv()
            fw = d2d(w, 2, jd, c)
            fw.start()
            sent.append(fw)
        for w in full:
            for k, block in enumerate([jx, jy, jd]):
                d2d(w, k, block, 1 - c).wait_recv()
        for k, chip in enumerate(chips):
            conv(k, 2 * chip[0] + chip[1]).wait_recv()
        for cp in sent:
            cp.wait_send()
        for cp in local:
            cp.wait()

    out_shape = [jax.ShapeDtypeStruct((N_SHARD,) + shp, BF16) for shp in SHARD_SHAPES]
    out_shape.append(jax.ShapeDtypeStruct((N_SHARD,) + CW_SHARD, F32))
    out_shape.append(jax.ShapeDtypeStruct(wpool.shape, BF16))
    return pl.pallas_call(
        body, name="gather_weights",
        in_specs=[_whole() if w in full else HBM_SPEC for w in range(N_BIG)] + [_whole()] * 2,
        out_specs=[HBM_SPEC] * (N_BIG + 1) + [_whole()],
        out_shape=out_shape,
        scratch_shapes=[pltpu.VMEM(shp, BF16) for shp in SHARD_SHAPES]
        + [pltpu.VMEM(shp, F32) for w, shp in enumerate(SHARD_SHAPES) if w not in full] + [
            pltpu.SemaphoreType.DMA((N_BIG, 3)), pltpu.SemaphoreType.DMA((N_BIG, 3)),
            pltpu.SemaphoreType.DMA((N_BIG, 3)), pltpu.SemaphoreType.DMA((N_BIG, 3)),
            pltpu.SemaphoreType.DMA((3,)), pltpu.SemaphoreType.DMA((3,)),
            pltpu.SemaphoreType.DMA((N_BIG + 1,)), pltpu.SemaphoreType.DMA((N_BIG - len(full),))],
        compiler_params=pltpu.CompilerParams(vmem_limit_bytes=VMEM_LIMIT),
    )(*shards, cw_shard, wpool)


def _gather_rider(arrays, ops, handshake=None):
    ws = sorted(arrays)

    def make(inplace, srcs, lands, send_sems, recv_sems):
        del srcs, lands
        x, y, c = _mesh_pos()
        j0, jx, jy, jd = 2 * x + y, 2 * (1 - x) + y, 2 * x + (1 - y), 2 * (1 - x) + (1 - y)
        x_nbr, y_nbr, sibling = (1 - x, y, c), (x, 1 - y, c), (x, y, 1 - c)
        starts, waits = [], []
        for n, (kind, w, (r0, nr)) in enumerate(ops):
            ref = inplace[ws.index(w)]
            hr = SHARD_SHAPES[w][0] // 2
            rows = lambda core: pl.ds(pl.multiple_of(core * hr + r0, 16), nr)
            mine, theirs = rows(c), rows(1 - c)
            if kind == "ici":
                moves = [(ref.at[j0, mine, :], x_nbr, ref.at[jx, mine, :]),
                         (ref.at[j0, mine, :], y_nbr, ref.at[jy, mine, :]),
                         (ref.at[j0, mine, :], (1 - x, 1 - y, c), ref.at[jd, mine, :])]
            elif kind == "nbr":
                moves = [(ref.at[j0, mine, :], x_nbr, ref.at[jx, mine, :]),
                         (ref.at[j0, mine, :], y_nbr, ref.at[jy, mine, :])]
            elif kind == "relay":
                passed = jnp.where(c == 0, jx, jy)
                to = (jnp.where(c == 0, x, 1 - x), jnp.where(c == 0, 1 - y, y), c)
                moves = [(ref.at[passed, mine, :], to, ref.at[jd, mine, :])]
            else:
                blocks = dict(d2d=[jx, jy, jd], d2d_nbr=[jx, jy], d2d_diag=[jd])[kind]
                moves = [(ref.at[b, mine, :], sibling, ref.at[b, theirs, :]) for b in blocks]
            for k, (src, to, landing) in enumerate(moves):
                sems = dict(send_sem=send_sems.at[3 * n + k], recv_sem=recv_sems.at[3 * n + k],
                            device_id=to, device_id_type=MESH)
                send = pltpu.make_async_remote_copy(src_ref=src, dst_ref=src, **sems)
                arrival = pltpu.make_async_remote_copy(src_ref=src, dst_ref=landing, **sems)
                starts.append(send)
                waits += [arrival.wait_recv, send.wait_send]
        return starts, waits

    return _Rider([arrays[w] for w in ws], [], [], 3 * len(ops), make, handshake)


def _whole_half(w):
    return (0, SHARD_SHAPES[w][0] // 2)


def _pair_rider(ws, g16s):
    def make(inplace, srcs, lands, send_sems, recv_sems):
        del inplace
        x, y, c = _mesh_pos()
        copies = [pltpu.make_async_remote_copy(
            src_ref=srcs[i].at[:, _half_rows(w, 1 - c), :], dst_ref=lands[i],
            send_sem=send_sems.at[i], recv_sem=recv_sems.at[i], device_id=(x, y, 1 - c), device_id_type=MESH)
            for i, w in enumerate(ws)]
        return copies, [cp.wait for cp in copies]

    lands = [jax.ShapeDtypeStruct((N_SHARD, SHARD_SHAPES[w][0] // 2, SHARD_SHAPES[w][1]), BF16) for w in ws]
    return _Rider([], g16s, lands, len(ws), make)


def _chip_rider(ws, p16s):
    def make(inplace, srcs, lands, send_sems, recv_sems):
        del inplace
        x, y, c = _mesh_pos()
        copies = []
        for i in range(len(ws)):
            for k, chip in enumerate(_other_chips(x, y)):
                copies.append(pltpu.make_async_remote_copy(
                    src_ref=srcs[i].at[2 * chip[0] + chip[1]], dst_ref=lands[i].at[k],
                    send_sem=send_sems.at[3 * i + k], recv_sem=recv_sems.at[3 * i + k],
                    device_id=(chip[0], chip[1], c), device_id_type=MESH))
        return copies, [cp.wait for cp in copies]

    lands = [jax.ShapeDtypeStruct((3, SHARD_SHAPES[w][0] // 2, SHARD_SHAPES[w][1]), BF16) for w in ws]
    return _Rider([], p16s, lands, 3 * len(ws), make)


def _final_rider(halves):
    def make(inplace, srcs, lands, send_sems, recv_sems):
        del inplace
        x, y, c = _mesh_pos()
        copies = [pltpu.make_async_remote_copy(
            src_ref=srcs[i], dst_ref=lands[i], send_sem=send_sems.at[i], recv_sem=recv_sems.at[i],
            device_id=(x, y, 1 - c), device_id_type=MESH) for i in range(len(halves))]
        return copies, [cp.wait for cp in copies]

    return _Rider([], halves, [jax.ShapeDtypeStruct(h.shape, h.dtype) for h in halves], len(halves), make)


def _comm_only(name, riders):
    _, res = _call(lambda: None, name=name, grid=(), in_specs=[], out_specs=[], out_shape=[], operands=(),
                   riders=riders)
    return res


class _SemList:
    def __init__(self, refs):
        self.at = list(refs)


def _merged_rider(riders):
    srcs = [a for r in riders for a in r.srcs]
    lands = [a for r in riders for a in r.lands]

    def make(inplace, src_refs, land_refs, send_sems, recv_sems):
        starts, waits = [], []
        s0 = l0 = c0 = 0
        for r in riders:
            part = r.make(inplace, src_refs[s0:s0 + len(r.srcs)], land_refs[l0:l0 + len(r.lands)],
                          _SemList(send_sems.at[c0:c0 + r.n_copies]), _SemList(recv_sems.at[c0:c0 + r.n_copies]))
            starts += part[0]
            waits += part[1]
            s0, l0, c0 = s0 + len(r.srcs), l0 + len(r.lands), c0 + r.n_copies
        return starts, waits

    return _Rider([], srcs, lands, sum(r.n_copies for r in riders), make)


def _split_start(name, rider, handshake=None):
    assert not rider.inplace
    ns, nl, n = len(rider.srcs), len(rider.lands), rider.n_copies
    barrier_id, peers = handshake if handshake is not None else (None, None)

    def body(*refs):
        if handshake is not None:
            _shake_hands(peers)
        srcs, lands = refs[:ns], refs[ns:ns + nl]
        sems = refs[ns + nl:ns + nl + 2 * n]
        token = refs[-1]
        starts, _ = rider.make([], srcs, lands, _SemList(sems[:n]), _SemList(sems[n:]))
        for cp in starts:
            cp.start()
        token[...] = jnp.zeros_like(token)

    buffers = [pltpu.with_memory_space_constraint(a, pltpu.HBM) for a in rider.srcs]
    buffers += [pltpu.with_memory_space_constraint(lax.empty(s.shape, s.dtype), pltpu.HBM) for s in rider.lands]
    hbm = pl.BlockSpec(memory_space=pltpu.HBM)
    sem = pl.BlockSpec(memory_space=pltpu.SEMAPHORE)
    outs = pl.pallas_call(
        body, name=name,
        out_shape=tuple([pltpu.SemaphoreType.DMA(())] * (2 * n) + [pltpu.HBM(b.shape, b.dtype) for b in buffers]
                        + [jax.ShapeDtypeStruct((8, 128), F32)]),
        in_specs=[hbm] * (ns + nl),
        out_specs=tuple([sem] * (2 * n) + [hbm] * (ns + nl) + [_whole()]),
        input_output_aliases={i: 2 * n + i for i in range(ns + nl)},
        compiler_params=pltpu.CompilerParams(has_side_effects=pltpu.SideEffectType.DATAFLOW_SIDE_EFFECTING,
                                             collective_id=barrier_id),
    )(*buffers)
    return (rider, outs[:2 * n], outs[2 * n:2 * n + ns + nl]), outs[-1]


def _split_parts(state, riders):
    merged, sems, buffers = state
    n, ns = merged.n_copies, len(merged.srcs)
    parts, s0, l0, c0 = [], 0, 0, 0
    for r in riders:
        parts.append((r, list(sems[c0:c0 + r.n_copies]) + list(sems[n + c0:n + c0 + r.n_copies]),
                      list(buffers[s0:s0 + len(r.srcs)]) + list(buffers[ns + l0:ns + l0 + len(r.lands)])))
        s0, l0, c0 = s0 + len(r.srcs), l0 + len(r.lands), c0 + r.n_copies
    return parts


def _split_wait(name, state, after):
    rider, sems, buffers = state
    ns, nl, n = len(rider.srcs), len(rider.lands), rider.n_copies

    def body(*refs):
        srcs, lands = refs[:ns], refs[ns:ns + nl]
        sem_refs = refs[ns + nl:ns + nl + 2 * n]
        _, waits = rider.make([], srcs, lands, _SemList(sem_refs[:n]), _SemList(sem_refs[n:]))
        for wait in waits:
            wait()

    hbm = pl.BlockSpec(memory_space=pltpu.HBM)
    sem = pl.BlockSpec(memory_space=pltpu.SEMAPHORE)
    outs = pl.pallas_call(
        body, name=name,
        out_shape=tuple(pltpu.HBM(b.shape, b.dtype) for b in buffers),
        in_specs=[hbm] * (ns + nl) + [sem] * (2 * n) + [HBM_SPEC],
        out_specs=tuple([hbm] * (ns + nl)),
        input_output_aliases={i: i for i in range(ns + nl)},
        compiler_params=pltpu.CompilerParams(has_side_effects=pltpu.SideEffectType.DATAFLOW_SIDE_EFFECTING),
    )(*buffers, *sems, after)
    return list(outs[:ns]), list(outs[ns:])


def _pair_sum(pos, ws, g32s, recvs):
    n = len(ws)

    def body(pos_ref, *refs):
        del pos_ref
        g_refs, r_refs = refs[:n], refs[n:2 * n]
        p32_refs, p16_refs = refs[2 * n:3 * n], refs[3 * n:]
        x, y, _ = _mesh_pos()
        for i in range(n):
            tot = g_refs[i][...] + r_refs[i][...].astype(F32)
            p16_refs[i][...] = tot.astype(BF16)

            @pl.when(pl.program_id(0) == 2 * x + y)
            def _(i=i, tot=tot):
                p32_refs[i][...] = tot

    halves = [(SHARD_SHAPES[w][0] // 2, SHARD_SHAPES[w][1]) for w in ws]
    own = [pl.BlockSpec((None, None) + h, lambda j, pos_ref: (j, pos_ref[0], 0, 0)) for h in halves]
    blk = [pl.BlockSpec((None,) + h, lambda j, pos_ref: (j, 0, 0)) for h in halves]
    mine = [pl.BlockSpec(h, lambda j, pos_ref: (0, 0)) for h in halves]
    g4 = [g.reshape((N_SHARD, 2) + h) for g, h in zip(g32s, halves)]
    outs = pl.pallas_call(
        body, name="pair_sum_" + "_".join(str(w) for w in ws),
        grid_spec=pltpu.PrefetchScalarGridSpec(
            num_scalar_prefetch=1, grid=(N_SHARD,), in_specs=own + blk, out_specs=mine + blk),
        out_shape=[jax.ShapeDtypeStruct(h, F32) for h in halves]
        + [jax.ShapeDtypeStruct((N_SHARD,) + h, BF16) for h in halves],
        compiler_params=_params(("arbitrary",)),
    )(pos, *g4, *recvs)
    return outs[:n], outs[n:]


def _chip_sum(p32s, recvs):
    parts = 2

    def body(*refs):
        p_refs, r_refs, f_refs = refs[:N_BIG], refs[N_BIG:2 * N_BIG], refs[2 * N_BIG:]
        for w in range(N_BIG):
            f_refs[w][...] = ((p_refs[w][...] + r_refs[w][0].astype(F32)) + r_refs[w][1].astype(F32)) \
                + r_refs[w][2].astype(F32)

    quarters = [(r // 2 // parts, cc) for r, cc in SHARD_SHAPES]
    own = [pl.BlockSpec(qt, lambda i: (i, 0)) for qt in quarters]
    rcv = [pl.BlockSpec((3,) + qt, lambda i: (0, i, 0)) for qt in quarters]
    out = [pl.BlockSpec(qt, lambda i: (i, 0)) for qt in quarters]
    return pl.pallas_call(
        body, name="chip_sum", grid=(parts,), in_specs=own + rcv, out_specs=out,
        out_shape=[jax.ShapeDtypeStruct((r // 2, cc), F32) for r, cc in SHARD_SHAPES],
        compiler_params=_params(("arbitrary",)),
    )(*p32s, *recvs)


def _adamw(w, g, m, v):
    m_new = ADAM_B1 * m + (1.0 - ADAM_B1) * g
    v_new = ADAM_B2 * v + (1.0 - ADAM_B2) * (g * g)
    m_hat = m_new / (1.0 - ADAM_B1 ** ADAM_STEP)
    v_hat = v_new / (1.0 - ADAM_B2 ** ADAM_STEP)
    delta = -ADAM_LR * (m_hat / (jnp.sqrt(v_hat) + ADAM_EPS) + ADAM_WD * w)
    return delta, m_new, v_new


def _adam_half(name, pos, grads, ws, ms, vs, into=None):
    nb = 4
    which = (lambda ref: ref[0]) if into is None else (lambda ref: 1 - ref[0])

    def body(which_ref, *refs):
        del which_ref
        groups = [refs[i * N_BIG:(i + 1) * N_BIG] for i in range(4)]
        g_refs, w_refs, m_refs, v_refs = groups
        go_refs, do_refs, mo_refs, vo_refs = [refs[len(refs) - (4 - i) * N_BIG:len(refs) - (3 - i) * N_BIG]
                                              for i in range(4)]
        for w in range(N_BIG):
            g = g_refs[w][...]
            delta, m_new, v_new = _adamw(w_refs[w][...], g, m_refs[w][...], v_refs[w][...])
            go_refs[w][...] = g
            do_refs[w][...] = delta
            mo_refs[w][...] = m_new
            vo_refs[w][...] = v_new

    blocks = [(r // 2 // nb, cc) for r, cc in SHARD_SHAPES]
    half = [pl.BlockSpec(b, lambda i, which_ref: (i, 0)) for b in blocks]
    full = [pl.BlockSpec((None,) + b, lambda i, which_ref: (0, which(which_ref) * nb + i, 0)) for b in blocks]
    shapes = [jax.ShapeDtypeStruct((1,) + shp, F32) for shp in SHARD_SHAPES]
    carried = [] if into is None else [a for kind in into for a in kind]
    first = 1 + 4 * N_BIG
    outs = pl.pallas_call(
        body, name=name,
        grid_spec=pltpu.PrefetchScalarGridSpec(
            num_scalar_prefetch=1, grid=(nb,), in_specs=half + full * 3 + [HBM_SPEC] * len(carried),
            out_specs=full * 4),
        out_shape=shapes * 4,
        input_output_aliases={first + i: i for i in range(len(carried))},
        compiler_params=_params(("arbitrary",)),
    )(pos, *grads, *ws, *ms, *vs, *carried)
    return [outs[i * N_BIG:(i + 1) * N_BIG] for i in range(4)]


SMALL_ROWS = 8
ROW_CONV_B, ROW_POOL_SCALE, ROW_LN1_G, ROW_LN1_B, ROW_LN2_G, ROW_LN2_B, ROW_LOSS = range(7)
SMALL_VECS = ((ROW_CONV_B, D_FF), (ROW_POOL_SCALE, POOL_W), (ROW_LN1_G, D_MODEL), (ROW_LN1_B, D_MODEL),
              (ROW_LN2_G, D_MODEL), (ROW_LN2_B, D_MODEL))


def _small_pack(loss, vec_grads):
    def body(*refs):
        loss_ref, gvec, out_ref = refs[0], refs[1:-1], refs[-1]
        out_ref[...] = jnp.zeros_like(out_ref)
        for (row, n), ref in zip(SMALL_VECS, gvec):
            out_ref[row:row + 1, 0:n] = ref[...]
        out_ref[ROW_LOSS:ROW_LOSS + 1, 0:HEAD_DIM] = jnp.broadcast_to(loss_ref[...], (1, HEAD_DIM))

    return pl.pallas_call(
        body, name="small_pack", in_specs=[_whole()] * (1 + len(vec_grads)), out_specs=_whole(),
        out_shape=jax.ShapeDtypeStruct((SMALL_ROWS, D_FF), F32),
    )(loss, *vec_grads)


def _small_pair_sum(own, sibling):
    n = len(own)

    def body(*refs):
        x, y, _ = _mesh_pos()
        for i in range(n):
            refs[2 * n + i][2 * x + y] = refs[i][...] + refs[n + i][...]

    return pl.pallas_call(
        body, name="small_pair_sum", in_specs=[_whole()] * (2 * n), out_specs=[_whole()] * n,
        out_shape=[jax.ShapeDtypeStruct((N_SHARD,) + a.shape, F32) for a in own],
        compiler_params=pltpu.CompilerParams(vmem_limit_bytes=VMEM_LIMIT),
    )(*own, *sibling)


def _small_chip_rider(gathered):
    n = len(gathered)

    def make(inplace, srcs, lands, send_sems, recv_sems):
        del inplace, lands
        x, y, c = _mesh_pos()
        j0 = 2 * x + y
        starts, waits = [], []
        for i in range(n):
            for k, chip in enumerate(_other_chips(x, y)):
                sems = dict(send_sem=send_sems.at[3 * i + k], recv_sem=recv_sems.at[3 * i + k],
                            device_id=(chip[0], chip[1], c), device_id_type=MESH)
                send = pltpu.make_async_remote_copy(src_ref=srcs[i].at[j0], dst_ref=srcs[i].at[j0], **sems)
                arrival = pltpu.make_async_remote_copy(
                    src_ref=srcs[i].at[j0], dst_ref=srcs[i].at[2 * chip[0] + chip[1]], **sems)
                starts.append(send)
                waits += [arrival.wait_recv, send.wait_send]
        return starts, waits

    return _Rider([], gathered, [], 3 * n, make)


def _small_adam(all_a, all_b, all_c, wp, cwp, vec_ws, m_wp, m_cwp, vec_ms, v_wp, v_cwp, vec_vs):
    nv = len(SMALL_VECS)
    np_ = 2 + nv

    def body(*refs):
        all_a_ref, all_b_ref, all_c_ref = refs[0:3]
        w_all, m_all, v_all = (refs[3 + i * np_:3 + (i + 1) * np_] for i in range(3))
        loss_out = refs[3 + 3 * np_]
        outs = refs[4 + 3 * np_:]
        x, y, _ = _mesh_pos()
        j0 = 2 * x + y
        tot_a = ((all_a_ref[0] + all_a_ref[1]) + all_a_ref[2]) + all_a_ref[3]
        tot_b = ((all_b_ref[0] + all_b_ref[1]) + all_b_ref[2]) + all_b_ref[3]
        tot_c = ((all_c_ref[0, j0] + all_c_ref[1, j0]) + all_c_ref[2, j0]) + all_c_ref[3, j0]
        loss_out[...] = tot_b[ROW_LOSS:ROW_LOSS + 1, 0:1]
        grads = [tot_a, tot_c] + [tot_b[row:row + 1, 0:n] for row, n in SMALL_VECS]
        for p in range(np_):
            for at, g in ([(j, tot_c[j:j + 1]) for j in range(3)] if p == 1 else [(Ellipsis, grads[p])]):
                delta, m_new, v_new = _adamw(w_all[p][at], g, m_all[p][at], v_all[p][at])
                outs[p][at] = g
                outs[np_ + p][at] = delta
                outs[2 * np_ + p][at] = m_new
                outs[3 * np_ + p][at] = v_new

    pshapes = [wp.shape, CW_SHARD] + [wv.shape for wv in vec_ws]
    out_shape = [jax.ShapeDtypeStruct((1, 1), F32)] + [jax.ShapeDtypeStruct(s, F32) for s in pshapes] * 4
    outs = pl.pallas_call(
        body, name="small_adam",
        in_specs=[_whole()] * (3 + 3 * np_), out_specs=[_whole()] * len(out_shape), out_shape=out_shape,
        compiler_params=pltpu.CompilerParams(vmem_limit_bytes=VMEM_LIMIT),
    )(all_a, all_b, all_c, wp, cwp, *vec_ws, m_wp, m_cwp, *vec_ms, v_wp, v_cwp, *vec_vs)
    return outs[0], [outs[1 + i * np_:1 + (i + 1) * np_] for i in range(4)]


def kernel(x, w_in, w_pool, pool_scale, w_out, ln1_g, ln1_b, w_up, conv_w, conv_b, w_down, ln2_g, ln2_b, loss_target, m_w_in, m_w_pool, m_pool_scale, m_w_out, m_ln1_g, m_ln1_b, m_w_up, m_conv_w, m_conv_b, m_w_down, m_ln2_g, m_ln2_b, v_w_in, v_w_pool, v_pool_scale, v_w_out, v_ln1_g, v_ln1_b, v_w_up, v_conv_w, v_conv_b, v_w_down, v_ln2_g, v_ln2_b):
    pos = lax.axis_index("c").astype(jnp.int32).reshape(1)
    order = ("w_in", "w_out", "w_up", "w_down")
    w_in_i, w_out_i, w_up_i, w_down_i = range(N_BIG)
    vec_names = ("conv_b", "pool_scale", "ln1_g", "ln1_b", "ln2_g", "ln2_b")

    taps_first = lambda a: jnp.transpose(a, (1, 0, 2))
    gathered = _gather_weights([w_in[0], w_out[0], w_up[0], w_down[0]], taps_first(conv_w), w_pool[0], (w_in_i,))
    cw_full = jnp.transpose(gathered[N_BIG].reshape(N_SHARD, 3, DOWN_SH), (1, 0, 2)).reshape(3, D_FF)
    up_a, up_b, up_c = (0, 176), (176, 176), (352, 160)
    assert up_c[0] + up_c[1] == SHARD_SHAPES[w_up_i][0] // 2

    class MeshComm:
        def __init__(self):
            self.w = {i: gathered[i] for i in range(N_BIG)}
            self.g32, self.g16, self.p32, self.p16, self.recv_b = {}, {}, {}, {}, {}
            self.up_complete = False
            self.tokens, self.chips = {}, []

        def weight(self, name):
            i = order.index(name)
            if name == "w_up" and not self.up_complete:
                (arrs, _), = _comm_only("gather_up_last", [_gather_rider(
                    {i: self.w[i]}, [("d2d_diag", i, up_b), ("d2d", i, up_c)], (12, "sibling"))])
                self.w[i], self.up_complete = arrs[0], True
            full = self.w[i]
            return full.reshape(-1, full.shape[-1]) if name in ("w_out", "w_down") else full

        def _gather(self, ws, ops, handshake):
            return _gather_rider({w: self.w[w] for w in ws}, ops, handshake), ("w", ws)

        def _pair(self, ws):
            return _pair_rider(ws, [self.g16[w] for w in ws]), ("recv_a", ws)

        def _chip(self, ws):
            return _chip_rider(ws, [self.p16[w] for w in ws]), ("recv_b", ws)

        def plan(self, call):
            out_all, down_all = _whole_half(w_out_i), _whole_half(w_down_i)
            if call == "proj_pool":
                return [self._gather([w_out_i, w_up_i, w_down_i],
                                     [("ici", w_out_i, out_all), ("nbr", w_down_i, down_all),
                                      ("nbr", w_up_i, up_a)], (9, "chips"))]
            if call == "retention_fwd":
                return [self._gather([w_out_i, w_up_i, w_down_i],
                                     [("d2d", w_out_i, out_all),
                                      ("relay", w_down_i, down_all), ("d2d_nbr", w_down_i, down_all),
                                      ("relay", w_up_i, up_a), ("d2d_nbr", w_up_i, up_a), ("nbr", w_up_i, up_b)],
                                     (10, "both"))]
            if call == "outproj_ln1":
                return [self._gather([w_up_i, w_down_i],
                                     [("d2d_diag", w_down_i, down_all), ("d2d_diag", w_up_i, up_a),
                                      ("relay", w_up_i, up_b), ("d2d_nbr", w_up_i, up_b), ("ici", w_up_i, up_c)],
                                     (11, "both"))]
            return []

        def after(self, call):
            return tuple(self.tokens.pop(call, ()))

        def riders(self, call):
            self.pending = self.plan(call)
            return [r for r, _ in self.pending]

        def _start(self, name, rider, before, handshake):
            state, token = _split_start(name, rider, handshake)
            self.tokens.setdefault(before, []).append(token)
            return state

        def _finish_pair(self, name, state, ws, after):
            _, lands = _split_wait(name, state, after)
            self._finish_sum(ws, lands)

        def landed(self, call, results, outs):
            for (_, (slot, ws)), (inplace, lands) in zip(self.pending, results):
                for w, arr in zip(ws, inplace if len(inplace) else lands):
                    getattr(self, slot)[w] = arr
            if call == "wgrad_out":
                self._finish_pair("pair_exchange_up_wait", self.pair_up, [w_up_i], outs[1])
                self.chips.append(([w_up_i], self._start(
                    "chip_exchange_up_start", self._chip([w_up_i])[0], "wgrad_down", (5, "chips"))))
            if call == "mix_bwd":
                ws = [w_out_i, w_down_i]
                self._finish_pair("pair_exchange_out_down_wait", self.pair_out_down, ws, outs[0])
            if call == "retention_bwd":
                own, sibling = _split_wait("small_pair_wait", self.small_pair, outs[0])
                self.small_chip = self._start(
                    "small_chip_start", _small_chip_rider(_small_pair_sum(own, sibling)), "wgrad_in",
                    (6, "chips"))

        def small_gradients(self, loss, small):
            dcw4 = jnp.transpose(small["conv_w"].reshape(3, N_SHARD, DOWN_SH), (1, 0, 2))
            own = [small["w_pool"], _small_pack(loss, [small[n] for n in vec_names]), dcw4]
            ws = [w_out_i, w_down_i]
            parts = [self._chip(ws)[0], _final_rider(own)]
            chip, self.small_pair = _split_parts(
                self._start("chip_out_down_small_pair_start", _merged_rider(parts), "retention_bwd",
                            (7, "both")), parts)
            self.chips.append((ws, chip))

        def gradient(self, name, g32, g16):
            w = order.index(name)
            shape = (N_SHARD,) + SHARD_SHAPES[w]
            self.g32[w], self.g16[w] = g32.reshape(shape), g16.reshape(shape)
            if name == "w_up":
                self.pair_up = self._start("pair_exchange_up_start", self._pair([w])[0], "wgrad_out",
                                           (1, "sibling"))
            if name == "w_down":
                self.pair_out_down = self._start("pair_exchange_out_down_start",
                                                 self._pair([w_out_i, w_down_i])[0], "mix_bwd", (2, "sibling"))

        def wgrad_in(self, xb, dproj):
            w = w_in_i
            g32, landed = _wgrad_send(xb, dproj, IN_SH, "wgrad_in", 4, after=self.after("wgrad_in"))
            self.g32[w] = g32
            self._finish_sum([w], [landed])
            self.chips.append(([w], self._start("chip_exchange_in_start", self._chip([w])[0], "dx", (8, "chips"))))

        def _finish_sum(self, ws, lands):
            p32s, p16s = _pair_sum(pos, ws, [self.g32[w] for w in ws], lands)
            for w, p32, p16 in zip(ws, p32s, p16s):
                self.p32[w], self.p16[w] = p32, p16

        def finish(self, after):
            for n, (ws, state) in enumerate(self.chips):
                _, lands = _split_wait("chip_exchange_wait_%d" % n, state, after)
                for w, arr in zip(ws, lands):
                    self.recv_b[w] = arr
            return _split_wait("small_chip_wait", self.small_chip, after)[0]

    comm = MeshComm()
    loss, grad_x, small = _local_step(x[0], loss_target[0], cw_full, conv_b, gathered[N_BIG + 1], pool_scale,
                                      ln1_g, ln1_b, ln2_g, ln2_b, comm)

    given = dict(w_pool=w_pool, pool_scale=pool_scale, ln1_g=ln1_g, ln1_b=ln1_b, conv_w=conv_w, conv_b=conv_b,
                 ln2_g=ln2_g, ln2_b=ln2_b)
    given_m = dict(w_pool=m_w_pool, pool_scale=m_pool_scale, ln1_g=m_ln1_g, ln1_b=m_ln1_b, conv_w=m_conv_w,
                   conv_b=m_conv_b, ln2_g=m_ln2_g, ln2_b=m_ln2_b)
    given_v = dict(w_pool=v_w_pool, pool_scale=v_pool_scale, ln1_g=v_ln1_g, ln1_b=v_ln1_b, conv_w=v_conv_w,
                   conv_b=v_conv_b, ln2_g=v_ln2_g, ln2_b=v_ln2_b)
    args = []
    for src in (given, given_m, given_v):
        args += [src["w_pool"][0], taps_first(src["conv_w"]), [src[n] for n in vec_names]]
    small_sums = comm.finish(grad_x)
    loss_tot, small_out = _small_adam(*small_sums, *args)
    every = range(N_BIG)
    mine = _chip_sum([comm.p32[w] for w in every], [comm.recv_b[w] for w in every])
    final_state, _ = _split_start("pair_exchange_f32_start", _final_rider(mine), (3, "sibling"))
    mine = final_state[2][:N_BIG]
    big = ([w_in, w_out, w_up, w_down], [m_w_in, m_w_out, m_w_up, m_w_down], [v_w_in, v_w_out, v_w_up, v_w_down])
    own_half = _adam_half("adam_own_half", pos, mine, *big)
    _, theirs = _split_wait("pair_exchange_f32_wait", final_state, own_half[0][0])
    big_out = _adam_half("adam_other_half", pos, theirs, *big, into=own_half)

    names = ("w_in", "w_pool", "pool_scale", "w_out", "ln1_g", "ln1_b", "w_up", "conv_w", "conv_b", "w_down",
             "ln2_g", "ln2_b")
    small_names = ("w_pool", "conv_w") + vec_names
    result = [loss_tot.reshape(()), grad_x[None]]
    for kind in range(4):
        for n in names:
            if n in order:
                result.append(big_out[kind][order.index(n)])
            else:
                val = small_out[kind][small_names.index(n)]
                if n == "conv_w":
                    val = taps_first(val)
                elif n == "w_pool":
                    val = val[None]
                result.append(val)
    return tuple(result)
```

```python
import functools

import numpy as np
import jax
import jax.numpy as jnp
from jax import lax
from jax.experimental import pallas as pl
from jax.experimental.pallas import tpu as pltpu

F32 = jnp.float32
BF16 = jnp.bfloat16

D_MODEL = 1024
HEADS = 4
HEAD_DIM = 128
RET_W = HEADS * HEAD_DIM
POOL_WINDOWS = (2, 4, 8, 16)
POOL_W = 512
IN_W = 4 * RET_W + POOL_W
D_FF = 2816
N_SHARD = 4
IN_SH = IN_W // N_SHARD
UP_SH = 2 * D_FF // N_SHARD
DOWN_SH = D_FF // N_SHARD
OUT_SH = D_MODEL // N_SHARD
ROPE_BASE = 10000.0
LN_EPS = 1e-5
RMS_EPS = 1e-6
ALPHA = 2.0 ** 0.25
K_SCALE = HEAD_DIM ** -0.5
SUPER = 256
CHUNK = 64
POOL_HALO = 16
CONV_HALO = 8
FFN_STRIP = 128
LN_ROWS = 32

ADAM_LR = 0.001
ADAM_B1 = 0.9
ADAM_B2 = 0.999
ADAM_EPS = 1e-08
ADAM_WD = 0.01
ADAM_STEP = 10

MESH = pl.DeviceIdType.MESH
VMEM_LIMIT = 56 * 1024 * 1024


def _dot(a, b):
    return jnp.dot(a, b, preferred_element_type=F32)


def _dot_nt(a, b):
    return lax.dot_general(a, b, (((1,), (1,)), ((), ())), preferred_element_type=F32)


def _dot_tn(a, b):
    return lax.dot_general(a, b, (((0,), (0,)), ((), ())), preferred_element_type=F32)


def _sigmoid(x):
    return 1.0 / (1.0 + jnp.exp(-x))


def _params(sem):
    return pltpu.CompilerParams(dimension_semantics=sem, vmem_limit_bytes=VMEM_LIMIT)


def _whole():
    return pl.BlockSpec(memory_space=pltpu.VMEM)


HBM_SPEC = pl.BlockSpec(memory_space=pl.ANY)


class _Rider:
    def __init__(self, inplace, srcs, lands, n_copies, make, handshake=None):
        self.inplace, self.srcs, self.lands, self.n_copies, self.make = list(inplace), list(srcs), list(lands), n_copies, make
        self.handshake = handshake


def _call(body, *, name, grid, in_specs, out_specs, out_shape, operands, scratch_shapes=(), sem=(),
          aliases=None, riders=(), after=()):
    n_in, n_out, n_scr = len(in_specs), len(out_shape), len(scratch_shapes)
    in_specs, out_specs, out_shape = list(in_specs), list(out_specs), list(out_shape)
    operands, scratch_shapes, aliases = list(operands), list(scratch_shapes), dict(aliases or {})
    in_specs += [_whole()] * len(after)
    operands += list(after)
    shakes = [r.handshake for r in riders if r.handshake is not None]
    assert len(shakes) <= 1
    for r in riders:
        for a in r.inplace:
            aliases[len(in_specs)] = len(out_shape)
            in_specs.append(HBM_SPEC)
            operands.append(a)
            out_specs.append(HBM_SPEC)
            out_shape.append(jax.ShapeDtypeStruct(a.shape, a.dtype))
        for a in r.srcs:
            in_specs.append(HBM_SPEC)
            operands.append(a)
        for shp in r.lands:
            out_specs.append(HBM_SPEC)
            out_shape.append(shp)
        scratch_shapes += [pltpu.SemaphoreType.DMA((r.n_copies,)), pltpu.SemaphoreType.DMA((r.n_copies,))]

    def full(*refs):
        ins = refs[:n_in]
        at = n_in + len(after)
        r_srcs = []
        for r in riders:
            at += len(r.inplace)
            r_srcs.append(refs[at:at + len(r.srcs)])
            at += len(r.srcs)
        outs = refs[at:at + n_out]
        at += n_out
        r_outs = []
        for r in riders:
            r_outs.append((refs[at:at + len(r.inplace)], refs[at + len(r.inplace):at + len(r.inplace) + len(r.lands)]))
            at += len(r.inplace) + len(r.lands)
        scr = refs[at:at + n_scr]
        at += n_scr
        r_sems = [refs[at + 2 * i:at + 2 * i + 2] for i in range(len(riders))]

        def copies():
            return [r.make(r_outs[i][0], r_srcs[i], r_outs[i][1], r_sems[i][0], r_sems[i][1])
                    for i, r in enumerate(riders)]

        def start():
            if shakes:
                _shake_hands(shakes[0][1])
            for starts, _ in copies():
                for cp in starts:
                    cp.start()

        def finish():
            for _, waits in copies():
                for wait in waits:
                    wait()

        if riders and grid:
            first = functools.reduce(jnp.logical_and, [pl.program_id(d) == 0 for d in range(len(grid))])
            last = functools.reduce(jnp.logical_and, [pl.program_id(d) == grid[d] - 1 for d in range(len(grid))])
            pl.when(first)(start)
            body(*ins, *outs, *scr)
            pl.when(last)(finish)
        else:
            if riders:
                start()
            body(*ins, *outs, *scr)
            if riders:
                finish()

    barrier_id = shakes[0][0] if shakes else None
    params = pltpu.CompilerParams(vmem_limit_bytes=VMEM_LIMIT, collective_id=barrier_id,
                                  **(dict(dimension_semantics=sem) if grid else {}))
    res = pl.pallas_call(
        full, name=name, grid=grid, in_specs=in_specs, out_specs=out_specs, out_shape=out_shape,
        scratch_shapes=scratch_shapes, input_output_aliases=aliases, compiler_params=params,
    )(*operands)
    outs, at, rider_res = res[:n_out], n_out, []
    for r in riders:
        rider_res.append((res[at:at + len(r.inplace)], res[at + len(r.inplace):at + len(r.inplace) + len(r.lands)]))
        at += len(r.inplace) + len(r.lands)
    return list(outs), rider_res


def _gammas():
    return [1.0 - 2.0 ** (-5.0 - h) for h in range(HEADS)]


def _decay_tables():
    idx = np.arange(SUPER)
    dist = np.abs(idx[:, None] - idx[None, :]).astype(np.float64)
    visible = (idx[None, :] // CHUNK) <= (idx[:, None] // CHUNK)
    mask = np.stack([np.where(visible, g ** dist, 0.0) for g in _gammas()])
    qd = np.concatenate([np.repeat((g ** (idx + 1.0))[:, None], HEAD_DIM, 1) for g in _gammas()], 1)
    kd = np.concatenate([np.repeat((g ** (SUPER - 1.0 - idx))[:, None], HEAD_DIM, 1) for g in _gammas()], 1)
    return (jnp.asarray(mask, F32), jnp.asarray(qd, F32), jnp.asarray(kd, F32))


def _rope_tables(s):
    inv_freq = ROPE_BASE ** (-np.arange(0, HEAD_DIM, 2, dtype=np.float64) / HEAD_DIM)
    ang = np.arange(s, dtype=np.float64)[:, None] * inv_freq[None, :]
    cos, sin = np.cos(ang), np.sin(ang)
    return (jnp.asarray(np.concatenate([cos, cos], 1), F32),
            jnp.asarray(np.concatenate([-sin, sin], 1), F32))


def _rope(t, cosf, sinf):
    return t * cosf + pltpu.roll(t, HEAD_DIM // 2, 1) * sinf


def _rope_t(t, cosf, sinf):
    return t * cosf - pltpu.roll(t, HEAD_DIM // 2, 1) * sinf


def _layernorm_fwd(z):
    mu = jnp.mean(z, axis=-1, keepdims=True)
    zc = z - mu
    var = jnp.mean(zc * zc, axis=-1, keepdims=True)
    rstd = lax.rsqrt(var + LN_EPS)
    return zc * rstd, rstd


def _layernorm_bwd(dy, xhat, rstd, gain):
    dxh = dy * gain
    m1 = jnp.mean(dxh, axis=-1, keepdims=True)
    m2 = jnp.mean(dxh * xhat, axis=-1, keepdims=True)
    return rstd * (dxh - m1 - xhat * m2)


def _proj_pool(x, win4, cosf, sinf, wpool, pscale, ts, riders=(), after=()):
    s = x.shape[0]
    nt = s // ts

    def body(x_ref, w_ref, cos_ref, sin_ref, wp_ref, ps_ref,
             xb_ref, q_ref, k_ref, v_ref, g_ref, pooled_ref, cat_ref, proj_scr, pext_scr):
        i = pl.program_id(0)
        xb = x_ref[...].astype(BF16)
        xb_ref[...] = xb
        for j in range(N_SHARD):
            proj_scr[:, j * IN_SH:(j + 1) * IN_SH] = _dot(xb, w_ref[j])
        cosf_t = cos_ref[...]
        sinf_t = sin_ref[...]
        for h in range(HEADS):
            lo = h * HEAD_DIM
            q_ref[:, lo:lo + HEAD_DIM] = _rope(proj_scr[:, lo:lo + HEAD_DIM], cosf_t, sinf_t).astype(BF16)
            kk = _rope(proj_scr[:, RET_W + lo:RET_W + lo + HEAD_DIM], cosf_t, sinf_t) * K_SCALE
            k_ref[:, lo:lo + HEAD_DIM] = kk.astype(BF16)
        v_ref[...] = proj_scr[:, 2 * RET_W:3 * RET_W].astype(BF16)
        g_ref[...] = proj_scr[:, 3 * RET_W:4 * RET_W]

        @pl.when(i == 0)
        def _():
            pext_scr[0:POOL_HALO, :] = jnp.zeros((POOL_HALO, POOL_W), F32)

        pext_scr[POOL_HALO:POOL_HALO + ts, :] = proj_scr[:, 4 * RET_W:IN_W]
        pos = (i * ts + lax.broadcasted_iota(jnp.int32, (ts, 1), 0) + 1).astype(F32)
        for gi, w in enumerate(POOL_WINDOWS):
            lo = gi * HEAD_DIM
            ext = pext_scr[:, lo:lo + HEAD_DIM]
            acc = ext
            shift = 1
            while shift < w:
                acc = acc + pltpu.roll(acc, shift, 0)
                shift *= 2
            tok = ext[POOL_HALO:POOL_HALO + ts]
            pooled = acc[POOL_HALO:POOL_HALO + ts] / jnp.minimum(pos, float(w)) - tok
            pooled_b = pooled.astype(BF16)
            pooled_ref[:, lo:lo + HEAD_DIM] = pooled_b
            lin = _dot(pooled_b, wp_ref[gi])
            cat_ref[:, lo:lo + HEAD_DIM] = (lin * ps_ref[:, lo:lo + HEAD_DIM]).astype(BF16)
        pext_scr[0:POOL_HALO, :] = pext_scr[ts:ts + POOL_HALO, :]

    tile = lambda w: pl.BlockSpec((ts, w), lambda i: (i, 0))
    return _call(
        body, name="proj_pool", grid=(nt,),
        in_specs=[tile(D_MODEL), _whole(), tile(HEAD_DIM), tile(HEAD_DIM), _whole(), _whole()],
        out_specs=[tile(D_MODEL), tile(RET_W), tile(RET_W), tile(RET_W), tile(RET_W), tile(POOL_W),
                   pl.BlockSpec((ts, POOL_W), lambda i: (i, 1))],
        out_shape=[jax.ShapeDtypeStruct((s, D_MODEL), BF16), jax.ShapeDtypeStruct((s, RET_W), BF16),
                   jax.ShapeDtypeStruct((s, RET_W), BF16), jax.ShapeDtypeStruct((s, RET_W), BF16),
                   jax.ShapeDtypeStruct((s, RET_W), F32), jax.ShapeDtypeStruct((s, POOL_W), BF16),
                   jax.ShapeDtypeStruct((s, 2 * RET_W), BF16)],
        scratch_shapes=[pltpu.VMEM((ts, IN_W), F32), pltpu.VMEM((ts + POOL_HALO, POOL_W), F32)],
        sem=("arbitrary",), operands=(x, win4, cosf, sinf, wpool, pscale), riders=riders, after=after,
    )


def _retention_fwd(q, k, v, g, cat, mask, qd, kd, riders=(), after=()):
    s = q.shape[0]
    ns = s // SUPER
    cdec = [gm ** float(SUPER) for gm in _gammas()]

    def body(q_ref, k_ref, v_ref, g_ref, cat_in, mask_ref, qd_ref, kd_ref,
             ret_ref, cat_ref, st_ref, state_scr):
        del cat_in
        n = pl.program_id(0)

        @pl.when(n == 0)
        def _():
            state_scr[...] = jnp.zeros_like(state_scr)

        for h in range(HEADS):
            sl = slice(h * HEAD_DIM, (h + 1) * HEAD_DIM)
            qh, kh, vh = q_ref[:, sl], k_ref[:, sl], v_ref[:, sl]
            sc = _dot_nt(qh, kh) * mask_ref[h]
            st = state_scr[h]
            stb = st.astype(BF16)
            st_ref[0, h] = stb
            qdb = (qh.astype(F32) * qd_ref[:, sl]).astype(BF16)
            kdb = (kh.astype(F32) * kd_ref[:, sl]).astype(BF16)
            ret = _dot(sc.astype(BF16), vh) + _dot(qdb, stb)
            state_scr[h] = st * cdec[h] + _dot_tn(kdb, vh)
            ret_ref[:, sl] = ret
            r = lax.rsqrt(jnp.mean(ret * ret, axis=-1, keepdims=True) + RMS_EPS)
            gh = g_ref[:, sl]
            cat_ref[:, sl] = ((ret * r) * (gh * _sigmoid(gh))).astype(BF16)

    tile = pl.BlockSpec((SUPER, RET_W), lambda n: (n, 0))
    return _call(
        body, name="retention_fwd", grid=(ns,),
        in_specs=[tile, tile, tile, tile, HBM_SPEC, _whole(), _whole(), _whole()],
        out_specs=[tile, tile, pl.BlockSpec((1, HEADS, HEAD_DIM, HEAD_DIM), lambda n: (n, 0, 0, 0))],
        out_shape=[jax.ShapeDtypeStruct((s, RET_W), F32), jax.ShapeDtypeStruct((s, 2 * RET_W), BF16),
                   jax.ShapeDtypeStruct((ns, HEADS, HEAD_DIM, HEAD_DIM), BF16)],
        scratch_shapes=[pltpu.VMEM((HEADS, HEAD_DIM, HEAD_DIM), F32)],
        aliases={4: 1}, sem=("arbitrary",), operands=(q, k, v, g, cat, mask, qd, kd), riders=riders,
        after=after,
    )


def _outproj_ln1(x, cat, wout, g1, b1, ts, riders=(), after=()):
    s = x.shape[0]

    def body(x_ref, cat_ref, w_ref, g_ref, b_ref, xhat_ref, rstd_ref, h1b_ref):
        z = ALPHA * x_ref[...] + _dot(cat_ref[...], w_ref[...])
        xhat, rstd = _layernorm_fwd(z)
        xhat_ref[...] = xhat
        rstd_ref[...] = rstd
        h1b_ref[...] = (xhat * g_ref[...] + b_ref[...]).astype(BF16)

    tile = lambda w: pl.BlockSpec((ts, w), lambda i: (i, 0))
    return _call(
        body, name="outproj_ln1", grid=(s // ts,),
        in_specs=[tile(D_MODEL), tile(D_MODEL), _whole(), _whole(), _whole()],
        out_specs=[tile(D_MODEL), tile(1), tile(D_MODEL)],
        out_shape=[jax.ShapeDtypeStruct((s, D_MODEL), F32), jax.ShapeDtypeStruct((s, 1), F32),
                   jax.ShapeDtypeStruct((s, D_MODEL), BF16)],
        sem=("arbitrary",), operands=(x, cat, wout, g1, b1), riders=riders, after=after,
    )


def _ffn_fwd_loss(xhat1, h1b, target, wup4, wdown, cw, cb, g1, b1, g2, b2, ts):
    s = xhat1.shape[0]

    def body(xhat_ref, h1b_ref, tgt_ref, wup_ref, wdn_ref, cw_ref, cb_ref, g1_ref, b1_ref, g2_ref, b2_ref,
             ub_ref, act_ref, sd_ref, dz2_ref, dz2b_ref, loss_ref, dg2_ref, db2_ref, val_scr, gext_scr, ffn_scr):
        i = pl.program_id(0)

        @pl.when(i == 0)
        def _():
            gext_scr[0:CONV_HALO, :] = jnp.zeros((CONV_HALO, D_FF), F32)
            loss_ref[...] = jnp.zeros_like(loss_ref)
            dg2_ref[...] = jnp.zeros_like(dg2_ref)
            db2_ref[...] = jnp.zeros_like(db2_ref)

        for half in range(2):
            lo = half * UP_SH
            gext_scr[CONV_HALO:CONV_HALO + ts, lo:lo + UP_SH] = _dot(h1b_ref[...], wup_ref[2 + half])
            val_scr[:, lo:lo + UP_SH] = _dot(h1b_ref[...], wup_ref[half])
            for c0 in range(lo, lo + UP_SH, FFN_STRIP):
                cols = slice(c0, c0 + FFN_STRIP)
                ext = gext_scr[:, cols]
                gate = ext[CONV_HALO:]
                hc = cb_ref[:, cols] + ((pltpu.roll(ext, 2, 0)[CONV_HALO:] * cw_ref[0:1, cols]
                                         + pltpu.roll(ext, 1, 0)[CONV_HALO:] * cw_ref[1:2, cols])
                                        + gate * cw_ref[2:3, cols])
                val = val_scr[:, cols]
                sg = _sigmoid(hc)
                si = hc * sg
                act_ref[:, cols] = (si * val).astype(BF16)
                ub_ref[:, cols] = val.astype(BF16)
                ub_ref[:, D_FF + c0:D_FF + c0 + FFN_STRIP] = gate.astype(BF16)
                sd_ref[:, cols] = hc.astype(BF16)
            part = _dot(act_ref[:, lo:lo + UP_SH], wdn_ref[lo:lo + UP_SH, :])
            if half == 0:
                ffn_scr[...] = part
            else:
                ffn_scr[...] += part

        gext_scr[0:CONV_HALO, :] = gext_scr[ts:ts + CONV_HALO, :]

        loss_acc = jnp.zeros((1, 1), F32)
        dg2_acc = jnp.zeros((1, D_MODEL), F32)
        db2_acc = jnp.zeros((1, D_MODEL), F32)
        for r0 in range(0, ts, LN_ROWS):
            rows = slice(r0, r0 + LN_ROWS)
            h1 = xhat_ref[rows, :] * g1_ref[...] + b1_ref[...]
            xhat2, rstd2 = _layernorm_fwd(ALPHA * h1 + ffn_scr[rows, :])
            diff = (xhat2 * g2_ref[...] + b2_ref[...]) - tgt_ref[rows, :]
            row = jnp.mean(diff * diff, axis=-1, keepdims=True)
            loss_acc = loss_acc + 0.5 * jnp.sum(row, axis=0, keepdims=True)
            dy = diff * (1.0 / D_MODEL)
            dg2_acc = dg2_acc + jnp.sum(dy * xhat2, axis=0, keepdims=True)
            db2_acc = db2_acc + jnp.sum(dy, axis=0, keepdims=True)
            dz2 = _layernorm_bwd(dy, xhat2, rstd2, g2_ref[...])
            dz2_ref[rows, :] = dz2
            dz2b_ref[rows, :] = dz2.astype(BF16)
        loss_ref[...] += loss_acc
        dg2_ref[...] += dg2_acc
        db2_ref[...] += db2_acc

    tile = lambda w: pl.BlockSpec((ts, w), lambda i: (i, 0))
    acc = lambda w: pl.BlockSpec((1, w), lambda i: (0, 0))
    return pl.pallas_call(
        body, name="ffn_fwd_loss", grid=(s // ts,),
        in_specs=[tile(D_MODEL), tile(D_MODEL), tile(D_MODEL)] + [_whole()] * 8,
        out_specs=[tile(2 * D_FF), tile(D_FF), tile(D_FF), tile(D_MODEL), tile(D_MODEL),
                   acc(1), acc(D_MODEL), acc(D_MODEL)],
        out_shape=[jax.ShapeDtypeStruct((s, 2 * D_FF), BF16), jax.ShapeDtypeStruct((s, D_FF), BF16),
                   jax.ShapeDtypeStruct((s, D_FF), BF16), jax.ShapeDtypeStruct((s, D_MODEL), F32),
                   jax.ShapeDtypeStruct((s, D_MODEL), BF16),
                   jax.ShapeDtypeStruct((1, 1), F32), jax.ShapeDtypeStruct((1, D_MODEL), F32),
                   jax.ShapeDtypeStruct((1, D_MODEL), F32)],
        scratch_shapes=[pltpu.VMEM((ts, D_FF), F32), pltpu.VMEM((ts + CONV_HALO, D_FF), F32),
                        pltpu.VMEM((ts, D_MODEL), F32)],
        compiler_params=_params(("arbitrary",)),
    )(xhat1, h1b, target, wup4, wdown, cw, cb, g1, b1, g2, b2)


def _ffn_bwd(dz2, dz2b, ub, sd, xhat1, rstd1, wup4, wdown, cw, g1, ts):
    s = dz2.shape[0]
    nt = s // ts

    def body(dz2_ref, dz2b_ref, ub_ref, sd_ref, xhat_ref, rstd_ref, wup_ref, wdn_ref, cw_ref, g1_ref,
             dub_ref, dz1_ref, dz1b_ref, dg1_ref, db1_ref, dcw_ref, dcb_ref, dext_scr, da_scr):
        i = pl.program_id(0)

        @pl.when(i == 0)
        def _():
            dext_scr[ts:ts + CONV_HALO, :] = jnp.zeros((CONV_HALO, D_FF), F32)
            dg1_ref[...] = jnp.zeros_like(dg1_ref)
            db1_ref[...] = jnp.zeros_like(db1_ref)
            dcw_ref[...] = jnp.zeros_like(dcw_ref)
            dcb_ref[...] = jnp.zeros_like(dcb_ref)

        da_scr[...] = _dot_nt(dz2b_ref[...], wdn_ref[...])
        n_ext = ts + CONV_HALO
        for c0 in range(0, D_FF, FFN_STRIP):
            cols = slice(c0, c0 + FFN_STRIP)
            gcols = slice(D_FF + c0, D_FF + c0 + FFN_STRIP)
            val = ub_ref[:, cols].astype(F32)
            gate = ub_ref[:, gcols].astype(F32)
            da = da_scr[:, cols]
            hc = sd_ref[:, cols].astype(F32)
            sg = _sigmoid(hc)
            dhc = da * val * (sg * (1.0 + hc * (1.0 - sg)))
            dext_scr[0:ts, cols] = dhc
            dext = dext_scr[:, cols]
            dhc1 = pltpu.roll(dext, n_ext - 1, 0)[0:ts]
            dhc2 = pltpu.roll(dext, n_ext - 2, 0)[0:ts]
            dcb_ref[:, cols] += jnp.sum(dhc, axis=0, keepdims=True)
            dcw_ref[0:1, cols] += jnp.sum(dhc2 * gate, axis=0, keepdims=True)
            dcw_ref[1:2, cols] += jnp.sum(dhc1 * gate, axis=0, keepdims=True)
            dcw_ref[2:3, cols] += jnp.sum(dhc * gate, axis=0, keepdims=True)
            dgate = dhc * cw_ref[2:3, cols] + dhc1 * cw_ref[1:2, cols] + dhc2 * cw_ref[0:1, cols]
            dub_ref[:, cols] = (da * (hc * sg)).astype(BF16)
            dub_ref[:, gcols] = dgate.astype(BF16)
        dext_scr[ts:n_ext, :] = dext_scr[0:CONV_HALO, :]
        dh1 = ALPHA * dz2_ref[...]
        for j in range(N_SHARD):
            dh1 = dh1 + _dot_nt(dub_ref[:, j * UP_SH:(j + 1) * UP_SH], wup_ref[j])
        xhat = xhat_ref[...]
        dg1_ref[...] += jnp.sum(dh1 * xhat, axis=0, keepdims=True)
        db1_ref[...] += jnp.sum(dh1, axis=0, keepdims=True)
        dz1 = _layernorm_bwd(dh1, xhat, rstd_ref[...], g1_ref[...])
        dz1_ref[...] = dz1
        dz1b_ref[...] = dz1.astype(BF16)

    tile = lambda w: pl.BlockSpec((ts, w), lambda i: (nt - 1 - i, 0))
    acc = lambda rws, w: pl.BlockSpec((rws, w), lambda i: (0, 0))
    return pl.pallas_call(
        body, name="ffn_bwd", grid=(nt,),
        in_specs=[tile(D_MODEL), tile(D_MODEL), tile(2 * D_FF), tile(D_FF), tile(D_MODEL), tile(1)]
        + [_whole()] * 4,
        out_specs=[tile(2 * D_FF), tile(D_MODEL), tile(D_MODEL), acc(1, D_MODEL), acc(1, D_MODEL),
                   acc(3, D_FF), acc(1, D_FF)],
        out_shape=[jax.ShapeDtypeStruct((s, 2 * D_FF), BF16),
                   jax.ShapeDtypeStruct((s, D_MODEL), F32), jax.ShapeDtypeStruct((s, D_MODEL), BF16),
                   jax.ShapeDtypeStruct((1, D_MODEL), F32),
                   jax.ShapeDtypeStruct((1, D_MODEL), F32), jax.ShapeDtypeStruct((3, D_FF), F32),
                   jax.ShapeDtypeStruct((1, D_FF), F32)],
        scratch_shapes=[pltpu.VMEM((ts + CONV_HALO, D_FF), F32), pltpu.VMEM((ts, D_FF), F32)],
        compiler_params=_params(("arbitrary",)),
    )(dz2, dz2b, ub, sd, xhat1, rstd1, wup4, wdown, cw, g1)


def _mix_bwd(dz1, pooled, ret, g, wout, wpool, pscale, ts, riders=(), after=()):
    s = dz1.shape[0]
    nt = s // ts

    def body(dz1_ref, pooled_ref, ret_ref, g_ref, wout_ref, wp_ref, ps_ref,
             dret_ref, dgp_ref, dwp_ref, dps_ref, eext_scr):
        i = pl.program_id(0)
        r = nt - 1 - i

        @pl.when(i == 0)
        def _():
            eext_scr[ts:ts + POOL_HALO, :] = jnp.zeros((POOL_HALO, POOL_W), F32)
            dwp_ref[...] = jnp.zeros_like(dwp_ref)
            dps_ref[...] = jnp.zeros_like(dps_ref)

        dzb = dz1_ref[...].astype(BF16)
        dcat_r = _dot_nt(dzb, wout_ref[0:RET_W, :])
        dcat_p = _dot_nt(dzb, wout_ref[RET_W:2 * RET_W, :])
        pos = (r * ts + lax.broadcasted_iota(jnp.int32, (ts, 1), 0) + 1).astype(F32)
        dpooled = []
        for gi, w in enumerate(POOL_WINDOWS):
            sl = slice(gi * HEAD_DIM, (gi + 1) * HEAD_DIM)
            pb = pooled_ref[:, sl]
            dy = dcat_p[:, sl]
            dps_ref[:, sl] += jnp.sum(dy * _dot(pb, wp_ref[gi]), axis=0, keepdims=True)
            dlin = (dy * ps_ref[:, sl]).astype(BF16)
            dwp_ref[gi] += _dot_tn(pb, dlin)
            dpg = _dot_nt(dlin, wp_ref[gi])
            dpooled.append(dpg)
            eext_scr[0:ts, sl] = dpg / jnp.minimum(pos, float(w))
        for gi, w in enumerate(POOL_WINDOWS):
            sl = slice(gi * HEAD_DIM, (gi + 1) * HEAD_DIM)
            acc = eext_scr[:, sl]
            shift = 1
            while shift < w:
                acc = acc + pltpu.roll(acc, ts + POOL_HALO - shift, 0)
                shift *= 2
            dgp_ref[:, RET_W + gi * HEAD_DIM:RET_W + (gi + 1) * HEAD_DIM] = (acc[0:ts] - dpooled[gi]).astype(BF16)
        eext_scr[ts:ts + POOL_HALO, :] = eext_scr[0:POOL_HALO, :]
        for h in range(HEADS):
            sl = slice(h * HEAD_DIM, (h + 1) * HEAD_DIM)
            rt = ret_ref[:, sl]
            rr = lax.rsqrt(jnp.mean(rt * rt, axis=-1, keepdims=True) + RMS_EPS)
            rn = rt * rr
            gh = g_ref[:, sl]
            sg = _sigmoid(gh)
            dy = dcat_r[:, sl]
            dgp_ref[:, sl] = (dy * rn * (sg * (1.0 + gh * (1.0 - sg)))).astype(BF16)
            drn = dy * (gh * sg)
            dret_ref[:, sl] = (rr * (drn - rn * jnp.mean(drn * rn, axis=-1, keepdims=True))).astype(BF16)

    tile = lambda w: pl.BlockSpec((ts, w), lambda i: (nt - 1 - i, 0))
    return _call(
        body, name="mix_bwd", grid=(nt,),
        in_specs=[tile(D_MODEL), tile(POOL_W), tile(RET_W), tile(RET_W), _whole(), _whole(), _whole()],
        out_specs=[tile(RET_W), tile(2 * RET_W),
                   pl.BlockSpec((len(POOL_WINDOWS), HEAD_DIM, HEAD_DIM), lambda i: (0, 0, 0)),
                   pl.BlockSpec((1, POOL_W), lambda i: (0, 0))],
        out_shape=[jax.ShapeDtypeStruct((s, RET_W), BF16), jax.ShapeDtypeStruct((s, 2 * RET_W), BF16),
                   jax.ShapeDtypeStruct((len(POOL_WINDOWS), HEAD_DIM, HEAD_DIM), F32),
                   jax.ShapeDtypeStruct((1, POOL_W), F32)],
        scratch_shapes=[pltpu.VMEM((ts + POOL_HALO, POOL_W), F32)],
        sem=("arbitrary",), operands=(dz1, pooled, ret, g, wout, wpool, pscale), riders=riders,
        after=after,
    )


def _retention_bwd(q, k, v, dret, dgp, states, mask, qd, kd, cosf, sinf, riders=(), after=()):
    s = q.shape[0]
    ns = s // SUPER
    cdec = [gm ** float(SUPER) for gm in _gammas()]

    def body(q_ref, k_ref, v_ref, do_ref, dgp_ref, st_ref, mask_ref, qd_ref, kd_ref, cos_ref, sin_ref,
             dproj_ref, dstate_scr):
        i = pl.program_id(0)

        @pl.when(i == 0)
        def _():
            dstate_scr[...] = jnp.zeros_like(dstate_scr)

        cosf_t = cos_ref[...]
        sinf_t = sin_ref[...]
        for h in range(HEADS):
            sl = slice(h * HEAD_DIM, (h + 1) * HEAD_DIM)
            qh, kh, vh, doh = q_ref[:, sl], k_ref[:, sl], v_ref[:, sl], do_ref[:, sl]
            dscb = (_dot_nt(doh, vh) * mask_ref[0, h]).astype(BF16)
            dsctb = (_dot_nt(vh, doh) * mask_ref[1, h]).astype(BF16)
            sctb = (_dot_nt(kh, qh) * mask_ref[1, h]).astype(BF16)
            stb = st_ref[0, h]
            dst = dstate_scr[h]
            dstb = dst.astype(BF16)
            qdb = (qh.astype(F32) * qd_ref[:, sl]).astype(BF16)
            kdb = (kh.astype(F32) * kd_ref[:, sl]).astype(BF16)
            dq = _dot(dscb, kh) + _dot_nt(doh, stb) * qd_ref[:, sl]
            dk = _dot(dsctb, qh) + _dot_nt(vh, dstb) * kd_ref[:, sl]
            dv = _dot(sctb, doh) + _dot(kdb, dstb)
            dstate_scr[h] = dst * cdec[h] + _dot_tn(qdb, doh)
            lo = h * HEAD_DIM
            dproj_ref[:, lo:lo + HEAD_DIM] = _rope_t(dq, cosf_t, sinf_t).astype(BF16)
            dproj_ref[:, RET_W + lo:RET_W + lo + HEAD_DIM] = _rope_t(dk * K_SCALE, cosf_t, sinf_t).astype(BF16)
            dproj_ref[:, 2 * RET_W + lo:2 * RET_W + lo + HEAD_DIM] = dv.astype(BF16)
        dproj_ref[:, 3 * RET_W:IN_W] = dgp_ref[...]

    tile = lambda w: pl.BlockSpec((SUPER, w), lambda i: (ns - 1 - i, 0))
    return _call(
        body, name="retention_bwd", grid=(ns,),
        in_specs=[tile(RET_W), tile(RET_W), tile(RET_W), tile(RET_W), tile(2 * RET_W),
                  pl.BlockSpec((1, HEADS, HEAD_DIM, HEAD_DIM), lambda i: (ns - 1 - i, 0, 0, 0)),
                  _whole(), _whole(), _whole(), tile(HEAD_DIM), tile(HEAD_DIM)],
        out_specs=[tile(IN_W)],
        out_shape=[jax.ShapeDtypeStruct((s, IN_W), BF16)],
        scratch_shapes=[pltpu.VMEM((HEADS, HEAD_DIM, HEAD_DIM), F32)],
        sem=("arbitrary",), operands=(q, k, v, dret, dgp, states, mask, qd, kd, cosf, sinf), riders=riders,
        after=after,
    )


def _dx(dz1, dproj, win4, ts, riders=(), after=()):
    s = dz1.shape[0]

    def body(dz1_ref, dp_ref, w_ref, dx_ref):
        acc = ALPHA * dz1_ref[...]
        for j in range(N_SHARD):
            acc = acc + _dot_nt(dp_ref[:, j * IN_SH:(j + 1) * IN_SH], w_ref[j])
        dx_ref[...] = acc

    tile = lambda w: pl.BlockSpec((ts, w), lambda i: (i, 0))
    return _call(
        body, name="dx", grid=(s // ts,),
        in_specs=[tile(D_MODEL), tile(IN_W), _whole()],
        out_specs=[tile(D_MODEL)],
        out_shape=[jax.ShapeDtypeStruct((s, D_MODEL), F32)],
        sem=("arbitrary",), operands=(dz1, dproj, win4), riders=riders, after=after,
    )


def _wgrad(a, b, tm, tn, name, stacked, m_outer, riders=(), after=()):
    s, m = a.shape
    n = b.shape[1]

    def body(a_ref, b_ref, o32_ref, o16_ref):
        res = _dot_tn(a_ref[...], b_ref[...])
        o32_ref[...] = res.reshape(o32_ref.shape)
        o16_ref[...] = res.astype(BF16).reshape(o16_ref.shape)

    if m_outer:
        grid, blocks = (m // tm, n // tn), (lambda g0, g1: (g0, g1))
    else:
        grid, blocks = (n // tn, m // tm), (lambda g0, g1: (g1, g0))
    if stacked:
        shape = (n // tn, m, tn)
        ospec = pl.BlockSpec((1, tm, tn), lambda g0, g1: (blocks(g0, g1)[1], blocks(g0, g1)[0], 0))
    else:
        shape = (m, n)
        ospec = pl.BlockSpec((tm, tn), lambda g0, g1: blocks(g0, g1))
    return _call(
        body, name=name, grid=grid,
        in_specs=[pl.BlockSpec((s, tm), lambda g0, g1: (0, blocks(g0, g1)[0])),
                  pl.BlockSpec((s, tn), lambda g0, g1: (0, blocks(g0, g1)[1]))],
        out_specs=[ospec, ospec],
        out_shape=[jax.ShapeDtypeStruct(shape, F32), jax.ShapeDtypeStruct(shape, BF16)],
        sem=("arbitrary", "arbitrary"), operands=(a, b), riders=riders, after=after,
    )


def _wgrad_send(a, b, tn, name, barrier_id, after=()):
    s, m = a.shape
    n = b.shape[1]
    nb, hm = n // tn, m // 2

    def body(*refs):
        a_ref, b_ref = refs[:2]
        o32_ref, land_ref, send_scr, send_sems, recv_sems = refs[2 + len(after):]
        j = pl.program_id(0)
        x, y, c = _mesh_pos()

        @pl.when(j == 0)
        def _():
            _shake_hands("sibling")

        o32_ref[0] = _dot_tn(a_ref[...], b_ref[...])
        theirs = pl.ds(pl.multiple_of((1 - c) * hm, 16), hm)
        copies = [pltpu.make_async_remote_copy(
            src_ref=send_scr.at[blk], dst_ref=land_ref.at[blk], send_sem=send_sems.at[blk],
            recv_sem=recv_sems.at[blk], device_id=(x, y, 1 - c), device_id_type=MESH) for blk in range(nb)]
        for blk in range(nb):
            @pl.when(j == blk)
            def _(blk=blk):
                send_scr[blk] = o32_ref[0, theirs, :].astype(BF16)
                copies[blk].start()

        @pl.when(j == nb - 1)
        def _():
            for cp in copies:
                cp.wait()

    return pl.pallas_call(
        body, name=name, grid=(nb,),
        in_specs=[pl.BlockSpec((s, m), lambda j: (0, 0)), pl.BlockSpec((s, tn), lambda j: (0, j))]
        + [_whole()] * len(after),
        out_specs=[pl.BlockSpec((1, m, tn), lambda j: (j, 0, 0)), HBM_SPEC],
        out_shape=[jax.ShapeDtypeStruct((nb, m, tn), F32), jax.ShapeDtypeStruct((nb, hm, tn), BF16)],
        scratch_shapes=[pltpu.VMEM((nb, hm, tn), BF16), pltpu.SemaphoreType.DMA((nb,)),
                        pltpu.SemaphoreType.DMA((nb,))],
        compiler_params=pltpu.CompilerParams(dimension_semantics=("arbitrary",), vmem_limit_bytes=VMEM_LIMIT,
                                             collective_id=barrier_id),
    )(a, b, *after)


class _NoComm:
    def __init__(self, win4, wout, wup4, wdown):
        self.weights = dict(w_in=win4, w_out=wout, w_up=wup4, w_down=wdown)
        self.grads = {}

    def weight(self, name):
        return self.weights[name]

    def riders(self, call):
        return ()

    def after(self, call):
        return ()

    def landed(self, call, results, outs):
        pass

    def small_gradients(self, loss, small):
        pass

    def gradient(self, name, g32, g16):
        self.grads[name] = (g32, g16)

    def wgrad_in(self, xb, dproj):
        (g32, g16), _ = _wgrad(xb, dproj, D_MODEL, IN_SH, "wgrad_in", True, True)
        self.gradient("w_in", g32, g16)


def _local_step(x, target, cw, cb, wpool_b, pscale, g1, b1, g2, b2, comm):
    s = x.shape[0]
    ts_a = min(512, s)
    ts_f = min(256, s)
    mask, qd, kd = _decay_tables()
    cosf, sinf = _rope_tables(s)

    def run(call, fn, *args):
        outs, res = fn(*args, riders=comm.riders(call), after=comm.after(call))
        comm.landed(call, res, outs)
        return outs

    xb, q, k, v, g, pooled, cat = run("proj_pool", _proj_pool, x, comm.weight("w_in"), cosf, sinf, wpool_b,
                                      pscale, ts_a)
    ret, cat, states = run("retention_fwd", _retention_fwd, q, k, v, g, cat, mask, qd, kd)
    wout = comm.weight("w_out")
    xhat1, rstd1, h1b = run("outproj_ln1", _outproj_ln1, x, cat, wout, g1, b1, ts_a)
    wup4, wdown = comm.weight("w_up"), comm.weight("w_down")
    ub, act, sd, dz2, dz2b, loss, dg2, db2 = _ffn_fwd_loss(xhat1, h1b, target, wup4, wdown, cw, cb, g1, b1, g2, b2,
                                                           ts_f)

    dub, dz1, dz1b, dg1, db1, dcw, dcb = _ffn_bwd(dz2, dz2b, ub, sd, xhat1, rstd1, wup4, wdown, cw, g1, ts_f)
    half = D_MODEL // 2
    comm.gradient("w_up", *run("wgrad_up", _wgrad, h1b, dub, half, UP_SH, "wgrad_up", True, False))
    comm.gradient("w_out", *run("wgrad_out", _wgrad, cat, dz1b, D_MODEL, half, "wgrad_out", False, True))
    comm.gradient("w_down", *run("wgrad_down", _wgrad, act, dz2b, D_FF // 2, half, "wgrad_down", False, True))
    dret, dgp, dwp, dps = run("mix_bwd", _mix_bwd, dz1b, pooled, ret, g, wout, wpool_b, pscale, ts_a)
    small = dict(w_pool=dwp, pool_scale=dps, ln1_g=dg1, ln1_b=db1, conv_w=dcw, conv_b=dcb,
                 ln2_g=dg2, ln2_b=db2)
    comm.small_gradients(loss, small)
    mask_both = jnp.stack([mask, jnp.swapaxes(mask, 1, 2)])
    dproj, = run("retention_bwd", _retention_bwd, q, k, v, dret, dgp, states, mask_both, qd, kd, cosf, sinf)
    comm.wgrad_in(xb, dproj)
    (grad_x,), _ = _dx(dz1, dproj, comm.weight("w_in"), ts_a, after=comm.after("dx"))
    return loss, grad_x, small


CAST_ROWS = 64
SHARD_SHAPES = ((D_MODEL, IN_SH), (OUT_SH, D_MODEL), (D_MODEL, UP_SH), (DOWN_SH, D_MODEL))
N_BIG = len(SHARD_SHAPES)
CW_SHARD = (3, 1, DOWN_SH)


def _mesh_pos():
    return lax.axis_index("x"), lax.axis_index("y"), lax.axis_index("c")


def _other_chips(x, y):
    return [(1 - x, y), (x, 1 - y), (1 - x, 1 - y)]


def _shake_hands(peers):
    x, y, c = _mesh_pos()
    others = [(x, y, 1 - c)] if peers in ("sibling", "both") else []
    if peers in ("chips", "both"):
        others += [(chip[0], chip[1], c) for chip in _other_chips(x, y)]
    barrier = pltpu.get_barrier_semaphore()
    for peer in others:
        pl.semaphore_signal(barrier, inc=1, device_id=peer, device_id_type=MESH)
    pl.semaphore_wait(barrier, len(others))


def _half_rows(w, which):
    hr = SHARD_SHAPES[w][0] // 2
    return pl.ds(pl.multiple_of(which * hr, 16), hr)


def _gather_weights(shards, cw_shard, wpool, full):
    def body(*refs):
        in_refs = refs[:N_BIG]
        cw_ref, wpool_ref = refs[N_BIG:N_BIG + 2]
        out_refs = refs[N_BIG + 2:2 * N_BIG + 2]
        cwo_ref, wpool_b_ref = refs[2 * N_BIG + 2:2 * N_BIG + 4]
        stage = refs[2 * N_BIG + 4:3 * N_BIG + 4]
        raw = refs[3 * N_BIG + 4:4 * N_BIG + 4 - len(full)]
        send_sems, recv_sems, fsend_sems, frecv_sems, cw_send, cw_recv, local_sems, load_sems = \
            refs[4 * N_BIG + 4 - len(full):]
        x, y, c = _mesh_pos()
        j0 = 2 * x + y
        chips = _other_chips(x, y)

        fetched = [w for w in range(N_BIG) if w not in full]
        f32 = {w: in_refs[w] for w in full}
        loads = []
        for n, w in enumerate(fetched):
            f32[w] = raw[n]
            loads.append(pltpu.make_async_copy(in_refs[w], raw[n], load_sems.at[n]))
            loads[-1].start()

        def cast_to_stage(w):
            def cast(i, carry):
                rows = pl.ds(pl.multiple_of(i * CAST_ROWS, CAST_ROWS), CAST_ROWS)
                stage[w][rows, :] = f32[w][rows, :].astype(BF16)
                return carry
            lax.fori_loop(0, SHARD_SHAPES[w][0] // CAST_ROWS, cast, 0)

        for w in full:
            cast_to_stage(w)

        jx, jy, jd = 2 * (1 - x) + y, 2 * x + (1 - y), 2 * (1 - x) + (1 - y)
        neighbours = [((1 - x, y, c), jx), ((x, 1 - y, c), jy)]
        passed = jnp.where(c == 0, jx, jy)
        pass_to = (jnp.where(c == 0, x, 1 - x), jnp.where(c == 0, 1 - y, y), c)

        def nbr(w, k, block):
            return pltpu.make_async_remote_copy(
                src_ref=stage[w].at[_half_rows(w, c), :], dst_ref=out_refs[w].at[block, _half_rows(w, c), :],
                send_sem=send_sems.at[w, k], recv_sem=recv_sems.at[w, k],
                device_id=neighbours[k][0], device_id_type=MESH)

        def relay(w, block):
            return pltpu.make_async_remote_copy(
                src_ref=out_refs[w].at[passed, _half_rows(w, c), :],
                dst_ref=out_refs[w].at[block, _half_rows(w, c), :],
                send_sem=send_sems.at[w, 2], recv_sem=recv_sems.at[w, 2],
                device_id=pass_to, device_id_type=MESH)

        def d2d(w, k, block, half):
            return pltpu.make_async_remote_copy(
                src_ref=out_refs[w].at[block, _half_rows(w, half), :],
                dst_ref=out_refs[w].at[block, _half_rows(w, half), :],
                send_sem=fsend_sems.at[w, k], recv_sem=frecv_sems.at[w, k],
                device_id=(x, y, 1 - c), device_id_type=MESH)

        def conv(k, block):
            chip = chips[k]
            return pltpu.make_async_remote_copy(
                src_ref=cw_ref, dst_ref=cwo_ref.at[block], send_sem=cw_send.at[k], recv_sem=cw_recv.at[k],
                device_id=(chip[0], chip[1], c), device_id_type=MESH)

        sent = [nbr(w, k, j0) for w in full for k in range(2)] + [conv(k, j0) for k in range(3)]
        for cp in sent:
            cp.start()
        for n, w in enumerate(fetched):
            loads[n].wait()
            cast_to_stage(w)
        local = [pltpu.make_async_copy(stage[w], out_refs[w].at[j0], local_sems.at[w]) for w in range(N_BIG)]
        local.append(pltpu.make_async_copy(cw_ref, cwo_ref.at[j0], local_sems.at[N_BIG]))
        for cp in local:
            cp.start()
        wpool_b_ref[...] = wpool_ref[...].astype(BF16)
        for w in full:
            for k, (_, block) in enumerate(neighbours):
                nbr(w, k, block).wait_recv()
            later = [relay(w, passed)] + [d2d(w, k, block, c) for k, (_, block) in enumerate(neighbours)]
            for cp in later:
                cp.start()
            sent += later
        for w in full:
            relay(w, jd).wait_recv()
            fw = d2d(w, 2, jd, c)
            fw.start()
            sent.append(fw)
        for w in full:
            for k, block in enumerate([jx, jy, jd]):
                d2d(w, k, block, 1 - c).wait_recv()
        for k, chip in enumerate(chips):
            conv(k, 2 * chip[0] + chip[1]).wait_recv()
        for cp in sent:
            cp.wait_send()
        for cp in local:
            cp.wait()

    out_shape = [jax.ShapeDtypeStruct((N_SHARD,) + shp, BF16) for shp in SHARD_SHAPES]
    out_shape.append(jax.ShapeDtypeStruct((N_SHARD,) + CW_SHARD, F32))
    out_shape.append(jax.ShapeDtypeStruct(wpool.shape, BF16))
    return pl.pallas_call(
        body, name="gather_weights",
        in_specs=[_whole() if w in full else HBM_SPEC for w in range(N_BIG)] + [_whole()] * 2,
        out_specs=[HBM_SPEC] * (N_BIG + 1) + [_whole()],
        out_shape=out_shape,
        scratch_shapes=[pltpu.VMEM(shp, BF16) for shp in SHARD_SHAPES]
        + [pltpu.VMEM(shp, F32) for w, shp in enumerate(SHARD_SHAPES) if w not in full] + [
            pltpu.SemaphoreType.DMA((N_BIG, 3)), pltpu.SemaphoreType.DMA((N_BIG, 3)),
            pltpu.SemaphoreType.DMA((N_BIG, 3)), pltpu.SemaphoreType.DMA((N_BIG, 3)),
            pltpu.SemaphoreType.DMA((3,)), pltpu.SemaphoreType.DMA((3,)),
            pltpu.SemaphoreType.DMA((N_BIG + 1,)), pltpu.SemaphoreType.DMA((N_BIG - len(full),))],
        compiler_params=pltpu.CompilerParams(vmem_limit_bytes=VMEM_LIMIT),
    )(*shards, cw_shard, wpool)


def _gather_rider(arrays, ops, handshake=None):
    ws = sorted(arrays)

    def make(inplace, srcs, lands, send_sems, recv_sems):
        del srcs, lands
        x, y, c = _mesh_pos()
        j0, jx, jy, jd = 2 * x + y, 2 * (1 - x) + y, 2 * x + (1 - y), 2 * (1 - x) + (1 - y)
        x_nbr, y_nbr, sibling = (1 - x, y, c), (x, 1 - y, c), (x, y, 1 - c)
        starts, waits = [], []
        for n, (kind, w, (r0, nr)) in enumerate(ops):
            ref = inplace[ws.index(w)]
            hr = SHARD_SHAPES[w][0] // 2
            rows = lambda core: pl.ds(pl.multiple_of(core * hr + r0, 16), nr)
            mine, theirs = rows(c), rows(1 - c)
            if kind == "ici":
                moves = [(ref.at[j0, mine, :], x_nbr, ref.at[jx, mine, :]),
                         (ref.at[j0, mine, :], y_nbr, ref.at[jy, mine, :]),
                         (ref.at[j0, mine, :], (1 - x, 1 - y, c), ref.at[jd, mine, :])]
            elif kind == "nbr":
                moves = [(ref.at[j0, mine, :], x_nbr, ref.at[jx, mine, :]),
                         (ref.at[j0, mine, :], y_nbr, ref.at[jy, mine, :])]
            elif kind == "relay":
                passed = jnp.where(c == 0, jx, jy)
                to = (jnp.where(c == 0, x, 1 - x), jnp.where(c == 0, 1 - y, y), c)
                moves = [(ref.at[passed, mine, :], to, ref.at[jd, mine, :])]
            else:
                blocks = dict(d2d=[jx, jy, jd], d2d_nbr=[jx, jy], d2d_diag=[jd])[kind]
                moves = [(ref.at[b, mine, :], sibling, ref.at[b, theirs, :]) for b in blocks]
            for k, (src, to, landing) in enumerate(moves):
                sems = dict(send_sem=send_sems.at[3 * n + k], recv_sem=recv_sems.at[3 * n + k],
                            device_id=to, device_id_type=MESH)
                send = pltpu.make_async_remote_copy(src_ref=src, dst_ref=src, **sems)
                arrival = pltpu.make_async_remote_copy(src_ref=src, dst_ref=landing, **sems)
                starts.append(send)
                waits += [arrival.wait_recv, send.wait_send]
        return starts, waits

    return _Rider([arrays[w] for w in ws], [], [], 3 * len(ops), make, handshake)


def _whole_half(w):
    return (0, SHARD_SHAPES[w][0] // 2)


def _pair_rider(ws, g16s):
    def make(inplace, srcs, lands, send_sems, recv_sems):
        del inplace
        x, y, c = _mesh_pos()
        copies = [pltpu.make_async_remote_copy(
            src_ref=srcs[i].at[:, _half_rows(w, 1 - c), :], dst_ref=lands[i],
            send_sem=send_sems.at[i], recv_sem=recv_sems.at[i], device_id=(x, y, 1 - c), device_id_type=MESH)
            for i, w in enumerate(ws)]
        return copies, [cp.wait for cp in copies]

    lands = [jax.ShapeDtypeStruct((N_SHARD, SHARD_SHAPES[w][0] // 2, SHARD_SHAPES[w][1]), BF16) for w in ws]
    return _Rider([], g16s, lands, len(ws), make)


def _chip_rider(ws, p16s):
    def make(inplace, srcs, lands, send_sems, recv_sems):
        del inplace
        x, y, c = _mesh_pos()
        copies = []
        for i in range(len(ws)):
            for k, chip in enumerate(_other_chips(x, y)):
                copies.append(pltpu.make_async_remote_copy(
                    src_ref=srcs[i].at[2 * chip[0] + chip[1]], dst_ref=lands[i].at[k],
                    send_sem=send_sems.at[3 * i + k], recv_sem=recv_sems.at[3 * i + k],
                    device_id=(chip[0], chip[1], c), device_id_type=MESH))
        return copies, [cp.wait for cp in copies]

    lands = [jax.ShapeDtypeStruct((3, SHARD_SHAPES[w][0] // 2, SHARD_SHAPES[w][1]), BF16) for w in ws]
    return _Rider([], p16s, lands, 3 * len(ws), make)


def _final_rider(halves):
    def make(inplace, srcs, lands, send_sems, recv_sems):
        del inplace
        x, y, c = _mesh_pos()
        copies = [pltpu.make_async_remote_copy(
            src_ref=srcs[i], dst_ref=lands[i], send_sem=send_sems.at[i], recv_sem=recv_sems.at[i],
            device_id=(x, y, 1 - c), device_id_type=MESH) for i in range(len(halves))]
        return copies, [cp.wait for cp in copies]

    return _Rider([], halves, [jax.ShapeDtypeStruct(h.shape, h.dtype) for h in halves], len(halves), make)


def _comm_only(name, riders):
    _, res = _call(lambda: None, name=name, grid=(), in_specs=[], out_specs=[], out_shape=[], operands=(),
                   riders=riders)
    return res


class _SemList:
    def __init__(self, refs):
        self.at = list(refs)


def _merged_rider(riders):
    srcs = [a for r in riders for a in r.srcs]
    lands = [a for r in riders for a in r.lands]

    def make(inplace, src_refs, land_refs, send_sems, recv_sems):
        starts, waits = [], []
        s0 = l0 = c0 = 0
        for r in riders:
            part = r.make(inplace, src_refs[s0:s0 + len(r.srcs)], land_refs[l0:l0 + len(r.lands)],
                          _SemList(send_sems.at[c0:c0 + r.n_copies]), _SemList(recv_sems.at[c0:c0 + r.n_copies]))
            starts += part[0]
            waits += part[1]
            s0, l0, c0 = s0 + len(r.srcs), l0 + len(r.lands), c0 + r.n_copies
        return starts, waits

    return _Rider([], srcs, lands, sum(r.n_copies for r in riders), make)


def _split_start(name, rider, handshake=None):
    assert not rider.inplace
    ns, nl, n = len(rider.srcs), len(rider.lands), rider.n_copies
    barrier_id, peers = handshake if handshake is not None else (None, None)

    def body(*refs):
        if handshake is not None:
            _shake_hands(peers)
        srcs, lands = refs[:ns], refs[ns:ns + nl]
        sems = refs[ns + nl:ns + nl + 2 * n]
        token = refs[-1]
        starts, _ = rider.make([], srcs, lands, _SemList(sems[:n]), _SemList(sems[n:]))
        for cp in starts:
            cp.start()
        token[...] = jnp.zeros_like(token)

    buffers = [pltpu.with_memory_space_constraint(a, pltpu.HBM) for a in rider.srcs]
    buffers += [pltpu.with_memory_space_constraint(lax.empty(s.shape, s.dtype), pltpu.HBM) for s in rider.lands]
    hbm = pl.BlockSpec(memory_space=pltpu.HBM)
    sem = pl.BlockSpec(memory_space=pltpu.SEMAPHORE)
    outs = pl.pallas_call(
        body, name=name,
        out_shape=tuple([pltpu.SemaphoreType.DMA(())] * (2 * n) + [pltpu.HBM(b.shape, b.dtype) for b in buffers]
                        + [jax.ShapeDtypeStruct((8, 128), F32)]),
        in_specs=[hbm] * (ns + nl),
        out_specs=tuple([sem] * (2 * n) + [hbm] * (ns + nl) + [_whole()]),
        input_output_aliases={i: 2 * n + i for i in range(ns + nl)},
        compiler_params=pltpu.CompilerParams(has_side_effects=pltpu.SideEffectType.DATAFLOW_SIDE_EFFECTING,
                                             collective_id=barrier_id),
    )(*buffers)
    return (rider, outs[:2 * n], outs[2 * n:2 * n + ns + nl]), outs[-1]


def _split_parts(state, riders):
    merged, sems, buffers = state
    n, ns = merged.n_copies, len(merged.srcs)
    parts, s0, l0, c0 = [], 0, 0, 0
    for r in riders:
        parts.append((r, list(sems[c0:c0 + r.n_copies]) + list(sems[n + c0:n + c0 + r.n_copies]),
                      list(buffers[s0:s0 + len(r.srcs)]) + list(buffers[ns + l0:ns + l0 + len(r.lands)])))
        s0, l0, c0 = s0 + len(r.srcs), l0 + len(r.lands), c0 + r.n_copies
    return parts


def _split_wait(name, state, after):
    rider, sems, buffers = state
    ns, nl, n = len(rider.srcs), len(rider.lands), rider.n_copies

    def body(*refs):
        srcs, lands = refs[:ns], refs[ns:ns + nl]
        sem_refs = refs[ns + nl:ns + nl + 2 * n]
        _, waits = rider.make([], srcs, lands, _SemList(sem_refs[:n]), _SemList(sem_refs[n:]))
        for wait in waits:
            wait()

    hbm = pl.BlockSpec(memory_space=pltpu.HBM)
    sem = pl.BlockSpec(memory_space=pltpu.SEMAPHORE)
    outs = pl.pallas_call(
        body, name=name,
        out_shape=tuple(pltpu.HBM(b.shape, b.dtype) for b in buffers),
        in_specs=[hbm] * (ns + nl) + [sem] * (2 * n) + [HBM_SPEC],
        out_specs=tuple([hbm] * (ns + nl)),
        input_output_aliases={i: i for i in range(ns + nl)},
        compiler_params=pltpu.CompilerParams(has_side_effects=pltpu.SideEffectType.DATAFLOW_SIDE_EFFECTING),
    )(*buffers, *sems, after)
    return list(outs[:ns]), list(outs[ns:])


def _pair_sum(pos, ws, g32s, recvs):
    n = len(ws)

    def body(pos_ref, *refs):
        del pos_ref
        g_refs, r_refs = refs[:n], refs[n:2 * n]
        p32_refs, p16_refs = refs[2 * n:3 * n], refs[3 * n:]
        x, y, _ = _mesh_pos()
        for i in range(n):
            tot = g_refs[i][...] + r_refs[i][...].astype(F32)
            p16_refs[i][...] = tot.astype(BF16)

            @pl.when(pl.program_id(0) == 2 * x + y)
            def _(i=i, tot=tot):
                p32_refs[i][...] = tot

    halves = [(SHARD_SHAPES[w][0] // 2, SHARD_SHAPES[w][1]) for w in ws]
    own = [pl.BlockSpec((None, None) + h, lambda j, pos_ref: (j, pos_ref[0], 0, 0)) for h in halves]
    blk = [pl.BlockSpec((None,) + h, lambda j, pos_ref: (j, 0, 0)) for h in halves]
    mine = [pl.BlockSpec(h, lambda j, pos_ref: (0, 0)) for h in halves]
    g4 = [g.reshape((N_SHARD, 2) + h) for g, h in zip(g32s, halves)]
    outs = pl.pallas_call(
        body, name="pair_sum_" + "_".join(str(w) for w in ws),
        grid_spec=pltpu.PrefetchScalarGridSpec(
            num_scalar_prefetch=1, grid=(N_SHARD,), in_specs=own + blk, out_specs=mine + blk),
        out_shape=[jax.ShapeDtypeStruct(h, F32) for h in halves]
        + [jax.ShapeDtypeStruct((N_SHARD,) + h, BF16) for h in halves],
        compiler_params=_params(("arbitrary",)),
    )(pos, *g4, *recvs)
    return outs[:n], outs[n:]


def _chip_sum(p32s, recvs):
    parts = 2

    def body(*refs):
        p_refs, r_refs, f_refs = refs[:N_BIG], refs[N_BIG:2 * N_BIG], refs[2 * N_BIG:]
        for w in range(N_BIG):
            f_refs[w][...] = ((p_refs[w][...] + r_refs[w][0].astype(F32)) + r_refs[w][1].astype(F32)) \
                + r_refs[w][2].astype(F32)

    quarters = [(r // 2 // parts, cc) for r, cc in SHARD_SHAPES]
    own = [pl.BlockSpec(qt, lambda i: (i, 0)) for qt in quarters]
    rcv = [pl.BlockSpec((3,) + qt, lambda i: (0, i, 0)) for qt in quarters]
    out = [pl.BlockSpec(qt, lambda i: (i, 0)) for qt in quarters]
    return pl.pallas_call(
        body, name="chip_sum", grid=(parts,), in_specs=own + rcv, out_specs=out,
        out_shape=[jax.ShapeDtypeStruct((r // 2, cc), F32) for r, cc in SHARD_SHAPES],
        compiler_params=_params(("arbitrary",)),
    )(*p32s, *recvs)


def _adamw(w, g, m, v):
    m_new = ADAM_B1 * m + (1.0 - ADAM_B1) * g
    v_new = ADAM_B2 * v + (1.0 - ADAM_B2) * (g * g)
    m_hat = m_new / (1.0 - ADAM_B1 ** ADAM_STEP)
    v_hat = v_new / (1.0 - ADAM_B2 ** ADAM_STEP)
    delta = -ADAM_LR * (m_hat / (jnp.sqrt(v_hat) + ADAM_EPS) + ADAM_WD * w)
    return delta, m_new, v_new


def _adam_half(name, pos, grads, ws, ms, vs, into=None):
    nb = 4
    which = (lambda ref: ref[0]) if into is None else (lambda ref: 1 - ref[0])

    def body(which_ref, *refs):
        del which_ref
        groups = [refs[i * N_BIG:(i + 1) * N_BIG] for i in range(4)]
        g_refs, w_refs, m_refs, v_refs = groups
        go_refs, do_refs, mo_refs, vo_refs = [refs[len(refs) - (4 - i) * N_BIG:len(refs) - (3 - i) * N_BIG]
                                              for i in range(4)]
        for w in range(N_BIG):
            g = g_refs[w][...]
            delta, m_new, v_new = _adamw(w_refs[w][...], g, m_refs[w][...], v_refs[w][...])
            go_refs[w][...] = g
            do_refs[w][...] = delta
            mo_refs[w][...] = m_new
            vo_refs[w][...] = v_new

    blocks = [(r // 2 // nb, cc) for r, cc in SHARD_SHAPES]
    half = [pl.BlockSpec(b, lambda i, which_ref: (i, 0)) for b in blocks]
    full = [pl.BlockSpec((None,) + b, lambda i, which_ref: (0, which(which_ref) * nb + i, 0)) for b in blocks]
    shapes = [jax.ShapeDtypeStruct((1,) + shp, F32) for shp in SHARD_SHAPES]
    carried = [] if into is None else [a for kind in into for a in kind]
    first = 1 + 4 * N_BIG
    outs = pl.pallas_call(
        body, name=name,
        grid_spec=pltpu.PrefetchScalarGridSpec(
            num_scalar_prefetch=1, grid=(nb,), in_specs=half + full * 3 + [HBM_SPEC] * len(carried),
            out_specs=full * 4),
        out_shape=shapes * 4,
        input_output_aliases={first + i: i for i in range(len(carried))},
        compiler_params=_params(("arbitrary",)),
    )(pos, *grads, *ws, *ms, *vs, *carried)
    return [outs[i * N_BIG:(i + 1) * N_BIG] for i in range(4)]


def _chip_sum_adam_send(pos, p32s, recvs, ws, ms, vs, barrier_id):
    nb = 4
    blocks = [(r // 2 // nb, cc) for r, cc in SHARD_SHAPES]

    def body(pos_ref, *refs):
        del pos_ref
        groups = [refs[i * N_BIG:(i + 1) * N_BIG] for i in range(10)]
        p_refs, r_refs, w_refs, m_refs, v_refs, go_refs, do_refs, mo_refs, vo_refs, land_refs = groups
        send_scr = refs[10 * N_BIG:11 * N_BIG]
        send_sems, recv_sems = refs[11 * N_BIG:]
        i = pl.program_id(0)
        x, y, c = _mesh_pos()

        @pl.when(i == 0)
        def _():
            _shake_hands("sibling")

        def copy(w, blk):
            rows = pl.ds(pl.multiple_of(blk * blocks[w][0], 8), blocks[w][0])
            return pltpu.make_async_remote_copy(
                src_ref=send_scr[w].at[blk], dst_ref=land_refs[w].at[rows, :],
                send_sem=send_sems.at[w, blk], recv_sem=recv_sems.at[w, blk],
                device_id=(x, y, 1 - c), device_id_type=MESH)

        for w in range(N_BIG):
            g = ((p_refs[w][...] + r_refs[w][0].astype(F32)) + r_refs[w][1].astype(F32)) \
                + r_refs[w][2].astype(F32)
            send_scr[w][i] = g
            copy(w, i).start()
            delta, m_new, v_new = _adamw(w_refs[w][...], g, m_refs[w][...], v_refs[w][...])
            go_refs[w][...] = g
            do_refs[w][...] = delta
            mo_refs[w][...] = m_new
            vo_refs[w][...] = v_new

        @pl.when(i == nb - 1)
        def _():
            for w in range(N_BIG):
                for blk in range(nb):
                    copy(w, blk).wait()

    own = [pl.BlockSpec(b, lambda i, pos_ref: (i, 0)) for b in blocks]
    rcv = [pl.BlockSpec((3,) + b, lambda i, pos_ref: (0, i, 0)) for b in blocks]
    full = [pl.BlockSpec((None,) + b, lambda i, pos_ref: (0, pos_ref[0] * nb + i, 0)) for b in blocks]
    shapes = [jax.ShapeDtypeStruct((1,) + shp, F32) for shp in SHARD_SHAPES]
    outs = pl.pallas_call(
        body, name="chip_sum_adam_own_half",
        grid_spec=pltpu.PrefetchScalarGridSpec(
            num_scalar_prefetch=1, grid=(nb,), in_specs=own + rcv + full * 3,
            out_specs=full * 4 + [HBM_SPEC] * N_BIG,
            scratch_shapes=[pltpu.VMEM((nb,) + b, F32) for b in blocks]
            + [pltpu.SemaphoreType.DMA((N_BIG, nb)), pltpu.SemaphoreType.DMA((N_BIG, nb))]),
        out_shape=shapes * 4 + [jax.ShapeDtypeStruct((r // 2, cc), F32) for r, cc in SHARD_SHAPES],
        compiler_params=pltpu.CompilerParams(dimension_semantics=("arbitrary",), vmem_limit_bytes=VMEM_LIMIT,
                                             collective_id=barrier_id),
    )(pos, *p32s, *recvs, *ws, *ms, *vs)
    return [outs[k * N_BIG:(k + 1) * N_BIG] for k in range(4)], outs[4 * N_BIG:]


SMALL_ROWS = 8
ROW_CONV_B, ROW_POOL_SCALE, ROW_LN1_G, ROW_LN1_B, ROW_LN2_G, ROW_LN2_B, ROW_LOSS = range(7)
SMALL_VECS = ((ROW_CONV_B, D_FF), (ROW_POOL_SCALE, POOL_W), (ROW_LN1_G, D_MODEL), (ROW_LN1_B, D_MODEL),
              (ROW_LN2_G, D_MODEL), (ROW_LN2_B, D_MODEL))


def _small_pack(loss, vec_grads):
    def body(*refs):
        loss_ref, gvec, out_ref = refs[0], refs[1:-1], refs[-1]
        out_ref[...] = jnp.zeros_like(out_ref)
        for (row, n), ref in zip(SMALL_VECS, gvec):
            out_ref[row:row + 1, 0:n] = ref[...]
        out_ref[ROW_LOSS:ROW_LOSS + 1, 0:HEAD_DIM] = jnp.broadcast_to(loss_ref[...], (1, HEAD_DIM))

    return pl.pallas_call(
        body, name="small_pack", in_specs=[_whole()] * (1 + len(vec_grads)), out_specs=_whole(),
        out_shape=jax.ShapeDtypeStruct((SMALL_ROWS, D_FF), F32),
    )(loss, *vec_grads)


def _small_pair_sum(own, sibling):
    n = len(own)

    def body(*refs):
        x, y, _ = _mesh_pos()
        for i in range(n):
            refs[2 * n + i][2 * x + y] = refs[i][...] + refs[n + i][...]

    return pl.pallas_call(
        body, name="small_pair_sum", in_specs=[_whole()] * (2 * n), out_specs=[_whole()] * n,
        out_shape=[jax.ShapeDtypeStruct((N_SHARD,) + a.shape, F32) for a in own],
        compiler_params=pltpu.CompilerParams(vmem_limit_bytes=VMEM_LIMIT),
    )(*own, *sibling)


def _small_chip_rider(gathered):
    n = len(gathered)

    def make(inplace, srcs, lands, send_sems, recv_sems):
        del inplace, lands
        x, y, c = _mesh_pos()
        j0 = 2 * x + y
        starts, waits = [], []
        for i in range(n):
            for k, chip in enumerate(_other_chips(x, y)):
                sems = dict(send_sem=send_sems.at[3 * i + k], recv_sem=recv_sems.at[3 * i + k],
                            device_id=(chip[0], chip[1], c), device_id_type=MESH)
                send = pltpu.make_async_remote_copy(src_ref=srcs[i].at[j0], dst_ref=srcs[i].at[j0], **sems)
                arrival = pltpu.make_async_remote_copy(
                    src_ref=srcs[i].at[j0], dst_ref=srcs[i].at[2 * chip[0] + chip[1]], **sems)
                starts.append(send)
                waits += [arrival.wait_recv, send.wait_send]
        return starts, waits

    return _Rider([], gathered, [], 3 * n, make)


def _small_adam(all_a, all_b, all_c, wp, cwp, vec_ws, m_wp, m_cwp, vec_ms, v_wp, v_cwp, vec_vs):
    nv = len(SMALL_VECS)
    np_ = 2 + nv

    def body(*refs):
        all_a_ref, all_b_ref, all_c_ref = refs[0:3]
        w_all, m_all, v_all = (refs[3 + i * np_:3 + (i + 1) * np_] for i in range(3))
        loss_out = refs[3 + 3 * np_]
        outs = refs[4 + 3 * np_:]
        x, y, _ = _mesh_pos()
        j0 = 2 * x + y
        tot_a = ((all_a_ref[0] + all_a_ref[1]) + all_a_ref[2]) + all_a_ref[3]
        tot_b = ((all_b_ref[0] + all_b_ref[1]) + all_b_ref[2]) + all_b_ref[3]
        tot_c = ((all_c_ref[0, j0] + all_c_ref[1, j0]) + all_c_ref[2, j0]) + all_c_ref[3, j0]
        loss_out[...] = tot_b[ROW_LOSS:ROW_LOSS + 1, 0:1]
        grads = [tot_a, tot_c] + [tot_b[row:row + 1, 0:n] for row, n in SMALL_VECS]
        for p in range(np_):
            for at, g in ([(j, tot_c[j:j + 1]) for j in range(3)] if p == 1 else [(Ellipsis, grads[p])]):
                delta, m_new, v_new = _adamw(w_all[p][at], g, m_all[p][at], v_all[p][at])
                outs[p][at] = g
                outs[np_ + p][at] = delta
                outs[2 * np_ + p][at] = m_new
                outs[3 * np_ + p][at] = v_new

    pshapes = [wp.shape, CW_SHARD] + [wv.shape for wv in vec_ws]
    out_shape = [jax.ShapeDtypeStruct((1, 1), F32)] + [jax.ShapeDtypeStruct(s, F32) for s in pshapes] * 4
    outs = pl.pallas_call(
        body, name="small_adam",
        in_specs=[_whole()] * (3 + 3 * np_), out_specs=[_whole()] * len(out_shape), out_shape=out_shape,
        compiler_params=pltpu.CompilerParams(vmem_limit_bytes=VMEM_LIMIT),
    )(all_a, all_b, all_c, wp, cwp, *vec_ws, m_wp, m_cwp, *vec_ms, v_wp, v_cwp, *vec_vs)
    return outs[0], [outs[1 + i * np_:1 + (i + 1) * np_] for i in range(4)]


def kernel(x, w_in, w_pool, pool_scale, w_out, ln1_g, ln1_b, w_up, conv_w, conv_b, w_down, ln2_g, ln2_b, loss_target, m_w_in, m_w_pool, m_pool_scale, m_w_out, m_ln1_g, m_ln1_b, m_w_up, m_conv_w, m_conv_b, m_w_down, m_ln2_g, m_ln2_b, v_w_in, v_w_pool, v_pool_scale, v_w_out, v_ln1_g, v_ln1_b, v_w_up, v_conv_w, v_conv_b, v_w_down, v_ln2_g, v_ln2_b):
    pos = lax.axis_index("c").astype(jnp.int32).reshape(1)
    order = ("w_in", "w_out", "w_up", "w_down")
    w_in_i, w_out_i, w_up_i, w_down_i = range(N_BIG)
    vec_names = ("conv_b", "pool_scale", "ln1_g", "ln1_b", "ln2_g", "ln2_b")

    taps_first = lambda a: jnp.transpose(a, (1, 0, 2))
    gathered = _gather_weights([w_in[0], w_out[0], w_up[0], w_down[0]], taps_first(conv_w), w_pool[0], (w_in_i,))
    cw_full = jnp.transpose(gathered[N_BIG].reshape(N_SHARD, 3, DOWN_SH), (1, 0, 2)).reshape(3, D_FF)
    up_a, up_b, up_c = (0, 176), (176, 176), (352, 160)
    assert up_c[0] + up_c[1] == SHARD_SHAPES[w_up_i][0] // 2

    class MeshComm:
        def __init__(self):
            self.w = {i: gathered[i] for i in range(N_BIG)}
            self.g32, self.g16, self.p32, self.p16, self.recv_b = {}, {}, {}, {}, {}
            self.up_complete = False
            self.tokens, self.chips = {}, []

        def weight(self, name):
            i = order.index(name)
            if name == "w_up" and not self.up_complete:
                (arrs, _), = _comm_only("gather_up_last", [_gather_rider(
                    {i: self.w[i]}, [("d2d_diag", i, up_b), ("d2d", i, up_c)], (12, "sibling"))])
                self.w[i], self.up_complete = arrs[0], True
            full = self.w[i]
            return full.reshape(-1, full.shape[-1]) if name in ("w_out", "w_down") else full

        def _gather(self, ws, ops, handshake):
            return _gather_rider({w: self.w[w] for w in ws}, ops, handshake), ("w", ws)

        def _pair(self, ws):
            return _pair_rider(ws, [self.g16[w] for w in ws]), ("recv_a", ws)

        def _chip(self, ws):
            return _chip_rider(ws, [self.p16[w] for w in ws]), ("recv_b", ws)

        def plan(self, call):
            out_all, down_all = _whole_half(w_out_i), _whole_half(w_down_i)
            if call == "proj_pool":
                return [self._gather([w_out_i, w_up_i, w_down_i],
                                     [("ici", w_out_i, out_all), ("nbr", w_down_i, down_all),
                                      ("nbr", w_up_i, up_a)], (9, "chips"))]
            if call == "retention_fwd":
                return [self._gather([w_out_i, w_up_i, w_down_i],
                                     [("d2d", w_out_i, out_all),
                                      ("relay", w_down_i, down_all), ("d2d_nbr", w_down_i, down_all),
                                      ("relay", w_up_i, up_a), ("d2d_nbr", w_up_i, up_a), ("nbr", w_up_i, up_b)],
                                     (10, "both"))]
            if call == "outproj_ln1":
                return [self._gather([w_up_i, w_down_i],
                                     [("d2d_diag", w_down_i, down_all), ("d2d_diag", w_up_i, up_a),
                                      ("relay", w_up_i, up_b), ("d2d_nbr", w_up_i, up_b), ("ici", w_up_i, up_c)],
                                     (11, "both"))]
            return []

        def after(self, call):
            return tuple(self.tokens.pop(call, ()))

        def riders(self, call):
            self.pending = self.plan(call)
            return [r for r, _ in self.pending]

        def _start(self, name, rider, before, handshake):
            state, token = _split_start(name, rider, handshake)
            self.tokens.setdefault(before, []).append(token)
            return state

        def _finish_pair(self, name, state, ws, after):
            _, lands = _split_wait(name, state, after)
            self._finish_sum(ws, lands)

        def landed(self, call, results, outs):
            for (_, (slot, ws)), (inplace, lands) in zip(self.pending, results):
                for w, arr in zip(ws, inplace if len(inplace) else lands):
                    getattr(self, slot)[w] = arr
            if call == "wgrad_out":
                self._finish_pair("pair_exchange_up_wait", self.pair_up, [w_up_i], outs[1])
                self.chips.append(([w_up_i], self._start(
                    "chip_exchange_up_start", self._chip([w_up_i])[0], "wgrad_down", (5, "chips"))))
            if call == "mix_bwd":
                ws = [w_out_i, w_down_i]
                self._finish_pair("pair_exchange_out_down_wait", self.pair_out_down, ws, outs[0])
            if call == "retention_bwd":
                own, sibling = _split_wait("small_pair_wait", self.small_pair, outs[0])
                self.small_chip = self._start(
                    "small_chip_start", _small_chip_rider(_small_pair_sum(own, sibling)), "wgrad_in",
                    (6, "chips"))

        def small_gradients(self, loss, small):
            dcw4 = jnp.transpose(small["conv_w"].reshape(3, N_SHARD, DOWN_SH), (1, 0, 2))
            own = [small["w_pool"], _small_pack(loss, [small[n] for n in vec_names]), dcw4]
            ws = [w_out_i, w_down_i]
            parts = [self._chip(ws)[0], _final_rider(own)]
            chip, self.small_pair = _split_parts(
                self._start("chip_out_down_small_pair_start", _merged_rider(parts), "retention_bwd",
                            (7, "both")), parts)
            self.chips.append((ws, chip))

        def gradient(self, name, g32, g16):
            w = order.index(name)
            shape = (N_SHARD,) + SHARD_SHAPES[w]
            self.g32[w], self.g16[w] = g32.reshape(shape), g16.reshape(shape)
            if name == "w_up":
                self.pair_up = self._start("pair_exchange_up_start", self._pair([w])[0], "wgrad_out",
                                           (1, "sibling"))
            if name == "w_down":
                self.pair_out_down = self._start("pair_exchange_out_down_start",
                                                 self._pair([w_out_i, w_down_i])[0], "mix_bwd", (2, "sibling"))

        def wgrad_in(self, xb, dproj):
            w = w_in_i
            g32, landed = _wgrad_send(xb, dproj, IN_SH, "wgrad_in", 4, after=self.after("wgrad_in"))
            self.g32[w] = g32
            self._finish_sum([w], [landed])
            self.chips.append(([w], self._start("chip_exchange_in_start", self._chip([w])[0], "dx", (8, "chips"))))

        def _finish_sum(self, ws, lands):
            p32s, p16s = _pair_sum(pos, ws, [self.g32[w] for w in ws], lands)
            for w, p32, p16 in zip(ws, p32s, p16s):
                self.p32[w], self.p16[w] = p32, p16

        def finish(self, after):
            for n, (ws, state) in enumerate(self.chips):
                _, lands = _split_wait("chip_exchange_wait_%d" % n, state, after)
                for w, arr in zip(ws, lands):
                    self.recv_b[w] = arr
            return _split_wait("small_chip_wait", self.small_chip, after)[0]

    comm = MeshComm()
    loss, grad_x, small = _local_step(x[0], loss_target[0], cw_full, conv_b, gathered[N_BIG + 1], pool_scale,
                                      ln1_g, ln1_b, ln2_g, ln2_b, comm)

    given = dict(w_pool=w_pool, pool_scale=pool_scale, ln1_g=ln1_g, ln1_b=ln1_b, conv_w=conv_w, conv_b=conv_b,
                 ln2_g=ln2_g, ln2_b=ln2_b)
    given_m = dict(w_pool=m_w_pool, pool_scale=m_pool_scale, ln1_g=m_ln1_g, ln1_b=m_ln1_b, conv_w=m_conv_w,
                   conv_b=m_conv_b, ln2_g=m_ln2_g, ln2_b=m_ln2_b)
    given_v = dict(w_pool=v_w_pool, pool_scale=v_pool_scale, ln1_g=v_ln1_g, ln1_b=v_ln1_b, conv_w=v_conv_w,
                   conv_b=v_conv_b, ln2_g=v_ln2_g, ln2_b=v_ln2_b)
    args = []
    for src in (given, given_m, given_v):
        args += [src["w_pool"][0], taps_first(src["conv_w"]), [src[n] for n in vec_names]]
    small_sums = comm.finish(grad_x)
    loss_tot, small_out = _small_adam(*small_sums, *args)
    every = range(N_BIG)
    big = ([w_in, w_out, w_up, w_down], [m_w_in, m_w_out, m_w_up, m_w_down], [v_w_in, v_w_out, v_w_up, v_w_down])
    own_half, theirs = _chip_sum_adam_send(pos, [comm.p32[w] for w in every], [comm.recv_b[w] for w in every],
                                           *big, 3)
    big_out = _adam_half("adam_other_half", pos, theirs, *big, into=own_half)

    names = ("w_in", "w_pool", "pool_scale", "w_out", "ln1_g", "ln1_b", "w_up", "conv_w", "conv_b", "w_down",
             "ln2_g", "ln2_b")
    small_names = ("w_pool", "conv_w") + vec_names
    result = [loss_tot.reshape(()), grad_x[None]]
    for kind in range(4):
        for n in names:
            if n in order:
                result.append(big_out[kind][order.index(n)])
            else:
                val = small_out[kind][small_names.index(n)]
                if n == "conv_w":
                    val = taps_first(val)
                elif n == "w_pool":
                    val = val[None]
                result.append(val)
    return tuple(result)
```

```python
import functools

import numpy as np
import jax
import jax.numpy as jnp
from jax import lax
from jax.experimental import pallas as pl
from jax.experimental.pallas import tpu as pltpu

F32 = jnp.float32
BF16 = jnp.bfloat16

D_MODEL = 1024
HEADS = 4
HEAD_DIM = 128
RET_W = HEADS * HEAD_DIM
POOL_WINDOWS = (2, 4, 8, 16)
POOL_W = 512
IN_W = 4 * RET_W + POOL_W
D_FF = 2816
N_SHARD = 4
IN_SH = IN_W // N_SHARD
UP_SH = 2 * D_FF // N_SHARD
DOWN_SH = D_FF // N_SHARD
OUT_SH = D_MODEL // N_SHARD
ROPE_BASE = 10000.0
LN_EPS = 1e-5
RMS_EPS = 1e-6
ALPHA = 2.0 ** 0.25
K_SCALE = HEAD_DIM ** -0.5
SUPER = 256
CHUNK = 64
POOL_HALO = 16
CONV_HALO = 8
FFN_STRIP = 128
LN_ROWS = 32

ADAM_LR = 0.001
ADAM_B1 = 0.9
ADAM_B2 = 0.999
ADAM_EPS = 1e-08
ADAM_WD = 0.01
ADAM_STEP = 10

MESH = pl.DeviceIdType.MESH
VMEM_LIMIT = 56 * 1024 * 1024


def _dot(a, b):
    return jnp.dot(a, b, preferred_element_type=F32)


def _dot_nt(a, b):
    return lax.dot_general(a, b, (((1,), (1,)), ((), ())), preferred_element_type=F32)


def _dot_tn(a, b):
    return lax.dot_general(a, b, (((0,), (0,)), ((), ())), preferred_element_type=F32)


def _sigmoid(x):
    return 1.0 / (1.0 + jnp.exp(-x))


def _params(sem):
    return pltpu.CompilerParams(dimension_semantics=sem, vmem_limit_bytes=VMEM_LIMIT)


def _whole():
    return pl.BlockSpec(memory_space=pltpu.VMEM)


HBM_SPEC = pl.BlockSpec(memory_space=pl.ANY)


class _Rider:
    def __init__(self, inplace, srcs, lands, n_copies, make, handshake=None):
        self.inplace, self.srcs, self.lands, self.n_copies, self.make = list(inplace), list(srcs), list(lands), n_copies, make
        self.handshake = handshake


def _call(body, *, name, grid, in_specs, out_specs, out_shape, operands, scratch_shapes=(), sem=(),
          aliases=None, riders=(), after=()):
    n_in, n_out, n_scr = len(in_specs), len(out_shape), len(scratch_shapes)
    in_specs, out_specs, out_shape = list(in_specs), list(out_specs), list(out_shape)
    operands, scratch_shapes, aliases = list(operands), list(scratch_shapes), dict(aliases or {})
    in_specs += [_whole()] * len(after)
    operands += list(after)
    shakes = [r.handshake for r in riders if r.handshake is not None]
    assert len(shakes) <= 1
    for r in riders:
        for a in r.inplace:
            aliases[len(in_specs)] = len(out_shape)
            in_specs.append(HBM_SPEC)
            operands.append(a)
            out_specs.append(HBM_SPEC)
            out_shape.append(jax.ShapeDtypeStruct(a.shape, a.dtype))
        for a in r.srcs:
            in_specs.append(HBM_SPEC)
            operands.append(a)
        for shp in r.lands:
            out_specs.append(HBM_SPEC)
            out_shape.append(shp)
        scratch_shapes += [pltpu.SemaphoreType.DMA((r.n_copies,)), pltpu.SemaphoreType.DMA((r.n_copies,))]

    def full(*refs):
        ins = refs[:n_in]
        at = n_in + len(after)
        r_srcs = []
        for r in riders:
            at += len(r.inplace)
            r_srcs.append(refs[at:at + len(r.srcs)])
            at += len(r.srcs)
        outs = refs[at:at + n_out]
        at += n_out
        r_outs = []
        for r in riders:
            r_outs.append((refs[at:at + len(r.inplace)], refs[at + len(r.inplace):at + len(r.inplace) + len(r.lands)]))
            at += len(r.inplace) + len(r.lands)
        scr = refs[at:at + n_scr]
        at += n_scr
        r_sems = [refs[at + 2 * i:at + 2 * i + 2] for i in range(len(riders))]

        def copies():
            return [r.make(r_outs[i][0], r_srcs[i], r_outs[i][1], r_sems[i][0], r_sems[i][1])
                    for i, r in enumerate(riders)]

        def start():
            if shakes:
                _shake_hands(shakes[0][1])
            for starts, _ in copies():
                for cp in starts:
                    cp.start()

        def finish():
            for _, waits in copies():
                for wait in waits:
                    wait()

        if riders and grid:
            first = functools.reduce(jnp.logical_and, [pl.program_id(d) == 0 for d in range(len(grid))])
            last = functools.reduce(jnp.logical_and, [pl.program_id(d) == grid[d] - 1 for d in range(len(grid))])
            pl.when(first)(start)
            body(*ins, *outs, *scr)
            pl.when(last)(finish)
        else:
            if riders:
                start()
            body(*ins, *outs, *scr)
            if riders:
                finish()

    barrier_id = shakes[0][0] if shakes else None
    params = pltpu.CompilerParams(vmem_limit_bytes=VMEM_LIMIT, collective_id=barrier_id,
                                  **(dict(dimension_semantics=sem) if grid else {}))
    res = pl.pallas_call(
        full, name=name, grid=grid, in_specs=in_specs, out_specs=out_specs, out_shape=out_shape,
        scratch_shapes=scratch_shapes, input_output_aliases=aliases, compiler_params=params,
    )(*operands)
    outs, at, rider_res = res[:n_out], n_out, []
    for r in riders:
        rider_res.append((res[at:at + len(r.inplace)], res[at + len(r.inplace):at + len(r.inplace) + len(r.lands)]))
        at += len(r.inplace) + len(r.lands)
    return list(outs), rider_res


def _gammas():
    return [1.0 - 2.0 ** (-5.0 - h) for h in range(HEADS)]


def _decay_tables():
    idx = np.arange(SUPER)
    dist = np.abs(idx[:, None] - idx[None, :]).astype(np.float64)
    visible = (idx[None, :] // CHUNK) <= (idx[:, None] // CHUNK)
    mask = np.stack([np.where(visible, g ** dist, 0.0) for g in _gammas()])
    qd = np.concatenate([np.repeat((g ** (idx + 1.0))[:, None], HEAD_DIM, 1) for g in _gammas()], 1)
    kd = np.concatenate([np.repeat((g ** (SUPER - 1.0 - idx))[:, None], HEAD_DIM, 1) for g in _gammas()], 1)
    return (jnp.asarray(mask, F32), jnp.asarray(qd, F32), jnp.asarray(kd, F32))


def _rope_tables(s):
    inv_freq = ROPE_BASE ** (-np.arange(0, HEAD_DIM, 2, dtype=np.float64) / HEAD_DIM)
    ang = np.arange(s, dtype=np.float64)[:, None] * inv_freq[None, :]
    cos, sin = np.cos(ang), np.sin(ang)
    return (jnp.asarray(np.concatenate([cos, cos], 1), F32),
            jnp.asarray(np.concatenate([-sin, sin], 1), F32))


def _rope(t, cosf, sinf):
    return t * cosf + pltpu.roll(t, HEAD_DIM // 2, 1) * sinf


def _rope_t(t, cosf, sinf):
    return t * cosf - pltpu.roll(t, HEAD_DIM // 2, 1) * sinf


def _layernorm_fwd(z):
    mu = jnp.mean(z, axis=-1, keepdims=True)
    zc = z - mu
    var = jnp.mean(zc * zc, axis=-1, keepdims=True)
    rstd = lax.rsqrt(var + LN_EPS)
    return zc * rstd, rstd


def _layernorm_bwd(dy, xhat, rstd, gain):
    dxh = dy * gain
    m1 = jnp.mean(dxh, axis=-1, keepdims=True)
    m2 = jnp.mean(dxh * xhat, axis=-1, keepdims=True)
    return rstd * (dxh - m1 - xhat * m2)


def _proj_pool(x, win4, cosf, sinf, wpool, pscale, ts, riders=(), after=()):
    s = x.shape[0]
    nt = s // ts

    def body(x_ref, w_ref, cos_ref, sin_ref, wp_ref, ps_ref,
             xb_ref, q_ref, k_ref, v_ref, g_ref, pooled_ref, cat_ref, proj_scr, pext_scr):
        i = pl.program_id(0)
        xb = x_ref[...].astype(BF16)
        xb_ref[...] = xb
        for j in range(N_SHARD):
            proj_scr[:, j * IN_SH:(j + 1) * IN_SH] = _dot(xb, w_ref[j])
        cosf_t = cos_ref[...]
        sinf_t = sin_ref[...]
        for h in range(HEADS):
            lo = h * HEAD_DIM
            q_ref[:, lo:lo + HEAD_DIM] = _rope(proj_scr[:, lo:lo + HEAD_DIM], cosf_t, sinf_t).astype(BF16)
            kk = _rope(proj_scr[:, RET_W + lo:RET_W + lo + HEAD_DIM], cosf_t, sinf_t) * K_SCALE
            k_ref[:, lo:lo + HEAD_DIM] = kk.astype(BF16)
        v_ref[...] = proj_scr[:, 2 * RET_W:3 * RET_W].astype(BF16)
        g_ref[...] = proj_scr[:, 3 * RET_W:4 * RET_W]

        @pl.when(i == 0)
        def _():
            pext_scr[0:POOL_HALO, :] = jnp.zeros((POOL_HALO, POOL_W), F32)

        pext_scr[POOL_HALO:POOL_HALO + ts, :] = proj_scr[:, 4 * RET_W:IN_W]
        pos = (i * ts + lax.broadcasted_iota(jnp.int32, (ts, 1), 0) + 1).astype(F32)
        for gi, w in enumerate(POOL_WINDOWS):
            lo = gi * HEAD_DIM
            ext = pext_scr[:, lo:lo + HEAD_DIM]
            acc = ext
            shift = 1
            while shift < w:
                acc = acc + pltpu.roll(acc, shift, 0)
                shift *= 2
            tok = ext[POOL_HALO:POOL_HALO + ts]
            pooled = acc[POOL_HALO:POOL_HALO + ts] / jnp.minimum(pos, float(w)) - tok
            pooled_b = pooled.astype(BF16)
            pooled_ref[:, lo:lo + HEAD_DIM] = pooled_b
            lin = _dot(pooled_b, wp_ref[gi])
            cat_ref[:, lo:lo + HEAD_DIM] = (lin * ps_ref[:, lo:lo + HEAD_DIM]).astype(BF16)
        pext_scr[0:POOL_HALO, :] = pext_scr[ts:ts + POOL_HALO, :]

    tile = lambda w: pl.BlockSpec((ts, w), lambda i: (i, 0))
    return _call(
        body, name="proj_pool", grid=(nt,),
        in_specs=[tile(D_MODEL), _whole(), tile(HEAD_DIM), tile(HEAD_DIM), _whole(), _whole()],
        out_specs=[tile(D_MODEL), tile(RET_W), tile(RET_W), tile(RET_W), tile(RET_W), tile(POOL_W),
                   pl.BlockSpec((ts, POOL_W), lambda i: (i, 1))],
        out_shape=[jax.ShapeDtypeStruct((s, D_MODEL), BF16), jax.ShapeDtypeStruct((s, RET_W), BF16),
                   jax.ShapeDtypeStruct((s, RET_W), BF16), jax.ShapeDtypeStruct((s, RET_W), BF16),
                   jax.ShapeDtypeStruct((s, RET_W), F32), jax.ShapeDtypeStruct((s, POOL_W), BF16),
                   jax.ShapeDtypeStruct((s, 2 * RET_W), BF16)],
        scratch_shapes=[pltpu.VMEM((ts, IN_W), F32), pltpu.VMEM((ts + POOL_HALO, POOL_W), F32)],
        sem=("arbitrary",), operands=(x, win4, cosf, sinf, wpool, pscale), riders=riders, after=after,
    )


def _retention_fwd(q, k, v, g, cat, mask, qd, kd, riders=(), after=()):
    s = q.shape[0]
    ns = s // SUPER
    cdec = [gm ** float(SUPER) for gm in _gammas()]

    def body(q_ref, k_ref, v_ref, g_ref, cat_in, mask_ref, qd_ref, kd_ref,
             ret_ref, cat_ref, st_ref, state_scr):
        del cat_in
        n = pl.program_id(0)

        @pl.when(n == 0)
        def _():
            state_scr[...] = jnp.zeros_like(state_scr)

        for h in range(HEADS):
            sl = slice(h * HEAD_DIM, (h + 1) * HEAD_DIM)
            qh, kh, vh = q_ref[:, sl], k_ref[:, sl], v_ref[:, sl]
            sc = _dot_nt(qh, kh) * mask_ref[h]
            st = state_scr[h]
            stb = st.astype(BF16)
            st_ref[0, h] = stb
            qdb = (qh.astype(F32) * qd_ref[:, sl]).astype(BF16)
            kdb = (kh.astype(F32) * kd_ref[:, sl]).astype(BF16)
            ret = _dot(sc.astype(BF16), vh) + _dot(qdb, stb)
            state_scr[h] = st * cdec[h] + _dot_tn(kdb, vh)
            ret_ref[:, sl] = ret
            r = lax.rsqrt(jnp.mean(ret * ret, axis=-1, keepdims=True) + RMS_EPS)
            gh = g_ref[:, sl]
            cat_ref[:, sl] = ((ret * r) * (gh * _sigmoid(gh))).astype(BF16)

    tile = pl.BlockSpec((SUPER, RET_W), lambda n: (n, 0))
    return _call(
        body, name="retention_fwd", grid=(ns,),
        in_specs=[tile, tile, tile, tile, HBM_SPEC, _whole(), _whole(), _whole()],
        out_specs=[tile, tile, pl.BlockSpec((1, HEADS, HEAD_DIM, HEAD_DIM), lambda n: (n, 0, 0, 0))],
        out_shape=[jax.ShapeDtypeStruct((s, RET_W), F32), jax.ShapeDtypeStruct((s, 2 * RET_W), BF16),
                   jax.ShapeDtypeStruct((ns, HEADS, HEAD_DIM, HEAD_DIM), BF16)],
        scratch_shapes=[pltpu.VMEM((HEADS, HEAD_DIM, HEAD_DIM), F32)],
        aliases={4: 1}, sem=("arbitrary",), operands=(q, k, v, g, cat, mask, qd, kd), riders=riders,
        after=after,
    )


def _outproj_ln1(x, cat, wout, g1, b1, ts, riders=(), after=()):
    s = x.shape[0]

    def body(x_ref, cat_ref, w_ref, g_ref, b_ref, xhat_ref, rstd_ref, h1b_ref):
        z = ALPHA * x_ref[...] + _dot(cat_ref[...], w_ref[...])
        xhat, rstd = _layernorm_fwd(z)
        xhat_ref[...] = xhat
        rstd_ref[...] = rstd
        h1b_ref[...] = (xhat * g_ref[...] + b_ref[...]).astype(BF16)

    tile = lambda w: pl.BlockSpec((ts, w), lambda i: (i, 0))
    return _call(
        body, name="outproj_ln1", grid=(s // ts,),
        in_specs=[tile(D_MODEL), tile(D_MODEL), _whole(), _whole(), _whole()],
        out_specs=[tile(D_MODEL), tile(1), tile(D_MODEL)],
        out_shape=[jax.ShapeDtypeStruct((s, D_MODEL), F32), jax.ShapeDtypeStruct((s, 1), F32),
                   jax.ShapeDtypeStruct((s, D_MODEL), BF16)],
        sem=("arbitrary",), operands=(x, cat, wout, g1, b1), riders=riders, after=after,
    )


def _ffn_fwd_loss(xhat1, h1b, target, wup4, wdown, cw, cb, g1, b1, g2, b2, ts):
    s = xhat1.shape[0]

    def body(xhat_ref, h1b_ref, tgt_ref, wup_ref, wdn_ref, cw_ref, cb_ref, g1_ref, b1_ref, g2_ref, b2_ref,
             ub_ref, act_ref, sd_ref, dz2_ref, dz2b_ref, loss_ref, dg2_ref, db2_ref, val_scr, gext_scr, ffn_scr):
        i = pl.program_id(0)

        @pl.when(i == 0)
        def _():
            gext_scr[0:CONV_HALO, :] = jnp.zeros((CONV_HALO, D_FF), F32)
            loss_ref[...] = jnp.zeros_like(loss_ref)
            dg2_ref[...] = jnp.zeros_like(dg2_ref)
            db2_ref[...] = jnp.zeros_like(db2_ref)

        for half in range(2):
            lo = half * UP_SH
            gext_scr[CONV_HALO:CONV_HALO + ts, lo:lo + UP_SH] = _dot(h1b_ref[...], wup_ref[2 + half])
            val_scr[:, lo:lo + UP_SH] = _dot(h1b_ref[...], wup_ref[half])
            for c0 in range(lo, lo + UP_SH, FFN_STRIP):
                cols = slice(c0, c0 + FFN_STRIP)
                ext = gext_scr[:, cols]
                gate = ext[CONV_HALO:]
                hc = cb_ref[:, cols] + ((pltpu.roll(ext, 2, 0)[CONV_HALO:] * cw_ref[0:1, cols]
                                         + pltpu.roll(ext, 1, 0)[CONV_HALO:] * cw_ref[1:2, cols])
                                        + gate * cw_ref[2:3, cols])
                val = val_scr[:, cols]
                sg = _sigmoid(hc)
                si = hc * sg
                act_ref[:, cols] = (si * val).astype(BF16)
                ub_ref[:, cols] = val.astype(BF16)
                ub_ref[:, D_FF + c0:D_FF + c0 + FFN_STRIP] = gate.astype(BF16)
                sd_ref[:, cols] = hc.astype(BF16)
            part = _dot(act_ref[:, lo:lo + UP_SH], wdn_ref[lo:lo + UP_SH, :])
            if half == 0:
                ffn_scr[...] = part
            else:
                ffn_scr[...] += part

        gext_scr[0:CONV_HALO, :] = gext_scr[ts:ts + CONV_HALO, :]

        loss_acc = jnp.zeros((1, 1), F32)
        dg2_acc = jnp.zeros((1, D_MODEL), F32)
        db2_acc = jnp.zeros((1, D_MODEL), F32)
        for r0 in range(0, ts, LN_ROWS):
            rows = slice(r0, r0 + LN_ROWS)
            h1 = xhat_ref[rows, :] * g1_ref[...] + b1_ref[...]
            xhat2, rstd2 = _layernorm_fwd(ALPHA * h1 + ffn_scr[rows, :])
            diff = (xhat2 * g2_ref[...] + b2_ref[...]) - tgt_ref[rows, :]
            row = jnp.mean(diff * diff, axis=-1, keepdims=True)
            loss_acc = loss_acc + 0.5 * jnp.sum(row, axis=0, keepdims=True)
            dy = diff * (1.0 / D_MODEL)
            dg2_acc = dg2_acc + jnp.sum(dy * xhat2, axis=0, keepdims=True)
            db2_acc = db2_acc + jnp.sum(dy, axis=0, keepdims=True)
            dz2 = _layernorm_bwd(dy, xhat2, rstd2, g2_ref[...])
            dz2_ref[rows, :] = dz2
            dz2b_ref[rows, :] = dz2.astype(BF16)
        loss_ref[...] += loss_acc
        dg2_ref[...] += dg2_acc
        db2_ref[...] += db2_acc

    tile = lambda w: pl.BlockSpec((ts, w), lambda i: (i, 0))
    acc = lambda w: pl.BlockSpec((1, w), lambda i: (0, 0))
    return pl.pallas_call(
        body, name="ffn_fwd_loss", grid=(s // ts,),
        in_specs=[tile(D_MODEL), tile(D_MODEL), tile(D_MODEL)] + [_whole()] * 8,
        out_specs=[tile(2 * D_FF), tile(D_FF), tile(D_FF), tile(D_MODEL), tile(D_MODEL),
                   acc(1), acc(D_MODEL), acc(D_MODEL)],
        out_shape=[jax.ShapeDtypeStruct((s, 2 * D_FF), BF16), jax.ShapeDtypeStruct((s, D_FF), BF16),
                   jax.ShapeDtypeStruct((s, D_FF), BF16), jax.ShapeDtypeStruct((s, D_MODEL), F32),
                   jax.ShapeDtypeStruct((s, D_MODEL), BF16),
                   jax.ShapeDtypeStruct((1, 1), F32), jax.ShapeDtypeStruct((1, D_MODEL), F32),
                   jax.ShapeDtypeStruct((1, D_MODEL), F32)],
        scratch_shapes=[pltpu.VMEM((ts, D_FF), F32), pltpu.VMEM((ts + CONV_HALO, D_FF), F32),
                        pltpu.VMEM((ts, D_MODEL), F32)],
        compiler_params=_params(("arbitrary",)),
    )(xhat1, h1b, target, wup4, wdown, cw, cb, g1, b1, g2, b2)


def _ffn_bwd(dz2, dz2b, ub, sd, xhat1, rstd1, wup4, wdown, cw, g1, ts):
    s = dz2.shape[0]
    nt = s // ts

    def body(dz2_ref, dz2b_ref, ub_ref, sd_ref, xhat_ref, rstd_ref, wup_ref, wdn_ref, cw_ref, g1_ref,
             dub_ref, dz1_ref, dz1b_ref, dg1_ref, db1_ref, dcw_ref, dcb_ref, dext_scr, da_scr):
        i = pl.program_id(0)

        @pl.when(i == 0)
        def _():
            dext_scr[ts:ts + CONV_HALO, :] = jnp.zeros((CONV_HALO, D_FF), F32)
            dg1_ref[...] = jnp.zeros_like(dg1_ref)
            db1_ref[...] = jnp.zeros_like(db1_ref)
            dcw_ref[...] = jnp.zeros_like(dcw_ref)
            dcb_ref[...] = jnp.zeros_like(dcb_ref)

        da_scr[...] = _dot_nt(dz2b_ref[...], wdn_ref[...])
        n_ext = ts + CONV_HALO
        for c0 in range(0, D_FF, FFN_STRIP):
            cols = slice(c0, c0 + FFN_STRIP)
            gcols = slice(D_FF + c0, D_FF + c0 + FFN_STRIP)
            val = ub_ref[:, cols].astype(F32)
            gate = ub_ref[:, gcols].astype(F32)
            da = da_scr[:, cols]
            hc = sd_ref[:, cols].astype(F32)
            sg = _sigmoid(hc)
            dhc = da * val * (sg * (1.0 + hc * (1.0 - sg)))
            dext_scr[0:ts, cols] = dhc
            dext = dext_scr[:, cols]
            dhc1 = pltpu.roll(dext, n_ext - 1, 0)[0:ts]
            dhc2 = pltpu.roll(dext, n_ext - 2, 0)[0:ts]
            dcb_ref[:, cols] += jnp.sum(dhc, axis=0, keepdims=True)
            dcw_ref[0:1, cols] += jnp.sum(dhc2 * gate, axis=0, keepdims=True)
            dcw_ref[1:2, cols] += jnp.sum(dhc1 * gate, axis=0, keepdims=True)
            dcw_ref[2:3, cols] += jnp.sum(dhc * gate, axis=0, keepdims=True)
            dgate = dhc * cw_ref[2:3, cols] + dhc1 * cw_ref[1:2, cols] + dhc2 * cw_ref[0:1, cols]
            dub_ref[:, cols] = (da * (hc * sg)).astype(BF16)
            dub_ref[:, gcols] = dgate.astype(BF16)
        dext_scr[ts:n_ext, :] = dext_scr[0:CONV_HALO, :]
        dh1 = ALPHA * dz2_ref[...]
        for j in range(N_SHARD):
            dh1 = dh1 + _dot_nt(dub_ref[:, j * UP_SH:(j + 1) * UP_SH], wup_ref[j])
        xhat = xhat_ref[...]
        dg1_ref[...] += jnp.sum(dh1 * xhat, axis=0, keepdims=True)
        db1_ref[...] += jnp.sum(dh1, axis=0, keepdims=True)
        dz1 = _layernorm_bwd(dh1, xhat, rstd_ref[...], g1_ref[...])
        dz1_ref[...] = dz1
        dz1b_ref[...] = dz1.astype(BF16)

    tile = lambda w: pl.BlockSpec((ts, w), lambda i: (nt - 1 - i, 0))
    acc = lambda rws, w: pl.BlockSpec((rws, w), lambda i: (0, 0))
    return pl.pallas_call(
        body, name="ffn_bwd", grid=(nt,),
        in_specs=[tile(D_MODEL), tile(D_MODEL), tile(2 * D_FF), tile(D_FF), tile(D_MODEL), tile(1)]
        + [_whole()] * 4,
        out_specs=[tile(2 * D_FF), tile(D_MODEL), tile(D_MODEL), acc(1, D_MODEL), acc(1, D_MODEL),
                   acc(3, D_FF), acc(1, D_FF)],
        out_shape=[jax.ShapeDtypeStruct((s, 2 * D_FF), BF16),
                   jax.ShapeDtypeStruct((s, D_MODEL), F32), jax.ShapeDtypeStruct((s, D_MODEL), BF16),
                   jax.ShapeDtypeStruct((1, D_MODEL), F32),
                   jax.ShapeDtypeStruct((1, D_MODEL), F32), jax.ShapeDtypeStruct((3, D_FF), F32),
                   jax.ShapeDtypeStruct((1, D_FF), F32)],
        scratch_shapes=[pltpu.VMEM((ts + CONV_HALO, D_FF), F32), pltpu.VMEM((ts, D_FF), F32)],
        compiler_params=_params(("arbitrary",)),
    )(dz2, dz2b, ub, sd, xhat1, rstd1, wup4, wdown, cw, g1)


def _mix_bwd(dz1, pooled, ret, g, wout, wpool, pscale, ts, riders=(), after=()):
    s = dz1.shape[0]
    nt = s // ts

    def body(dz1_ref, pooled_ref, ret_ref, g_ref, wout_ref, wp_ref, ps_ref,
             dret_ref, dgp_ref, dwp_ref, dps_ref, eext_scr):
        i = pl.program_id(0)
        r = nt - 1 - i

        @pl.when(i == 0)
        def _():
            eext_scr[ts:ts + POOL_HALO, :] = jnp.zeros((POOL_HALO, POOL_W), F32)
            dwp_ref[...] = jnp.zeros_like(dwp_ref)
            dps_ref[...] = jnp.zeros_like(dps_ref)

        dzb = dz1_ref[...].astype(BF16)
        dcat_r = _dot_nt(dzb, wout_ref[0:RET_W, :])
        dcat_p = _dot_nt(dzb, wout_ref[RET_W:2 * RET_W, :])
        pos = (r * ts + lax.broadcasted_iota(jnp.int32, (ts, 1), 0) + 1).astype(F32)
        dpooled = []
        for gi, w in enumerate(POOL_WINDOWS):
            sl = slice(gi * HEAD_DIM, (gi + 1) * HEAD_DIM)
            pb = pooled_ref[:, sl]
            dy = dcat_p[:, sl]
            dps_ref[:, sl] += jnp.sum(dy * _dot(pb, wp_ref[gi]), axis=0, keepdims=True)
            dlin = (dy * ps_ref[:, sl]).astype(BF16)
            dwp_ref[gi] += _dot_tn(pb, dlin)
            dpg = _dot_nt(dlin, wp_ref[gi])
            dpooled.append(dpg)
            eext_scr[0:ts, sl] = dpg / jnp.minimum(pos, float(w))
        for gi, w in enumerate(POOL_WINDOWS):
            sl = slice(gi * HEAD_DIM, (gi + 1) * HEAD_DIM)
            acc = eext_scr[:, sl]
            shift = 1
            while shift < w:
                acc = acc + pltpu.roll(acc, ts + POOL_HALO - shift, 0)
                shift *= 2
            dgp_ref[:, RET_W + gi * HEAD_DIM:RET_W + (gi + 1) * HEAD_DIM] = (acc[0:ts] - dpooled[gi]).astype(BF16)
        eext_scr[ts:ts + POOL_HALO, :] = eext_scr[0:POOL_HALO, :]
        for h in range(HEADS):
            sl = slice(h * HEAD_DIM, (h + 1) * HEAD_DIM)
            rt = ret_ref[:, sl]
            rr = lax.rsqrt(jnp.mean(rt * rt, axis=-1, keepdims=True) + RMS_EPS)
            rn = rt * rr
            gh = g_ref[:, sl]
            sg = _sigmoid(gh)
            dy = dcat_r[:, sl]
            dgp_ref[:, sl] = (dy * rn * (sg * (1.0 + gh * (1.0 - sg)))).astype(BF16)
            drn = dy * (gh * sg)
            dret_ref[:, sl] = (rr * (drn - rn * jnp.mean(drn * rn, axis=-1, keepdims=True))).astype(BF16)

    tile = lambda w: pl.BlockSpec((ts, w), lambda i: (nt - 1 - i, 0))
    return _call(
        body, name="mix_bwd", grid=(nt,),
        in_specs=[tile(D_MODEL), tile(POOL_W), tile(RET_W), tile(RET_W), _whole(), _whole(), _whole()],
        out_specs=[tile(RET_W), tile(2 * RET_W),
                   pl.BlockSpec((len(POOL_WINDOWS), HEAD_DIM, HEAD_DIM), lambda i: (0, 0, 0)),
                   pl.BlockSpec((1, POOL_W), lambda i: (0, 0))],
        out_shape=[jax.ShapeDtypeStruct((s, RET_W), BF16), jax.ShapeDtypeStruct((s, 2 * RET_W), BF16),
                   jax.ShapeDtypeStruct((len(POOL_WINDOWS), HEAD_DIM, HEAD_DIM), F32),
                   jax.ShapeDtypeStruct((1, POOL_W), F32)],
        scratch_shapes=[pltpu.VMEM((ts + POOL_HALO, POOL_W), F32)],
        sem=("arbitrary",), operands=(dz1, pooled, ret, g, wout, wpool, pscale), riders=riders,
        after=after,
    )


def _retention_bwd(q, k, v, dret, dgp, states, mask, qd, kd, cosf, sinf, riders=(), after=()):
    s = q.shape[0]
    ns = s // SUPER
    cdec = [gm ** float(SUPER) for gm in _gammas()]

    def body(q_ref, k_ref, v_ref, do_ref, dgp_ref, st_ref, mask_ref, qd_ref, kd_ref, cos_ref, sin_ref,
             dproj_ref, dstate_scr):
        i = pl.program_id(0)

        @pl.when(i == 0)
        def _():
            dstate_scr[...] = jnp.zeros_like(dstate_scr)

        cosf_t = cos_ref[...]
        sinf_t = sin_ref[...]
        for h in range(HEADS):
            sl = slice(h * HEAD_DIM, (h + 1) * HEAD_DIM)
            qh, kh, vh, doh = q_ref[:, sl], k_ref[:, sl], v_ref[:, sl], do_ref[:, sl]
            dscb = (_dot_nt(doh, vh) * mask_ref[0, h]).astype(BF16)
            dsctb = (_dot_nt(vh, doh) * mask_ref[1, h]).astype(BF16)
            sctb = (_dot_nt(kh, qh) * mask_ref[1, h]).astype(BF16)
            stb = st_ref[0, h]
            dst = dstate_scr[h]
            dstb = dst.astype(BF16)
            qdb = (qh.astype(F32) * qd_ref[:, sl]).astype(BF16)
            kdb = (kh.astype(F32) * kd_ref[:, sl]).astype(BF16)
            dq = _dot(dscb, kh) + _dot_nt(doh, stb) * qd_ref[:, sl]
            dk = _dot(dsctb, qh) + _dot_nt(vh, dstb) * kd_ref[:, sl]
            dv = _dot(sctb, doh) + _dot(kdb, dstb)
            dstate_scr[h] = dst * cdec[h] + _dot_tn(qdb, doh)
            lo = h * HEAD_DIM
            dproj_ref[:, lo:lo + HEAD_DIM] = _rope_t(dq, cosf_t, sinf_t).astype(BF16)
            dproj_ref[:, RET_W + lo:RET_W + lo + HEAD_DIM] = _rope_t(dk * K_SCALE, cosf_t, sinf_t).astype(BF16)
            dproj_ref[:, 2 * RET_W + lo:2 * RET_W + lo + HEAD_DIM] = dv.astype(BF16)
        dproj_ref[:, 3 * RET_W:IN_W] = dgp_ref[...]

    tile = lambda w: pl.BlockSpec((SUPER, w), lambda i: (ns - 1 - i, 0))
    return _call(
        body, name="retention_bwd", grid=(ns,),
        in_specs=[tile(RET_W), tile(RET_W), tile(RET_W), tile(RET_W), tile(2 * RET_W),
                  pl.BlockSpec((1, HEADS, HEAD_DIM, HEAD_DIM), lambda i: (ns - 1 - i, 0, 0, 0)),
                  _whole(), _whole(), _whole(), tile(HEAD_DIM), tile(HEAD_DIM)],
        out_specs=[tile(IN_W)],
        out_shape=[jax.ShapeDtypeStruct((s, IN_W), BF16)],
        scratch_shapes=[pltpu.VMEM((HEADS, HEAD_DIM, HEAD_DIM), F32)],
        sem=("arbitrary",), operands=(q, k, v, dret, dgp, states, mask, qd, kd, cosf, sinf), riders=riders,
        after=after,
    )


def _dx(dz1, dproj, win4, ts, riders=(), after=()):
    s = dz1.shape[0]

    def body(dz1_ref, dp_ref, w_ref, dx_ref):
        acc = ALPHA * dz1_ref[...]
        for j in range(N_SHARD):
            acc = acc + _dot_nt(dp_ref[:, j * IN_SH:(j + 1) * IN_SH], w_ref[j])
        dx_ref[...] = acc

    tile = lambda w: pl.BlockSpec((ts, w), lambda i: (i, 0))
    return _call(
        body, name="dx", grid=(s // ts,),
        in_specs=[tile(D_MODEL), tile(IN_W), _whole()],
        out_specs=[tile(D_MODEL)],
        out_shape=[jax.ShapeDtypeStruct((s, D_MODEL), F32)],
        sem=("arbitrary",), operands=(dz1, dproj, win4), riders=riders, after=after,
    )


def _wgrad(a, b, tm, tn, name, stacked, m_outer, riders=(), after=()):
    s, m = a.shape
    n = b.shape[1]

    def body(a_ref, b_ref, o32_ref, o16_ref):
        res = _dot_tn(a_ref[...], b_ref[...])
        o32_ref[...] = res.reshape(o32_ref.shape)
        o16_ref[...] = res.astype(BF16).reshape(o16_ref.shape)

    if m_outer:
        grid, blocks = (m // tm, n // tn), (lambda g0, g1: (g0, g1))
    else:
        grid, blocks = (n // tn, m // tm), (lambda g0, g1: (g1, g0))
    if stacked:
        shape = (n // tn, m, tn)
        ospec = pl.BlockSpec((1, tm, tn), lambda g0, g1: (blocks(g0, g1)[1], blocks(g0, g1)[0], 0))
    else:
        shape = (m, n)
        ospec = pl.BlockSpec((tm, tn), lambda g0, g1: blocks(g0, g1))
    return _call(
        body, name=name, grid=grid,
        in_specs=[pl.BlockSpec((s, tm), lambda g0, g1: (0, blocks(g0, g1)[0])),
                  pl.BlockSpec((s, tn), lambda g0, g1: (0, blocks(g0, g1)[1]))],
        out_specs=[ospec, ospec],
        out_shape=[jax.ShapeDtypeStruct(shape, F32), jax.ShapeDtypeStruct(shape, BF16)],
        sem=("arbitrary", "arbitrary"), operands=(a, b), riders=riders, after=after,
    )


def _wgrad_send(a, b, tn, name, barrier_id, after=()):
    s, m = a.shape
    n = b.shape[1]
    nb, hm = n // tn, m // 2

    def body(*refs):
        a_ref, b_ref = refs[:2]
        o32_ref, land_ref, send_scr, send_sems, recv_sems = refs[2 + len(after):]
        j = pl.program_id(0)
        x, y, c = _mesh_pos()

        @pl.when(j == 0)
        def _():
            _shake_hands("sibling")

        o32_ref[0] = _dot_tn(a_ref[...], b_ref[...])
        theirs = pl.ds(pl.multiple_of((1 - c) * hm, 16), hm)
        copies = [pltpu.make_async_remote_copy(
            src_ref=send_scr.at[blk], dst_ref=land_ref.at[blk], send_sem=send_sems.at[blk],
            recv_sem=recv_sems.at[blk], device_id=(x, y, 1 - c), device_id_type=MESH) for blk in range(nb)]
        for blk in range(nb):
            @pl.when(j == blk)
            def _(blk=blk):
                send_scr[blk] = o32_ref[0, theirs, :].astype(BF16)
                copies[blk].start()

        @pl.when(j == nb - 1)
        def _():
            for cp in copies:
                cp.wait()

    return pl.pallas_call(
        body, name=name, grid=(nb,),
        in_specs=[pl.BlockSpec((s, m), lambda j: (0, 0)), pl.BlockSpec((s, tn), lambda j: (0, j))]
        + [_whole()] * len(after),
        out_specs=[pl.BlockSpec((1, m, tn), lambda j: (j, 0, 0)), HBM_SPEC],
        out_shape=[jax.ShapeDtypeStruct((nb, m, tn), F32), jax.ShapeDtypeStruct((nb, hm, tn), BF16)],
        scratch_shapes=[pltpu.VMEM((nb, hm, tn), BF16), pltpu.SemaphoreType.DMA((nb,)),
                        pltpu.SemaphoreType.DMA((nb,))],
        compiler_params=pltpu.CompilerParams(dimension_semantics=("arbitrary",), vmem_limit_bytes=VMEM_LIMIT,
                                             collective_id=barrier_id),
    )(a, b, *after)


class _NoComm:
    def __init__(self, win4, wout, wup4, wdown):
        self.weights = dict(w_in=win4, w_out=wout, w_up=wup4, w_down=wdown)
        self.grads = {}

    def weight(self, name):
        return self.weights[name]

    def riders(self, call):
        return ()

    def after(self, call):
        return ()

    def landed(self, call, results, outs):
        pass

    def small_gradients(self, loss, small):
        pass

    def gradient(self, name, g32, g16):
        self.grads[name] = (g32, g16)

    def wgrad_in(self, xb, dproj):
        (g32, g16), _ = _wgrad(xb, dproj, D_MODEL, IN_SH, "wgrad_in", True, True)
        self.gradient("w_in", g32, g16)


def _local_step(x, target, cw, cb, wpool_b, pscale, g1, b1, g2, b2, comm):
    s = x.shape[0]
    ts_a = min(512, s)
    ts_f = min(256, s)
    mask, qd, kd = _decay_tables()
    cosf, sinf = _rope_tables(s)

    def run(call, fn, *args):
        outs, res = fn(*args, riders=comm.riders(call), after=comm.after(call))
        comm.landed(call, res, outs)
        return outs

    xb, q, k, v, g, pooled, cat = run("proj_pool", _proj_pool, x, comm.weight("w_in"), cosf, sinf, wpool_b,
                                      pscale, ts_a)
    ret, cat, states = run("retention_fwd", _retention_fwd, q, k, v, g, cat, mask, qd, kd)
    wout = comm.weight("w_out")
    xhat1, rstd1, h1b = run("outproj_ln1", _outproj_ln1, x, cat, wout, g1, b1, ts_a)
    wup4, wdown = comm.weight("w_up"), comm.weight("w_down")
    ub, act, sd, dz2, dz2b, loss, dg2, db2 = _ffn_fwd_loss(xhat1, h1b, target, wup4, wdown, cw, cb, g1, b1, g2, b2,
                                                           ts_f)

    dub, dz1, dz1b, dg1, db1, dcw, dcb = _ffn_bwd(dz2, dz2b, ub, sd, xhat1, rstd1, wup4, wdown, cw, g1, ts_f)
    half = D_MODEL // 2
    comm.gradient("w_up", *run("wgrad_up", _wgrad, h1b, dub, half, UP_SH, "wgrad_up", True, False))
    comm.gradient("w_out", *run("wgrad_out", _wgrad, cat, dz1b, D_MODEL, half, "wgrad_out", False, True))
    comm.gradient("w_down", *run("wgrad_down", _wgrad, act, dz2b, D_FF // 2, half, "wgrad_down", False, True))
    dret, dgp, dwp, dps = run("mix_bwd", _mix_bwd, dz1b, pooled, ret, g, wout, wpool_b, pscale, ts_a)
    small = dict(w_pool=dwp, pool_scale=dps, ln1_g=dg1, ln1_b=db1, conv_w=dcw, conv_b=dcb,
                 ln2_g=dg2, ln2_b=db2)
    comm.small_gradients(loss, small)
    mask_both = jnp.stack([mask, jnp.swapaxes(mask, 1, 2)])
    dproj, = run("retention_bwd", _retention_bwd, q, k, v, dret, dgp, states, mask_both, qd, kd, cosf, sinf)
    comm.wgrad_in(xb, dproj)
    (grad_x,), _ = _dx(dz1, dproj, comm.weight("w_in"), ts_a, after=comm.after("dx"))
    return loss, grad_x, small


CAST_ROWS = 64
SHARD_SHAPES = ((D_MODEL, IN_SH), (OUT_SH, D_MODEL), (D_MODEL, UP_SH), (DOWN_SH, D_MODEL))
N_BIG = len(SHARD_SHAPES)
CW_SHARD = (3, 1, DOWN_SH)


def _mesh_pos():
    return lax.axis_index("x"), lax.axis_index("y"), lax.axis_index("c")


def _other_chips(x, y):
    return [(1 - x, y), (x, 1 - y), (1 - x, 1 - y)]


def _shake_hands(peers):
    x, y, c = _mesh_pos()
    others = [(x, y, 1 - c)] if peers in ("sibling", "both", "all") else []
    if peers in ("chips", "both", "all"):
        others += [(chip[0], chip[1], c) for chip in _other_chips(x, y)]
    if peers == "all":
        others += [(chip[0], chip[1], 1 - c) for chip in _other_chips(x, y)]
    barrier = pltpu.get_barrier_semaphore()
    for peer in others:
        pl.semaphore_signal(barrier, inc=1, device_id=peer, device_id_type=MESH)
    pl.semaphore_wait(barrier, len(others))


def _half_rows(w, which):
    hr = SHARD_SHAPES[w][0] // 2
    return pl.ds(pl.multiple_of(which * hr, 16), hr)


def _gather_weights(shards, cw_shard, wpool, full):
    def body(*refs):
        in_refs = refs[:N_BIG]
        cw_ref, wpool_ref = refs[N_BIG:N_BIG + 2]
        out_refs = refs[N_BIG + 2:2 * N_BIG + 2]
        cwo_ref, wpool_b_ref = refs[2 * N_BIG + 2:2 * N_BIG + 4]
        stage = refs[2 * N_BIG + 4:3 * N_BIG + 4]
        raw = refs[3 * N_BIG + 4:4 * N_BIG + 4 - len(full)]
        send_sems, recv_sems, fsend_sems, frecv_sems, cw_send, cw_recv, local_sems, load_sems = \
            refs[4 * N_BIG + 4 - len(full):]
        x, y, c = _mesh_pos()
        j0 = 2 * x + y
        chips = _other_chips(x, y)

        fetched = [w for w in range(N_BIG) if w not in full]
        f32 = {w: in_refs[w] for w in full}
        loads = []
        for n, w in enumerate(fetched):
            f32[w] = raw[n]
            loads.append(pltpu.make_async_copy(in_refs[w], raw[n], load_sems.at[n]))
            loads[-1].start()

        def cast_to_stage(w):
            def cast(i, carry):
                rows = pl.ds(pl.multiple_of(i * CAST_ROWS, CAST_ROWS), CAST_ROWS)
                stage[w][rows, :] = f32[w][rows, :].astype(BF16)
                return carry
            lax.fori_loop(0, SHARD_SHAPES[w][0] // CAST_ROWS, cast, 0)

        for w in full:
            cast_to_stage(w)

        jx, jy, jd = 2 * (1 - x) + y, 2 * x + (1 - y), 2 * (1 - x) + (1 - y)
        neighbours = [((1 - x, y, c), jx), ((x, 1 - y, c), jy)]
        passed = jnp.where(c == 0, jx, jy)
        pass_to = (jnp.where(c == 0, x, 1 - x), jnp.where(c == 0, 1 - y, y), c)

        def nbr(w, k, block):
            return pltpu.make_async_remote_copy(
                src_ref=stage[w].at[_half_rows(w, c), :], dst_ref=out_refs[w].at[block, _half_rows(w, c), :],
                send_sem=send_sems.at[w, k], recv_sem=recv_sems.at[w, k],
                device_id=neighbours[k][0], device_id_type=MESH)

        def relay(w, block):
            return pltpu.make_async_remote_copy(
                src_ref=out_refs[w].at[passed, _half_rows(w, c), :],
                dst_ref=out_refs[w].at[block, _half_rows(w, c), :],
                send_sem=send_sems.at[w, 2], recv_sem=recv_sems.at[w, 2],
                device_id=pass_to, device_id_type=MESH)

        def d2d(w, k, block, half):
            return pltpu.make_async_remote_copy(
                src_ref=out_refs[w].at[block, _half_rows(w, half), :],
                dst_ref=out_refs[w].at[block, _half_rows(w, half), :],
                send_sem=fsend_sems.at[w, k], recv_sem=frecv_sems.at[w, k],
                device_id=(x, y, 1 - c), device_id_type=MESH)

        def conv(k, block):
            chip = chips[k]
            return pltpu.make_async_remote_copy(
                src_ref=cw_ref, dst_ref=cwo_ref.at[block], send_sem=cw_send.at[k], recv_sem=cw_recv.at[k],
                device_id=(chip[0], chip[1], c), device_id_type=MESH)

        sent = [nbr(w, k, j0) for w in full for k in range(2)] + [conv(k, j0) for k in range(3)]
        for cp in sent:
            cp.start()
        for n, w in enumerate(fetched):
            loads[n].wait()
            cast_to_stage(w)
        local = [pltpu.make_async_copy(stage[w], out_refs[w].at[j0], local_sems.at[w]) for w in range(N_BIG)]
        local.append(pltpu.make_async_copy(cw_ref, cwo_ref.at[j0], local_sems.at[N_BIG]))
        for cp in local:
            cp.start()
        wpool_b_ref[...] = wpool_ref[...].astype(BF16)
        for w in full:
            for k, (_, block) in enumerate(neighbours):
                nbr(w, k, block).wait_recv()
            later = [relay(w, passed)] + [d2d(w, k, block, c) for k, (_, block) in enumerate(neighbours)]
            for cp in later:
                cp.start()
            sent += later
        for w in full:
            relay(w, jd).wait_recv()
            fw = d2d(w, 2, jd, c)
            fw.start()
            sent.append(fw)
        for w in full:
            for k, block in enumerate([jx, jy, jd]):
                d2d(w, k, block, 1 - c).wait_recv()
        for k, chip in enumerate(chips):
            conv(k, 2 * chip[0] + chip[1]).wait_recv()
        for cp in sent:
            cp.wait_send()
        for cp in local:
            cp.wait()

    out_shape = [jax.ShapeDtypeStruct((N_SHARD,) + shp, BF16) for shp in SHARD_SHAPES]
    out_shape.append(jax.ShapeDtypeStruct((N_SHARD,) + CW_SHARD, F32))
    out_shape.append(jax.ShapeDtypeStruct(wpool.shape, BF16))
    return pl.pallas_call(
        body, name="gather_weights",
        in_specs=[_whole() if w in full else HBM_SPEC for w in range(N_BIG)] + [_whole()] * 2,
        out_specs=[HBM_SPEC] * (N_BIG + 1) + [_whole()],
        out_shape=out_shape,
        scratch_shapes=[pltpu.VMEM(shp, BF16) for shp in SHARD_SHAPES]
        + [pltpu.VMEM(shp, F32) for w, shp in enumerate(SHARD_SHAPES) if w not in full] + [
            pltpu.SemaphoreType.DMA((N_BIG, 3)), pltpu.SemaphoreType.DMA((N_BIG, 3)),
            pltpu.SemaphoreType.DMA((N_BIG, 3)), pltpu.SemaphoreType.DMA((N_BIG, 3)),
            pltpu.SemaphoreType.DMA((3,)), pltpu.SemaphoreType.DMA((3,)),
            pltpu.SemaphoreType.DMA((N_BIG + 1,)), pltpu.SemaphoreType.DMA((N_BIG - len(full),))],
        compiler_params=pltpu.CompilerParams(vmem_limit_bytes=VMEM_LIMIT),
    )(*shards, cw_shard, wpool)


def _gather_rider(arrays, ops, handshake=None):
    ws = sorted(arrays)

    def make(inplace, srcs, lands, send_sems, recv_sems):
        del srcs, lands
        x, y, c = _mesh_pos()
        j0, jx, jy, jd = 2 * x + y, 2 * (1 - x) + y, 2 * x + (1 - y), 2 * (1 - x) + (1 - y)
        x_nbr, y_nbr, sibling = (1 - x, y, c), (x, 1 - y, c), (x, y, 1 - c)
        starts, waits = [], []
        for n, (kind, w, (r0, nr)) in enumerate(ops):
            ref = inplace[ws.index(w)]
            hr = SHARD_SHAPES[w][0] // 2
            rows = lambda core: pl.ds(pl.multiple_of(core * hr + r0, 16), nr)
            mine, theirs = rows(c), rows(1 - c)
            if kind == "ici":
                moves = [(ref.at[j0, mine, :], x_nbr, ref.at[jx, mine, :]),
                         (ref.at[j0, mine, :], y_nbr, ref.at[jy, mine, :]),
                         (ref.at[j0, mine, :], (1 - x, 1 - y, c), ref.at[jd, mine, :])]
            elif kind == "nbr":
                moves = [(ref.at[j0, mine, :], x_nbr, ref.at[jx, mine, :]),
                         (ref.at[j0, mine, :], y_nbr, ref.at[jy, mine, :])]
            elif kind == "relay":
                passed = jnp.where(c == 0, jx, jy)
                to = (jnp.where(c == 0, x, 1 - x), jnp.where(c == 0, 1 - y, y), c)
                moves = [(ref.at[passed, mine, :], to, ref.at[jd, mine, :])]
            else:
                blocks = dict(d2d=[jx, jy, jd], d2d_nbr=[jx, jy], d2d_diag=[jd])[kind]
                moves = [(ref.at[b, mine, :], sibling, ref.at[b, theirs, :]) for b in blocks]
            for k, (src, to, landing) in enumerate(moves):
                sems = dict(send_sem=send_sems.at[3 * n + k], recv_sem=recv_sems.at[3 * n + k],
                            device_id=to, device_id_type=MESH)
                send = pltpu.make_async_remote_copy(src_ref=src, dst_ref=src, **sems)
                arrival = pltpu.make_async_remote_copy(src_ref=src, dst_ref=landing, **sems)
                starts.append(send)
                waits += [arrival.wait_recv, send.wait_send]
        return starts, waits

    return _Rider([arrays[w] for w in ws], [], [], 3 * len(ops), make, handshake)


def _whole_half(w):
    return (0, SHARD_SHAPES[w][0] // 2)


def _pair_rider(ws, g16s):
    def make(inplace, srcs, lands, send_sems, recv_sems):
        del inplace
        x, y, c = _mesh_pos()
        copies = [pltpu.make_async_remote_copy(
            src_ref=srcs[i].at[:, _half_rows(w, 1 - c), :], dst_ref=lands[i],
            send_sem=send_sems.at[i], recv_sem=recv_sems.at[i], device_id=(x, y, 1 - c), device_id_type=MESH)
            for i, w in enumerate(ws)]
        return copies, [cp.wait for cp in copies]

    lands = [jax.ShapeDtypeStruct((N_SHARD, SHARD_SHAPES[w][0] // 2, SHARD_SHAPES[w][1]), BF16) for w in ws]
    return _Rider([], g16s, lands, len(ws), make)


def _chip_rider(ws, p16s):
    def make(inplace, srcs, lands, send_sems, recv_sems):
        del inplace
        x, y, c = _mesh_pos()
        copies = []
        for i in range(len(ws)):
            for k, chip in enumerate(_other_chips(x, y)):
                copies.append(pltpu.make_async_remote_copy(
                    src_ref=srcs[i].at[2 * chip[0] + chip[1]], dst_ref=lands[i].at[k],
                    send_sem=send_sems.at[3 * i + k], recv_sem=recv_sems.at[3 * i + k],
                    device_id=(chip[0], chip[1], c), device_id_type=MESH))
        return copies, [cp.wait for cp in copies]

    lands = [jax.ShapeDtypeStruct((3, SHARD_SHAPES[w][0] // 2, SHARD_SHAPES[w][1]), BF16) for w in ws]
    return _Rider([], p16s, lands, 3 * len(ws), make)


def _final_rider(halves):
    def make(inplace, srcs, lands, send_sems, recv_sems):
        del inplace
        x, y, c = _mesh_pos()
        copies = [pltpu.make_async_remote_copy(
            src_ref=srcs[i], dst_ref=lands[i], send_sem=send_sems.at[i], recv_sem=recv_sems.at[i],
            device_id=(x, y, 1 - c), device_id_type=MESH) for i in range(len(halves))]
        return copies, [cp.wait for cp in copies]

    return _Rider([], halves, [jax.ShapeDtypeStruct(h.shape, h.dtype) for h in halves], len(halves), make)


N_DEV = 2 * N_SHARD


def _device_number(x, y, c):
    return 2 * (2 * x + y) + c


def _small_all_rider(own):
    n = len(own)

    def make(inplace, srcs, lands, send_sems, recv_sems):
        del inplace
        x, y, c = _mesh_pos()
        peers = [(x, y, 1 - c)] + [(chip[0], chip[1], core) for chip in _other_chips(x, y) for core in (c, 1 - c)]
        copies = []
        for i in range(n):
            for k, peer in enumerate(peers):
                copies.append(pltpu.make_async_remote_copy(
                    src_ref=srcs[i], dst_ref=lands[i].at[_device_number(x, y, c)],
                    send_sem=send_sems.at[(N_DEV - 1) * i + k], recv_sem=recv_sems.at[(N_DEV - 1) * i + k],
                    device_id=peer, device_id_type=MESH))
        return copies, [cp.wait for cp in copies]

    lands = [jax.ShapeDtypeStruct((N_DEV,) + a.shape, a.dtype) for a in own]
    return _Rider([], own, lands, (N_DEV - 1) * n, make)


def _comm_only(name, riders):
    _, res = _call(lambda: None, name=name, grid=(), in_specs=[], out_specs=[], out_shape=[], operands=(),
                   riders=riders)
    return res


class _SemList:
    def __init__(self, refs):
        self.at = list(refs)


def _merged_rider(riders):
    srcs = [a for r in riders for a in r.srcs]
    lands = [a for r in riders for a in r.lands]

    def make(inplace, src_refs, land_refs, send_sems, recv_sems):
        starts, waits = [], []
        s0 = l0 = c0 = 0
        for r in riders:
            part = r.make(inplace, src_refs[s0:s0 + len(r.srcs)], land_refs[l0:l0 + len(r.lands)],
                          _SemList(send_sems.at[c0:c0 + r.n_copies]), _SemList(recv_sems.at[c0:c0 + r.n_copies]))
            starts += part[0]
            waits += part[1]
            s0, l0, c0 = s0 + len(r.srcs), l0 + len(r.lands), c0 + r.n_copies
        return starts, waits

    return _Rider([], srcs, lands, sum(r.n_copies for r in riders), make)


def _split_start(name, rider, handshake=None):
    assert not rider.inplace
    ns, nl, n = len(rider.srcs), len(rider.lands), rider.n_copies
    barrier_id, peers = handshake if handshake is not None else (None, None)

    def body(*refs):
        if handshake is not None:
            _shake_hands(peers)
        srcs, lands = refs[:ns], refs[ns:ns + nl]
        sems = refs[ns + nl:ns + nl + 2 * n]
        token = refs[-1]
        starts, _ = rider.make([], srcs, lands, _SemList(sems[:n]), _SemList(sems[n:]))
        for cp in starts:
            cp.start()
        token[...] = jnp.zeros_like(token)

    buffers = [pltpu.with_memory_space_constraint(a, pltpu.HBM) for a in rider.srcs]
    buffers += [pltpu.with_memory_space_constraint(lax.empty(s.shape, s.dtype), pltpu.HBM) for s in rider.lands]
    hbm = pl.BlockSpec(memory_space=pltpu.HBM)
    sem = pl.BlockSpec(memory_space=pltpu.SEMAPHORE)
    outs = pl.pallas_call(
        body, name=name,
        out_shape=tuple([pltpu.SemaphoreType.DMA(())] * (2 * n) + [pltpu.HBM(b.shape, b.dtype) for b in buffers]
                        + [jax.ShapeDtypeStruct((8, 128), F32)]),
        in_specs=[hbm] * (ns + nl),
        out_specs=tuple([sem] * (2 * n) + [hbm] * (ns + nl) + [_whole()]),
        input_output_aliases={i: 2 * n + i for i in range(ns + nl)},
        compiler_params=pltpu.CompilerParams(has_side_effects=pltpu.SideEffectType.DATAFLOW_SIDE_EFFECTING,
                                             collective_id=barrier_id),
    )(*buffers)
    return (rider, outs[:2 * n], outs[2 * n:2 * n + ns + nl]), outs[-1]


def _split_parts(state, riders):
    merged, sems, buffers = state
    n, ns = merged.n_copies, len(merged.srcs)
    parts, s0, l0, c0 = [], 0, 0, 0
    for r in riders:
        parts.append((r, list(sems[c0:c0 + r.n_copies]) + list(sems[n + c0:n + c0 + r.n_copies]),
                      list(buffers[s0:s0 + len(r.srcs)]) + list(buffers[ns + l0:ns + l0 + len(r.lands)])))
        s0, l0, c0 = s0 + len(r.srcs), l0 + len(r.lands), c0 + r.n_copies
    return parts


def _split_wait(name, state, after):
    rider, sems, buffers = state
    ns, nl, n = len(rider.srcs), len(rider.lands), rider.n_copies

    def body(*refs):
        srcs, lands = refs[:ns], refs[ns:ns + nl]
        sem_refs = refs[ns + nl:ns + nl + 2 * n]
        _, waits = rider.make([], srcs, lands, _SemList(sem_refs[:n]), _SemList(sem_refs[n:]))
        for wait in waits:
            wait()

    hbm = pl.BlockSpec(memory_space=pltpu.HBM)
    sem = pl.BlockSpec(memory_space=pltpu.SEMAPHORE)
    outs = pl.pallas_call(
        body, name=name,
        out_shape=tuple(pltpu.HBM(b.shape, b.dtype) for b in buffers),
        in_specs=[hbm] * (ns + nl) + [sem] * (2 * n) + [HBM_SPEC],
        out_specs=tuple([hbm] * (ns + nl)),
        input_output_aliases={i: i for i in range(ns + nl)},
        compiler_params=pltpu.CompilerParams(has_side_effects=pltpu.SideEffectType.DATAFLOW_SIDE_EFFECTING),
    )(*buffers, *sems, after)
    return list(outs[:ns]), list(outs[ns:])


def _pair_sum(pos, ws, g32s, recvs):
    n = len(ws)

    def body(pos_ref, *refs):
        del pos_ref
        g_refs, r_refs = refs[:n], refs[n:2 * n]
        p32_refs, p16_refs = refs[2 * n:3 * n], refs[3 * n:]
        x, y, _ = _mesh_pos()
        for i in range(n):
            tot = g_refs[i][...] + r_refs[i][...].astype(F32)
            p16_refs[i][...] = tot.astype(BF16)

            @pl.when(pl.program_id(0) == 2 * x + y)
            def _(i=i, tot=tot):
                p32_refs[i][...] = tot

    halves = [(SHARD_SHAPES[w][0] // 2, SHARD_SHAPES[w][1]) for w in ws]
    own = [pl.BlockSpec((None, None) + h, lambda j, pos_ref: (j, pos_ref[0], 0, 0)) for h in halves]
    blk = [pl.BlockSpec((None,) + h, lambda j, pos_ref: (j, 0, 0)) for h in halves]
    mine = [pl.BlockSpec(h, lambda j, pos_ref: (0, 0)) for h in halves]
    g4 = [g.reshape((N_SHARD, 2) + h) for g, h in zip(g32s, halves)]
    outs = pl.pallas_call(
        body, name="pair_sum_" + "_".join(str(w) for w in ws),
        grid_spec=pltpu.PrefetchScalarGridSpec(
            num_scalar_prefetch=1, grid=(N_SHARD,), in_specs=own + blk, out_specs=mine + blk),
        out_shape=[jax.ShapeDtypeStruct(h, F32) for h in halves]
        + [jax.ShapeDtypeStruct((N_SHARD,) + h, BF16) for h in halves],
        compiler_params=_params(("arbitrary",)),
    )(pos, *g4, *recvs)
    return outs[:n], outs[n:]


def _chip_sum(p32s, recvs):
    parts = 2

    def body(*refs):
        p_refs, r_refs, f_refs = refs[:N_BIG], refs[N_BIG:2 * N_BIG], refs[2 * N_BIG:]
        for w in range(N_BIG):
            f_refs[w][...] = ((p_refs[w][...] + r_refs[w][0].astype(F32)) + r_refs[w][1].astype(F32)) \
                + r_refs[w][2].astype(F32)

    quarters = [(r // 2 // parts, cc) for r, cc in SHARD_SHAPES]
    own = [pl.BlockSpec(qt, lambda i: (i, 0)) for qt in quarters]
    rcv = [pl.BlockSpec((3,) + qt, lambda i: (0, i, 0)) for qt in quarters]
    out = [pl.BlockSpec(qt, lambda i: (i, 0)) for qt in quarters]
    return pl.pallas_call(
        body, name="chip_sum", grid=(parts,), in_specs=own + rcv, out_specs=out,
        out_shape=[jax.ShapeDtypeStruct((r // 2, cc), F32) for r, cc in SHARD_SHAPES],
        compiler_params=_params(("arbitrary",)),
    )(*p32s, *recvs)


def _adamw(w, g, m, v):
    m_new = ADAM_B1 * m + (1.0 - ADAM_B1) * g
    v_new = ADAM_B2 * v + (1.0 - ADAM_B2) * (g * g)
    m_hat = m_new / (1.0 - ADAM_B1 ** ADAM_STEP)
    v_hat = v_new / (1.0 - ADAM_B2 ** ADAM_STEP)
    delta = -ADAM_LR * (m_hat / (jnp.sqrt(v_hat) + ADAM_EPS) + ADAM_WD * w)
    return delta, m_new, v_new


def _adam_half(name, pos, grads, ws, ms, vs, into=None):
    nb = 4
    which = (lambda ref: ref[0]) if into is None else (lambda ref: 1 - ref[0])

    def body(which_ref, *refs):
        del which_ref
        groups = [refs[i * N_BIG:(i + 1) * N_BIG] for i in range(4)]
        g_refs, w_refs, m_refs, v_refs = groups
        go_refs, do_refs, mo_refs, vo_refs = [refs[len(refs) - (4 - i) * N_BIG:len(refs) - (3 - i) * N_BIG]
                                              for i in range(4)]
        for w in range(N_BIG):
            g = g_refs[w][...]
            delta, m_new, v_new = _adamw(w_refs[w][...], g, m_refs[w][...], v_refs[w][...])
            go_refs[w][...] = g
            do_refs[w][...] = delta
            mo_refs[w][...] = m_new
            vo_refs[w][...] = v_new

    blocks = [(r // 2 // nb, cc) for r, cc in SHARD_SHAPES]
    half = [pl.BlockSpec(b, lambda i, which_ref: (i, 0)) for b in blocks]
    full = [pl.BlockSpec((None,) + b, lambda i, which_ref: (0, which(which_ref) * nb + i, 0)) for b in blocks]
    shapes = [jax.ShapeDtypeStruct((1,) + shp, F32) for shp in SHARD_SHAPES]
    carried = [] if into is None else [a for kind in into for a in kind]
    first = 1 + 4 * N_BIG
    outs = pl.pallas_call(
        body, name=name,
        grid_spec=pltpu.PrefetchScalarGridSpec(
            num_scalar_prefetch=1, grid=(nb,), in_specs=half + full * 3 + [HBM_SPEC] * len(carried),
            out_specs=full * 4),
        out_shape=shapes * 4,
        input_output_aliases={first + i: i for i in range(len(carried))},
        compiler_params=_params(("arbitrary",)),
    )(pos, *grads, *ws, *ms, *vs, *carried)
    return [outs[i * N_BIG:(i + 1) * N_BIG] for i in range(4)]


SMALL_ROWS = 8
ROW_CONV_B, ROW_POOL_SCALE, ROW_LN1_G, ROW_LN1_B, ROW_LN2_G, ROW_LN2_B, ROW_LOSS = range(7)
SMALL_VECS = ((ROW_CONV_B, D_FF), (ROW_POOL_SCALE, POOL_W), (ROW_LN1_G, D_MODEL), (ROW_LN1_B, D_MODEL),
              (ROW_LN2_G, D_MODEL), (ROW_LN2_B, D_MODEL))


def _small_pack(loss, vec_grads):
    def body(*refs):
        loss_ref, gvec, out_ref = refs[0], refs[1:-1], refs[-1]
        out_ref[...] = jnp.zeros_like(out_ref)
        for (row, n), ref in zip(SMALL_VECS, gvec):
            out_ref[row:row + 1, 0:n] = ref[...]
        out_ref[ROW_LOSS:ROW_LOSS + 1, 0:HEAD_DIM] = jnp.broadcast_to(loss_ref[...], (1, HEAD_DIM))

    return pl.pallas_call(
        body, name="small_pack", in_specs=[_whole()] * (1 + len(vec_grads)), out_specs=_whole(),
        out_shape=jax.ShapeDtypeStruct((SMALL_ROWS, D_FF), F32),
    )(loss, *vec_grads)


def _small_pair_sum(own, sibling):
    n = len(own)

    def body(*refs):
        x, y, _ = _mesh_pos()
        for i in range(n):
            refs[2 * n + i][2 * x + y] = refs[i][...] + refs[n + i][...]

    return pl.pallas_call(
        body, name="small_pair_sum", in_specs=[_whole()] * (2 * n), out_specs=[_whole()] * n,
        out_shape=[jax.ShapeDtypeStruct((N_SHARD,) + a.shape, F32) for a in own],
        compiler_params=pltpu.CompilerParams(vmem_limit_bytes=VMEM_LIMIT),
    )(*own, *sibling)


def _small_chip_rider(gathered):
    n = len(gathered)

    def make(inplace, srcs, lands, send_sems, recv_sems):
        del inplace, lands
        x, y, c = _mesh_pos()
        j0 = 2 * x + y
        starts, waits = [], []
        for i in range(n):
            for k, chip in enumerate(_other_chips(x, y)):
                sems = dict(send_sem=send_sems.at[3 * i + k], recv_sem=recv_sems.at[3 * i + k],
                            device_id=(chip[0], chip[1], c), device_id_type=MESH)
                send = pltpu.make_async_remote_copy(src_ref=srcs[i].at[j0], dst_ref=srcs[i].at[j0], **sems)
                arrival = pltpu.make_async_remote_copy(
                    src_ref=srcs[i].at[j0], dst_ref=srcs[i].at[2 * chip[0] + chip[1]], **sems)
                starts.append(send)
                waits += [arrival.wait_recv, send.wait_send]
        return starts, waits

    return _Rider([], gathered, [], 3 * n, make)


def _small_adam(all_a, all_b, all_c, own_a, own_b, own_c, wp, cwp, vec_ws, m_wp, m_cwp, vec_ms, v_wp, v_cwp, vec_vs):
    nv = len(SMALL_VECS)
    np_ = 2 + nv

    def body(*refs):
        all_a_ref, all_b_ref, all_c_ref, own_a_ref, own_b_ref, own_c_ref = refs[0:6]
        refs = refs[3:]
        w_all, m_all, v_all = (refs[3 + i * np_:3 + (i + 1) * np_] for i in range(3))
        loss_out = refs[3 + 3 * np_]
        outs = refs[4 + 3 * np_:]
        x, y, c = _mesh_pos()
        j0 = 2 * x + y
        me = _device_number(x, y, c)

        def total(sent, own):
            by_dev = [jnp.where(me == d, own, sent(d)) for d in range(N_DEV)]
            chips = [by_dev[2 * j] + by_dev[2 * j + 1] for j in range(N_SHARD)]
            return ((chips[0] + chips[1]) + chips[2]) + chips[3]

        tot_a = total(lambda d: all_a_ref[d], own_a_ref[...])
        tot_b = total(lambda d: all_b_ref[d], own_b_ref[...])
        tot_c = total(lambda d: all_c_ref[d, j0], own_c_ref[j0])
        loss_out[...] = tot_b[ROW_LOSS:ROW_LOSS + 1, 0:1]
        grads = [tot_a, tot_c] + [tot_b[row:row + 1, 0:n] for row, n in SMALL_VECS]
        for p in range(np_):
            for at, g in ([(j, tot_c[j:j + 1]) for j in range(3)] if p == 1 else [(Ellipsis, grads[p])]):
                delta, m_new, v_new = _adamw(w_all[p][at], g, m_all[p][at], v_all[p][at])
                outs[p][at] = g
                outs[np_ + p][at] = delta
                outs[2 * np_ + p][at] = m_new
                outs[3 * np_ + p][at] = v_new

    pshapes = [wp.shape, CW_SHARD] + [wv.shape for wv in vec_ws]
    out_shape = [jax.ShapeDtypeStruct((1, 1), F32)] + [jax.ShapeDtypeStruct(s, F32) for s in pshapes] * 4
    outs = pl.pallas_call(
        body, name="small_adam",
        in_specs=[_whole()] * (6 + 3 * np_), out_specs=[_whole()] * len(out_shape), out_shape=out_shape,
        compiler_params=pltpu.CompilerParams(vmem_limit_bytes=VMEM_LIMIT),
    )(all_a, all_b, all_c, own_a, own_b, own_c, wp, cwp, *vec_ws, m_wp, m_cwp, *vec_ms, v_wp, v_cwp, *vec_vs)
    return outs[0], [outs[1 + i * np_:1 + (i + 1) * np_] for i in range(4)]


def kernel(x, w_in, w_pool, pool_scale, w_out, ln1_g, ln1_b, w_up, conv_w, conv_b, w_down, ln2_g, ln2_b, loss_target, m_w_in, m_w_pool, m_pool_scale, m_w_out, m_ln1_g, m_ln1_b, m_w_up, m_conv_w, m_conv_b, m_w_down, m_ln2_g, m_ln2_b, v_w_in, v_w_pool, v_pool_scale, v_w_out, v_ln1_g, v_ln1_b, v_w_up, v_conv_w, v_conv_b, v_w_down, v_ln2_g, v_ln2_b):
    pos = lax.axis_index("c").astype(jnp.int32).reshape(1)
    order = ("w_in", "w_out", "w_up", "w_down")
    w_in_i, w_out_i, w_up_i, w_down_i = range(N_BIG)
    vec_names = ("conv_b", "pool_scale", "ln1_g", "ln1_b", "ln2_g", "ln2_b")

    taps_first = lambda a: jnp.transpose(a, (1, 0, 2))
    gathered = _gather_weights([w_in[0], w_out[0], w_up[0], w_down[0]], taps_first(conv_w), w_pool[0], (w_in_i,))
    cw_full = jnp.transpose(gathered[N_BIG].reshape(N_SHARD, 3, DOWN_SH), (1, 0, 2)).reshape(3, D_FF)
    up_a, up_b, up_c = (0, 176), (176, 176), (352, 160)
    assert up_c[0] + up_c[1] == SHARD_SHAPES[w_up_i][0] // 2

    class MeshComm:
        def __init__(self):
            self.w = {i: gathered[i] for i in range(N_BIG)}
            self.g32, self.g16, self.p32, self.p16, self.recv_b = {}, {}, {}, {}, {}
            self.up_complete = False
            self.tokens, self.chips = {}, []

        def weight(self, name):
            i = order.index(name)
            if name == "w_up" and not self.up_complete:
                (arrs, _), = _comm_only("gather_up_last", [_gather_rider(
                    {i: self.w[i]}, [("d2d_diag", i, up_b), ("d2d", i, up_c)], (12, "sibling"))])
                self.w[i], self.up_complete = arrs[0], True
            full = self.w[i]
            return full.reshape(-1, full.shape[-1]) if name in ("w_out", "w_down") else full

        def _gather(self, ws, ops, handshake):
            return _gather_rider({w: self.w[w] for w in ws}, ops, handshake), ("w", ws)

        def _pair(self, ws):
            return _pair_rider(ws, [self.g16[w] for w in ws]), ("recv_a", ws)

        def _chip(self, ws):
            return _chip_rider(ws, [self.p16[w] for w in ws]), ("recv_b", ws)

        def plan(self, call):
            out_all, down_all = _whole_half(w_out_i), _whole_half(w_down_i)
            if call == "proj_pool":
                return [self._gather([w_out_i, w_up_i, w_down_i],
                                     [("ici", w_out_i, out_all), ("nbr", w_down_i, down_all),
                                      ("nbr", w_up_i, up_a)], (9, "chips"))]
            if call == "retention_fwd":
                return [self._gather([w_out_i, w_up_i, w_down_i],
                                     [("d2d", w_out_i, out_all),
                                      ("relay", w_down_i, down_all), ("d2d_nbr", w_down_i, down_all),
                                      ("relay", w_up_i, up_a), ("d2d_nbr", w_up_i, up_a), ("nbr", w_up_i, up_b)],
                                     (10, "both"))]
            if call == "outproj_ln1":
                return [self._gather([w_up_i, w_down_i],
                                     [("d2d_diag", w_down_i, down_all), ("d2d_diag", w_up_i, up_a),
                                      ("relay", w_up_i, up_b), ("d2d_nbr", w_up_i, up_b), ("ici", w_up_i, up_c)],
                                     (11, "both"))]
            return []

        def after(self, call):
            return tuple(self.tokens.pop(call, ()))

        def riders(self, call):
            self.pending = self.plan(call)
            return [r for r, _ in self.pending]

        def _start(self, name, rider, before, handshake):
            state, token = _split_start(name, rider, handshake)
            self.tokens.setdefault(before, []).append(token)
            return state

        def _finish_pair(self, name, state, ws, after):
            _, lands = _split_wait(name, state, after)
            self._finish_sum(ws, lands)

        def landed(self, call, results, outs):
            for (_, (slot, ws)), (inplace, lands) in zip(self.pending, results):
                for w, arr in zip(ws, inplace if len(inplace) else lands):
                    getattr(self, slot)[w] = arr
            if call == "wgrad_out":
                self._finish_pair("pair_exchange_up_wait", self.pair_up, [w_up_i], outs[1])
                self.chips.append(([w_up_i], self._start(
                    "chip_exchange_up_start", self._chip([w_up_i])[0], "wgrad_down", (5, "chips"))))
            if call == "mix_bwd":
                ws = [w_out_i, w_down_i]
                self._finish_pair("pair_exchange_out_down_wait", self.pair_out_down, ws, outs[0])

        def small_gradients(self, loss, small):
            dcw4 = jnp.transpose(small["conv_w"].reshape(3, N_SHARD, DOWN_SH), (1, 0, 2))
            own = [small["w_pool"], _small_pack(loss, [small[n] for n in vec_names]), dcw4]
            ws = [w_out_i, w_down_i]
            parts = [self._chip(ws)[0], _small_all_rider(own)]
            chip, self.small_all = _split_parts(
                self._start("chip_out_down_small_all_start", _merged_rider(parts), "retention_bwd",
                            (7, "all")), parts)
            self.chips.append((ws, chip))

        def gradient(self, name, g32, g16):
            w = order.index(name)
            shape = (N_SHARD,) + SHARD_SHAPES[w]
            self.g32[w], self.g16[w] = g32.reshape(shape), g16.reshape(shape)
            if name == "w_up":
                self.pair_up = self._start("pair_exchange_up_start", self._pair([w])[0], "wgrad_out",
                                           (1, "sibling"))
            if name == "w_down":
                self.pair_out_down = self._start("pair_exchange_out_down_start",
                                                 self._pair([w_out_i, w_down_i])[0], "mix_bwd", (2, "sibling"))

        def wgrad_in(self, xb, dproj):
            w = w_in_i
            g32, landed = _wgrad_send(xb, dproj, IN_SH, "wgrad_in", 4, after=self.after("wgrad_in"))
            self.g32[w] = g32
            self._finish_sum([w], [landed])
            self.chips.append(([w], self._start("chip_exchange_in_start", self._chip([w])[0], "dx", (8, "chips"))))

        def _finish_sum(self, ws, lands):
            p32s, p16s = _pair_sum(pos, ws, [self.g32[w] for w in ws], lands)
            for w, p32, p16 in zip(ws, p32s, p16s):
                self.p32[w], self.p16[w] = p32, p16

        def finish(self, after):
            for n, (ws, state) in enumerate(self.chips):
                _, lands = _split_wait("chip_exchange_wait_%d" % n, state, after)
                for w, arr in zip(ws, lands):
                    self.recv_b[w] = arr
            own, sent = _split_wait("small_all_wait", self.small_all, after)
            return list(sent) + list(own)

    comm = MeshComm()
    loss, grad_x, small = _local_step(x[0], loss_target[0], cw_full, conv_b, gathered[N_BIG + 1], pool_scale,
                                      ln1_g, ln1_b, ln2_g, ln2_b, comm)

    given = dict(w_pool=w_pool, pool_scale=pool_scale, ln1_g=ln1_g, ln1_b=ln1_b, conv_w=conv_w, conv_b=conv_b,
                 ln2_g=ln2_g, ln2_b=ln2_b)
    given_m = dict(w_pool=m_w_pool, pool_scale=m_pool_scale, ln1_g=m_ln1_g, ln1_b=m_ln1_b, conv_w=m_conv_w,
                   conv_b=m_conv_b, ln2_g=m_ln2_g, ln2_b=m_ln2_b)
    given_v = dict(w_pool=v_w_pool, pool_scale=v_pool_scale, ln1_g=v_ln1_g, ln1_b=v_ln1_b, conv_w=v_conv_w,
                   conv_b=v_conv_b, ln2_g=v_ln2_g, ln2_b=v_ln2_b)
    args = []
    for src in (given, given_m, given_v):
        args += [src["w_pool"][0], taps_first(src["conv_w"]), [src[n] for n in vec_names]]
    small_sums = comm.finish(grad_x)
    loss_tot, small_out = _small_adam(*small_sums, *args)
    every = range(N_BIG)
    mine = _chip_sum([comm.p32[w] for w in every], [comm.recv_b[w] for w in every])
    final_state, _ = _split_start("pair_exchange_f32_start", _final_rider(mine), (3, "sibling"))
    mine = final_state[2][:N_BIG]
    big = ([w_in, w_out, w_up, w_down], [m_w_in, m_w_out, m_w_up, m_w_down], [v_w_in, v_w_out, v_w_up, v_w_down])
    own_half = _adam_half("adam_own_half", pos, mine, *big)
    _, theirs = _split_wait("pair_exchange_f32_wait", final_state, own_half[0][0])
    big_out = _adam_half("adam_other_half", pos, theirs, *big, into=own_half)

    names = ("w_in", "w_pool", "pool_scale", "w_out", "ln1_g", "ln1_b", "w_up", "conv_w", "conv_b", "w_down",
             "ln2_g", "ln2_b")
    small_names = ("w_pool", "conv_w") + vec_names
    result = [loss_tot.reshape(()), grad_x[None]]
    for kind in range(4):
        for n in names:
            if n in order:
                result.append(big_out[kind][order.index(n)])
            else:
                val = small_out[kind][small_names.index(n)]
                if n == "conv_w":
                    val = taps_first(val)
                elif n == "w_pool":
                    val = val[None]
                result.append(val)
    return tuple(result)
```

```python
import functools

import numpy as np
import jax
import jax.numpy as jnp
from jax import lax
from jax.experimental import pallas as pl
from jax.experimental.pallas import tpu as pltpu

F32 = jnp.float32
BF16 = jnp.bfloat16

D_MODEL = 1024
HEADS = 4
HEAD_DIM = 128
RET_W = HEADS * HEAD_DIM
POOL_WINDOWS = (2, 4, 8, 16)
POOL_W = 512
IN_W = 4 * RET_W + POOL_W
D_FF = 2816
N_SHARD = 4
IN_SH = IN_W // N_SHARD
UP_SH = 2 * D_FF // N_SHARD
DOWN_SH = D_FF // N_SHARD
OUT_SH = D_MODEL // N_SHARD
ROPE_BASE = 10000.0
LN_EPS = 1e-5
RMS_EPS = 1e-6
ALPHA = 2.0 ** 0.25
K_SCALE = HEAD_DIM ** -0.5
SUPER = 256
CHUNK = 64
POOL_HALO = 16
CONV_HALO = 8
FFN_STRIP = 128
LN_ROWS = 32

ADAM_LR = 0.001
ADAM_B1 = 0.9
ADAM_B2 = 0.999
ADAM_EPS = 1e-08
ADAM_WD = 0.01
ADAM_STEP = 10

MESH = pl.DeviceIdType.MESH
VMEM_LIMIT = 56 * 1024 * 1024


def _dot(a, b):
    return jnp.dot(a, b, preferred_element_type=F32)


def _dot_nt(a, b):
    return lax.dot_general(a, b, (((1,), (1,)), ((), ())), preferred_element_type=F32)


def _dot_tn(a, b):
    return lax.dot_general(a, b, (((0,), (0,)), ((), ())), preferred_element_type=F32)


def _sigmoid(x):
    return 1.0 / (1.0 + jnp.exp(-x))


def _params(sem):
    return pltpu.CompilerParams(dimension_semantics=sem, vmem_limit_bytes=VMEM_LIMIT)


def _whole():
    return pl.BlockSpec(memory_space=pltpu.VMEM)


HBM_SPEC = pl.BlockSpec(memory_space=pl.ANY)


class _Rider:
    def __init__(self, inplace, srcs, lands, n_copies, make, handshake=None):
        self.inplace, self.srcs, self.lands, self.n_copies, self.make = list(inplace), list(srcs), list(lands), n_copies, make
        self.handshake = handshake


def _call(body, *, name, grid, in_specs, out_specs, out_shape, operands, scratch_shapes=(), sem=(),
          aliases=None, riders=(), after=()):
    n_in, n_out, n_scr = len(in_specs), len(out_shape), len(scratch_shapes)
    in_specs, out_specs, out_shape = list(in_specs), list(out_specs), list(out_shape)
    operands, scratch_shapes, aliases = list(operands), list(scratch_shapes), dict(aliases or {})
    in_specs += [_whole()] * len(after)
    operands += list(after)
    shakes = [r.handshake for r in riders if r.handshake is not None]
    assert len(shakes) <= 1
    for r in riders:
        for a in r.inplace:
            aliases[len(in_specs)] = len(out_shape)
            in_specs.append(HBM_SPEC)
            operands.append(a)
            out_specs.append(HBM_SPEC)
            out_shape.append(jax.ShapeDtypeStruct(a.shape, a.dtype))
        for a in r.srcs:
            in_specs.append(HBM_SPEC)
            operands.append(a)
        for shp in r.lands:
            out_specs.append(HBM_SPEC)
            out_shape.append(shp)
        scratch_shapes += [pltpu.SemaphoreType.DMA((r.n_copies,)), pltpu.SemaphoreType.DMA((r.n_copies,))]

    def full(*refs):
        ins = refs[:n_in]
        at = n_in + len(after)
        r_srcs = []
        for r in riders:
            at += len(r.inplace)
            r_srcs.append(refs[at:at + len(r.srcs)])
            at += len(r.srcs)
        outs = refs[at:at + n_out]
        at += n_out
        r_outs = []
        for r in riders:
            r_outs.append((refs[at:at + len(r.inplace)], refs[at + len(r.inplace):at + len(r.inplace) + len(r.lands)]))
            at += len(r.inplace) + len(r.lands)
        scr = refs[at:at + n_scr]
        at += n_scr
        r_sems = [refs[at + 2 * i:at + 2 * i + 2] for i in range(len(riders))]

        def copies():
            return [r.make(r_outs[i][0], r_srcs[i], r_outs[i][1], r_sems[i][0], r_sems[i][1])
                    for i, r in enumerate(riders)]

        def start():
            if shakes:
                _shake_hands(shakes[0][1])
            for starts, _ in copies():
                for cp in starts:
                    cp.start()

        def finish():
            for _, waits in copies():
                for wait in waits:
                    wait()

        if riders and grid:
            first = functools.reduce(jnp.logical_and, [pl.program_id(d) == 0 for d in range(len(grid))])
            last = functools.reduce(jnp.logical_and, [pl.program_id(d) == grid[d] - 1 for d in range(len(grid))])
            pl.when(first)(start)
            body(*ins, *outs, *scr)
            pl.when(last)(finish)
        else:
            if riders:
                start()
            body(*ins, *outs, *scr)
            if riders:
                finish()

    barrier_id = shakes[0][0] if shakes else None
    params = pltpu.CompilerParams(vmem_limit_bytes=VMEM_LIMIT, collective_id=barrier_id,
                                  **(dict(dimension_semantics=sem) if grid else {}))
    res = pl.pallas_call(
        full, name=name, grid=grid, in_specs=in_specs, out_specs=out_specs, out_shape=out_shape,
        scratch_shapes=scratch_shapes, input_output_aliases=aliases, compiler_params=params,
    )(*operands)
    outs, at, rider_res = res[:n_out], n_out, []
    for r in riders:
        rider_res.append((res[at:at + len(r.inplace)], res[at + len(r.inplace):at + len(r.inplace) + len(r.lands)]))
        at += len(r.inplace) + len(r.lands)
    return list(outs), rider_res


def _gammas():
    return [1.0 - 2.0 ** (-5.0 - h) for h in range(HEADS)]


def _decay_tables():
    idx = np.arange(SUPER)
    dist = np.abs(idx[:, None] - idx[None, :]).astype(np.float64)
    visible = (idx[None, :] // CHUNK) <= (idx[:, None] // CHUNK)
    mask = np.stack([np.where(visible, g ** dist, 0.0) for g in _gammas()])
    qd = np.concatenate([np.repeat((g ** (idx + 1.0))[:, None], HEAD_DIM, 1) for g in _gammas()], 1)
    kd = np.concatenate([np.repeat((g ** (SUPER - 1.0 - idx))[:, None], HEAD_DIM, 1) for g in _gammas()], 1)
    return (jnp.asarray(mask, F32), jnp.asarray(qd, F32), jnp.asarray(kd, F32))


def _rope_tables(s):
    inv_freq = ROPE_BASE ** (-np.arange(0, HEAD_DIM, 2, dtype=np.float64) / HEAD_DIM)
    ang = np.arange(s, dtype=np.float64)[:, None] * inv_freq[None, :]
    cos, sin = np.cos(ang), np.sin(ang)
    return (jnp.asarray(np.concatenate([cos, cos], 1), F32),
            jnp.asarray(np.concatenate([-sin, sin], 1), F32))


def _rope(t, cosf, sinf):
    return t * cosf + pltpu.roll(t, HEAD_DIM // 2, 1) * sinf


def _rope_t(t, cosf, sinf):
    return t * cosf - pltpu.roll(t, HEAD_DIM // 2, 1) * sinf


def _layernorm_fwd(z):
    mu = jnp.mean(z, axis=-1, keepdims=True)
    zc = z - mu
    var = jnp.mean(zc * zc, axis=-1, keepdims=True)
    rstd = lax.rsqrt(var + LN_EPS)
    return zc * rstd, rstd


def _layernorm_bwd(dy, xhat, rstd, gain):
    dxh = dy * gain
    m1 = jnp.mean(dxh, axis=-1, keepdims=True)
    m2 = jnp.mean(dxh * xhat, axis=-1, keepdims=True)
    return rstd * (dxh - m1 - xhat * m2)


def _proj_pool(x, win4, cosf, sinf, wpool, pscale, ts, riders=(), after=()):
    s = x.shape[0]
    nt = s // ts

    def body(x_ref, w_ref, cos_ref, sin_ref, wp_ref, ps_ref,
             xb_ref, q_ref, k_ref, v_ref, g_ref, pooled_ref, cat_ref, proj_scr, pext_scr):
        i = pl.program_id(0)
        xb = x_ref[...].astype(BF16)
        xb_ref[...] = xb
        for j in range(N_SHARD):
            proj_scr[:, j * IN_SH:(j + 1) * IN_SH] = _dot(xb, w_ref[j])
        cosf_t = cos_ref[...]
        sinf_t = sin_ref[...]
        for h in range(HEADS):
            lo = h * HEAD_DIM
            q_ref[:, lo:lo + HEAD_DIM] = _rope(proj_scr[:, lo:lo + HEAD_DIM], cosf_t, sinf_t).astype(BF16)
            kk = _rope(proj_scr[:, RET_W + lo:RET_W + lo + HEAD_DIM], cosf_t, sinf_t) * K_SCALE
            k_ref[:, lo:lo + HEAD_DIM] = kk.astype(BF16)
        v_ref[...] = proj_scr[:, 2 * RET_W:3 * RET_W].astype(BF16)
        g_ref[...] = proj_scr[:, 3 * RET_W:4 * RET_W]

        @pl.when(i == 0)
        def _():
            pext_scr[0:POOL_HALO, :] = jnp.zeros((POOL_HALO, POOL_W), F32)

        pext_scr[POOL_HALO:POOL_HALO + ts, :] = proj_scr[:, 4 * RET_W:IN_W]
        pos = (i * ts + lax.broadcasted_iota(jnp.int32, (ts, 1), 0) + 1).astype(F32)
        for gi, w in enumerate(POOL_WINDOWS):
            lo = gi * HEAD_DIM
            ext = pext_scr[:, lo:lo + HEAD_DIM]
            acc = ext
            shift = 1
            while shift < w:
                acc = acc + pltpu.roll(acc, shift, 0)
                shift *= 2
            tok = ext[POOL_HALO:POOL_HALO + ts]
            pooled = acc[POOL_HALO:POOL_HALO + ts] / jnp.minimum(pos, float(w)) - tok
            pooled_b = pooled.astype(BF16)
            pooled_ref[:, lo:lo + HEAD_DIM] = pooled_b
            lin = _dot(pooled_b, wp_ref[gi])
            cat_ref[:, lo:lo + HEAD_DIM] = (lin * ps_ref[:, lo:lo + HEAD_DIM]).astype(BF16)
        pext_scr[0:POOL_HALO, :] = pext_scr[ts:ts + POOL_HALO, :]

    tile = lambda w: pl.BlockSpec((ts, w), lambda i: (i, 0))
    return _call(
        body, name="proj_pool", grid=(nt,),
        in_specs=[tile(D_MODEL), _whole(), tile(HEAD_DIM), tile(HEAD_DIM), _whole(), _whole()],
        out_specs=[tile(D_MODEL), tile(RET_W), tile(RET_W), tile(RET_W), tile(RET_W), tile(POOL_W),
                   pl.BlockSpec((ts, POOL_W), lambda i: (i, 1))],
        out_shape=[jax.ShapeDtypeStruct((s, D_MODEL), BF16), jax.ShapeDtypeStruct((s, RET_W), BF16),
                   jax.ShapeDtypeStruct((s, RET_W), BF16), jax.ShapeDtypeStruct((s, RET_W), BF16),
                   jax.ShapeDtypeStruct((s, RET_W), F32), jax.ShapeDtypeStruct((s, POOL_W), BF16),
                   jax.ShapeDtypeStruct((s, 2 * RET_W), BF16)],
        scratch_shapes=[pltpu.VMEM((ts, IN_W), F32), pltpu.VMEM((ts + POOL_HALO, POOL_W), F32)],
        sem=("arbitrary",), operands=(x, win4, cosf, sinf, wpool, pscale), riders=riders, after=after,
    )


def _retention_fwd(q, k, v, g, cat, mask, qd, kd, riders=(), after=()):
    s = q.shape[0]
    ns = s // SUPER
    cdec = [gm ** float(SUPER) for gm in _gammas()]

    def body(q_ref, k_ref, v_ref, g_ref, cat_in, mask_ref, qd_ref, kd_ref,
             ret_ref, cat_ref, st_ref, state_scr):
        del cat_in
        n = pl.program_id(0)

        @pl.when(n == 0)
        def _():
            state_scr[...] = jnp.zeros_like(state_scr)

        for h in range(HEADS):
            sl = slice(h * HEAD_DIM, (h + 1) * HEAD_DIM)
            qh, kh, vh = q_ref[:, sl], k_ref[:, sl], v_ref[:, sl]
            sc = _dot_nt(qh, kh) * mask_ref[h]
            st = state_scr[h]
            stb = st.astype(BF16)
            st_ref[0, h] = stb
            qdb = (qh.astype(F32) * qd_ref[:, sl]).astype(BF16)
            kdb = (kh.astype(F32) * kd_ref[:, sl]).astype(BF16)
            ret = _dot(sc.astype(BF16), vh) + _dot(qdb, stb)
            state_scr[h] = st * cdec[h] + _dot_tn(kdb, vh)
            ret_ref[:, sl] = ret
            r = lax.rsqrt(jnp.mean(ret * ret, axis=-1, keepdims=True) + RMS_EPS)
            gh = g_ref[:, sl]
            cat_ref[:, sl] = ((ret * r) * (gh * _sigmoid(gh))).astype(BF16)

    tile = pl.BlockSpec((SUPER, RET_W), lambda n: (n, 0))
    return _call(
        body, name="retention_fwd", grid=(ns,),
        in_specs=[tile, tile, tile, tile, HBM_SPEC, _whole(), _whole(), _whole()],
        out_specs=[tile, tile, pl.BlockSpec((1, HEADS, HEAD_DIM, HEAD_DIM), lambda n: (n, 0, 0, 0))],
        out_shape=[jax.ShapeDtypeStruct((s, RET_W), F32), jax.ShapeDtypeStruct((s, 2 * RET_W), BF16),
                   jax.ShapeDtypeStruct((ns, HEADS, HEAD_DIM, HEAD_DIM), BF16)],
        scratch_shapes=[pltpu.VMEM((HEADS, HEAD_DIM, HEAD_DIM), F32)],
        aliases={4: 1}, sem=("arbitrary",), operands=(q, k, v, g, cat, mask, qd, kd), riders=riders,
        after=after,
    )


def _outproj_ln1(x, cat, wout, g1, b1, ts, riders=(), after=()):
    s = x.shape[0]

    def body(x_ref, cat_ref, w_ref, g_ref, b_ref, xhat_ref, rstd_ref, h1b_ref):
        z = ALPHA * x_ref[...] + _dot(cat_ref[...], w_ref[...])
        xhat, rstd = _layernorm_fwd(z)
        xhat_ref[...] = xhat
        rstd_ref[...] = rstd
        h1b_ref[...] = (xhat * g_ref[...] + b_ref[...]).astype(BF16)

    tile = lambda w: pl.BlockSpec((ts, w), lambda i: (i, 0))
    return _call(
        body, name="outproj_ln1", grid=(s // ts,),
        in_specs=[tile(D_MODEL), tile(D_MODEL), _whole(), _whole(), _whole()],
        out_specs=[tile(D_MODEL), tile(1), tile(D_MODEL)],
        out_shape=[jax.ShapeDtypeStruct((s, D_MODEL), F32), jax.ShapeDtypeStruct((s, 1), F32),
                   jax.ShapeDtypeStruct((s, D_MODEL), BF16)],
        sem=("arbitrary",), operands=(x, cat, wout, g1, b1), riders=riders, after=after,
    )


def _ffn_fwd_loss(xhat1, h1b, target, wup4, wdown, cw, cb, g1, b1, g2, b2, ts):
    s = xhat1.shape[0]

    def body(xhat_ref, h1b_ref, tgt_ref, wup_ref, wdn_ref, cw_ref, cb_ref, g1_ref, b1_ref, g2_ref, b2_ref,
             ub_ref, act_ref, sd_ref, dz2_ref, dz2b_ref, loss_ref, dg2_ref, db2_ref, val_scr, gext_scr, ffn_scr):
        i = pl.program_id(0)

        @pl.when(i == 0)
        def _():
            gext_scr[0:CONV_HALO, :] = jnp.zeros((CONV_HALO, D_FF), F32)
            loss_ref[...] = jnp.zeros_like(loss_ref)
            dg2_ref[...] = jnp.zeros_like(dg2_ref)
            db2_ref[...] = jnp.zeros_like(db2_ref)

        for half in range(2):
            lo = half * UP_SH
            gext_scr[CONV_HALO:CONV_HALO + ts, lo:lo + UP_SH] = _dot(h1b_ref[...], wup_ref[2 + half])
            val_scr[:, lo:lo + UP_SH] = _dot(h1b_ref[...], wup_ref[half])
            for c0 in range(lo, lo + UP_SH, FFN_STRIP):
                cols = slice(c0, c0 + FFN_STRIP)
                ext = gext_scr[:, cols]
                gate = ext[CONV_HALO:]
                hc = cb_ref[:, cols] + ((pltpu.roll(ext, 2, 0)[CONV_HALO:] * cw_ref[0:1, cols]
                                         + pltpu.roll(ext, 1, 0)[CONV_HALO:] * cw_ref[1:2, cols])
                                        + gate * cw_ref[2:3, cols])
                val = val_scr[:, cols]
                sg = _sigmoid(hc)
                si = hc * sg
                act_ref[:, cols] = (si * val).astype(BF16)
                ub_ref[:, cols] = val.astype(BF16)
                ub_ref[:, D_FF + c0:D_FF + c0 + FFN_STRIP] = gate.astype(BF16)
                sd_ref[:, cols] = hc.astype(BF16)
            part = _dot(act_ref[:, lo:lo + UP_SH], wdn_ref[lo:lo + UP_SH, :])
            if half == 0:
                ffn_scr[...] = part
            else:
                ffn_scr[...] += part

        gext_scr[0:CONV_HALO, :] = gext_scr[ts:ts + CONV_HALO, :]

        loss_acc = jnp.zeros((1, 1), F32)
        dg2_acc = jnp.zeros((1, D_MODEL), F32)
        db2_acc = jnp.zeros((1, D_MODEL), F32)
        for r0 in range(0, ts, LN_ROWS):
            rows = slice(r0, r0 + LN_ROWS)
            h1 = xhat_ref[rows, :] * g1_ref[...] + b1_ref[...]
            xhat2, rstd2 = _layernorm_fwd(ALPHA * h1 + ffn_scr[rows, :])
            diff = (xhat2 * g2_ref[...] + b2_ref[...]) - tgt_ref[rows, :]
            row = jnp.mean(diff * diff, axis=-1, keepdims=True)
            loss_acc = loss_acc + 0.5 * jnp.sum(row, axis=0, keepdims=True)
            dy = diff * (1.0 / D_MODEL)
            dg2_acc = dg2_acc + jnp.sum(dy * xhat2, axis=0, keepdims=True)
            db2_acc = db2_acc + jnp.sum(dy, axis=0, keepdims=True)
            dz2 = _layernorm_bwd(dy, xhat2, rstd2, g2_ref[...])
            dz2_ref[rows, :] = dz2
            dz2b_ref[rows, :] = dz2.astype(BF16)
        loss_ref[...] += loss_acc
        dg2_ref[...] += dg2_acc
        db2_ref[...] += db2_acc

    tile = lambda w: pl.BlockSpec((ts, w), lambda i: (i, 0))
    acc = lambda w: pl.BlockSpec((1, w), lambda i: (0, 0))
    return pl.pallas_call(
        body, name="ffn_fwd_loss", grid=(s // ts,),
        in_specs=[tile(D_MODEL), tile(D_MODEL), tile(D_MODEL)] + [_whole()] * 8,
        out_specs=[tile(2 * D_FF), tile(D_FF), tile(D_FF), tile(D_MODEL), tile(D_MODEL),
                   acc(1), acc(D_MODEL), acc(D_MODEL)],
        out_shape=[jax.ShapeDtypeStruct((s, 2 * D_FF), BF16), jax.ShapeDtypeStruct((s, D_FF), BF16),
                   jax.ShapeDtypeStruct((s, D_FF), BF16), jax.ShapeDtypeStruct((s, D_MODEL), F32),
                   jax.ShapeDtypeStruct((s, D_MODEL), BF16),
                   jax.ShapeDtypeStruct((1, 1), F32), jax.ShapeDtypeStruct((1, D_MODEL), F32),
                   jax.ShapeDtypeStruct((1, D_MODEL), F32)],
        scratch_shapes=[pltpu.VMEM((ts, D_FF), F32), pltpu.VMEM((ts + CONV_HALO, D_FF), F32),
                        pltpu.VMEM((ts, D_MODEL), F32)],
        compiler_params=_params(("arbitrary",)),
    )(xhat1, h1b, target, wup4, wdown, cw, cb, g1, b1, g2, b2)


def _ffn_bwd(dz2, dz2b, ub, sd, xhat1, rstd1, wup4, wdown, cw, g1, ts):
    s = dz2.shape[0]
    nt = s // ts

    def body(dz2_ref, dz2b_ref, ub_ref, sd_ref, xhat_ref, rstd_ref, wup_ref, wdn_ref, cw_ref, g1_ref,
             dub_ref, dz1_ref, dz1b_ref, dg1_ref, db1_ref, dcw_ref, dcb_ref, dext_scr, da_scr):
        i = pl.program_id(0)

        @pl.when(i == 0)
        def _():
            dext_scr[ts:ts + CONV_HALO, :] = jnp.zeros((CONV_HALO, D_FF), F32)
            dg1_ref[...] = jnp.zeros_like(dg1_ref)
            db1_ref[...] = jnp.zeros_like(db1_ref)
            dcw_ref[...] = jnp.zeros_like(dcw_ref)
            dcb_ref[...] = jnp.zeros_like(dcb_ref)

        da_scr[...] = _dot_nt(dz2b_ref[...], wdn_ref[...])
        n_ext = ts + CONV_HALO
        for c0 in range(0, D_FF, FFN_STRIP):
            cols = slice(c0, c0 + FFN_STRIP)
            gcols = slice(D_FF + c0, D_FF + c0 + FFN_STRIP)
            val = ub_ref[:, cols].astype(F32)
            gate = ub_ref[:, gcols].astype(F32)
            da = da_scr[:, cols]
            hc = sd_ref[:, cols].astype(F32)
            sg = _sigmoid(hc)
            dhc = da * val * (sg * (1.0 + hc * (1.0 - sg)))
            dext_scr[0:ts, cols] = dhc
            dext = dext_scr[:, cols]
            dhc1 = pltpu.roll(dext, n_ext - 1, 0)[0:ts]
            dhc2 = pltpu.roll(dext, n_ext - 2, 0)[0:ts]
            dcb_ref[:, cols] += jnp.sum(dhc, axis=0, keepdims=True)
            dcw_ref[0:1, cols] += jnp.sum(dhc2 * gate, axis=0, keepdims=True)
            dcw_ref[1:2, cols] += jnp.sum(dhc1 * gate, axis=0, keepdims=True)
            dcw_ref[2:3, cols] += jnp.sum(dhc * gate, axis=0, keepdims=True)
            dgate = dhc * cw_ref[2:3, cols] + dhc1 * cw_ref[1:2, cols] + dhc2 * cw_ref[0:1, cols]
            dub_ref[:, cols] = (da * (hc * sg)).astype(BF16)
            dub_ref[:, gcols] = dgate.astype(BF16)
        dext_scr[ts:n_ext, :] = dext_scr[0:CONV_HALO, :]
        dh1 = ALPHA * dz2_ref[...]
        for j in range(N_SHARD):
            dh1 = dh1 + _dot_nt(dub_ref[:, j * UP_SH:(j + 1) * UP_SH], wup_ref[j])
        xhat = xhat_ref[...]
        dg1_ref[...] += jnp.sum(dh1 * xhat, axis=0, keepdims=True)
        db1_ref[...] += jnp.sum(dh1, axis=0, keepdims=True)
        dz1 = _layernorm_bwd(dh1, xhat, rstd_ref[...], g1_ref[...])
        dz1_ref[...] = dz1
        dz1b_ref[...] = dz1.astype(BF16)

    tile = lambda w: pl.BlockSpec((ts, w), lambda i: (nt - 1 - i, 0))
    acc = lambda rws, w: pl.BlockSpec((rws, w), lambda i: (0, 0))
    return pl.pallas_call(
        body, name="ffn_bwd", grid=(nt,),
        in_specs=[tile(D_MODEL), tile(D_MODEL), tile(2 * D_FF), tile(D_FF), tile(D_MODEL), tile(1)]
        + [_whole()] * 4,
        out_specs=[tile(2 * D_FF), tile(D_MODEL), tile(D_MODEL), acc(1, D_MODEL), acc(1, D_MODEL),
                   acc(3, D_FF), acc(1, D_FF)],
        out_shape=[jax.ShapeDtypeStruct((s, 2 * D_FF), BF16),
                   jax.ShapeDtypeStruct((s, D_MODEL), F32), jax.ShapeDtypeStruct((s, D_MODEL), BF16),
                   jax.ShapeDtypeStruct((1, D_MODEL), F32),
                   jax.ShapeDtypeStruct((1, D_MODEL), F32), jax.ShapeDtypeStruct((3, D_FF), F32),
                   jax.ShapeDtypeStruct((1, D_FF), F32)],
        scratch_shapes=[pltpu.VMEM((ts + CONV_HALO, D_FF), F32), pltpu.VMEM((ts, D_FF), F32)],
        compiler_params=_params(("arbitrary",)),
    )(dz2, dz2b, ub, sd, xhat1, rstd1, wup4, wdown, cw, g1)


def _mix_bwd(dz1, pooled, ret, g, wout, wpool, pscale, loss, vec_grads, ts, riders=(), after=()):
    s = dz1.shape[0]
    nt = s // ts

    def body(dz1_ref, pooled_ref, ret_ref, g_ref, wout_ref, wp_ref, ps_ref, loss_ref, dcb_ref, dg1_ref, db1_ref,
             dg2_ref, db2_ref, dret_ref, dgp_ref, dwp_ref, dps_ref, packed_ref, eext_scr):
        i = pl.program_id(0)
        r = nt - 1 - i

        @pl.when(i == 0)
        def _():
            eext_scr[ts:ts + POOL_HALO, :] = jnp.zeros((POOL_HALO, POOL_W), F32)
            dwp_ref[...] = jnp.zeros_like(dwp_ref)
            dps_ref[...] = jnp.zeros_like(dps_ref)

        dzb = dz1_ref[...].astype(BF16)
        dcat_r = _dot_nt(dzb, wout_ref[0:RET_W, :])
        dcat_p = _dot_nt(dzb, wout_ref[RET_W:2 * RET_W, :])
        pos = (r * ts + lax.broadcasted_iota(jnp.int32, (ts, 1), 0) + 1).astype(F32)
        dpooled = []
        for gi, w in enumerate(POOL_WINDOWS):
            sl = slice(gi * HEAD_DIM, (gi + 1) * HEAD_DIM)
            pb = pooled_ref[:, sl]
            dy = dcat_p[:, sl]
            dps_ref[:, sl] += jnp.sum(dy * _dot(pb, wp_ref[gi]), axis=0, keepdims=True)
            dlin = (dy * ps_ref[:, sl]).astype(BF16)
            dwp_ref[gi] += _dot_tn(pb, dlin)
            dpg = _dot_nt(dlin, wp_ref[gi])
            dpooled.append(dpg)
            eext_scr[0:ts, sl] = dpg / jnp.minimum(pos, float(w))
        for gi, w in enumerate(POOL_WINDOWS):
            sl = slice(gi * HEAD_DIM, (gi + 1) * HEAD_DIM)
            acc = eext_scr[:, sl]
            shift = 1
            while shift < w:
                acc = acc + pltpu.roll(acc, ts + POOL_HALO - shift, 0)
                shift *= 2
            dgp_ref[:, RET_W + gi * HEAD_DIM:RET_W + (gi + 1) * HEAD_DIM] = (acc[0:ts] - dpooled[gi]).astype(BF16)
        eext_scr[ts:ts + POOL_HALO, :] = eext_scr[0:POOL_HALO, :]
        for h in range(HEADS):
            sl = slice(h * HEAD_DIM, (h + 1) * HEAD_DIM)
            rt = ret_ref[:, sl]
            rr = lax.rsqrt(jnp.mean(rt * rt, axis=-1, keepdims=True) + RMS_EPS)
            rn = rt * rr
            gh = g_ref[:, sl]
            sg = _sigmoid(gh)
            dy = dcat_r[:, sl]
            dgp_ref[:, sl] = (dy * rn * (sg * (1.0 + gh * (1.0 - sg)))).astype(BF16)
            drn = dy * (gh * sg)
            dret_ref[:, sl] = (rr * (drn - rn * jnp.mean(drn * rn, axis=-1, keepdims=True))).astype(BF16)

        @pl.when(i == nt - 1)
        def _():
            packed_ref[...] = jnp.zeros_like(packed_ref)
            rows = (dcb_ref, dps_ref, dg1_ref, db1_ref, dg2_ref, db2_ref)
            for (row, n), ref in zip(SMALL_VECS, rows):
                packed_ref[row:row + 1, 0:n] = ref[...]
            packed_ref[ROW_LOSS:ROW_LOSS + 1, 0:HEAD_DIM] = jnp.broadcast_to(loss_ref[...], (1, HEAD_DIM))

    tile = lambda w: pl.BlockSpec((ts, w), lambda i: (nt - 1 - i, 0))
    return _call(
        body, name="mix_bwd", grid=(nt,),
        in_specs=[tile(D_MODEL), tile(POOL_W), tile(RET_W), tile(RET_W)] + [_whole()] * 9,
        out_specs=[tile(RET_W), tile(2 * RET_W),
                   pl.BlockSpec((len(POOL_WINDOWS), HEAD_DIM, HEAD_DIM), lambda i: (0, 0, 0)),
                   pl.BlockSpec((1, POOL_W), lambda i: (0, 0)),
                   pl.BlockSpec((SMALL_ROWS, D_FF), lambda i: (0, 0))],
        out_shape=[jax.ShapeDtypeStruct((s, RET_W), BF16), jax.ShapeDtypeStruct((s, 2 * RET_W), BF16),
                   jax.ShapeDtypeStruct((len(POOL_WINDOWS), HEAD_DIM, HEAD_DIM), F32),
                   jax.ShapeDtypeStruct((1, POOL_W), F32), jax.ShapeDtypeStruct((SMALL_ROWS, D_FF), F32)],
        scratch_shapes=[pltpu.VMEM((ts + POOL_HALO, POOL_W), F32)],
        sem=("arbitrary",), operands=(dz1, pooled, ret, g, wout, wpool, pscale, loss, *vec_grads), riders=riders,
        after=after,
    )


def _retention_bwd(q, k, v, dret, dgp, states, mask, qd, kd, cosf, sinf, riders=(), after=()):
    s = q.shape[0]
    ns = s // SUPER
    cdec = [gm ** float(SUPER) for gm in _gammas()]

    def body(q_ref, k_ref, v_ref, do_ref, dgp_ref, st_ref, mask_ref, qd_ref, kd_ref, cos_ref, sin_ref,
             dproj_ref, dstate_scr):
        i = pl.program_id(0)

        @pl.when(i == 0)
        def _():
            dstate_scr[...] = jnp.zeros_like(dstate_scr)

        cosf_t = cos_ref[...]
        sinf_t = sin_ref[...]
        for h in range(HEADS):
            sl = slice(h * HEAD_DIM, (h + 1) * HEAD_DIM)
            qh, kh, vh, doh = q_ref[:, sl], k_ref[:, sl], v_ref[:, sl], do_ref[:, sl]
            dscb = (_dot_nt(doh, vh) * mask_ref[0, h]).astype(BF16)
            dsctb = (_dot_nt(vh, doh) * mask_ref[1, h]).astype(BF16)
            sctb = (_dot_nt(kh, qh) * mask_ref[1, h]).astype(BF16)
            stb = st_ref[0, h]
            dst = dstate_scr[h]
            dstb = dst.astype(BF16)
            qdb = (qh.astype(F32) * qd_ref[:, sl]).astype(BF16)
            kdb = (kh.astype(F32) * kd_ref[:, sl]).astype(BF16)
            dq = _dot(dscb, kh) + _dot_nt(doh, stb) * qd_ref[:, sl]
            dk = _dot(dsctb, qh) + _dot_nt(vh, dstb) * kd_ref[:, sl]
            dv = _dot(sctb, doh) + _dot(kdb, dstb)
            dstate_scr[h] = dst * cdec[h] + _dot_tn(qdb, doh)
            lo = h * HEAD_DIM
            dproj_ref[:, lo:lo + HEAD_DIM] = _rope_t(dq, cosf_t, sinf_t).astype(BF16)
            dproj_ref[:, RET_W + lo:RET_W + lo + HEAD_DIM] = _rope_t(dk * K_SCALE, cosf_t, sinf_t).astype(BF16)
            dproj_ref[:, 2 * RET_W + lo:2 * RET_W + lo + HEAD_DIM] = dv.astype(BF16)
        dproj_ref[:, 3 * RET_W:IN_W] = dgp_ref[...]

    tile = lambda w: pl.BlockSpec((SUPER, w), lambda i: (ns - 1 - i, 0))
    return _call(
        body, name="retention_bwd", grid=(ns,),
        in_specs=[tile(RET_W), tile(RET_W), tile(RET_W), tile(RET_W), tile(2 * RET_W),
                  pl.BlockSpec((1, HEADS, HEAD_DIM, HEAD_DIM), lambda i: (ns - 1 - i, 0, 0, 0)),
                  _whole(), _whole(), _whole(), tile(HEAD_DIM), tile(HEAD_DIM)],
        out_specs=[tile(IN_W)],
        out_shape=[jax.ShapeDtypeStruct((s, IN_W), BF16)],
        scratch_shapes=[pltpu.VMEM((HEADS, HEAD_DIM, HEAD_DIM), F32)],
        sem=("arbitrary",), operands=(q, k, v, dret, dgp, states, mask, qd, kd, cosf, sinf), riders=riders,
        after=after,
    )


def _dx(dz1, dproj, win4, ts, riders=(), after=()):
    s = dz1.shape[0]

    def body(dz1_ref, dp_ref, w_ref, dx_ref):
        acc = ALPHA * dz1_ref[...]
        for j in range(N_SHARD):
            acc = acc + _dot_nt(dp_ref[:, j * IN_SH:(j + 1) * IN_SH], w_ref[j])
        dx_ref[...] = acc

    tile = lambda w: pl.BlockSpec((ts, w), lambda i: (i, 0))
    return _call(
        body, name="dx", grid=(s // ts,),
        in_specs=[tile(D_MODEL), tile(IN_W), _whole()],
        out_specs=[tile(D_MODEL)],
        out_shape=[jax.ShapeDtypeStruct((s, D_MODEL), F32)],
        sem=("arbitrary",), operands=(dz1, dproj, win4), riders=riders, after=after,
    )


def _wgrad(a, b, tm, tn, name, stacked, m_outer, riders=(), after=()):
    s, m = a.shape
    n = b.shape[1]

    def body(a_ref, b_ref, o32_ref, o16_ref):
        res = _dot_tn(a_ref[...], b_ref[...])
        o32_ref[...] = res.reshape(o32_ref.shape)
        o16_ref[...] = res.astype(BF16).reshape(o16_ref.shape)

    if m_outer:
        grid, blocks = (m // tm, n // tn), (lambda g0, g1: (g0, g1))
    else:
        grid, blocks = (n // tn, m // tm), (lambda g0, g1: (g1, g0))
    if stacked:
        shape = (n // tn, m, tn)
        ospec = pl.BlockSpec((1, tm, tn), lambda g0, g1: (blocks(g0, g1)[1], blocks(g0, g1)[0], 0))
    else:
        shape = (m, n)
        ospec = pl.BlockSpec((tm, tn), lambda g0, g1: blocks(g0, g1))
    return _call(
        body, name=name, grid=grid,
        in_specs=[pl.BlockSpec((s, tm), lambda g0, g1: (0, blocks(g0, g1)[0])),
                  pl.BlockSpec((s, tn), lambda g0, g1: (0, blocks(g0, g1)[1]))],
        out_specs=[ospec, ospec],
        out_shape=[jax.ShapeDtypeStruct(shape, F32), jax.ShapeDtypeStruct(shape, BF16)],
        sem=("arbitrary", "arbitrary"), operands=(a, b), riders=riders, after=after,
    )


def _wgrad_send(a, b, tn, name, barrier_id, after=()):
    s, m = a.shape
    n = b.shape[1]
    nb, hm = n // tn, m // 2

    def body(*refs):
        a_ref, b_ref = refs[:2]
        o32_ref, land_ref, send_scr, send_sems, recv_sems = refs[2 + len(after):]
        j = pl.program_id(0)
        x, y, c = _mesh_pos()

        @pl.when(j == 0)
        def _():
            _shake_hands("sibling")

        o32_ref[0] = _dot_tn(a_ref[...], b_ref[...])
        theirs = pl.ds(pl.multiple_of((1 - c) * hm, 16), hm)
        copies = [pltpu.make_async_remote_copy(
            src_ref=send_scr.at[blk], dst_ref=land_ref.at[blk], send_sem=send_sems.at[blk],
            recv_sem=recv_sems.at[blk], device_id=(x, y, 1 - c), device_id_type=MESH) for blk in range(nb)]
        for blk in range(nb):
            @pl.when(j == blk)
            def _(blk=blk):
                send_scr[blk] = o32_ref[0, theirs, :].astype(BF16)
                copies[blk].start()

        @pl.when(j == nb - 1)
        def _():
            for cp in copies:
                cp.wait()

    return pl.pallas_call(
        body, name=name, grid=(nb,),
        in_specs=[pl.BlockSpec((s, m), lambda j: (0, 0)), pl.BlockSpec((s, tn), lambda j: (0, j))]
        + [_whole()] * len(after),
        out_specs=[pl.BlockSpec((1, m, tn), lambda j: (j, 0, 0)), HBM_SPEC],
        out_shape=[jax.ShapeDtypeStruct((nb, m, tn), F32), jax.ShapeDtypeStruct((nb, hm, tn), BF16)],
        scratch_shapes=[pltpu.VMEM((nb, hm, tn), BF16), pltpu.SemaphoreType.DMA((nb,)),
                        pltpu.SemaphoreType.DMA((nb,))],
        compiler_params=pltpu.CompilerParams(dimension_semantics=("arbitrary",), vmem_limit_bytes=VMEM_LIMIT,
                                             collective_id=barrier_id),
    )(a, b, *after)


class _NoComm:
    def __init__(self, win4, wout, wup4, wdown):
        self.weights = dict(w_in=win4, w_out=wout, w_up=wup4, w_down=wdown)
        self.grads = {}

    def weight(self, name):
        return self.weights[name]

    def riders(self, call):
        return ()

    def after(self, call):
        return ()

    def landed(self, call, results, outs):
        pass

    def small_gradients(self, small, packed):
        pass

    def gradient(self, name, g32, g16):
        self.grads[name] = (g32, g16)

    def wgrad_in(self, xb, dproj):
        (g32, g16), _ = _wgrad(xb, dproj, D_MODEL, IN_SH, "wgrad_in", True, True)
        self.gradient("w_in", g32, g16)


def _local_step(x, target, cw, cb, wpool_b, pscale, g1, b1, g2, b2, comm):
    s = x.shape[0]
    ts_a = min(512, s)
    ts_f = min(256, s)
    mask, qd, kd = _decay_tables()
    cosf, sinf = _rope_tables(s)

    def run(call, fn, *args):
        outs, res = fn(*args, riders=comm.riders(call), after=comm.after(call))
        comm.landed(call, res, outs)
        return outs

    xb, q, k, v, g, pooled, cat = run("proj_pool", _proj_pool, x, comm.weight("w_in"), cosf, sinf, wpool_b,
                                      pscale, ts_a)
    ret, cat, states = run("retention_fwd", _retention_fwd, q, k, v, g, cat, mask, qd, kd)
    wout = comm.weight("w_out")
    xhat1, rstd1, h1b = run("outproj_ln1", _outproj_ln1, x, cat, wout, g1, b1, ts_a)
    wup4, wdown = comm.weight("w_up"), comm.weight("w_down")
    ub, act, sd, dz2, dz2b, loss, dg2, db2 = _ffn_fwd_loss(xhat1, h1b, target, wup4, wdown, cw, cb, g1, b1, g2, b2,
                                                           ts_f)

    dub, dz1, dz1b, dg1, db1, dcw, dcb = _ffn_bwd(dz2, dz2b, ub, sd, xhat1, rstd1, wup4, wdown, cw, g1, ts_f)
    half = D_MODEL // 2
    comm.gradient("w_up", *run("wgrad_up", _wgrad, h1b, dub, half, UP_SH, "wgrad_up", True, False))
    comm.gradient("w_out", *run("wgrad_out", _wgrad, cat, dz1b, D_MODEL, half, "wgrad_out", False, True))
    comm.gradient("w_down", *run("wgrad_down", _wgrad, act, dz2b, D_FF // 2, half, "wgrad_down", False, True))
    dret, dgp, dwp, dps, packed = run("mix_bwd", _mix_bwd, dz1b, pooled, ret, g, wout, wpool_b, pscale, loss,
                                      [dcb, dg1, db1, dg2, db2], ts_a)
    small = dict(w_pool=dwp, pool_scale=dps, ln1_g=dg1, ln1_b=db1, conv_w=dcw, conv_b=dcb,
                 ln2_g=dg2, ln2_b=db2)
    comm.small_gradients(small, packed)
    mask_both = jnp.stack([mask, jnp.swapaxes(mask, 1, 2)])
    dproj, = run("retention_bwd", _retention_bwd, q, k, v, dret, dgp, states, mask_both, qd, kd, cosf, sinf)
    comm.wgrad_in(xb, dproj)
    (grad_x,), _ = _dx(dz1, dproj, comm.weight("w_in"), ts_a, after=comm.after("dx"))
    return loss, grad_x, small


CAST_ROWS = 64
SHARD_SHAPES = ((D_MODEL, IN_SH), (OUT_SH, D_MODEL), (D_MODEL, UP_SH), (DOWN_SH, D_MODEL))
N_BIG = len(SHARD_SHAPES)
CW_SHARD = (3, 1, DOWN_SH)


def _mesh_pos():
    return lax.axis_index("x"), lax.axis_index("y"), lax.axis_index("c")


def _other_chips(x, y):
    return [(1 - x, y), (x, 1 - y), (1 - x, 1 - y)]


def _shake_hands(peers):
    x, y, c = _mesh_pos()
    others = [(x, y, 1 - c)] if peers in ("sibling", "both", "all") else []
    if peers in ("chips", "both", "all"):
        others += [(chip[0], chip[1], c) for chip in _other_chips(x, y)]
    if peers == "all":
        others += [(chip[0], chip[1], 1 - c) for chip in _other_chips(x, y)]
    barrier = pltpu.get_barrier_semaphore()
    for peer in others:
        pl.semaphore_signal(barrier, inc=1, device_id=peer, device_id_type=MESH)
    pl.semaphore_wait(barrier, len(others))


def _half_rows(w, which):
    hr = SHARD_SHAPES[w][0] // 2
    return pl.ds(pl.multiple_of(which * hr, 16), hr)


def _gather_weights(shards, cw_shard, wpool, full):
    def body(*refs):
        in_refs = refs[:N_BIG]
        cw_ref, wpool_ref = refs[N_BIG:N_BIG + 2]
        out_refs = refs[N_BIG + 2:2 * N_BIG + 2]
        cwo_ref, wpool_b_ref = refs[2 * N_BIG + 2:2 * N_BIG + 4]
        stage = refs[2 * N_BIG + 4:3 * N_BIG + 4]
        raw = refs[3 * N_BIG + 4:4 * N_BIG + 4 - len(full)]
        send_sems, recv_sems, fsend_sems, frecv_sems, cw_send, cw_recv, local_sems, load_sems = \
            refs[4 * N_BIG + 4 - len(full):]
        x, y, c = _mesh_pos()
        j0 = 2 * x + y
        chips = _other_chips(x, y)

        fetched = [w for w in range(N_BIG) if w not in full]
        f32 = {w: in_refs[w] for w in full}
        loads = []
        for n, w in enumerate(fetched):
            f32[w] = raw[n]
            loads.append(pltpu.make_async_copy(in_refs[w], raw[n], load_sems.at[n]))
            loads[-1].start()

        def cast_to_stage(w):
            def cast(i, carry):
                rows = pl.ds(pl.multiple_of(i * CAST_ROWS, CAST_ROWS), CAST_ROWS)
                stage[w][rows, :] = f32[w][rows, :].astype(BF16)
                return carry
            lax.fori_loop(0, SHARD_SHAPES[w][0] // CAST_ROWS, cast, 0)

        for w in full:
            cast_to_stage(w)

        jx, jy, jd = 2 * (1 - x) + y, 2 * x + (1 - y), 2 * (1 - x) + (1 - y)
        neighbours = [((1 - x, y, c), jx), ((x, 1 - y, c), jy)]
        passed = jnp.where(c == 0, jx, jy)
        pass_to = (jnp.where(c == 0, x, 1 - x), jnp.where(c == 0, 1 - y, y), c)

        def nbr(w, k, block):
            return pltpu.make_async_remote_copy(
                src_ref=stage[w].at[_half_rows(w, c), :], dst_ref=out_refs[w].at[block, _half_rows(w, c), :],
                send_sem=send_sems.at[w, k], recv_sem=recv_sems.at[w, k],
                device_id=neighbours[k][0], device_id_type=MESH)

        def relay(w, block):
            return pltpu.make_async_remote_copy(
                src_ref=out_refs[w].at[passed, _half_rows(w, c), :],
                dst_ref=out_refs[w].at[block, _half_rows(w, c), :],
                send_sem=send_sems.at[w, 2], recv_sem=recv_sems.at[w, 2],
                device_id=pass_to, device_id_type=MESH)

        def d2d(w, k, block, half):
            return pltpu.make_async_remote_copy(
                src_ref=out_refs[w].at[block, _half_rows(w, half), :],
                dst_ref=out_refs[w].at[block, _half_rows(w, half), :],
                send_sem=fsend_sems.at[w, k], recv_sem=frecv_sems.at[w, k],
                device_id=(x, y, 1 - c), device_id_type=MESH)

        def conv(k, block):
            chip = chips[k]
            return pltpu.make_async_remote_copy(
                src_ref=cw_ref, dst_ref=cwo_ref.at[block], send_sem=cw_send.at[k], recv_sem=cw_recv.at[k],
                device_id=(chip[0], chip[1], c), device_id_type=MESH)

        sent = [nbr(w, k, j0) for w in full for k in range(2)] + [conv(k, j0) for k in range(3)]
        for cp in sent:
            cp.start()
        for n, w in enumerate(fetched):
            loads[n].wait()
            cast_to_stage(w)
        local = [pltpu.make_async_copy(stage[w], out_refs[w].at[j0], local_sems.at[w]) for w in range(N_BIG)]
        local.append(pltpu.make_async_copy(cw_ref, cwo_ref.at[j0], local_sems.at[N_BIG]))
        for cp in local:
            cp.start()
        wpool_b_ref[...] = wpool_ref[...].astype(BF16)
        for w in full:
            for k, (_, block) in enumerate(neighbours):
                nbr(w, k, block).wait_recv()
            later = [relay(w, passed)] + [d2d(w, k, block, c) for k, (_, block) in enumerate(neighbours)]
            for cp in later:
                cp.start()
            sent += later
        for w in full:
            relay(w, jd).wait_recv()
            fw = d2d(w, 2, jd, c)
            fw.start()
            sent.append(fw)
        for w in full:
            for k, block in enumerate([jx, jy, jd]):
                d2d(w, k, block, 1 - c).wait_recv()
        for k, chip in enumerate(chips):
            conv(k, 2 * chip[0] + chip[1]).wait_recv()
        for cp in sent:
            cp.wait_send()
        for cp in local:
            cp.wait()

    out_shape = [jax.ShapeDtypeStruct((N_SHARD,) + shp, BF16) for shp in SHARD_SHAPES]
    out_shape.append(jax.ShapeDtypeStruct((N_SHARD,) + CW_SHARD, F32))
    out_shape.append(jax.ShapeDtypeStruct(wpool.shape, BF16))
    return pl.pallas_call(
        body, name="gather_weights",
        in_specs=[_whole() if w in full else HBM_SPEC for w in range(N_BIG)] + [_whole()] * 2,
        out_specs=[HBM_SPEC] * (N_BIG + 1) + [_whole()],
        out_shape=out_shape,
        scratch_shapes=[pltpu.VMEM(shp, BF16) for shp in SHARD_SHAPES]
        + [pltpu.VMEM(shp, F32) for w, shp in enumerate(SHARD_SHAPES) if w not in full] + [
            pltpu.SemaphoreType.DMA((N_BIG, 3)), pltpu.SemaphoreType.DMA((N_BIG, 3)),
            pltpu.SemaphoreType.DMA((N_BIG, 3)), pltpu.SemaphoreType.DMA((N_BIG, 3)),
            pltpu.SemaphoreType.DMA((3,)), pltpu.SemaphoreType.DMA((3,)),
            pltpu.SemaphoreType.DMA((N_BIG + 1,)), pltpu.SemaphoreType.DMA((N_BIG - len(full),))],
        compiler_params=pltpu.CompilerParams(vmem_limit_bytes=VMEM_LIMIT),
    )(*shards, cw_shard, wpool)


def _gather_rider(arrays, ops, handshake=None):
    ws = sorted(arrays)

    def make(inplace, srcs, lands, send_sems, recv_sems):
        del srcs, lands
        x, y, c = _mesh_pos()
        j0, jx, jy, jd = 2 * x + y, 2 * (1 - x) + y, 2 * x + (1 - y), 2 * (1 - x) + (1 - y)
        x_nbr, y_nbr, sibling = (1 - x, y, c), (x, 1 - y, c), (x, y, 1 - c)
        starts, waits = [], []
        for n, (kind, w, (r0, nr)) in enumerate(ops):
            ref = inplace[ws.index(w)]
            hr = SHARD_SHAPES[w][0] // 2
            rows = lambda core: pl.ds(pl.multiple_of(core * hr + r0, 16), nr)
            mine, theirs = rows(c), rows(1 - c)
            if kind == "ici":
                moves = [(ref.at[j0, mine, :], x_nbr, ref.at[jx, mine, :]),
                         (ref.at[j0, mine, :], y_nbr, ref.at[jy, mine, :]),
                         (ref.at[j0, mine, :], (1 - x, 1 - y, c), ref.at[jd, mine, :])]
            elif kind == "nbr":
                moves = [(ref.at[j0, mine, :], x_nbr, ref.at[jx, mine, :]),
                         (ref.at[j0, mine, :], y_nbr, ref.at[jy, mine, :])]
            elif kind == "relay":
                passed = jnp.where(c == 0, jx, jy)
                to = (jnp.where(c == 0, x, 1 - x), jnp.where(c == 0, 1 - y, y), c)
                moves = [(ref.at[passed, mine, :], to, ref.at[jd, mine, :])]
            else:
                blocks = dict(d2d=[jx, jy, jd], d2d_nbr=[jx, jy], d2d_diag=[jd])[kind]
                moves = [(ref.at[b, mine, :], sibling, ref.at[b, theirs, :]) for b in blocks]
            for k, (src, to, landing) in enumerate(moves):
                sems = dict(send_sem=send_sems.at[3 * n + k], recv_sem=recv_sems.at[3 * n + k],
                            device_id=to, device_id_type=MESH)
                send = pltpu.make_async_remote_copy(src_ref=src, dst_ref=src, **sems)
                arrival = pltpu.make_async_remote_copy(src_ref=src, dst_ref=landing, **sems)
                starts.append(send)
                waits += [arrival.wait_recv, send.wait_send]
        return starts, waits

    return _Rider([arrays[w] for w in ws], [], [], 3 * len(ops), make, handshake)


def _whole_half(w):
    return (0, SHARD_SHAPES[w][0] // 2)


def _pair_rider(ws, g16s):
    def make(inplace, srcs, lands, send_sems, recv_sems):
        del inplace
        x, y, c = _mesh_pos()
        copies = [pltpu.make_async_remote_copy(
            src_ref=srcs[i].at[:, _half_rows(w, 1 - c), :], dst_ref=lands[i],
            send_sem=send_sems.at[i], recv_sem=recv_sems.at[i], device_id=(x, y, 1 - c), device_id_type=MESH)
            for i, w in enumerate(ws)]
        return copies, [cp.wait for cp in copies]

    lands = [jax.ShapeDtypeStruct((N_SHARD, SHARD_SHAPES[w][0] // 2, SHARD_SHAPES[w][1]), BF16) for w in ws]
    return _Rider([], g16s, lands, len(ws), make)


def _chip_rider(ws, p16s):
    def make(inplace, srcs, lands, send_sems, recv_sems):
        del inplace
        x, y, c = _mesh_pos()
        copies = []
        for i in range(len(ws)):
            for k, chip in enumerate(_other_chips(x, y)):
                copies.append(pltpu.make_async_remote_copy(
                    src_ref=srcs[i].at[2 * chip[0] + chip[1]], dst_ref=lands[i].at[k],
                    send_sem=send_sems.at[3 * i + k], recv_sem=recv_sems.at[3 * i + k],
                    device_id=(chip[0], chip[1], c), device_id_type=MESH))
        return copies, [cp.wait for cp in copies]

    lands = [jax.ShapeDtypeStruct((3, SHARD_SHAPES[w][0] // 2, SHARD_SHAPES[w][1]), BF16) for w in ws]
    return _Rider([], p16s, lands, 3 * len(ws), make)


def _final_rider(halves):
    def make(inplace, srcs, lands, send_sems, recv_sems):
        del inplace
        x, y, c = _mesh_pos()
        copies = [pltpu.make_async_remote_copy(
            src_ref=srcs[i], dst_ref=lands[i], send_sem=send_sems.at[i], recv_sem=recv_sems.at[i],
            device_id=(x, y, 1 - c), device_id_type=MESH) for i in range(len(halves))]
        return copies, [cp.wait for cp in copies]

    return _Rider([], halves, [jax.ShapeDtypeStruct(h.shape, h.dtype) for h in halves], len(halves), make)


N_DEV = 2 * N_SHARD


def _device_number(x, y, c):
    return 2 * (2 * x + y) + c


def _small_all_rider(own):
    n = len(own)

    def make(inplace, srcs, lands, send_sems, recv_sems):
        del inplace
        x, y, c = _mesh_pos()
        peers = [(x, y, 1 - c)] + [(chip[0], chip[1], core) for chip in _other_chips(x, y) for core in (c, 1 - c)]
        copies = []
        for i in range(n):
            for k, peer in enumerate(peers):
                copies.append(pltpu.make_async_remote_copy(
                    src_ref=srcs[i], dst_ref=lands[i].at[_device_number(x, y, c)],
                    send_sem=send_sems.at[(N_DEV - 1) * i + k], recv_sem=recv_sems.at[(N_DEV - 1) * i + k],
                    device_id=peer, device_id_type=MESH))
        return copies, [cp.wait for cp in copies]

    lands = [jax.ShapeDtypeStruct((N_DEV,) + a.shape, a.dtype) for a in own]
    return _Rider([], own, lands, (N_DEV - 1) * n, make)


def _comm_only(name, riders):
    _, res = _call(lambda: None, name=name, grid=(), in_specs=[], out_specs=[], out_shape=[], operands=(),
                   riders=riders)
    return res


class _SemList:
    def __init__(self, refs):
        self.at = list(refs)


def _merged_rider(riders):
    srcs = [a for r in riders for a in r.srcs]
    lands = [a for r in riders for a in r.lands]

    def make(inplace, src_refs, land_refs, send_sems, recv_sems):
        starts, waits = [], []
        s0 = l0 = c0 = 0
        for r in riders:
            part = r.make(inplace, src_refs[s0:s0 + len(r.srcs)], land_refs[l0:l0 + len(r.lands)],
                          _SemList(send_sems.at[c0:c0 + r.n_copies]), _SemList(recv_sems.at[c0:c0 + r.n_copies]))
            starts += part[0]
            waits += part[1]
            s0, l0, c0 = s0 + len(r.srcs), l0 + len(r.lands), c0 + r.n_copies
        return starts, waits

    return _Rider([], srcs, lands, sum(r.n_copies for r in riders), make)


def _split_start(name, rider, handshake=None):
    assert not rider.inplace
    ns, nl, n = len(rider.srcs), len(rider.lands), rider.n_copies
    barrier_id, peers = handshake if handshake is not None else (None, None)

    def body(*refs):
        if handshake is not None:
            _shake_hands(peers)
        srcs, lands = refs[:ns], refs[ns:ns + nl]
        sems = refs[ns + nl:ns + nl + 2 * n]
        token = refs[-1]
        starts, _ = rider.make([], srcs, lands, _SemList(sems[:n]), _SemList(sems[n:]))
        for cp in starts:
            cp.start()
        token[...] = jnp.zeros_like(token)

    buffers = [pltpu.with_memory_space_constraint(a, pltpu.HBM) for a in rider.srcs]
    buffers += [pltpu.with_memory_space_constraint(lax.empty(s.shape, s.dtype), pltpu.HBM) for s in rider.lands]
    hbm = pl.BlockSpec(memory_space=pltpu.HBM)
    sem = pl.BlockSpec(memory_space=pltpu.SEMAPHORE)
    outs = pl.pallas_call(
        body, name=name,
        out_shape=tuple([pltpu.SemaphoreType.DMA(())] * (2 * n) + [pltpu.HBM(b.shape, b.dtype) for b in buffers]
                        + [jax.ShapeDtypeStruct((8, 128), F32)]),
        in_specs=[hbm] * (ns + nl),
        out_specs=tuple([sem] * (2 * n) + [hbm] * (ns + nl) + [_whole()]),
        input_output_aliases={i: 2 * n + i for i in range(ns + nl)},
        compiler_params=pltpu.CompilerParams(has_side_effects=pltpu.SideEffectType.DATAFLOW_SIDE_EFFECTING,
                                             collective_id=barrier_id),
    )(*buffers)
    return (rider, outs[:2 * n], outs[2 * n:2 * n + ns + nl]), outs[-1]


def _split_parts(state, riders):
    merged, sems, buffers = state
    n, ns = merged.n_copies, len(merged.srcs)
    parts, s0, l0, c0 = [], 0, 0, 0
    for r in riders:
        parts.append((r, list(sems[c0:c0 + r.n_copies]) + list(sems[n + c0:n + c0 + r.n_copies]),
                      list(buffers[s0:s0 + len(r.srcs)]) + list(buffers[ns + l0:ns + l0 + len(r.lands)])))
        s0, l0, c0 = s0 + len(r.srcs), l0 + len(r.lands), c0 + r.n_copies
    return parts


def _split_wait(name, state, after):
    rider, sems, buffers = state
    ns, nl, n = len(rider.srcs), len(rider.lands), rider.n_copies

    def body(*refs):
        srcs, lands = refs[:ns], refs[ns:ns + nl]
        sem_refs = refs[ns + nl:ns + nl + 2 * n]
        _, waits = rider.make([], srcs, lands, _SemList(sem_refs[:n]), _SemList(sem_refs[n:]))
        for wait in waits:
            wait()

    hbm = pl.BlockSpec(memory_space=pltpu.HBM)
    sem = pl.BlockSpec(memory_space=pltpu.SEMAPHORE)
    outs = pl.pallas_call(
        body, name=name,
        out_shape=tuple(pltpu.HBM(b.shape, b.dtype) for b in buffers),
        in_specs=[hbm] * (ns + nl) + [sem] * (2 * n) + [HBM_SPEC],
        out_specs=tuple([hbm] * (ns + nl)),
        input_output_aliases={i: i for i in range(ns + nl)},
        compiler_params=pltpu.CompilerParams(has_side_effects=pltpu.SideEffectType.DATAFLOW_SIDE_EFFECTING),
    )(*buffers, *sems, after)
    return list(outs[:ns]), list(outs[ns:])


def _pair_sum(pos, ws, g32s, recvs):
    n = len(ws)

    def body(pos_ref, *refs):
        del pos_ref
        g_refs, r_refs = refs[:n], refs[n:2 * n]
        p32_refs, p16_refs = refs[2 * n:3 * n], refs[3 * n:]
        x, y, _ = _mesh_pos()
        for i in range(n):
            tot = g_refs[i][...] + r_refs[i][...].astype(F32)
            p16_refs[i][...] = tot.astype(BF16)

            @pl.when(pl.program_id(0) == 2 * x + y)
            def _(i=i, tot=tot):
                p32_refs[i][...] = tot

    halves = [(SHARD_SHAPES[w][0] // 2, SHARD_SHAPES[w][1]) for w in ws]
    own = [pl.BlockSpec((None, None) + h, lambda j, pos_ref: (j, pos_ref[0], 0, 0)) for h in halves]
    blk = [pl.BlockSpec((None,) + h, lambda j, pos_ref: (j, 0, 0)) for h in halves]
    mine = [pl.BlockSpec(h, lambda j, pos_ref: (0, 0)) for h in halves]
    g4 = [g.reshape((N_SHARD, 2) + h) for g, h in zip(g32s, halves)]
    outs = pl.pallas_call(
        body, name="pair_sum_" + "_".join(str(w) for w in ws),
        grid_spec=pltpu.PrefetchScalarGridSpec(
            num_scalar_prefetch=1, grid=(N_SHARD,), in_specs=own + blk, out_specs=mine + blk),
        out_shape=[jax.ShapeDtypeStruct(h, F32) for h in halves]
        + [jax.ShapeDtypeStruct((N_SHARD,) + h, BF16) for h in halves],
        compiler_params=_params(("arbitrary",)),
    )(pos, *g4, *recvs)
    return outs[:n], outs[n:]


def _chip_sum(p32s, recvs):
    parts = 2

    def body(*refs):
        p_refs, r_refs, f_refs = refs[:N_BIG], refs[N_BIG:2 * N_BIG], refs[2 * N_BIG:]
        for w in range(N_BIG):
            f_refs[w][...] = ((p_refs[w][...] + r_refs[w][0].astype(F32)) + r_refs[w][1].astype(F32)) \
                + r_refs[w][2].astype(F32)

    quarters = [(r // 2 // parts, cc) for r, cc in SHARD_SHAPES]
    own = [pl.BlockSpec(qt, lambda i: (i, 0)) for qt in quarters]
    rcv = [pl.BlockSpec((3,) + qt, lambda i: (0, i, 0)) for qt in quarters]
    out = [pl.BlockSpec(qt, lambda i: (i, 0)) for qt in quarters]
    return pl.pallas_call(
        body, name="chip_sum", grid=(parts,), in_specs=own + rcv, out_specs=out,
        out_shape=[jax.ShapeDtypeStruct((r // 2, cc), F32) for r, cc in SHARD_SHAPES],
        compiler_params=_params(("arbitrary",)),
    )(*p32s, *recvs)


def _adamw(w, g, m, v):
    m_new = ADAM_B1 * m + (1.0 - ADAM_B1) * g
    v_new = ADAM_B2 * v + (1.0 - ADAM_B2) * (g * g)
    m_hat = m_new / (1.0 - ADAM_B1 ** ADAM_STEP)
    v_hat = v_new / (1.0 - ADAM_B2 ** ADAM_STEP)
    delta = -ADAM_LR * (m_hat / (jnp.sqrt(v_hat) + ADAM_EPS) + ADAM_WD * w)
    return delta, m_new, v_new


def _adam_half(name, pos, grads, ws, ms, vs, into=None):
    nb = 4
    which = (lambda ref: ref[0]) if into is None else (lambda ref: 1 - ref[0])

    def body(which_ref, *refs):
        del which_ref
        groups = [refs[i * N_BIG:(i + 1) * N_BIG] for i in range(4)]
        g_refs, w_refs, m_refs, v_refs = groups
        go_refs, do_refs, mo_refs, vo_refs = [refs[len(refs) - (4 - i) * N_BIG:len(refs) - (3 - i) * N_BIG]
                                              for i in range(4)]
        for w in range(N_BIG):
            g = g_refs[w][...]
            delta, m_new, v_new = _adamw(w_refs[w][...], g, m_refs[w][...], v_refs[w][...])
            go_refs[w][...] = g
            do_refs[w][...] = delta
            mo_refs[w][...] = m_new
            vo_refs[w][...] = v_new

    blocks = [(r // 2 // nb, cc) for r, cc in SHARD_SHAPES]
    half = [pl.BlockSpec(b, lambda i, which_ref: (i, 0)) for b in blocks]
    full = [pl.BlockSpec((None,) + b, lambda i, which_ref: (0, which(which_ref) * nb + i, 0)) for b in blocks]
    shapes = [jax.ShapeDtypeStruct((1,) + shp, F32) for shp in SHARD_SHAPES]
    carried = [] if into is None else [a for kind in into for a in kind]
    first = 1 + 4 * N_BIG
    outs = pl.pallas_call(
        body, name=name,
        grid_spec=pltpu.PrefetchScalarGridSpec(
            num_scalar_prefetch=1, grid=(nb,), in_specs=half + full * 3 + [HBM_SPEC] * len(carried),
            out_specs=full * 4),
        out_shape=shapes * 4,
        input_output_aliases={first + i: i for i in range(len(carried))},
        compiler_params=_params(("arbitrary",)),
    )(pos, *grads, *ws, *ms, *vs, *carried)
    return [outs[i * N_BIG:(i + 1) * N_BIG] for i in range(4)]


SMALL_ROWS = 8
ROW_CONV_B, ROW_POOL_SCALE, ROW_LN1_G, ROW_LN1_B, ROW_LN2_G, ROW_LN2_B, ROW_LOSS = range(7)
SMALL_VECS = ((ROW_CONV_B, D_FF), (ROW_POOL_SCALE, POOL_W), (ROW_LN1_G, D_MODEL), (ROW_LN1_B, D_MODEL),
              (ROW_LN2_G, D_MODEL), (ROW_LN2_B, D_MODEL))


def _small_adam(all_a, all_b, all_c, own_a, own_b, own_c, wp, cwp, vec_ws, m_wp, m_cwp, vec_ms,
                v_wp, v_cwp, vec_vs):
    nv = len(SMALL_VECS)
    np_ = 2 + nv

    def body(*refs):
        all_a_ref, all_b_ref, all_c_ref, own_a_ref, own_b_ref, own_c_ref = refs[0:6]
        refs = refs[3:]
        w_all, m_all, v_all = (refs[3 + i * np_:3 + (i + 1) * np_] for i in range(3))
        loss_out = refs[3 + 3 * np_]
        outs = refs[4 + 3 * np_:]
        x, y, c = _mesh_pos()
        j0 = 2 * x + y
        me = _device_number(x, y, c)

        def total(sent, own):
            by_dev = [jnp.where(me == d, own, sent(d)) for d in range(N_DEV)]
            chips = [by_dev[2 * j] + by_dev[2 * j + 1] for j in range(N_SHARD)]
            return ((chips[0] + chips[1]) + chips[2]) + chips[3]

        tot_a = total(lambda d: all_a_ref[d], own_a_ref[...])
        tot_b = total(lambda d: all_b_ref[d], own_b_ref[...])
        tot_c = total(lambda d: all_c_ref[d, j0], own_c_ref[j0])
        loss_out[...] = tot_b[ROW_LOSS:ROW_LOSS + 1, 0:1]
        grads = [tot_a, tot_c] + [tot_b[row:row + 1, 0:n] for row, n in SMALL_VECS]
        for p in range(np_):
            for at, g in ([(j, tot_c[j:j + 1]) for j in range(3)] if p == 1 else [(Ellipsis, grads[p])]):
                delta, m_new, v_new = _adamw(w_all[p][at], g, m_all[p][at], v_all[p][at])
                outs[p][at] = g
                outs[np_ + p][at] = delta
                outs[2 * np_ + p][at] = m_new
                outs[3 * np_ + p][at] = v_new

    pshapes = [wp.shape, CW_SHARD] + [wv.shape for wv in vec_ws]
    out_shape = [jax.ShapeDtypeStruct((1, 1), F32)] + [jax.ShapeDtypeStruct(s, F32) for s in pshapes] * 4
    outs = pl.pallas_call(
        body, name="small_adam",
        in_specs=[_whole()] * (6 + 3 * np_), out_specs=[_whole()] * len(out_shape), out_shape=out_shape,
        compiler_params=pltpu.CompilerParams(vmem_limit_bytes=VMEM_LIMIT),
    )(all_a, all_b, all_c, own_a, own_b, own_c, wp, cwp, *vec_ws, m_wp, m_cwp, *vec_ms, v_wp, v_cwp, *vec_vs)
    return outs[0], [outs[1 + i * np_:1 + (i + 1) * np_] for i in range(4)]


def kernel(x, w_in, w_pool, pool_scale, w_out, ln1_g, ln1_b, w_up, conv_w, conv_b, w_down, ln2_g, ln2_b, loss_target, m_w_in, m_w_pool, m_pool_scale, m_w_out, m_ln1_g, m_ln1_b, m_w_up, m_conv_w, m_conv_b, m_w_down, m_ln2_g, m_ln2_b, v_w_in, v_w_pool, v_pool_scale, v_w_out, v_ln1_g, v_ln1_b, v_w_up, v_conv_w, v_conv_b, v_w_down, v_ln2_g, v_ln2_b):
    pos = lax.axis_index("c").astype(jnp.int32).reshape(1)
    order = ("w_in", "w_out", "w_up", "w_down")
    w_in_i, w_out_i, w_up_i, w_down_i = range(N_BIG)
    vec_names = ("conv_b", "pool_scale", "ln1_g", "ln1_b", "ln2_g", "ln2_b")

    taps_first = lambda a: jnp.transpose(a, (1, 0, 2))
    gathered = _gather_weights([w_in[0], w_out[0], w_up[0], w_down[0]], taps_first(conv_w), w_pool[0], (w_in_i,))
    cw_full = jnp.transpose(gathered[N_BIG].reshape(N_SHARD, 3, DOWN_SH), (1, 0, 2)).reshape(3, D_FF)
    up_a, up_b, up_c = (0, 176), (176, 176), (352, 160)
    assert up_c[0] + up_c[1] == SHARD_SHAPES[w_up_i][0] // 2

    class MeshComm:
        def __init__(self):
            self.w = {i: gathered[i] for i in range(N_BIG)}
            self.g32, self.g16, self.p32, self.p16, self.recv_b = {}, {}, {}, {}, {}
            self.up_complete = False
            self.tokens, self.chips = {}, []

        def weight(self, name):
            i = order.index(name)
            if name == "w_up" and not self.up_complete:
                (arrs, _), = _comm_only("gather_up_last", [_gather_rider(
                    {i: self.w[i]}, [("d2d_diag", i, up_b), ("d2d", i, up_c)], (12, "sibling"))])
                self.w[i], self.up_complete = arrs[0], True
            full = self.w[i]
            return full.reshape(-1, full.shape[-1]) if name in ("w_out", "w_down") else full

        def _gather(self, ws, ops, handshake):
            return _gather_rider({w: self.w[w] for w in ws}, ops, handshake), ("w", ws)

        def _pair(self, ws):
            return _pair_rider(ws, [self.g16[w] for w in ws]), ("recv_a", ws)

        def _chip(self, ws):
            return _chip_rider(ws, [self.p16[w] for w in ws]), ("recv_b", ws)

        def plan(self, call):
            out_all, down_all = _whole_half(w_out_i), _whole_half(w_down_i)
            if call == "proj_pool":
                return [self._gather([w_out_i, w_up_i, w_down_i],
                                     [("ici", w_out_i, out_all), ("nbr", w_down_i, down_all),
                                      ("nbr", w_up_i, up_a)], (9, "chips"))]
            if call == "retention_fwd":
                return [self._gather([w_out_i, w_up_i, w_down_i],
                                     [("d2d", w_out_i, out_all),
                                      ("relay", w_down_i, down_all), ("d2d_nbr", w_down_i, down_all),
                                      ("relay", w_up_i, up_a), ("d2d_nbr", w_up_i, up_a), ("nbr", w_up_i, up_b)],
                                     (10, "both"))]
            if call == "outproj_ln1":
                return [self._gather([w_up_i, w_down_i],
                                     [("d2d_diag", w_down_i, down_all), ("d2d_diag", w_up_i, up_a),
                                      ("relay", w_up_i, up_b), ("d2d_nbr", w_up_i, up_b), ("ici", w_up_i, up_c)],
                                     (11, "both"))]
            return []

        def after(self, call):
            return tuple(self.tokens.pop(call, ()))

        def riders(self, call):
            self.pending = self.plan(call)
            return [r for r, _ in self.pending]

        def _start(self, name, rider, before, handshake):
            state, token = _split_start(name, rider, handshake)
            self.tokens.setdefault(before, []).append(token)
            return state

        def _finish_pair(self, name, state, ws, after):
            _, lands = _split_wait(name, state, after)
            self._finish_sum(ws, lands)

        def landed(self, call, results, outs):
            for (_, (slot, ws)), (inplace, lands) in zip(self.pending, results):
                for w, arr in zip(ws, inplace if len(inplace) else lands):
                    getattr(self, slot)[w] = arr
            if call == "wgrad_out":
                self._finish_pair("pair_exchange_up_wait", self.pair_up, [w_up_i], outs[1])
                self.chips.append(([w_up_i], self._start(
                    "chip_exchange_up_start", self._chip([w_up_i])[0], "wgrad_down", (5, "chips"))))
            if call == "mix_bwd":
                ws = [w_out_i, w_down_i]
                self._finish_pair("pair_exchange_out_down_wait", self.pair_out_down, ws, outs[0])

        def small_gradients(self, small, packed):
            dcw4 = jnp.transpose(small["conv_w"].reshape(3, N_SHARD, DOWN_SH), (1, 0, 2))
            own = [small["w_pool"], packed, dcw4]
            ws = [w_out_i, w_down_i]
            parts = [self._chip(ws)[0], _small_all_rider(own)]
            chip, self.small_all = _split_parts(
                self._start("chip_out_down_small_all_start", _merged_rider(parts), "retention_bwd",
                            (7, "all")), parts)
            self.chips.append((ws, chip))

        def gradient(self, name, g32, g16):
            w = order.index(name)
            shape = (N_SHARD,) + SHARD_SHAPES[w]
            self.g32[w], self.g16[w] = g32.reshape(shape), g16.reshape(shape)
            if name == "w_up":
                self.pair_up = self._start("pair_exchange_up_start", self._pair([w])[0], "wgrad_out",
                                           (1, "sibling"))
            if name == "w_down":
                self.pair_out_down = self._start("pair_exchange_out_down_start",
                                                 self._pair([w_out_i, w_down_i])[0], "mix_bwd", (2, "sibling"))

        def wgrad_in(self, xb, dproj):
            w = w_in_i
            g32, landed = _wgrad_send(xb, dproj, IN_SH, "wgrad_in", 4, after=self.after("wgrad_in"))
            self.g32[w] = g32
            self._finish_sum([w], [landed])
            self.chips.append(([w], self._start("chip_exchange_in_start", self._chip([w])[0], "dx", (8, "chips"))))

        def _finish_sum(self, ws, lands):
            p32s, p16s = _pair_sum(pos, ws, [self.g32[w] for w in ws], lands)
            for w, p32, p16 in zip(ws, p32s, p16s):
                self.p32[w], self.p16[w] = p32, p16

        def finish(self, after):
            for n, (ws, state) in enumerate(self.chips):
                _, lands = _split_wait("chip_exchange_wait_%d" % n, state, after)
                for w, arr in zip(ws, lands):
                    self.recv_b[w] = arr
            own, sent = _split_wait("small_all_wait", self.small_all, after)
            return list(sent) + list(own)

    comm = MeshComm()
    loss, grad_x, small = _local_step(x[0], loss_target[0], cw_full, conv_b, gathered[N_BIG + 1], pool_scale,
                                      ln1_g, ln1_b, ln2_g, ln2_b, comm)

    given = dict(w_pool=w_pool, pool_scale=pool_scale, ln1_g=ln1_g, ln1_b=ln1_b, conv_w=conv_w, conv_b=conv_b,
                 ln2_g=ln2_g, ln2_b=ln2_b)
    given_m = dict(w_pool=m_w_pool, pool_scale=m_pool_scale, ln1_g=m_ln1_g, ln1_b=m_ln1_b, conv_w=m_conv_w,
                   conv_b=m_conv_b, ln2_g=m_ln2_g, ln2_b=m_ln2_b)
    given_v = dict(w_pool=v_w_pool, pool_scale=v_pool_scale, ln1_g=v_ln1_g, ln1_b=v_ln1_b, conv_w=v_conv_w,
                   conv_b=v_conv_b, ln2_g=v_ln2_g, ln2_b=v_ln2_b)
    args = []
    for src in (given, given_m, given_v):
        args += [src["w_pool"][0], taps_first(src["conv_w"]), [src[n] for n in vec_names]]
    small_sums = comm.finish(grad_x)
    loss_tot, small_out = _small_adam(*small_sums, *args)
    every = range(N_BIG)
    mine = _chip_sum([comm.p32[w] for w in every], [comm.recv_b[w] for w in every])
    final_state, _ = _split_start("pair_exchange_f32_start", _final_rider(mine), (3, "sibling"))
    mine = final_state[2][:N_BIG]
    big = ([w_in, w_out, w_up, w_down], [m_w_in, m_w_out, m_w_up, m_w_down], [v_w_in, v_w_out, v_w_up, v_w_down])
    own_half = _adam_half("adam_own_half", pos, mine, *big)
    _, theirs = _split_wait("pair_exchange_f32_wait", final_state, own_half[0][0])
    big_out = _adam_half("adam_other_half", pos, theirs, *big, into=own_half)

    names = ("w_in", "w_pool", "pool_scale", "w_out", "ln1_g", "ln1_b", "w_up", "conv_w", "conv_b", "w_down",
             "ln2_g", "ln2_b")
    small_names = ("w_pool", "conv_w") + vec_names
    result = [loss_tot.reshape(()), grad_x[None]]
    for kind in range(4):
        for n in names:
            if n in order:
                result.append(big_out[kind][order.index(n)])
            else:
                val = small_out[kind][small_names.index(n)]
                if n == "conv_w":
                    val = taps_first(val)
                elif n == "w_pool":
                    val = val[None]
                result.append(val)
    return tuple(result)
```

```python
import functools

import numpy as np
import jax
import jax.numpy as jnp
from jax import lax
from jax.experimental import pallas as pl
from jax.experimental.pallas import tpu as pltpu

F32 = jnp.float32
BF16 = jnp.bfloat16

D_MODEL = 1024
HEADS = 4
HEAD_DIM = 128
RET_W = HEADS * HEAD_DIM
POOL_WINDOWS = (2, 4, 8, 16)
POOL_W = 512
IN_W = 4 * RET_W + POOL_W
D_FF = 2816
N_SHARD = 4
IN_SH = IN_W // N_SHARD
UP_SH = 2 * D_FF // N_SHARD
DOWN_SH = D_FF // N_SHARD
OUT_SH = D_MODEL // N_SHARD
ROPE_BASE = 10000.0
LN_EPS = 1e-5
RMS_EPS = 1e-6
ALPHA = 2.0 ** 0.25
K_SCALE = HEAD_DIM ** -0.5
SUPER = 256
CHUNK = 64
POOL_HALO = 16
CONV_HALO = 8
FFN_STRIP = 128
LN_ROWS = 32

ADAM_LR = 0.001
ADAM_B1 = 0.9
ADAM_B2 = 0.999
ADAM_EPS = 1e-08
ADAM_WD = 0.01
ADAM_STEP = 10

MESH = pl.DeviceIdType.MESH
VMEM_LIMIT = 56 * 1024 * 1024


def _dot(a, b):
    return jnp.dot(a, b, preferred_element_type=F32)


def _dot_nt(a, b):
    return lax.dot_general(a, b, (((1,), (1,)), ((), ())), preferred_element_type=F32)


def _dot_tn(a, b):
    return lax.dot_general(a, b, (((0,), (0,)), ((), ())), preferred_element_type=F32)


def _sigmoid(x):
    return 1.0 / (1.0 + jnp.exp(-x))


def _params(sem):
    return pltpu.CompilerParams(dimension_semantics=sem, vmem_limit_bytes=VMEM_LIMIT)


def _whole():
    return pl.BlockSpec(memory_space=pltpu.VMEM)


HBM_SPEC = pl.BlockSpec(memory_space=pl.ANY)


class _Rider:
    def __init__(self, inplace, srcs, lands, n_copies, make, handshake=None):
        self.inplace, self.srcs, self.lands, self.n_copies, self.make = list(inplace), list(srcs), list(lands), n_copies, make
        self.handshake = handshake


def _call(body, *, name, grid, in_specs, out_specs, out_shape, operands, scratch_shapes=(), sem=(),
          aliases=None, riders=(), after=()):
    n_in, n_out, n_scr = len(in_specs), len(out_shape), len(scratch_shapes)
    in_specs, out_specs, out_shape = list(in_specs), list(out_specs), list(out_shape)
    operands, scratch_shapes, aliases = list(operands), list(scratch_shapes), dict(aliases or {})
    in_specs += [_whole()] * len(after)
    operands += list(after)
    shakes = [r.handshake for r in riders if r.handshake is not None]
    assert len(shakes) <= 1
    for r in riders:
        for a in r.inplace:
            aliases[len(in_specs)] = len(out_shape)
            in_specs.append(HBM_SPEC)
            operands.append(a)
            out_specs.append(HBM_SPEC)
            out_shape.append(jax.ShapeDtypeStruct(a.shape, a.dtype))
        for a in r.srcs:
            in_specs.append(HBM_SPEC)
            operands.append(a)
        for shp in r.lands:
            out_specs.append(HBM_SPEC)
            out_shape.append(shp)
        scratch_shapes += [pltpu.SemaphoreType.DMA((r.n_copies,)), pltpu.SemaphoreType.DMA((r.n_copies,))]

    def full(*refs):
        ins = refs[:n_in]
        at = n_in + len(after)
        r_srcs = []
        for r in riders:
            at += len(r.inplace)
            r_srcs.append(refs[at:at + len(r.srcs)])
            at += len(r.srcs)
        outs = refs[at:at + n_out]
        at += n_out
        r_outs = []
        for r in riders:
            r_outs.append((refs[at:at + len(r.inplace)], refs[at + len(r.inplace):at + len(r.inplace) + len(r.lands)]))
            at += len(r.inplace) + len(r.lands)
        scr = refs[at:at + n_scr]
        at += n_scr
        r_sems = [refs[at + 2 * i:at + 2 * i + 2] for i in range(len(riders))]

        def copies():
            return [r.make(r_outs[i][0], r_srcs[i], r_outs[i][1], r_sems[i][0], r_sems[i][1])
                    for i, r in enumerate(riders)]

        def start():
            if shakes:
                _shake_hands(shakes[0][1])
            for starts, _ in copies():
                for cp in starts:
                    cp.start()

        def finish():
            for _, waits in copies():
                for wait in waits:
                    wait()

        if riders and grid:
            first = functools.reduce(jnp.logical_and, [pl.program_id(d) == 0 for d in range(len(grid))])
            last = functools.reduce(jnp.logical_and, [pl.program_id(d) == grid[d] - 1 for d in range(len(grid))])
            pl.when(first)(start)
            body(*ins, *outs, *scr)
            pl.when(last)(finish)
        else:
            if riders:
                start()
            body(*ins, *outs, *scr)
            if riders:
                finish()

    barrier_id = shakes[0][0] if shakes else None
    params = pltpu.CompilerParams(vmem_limit_bytes=VMEM_LIMIT, collective_id=barrier_id,
                                  **(dict(dimension_semantics=sem) if grid else {}))
    res = pl.pallas_call(
        full, name=name, grid=grid, in_specs=in_specs, out_specs=out_specs, out_shape=out_shape,
        scratch_shapes=scratch_shapes, input_output_aliases=aliases, compiler_params=params,
    )(*operands)
    outs, at, rider_res = res[:n_out], n_out, []
    for r in riders:
        rider_res.append((res[at:at + len(r.inplace)], res[at + len(r.inplace):at + len(r.inplace) + len(r.lands)]))
        at += len(r.inplace) + len(r.lands)
    return list(outs), rider_res


def _gammas():
    return [1.0 - 2.0 ** (-5.0 - h) for h in range(HEADS)]


def _decay_tables():
    idx = np.arange(SUPER)
    dist = np.abs(idx[:, None] - idx[None, :]).astype(np.float64)
    visible = (idx[None, :] // CHUNK) <= (idx[:, None] // CHUNK)
    mask = np.stack([np.where(visible, g ** dist, 0.0) for g in _gammas()])
    qd = np.concatenate([np.repeat((g ** (idx + 1.0))[:, None], HEAD_DIM, 1) for g in _gammas()], 1)
    kd = np.concatenate([np.repeat((g ** (SUPER - 1.0 - idx))[:, None], HEAD_DIM, 1) for g in _gammas()], 1)
    return (jnp.asarray(mask, F32), jnp.asarray(qd, F32), jnp.asarray(kd, F32))


def _rope_tables(s):
    inv_freq = ROPE_BASE ** (-np.arange(0, HEAD_DIM, 2, dtype=np.float64) / HEAD_DIM)
    ang = np.arange(s, dtype=np.float64)[:, None] * inv_freq[None, :]
    cos, sin = np.cos(ang), np.sin(ang)
    return (jnp.asarray(np.concatenate([cos, cos], 1), F32),
            jnp.asarray(np.concatenate([-sin, sin], 1), F32))


def _rope(t, cosf, sinf):
    return t * cosf + pltpu.roll(t, HEAD_DIM // 2, 1) * sinf


def _rope_t(t, cosf, sinf):
    return t * cosf - pltpu.roll(t, HEAD_DIM // 2, 1) * sinf


def _layernorm_fwd(z):
    mu = jnp.mean(z, axis=-1, keepdims=True)
    zc = z - mu
    var = jnp.mean(zc * zc, axis=-1, keepdims=True)
    rstd = lax.rsqrt(var + LN_EPS)
    return zc * rstd, rstd


def _layernorm_bwd(dy, xhat, rstd, gain):
    dxh = dy * gain
    m1 = jnp.mean(dxh, axis=-1, keepdims=True)
    m2 = jnp.mean(dxh * xhat, axis=-1, keepdims=True)
    return rstd * (dxh - m1 - xhat * m2)


def _proj_pool(x, win4, cosf, sinf, wpool, pscale, ts, riders=(), after=()):
    s = x.shape[0]
    nt = s // ts

    def body(x_ref, w_ref, cos_ref, sin_ref, wp_ref, ps_ref,
             xb_ref, q_ref, k_ref, v_ref, g_ref, pooled_ref, cat_ref, proj_scr, pext_scr):
        i = pl.program_id(0)
        xb = x_ref[...].astype(BF16)
        xb_ref[...] = xb
        for j in range(N_SHARD):
            proj_scr[:, j * IN_SH:(j + 1) * IN_SH] = _dot(xb, w_ref[j])
        cosf_t = cos_ref[...]
        sinf_t = sin_ref[...]
        for h in range(HEADS):
            lo = h * HEAD_DIM
            q_ref[:, lo:lo + HEAD_DIM] = _rope(proj_scr[:, lo:lo + HEAD_DIM], cosf_t, sinf_t).astype(BF16)
            kk = _rope(proj_scr[:, RET_W + lo:RET_W + lo + HEAD_DIM], cosf_t, sinf_t) * K_SCALE
            k_ref[:, lo:lo + HEAD_DIM] = kk.astype(BF16)
        v_ref[...] = proj_scr[:, 2 * RET_W:3 * RET_W].astype(BF16)
        g_ref[...] = proj_scr[:, 3 * RET_W:4 * RET_W]

        @pl.when(i == 0)
        def _():
            pext_scr[0:POOL_HALO, :] = jnp.zeros((POOL_HALO, POOL_W), F32)

        pext_scr[POOL_HALO:POOL_HALO + ts, :] = proj_scr[:, 4 * RET_W:IN_W]
        pos = (i * ts + lax.broadcasted_iota(jnp.int32, (ts, 1), 0) + 1).astype(F32)
        for gi, w in enumerate(POOL_WINDOWS):
            lo = gi * HEAD_DIM
            ext = pext_scr[:, lo:lo + HEAD_DIM]
            acc = ext
            shift = 1
            while shift < w:
                acc = acc + pltpu.roll(acc, shift, 0)
                shift *= 2
            tok = ext[POOL_HALO:POOL_HALO + ts]
            pooled = acc[POOL_HALO:POOL_HALO + ts] / jnp.minimum(pos, float(w)) - tok
            pooled_b = pooled.astype(BF16)
            pooled_ref[:, lo:lo + HEAD_DIM] = pooled_b
            lin = _dot(pooled_b, wp_ref[gi])
            cat_ref[:, lo:lo + HEAD_DIM] = (lin * ps_ref[:, lo:lo + HEAD_DIM]).astype(BF16)
        pext_scr[0:POOL_HALO, :] = pext_scr[ts:ts + POOL_HALO, :]

    tile = lambda w: pl.BlockSpec((ts, w), lambda i: (i, 0))
    return _call(
        body, name="proj_pool", grid=(nt,),
        in_specs=[tile(D_MODEL), _whole(), tile(HEAD_DIM), tile(HEAD_DIM), _whole(), _whole()],
        out_specs=[tile(D_MODEL), tile(RET_W), tile(RET_W), tile(RET_W), tile(RET_W), tile(POOL_W),
                   pl.BlockSpec((ts, POOL_W), lambda i: (i, 1))],
        out_shape=[jax.ShapeDtypeStruct((s, D_MODEL), BF16), jax.ShapeDtypeStruct((s, RET_W), BF16),
                   jax.ShapeDtypeStruct((s, RET_W), BF16), jax.ShapeDtypeStruct((s, RET_W), BF16),
                   jax.ShapeDtypeStruct((s, RET_W), F32), jax.ShapeDtypeStruct((s, POOL_W), BF16),
                   jax.ShapeDtypeStruct((s, 2 * RET_W), BF16)],
        scratch_shapes=[pltpu.VMEM((ts, IN_W), F32), pltpu.VMEM((ts + POOL_HALO, POOL_W), F32)],
        sem=("arbitrary",), operands=(x, win4, cosf, sinf, wpool, pscale), riders=riders, after=after,
    )


def _retention_fwd(q, k, v, g, cat, mask, qd, kd, riders=(), after=()):
    s = q.shape[0]
    ns = s // SUPER
    cdec = [gm ** float(SUPER) for gm in _gammas()]

    def body(q_ref, k_ref, v_ref, g_ref, cat_in, mask_ref, qd_ref, kd_ref,
             ret_ref, cat_ref, st_ref, state_scr):
        del cat_in
        n = pl.program_id(0)

        @pl.when(n == 0)
        def _():
            state_scr[...] = jnp.zeros_like(state_scr)

        for h in range(HEADS):
            sl = slice(h * HEAD_DIM, (h + 1) * HEAD_DIM)
            qh, kh, vh = q_ref[:, sl], k_ref[:, sl], v_ref[:, sl]
            sc = _dot_nt(qh, kh) * mask_ref[h]
            st = state_scr[h]
            stb = st.astype(BF16)
            st_ref[0, h] = stb
            qdb = (qh.astype(F32) * qd_ref[:, sl]).astype(BF16)
            kdb = (kh.astype(F32) * kd_ref[:, sl]).astype(BF16)
            ret = _dot(sc.astype(BF16), vh) + _dot(qdb, stb)
            state_scr[h] = st * cdec[h] + _dot_tn(kdb, vh)
            ret_ref[:, sl] = ret
            r = lax.rsqrt(jnp.mean(ret * ret, axis=-1, keepdims=True) + RMS_EPS)
            gh = g_ref[:, sl]
            cat_ref[:, sl] = ((ret * r) * (gh * _sigmoid(gh))).astype(BF16)

    tile = pl.BlockSpec((SUPER, RET_W), lambda n: (n, 0))
    return _call(
        body, name="retention_fwd", grid=(ns,),
        in_specs=[tile, tile, tile, tile, HBM_SPEC, _whole(), _whole(), _whole()],
        out_specs=[tile, tile, pl.BlockSpec((1, HEADS, HEAD_DIM, HEAD_DIM), lambda n: (n, 0, 0, 0))],
        out_shape=[jax.ShapeDtypeStruct((s, RET_W), F32), jax.ShapeDtypeStruct((s, 2 * RET_W), BF16),
                   jax.ShapeDtypeStruct((ns, HEADS, HEAD_DIM, HEAD_DIM), BF16)],
        scratch_shapes=[pltpu.VMEM((HEADS, HEAD_DIM, HEAD_DIM), F32)],
        aliases={4: 1}, sem=("arbitrary",), operands=(q, k, v, g, cat, mask, qd, kd), riders=riders,
        after=after,
    )


def _outproj_ln1(x, cat, wout, g1, b1, ts, riders=(), after=()):
    s = x.shape[0]

    def body(x_ref, cat_ref, w_ref, g_ref, b_ref, xhat_ref, rstd_ref, h1b_ref):
        z = ALPHA * x_ref[...] + _dot(cat_ref[...], w_ref[...])
        xhat, rstd = _layernorm_fwd(z)
        xhat_ref[...] = xhat
        rstd_ref[...] = rstd
        h1b_ref[...] = (xhat * g_ref[...] + b_ref[...]).astype(BF16)

    tile = lambda w: pl.BlockSpec((ts, w), lambda i: (i, 0))
    return _call(
        body, name="outproj_ln1", grid=(s // ts,),
        in_specs=[tile(D_MODEL), tile(D_MODEL), _whole(), _whole(), _whole()],
        out_specs=[tile(D_MODEL), tile(1), tile(D_MODEL)],
        out_shape=[jax.ShapeDtypeStruct((s, D_MODEL), F32), jax.ShapeDtypeStruct((s, 1), F32),
                   jax.ShapeDtypeStruct((s, D_MODEL), BF16)],
        sem=("arbitrary",), operands=(x, cat, wout, g1, b1), riders=riders, after=after,
    )


def _ffn_fwd_loss(xhat1, h1b, target, wup4, wdown, cw, cb, g1, b1, g2, b2, ts):
    s = xhat1.shape[0]

    def body(xhat_ref, h1b_ref, tgt_ref, wup_ref, wdn_ref, cw_ref, cb_ref, g1_ref, b1_ref, g2_ref, b2_ref,
             ub_ref, act_ref, sd_ref, dz2_ref, dz2b_ref, loss_ref, dg2_ref, db2_ref, val_scr, gext_scr, ffn_scr):
        i = pl.program_id(0)

        @pl.when(i == 0)
        def _():
            gext_scr[0:CONV_HALO, :] = jnp.zeros((CONV_HALO, D_FF), F32)
            loss_ref[...] = jnp.zeros_like(loss_ref)
            dg2_ref[...] = jnp.zeros_like(dg2_ref)
            db2_ref[...] = jnp.zeros_like(db2_ref)

        for half in range(2):
            lo = half * UP_SH
            gext_scr[CONV_HALO:CONV_HALO + ts, lo:lo + UP_SH] = _dot(h1b_ref[...], wup_ref[2 + half])
            val_scr[:, lo:lo + UP_SH] = _dot(h1b_ref[...], wup_ref[half])
            for c0 in range(lo, lo + UP_SH, FFN_STRIP):
                cols = slice(c0, c0 + FFN_STRIP)
                ext = gext_scr[:, cols]
                gate = ext[CONV_HALO:]
                hc = cb_ref[:, cols] + ((pltpu.roll(ext, 2, 0)[CONV_HALO:] * cw_ref[0:1, cols]
                                         + pltpu.roll(ext, 1, 0)[CONV_HALO:] * cw_ref[1:2, cols])
                                        + gate * cw_ref[2:3, cols])
                val = val_scr[:, cols]
                sg = _sigmoid(hc)
                si = hc * sg
                act_ref[:, cols] = (si * val).astype(BF16)
                ub_ref[:, cols] = val.astype(BF16)
                ub_ref[:, D_FF + c0:D_FF + c0 + FFN_STRIP] = gate.astype(BF16)
                sd_ref[:, cols] = hc.astype(BF16)
            part = _dot(act_ref[:, lo:lo + UP_SH], wdn_ref[lo:lo + UP_SH, :])
            if half == 0:
                ffn_scr[...] = part
            else:
                ffn_scr[...] += part

        gext_scr[0:CONV_HALO, :] = gext_scr[ts:ts + CONV_HALO, :]

        loss_acc = jnp.zeros((1, 1), F32)
        dg2_acc = jnp.zeros((1, D_MODEL), F32)
        db2_acc = jnp.zeros((1, D_MODEL), F32)
        for r0 in range(0, ts, LN_ROWS):
            rows = slice(r0, r0 + LN_ROWS)
            h1 = xhat_ref[rows, :] * g1_ref[...] + b1_ref[...]
            xhat2, rstd2 = _layernorm_fwd(ALPHA * h1 + ffn_scr[rows, :])
            diff = (xhat2 * g2_ref[...] + b2_ref[...]) - tgt_ref[rows, :]
            row = jnp.mean(diff * diff, axis=-1, keepdims=True)
            loss_acc = loss_acc + 0.5 * jnp.sum(row, axis=0, keepdims=True)
            dy = diff * (1.0 / D_MODEL)
            dg2_acc = dg2_acc + jnp.sum(dy * xhat2, axis=0, keepdims=True)
            db2_acc = db2_acc + jnp.sum(dy, axis=0, keepdims=True)
            dz2 = _layernorm_bwd(dy, xhat2, rstd2, g2_ref[...])
            dz2_ref[rows, :] = dz2
            dz2b_ref[rows, :] = dz2.astype(BF16)
        loss_ref[...] += loss_acc
        dg2_ref[...] += dg2_acc
        db2_ref[...] += db2_acc

    tile = lambda w: pl.BlockSpec((ts, w), lambda i: (i, 0))
    acc = lambda w: pl.BlockSpec((1, w), lambda i: (0, 0))
    return pl.pallas_call(
        body, name="ffn_fwd_loss", grid=(s // ts,),
        in_specs=[tile(D_MODEL), tile(D_MODEL), tile(D_MODEL)] + [_whole()] * 8,
        out_specs=[tile(2 * D_FF), tile(D_FF), tile(D_FF), tile(D_MODEL), tile(D_MODEL),
                   acc(1), acc(D_MODEL), acc(D_MODEL)],
        out_shape=[jax.ShapeDtypeStruct((s, 2 * D_FF), BF16), jax.ShapeDtypeStruct((s, D_FF), BF16),
                   jax.ShapeDtypeStruct((s, D_FF), BF16), jax.ShapeDtypeStruct((s, D_MODEL), F32),
                   jax.ShapeDtypeStruct((s, D_MODEL), BF16),
                   jax.ShapeDtypeStruct((1, 1), F32), jax.ShapeDtypeStruct((1, D_MODEL), F32),
                   jax.ShapeDtypeStruct((1, D_MODEL), F32)],
        scratch_shapes=[pltpu.VMEM((ts, D_FF), F32), pltpu.VMEM((ts + CONV_HALO, D_FF), F32),
                        pltpu.VMEM((ts, D_MODEL), F32)],
        compiler_params=_params(("arbitrary",)),
    )(xhat1, h1b, target, wup4, wdown, cw, cb, g1, b1, g2, b2)


def _ffn_bwd(dz2, dz2b, ub, sd, xhat1, rstd1, wup4, wdown, cw, g1, ts):
    s = dz2.shape[0]
    nt = s // ts

    def body(dz2_ref, dz2b_ref, ub_ref, sd_ref, xhat_ref, rstd_ref, wup_ref, wdn_ref, cw_ref, g1_ref,
             dub_ref, dz1_ref, dz1b_ref, dg1_ref, db1_ref, dcw_ref, dcb_ref, dext_scr, da_scr):
        i = pl.program_id(0)

        @pl.when(i == 0)
        def _():
            dext_scr[ts:ts + CONV_HALO, :] = jnp.zeros((CONV_HALO, D_FF), F32)
            dg1_ref[...] = jnp.zeros_like(dg1_ref)
            db1_ref[...] = jnp.zeros_like(db1_ref)
            dcw_ref[...] = jnp.zeros_like(dcw_ref)
            dcb_ref[...] = jnp.zeros_like(dcb_ref)

        da_scr[...] = _dot_nt(dz2b_ref[...], wdn_ref[...])
        n_ext = ts + CONV_HALO
        for c0 in range(0, D_FF, FFN_STRIP):
            cols = slice(c0, c0 + FFN_STRIP)
            gcols = slice(D_FF + c0, D_FF + c0 + FFN_STRIP)
            val = ub_ref[:, cols].astype(F32)
            gate = ub_ref[:, gcols].astype(F32)
            da = da_scr[:, cols]
            hc = sd_ref[:, cols].astype(F32)
            sg = _sigmoid(hc)
            dhc = da * val * (sg * (1.0 + hc * (1.0 - sg)))
            dext_scr[0:ts, cols] = dhc
            dext = dext_scr[:, cols]
            dhc1 = pltpu.roll(dext, n_ext - 1, 0)[0:ts]
            dhc2 = pltpu.roll(dext, n_ext - 2, 0)[0:ts]
            dcb_ref[:, cols] += jnp.sum(dhc, axis=0, keepdims=True)
            dcw_ref[0:1, cols] += jnp.sum(dhc2 * gate, axis=0, keepdims=True)
            dcw_ref[1:2, cols] += jnp.sum(dhc1 * gate, axis=0, keepdims=True)
            dcw_ref[2:3, cols] += jnp.sum(dhc * gate, axis=0, keepdims=True)
            dgate = dhc * cw_ref[2:3, cols] + dhc1 * cw_ref[1:2, cols] + dhc2 * cw_ref[0:1, cols]
            dub_ref[:, cols] = (da * (hc * sg)).astype(BF16)
            dub_ref[:, gcols] = dgate.astype(BF16)
        dext_scr[ts:n_ext, :] = dext_scr[0:CONV_HALO, :]
        dh1 = ALPHA * dz2_ref[...]
        for j in range(N_SHARD):
            dh1 = dh1 + _dot_nt(dub_ref[:, j * UP_SH:(j + 1) * UP_SH], wup_ref[j])
        xhat = xhat_ref[...]
        dg1_ref[...] += jnp.sum(dh1 * xhat, axis=0, keepdims=True)
        db1_ref[...] += jnp.sum(dh1, axis=0, keepdims=True)
        dz1 = _layernorm_bwd(dh1, xhat, rstd_ref[...], g1_ref[...])
        dz1_ref[...] = dz1
        dz1b_ref[...] = dz1.astype(BF16)

    tile = lambda w: pl.BlockSpec((ts, w), lambda i: (nt - 1 - i, 0))
    acc = lambda rws, w: pl.BlockSpec((rws, w), lambda i: (0, 0))
    return pl.pallas_call(
        body, name="ffn_bwd", grid=(nt,),
        in_specs=[tile(D_MODEL), tile(D_MODEL), tile(2 * D_FF), tile(D_FF), tile(D_MODEL), tile(1)]
        + [_whole()] * 4,
        out_specs=[tile(2 * D_FF), tile(D_MODEL), tile(D_MODEL), acc(1, D_MODEL), acc(1, D_MODEL),
                   acc(3, D_FF), acc(1, D_FF)],
        out_shape=[jax.ShapeDtypeStruct((s, 2 * D_FF), BF16),
                   jax.ShapeDtypeStruct((s, D_MODEL), F32), jax.ShapeDtypeStruct((s, D_MODEL), BF16),
                   jax.ShapeDtypeStruct((1, D_MODEL), F32),
                   jax.ShapeDtypeStruct((1, D_MODEL), F32), jax.ShapeDtypeStruct((3, D_FF), F32),
                   jax.ShapeDtypeStruct((1, D_FF), F32)],
        scratch_shapes=[pltpu.VMEM((ts + CONV_HALO, D_FF), F32), pltpu.VMEM((ts, D_FF), F32)],
        compiler_params=_params(("arbitrary",)),
    )(dz2, dz2b, ub, sd, xhat1, rstd1, wup4, wdown, cw, g1)


def _mix_bwd(dz1, pooled, ret, g, wout, wpool, pscale, loss, vec_grads, ts, riders=(), after=()):
    s = dz1.shape[0]
    nt = s // ts

    def body(dz1_ref, pooled_ref, ret_ref, g_ref, wout_ref, wp_ref, ps_ref, loss_ref, dcb_ref, dg1_ref, db1_ref,
             dg2_ref, db2_ref, dret_ref, dgp_ref, dwp_ref, dps_ref, packed_ref, eext_scr):
        i = pl.program_id(0)
        r = nt - 1 - i

        @pl.when(i == 0)
        def _():
            eext_scr[ts:ts + POOL_HALO, :] = jnp.zeros((POOL_HALO, POOL_W), F32)
            dwp_ref[...] = jnp.zeros_like(dwp_ref)
            dps_ref[...] = jnp.zeros_like(dps_ref)

        dzb = dz1_ref[...].astype(BF16)
        dcat_r = _dot_nt(dzb, wout_ref[0:RET_W, :])
        dcat_p = _dot_nt(dzb, wout_ref[RET_W:2 * RET_W, :])
        pos = (r * ts + lax.broadcasted_iota(jnp.int32, (ts, 1), 0) + 1).astype(F32)
        dpooled = []
        for gi, w in enumerate(POOL_WINDOWS):
            sl = slice(gi * HEAD_DIM, (gi + 1) * HEAD_DIM)
            pb = pooled_ref[:, sl]
            dy = dcat_p[:, sl]
            dps_ref[:, sl] += jnp.sum(dy * _dot(pb, wp_ref[gi]), axis=0, keepdims=True)
            dlin = (dy * ps_ref[:, sl]).astype(BF16)
            dwp_ref[gi] += _dot_tn(pb, dlin)
            dpg = _dot_nt(dlin, wp_ref[gi])
            dpooled.append(dpg)
            eext_scr[0:ts, sl] = dpg / jnp.minimum(pos, float(w))
        for gi, w in enumerate(POOL_WINDOWS):
            sl = slice(gi * HEAD_DIM, (gi + 1) * HEAD_DIM)
            acc = eext_scr[:, sl]
            shift = 1
            while shift < w:
                acc = acc + pltpu.roll(acc, ts + POOL_HALO - shift, 0)
                shift *= 2
            dgp_ref[:, RET_W + gi * HEAD_DIM:RET_W + (gi + 1) * HEAD_DIM] = (acc[0:ts] - dpooled[gi]).astype(BF16)
        eext_scr[ts:ts + POOL_HALO, :] = eext_scr[0:POOL_HALO, :]
        for h in range(HEADS):
            sl = slice(h * HEAD_DIM, (h + 1) * HEAD_DIM)
            rt = ret_ref[:, sl]
            rr = lax.rsqrt(jnp.mean(rt * rt, axis=-1, keepdims=True) + RMS_EPS)
            rn = rt * rr
            gh = g_ref[:, sl]
            sg = _sigmoid(gh)
            dy = dcat_r[:, sl]
            dgp_ref[:, sl] = (dy * rn * (sg * (1.0 + gh * (1.0 - sg)))).astype(BF16)
            drn = dy * (gh * sg)
            dret_ref[:, sl] = (rr * (drn - rn * jnp.mean(drn * rn, axis=-1, keepdims=True))).astype(BF16)

        @pl.when(i == nt - 1)
        def _():
            packed_ref[...] = jnp.zeros_like(packed_ref)
            rows = (dcb_ref, dps_ref, dg1_ref, db1_ref, dg2_ref, db2_ref)
            for (row, n), ref in zip(SMALL_VECS, rows):
                packed_ref[row:row + 1, 0:n] = ref[...]
            packed_ref[ROW_LOSS:ROW_LOSS + 1, 0:HEAD_DIM] = jnp.broadcast_to(loss_ref[...], (1, HEAD_DIM))

    tile = lambda w: pl.BlockSpec((ts, w), lambda i: (nt - 1 - i, 0))
    return _call(
        body, name="mix_bwd", grid=(nt,),
        in_specs=[tile(D_MODEL), tile(POOL_W), tile(RET_W), tile(RET_W)] + [_whole()] * 9,
        out_specs=[tile(RET_W), tile(2 * RET_W),
                   pl.BlockSpec((len(POOL_WINDOWS), HEAD_DIM, HEAD_DIM), lambda i: (0, 0, 0)),
                   pl.BlockSpec((1, POOL_W), lambda i: (0, 0)),
                   pl.BlockSpec((SMALL_ROWS, D_FF), lambda i: (0, 0))],
        out_shape=[jax.ShapeDtypeStruct((s, RET_W), BF16), jax.ShapeDtypeStruct((s, 2 * RET_W), BF16),
                   jax.ShapeDtypeStruct((len(POOL_WINDOWS), HEAD_DIM, HEAD_DIM), F32),
                   jax.ShapeDtypeStruct((1, POOL_W), F32), jax.ShapeDtypeStruct((SMALL_ROWS, D_FF), F32)],
        scratch_shapes=[pltpu.VMEM((ts + POOL_HALO, POOL_W), F32)],
        sem=("arbitrary",), operands=(dz1, pooled, ret, g, wout, wpool, pscale, loss, *vec_grads), riders=riders,
        after=after,
    )


def _retention_bwd(q, k, v, dret, dgp, states, mask, qd, kd, cosf, sinf, riders=(), after=()):
    s = q.shape[0]
    ns = s // SUPER
    cdec = [gm ** float(SUPER) for gm in _gammas()]

    def body(q_ref, k_ref, v_ref, do_ref, dgp_ref, st_ref, mask_ref, qd_ref, kd_ref, cos_ref, sin_ref,
             dproj_ref, dstate_scr):
        i = pl.program_id(0)

        @pl.when(i == 0)
        def _():
            dstate_scr[...] = jnp.zeros_like(dstate_scr)

        cosf_t = cos_ref[...]
        sinf_t = sin_ref[...]
        for h in range(HEADS):
            sl = slice(h * HEAD_DIM, (h + 1) * HEAD_DIM)
            qh, kh, vh, doh = q_ref[:, sl], k_ref[:, sl], v_ref[:, sl], do_ref[:, sl]
            dscb = (_dot_nt(doh, vh) * mask_ref[0, h]).astype(BF16)
            dsctb = (_dot_nt(vh, doh) * mask_ref[1, h]).astype(BF16)
            sctb = (_dot_nt(kh, qh) * mask_ref[1, h]).astype(BF16)
            stb = st_ref[0, h]
            dst = dstate_scr[h]
            dstb = dst.astype(BF16)
            qdb = (qh.astype(F32) * qd_ref[:, sl]).astype(BF16)
            kdb = (kh.astype(F32) * kd_ref[:, sl]).astype(BF16)
            dq = _dot(dscb, kh) + _dot_nt(doh, stb) * qd_ref[:, sl]
            dk = _dot(dsctb, qh) + _dot_nt(vh, dstb) * kd_ref[:, sl]
            dv = _dot(sctb, doh) + _dot(kdb, dstb)
            dstate_scr[h] = dst * cdec[h] + _dot_tn(qdb, doh)
            lo = h * HEAD_DIM
            dproj_ref[:, lo:lo + HEAD_DIM] = _rope_t(dq, cosf_t, sinf_t).astype(BF16)
            dproj_ref[:, RET_W + lo:RET_W + lo + HEAD_DIM] = _rope_t(dk * K_SCALE, cosf_t, sinf_t).astype(BF16)
            dproj_ref[:, 2 * RET_W + lo:2 * RET_W + lo + HEAD_DIM] = dv.astype(BF16)
        dproj_ref[:, 3 * RET_W:IN_W] = dgp_ref[...]

    tile = lambda w: pl.BlockSpec((SUPER, w), lambda i: (ns - 1 - i, 0))
    return _call(
        body, name="retention_bwd", grid=(ns,),
        in_specs=[tile(RET_W), tile(RET_W), tile(RET_W), tile(RET_W), tile(2 * RET_W),
                  pl.BlockSpec((1, HEADS, HEAD_DIM, HEAD_DIM), lambda i: (ns - 1 - i, 0, 0, 0)),
                  _whole(), _whole(), _whole(), tile(HEAD_DIM), tile(HEAD_DIM)],
        out_specs=[tile(IN_W)],
        out_shape=[jax.ShapeDtypeStruct((s, IN_W), BF16)],
        scratch_shapes=[pltpu.VMEM((HEADS, HEAD_DIM, HEAD_DIM), F32)],
        sem=("arbitrary",), operands=(q, k, v, dret, dgp, states, mask, qd, kd, cosf, sinf), riders=riders,
        after=after,
    )


def _dx(dz1, dproj, win4, ts, riders=(), after=()):
    s = dz1.shape[0]

    def body(dz1_ref, dp_ref, w_ref, dx_ref):
        acc = ALPHA * dz1_ref[...]
        for j in range(N_SHARD):
            acc = acc + _dot_nt(dp_ref[:, j * IN_SH:(j + 1) * IN_SH], w_ref[j])
        dx_ref[...] = acc

    tile = lambda w: pl.BlockSpec((ts, w), lambda i: (i, 0))
    return _call(
        body, name="dx", grid=(s // ts,),
        in_specs=[tile(D_MODEL), tile(IN_W), _whole()],
        out_specs=[tile(D_MODEL)],
        out_shape=[jax.ShapeDtypeStruct((s, D_MODEL), F32)],
        sem=("arbitrary",), operands=(dz1, dproj, win4), riders=riders, after=after,
    )


def _wgrad(a, b, tm, tn, name, stacked, m_outer, riders=(), after=()):
    s, m = a.shape
    n = b.shape[1]

    def body(a_ref, b_ref, o32_ref, o16_ref):
        res = _dot_tn(a_ref[...], b_ref[...])
        o32_ref[...] = res.reshape(o32_ref.shape)
        o16_ref[...] = res.astype(BF16).reshape(o16_ref.shape)

    if m_outer:
        grid, blocks = (m // tm, n // tn), (lambda g0, g1: (g0, g1))
    else:
        grid, blocks = (n // tn, m // tm), (lambda g0, g1: (g1, g0))
    if stacked:
        shape = (n // tn, m, tn)
        ospec = pl.BlockSpec((1, tm, tn), lambda g0, g1: (blocks(g0, g1)[1], blocks(g0, g1)[0], 0))
    else:
        shape = (m, n)
        ospec = pl.BlockSpec((tm, tn), lambda g0, g1: blocks(g0, g1))
    return _call(
        body, name=name, grid=grid,
        in_specs=[pl.BlockSpec((s, tm), lambda g0, g1: (0, blocks(g0, g1)[0])),
                  pl.BlockSpec((s, tn), lambda g0, g1: (0, blocks(g0, g1)[1]))],
        out_specs=[ospec, ospec],
        out_shape=[jax.ShapeDtypeStruct(shape, F32), jax.ShapeDtypeStruct(shape, BF16)],
        sem=("arbitrary", "arbitrary"), operands=(a, b), riders=riders, after=after,
    )


def _wgrad_send(pos, a, b, tn, name, barrier_id, after=()):
    s, m = a.shape
    n = b.shape[1]
    nb, hm = n // tn, m // 2

    def body(pos_ref, a_theirs_ref, a_mine_ref, b_ref, *refs):
        del pos_ref
        o32_ref, land_ref, send_scr, send_sems, recv_sems = refs[len(after):]
        j = pl.program_id(0)
        x, y, c = _mesh_pos()

        @pl.when(j == 0)
        def _():
            _shake_hands("sibling")

        def copy(blk):
            return pltpu.make_async_remote_copy(
                src_ref=send_scr.at[blk], dst_ref=land_ref.at[blk], send_sem=send_sems.at[blk],
                recv_sem=recv_sems.at[blk], device_id=(x, y, 1 - c), device_id_type=MESH)

        rhs = b_ref[...]
        theirs = _dot_tn(a_theirs_ref[...], rhs)
        o32_ref[0, pl.ds(pl.multiple_of((1 - c) * hm, 16), hm), :] = theirs
        send_scr[j] = theirs.astype(BF16)
        copy(j).start()
        o32_ref[0, pl.ds(pl.multiple_of(c * hm, 16), hm), :] = _dot_tn(a_mine_ref[...], rhs)

        @pl.when(j == nb - 1)
        def _():
            for blk in range(nb):
                copy(blk).wait()

    return pl.pallas_call(
        body, name=name,
        grid_spec=pltpu.PrefetchScalarGridSpec(
            num_scalar_prefetch=1, grid=(nb,),
            in_specs=[pl.BlockSpec((s, hm), lambda j, pos_ref: (0, 1 - pos_ref[0])),
                      pl.BlockSpec((s, hm), lambda j, pos_ref: (0, pos_ref[0])),
                      pl.BlockSpec((s, tn), lambda j, pos_ref: (0, j))] + [_whole()] * len(after),
            out_specs=[pl.BlockSpec((1, m, tn), lambda j, pos_ref: (j, 0, 0)), HBM_SPEC],
            scratch_shapes=[pltpu.VMEM((nb, hm, tn), BF16), pltpu.SemaphoreType.DMA((nb,)),
                            pltpu.SemaphoreType.DMA((nb,))]),
        out_shape=[jax.ShapeDtypeStruct((nb, m, tn), F32), jax.ShapeDtypeStruct((nb, hm, tn), BF16)],
        compiler_params=pltpu.CompilerParams(dimension_semantics=("arbitrary",), vmem_limit_bytes=VMEM_LIMIT,
                                             collective_id=barrier_id),
    )(pos, a, a, b, *after)


class _NoComm:
    def __init__(self, win4, wout, wup4, wdown):
        self.weights = dict(w_in=win4, w_out=wout, w_up=wup4, w_down=wdown)
        self.grads = {}

    def weight(self, name):
        return self.weights[name]

    def riders(self, call):
        return ()

    def after(self, call):
        return ()

    def landed(self, call, results, outs):
        pass

    def small_gradients(self, small, packed):
        pass

    def gradient(self, name, g32, g16):
        self.grads[name] = (g32, g16)

    def wgrad_in(self, xb, dproj):
        (g32, g16), _ = _wgrad(xb, dproj, D_MODEL, IN_SH, "wgrad_in", True, True)
        self.gradient("w_in", g32, g16)


def _local_step(x, target, cw, cb, wpool_b, pscale, g1, b1, g2, b2, comm):
    s = x.shape[0]
    ts_a = min(512, s)
    ts_f = min(256, s)
    mask, qd, kd = _decay_tables()
    cosf, sinf = _rope_tables(s)

    def run(call, fn, *args):
        outs, res = fn(*args, riders=comm.riders(call), after=comm.after(call))
        comm.landed(call, res, outs)
        return outs

    xb, q, k, v, g, pooled, cat = run("proj_pool", _proj_pool, x, comm.weight("w_in"), cosf, sinf, wpool_b,
                                      pscale, ts_a)
    ret, cat, states = run("retention_fwd", _retention_fwd, q, k, v, g, cat, mask, qd, kd)
    wout = comm.weight("w_out")
    xhat1, rstd1, h1b = run("outproj_ln1", _outproj_ln1, x, cat, wout, g1, b1, ts_a)
    wup4, wdown = comm.weight("w_up"), comm.weight("w_down")
    ub, act, sd, dz2, dz2b, loss, dg2, db2 = _ffn_fwd_loss(xhat1, h1b, target, wup4, wdown, cw, cb, g1, b1, g2, b2,
                                                           ts_f)

    dub, dz1, dz1b, dg1, db1, dcw, dcb = _ffn_bwd(dz2, dz2b, ub, sd, xhat1, rstd1, wup4, wdown, cw, g1, ts_f)
    half = D_MODEL // 2
    comm.gradient("w_up", *run("wgrad_up", _wgrad, h1b, dub, half, UP_SH, "wgrad_up", True, False))
    comm.gradient("w_out", *run("wgrad_out", _wgrad, cat, dz1b, D_MODEL, half, "wgrad_out", False, True))
    comm.gradient("w_down", *run("wgrad_down", _wgrad, act, dz2b, D_FF // 2, half, "wgrad_down", False, True))
    dret, dgp, dwp, dps, packed = run("mix_bwd", _mix_bwd, dz1b, pooled, ret, g, wout, wpool_b, pscale, loss,
                                      [dcb, dg1, db1, dg2, db2], ts_a)
    small = dict(w_pool=dwp, pool_scale=dps, ln1_g=dg1, ln1_b=db1, conv_w=dcw, conv_b=dcb,
                 ln2_g=dg2, ln2_b=db2)
    comm.small_gradients(small, packed)
    mask_both = jnp.stack([mask, jnp.swapaxes(mask, 1, 2)])
    dproj, = run("retention_bwd", _retention_bwd, q, k, v, dret, dgp, states, mask_both, qd, kd, cosf, sinf)
    comm.wgrad_in(xb, dproj)
    (grad_x,), _ = _dx(dz1, dproj, comm.weight("w_in"), ts_a, after=comm.after("dx"))
    return loss, grad_x, small


CAST_ROWS = 64
SHARD_SHAPES = ((D_MODEL, IN_SH), (OUT_SH, D_MODEL), (D_MODEL, UP_SH), (DOWN_SH, D_MODEL))
N_BIG = len(SHARD_SHAPES)
CW_SHARD = (3, 1, DOWN_SH)


def _mesh_pos():
    return lax.axis_index("x"), lax.axis_index("y"), lax.axis_index("c")


def _other_chips(x, y):
    return [(1 - x, y), (x, 1 - y), (1 - x, 1 - y)]


def _shake_hands(peers):
    x, y, c = _mesh_pos()
    others = [(x, y, 1 - c)] if peers in ("sibling", "both", "all") else []
    if peers in ("chips", "both", "all"):
        others += [(chip[0], chip[1], c) for chip in _other_chips(x, y)]
    if peers == "all":
        others += [(chip[0], chip[1], 1 - c) for chip in _other_chips(x, y)]
    barrier = pltpu.get_barrier_semaphore()
    for peer in others:
        pl.semaphore_signal(barrier, inc=1, device_id=peer, device_id_type=MESH)
    pl.semaphore_wait(barrier, len(others))


def _half_rows(w, which):
    hr = SHARD_SHAPES[w][0] // 2
    return pl.ds(pl.multiple_of(which * hr, 16), hr)


def _gather_weights(shards, cw_shard, wpool, full):
    def body(*refs):
        in_refs = refs[:N_BIG]
        cw_ref, wpool_ref = refs[N_BIG:N_BIG + 2]
        out_refs = refs[N_BIG + 2:2 * N_BIG + 2]
        cwo_ref, wpool_b_ref = refs[2 * N_BIG + 2:2 * N_BIG + 4]
        stage = refs[2 * N_BIG + 4:3 * N_BIG + 4]
        raw = refs[3 * N_BIG + 4:4 * N_BIG + 4 - len(full)]
        send_sems, recv_sems, fsend_sems, frecv_sems, cw_send, cw_recv, local_sems, load_sems = \
            refs[4 * N_BIG + 4 - len(full):]
        x, y, c = _mesh_pos()
        j0 = 2 * x + y
        chips = _other_chips(x, y)

        fetched = [w for w in range(N_BIG) if w not in full]
        f32 = {w: in_refs[w] for w in full}
        loads = []
        for n, w in enumerate(fetched):
            f32[w] = raw[n]
            loads.append(pltpu.make_async_copy(in_refs[w], raw[n], load_sems.at[n]))
            loads[-1].start()

        def cast_to_stage(w):
            def cast(i, carry):
                rows = pl.ds(pl.multiple_of(i * CAST_ROWS, CAST_ROWS), CAST_ROWS)
                stage[w][rows, :] = f32[w][rows, :].astype(BF16)
                return carry
            lax.fori_loop(0, SHARD_SHAPES[w][0] // CAST_ROWS, cast, 0)

        for w in full:
            cast_to_stage(w)

        jx, jy, jd = 2 * (1 - x) + y, 2 * x + (1 - y), 2 * (1 - x) + (1 - y)
        neighbours = [((1 - x, y, c), jx), ((x, 1 - y, c), jy)]
        passed = jnp.where(c == 0, jx, jy)
        pass_to = (jnp.where(c == 0, x, 1 - x), jnp.where(c == 0, 1 - y, y), c)

        def nbr(w, k, block):
            return pltpu.make_async_remote_copy(
                src_ref=stage[w].at[_half_rows(w, c), :], dst_ref=out_refs[w].at[block, _half_rows(w, c), :],
                send_sem=send_sems.at[w, k], recv_sem=recv_sems.at[w, k],
                device_id=neighbours[k][0], device_id_type=MESH)

        def relay(w, block):
            return pltpu.make_async_remote_copy(
                src_ref=out_refs[w].at[passed, _half_rows(w, c), :],
                dst_ref=out_refs[w].at[block, _half_rows(w, c), :],
                send_sem=send_sems.at[w, 2], recv_sem=recv_sems.at[w, 2],
                device_id=pass_to, device_id_type=MESH)

        def d2d(w, k, block, half):
            return pltpu.make_async_remote_copy(
                src_ref=out_refs[w].at[block, _half_rows(w, half), :],
                dst_ref=out_refs[w].at[block, _half_rows(w, half), :],
                send_sem=fsend_sems.at[w, k], recv_sem=frecv_sems.at[w, k],
                device_id=(x, y, 1 - c), device_id_type=MESH)

        def conv(k, block):
            chip = chips[k]
            return pltpu.make_async_remote_copy(
                src_ref=cw_ref, dst_ref=cwo_ref.at[block], send_sem=cw_send.at[k], recv_sem=cw_recv.at[k],
                device_id=(chip[0], chip[1], c), device_id_type=MESH)

        sent = [nbr(w, k, j0) for w in full for k in range(2)] + [conv(k, j0) for k in range(3)]
        for cp in sent:
            cp.start()
        for n, w in enumerate(fetched):
            loads[n].wait()
            cast_to_stage(w)
        local = [pltpu.make_async_copy(stage[w], out_refs[w].at[j0], local_sems.at[w]) for w in range(N_BIG)]
        local.append(pltpu.make_async_copy(cw_ref, cwo_ref.at[j0], local_sems.at[N_BIG]))
        for cp in local:
            cp.start()
        wpool_b_ref[...] = wpool_ref[...].astype(BF16)
        for w in full:
            for k, (_, block) in enumerate(neighbours):
                nbr(w, k, block).wait_recv()
            later = [relay(w, passed)] + [d2d(w, k, block, c) for k, (_, block) in enumerate(neighbours)]
            for cp in later:
                cp.start()
            sent += later
        for w in full:
            relay(w, jd).wait_recv()
            fw = d2d(w, 2, jd, c)
            fw.start()
            sent.append(fw)
        for w in full:
            for k, block in enumerate([jx, jy, jd]):
                d2d(w, k, block, 1 - c).wait_recv()
        for k, chip in enumerate(chips):
            conv(k, 2 * chip[0] + chip[1]).wait_recv()
        for cp in sent:
            cp.wait_send()
        for cp in local:
            cp.wait()

    out_shape = [jax.ShapeDtypeStruct((N_SHARD,) + shp, BF16) for shp in SHARD_SHAPES]
    out_shape.append(jax.ShapeDtypeStruct((N_SHARD,) + CW_SHARD, F32))
    out_shape.append(jax.ShapeDtypeStruct(wpool.shape, BF16))
    return pl.pallas_call(
        body, name="gather_weights",
        in_specs=[_whole() if w in full else HBM_SPEC for w in range(N_BIG)] + [_whole()] * 2,
        out_specs=[HBM_SPEC] * (N_BIG + 1) + [_whole()],
        out_shape=out_shape,
        scratch_shapes=[pltpu.VMEM(shp, BF16) for shp in SHARD_SHAPES]
        + [pltpu.VMEM(shp, F32) for w, shp in enumerate(SHARD_SHAPES) if w not in full] + [
            pltpu.SemaphoreType.DMA((N_BIG, 3)), pltpu.SemaphoreType.DMA((N_BIG, 3)),
            pltpu.SemaphoreType.DMA((N_BIG, 3)), pltpu.SemaphoreType.DMA((N_BIG, 3)),
            pltpu.SemaphoreType.DMA((3,)), pltpu.SemaphoreType.DMA((3,)),
            pltpu.SemaphoreType.DMA((N_BIG + 1,)), pltpu.SemaphoreType.DMA((N_BIG - len(full),))],
        compiler_params=pltpu.CompilerParams(vmem_limit_bytes=VMEM_LIMIT),
    )(*shards, cw_shard, wpool)


def _gather_rider(arrays, ops, handshake=None):
    ws = sorted(arrays)

    def make(inplace, srcs, lands, send_sems, recv_sems):
        del srcs, lands
        x, y, c = _mesh_pos()
        j0, jx, jy, jd = 2 * x + y, 2 * (1 - x) + y, 2 * x + (1 - y), 2 * (1 - x) + (1 - y)
        x_nbr, y_nbr, sibling = (1 - x, y, c), (x, 1 - y, c), (x, y, 1 - c)
        starts, waits = [], []
        for n, (kind, w, (r0, nr)) in enumerate(ops):
            ref = inplace[ws.index(w)]
            hr = SHARD_SHAPES[w][0] // 2
            rows = lambda core: pl.ds(pl.multiple_of(core * hr + r0, 16), nr)
            mine, theirs = rows(c), rows(1 - c)
            if kind == "ici":
                moves = [(ref.at[j0, mine, :], x_nbr, ref.at[jx, mine, :]),
                         (ref.at[j0, mine, :], y_nbr, ref.at[jy, mine, :]),
                         (ref.at[j0, mine, :], (1 - x, 1 - y, c), ref.at[jd, mine, :])]
            elif kind == "nbr":
                moves = [(ref.at[j0, mine, :], x_nbr, ref.at[jx, mine, :]),
                         (ref.at[j0, mine, :], y_nbr, ref.at[jy, mine, :])]
            elif kind == "relay":
                passed = jnp.where(c == 0, jx, jy)
                to = (jnp.where(c == 0, x, 1 - x), jnp.where(c == 0, 1 - y, y), c)
                moves = [(ref.at[passed, mine, :], to, ref.at[jd, mine, :])]
            else:
                blocks = dict(d2d=[jx, jy, jd], d2d_nbr=[jx, jy], d2d_diag=[jd])[kind]
                moves = [(ref.at[b, mine, :], sibling, ref.at[b, theirs, :]) for b in blocks]
            for k, (src, to, landing) in enumerate(moves):
                sems = dict(send_sem=send_sems.at[3 * n + k], recv_sem=recv_sems.at[3 * n + k],
                            device_id=to, device_id_type=MESH)
                send = pltpu.make_async_remote_copy(src_ref=src, dst_ref=src, **sems)
                arrival = pltpu.make_async_remote_copy(src_ref=src, dst_ref=landing, **sems)
                starts.append(send)
                waits += [arrival.wait_recv, send.wait_send]
        return starts, waits

    return _Rider([arrays[w] for w in ws], [], [], 3 * len(ops), make, handshake)


def _whole_half(w):
    return (0, SHARD_SHAPES[w][0] // 2)


def _pair_rider(ws, g16s):
    def make(inplace, srcs, lands, send_sems, recv_sems):
        del inplace
        x, y, c = _mesh_pos()
        copies = [pltpu.make_async_remote_copy(
            src_ref=srcs[i].at[:, _half_rows(w, 1 - c), :], dst_ref=lands[i],
            send_sem=send_sems.at[i], recv_sem=recv_sems.at[i], device_id=(x, y, 1 - c), device_id_type=MESH)
            for i, w in enumerate(ws)]
        return copies, [cp.wait for cp in copies]

    lands = [jax.ShapeDtypeStruct((N_SHARD, SHARD_SHAPES[w][0] // 2, SHARD_SHAPES[w][1]), BF16) for w in ws]
    return _Rider([], g16s, lands, len(ws), make)


def _chip_rider(ws, p16s):
    def make(inplace, srcs, lands, send_sems, recv_sems):
        del inplace
        x, y, c = _mesh_pos()
        copies = []
        for i in range(len(ws)):
            for k, chip in enumerate(_other_chips(x, y)):
                copies.append(pltpu.make_async_remote_copy(
                    src_ref=srcs[i].at[2 * chip[0] + chip[1]], dst_ref=lands[i].at[k],
                    send_sem=send_sems.at[3 * i + k], recv_sem=recv_sems.at[3 * i + k],
                    device_id=(chip[0], chip[1], c), device_id_type=MESH))
        return copies, [cp.wait for cp in copies]

    lands = [jax.ShapeDtypeStruct((3, SHARD_SHAPES[w][0] // 2, SHARD_SHAPES[w][1]), BF16) for w in ws]
    return _Rider([], p16s, lands, 3 * len(ws), make)


def _final_rider(halves):
    def make(inplace, srcs, lands, send_sems, recv_sems):
        del inplace
        x, y, c = _mesh_pos()
        copies = [pltpu.make_async_remote_copy(
            src_ref=srcs[i], dst_ref=lands[i], send_sem=send_sems.at[i], recv_sem=recv_sems.at[i],
            device_id=(x, y, 1 - c), device_id_type=MESH) for i in range(len(halves))]
        return copies, [cp.wait for cp in copies]

    return _Rider([], halves, [jax.ShapeDtypeStruct(h.shape, h.dtype) for h in halves], len(halves), make)


N_DEV = 2 * N_SHARD


def _device_number(x, y, c):
    return 2 * (2 * x + y) + c


def _small_all_rider(own):
    n = len(own)

    def make(inplace, srcs, lands, send_sems, recv_sems):
        del inplace
        x, y, c = _mesh_pos()
        peers = [(x, y, 1 - c)] + [(chip[0], chip[1], core) for chip in _other_chips(x, y) for core in (c, 1 - c)]
        copies = []
        for i in range(n):
            for k, peer in enumerate(peers):
                copies.append(pltpu.make_async_remote_copy(
                    src_ref=srcs[i], dst_ref=lands[i].at[_device_number(x, y, c)],
                    send_sem=send_sems.at[(N_DEV - 1) * i + k], recv_sem=recv_sems.at[(N_DEV - 1) * i + k],
                    device_id=peer, device_id_type=MESH))
        return copies, [cp.wait for cp in copies]

    lands = [jax.ShapeDtypeStruct((N_DEV,) + a.shape, a.dtype) for a in own]
    return _Rider([], own, lands, (N_DEV - 1) * n, make)


def _comm_only(name, riders):
    _, res = _call(lambda: None, name=name, grid=(), in_specs=[], out_specs=[], out_shape=[], operands=(),
                   riders=riders)
    return res


class _SemList:
    def __init__(self, refs):
        self.at = list(refs)


def _merged_rider(riders):
    srcs = [a for r in riders for a in r.srcs]
    lands = [a for r in riders for a in r.lands]

    def make(inplace, src_refs, land_refs, send_sems, recv_sems):
        starts, waits = [], []
        s0 = l0 = c0 = 0
        for r in riders:
            part = r.make(inplace, src_refs[s0:s0 + len(r.srcs)], land_refs[l0:l0 + len(r.lands)],
                          _SemList(send_sems.at[c0:c0 + r.n_copies]), _SemList(recv_sems.at[c0:c0 + r.n_copies]))
            starts += part[0]
            waits += part[1]
            s0, l0, c0 = s0 + len(r.srcs), l0 + len(r.lands), c0 + r.n_copies
        return starts, waits

    return _Rider([], srcs, lands, sum(r.n_copies for r in riders), make)


def _split_start(name, rider, handshake=None):
    assert not rider.inplace
    ns, nl, n = len(rider.srcs), len(rider.lands), rider.n_copies
    barrier_id, peers = handshake if handshake is not None else (None, None)

    def body(*refs):
        if handshake is not None:
            _shake_hands(peers)
        srcs, lands = refs[:ns], refs[ns:ns + nl]
        sems = refs[ns + nl:ns + nl + 2 * n]
        token = refs[-1]
        starts, _ = rider.make([], srcs, lands, _SemList(sems[:n]), _SemList(sems[n:]))
        for cp in starts:
            cp.start()
        token[...] = jnp.zeros_like(token)

    buffers = [pltpu.with_memory_space_constraint(a, pltpu.HBM) for a in rider.srcs]
    buffers += [pltpu.with_memory_space_constraint(lax.empty(s.shape, s.dtype), pltpu.HBM) for s in rider.lands]
    hbm = pl.BlockSpec(memory_space=pltpu.HBM)
    sem = pl.BlockSpec(memory_space=pltpu.SEMAPHORE)
    outs = pl.pallas_call(
        body, name=name,
        out_shape=tuple([pltpu.SemaphoreType.DMA(())] * (2 * n) + [pltpu.HBM(b.shape, b.dtype) for b in buffers]
                        + [jax.ShapeDtypeStruct((8, 128), F32)]),
        in_specs=[hbm] * (ns + nl),
        out_specs=tuple([sem] * (2 * n) + [hbm] * (ns + nl) + [_whole()]),
        input_output_aliases={i: 2 * n + i for i in range(ns + nl)},
        compiler_params=pltpu.CompilerParams(has_side_effects=pltpu.SideEffectType.DATAFLOW_SIDE_EFFECTING,
                                             collective_id=barrier_id),
    )(*buffers)
    return (rider, outs[:2 * n], outs[2 * n:2 * n + ns + nl]), outs[-1]


def _split_parts(state, riders):
    merged, sems, buffers = state
    n, ns = merged.n_copies, len(merged.srcs)
    parts, s0, l0, c0 = [], 0, 0, 0
    for r in riders:
        parts.append((r, list(sems[c0:c0 + r.n_copies]) + list(sems[n + c0:n + c0 + r.n_copies]),
                      list(buffers[s0:s0 + len(r.srcs)]) + list(buffers[ns + l0:ns + l0 + len(r.lands)])))
        s0, l0, c0 = s0 + len(r.srcs), l0 + len(r.lands), c0 + r.n_copies
    return parts


def _split_wait(name, state, after):
    rider, sems, buffers = state
    ns, nl, n = len(rider.srcs), len(rider.lands), rider.n_copies

    def body(*refs):
        srcs, lands = refs[:ns], refs[ns:ns + nl]
        sem_refs = refs[ns + nl:ns + nl + 2 * n]
        _, waits = rider.make([], srcs, lands, _SemList(sem_refs[:n]), _SemList(sem_refs[n:]))
        for wait in waits:
            wait()

    hbm = pl.BlockSpec(memory_space=pltpu.HBM)
    sem = pl.BlockSpec(memory_space=pltpu.SEMAPHORE)
    outs = pl.pallas_call(
        body, name=name,
        out_shape=tuple(pltpu.HBM(b.shape, b.dtype) for b in buffers),
        in_specs=[hbm] * (ns + nl) + [sem] * (2 * n) + [HBM_SPEC],
        out_specs=tuple([hbm] * (ns + nl)),
        input_output_aliases={i: i for i in range(ns + nl)},
        compiler_params=pltpu.CompilerParams(has_side_effects=pltpu.SideEffectType.DATAFLOW_SIDE_EFFECTING),
    )(*buffers, *sems, after)
    return list(outs[:ns]), list(outs[ns:])


def _pair_sum(pos, ws, g32s, recvs):
    n = len(ws)

    def body(pos_ref, *refs):
        del pos_ref
        g_refs, r_refs = refs[:n], refs[n:2 * n]
        p32_refs, p16_refs = refs[2 * n:3 * n], refs[3 * n:]
        x, y, _ = _mesh_pos()
        for i in range(n):
            tot = g_refs[i][...] + r_refs[i][...].astype(F32)
            p16_refs[i][...] = tot.astype(BF16)

            @pl.when(pl.program_id(0) == 2 * x + y)
            def _(i=i, tot=tot):
                p32_refs[i][...] = tot

    halves = [(SHARD_SHAPES[w][0] // 2, SHARD_SHAPES[w][1]) for w in ws]
    own = [pl.BlockSpec((None, None) + h, lambda j, pos_ref: (j, pos_ref[0], 0, 0)) for h in halves]
    blk = [pl.BlockSpec((None,) + h, lambda j, pos_ref: (j, 0, 0)) for h in halves]
    mine = [pl.BlockSpec(h, lambda j, pos_ref: (0, 0)) for h in halves]
    g4 = [g.reshape((N_SHARD, 2) + h) for g, h in zip(g32s, halves)]
    outs = pl.pallas_call(
        body, name="pair_sum_" + "_".join(str(w) for w in ws),
        grid_spec=pltpu.PrefetchScalarGridSpec(
            num_scalar_prefetch=1, grid=(N_SHARD,), in_specs=own + blk, out_specs=mine + blk),
        out_shape=[jax.ShapeDtypeStruct(h, F32) for h in halves]
        + [jax.ShapeDtypeStruct((N_SHARD,) + h, BF16) for h in halves],
        compiler_params=_params(("arbitrary",)),
    )(pos, *g4, *recvs)
    return outs[:n], outs[n:]


def _chip_sum(p32s, recvs):
    parts = 2

    def body(*refs):
        p_refs, r_refs, f_refs = refs[:N_BIG], refs[N_BIG:2 * N_BIG], refs[2 * N_BIG:]
        for w in range(N_BIG):
            f_refs[w][...] = ((p_refs[w][...] + r_refs[w][0].astype(F32)) + r_refs[w][1].astype(F32)) \
                + r_refs[w][2].astype(F32)

    quarters = [(r // 2 // parts, cc) for r, cc in SHARD_SHAPES]
    own = [pl.BlockSpec(qt, lambda i: (i, 0)) for qt in quarters]
    rcv = [pl.BlockSpec((3,) + qt, lambda i: (0, i, 0)) for qt in quarters]
    out = [pl.BlockSpec(qt, lambda i: (i, 0)) for qt in quarters]
    return pl.pallas_call(
        body, name="chip_sum", grid=(parts,), in_specs=own + rcv, out_specs=out,
        out_shape=[jax.ShapeDtypeStruct((r // 2, cc), F32) for r, cc in SHARD_SHAPES],
        compiler_params=_params(("arbitrary",)),
    )(*p32s, *recvs)


def _adamw(w, g, m, v):
    m_new = ADAM_B1 * m + (1.0 - ADAM_B1) * g
    v_new = ADAM_B2 * v + (1.0 - ADAM_B2) * (g * g)
    m_hat = m_new / (1.0 - ADAM_B1 ** ADAM_STEP)
    v_hat = v_new / (1.0 - ADAM_B2 ** ADAM_STEP)
    delta = -ADAM_LR * (m_hat / (jnp.sqrt(v_hat) + ADAM_EPS) + ADAM_WD * w)
    return delta, m_new, v_new


def _adam_half(name, pos, grads, ws, ms, vs, into=None):
    nb = 4
    which = (lambda ref: ref[0]) if into is None else (lambda ref: 1 - ref[0])

    def body(which_ref, *refs):
        del which_ref
        groups = [refs[i * N_BIG:(i + 1) * N_BIG] for i in range(4)]
        g_refs, w_refs, m_refs, v_refs = groups
        go_refs, do_refs, mo_refs, vo_refs = [refs[len(refs) - (4 - i) * N_BIG:len(refs) - (3 - i) * N_BIG]
                                              for i in range(4)]
        for w in range(N_BIG):
            g = g_refs[w][...]
            delta, m_new, v_new = _adamw(w_refs[w][...], g, m_refs[w][...], v_refs[w][...])
            go_refs[w][...] = g
            do_refs[w][...] = delta
            mo_refs[w][...] = m_new
            vo_refs[w][...] = v_new

    blocks = [(r // 2 // nb, cc) for r, cc in SHARD_SHAPES]
    half = [pl.BlockSpec(b, lambda i, which_ref: (i, 0)) for b in blocks]
    full = [pl.BlockSpec((None,) + b, lambda i, which_ref: (0, which(which_ref) * nb + i, 0)) for b in blocks]
    shapes = [jax.ShapeDtypeStruct((1,) + shp, F32) for shp in SHARD_SHAPES]
    carried = [] if into is None else [a for kind in into for a in kind]
    first = 1 + 4 * N_BIG
    outs = pl.pallas_call(
        body, name=name,
        grid_spec=pltpu.PrefetchScalarGridSpec(
            num_scalar_prefetch=1, grid=(nb,), in_specs=half + full * 3 + [HBM_SPEC] * len(carried),
            out_specs=full * 4),
        out_shape=shapes * 4,
        input_output_aliases={first + i: i for i in range(len(carried))},
        compiler_params=_params(("arbitrary",)),
    )(pos, *grads, *ws, *ms, *vs, *carried)
    return [outs[i * N_BIG:(i + 1) * N_BIG] for i in range(4)]


SMALL_ROWS = 8
ROW_CONV_B, ROW_POOL_SCALE, ROW_LN1_G, ROW_LN1_B, ROW_LN2_G, ROW_LN2_B, ROW_LOSS = range(7)
SMALL_VECS = ((ROW_CONV_B, D_FF), (ROW_POOL_SCALE, POOL_W), (ROW_LN1_G, D_MODEL), (ROW_LN1_B, D_MODEL),
              (ROW_LN2_G, D_MODEL), (ROW_LN2_B, D_MODEL))


def _small_adam(all_a, all_b, all_c, own_a, own_b, own_c, wp, cwp, vec_ws, m_wp, m_cwp, vec_ms,
                v_wp, v_cwp, vec_vs):
    nv = len(SMALL_VECS)
    np_ = 2 + nv

    def body(*refs):
        all_a_ref, all_b_ref, all_c_ref, own_a_ref, own_b_ref, own_c_ref = refs[0:6]
        refs = refs[3:]
        w_all, m_all, v_all = (refs[3 + i * np_:3 + (i + 1) * np_] for i in range(3))
        loss_out = refs[3 + 3 * np_]
        outs = refs[4 + 3 * np_:]
        x, y, c = _mesh_pos()
        j0 = 2 * x + y
        me = _device_number(x, y, c)

        def total(sent, own):
            by_dev = [jnp.where(me == d, own, sent(d)) for d in range(N_DEV)]
            chips = [by_dev[2 * j] + by_dev[2 * j + 1] for j in range(N_SHARD)]
            return ((chips[0] + chips[1]) + chips[2]) + chips[3]

        tot_a = total(lambda d: all_a_ref[d], own_a_ref[...])
        tot_b = total(lambda d: all_b_ref[d], own_b_ref[...])
        tot_c = total(lambda d: all_c_ref[d, j0], own_c_ref[j0])
        loss_out[...] = tot_b[ROW_LOSS:ROW_LOSS + 1, 0:1]
        grads = [tot_a, tot_c] + [tot_b[row:row + 1, 0:n] for row, n in SMALL_VECS]
        for p in range(np_):
            for at, g in ([(j, tot_c[j:j + 1]) for j in range(3)] if p == 1 else [(Ellipsis, grads[p])]):
                delta, m_new, v_new = _adamw(w_all[p][at], g, m_all[p][at], v_all[p][at])
                outs[p][at] = g
                outs[np_ + p][at] = delta
                outs[2 * np_ + p][at] = m_new
                outs[3 * np_ + p][at] = v_new

    pshapes = [wp.shape, CW_SHARD] + [wv.shape for wv in vec_ws]
    out_shape = [jax.ShapeDtypeStruct((1, 1), F32)] + [jax.ShapeDtypeStruct(s, F32) for s in pshapes] * 4
    outs = pl.pallas_call(
        body, name="small_adam",
        in_specs=[_whole()] * (6 + 3 * np_), out_specs=[_whole()] * len(out_shape), out_shape=out_shape,
        compiler_params=pltpu.CompilerParams(vmem_limit_bytes=VMEM_LIMIT),
    )(all_a, all_b, all_c, own_a, own_b, own_c, wp, cwp, *vec_ws, m_wp, m_cwp, *vec_ms, v_wp, v_cwp, *vec_vs)
    return outs[0], [outs[1 + i * np_:1 + (i + 1) * np_] for i in range(4)]


def kernel(x, w_in, w_pool, pool_scale, w_out, ln1_g, ln1_b, w_up, conv_w, conv_b, w_down, ln2_g, ln2_b, loss_target, m_w_in, m_w_pool, m_pool_scale, m_w_out, m_ln1_g, m_ln1_b, m_w_up, m_conv_w, m_conv_b, m_w_down, m_ln2_g, m_ln2_b, v_w_in, v_w_pool, v_pool_scale, v_w_out, v_ln1_g, v_ln1_b, v_w_up, v_conv_w, v_conv_b, v_w_down, v_ln2_g, v_ln2_b):
    pos = lax.axis_index("c").astype(jnp.int32).reshape(1)
    order = ("w_in", "w_out", "w_up", "w_down")
    w_in_i, w_out_i, w_up_i, w_down_i = range(N_BIG)
    vec_names = ("conv_b", "pool_scale", "ln1_g", "ln1_b", "ln2_g", "ln2_b")

    taps_first = lambda a: jnp.transpose(a, (1, 0, 2))
    gathered = _gather_weights([w_in[0], w_out[0], w_up[0], w_down[0]], taps_first(conv_w), w_pool[0], (w_in_i,))
    cw_full = jnp.transpose(gathered[N_BIG].reshape(N_SHARD, 3, DOWN_SH), (1, 0, 2)).reshape(3, D_FF)
    up_a, up_b, up_c = (0, 176), (176, 176), (352, 160)
    assert up_c[0] + up_c[1] == SHARD_SHAPES[w_up_i][0] // 2

    class MeshComm:
        def __init__(self):
            self.w = {i: gathered[i] for i in range(N_BIG)}
            self.g32, self.g16, self.p32, self.p16, self.recv_b = {}, {}, {}, {}, {}
            self.up_complete = False
            self.tokens, self.chips = {}, []

        def weight(self, name):
            i = order.index(name)
            if name == "w_up" and not self.up_complete:
                (arrs, _), = _comm_only("gather_up_last", [_gather_rider(
                    {i: self.w[i]}, [("d2d_diag", i, up_b), ("d2d", i, up_c)], (12, "sibling"))])
                self.w[i], self.up_complete = arrs[0], True
            full = self.w[i]
            return full.reshape(-1, full.shape[-1]) if name in ("w_out", "w_down") else full

        def _gather(self, ws, ops, handshake):
            return _gather_rider({w: self.w[w] for w in ws}, ops, handshake), ("w", ws)

        def _pair(self, ws):
            return _pair_rider(ws, [self.g16[w] for w in ws]), ("recv_a", ws)

        def _chip(self, ws):
            return _chip_rider(ws, [self.p16[w] for w in ws]), ("recv_b", ws)

        def plan(self, call):
            out_all, down_all = _whole_half(w_out_i), _whole_half(w_down_i)
            if call == "proj_pool":
                return [self._gather([w_out_i, w_up_i, w_down_i],
                                     [("ici", w_out_i, out_all), ("nbr", w_down_i, down_all),
                                      ("nbr", w_up_i, up_a)], (9, "chips"))]
            if call == "retention_fwd":
                return [self._gather([w_out_i, w_up_i, w_down_i],
                                     [("d2d", w_out_i, out_all),
                                      ("relay", w_down_i, down_all), ("d2d_nbr", w_down_i, down_all),
                                      ("relay", w_up_i, up_a), ("d2d_nbr", w_up_i, up_a), ("nbr", w_up_i, up_b)],
                                     (10, "both"))]
            if call == "outproj_ln1":
                return [self._gather([w_up_i, w_down_i],
                                     [("d2d_diag", w_down_i, down_all), ("d2d_diag", w_up_i, up_a),
                                      ("relay", w_up_i, up_b), ("d2d_nbr", w_up_i, up_b), ("ici", w_up_i, up_c)],
                                     (11, "both"))]
            return []

        def after(self, call):
            return tuple(self.tokens.pop(call, ()))

        def riders(self, call):
            self.pending = self.plan(call)
            return [r for r, _ in self.pending]

        def _start(self, name, rider, before, handshake):
            state, token = _split_start(name, rider, handshake)
            self.tokens.setdefault(before, []).append(token)
            return state

        def _finish_pair(self, name, state, ws, after):
            _, lands = _split_wait(name, state, after)
            self._finish_sum(ws, lands)

        def landed(self, call, results, outs):
            for (_, (slot, ws)), (inplace, lands) in zip(self.pending, results):
                for w, arr in zip(ws, inplace if len(inplace) else lands):
                    getattr(self, slot)[w] = arr
            if call == "wgrad_out":
                self._finish_pair("pair_exchange_up_wait", self.pair_up, [w_up_i], outs[1])
                self.chips.append(([w_up_i], self._start(
                    "chip_exchange_up_start", self._chip([w_up_i])[0], "wgrad_down", (5, "chips"))))
            if call == "mix_bwd":
                ws = [w_out_i, w_down_i]
                self._finish_pair("pair_exchange_out_down_wait", self.pair_out_down, ws, outs[0])

        def small_gradients(self, small, packed):
            dcw4 = jnp.transpose(small["conv_w"].reshape(3, N_SHARD, DOWN_SH), (1, 0, 2))
            own = [small["w_pool"], packed, dcw4]
            ws = [w_out_i, w_down_i]
            parts = [self._chip(ws)[0], _small_all_rider(own)]
            chip, self.small_all = _split_parts(
                self._start("chip_out_down_small_all_start", _merged_rider(parts), "retention_bwd",
                            (7, "all")), parts)
            self.chips.append((ws, chip))

        def gradient(self, name, g32, g16):
            w = order.index(name)
            shape = (N_SHARD,) + SHARD_SHAPES[w]
            self.g32[w], self.g16[w] = g32.reshape(shape), g16.reshape(shape)
            if name == "w_up":
                self.pair_up = self._start("pair_exchange_up_start", self._pair([w])[0], "wgrad_out",
                                           (1, "sibling"))
            if name == "w_down":
                self.pair_out_down = self._start("pair_exchange_out_down_start",
                                                 self._pair([w_out_i, w_down_i])[0], "mix_bwd", (2, "sibling"))

        def wgrad_in(self, xb, dproj):
            w = w_in_i
            g32, landed = _wgrad_send(pos, xb, dproj, IN_SH, "wgrad_in", 4, after=self.after("wgrad_in"))
            self.g32[w] = g32
            self._finish_sum([w], [landed])
            self.chips.append(([w], self._start("chip_exchange_in_start", self._chip([w])[0], "dx", (8, "chips"))))

        def _finish_sum(self, ws, lands):
            p32s, p16s = _pair_sum(pos, ws, [self.g32[w] for w in ws], lands)
            for w, p32, p16 in zip(ws, p32s, p16s):
                self.p32[w], self.p16[w] = p32, p16

        def finish(self, after):
            for n, (ws, state) in enumerate(self.chips):
                _, lands = _split_wait("chip_exchange_wait_%d" % n, state, after)
                for w, arr in zip(ws, lands):
                    self.recv_b[w] = arr
            own, sent = _split_wait("small_all_wait", self.small_all, after)
            return list(sent) + list(own)

    comm = MeshComm()
    loss, grad_x, small = _local_step(x[0], loss_target[0], cw_full, conv_b, gathered[N_BIG + 1], pool_scale,
                                      ln1_g, ln1_b, ln2_g, ln2_b, comm)

    given = dict(w_pool=w_pool, pool_scale=pool_scale, ln1_g=ln1_g, ln1_b=ln1_b, conv_w=conv_w, conv_b=conv_b,
                 ln2_g=ln2_g, ln2_b=ln2_b)
    given_m = dict(w_pool=m_w_pool, pool_scale=m_pool_scale, ln1_g=m_ln1_g, ln1_b=m_ln1_b, conv_w=m_conv_w,
                   conv_b=m_conv_b, ln2_g=m_ln2_g, ln2_b=m_ln2_b)
    given_v = dict(w_pool=v_w_pool, pool_scale=v_pool_scale, ln1_g=v_ln1_g, ln1_b=v_ln1_b, conv_w=v_conv_w,
                   conv_b=v_conv_b, ln2_g=v_ln2_g, ln2_b=v_ln2_b)
    args = []
    for src in (given, given_m, given_v):
        args += [src["w_pool"][0], taps_first(src["conv_w"]), [src[n] for n in vec_names]]
    small_sums = comm.finish(grad_x)
    loss_tot, small_out = _small_adam(*small_sums, *args)
    every = range(N_BIG)
    mine = _chip_sum([comm.p32[w] for w in every], [comm.recv_b[w] for w in every])
    final_state, _ = _split_start("pair_exchange_f32_start", _final_rider(mine), (3, "sibling"))
    mine = final_state[2][:N_BIG]
    big = ([w_in, w_out, w_up, w_down], [m_w_in, m_w_out, m_w_up, m_w_down], [v_w_in, v_w_out, v_w_up, v_w_down])
    own_half = _adam_half("adam_own_half", pos, mine, *big)
    _, theirs = _split_wait("pair_exchange_f32_wait", final_state, own_half[0][0])
    big_out = _adam_half("adam_other_half", pos, theirs, *big, into=own_half)

    names = ("w_in", "w_pool", "pool_scale", "w_out", "ln1_g", "ln1_b", "w_up", "conv_w", "conv_b", "w_down",
             "ln2_g", "ln2_b")
    small_names = ("w_pool", "conv_w") + vec_names
    result = [loss_tot.reshape(()), grad_x[None]]
    for kind in range(4):
        for n in names:
            if n in order:
                result.append(big_out[kind][order.index(n)])
            else:
                val = small_out[kind][small_names.index(n)]
                if n == "conv_w":
                    val = taps_first(val)
                elif n == "w_pool":
                    val = val[None]
                result.append(val)
    return tuple(result)
```

```python
import functools

import numpy as np
import jax
import jax.numpy as jnp
from jax import lax
from jax.experimental import pallas as pl
from jax.experimental.pallas import tpu as pltpu

F32 = jnp.float32
BF16 = jnp.bfloat16

D_MODEL = 1024
HEADS = 4
HEAD_DIM = 128
RET_W = HEADS * HEAD_DIM
POOL_WINDOWS = (2, 4, 8, 16)
POOL_W = 512
IN_W = 4 * RET_W + POOL_W
D_FF = 2816
N_SHARD = 4
IN_SH = IN_W // N_SHARD
UP_SH = 2 * D_FF // N_SHARD
DOWN_SH = D_FF // N_SHARD
OUT_SH = D_MODEL // N_SHARD
ROPE_BASE = 10000.0
LN_EPS = 1e-5
RMS_EPS = 1e-6
ALPHA = 2.0 ** 0.25
K_SCALE = HEAD_DIM ** -0.5
SUPER = 256
CHUNK = 64
POOL_HALO = 16
CONV_HALO = 8
FFN_STRIP = 128
LN_ROWS = 32

ADAM_LR = 0.001
ADAM_B1 = 0.9
ADAM_B2 = 0.999
ADAM_EPS = 1e-08
ADAM_WD = 0.01
ADAM_STEP = 10

MESH = pl.DeviceIdType.MESH
VMEM_LIMIT = 56 * 1024 * 1024


def _dot(a, b):
    return jnp.dot(a, b, preferred_element_type=F32)


def _dot_nt(a, b):
    return lax.dot_general(a, b, (((1,), (1,)), ((), ())), preferred_element_type=F32)


def _dot_tn(a, b):
    return lax.dot_general(a, b, (((0,), (0,)), ((), ())), preferred_element_type=F32)


def _sigmoid(x):
    return 1.0 / (1.0 + jnp.exp(-x))


def _params(sem):
    return pltpu.CompilerParams(dimension_semantics=sem, vmem_limit_bytes=VMEM_LIMIT)


def _whole():
    return pl.BlockSpec(memory_space=pltpu.VMEM)


HBM_SPEC = pl.BlockSpec(memory_space=pl.ANY)


class _Rider:
    def __init__(self, inplace, srcs, lands, n_copies, make, handshake=None):
        self.inplace, self.srcs, self.lands, self.n_copies, self.make = list(inplace), list(srcs), list(lands), n_copies, make
        self.handshake = handshake


def _call(body, *, name, grid, in_specs, out_specs, out_shape, operands, scratch_shapes=(), sem=(),
          aliases=None, riders=(), after=()):
    n_in, n_out, n_scr = len(in_specs), len(out_shape), len(scratch_shapes)
    in_specs, out_specs, out_shape = list(in_specs), list(out_specs), list(out_shape)
    operands, scratch_shapes, aliases = list(operands), list(scratch_shapes), dict(aliases or {})
    in_specs += [_whole()] * len(after)
    operands += list(after)
    shakes = [r.handshake for r in riders if r.handshake is not None]
    assert len(shakes) <= 1
    for r in riders:
        for a in r.inplace:
            aliases[len(in_specs)] = len(out_shape)
            in_specs.append(HBM_SPEC)
            operands.append(a)
            out_specs.append(HBM_SPEC)
            out_shape.append(jax.ShapeDtypeStruct(a.shape, a.dtype))
        for a in r.srcs:
            in_specs.append(HBM_SPEC)
            operands.append(a)
        for shp in r.lands:
            out_specs.append(HBM_SPEC)
            out_shape.append(shp)
        scratch_shapes += [pltpu.SemaphoreType.DMA((r.n_copies,)), pltpu.SemaphoreType.DMA((r.n_copies,))]

    def full(*refs):
        ins = refs[:n_in]
        at = n_in + len(after)
        r_srcs = []
        for r in riders:
            at += len(r.inplace)
            r_srcs.append(refs[at:at + len(r.srcs)])
            at += len(r.srcs)
        outs = refs[at:at + n_out]
        at += n_out
        r_outs = []
        for r in riders:
            r_outs.append((refs[at:at + len(r.inplace)], refs[at + len(r.inplace):at + len(r.inplace) + len(r.lands)]))
            at += len(r.inplace) + len(r.lands)
        scr = refs[at:at + n_scr]
        at += n_scr
        r_sems = [refs[at + 2 * i:at + 2 * i + 2] for i in range(len(riders))]

        def copies():
            return [r.make(r_outs[i][0], r_srcs[i], r_outs[i][1], r_sems[i][0], r_sems[i][1])
                    for i, r in enumerate(riders)]

        def start():
            if shakes:
                _shake_hands(shakes[0][1])
            for starts, _ in copies():
                for cp in starts:
                    cp.start()

        def finish():
            for _, waits in copies():
                for wait in waits:
                    wait()

        if riders and grid:
            first = functools.reduce(jnp.logical_and, [pl.program_id(d) == 0 for d in range(len(grid))])
            last = functools.reduce(jnp.logical_and, [pl.program_id(d) == grid[d] - 1 for d in range(len(grid))])
            pl.when(first)(start)
            body(*ins, *outs, *scr)
            pl.when(last)(finish)
        else:
            if riders:
                start()
            body(*ins, *outs, *scr)
            if riders:
                finish()

    barrier_id = shakes[0][0] if shakes else None
    params = pltpu.CompilerParams(vmem_limit_bytes=VMEM_LIMIT, collective_id=barrier_id,
                                  **(dict(dimension_semantics=sem) if grid else {}))
    res = pl.pallas_call(
        full, name=name, grid=grid, in_specs=in_specs, out_specs=out_specs, out_shape=out_shape,
        scratch_shapes=scratch_shapes, input_output_aliases=aliases, compiler_params=params,
    )(*operands)
    outs, at, rider_res = res[:n_out], n_out, []
    for r in riders:
        rider_res.append((res[at:at + len(r.inplace)], res[at + len(r.inplace):at + len(r.inplace) + len(r.lands)]))
        at += len(r.inplace) + len(r.lands)
    return list(outs), rider_res


def _gammas():
    return [1.0 - 2.0 ** (-5.0 - h) for h in range(HEADS)]


def _decay_tables():
    idx = np.arange(SUPER)
    dist = np.abs(idx[:, None] - idx[None, :]).astype(np.float64)
    visible = (idx[None, :] // CHUNK) <= (idx[:, None] // CHUNK)
    mask = np.stack([np.where(visible, g ** dist, 0.0) for g in _gammas()])
    qd = np.concatenate([np.repeat((g ** (idx + 1.0))[:, None], HEAD_DIM, 1) for g in _gammas()], 1)
    kd = np.concatenate([np.repeat((g ** (SUPER - 1.0 - idx))[:, None], HEAD_DIM, 1) for g in _gammas()], 1)
    return (jnp.asarray(mask, F32), jnp.asarray(qd, F32), jnp.asarray(kd, F32))


def _rope_tables(s):
    inv_freq = ROPE_BASE ** (-np.arange(0, HEAD_DIM, 2, dtype=np.float64) / HEAD_DIM)
    ang = np.arange(s, dtype=np.float64)[:, None] * inv_freq[None, :]
    cos, sin = np.cos(ang), np.sin(ang)
    return (jnp.asarray(np.concatenate([cos, cos], 1), F32),
            jnp.asarray(np.concatenate([-sin, sin], 1), F32))


def _rope(t, cosf, sinf):
    return t * cosf + pltpu.roll(t, HEAD_DIM // 2, 1) * sinf


def _rope_t(t, cosf, sinf):
    return t * cosf - pltpu.roll(t, HEAD_DIM // 2, 1) * sinf


def _layernorm_fwd(z):
    mu = jnp.mean(z, axis=-1, keepdims=True)
    zc = z - mu
    var = jnp.mean(zc * zc, axis=-1, keepdims=True)
    rstd = lax.rsqrt(var + LN_EPS)
    return zc * rstd, rstd


def _layernorm_bwd(dy, xhat, rstd, gain):
    dxh = dy * gain
    m1 = jnp.mean(dxh, axis=-1, keepdims=True)
    m2 = jnp.mean(dxh * xhat, axis=-1, keepdims=True)
    return rstd * (dxh - m1 - xhat * m2)


def _proj_pool(x, win4, cosf, sinf, wpool, pscale, ts, riders=(), after=()):
    s = x.shape[0]
    nt = s // ts

    def body(x_ref, w_ref, cos_ref, sin_ref, wp_ref, ps_ref,
             xb_ref, q_ref, k_ref, v_ref, g_ref, pooled_ref, cat_ref, proj_scr, pext_scr):
        i = pl.program_id(0)
        xb = x_ref[...].astype(BF16)
        xb_ref[...] = xb
        for j in range(N_SHARD):
            proj_scr[:, j * IN_SH:(j + 1) * IN_SH] = _dot(xb, w_ref[j])
        cosf_t = cos_ref[...]
        sinf_t = sin_ref[...]
        for h in range(HEADS):
            lo = h * HEAD_DIM
            q_ref[:, lo:lo + HEAD_DIM] = _rope(proj_scr[:, lo:lo + HEAD_DIM], cosf_t, sinf_t).astype(BF16)
            kk = _rope(proj_scr[:, RET_W + lo:RET_W + lo + HEAD_DIM], cosf_t, sinf_t) * K_SCALE
            k_ref[:, lo:lo + HEAD_DIM] = kk.astype(BF16)
        v_ref[...] = proj_scr[:, 2 * RET_W:3 * RET_W].astype(BF16)
        g_ref[...] = proj_scr[:, 3 * RET_W:4 * RET_W]

        @pl.when(i == 0)
        def _():
            pext_scr[0:POOL_HALO, :] = jnp.zeros((POOL_HALO, POOL_W), F32)

        pext_scr[POOL_HALO:POOL_HALO + ts, :] = proj_scr[:, 4 * RET_W:IN_W]
        pos = (i * ts + lax.broadcasted_iota(jnp.int32, (ts, 1), 0) + 1).astype(F32)
        for gi, w in enumerate(POOL_WINDOWS):
            lo = gi * HEAD_DIM
            ext = pext_scr[:, lo:lo + HEAD_DIM]
            acc = ext
            shift = 1
            while shift < w:
                acc = acc + pltpu.roll(acc, shift, 0)
                shift *= 2
            tok = ext[POOL_HALO:POOL_HALO + ts]
            pooled = acc[POOL_HALO:POOL_HALO + ts] / jnp.minimum(pos, float(w)) - tok
            pooled_b = pooled.astype(BF16)
            pooled_ref[:, lo:lo + HEAD_DIM] = pooled_b
            lin = _dot(pooled_b, wp_ref[gi])
            cat_ref[:, lo:lo + HEAD_DIM] = (lin * ps_ref[:, lo:lo + HEAD_DIM]).astype(BF16)
        pext_scr[0:POOL_HALO, :] = pext_scr[ts:ts + POOL_HALO, :]

    tile = lambda w: pl.BlockSpec((ts, w), lambda i: (i, 0))
    return _call(
        body, name="proj_pool", grid=(nt,),
        in_specs=[tile(D_MODEL), _whole(), tile(HEAD_DIM), tile(HEAD_DIM), _whole(), _whole()],
        out_specs=[tile(D_MODEL), tile(RET_W), tile(RET_W), tile(RET_W), tile(RET_W), tile(POOL_W),
                   pl.BlockSpec((ts, POOL_W), lambda i: (i, 1))],
        out_shape=[jax.ShapeDtypeStruct((s, D_MODEL), BF16), jax.ShapeDtypeStruct((s, RET_W), BF16),
                   jax.ShapeDtypeStruct((s, RET_W), BF16), jax.ShapeDtypeStruct((s, RET_W), BF16),
                   jax.ShapeDtypeStruct((s, RET_W), F32), jax.ShapeDtypeStruct((s, POOL_W), BF16),
                   jax.ShapeDtypeStruct((s, 2 * RET_W), BF16)],
        scratch_shapes=[pltpu.VMEM((ts, IN_W), F32), pltpu.VMEM((ts + POOL_HALO, POOL_W), F32)],
        sem=("arbitrary",), operands=(x, win4, cosf, sinf, wpool, pscale), riders=riders, after=after,
    )


def _retention_fwd(q, k, v, g, cat, mask, qd, kd, riders=(), after=()):
    s = q.shape[0]
    ns = s // SUPER
    cdec = [gm ** float(SUPER) for gm in _gammas()]

    def body(q_ref, k_ref, v_ref, g_ref, cat_in, mask_ref, qd_ref, kd_ref,
             ret_ref, cat_ref, st_ref, state_scr):
        del cat_in
        n = pl.program_id(0)

        @pl.when(n == 0)
        def _():
            state_scr[...] = jnp.zeros_like(state_scr)

        for h in range(HEADS):
            sl = slice(h * HEAD_DIM, (h + 1) * HEAD_DIM)
            qh, kh, vh = q_ref[:, sl], k_ref[:, sl], v_ref[:, sl]
            sc = _dot_nt(qh, kh) * mask_ref[h]
            st = state_scr[h]
            stb = st.astype(BF16)
            st_ref[0, h] = stb
            qdb = (qh.astype(F32) * qd_ref[:, sl]).astype(BF16)
            kdb = (kh.astype(F32) * kd_ref[:, sl]).astype(BF16)
            ret = _dot(sc.astype(BF16), vh) + _dot(qdb, stb)
            state_scr[h] = st * cdec[h] + _dot_tn(kdb, vh)
            ret_ref[:, sl] = ret
            r = lax.rsqrt(jnp.mean(ret * ret, axis=-1, keepdims=True) + RMS_EPS)
            gh = g_ref[:, sl]
            cat_ref[:, sl] = ((ret * r) * (gh * _sigmoid(gh))).astype(BF16)

    tile = pl.BlockSpec((SUPER, RET_W), lambda n: (n, 0))
    return _call(
        body, name="retention_fwd", grid=(ns,),
        in_specs=[tile, tile, tile, tile, HBM_SPEC, _whole(), _whole(), _whole()],
        out_specs=[tile, tile, pl.BlockSpec((1, HEADS, HEAD_DIM, HEAD_DIM), lambda n: (n, 0, 0, 0))],
        out_shape=[jax.ShapeDtypeStruct((s, RET_W), F32), jax.ShapeDtypeStruct((s, 2 * RET_W), BF16),
                   jax.ShapeDtypeStruct((ns, HEADS, HEAD_DIM, HEAD_DIM), BF16)],
        scratch_shapes=[pltpu.VMEM((HEADS, HEAD_DIM, HEAD_DIM), F32)],
        aliases={4: 1}, sem=("arbitrary",), operands=(q, k, v, g, cat, mask, qd, kd), riders=riders,
        after=after,
    )


def _outproj_ln1(x, cat, wout, g1, b1, ts, riders=(), after=()):
    s = x.shape[0]

    def body(x_ref, cat_ref, w_ref, g_ref, b_ref, xhat_ref, rstd_ref, h1b_ref):
        z = ALPHA * x_ref[...] + _dot(cat_ref[...], w_ref[...])
        xhat, rstd = _layernorm_fwd(z)
        xhat_ref[...] = xhat
        rstd_ref[...] = rstd
        h1b_ref[...] = (xhat * g_ref[...] + b_ref[...]).astype(BF16)

    tile = lambda w: pl.BlockSpec((ts, w), lambda i: (i, 0))
    return _call(
        body, name="outproj_ln1", grid=(s // ts,),
        in_specs=[tile(D_MODEL), tile(D_MODEL), _whole(), _whole(), _whole()],
        out_specs=[tile(D_MODEL), tile(1), tile(D_MODEL)],
        out_shape=[jax.ShapeDtypeStruct((s, D_MODEL), F32), jax.ShapeDtypeStruct((s, 1), F32),
                   jax.ShapeDtypeStruct((s, D_MODEL), BF16)],
        sem=("arbitrary",), operands=(x, cat, wout, g1, b1), riders=riders, after=after,
    )


def _ffn_fwd_loss(xhat1, h1b, target, wup4, wdown, cw, cb, g1, b1, g2, b2, ts):
    s = xhat1.shape[0]

    def body(xhat_ref, h1b_ref, tgt_ref, wup_ref, wdn_ref, cw_ref, cb_ref, g1_ref, b1_ref, g2_ref, b2_ref,
             ub_ref, act_ref, sd_ref, dz2_ref, dz2b_ref, loss_ref, dg2_ref, db2_ref, val_scr, gext_scr, ffn_scr):
        i = pl.program_id(0)

        @pl.when(i == 0)
        def _():
            gext_scr[0:CONV_HALO, :] = jnp.zeros((CONV_HALO, D_FF), F32)
            loss_ref[...] = jnp.zeros_like(loss_ref)
            dg2_ref[...] = jnp.zeros_like(dg2_ref)
            db2_ref[...] = jnp.zeros_like(db2_ref)

        for half in range(2):
            lo = half * UP_SH
            gext_scr[CONV_HALO:CONV_HALO + ts, lo:lo + UP_SH] = _dot(h1b_ref[...], wup_ref[2 + half])
            val_scr[:, lo:lo + UP_SH] = _dot(h1b_ref[...], wup_ref[half])
            for c0 in range(lo, lo + UP_SH, FFN_STRIP):
                cols = slice(c0, c0 + FFN_STRIP)
                ext = gext_scr[:, cols]
                gate = ext[CONV_HALO:]
                hc = cb_ref[:, cols] + ((pltpu.roll(ext, 2, 0)[CONV_HALO:] * cw_ref[0:1, cols]
                                         + pltpu.roll(ext, 1, 0)[CONV_HALO:] * cw_ref[1:2, cols])
                                        + gate * cw_ref[2:3, cols])
                val = val_scr[:, cols]
                sg = _sigmoid(hc)
                si = hc * sg
                act_ref[:, cols] = (si * val).astype(BF16)
                ub_ref[:, cols] = val.astype(BF16)
                ub_ref[:, D_FF + c0:D_FF + c0 + FFN_STRIP] = gate.astype(BF16)
                sd_ref[:, cols] = hc.astype(BF16)
            part = _dot(act_ref[:, lo:lo + UP_SH], wdn_ref[lo:lo + UP_SH, :])
            if half == 0:
                ffn_scr[...] = part
            else:
                ffn_scr[...] += part

        gext_scr[0:CONV_HALO, :] = gext_scr[ts:ts + CONV_HALO, :]

        loss_acc = jnp.zeros((1, 1), F32)
        dg2_acc = jnp.zeros((1, D_MODEL), F32)
        db2_acc = jnp.zeros((1, D_MODEL), F32)
        for r0 in range(0, ts, LN_ROWS):
            rows = slice(r0, r0 + LN_ROWS)
            h1 = xhat_ref[rows, :] * g1_ref[...] + b1_ref[...]
            xhat2, rstd2 = _layernorm_fwd(ALPHA * h1 + ffn_scr[rows, :])
            diff = (xhat2 * g2_ref[...] + b2_ref[...]) - tgt_ref[rows, :]
            row = jnp.mean(diff * diff, axis=-1, keepdims=True)
            loss_acc = loss_acc + 0.5 * jnp.sum(row, axis=0, keepdims=True)
            dy = diff * (1.0 / D_MODEL)
            dg2_acc = dg2_acc + jnp.sum(dy * xhat2, axis=0, keepdims=True)
            db2_acc = db2_acc + jnp.sum(dy, axis=0, keepdims=True)
            dz2 = _layernorm_bwd(dy, xhat2, rstd2, g2_ref[...])
            dz2_ref[rows, :] = dz2
            dz2b_ref[rows, :] = dz2.astype(BF16)
        loss_ref[...] += loss_acc
        dg2_ref[...] += dg2_acc
        db2_ref[...] += db2_acc

    tile = lambda w: pl.BlockSpec((ts, w), lambda i: (i, 0))
    acc = lambda w: pl.BlockSpec((1, w), lambda i: (0, 0))
    return pl.pallas_call(
        body, name="ffn_fwd_loss", grid=(s // ts,),
        in_specs=[tile(D_MODEL), tile(D_MODEL), tile(D_MODEL)] + [_whole()] * 8,
        out_specs=[tile(2 * D_FF), tile(D_FF), tile(D_FF), tile(D_MODEL), tile(D_MODEL),
                   acc(1), acc(D_MODEL), acc(D_MODEL)],
        out_shape=[jax.ShapeDtypeStruct((s, 2 * D_FF), BF16), jax.ShapeDtypeStruct((s, D_FF), BF16),
                   jax.ShapeDtypeStruct((s, D_FF), BF16), jax.ShapeDtypeStruct((s, D_MODEL), F32),
                   jax.ShapeDtypeStruct((s, D_MODEL), BF16),
                   jax.ShapeDtypeStruct((1, 1), F32), jax.ShapeDtypeStruct((1, D_MODEL), F32),
                   jax.ShapeDtypeStruct((1, D_MODEL), F32)],
        scratch_shapes=[pltpu.VMEM((ts, D_FF), F32), pltpu.VMEM((ts + CONV_HALO, D_FF), F32),
                        pltpu.VMEM((ts, D_MODEL), F32)],
        compiler_params=_params(("arbitrary",)),
    )(xhat1, h1b, target, wup4, wdown, cw, cb, g1, b1, g2, b2)


def _ffn_bwd(dz2, dz2b, ub, sd, xhat1, rstd1, wup4, wdown, cw, g1, ts):
    s = dz2.shape[0]
    nt = s // ts

    def body(dz2_ref, dz2b_ref, ub_ref, sd_ref, xhat_ref, rstd_ref, wup_ref, wdn_ref, cw_ref, g1_ref,
             dub_ref, dz1_ref, dz1b_ref, dg1_ref, db1_ref, dcw_ref, dcb_ref, dext_scr, da_scr):
        i = pl.program_id(0)

        @pl.when(i == 0)
        def _():
            dext_scr[ts:ts + CONV_HALO, :] = jnp.zeros((CONV_HALO, D_FF), F32)
            dg1_ref[...] = jnp.zeros_like(dg1_ref)
            db1_ref[...] = jnp.zeros_like(db1_ref)
            dcw_ref[...] = jnp.zeros_like(dcw_ref)
            dcb_ref[...] = jnp.zeros_like(dcb_ref)

        da_scr[...] = _dot_nt(dz2b_ref[...], wdn_ref[...])
        n_ext = ts + CONV_HALO
        for c0 in range(0, D_FF, FFN_STRIP):
            cols = slice(c0, c0 + FFN_STRIP)
            gcols = slice(D_FF + c0, D_FF + c0 + FFN_STRIP)
            val = ub_ref[:, cols].astype(F32)
            gate = ub_ref[:, gcols].astype(F32)
            da = da_scr[:, cols]
            hc = sd_ref[:, cols].astype(F32)
            sg = _sigmoid(hc)
            dhc = da * val * (sg * (1.0 + hc * (1.0 - sg)))
            dext_scr[0:ts, cols] = dhc
            dext = dext_scr[:, cols]
            dhc1 = pltpu.roll(dext, n_ext - 1, 0)[0:ts]
            dhc2 = pltpu.roll(dext, n_ext - 2, 0)[0:ts]
            dcb_ref[:, cols] += jnp.sum(dhc, axis=0, keepdims=True)
            dcw_ref[0:1, cols] += jnp.sum(dhc2 * gate, axis=0, keepdims=True)
            dcw_ref[1:2, cols] += jnp.sum(dhc1 * gate, axis=0, keepdims=True)
            dcw_ref[2:3, cols] += jnp.sum(dhc * gate, axis=0, keepdims=True)
            dgate = dhc * cw_ref[2:3, cols] + dhc1 * cw_ref[1:2, cols] + dhc2 * cw_ref[0:1, cols]
            dub_ref[:, cols] = (da * (hc * sg)).astype(BF16)
            dub_ref[:, gcols] = dgate.astype(BF16)
        dext_scr[ts:n_ext, :] = dext_scr[0:CONV_HALO, :]
        dh1 = ALPHA * dz2_ref[...]
        for j in range(N_SHARD):
            dh1 = dh1 + _dot_nt(dub_ref[:, j * UP_SH:(j + 1) * UP_SH], wup_ref[j])
        xhat = xhat_ref[...]
        dg1_ref[...] += jnp.sum(dh1 * xhat, axis=0, keepdims=True)
        db1_ref[...] += jnp.sum(dh1, axis=0, keepdims=True)
        dz1 = _layernorm_bwd(dh1, xhat, rstd_ref[...], g1_ref[...])
        dz1_ref[...] = dz1
        dz1b_ref[...] = dz1.astype(BF16)

    tile = lambda w: pl.BlockSpec((ts, w), lambda i: (nt - 1 - i, 0))
    acc = lambda rws, w: pl.BlockSpec((rws, w), lambda i: (0, 0))
    return pl.pallas_call(
        body, name="ffn_bwd", grid=(nt,),
        in_specs=[tile(D_MODEL), tile(D_MODEL), tile(2 * D_FF), tile(D_FF), tile(D_MODEL), tile(1)]
        + [_whole()] * 4,
        out_specs=[tile(2 * D_FF), tile(D_MODEL), tile(D_MODEL), acc(1, D_MODEL), acc(1, D_MODEL),
                   acc(3, D_FF), acc(1, D_FF)],
        out_shape=[jax.ShapeDtypeStruct((s, 2 * D_FF), BF16),
                   jax.ShapeDtypeStruct((s, D_MODEL), F32), jax.ShapeDtypeStruct((s, D_MODEL), BF16),
                   jax.ShapeDtypeStruct((1, D_MODEL), F32),
                   jax.ShapeDtypeStruct((1, D_MODEL), F32), jax.ShapeDtypeStruct((3, D_FF), F32),
                   jax.ShapeDtypeStruct((1, D_FF), F32)],
        scratch_shapes=[pltpu.VMEM((ts + CONV_HALO, D_FF), F32), pltpu.VMEM((ts, D_FF), F32)],
        compiler_params=_params(("arbitrary",)),
    )(dz2, dz2b, ub, sd, xhat1, rstd1, wup4, wdown, cw, g1)


def _mix_bwd(dz1, pooled, ret, g, wout, wpool, pscale, loss, vec_grads, ts, riders=(), after=()):
    s = dz1.shape[0]
    nt = s // ts

    def body(dz1_ref, pooled_ref, ret_ref, g_ref, wout_ref, wp_ref, ps_ref, loss_ref, dcb_ref, dg1_ref, db1_ref,
             dg2_ref, db2_ref, dret_ref, dgp_ref, dwp_ref, dps_ref, packed_ref, eext_scr):
        i = pl.program_id(0)
        r = nt - 1 - i

        @pl.when(i == 0)
        def _():
            eext_scr[ts:ts + POOL_HALO, :] = jnp.zeros((POOL_HALO, POOL_W), F32)
            dwp_ref[...] = jnp.zeros_like(dwp_ref)
            dps_ref[...] = jnp.zeros_like(dps_ref)

        dzb = dz1_ref[...].astype(BF16)
        dcat_r = _dot_nt(dzb, wout_ref[0:RET_W, :])
        dcat_p = _dot_nt(dzb, wout_ref[RET_W:2 * RET_W, :])
        pos = (r * ts + lax.broadcasted_iota(jnp.int32, (ts, 1), 0) + 1).astype(F32)
        dpooled = []
        for gi, w in enumerate(POOL_WINDOWS):
            sl = slice(gi * HEAD_DIM, (gi + 1) * HEAD_DIM)
            pb = pooled_ref[:, sl]
            dy = dcat_p[:, sl]
            dps_ref[:, sl] += jnp.sum(dy * _dot(pb, wp_ref[gi]), axis=0, keepdims=True)
            dlin = (dy * ps_ref[:, sl]).astype(BF16)
            dwp_ref[gi] += _dot_tn(pb, dlin)
            dpg = _dot_nt(dlin, wp_ref[gi])
            dpooled.append(dpg)
            eext_scr[0:ts, sl] = dpg / jnp.minimum(pos, float(w))
        for gi, w in enumerate(POOL_WINDOWS):
            sl = slice(gi * HEAD_DIM, (gi + 1) * HEAD_DIM)
            acc = eext_scr[:, sl]
            shift = 1
            while shift < w:
                acc = acc + pltpu.roll(acc, ts + POOL_HALO - shift, 0)
                shift *= 2
            dgp_ref[:, RET_W + gi * HEAD_DIM:RET_W + (gi + 1) * HEAD_DIM] = (acc[0:ts] - dpooled[gi]).astype(BF16)
        eext_scr[ts:ts + POOL_HALO, :] = eext_scr[0:POOL_HALO, :]
        for h in range(HEADS):
            sl = slice(h * HEAD_DIM, (h + 1) * HEAD_DIM)
            rt = ret_ref[:, sl]
            rr = lax.rsqrt(jnp.mean(rt * rt, axis=-1, keepdims=True) + RMS_EPS)
            rn = rt * rr
            gh = g_ref[:, sl]
            sg = _sigmoid(gh)
            dy = dcat_r[:, sl]
            dgp_ref[:, sl] = (dy * rn * (sg * (1.0 + gh * (1.0 - sg)))).astype(BF16)
            drn = dy * (gh * sg)
            dret_ref[:, sl] = (rr * (drn - rn * jnp.mean(drn * rn, axis=-1, keepdims=True))).astype(BF16)

        @pl.when(i == nt - 1)
        def _():
            packed_ref[...] = jnp.zeros_like(packed_ref)
            rows = (dcb_ref, dps_ref, dg1_ref, db1_ref, dg2_ref, db2_ref)
            for (row, n), ref in zip(SMALL_VECS, rows):
                packed_ref[row:row + 1, 0:n] = ref[...]
            packed_ref[ROW_LOSS:ROW_LOSS + 1, 0:HEAD_DIM] = jnp.broadcast_to(loss_ref[...], (1, HEAD_DIM))

    tile = lambda w: pl.BlockSpec((ts, w), lambda i: (nt - 1 - i, 0))
    return _call(
        body, name="mix_bwd", grid=(nt,),
        in_specs=[tile(D_MODEL), tile(POOL_W), tile(RET_W), tile(RET_W)] + [_whole()] * 9,
        out_specs=[tile(RET_W), tile(2 * RET_W),
                   pl.BlockSpec((len(POOL_WINDOWS), HEAD_DIM, HEAD_DIM), lambda i: (0, 0, 0)),
                   pl.BlockSpec((1, POOL_W), lambda i: (0, 0)),
                   pl.BlockSpec((SMALL_ROWS, D_FF), lambda i: (0, 0))],
        out_shape=[jax.ShapeDtypeStruct((s, RET_W), BF16), jax.ShapeDtypeStruct((s, 2 * RET_W), BF16),
                   jax.ShapeDtypeStruct((len(POOL_WINDOWS), HEAD_DIM, HEAD_DIM), F32),
                   jax.ShapeDtypeStruct((1, POOL_W), F32), jax.ShapeDtypeStruct((SMALL_ROWS, D_FF), F32)],
        scratch_shapes=[pltpu.VMEM((ts + POOL_HALO, POOL_W), F32)],
        sem=("arbitrary",), operands=(dz1, pooled, ret, g, wout, wpool, pscale, loss, *vec_grads), riders=riders,
        after=after,
    )


def _retention_bwd(q, k, v, dret, dgp, states, mask, qd, kd, cosf, sinf, riders=(), after=()):
    s = q.shape[0]
    ns = s // SUPER
    cdec = [gm ** float(SUPER) for gm in _gammas()]

    def body(q_ref, k_ref, v_ref, do_ref, dgp_ref, st_ref, mask_ref, qd_ref, kd_ref, cos_ref, sin_ref,
             dproj_ref, dstate_scr):
        i = pl.program_id(0)

        @pl.when(i == 0)
        def _():
            dstate_scr[...] = jnp.zeros_like(dstate_scr)

        cosf_t = cos_ref[...]
        sinf_t = sin_ref[...]
        for h in range(HEADS):
            sl = slice(h * HEAD_DIM, (h + 1) * HEAD_DIM)
            qh, kh, vh, doh = q_ref[:, sl], k_ref[:, sl], v_ref[:, sl], do_ref[:, sl]
            dscb = (_dot_nt(doh, vh) * mask_ref[0, h]).astype(BF16)
            dsctb = (_dot_nt(vh, doh) * mask_ref[1, h]).astype(BF16)
            sctb = (_dot_nt(kh, qh) * mask_ref[1, h]).astype(BF16)
            stb = st_ref[0, h]
            dst = dstate_scr[h]
            dstb = dst.astype(BF16)
            qdb = (qh.astype(F32) * qd_ref[:, sl]).astype(BF16)
            kdb = (kh.astype(F32) * kd_ref[:, sl]).astype(BF16)
            dq = _dot(dscb, kh) + _dot_nt(doh, stb) * qd_ref[:, sl]
            dk = _dot(dsctb, qh) + _dot_nt(vh, dstb) * kd_ref[:, sl]
            dv = _dot(sctb, doh) + _dot(kdb, dstb)
            dstate_scr[h] = dst * cdec[h] + _dot_tn(qdb, doh)
            lo = h * HEAD_DIM
            dproj_ref[:, lo:lo + HEAD_DIM] = _rope_t(dq, cosf_t, sinf_t).astype(BF16)
            dproj_ref[:, RET_W + lo:RET_W + lo + HEAD_DIM] = _rope_t(dk * K_SCALE, cosf_t, sinf_t).astype(BF16)
            dproj_ref[:, 2 * RET_W + lo:2 * RET_W + lo + HEAD_DIM] = dv.astype(BF16)
        dproj_ref[:, 3 * RET_W:IN_W] = dgp_ref[...]

    tile = lambda w: pl.BlockSpec((SUPER, w), lambda i: (ns - 1 - i, 0))
    return _call(
        body, name="retention_bwd", grid=(ns,),
        in_specs=[tile(RET_W), tile(RET_W), tile(RET_W), tile(RET_W), tile(2 * RET_W),
                  pl.BlockSpec((1, HEADS, HEAD_DIM, HEAD_DIM), lambda i: (ns - 1 - i, 0, 0, 0)),
                  _whole(), _whole(), _whole(), tile(HEAD_DIM), tile(HEAD_DIM)],
        out_specs=[tile(IN_W)],
        out_shape=[jax.ShapeDtypeStruct((s, IN_W), BF16)],
        scratch_shapes=[pltpu.VMEM((HEADS, HEAD_DIM, HEAD_DIM), F32)],
        sem=("arbitrary",), operands=(q, k, v, dret, dgp, states, mask, qd, kd, cosf, sinf), riders=riders,
        after=after,
    )


def _dx(dz1, dproj, win4, ts, riders=(), after=()):
    s = dz1.shape[0]

    def body(dz1_ref, dp_ref, w_ref, dx_ref):
        acc = ALPHA * dz1_ref[...]
        for j in range(N_SHARD):
            acc = acc + _dot_nt(dp_ref[:, j * IN_SH:(j + 1) * IN_SH], w_ref[j])
        dx_ref[...] = acc

    tile = lambda w: pl.BlockSpec((ts, w), lambda i: (i, 0))
    return _call(
        body, name="dx", grid=(s // ts,),
        in_specs=[tile(D_MODEL), tile(IN_W), _whole()],
        out_specs=[tile(D_MODEL)],
        out_shape=[jax.ShapeDtypeStruct((s, D_MODEL), F32)],
        sem=("arbitrary",), operands=(dz1, dproj, win4), riders=riders, after=after,
    )


def _wgrad(a, b, tm, tn, name, stacked, m_outer, riders=(), after=()):
    s, m = a.shape
    n = b.shape[1]

    def body(a_ref, b_ref, o32_ref, o16_ref):
        res = _dot_tn(a_ref[...], b_ref[...])
        o32_ref[...] = res.reshape(o32_ref.shape)
        o16_ref[...] = res.astype(BF16).reshape(o16_ref.shape)

    if m_outer:
        grid, blocks = (m // tm, n // tn), (lambda g0, g1: (g0, g1))
    else:
        grid, blocks = (n // tn, m // tm), (lambda g0, g1: (g1, g0))
    if stacked:
        shape = (n // tn, m, tn)
        ospec = pl.BlockSpec((1, tm, tn), lambda g0, g1: (blocks(g0, g1)[1], blocks(g0, g1)[0], 0))
    else:
        shape = (m, n)
        ospec = pl.BlockSpec((tm, tn), lambda g0, g1: blocks(g0, g1))
    return _call(
        body, name=name, grid=grid,
        in_specs=[pl.BlockSpec((s, tm), lambda g0, g1: (0, blocks(g0, g1)[0])),
                  pl.BlockSpec((s, tn), lambda g0, g1: (0, blocks(g0, g1)[1]))],
        out_specs=[ospec, ospec],
        out_shape=[jax.ShapeDtypeStruct(shape, F32), jax.ShapeDtypeStruct(shape, BF16)],
        sem=("arbitrary", "arbitrary"), operands=(a, b), riders=riders, after=after,
    )


def _wgrad_send(a, b, tn, name, barrier_id, after=()):
    s, m = a.shape
    n = b.shape[1]
    nb, hm = n // tn, m // 2

    def body(*refs):
        a_ref, b_ref = refs[:2]
        o32_ref, land_ref, send_scr, send_sems, recv_sems = refs[2 + len(after):]
        j = pl.program_id(0)
        x, y, c = _mesh_pos()

        @pl.when(j == 0)
        def _():
            _shake_hands("sibling")

        o32_ref[0] = _dot_tn(a_ref[...], b_ref[...])
        theirs = pl.ds(pl.multiple_of((1 - c) * hm, 16), hm)
        copies = [pltpu.make_async_remote_copy(
            src_ref=send_scr.at[blk], dst_ref=land_ref.at[blk], send_sem=send_sems.at[blk],
            recv_sem=recv_sems.at[blk], device_id=(x, y, 1 - c), device_id_type=MESH) for blk in range(nb)]
        for blk in range(nb):
            @pl.when(j == blk)
            def _(blk=blk):
                send_scr[blk] = o32_ref[0, theirs, :].astype(BF16)
                copies[blk].start()

        @pl.when(j == nb - 1)
        def _():
            for cp in copies:
                cp.wait()

    return pl.pallas_call(
        body, name=name, grid=(nb,),
        in_specs=[pl.BlockSpec((s, m), lambda j: (0, 0)), pl.BlockSpec((s, tn), lambda j: (0, j))]
        + [_whole()] * len(after),
        out_specs=[pl.BlockSpec((1, m, tn), lambda j: (j, 0, 0)), HBM_SPEC],
        out_shape=[jax.ShapeDtypeStruct((nb, m, tn), F32), jax.ShapeDtypeStruct((nb, hm, tn), BF16)],
        scratch_shapes=[pltpu.VMEM((nb, hm, tn), BF16), pltpu.SemaphoreType.DMA((nb,)),
                        pltpu.SemaphoreType.DMA((nb,))],
        compiler_params=pltpu.CompilerParams(dimension_semantics=("arbitrary",), vmem_limit_bytes=VMEM_LIMIT,
                                             collective_id=barrier_id),
    )(a, b, *after)


class _NoComm:
    def __init__(self, win4, wout, wup4, wdown):
        self.weights = dict(w_in=win4, w_out=wout, w_up=wup4, w_down=wdown)
        self.grads = {}

    def weight(self, name):
        return self.weights[name]

    def riders(self, call):
        return ()

    def after(self, call):
        return ()

    def landed(self, call, results, outs):
        pass

    def small_gradients(self, small, packed):
        pass

    def gradient(self, name, g32, g16):
        self.grads[name] = (g32, g16)

    def wgrad_in(self, xb, dproj):
        (g32, g16), _ = _wgrad(xb, dproj, D_MODEL, IN_SH, "wgrad_in", True, True)
        self.gradient("w_in", g32, g16)


def _local_step(x, target, cw, cb, wpool_b, pscale, g1, b1, g2, b2, comm):
    s = x.shape[0]
    ts_a = min(512, s)
    ts_f = min(256, s)
    mask, qd, kd = _decay_tables()
    cosf, sinf = _rope_tables(s)

    def run(call, fn, *args):
        outs, res = fn(*args, riders=comm.riders(call), after=comm.after(call))
        comm.landed(call, res, outs)
        return outs

    xb, q, k, v, g, pooled, cat = run("proj_pool", _proj_pool, x, comm.weight("w_in"), cosf, sinf, wpool_b,
                                      pscale, ts_a)
    ret, cat, states = run("retention_fwd", _retention_fwd, q, k, v, g, cat, mask, qd, kd)
    wout = comm.weight("w_out")
    xhat1, rstd1, h1b = run("outproj_ln1", _outproj_ln1, x, cat, wout, g1, b1, ts_a)
    wup4, wdown = comm.weight("w_up"), comm.weight("w_down")
    ub, act, sd, dz2, dz2b, loss, dg2, db2 = _ffn_fwd_loss(xhat1, h1b, target, wup4, wdown, cw, cb, g1, b1, g2, b2,
                                                           ts_f)

    dub, dz1, dz1b, dg1, db1, dcw, dcb = _ffn_bwd(dz2, dz2b, ub, sd, xhat1, rstd1, wup4, wdown, cw, g1, ts_f)
    half = D_MODEL // 2
    comm.gradient("w_up", *run("wgrad_up", _wgrad, h1b, dub, half, UP_SH, "wgrad_up", True, False))
    comm.gradient("w_out", *run("wgrad_out", _wgrad, cat, dz1b, D_MODEL, half, "wgrad_out", False, True))
    comm.gradient("w_down", *run("wgrad_down", _wgrad, act, dz2b, D_FF // 2, half, "wgrad_down", False, True))
    dret, dgp, dwp, dps, packed = run("mix_bwd", _mix_bwd, dz1b, pooled, ret, g, wout, wpool_b, pscale, loss,
                                      [dcb, dg1, db1, dg2, db2], ts_a)
    small = dict(w_pool=dwp, pool_scale=dps, ln1_g=dg1, ln1_b=db1, conv_w=dcw, conv_b=dcb,
                 ln2_g=dg2, ln2_b=db2)
    comm.small_gradients(small, packed)
    mask_both = jnp.stack([mask, jnp.swapaxes(mask, 1, 2)])
    dproj, = run("retention_bwd", _retention_bwd, q, k, v, dret, dgp, states, mask_both, qd, kd, cosf, sinf)
    comm.wgrad_in(xb, dproj)
    (grad_x,), _ = _dx(dz1, dproj, comm.weight("w_in"), ts_a, after=comm.after("dx"))
    return loss, grad_x, small


CAST_ROWS = 64
SHARD_SHAPES = ((D_MODEL, IN_SH), (OUT_SH, D_MODEL), (D_MODEL, UP_SH), (DOWN_SH, D_MODEL))
N_BIG = len(SHARD_SHAPES)
CW_SHARD = (3, 1, DOWN_SH)


def _mesh_pos():
    return lax.axis_index("x"), lax.axis_index("y"), lax.axis_index("c")


def _other_chips(x, y):
    return [(1 - x, y), (x, 1 - y), (1 - x, 1 - y)]


def _shake_hands(peers):
    x, y, c = _mesh_pos()
    others = [(x, y, 1 - c)] if peers in ("sibling", "both", "all") else []
    if peers in ("chips", "both", "all"):
        others += [(chip[0], chip[1], c) for chip in _other_chips(x, y)]
    if peers == "all":
        others += [(chip[0], chip[1], 1 - c) for chip in _other_chips(x, y)]
    barrier = pltpu.get_barrier_semaphore()
    for peer in others:
        pl.semaphore_signal(barrier, inc=1, device_id=peer, device_id_type=MESH)
    pl.semaphore_wait(barrier, len(others))


def _half_rows(w, which):
    hr = SHARD_SHAPES[w][0] // 2
    return pl.ds(pl.multiple_of(which * hr, 16), hr)


def _gather_weights(shards, cw_shard, wpool, full):
    def body(*refs):
        in_refs = refs[:N_BIG]
        cw_ref, wpool_ref = refs[N_BIG:N_BIG + 2]
        out_refs = refs[N_BIG + 2:2 * N_BIG + 2]
        cwo_ref, wpool_b_ref = refs[2 * N_BIG + 2:2 * N_BIG + 4]
        stage = refs[2 * N_BIG + 4:3 * N_BIG + 4]
        raw = refs[3 * N_BIG + 4:4 * N_BIG + 4 - len(full)]
        send_sems, recv_sems, fsend_sems, frecv_sems, cw_send, cw_recv, local_sems, load_sems = \
            refs[4 * N_BIG + 4 - len(full):]
        x, y, c = _mesh_pos()
        j0 = 2 * x + y
        chips = _other_chips(x, y)

        fetched = [w for w in range(N_BIG) if w not in full]
        f32 = {w: in_refs[w] for w in full}
        loads = []
        for n, w in enumerate(fetched):
            f32[w] = raw[n]
            loads.append(pltpu.make_async_copy(in_refs[w], raw[n], load_sems.at[n]))
            loads[-1].start()

        def cast_to_stage(w):
            def cast(i, carry):
                rows = pl.ds(pl.multiple_of(i * CAST_ROWS, CAST_ROWS), CAST_ROWS)
                stage[w][rows, :] = f32[w][rows, :].astype(BF16)
                return carry
            lax.fori_loop(0, SHARD_SHAPES[w][0] // CAST_ROWS, cast, 0)

        for w in full:
            cast_to_stage(w)

        jx, jy, jd = 2 * (1 - x) + y, 2 * x + (1 - y), 2 * (1 - x) + (1 - y)
        neighbours = [((1 - x, y, c), jx), ((x, 1 - y, c), jy)]
        passed = jnp.where(c == 0, jx, jy)
        pass_to = (jnp.where(c == 0, x, 1 - x), jnp.where(c == 0, 1 - y, y), c)

        def nbr(w, k, block):
            return pltpu.make_async_remote_copy(
                src_ref=stage[w].at[_half_rows(w, c), :], dst_ref=out_refs[w].at[block, _half_rows(w, c), :],
                send_sem=send_sems.at[w, k], recv_sem=recv_sems.at[w, k],
                device_id=neighbours[k][0], device_id_type=MESH)

        def relay(w, block):
            return pltpu.make_async_remote_copy(
                src_ref=out_refs[w].at[passed, _half_rows(w, c), :],
                dst_ref=out_refs[w].at[block, _half_rows(w, c), :],
                send_sem=send_sems.at[w, 2], recv_sem=recv_sems.at[w, 2],
                device_id=pass_to, device_id_type=MESH)

        def d2d(w, k, block, half):
            return pltpu.make_async_remote_copy(
                src_ref=out_refs[w].at[block, _half_rows(w, half), :],
                dst_ref=out_refs[w].at[block, _half_rows(w, half), :],
                send_sem=fsend_sems.at[w, k], recv_sem=frecv_sems.at[w, k],
                device_id=(x, y, 1 - c), device_id_type=MESH)

        def conv(k, block):
            chip = chips[k]
            return pltpu.make_async_remote_copy(
                src_ref=cw_ref, dst_ref=cwo_ref.at[block], send_sem=cw_send.at[k], recv_sem=cw_recv.at[k],
                device_id=(chip[0], chip[1], c), device_id_type=MESH)

        sent = [nbr(w, k, j0) for w in full for k in range(2)] + [conv(k, j0) for k in range(3)]
        for cp in sent:
            cp.start()
        for n, w in enumerate(fetched):
            loads[n].wait()
            cast_to_stage(w)
        local = [pltpu.make_async_copy(stage[w], out_refs[w].at[j0], local_sems.at[w]) for w in range(N_BIG)]
        local.append(pltpu.make_async_copy(cw_ref, cwo_ref.at[j0], local_sems.at[N_BIG]))
        for cp in local:
            cp.start()
        wpool_b_ref[...] = wpool_ref[...].astype(BF16)
        for w in full:
            for k, (_, block) in enumerate(neighbours):
                nbr(w, k, block).wait_recv()
            later = [relay(w, passed)] + [d2d(w, k, block, c) for k, (_, block) in enumerate(neighbours)]
            for cp in later:
                cp.start()
            sent += later
        for w in full:
            relay(w, jd).wait_recv()
            fw = d2d(w, 2, jd, c)
            fw.start()
            sent.append(fw)
        for w in full:
            for k, block in enumerate([jx, jy, jd]):
                d2d(w, k, block, 1 - c).wait_recv()
        for k, chip in enumerate(chips):
            conv(k, 2 * chip[0] + chip[1]).wait_recv()
        for cp in sent:
            cp.wait_send()
        for cp in local:
            cp.wait()

    out_shape = [jax.ShapeDtypeStruct((N_SHARD,) + shp, BF16) for shp in SHARD_SHAPES]
    out_shape.append(jax.ShapeDtypeStruct((N_SHARD,) + CW_SHARD, F32))
    out_shape.append(jax.ShapeDtypeStruct(wpool.shape, BF16))
    return pl.pallas_call(
        body, name="gather_weights",
        in_specs=[_whole() if w in full else HBM_SPEC for w in range(N_BIG)] + [_whole()] * 2,
        out_specs=[HBM_SPEC] * (N_BIG + 1) + [_whole()],
        out_shape=out_shape,
        scratch_shapes=[pltpu.VMEM(shp, BF16) for shp in SHARD_SHAPES]
        + [pltpu.VMEM(shp, F32) for w, shp in enumerate(SHARD_SHAPES) if w not in full] + [
            pltpu.SemaphoreType.DMA((N_BIG, 3)), pltpu.SemaphoreType.DMA((N_BIG, 3)),
            pltpu.SemaphoreType.DMA((N_BIG, 3)), pltpu.SemaphoreType.DMA((N_BIG, 3)),
            pltpu.SemaphoreType.DMA((3,)), pltpu.SemaphoreType.DMA((3,)),
            pltpu.SemaphoreType.DMA((N_BIG + 1,)), pltpu.SemaphoreType.DMA((N_BIG - len(full),))],
        compiler_params=pltpu.CompilerParams(vmem_limit_bytes=VMEM_LIMIT),
    )(*shards, cw_shard, wpool)


def _gather_rider(arrays, ops, handshake=None):
    ws = sorted(arrays)

    def make(inplace, srcs, lands, send_sems, recv_sems):
        del srcs, lands
        x, y, c = _mesh_pos()
        j0, jx, jy, jd = 2 * x + y, 2 * (1 - x) + y, 2 * x + (1 - y), 2 * (1 - x) + (1 - y)
        x_nbr, y_nbr, sibling = (1 - x, y, c), (x, 1 - y, c), (x, y, 1 - c)
        starts, waits = [], []
        for n, (kind, w, (r0, nr)) in enumerate(ops):
            ref = inplace[ws.index(w)]
            hr = SHARD_SHAPES[w][0] // 2
            rows = lambda core: pl.ds(pl.multiple_of(core * hr + r0, 16), nr)
            mine, theirs = rows(c), rows(1 - c)
            if kind == "ici":
                moves = [(ref.at[j0, mine, :], x_nbr, ref.at[jx, mine, :]),
                         (ref.at[j0, mine, :], y_nbr, ref.at[jy, mine, :]),
                         (ref.at[j0, mine, :], (1 - x, 1 - y, c), ref.at[jd, mine, :])]
            elif kind == "nbr":
                moves = [(ref.at[j0, mine, :], x_nbr, ref.at[jx, mine, :]),
                         (ref.at[j0, mine, :], y_nbr, ref.at[jy, mine, :])]
            elif kind == "relay":
                passed = jnp.where(c == 0, jx, jy)
                to = (jnp.where(c == 0, x, 1 - x), jnp.where(c == 0, 1 - y, y), c)
                moves = [(ref.at[passed, mine, :], to, ref.at[jd, mine, :])]
            else:
                blocks = dict(d2d=[jx, jy, jd], d2d_nbr=[jx, jy], d2d_diag=[jd])[kind]
                moves = [(ref.at[b, mine, :], sibling, ref.at[b, theirs, :]) for b in blocks]
            for k, (src, to, landing) in enumerate(moves):
                sems = dict(send_sem=send_sems.at[3 * n + k], recv_sem=recv_sems.at[3 * n + k],
                            device_id=to, device_id_type=MESH)
                send = pltpu.make_async_remote_copy(src_ref=src, dst_ref=src, **sems)
                arrival = pltpu.make_async_remote_copy(src_ref=src, dst_ref=landing, **sems)
                starts.append(send)
                waits += [arrival.wait_recv, send.wait_send]
        return starts, waits

    return _Rider([arrays[w] for w in ws], [], [], 3 * len(ops), make, handshake)


def _whole_half(w):
    return (0, SHARD_SHAPES[w][0] // 2)


def _pair_rider(ws, g16s):
    def make(inplace, srcs, lands, send_sems, recv_sems):
        del inplace
        x, y, c = _mesh_pos()
        copies = [pltpu.make_async_remote_copy(
            src_ref=srcs[i].at[:, _half_rows(w, 1 - c), :], dst_ref=lands[i],
            send_sem=send_sems.at[i], recv_sem=recv_sems.at[i], device_id=(x, y, 1 - c), device_id_type=MESH)
            for i, w in enumerate(ws)]
        return copies, [cp.wait for cp in copies]

    lands = [jax.ShapeDtypeStruct((N_SHARD, SHARD_SHAPES[w][0] // 2, SHARD_SHAPES[w][1]), BF16) for w in ws]
    return _Rider([], g16s, lands, len(ws), make)


def _chip_rider(ws, p16s):
    def make(inplace, srcs, lands, send_sems, recv_sems):
        del inplace
        x, y, c = _mesh_pos()
        copies = []
        for i in range(len(ws)):
            for k, chip in enumerate(_other_chips(x, y)):
                copies.append(pltpu.make_async_remote_copy(
                    src_ref=srcs[i].at[2 * chip[0] + chip[1]], dst_ref=lands[i].at[k],
                    send_sem=send_sems.at[3 * i + k], recv_sem=recv_sems.at[3 * i + k],
                    device_id=(chip[0], chip[1], c), device_id_type=MESH))
        return copies, [cp.wait for cp in copies]

    lands = [jax.ShapeDtypeStruct((3, SHARD_SHAPES[w][0] // 2, SHARD_SHAPES[w][1]), BF16) for w in ws]
    return _Rider([], p16s, lands, 3 * len(ws), make)


def _final_rider(halves):
    def make(inplace, srcs, lands, send_sems, recv_sems):
        del inplace
        x, y, c = _mesh_pos()
        copies = [pltpu.make_async_remote_copy(
            src_ref=srcs[i], dst_ref=lands[i], send_sem=send_sems.at[i], recv_sem=recv_sems.at[i],
            device_id=(x, y, 1 - c), device_id_type=MESH) for i in range(len(halves))]
        return copies, [cp.wait for cp in copies]

    return _Rider([], halves, [jax.ShapeDtypeStruct(h.shape, h.dtype) for h in halves], len(halves), make)


N_DEV = 2 * N_SHARD


def _device_number(x, y, c):
    return 2 * (2 * x + y) + c


def _small_all_rider(own):
    n = len(own)

    def make(inplace, srcs, lands, send_sems, recv_sems):
        del inplace
        x, y, c = _mesh_pos()
        peers = [(x, y, 1 - c)] + [(chip[0], chip[1], core) for chip in _other_chips(x, y) for core in (c, 1 - c)]
        copies = []
        for i in range(n):
            for k, peer in enumerate(peers):
                copies.append(pltpu.make_async_remote_copy(
                    src_ref=srcs[i], dst_ref=lands[i].at[_device_number(x, y, c)],
                    send_sem=send_sems.at[(N_DEV - 1) * i + k], recv_sem=recv_sems.at[(N_DEV - 1) * i + k],
                    device_id=peer, device_id_type=MESH))
        return copies, [cp.wait for cp in copies]

    lands = [jax.ShapeDtypeStruct((N_DEV,) + a.shape, a.dtype) for a in own]
    return _Rider([], own, lands, (N_DEV - 1) * n, make)


def _comm_only(name, riders):
    _, res = _call(lambda: None, name=name, grid=(), in_specs=[], out_specs=[], out_shape=[], operands=(),
                   riders=riders)
    return res


class _SemList:
    def __init__(self, refs):
        self.at = list(refs)


def _merged_rider(riders):
    srcs = [a for r in riders for a in r.srcs]
    lands = [a for r in riders for a in r.lands]

    def make(inplace, src_refs, land_refs, send_sems, recv_sems):
        starts, waits = [], []
        s0 = l0 = c0 = 0
        for r in riders:
            part = r.make(inplace, src_refs[s0:s0 + len(r.srcs)], land_refs[l0:l0 + len(r.lands)],
                          _SemList(send_sems.at[c0:c0 + r.n_copies]), _SemList(recv_sems.at[c0:c0 + r.n_copies]))
            starts += part[0]
            waits += part[1]
            s0, l0, c0 = s0 + len(r.srcs), l0 + len(r.lands), c0 + r.n_copies
        return starts, waits

    return _Rider([], srcs, lands, sum(r.n_copies for r in riders), make)


def _split_start(name, rider, handshake=None):
    assert not rider.inplace
    ns, nl, n = len(rider.srcs), len(rider.lands), rider.n_copies
    barrier_id, peers = handshake if handshake is not None else (None, None)

    def body(*refs):
        if handshake is not None:
            _shake_hands(peers)
        srcs, lands = refs[:ns], refs[ns:ns + nl]
        sems = refs[ns + nl:ns + nl + 2 * n]
        token = refs[-1]
        starts, _ = rider.make([], srcs, lands, _SemList(sems[:n]), _SemList(sems[n:]))
        for cp in starts:
            cp.start()
        token[...] = jnp.zeros_like(token)

    buffers = [pltpu.with_memory_space_constraint(a, pltpu.HBM) for a in rider.srcs]
    buffers += [pltpu.with_memory_space_constraint(lax.empty(s.shape, s.dtype), pltpu.HBM) for s in rider.lands]
    hbm = pl.BlockSpec(memory_space=pltpu.HBM)
    sem = pl.BlockSpec(memory_space=pltpu.SEMAPHORE)
    outs = pl.pallas_call(
        body, name=name,
        out_shape=tuple([pltpu.SemaphoreType.DMA(())] * (2 * n) + [pltpu.HBM(b.shape, b.dtype) for b in buffers]
                        + [jax.ShapeDtypeStruct((8, 128), F32)]),
        in_specs=[hbm] * (ns + nl),
        out_specs=tuple([sem] * (2 * n) + [hbm] * (ns + nl) + [_whole()]),
        input_output_aliases={i: 2 * n + i for i in range(ns + nl)},
        compiler_params=pltpu.CompilerParams(has_side_effects=pltpu.SideEffectType.DATAFLOW_SIDE_EFFECTING,
                                             collective_id=barrier_id),
    )(*buffers)
    return (rider, outs[:2 * n], outs[2 * n:2 * n + ns + nl]), outs[-1]


def _split_parts(state, riders):
    merged, sems, buffers = state
    n, ns = merged.n_copies, len(merged.srcs)
    parts, s0, l0, c0 = [], 0, 0, 0
    for r in riders:
        parts.append((r, list(sems[c0:c0 + r.n_copies]) + list(sems[n + c0:n + c0 + r.n_copies]),
                      list(buffers[s0:s0 + len(r.srcs)]) + list(buffers[ns + l0:ns + l0 + len(r.lands)])))
        s0, l0, c0 = s0 + len(r.srcs), l0 + len(r.lands), c0 + r.n_copies
    return parts


def _split_wait(name, state, after):
    rider, sems, buffers = state
    ns, nl, n = len(rider.srcs), len(rider.lands), rider.n_copies

    def body(*refs):
        srcs, lands = refs[:ns], refs[ns:ns + nl]
        sem_refs = refs[ns + nl:ns + nl + 2 * n]
        _, waits = rider.make([], srcs, lands, _SemList(sem_refs[:n]), _SemList(sem_refs[n:]))
        for wait in waits:
            wait()

    hbm = pl.BlockSpec(memory_space=pltpu.HBM)
    sem = pl.BlockSpec(memory_space=pltpu.SEMAPHORE)
    outs = pl.pallas_call(
        body, name=name,
        out_shape=tuple(pltpu.HBM(b.shape, b.dtype) for b in buffers),
        in_specs=[hbm] * (ns + nl) + [sem] * (2 * n) + [HBM_SPEC],
        out_specs=tuple([hbm] * (ns + nl)),
        input_output_aliases={i: i for i in range(ns + nl)},
        compiler_params=pltpu.CompilerParams(has_side_effects=pltpu.SideEffectType.DATAFLOW_SIDE_EFFECTING),
    )(*buffers, *sems, after)
    return list(outs[:ns]), list(outs[ns:])


def _pair_sum(pos, ws, g32s, recvs):
    n = len(ws)

    def body(pos_ref, *refs):
        del pos_ref
        g_refs, r_refs = refs[:n], refs[n:2 * n]
        p32_refs, p16_refs = refs[2 * n:3 * n], refs[3 * n:]
        x, y, _ = _mesh_pos()
        for i in range(n):
            tot = g_refs[i][...] + r_refs[i][...].astype(F32)
            p16_refs[i][...] = tot.astype(BF16)

            @pl.when(pl.program_id(0) == 2 * x + y)
            def _(i=i, tot=tot):
                p32_refs[i][...] = tot

    halves = [(SHARD_SHAPES[w][0] // 2, SHARD_SHAPES[w][1]) for w in ws]
    own = [pl.BlockSpec((None, None) + h, lambda j, pos_ref: (j, pos_ref[0], 0, 0)) for h in halves]
    blk = [pl.BlockSpec((None,) + h, lambda j, pos_ref: (j, 0, 0)) for h in halves]
    mine = [pl.BlockSpec(h, lambda j, pos_ref: (0, 0)) for h in halves]
    g4 = [g.reshape((N_SHARD, 2) + h) for g, h in zip(g32s, halves)]
    outs = pl.pallas_call(
        body, name="pair_sum_" + "_".join(str(w) for w in ws),
        grid_spec=pltpu.PrefetchScalarGridSpec(
            num_scalar_prefetch=1, grid=(N_SHARD,), in_specs=own + blk, out_specs=mine + blk),
        out_shape=[jax.ShapeDtypeStruct(h, F32) for h in halves]
        + [jax.ShapeDtypeStruct((N_SHARD,) + h, BF16) for h in halves],
        compiler_params=_params(("arbitrary",)),
    )(pos, *g4, *recvs)
    return outs[:n], outs[n:]


def _chip_sum(p32s, recvs):
    parts = 2

    def body(*refs):
        p_refs, r_refs, f_refs = refs[:N_BIG], refs[N_BIG:2 * N_BIG], refs[2 * N_BIG:]
        for w in range(N_BIG):
            f_refs[w][...] = ((p_refs[w][...] + r_refs[w][0].astype(F32)) + r_refs[w][1].astype(F32)) \
                + r_refs[w][2].astype(F32)

    quarters = [(r // 2 // parts, cc) for r, cc in SHARD_SHAPES]
    own = [pl.BlockSpec(qt, lambda i: (i, 0)) for qt in quarters]
    rcv = [pl.BlockSpec((3,) + qt, lambda i: (0, i, 0)) for qt in quarters]
    out = [pl.BlockSpec(qt, lambda i: (i, 0)) for qt in quarters]
    return pl.pallas_call(
        body, name="chip_sum", grid=(parts,), in_specs=own + rcv, out_specs=out,
        out_shape=[jax.ShapeDtypeStruct((r // 2, cc), F32) for r, cc in SHARD_SHAPES],
        compiler_params=_params(("arbitrary",)),
    )(*p32s, *recvs)


def _adamw(w, g, m, v):
    m_new = ADAM_B1 * m + (1.0 - ADAM_B1) * g
    v_new = ADAM_B2 * v + (1.0 - ADAM_B2) * (g * g)
    m_hat = m_new / (1.0 - ADAM_B1 ** ADAM_STEP)
    v_hat = v_new / (1.0 - ADAM_B2 ** ADAM_STEP)
    delta = -ADAM_LR * (m_hat / (jnp.sqrt(v_hat) + ADAM_EPS) + ADAM_WD * w)
    return delta, m_new, v_new


def _adam_half(name, pos, grads, ws, ms, vs, into=None):
    nb = 4
    which = (lambda ref: ref[0]) if into is None else (lambda ref: 1 - ref[0])

    def body(which_ref, *refs):
        del which_ref
        groups = [refs[i * N_BIG:(i + 1) * N_BIG] for i in range(4)]
        g_refs, w_refs, m_refs, v_refs = groups
        go_refs, do_refs, mo_refs, vo_refs = [refs[len(refs) - (4 - i) * N_BIG:len(refs) - (3 - i) * N_BIG]
                                              for i in range(4)]
        for w in range(N_BIG):
            g = g_refs[w][...]
            delta, m_new, v_new = _adamw(w_refs[w][...], g, m_refs[w][...], v_refs[w][...])
            go_refs[w][...] = g
            do_refs[w][...] = delta
            mo_refs[w][...] = m_new
            vo_refs[w][...] = v_new

    blocks = [(r // 2 // nb, cc) for r, cc in SHARD_SHAPES]
    half = [pl.BlockSpec(b, lambda i, which_ref: (i, 0)) for b in blocks]
    full = [pl.BlockSpec((None,) + b, lambda i, which_ref: (0, which(which_ref) * nb + i, 0)) for b in blocks]
    shapes = [jax.ShapeDtypeStruct((1,) + shp, F32) for shp in SHARD_SHAPES]
    carried = [] if into is None else [a for kind in into for a in kind]
    first = 1 + 4 * N_BIG
    outs = pl.pallas_call(
        body, name=name,
        grid_spec=pltpu.PrefetchScalarGridSpec(
            num_scalar_prefetch=1, grid=(nb,), in_specs=half + full * 3 + [HBM_SPEC] * len(carried),
            out_specs=full * 4),
        out_shape=shapes * 4,
        input_output_aliases={first + i: i for i in range(len(carried))},
        compiler_params=_params(("arbitrary",)),
    )(pos, *grads, *ws, *ms, *vs, *carried)
    return [outs[i * N_BIG:(i + 1) * N_BIG] for i in range(4)]


SMALL_ROWS = 8
ROW_CONV_B, ROW_POOL_SCALE, ROW_LN1_G, ROW_LN1_B, ROW_LN2_G, ROW_LN2_B, ROW_LOSS = range(7)
SMALL_VECS = ((ROW_CONV_B, D_FF), (ROW_POOL_SCALE, POOL_W), (ROW_LN1_G, D_MODEL), (ROW_LN1_B, D_MODEL),
              (ROW_LN2_G, D_MODEL), (ROW_LN2_B, D_MODEL))


def _small_adam(all_a, all_b, all_c, own_a, own_b, own_c, wp, cwp, vec_ws, m_wp, m_cwp, vec_ms,
                v_wp, v_cwp, vec_vs):
    nv = len(SMALL_VECS)
    np_ = 2 + nv

    def body(*refs):
        all_a_ref, all_b_ref, all_c_ref, own_a_ref, own_b_ref, own_c_ref = refs[0:6]
        refs = refs[3:]
        w_all, m_all, v_all = (refs[3 + i * np_:3 + (i + 1) * np_] for i in range(3))
        loss_out = refs[3 + 3 * np_]
        outs = refs[4 + 3 * np_:]
        x, y, c = _mesh_pos()
        j0 = 2 * x + y
        me = _device_number(x, y, c)

        def total(sent, own):
            by_dev = [jnp.where(me == d, own, sent(d)) for d in range(N_DEV)]
            chips = [by_dev[2 * j] + by_dev[2 * j + 1] for j in range(N_SHARD)]
            return ((chips[0] + chips[1]) + chips[2]) + chips[3]

        tot_a = total(lambda d: all_a_ref[d], own_a_ref[...])
        tot_b = total(lambda d: all_b_ref[d], own_b_ref[...])
        tot_c = total(lambda d: all_c_ref[d, j0], own_c_ref[j0])
        loss_out[...] = tot_b[ROW_LOSS:ROW_LOSS + 1, 0:1]
        grads = [tot_a, tot_c] + [tot_b[row:row + 1, 0:n] for row, n in SMALL_VECS]
        for p in range(np_):
            for at, g in ([(j, tot_c[j:j + 1]) for j in range(3)] if p == 1 else [(Ellipsis, grads[p])]):
                delta, m_new, v_new = _adamw(w_all[p][at], g, m_all[p][at], v_all[p][at])
                outs[p][at] = g
                outs[np_ + p][at] = delta
                outs[2 * np_ + p][at] = m_new
                outs[3 * np_ + p][at] = v_new

    pshapes = [wp.shape, CW_SHARD] + [wv.shape for wv in vec_ws]
    out_shape = [jax.ShapeDtypeStruct((1, 1), F32)] + [jax.ShapeDtypeStruct(s, F32) for s in pshapes] * 4
    outs = pl.pallas_call(
        body, name="small_adam",
        in_specs=[_whole()] * (6 + 3 * np_), out_specs=[_whole()] * len(out_shape), out_shape=out_shape,
        compiler_params=pltpu.CompilerParams(vmem_limit_bytes=VMEM_LIMIT),
    )(all_a, all_b, all_c, own_a, own_b, own_c, wp, cwp, *vec_ws, m_wp, m_cwp, *vec_ms, v_wp, v_cwp, *vec_vs)
    return outs[0], [outs[1 + i * np_:1 + (i + 1) * np_] for i in range(4)]


def kernel(x, w_in, w_pool, pool_scale, w_out, ln1_g, ln1_b, w_up, conv_w, conv_b, w_down, ln2_g, ln2_b, loss_target, m_w_in, m_w_pool, m_pool_scale, m_w_out, m_ln1_g, m_ln1_b, m_w_up, m_conv_w, m_conv_b, m_w_down, m_ln2_g, m_ln2_b, v_w_in, v_w_pool, v_pool_scale, v_w_out, v_ln1_g, v_ln1_b, v_w_up, v_conv_w, v_conv_b, v_w_down, v_ln2_g, v_ln2_b):
    pos = lax.axis_index("c").astype(jnp.int32).reshape(1)
    order = ("w_in", "w_out", "w_up", "w_down")
    w_in_i, w_out_i, w_up_i, w_down_i = range(N_BIG)
    vec_names = ("conv_b", "pool_scale", "ln1_g", "ln1_b", "ln2_g", "ln2_b")

    taps_first = lambda a: jnp.transpose(a, (1, 0, 2))
    gathered = _gather_weights([w_in[0], w_out[0], w_up[0], w_down[0]], taps_first(conv_w), w_pool[0], (w_in_i,))
    cw_full = jnp.transpose(gathered[N_BIG].reshape(N_SHARD, 3, DOWN_SH), (1, 0, 2)).reshape(3, D_FF)
    up_a, up_b, up_c = (0, 192), (192, 192), (384, 128)
    assert up_c[0] + up_c[1] == SHARD_SHAPES[w_up_i][0] // 2

    class MeshComm:
        def __init__(self):
            self.w = {i: gathered[i] for i in range(N_BIG)}
            self.g32, self.g16, self.p32, self.p16, self.recv_b = {}, {}, {}, {}, {}
            self.up_complete = False
            self.tokens, self.chips = {}, []

        def weight(self, name):
            i = order.index(name)
            if name == "w_up" and not self.up_complete:
                (arrs, _), = _comm_only("gather_up_last", [_gather_rider(
                    {i: self.w[i]}, [("d2d_diag", i, up_b), ("d2d", i, up_c)], (12, "sibling"))])
                self.w[i], self.up_complete = arrs[0], True
            full = self.w[i]
            return full.reshape(-1, full.shape[-1]) if name in ("w_out", "w_down") else full

        def _gather(self, ws, ops, handshake):
            return _gather_rider({w: self.w[w] for w in ws}, ops, handshake), ("w", ws)

        def _pair(self, ws):
            return _pair_rider(ws, [self.g16[w] for w in ws]), ("recv_a", ws)

        def _chip(self, ws):
            return _chip_rider(ws, [self.p16[w] for w in ws]), ("recv_b", ws)

        def plan(self, call):
            out_all, down_all = _whole_half(w_out_i), _whole_half(w_down_i)
            if call == "proj_pool":
                return [self._gather([w_out_i, w_up_i, w_down_i],
                                     [("ici", w_out_i, out_all), ("nbr", w_down_i, down_all),
                                      ("nbr", w_up_i, up_a)], (9, "chips"))]
            if call == "retention_fwd":
                return [self._gather([w_out_i, w_up_i, w_down_i],
                                     [("d2d", w_out_i, out_all),
                                      ("relay", w_down_i, down_all), ("d2d_nbr", w_down_i, down_all),
                                      ("relay", w_up_i, up_a), ("d2d_nbr", w_up_i, up_a), ("nbr", w_up_i, up_b)],
                                     (10, "both"))]
            if call == "outproj_ln1":
                return [self._gather([w_up_i, w_down_i],
                                     [("d2d_diag", w_down_i, down_all), ("d2d_diag", w_up_i, up_a),
                                      ("relay", w_up_i, up_b), ("d2d_nbr", w_up_i, up_b), ("ici", w_up_i, up_c)],
                                     (11, "both"))]
            return []

        def after(self, call):
            return tuple(self.tokens.pop(call, ()))

        def riders(self, call):
            self.pending = self.plan(call)
            return [r for r, _ in self.pending]

        def _start(self, name, rider, before, handshake):
            state, token = _split_start(name, rider, handshake)
            self.tokens.setdefault(before, []).append(token)
            return state

        def _finish_pair(self, name, state, ws, after):
            _, lands = _split_wait(name, state, after)
            self._finish_sum(ws, lands)

        def landed(self, call, results, outs):
            for (_, (slot, ws)), (inplace, lands) in zip(self.pending, results):
                for w, arr in zip(ws, inplace if len(inplace) else lands):
                    getattr(self, slot)[w] = arr
            if call == "wgrad_out":
                self._finish_pair("pair_exchange_up_wait", self.pair_up, [w_up_i], outs[1])
                self.chips.append(([w_up_i], self._start(
                    "chip_exchange_up_start", self._chip([w_up_i])[0], "wgrad_down", (5, "chips"))))
            if call == "mix_bwd":
                ws = [w_out_i, w_down_i]
                self._finish_pair("pair_exchange_out_down_wait", self.pair_out_down, ws, outs[0])

        def small_gradients(self, small, packed):
            dcw4 = jnp.transpose(small["conv_w"].reshape(3, N_SHARD, DOWN_SH), (1, 0, 2))
            own = [small["w_pool"], packed, dcw4]
            ws = [w_out_i, w_down_i]
            parts = [self._chip(ws)[0], _small_all_rider(own)]
            chip, self.small_all = _split_parts(
                self._start("chip_out_down_small_all_start", _merged_rider(parts), "retention_bwd",
                            (7, "all")), parts)
            self.chips.append((ws, chip))

        def gradient(self, name, g32, g16):
            w = order.index(name)
            shape = (N_SHARD,) + SHARD_SHAPES[w]
            self.g32[w], self.g16[w] = g32.reshape(shape), g16.reshape(shape)
            if name == "w_up":
                self.pair_up = self._start("pair_exchange_up_start", self._pair([w])[0], "wgrad_out",
                                           (1, "sibling"))
            if name == "w_down":
                self.pair_out_down = self._start("pair_exchange_out_down_start",
                                                 self._pair([w_out_i, w_down_i])[0], "mix_bwd", (2, "sibling"))

        def wgrad_in(self, xb, dproj):
            w = w_in_i
            g32, landed = _wgrad_send(xb, dproj, IN_SH, "wgrad_in", 4, after=self.after("wgrad_in"))
            self.g32[w] = g32
            self._finish_sum([w], [landed])
            self.chips.append(([w], self._start("chip_exchange_in_start", self._chip([w])[0], "dx", (8, "chips"))))

        def _finish_sum(self, ws, lands):
            p32s, p16s = _pair_sum(pos, ws, [self.g32[w] for w in ws], lands)
            for w, p32, p16 in zip(ws, p32s, p16s):
                self.p32[w], self.p16[w] = p32, p16

        def finish(self, after):
            for n, (ws, state) in enumerate(self.chips):
                _, lands = _split_wait("chip_exchange_wait_%d" % n, state, after)
                for w, arr in zip(ws, lands):
                    self.recv_b[w] = arr
            own, sent = _split_wait("small_all_wait", self.small_all, after)
            return list(sent) + list(own)

    comm = MeshComm()
    loss, grad_x, small = _local_step(x[0], loss_target[0], cw_full, conv_b, gathered[N_BIG + 1], pool_scale,
                                      ln1_g, ln1_b, ln2_g, ln2_b, comm)

    given = dict(w_pool=w_pool, pool_scale=pool_scale, ln1_g=ln1_g, ln1_b=ln1_b, conv_w=conv_w, conv_b=conv_b,
                 ln2_g=ln2_g, ln2_b=ln2_b)
    given_m = dict(w_pool=m_w_pool, pool_scale=m_pool_scale, ln1_g=m_ln1_g, ln1_b=m_ln1_b, conv_w=m_conv_w,
                   conv_b=m_conv_b, ln2_g=m_ln2_g, ln2_b=m_ln2_b)
    given_v = dict(w_pool=v_w_pool, pool_scale=v_pool_scale, ln1_g=v_ln1_g, ln1_b=v_ln1_b, conv_w=v_conv_w,
                   conv_b=v_conv_b, ln2_g=v_ln2_g, ln2_b=v_ln2_b)
    args = []
    for src in (given, given_m, given_v):
        args += [src["w_pool"][0], taps_first(src["conv_w"]), [src[n] for n in vec_names]]
    small_sums = comm.finish(grad_x)
    loss_tot, small_out = _small_adam(*small_sums, *args)
    every = range(N_BIG)
    mine = _chip_sum([comm.p32[w] for w in every], [comm.recv_b[w] for w in every])
    final_state, _ = _split_start("pair_exchange_f32_start", _final_rider(mine), (3, "sibling"))
    mine = final_state[2][:N_BIG]
    big = ([w_in, w_out, w_up, w_down], [m_w_in, m_w_out, m_w_up, m_w_down], [v_w_in, v_w_out, v_w_up, v_w_down])
    own_half = _adam_half("adam_own_half", pos, mine, *big)
    _, theirs = _split_wait("pair_exchange_f32_wait", final_state, own_half[0][0])
    big_out = _adam_half("adam_other_half", pos, theirs, *big, into=own_half)

    names = ("w_in", "w_pool", "pool_scale", "w_out", "ln1_g", "ln1_b", "w_up", "conv_w", "conv_b", "w_down",
             "ln2_g", "ln2_b")
    small_names = ("w_pool", "conv_w") + vec_names
    result = [loss_tot.reshape(()), grad_x[None]]
    for kind in range(4):
        for n in names:
            if n in order:
                result.append(big_out[kind][order.index(n)])
            else:
                val = small_out[kind][small_names.index(n)]
                if n == "conv_w":
                    val = taps_first(val)
                elif n == "w_pool":
                    val = val[None]
                result.append(val)
    return tuple(result)
```

```python
import functools

import numpy as np
import jax
import jax.numpy as jnp
from jax import lax
from jax.experimental import pallas as pl
from jax.experimental.pallas import tpu as pltpu

F32 = jnp.float32
BF16 = jnp.bfloat16

D_MODEL = 1024
HEADS = 4
HEAD_DIM = 128
RET_W = HEADS * HEAD_DIM
POOL_WINDOWS = (2, 4, 8, 16)
POOL_W = 512
IN_W = 4 * RET_W + POOL_W
D_FF = 2816
N_SHARD = 4
IN_SH = IN_W // N_SHARD
UP_SH = 2 * D_FF // N_SHARD
DOWN_SH = D_FF // N_SHARD
OUT_SH = D_MODEL // N_SHARD
ROPE_BASE = 10000.0
LN_EPS = 1e-5
RMS_EPS = 1e-6
ALPHA = 2.0 ** 0.25
K_SCALE = HEAD_DIM ** -0.5
SUPER = 256
CHUNK = 64
POOL_HALO = 16
CONV_HALO = 8
FFN_STRIP = 128
LN_ROWS = 32

ADAM_LR = 0.001
ADAM_B1 = 0.9
ADAM_B2 = 0.999
ADAM_EPS = 1e-08
ADAM_WD = 0.01
ADAM_STEP = 10

MESH = pl.DeviceIdType.MESH
VMEM_LIMIT = 56 * 1024 * 1024


def _dot(a, b):
    return jnp.dot(a, b, preferred_element_type=F32)


def _dot_nt(a, b):
    return lax.dot_general(a, b, (((1,), (1,)), ((), ())), preferred_element_type=F32)


def _dot_tn(a, b):
    return lax.dot_general(a, b, (((0,), (0,)), ((), ())), preferred_element_type=F32)


def _sigmoid(x):
    return 1.0 / (1.0 + jnp.exp(-x))


def _params(sem):
    return pltpu.CompilerParams(dimension_semantics=sem, vmem_limit_bytes=VMEM_LIMIT)


def _whole():
    return pl.BlockSpec(memory_space=pltpu.VMEM)


HBM_SPEC = pl.BlockSpec(memory_space=pl.ANY)


class _Rider:
    def __init__(self, inplace, srcs, lands, n_copies, make, handshake=None):
        self.inplace, self.srcs, self.lands, self.n_copies, self.make = list(inplace), list(srcs), list(lands), n_copies, make
        self.handshake = handshake


def _call(body, *, name, grid, in_specs, out_specs, out_shape, operands, scratch_shapes=(), sem=(),
          aliases=None, riders=(), after=()):
    n_in, n_out, n_scr = len(in_specs), len(out_shape), len(scratch_shapes)
    in_specs, out_specs, out_shape = list(in_specs), list(out_specs), list(out_shape)
    operands, scratch_shapes, aliases = list(operands), list(scratch_shapes), dict(aliases or {})
    in_specs += [_whole()] * len(after)
    operands += list(after)
    shakes = [r.handshake for r in riders if r.handshake is not None]
    assert len(shakes) <= 1
    for r in riders:
        for a in r.inplace:
            aliases[len(in_specs)] = len(out_shape)
            in_specs.append(HBM_SPEC)
            operands.append(a)
            out_specs.append(HBM_SPEC)
            out_shape.append(jax.ShapeDtypeStruct(a.shape, a.dtype))
        for a in r.srcs:
            in_specs.append(HBM_SPEC)
            operands.append(a)
        for shp in r.lands:
            out_specs.append(HBM_SPEC)
            out_shape.append(shp)
        scratch_shapes += [pltpu.SemaphoreType.DMA((r.n_copies,)), pltpu.SemaphoreType.DMA((r.n_copies,))]

    def full(*refs):
        ins = refs[:n_in]
        at = n_in + len(after)
        r_srcs = []
        for r in riders:
            at += len(r.inplace)
            r_srcs.append(refs[at:at + len(r.srcs)])
            at += len(r.srcs)
        outs = refs[at:at + n_out]
        at += n_out
        r_outs = []
        for r in riders:
            r_outs.append((refs[at:at + len(r.inplace)], refs[at + len(r.inplace):at + len(r.inplace) + len(r.lands)]))
            at += len(r.inplace) + len(r.lands)
        scr = refs[at:at + n_scr]
        at += n_scr
        r_sems = [refs[at + 2 * i:at + 2 * i + 2] for i in range(len(riders))]

        def copies():
            return [r.make(r_outs[i][0], r_srcs[i], r_outs[i][1], r_sems[i][0], r_sems[i][1])
                    for i, r in enumerate(riders)]

        def start():
            if shakes:
                _shake_hands(shakes[0][1])
            for starts, _ in copies():
                for cp in starts:
                    cp.start()

        def finish():
            for _, waits in copies():
                for wait in waits:
                    wait()

        if riders and grid:
            first = functools.reduce(jnp.logical_and, [pl.program_id(d) == 0 for d in range(len(grid))])
            last = functools.reduce(jnp.logical_and, [pl.program_id(d) == grid[d] - 1 for d in range(len(grid))])
            pl.when(first)(start)
            body(*ins, *outs, *scr)
            pl.when(last)(finish)
        else:
            if riders:
                start()
            body(*ins, *outs, *scr)
            if riders:
                finish()

    barrier_id = shakes[0][0] if shakes else None
    params = pltpu.CompilerParams(vmem_limit_bytes=VMEM_LIMIT, collective_id=barrier_id,
                                  **(dict(dimension_semantics=sem) if grid else {}))
    res = pl.pallas_call(
        full, name=name, grid=grid, in_specs=in_specs, out_specs=out_specs, out_shape=out_shape,
        scratch_shapes=scratch_shapes, input_output_aliases=aliases, compiler_params=params,
    )(*operands)
    outs, at, rider_res = res[:n_out], n_out, []
    for r in riders:
        rider_res.append((res[at:at + len(r.inplace)], res[at + len(r.inplace):at + len(r.inplace) + len(r.lands)]))
        at += len(r.inplace) + len(r.lands)
    return list(outs), rider_res


def _gammas():
    return [1.0 - 2.0 ** (-5.0 - h) for h in range(HEADS)]


def _decay_tables():
    idx = np.arange(SUPER)
    dist = np.abs(idx[:, None] - idx[None, :]).astype(np.float64)
    visible = (idx[None, :] // CHUNK) <= (idx[:, None] // CHUNK)
    mask = np.stack([np.where(visible, g ** dist, 0.0) for g in _gammas()])
    qd = np.concatenate([np.repeat((g ** (idx + 1.0))[:, None], HEAD_DIM, 1) for g in _gammas()], 1)
    kd = np.concatenate([np.repeat((g ** (SUPER - 1.0 - idx))[:, None], HEAD_DIM, 1) for g in _gammas()], 1)
    return (jnp.asarray(mask, F32), jnp.asarray(qd, F32), jnp.asarray(kd, F32))


def _rope_tables(s):
    inv_freq = ROPE_BASE ** (-np.arange(0, HEAD_DIM, 2, dtype=np.float64) / HEAD_DIM)
    ang = np.arange(s, dtype=np.float64)[:, None] * inv_freq[None, :]
    cos, sin = np.cos(ang), np.sin(ang)
    return (jnp.asarray(np.concatenate([cos, cos], 1), F32),
            jnp.asarray(np.concatenate([-sin, sin], 1), F32))


def _rope(t, cosf, sinf):
    return t * cosf + pltpu.roll(t, HEAD_DIM // 2, 1) * sinf


def _rope_t(t, cosf, sinf):
    return t * cosf - pltpu.roll(t, HEAD_DIM // 2, 1) * sinf


def _layernorm_fwd(z):
    mu = jnp.mean(z, axis=-1, keepdims=True)
    zc = z - mu
    var = jnp.mean(zc * zc, axis=-1, keepdims=True)
    rstd = lax.rsqrt(var + LN_EPS)
    return zc * rstd, rstd


def _layernorm_bwd(dy, xhat, rstd, gain):
    dxh = dy * gain
    m1 = jnp.mean(dxh, axis=-1, keepdims=True)
    m2 = jnp.mean(dxh * xhat, axis=-1, keepdims=True)
    return rstd * (dxh - m1 - xhat * m2)


def _proj_pool(x, win4, cosf, sinf, wpool, pscale, ts, riders=(), after=()):
    s = x.shape[0]
    nt = s // ts

    def body(x_ref, w_ref, cos_ref, sin_ref, wp_ref, ps_ref,
             xb_ref, q_ref, k_ref, v_ref, g_ref, pooled_ref, cat_ref, proj_scr, pext_scr):
        i = pl.program_id(0)
        xb = x_ref[...].astype(BF16)
        xb_ref[...] = xb
        for j in range(N_SHARD):
            proj_scr[:, j * IN_SH:(j + 1) * IN_SH] = _dot(xb, w_ref[j])
        cosf_t = cos_ref[...]
        sinf_t = sin_ref[...]
        for h in range(HEADS):
            lo = h * HEAD_DIM
            q_ref[:, lo:lo + HEAD_DIM] = _rope(proj_scr[:, lo:lo + HEAD_DIM], cosf_t, sinf_t).astype(BF16)
            kk = _rope(proj_scr[:, RET_W + lo:RET_W + lo + HEAD_DIM], cosf_t, sinf_t) * K_SCALE
            k_ref[:, lo:lo + HEAD_DIM] = kk.astype(BF16)
        v_ref[...] = proj_scr[:, 2 * RET_W:3 * RET_W].astype(BF16)
        g_ref[...] = proj_scr[:, 3 * RET_W:4 * RET_W]

        @pl.when(i == 0)
        def _():
            pext_scr[0:POOL_HALO, :] = jnp.zeros((POOL_HALO, POOL_W), F32)

        pext_scr[POOL_HALO:POOL_HALO + ts, :] = proj_scr[:, 4 * RET_W:IN_W]
        pos = (i * ts + lax.broadcasted_iota(jnp.int32, (ts, 1), 0) + 1).astype(F32)
        for gi, w in enumerate(POOL_WINDOWS):
            lo = gi * HEAD_DIM
            ext = pext_scr[:, lo:lo + HEAD_DIM]
            acc = ext
            shift = 1
            while shift < w:
                acc = acc + pltpu.roll(acc, shift, 0)
                shift *= 2
            tok = ext[POOL_HALO:POOL_HALO + ts]
            pooled = acc[POOL_HALO:POOL_HALO + ts] / jnp.minimum(pos, float(w)) - tok
            pooled_b = pooled.astype(BF16)
            pooled_ref[:, lo:lo + HEAD_DIM] = pooled_b
            lin = _dot(pooled_b, wp_ref[gi])
            cat_ref[:, lo:lo + HEAD_DIM] = (lin * ps_ref[:, lo:lo + HEAD_DIM]).astype(BF16)
        pext_scr[0:POOL_HALO, :] = pext_scr[ts:ts + POOL_HALO, :]

    tile = lambda w: pl.BlockSpec((ts, w), lambda i: (i, 0))
    return _call(
        body, name="proj_pool", grid=(nt,),
        in_specs=[tile(D_MODEL), _whole(), tile(HEAD_DIM), tile(HEAD_DIM), _whole(), _whole()],
        out_specs=[tile(D_MODEL), tile(RET_W), tile(RET_W), tile(RET_W), tile(RET_W), tile(POOL_W),
                   pl.BlockSpec((ts, POOL_W), lambda i: (i, 1))],
        out_shape=[jax.ShapeDtypeStruct((s, D_MODEL), BF16), jax.ShapeDtypeStruct((s, RET_W), BF16),
                   jax.ShapeDtypeStruct((s, RET_W), BF16), jax.ShapeDtypeStruct((s, RET_W), BF16),
                   jax.ShapeDtypeStruct((s, RET_W), F32), jax.ShapeDtypeStruct((s, POOL_W), BF16),
                   jax.ShapeDtypeStruct((s, 2 * RET_W), BF16)],
        scratch_shapes=[pltpu.VMEM((ts, IN_W), F32), pltpu.VMEM((ts + POOL_HALO, POOL_W), F32)],
        sem=("arbitrary",), operands=(x, win4, cosf, sinf, wpool, pscale), riders=riders, after=after,
    )


def _retention_fwd(q, k, v, g, cat, mask, qd, kd, riders=(), after=()):
    s = q.shape[0]
    ns = s // SUPER
    cdec = [gm ** float(SUPER) for gm in _gammas()]

    def body(q_ref, k_ref, v_ref, g_ref, cat_in, mask_ref, qd_ref, kd_ref,
             ret_ref, cat_ref, st_ref, state_scr):
        del cat_in
        n = pl.program_id(0)

        @pl.when(n == 0)
        def _():
            state_scr[...] = jnp.zeros_like(state_scr)

        for h in range(HEADS):
            sl = slice(h * HEAD_DIM, (h + 1) * HEAD_DIM)
            qh, kh, vh = q_ref[:, sl], k_ref[:, sl], v_ref[:, sl]
            sc = _dot_nt(qh, kh) * mask_ref[h]
            st = state_scr[h]
            stb = st.astype(BF16)
            st_ref[0, h] = stb
            qdb = (qh.astype(F32) * qd_ref[:, sl]).astype(BF16)
            kdb = (kh.astype(F32) * kd_ref[:, sl]).astype(BF16)
            ret = _dot(sc.astype(BF16), vh) + _dot(qdb, stb)
            state_scr[h] = st * cdec[h] + _dot_tn(kdb, vh)
            ret_ref[:, sl] = ret
            r = lax.rsqrt(jnp.mean(ret * ret, axis=-1, keepdims=True) + RMS_EPS)
            gh = g_ref[:, sl]
            cat_ref[:, sl] = ((ret * r) * (gh * _sigmoid(gh))).astype(BF16)

    tile = pl.BlockSpec((SUPER, RET_W), lambda n: (n, 0))
    return _call(
        body, name="retention_fwd", grid=(ns,),
        in_specs=[tile, tile, tile, tile, HBM_SPEC, _whole(), _whole(), _whole()],
        out_specs=[tile, tile, pl.BlockSpec((1, HEADS, HEAD_DIM, HEAD_DIM), lambda n: (n, 0, 0, 0))],
        out_shape=[jax.ShapeDtypeStruct((s, RET_W), F32), jax.ShapeDtypeStruct((s, 2 * RET_W), BF16),
                   jax.ShapeDtypeStruct((ns, HEADS, HEAD_DIM, HEAD_DIM), BF16)],
        scratch_shapes=[pltpu.VMEM((HEADS, HEAD_DIM, HEAD_DIM), F32)],
        aliases={4: 1}, sem=("arbitrary",), operands=(q, k, v, g, cat, mask, qd, kd), riders=riders,
        after=after,
    )


def _outproj_ln1(x, cat, wout, g1, b1, ts, riders=(), after=()):
    s = x.shape[0]

    def body(x_ref, cat_ref, w_ref, g_ref, b_ref, xhat_ref, rstd_ref, h1b_ref):
        z = ALPHA * x_ref[...] + _dot(cat_ref[...], w_ref[...])
        xhat, rstd = _layernorm_fwd(z)
        xhat_ref[...] = xhat
        rstd_ref[...] = rstd
        h1b_ref[...] = (xhat * g_ref[...] + b_ref[...]).astype(BF16)

    tile = lambda w: pl.BlockSpec((ts, w), lambda i: (i, 0))
    return _call(
        body, name="outproj_ln1", grid=(s // ts,),
        in_specs=[tile(D_MODEL), tile(D_MODEL), _whole(), _whole(), _whole()],
        out_specs=[tile(D_MODEL), tile(1), tile(D_MODEL)],
        out_shape=[jax.ShapeDtypeStruct((s, D_MODEL), F32), jax.ShapeDtypeStruct((s, 1), F32),
                   jax.ShapeDtypeStruct((s, D_MODEL), BF16)],
        sem=("arbitrary",), operands=(x, cat, wout, g1, b1), riders=riders, after=after,
    )


def _ffn_fwd_loss(xhat1, h1b, target, wup4, wdown, cw, cb, g1, b1, g2, b2, ts):
    s = xhat1.shape[0]

    def body(xhat_ref, h1b_ref, tgt_ref, wup_ref, wdn_ref, cw_ref, cb_ref, g1_ref, b1_ref, g2_ref, b2_ref,
             ub_ref, act_ref, sd_ref, dz2_ref, dz2b_ref, loss_ref, dg2_ref, db2_ref, val_scr, gext_scr, ffn_scr):
        i = pl.program_id(0)

        @pl.when(i == 0)
        def _():
            gext_scr[0:CONV_HALO, :] = jnp.zeros((CONV_HALO, D_FF), F32)
            loss_ref[...] = jnp.zeros_like(loss_ref)
            dg2_ref[...] = jnp.zeros_like(dg2_ref)
            db2_ref[...] = jnp.zeros_like(db2_ref)

        for half in range(2):
            lo = half * UP_SH
            gext_scr[CONV_HALO:CONV_HALO + ts, lo:lo + UP_SH] = _dot(h1b_ref[...], wup_ref[2 + half])
            val_scr[:, lo:lo + UP_SH] = _dot(h1b_ref[...], wup_ref[half])
            for c0 in range(lo, lo + UP_SH, FFN_STRIP):
                cols = slice(c0, c0 + FFN_STRIP)
                ext = gext_scr[:, cols]
                gate = ext[CONV_HALO:]
                hc = cb_ref[:, cols] + ((pltpu.roll(ext, 2, 0)[CONV_HALO:] * cw_ref[0:1, cols]
                                         + pltpu.roll(ext, 1, 0)[CONV_HALO:] * cw_ref[1:2, cols])
                                        + gate * cw_ref[2:3, cols])
                val = val_scr[:, cols]
                sg = _sigmoid(hc)
                si = hc * sg
                act_ref[:, cols] = (si * val).astype(BF16)
                ub_ref[:, cols] = val.astype(BF16)
                ub_ref[:, D_FF + c0:D_FF + c0 + FFN_STRIP] = gate.astype(BF16)
                sd_ref[:, cols] = hc.astype(BF16)
            part = _dot(act_ref[:, lo:lo + UP_SH], wdn_ref[lo:lo + UP_SH, :])
            if half == 0:
                ffn_scr[...] = part
            else:
                ffn_scr[...] += part

        gext_scr[0:CONV_HALO, :] = gext_scr[ts:ts + CONV_HALO, :]

        loss_acc = jnp.zeros((1, 1), F32)
        dg2_acc = jnp.zeros((1, D_MODEL), F32)
        db2_acc = jnp.zeros((1, D_MODEL), F32)
        for r0 in range(0, ts, LN_ROWS):
            rows = slice(r0, r0 + LN_ROWS)
            h1 = xhat_ref[rows, :] * g1_ref[...] + b1_ref[...]
            xhat2, rstd2 = _layernorm_fwd(ALPHA * h1 + ffn_scr[rows, :])
            diff = (xhat2 * g2_ref[...] + b2_ref[...]) - tgt_ref[rows, :]
            row = jnp.mean(diff * diff, axis=-1, keepdims=True)
            loss_acc = loss_acc + 0.5 * jnp.sum(row, axis=0, keepdims=True)
            dy = diff * (1.0 / D_MODEL)
            dg2_acc = dg2_acc + jnp.sum(dy * xhat2, axis=0, keepdims=True)
            db2_acc = db2_acc + jnp.sum(dy, axis=0, keepdims=True)
            dz2 = _layernorm_bwd(dy, xhat2, rstd2, g2_ref[...])
            dz2_ref[rows, :] = dz2
            dz2b_ref[rows, :] = dz2.astype(BF16)
        loss_ref[...] += loss_acc
        dg2_ref[...] += dg2_acc
        db2_ref[...] += db2_acc

    tile = lambda w: pl.BlockSpec((ts, w), lambda i: (i, 0))
    acc = lambda w: pl.BlockSpec((1, w), lambda i: (0, 0))
    return pl.pallas_call(
        body, name="ffn_fwd_loss", grid=(s // ts,),
        in_specs=[tile(D_MODEL), tile(D_MODEL), tile(D_MODEL)] + [_whole()] * 8,
        out_specs=[tile(2 * D_FF), tile(D_FF), tile(D_FF), tile(D_MODEL), tile(D_MODEL),
                   acc(1), acc(D_MODEL), acc(D_MODEL)],
        out_shape=[jax.ShapeDtypeStruct((s, 2 * D_FF), BF16), jax.ShapeDtypeStruct((s, D_FF), BF16),
                   jax.ShapeDtypeStruct((s, D_FF), BF16), jax.ShapeDtypeStruct((s, D_MODEL), F32),
                   jax.ShapeDtypeStruct((s, D_MODEL), BF16),
                   jax.ShapeDtypeStruct((1, 1), F32), jax.ShapeDtypeStruct((1, D_MODEL), F32),
                   jax.ShapeDtypeStruct((1, D_MODEL), F32)],
        scratch_shapes=[pltpu.VMEM((ts, D_FF), F32), pltpu.VMEM((ts + CONV_HALO, D_FF), F32),
                        pltpu.VMEM((ts, D_MODEL), F32)],
        compiler_params=_params(("arbitrary",)),
    )(xhat1, h1b, target, wup4, wdown, cw, cb, g1, b1, g2, b2)


def _ffn_bwd(dz2, dz2b, ub, sd, xhat1, rstd1, wup4, wdown, cw, g1, ts):
    s = dz2.shape[0]
    nt = s // ts

    def body(dz2_ref, dz2b_ref, ub_ref, sd_ref, xhat_ref, rstd_ref, wup_ref, wdn_ref, cw_ref, g1_ref,
             dub_ref, dz1_ref, dz1b_ref, dg1_ref, db1_ref, dcw_ref, dcb_ref, dext_scr, da_scr):
        i = pl.program_id(0)

        @pl.when(i == 0)
        def _():
            dext_scr[ts:ts + CONV_HALO, :] = jnp.zeros((CONV_HALO, D_FF), F32)
            dg1_ref[...] = jnp.zeros_like(dg1_ref)
            db1_ref[...] = jnp.zeros_like(db1_ref)
            dcw_ref[...] = jnp.zeros_like(dcw_ref)
            dcb_ref[...] = jnp.zeros_like(dcb_ref)

        da_scr[...] = _dot_nt(dz2b_ref[...], wdn_ref[...])
        n_ext = ts + CONV_HALO
        for c0 in range(0, D_FF, FFN_STRIP):
            cols = slice(c0, c0 + FFN_STRIP)
            gcols = slice(D_FF + c0, D_FF + c0 + FFN_STRIP)
            val = ub_ref[:, cols].astype(F32)
            gate = ub_ref[:, gcols].astype(F32)
            da = da_scr[:, cols]
            hc = sd_ref[:, cols].astype(F32)
            sg = _sigmoid(hc)
            dhc = da * val * (sg * (1.0 + hc * (1.0 - sg)))
            dext_scr[0:ts, cols] = dhc
            dext = dext_scr[:, cols]
            dhc1 = pltpu.roll(dext, n_ext - 1, 0)[0:ts]
            dhc2 = pltpu.roll(dext, n_ext - 2, 0)[0:ts]
            dcb_ref[:, cols] += jnp.sum(dhc, axis=0, keepdims=True)
            dcw_ref[0:1, cols] += jnp.sum(dhc2 * gate, axis=0, keepdims=True)
            dcw_ref[1:2, cols] += jnp.sum(dhc1 * gate, axis=0, keepdims=True)
            dcw_ref[2:3, cols] += jnp.sum(dhc * gate, axis=0, keepdims=True)
            dgate = dhc * cw_ref[2:3, cols] + dhc1 * cw_ref[1:2, cols] + dhc2 * cw_ref[0:1, cols]
            dub_ref[:, cols] = (da * (hc * sg)).astype(BF16)
            dub_ref[:, gcols] = dgate.astype(BF16)
        dext_scr[ts:n_ext, :] = dext_scr[0:CONV_HALO, :]
        dh1 = ALPHA * dz2_ref[...]
        for j in range(N_SHARD):
            dh1 = dh1 + _dot_nt(dub_ref[:, j * UP_SH:(j + 1) * UP_SH], wup_ref[j])
        xhat = xhat_ref[...]
        dg1_ref[...] += jnp.sum(dh1 * xhat, axis=0, keepdims=True)
        db1_ref[...] += jnp.sum(dh1, axis=0, keepdims=True)
        dz1 = _layernorm_bwd(dh1, xhat, rstd_ref[...], g1_ref[...])
        dz1_ref[...] = dz1
        dz1b_ref[...] = dz1.astype(BF16)

    tile = lambda w: pl.BlockSpec((ts, w), lambda i: (nt - 1 - i, 0))
    acc = lambda rws, w: pl.BlockSpec((rws, w), lambda i: (0, 0))
    return pl.pallas_call(
        body, name="ffn_bwd", grid=(nt,),
        in_specs=[tile(D_MODEL), tile(D_MODEL), tile(2 * D_FF), tile(D_FF), tile(D_MODEL), tile(1)]
        + [_whole()] * 4,
        out_specs=[tile(2 * D_FF), tile(D_MODEL), tile(D_MODEL), acc(1, D_MODEL), acc(1, D_MODEL),
                   acc(3, D_FF), acc(1, D_FF)],
        out_shape=[jax.ShapeDtypeStruct((s, 2 * D_FF), BF16),
                   jax.ShapeDtypeStruct((s, D_MODEL), F32), jax.ShapeDtypeStruct((s, D_MODEL), BF16),
                   jax.ShapeDtypeStruct((1, D_MODEL), F32),
                   jax.ShapeDtypeStruct((1, D_MODEL), F32), jax.ShapeDtypeStruct((3, D_FF), F32),
                   jax.ShapeDtypeStruct((1, D_FF), F32)],
        scratch_shapes=[pltpu.VMEM((ts + CONV_HALO, D_FF), F32), pltpu.VMEM((ts, D_FF), F32)],
        compiler_params=_params(("arbitrary",)),
    )(dz2, dz2b, ub, sd, xhat1, rstd1, wup4, wdown, cw, g1)


def _mix_bwd(dz1, pooled, ret, g, wout, wpool, pscale, loss, vec_grads, ts, riders=(), after=()):
    s = dz1.shape[0]
    nt = s // ts

    def body(dz1_ref, pooled_ref, ret_ref, g_ref, wout_ref, wp_ref, ps_ref, loss_ref, dcb_ref, dg1_ref, db1_ref,
             dg2_ref, db2_ref, dret_ref, dgp_ref, dwp_ref, dps_ref, packed_ref, eext_scr):
        i = pl.program_id(0)
        r = nt - 1 - i

        @pl.when(i == 0)
        def _():
            eext_scr[ts:ts + POOL_HALO, :] = jnp.zeros((POOL_HALO, POOL_W), F32)
            dwp_ref[...] = jnp.zeros_like(dwp_ref)
            dps_ref[...] = jnp.zeros_like(dps_ref)

        dzb = dz1_ref[...].astype(BF16)
        dcat_r = _dot_nt(dzb, wout_ref[0:RET_W, :])
        dcat_p = _dot_nt(dzb, wout_ref[RET_W:2 * RET_W, :])
        pos = (r * ts + lax.broadcasted_iota(jnp.int32, (ts, 1), 0) + 1).astype(F32)
        dpooled = []
        for gi, w in enumerate(POOL_WINDOWS):
            sl = slice(gi * HEAD_DIM, (gi + 1) * HEAD_DIM)
            pb = pooled_ref[:, sl]
            dy = dcat_p[:, sl]
            dps_ref[:, sl] += jnp.sum(dy * _dot(pb, wp_ref[gi]), axis=0, keepdims=True)
            dlin = (dy * ps_ref[:, sl]).astype(BF16)
            dwp_ref[gi] += _dot_tn(pb, dlin)
            dpg = _dot_nt(dlin, wp_ref[gi])
            dpooled.append(dpg)
            eext_scr[0:ts, sl] = dpg / jnp.minimum(pos, float(w))
        for gi, w in enumerate(POOL_WINDOWS):
            sl = slice(gi * HEAD_DIM, (gi + 1) * HEAD_DIM)
            acc = eext_scr[:, sl]
            shift = 1
            while shift < w:
                acc = acc + pltpu.roll(acc, ts + POOL_HALO - shift, 0)
                shift *= 2
            dgp_ref[:, RET_W + gi * HEAD_DIM:RET_W + (gi + 1) * HEAD_DIM] = (acc[0:ts] - dpooled[gi]).astype(BF16)
        eext_scr[ts:ts + POOL_HALO, :] = eext_scr[0:POOL_HALO, :]
        for h in range(HEADS):
            sl = slice(h * HEAD_DIM, (h + 1) * HEAD_DIM)
            rt = ret_ref[:, sl]
            rr = lax.rsqrt(jnp.mean(rt * rt, axis=-1, keepdims=True) + RMS_EPS)
            rn = rt * rr
            gh = g_ref[:, sl]
            sg = _sigmoid(gh)
            dy = dcat_r[:, sl]
            dgp_ref[:, sl] = (dy * rn * (sg * (1.0 + gh * (1.0 - sg)))).astype(BF16)
            drn = dy * (gh * sg)
            dret_ref[:, sl] = (rr * (drn - rn * jnp.mean(drn * rn, axis=-1, keepdims=True))).astype(BF16)

        @pl.when(i == nt - 1)
        def _():
            packed_ref[...] = jnp.zeros_like(packed_ref)
            rows = (dcb_ref, dps_ref, dg1_ref, db1_ref, dg2_ref, db2_ref)
            for (row, n), ref in zip(SMALL_VECS, rows):
                packed_ref[row:row + 1, 0:n] = ref[...]
            packed_ref[ROW_LOSS:ROW_LOSS + 1, 0:HEAD_DIM] = jnp.broadcast_to(loss_ref[...], (1, HEAD_DIM))

    tile = lambda w: pl.BlockSpec((ts, w), lambda i: (nt - 1 - i, 0))
    return _call(
        body, name="mix_bwd", grid=(nt,),
        in_specs=[tile(D_MODEL), tile(POOL_W), tile(RET_W), tile(RET_W)] + [_whole()] * 9,
        out_specs=[tile(RET_W), tile(2 * RET_W),
                   pl.BlockSpec((len(POOL_WINDOWS), HEAD_DIM, HEAD_DIM), lambda i: (0, 0, 0)),
                   pl.BlockSpec((1, POOL_W), lambda i: (0, 0)),
                   pl.BlockSpec((SMALL_ROWS, D_FF), lambda i: (0, 0))],
        out_shape=[jax.ShapeDtypeStruct((s, RET_W), BF16), jax.ShapeDtypeStruct((s, 2 * RET_W), BF16),
                   jax.ShapeDtypeStruct((len(POOL_WINDOWS), HEAD_DIM, HEAD_DIM), F32),
                   jax.ShapeDtypeStruct((1, POOL_W), F32), jax.ShapeDtypeStruct((SMALL_ROWS, D_FF), F32)],
        scratch_shapes=[pltpu.VMEM((ts + POOL_HALO, POOL_W), F32)],
        sem=("arbitrary",), operands=(dz1, pooled, ret, g, wout, wpool, pscale, loss, *vec_grads), riders=riders,
        after=after,
    )


def _retention_bwd(q, k, v, dret, dgp, states, mask, qd, kd, cosf, sinf, riders=(), after=()):
    s = q.shape[0]
    ns = s // SUPER
    cdec = [gm ** float(SUPER) for gm in _gammas()]

    def body(q_ref, k_ref, v_ref, do_ref, dgp_ref, st_ref, mask_ref, qd_ref, kd_ref, cos_ref, sin_ref,
             dproj_ref, dstate_scr):
        i = pl.program_id(0)

        @pl.when(i == 0)
        def _():
            dstate_scr[...] = jnp.zeros_like(dstate_scr)

        cosf_t = cos_ref[...]
        sinf_t = sin_ref[...]
        for h in range(HEADS):
            sl = slice(h * HEAD_DIM, (h + 1) * HEAD_DIM)
            qh, kh, vh, doh = q_ref[:, sl], k_ref[:, sl], v_ref[:, sl], do_ref[:, sl]
            dscb = (_dot_nt(doh, vh) * mask_ref[0, h]).astype(BF16)
            dsctb = (_dot_nt(vh, doh) * mask_ref[1, h]).astype(BF16)
            sctb = (_dot_nt(kh, qh) * mask_ref[1, h]).astype(BF16)
            stb = st_ref[0, h]
            dst = dstate_scr[h]
            dstb = dst.astype(BF16)
            qdb = (qh.astype(F32) * qd_ref[:, sl]).astype(BF16)
            kdb = (kh.astype(F32) * kd_ref[:, sl]).astype(BF16)
            dq = _dot(dscb, kh) + _dot_nt(doh, stb) * qd_ref[:, sl]
            dk = _dot(dsctb, qh) + _dot_nt(vh, dstb) * kd_ref[:, sl]
            dv = _dot(sctb, doh) + _dot(kdb, dstb)
            dstate_scr[h] = dst * cdec[h] + _dot_tn(qdb, doh)
            lo = h * HEAD_DIM
            dproj_ref[:, lo:lo + HEAD_DIM] = _rope_t(dq, cosf_t, sinf_t).astype(BF16)
            dproj_ref[:, RET_W + lo:RET_W + lo + HEAD_DIM] = _rope_t(dk * K_SCALE, cosf_t, sinf_t).astype(BF16)
            dproj_ref[:, 2 * RET_W + lo:2 * RET_W + lo + HEAD_DIM] = dv.astype(BF16)
        dproj_ref[:, 3 * RET_W:IN_W] = dgp_ref[...]

    tile = lambda w: pl.BlockSpec((SUPER, w), lambda i: (ns - 1 - i, 0))
    return _call(
        body, name="retention_bwd", grid=(ns,),
        in_specs=[tile(RET_W), tile(RET_W), tile(RET_W), tile(RET_W), tile(2 * RET_W),
                  pl.BlockSpec((1, HEADS, HEAD_DIM, HEAD_DIM), lambda i: (ns - 1 - i, 0, 0, 0)),
                  _whole(), _whole(), _whole(), tile(HEAD_DIM), tile(HEAD_DIM)],
        out_specs=[tile(IN_W)],
        out_shape=[jax.ShapeDtypeStruct((s, IN_W), BF16)],
        scratch_shapes=[pltpu.VMEM((HEADS, HEAD_DIM, HEAD_DIM), F32)],
        sem=("arbitrary",), operands=(q, k, v, dret, dgp, states, mask, qd, kd, cosf, sinf), riders=riders,
        after=after,
    )


def _dx(dz1, dproj, win4, ts, riders=(), after=()):
    s = dz1.shape[0]

    def body(dz1_ref, dp_ref, w_ref, dx_ref):
        acc = ALPHA * dz1_ref[...]
        for j in range(N_SHARD):
            acc = acc + _dot_nt(dp_ref[:, j * IN_SH:(j + 1) * IN_SH], w_ref[j])
        dx_ref[...] = acc

    tile = lambda w: pl.BlockSpec((ts, w), lambda i: (i, 0))
    return _call(
        body, name="dx", grid=(s // ts,),
        in_specs=[tile(D_MODEL), tile(IN_W), _whole()],
        out_specs=[tile(D_MODEL)],
        out_shape=[jax.ShapeDtypeStruct((s, D_MODEL), F32)],
        sem=("arbitrary",), operands=(dz1, dproj, win4), riders=riders, after=after,
    )


def _wgrad(a, b, tm, tn, name, stacked, m_outer, riders=(), after=()):
    s, m = a.shape
    n = b.shape[1]

    def body(a_ref, b_ref, o32_ref, o16_ref):
        res = _dot_tn(a_ref[...], b_ref[...])
        o32_ref[...] = res.reshape(o32_ref.shape)
        o16_ref[...] = res.astype(BF16).reshape(o16_ref.shape)

    if m_outer:
        grid, blocks = (m // tm, n // tn), (lambda g0, g1: (g0, g1))
    else:
        grid, blocks = (n // tn, m // tm), (lambda g0, g1: (g1, g0))
    if stacked:
        shape = (n // tn, m, tn)
        ospec = pl.BlockSpec((1, tm, tn), lambda g0, g1: (blocks(g0, g1)[1], blocks(g0, g1)[0], 0))
    else:
        shape = (m, n)
        ospec = pl.BlockSpec((tm, tn), lambda g0, g1: blocks(g0, g1))
    return _call(
        body, name=name, grid=grid,
        in_specs=[pl.BlockSpec((s, tm), lambda g0, g1: (0, blocks(g0, g1)[0])),
                  pl.BlockSpec((s, tn), lambda g0, g1: (0, blocks(g0, g1)[1]))],
        out_specs=[ospec, ospec],
        out_shape=[jax.ShapeDtypeStruct(shape, F32), jax.ShapeDtypeStruct(shape, BF16)],
        sem=("arbitrary", "arbitrary"), operands=(a, b), riders=riders, after=after,
    )


def _wgrad_send(a, b, tn, name, barrier_id, after=()):
    s, m = a.shape
    n = b.shape[1]
    nb, hm = n // tn, m // 2

    def body(*refs):
        a_ref, b_ref = refs[:2]
        o32_ref, land_ref, send_scr, send_sems, recv_sems = refs[2 + len(after):]
        j = pl.program_id(0)
        x, y, c = _mesh_pos()

        @pl.when(j == 0)
        def _():
            _shake_hands("sibling")

        o32_ref[0] = _dot_tn(a_ref[...], b_ref[...])
        theirs = pl.ds(pl.multiple_of((1 - c) * hm, 16), hm)
        copies = [pltpu.make_async_remote_copy(
            src_ref=send_scr.at[blk], dst_ref=land_ref.at[blk], send_sem=send_sems.at[blk],
            recv_sem=recv_sems.at[blk], device_id=(x, y, 1 - c), device_id_type=MESH) for blk in range(nb)]
        for blk in range(nb):
            @pl.when(j == blk)
            def _(blk=blk):
                send_scr[blk] = o32_ref[0, theirs, :].astype(BF16)
                copies[blk].start()

        @pl.when(j == nb - 1)
        def _():
            for cp in copies:
                cp.wait()

    return pl.pallas_call(
        body, name=name, grid=(nb,),
        in_specs=[pl.BlockSpec((s, m), lambda j: (0, 0)), pl.BlockSpec((s, tn), lambda j: (0, j))]
        + [_whole()] * len(after),
        out_specs=[pl.BlockSpec((1, m, tn), lambda j: (j, 0, 0)), HBM_SPEC],
        out_shape=[jax.ShapeDtypeStruct((nb, m, tn), F32), jax.ShapeDtypeStruct((nb, hm, tn), BF16)],
        scratch_shapes=[pltpu.VMEM((nb, hm, tn), BF16), pltpu.SemaphoreType.DMA((nb,)),
                        pltpu.SemaphoreType.DMA((nb,))],
        compiler_params=pltpu.CompilerParams(dimension_semantics=("arbitrary",), vmem_limit_bytes=VMEM_LIMIT,
                                             collective_id=barrier_id),
    )(a, b, *after)


class _NoComm:
    def __init__(self, win4, wout, wup4, wdown):
        self.weights = dict(w_in=win4, w_out=wout, w_up=wup4, w_down=wdown)
        self.grads = {}

    def weight(self, name):
        return self.weights[name]

    def riders(self, call):
        return ()

    def after(self, call):
        return ()

    def landed(self, call, results, outs):
        pass

    def small_gradients(self, small, packed):
        pass

    def gradient(self, name, g32, g16):
        self.grads[name] = (g32, g16)

    def wgrad_in(self, xb, dproj):
        (g32, g16), _ = _wgrad(xb, dproj, D_MODEL, IN_SH, "wgrad_in", True, True)
        self.gradient("w_in", g32, g16)


def _local_step(x, target, cw, cb, wpool_b, pscale, g1, b1, g2, b2, comm):
    s = x.shape[0]
    ts_a = min(512, s)
    ts_f = min(256, s)
    mask, qd, kd = _decay_tables()
    cosf, sinf = _rope_tables(s)

    def run(call, fn, *args):
        outs, res = fn(*args, riders=comm.riders(call), after=comm.after(call))
        comm.landed(call, res, outs)
        return outs

    xb, q, k, v, g, pooled, cat = run("proj_pool", _proj_pool, x, comm.weight("w_in"), cosf, sinf, wpool_b,
                                      pscale, ts_a)
    ret, cat, states = run("retention_fwd", _retention_fwd, q, k, v, g, cat, mask, qd, kd)
    wout = comm.weight("w_out")
    xhat1, rstd1, h1b = run("outproj_ln1", _outproj_ln1, x, cat, wout, g1, b1, ts_a)
    wup4, wdown = comm.weight("w_up"), comm.weight("w_down")
    ub, act, sd, dz2, dz2b, loss, dg2, db2 = _ffn_fwd_loss(xhat1, h1b, target, wup4, wdown, cw, cb, g1, b1, g2, b2,
                                                           ts_f)

    dub, dz1, dz1b, dg1, db1, dcw, dcb = _ffn_bwd(dz2, dz2b, ub, sd, xhat1, rstd1, wup4, wdown, cw, g1, ts_f)
    half = D_MODEL // 2
    comm.gradient("w_up", *run("wgrad_up", _wgrad, h1b, dub, half, UP_SH, "wgrad_up", True, False))
    comm.gradient("w_out", *run("wgrad_out", _wgrad, cat, dz1b, D_MODEL, half, "wgrad_out", False, True))
    comm.gradient("w_down", *run("wgrad_down", _wgrad, act, dz2b, D_FF // 2, half, "wgrad_down", False, True))
    dret, dgp, dwp, dps, packed = run("mix_bwd", _mix_bwd, dz1b, pooled, ret, g, wout, wpool_b, pscale, loss,
                                      [dcb, dg1, db1, dg2, db2], ts_a)
    small = dict(w_pool=dwp, pool_scale=dps, ln1_g=dg1, ln1_b=db1, conv_w=dcw, conv_b=dcb,
                 ln2_g=dg2, ln2_b=db2)
    comm.small_gradients(small, packed)
    mask_both = jnp.stack([mask, jnp.swapaxes(mask, 1, 2)])
    dproj, = run("retention_bwd", _retention_bwd, q, k, v, dret, dgp, states, mask_both, qd, kd, cosf, sinf)
    comm.wgrad_in(xb, dproj)
    (grad_x,), _ = _dx(dz1, dproj, comm.weight("w_in"), ts_a, after=comm.after("dx"))
    return loss, grad_x, small


CAST_ROWS = 64
SHARD_SHAPES = ((D_MODEL, IN_SH), (OUT_SH, D_MODEL), (D_MODEL, UP_SH), (DOWN_SH, D_MODEL))
N_BIG = len(SHARD_SHAPES)
CW_SHARD = (3, 1, DOWN_SH)


def _mesh_pos():
    return lax.axis_index("x"), lax.axis_index("y"), lax.axis_index("c")


def _other_chips(x, y):
    return [(1 - x, y), (x, 1 - y), (1 - x, 1 - y)]


def _shake_hands(peers):
    x, y, c = _mesh_pos()
    others = [(x, y, 1 - c)] if peers in ("sibling", "both", "all") else []
    if peers in ("chips", "both", "all"):
        others += [(chip[0], chip[1], c) for chip in _other_chips(x, y)]
    if peers == "all":
        others += [(chip[0], chip[1], 1 - c) for chip in _other_chips(x, y)]
    barrier = pltpu.get_barrier_semaphore()
    for peer in others:
        pl.semaphore_signal(barrier, inc=1, device_id=peer, device_id_type=MESH)
    pl.semaphore_wait(barrier, len(others))


def _half_rows(w, which):
    hr = SHARD_SHAPES[w][0] // 2
    return pl.ds(pl.multiple_of(which * hr, 16), hr)


def _gather_weights(shards, cw_shard, wpool, full):
    def body(*refs):
        in_refs = refs[:N_BIG]
        cw_ref, wpool_ref = refs[N_BIG:N_BIG + 2]
        out_refs = refs[N_BIG + 2:2 * N_BIG + 2]
        cwo_ref, wpool_b_ref = refs[2 * N_BIG + 2:2 * N_BIG + 4]
        stage = refs[2 * N_BIG + 4:3 * N_BIG + 4]
        raw = refs[3 * N_BIG + 4:4 * N_BIG + 4 - len(full)]
        send_sems, recv_sems, fsend_sems, frecv_sems, cw_send, cw_recv, local_sems, load_sems = \
            refs[4 * N_BIG + 4 - len(full):]
        x, y, c = _mesh_pos()
        j0 = 2 * x + y
        chips = _other_chips(x, y)

        fetched = [w for w in range(N_BIG) if w not in full]
        f32 = {w: in_refs[w] for w in full}
        loads = []
        for n, w in enumerate(fetched):
            f32[w] = raw[n]
            loads.append(pltpu.make_async_copy(in_refs[w], raw[n], load_sems.at[n]))
            loads[-1].start()

        def cast_to_stage(w):
            def cast(i, carry):
                rows = pl.ds(pl.multiple_of(i * CAST_ROWS, CAST_ROWS), CAST_ROWS)
                stage[w][rows, :] = f32[w][rows, :].astype(BF16)
                return carry
            lax.fori_loop(0, SHARD_SHAPES[w][0] // CAST_ROWS, cast, 0)

        for w in full:
            cast_to_stage(w)

        jx, jy, jd = 2 * (1 - x) + y, 2 * x + (1 - y), 2 * (1 - x) + (1 - y)
        neighbours = [((1 - x, y, c), jx), ((x, 1 - y, c), jy)]
        passed = jnp.where(c == 0, jx, jy)
        pass_to = (jnp.where(c == 0, x, 1 - x), jnp.where(c == 0, 1 - y, y), c)

        def nbr(w, k, block):
            return pltpu.make_async_remote_copy(
                src_ref=stage[w].at[_half_rows(w, c), :], dst_ref=out_refs[w].at[block, _half_rows(w, c), :],
                send_sem=send_sems.at[w, k], recv_sem=recv_sems.at[w, k],
                device_id=neighbours[k][0], device_id_type=MESH)

        def relay(w, block):
            return pltpu.make_async_remote_copy(
                src_ref=out_refs[w].at[passed, _half_rows(w, c), :],
                dst_ref=out_refs[w].at[block, _half_rows(w, c), :],
                send_sem=send_sems.at[w, 2], recv_sem=recv_sems.at[w, 2],
                device_id=pass_to, device_id_type=MESH)

        def d2d(w, k, block, half):
            return pltpu.make_async_remote_copy(
                src_ref=out_refs[w].at[block, _half_rows(w, half), :],
                dst_ref=out_refs[w].at[block, _half_rows(w, half), :],
                send_sem=fsend_sems.at[w, k], recv_sem=frecv_sems.at[w, k],
                device_id=(x, y, 1 - c), device_id_type=MESH)

        def conv(k, block):
            chip = chips[k]
            return pltpu.make_async_remote_copy(
                src_ref=cw_ref, dst_ref=cwo_ref.at[block], send_sem=cw_send.at[k], recv_sem=cw_recv.at[k],
                device_id=(chip[0], chip[1], c), device_id_type=MESH)

        sent = [nbr(w, k, j0) for w in full for k in range(2)] + [conv(k, j0) for k in range(3)]
        for cp in sent:
            cp.start()
        for n, w in enumerate(fetched):
            loads[n].wait()
            cast_to_stage(w)
        local = [pltpu.make_async_copy(stage[w], out_refs[w].at[j0], local_sems.at[w]) for w in range(N_BIG)]
        local.append(pltpu.make_async_copy(cw_ref, cwo_ref.at[j0], local_sems.at[N_BIG]))
        for cp in local:
            cp.start()
        wpool_b_ref[...] = wpool_ref[...].astype(BF16)
        for w in full:
            for k, (_, block) in enumerate(neighbours):
                nbr(w, k, block).wait_recv()
            later = [relay(w, passed)] + [d2d(w, k, block, c) for k, (_, block) in enumerate(neighbours)]
            for cp in later:
                cp.start()
            sent += later
        for w in full:
            relay(w, jd).wait_recv()
            fw = d2d(w, 2, jd, c)
            fw.start()
            sent.append(fw)
        for w in full:
            for k, block in enumerate([jx, jy, jd]):
                d2d(w, k, block, 1 - c).wait_recv()
        for k, chip in enumerate(chips):
            conv(k, 2 * chip[0] + chip[1]).wait_recv()
        for cp in sent:
            cp.wait_send()
        for cp in local:
            cp.wait()

    out_shape = [jax.ShapeDtypeStruct((N_SHARD,) + shp, BF16) for shp in SHARD_SHAPES]
    out_shape.append(jax.ShapeDtypeStruct((N_SHARD,) + CW_SHARD, F32))
    out_shape.append(jax.ShapeDtypeStruct(wpool.shape, BF16))
    return pl.pallas_call(
        body, name="gather_weights",
        in_specs=[_whole() if w in full else HBM_SPEC for w in range(N_BIG)] + [_whole()] * 2,
        out_specs=[HBM_SPEC] * (N_BIG + 1) + [_whole()],
        out_shape=out_shape,
        scratch_shapes=[pltpu.VMEM(shp, BF16) for shp in SHARD_SHAPES]
        + [pltpu.VMEM(shp, F32) for w, shp in enumerate(SHARD_SHAPES) if w not in full] + [
            pltpu.SemaphoreType.DMA((N_BIG, 3)), pltpu.SemaphoreType.DMA((N_BIG, 3)),
            pltpu.SemaphoreType.DMA((N_BIG, 3)), pltpu.SemaphoreType.DMA((N_BIG, 3)),
            pltpu.SemaphoreType.DMA((3,)), pltpu.SemaphoreType.DMA((3,)),
            pltpu.SemaphoreType.DMA((N_BIG + 1,)), pltpu.SemaphoreType.DMA((N_BIG - len(full),))],
        compiler_params=pltpu.CompilerParams(vmem_limit_bytes=VMEM_LIMIT),
    )(*shards, cw_shard, wpool)


def _gather_rider(arrays, ops, handshake=None):
    ws = sorted(arrays)

    def make(inplace, srcs, lands, send_sems, recv_sems):
        del srcs, lands
        x, y, c = _mesh_pos()
        j0, jx, jy, jd = 2 * x + y, 2 * (1 - x) + y, 2 * x + (1 - y), 2 * (1 - x) + (1 - y)
        x_nbr, y_nbr, sibling = (1 - x, y, c), (x, 1 - y, c), (x, y, 1 - c)
        starts, waits = [], []
        for n, (kind, w, (r0, nr)) in enumerate(ops):
            ref = inplace[ws.index(w)]
            hr = SHARD_SHAPES[w][0] // 2
            rows = lambda core: pl.ds(pl.multiple_of(core * hr + r0, 16), nr)
            mine, theirs = rows(c), rows(1 - c)
            if kind == "ici":
                moves = [(ref.at[j0, mine, :], x_nbr, ref.at[jx, mine, :]),
                         (ref.at[j0, mine, :], y_nbr, ref.at[jy, mine, :]),
                         (ref.at[j0, mine, :], (1 - x, 1 - y, c), ref.at[jd, mine, :])]
            elif kind == "nbr":
                moves = [(ref.at[j0, mine, :], x_nbr, ref.at[jx, mine, :]),
                         (ref.at[j0, mine, :], y_nbr, ref.at[jy, mine, :])]
            elif kind == "relay":
                passed = jnp.where(c == 0, jx, jy)
                to = (jnp.where(c == 0, x, 1 - x), jnp.where(c == 0, 1 - y, y), c)
                moves = [(ref.at[passed, mine, :], to, ref.at[jd, mine, :])]
            else:
                blocks = dict(d2d=[jx, jy, jd], d2d_nbr=[jx, jy], d2d_diag=[jd])[kind]
                moves = [(ref.at[b, mine, :], sibling, ref.at[b, theirs, :]) for b in blocks]
            for k, (src, to, landing) in enumerate(moves):
                sems = dict(send_sem=send_sems.at[3 * n + k], recv_sem=recv_sems.at[3 * n + k],
                            device_id=to, device_id_type=MESH)
                send = pltpu.make_async_remote_copy(src_ref=src, dst_ref=src, **sems)
                arrival = pltpu.make_async_remote_copy(src_ref=src, dst_ref=landing, **sems)
                starts.append(send)
                waits += [arrival.wait_recv, send.wait_send]
        return starts, waits

    return _Rider([arrays[w] for w in ws], [], [], 3 * len(ops), make, handshake)


def _whole_half(w):
    return (0, SHARD_SHAPES[w][0] // 2)


def _pair_rider(ws, g16s):
    def make(inplace, srcs, lands, send_sems, recv_sems):
        del inplace
        x, y, c = _mesh_pos()
        copies = [pltpu.make_async_remote_copy(
            src_ref=srcs[i].at[:, _half_rows(w, 1 - c), :], dst_ref=lands[i],
            send_sem=send_sems.at[i], recv_sem=recv_sems.at[i], device_id=(x, y, 1 - c), device_id_type=MESH)
            for i, w in enumerate(ws)]
        return copies, [cp.wait for cp in copies]

    lands = [jax.ShapeDtypeStruct((N_SHARD, SHARD_SHAPES[w][0] // 2, SHARD_SHAPES[w][1]), BF16) for w in ws]
    return _Rider([], g16s, lands, len(ws), make)


def _chip_rider(ws, p16s):
    def make(inplace, srcs, lands, send_sems, recv_sems):
        del inplace
        x, y, c = _mesh_pos()
        copies = []
        for i in range(len(ws)):
            for k, chip in enumerate(_other_chips(x, y)):
                copies.append(pltpu.make_async_remote_copy(
                    src_ref=srcs[i].at[2 * chip[0] + chip[1]], dst_ref=lands[i].at[k],
                    send_sem=send_sems.at[3 * i + k], recv_sem=recv_sems.at[3 * i + k],
                    device_id=(chip[0], chip[1], c), device_id_type=MESH))
        return copies, [cp.wait for cp in copies]

    lands = [jax.ShapeDtypeStruct((3, SHARD_SHAPES[w][0] // 2, SHARD_SHAPES[w][1]), BF16) for w in ws]
    return _Rider([], p16s, lands, 3 * len(ws), make)


def _final_rider(halves):
    def make(inplace, srcs, lands, send_sems, recv_sems):
        del inplace
        x, y, c = _mesh_pos()
        copies = [pltpu.make_async_remote_copy(
            src_ref=srcs[i], dst_ref=lands[i], send_sem=send_sems.at[i], recv_sem=recv_sems.at[i],
            device_id=(x, y, 1 - c), device_id_type=MESH) for i in range(len(halves))]
        return copies, [cp.wait for cp in copies]

    return _Rider([], halves, [jax.ShapeDtypeStruct(h.shape, h.dtype) for h in halves], len(halves), make)


N_DEV = 2 * N_SHARD


def _device_number(x, y, c):
    return 2 * (2 * x + y) + c


def _small_all_rider(own):
    n = len(own)

    def make(inplace, srcs, lands, send_sems, recv_sems):
        del inplace
        x, y, c = _mesh_pos()
        peers = [(x, y, 1 - c)] + [(chip[0], chip[1], core) for chip in _other_chips(x, y) for core in (c, 1 - c)]
        copies = []
        for i in range(n):
            for k, peer in enumerate(peers):
                copies.append(pltpu.make_async_remote_copy(
                    src_ref=srcs[i], dst_ref=lands[i].at[_device_number(x, y, c)],
                    send_sem=send_sems.at[(N_DEV - 1) * i + k], recv_sem=recv_sems.at[(N_DEV - 1) * i + k],
                    device_id=peer, device_id_type=MESH))
        return copies, [cp.wait for cp in copies]

    lands = [jax.ShapeDtypeStruct((N_DEV,) + a.shape, a.dtype) for a in own]
    return _Rider([], own, lands, (N_DEV - 1) * n, make)


def _comm_only(name, riders):
    _, res = _call(lambda: None, name=name, grid=(), in_specs=[], out_specs=[], out_shape=[], operands=(),
                   riders=riders)
    return res


class _SemList:
    def __init__(self, refs):
        self.at = list(refs)


def _merged_rider(riders):
    srcs = [a for r in riders for a in r.srcs]
    lands = [a for r in riders for a in r.lands]

    def make(inplace, src_refs, land_refs, send_sems, recv_sems):
        starts, waits = [], []
        s0 = l0 = c0 = 0
        for r in riders:
            part = r.make(inplace, src_refs[s0:s0 + len(r.srcs)], land_refs[l0:l0 + len(r.lands)],
                          _SemList(send_sems.at[c0:c0 + r.n_copies]), _SemList(recv_sems.at[c0:c0 + r.n_copies]))
            starts += part[0]
            waits += part[1]
            s0, l0, c0 = s0 + len(r.srcs), l0 + len(r.lands), c0 + r.n_copies
        return starts, waits

    return _Rider([], srcs, lands, sum(r.n_copies for r in riders), make)


def _split_start(name, rider, handshake=None):
    assert not rider.inplace
    ns, nl, n = len(rider.srcs), len(rider.lands), rider.n_copies
    barrier_id, peers = handshake if handshake is not None else (None, None)

    def body(*refs):
        if handshake is not None:
            _shake_hands(peers)
        srcs, lands = refs[:ns], refs[ns:ns + nl]
        sems = refs[ns + nl:ns + nl + 2 * n]
        token = refs[-1]
        starts, _ = rider.make([], srcs, lands, _SemList(sems[:n]), _SemList(sems[n:]))
        for cp in starts:
            cp.start()
        token[...] = jnp.zeros_like(token)

    buffers = [pltpu.with_memory_space_constraint(a, pltpu.HBM) for a in rider.srcs]
    buffers += [pltpu.with_memory_space_constraint(lax.empty(s.shape, s.dtype), pltpu.HBM) for s in rider.lands]
    hbm = pl.BlockSpec(memory_space=pltpu.HBM)
    sem = pl.BlockSpec(memory_space=pltpu.SEMAPHORE)
    outs = pl.pallas_call(
        body, name=name,
        out_shape=tuple([pltpu.SemaphoreType.DMA(())] * (2 * n) + [pltpu.HBM(b.shape, b.dtype) for b in buffers]
                        + [jax.ShapeDtypeStruct((8, 128), F32)]),
        in_specs=[hbm] * (ns + nl),
        out_specs=tuple([sem] * (2 * n) + [hbm] * (ns + nl) + [_whole()]),
        input_output_aliases={i: 2 * n + i for i in range(ns + nl)},
        compiler_params=pltpu.CompilerParams(has_side_effects=pltpu.SideEffectType.DATAFLOW_SIDE_EFFECTING,
                                             collective_id=barrier_id),
    )(*buffers)
    return (rider, outs[:2 * n], outs[2 * n:2 * n + ns + nl]), outs[-1]


def _split_parts(state, riders):
    merged, sems, buffers = state
    n, ns = merged.n_copies, len(merged.srcs)
    parts, s0, l0, c0 = [], 0, 0, 0
    for r in riders:
        parts.append((r, list(sems[c0:c0 + r.n_copies]) + list(sems[n + c0:n + c0 + r.n_copies]),
                      list(buffers[s0:s0 + len(r.srcs)]) + list(buffers[ns + l0:ns + l0 + len(r.lands)])))
        s0, l0, c0 = s0 + len(r.srcs), l0 + len(r.lands), c0 + r.n_copies
    return parts


def _split_wait(name, state, after):
    rider, sems, buffers = state
    ns, nl, n = len(rider.srcs), len(rider.lands), rider.n_copies

    def body(*refs):
        srcs, lands = refs[:ns], refs[ns:ns + nl]
        sem_refs = refs[ns + nl:ns + nl + 2 * n]
        _, waits = rider.make([], srcs, lands, _SemList(sem_refs[:n]), _SemList(sem_refs[n:]))
        for wait in waits:
            wait()

    hbm = pl.BlockSpec(memory_space=pltpu.HBM)
    sem = pl.BlockSpec(memory_space=pltpu.SEMAPHORE)
    outs = pl.pallas_call(
        body, name=name,
        out_shape=tuple(pltpu.HBM(b.shape, b.dtype) for b in buffers),
        in_specs=[hbm] * (ns + nl) + [sem] * (2 * n) + [HBM_SPEC],
        out_specs=tuple([hbm] * (ns + nl)),
        input_output_aliases={i: i for i in range(ns + nl)},
        compiler_params=pltpu.CompilerParams(has_side_effects=pltpu.SideEffectType.DATAFLOW_SIDE_EFFECTING),
    )(*buffers, *sems, after)
    return list(outs[:ns]), list(outs[ns:])


def _pair_sum(pos, ws, g32s, recvs):
    n = len(ws)

    def body(pos_ref, *refs):
        del pos_ref
        g_refs, r_refs = refs[:n], refs[n:2 * n]
        p32_refs, p16_refs = refs[2 * n:3 * n], refs[3 * n:]
        x, y, _ = _mesh_pos()
        for i in range(n):
            tot = g_refs[i][...] + r_refs[i][...].astype(F32)
            p16_refs[i][...] = tot.astype(BF16)

            @pl.when(pl.program_id(0) == 2 * x + y)
            def _(i=i, tot=tot):
                p32_refs[i][...] = tot

    halves = [(SHARD_SHAPES[w][0] // 2, SHARD_SHAPES[w][1]) for w in ws]
    own = [pl.BlockSpec((None, None) + h, lambda j, pos_ref: (j, pos_ref[0], 0, 0)) for h in halves]
    blk = [pl.BlockSpec((None,) + h, lambda j, pos_ref: (j, 0, 0)) for h in halves]
    mine = [pl.BlockSpec(h, lambda j, pos_ref: (0, 0)) for h in halves]
    g4 = [g.reshape((N_SHARD, 2) + h) for g, h in zip(g32s, halves)]
    outs = pl.pallas_call(
        body, name="pair_sum_" + "_".join(str(w) for w in ws),
        grid_spec=pltpu.PrefetchScalarGridSpec(
            num_scalar_prefetch=1, grid=(N_SHARD,), in_specs=own + blk, out_specs=mine + blk),
        out_shape=[jax.ShapeDtypeStruct(h, F32) for h in halves]
        + [jax.ShapeDtypeStruct((N_SHARD,) + h, BF16) for h in halves],
        compiler_params=_params(("arbitrary",)),
    )(pos, *g4, *recvs)
    return outs[:n], outs[n:]


def _chip_sum(p32s, recvs):
    parts = 2

    def body(*refs):
        p_refs, r_refs, f_refs = refs[:N_BIG], refs[N_BIG:2 * N_BIG], refs[2 * N_BIG:]
        for w in range(N_BIG):
            f_refs[w][...] = ((p_refs[w][...] + r_refs[w][0].astype(F32)) + r_refs[w][1].astype(F32)) \
                + r_refs[w][2].astype(F32)

    quarters = [(r // 2 // parts, cc) for r, cc in SHARD_SHAPES]
    own = [pl.BlockSpec(qt, lambda i: (i, 0)) for qt in quarters]
    rcv = [pl.BlockSpec((3,) + qt, lambda i: (0, i, 0)) for qt in quarters]
    out = [pl.BlockSpec(qt, lambda i: (i, 0)) for qt in quarters]
    return pl.pallas_call(
        body, name="chip_sum", grid=(parts,), in_specs=own + rcv, out_specs=out,
        out_shape=[jax.ShapeDtypeStruct((r // 2, cc), F32) for r, cc in SHARD_SHAPES],
        compiler_params=_params(("arbitrary",)),
    )(*p32s, *recvs)


def _adamw(w, g, m, v):
    m_new = ADAM_B1 * m + (1.0 - ADAM_B1) * g
    v_new = ADAM_B2 * v + (1.0 - ADAM_B2) * (g * g)
    m_hat = m_new / (1.0 - ADAM_B1 ** ADAM_STEP)
    v_hat = v_new / (1.0 - ADAM_B2 ** ADAM_STEP)
    delta = -ADAM_LR * (m_hat / (jnp.sqrt(v_hat) + ADAM_EPS) + ADAM_WD * w)
    return delta, m_new, v_new


def _adam_half(name, pos, grads, ws, ms, vs, into=None):
    nb = 4
    which = (lambda ref: ref[0]) if into is None else (lambda ref: 1 - ref[0])

    def body(which_ref, *refs):
        del which_ref
        groups = [refs[i * N_BIG:(i + 1) * N_BIG] for i in range(4)]
        g_refs, w_refs, m_refs, v_refs = groups
        go_refs, do_refs, mo_refs, vo_refs = [refs[len(refs) - (4 - i) * N_BIG:len(refs) - (3 - i) * N_BIG]
                                              for i in range(4)]
        for w in range(N_BIG):
            g = g_refs[w][...]
            delta, m_new, v_new = _adamw(w_refs[w][...], g, m_refs[w][...], v_refs[w][...])
            go_refs[w][...] = g
            do_refs[w][...] = delta
            mo_refs[w][...] = m_new
            vo_refs[w][...] = v_new

    blocks = [(r // 2 // nb, cc) for r, cc in SHARD_SHAPES]
    half = [pl.BlockSpec(b, lambda i, which_ref: (i, 0)) for b in blocks]
    full = [pl.BlockSpec((None,) + b, lambda i, which_ref: (0, which(which_ref) * nb + i, 0)) for b in blocks]
    shapes = [jax.ShapeDtypeStruct((1,) + shp, F32) for shp in SHARD_SHAPES]
    carried = [] if into is None else [a for kind in into for a in kind]
    first = 1 + 4 * N_BIG
    outs = pl.pallas_call(
        body, name=name,
        grid_spec=pltpu.PrefetchScalarGridSpec(
            num_scalar_prefetch=1, grid=(nb,), in_specs=half + full * 3 + [HBM_SPEC] * len(carried),
            out_specs=full * 4),
        out_shape=shapes * 4,
        input_output_aliases={first + i: i for i in range(len(carried))},
        compiler_params=_params(("arbitrary",)),
    )(pos, *grads, *ws, *ms, *vs, *carried)
    return [outs[i * N_BIG:(i + 1) * N_BIG] for i in range(4)]


SMALL_ROWS = 8
ROW_CONV_B, ROW_POOL_SCALE, ROW_LN1_G, ROW_LN1_B, ROW_LN2_G, ROW_LN2_B, ROW_LOSS = range(7)
SMALL_VECS = ((ROW_CONV_B, D_FF), (ROW_POOL_SCALE, POOL_W), (ROW_LN1_G, D_MODEL), (ROW_LN1_B, D_MODEL),
              (ROW_LN2_G, D_MODEL), (ROW_LN2_B, D_MODEL))


def _small_adam(all_a, all_b, all_c, own_a, own_b, own_c, wp, cwp, vec_ws, m_wp, m_cwp, vec_ms,
                v_wp, v_cwp, vec_vs):
    nv = len(SMALL_VECS)
    np_ = 2 + nv

    def body(*refs):
        all_a_ref, all_b_ref, all_c_ref, own_a_ref, own_b_ref, own_c_ref = refs[0:6]
        refs = refs[3:]
        w_all, m_all, v_all = (refs[3 + i * np_:3 + (i + 1) * np_] for i in range(3))
        loss_out = refs[3 + 3 * np_]
        outs = refs[4 + 3 * np_:]
        x, y, c = _mesh_pos()
        j0 = 2 * x + y
        me = _device_number(x, y, c)

        def total(sent, own):
            by_dev = [jnp.where(me == d, own, sent(d)) for d in range(N_DEV)]
            chips = [by_dev[2 * j] + by_dev[2 * j + 1] for j in range(N_SHARD)]
            return ((chips[0] + chips[1]) + chips[2]) + chips[3]

        tot_a = total(lambda d: all_a_ref[d], own_a_ref[...])
        tot_b = total(lambda d: all_b_ref[d], own_b_ref[...])
        tot_c = total(lambda d: all_c_ref[d, j0], own_c_ref[j0])
        loss_out[...] = tot_b[ROW_LOSS:ROW_LOSS + 1, 0:1]
        grads = [tot_a, tot_c] + [tot_b[row:row + 1, 0:n] for row, n in SMALL_VECS]
        for p in range(np_):
            for at, g in ([(j, tot_c[j:j + 1]) for j in range(3)] if p == 1 else [(Ellipsis, grads[p])]):
                delta, m_new, v_new = _adamw(w_all[p][at], g, m_all[p][at], v_all[p][at])
                outs[p][at] = g
                outs[np_ + p][at] = delta
                outs[2 * np_ + p][at] = m_new
                outs[3 * np_ + p][at] = v_new

    pshapes = [wp.shape, CW_SHARD] + [wv.shape for wv in vec_ws]
    out_shape = [jax.ShapeDtypeStruct((1, 1), F32)] + [jax.ShapeDtypeStruct(s, F32) for s in pshapes] * 4
    outs = pl.pallas_call(
        body, name="small_adam",
        in_specs=[_whole()] * (6 + 3 * np_), out_specs=[_whole()] * len(out_shape), out_shape=out_shape,
        compiler_params=pltpu.CompilerParams(vmem_limit_bytes=VMEM_LIMIT),
    )(all_a, all_b, all_c, own_a, own_b, own_c, wp, cwp, *vec_ws, m_wp, m_cwp, *vec_ms, v_wp, v_cwp, *vec_vs)
    return outs[0], [outs[1 + i * np_:1 + (i + 1) * np_] for i in range(4)]


def kernel(x, w_in, w_pool, pool_scale, w_out, ln1_g, ln1_b, w_up, conv_w, conv_b, w_down, ln2_g, ln2_b, loss_target, m_w_in, m_w_pool, m_pool_scale, m_w_out, m_ln1_g, m_ln1_b, m_w_up, m_conv_w, m_conv_b, m_w_down, m_ln2_g, m_ln2_b, v_w_in, v_w_pool, v_pool_scale, v_w_out, v_ln1_g, v_ln1_b, v_w_up, v_conv_w, v_conv_b, v_w_down, v_ln2_g, v_ln2_b):
    pos = lax.axis_index("c").astype(jnp.int32).reshape(1)
    order = ("w_in", "w_out", "w_up", "w_down")
    w_in_i, w_out_i, w_up_i, w_down_i = range(N_BIG)
    vec_names = ("conv_b", "pool_scale", "ln1_g", "ln1_b", "ln2_g", "ln2_b")

    taps_first = lambda a: jnp.transpose(a, (1, 0, 2))
    gathered = _gather_weights([w_in[0], w_out[0], w_up[0], w_down[0]], taps_first(conv_w), w_pool[0], (w_in_i,))
    cw_full = jnp.transpose(gathered[N_BIG].reshape(N_SHARD, 3, DOWN_SH), (1, 0, 2)).reshape(3, D_FF)
    up_a, up_b, up_c = (0, 208), (208, 192), (400, 112)
    assert up_c[0] + up_c[1] == SHARD_SHAPES[w_up_i][0] // 2

    class MeshComm:
        def __init__(self):
            self.w = {i: gathered[i] for i in range(N_BIG)}
            self.g32, self.g16, self.p32, self.p16, self.recv_b = {}, {}, {}, {}, {}
            self.up_complete = False
            self.tokens, self.chips = {}, []

        def weight(self, name):
            i = order.index(name)
            if name == "w_up" and not self.up_complete:
                (arrs, _), = _comm_only("gather_up_last", [_gather_rider(
                    {i: self.w[i]}, [("d2d_diag", i, up_b), ("d2d", i, up_c)], (12, "sibling"))])
                self.w[i], self.up_complete = arrs[0], True
            full = self.w[i]
            return full.reshape(-1, full.shape[-1]) if name in ("w_out", "w_down") else full

        def _gather(self, ws, ops, handshake):
            return _gather_rider({w: self.w[w] for w in ws}, ops, handshake), ("w", ws)

        def _pair(self, ws):
            return _pair_rider(ws, [self.g16[w] for w in ws]), ("recv_a", ws)

        def _chip(self, ws):
            return _chip_rider(ws, [self.p16[w] for w in ws]), ("recv_b", ws)

        def plan(self, call):
            out_all, down_all = _whole_half(w_out_i), _whole_half(w_down_i)
            if call == "proj_pool":
                return [self._gather([w_out_i, w_up_i, w_down_i],
                                     [("ici", w_out_i, out_all), ("nbr", w_down_i, down_all),
                                      ("nbr", w_up_i, up_a)], (9, "chips"))]
            if call == "retention_fwd":
                return [self._gather([w_out_i, w_up_i, w_down_i],
                                     [("d2d", w_out_i, out_all),
                                      ("relay", w_down_i, down_all), ("d2d_nbr", w_down_i, down_all),
                                      ("relay", w_up_i, up_a), ("d2d_nbr", w_up_i, up_a), ("nbr", w_up_i, up_b)],
                                     (10, "both"))]
            if call == "outproj_ln1":
                return [self._gather([w_up_i, w_down_i],
                                     [("d2d_diag", w_down_i, down_all), ("d2d_diag", w_up_i, up_a),
                                      ("relay", w_up_i, up_b), ("d2d_nbr", w_up_i, up_b), ("ici", w_up_i, up_c)],
                                     (11, "both"))]
            return []

        def after(self, call):
            return tuple(self.tokens.pop(call, ()))

        def riders(self, call):
            self.pending = self.plan(call)
            return [r for r, _ in self.pending]

        def _start(self, name, rider, before, handshake):
            state, token = _split_start(name, rider, handshake)
            self.tokens.setdefault(before, []).append(token)
            return state

        def _finish_pair(self, name, state, ws, after):
            _, lands = _split_wait(name, state, after)
            self._finish_sum(ws, lands)

        def landed(self, call, results, outs):
            for (_, (slot, ws)), (inplace, lands) in zip(self.pending, results):
                for w, arr in zip(ws, inplace if len(inplace) else lands):
                    getattr(self, slot)[w] = arr
            if call == "wgrad_out":
                self._finish_pair("pair_exchange_up_wait", self.pair_up, [w_up_i], outs[1])
                self.chips.append(([w_up_i], self._start(
                    "chip_exchange_up_start", self._chip([w_up_i])[0], "wgrad_down", (5, "chips"))))
            if call == "mix_bwd":
                ws = [w_out_i, w_down_i]
                self._finish_pair("pair_exchange_out_down_wait", self.pair_out_down, ws, outs[0])

        def small_gradients(self, small, packed):
            dcw4 = jnp.transpose(small["conv_w"].reshape(3, N_SHARD, DOWN_SH), (1, 0, 2))
            own = [small["w_pool"], packed, dcw4]
            ws = [w_out_i, w_down_i]
            parts = [self._chip(ws)[0], _small_all_rider(own)]
            chip, self.small_all = _split_parts(
                self._start("chip_out_down_small_all_start", _merged_rider(parts), "retention_bwd",
                            (7, "all")), parts)
            self.chips.append((ws, chip))

        def gradient(self, name, g32, g16):
            w = order.index(name)
            shape = (N_SHARD,) + SHARD_SHAPES[w]
            self.g32[w], self.g16[w] = g32.reshape(shape), g16.reshape(shape)
            if name == "w_up":
                self.pair_up = self._start("pair_exchange_up_start", self._pair([w])[0], "wgrad_out",
                                           (1, "sibling"))
            if name == "w_down":
                self.pair_out_down = self._start("pair_exchange_out_down_start",
                                                 self._pair([w_out_i, w_down_i])[0], "mix_bwd", (2, "sibling"))

        def wgrad_in(self, xb, dproj):
            w = w_in_i
            g32, landed = _wgrad_send(xb, dproj, IN_SH, "wgrad_in", 4, after=self.after("wgrad_in"))
            self.g32[w] = g32
            self._finish_sum([w], [landed])
            self.chips.append(([w], self._start("chip_exchange_in_start", self._chip([w])[0], "dx", (8, "chips"))))

        def _finish_sum(self, ws, lands):
            p32s, p16s = _pair_sum(pos, ws, [self.g32[w] for w in ws], lands)
            for w, p32, p16 in zip(ws, p32s, p16s):
                self.p32[w], self.p16[w] = p32, p16

        def finish(self, after):
            for n, (ws, state) in enumerate(self.chips):
                _, lands = _split_wait("chip_exchange_wait_%d" % n, state, after)
                for w, arr in zip(ws, lands):
                    self.recv_b[w] = arr
            own, sent = _split_wait("small_all_wait", self.small_all, after)
            return list(sent) + list(own)

    comm = MeshComm()
    loss, grad_x, small = _local_step(x[0], loss_target[0], cw_full, conv_b, gathered[N_BIG + 1], pool_scale,
                                      ln1_g, ln1_b, ln2_g, ln2_b, comm)

    given = dict(w_pool=w_pool, pool_scale=pool_scale, ln1_g=ln1_g, ln1_b=ln1_b, conv_w=conv_w, conv_b=conv_b,
                 ln2_g=ln2_g, ln2_b=ln2_b)
    given_m = dict(w_pool=m_w_pool, pool_scale=m_pool_scale, ln1_g=m_ln1_g, ln1_b=m_ln1_b, conv_w=m_conv_w,
                   conv_b=m_conv_b, ln2_g=m_ln2_g, ln2_b=m_ln2_b)
    given_v = dict(w_pool=v_w_pool, pool_scale=v_pool_scale, ln1_g=v_ln1_g, ln1_b=v_ln1_b, conv_w=v_conv_w,
                   conv_b=v_conv_b, ln2_g=v_ln2_g, ln2_b=v_ln2_b)
    args = []
    for src in (given, given_m, given_v):
        args += [src["w_pool"][0], taps_first(src["conv_w"]), [src[n] for n in vec_names]]
    small_sums = comm.finish(grad_x)
    loss_tot, small_out = _small_adam(*small_sums, *args)
    every = range(N_BIG)
    mine = _chip_sum([comm.p32[w] for w in every], [comm.recv_b[w] for w in every])
    final_state, _ = _split_start("pair_exchange_f32_start", _final_rider(mine), (3, "sibling"))
    mine = final_state[2][:N_BIG]
    big = ([w_in, w_out, w_up, w_down], [m_w_in, m_w_out, m_w_up, m_w_down], [v_w_in, v_w_out, v_w_up, v_w_down])
    own_half = _adam_half("adam_own_half", pos, mine, *big)
    _, theirs = _split_wait("pair_exchange_f32_wait", final_state, own_half[0][0])
    big_out = _adam_half("adam_other_half", pos, theirs, *big, into=own_half)

    names = ("w_in", "w_pool", "pool_scale", "w_out", "ln1_g", "ln1_b", "w_up", "conv_w", "conv_b", "w_down",
             "ln2_g", "ln2_b")
    small_names = ("w_pool", "conv_w") + vec_names
    result = [loss_tot.reshape(()), grad_x[None]]
    for kind in range(4):
        for n in names:
            if n in order:
                result.append(big_out[kind][order.index(n)])
            else:
                val = small_out[kind][small_names.index(n)]
                if n == "conv_w":
                    val = taps_first(val)
                elif n == "w_pool":
                    val = val[None]
                result.append(val)
    return tuple(result)
```

```python
import functools

import numpy as np
import jax
import jax.numpy as jnp
from jax import lax
from jax.experimental import pallas as pl
from jax.experimental.pallas import tpu as pltpu

F32 = jnp.float32
BF16 = jnp.bfloat16

D_MODEL = 1024
HEADS = 4
HEAD_DIM = 128
RET_W = HEADS * HEAD_DIM
POOL_WINDOWS = (2, 4, 8, 16)
POOL_W = 512
IN_W = 4 * RET_W + POOL_W
D_FF = 2816
N_SHARD = 4
IN_SH = IN_W // N_SHARD
UP_SH = 2 * D_FF // N_SHARD
DOWN_SH = D_FF // N_SHARD
OUT_SH = D_MODEL // N_SHARD
ROPE_BASE = 10000.0
LN_EPS = 1e-5
RMS_EPS = 1e-6
ALPHA = 2.0 ** 0.25
K_SCALE = HEAD_DIM ** -0.5
SUPER = 256
CHUNK = 64
POOL_HALO = 16
CONV_HALO = 8
FFN_STRIP = 128
LN_ROWS = 32

ADAM_LR = 0.001
ADAM_B1 = 0.9
ADAM_B2 = 0.999
ADAM_EPS = 1e-08
ADAM_WD = 0.01
ADAM_STEP = 10

MESH = pl.DeviceIdType.MESH
VMEM_LIMIT = 56 * 1024 * 1024


def _dot(a, b):
    return jnp.dot(a, b, preferred_element_type=F32)


def _dot_nt(a, b):
    return lax.dot_general(a, b, (((1,), (1,)), ((), ())), preferred_element_type=F32)


def _dot_tn(a, b):
    return lax.dot_general(a, b, (((0,), (0,)), ((), ())), preferred_element_type=F32)


def _sigmoid(x):
    return 1.0 / (1.0 + jnp.exp(-x))


def _params(sem):
    return pltpu.CompilerParams(dimension_semantics=sem, vmem_limit_bytes=VMEM_LIMIT)


def _whole():
    return pl.BlockSpec(memory_space=pltpu.VMEM)


HBM_SPEC = pl.BlockSpec(memory_space=pl.ANY)


class _Rider:
    def __init__(self, inplace, srcs, lands, n_copies, make, handshake=None):
        self.inplace, self.srcs, self.lands, self.n_copies, self.make = list(inplace), list(srcs), list(lands), n_copies, make
        self.handshake = handshake


def _call(body, *, name, grid, in_specs, out_specs, out_shape, operands, scratch_shapes=(), sem=(),
          aliases=None, riders=(), after=()):
    n_in, n_out, n_scr = len(in_specs), len(out_shape), len(scratch_shapes)
    in_specs, out_specs, out_shape = list(in_specs), list(out_specs), list(out_shape)
    operands, scratch_shapes, aliases = list(operands), list(scratch_shapes), dict(aliases or {})
    in_specs += [_whole()] * len(after)
    operands += list(after)
    shakes = [r.handshake for r in riders if r.handshake is not None]
    assert len(shakes) <= 1
    for r in riders:
        for a in r.inplace:
            aliases[len(in_specs)] = len(out_shape)
            in_specs.append(HBM_SPEC)
            operands.append(a)
            out_specs.append(HBM_SPEC)
            out_shape.append(jax.ShapeDtypeStruct(a.shape, a.dtype))
        for a in r.srcs:
            in_specs.append(HBM_SPEC)
            operands.append(a)
        for shp in r.lands:
            out_specs.append(HBM_SPEC)
            out_shape.append(shp)
        scratch_shapes += [pltpu.SemaphoreType.DMA((r.n_copies,)), pltpu.SemaphoreType.DMA((r.n_copies,))]

    def full(*refs):
        ins = refs[:n_in]
        at = n_in + len(after)
        r_srcs = []
        for r in riders:
            at += len(r.inplace)
            r_srcs.append(refs[at:at + len(r.srcs)])
            at += len(r.srcs)
        outs = refs[at:at + n_out]
        at += n_out
        r_outs = []
        for r in riders:
            r_outs.append((refs[at:at + len(r.inplace)], refs[at + len(r.inplace):at + len(r.inplace) + len(r.lands)]))
            at += len(r.inplace) + len(r.lands)
        scr = refs[at:at + n_scr]
        at += n_scr
        r_sems = [refs[at + 2 * i:at + 2 * i + 2] for i in range(len(riders))]

        def copies():
            return [r.make(r_outs[i][0], r_srcs[i], r_outs[i][1], r_sems[i][0], r_sems[i][1])
                    for i, r in enumerate(riders)]

        def start():
            if shakes:
                _shake_hands(shakes[0][1])
            for starts, _ in copies():
                for cp in starts:
                    cp.start()

        def finish():
            for _, waits in copies():
                for wait in waits:
                    wait()

        if riders and grid:
            first = functools.reduce(jnp.logical_and, [pl.program_id(d) == 0 for d in range(len(grid))])
            last = functools.reduce(jnp.logical_and, [pl.program_id(d) == grid[d] - 1 for d in range(len(grid))])
            pl.when(first)(start)
            body(*ins, *outs, *scr)
            pl.when(last)(finish)
        else:
            if riders:
                start()
            body(*ins, *outs, *scr)
            if riders:
                finish()

    barrier_id = shakes[0][0] if shakes else None
    params = pltpu.CompilerParams(vmem_limit_bytes=VMEM_LIMIT, collective_id=barrier_id,
                                  **(dict(dimension_semantics=sem) if grid else {}))
    res = pl.pallas_call(
        full, name=name, grid=grid, in_specs=in_specs, out_specs=out_specs, out_shape=out_shape,
        scratch_shapes=scratch_shapes, input_output_aliases=aliases, compiler_params=params,
    )(*operands)
    outs, at, rider_res = res[:n_out], n_out, []
    for r in riders:
        rider_res.append((res[at:at + len(r.inplace)], res[at + len(r.inplace):at + len(r.inplace) + len(r.lands)]))
        at += len(r.inplace) + len(r.lands)
    return list(outs), rider_res


def _gammas():
    return [1.0 - 2.0 ** (-5.0 - h) for h in range(HEADS)]


def _decay_tables():
    idx = np.arange(SUPER)
    dist = np.abs(idx[:, None] - idx[None, :]).astype(np.float64)
    visible = (idx[None, :] // CHUNK) <= (idx[:, None] // CHUNK)
    mask = np.stack([np.where(visible, g ** dist, 0.0) for g in _gammas()])
    qd = np.concatenate([np.repeat((g ** (idx + 1.0))[:, None], HEAD_DIM, 1) for g in _gammas()], 1)
    kd = np.concatenate([np.repeat((g ** (SUPER - 1.0 - idx))[:, None], HEAD_DIM, 1) for g in _gammas()], 1)
    return (jnp.asarray(mask, F32), jnp.asarray(qd, F32), jnp.asarray(kd, F32))


def _rope_tables(s):
    inv_freq = ROPE_BASE ** (-np.arange(0, HEAD_DIM, 2, dtype=np.float64) / HEAD_DIM)
    ang = np.arange(s, dtype=np.float64)[:, None] * inv_freq[None, :]
    cos, sin = np.cos(ang), np.sin(ang)
    return (jnp.asarray(np.concatenate([cos, cos], 1), F32),
            jnp.asarray(np.concatenate([-sin, sin], 1), F32))


def _rope(t, cosf, sinf):
    return t * cosf + pltpu.roll(t, HEAD_DIM // 2, 1) * sinf


def _rope_t(t, cosf, sinf):
    return t * cosf - pltpu.roll(t, HEAD_DIM // 2, 1) * sinf


def _layernorm_fwd(z):
    mu = jnp.mean(z, axis=-1, keepdims=True)
    zc = z - mu
    var = jnp.mean(zc * zc, axis=-1, keepdims=True)
    rstd = lax.rsqrt(var + LN_EPS)
    return zc * rstd, rstd


def _layernorm_bwd(dy, xhat, rstd, gain):
    dxh = dy * gain
    m1 = jnp.mean(dxh, axis=-1, keepdims=True)
    m2 = jnp.mean(dxh * xhat, axis=-1, keepdims=True)
    return rstd * (dxh - m1 - xhat * m2)


def _proj_pool(x, win4, cosf, sinf, wpool, pscale, ts, riders=(), after=()):
    s = x.shape[0]
    nt = s // ts

    def body(x_ref, w_ref, cos_ref, sin_ref, wp_ref, ps_ref,
             xb_ref, q_ref, k_ref, v_ref, g_ref, pooled_ref, cat_ref, proj_scr, pext_scr):
        i = pl.program_id(0)
        xb = x_ref[...].astype(BF16)
        xb_ref[...] = xb
        for j in range(N_SHARD):
            proj_scr[:, j * IN_SH:(j + 1) * IN_SH] = _dot(xb, w_ref[j])
        cosf_t = cos_ref[...]
        sinf_t = sin_ref[...]
        for h in range(HEADS):
            lo = h * HEAD_DIM
            q_ref[:, lo:lo + HEAD_DIM] = _rope(proj_scr[:, lo:lo + HEAD_DIM], cosf_t, sinf_t).astype(BF16)
            kk = _rope(proj_scr[:, RET_W + lo:RET_W + lo + HEAD_DIM], cosf_t, sinf_t) * K_SCALE
            k_ref[:, lo:lo + HEAD_DIM] = kk.astype(BF16)
        v_ref[...] = proj_scr[:, 2 * RET_W:3 * RET_W].astype(BF16)
        g_ref[...] = proj_scr[:, 3 * RET_W:4 * RET_W]

        @pl.when(i == 0)
        def _():
            pext_scr[0:POOL_HALO, :] = jnp.zeros((POOL_HALO, POOL_W), F32)

        pext_scr[POOL_HALO:POOL_HALO + ts, :] = proj_scr[:, 4 * RET_W:IN_W]
        pos = (i * ts + lax.broadcasted_iota(jnp.int32, (ts, 1), 0) + 1).astype(F32)
        for gi, w in enumerate(POOL_WINDOWS):
            lo = gi * HEAD_DIM
            ext = pext_scr[:, lo:lo + HEAD_DIM]
            acc = ext
            shift = 1
            while shift < w:
                acc = acc + pltpu.roll(acc, shift, 0)
                shift *= 2
            tok = ext[POOL_HALO:POOL_HALO + ts]
            pooled = acc[POOL_HALO:POOL_HALO + ts] / jnp.minimum(pos, float(w)) - tok
            pooled_b = pooled.astype(BF16)
            pooled_ref[:, lo:lo + HEAD_DIM] = pooled_b
            lin = _dot(pooled_b, wp_ref[gi])
            cat_ref[:, lo:lo + HEAD_DIM] = (lin * ps_ref[:, lo:lo + HEAD_DIM]).astype(BF16)
        pext_scr[0:POOL_HALO, :] = pext_scr[ts:ts + POOL_HALO, :]

    tile = lambda w: pl.BlockSpec((ts, w), lambda i: (i, 0))
    return _call(
        body, name="proj_pool", grid=(nt,),
        in_specs=[tile(D_MODEL), _whole(), tile(HEAD_DIM), tile(HEAD_DIM), _whole(), _whole()],
        out_specs=[tile(D_MODEL), tile(RET_W), tile(RET_W), tile(RET_W), tile(RET_W), tile(POOL_W),
                   pl.BlockSpec((ts, POOL_W), lambda i: (i, 1))],
        out_shape=[jax.ShapeDtypeStruct((s, D_MODEL), BF16), jax.ShapeDtypeStruct((s, RET_W), BF16),
                   jax.ShapeDtypeStruct((s, RET_W), BF16), jax.ShapeDtypeStruct((s, RET_W), BF16),
                   jax.ShapeDtypeStruct((s, RET_W), F32), jax.ShapeDtypeStruct((s, POOL_W), BF16),
                   jax.ShapeDtypeStruct((s, 2 * RET_W), BF16)],
        scratch_shapes=[pltpu.VMEM((ts, IN_W), F32), pltpu.VMEM((ts + POOL_HALO, POOL_W), F32)],
        sem=("arbitrary",), operands=(x, win4, cosf, sinf, wpool, pscale), riders=riders, after=after,
    )


def _retention_fwd(q, k, v, g, cat, mask, qd, kd, riders=(), after=()):
    s = q.shape[0]
    ns = s // SUPER
    cdec = [gm ** float(SUPER) for gm in _gammas()]

    def body(q_ref, k_ref, v_ref, g_ref, cat_in, mask_ref, qd_ref, kd_ref,
             ret_ref, cat_ref, st_ref, state_scr):
        del cat_in
        n = pl.program_id(0)

        @pl.when(n == 0)
        def _():
            state_scr[...] = jnp.zeros_like(state_scr)

        for h in range(HEADS):
            sl = slice(h * HEAD_DIM, (h + 1) * HEAD_DIM)
            qh, kh, vh = q_ref[:, sl], k_ref[:, sl], v_ref[:, sl]
            sc = _dot_nt(qh, kh) * mask_ref[h]
            st = state_scr[h]
            stb = st.astype(BF16)
            st_ref[0, h] = stb
            qdb = (qh.astype(F32) * qd_ref[:, sl]).astype(BF16)
            kdb = (kh.astype(F32) * kd_ref[:, sl]).astype(BF16)
            ret = _dot(sc.astype(BF16), vh) + _dot(qdb, stb)
            state_scr[h] = st * cdec[h] + _dot_tn(kdb, vh)
            ret_ref[:, sl] = ret
            r = lax.rsqrt(jnp.mean(ret * ret, axis=-1, keepdims=True) + RMS_EPS)
            gh = g_ref[:, sl]
            cat_ref[:, sl] = ((ret * r) * (gh * _sigmoid(gh))).astype(BF16)

    tile = pl.BlockSpec((SUPER, RET_W), lambda n: (n, 0))
    return _call(
        body, name="retention_fwd", grid=(ns,),
        in_specs=[tile, tile, tile, tile, HBM_SPEC, _whole(), _whole(), _whole()],
        out_specs=[tile, tile, pl.BlockSpec((1, HEADS, HEAD_DIM, HEAD_DIM), lambda n: (n, 0, 0, 0))],
        out_shape=[jax.ShapeDtypeStruct((s, RET_W), F32), jax.ShapeDtypeStruct((s, 2 * RET_W), BF16),
                   jax.ShapeDtypeStruct((ns, HEADS, HEAD_DIM, HEAD_DIM), BF16)],
        scratch_shapes=[pltpu.VMEM((HEADS, HEAD_DIM, HEAD_DIM), F32)],
        aliases={4: 1}, sem=("arbitrary",), operands=(q, k, v, g, cat, mask, qd, kd), riders=riders,
        after=after,
    )


def _outproj_ln1(x, cat, wout, g1, b1, ts, riders=(), after=()):
    s = x.shape[0]

    def body(x_ref, cat_ref, w_ref, g_ref, b_ref, xhat_ref, rstd_ref, h1b_ref):
        z = ALPHA * x_ref[...] + _dot(cat_ref[...], w_ref[...])
        xhat, rstd = _layernorm_fwd(z)
        xhat_ref[...] = xhat
        rstd_ref[...] = rstd
        h1b_ref[...] = (xhat * g_ref[...] + b_ref[...]).astype(BF16)

    tile = lambda w: pl.BlockSpec((ts, w), lambda i: (i, 0))
    return _call(
        body, name="outproj_ln1", grid=(s // ts,),
        in_specs=[tile(D_MODEL), tile(D_MODEL), _whole(), _whole(), _whole()],
        out_specs=[tile(D_MODEL), tile(1), tile(D_MODEL)],
        out_shape=[jax.ShapeDtypeStruct((s, D_MODEL), F32), jax.ShapeDtypeStruct((s, 1), F32),
                   jax.ShapeDtypeStruct((s, D_MODEL), BF16)],
        sem=("arbitrary",), operands=(x, cat, wout, g1, b1), riders=riders, after=after,
    )


def _ffn_fwd_loss(xhat1, h1b, target, wup4, wdown, cw, cb, g1, b1, g2, b2, ts):
    s = xhat1.shape[0]

    def body(xhat_ref, h1b_ref, tgt_ref, wup_ref, wdn_ref, cw_ref, cb_ref, g1_ref, b1_ref, g2_ref, b2_ref,
             ub_ref, act_ref, sd_ref, dz2_ref, dz2b_ref, loss_ref, dg2_ref, db2_ref, val_scr, gext_scr, ffn_scr):
        i = pl.program_id(0)

        @pl.when(i == 0)
        def _():
            gext_scr[0:CONV_HALO, :] = jnp.zeros((CONV_HALO, D_FF), F32)
            loss_ref[...] = jnp.zeros_like(loss_ref)
            dg2_ref[...] = jnp.zeros_like(dg2_ref)
            db2_ref[...] = jnp.zeros_like(db2_ref)

        for half in range(2):
            lo = half * UP_SH
            gext_scr[CONV_HALO:CONV_HALO + ts, lo:lo + UP_SH] = _dot(h1b_ref[...], wup_ref[2 + half])
            val_scr[:, lo:lo + UP_SH] = _dot(h1b_ref[...], wup_ref[half])
            for c0 in range(lo, lo + UP_SH, FFN_STRIP):
                cols = slice(c0, c0 + FFN_STRIP)
                ext = gext_scr[:, cols]
                gate = ext[CONV_HALO:]
                hc = cb_ref[:, cols] + ((pltpu.roll(ext, 2, 0)[CONV_HALO:] * cw_ref[0:1, cols]
                                         + pltpu.roll(ext, 1, 0)[CONV_HALO:] * cw_ref[1:2, cols])
                                        + gate * cw_ref[2:3, cols])
                val = val_scr[:, cols]
                sg = _sigmoid(hc)
                si = hc * sg
                act_ref[:, cols] = (si * val).astype(BF16)
                ub_ref[:, cols] = val.astype(BF16)
                ub_ref[:, D_FF + c0:D_FF + c0 + FFN_STRIP] = gate.astype(BF16)
                sd_ref[:, cols] = hc.astype(BF16)
            part = _dot(act_ref[:, lo:lo + UP_SH], wdn_ref[lo:lo + UP_SH, :])
            if half == 0:
                ffn_scr[...] = part
            else:
                ffn_scr[...] += part

        gext_scr[0:CONV_HALO, :] = gext_scr[ts:ts + CONV_HALO, :]

        loss_acc = jnp.zeros((1, 1), F32)
        dg2_acc = jnp.zeros((1, D_MODEL), F32)
        db2_acc = jnp.zeros((1, D_MODEL), F32)
        for r0 in range(0, ts, LN_ROWS):
            rows = slice(r0, r0 + LN_ROWS)
            h1 = xhat_ref[rows, :] * g1_ref[...] + b1_ref[...]
            xhat2, rstd2 = _layernorm_fwd(ALPHA * h1 + ffn_scr[rows, :])
            diff = (xhat2 * g2_ref[...] + b2_ref[...]) - tgt_ref[rows, :]
            row = jnp.mean(diff * diff, axis=-1, keepdims=True)
            loss_acc = loss_acc + 0.5 * jnp.sum(row, axis=0, keepdims=True)
            dy = diff * (1.0 / D_MODEL)
            dg2_acc = dg2_acc + jnp.sum(dy * xhat2, axis=0, keepdims=True)
            db2_acc = db2_acc + jnp.sum(dy, axis=0, keepdims=True)
            dz2 = _layernorm_bwd(dy, xhat2, rstd2, g2_ref[...])
            dz2_ref[rows, :] = dz2
            dz2b_ref[rows, :] = dz2.astype(BF16)
        loss_ref[...] += loss_acc
        dg2_ref[...] += dg2_acc
        db2_ref[...] += db2_acc

    tile = lambda w: pl.BlockSpec((ts, w), lambda i: (i, 0))
    acc = lambda w: pl.BlockSpec((1, w), lambda i: (0, 0))
    return pl.pallas_call(
        body, name="ffn_fwd_loss", grid=(s // ts,),
        in_specs=[tile(D_MODEL), tile(D_MODEL), tile(D_MODEL)] + [_whole()] * 8,
        out_specs=[tile(2 * D_FF), tile(D_FF), tile(D_FF), tile(D_MODEL), tile(D_MODEL),
                   acc(1), acc(D_MODEL), acc(D_MODEL)],
        out_shape=[jax.ShapeDtypeStruct((s, 2 * D_FF), BF16), jax.ShapeDtypeStruct((s, D_FF), BF16),
                   jax.ShapeDtypeStruct((s, D_FF), BF16), jax.ShapeDtypeStruct((s, D_MODEL), F32),
                   jax.ShapeDtypeStruct((s, D_MODEL), BF16),
                   jax.ShapeDtypeStruct((1, 1), F32), jax.ShapeDtypeStruct((1, D_MODEL), F32),
                   jax.ShapeDtypeStruct((1, D_MODEL), F32)],
        scratch_shapes=[pltpu.VMEM((ts, D_FF), F32), pltpu.VMEM((ts + CONV_HALO, D_FF), F32),
                        pltpu.VMEM((ts, D_MODEL), F32)],
        compiler_params=_params(("arbitrary",)),
    )(xhat1, h1b, target, wup4, wdown, cw, cb, g1, b1, g2, b2)


def _ffn_bwd(dz2, dz2b, ub, sd, xhat1, rstd1, wup4, wdown, cw, g1, ts):
    s = dz2.shape[0]
    nt = s // ts

    def body(dz2_ref, dz2b_ref, ub_ref, sd_ref, xhat_ref, rstd_ref, wup_ref, wdn_ref, cw_ref, g1_ref,
             dub_ref, dz1_ref, dz1b_ref, dg1_ref, db1_ref, dcw_ref, dcb_ref, dext_scr, da_scr):
        i = pl.program_id(0)

        @pl.when(i == 0)
        def _():
            dext_scr[ts:ts + CONV_HALO, :] = jnp.zeros((CONV_HALO, D_FF), F32)
            dg1_ref[...] = jnp.zeros_like(dg1_ref)
            db1_ref[...] = jnp.zeros_like(db1_ref)
            dcw_ref[...] = jnp.zeros_like(dcw_ref)
            dcb_ref[...] = jnp.zeros_like(dcb_ref)

        da_scr[...] = _dot_nt(dz2b_ref[...], wdn_ref[...])
        n_ext = ts + CONV_HALO
        for c0 in range(0, D_FF, FFN_STRIP):
            cols = slice(c0, c0 + FFN_STRIP)
            gcols = slice(D_FF + c0, D_FF + c0 + FFN_STRIP)
            val = ub_ref[:, cols].astype(F32)
            gate = ub_ref[:, gcols].astype(F32)
            da = da_scr[:, cols]
            hc = sd_ref[:, cols].astype(F32)
            sg = _sigmoid(hc)
            dhc = da * val * (sg * (1.0 + hc * (1.0 - sg)))
            dext_scr[0:ts, cols] = dhc
            dext = dext_scr[:, cols]
            dhc1 = pltpu.roll(dext, n_ext - 1, 0)[0:ts]
            dhc2 = pltpu.roll(dext, n_ext - 2, 0)[0:ts]
            dcb_ref[:, cols] += jnp.sum(dhc, axis=0, keepdims=True)
            dcw_ref[0:1, cols] += jnp.sum(dhc2 * gate, axis=0, keepdims=True)
            dcw_ref[1:2, cols] += jnp.sum(dhc1 * gate, axis=0, keepdims=True)
            dcw_ref[2:3, cols] += jnp.sum(dhc * gate, axis=0, keepdims=True)
            dgate = dhc * cw_ref[2:3, cols] + dhc1 * cw_ref[1:2, cols] + dhc2 * cw_ref[0:1, cols]
            dub_ref[:, cols] = (da * (hc * sg)).astype(BF16)
            dub_ref[:, gcols] = dgate.astype(BF16)
        dext_scr[ts:n_ext, :] = dext_scr[0:CONV_HALO, :]
        dh1 = ALPHA * dz2_ref[...]
        for j in range(N_SHARD):
            dh1 = dh1 + _dot_nt(dub_ref[:, j * UP_SH:(j + 1) * UP_SH], wup_ref[j])
        xhat = xhat_ref[...]
        dg1_ref[...] += jnp.sum(dh1 * xhat, axis=0, keepdims=True)
        db1_ref[...] += jnp.sum(dh1, axis=0, keepdims=True)
        dz1 = _layernorm_bwd(dh1, xhat, rstd_ref[...], g1_ref[...])
        dz1_ref[...] = dz1
        dz1b_ref[...] = dz1.astype(BF16)

    tile = lambda w: pl.BlockSpec((ts, w), lambda i: (nt - 1 - i, 0))
    acc = lambda rws, w: pl.BlockSpec((rws, w), lambda i: (0, 0))
    return pl.pallas_call(
        body, name="ffn_bwd", grid=(nt,),
        in_specs=[tile(D_MODEL), tile(D_MODEL), tile(2 * D_FF), tile(D_FF), tile(D_MODEL), tile(1)]
        + [_whole()] * 4,
        out_specs=[tile(2 * D_FF), tile(D_MODEL), tile(D_MODEL), acc(1, D_MODEL), acc(1, D_MODEL),
                   acc(3, D_FF), acc(1, D_FF)],
        out_shape=[jax.ShapeDtypeStruct((s, 2 * D_FF), BF16),
                   jax.ShapeDtypeStruct((s, D_MODEL), F32), jax.ShapeDtypeStruct((s, D_MODEL), BF16),
                   jax.ShapeDtypeStruct((1, D_MODEL), F32),
                   jax.ShapeDtypeStruct((1, D_MODEL), F32), jax.ShapeDtypeStruct((3, D_FF), F32),
                   jax.ShapeDtypeStruct((1, D_FF), F32)],
        scratch_shapes=[pltpu.VMEM((ts + CONV_HALO, D_FF), F32), pltpu.VMEM((ts, D_FF), F32)],
        compiler_params=_params(("arbitrary",)),
    )(dz2, dz2b, ub, sd, xhat1, rstd1, wup4, wdown, cw, g1)


def _mix_bwd(dz1, pooled, ret, g, wout, wpool, pscale, loss, vec_grads, ts, riders=(), after=()):
    s = dz1.shape[0]
    nt = s // ts

    def body(dz1_ref, pooled_ref, ret_ref, g_ref, wout_ref, wp_ref, ps_ref, loss_ref, dcb_ref, dg1_ref, db1_ref,
             dg2_ref, db2_ref, dret_ref, dgp_ref, dwp_ref, dps_ref, packed_ref, eext_scr):
        i = pl.program_id(0)
        r = nt - 1 - i

        @pl.when(i == 0)
        def _():
            eext_scr[ts:ts + POOL_HALO, :] = jnp.zeros((POOL_HALO, POOL_W), F32)
            dwp_ref[...] = jnp.zeros_like(dwp_ref)
            dps_ref[...] = jnp.zeros_like(dps_ref)

        dzb = dz1_ref[...].astype(BF16)
        dcat_r = _dot_nt(dzb, wout_ref[0:RET_W, :])
        dcat_p = _dot_nt(dzb, wout_ref[RET_W:2 * RET_W, :])
        pos = (r * ts + lax.broadcasted_iota(jnp.int32, (ts, 1), 0) + 1).astype(F32)
        dpooled = []
        for gi, w in enumerate(POOL_WINDOWS):
            sl = slice(gi * HEAD_DIM, (gi + 1) * HEAD_DIM)
            pb = pooled_ref[:, sl]
            dy = dcat_p[:, sl]
            dps_ref[:, sl] += jnp.sum(dy * _dot(pb, wp_ref[gi]), axis=0, keepdims=True)
            dlin = (dy * ps_ref[:, sl]).astype(BF16)
            dwp_ref[gi] += _dot_tn(pb, dlin)
            dpg = _dot_nt(dlin, wp_ref[gi])
            dpooled.append(dpg)
            eext_scr[0:ts, sl] = dpg / jnp.minimum(pos, float(w))
        for gi, w in enumerate(POOL_WINDOWS):
            sl = slice(gi * HEAD_DIM, (gi + 1) * HEAD_DIM)
            acc = eext_scr[:, sl]
            shift = 1
            while shift < w:
                acc = acc + pltpu.roll(acc, ts + POOL_HALO - shift, 0)
                shift *= 2
            dgp_ref[:, RET_W + gi * HEAD_DIM:RET_W + (gi + 1) * HEAD_DIM] = (acc[0:ts] - dpooled[gi]).astype(BF16)
        eext_scr[ts:ts + POOL_HALO, :] = eext_scr[0:POOL_HALO, :]
        for h in range(HEADS):
            sl = slice(h * HEAD_DIM, (h + 1) * HEAD_DIM)
            rt = ret_ref[:, sl]
            rr = lax.rsqrt(jnp.mean(rt * rt, axis=-1, keepdims=True) + RMS_EPS)
            rn = rt * rr
            gh = g_ref[:, sl]
            sg = _sigmoid(gh)
            dy = dcat_r[:, sl]
            dgp_ref[:, sl] = (dy * rn * (sg * (1.0 + gh * (1.0 - sg)))).astype(BF16)
            drn = dy * (gh * sg)
            dret_ref[:, sl] = (rr * (drn - rn * jnp.mean(drn * rn, axis=-1, keepdims=True))).astype(BF16)

        @pl.when(i == nt - 1)
        def _():
            packed_ref[...] = jnp.zeros_like(packed_ref)
            rows = (dcb_ref, dps_ref, dg1_ref, db1_ref, dg2_ref, db2_ref)
            for (row, n), ref in zip(SMALL_VECS, rows):
                packed_ref[row:row + 1, 0:n] = ref[...]
            packed_ref[ROW_LOSS:ROW_LOSS + 1, 0:HEAD_DIM] = jnp.broadcast_to(loss_ref[...], (1, HEAD_DIM))

    tile = lambda w: pl.BlockSpec((ts, w), lambda i: (nt - 1 - i, 0))
    return _call(
        body, name="mix_bwd", grid=(nt,),
        in_specs=[tile(D_MODEL), tile(POOL_W), tile(RET_W), tile(RET_W)] + [_whole()] * 9,
        out_specs=[tile(RET_W), tile(2 * RET_W),
                   pl.BlockSpec((len(POOL_WINDOWS), HEAD_DIM, HEAD_DIM), lambda i: (0, 0, 0)),
                   pl.BlockSpec((1, POOL_W), lambda i: (0, 0)),
                   pl.BlockSpec((SMALL_ROWS, D_FF), lambda i: (0, 0))],
        out_shape=[jax.ShapeDtypeStruct((s, RET_W), BF16), jax.ShapeDtypeStruct((s, 2 * RET_W), BF16),
                   jax.ShapeDtypeStruct((len(POOL_WINDOWS), HEAD_DIM, HEAD_DIM), F32),
                   jax.ShapeDtypeStruct((1, POOL_W), F32), jax.ShapeDtypeStruct((SMALL_ROWS, D_FF), F32)],
        scratch_shapes=[pltpu.VMEM((ts + POOL_HALO, POOL_W), F32)],
        sem=("arbitrary",), operands=(dz1, pooled, ret, g, wout, wpool, pscale, loss, *vec_grads), riders=riders,
        after=after,
    )


def _retention_bwd(q, k, v, dret, dgp, states, mask, qd, kd, cosf, sinf, riders=(), after=()):
    s = q.shape[0]
    ns = s // SUPER
    cdec = [gm ** float(SUPER) for gm in _gammas()]

    def body(q_ref, k_ref, v_ref, do_ref, dgp_ref, st_ref, mask_ref, qd_ref, kd_ref, cos_ref, sin_ref,
             dproj_ref, dstate_scr):
        i = pl.program_id(0)

        @pl.when(i == 0)
        def _():
            dstate_scr[...] = jnp.zeros_like(dstate_scr)

        cosf_t = cos_ref[...]
        sinf_t = sin_ref[...]
        for h in range(HEADS):
            sl = slice(h * HEAD_DIM, (h + 1) * HEAD_DIM)
            qh, kh, vh, doh = q_ref[:, sl], k_ref[:, sl], v_ref[:, sl], do_ref[:, sl]
            dscb = (_dot_nt(doh, vh) * mask_ref[0, h]).astype(BF16)
            dsctb = (_dot_nt(vh, doh) * mask_ref[1, h]).astype(BF16)
            sctb = (_dot_nt(kh, qh) * mask_ref[1, h]).astype(BF16)
            stb = st_ref[0, h]
            dst = dstate_scr[h]
            dstb = dst.astype(BF16)
            qdb = (qh.astype(F32) * qd_ref[:, sl]).astype(BF16)
            kdb = (kh.astype(F32) * kd_ref[:, sl]).astype(BF16)
            dq = _dot(dscb, kh) + _dot_nt(doh, stb) * qd_ref[:, sl]
            dk = _dot(dsctb, qh) + _dot_nt(vh, dstb) * kd_ref[:, sl]
            dv = _dot(sctb, doh) + _dot(kdb, dstb)
            dstate_scr[h] = dst * cdec[h] + _dot_tn(qdb, doh)
            lo = h * HEAD_DIM
            dproj_ref[:, lo:lo + HEAD_DIM] = _rope_t(dq, cosf_t, sinf_t).astype(BF16)
            dproj_ref[:, RET_W + lo:RET_W + lo + HEAD_DIM] = _rope_t(dk * K_SCALE, cosf_t, sinf_t).astype(BF16)
            dproj_ref[:, 2 * RET_W + lo:2 * RET_W + lo + HEAD_DIM] = dv.astype(BF16)
        dproj_ref[:, 3 * RET_W:IN_W] = dgp_ref[...]

    tile = lambda w: pl.BlockSpec((SUPER, w), lambda i: (ns - 1 - i, 0))
    return _call(
        body, name="retention_bwd", grid=(ns,),
        in_specs=[tile(RET_W), tile(RET_W), tile(RET_W), tile(RET_W), tile(2 * RET_W),
                  pl.BlockSpec((1, HEADS, HEAD_DIM, HEAD_DIM), lambda i: (ns - 1 - i, 0, 0, 0)),
                  _whole(), _whole(), _whole(), tile(HEAD_DIM), tile(HEAD_DIM)],
        out_specs=[tile(IN_W)],
        out_shape=[jax.ShapeDtypeStruct((s, IN_W), BF16)],
        scratch_shapes=[pltpu.VMEM((HEADS, HEAD_DIM, HEAD_DIM), F32)],
        sem=("arbitrary",), operands=(q, k, v, dret, dgp, states, mask, qd, kd, cosf, sinf), riders=riders,
        after=after,
    )


def _dx(dz1, dproj, win4, ts, riders=(), after=()):
    s = dz1.shape[0]

    def body(dz1_ref, dp_ref, w_ref, dx_ref):
        acc = ALPHA * dz1_ref[...]
        for j in range(N_SHARD):
            acc = acc + _dot_nt(dp_ref[:, j * IN_SH:(j + 1) * IN_SH], w_ref[j])
        dx_ref[...] = acc

    tile = lambda w: pl.BlockSpec((ts, w), lambda i: (i, 0))
    return _call(
        body, name="dx", grid=(s // ts,),
        in_specs=[tile(D_MODEL), tile(IN_W), _whole()],
        out_specs=[tile(D_MODEL)],
        out_shape=[jax.ShapeDtypeStruct((s, D_MODEL), F32)],
        sem=("arbitrary",), operands=(dz1, dproj, win4), riders=riders, after=after,
    )


def _wgrad(a, b, tm, tn, name, stacked, m_outer, riders=(), after=()):
    s, m = a.shape
    n = b.shape[1]

    def body(a_ref, b_ref, o32_ref, o16_ref):
        res = _dot_tn(a_ref[...], b_ref[...])
        o32_ref[...] = res.reshape(o32_ref.shape)
        o16_ref[...] = res.astype(BF16).reshape(o16_ref.shape)

    if m_outer:
        grid, blocks = (m // tm, n // tn), (lambda g0, g1: (g0, g1))
    else:
        grid, blocks = (n // tn, m // tm), (lambda g0, g1: (g1, g0))
    if stacked:
        shape = (n // tn, m, tn)
        ospec = pl.BlockSpec((1, tm, tn), lambda g0, g1: (blocks(g0, g1)[1], blocks(g0, g1)[0], 0))
    else:
        shape = (m, n)
        ospec = pl.BlockSpec((tm, tn), lambda g0, g1: blocks(g0, g1))
    return _call(
        body, name=name, grid=grid,
        in_specs=[pl.BlockSpec((s, tm), lambda g0, g1: (0, blocks(g0, g1)[0])),
                  pl.BlockSpec((s, tn), lambda g0, g1: (0, blocks(g0, g1)[1]))],
        out_specs=[ospec, ospec],
        out_shape=[jax.ShapeDtypeStruct(shape, F32), jax.ShapeDtypeStruct(shape, BF16)],
        sem=("arbitrary", "arbitrary"), operands=(a, b), riders=riders, after=after,
    )


def _wgrad_send(a, b, tn, name, barrier_id, after=()):
    s, m = a.shape
    n = b.shape[1]
    nb, hm = n // tn, m // 2

    def body(*refs):
        a_ref, b_ref = refs[:2]
        o32_ref, land_ref, send_scr, send_sems, recv_sems = refs[2 + len(after):]
        j = pl.program_id(0)
        x, y, c = _mesh_pos()

        @pl.when(j == 0)
        def _():
            _shake_hands("sibling")

        o32_ref[0] = _dot_tn(a_ref[...], b_ref[...])
        theirs = pl.ds(pl.multiple_of((1 - c) * hm, 16), hm)
        copies = [pltpu.make_async_remote_copy(
            src_ref=send_scr.at[blk], dst_ref=land_ref.at[blk], send_sem=send_sems.at[blk],
            recv_sem=recv_sems.at[blk], device_id=(x, y, 1 - c), device_id_type=MESH) for blk in range(nb)]
        for blk in range(nb):
            @pl.when(j == blk)
            def _(blk=blk):
                send_scr[blk] = o32_ref[0, theirs, :].astype(BF16)
                copies[blk].start()

        @pl.when(j == nb - 1)
        def _():
            for cp in copies:
                cp.wait()

    return pl.pallas_call(
        body, name=name, grid=(nb,),
        in_specs=[pl.BlockSpec((s, m), lambda j: (0, 0)), pl.BlockSpec((s, tn), lambda j: (0, j))]
        + [_whole()] * len(after),
        out_specs=[pl.BlockSpec((1, m, tn), lambda j: (j, 0, 0)), HBM_SPEC],
        out_shape=[jax.ShapeDtypeStruct((nb, m, tn), F32), jax.ShapeDtypeStruct((nb, hm, tn), BF16)],
        scratch_shapes=[pltpu.VMEM((nb, hm, tn), BF16), pltpu.SemaphoreType.DMA((nb,)),
                        pltpu.SemaphoreType.DMA((nb,))],
        compiler_params=pltpu.CompilerParams(dimension_semantics=("arbitrary",), vmem_limit_bytes=VMEM_LIMIT,
                                             collective_id=barrier_id),
    )(a, b, *after)


class _NoComm:
    def __init__(self, win4, wout, wup4, wdown):
        self.weights = dict(w_in=win4, w_out=wout, w_up=wup4, w_down=wdown)
        self.grads = {}

    def weight(self, name):
        return self.weights[name]

    def riders(self, call):
        return ()

    def after(self, call):
        return ()

    def landed(self, call, results, outs):
        pass

    def small_gradients(self, small, packed):
        pass

    def gradient(self, name, g32, g16):
        self.grads[name] = (g32, g16)

    def wgrad_in(self, xb, dproj):
        (g32, g16), _ = _wgrad(xb, dproj, D_MODEL, IN_SH, "wgrad_in", True, True)
        self.gradient("w_in", g32, g16)


def _local_step(x, target, cw, cb, wpool_b, pscale, g1, b1, g2, b2, comm):
    s = x.shape[0]
    ts_a = min(512, s)
    ts_f = min(256, s)
    mask, qd, kd = _decay_tables()
    cosf, sinf = _rope_tables(s)

    def run(call, fn, *args):
        outs, res = fn(*args, riders=comm.riders(call), after=comm.after(call))
        comm.landed(call, res, outs)
        return outs

    xb, q, k, v, g, pooled, cat = run("proj_pool", _proj_pool, x, comm.weight("w_in"), cosf, sinf, wpool_b,
                                      pscale, ts_a)
    ret, cat, states = run("retention_fwd", _retention_fwd, q, k, v, g, cat, mask, qd, kd)
    wout = comm.weight("w_out")
    xhat1, rstd1, h1b = run("outproj_ln1", _outproj_ln1, x, cat, wout, g1, b1, ts_a)
    wup4, wdown = comm.weight("w_up"), comm.weight("w_down")
    ub, act, sd, dz2, dz2b, loss, dg2, db2 = _ffn_fwd_loss(xhat1, h1b, target, wup4, wdown, cw, cb, g1, b1, g2, b2,
                                                           ts_f)

    dub, dz1, dz1b, dg1, db1, dcw, dcb = _ffn_bwd(dz2, dz2b, ub, sd, xhat1, rstd1, wup4, wdown, cw, g1, ts_f)
    half = D_MODEL // 2
    comm.gradient("w_up", *run("wgrad_up", _wgrad, h1b, dub, half, UP_SH, "wgrad_up", True, False))
    comm.gradient("w_out", *run("wgrad_out", _wgrad, cat, dz1b, D_MODEL, half, "wgrad_out", False, True))
    comm.gradient("w_down", *run("wgrad_down", _wgrad, act, dz2b, D_FF // 2, half, "wgrad_down", False, True))
    dret, dgp, dwp, dps, packed = run("mix_bwd", _mix_bwd, dz1b, pooled, ret, g, wout, wpool_b, pscale, loss,
                                      [dcb, dg1, db1, dg2, db2], ts_a)
    small = dict(w_pool=dwp, pool_scale=dps, ln1_g=dg1, ln1_b=db1, conv_w=dcw, conv_b=dcb,
                 ln2_g=dg2, ln2_b=db2)
    comm.small_gradients(small, packed)
    mask_both = jnp.stack([mask, jnp.swapaxes(mask, 1, 2)])
    dproj, = run("retention_bwd", _retention_bwd, q, k, v, dret, dgp, states, mask_both, qd, kd, cosf, sinf)
    comm.wgrad_in(xb, dproj)
    (grad_x,), _ = _dx(dz1, dproj, comm.weight("w_in"), ts_a, after=comm.after("dx"))
    return loss, grad_x, small


CAST_ROWS = 64
SHARD_SHAPES = ((D_MODEL, IN_SH), (OUT_SH, D_MODEL), (D_MODEL, UP_SH), (DOWN_SH, D_MODEL))
N_BIG = len(SHARD_SHAPES)
CW_SHARD = (3, 1, DOWN_SH)


def _mesh_pos():
    return lax.axis_index("x"), lax.axis_index("y"), lax.axis_index("c")


def _other_chips(x, y):
    return [(1 - x, y), (x, 1 - y), (1 - x, 1 - y)]


def _shake_hands(peers):
    x, y, c = _mesh_pos()
    others = [(x, y, 1 - c)] if peers in ("sibling", "both", "all") else []
    if peers in ("chips", "both", "all"):
        others += [(chip[0], chip[1], c) for chip in _other_chips(x, y)]
    if peers == "all":
        others += [(chip[0], chip[1], 1 - c) for chip in _other_chips(x, y)]
    barrier = pltpu.get_barrier_semaphore()
    for peer in others:
        pl.semaphore_signal(barrier, inc=1, device_id=peer, device_id_type=MESH)
    pl.semaphore_wait(barrier, len(others))


def _half_rows(w, which):
    hr = SHARD_SHAPES[w][0] // 2
    return pl.ds(pl.multiple_of(which * hr, 16), hr)


def _gather_weights(shards, cw_shard, wpool, full):
    def body(*refs):
        in_refs = refs[:N_BIG]
        cw_ref, wpool_ref = refs[N_BIG:N_BIG + 2]
        out_refs = refs[N_BIG + 2:2 * N_BIG + 2]
        cwo_ref, wpool_b_ref = refs[2 * N_BIG + 2:2 * N_BIG + 4]
        stage = refs[2 * N_BIG + 4:3 * N_BIG + 4]
        raw = refs[3 * N_BIG + 4:4 * N_BIG + 4 - len(full)]
        send_sems, recv_sems, fsend_sems, frecv_sems, cw_send, cw_recv, local_sems, load_sems = \
            refs[4 * N_BIG + 4 - len(full):]
        x, y, c = _mesh_pos()
        j0 = 2 * x + y
        chips = _other_chips(x, y)

        fetched = [w for w in range(N_BIG) if w not in full]
        f32 = {w: in_refs[w] for w in full}
        loads = []
        for n, w in enumerate(fetched):
            f32[w] = raw[n]
            loads.append(pltpu.make_async_copy(in_refs[w], raw[n], load_sems.at[n]))
            loads[-1].start()

        def cast_to_stage(w):
            def cast(i, carry):
                rows = pl.ds(pl.multiple_of(i * CAST_ROWS, CAST_ROWS), CAST_ROWS)
                stage[w][rows, :] = f32[w][rows, :].astype(BF16)
                return carry
            lax.fori_loop(0, SHARD_SHAPES[w][0] // CAST_ROWS, cast, 0)

        for w in full:
            cast_to_stage(w)

        jx, jy, jd = 2 * (1 - x) + y, 2 * x + (1 - y), 2 * (1 - x) + (1 - y)
        neighbours = [((1 - x, y, c), jx), ((x, 1 - y, c), jy)]
        passed = jnp.where(c == 0, jx, jy)
        pass_to = (jnp.where(c == 0, x, 1 - x), jnp.where(c == 0, 1 - y, y), c)

        def nbr(w, k, block):
            return pltpu.make_async_remote_copy(
                src_ref=stage[w].at[_half_rows(w, c), :], dst_ref=out_refs[w].at[block, _half_rows(w, c), :],
                send_sem=send_sems.at[w, k], recv_sem=recv_sems.at[w, k],
                device_id=neighbours[k][0], device_id_type=MESH)

        def relay(w, block):
            return pltpu.make_async_remote_copy(
                src_ref=out_refs[w].at[passed, _half_rows(w, c), :],
                dst_ref=out_refs[w].at[block, _half_rows(w, c), :],
                send_sem=send_sems.at[w, 2], recv_sem=recv_sems.at[w, 2],
                device_id=pass_to, device_id_type=MESH)

        def d2d(w, k, block, half):
            return pltpu.make_async_remote_copy(
                src_ref=out_refs[w].at[block, _half_rows(w, half), :],
                dst_ref=out_refs[w].at[block, _half_rows(w, half), :],
                send_sem=fsend_sems.at[w, k], recv_sem=frecv_sems.at[w, k],
                device_id=(x, y, 1 - c), device_id_type=MESH)

        def conv(k, block):
            chip = chips[k]
            return pltpu.make_async_remote_copy(
                src_ref=cw_ref, dst_ref=cwo_ref.at[block], send_sem=cw_send.at[k], recv_sem=cw_recv.at[k],
                device_id=(chip[0], chip[1], c), device_id_type=MESH)

        _shake_hands("both")
        sent = [nbr(w, k, j0) for w in full for k in range(2)] + [conv(k, j0) for k in range(3)]
        for cp in sent:
            cp.start()
        for n, w in enumerate(fetched):
            loads[n].wait()
            cast_to_stage(w)
        local = [pltpu.make_async_copy(stage[w], out_refs[w].at[j0], local_sems.at[w]) for w in range(N_BIG)]
        local.append(pltpu.make_async_copy(cw_ref, cwo_ref.at[j0], local_sems.at[N_BIG]))
        for cp in local:
            cp.start()
        wpool_b_ref[...] = wpool_ref[...].astype(BF16)
        for w in full:
            for k, (_, block) in enumerate(neighbours):
                nbr(w, k, block).wait_recv()
            later = [relay(w, passed)] + [d2d(w, k, block, c) for k, (_, block) in enumerate(neighbours)]
            for cp in later:
                cp.start()
            sent += later
        for w in full:
            relay(w, jd).wait_recv()
            fw = d2d(w, 2, jd, c)
            fw.start()
            sent.append(fw)
        for w in full:
            for k, block in enumerate([jx, jy, jd]):
                d2d(w, k, block, 1 - c).wait_recv()
        for k, chip in enumerate(chips):
            conv(k, 2 * chip[0] + chip[1]).wait_recv()
        for cp in sent:
            cp.wait_send()
        for cp in local:
            cp.wait()

    out_shape = [jax.ShapeDtypeStruct((N_SHARD,) + shp, BF16) for shp in SHARD_SHAPES]
    out_shape.append(jax.ShapeDtypeStruct((N_SHARD,) + CW_SHARD, F32))
    out_shape.append(jax.ShapeDtypeStruct(wpool.shape, BF16))
    return pl.pallas_call(
        body, name="gather_weights",
        in_specs=[_whole() if w in full else HBM_SPEC for w in range(N_BIG)] + [_whole()] * 2,
        out_specs=[HBM_SPEC] * (N_BIG + 1) + [_whole()],
        out_shape=out_shape,
        scratch_shapes=[pltpu.VMEM(shp, BF16) for shp in SHARD_SHAPES]
        + [pltpu.VMEM(shp, F32) for w, shp in enumerate(SHARD_SHAPES) if w not in full] + [
            pltpu.SemaphoreType.DMA((N_BIG, 3)), pltpu.SemaphoreType.DMA((N_BIG, 3)),
            pltpu.SemaphoreType.DMA((N_BIG, 3)), pltpu.SemaphoreType.DMA((N_BIG, 3)),
            pltpu.SemaphoreType.DMA((3,)), pltpu.SemaphoreType.DMA((3,)),
            pltpu.SemaphoreType.DMA((N_BIG + 1,)), pltpu.SemaphoreType.DMA((N_BIG - len(full),))],
        compiler_params=pltpu.CompilerParams(vmem_limit_bytes=VMEM_LIMIT, collective_id=13),
    )(*shards, cw_shard, wpool)


def _gather_rider(arrays, ops, handshake=None):
    ws = sorted(arrays)

    def make(inplace, srcs, lands, send_sems, recv_sems):
        del srcs, lands
        x, y, c = _mesh_pos()
        j0, jx, jy, jd = 2 * x + y, 2 * (1 - x) + y, 2 * x + (1 - y), 2 * (1 - x) + (1 - y)
        x_nbr, y_nbr, sibling = (1 - x, y, c), (x, 1 - y, c), (x, y, 1 - c)
        starts, waits = [], []
        for n, (kind, w, (r0, nr)) in enumerate(ops):
            ref = inplace[ws.index(w)]
            hr = SHARD_SHAPES[w][0] // 2
            rows = lambda core: pl.ds(pl.multiple_of(core * hr + r0, 16), nr)
            mine, theirs = rows(c), rows(1 - c)
            if kind == "ici":
                moves = [(ref.at[j0, mine, :], x_nbr, ref.at[jx, mine, :]),
                         (ref.at[j0, mine, :], y_nbr, ref.at[jy, mine, :]),
                         (ref.at[j0, mine, :], (1 - x, 1 - y, c), ref.at[jd, mine, :])]
            elif kind == "nbr":
                moves = [(ref.at[j0, mine, :], x_nbr, ref.at[jx, mine, :]),
                         (ref.at[j0, mine, :], y_nbr, ref.at[jy, mine, :])]
            elif kind == "relay":
                passed = jnp.where(c == 0, jx, jy)
                to = (jnp.where(c == 0, x, 1 - x), jnp.where(c == 0, 1 - y, y), c)
                moves = [(ref.at[passed, mine, :], to, ref.at[jd, mine, :])]
            else:
                blocks = dict(d2d=[jx, jy, jd], d2d_nbr=[jx, jy], d2d_diag=[jd])[kind]
                moves = [(ref.at[b, mine, :], sibling, ref.at[b, theirs, :]) for b in blocks]
            for k, (src, to, landing) in enumerate(moves):
                sems = dict(send_sem=send_sems.at[3 * n + k], recv_sem=recv_sems.at[3 * n + k],
                            device_id=to, device_id_type=MESH)
                send = pltpu.make_async_remote_copy(src_ref=src, dst_ref=src, **sems)
                arrival = pltpu.make_async_remote_copy(src_ref=src, dst_ref=landing, **sems)
                starts.append(send)
                waits += [arrival.wait_recv, send.wait_send]
        return starts, waits

    return _Rider([arrays[w] for w in ws], [], [], 3 * len(ops), make, handshake)


def _whole_half(w):
    return (0, SHARD_SHAPES[w][0] // 2)


def _pair_rider(ws, g16s):
    def make(inplace, srcs, lands, send_sems, recv_sems):
        del inplace
        x, y, c = _mesh_pos()
        copies = [pltpu.make_async_remote_copy(
            src_ref=srcs[i].at[:, _half_rows(w, 1 - c), :], dst_ref=lands[i],
            send_sem=send_sems.at[i], recv_sem=recv_sems.at[i], device_id=(x, y, 1 - c), device_id_type=MESH)
            for i, w in enumerate(ws)]
        return copies, [cp.wait for cp in copies]

    lands = [jax.ShapeDtypeStruct((N_SHARD, SHARD_SHAPES[w][0] // 2, SHARD_SHAPES[w][1]), BF16) for w in ws]
    return _Rider([], g16s, lands, len(ws), make)


def _chip_rider(ws, p16s):
    def make(inplace, srcs, lands, send_sems, recv_sems):
        del inplace
        x, y, c = _mesh_pos()
        copies = []
        for i in range(len(ws)):
            for k, chip in enumerate(_other_chips(x, y)):
                copies.append(pltpu.make_async_remote_copy(
                    src_ref=srcs[i].at[2 * chip[0] + chip[1]], dst_ref=lands[i].at[k],
                    send_sem=send_sems.at[3 * i + k], recv_sem=recv_sems.at[3 * i + k],
                    device_id=(chip[0], chip[1], c), device_id_type=MESH))
        return copies, [cp.wait for cp in copies]

    lands = [jax.ShapeDtypeStruct((3, SHARD_SHAPES[w][0] // 2, SHARD_SHAPES[w][1]), BF16) for w in ws]
    return _Rider([], p16s, lands, 3 * len(ws), make)


def _final_rider(halves):
    def make(inplace, srcs, lands, send_sems, recv_sems):
        del inplace
        x, y, c = _mesh_pos()
        copies = [pltpu.make_async_remote_copy(
            src_ref=srcs[i], dst_ref=lands[i], send_sem=send_sems.at[i], recv_sem=recv_sems.at[i],
            device_id=(x, y, 1 - c), device_id_type=MESH) for i in range(len(halves))]
        return copies, [cp.wait for cp in copies]

    return _Rider([], halves, [jax.ShapeDtypeStruct(h.shape, h.dtype) for h in halves], len(halves), make)


N_DEV = 2 * N_SHARD


def _device_number(x, y, c):
    return 2 * (2 * x + y) + c


def _small_all_rider(own):
    n = len(own)

    def make(inplace, srcs, lands, send_sems, recv_sems):
        del inplace
        x, y, c = _mesh_pos()
        peers = [(x, y, 1 - c)] + [(chip[0], chip[1], core) for chip in _other_chips(x, y) for core in (c, 1 - c)]
        copies = []
        for i in range(n):
            for k, peer in enumerate(peers):
                copies.append(pltpu.make_async_remote_copy(
                    src_ref=srcs[i], dst_ref=lands[i].at[_device_number(x, y, c)],
                    send_sem=send_sems.at[(N_DEV - 1) * i + k], recv_sem=recv_sems.at[(N_DEV - 1) * i + k],
                    device_id=peer, device_id_type=MESH))
        return copies, [cp.wait for cp in copies]

    lands = [jax.ShapeDtypeStruct((N_DEV,) + a.shape, a.dtype) for a in own]
    return _Rider([], own, lands, (N_DEV - 1) * n, make)


def _comm_only(name, riders):
    _, res = _call(lambda: None, name=name, grid=(), in_specs=[], out_specs=[], out_shape=[], operands=(),
                   riders=riders)
    return res


class _SemList:
    def __init__(self, refs):
        self.at = list(refs)


def _merged_rider(riders):
    srcs = [a for r in riders for a in r.srcs]
    lands = [a for r in riders for a in r.lands]

    def make(inplace, src_refs, land_refs, send_sems, recv_sems):
        starts, waits = [], []
        s0 = l0 = c0 = 0
        for r in riders:
            part = r.make(inplace, src_refs[s0:s0 + len(r.srcs)], land_refs[l0:l0 + len(r.lands)],
                          _SemList(send_sems.at[c0:c0 + r.n_copies]), _SemList(recv_sems.at[c0:c0 + r.n_copies]))
            starts += part[0]
            waits += part[1]
            s0, l0, c0 = s0 + len(r.srcs), l0 + len(r.lands), c0 + r.n_copies
        return starts, waits

    return _Rider([], srcs, lands, sum(r.n_copies for r in riders), make)


def _split_start(name, rider, handshake=None):
    assert not rider.inplace
    ns, nl, n = len(rider.srcs), len(rider.lands), rider.n_copies
    barrier_id, peers = handshake if handshake is not None else (None, None)

    def body(*refs):
        if handshake is not None:
            _shake_hands(peers)
        srcs, lands = refs[:ns], refs[ns:ns + nl]
        sems = refs[ns + nl:ns + nl + 2 * n]
        token = refs[-1]
        starts, _ = rider.make([], srcs, lands, _SemList(sems[:n]), _SemList(sems[n:]))
        for cp in starts:
            cp.start()
        token[...] = jnp.zeros_like(token)

    buffers = [pltpu.with_memory_space_constraint(a, pltpu.HBM) for a in rider.srcs]
    buffers += [pltpu.with_memory_space_constraint(lax.empty(s.shape, s.dtype), pltpu.HBM) for s in rider.lands]
    hbm = pl.BlockSpec(memory_space=pltpu.HBM)
    sem = pl.BlockSpec(memory_space=pltpu.SEMAPHORE)
    outs = pl.pallas_call(
        body, name=name,
        out_shape=tuple([pltpu.SemaphoreType.DMA(())] * (2 * n) + [pltpu.HBM(b.shape, b.dtype) for b in buffers]
                        + [jax.ShapeDtypeStruct((8, 128), F32)]),
        in_specs=[hbm] * (ns + nl),
        out_specs=tuple([sem] * (2 * n) + [hbm] * (ns + nl) + [_whole()]),
        input_output_aliases={i: 2 * n + i for i in range(ns + nl)},
        compiler_params=pltpu.CompilerParams(has_side_effects=pltpu.SideEffectType.DATAFLOW_SIDE_EFFECTING,
                                             collective_id=barrier_id),
    )(*buffers)
    return (rider, outs[:2 * n], outs[2 * n:2 * n + ns + nl]), outs[-1]


def _split_parts(state, riders):
    merged, sems, buffers = state
    n, ns = merged.n_copies, len(merged.srcs)
    parts, s0, l0, c0 = [], 0, 0, 0
    for r in riders:
        parts.append((r, list(sems[c0:c0 + r.n_copies]) + list(sems[n + c0:n + c0 + r.n_copies]),
                      list(buffers[s0:s0 + len(r.srcs)]) + list(buffers[ns + l0:ns + l0 + len(r.lands)])))
        s0, l0, c0 = s0 + len(r.srcs), l0 + len(r.lands), c0 + r.n_copies
    return parts


def _split_wait(name, state, after):
    rider, sems, buffers = state
    ns, nl, n = len(rider.srcs), len(rider.lands), rider.n_copies

    def body(*refs):
        srcs, lands = refs[:ns], refs[ns:ns + nl]
        sem_refs = refs[ns + nl:ns + nl + 2 * n]
        _, waits = rider.make([], srcs, lands, _SemList(sem_refs[:n]), _SemList(sem_refs[n:]))
        for wait in waits:
            wait()

    hbm = pl.BlockSpec(memory_space=pltpu.HBM)
    sem = pl.BlockSpec(memory_space=pltpu.SEMAPHORE)
    outs = pl.pallas_call(
        body, name=name,
        out_shape=tuple(pltpu.HBM(b.shape, b.dtype) for b in buffers),
        in_specs=[hbm] * (ns + nl) + [sem] * (2 * n) + [HBM_SPEC],
        out_specs=tuple([hbm] * (ns + nl)),
        input_output_aliases={i: i for i in range(ns + nl)},
        compiler_params=pltpu.CompilerParams(has_side_effects=pltpu.SideEffectType.DATAFLOW_SIDE_EFFECTING),
    )(*buffers, *sems, after)
    return list(outs[:ns]), list(outs[ns:])


def _pair_sum(pos, ws, g32s, recvs):
    n = len(ws)

    def body(pos_ref, *refs):
        del pos_ref
        g_refs, r_refs = refs[:n], refs[n:2 * n]
        p32_refs, p16_refs = refs[2 * n:3 * n], refs[3 * n:]
        x, y, _ = _mesh_pos()
        for i in range(n):
            tot = g_refs[i][...] + r_refs[i][...].astype(F32)
            p16_refs[i][...] = tot.astype(BF16)

            @pl.when(pl.program_id(0) == 2 * x + y)
            def _(i=i, tot=tot):
                p32_refs[i][...] = tot

    halves = [(SHARD_SHAPES[w][0] // 2, SHARD_SHAPES[w][1]) for w in ws]
    own = [pl.BlockSpec((None, None) + h, lambda j, pos_ref: (j, pos_ref[0], 0, 0)) for h in halves]
    blk = [pl.BlockSpec((None,) + h, lambda j, pos_ref: (j, 0, 0)) for h in halves]
    mine = [pl.BlockSpec(h, lambda j, pos_ref: (0, 0)) for h in halves]
    g4 = [g.reshape((N_SHARD, 2) + h) for g, h in zip(g32s, halves)]
    outs = pl.pallas_call(
        body, name="pair_sum_" + "_".join(str(w) for w in ws),
        grid_spec=pltpu.PrefetchScalarGridSpec(
            num_scalar_prefetch=1, grid=(N_SHARD,), in_specs=own + blk, out_specs=mine + blk),
        out_shape=[jax.ShapeDtypeStruct(h, F32) for h in halves]
        + [jax.ShapeDtypeStruct((N_SHARD,) + h, BF16) for h in halves],
        compiler_params=_params(("arbitrary",)),
    )(pos, *g4, *recvs)
    return outs[:n], outs[n:]


def _chip_sum(p32s, recvs):
    parts = 2

    def body(*refs):
        p_refs, r_refs, f_refs = refs[:N_BIG], refs[N_BIG:2 * N_BIG], refs[2 * N_BIG:]
        for w in range(N_BIG):
            f_refs[w][...] = ((p_refs[w][...] + r_refs[w][0].astype(F32)) + r_refs[w][1].astype(F32)) \
                + r_refs[w][2].astype(F32)

    quarters = [(r // 2 // parts, cc) for r, cc in SHARD_SHAPES]
    own = [pl.BlockSpec(qt, lambda i: (i, 0)) for qt in quarters]
    rcv = [pl.BlockSpec((3,) + qt, lambda i: (0, i, 0)) for qt in quarters]
    out = [pl.BlockSpec(qt, lambda i: (i, 0)) for qt in quarters]
    return pl.pallas_call(
        body, name="chip_sum", grid=(parts,), in_specs=own + rcv, out_specs=out,
        out_shape=[jax.ShapeDtypeStruct((r // 2, cc), F32) for r, cc in SHARD_SHAPES],
        compiler_params=_params(("arbitrary",)),
    )(*p32s, *recvs)


def _adamw(w, g, m, v):
    m_new = ADAM_B1 * m + (1.0 - ADAM_B1) * g
    v_new = ADAM_B2 * v + (1.0 - ADAM_B2) * (g * g)
    m_hat = m_new / (1.0 - ADAM_B1 ** ADAM_STEP)
    v_hat = v_new / (1.0 - ADAM_B2 ** ADAM_STEP)
    delta = -ADAM_LR * (m_hat / (jnp.sqrt(v_hat) + ADAM_EPS) + ADAM_WD * w)
    return delta, m_new, v_new


def _adam_half(name, pos, grads, ws, ms, vs, into=None):
    nb = 4
    which = (lambda ref: ref[0]) if into is None else (lambda ref: 1 - ref[0])

    def body(which_ref, *refs):
        del which_ref
        groups = [refs[i * N_BIG:(i + 1) * N_BIG] for i in range(4)]
        g_refs, w_refs, m_refs, v_refs = groups
        go_refs, do_refs, mo_refs, vo_refs = [refs[len(refs) - (4 - i) * N_BIG:len(refs) - (3 - i) * N_BIG]
                                              for i in range(4)]
        for w in range(N_BIG):
            g = g_refs[w][...]
            delta, m_new, v_new = _adamw(w_refs[w][...], g, m_refs[w][...], v_refs[w][...])
            go_refs[w][...] = g
            do_refs[w][...] = delta
            mo_refs[w][...] = m_new
            vo_refs[w][...] = v_new

    blocks = [(r // 2 // nb, cc) for r, cc in SHARD_SHAPES]
    half = [pl.BlockSpec(b, lambda i, which_ref: (i, 0)) for b in blocks]
    full = [pl.BlockSpec((None,) + b, lambda i, which_ref: (0, which(which_ref) * nb + i, 0)) for b in blocks]
    shapes = [jax.ShapeDtypeStruct((1,) + shp, F32) for shp in SHARD_SHAPES]
    carried = [] if into is None else [a for kind in into for a in kind]
    first = 1 + 4 * N_BIG
    outs = pl.pallas_call(
        body, name=name,
        grid_spec=pltpu.PrefetchScalarGridSpec(
            num_scalar_prefetch=1, grid=(nb,), in_specs=half + full * 3 + [HBM_SPEC] * len(carried),
            out_specs=full * 4),
        out_shape=shapes * 4,
        input_output_aliases={first + i: i for i in range(len(carried))},
        compiler_params=_params(("arbitrary",)),
    )(pos, *grads, *ws, *ms, *vs, *carried)
    return [outs[i * N_BIG:(i + 1) * N_BIG] for i in range(4)]


SMALL_ROWS = 8
ROW_CONV_B, ROW_POOL_SCALE, ROW_LN1_G, ROW_LN1_B, ROW_LN2_G, ROW_LN2_B, ROW_LOSS = range(7)
SMALL_VECS = ((ROW_CONV_B, D_FF), (ROW_POOL_SCALE, POOL_W), (ROW_LN1_G, D_MODEL), (ROW_LN1_B, D_MODEL),
              (ROW_LN2_G, D_MODEL), (ROW_LN2_B, D_MODEL))


def _small_adam(all_a, all_b, all_c, own_a, own_b, own_c, wp, cwp, vec_ws, m_wp, m_cwp, vec_ms,
                v_wp, v_cwp, vec_vs):
    nv = len(SMALL_VECS)
    np_ = 2 + nv

    def body(*refs):
        all_a_ref, all_b_ref, all_c_ref, own_a_ref, own_b_ref, own_c_ref = refs[0:6]
        refs = refs[3:]
        w_all, m_all, v_all = (refs[3 + i * np_:3 + (i + 1) * np_] for i in range(3))
        loss_out = refs[3 + 3 * np_]
        outs = refs[4 + 3 * np_:]
        x, y, c = _mesh_pos()
        j0 = 2 * x + y
        me = _device_number(x, y, c)

        def total(sent, own):
            by_dev = [jnp.where(me == d, own, sent(d)) for d in range(N_DEV)]
            chips = [by_dev[2 * j] + by_dev[2 * j + 1] for j in range(N_SHARD)]
            return ((chips[0] + chips[1]) + chips[2]) + chips[3]

        tot_a = total(lambda d: all_a_ref[d], own_a_ref[...])
        tot_b = total(lambda d: all_b_ref[d], own_b_ref[...])
        tot_c = total(lambda d: all_c_ref[d, j0], own_c_ref[j0])
        loss_out[...] = tot_b[ROW_LOSS:ROW_LOSS + 1, 0:1]
        grads = [tot_a, tot_c] + [tot_b[row:row + 1, 0:n] for row, n in SMALL_VECS]
        for p in range(np_):
            for at, g in ([(j, tot_c[j:j + 1]) for j in range(3)] if p == 1 else [(Ellipsis, grads[p])]):
                delta, m_new, v_new = _adamw(w_all[p][at], g, m_all[p][at], v_all[p][at])
                outs[p][at] = g
                outs[np_ + p][at] = delta
                outs[2 * np_ + p][at] = m_new
                outs[3 * np_ + p][at] = v_new

    pshapes = [wp.shape, CW_SHARD] + [wv.shape for wv in vec_ws]
    out_shape = [jax.ShapeDtypeStruct((1, 1), F32)] + [jax.ShapeDtypeStruct(s, F32) for s in pshapes] * 4
    outs = pl.pallas_call(
        body, name="small_adam",
        in_specs=[_whole()] * (6 + 3 * np_), out_specs=[_whole()] * len(out_shape), out_shape=out_shape,
        compiler_params=pltpu.CompilerParams(vmem_limit_bytes=VMEM_LIMIT),
    )(all_a, all_b, all_c, own_a, own_b, own_c, wp, cwp, *vec_ws, m_wp, m_cwp, *vec_ms, v_wp, v_cwp, *vec_vs)
    return outs[0], [outs[1 + i * np_:1 + (i + 1) * np_] for i in range(4)]


def kernel(x, w_in, w_pool, pool_scale, w_out, ln1_g, ln1_b, w_up, conv_w, conv_b, w_down, ln2_g, ln2_b, loss_target, m_w_in, m_w_pool, m_pool_scale, m_w_out, m_ln1_g, m_ln1_b, m_w_up, m_conv_w, m_conv_b, m_w_down, m_ln2_g, m_ln2_b, v_w_in, v_w_pool, v_pool_scale, v_w_out, v_ln1_g, v_ln1_b, v_w_up, v_conv_w, v_conv_b, v_w_down, v_ln2_g, v_ln2_b):
    pos = lax.axis_index("c").astype(jnp.int32).reshape(1)
    order = ("w_in", "w_out", "w_up", "w_down")
    w_in_i, w_out_i, w_up_i, w_down_i = range(N_BIG)
    vec_names = ("conv_b", "pool_scale", "ln1_g", "ln1_b", "ln2_g", "ln2_b")

    taps_first = lambda a: jnp.transpose(a, (1, 0, 2))
    gathered = _gather_weights([w_in[0], w_out[0], w_up[0], w_down[0]], taps_first(conv_w), w_pool[0], (w_in_i,))
    cw_full = jnp.transpose(gathered[N_BIG].reshape(N_SHARD, 3, DOWN_SH), (1, 0, 2)).reshape(3, D_FF)
    up_a, up_b, up_c = (0, 192), (192, 192), (384, 128)
    assert up_c[0] + up_c[1] == SHARD_SHAPES[w_up_i][0] // 2

    class MeshComm:
        def __init__(self):
            self.w = {i: gathered[i] for i in range(N_BIG)}
            self.g32, self.g16, self.p32, self.p16, self.recv_b = {}, {}, {}, {}, {}
            self.up_complete = False
            self.tokens, self.chips = {}, []

        def weight(self, name):
            i = order.index(name)
            if name == "w_up" and not self.up_complete:
                (arrs, _), = _comm_only("gather_up_last", [_gather_rider(
                    {i: self.w[i]}, [("d2d_diag", i, up_b), ("d2d", i, up_c)], (12, "sibling"))])
                self.w[i], self.up_complete = arrs[0], True
            full = self.w[i]
            return full.reshape(-1, full.shape[-1]) if name in ("w_out", "w_down") else full

        def _gather(self, ws, ops, handshake):
            return _gather_rider({w: self.w[w] for w in ws}, ops, handshake), ("w", ws)

        def _pair(self, ws):
            return _pair_rider(ws, [self.g16[w] for w in ws]), ("recv_a", ws)

        def _chip(self, ws):
            return _chip_rider(ws, [self.p16[w] for w in ws]), ("recv_b", ws)

        def plan(self, call):
            out_all, down_all = _whole_half(w_out_i), _whole_half(w_down_i)
            if call == "proj_pool":
                return [self._gather([w_out_i, w_up_i, w_down_i],
                                     [("ici", w_out_i, out_all), ("nbr", w_down_i, down_all),
                                      ("nbr", w_up_i, up_a)], (9, "chips"))]
            if call == "retention_fwd":
                return [self._gather([w_out_i, w_up_i, w_down_i],
                                     [("d2d", w_out_i, out_all),
                                      ("relay", w_down_i, down_all), ("d2d_nbr", w_down_i, down_all),
                                      ("relay", w_up_i, up_a), ("d2d_nbr", w_up_i, up_a), ("nbr", w_up_i, up_b)],
                                     (10, "both"))]
            if call == "outproj_ln1":
                return [self._gather([w_up_i, w_down_i],
                                     [("d2d_diag", w_down_i, down_all), ("d2d_diag", w_up_i, up_a),
                                      ("relay", w_up_i, up_b), ("d2d_nbr", w_up_i, up_b), ("ici", w_up_i, up_c)],
                                     (11, "both"))]
            return []

        def after(self, call):
            return tuple(self.tokens.pop(call, ()))

        def riders(self, call):
            self.pending = self.plan(call)
            return [r for r, _ in self.pending]

        def _start(self, name, rider, before, handshake):
            state, token = _split_start(name, rider, handshake)
            self.tokens.setdefault(before, []).append(token)
            return state

        def _finish_pair(self, name, state, ws, after):
            _, lands = _split_wait(name, state, after)
            self._finish_sum(ws, lands)

        def landed(self, call, results, outs):
            for (_, (slot, ws)), (inplace, lands) in zip(self.pending, results):
                for w, arr in zip(ws, inplace if len(inplace) else lands):
                    getattr(self, slot)[w] = arr
            if call == "wgrad_out":
                self._finish_pair("pair_exchange_up_wait", self.pair_up, [w_up_i], outs[1])
                self.chips.append(([w_up_i], self._start(
                    "chip_exchange_up_start", self._chip([w_up_i])[0], "wgrad_down", (5, "chips"))))
            if call == "mix_bwd":
                ws = [w_out_i, w_down_i]
                self._finish_pair("pair_exchange_out_down_wait", self.pair_out_down, ws, outs[0])

        def small_gradients(self, small, packed):
            dcw4 = jnp.transpose(small["conv_w"].reshape(3, N_SHARD, DOWN_SH), (1, 0, 2))
            own = [small["w_pool"], packed, dcw4]
            ws = [w_out_i, w_down_i]
            parts = [self._chip(ws)[0], _small_all_rider(own)]
            chip, self.small_all = _split_parts(
                self._start("chip_out_down_small_all_start", _merged_rider(parts), "retention_bwd",
                            (7, "all")), parts)
            self.chips.append((ws, chip))

        def gradient(self, name, g32, g16):
            w = order.index(name)
            shape = (N_SHARD,) + SHARD_SHAPES[w]
            self.g32[w], self.g16[w] = g32.reshape(shape), g16.reshape(shape)
            if name == "w_up":
                self.pair_up = self._start("pair_exchange_up_start", self._pair([w])[0], "wgrad_out",
                                           (1, "sibling"))
            if name == "w_down":
                self.pair_out_down = self._start("pair_exchange_out_down_start",
                                                 self._pair([w_out_i, w_down_i])[0], "mix_bwd", (2, "sibling"))

        def wgrad_in(self, xb, dproj):
            w = w_in_i
            g32, landed = _wgrad_send(xb, dproj, IN_SH, "wgrad_in", 4, after=self.after("wgrad_in"))
            self.g32[w] = g32
            self._finish_sum([w], [landed])
            self.chips.append(([w], self._start("chip_exchange_in_start", self._chip([w])[0], "dx", (8, "chips"))))

        def _finish_sum(self, ws, lands):
            p32s, p16s = _pair_sum(pos, ws, [self.g32[w] for w in ws], lands)
            for w, p32, p16 in zip(ws, p32s, p16s):
                self.p32[w], self.p16[w] = p32, p16

        def finish(self, after):
            for n, (ws, state) in enumerate(self.chips):
                _, lands = _split_wait("chip_exchange_wait_%d" % n, state, after)
                for w, arr in zip(ws, lands):
                    self.recv_b[w] = arr
            own, sent = _split_wait("small_all_wait", self.small_all, after)
            return list(sent) + list(own)

    comm = MeshComm()
    loss, grad_x, small = _local_step(x[0], loss_target[0], cw_full, conv_b, gathered[N_BIG + 1], pool_scale,
                                      ln1_g, ln1_b, ln2_g, ln2_b, comm)

    given = dict(w_pool=w_pool, pool_scale=pool_scale, ln1_g=ln1_g, ln1_b=ln1_b, conv_w=conv_w, conv_b=conv_b,
                 ln2_g=ln2_g, ln2_b=ln2_b)
    given_m = dict(w_pool=m_w_pool, pool_scale=m_pool_scale, ln1_g=m_ln1_g, ln1_b=m_ln1_b, conv_w=m_conv_w,
                   conv_b=m_conv_b, ln2_g=m_ln2_g, ln2_b=m_ln2_b)
    given_v = dict(w_pool=v_w_pool, pool_scale=v_pool_scale, ln1_g=v_ln1_g, ln1_b=v_ln1_b, conv_w=v_conv_w,
                   conv_b=v_conv_b, ln2_g=v_ln2_g, ln2_b=v_ln2_b)
    args = []
    for src in (given, given_m, given_v):
        args += [src["w_pool"][0], taps_first(src["conv_w"]), [src[n] for n in vec_names]]
    small_sums = comm.finish(grad_x)
    loss_tot, small_out = _small_adam(*small_sums, *args)
    every = range(N_BIG)
    mine = _chip_sum([comm.p32[w] for w in every], [comm.recv_b[w] for w in every])
    final_state, _ = _split_start("pair_exchange_f32_start", _final_rider(mine), (3, "sibling"))
    mine = final_state[2][:N_BIG]
    big = ([w_in, w_out, w_up, w_down], [m_w_in, m_w_out, m_w_up, m_w_down], [v_w_in, v_w_out, v_w_up, v_w_down])
    own_half = _adam_half("adam_own_half", pos, mine, *big)
    _, theirs = _split_wait("pair_exchange_f32_wait", final_state, own_half[0][0])
    big_out = _adam_half("adam_other_half", pos, theirs, *big, into=own_half)

    names = ("w_in", "w_pool", "pool_scale", "w_out", "ln1_g", "ln1_b", "w_up", "conv_w", "conv_b", "w_down",
             "ln2_g", "ln2_b")
    small_names = ("w_pool", "conv_w") + vec_names
    result = [loss_tot.reshape(()), grad_x[None]]
    for kind in range(4):
        for n in names:
            if n in order:
                result.append(big_out[kind][order.index(n)])
            else:
                val = small_out[kind][small_names.index(n)]
                if n == "conv_w":
                    val = taps_first(val)
                elif n == "w_pool":
                    val = val[None]
                result.append(val)
    return tuple(result)
```

```python
import functools

import numpy as np
import jax
import jax.numpy as jnp
from jax import lax
from jax.experimental import pallas as pl
from jax.experimental.pallas import tpu as pltpu

F32 = jnp.float32
BF16 = jnp.bfloat16

D_MODEL = 1024
HEADS = 4
HEAD_DIM = 128
RET_W = HEADS * HEAD_DIM
POOL_WINDOWS = (2, 4, 8, 16)
POOL_W = 512
IN_W = 4 * RET_W + POOL_W
D_FF = 2816
N_SHARD = 4
IN_SH = IN_W // N_SHARD
UP_SH = 2 * D_FF // N_SHARD
DOWN_SH = D_FF // N_SHARD
OUT_SH = D_MODEL // N_SHARD
ROPE_BASE = 10000.0
LN_EPS = 1e-5
RMS_EPS = 1e-6
ALPHA = 2.0 ** 0.25
K_SCALE = HEAD_DIM ** -0.5
SUPER = 256
CHUNK = 64
POOL_HALO = 16
CONV_HALO = 8
FFN_STRIP = 128
LN_ROWS = 32

ADAM_LR = 0.001
ADAM_B1 = 0.9
ADAM_B2 = 0.999
ADAM_EPS = 1e-08
ADAM_WD = 0.01
ADAM_STEP = 10

MESH = pl.DeviceIdType.MESH
VMEM_LIMIT = 56 * 1024 * 1024


def _dot(a, b):
    return jnp.dot(a, b, preferred_element_type=F32)


def _dot_nt(a, b):
    return lax.dot_general(a, b, (((1,), (1,)), ((), ())), preferred_element_type=F32)


def _dot_tn(a, b):
    return lax.dot_general(a, b, (((0,), (0,)), ((), ())), preferred_element_type=F32)


def _sigmoid(x):
    return 1.0 / (1.0 + jnp.exp(-x))


def _params(sem):
    return pltpu.CompilerParams(dimension_semantics=sem, vmem_limit_bytes=VMEM_LIMIT)


def _whole():
    return pl.BlockSpec(memory_space=pltpu.VMEM)


HBM_SPEC = pl.BlockSpec(memory_space=pl.ANY)


class _Rider:
    def __init__(self, inplace, srcs, lands, n_copies, make, handshake=None):
        self.inplace, self.srcs, self.lands, self.n_copies, self.make = list(inplace), list(srcs), list(lands), n_copies, make
        self.handshake = handshake


def _call(body, *, name, grid, in_specs, out_specs, out_shape, operands, scratch_shapes=(), sem=(),
          aliases=None, riders=(), after=()):
    n_in, n_out, n_scr = len(in_specs), len(out_shape), len(scratch_shapes)
    in_specs, out_specs, out_shape = list(in_specs), list(out_specs), list(out_shape)
    operands, scratch_shapes, aliases = list(operands), list(scratch_shapes), dict(aliases or {})
    in_specs += [_whole()] * len(after)
    operands += list(after)
    shakes = [r.handshake for r in riders if r.handshake is not None]
    assert len(shakes) <= 1
    for r in riders:
        for a in r.inplace:
            aliases[len(in_specs)] = len(out_shape)
            in_specs.append(HBM_SPEC)
            operands.append(a)
            out_specs.append(HBM_SPEC)
            out_shape.append(jax.ShapeDtypeStruct(a.shape, a.dtype))
        for a in r.srcs:
            in_specs.append(HBM_SPEC)
            operands.append(a)
        for shp in r.lands:
            out_specs.append(HBM_SPEC)
            out_shape.append(shp)
        scratch_shapes += [pltpu.SemaphoreType.DMA((r.n_copies,)), pltpu.SemaphoreType.DMA((r.n_copies,))]

    def full(*refs):
        ins = refs[:n_in]
        at = n_in + len(after)
        r_srcs = []
        for r in riders:
            at += len(r.inplace)
            r_srcs.append(refs[at:at + len(r.srcs)])
            at += len(r.srcs)
        outs = refs[at:at + n_out]
        at += n_out
        r_outs = []
        for r in riders:
            r_outs.append((refs[at:at + len(r.inplace)], refs[at + len(r.inplace):at + len(r.inplace) + len(r.lands)]))
            at += len(r.inplace) + len(r.lands)
        scr = refs[at:at + n_scr]
        at += n_scr
        r_sems = [refs[at + 2 * i:at + 2 * i + 2] for i in range(len(riders))]

        def copies():
            return [r.make(r_outs[i][0], r_srcs[i], r_outs[i][1], r_sems[i][0], r_sems[i][1])
                    for i, r in enumerate(riders)]

        def start():
            if shakes:
                _shake_hands(shakes[0][1])
            for starts, _ in copies():
                for cp in starts:
                    cp.start()

        def finish():
            for _, waits in copies():
                for wait in waits:
                    wait()

        if riders and grid:
            first = functools.reduce(jnp.logical_and, [pl.program_id(d) == 0 for d in range(len(grid))])
            last = functools.reduce(jnp.logical_and, [pl.program_id(d) == grid[d] - 1 for d in range(len(grid))])
            pl.when(first)(start)
            body(*ins, *outs, *scr)
            pl.when(last)(finish)
        else:
            if riders:
                start()
            body(*ins, *outs, *scr)
            if riders:
                finish()

    barrier_id = shakes[0][0] if shakes else None
    params = pltpu.CompilerParams(vmem_limit_bytes=VMEM_LIMIT, collective_id=barrier_id,
                                  **(dict(dimension_semantics=sem) if grid else {}))
    res = pl.pallas_call(
        full, name=name, grid=grid, in_specs=in_specs, out_specs=out_specs, out_shape=out_shape,
        scratch_shapes=scratch_shapes, input_output_aliases=aliases, compiler_params=params,
    )(*operands)
    outs, at, rider_res = res[:n_out], n_out, []
    for r in riders:
        rider_res.append((res[at:at + len(r.inplace)], res[at + len(r.inplace):at + len(r.inplace) + len(r.lands)]))
        at += len(r.inplace) + len(r.lands)
    return list(outs), rider_res


def _gammas():
    return [1.0 - 2.0 ** (-5.0 - h) for h in range(HEADS)]


def _decay_tables():
    idx = np.arange(SUPER)
    dist = np.abs(idx[:, None] - idx[None, :]).astype(np.float64)
    visible = (idx[None, :] // CHUNK) <= (idx[:, None] // CHUNK)
    mask = np.stack([np.where(visible, g ** dist, 0.0) for g in _gammas()])
    qd = np.concatenate([np.repeat((g ** (idx + 1.0))[:, None], HEAD_DIM, 1) for g in _gammas()], 1)
    kd = np.concatenate([np.repeat((g ** (SUPER - 1.0 - idx))[:, None], HEAD_DIM, 1) for g in _gammas()], 1)
    return (jnp.asarray(mask, F32), jnp.asarray(qd, F32), jnp.asarray(kd, F32))


def _rope_tables(s):
    inv_freq = ROPE_BASE ** (-np.arange(0, HEAD_DIM, 2, dtype=np.float64) / HEAD_DIM)
    ang = np.arange(s, dtype=np.float64)[:, None] * inv_freq[None, :]
    cos, sin = np.cos(ang), np.sin(ang)
    return (jnp.asarray(np.concatenate([cos, cos], 1), F32),
            jnp.asarray(np.concatenate([-sin, sin], 1), F32))


def _rope(t, cosf, sinf):
    return t * cosf + pltpu.roll(t, HEAD_DIM // 2, 1) * sinf


def _rope_t(t, cosf, sinf):
    return t * cosf - pltpu.roll(t, HEAD_DIM // 2, 1) * sinf


def _layernorm_fwd(z):
    mu = jnp.mean(z, axis=-1, keepdims=True)
    zc = z - mu
    var = jnp.mean(zc * zc, axis=-1, keepdims=True)
    rstd = lax.rsqrt(var + LN_EPS)
    return zc * rstd, rstd


def _layernorm_bwd(dy, xhat, rstd, gain):
    dxh = dy * gain
    m1 = jnp.mean(dxh, axis=-1, keepdims=True)
    m2 = jnp.mean(dxh * xhat, axis=-1, keepdims=True)
    return rstd * (dxh - m1 - xhat * m2)


def _proj_pool(x, win4, cosf, sinf, wpool, pscale, ts, riders=(), after=()):
    s = x.shape[0]
    nt = s // ts

    def body(x_ref, w_ref, cos_ref, sin_ref, wp_ref, ps_ref,
             xb_ref, q_ref, k_ref, v_ref, g_ref, pooled_ref, cat_ref, proj_scr, pext_scr):
        i = pl.program_id(0)
        xb = x_ref[...].astype(BF16)
        xb_ref[...] = xb
        for j in range(N_SHARD):
            proj_scr[:, j * IN_SH:(j + 1) * IN_SH] = _dot(xb, w_ref[j])
        cosf_t = cos_ref[...]
        sinf_t = sin_ref[...]
        for h in range(HEADS):
            lo = h * HEAD_DIM
            q_ref[:, lo:lo + HEAD_DIM] = _rope(proj_scr[:, lo:lo + HEAD_DIM], cosf_t, sinf_t).astype(BF16)
            kk = _rope(proj_scr[:, RET_W + lo:RET_W + lo + HEAD_DIM], cosf_t, sinf_t) * K_SCALE
            k_ref[:, lo:lo + HEAD_DIM] = kk.astype(BF16)
        v_ref[...] = proj_scr[:, 2 * RET_W:3 * RET_W].astype(BF16)
        g_ref[...] = proj_scr[:, 3 * RET_W:4 * RET_W]

        @pl.when(i == 0)
        def _():
            pext_scr[0:POOL_HALO, :] = jnp.zeros((POOL_HALO, POOL_W), F32)

        pext_scr[POOL_HALO:POOL_HALO + ts, :] = proj_scr[:, 4 * RET_W:IN_W]
        pos = (i * ts + lax.broadcasted_iota(jnp.int32, (ts, 1), 0) + 1).astype(F32)
        for gi, w in enumerate(POOL_WINDOWS):
            lo = gi * HEAD_DIM
            ext = pext_scr[:, lo:lo + HEAD_DIM]
            acc = ext
            shift = 1
            while shift < w:
                acc = acc + pltpu.roll(acc, shift, 0)
                shift *= 2
            tok = ext[POOL_HALO:POOL_HALO + ts]
            pooled = acc[POOL_HALO:POOL_HALO + ts] / jnp.minimum(pos, float(w)) - tok
            pooled_b = pooled.astype(BF16)
            pooled_ref[:, lo:lo + HEAD_DIM] = pooled_b
            lin = _dot(pooled_b, wp_ref[gi])
            cat_ref[:, lo:lo + HEAD_DIM] = (lin * ps_ref[:, lo:lo + HEAD_DIM]).astype(BF16)
        pext_scr[0:POOL_HALO, :] = pext_scr[ts:ts + POOL_HALO, :]

    tile = lambda w: pl.BlockSpec((ts, w), lambda i: (i, 0))
    return _call(
        body, name="proj_pool", grid=(nt,),
        in_specs=[tile(D_MODEL), _whole(), tile(HEAD_DIM), tile(HEAD_DIM), _whole(), _whole()],
        out_specs=[tile(D_MODEL), tile(RET_W), tile(RET_W), tile(RET_W), tile(RET_W), tile(POOL_W),
                   pl.BlockSpec((ts, POOL_W), lambda i: (i, 1))],
        out_shape=[jax.ShapeDtypeStruct((s, D_MODEL), BF16), jax.ShapeDtypeStruct((s, RET_W), BF16),
                   jax.ShapeDtypeStruct((s, RET_W), BF16), jax.ShapeDtypeStruct((s, RET_W), BF16),
                   jax.ShapeDtypeStruct((s, RET_W), F32), jax.ShapeDtypeStruct((s, POOL_W), BF16),
                   jax.ShapeDtypeStruct((s, 2 * RET_W), BF16)],
        scratch_shapes=[pltpu.VMEM((ts, IN_W), F32), pltpu.VMEM((ts + POOL_HALO, POOL_W), F32)],
        sem=("arbitrary",), operands=(x, win4, cosf, sinf, wpool, pscale), riders=riders, after=after,
    )


def _retention_fwd(q, k, v, g, cat, mask, qd, kd, riders=(), after=()):
    s = q.shape[0]
    ns = s // SUPER
    cdec = [gm ** float(SUPER) for gm in _gammas()]

    def body(q_ref, k_ref, v_ref, g_ref, cat_in, mask_ref, qd_ref, kd_ref,
             ret_ref, cat_ref, st_ref, state_scr):
        del cat_in
        n = pl.program_id(0)

        @pl.when(n == 0)
        def _():
            state_scr[...] = jnp.zeros_like(state_scr)

        for h in range(HEADS):
            sl = slice(h * HEAD_DIM, (h + 1) * HEAD_DIM)
            qh, kh, vh = q_ref[:, sl], k_ref[:, sl], v_ref[:, sl]
            sc = _dot_nt(qh, kh) * mask_ref[h]
            st = state_scr[h]
            stb = st.astype(BF16)
            st_ref[0, h] = stb
            qdb = (qh.astype(F32) * qd_ref[:, sl]).astype(BF16)
            kdb = (kh.astype(F32) * kd_ref[:, sl]).astype(BF16)
            ret = _dot(sc.astype(BF16), vh) + _dot(qdb, stb)
            state_scr[h] = st * cdec[h] + _dot_tn(kdb, vh)
            ret_ref[:, sl] = ret
            r = lax.rsqrt(jnp.mean(ret * ret, axis=-1, keepdims=True) + RMS_EPS)
            gh = g_ref[:, sl]
            cat_ref[:, sl] = ((ret * r) * (gh * _sigmoid(gh))).astype(BF16)

    tile = pl.BlockSpec((SUPER, RET_W), lambda n: (n, 0))
    return _call(
        body, name="retention_fwd", grid=(ns,),
        in_specs=[tile, tile, tile, tile, HBM_SPEC, _whole(), _whole(), _whole()],
        out_specs=[tile, tile, pl.BlockSpec((1, HEADS, HEAD_DIM, HEAD_DIM), lambda n: (n, 0, 0, 0))],
        out_shape=[jax.ShapeDtypeStruct((s, RET_W), F32), jax.ShapeDtypeStruct((s, 2 * RET_W), BF16),
                   jax.ShapeDtypeStruct((ns, HEADS, HEAD_DIM, HEAD_DIM), BF16)],
        scratch_shapes=[pltpu.VMEM((HEADS, HEAD_DIM, HEAD_DIM), F32)],
        aliases={4: 1}, sem=("arbitrary",), operands=(q, k, v, g, cat, mask, qd, kd), riders=riders,
        after=after,
    )


def _outproj_ln1(x, cat, wout, g1, b1, ts, riders=(), after=()):
    s = x.shape[0]

    def body(x_ref, cat_ref, w_ref, g_ref, b_ref, xhat_ref, rstd_ref, h1b_ref):
        z = ALPHA * x_ref[...] + _dot(cat_ref[...], w_ref[...])
        xhat, rstd = _layernorm_fwd(z)
        xhat_ref[...] = xhat
        rstd_ref[...] = rstd
        h1b_ref[...] = (xhat * g_ref[...] + b_ref[...]).astype(BF16)

    tile = lambda w: pl.BlockSpec((ts, w), lambda i: (i, 0))
    return _call(
        body, name="outproj_ln1", grid=(s // ts,),
        in_specs=[tile(D_MODEL), tile(D_MODEL), _whole(), _whole(), _whole()],
        out_specs=[tile(D_MODEL), tile(1), tile(D_MODEL)],
        out_shape=[jax.ShapeDtypeStruct((s, D_MODEL), F32), jax.ShapeDtypeStruct((s, 1), F32),
                   jax.ShapeDtypeStruct((s, D_MODEL), BF16)],
        sem=("arbitrary",), operands=(x, cat, wout, g1, b1), riders=riders, after=after,
    )


def _ffn_fwd_loss(xhat1, h1b, target, wup4, wdown, cw, cb, g1, b1, g2, b2, ts):
    s = xhat1.shape[0]

    def body(xhat_ref, h1b_ref, tgt_ref, wup_ref, wdn_ref, cw_ref, cb_ref, g1_ref, b1_ref, g2_ref, b2_ref,
             ub_ref, act_ref, sd_ref, dz2_ref, dz2b_ref, loss_ref, dg2_ref, db2_ref, val_scr, gext_scr, ffn_scr):
        i = pl.program_id(0)

        @pl.when(i == 0)
        def _():
            gext_scr[0:CONV_HALO, :] = jnp.zeros((CONV_HALO, D_FF), F32)
            loss_ref[...] = jnp.zeros_like(loss_ref)
            dg2_ref[...] = jnp.zeros_like(dg2_ref)
            db2_ref[...] = jnp.zeros_like(db2_ref)

        for half in range(2):
            lo = half * UP_SH
            gext_scr[CONV_HALO:CONV_HALO + ts, lo:lo + UP_SH] = _dot(h1b_ref[...], wup_ref[2 + half])
            val_scr[:, lo:lo + UP_SH] = _dot(h1b_ref[...], wup_ref[half])
            for c0 in range(lo, lo + UP_SH, FFN_STRIP):
                cols = slice(c0, c0 + FFN_STRIP)
                ext = gext_scr[:, cols]
                gate = ext[CONV_HALO:]
                hc = cb_ref[:, cols] + ((pltpu.roll(ext, 2, 0)[CONV_HALO:] * cw_ref[0:1, cols]
                                         + pltpu.roll(ext, 1, 0)[CONV_HALO:] * cw_ref[1:2, cols])
                                        + gate * cw_ref[2:3, cols])
                val = val_scr[:, cols]
                sg = _sigmoid(hc)
                si = hc * sg
                act_ref[:, cols] = (si * val).astype(BF16)
                ub_ref[:, cols] = val.astype(BF16)
                ub_ref[:, D_FF + c0:D_FF + c0 + FFN_STRIP] = gate.astype(BF16)
                sd_ref[:, cols] = hc.astype(BF16)
            part = _dot(act_ref[:, lo:lo + UP_SH], wdn_ref[lo:lo + UP_SH, :])
            if half == 0:
                ffn_scr[...] = part
            else:
                ffn_scr[...] += part

        gext_scr[0:CONV_HALO, :] = gext_scr[ts:ts + CONV_HALO, :]

        loss_acc = jnp.zeros((1, 1), F32)
        dg2_acc = jnp.zeros((1, D_MODEL), F32)
        db2_acc = jnp.zeros((1, D_MODEL), F32)
        for r0 in range(0, ts, LN_ROWS):
            rows = slice(r0, r0 + LN_ROWS)
            h1 = xhat_ref[rows, :] * g1_ref[...] + b1_ref[...]
            xhat2, rstd2 = _layernorm_fwd(ALPHA * h1 + ffn_scr[rows, :])
            diff = (xhat2 * g2_ref[...] + b2_ref[...]) - tgt_ref[rows, :]
            row = jnp.mean(diff * diff, axis=-1, keepdims=True)
            loss_acc = loss_acc + 0.5 * jnp.sum(row, axis=0, keepdims=True)
            dy = diff * (1.0 / D_MODEL)
            dg2_acc = dg2_acc + jnp.sum(dy * xhat2, axis=0, keepdims=True)
            db2_acc = db2_acc + jnp.sum(dy, axis=0, keepdims=True)
            dz2 = _layernorm_bwd(dy, xhat2, rstd2, g2_ref[...])
            dz2_ref[rows, :] = dz2
            dz2b_ref[rows, :] = dz2.astype(BF16)
        loss_ref[...] += loss_acc
        dg2_ref[...] += dg2_acc
        db2_ref[...] += db2_acc

    tile = lambda w: pl.BlockSpec((ts, w), lambda i: (i, 0))
    acc = lambda w: pl.BlockSpec((1, w), lambda i: (0, 0))
    return pl.pallas_call(
        body, name="ffn_fwd_loss", grid=(s // ts,),
        in_specs=[tile(D_MODEL), tile(D_MODEL), tile(D_MODEL)] + [_whole()] * 8,
        out_specs=[tile(2 * D_FF), tile(D_FF), tile(D_FF), tile(D_MODEL), tile(D_MODEL),
                   acc(1), acc(D_MODEL), acc(D_MODEL)],
        out_shape=[jax.ShapeDtypeStruct((s, 2 * D_FF), BF16), jax.ShapeDtypeStruct((s, D_FF), BF16),
                   jax.ShapeDtypeStruct((s, D_FF), BF16), jax.ShapeDtypeStruct((s, D_MODEL), F32),
                   jax.ShapeDtypeStruct((s, D_MODEL), BF16),
                   jax.ShapeDtypeStruct((1, 1), F32), jax.ShapeDtypeStruct((1, D_MODEL), F32),
                   jax.ShapeDtypeStruct((1, D_MODEL), F32)],
        scratch_shapes=[pltpu.VMEM((ts, D_FF), F32), pltpu.VMEM((ts + CONV_HALO, D_FF), F32),
                        pltpu.VMEM((ts, D_MODEL), F32)],
        compiler_params=_params(("arbitrary",)),
    )(xhat1, h1b, target, wup4, wdown, cw, cb, g1, b1, g2, b2)


def _ffn_bwd(dz2, dz2b, ub, sd, xhat1, rstd1, wup4, wdown, cw, g1, ts):
    s = dz2.shape[0]
    nt = s // ts

    def body(dz2_ref, dz2b_ref, ub_ref, sd_ref, xhat_ref, rstd_ref, wup_ref, wdn_ref, cw_ref, g1_ref,
             dub_ref, dz1_ref, dz1b_ref, dg1_ref, db1_ref, dcw_ref, dcb_ref, dext_scr, da_scr):
        i = pl.program_id(0)

        @pl.when(i == 0)
        def _():
            dext_scr[ts:ts + CONV_HALO, :] = jnp.zeros((CONV_HALO, D_FF), F32)
            dg1_ref[...] = jnp.zeros_like(dg1_ref)
            db1_ref[...] = jnp.zeros_like(db1_ref)
            dcw_ref[...] = jnp.zeros_like(dcw_ref)
            dcb_ref[...] = jnp.zeros_like(dcb_ref)

        da_scr[...] = _dot_nt(dz2b_ref[...], wdn_ref[...])
        n_ext = ts + CONV_HALO
        for c0 in range(0, D_FF, FFN_STRIP):
            cols = slice(c0, c0 + FFN_STRIP)
            gcols = slice(D_FF + c0, D_FF + c0 + FFN_STRIP)
            val = ub_ref[:, cols].astype(F32)
            gate = ub_ref[:, gcols].astype(F32)
            da = da_scr[:, cols]
            hc = sd_ref[:, cols].astype(F32)
            sg = _sigmoid(hc)
            dhc = da * val * (sg * (1.0 + hc * (1.0 - sg)))
            dext_scr[0:ts, cols] = dhc
            dext = dext_scr[:, cols]
            dhc1 = pltpu.roll(dext, n_ext - 1, 0)[0:ts]
            dhc2 = pltpu.roll(dext, n_ext - 2, 0)[0:ts]
            dcb_ref[:, cols] += jnp.sum(dhc, axis=0, keepdims=True)
            dcw_ref[0:1, cols] += jnp.sum(dhc2 * gate, axis=0, keepdims=True)
            dcw_ref[1:2, cols] += jnp.sum(dhc1 * gate, axis=0, keepdims=True)
            dcw_ref[2:3, cols] += jnp.sum(dhc * gate, axis=0, keepdims=True)
            dgate = dhc * cw_ref[2:3, cols] + dhc1 * cw_ref[1:2, cols] + dhc2 * cw_ref[0:1, cols]
            dub_ref[:, cols] = (da * (hc * sg)).astype(BF16)
            dub_ref[:, gcols] = dgate.astype(BF16)
        dext_scr[ts:n_ext, :] = dext_scr[0:CONV_HALO, :]
        dh1 = ALPHA * dz2_ref[...]
        for j in range(N_SHARD):
            dh1 = dh1 + _dot_nt(dub_ref[:, j * UP_SH:(j + 1) * UP_SH], wup_ref[j])
        xhat = xhat_ref[...]
        dg1_ref[...] += jnp.sum(dh1 * xhat, axis=0, keepdims=True)
        db1_ref[...] += jnp.sum(dh1, axis=0, keepdims=True)
        dz1 = _layernorm_bwd(dh1, xhat, rstd_ref[...], g1_ref[...])
        dz1_ref[...] = dz1
        dz1b_ref[...] = dz1.astype(BF16)

    tile = lambda w: pl.BlockSpec((ts, w), lambda i: (nt - 1 - i, 0))
    acc = lambda rws, w: pl.BlockSpec((rws, w), lambda i: (0, 0))
    return pl.pallas_call(
        body, name="ffn_bwd", grid=(nt,),
        in_specs=[tile(D_MODEL), tile(D_MODEL), tile(2 * D_FF), tile(D_FF), tile(D_MODEL), tile(1)]
        + [_whole()] * 4,
        out_specs=[tile(2 * D_FF), tile(D_MODEL), tile(D_MODEL), acc(1, D_MODEL), acc(1, D_MODEL),
                   acc(3, D_FF), acc(1, D_FF)],
        out_shape=[jax.ShapeDtypeStruct((s, 2 * D_FF), BF16),
                   jax.ShapeDtypeStruct((s, D_MODEL), F32), jax.ShapeDtypeStruct((s, D_MODEL), BF16),
                   jax.ShapeDtypeStruct((1, D_MODEL), F32),
                   jax.ShapeDtypeStruct((1, D_MODEL), F32), jax.ShapeDtypeStruct((3, D_FF), F32),
                   jax.ShapeDtypeStruct((1, D_FF), F32)],
        scratch_shapes=[pltpu.VMEM((ts + CONV_HALO, D_FF), F32), pltpu.VMEM((ts, D_FF), F32)],
        compiler_params=_params(("arbitrary",)),
    )(dz2, dz2b, ub, sd, xhat1, rstd1, wup4, wdown, cw, g1)


def _mix_bwd(dz1, pooled, ret, g, wout, wpool, pscale, loss, vec_grads, ts, riders=(), after=()):
    s = dz1.shape[0]
    nt = s // ts

    def body(dz1_ref, pooled_ref, ret_ref, g_ref, wout_ref, wp_ref, ps_ref, loss_ref, dcb_ref, dg1_ref, db1_ref,
             dg2_ref, db2_ref, dret_ref, dgp_ref, dwp_ref, dps_ref, packed_ref, eext_scr):
        i = pl.program_id(0)
        r = nt - 1 - i

        @pl.when(i == 0)
        def _():
            eext_scr[ts:ts + POOL_HALO, :] = jnp.zeros((POOL_HALO, POOL_W), F32)
            dwp_ref[...] = jnp.zeros_like(dwp_ref)
            dps_ref[...] = jnp.zeros_like(dps_ref)

        dzb = dz1_ref[...].astype(BF16)
        dcat_r = _dot_nt(dzb, wout_ref[0:RET_W, :])
        dcat_p = _dot_nt(dzb, wout_ref[RET_W:2 * RET_W, :])
        pos = (r * ts + lax.broadcasted_iota(jnp.int32, (ts, 1), 0) + 1).astype(F32)
        dpooled = []
        for gi, w in enumerate(POOL_WINDOWS):
            sl = slice(gi * HEAD_DIM, (gi + 1) * HEAD_DIM)
            pb = pooled_ref[:, sl]
            dy = dcat_p[:, sl]
            dps_ref[:, sl] += jnp.sum(dy * _dot(pb, wp_ref[gi]), axis=0, keepdims=True)
            dlin = (dy * ps_ref[:, sl]).astype(BF16)
            dwp_ref[gi] += _dot_tn(pb, dlin)
            dpg = _dot_nt(dlin, wp_ref[gi])
            dpooled.append(dpg)
            eext_scr[0:ts, sl] = dpg / jnp.minimum(pos, float(w))
        for gi, w in enumerate(POOL_WINDOWS):
            sl = slice(gi * HEAD_DIM, (gi + 1) * HEAD_DIM)
            acc = eext_scr[:, sl]
            shift = 1
            while shift < w:
                acc = acc + pltpu.roll(acc, ts + POOL_HALO - shift, 0)
                shift *= 2
            dgp_ref[:, RET_W + gi * HEAD_DIM:RET_W + (gi + 1) * HEAD_DIM] = (acc[0:ts] - dpooled[gi]).astype(BF16)
        eext_scr[ts:ts + POOL_HALO, :] = eext_scr[0:POOL_HALO, :]
        for h in range(HEADS):
            sl = slice(h * HEAD_DIM, (h + 1) * HEAD_DIM)
            rt = ret_ref[:, sl]
            rr = lax.rsqrt(jnp.mean(rt * rt, axis=-1, keepdims=True) + RMS_EPS)
            rn = rt * rr
            gh = g_ref[:, sl]
            sg = _sigmoid(gh)
            dy = dcat_r[:, sl]
            dgp_ref[:, sl] = (dy * rn * (sg * (1.0 + gh * (1.0 - sg)))).astype(BF16)
            drn = dy * (gh * sg)
            dret_ref[:, sl] = (rr * (drn - rn * jnp.mean(drn * rn, axis=-1, keepdims=True))).astype(BF16)

        @pl.when(i == nt - 1)
        def _():
            packed_ref[...] = jnp.zeros_like(packed_ref)
            rows = (dcb_ref, dps_ref, dg1_ref, db1_ref, dg2_ref, db2_ref)
            for (row, n), ref in zip(SMALL_VECS, rows):
                packed_ref[row:row + 1, 0:n] = ref[...]
            packed_ref[ROW_LOSS:ROW_LOSS + 1, 0:HEAD_DIM] = jnp.broadcast_to(loss_ref[...], (1, HEAD_DIM))

    tile = lambda w: pl.BlockSpec((ts, w), lambda i: (nt - 1 - i, 0))
    return _call(
        body, name="mix_bwd", grid=(nt,),
        in_specs=[tile(D_MODEL), tile(POOL_W), tile(RET_W), tile(RET_W)] + [_whole()] * 9,
        out_specs=[tile(RET_W), tile(2 * RET_W),
                   pl.BlockSpec((len(POOL_WINDOWS), HEAD_DIM, HEAD_DIM), lambda i: (0, 0, 0)),
                   pl.BlockSpec((1, POOL_W), lambda i: (0, 0)),
                   pl.BlockSpec((SMALL_ROWS, D_FF), lambda i: (0, 0))],
        out_shape=[jax.ShapeDtypeStruct((s, RET_W), BF16), jax.ShapeDtypeStruct((s, 2 * RET_W), BF16),
                   jax.ShapeDtypeStruct((len(POOL_WINDOWS), HEAD_DIM, HEAD_DIM), F32),
                   jax.ShapeDtypeStruct((1, POOL_W), F32), jax.ShapeDtypeStruct((SMALL_ROWS, D_FF), F32)],
        scratch_shapes=[pltpu.VMEM((ts + POOL_HALO, POOL_W), F32)],
        sem=("arbitrary",), operands=(dz1, pooled, ret, g, wout, wpool, pscale, loss, *vec_grads), riders=riders,
        after=after,
    )


def _retention_bwd(q, k, v, dret, dgp, states, mask, qd, kd, cosf, sinf, riders=(), after=()):
    s = q.shape[0]
    ns = s // SUPER
    cdec = [gm ** float(SUPER) for gm in _gammas()]

    def body(q_ref, k_ref, v_ref, do_ref, dgp_ref, st_ref, mask_ref, qd_ref, kd_ref, cos_ref, sin_ref,
             dproj_ref, dstate_scr):
        i = pl.program_id(0)

        @pl.when(i == 0)
        def _():
            dstate_scr[...] = jnp.zeros_like(dstate_scr)

        cosf_t = cos_ref[...]
        sinf_t = sin_ref[...]
        for h in range(HEADS):
            sl = slice(h * HEAD_DIM, (h + 1) * HEAD_DIM)
            qh, kh, vh, doh = q_ref[:, sl], k_ref[:, sl], v_ref[:, sl], do_ref[:, sl]
            dscb = (_dot_nt(doh, vh) * mask_ref[0, h]).astype(BF16)
            dsctb = (_dot_nt(vh, doh) * mask_ref[1, h]).astype(BF16)
            sctb = (_dot_nt(kh, qh) * mask_ref[1, h]).astype(BF16)
            stb = st_ref[0, h]
            dst = dstate_scr[h]
            dstb = dst.astype(BF16)
            qdb = (qh.astype(F32) * qd_ref[:, sl]).astype(BF16)
            kdb = (kh.astype(F32) * kd_ref[:, sl]).astype(BF16)
            dq = _dot(dscb, kh) + _dot_nt(doh, stb) * qd_ref[:, sl]
            dk = _dot(dsctb, qh) + _dot_nt(vh, dstb) * kd_ref[:, sl]
            dv = _dot(sctb, doh) + _dot(kdb, dstb)
            dstate_scr[h] = dst * cdec[h] + _dot_tn(qdb, doh)
            lo = h * HEAD_DIM
            dproj_ref[:, lo:lo + HEAD_DIM] = _rope_t(dq, cosf_t, sinf_t).astype(BF16)
            dproj_ref[:, RET_W + lo:RET_W + lo + HEAD_DIM] = _rope_t(dk * K_SCALE, cosf_t, sinf_t).astype(BF16)
            dproj_ref[:, 2 * RET_W + lo:2 * RET_W + lo + HEAD_DIM] = dv.astype(BF16)
        dproj_ref[:, 3 * RET_W:IN_W] = dgp_ref[...]

    tile = lambda w: pl.BlockSpec((SUPER, w), lambda i: (ns - 1 - i, 0))
    return _call(
        body, name="retention_bwd", grid=(ns,),
        in_specs=[tile(RET_W), tile(RET_W), tile(RET_W), tile(RET_W), tile(2 * RET_W),
                  pl.BlockSpec((1, HEADS, HEAD_DIM, HEAD_DIM), lambda i: (ns - 1 - i, 0, 0, 0)),
                  _whole(), _whole(), _whole(), tile(HEAD_DIM), tile(HEAD_DIM)],
        out_specs=[tile(IN_W)],
        out_shape=[jax.ShapeDtypeStruct((s, IN_W), BF16)],
        scratch_shapes=[pltpu.VMEM((HEADS, HEAD_DIM, HEAD_DIM), F32)],
        sem=("arbitrary",), operands=(q, k, v, dret, dgp, states, mask, qd, kd, cosf, sinf), riders=riders,
        after=after,
    )


def _dx(dz1, dproj, win4, ts, riders=(), after=()):
    s = dz1.shape[0]

    def body(dz1_ref, dp_ref, w_ref, dx_ref):
        acc = ALPHA * dz1_ref[...]
        for j in range(N_SHARD):
            acc = acc + _dot_nt(dp_ref[:, j * IN_SH:(j + 1) * IN_SH], w_ref[j])
        dx_ref[...] = acc

    tile = lambda w: pl.BlockSpec((ts, w), lambda i: (i, 0))
    return _call(
        body, name="dx", grid=(s // ts,),
        in_specs=[tile(D_MODEL), tile(IN_W), _whole()],
        out_specs=[tile(D_MODEL)],
        out_shape=[jax.ShapeDtypeStruct((s, D_MODEL), F32)],
        sem=("arbitrary",), operands=(dz1, dproj, win4), riders=riders, after=after,
    )


def _wgrad(a, b, tm, tn, name, stacked, m_outer, riders=(), after=()):
    s, m = a.shape
    n = b.shape[1]

    def body(a_ref, b_ref, o32_ref, o16_ref):
        res = _dot_tn(a_ref[...], b_ref[...])
        o32_ref[...] = res.reshape(o32_ref.shape)
        o16_ref[...] = res.astype(BF16).reshape(o16_ref.shape)

    if m_outer:
        grid, blocks = (m // tm, n // tn), (lambda g0, g1: (g0, g1))
    else:
        grid, blocks = (n // tn, m // tm), (lambda g0, g1: (g1, g0))
    if stacked:
        shape = (n // tn, m, tn)
        ospec = pl.BlockSpec((1, tm, tn), lambda g0, g1: (blocks(g0, g1)[1], blocks(g0, g1)[0], 0))
    else:
        shape = (m, n)
        ospec = pl.BlockSpec((tm, tn), lambda g0, g1: blocks(g0, g1))
    return _call(
        body, name=name, grid=grid,
        in_specs=[pl.BlockSpec((s, tm), lambda g0, g1: (0, blocks(g0, g1)[0])),
                  pl.BlockSpec((s, tn), lambda g0, g1: (0, blocks(g0, g1)[1]))],
        out_specs=[ospec, ospec],
        out_shape=[jax.ShapeDtypeStruct(shape, F32), jax.ShapeDtypeStruct(shape, BF16)],
        sem=("arbitrary", "arbitrary"), operands=(a, b), riders=riders, after=after,
    )


def _wgrad_send(a, b, tn, name, barrier_id, after=()):
    s, m = a.shape
    n = b.shape[1]
    nb, hm = n // tn, m // 2

    def body(*refs):
        a_ref, b_ref = refs[:2]
        o32_ref, land_ref, send_scr, send_sems, recv_sems = refs[2 + len(after):]
        j = pl.program_id(0)
        x, y, c = _mesh_pos()

        @pl.when(j == 0)
        def _():
            _shake_hands("sibling", wait=False)

        o32_ref[0] = _dot_tn(a_ref[...], b_ref[...])

        @pl.when(j == 0)
        def _():
            _shake_hands("sibling", signal=False)


        theirs = pl.ds(pl.multiple_of((1 - c) * hm, 16), hm)
        copies = [pltpu.make_async_remote_copy(
            src_ref=send_scr.at[blk], dst_ref=land_ref.at[blk], send_sem=send_sems.at[blk],
            recv_sem=recv_sems.at[blk], device_id=(x, y, 1 - c), device_id_type=MESH) for blk in range(nb)]
        for blk in range(nb):
            @pl.when(j == blk)
            def _(blk=blk):
                send_scr[blk] = o32_ref[0, theirs, :].astype(BF16)
                copies[blk].start()

        @pl.when(j == nb - 1)
        def _():
            for cp in copies:
                cp.wait()

    return pl.pallas_call(
        body, name=name, grid=(nb,),
        in_specs=[pl.BlockSpec((s, m), lambda j: (0, 0)), pl.BlockSpec((s, tn), lambda j: (0, j))]
        + [_whole()] * len(after),
        out_specs=[pl.BlockSpec((1, m, tn), lambda j: (j, 0, 0)), HBM_SPEC],
        out_shape=[jax.ShapeDtypeStruct((nb, m, tn), F32), jax.ShapeDtypeStruct((nb, hm, tn), BF16)],
        scratch_shapes=[pltpu.VMEM((nb, hm, tn), BF16), pltpu.SemaphoreType.DMA((nb,)),
                        pltpu.SemaphoreType.DMA((nb,))],
        compiler_params=pltpu.CompilerParams(dimension_semantics=("arbitrary",), vmem_limit_bytes=VMEM_LIMIT,
                                             collective_id=barrier_id),
    )(a, b, *after)


class _NoComm:
    def __init__(self, win4, wout, wup4, wdown):
        self.weights = dict(w_in=win4, w_out=wout, w_up=wup4, w_down=wdown)
        self.grads = {}

    def weight(self, name):
        return self.weights[name]

    def riders(self, call):
        return ()

    def after(self, call):
        return ()

    def landed(self, call, results, outs):
        pass

    def small_gradients(self, small, packed):
        pass

    def gradient(self, name, g32, g16):
        self.grads[name] = (g32, g16)

    def wgrad_in(self, xb, dproj):
        (g32, g16), _ = _wgrad(xb, dproj, D_MODEL, IN_SH, "wgrad_in", True, True)
        self.gradient("w_in", g32, g16)


def _local_step(x, target, cw, cb, wpool_b, pscale, g1, b1, g2, b2, comm):
    s = x.shape[0]
    ts_a = min(512, s)
    ts_f = min(256, s)
    mask, qd, kd = _decay_tables()
    cosf, sinf = _rope_tables(s)

    def run(call, fn, *args):
        outs, res = fn(*args, riders=comm.riders(call), after=comm.after(call))
        comm.landed(call, res, outs)
        return outs

    xb, q, k, v, g, pooled, cat = run("proj_pool", _proj_pool, x, comm.weight("w_in"), cosf, sinf, wpool_b,
                                      pscale, ts_a)
    ret, cat, states = run("retention_fwd", _retention_fwd, q, k, v, g, cat, mask, qd, kd)
    wout = comm.weight("w_out")
    xhat1, rstd1, h1b = run("outproj_ln1", _outproj_ln1, x, cat, wout, g1, b1, ts_a)
    wup4, wdown = comm.weight("w_up"), comm.weight("w_down")
    ub, act, sd, dz2, dz2b, loss, dg2, db2 = _ffn_fwd_loss(xhat1, h1b, target, wup4, wdown, cw, cb, g1, b1, g2, b2,
                                                           ts_f)

    dub, dz1, dz1b, dg1, db1, dcw, dcb = _ffn_bwd(dz2, dz2b, ub, sd, xhat1, rstd1, wup4, wdown, cw, g1, ts_f)
    half = D_MODEL // 2
    comm.gradient("w_up", *run("wgrad_up", _wgrad, h1b, dub, half, UP_SH, "wgrad_up", True, False))
    comm.gradient("w_out", *run("wgrad_out", _wgrad, cat, dz1b, D_MODEL, half, "wgrad_out", False, True))
    comm.gradient("w_down", *run("wgrad_down", _wgrad, act, dz2b, D_FF // 2, half, "wgrad_down", False, True))
    dret, dgp, dwp, dps, packed = run("mix_bwd", _mix_bwd, dz1b, pooled, ret, g, wout, wpool_b, pscale, loss,
                                      [dcb, dg1, db1, dg2, db2], ts_a)
    small = dict(w_pool=dwp, pool_scale=dps, ln1_g=dg1, ln1_b=db1, conv_w=dcw, conv_b=dcb,
                 ln2_g=dg2, ln2_b=db2)
    comm.small_gradients(small, packed)
    mask_both = jnp.stack([mask, jnp.swapaxes(mask, 1, 2)])
    dproj, = run("retention_bwd", _retention_bwd, q, k, v, dret, dgp, states, mask_both, qd, kd, cosf, sinf)
    comm.wgrad_in(xb, dproj)
    (grad_x,), _ = _dx(dz1, dproj, comm.weight("w_in"), ts_a, after=comm.after("dx"))
    return loss, grad_x, small


CAST_ROWS = 64
SHARD_SHAPES = ((D_MODEL, IN_SH), (OUT_SH, D_MODEL), (D_MODEL, UP_SH), (DOWN_SH, D_MODEL))
N_BIG = len(SHARD_SHAPES)
CW_SHARD = (3, 1, DOWN_SH)


def _mesh_pos():
    return lax.axis_index("x"), lax.axis_index("y"), lax.axis_index("c")


def _other_chips(x, y):
    return [(1 - x, y), (x, 1 - y), (1 - x, 1 - y)]


def _shake_hands(peers, signal=True, wait=True):
    x, y, c = _mesh_pos()
    others = [(x, y, 1 - c)] if peers in ("sibling", "both", "all") else []
    if peers in ("chips", "both", "all"):
        others += [(chip[0], chip[1], c) for chip in _other_chips(x, y)]
    if peers == "all":
        others += [(chip[0], chip[1], 1 - c) for chip in _other_chips(x, y)]
    barrier = pltpu.get_barrier_semaphore()
    if signal:
        for peer in others:
            pl.semaphore_signal(barrier, inc=1, device_id=peer, device_id_type=MESH)
    if wait:
        pl.semaphore_wait(barrier, len(others))


def _half_rows(w, which):
    hr = SHARD_SHAPES[w][0] // 2
    return pl.ds(pl.multiple_of(which * hr, 16), hr)


def _gather_weights(shards, cw_shard, wpool, full):
    def body(*refs):
        in_refs = refs[:N_BIG]
        cw_ref, wpool_ref = refs[N_BIG:N_BIG + 2]
        out_refs = refs[N_BIG + 2:2 * N_BIG + 2]
        cwo_ref, wpool_b_ref = refs[2 * N_BIG + 2:2 * N_BIG + 4]
        stage = refs[2 * N_BIG + 4:3 * N_BIG + 4]
        raw = refs[3 * N_BIG + 4:4 * N_BIG + 4 - len(full)]
        send_sems, recv_sems, fsend_sems, frecv_sems, cw_send, cw_recv, local_sems, load_sems = \
            refs[4 * N_BIG + 4 - len(full):]
        x, y, c = _mesh_pos()
        j0 = 2 * x + y
        chips = _other_chips(x, y)
        _shake_hands("both", wait=False)

        fetched = [w for w in range(N_BIG) if w not in full]
        f32 = {w: in_refs[w] for w in full}
        loads = []
        for n, w in enumerate(fetched):
            f32[w] = raw[n]
            loads.append(pltpu.make_async_copy(in_refs[w], raw[n], load_sems.at[n]))
            loads[-1].start()

        def cast_to_stage(w):
            def cast(i, carry):
                rows = pl.ds(pl.multiple_of(i * CAST_ROWS, CAST_ROWS), CAST_ROWS)
                stage[w][rows, :] = f32[w][rows, :].astype(BF16)
                return carry
            lax.fori_loop(0, SHARD_SHAPES[w][0] // CAST_ROWS, cast, 0)

        for w in full:
            cast_to_stage(w)

        jx, jy, jd = 2 * (1 - x) + y, 2 * x + (1 - y), 2 * (1 - x) + (1 - y)
        neighbours = [((1 - x, y, c), jx), ((x, 1 - y, c), jy)]
        passed = jnp.where(c == 0, jx, jy)
        pass_to = (jnp.where(c == 0, x, 1 - x), jnp.where(c == 0, 1 - y, y), c)

        def nbr(w, k, block):
            return pltpu.make_async_remote_copy(
                src_ref=stage[w].at[_half_rows(w, c), :], dst_ref=out_refs[w].at[block, _half_rows(w, c), :],
                send_sem=send_sems.at[w, k], recv_sem=recv_sems.at[w, k],
                device_id=neighbours[k][0], device_id_type=MESH)

        def relay(w, block):
            return pltpu.make_async_remote_copy(
                src_ref=out_refs[w].at[passed, _half_rows(w, c), :],
                dst_ref=out_refs[w].at[block, _half_rows(w, c), :],
                send_sem=send_sems.at[w, 2], recv_sem=recv_sems.at[w, 2],
                device_id=pass_to, device_id_type=MESH)

        def d2d(w, k, block, half):
            return pltpu.make_async_remote_copy(
                src_ref=out_refs[w].at[block, _half_rows(w, half), :],
                dst_ref=out_refs[w].at[block, _half_rows(w, half), :],
                send_sem=fsend_sems.at[w, k], recv_sem=frecv_sems.at[w, k],
                device_id=(x, y, 1 - c), device_id_type=MESH)

        def conv(k, block):
            chip = chips[k]
            return pltpu.make_async_remote_copy(
                src_ref=cw_ref, dst_ref=cwo_ref.at[block], send_sem=cw_send.at[k], recv_sem=cw_recv.at[k],
                device_id=(chip[0], chip[1], c), device_id_type=MESH)

        _shake_hands("both", signal=False)
        sent = [nbr(w, k, j0) for w in full for k in range(2)] + [conv(k, j0) for k in range(3)]
        for cp in sent:
            cp.start()
        for n, w in enumerate(fetched):
            loads[n].wait()
            cast_to_stage(w)
        local = [pltpu.make_async_copy(stage[w], out_refs[w].at[j0], local_sems.at[w]) for w in range(N_BIG)]
        local.append(pltpu.make_async_copy(cw_ref, cwo_ref.at[j0], local_sems.at[N_BIG]))
        for cp in local:
            cp.start()
        wpool_b_ref[...] = wpool_ref[...].astype(BF16)
        for w in full:
            for k, (_, block) in enumerate(neighbours):
                nbr(w, k, block).wait_recv()
            later = [relay(w, passed)] + [d2d(w, k, block, c) for k, (_, block) in enumerate(neighbours)]
            for cp in later:
                cp.start()
            sent += later
        for w in full:
            relay(w, jd).wait_recv()
            fw = d2d(w, 2, jd, c)
            fw.start()
            sent.append(fw)
        for w in full:
            for k, block in enumerate([jx, jy, jd]):
                d2d(w, k, block, 1 - c).wait_recv()
        for k, chip in enumerate(chips):
            conv(k, 2 * chip[0] + chip[1]).wait_recv()
        for cp in sent:
            cp.wait_send()
        for cp in local:
            cp.wait()

    out_shape = [jax.ShapeDtypeStruct((N_SHARD,) + shp, BF16) for shp in SHARD_SHAPES]
    out_shape.append(jax.ShapeDtypeStruct((N_SHARD,) + CW_SHARD, F32))
    out_shape.append(jax.ShapeDtypeStruct(wpool.shape, BF16))
    return pl.pallas_call(
        body, name="gather_weights",
        in_specs=[_whole() if w in full else HBM_SPEC for w in range(N_BIG)] + [_whole()] * 2,
        out_specs=[HBM_SPEC] * (N_BIG + 1) + [_whole()],
        out_shape=out_shape,
        scratch_shapes=[pltpu.VMEM(shp, BF16) for shp in SHARD_SHAPES]
        + [pltpu.VMEM(shp, F32) for w, shp in enumerate(SHARD_SHAPES) if w not in full] + [
            pltpu.SemaphoreType.DMA((N_BIG, 3)), pltpu.SemaphoreType.DMA((N_BIG, 3)),
            pltpu.SemaphoreType.DMA((N_BIG, 3)), pltpu.SemaphoreType.DMA((N_BIG, 3)),
            pltpu.SemaphoreType.DMA((3,)), pltpu.SemaphoreType.DMA((3,)),
            pltpu.SemaphoreType.DMA((N_BIG + 1,)), pltpu.SemaphoreType.DMA((N_BIG - len(full),))],
        compiler_params=pltpu.CompilerParams(vmem_limit_bytes=VMEM_LIMIT, collective_id=13),
    )(*shards, cw_shard, wpool)


def _gather_rider(arrays, ops, handshake=None):
    ws = sorted(arrays)

    def make(inplace, srcs, lands, send_sems, recv_sems):
        del srcs, lands
        x, y, c = _mesh_pos()
        j0, jx, jy, jd = 2 * x + y, 2 * (1 - x) + y, 2 * x + (1 - y), 2 * (1 - x) + (1 - y)
        x_nbr, y_nbr, sibling = (1 - x, y, c), (x, 1 - y, c), (x, y, 1 - c)
        starts, waits = [], []
        for n, (kind, w, (r0, nr)) in enumerate(ops):
            ref = inplace[ws.index(w)]
            hr = SHARD_SHAPES[w][0] // 2
            rows = lambda core: pl.ds(pl.multiple_of(core * hr + r0, 16), nr)
            mine, theirs = rows(c), rows(1 - c)
            if kind == "ici":
                moves = [(ref.at[j0, mine, :], x_nbr, ref.at[jx, mine, :]),
                         (ref.at[j0, mine, :], y_nbr, ref.at[jy, mine, :]),
                         (ref.at[j0, mine, :], (1 - x, 1 - y, c), ref.at[jd, mine, :])]
            elif kind == "nbr":
                moves = [(ref.at[j0, mine, :], x_nbr, ref.at[jx, mine, :]),
                         (ref.at[j0, mine, :], y_nbr, ref.at[jy, mine, :])]
            elif kind == "relay":
                passed = jnp.where(c == 0, jx, jy)
                to = (jnp.where(c == 0, x, 1 - x), jnp.where(c == 0, 1 - y, y), c)
                moves = [(ref.at[passed, mine, :], to, ref.at[jd, mine, :])]
            else:
                blocks = dict(d2d=[jx, jy, jd], d2d_nbr=[jx, jy], d2d_diag=[jd])[kind]
                moves = [(ref.at[b, mine, :], sibling, ref.at[b, theirs, :]) for b in blocks]
            for k, (src, to, landing) in enumerate(moves):
                sems = dict(send_sem=send_sems.at[3 * n + k], recv_sem=recv_sems.at[3 * n + k],
                            device_id=to, device_id_type=MESH)
                send = pltpu.make_async_remote_copy(src_ref=src, dst_ref=src, **sems)
                arrival = pltpu.make_async_remote_copy(src_ref=src, dst_ref=landing, **sems)
                starts.append(send)
                waits += [arrival.wait_recv, send.wait_send]
        return starts, waits

    return _Rider([arrays[w] for w in ws], [], [], 3 * len(ops), make, handshake)


def _whole_half(w):
    return (0, SHARD_SHAPES[w][0] // 2)


def _pair_rider(ws, g16s):
    def make(inplace, srcs, lands, send_sems, recv_sems):
        del inplace
        x, y, c = _mesh_pos()
        copies = [pltpu.make_async_remote_copy(
            src_ref=srcs[i].at[:, _half_rows(w, 1 - c), :], dst_ref=lands[i],
            send_sem=send_sems.at[i], recv_sem=recv_sems.at[i], device_id=(x, y, 1 - c), device_id_type=MESH)
            for i, w in enumerate(ws)]
        return copies, [cp.wait for cp in copies]

    lands = [jax.ShapeDtypeStruct((N_SHARD, SHARD_SHAPES[w][0] // 2, SHARD_SHAPES[w][1]), BF16) for w in ws]
    return _Rider([], g16s, lands, len(ws), make)


def _chip_rider(ws, p16s):
    def make(inplace, srcs, lands, send_sems, recv_sems):
        del inplace
        x, y, c = _mesh_pos()
        copies = []
        for i in range(len(ws)):
            for k, chip in enumerate(_other_chips(x, y)):
                copies.append(pltpu.make_async_remote_copy(
                    src_ref=srcs[i].at[2 * chip[0] + chip[1]], dst_ref=lands[i].at[k],
                    send_sem=send_sems.at[3 * i + k], recv_sem=recv_sems.at[3 * i + k],
                    device_id=(chip[0], chip[1], c), device_id_type=MESH))
        return copies, [cp.wait for cp in copies]

    lands = [jax.ShapeDtypeStruct((3, SHARD_SHAPES[w][0] // 2, SHARD_SHAPES[w][1]), BF16) for w in ws]
    return _Rider([], p16s, lands, 3 * len(ws), make)


def _final_rider(halves):
    def make(inplace, srcs, lands, send_sems, recv_sems):
        del inplace
        x, y, c = _mesh_pos()
        copies = [pltpu.make_async_remote_copy(
            src_ref=srcs[i], dst_ref=lands[i], send_sem=send_sems.at[i], recv_sem=recv_sems.at[i],
            device_id=(x, y, 1 - c), device_id_type=MESH) for i in range(len(halves))]
        return copies, [cp.wait for cp in copies]

    return _Rider([], halves, [jax.ShapeDtypeStruct(h.shape, h.dtype) for h in halves], len(halves), make)


N_DEV = 2 * N_SHARD


def _device_number(x, y, c):
    return 2 * (2 * x + y) + c


def _small_all_rider(own):
    n = len(own)

    def make(inplace, srcs, lands, send_sems, recv_sems):
        del inplace
        x, y, c = _mesh_pos()
        peers = [(x, y, 1 - c)] + [(chip[0], chip[1], core) for chip in _other_chips(x, y) for core in (c, 1 - c)]
        copies = []
        for i in range(n):
            for k, peer in enumerate(peers):
                copies.append(pltpu.make_async_remote_copy(
                    src_ref=srcs[i], dst_ref=lands[i].at[_device_number(x, y, c)],
                    send_sem=send_sems.at[(N_DEV - 1) * i + k], recv_sem=recv_sems.at[(N_DEV - 1) * i + k],
                    device_id=peer, device_id_type=MESH))
        return copies, [cp.wait for cp in copies]

    lands = [jax.ShapeDtypeStruct((N_DEV,) + a.shape, a.dtype) for a in own]
    return _Rider([], own, lands, (N_DEV - 1) * n, make)


def _comm_only(name, riders):
    _, res = _call(lambda: None, name=name, grid=(), in_specs=[], out_specs=[], out_shape=[], operands=(),
                   riders=riders)
    return res


class _SemList:
    def __init__(self, refs):
        self.at = list(refs)


def _merged_rider(riders):
    srcs = [a for r in riders for a in r.srcs]
    lands = [a for r in riders for a in r.lands]

    def make(inplace, src_refs, land_refs, send_sems, recv_sems):
        starts, waits = [], []
        s0 = l0 = c0 = 0
        for r in riders:
            part = r.make(inplace, src_refs[s0:s0 + len(r.srcs)], land_refs[l0:l0 + len(r.lands)],
                          _SemList(send_sems.at[c0:c0 + r.n_copies]), _SemList(recv_sems.at[c0:c0 + r.n_copies]))
            starts += part[0]
            waits += part[1]
            s0, l0, c0 = s0 + len(r.srcs), l0 + len(r.lands), c0 + r.n_copies
        return starts, waits

    return _Rider([], srcs, lands, sum(r.n_copies for r in riders), make)


def _split_start(name, rider, handshake=None):
    assert not rider.inplace
    ns, nl, n = len(rider.srcs), len(rider.lands), rider.n_copies
    barrier_id, peers = handshake if handshake is not None else (None, None)

    def body(*refs):
        if handshake is not None:
            _shake_hands(peers)
        srcs, lands = refs[:ns], refs[ns:ns + nl]
        sems = refs[ns + nl:ns + nl + 2 * n]
        token = refs[-1]
        starts, _ = rider.make([], srcs, lands, _SemList(sems[:n]), _SemList(sems[n:]))
        for cp in starts:
            cp.start()
        token[...] = jnp.zeros_like(token)

    buffers = [pltpu.with_memory_space_constraint(a, pltpu.HBM) for a in rider.srcs]
    buffers += [pltpu.with_memory_space_constraint(lax.empty(s.shape, s.dtype), pltpu.HBM) for s in rider.lands]
    hbm = pl.BlockSpec(memory_space=pltpu.HBM)
    sem = pl.BlockSpec(memory_space=pltpu.SEMAPHORE)
    outs = pl.pallas_call(
        body, name=name,
        out_shape=tuple([pltpu.SemaphoreType.DMA(())] * (2 * n) + [pltpu.HBM(b.shape, b.dtype) for b in buffers]
                        + [jax.ShapeDtypeStruct((8, 128), F32)]),
        in_specs=[hbm] * (ns + nl),
        out_specs=tuple([sem] * (2 * n) + [hbm] * (ns + nl) + [_whole()]),
        input_output_aliases={i: 2 * n + i for i in range(ns + nl)},
        compiler_params=pltpu.CompilerParams(has_side_effects=pltpu.SideEffectType.DATAFLOW_SIDE_EFFECTING,
                                             collective_id=barrier_id),
    )(*buffers)
    return (rider, outs[:2 * n], outs[2 * n:2 * n + ns + nl]), outs[-1]


def _split_parts(state, riders):
    merged, sems, buffers = state
    n, ns = merged.n_copies, len(merged.srcs)
    parts, s0, l0, c0 = [], 0, 0, 0
    for r in riders:
        parts.append((r, list(sems[c0:c0 + r.n_copies]) + list(sems[n + c0:n + c0 + r.n_copies]),
                      list(buffers[s0:s0 + len(r.srcs)]) + list(buffers[ns + l0:ns + l0 + len(r.lands)])))
        s0, l0, c0 = s0 + len(r.srcs), l0 + len(r.lands), c0 + r.n_copies
    return parts


def _split_wait(name, state, after):
    rider, sems, buffers = state
    ns, nl, n = len(rider.srcs), len(rider.lands), rider.n_copies

    def body(*refs):
        srcs, lands = refs[:ns], refs[ns:ns + nl]
        sem_refs = refs[ns + nl:ns + nl + 2 * n]
        _, waits = rider.make([], srcs, lands, _SemList(sem_refs[:n]), _SemList(sem_refs[n:]))
        for wait in waits:
            wait()

    hbm = pl.BlockSpec(memory_space=pltpu.HBM)
    sem = pl.BlockSpec(memory_space=pltpu.SEMAPHORE)
    outs = pl.pallas_call(
        body, name=name,
        out_shape=tuple(pltpu.HBM(b.shape, b.dtype) for b in buffers),
        in_specs=[hbm] * (ns + nl) + [sem] * (2 * n) + [HBM_SPEC],
        out_specs=tuple([hbm] * (ns + nl)),
        input_output_aliases={i: i for i in range(ns + nl)},
        compiler_params=pltpu.CompilerParams(has_side_effects=pltpu.SideEffectType.DATAFLOW_SIDE_EFFECTING),
    )(*buffers, *sems, after)
    return list(outs[:ns]), list(outs[ns:])


def _pair_sum(pos, ws, g32s, recvs):
    n = len(ws)

    def body(pos_ref, *refs):
        del pos_ref
        g_refs, r_refs = refs[:n], refs[n:2 * n]
        p32_refs, p16_refs = refs[2 * n:3 * n], refs[3 * n:]
        x, y, _ = _mesh_pos()
        for i in range(n):
            tot = g_refs[i][...] + r_refs[i][...].astype(F32)
            p16_refs[i][...] = tot.astype(BF16)

            @pl.when(pl.program_id(0) == 2 * x + y)
            def _(i=i, tot=tot):
                p32_refs[i][...] = tot

    halves = [(SHARD_SHAPES[w][0] // 2, SHARD_SHAPES[w][1]) for w in ws]
    own = [pl.BlockSpec((None, None) + h, lambda j, pos_ref: (j, pos_ref[0], 0, 0)) for h in halves]
    blk = [pl.BlockSpec((None,) + h, lambda j, pos_ref: (j, 0, 0)) for h in halves]
    mine = [pl.BlockSpec(h, lambda j, pos_ref: (0, 0)) for h in halves]
    g4 = [g.reshape((N_SHARD, 2) + h) for g, h in zip(g32s, halves)]
    outs = pl.pallas_call(
        body, name="pair_sum_" + "_".join(str(w) for w in ws),
        grid_spec=pltpu.PrefetchScalarGridSpec(
            num_scalar_prefetch=1, grid=(N_SHARD,), in_specs=own + blk, out_specs=mine + blk),
        out_shape=[jax.ShapeDtypeStruct(h, F32) for h in halves]
        + [jax.ShapeDtypeStruct((N_SHARD,) + h, BF16) for h in halves],
        compiler_params=_params(("arbitrary",)),
    )(pos, *g4, *recvs)
    return outs[:n], outs[n:]


def _chip_sum(p32s, recvs):
    parts = 2

    def body(*refs):
        p_refs, r_refs, f_refs = refs[:N_BIG], refs[N_BIG:2 * N_BIG], refs[2 * N_BIG:]
        for w in range(N_BIG):
            f_refs[w][...] = ((p_refs[w][...] + r_refs[w][0].astype(F32)) + r_refs[w][1].astype(F32)) \
                + r_refs[w][2].astype(F32)

    quarters = [(r // 2 // parts, cc) for r, cc in SHARD_SHAPES]
    own = [pl.BlockSpec(qt, lambda i: (i, 0)) for qt in quarters]
    rcv = [pl.BlockSpec((3,) + qt, lambda i: (0, i, 0)) for qt in quarters]
    out = [pl.BlockSpec(qt, lambda i: (i, 0)) for qt in quarters]
    return pl.pallas_call(
        body, name="chip_sum", grid=(parts,), in_specs=own + rcv, out_specs=out,
        out_shape=[jax.ShapeDtypeStruct((r // 2, cc), F32) for r, cc in SHARD_SHAPES],
        compiler_params=_params(("arbitrary",)),
    )(*p32s, *recvs)


def _adamw(w, g, m, v):
    m_new = ADAM_B1 * m + (1.0 - ADAM_B1) * g
    v_new = ADAM_B2 * v + (1.0 - ADAM_B2) * (g * g)
    m_hat = m_new / (1.0 - ADAM_B1 ** ADAM_STEP)
    v_hat = v_new / (1.0 - ADAM_B2 ** ADAM_STEP)
    delta = -ADAM_LR * (m_hat / (jnp.sqrt(v_hat) + ADAM_EPS) + ADAM_WD * w)
    return delta, m_new, v_new


def _adam_half(name, pos, grads, ws, ms, vs, into=None):
    nb = 4
    which = (lambda ref: ref[0]) if into is None else (lambda ref: 1 - ref[0])

    def body(which_ref, *refs):
        del which_ref
        groups = [refs[i * N_BIG:(i + 1) * N_BIG] for i in range(4)]
        g_refs, w_refs, m_refs, v_refs = groups
        go_refs, do_refs, mo_refs, vo_refs = [refs[len(refs) - (4 - i) * N_BIG:len(refs) - (3 - i) * N_BIG]
                                              for i in range(4)]
        for w in range(N_BIG):
            g = g_refs[w][...]
            delta, m_new, v_new = _adamw(w_refs[w][...], g, m_refs[w][...], v_refs[w][...])
            go_refs[w][...] = g
            do_refs[w][...] = delta
            mo_refs[w][...] = m_new
            vo_refs[w][...] = v_new

    blocks = [(r // 2 // nb, cc) for r, cc in SHARD_SHAPES]
    half = [pl.BlockSpec(b, lambda i, which_ref: (i, 0)) for b in blocks]
    full = [pl.BlockSpec((None,) + b, lambda i, which_ref: (0, which(which_ref) * nb + i, 0)) for b in blocks]
    shapes = [jax.ShapeDtypeStruct((1,) + shp, F32) for shp in SHARD_SHAPES]
    carried = [] if into is None else [a for kind in into for a in kind]
    first = 1 + 4 * N_BIG
    outs = pl.pallas_call(
        body, name=name,
        grid_spec=pltpu.PrefetchScalarGridSpec(
            num_scalar_prefetch=1, grid=(nb,), in_specs=half + full * 3 + [HBM_SPEC] * len(carried),
            out_specs=full * 4),
        out_shape=shapes * 4,
        input_output_aliases={first + i: i for i in range(len(carried))},
        compiler_params=_params(("arbitrary",)),
    )(pos, *grads, *ws, *ms, *vs, *carried)
    return [outs[i * N_BIG:(i + 1) * N_BIG] for i in range(4)]


SMALL_ROWS = 8
ROW_CONV_B, ROW_POOL_SCALE, ROW_LN1_G, ROW_LN1_B, ROW_LN2_G, ROW_LN2_B, ROW_LOSS = range(7)
SMALL_VECS = ((ROW_CONV_B, D_FF), (ROW_POOL_SCALE, POOL_W), (ROW_LN1_G, D_MODEL), (ROW_LN1_B, D_MODEL),
              (ROW_LN2_G, D_MODEL), (ROW_LN2_B, D_MODEL))


def _small_adam(all_a, all_b, all_c, own_a, own_b, own_c, wp, cwp, vec_ws, m_wp, m_cwp, vec_ms,
                v_wp, v_cwp, vec_vs):
    nv = len(SMALL_VECS)
    np_ = 2 + nv

    def body(*refs):
        all_a_ref, all_b_ref, all_c_ref, own_a_ref, own_b_ref, own_c_ref = refs[0:6]
        refs = refs[3:]
        w_all, m_all, v_all = (refs[3 + i * np_:3 + (i + 1) * np_] for i in range(3))
        loss_out = refs[3 + 3 * np_]
        outs = refs[4 + 3 * np_:]
        x, y, c = _mesh_pos()
        j0 = 2 * x + y
        me = _device_number(x, y, c)

        def total(sent, own):
            by_dev = [jnp.where(me == d, own, sent(d)) for d in range(N_DEV)]
            chips = [by_dev[2 * j] + by_dev[2 * j + 1] for j in range(N_SHARD)]
            return ((chips[0] + chips[1]) + chips[2]) + chips[3]

        tot_a = total(lambda d: all_a_ref[d], own_a_ref[...])
        tot_b = total(lambda d: all_b_ref[d], own_b_ref[...])
        tot_c = total(lambda d: all_c_ref[d, j0], own_c_ref[j0])
        loss_out[...] = tot_b[ROW_LOSS:ROW_LOSS + 1, 0:1]
        grads = [tot_a, tot_c] + [tot_b[row:row + 1, 0:n] for row, n in SMALL_VECS]
        for p in range(np_):
            for at, g in ([(j, tot_c[j:j + 1]) for j in range(3)] if p == 1 else [(Ellipsis, grads[p])]):
                delta, m_new, v_new = _adamw(w_all[p][at], g, m_all[p][at], v_all[p][at])
                outs[p][at] = g
                outs[np_ + p][at] = delta
                outs[2 * np_ + p][at] = m_new
                outs[3 * np_ + p][at] = v_new

    pshapes = [wp.shape, CW_SHARD] + [wv.shape for wv in vec_ws]
    out_shape = [jax.ShapeDtypeStruct((1, 1), F32)] + [jax.ShapeDtypeStruct(s, F32) for s in pshapes] * 4
    outs = pl.pallas_call(
        body, name="small_adam",
        in_specs=[_whole()] * (6 + 3 * np_), out_specs=[_whole()] * len(out_shape), out_shape=out_shape,
        compiler_params=pltpu.CompilerParams(vmem_limit_bytes=VMEM_LIMIT),
    )(all_a, all_b, all_c, own_a, own_b, own_c, wp, cwp, *vec_ws, m_wp, m_cwp, *vec_ms, v_wp, v_cwp, *vec_vs)
    return outs[0], [outs[1 + i * np_:1 + (i + 1) * np_] for i in range(4)]


def kernel(x, w_in, w_pool, pool_scale, w_out, ln1_g, ln1_b, w_up, conv_w, conv_b, w_down, ln2_g, ln2_b, loss_target, m_w_in, m_w_pool, m_pool_scale, m_w_out, m_ln1_g, m_ln1_b, m_w_up, m_conv_w, m_conv_b, m_w_down, m_ln2_g, m_ln2_b, v_w_in, v_w_pool, v_pool_scale, v_w_out, v_ln1_g, v_ln1_b, v_w_up, v_conv_w, v_conv_b, v_w_down, v_ln2_g, v_ln2_b):
    pos = lax.axis_index("c").astype(jnp.int32).reshape(1)
    order = ("w_in", "w_out", "w_up", "w_down")
    w_in_i, w_out_i, w_up_i, w_down_i = range(N_BIG)
    vec_names = ("conv_b", "pool_scale", "ln1_g", "ln1_b", "ln2_g", "ln2_b")

    taps_first = lambda a: jnp.transpose(a, (1, 0, 2))
    gathered = _gather_weights([w_in[0], w_out[0], w_up[0], w_down[0]], taps_first(conv_w), w_pool[0], (w_in_i,))
    cw_full = jnp.transpose(gathered[N_BIG].reshape(N_SHARD, 3, DOWN_SH), (1, 0, 2)).reshape(3, D_FF)
    up_a, up_b, up_c = (0, 192), (192, 192), (384, 128)
    assert up_c[0] + up_c[1] == SHARD_SHAPES[w_up_i][0] // 2

    class MeshComm:
        def __init__(self):
            self.w = {i: gathered[i] for i in range(N_BIG)}
            self.g32, self.g16, self.p32, self.p16, self.recv_b = {}, {}, {}, {}, {}
            self.up_complete = False
            self.tokens, self.chips = {}, []

        def weight(self, name):
            i = order.index(name)
            if name == "w_up" and not self.up_complete:
                (arrs, _), = _comm_only("gather_up_last", [_gather_rider(
                    {i: self.w[i]}, [("d2d_diag", i, up_b), ("d2d", i, up_c)], (12, "sibling"))])
                self.w[i], self.up_complete = arrs[0], True
            full = self.w[i]
            return full.reshape(-1, full.shape[-1]) if name in ("w_out", "w_down") else full

        def _gather(self, ws, ops, handshake):
            return _gather_rider({w: self.w[w] for w in ws}, ops, handshake), ("w", ws)

        def _pair(self, ws):
            return _pair_rider(ws, [self.g16[w] for w in ws]), ("recv_a", ws)

        def _chip(self, ws):
            return _chip_rider(ws, [self.p16[w] for w in ws]), ("recv_b", ws)

        def plan(self, call):
            out_all, down_all = _whole_half(w_out_i), _whole_half(w_down_i)
            if call == "proj_pool":
                return [self._gather([w_out_i, w_up_i, w_down_i],
                                     [("ici", w_out_i, out_all), ("nbr", w_down_i, down_all),
                                      ("nbr", w_up_i, up_a)], (9, "chips"))]
            if call == "retention_fwd":
                return [self._gather([w_out_i, w_up_i, w_down_i],
                                     [("d2d", w_out_i, out_all),
                                      ("relay", w_down_i, down_all), ("d2d_nbr", w_down_i, down_all),
                                      ("relay", w_up_i, up_a), ("d2d_nbr", w_up_i, up_a), ("nbr", w_up_i, up_b)],
                                     (10, "both"))]
            if call == "outproj_ln1":
                return [self._gather([w_up_i, w_down_i],
                                     [("d2d_diag", w_down_i, down_all), ("d2d_diag", w_up_i, up_a),
                                      ("relay", w_up_i, up_b), ("d2d_nbr", w_up_i, up_b), ("ici", w_up_i, up_c)],
                                     (11, "both"))]
            return []

        def after(self, call):
            return tuple(self.tokens.pop(call, ()))

        def riders(self, call):
            self.pending = self.plan(call)
            return [r for r, _ in self.pending]

        def _start(self, name, rider, before, handshake):
            state, token = _split_start(name, rider, handshake)
            self.tokens.setdefault(before, []).append(token)
            return state

        def _finish_pair(self, name, state, ws, after):
            _, lands = _split_wait(name, state, after)
            self._finish_sum(ws, lands)

        def landed(self, call, results, outs):
            for (_, (slot, ws)), (inplace, lands) in zip(self.pending, results):
                for w, arr in zip(ws, inplace if len(inplace) else lands):
                    getattr(self, slot)[w] = arr
            if call == "wgrad_out":
                self._finish_pair("pair_exchange_up_wait", self.pair_up, [w_up_i], outs[1])
                self.chips.append(([w_up_i], self._start(
                    "chip_exchange_up_start", self._chip([w_up_i])[0], "wgrad_down", (5, "chips"))))
            if call == "mix_bwd":
                ws = [w_out_i, w_down_i]
                self._finish_pair("pair_exchange_out_down_wait", self.pair_out_down, ws, outs[0])

        def small_gradients(self, small, packed):
            dcw4 = jnp.transpose(small["conv_w"].reshape(3, N_SHARD, DOWN_SH), (1, 0, 2))
            own = [small["w_pool"], packed, dcw4]
            ws = [w_out_i, w_down_i]
            parts = [self._chip(ws)[0], _small_all_rider(own)]
            chip, self.small_all = _split_parts(
                self._start("chip_out_down_small_all_start", _merged_rider(parts), "retention_bwd",
                            (7, "all")), parts)
            self.chips.append((ws, chip))

        def gradient(self, name, g32, g16):
            w = order.index(name)
            shape = (N_SHARD,) + SHARD_SHAPES[w]
            self.g32[w], self.g16[w] = g32.reshape(shape), g16.reshape(shape)
            if name == "w_up":
                self.pair_up = self._start("pair_exchange_up_start", self._pair([w])[0], "wgrad_out",
                                           (1, "sibling"))
            if name == "w_down":
                self.pair_out_down = self._start("pair_exchange_out_down_start",
                                                 self._pair([w_out_i, w_down_i])[0], "mix_bwd", (2, "sibling"))

        def wgrad_in(self, xb, dproj):
            w = w_in_i
            g32, landed = _wgrad_send(xb, dproj, IN_SH, "wgrad_in", 4, after=self.after("wgrad_in"))
            self.g32[w] = g32
            self._finish_sum([w], [landed])
            self.chips.append(([w], self._start("chip_exchange_in_start", self._chip([w])[0], "dx", (8, "chips"))))

        def _finish_sum(self, ws, lands):
            p32s, p16s = _pair_sum(pos, ws, [self.g32[w] for w in ws], lands)
            for w, p32, p16 in zip(ws, p32s, p16s):
                self.p32[w], self.p16[w] = p32, p16

        def finish(self, after):
            for n, (ws, state) in enumerate(self.chips):
                _, lands = _split_wait("chip_exchange_wait_%d" % n, state, after)
                for w, arr in zip(ws, lands):
                    self.recv_b[w] = arr
            own, sent = _split_wait("small_all_wait", self.small_all, after)
            return list(sent) + list(own)

    comm = MeshComm()
    loss, grad_x, small = _local_step(x[0], loss_target[0], cw_full, conv_b, gathered[N_BIG + 1], pool_scale,
                                      ln1_g, ln1_b, ln2_g, ln2_b, comm)

    given = dict(w_pool=w_pool, pool_scale=pool_scale, ln1_g=ln1_g, ln1_b=ln1_b, conv_w=conv_w, conv_b=conv_b,
                 ln2_g=ln2_g, ln2_b=ln2_b)
    given_m = dict(w_pool=m_w_pool, pool_scale=m_pool_scale, ln1_g=m_ln1_g, ln1_b=m_ln1_b, conv_w=m_conv_w,
                   conv_b=m_conv_b, ln2_g=m_ln2_g, ln2_b=m_ln2_b)
    given_v = dict(w_pool=v_w_pool, pool_scale=v_pool_scale, ln1_g=v_ln1_g, ln1_b=v_ln1_b, conv_w=v_conv_w,
                   conv_b=v_conv_b, ln2_g=v_ln2_g, ln2_b=v_ln2_b)
    args = []
    for src in (given, given_m, given_v):
        args += [src["w_pool"][0], taps_first(src["conv_w"]), [src[n] for n in vec_names]]
    small_sums = comm.finish(grad_x)
    loss_tot, small_out = _small_adam(*small_sums, *args)
    every = range(N_BIG)
    mine = _chip_sum([comm.p32[w] for w in every], [comm.recv_b[w] for w in every])
    final_state, _ = _split_start("pair_exchange_f32_start", _final_rider(mine), (3, "sibling"))
    mine = final_state[2][:N_BIG]
    big = ([w_in, w_out, w_up, w_down], [m_w_in, m_w_out, m_w_up, m_w_down], [v_w_in, v_w_out, v_w_up, v_w_down])
    own_half = _adam_half("adam_own_half", pos, mine, *big)
    _, theirs = _split_wait("pair_exchange_f32_wait", final_state, own_half[0][0])
    big_out = _adam_half("adam_other_half", pos, theirs, *big, into=own_half)

    names = ("w_in", "w_pool", "pool_scale", "w_out", "ln1_g", "ln1_b", "w_up", "conv_w", "conv_b", "w_down",
             "ln2_g", "ln2_b")
    small_names = ("w_pool", "conv_w") + vec_names
    result = [loss_tot.reshape(()), grad_x[None]]
    for kind in range(4):
        for n in names:
            if n in order:
                result.append(big_out[kind][order.index(n)])
            else:
                val = small_out[kind][small_names.index(n)]
                if n == "conv_w":
                    val = taps_first(val)
                elif n == "w_pool":
                    val = val[None]
                result.append(val)
    return tuple(result)
```

```python
import functools

import numpy as np
import jax
import jax.numpy as jnp
from jax import lax
from jax.experimental import pallas as pl
from jax.experimental.pallas import tpu as pltpu

F32 = jnp.float32
BF16 = jnp.bfloat16

D_MODEL = 1024
HEADS = 4
HEAD_DIM = 128
RET_W = HEADS * HEAD_DIM
POOL_WINDOWS = (2, 4, 8, 16)
POOL_W = 512
IN_W = 4 * RET_W + POOL_W
D_FF = 2816
N_SHARD = 4
IN_SH = IN_W // N_SHARD
UP_SH = 2 * D_FF // N_SHARD
DOWN_SH = D_FF // N_SHARD
OUT_SH = D_MODEL // N_SHARD
ROPE_BASE = 10000.0
LN_EPS = 1e-5
RMS_EPS = 1e-6
ALPHA = 2.0 ** 0.25
K_SCALE = HEAD_DIM ** -0.5
SUPER = 256
CHUNK = 64
POOL_HALO = 16
CONV_HALO = 8
FFN_STRIP = 128
LN_ROWS = 32

ADAM_LR = 0.001
ADAM_B1 = 0.9
ADAM_B2 = 0.999
ADAM_EPS = 1e-08
ADAM_WD = 0.01
ADAM_STEP = 10

MESH = pl.DeviceIdType.MESH
VMEM_LIMIT = 56 * 1024 * 1024


def _dot(a, b):
    return jnp.dot(a, b, preferred_element_type=F32)


def _dot_nt(a, b):
    return lax.dot_general(a, b, (((1,), (1,)), ((), ())), preferred_element_type=F32)


def _dot_tn(a, b):
    return lax.dot_general(a, b, (((0,), (0,)), ((), ())), preferred_element_type=F32)


def _sigmoid(x):
    return 1.0 / (1.0 + jnp.exp(-x))


def _params(sem):
    return pltpu.CompilerParams(dimension_semantics=sem, vmem_limit_bytes=VMEM_LIMIT)


def _whole():
    return pl.BlockSpec(memory_space=pltpu.VMEM)


HBM_SPEC = pl.BlockSpec(memory_space=pl.ANY)


class _Rider:
    def __init__(self, inplace, srcs, lands, n_copies, make, handshake=None):
        self.inplace, self.srcs, self.lands, self.n_copies, self.make = list(inplace), list(srcs), list(lands), n_copies, make
        self.handshake = handshake


def _call(body, *, name, grid, in_specs, out_specs, out_shape, operands, scratch_shapes=(), sem=(),
          aliases=None, riders=(), after=()):
    n_in, n_out, n_scr = len(in_specs), len(out_shape), len(scratch_shapes)
    in_specs, out_specs, out_shape = list(in_specs), list(out_specs), list(out_shape)
    operands, scratch_shapes, aliases = list(operands), list(scratch_shapes), dict(aliases or {})
    in_specs += [_whole()] * len(after)
    operands += list(after)
    shakes = [r.handshake for r in riders if r.handshake is not None]
    assert len(shakes) <= 1
    for r in riders:
        for a in r.inplace:
            aliases[len(in_specs)] = len(out_shape)
            in_specs.append(HBM_SPEC)
            operands.append(a)
            out_specs.append(HBM_SPEC)
            out_shape.append(jax.ShapeDtypeStruct(a.shape, a.dtype))
        for a in r.srcs:
            in_specs.append(HBM_SPEC)
            operands.append(a)
        for shp in r.lands:
            out_specs.append(HBM_SPEC)
            out_shape.append(shp)
        scratch_shapes += [pltpu.SemaphoreType.DMA((r.n_copies,)), pltpu.SemaphoreType.DMA((r.n_copies,))]

    def full(*refs):
        ins = refs[:n_in]
        at = n_in + len(after)
        r_srcs = []
        for r in riders:
            at += len(r.inplace)
            r_srcs.append(refs[at:at + len(r.srcs)])
            at += len(r.srcs)
        outs = refs[at:at + n_out]
        at += n_out
        r_outs = []
        for r in riders:
            r_outs.append((refs[at:at + len(r.inplace)], refs[at + len(r.inplace):at + len(r.inplace) + len(r.lands)]))
            at += len(r.inplace) + len(r.lands)
        scr = refs[at:at + n_scr]
        at += n_scr
        r_sems = [refs[at + 2 * i:at + 2 * i + 2] for i in range(len(riders))]

        def copies():
            return [r.make(r_outs[i][0], r_srcs[i], r_outs[i][1], r_sems[i][0], r_sems[i][1])
                    for i, r in enumerate(riders)]

        def start():
            if shakes:
                _shake_hands(shakes[0][1])
            for starts, _ in copies():
                for cp in starts:
                    cp.start()

        def finish():
            for _, waits in copies():
                for wait in waits:
                    wait()

        if riders and grid:
            first = functools.reduce(jnp.logical_and, [pl.program_id(d) == 0 for d in range(len(grid))])
            last = functools.reduce(jnp.logical_and, [pl.program_id(d) == grid[d] - 1 for d in range(len(grid))])
            pl.when(first)(start)
            body(*ins, *outs, *scr)
            pl.when(last)(finish)
        else:
            if riders:
                start()
            body(*ins, *outs, *scr)
            if riders:
                finish()

    barrier_id = shakes[0][0] if shakes else None
    params = pltpu.CompilerParams(vmem_limit_bytes=VMEM_LIMIT, collective_id=barrier_id,
                                  **(dict(dimension_semantics=sem) if grid else {}))
    res = pl.pallas_call(
        full, name=name, grid=grid, in_specs=in_specs, out_specs=out_specs, out_shape=out_shape,
        scratch_shapes=scratch_shapes, input_output_aliases=aliases, compiler_params=params,
    )(*operands)
    outs, at, rider_res = res[:n_out], n_out, []
    for r in riders:
        rider_res.append((res[at:at + len(r.inplace)], res[at + len(r.inplace):at + len(r.inplace) + len(r.lands)]))
        at += len(r.inplace) + len(r.lands)
    return list(outs), rider_res


def _gammas():
    return [1.0 - 2.0 ** (-5.0 - h) for h in range(HEADS)]


def _decay_tables():
    idx = np.arange(SUPER)
    dist = np.abs(idx[:, None] - idx[None, :]).astype(np.float64)
    visible = (idx[None, :] // CHUNK) <= (idx[:, None] // CHUNK)
    mask = np.stack([np.where(visible, g ** dist, 0.0) for g in _gammas()])
    qd = np.concatenate([np.repeat((g ** (idx + 1.0))[:, None], HEAD_DIM, 1) for g in _gammas()], 1)
    kd = np.concatenate([np.repeat((g ** (SUPER - 1.0 - idx))[:, None], HEAD_DIM, 1) for g in _gammas()], 1)
    return (jnp.asarray(mask, F32), jnp.asarray(qd, F32), jnp.asarray(kd, F32))


def _rope_tables(s):
    inv_freq = ROPE_BASE ** (-np.arange(0, HEAD_DIM, 2, dtype=np.float64) / HEAD_DIM)
    ang = np.arange(s, dtype=np.float64)[:, None] * inv_freq[None, :]
    cos, sin = np.cos(ang), np.sin(ang)
    return (jnp.asarray(np.concatenate([cos, cos], 1), F32),
            jnp.asarray(np.concatenate([-sin, sin], 1), F32))


def _rope(t, cosf, sinf):
    return t * cosf + pltpu.roll(t, HEAD_DIM // 2, 1) * sinf


def _rope_t(t, cosf, sinf):
    return t * cosf - pltpu.roll(t, HEAD_DIM // 2, 1) * sinf


def _layernorm_fwd(z):
    mu = jnp.mean(z, axis=-1, keepdims=True)
    zc = z - mu
    var = jnp.mean(zc * zc, axis=-1, keepdims=True)
    rstd = lax.rsqrt(var + LN_EPS)
    return zc * rstd, rstd


def _layernorm_bwd(dy, xhat, rstd, gain):
    dxh = dy * gain
    m1 = jnp.mean(dxh, axis=-1, keepdims=True)
    m2 = jnp.mean(dxh * xhat, axis=-1, keepdims=True)
    return rstd * (dxh - m1 - xhat * m2)


def _proj_pool(x, win4, cosf, sinf, wpool, pscale, ts, riders=(), after=()):
    s = x.shape[0]
    nt = s // ts

    def body(x_ref, w_ref, cos_ref, sin_ref, wp_ref, ps_ref,
             xb_ref, q_ref, k_ref, v_ref, g_ref, pooled_ref, cat_ref, proj_scr, pext_scr):
        i = pl.program_id(0)
        xb = x_ref[...].astype(BF16)
        xb_ref[...] = xb
        for j in range(N_SHARD):
            proj_scr[:, j * IN_SH:(j + 1) * IN_SH] = _dot(xb, w_ref[j])
        cosf_t = cos_ref[...]
        sinf_t = sin_ref[...]
        for h in range(HEADS):
            lo = h * HEAD_DIM
            q_ref[:, lo:lo + HEAD_DIM] = _rope(proj_scr[:, lo:lo + HEAD_DIM], cosf_t, sinf_t).astype(BF16)
            kk = _rope(proj_scr[:, RET_W + lo:RET_W + lo + HEAD_DIM], cosf_t, sinf_t) * K_SCALE
            k_ref[:, lo:lo + HEAD_DIM] = kk.astype(BF16)
        v_ref[...] = proj_scr[:, 2 * RET_W:3 * RET_W].astype(BF16)
        g_ref[...] = proj_scr[:, 3 * RET_W:4 * RET_W]

        @pl.when(i == 0)
        def _():
            pext_scr[0:POOL_HALO, :] = jnp.zeros((POOL_HALO, POOL_W), F32)

        pext_scr[POOL_HALO:POOL_HALO + ts, :] = proj_scr[:, 4 * RET_W:IN_W]
        pos = (i * ts + lax.broadcasted_iota(jnp.int32, (ts, 1), 0) + 1).astype(F32)
        for gi, w in enumerate(POOL_WINDOWS):
            lo = gi * HEAD_DIM
            ext = pext_scr[:, lo:lo + HEAD_DIM]
            acc = ext
            shift = 1
            while shift < w:
                acc = acc + pltpu.roll(acc, shift, 0)
                shift *= 2
            tok = ext[POOL_HALO:POOL_HALO + ts]
            pooled = acc[POOL_HALO:POOL_HALO + ts] / jnp.minimum(pos, float(w)) - tok
            pooled_b = pooled.astype(BF16)
            pooled_ref[:, lo:lo + HEAD_DIM] = pooled_b
            lin = _dot(pooled_b, wp_ref[gi])
            cat_ref[:, lo:lo + HEAD_DIM] = (lin * ps_ref[:, lo:lo + HEAD_DIM]).astype(BF16)
        pext_scr[0:POOL_HALO, :] = pext_scr[ts:ts + POOL_HALO, :]

    tile = lambda w: pl.BlockSpec((ts, w), lambda i: (i, 0))
    return _call(
        body, name="proj_pool", grid=(nt,),
        in_specs=[tile(D_MODEL), _whole(), tile(HEAD_DIM), tile(HEAD_DIM), _whole(), _whole()],
        out_specs=[tile(D_MODEL), tile(RET_W), tile(RET_W), tile(RET_W), tile(RET_W), tile(POOL_W),
                   pl.BlockSpec((ts, POOL_W), lambda i: (i, 1))],
        out_shape=[jax.ShapeDtypeStruct((s, D_MODEL), BF16), jax.ShapeDtypeStruct((s, RET_W), BF16),
                   jax.ShapeDtypeStruct((s, RET_W), BF16), jax.ShapeDtypeStruct((s, RET_W), BF16),
                   jax.ShapeDtypeStruct((s, RET_W), F32), jax.ShapeDtypeStruct((s, POOL_W), BF16),
                   jax.ShapeDtypeStruct((s, 2 * RET_W), BF16)],
        scratch_shapes=[pltpu.VMEM((ts, IN_W), F32), pltpu.VMEM((ts + POOL_HALO, POOL_W), F32)],
        sem=("arbitrary",), operands=(x, win4, cosf, sinf, wpool, pscale), riders=riders, after=after,
    )


def _retention_fwd(q, k, v, g, cat, mask, qd, kd, riders=(), after=()):
    s = q.shape[0]
    ns = s // SUPER
    cdec = [gm ** float(SUPER) for gm in _gammas()]

    def body(q_ref, k_ref, v_ref, g_ref, cat_in, mask_ref, qd_ref, kd_ref,
             ret_ref, cat_ref, st_ref, state_scr):
        del cat_in
        n = pl.program_id(0)

        @pl.when(n == 0)
        def _():
            state_scr[...] = jnp.zeros_like(state_scr)

        for h in range(HEADS):
            sl = slice(h * HEAD_DIM, (h + 1) * HEAD_DIM)
            qh, kh, vh = q_ref[:, sl], k_ref[:, sl], v_ref[:, sl]
            sc = _dot_nt(qh, kh) * mask_ref[h]
            st = state_scr[h]
            stb = st.astype(BF16)
            st_ref[0, h] = stb
            qdb = (qh.astype(F32) * qd_ref[:, sl]).astype(BF16)
            kdb = (kh.astype(F32) * kd_ref[:, sl]).astype(BF16)
            ret = _dot(sc.astype(BF16), vh) + _dot(qdb, stb)
            state_scr[h] = st * cdec[h] + _dot_tn(kdb, vh)
            ret_ref[:, sl] = ret
            r = lax.rsqrt(jnp.mean(ret * ret, axis=-1, keepdims=True) + RMS_EPS)
            gh = g_ref[:, sl]
            cat_ref[:, sl] = ((ret * r) * (gh * _sigmoid(gh))).astype(BF16)

    tile = pl.BlockSpec((SUPER, RET_W), lambda n: (n, 0))
    return _call(
        body, name="retention_fwd", grid=(ns,),
        in_specs=[tile, tile, tile, tile, HBM_SPEC, _whole(), _whole(), _whole()],
        out_specs=[tile, tile, pl.BlockSpec((1, HEADS, HEAD_DIM, HEAD_DIM), lambda n: (n, 0, 0, 0))],
        out_shape=[jax.ShapeDtypeStruct((s, RET_W), F32), jax.ShapeDtypeStruct((s, 2 * RET_W), BF16),
                   jax.ShapeDtypeStruct((ns, HEADS, HEAD_DIM, HEAD_DIM), BF16)],
        scratch_shapes=[pltpu.VMEM((HEADS, HEAD_DIM, HEAD_DIM), F32)],
        aliases={4: 1}, sem=("arbitrary",), operands=(q, k, v, g, cat, mask, qd, kd), riders=riders,
        after=after,
    )


def _outproj_ln1(x, cat, wout, g1, b1, ts, riders=(), after=()):
    s = x.shape[0]

    def body(x_ref, cat_ref, w_ref, g_ref, b_ref, xhat_ref, rstd_ref, h1b_ref):
        z = ALPHA * x_ref[...] + _dot(cat_ref[...], w_ref[...])
        xhat, rstd = _layernorm_fwd(z)
        xhat_ref[...] = xhat
        rstd_ref[...] = rstd
        h1b_ref[...] = (xhat * g_ref[...] + b_ref[...]).astype(BF16)

    tile = lambda w: pl.BlockSpec((ts, w), lambda i: (i, 0))
    return _call(
        body, name="outproj_ln1", grid=(s // ts,),
        in_specs=[tile(D_MODEL), tile(D_MODEL), _whole(), _whole(), _whole()],
        out_specs=[tile(D_MODEL), tile(1), tile(D_MODEL)],
        out_shape=[jax.ShapeDtypeStruct((s, D_MODEL), F32), jax.ShapeDtypeStruct((s, 1), F32),
                   jax.ShapeDtypeStruct((s, D_MODEL), BF16)],
        sem=("arbitrary",), operands=(x, cat, wout, g1, b1), riders=riders, after=after,
    )


def _ffn_fwd_loss(xhat1, h1b, target, wup4, wdown, cw, cb, g1, b1, g2, b2, ts):
    s = xhat1.shape[0]

    def body(xhat_ref, h1b_ref, tgt_ref, wup_ref, wdn_ref, cw_ref, cb_ref, g1_ref, b1_ref, g2_ref, b2_ref,
             ub_ref, act_ref, sd_ref, dz2_ref, dz2b_ref, loss_ref, dg2_ref, db2_ref, val_scr, gext_scr, ffn_scr):
        i = pl.program_id(0)

        @pl.when(i == 0)
        def _():
            gext_scr[0:CONV_HALO, :] = jnp.zeros((CONV_HALO, D_FF), F32)
            loss_ref[...] = jnp.zeros_like(loss_ref)
            dg2_ref[...] = jnp.zeros_like(dg2_ref)
            db2_ref[...] = jnp.zeros_like(db2_ref)

        for half in range(2):
            lo = half * UP_SH
            gext_scr[CONV_HALO:CONV_HALO + ts, lo:lo + UP_SH] = _dot(h1b_ref[...], wup_ref[2 + half])
            val_scr[:, lo:lo + UP_SH] = _dot(h1b_ref[...], wup_ref[half])
            for c0 in range(lo, lo + UP_SH, FFN_STRIP):
                cols = slice(c0, c0 + FFN_STRIP)
                ext = gext_scr[:, cols]
                gate = ext[CONV_HALO:]
                hc = cb_ref[:, cols] + ((pltpu.roll(ext, 2, 0)[CONV_HALO:] * cw_ref[0:1, cols]
                                         + pltpu.roll(ext, 1, 0)[CONV_HALO:] * cw_ref[1:2, cols])
                                        + gate * cw_ref[2:3, cols])
                val = val_scr[:, cols]
                sg = _sigmoid(hc)
                si = hc * sg
                act_ref[:, cols] = (si * val).astype(BF16)
                ub_ref[:, cols] = val.astype(BF16)
                ub_ref[:, D_FF + c0:D_FF + c0 + FFN_STRIP] = gate.astype(BF16)
                sd_ref[:, cols] = hc.astype(BF16)
            part = _dot(act_ref[:, lo:lo + UP_SH], wdn_ref[lo:lo + UP_SH, :])
            if half == 0:
                ffn_scr[...] = part
            else:
                ffn_scr[...] += part

        gext_scr[0:CONV_HALO, :] = gext_scr[ts:ts + CONV_HALO, :]

        loss_acc = jnp.zeros((1, 1), F32)
        dg2_acc = jnp.zeros((1, D_MODEL), F32)
        db2_acc = jnp.zeros((1, D_MODEL), F32)
        for r0 in range(0, ts, LN_ROWS):
            rows = slice(r0, r0 + LN_ROWS)
            h1 = xhat_ref[rows, :] * g1_ref[...] + b1_ref[...]
            xhat2, rstd2 = _layernorm_fwd(ALPHA * h1 + ffn_scr[rows, :])
            diff = (xhat2 * g2_ref[...] + b2_ref[...]) - tgt_ref[rows, :]
            row = jnp.mean(diff * diff, axis=-1, keepdims=True)
            loss_acc = loss_acc + 0.5 * jnp.sum(row, axis=0, keepdims=True)
            dy = diff * (1.0 / D_MODEL)
            dg2_acc = dg2_acc + jnp.sum(dy * xhat2, axis=0, keepdims=True)
            db2_acc = db2_acc + jnp.sum(dy, axis=0, keepdims=True)
            dz2 = _layernorm_bwd(dy, xhat2, rstd2, g2_ref[...])
            dz2_ref[rows, :] = dz2
            dz2b_ref[rows, :] = dz2.astype(BF16)
        loss_ref[...] += loss_acc
        dg2_ref[...] += dg2_acc
        db2_ref[...] += db2_acc

    tile = lambda w: pl.BlockSpec((ts, w), lambda i: (i, 0))
    acc = lambda w: pl.BlockSpec((1, w), lambda i: (0, 0))
    return pl.pallas_call(
        body, name="ffn_fwd_loss", grid=(s // ts,),
        in_specs=[tile(D_MODEL), tile(D_MODEL), tile(D_MODEL)] + [_whole()] * 8,
        out_specs=[tile(2 * D_FF), tile(D_FF), tile(D_FF), tile(D_MODEL), tile(D_MODEL),
                   acc(1), acc(D_MODEL), acc(D_MODEL)],
        out_shape=[jax.ShapeDtypeStruct((s, 2 * D_FF), BF16), jax.ShapeDtypeStruct((s, D_FF), BF16),
                   jax.ShapeDtypeStruct((s, D_FF), BF16), jax.ShapeDtypeStruct((s, D_MODEL), F32),
                   jax.ShapeDtypeStruct((s, D_MODEL), BF16),
                   jax.ShapeDtypeStruct((1, 1), F32), jax.ShapeDtypeStruct((1, D_MODEL), F32),
                   jax.ShapeDtypeStruct((1, D_MODEL), F32)],
        scratch_shapes=[pltpu.VMEM((ts, D_FF), F32), pltpu.VMEM((ts + CONV_HALO, D_FF), F32),
                        pltpu.VMEM((ts, D_MODEL), F32)],
        compiler_params=_params(("arbitrary",)),
    )(xhat1, h1b, target, wup4, wdown, cw, cb, g1, b1, g2, b2)


def _ffn_bwd(dz2, dz2b, ub, sd, xhat1, rstd1, wup4, wdown, cw, g1, ts):
    s = dz2.shape[0]
    nt = s // ts

    def body(dz2_ref, dz2b_ref, ub_ref, sd_ref, xhat_ref, rstd_ref, wup_ref, wdn_ref, cw_ref, g1_ref,
             dub_ref, dz1_ref, dz1b_ref, dg1_ref, db1_ref, dcw_ref, dcb_ref, dext_scr, da_scr):
        i = pl.program_id(0)

        @pl.when(i == 0)
        def _():
            dext_scr[ts:ts + CONV_HALO, :] = jnp.zeros((CONV_HALO, D_FF), F32)
            dg1_ref[...] = jnp.zeros_like(dg1_ref)
            db1_ref[...] = jnp.zeros_like(db1_ref)
            dcw_ref[...] = jnp.zeros_like(dcw_ref)
            dcb_ref[...] = jnp.zeros_like(dcb_ref)

        da_scr[...] = _dot_nt(dz2b_ref[...], wdn_ref[...])
        n_ext = ts + CONV_HALO
        for c0 in range(0, D_FF, FFN_STRIP):
            cols = slice(c0, c0 + FFN_STRIP)
            gcols = slice(D_FF + c0, D_FF + c0 + FFN_STRIP)
            val = ub_ref[:, cols].astype(F32)
            gate = ub_ref[:, gcols].astype(F32)
            da = da_scr[:, cols]
            hc = sd_ref[:, cols].astype(F32)
            sg = _sigmoid(hc)
            dhc = da * val * (sg * (1.0 + hc * (1.0 - sg)))
            dext_scr[0:ts, cols] = dhc
            dext = dext_scr[:, cols]
            dhc1 = pltpu.roll(dext, n_ext - 1, 0)[0:ts]
            dhc2 = pltpu.roll(dext, n_ext - 2, 0)[0:ts]
            dcb_ref[:, cols] += jnp.sum(dhc, axis=0, keepdims=True)
            dcw_ref[0:1, cols] += jnp.sum(dhc2 * gate, axis=0, keepdims=True)
            dcw_ref[1:2, cols] += jnp.sum(dhc1 * gate, axis=0, keepdims=True)
            dcw_ref[2:3, cols] += jnp.sum(dhc * gate, axis=0, keepdims=True)
            dgate = dhc * cw_ref[2:3, cols] + dhc1 * cw_ref[1:2, cols] + dhc2 * cw_ref[0:1, cols]
            dub_ref[:, cols] = (da * (hc * sg)).astype(BF16)
            dub_ref[:, gcols] = dgate.astype(BF16)
        dext_scr[ts:n_ext, :] = dext_scr[0:CONV_HALO, :]
        dh1 = ALPHA * dz2_ref[...]
        for j in range(N_SHARD):
            dh1 = dh1 + _dot_nt(dub_ref[:, j * UP_SH:(j + 1) * UP_SH], wup_ref[j])
        xhat = xhat_ref[...]
        dg1_ref[...] += jnp.sum(dh1 * xhat, axis=0, keepdims=True)
        db1_ref[...] += jnp.sum(dh1, axis=0, keepdims=True)
        dz1 = _layernorm_bwd(dh1, xhat, rstd_ref[...], g1_ref[...])
        dz1_ref[...] = dz1
        dz1b_ref[...] = dz1.astype(BF16)

    tile = lambda w: pl.BlockSpec((ts, w), lambda i: (nt - 1 - i, 0))
    acc = lambda rws, w: pl.BlockSpec((rws, w), lambda i: (0, 0))
    return pl.pallas_call(
        body, name="ffn_bwd", grid=(nt,),
        in_specs=[tile(D_MODEL), tile(D_MODEL), tile(2 * D_FF), tile(D_FF), tile(D_MODEL), tile(1)]
        + [_whole()] * 4,
        out_specs=[tile(2 * D_FF), tile(D_MODEL), tile(D_MODEL), acc(1, D_MODEL), acc(1, D_MODEL),
                   acc(3, D_FF), acc(1, D_FF)],
        out_shape=[jax.ShapeDtypeStruct((s, 2 * D_FF), BF16),
                   jax.ShapeDtypeStruct((s, D_MODEL), F32), jax.ShapeDtypeStruct((s, D_MODEL), BF16),
                   jax.ShapeDtypeStruct((1, D_MODEL), F32),
                   jax.ShapeDtypeStruct((1, D_MODEL), F32), jax.ShapeDtypeStruct((3, D_FF), F32),
                   jax.ShapeDtypeStruct((1, D_FF), F32)],
        scratch_shapes=[pltpu.VMEM((ts + CONV_HALO, D_FF), F32), pltpu.VMEM((ts, D_FF), F32)],
        compiler_params=_params(("arbitrary",)),
    )(dz2, dz2b, ub, sd, xhat1, rstd1, wup4, wdown, cw, g1)


def _mix_bwd(dz1, pooled, ret, g, wout, wpool, pscale, loss, vec_grads, ts, riders=(), after=()):
    s = dz1.shape[0]
    nt = s // ts

    def body(dz1_ref, pooled_ref, ret_ref, g_ref, wout_ref, wp_ref, ps_ref, loss_ref, dcb_ref, dg1_ref, db1_ref,
             dg2_ref, db2_ref, dret_ref, dgp_ref, dwp_ref, dps_ref, packed_ref, eext_scr):
        i = pl.program_id(0)
        r = nt - 1 - i

        @pl.when(i == 0)
        def _():
            eext_scr[ts:ts + POOL_HALO, :] = jnp.zeros((POOL_HALO, POOL_W), F32)
            dwp_ref[...] = jnp.zeros_like(dwp_ref)
            dps_ref[...] = jnp.zeros_like(dps_ref)

        dzb = dz1_ref[...].astype(BF16)
        dcat_r = _dot_nt(dzb, wout_ref[0:RET_W, :])
        dcat_p = _dot_nt(dzb, wout_ref[RET_W:2 * RET_W, :])
        pos = (r * ts + lax.broadcasted_iota(jnp.int32, (ts, 1), 0) + 1).astype(F32)
        dpooled = []
        for gi, w in enumerate(POOL_WINDOWS):
            sl = slice(gi * HEAD_DIM, (gi + 1) * HEAD_DIM)
            pb = pooled_ref[:, sl]
            dy = dcat_p[:, sl]
            dps_ref[:, sl] += jnp.sum(dy * _dot(pb, wp_ref[gi]), axis=0, keepdims=True)
            dlin = (dy * ps_ref[:, sl]).astype(BF16)
            dwp_ref[gi] += _dot_tn(pb, dlin)
            dpg = _dot_nt(dlin, wp_ref[gi])
            dpooled.append(dpg)
            eext_scr[0:ts, sl] = dpg / jnp.minimum(pos, float(w))
        for gi, w in enumerate(POOL_WINDOWS):
            sl = slice(gi * HEAD_DIM, (gi + 1) * HEAD_DIM)
            acc = eext_scr[:, sl]
            shift = 1
            while shift < w:
                acc = acc + pltpu.roll(acc, ts + POOL_HALO - shift, 0)
                shift *= 2
            dgp_ref[:, RET_W + gi * HEAD_DIM:RET_W + (gi + 1) * HEAD_DIM] = (acc[0:ts] - dpooled[gi]).astype(BF16)
        eext_scr[ts:ts + POOL_HALO, :] = eext_scr[0:POOL_HALO, :]
        for h in range(HEADS):
            sl = slice(h * HEAD_DIM, (h + 1) * HEAD_DIM)
            rt = ret_ref[:, sl]
            rr = lax.rsqrt(jnp.mean(rt * rt, axis=-1, keepdims=True) + RMS_EPS)
            rn = rt * rr
            gh = g_ref[:, sl]
            sg = _sigmoid(gh)
            dy = dcat_r[:, sl]
            dgp_ref[:, sl] = (dy * rn * (sg * (1.0 + gh * (1.0 - sg)))).astype(BF16)
            drn = dy * (gh * sg)
            dret_ref[:, sl] = (rr * (drn - rn * jnp.mean(drn * rn, axis=-1, keepdims=True))).astype(BF16)

        @pl.when(i == nt - 1)
        def _():
            packed_ref[...] = jnp.zeros_like(packed_ref)
            rows = (dcb_ref, dps_ref, dg1_ref, db1_ref, dg2_ref, db2_ref)
            for (row, n), ref in zip(SMALL_VECS, rows):
                packed_ref[row:row + 1, 0:n] = ref[...]
            packed_ref[ROW_LOSS:ROW_LOSS + 1, 0:HEAD_DIM] = jnp.broadcast_to(loss_ref[...], (1, HEAD_DIM))

    tile = lambda w: pl.BlockSpec((ts, w), lambda i: (nt - 1 - i, 0))
    return _call(
        body, name="mix_bwd", grid=(nt,),
        in_specs=[tile(D_MODEL), tile(POOL_W), tile(RET_W), tile(RET_W)] + [_whole()] * 9,
        out_specs=[tile(RET_W), tile(2 * RET_W),
                   pl.BlockSpec((len(POOL_WINDOWS), HEAD_DIM, HEAD_DIM), lambda i: (0, 0, 0)),
                   pl.BlockSpec((1, POOL_W), lambda i: (0, 0)),
                   pl.BlockSpec((SMALL_ROWS, D_FF), lambda i: (0, 0))],
        out_shape=[jax.ShapeDtypeStruct((s, RET_W), BF16), jax.ShapeDtypeStruct((s, 2 * RET_W), BF16),
                   jax.ShapeDtypeStruct((len(POOL_WINDOWS), HEAD_DIM, HEAD_DIM), F32),
                   jax.ShapeDtypeStruct((1, POOL_W), F32), jax.ShapeDtypeStruct((SMALL_ROWS, D_FF), F32)],
        scratch_shapes=[pltpu.VMEM((ts + POOL_HALO, POOL_W), F32)],
        sem=("arbitrary",), operands=(dz1, pooled, ret, g, wout, wpool, pscale, loss, *vec_grads), riders=riders,
        after=after,
    )


def _retention_bwd(q, k, v, dret, dgp, states, mask, qd, kd, cosf, sinf, riders=(), after=()):
    s = q.shape[0]
    ns = s // SUPER
    cdec = [gm ** float(SUPER) for gm in _gammas()]

    def body(q_ref, k_ref, v_ref, do_ref, dgp_ref, st_ref, mask_ref, qd_ref, kd_ref, cos_ref, sin_ref,
             dproj_ref, dstate_scr):
        i = pl.program_id(0)

        @pl.when(i == 0)
        def _():
            dstate_scr[...] = jnp.zeros_like(dstate_scr)

        cosf_t = cos_ref[...]
        sinf_t = sin_ref[...]
        for h in range(HEADS):
            sl = slice(h * HEAD_DIM, (h + 1) * HEAD_DIM)
            qh, kh, vh, doh = q_ref[:, sl], k_ref[:, sl], v_ref[:, sl], do_ref[:, sl]
            dscb = (_dot_nt(doh, vh) * mask_ref[0, h]).astype(BF16)
            dsctb = (_dot_nt(vh, doh) * mask_ref[1, h]).astype(BF16)
            sctb = (_dot_nt(kh, qh) * mask_ref[1, h]).astype(BF16)
            stb = st_ref[0, h]
            dst = dstate_scr[h]
            dstb = dst.astype(BF16)
            qdb = (qh.astype(F32) * qd_ref[:, sl]).astype(BF16)
            kdb = (kh.astype(F32) * kd_ref[:, sl]).astype(BF16)
            dq = _dot(dscb, kh) + _dot_nt(doh, stb) * qd_ref[:, sl]
            dk = _dot(dsctb, qh) + _dot_nt(vh, dstb) * kd_ref[:, sl]
            dv = _dot(sctb, doh) + _dot(kdb, dstb)
            dstate_scr[h] = dst * cdec[h] + _dot_tn(qdb, doh)
            lo = h * HEAD_DIM
            dproj_ref[:, lo:lo + HEAD_DIM] = _rope_t(dq, cosf_t, sinf_t).astype(BF16)
            dproj_ref[:, RET_W + lo:RET_W + lo + HEAD_DIM] = _rope_t(dk * K_SCALE, cosf_t, sinf_t).astype(BF16)
            dproj_ref[:, 2 * RET_W + lo:2 * RET_W + lo + HEAD_DIM] = dv.astype(BF16)
        dproj_ref[:, 3 * RET_W:IN_W] = dgp_ref[...]

    tile = lambda w: pl.BlockSpec((SUPER, w), lambda i: (ns - 1 - i, 0))
    return _call(
        body, name="retention_bwd", grid=(ns,),
        in_specs=[tile(RET_W), tile(RET_W), tile(RET_W), tile(RET_W), tile(2 * RET_W),
                  pl.BlockSpec((1, HEADS, HEAD_DIM, HEAD_DIM), lambda i: (ns - 1 - i, 0, 0, 0)),
                  _whole(), _whole(), _whole(), tile(HEAD_DIM), tile(HEAD_DIM)],
        out_specs=[tile(IN_W)],
        out_shape=[jax.ShapeDtypeStruct((s, IN_W), BF16)],
        scratch_shapes=[pltpu.VMEM((HEADS, HEAD_DIM, HEAD_DIM), F32)],
        sem=("arbitrary",), operands=(q, k, v, dret, dgp, states, mask, qd, kd, cosf, sinf), riders=riders,
        after=after,
    )


def _dx(dz1, dproj, win4, ts, riders=(), after=()):
    s = dz1.shape[0]

    def body(dz1_ref, dp_ref, w_ref, dx_ref):
        acc = ALPHA * dz1_ref[...]
        for j in range(N_SHARD):
            acc = acc + _dot_nt(dp_ref[:, j * IN_SH:(j + 1) * IN_SH], w_ref[j])
        dx_ref[...] = acc

    tile = lambda w: pl.BlockSpec((ts, w), lambda i: (i, 0))
    return _call(
        body, name="dx", grid=(s // ts,),
        in_specs=[tile(D_MODEL), tile(IN_W), _whole()],
        out_specs=[tile(D_MODEL)],
        out_shape=[jax.ShapeDtypeStruct((s, D_MODEL), F32)],
        sem=("arbitrary",), operands=(dz1, dproj, win4), riders=riders, after=after,
    )


def _wgrad(a, b, tm, tn, name, stacked, m_outer, riders=(), after=()):
    s, m = a.shape
    n = b.shape[1]

    def body(a_ref, b_ref, o32_ref, o16_ref):
        res = _dot_tn(a_ref[...], b_ref[...])
        o32_ref[...] = res.reshape(o32_ref.shape)
        o16_ref[...] = res.astype(BF16).reshape(o16_ref.shape)

    if m_outer:
        grid, blocks = (m // tm, n // tn), (lambda g0, g1: (g0, g1))
    else:
        grid, blocks = (n // tn, m // tm), (lambda g0, g1: (g1, g0))
    if stacked:
        shape = (n // tn, m, tn)
        ospec = pl.BlockSpec((1, tm, tn), lambda g0, g1: (blocks(g0, g1)[1], blocks(g0, g1)[0], 0))
    else:
        shape = (m, n)
        ospec = pl.BlockSpec((tm, tn), lambda g0, g1: blocks(g0, g1))
    return _call(
        body, name=name, grid=grid,
        in_specs=[pl.BlockSpec((s, tm), lambda g0, g1: (0, blocks(g0, g1)[0])),
                  pl.BlockSpec((s, tn), lambda g0, g1: (0, blocks(g0, g1)[1]))],
        out_specs=[ospec, ospec],
        out_shape=[jax.ShapeDtypeStruct(shape, F32), jax.ShapeDtypeStruct(shape, BF16)],
        sem=("arbitrary", "arbitrary"), operands=(a, b), riders=riders, after=after,
    )


def _wgrad_send(a, b, tn, name, barrier_id, after=()):
    s, m = a.shape
    n = b.shape[1]
    nb, hm = n // tn, m // 2

    def body(*refs):
        a_ref, b_ref = refs[:2]
        o32_ref, land_ref, send_scr, send_sems, recv_sems = refs[2 + len(after):]
        j = pl.program_id(0)
        x, y, c = _mesh_pos()

        @pl.when(j == 0)
        def _():
            _shake_hands("sibling", wait=False)

        o32_ref[0] = _dot_tn(a_ref[...], b_ref[...])

        @pl.when(j == 0)
        def _():
            _shake_hands("sibling", signal=False)


        theirs = pl.ds(pl.multiple_of((1 - c) * hm, 16), hm)
        copies = [pltpu.make_async_remote_copy(
            src_ref=send_scr.at[blk], dst_ref=land_ref.at[blk], send_sem=send_sems.at[blk],
            recv_sem=recv_sems.at[blk], device_id=(x, y, 1 - c), device_id_type=MESH) for blk in range(nb)]
        for blk in range(nb):
            @pl.when(j == blk)
            def _(blk=blk):
                send_scr[blk] = o32_ref[0, theirs, :].astype(BF16)
                copies[blk].start()

        @pl.when(j == nb - 1)
        def _():
            for cp in copies:
                cp.wait()

    return pl.pallas_call(
        body, name=name, grid=(nb,),
        in_specs=[pl.BlockSpec((s, m), lambda j: (0, 0)), pl.BlockSpec((s, tn), lambda j: (0, j))]
        + [_whole()] * len(after),
        out_specs=[pl.BlockSpec((1, m, tn), lambda j: (j, 0, 0)), HBM_SPEC],
        out_shape=[jax.ShapeDtypeStruct((nb, m, tn), F32), jax.ShapeDtypeStruct((nb, hm, tn), BF16)],
        scratch_shapes=[pltpu.VMEM((nb, hm, tn), BF16), pltpu.SemaphoreType.DMA((nb,)),
                        pltpu.SemaphoreType.DMA((nb,))],
        compiler_params=pltpu.CompilerParams(dimension_semantics=("arbitrary",), vmem_limit_bytes=VMEM_LIMIT,
                                             collective_id=barrier_id),
    )(a, b, *after)


class _NoComm:
    def __init__(self, win4, wout, wup4, wdown):
        self.weights = dict(w_in=win4, w_out=wout, w_up=wup4, w_down=wdown)
        self.grads = {}

    def weight(self, name):
        return self.weights[name]

    def riders(self, call):
        return ()

    def after(self, call):
        return ()

    def landed(self, call, results, outs):
        pass

    def small_gradients(self, small, packed):
        pass

    def gradient(self, name, g32, g16):
        self.grads[name] = (g32, g16)

    def wgrad_in(self, xb, dproj):
        (g32, g16), _ = _wgrad(xb, dproj, D_MODEL, IN_SH, "wgrad_in", True, True)
        self.gradient("w_in", g32, g16)


def _local_step(x, target, cw, cb, wpool_b, pscale, g1, b1, g2, b2, comm):
    s = x.shape[0]
    ts_a = min(512, s)
    ts_f = min(256, s)
    mask, qd, kd = _decay_tables()
    cosf, sinf = _rope_tables(s)

    def run(call, fn, *args):
        outs, res = fn(*args, riders=comm.riders(call), after=comm.after(call))
        comm.landed(call, res, outs)
        return outs

    xb, q, k, v, g, pooled, cat = run("proj_pool", _proj_pool, x, comm.weight("w_in"), cosf, sinf, wpool_b,
                                      pscale, ts_a)
    ret, cat, states = run("retention_fwd", _retention_fwd, q, k, v, g, cat, mask, qd, kd)
    wout = comm.weight("w_out")
    xhat1, rstd1, h1b = run("outproj_ln1", _outproj_ln1, x, cat, wout, g1, b1, ts_a)
    wup4, wdown = comm.weight("w_up"), comm.weight("w_down")
    ub, act, sd, dz2, dz2b, loss, dg2, db2 = _ffn_fwd_loss(xhat1, h1b, target, wup4, wdown, cw, cb, g1, b1, g2, b2,
                                                           ts_f)

    dub, dz1, dz1b, dg1, db1, dcw, dcb = _ffn_bwd(dz2, dz2b, ub, sd, xhat1, rstd1, wup4, wdown, cw, g1, ts_f)
    half = D_MODEL // 2
    comm.gradient("w_up", *run("wgrad_up", _wgrad, h1b, dub, half, UP_SH, "wgrad_up", True, False))
    comm.gradient("w_out", *run("wgrad_out", _wgrad, cat, dz1b, D_MODEL, half, "wgrad_out", False, True))
    comm.gradient("w_down", *run("wgrad_down", _wgrad, act, dz2b, D_FF // 2, half, "wgrad_down", False, True))
    dret, dgp, dwp, dps, packed = run("mix_bwd", _mix_bwd, dz1b, pooled, ret, g, wout, wpool_b, pscale, loss,
                                      [dcb, dg1, db1, dg2, db2], ts_a)
    small = dict(w_pool=dwp, pool_scale=dps, ln1_g=dg1, ln1_b=db1, conv_w=dcw, conv_b=dcb,
                 ln2_g=dg2, ln2_b=db2)
    comm.small_gradients(small, packed)
    mask_both = jnp.stack([mask, jnp.swapaxes(mask, 1, 2)])
    dproj, = run("retention_bwd", _retention_bwd, q, k, v, dret, dgp, states, mask_both, qd, kd, cosf, sinf)
    comm.wgrad_in(xb, dproj)
    (grad_x,), _ = _dx(dz1, dproj, comm.weight("w_in"), ts_a, after=comm.after("dx"))
    return loss, grad_x, small


CAST_ROWS = 64
SHARD_SHAPES = ((D_MODEL, IN_SH), (OUT_SH, D_MODEL), (D_MODEL, UP_SH), (DOWN_SH, D_MODEL))
N_BIG = len(SHARD_SHAPES)
CW_SHARD = (3, 1, DOWN_SH)


def _mesh_pos():
    return lax.axis_index("x"), lax.axis_index("y"), lax.axis_index("c")


def _other_chips(x, y):
    return [(1 - x, y), (x, 1 - y), (1 - x, 1 - y)]


def _shake_hands(peers, signal=True, wait=True):
    x, y, c = _mesh_pos()
    others = [(x, y, 1 - c)] if peers in ("sibling", "both", "all") else []
    if peers in ("chips", "both", "all"):
        others += [(chip[0], chip[1], c) for chip in _other_chips(x, y)]
    if peers == "all":
        others += [(chip[0], chip[1], 1 - c) for chip in _other_chips(x, y)]
    barrier = pltpu.get_barrier_semaphore()
    if signal:
        for peer in others:
            pl.semaphore_signal(barrier, inc=1, device_id=peer, device_id_type=MESH)
    if wait:
        pl.semaphore_wait(barrier, len(others))


def _half_rows(w, which):
    hr = SHARD_SHAPES[w][0] // 2
    return pl.ds(pl.multiple_of(which * hr, 16), hr)


def _gather_weights(shards, cw_shard, wpool, full):
    def body(*refs):
        in_refs = refs[:N_BIG]
        cw_ref, wpool_ref = refs[N_BIG:N_BIG + 2]
        out_refs = refs[N_BIG + 2:2 * N_BIG + 2]
        cwo_ref, wpool_b_ref = refs[2 * N_BIG + 2:2 * N_BIG + 4]
        stage = refs[2 * N_BIG + 4:3 * N_BIG + 4]
        raw = refs[3 * N_BIG + 4:4 * N_BIG + 4 - len(full)]
        send_sems, recv_sems, fsend_sems, frecv_sems, cw_send, cw_recv, local_sems, load_sems = \
            refs[4 * N_BIG + 4 - len(full):]
        x, y, c = _mesh_pos()
        j0 = 2 * x + y
        chips = _other_chips(x, y)
        _shake_hands("both", wait=False)

        fetched = [w for w in range(N_BIG) if w not in full]
        f32 = {w: in_refs[w] for w in full}
        loads = []
        for n, w in enumerate(fetched):
            f32[w] = raw[n]
            loads.append(pltpu.make_async_copy(in_refs[w], raw[n], load_sems.at[n]))
            loads[-1].start()

        def cast_to_stage(w):
            def cast(i, carry):
                rows = pl.ds(pl.multiple_of(i * CAST_ROWS, CAST_ROWS), CAST_ROWS)
                stage[w][rows, :] = f32[w][rows, :].astype(BF16)
                return carry
            lax.fori_loop(0, SHARD_SHAPES[w][0] // CAST_ROWS, cast, 0)

        for w in full:
            cast_to_stage(w)

        jx, jy, jd = 2 * (1 - x) + y, 2 * x + (1 - y), 2 * (1 - x) + (1 - y)
        neighbours = [((1 - x, y, c), jx), ((x, 1 - y, c), jy)]
        passed = jnp.where(c == 0, jx, jy)
        pass_to = (jnp.where(c == 0, x, 1 - x), jnp.where(c == 0, 1 - y, y), c)

        def nbr(w, k, block):
            return pltpu.make_async_remote_copy(
                src_ref=stage[w].at[_half_rows(w, c), :], dst_ref=out_refs[w].at[block, _half_rows(w, c), :],
                send_sem=send_sems.at[w, k], recv_sem=recv_sems.at[w, k],
                device_id=neighbours[k][0], device_id_type=MESH)

        def relay(w, block):
            return pltpu.make_async_remote_copy(
                src_ref=out_refs[w].at[passed, _half_rows(w, c), :],
                dst_ref=out_refs[w].at[block, _half_rows(w, c), :],
                send_sem=send_sems.at[w, 2], recv_sem=recv_sems.at[w, 2],
                device_id=pass_to, device_id_type=MESH)

        def d2d(w, k, block, half):
            return pltpu.make_async_remote_copy(
                src_ref=out_refs[w].at[block, _half_rows(w, half), :],
                dst_ref=out_refs[w].at[block, _half_rows(w, half), :],
                send_sem=fsend_sems.at[w, k], recv_sem=frecv_sems.at[w, k],
                device_id=(x, y, 1 - c), device_id_type=MESH)

        def conv(k, block):
            chip = chips[k]
            return pltpu.make_async_remote_copy(
                src_ref=cw_ref, dst_ref=cwo_ref.at[block], send_sem=cw_send.at[k], recv_sem=cw_recv.at[k],
                device_id=(chip[0], chip[1], c), device_id_type=MESH)

        _shake_hands("both", signal=False)
        sent = [nbr(w, k, j0) for w in full for k in range(2)] + [conv(k, j0) for k in range(3)]
        for cp in sent:
            cp.start()
        for n, w in enumerate(fetched):
            loads[n].wait()
            cast_to_stage(w)
        local = [pltpu.make_async_copy(stage[w], out_refs[w].at[j0], local_sems.at[w]) for w in range(N_BIG)]
        local.append(pltpu.make_async_copy(cw_ref, cwo_ref.at[j0], local_sems.at[N_BIG]))
        for cp in local:
            cp.start()
        wpool_b_ref[...] = wpool_ref[...].astype(BF16)
        for w in full:
            for k, (_, block) in enumerate(neighbours):
                nbr(w, k, block).wait_recv()
            later = [relay(w, passed)] + [d2d(w, k, block, c) for k, (_, block) in enumerate(neighbours)]
            for cp in later:
                cp.start()
            sent += later
        for w in full:
            relay(w, jd).wait_recv()
            fw = d2d(w, 2, jd, c)
            fw.start()
            sent.append(fw)
        for w in full:
            for k, block in enumerate([jx, jy, jd]):
                d2d(w, k, block, 1 - c).wait_recv()
        for k, chip in enumerate(chips):
            conv(k, 2 * chip[0] + chip[1]).wait_recv()
        for cp in sent:
            cp.wait_send()
        for cp in local:
            cp.wait()

    out_shape = [jax.ShapeDtypeStruct((N_SHARD,) + shp, BF16) for shp in SHARD_SHAPES]
    out_shape.append(jax.ShapeDtypeStruct((N_SHARD,) + CW_SHARD, F32))
    out_shape.append(jax.ShapeDtypeStruct(wpool.shape, BF16))
    return pl.pallas_call(
        body, name="gather_weights",
        in_specs=[_whole() if w in full else HBM_SPEC for w in range(N_BIG)] + [_whole()] * 2,
        out_specs=[HBM_SPEC] * (N_BIG + 1) + [_whole()],
        out_shape=out_shape,
        scratch_shapes=[pltpu.VMEM(shp, BF16) for shp in SHARD_SHAPES]
        + [pltpu.VMEM(shp, F32) for w, shp in enumerate(SHARD_SHAPES) if w not in full] + [
            pltpu.SemaphoreType.DMA((N_BIG, 3)), pltpu.SemaphoreType.DMA((N_BIG, 3)),
            pltpu.SemaphoreType.DMA((N_BIG, 3)), pltpu.SemaphoreType.DMA((N_BIG, 3)),
            pltpu.SemaphoreType.DMA((3,)), pltpu.SemaphoreType.DMA((3,)),
            pltpu.SemaphoreType.DMA((N_BIG + 1,)), pltpu.SemaphoreType.DMA((N_BIG - len(full),))],
        compiler_params=pltpu.CompilerParams(vmem_limit_bytes=VMEM_LIMIT, collective_id=13),
    )(*shards, cw_shard, wpool)


def _gather_rider(arrays, ops, handshake=None):
    ws = sorted(arrays)

    def make(inplace, srcs, lands, send_sems, recv_sems):
        del srcs, lands
        x, y, c = _mesh_pos()
        j0, jx, jy, jd = 2 * x + y, 2 * (1 - x) + y, 2 * x + (1 - y), 2 * (1 - x) + (1 - y)
        x_nbr, y_nbr, sibling = (1 - x, y, c), (x, 1 - y, c), (x, y, 1 - c)
        starts, waits = [], []
        for n, (kind, w, (r0, nr)) in enumerate(ops):
            ref = inplace[ws.index(w)]
            hr = SHARD_SHAPES[w][0] // 2
            rows = lambda core: pl.ds(pl.multiple_of(core * hr + r0, 16), nr)
            mine, theirs = rows(c), rows(1 - c)
            if kind == "ici":
                moves = [(ref.at[j0, mine, :], x_nbr, ref.at[jx, mine, :]),
                         (ref.at[j0, mine, :], y_nbr, ref.at[jy, mine, :]),
                         (ref.at[j0, mine, :], (1 - x, 1 - y, c), ref.at[jd, mine, :])]
            elif kind == "nbr":
                moves = [(ref.at[j0, mine, :], x_nbr, ref.at[jx, mine, :]),
                         (ref.at[j0, mine, :], y_nbr, ref.at[jy, mine, :])]
            elif kind == "relay":
                passed = jnp.where(c == 0, jx, jy)
                to = (jnp.where(c == 0, x, 1 - x), jnp.where(c == 0, 1 - y, y), c)
                moves = [(ref.at[passed, mine, :], to, ref.at[jd, mine, :])]
            else:
                blocks = dict(d2d=[jx, jy, jd], d2d_nbr=[jx, jy], d2d_diag=[jd])[kind]
                moves = [(ref.at[b, mine, :], sibling, ref.at[b, theirs, :]) for b in blocks]
            for k, (src, to, landing) in enumerate(moves):
                sems = dict(send_sem=send_sems.at[3 * n + k], recv_sem=recv_sems.at[3 * n + k],
                            device_id=to, device_id_type=MESH)
                send = pltpu.make_async_remote_copy(src_ref=src, dst_ref=src, **sems)
                arrival = pltpu.make_async_remote_copy(src_ref=src, dst_ref=landing, **sems)
                starts.append(send)
                waits += [arrival.wait_recv, send.wait_send]
        return starts, waits

    return _Rider([arrays[w] for w in ws], [], [], 3 * len(ops), make, handshake)


def _whole_half(w):
    return (0, SHARD_SHAPES[w][0] // 2)


def _pair_rider(ws, g16s):
    def make(inplace, srcs, lands, send_sems, recv_sems):
        del inplace
        x, y, c = _mesh_pos()
        copies = [pltpu.make_async_remote_copy(
            src_ref=srcs[i].at[:, _half_rows(w, 1 - c), :], dst_ref=lands[i],
            send_sem=send_sems.at[i], recv_sem=recv_sems.at[i], device_id=(x, y, 1 - c), device_id_type=MESH)
            for i, w in enumerate(ws)]
        return copies, [cp.wait for cp in copies]

    lands = [jax.ShapeDtypeStruct((N_SHARD, SHARD_SHAPES[w][0] // 2, SHARD_SHAPES[w][1]), BF16) for w in ws]
    return _Rider([], g16s, lands, len(ws), make)


def _chip_rider(ws, p16s):
    def make(inplace, srcs, lands, send_sems, recv_sems):
        del inplace
        x, y, c = _mesh_pos()
        copies = []
        for i in range(len(ws)):
            for k, chip in enumerate(_other_chips(x, y)):
                copies.append(pltpu.make_async_remote_copy(
                    src_ref=srcs[i].at[2 * chip[0] + chip[1]], dst_ref=lands[i].at[k],
                    send_sem=send_sems.at[3 * i + k], recv_sem=recv_sems.at[3 * i + k],
                    device_id=(chip[0], chip[1], c), device_id_type=MESH))
        return copies, [cp.wait for cp in copies]

    lands = [jax.ShapeDtypeStruct((3, SHARD_SHAPES[w][0] // 2, SHARD_SHAPES[w][1]), BF16) for w in ws]
    return _Rider([], p16s, lands, 3 * len(ws), make)


def _final_rider(halves):
    def make(inplace, srcs, lands, send_sems, recv_sems):
        del inplace
        x, y, c = _mesh_pos()
        copies = [pltpu.make_async_remote_copy(
            src_ref=srcs[i], dst_ref=lands[i], send_sem=send_sems.at[i], recv_sem=recv_sems.at[i],
            device_id=(x, y, 1 - c), device_id_type=MESH) for i in range(len(halves))]
        return copies, [cp.wait for cp in copies]

    return _Rider([], halves, [jax.ShapeDtypeStruct(h.shape, h.dtype) for h in halves], len(halves), make)


N_DEV = 2 * N_SHARD


def _device_number(x, y, c):
    return 2 * (2 * x + y) + c


def _small_all_rider(own):
    n = len(own)

    def make(inplace, srcs, lands, send_sems, recv_sems):
        del inplace
        x, y, c = _mesh_pos()
        peers = [(x, y, 1 - c)] + [(chip[0], chip[1], core) for chip in _other_chips(x, y) for core in (c, 1 - c)]
        copies = []
        for i in range(n):
            for k, peer in enumerate(peers):
                copies.append(pltpu.make_async_remote_copy(
                    src_ref=srcs[i], dst_ref=lands[i].at[_device_number(x, y, c)],
                    send_sem=send_sems.at[(N_DEV - 1) * i + k], recv_sem=recv_sems.at[(N_DEV - 1) * i + k],
                    device_id=peer, device_id_type=MESH))
        return copies, [cp.wait for cp in copies]

    lands = [jax.ShapeDtypeStruct((N_DEV,) + a.shape, a.dtype) for a in own]
    return _Rider([], own, lands, (N_DEV - 1) * n, make)


def _comm_only(name, riders):
    _, res = _call(lambda: None, name=name, grid=(), in_specs=[], out_specs=[], out_shape=[], operands=(),
                   riders=riders)
    return res


class _SemList:
    def __init__(self, refs):
        self.at = list(refs)


def _merged_rider(riders):
    srcs = [a for r in riders for a in r.srcs]
    lands = [a for r in riders for a in r.lands]

    def make(inplace, src_refs, land_refs, send_sems, recv_sems):
        starts, waits = [], []
        s0 = l0 = c0 = 0
        for r in riders:
            part = r.make(inplace, src_refs[s0:s0 + len(r.srcs)], land_refs[l0:l0 + len(r.lands)],
                          _SemList(send_sems.at[c0:c0 + r.n_copies]), _SemList(recv_sems.at[c0:c0 + r.n_copies]))
            starts += part[0]
            waits += part[1]
            s0, l0, c0 = s0 + len(r.srcs), l0 + len(r.lands), c0 + r.n_copies
        return starts, waits

    return _Rider([], srcs, lands, sum(r.n_copies for r in riders), make)


def _split_start(name, rider, handshake=None):
    assert not rider.inplace
    ns, nl, n = len(rider.srcs), len(rider.lands), rider.n_copies
    barrier_id, peers = handshake if handshake is not None else (None, None)

    def body(*refs):
        if handshake is not None:
            _shake_hands(peers)
        srcs, lands = refs[:ns], refs[ns:ns + nl]
        sems = refs[ns + nl:ns + nl + 2 * n]
        token = refs[-1]
        starts, _ = rider.make([], srcs, lands, _SemList(sems[:n]), _SemList(sems[n:]))
        for cp in starts:
            cp.start()
        token[...] = jnp.zeros_like(token)

    buffers = [pltpu.with_memory_space_constraint(a, pltpu.HBM) for a in rider.srcs]
    buffers += [pltpu.with_memory_space_constraint(lax.empty(s.shape, s.dtype), pltpu.HBM) for s in rider.lands]
    hbm = pl.BlockSpec(memory_space=pltpu.HBM)
    sem = pl.BlockSpec(memory_space=pltpu.SEMAPHORE)
    outs = pl.pallas_call(
        body, name=name,
        out_shape=tuple([pltpu.SemaphoreType.DMA(())] * (2 * n) + [pltpu.HBM(b.shape, b.dtype) for b in buffers]
                        + [jax.ShapeDtypeStruct((8, 128), F32)]),
        in_specs=[hbm] * (ns + nl),
        out_specs=tuple([sem] * (2 * n) + [hbm] * (ns + nl) + [_whole()]),
        input_output_aliases={i: 2 * n + i for i in range(ns + nl)},
        compiler_params=pltpu.CompilerParams(has_side_effects=pltpu.SideEffectType.DATAFLOW_SIDE_EFFECTING,
                                             collective_id=barrier_id),
    )(*buffers)
    return (rider, outs[:2 * n], outs[2 * n:2 * n + ns + nl]), outs[-1]


def _split_parts(state, riders):
    merged, sems, buffers = state
    n, ns = merged.n_copies, len(merged.srcs)
    parts, s0, l0, c0 = [], 0, 0, 0
    for r in riders:
        parts.append((r, list(sems[c0:c0 + r.n_copies]) + list(sems[n + c0:n + c0 + r.n_copies]),
                      list(buffers[s0:s0 + len(r.srcs)]) + list(buffers[ns + l0:ns + l0 + len(r.lands)])))
        s0, l0, c0 = s0 + len(r.srcs), l0 + len(r.lands), c0 + r.n_copies
    return parts


def _split_wait(name, state, after):
    rider, sems, buffers = state
    ns, nl, n = len(rider.srcs), len(rider.lands), rider.n_copies

    def body(*refs):
        srcs, lands = refs[:ns], refs[ns:ns + nl]
        sem_refs = refs[ns + nl:ns + nl + 2 * n]
        _, waits = rider.make([], srcs, lands, _SemList(sem_refs[:n]), _SemList(sem_refs[n:]))
        for wait in waits:
            wait()

    hbm = pl.BlockSpec(memory_space=pltpu.HBM)
    sem = pl.BlockSpec(memory_space=pltpu.SEMAPHORE)
    outs = pl.pallas_call(
        body, name=name,
        out_shape=tuple(pltpu.HBM(b.shape, b.dtype) for b in buffers),
        in_specs=[hbm] * (ns + nl) + [sem] * (2 * n) + [HBM_SPEC],
        out_specs=tuple([hbm] * (ns + nl)),
        input_output_aliases={i: i for i in range(ns + nl)},
        compiler_params=pltpu.CompilerParams(has_side_effects=pltpu.SideEffectType.DATAFLOW_SIDE_EFFECTING),
    )(*buffers, *sems, after)
    return list(outs[:ns]), list(outs[ns:])


def _pair_sum(pos, ws, g32s, recvs):
    n = len(ws)

    def body(pos_ref, *refs):
        del pos_ref
        g_refs, r_refs = refs[:n], refs[n:2 * n]
        p32_refs, p16_refs = refs[2 * n:3 * n], refs[3 * n:]
        x, y, _ = _mesh_pos()
        for i in range(n):
            tot = g_refs[i][...] + r_refs[i][...].astype(F32)
            p16_refs[i][...] = tot.astype(BF16)

            @pl.when(pl.program_id(0) == 2 * x + y)
            def _(i=i, tot=tot):
                p32_refs[i][...] = tot

    halves = [(SHARD_SHAPES[w][0] // 2, SHARD_SHAPES[w][1]) for w in ws]
    own = [pl.BlockSpec((None, None) + h, lambda j, pos_ref: (j, pos_ref[0], 0, 0)) for h in halves]
    blk = [pl.BlockSpec((None,) + h, lambda j, pos_ref: (j, 0, 0)) for h in halves]
    mine = [pl.BlockSpec(h, lambda j, pos_ref: (0, 0)) for h in halves]
    g4 = [g.reshape((N_SHARD, 2) + h) for g, h in zip(g32s, halves)]
    outs = pl.pallas_call(
        body, name="pair_sum_" + "_".join(str(w) for w in ws),
        grid_spec=pltpu.PrefetchScalarGridSpec(
            num_scalar_prefetch=1, grid=(N_SHARD,), in_specs=own + blk, out_specs=mine + blk),
        out_shape=[jax.ShapeDtypeStruct(h, F32) for h in halves]
        + [jax.ShapeDtypeStruct((N_SHARD,) + h, BF16) for h in halves],
        compiler_params=_params(("arbitrary",)),
    )(pos, *g4, *recvs)
    return outs[:n], outs[n:]


def _chip_sum(p32s, recvs):
    parts = 2

    def body(*refs):
        p_refs, r_refs, f_refs = refs[:N_BIG], refs[N_BIG:2 * N_BIG], refs[2 * N_BIG:]
        for w in range(N_BIG):
            f_refs[w][...] = ((p_refs[w][...] + r_refs[w][0].astype(F32)) + r_refs[w][1].astype(F32)) \
                + r_refs[w][2].astype(F32)

    quarters = [(r // 2 // parts, cc) for r, cc in SHARD_SHAPES]
    own = [pl.BlockSpec(qt, lambda i: (i, 0)) for qt in quarters]
    rcv = [pl.BlockSpec((3,) + qt, lambda i: (0, i, 0)) for qt in quarters]
    out = [pl.BlockSpec(qt, lambda i: (i, 0)) for qt in quarters]
    return pl.pallas_call(
        body, name="chip_sum", grid=(parts,), in_specs=own + rcv, out_specs=out,
        out_shape=[jax.ShapeDtypeStruct((r // 2, cc), F32) for r, cc in SHARD_SHAPES],
        compiler_params=_params(("arbitrary",)),
    )(*p32s, *recvs)


def _adamw(w, g, m, v):
    m_new = ADAM_B1 * m + (1.0 - ADAM_B1) * g
    v_new = ADAM_B2 * v + (1.0 - ADAM_B2) * (g * g)
    m_hat = m_new / (1.0 - ADAM_B1 ** ADAM_STEP)
    v_hat = v_new / (1.0 - ADAM_B2 ** ADAM_STEP)
    delta = -ADAM_LR * (m_hat / (jnp.sqrt(v_hat) + ADAM_EPS) + ADAM_WD * w)
    return delta, m_new, v_new


def _adam_half(name, pos, grads, ws, ms, vs, into=None):
    nb = 4
    which = (lambda ref: ref[0]) if into is None else (lambda ref: 1 - ref[0])

    blocks = [(r // 2 // nb, cc) for r, cc in SHARD_SHAPES]
    carried = [] if into is None else [a for kind in into for a in kind]
    slots = 3

    def body(which_ref, *refs):
        g_hbm, w_hbm, m_hbm, v_hbm = [refs[k * N_BIG:(k + 1) * N_BIG] for k in range(4)]
        at = 4 * N_BIG + len(carried)
        go_refs, do_refs, mo_refs, vo_refs = [refs[at + k * N_BIG:at + (k + 1) * N_BIG] for k in range(4)]
        bufs = refs[at + 4 * N_BIG:at + 8 * N_BIG]
        sems = refs[at + 8 * N_BIG]
        i = pl.program_id(0)
        half_index = which(which_ref)

        def fetch(step, slot):
            copies = []
            for w in range(N_BIG):
                rb = blocks[w][0]
                rows = pl.ds(pl.multiple_of(step * rb, 8), rb)
                copies.append(pltpu.make_async_copy(g_hbm[w].at[rows, :], bufs[w].at[slot], sems.at[w, slot]))
                rows = pl.ds(pl.multiple_of((half_index * nb + step) * rb, 8), rb)
                for k, src in enumerate((w_hbm, m_hbm, v_hbm)):
                    n = (k + 1) * N_BIG + w
                    copies.append(pltpu.make_async_copy(src[w].at[0, rows, :], bufs[n].at[slot], sems.at[n, slot]))
            return copies

        @pl.when(i == 0)
        def _():
            for step in range(slots - 1):
                for cp in fetch(step, step):
                    cp.start()

        @pl.when(i + slots - 1 < nb)
        def _():
            for cp in fetch(i + slots - 1, lax.rem(i + slots - 1, slots)):
                cp.start()

        slot = lax.rem(i, slots)
        for cp in fetch(i, slot):
            cp.wait()
        for w in range(N_BIG):
            g = bufs[w][slot]
            delta, m_new, v_new = _adamw(bufs[N_BIG + w][slot], g, bufs[2 * N_BIG + w][slot], bufs[3 * N_BIG + w][slot])
            go_refs[w][...] = g
            do_refs[w][...] = delta
            mo_refs[w][...] = m_new
            vo_refs[w][...] = v_new

    full = [pl.BlockSpec((None,) + b, lambda i, which_ref: (0, which(which_ref) * nb + i, 0)) for b in blocks]
    shapes = [jax.ShapeDtypeStruct((1,) + shp, F32) for shp in SHARD_SHAPES]
    first = 1 + 4 * N_BIG
    outs = pl.pallas_call(
        body, name=name,
        grid_spec=pltpu.PrefetchScalarGridSpec(
            num_scalar_prefetch=1, grid=(nb,), in_specs=[HBM_SPEC] * (4 * N_BIG + len(carried)),
            out_specs=full * 4,
            scratch_shapes=[pltpu.VMEM((slots,) + b, F32) for b in blocks] * 4
            + [pltpu.SemaphoreType.DMA((4 * N_BIG, slots))]),
        out_shape=shapes * 4,
        input_output_aliases={first + i: i for i in range(len(carried))},
        compiler_params=_params(("arbitrary",)),
    )(pos, *grads, *ws, *ms, *vs, *carried)
    return [outs[i * N_BIG:(i + 1) * N_BIG] for i in range(4)]


SMALL_ROWS = 8
ROW_CONV_B, ROW_POOL_SCALE, ROW_LN1_G, ROW_LN1_B, ROW_LN2_G, ROW_LN2_B, ROW_LOSS = range(7)
SMALL_VECS = ((ROW_CONV_B, D_FF), (ROW_POOL_SCALE, POOL_W), (ROW_LN1_G, D_MODEL), (ROW_LN1_B, D_MODEL),
              (ROW_LN2_G, D_MODEL), (ROW_LN2_B, D_MODEL))


def _small_adam(all_a, all_b, all_c, own_a, own_b, own_c, wp, cwp, vec_ws, m_wp, m_cwp, vec_ms,
                v_wp, v_cwp, vec_vs):
    nv = len(SMALL_VECS)
    np_ = 2 + nv

    def body(*refs):
        all_a_ref, all_b_ref, all_c_ref, own_a_ref, own_b_ref, own_c_ref = refs[0:6]
        refs = refs[3:]
        w_all, m_all, v_all = (refs[3 + i * np_:3 + (i + 1) * np_] for i in range(3))
        loss_out = refs[3 + 3 * np_]
        outs = refs[4 + 3 * np_:]
        x, y, c = _mesh_pos()
        j0 = 2 * x + y
        me = _device_number(x, y, c)

        def total(sent, own):
            by_dev = [jnp.where(me == d, own, sent(d)) for d in range(N_DEV)]
            chips = [by_dev[2 * j] + by_dev[2 * j + 1] for j in range(N_SHARD)]
            return ((chips[0] + chips[1]) + chips[2]) + chips[3]

        tot_a = total(lambda d: all_a_ref[d], own_a_ref[...])
        tot_b = total(lambda d: all_b_ref[d], own_b_ref[...])
        tot_c = total(lambda d: all_c_ref[d, j0], own_c_ref[j0])
        loss_out[...] = tot_b[ROW_LOSS:ROW_LOSS + 1, 0:1]
        grads = [tot_a, tot_c] + [tot_b[row:row + 1, 0:n] for row, n in SMALL_VECS]
        for p in range(np_):
            for at, g in ([(j, tot_c[j:j + 1]) for j in range(3)] if p == 1 else [(Ellipsis, grads[p])]):
                delta, m_new, v_new = _adamw(w_all[p][at], g, m_all[p][at], v_all[p][at])
                outs[p][at] = g
                outs[np_ + p][at] = delta
                outs[2 * np_ + p][at] = m_new
                outs[3 * np_ + p][at] = v_new

    pshapes = [wp.shape, CW_SHARD] + [wv.shape for wv in vec_ws]
    out_shape = [jax.ShapeDtypeStruct((1, 1), F32)] + [jax.ShapeDtypeStruct(s, F32) for s in pshapes] * 4
    outs = pl.pallas_call(
        body, name="small_adam",
        in_specs=[_whole()] * (6 + 3 * np_), out_specs=[_whole()] * len(out_shape), out_shape=out_shape,
        compiler_params=pltpu.CompilerParams(vmem_limit_bytes=VMEM_LIMIT),
    )(all_a, all_b, all_c, own_a, own_b, own_c, wp, cwp, *vec_ws, m_wp, m_cwp, *vec_ms, v_wp, v_cwp, *vec_vs)
    return outs[0], [outs[1 + i * np_:1 + (i + 1) * np_] for i in range(4)]


def kernel(x, w_in, w_pool, pool_scale, w_out, ln1_g, ln1_b, w_up, conv_w, conv_b, w_down, ln2_g, ln2_b, loss_target, m_w_in, m_w_pool, m_pool_scale, m_w_out, m_ln1_g, m_ln1_b, m_w_up, m_conv_w, m_conv_b, m_w_down, m_ln2_g, m_ln2_b, v_w_in, v_w_pool, v_pool_scale, v_w_out, v_ln1_g, v_ln1_b, v_w_up, v_conv_w, v_conv_b, v_w_down, v_ln2_g, v_ln2_b):
    pos = lax.axis_index("c").astype(jnp.int32).reshape(1)
    order = ("w_in", "w_out", "w_up", "w_down")
    w_in_i, w_out_i, w_up_i, w_down_i = range(N_BIG)
    vec_names = ("conv_b", "pool_scale", "ln1_g", "ln1_b", "ln2_g", "ln2_b")

    taps_first = lambda a: jnp.transpose(a, (1, 0, 2))
    gathered = _gather_weights([w_in[0], w_out[0], w_up[0], w_down[0]], taps_first(conv_w), w_pool[0], (w_in_i,))
    cw_full = jnp.transpose(gathered[N_BIG].reshape(N_SHARD, 3, DOWN_SH), (1, 0, 2)).reshape(3, D_FF)
    up_a, up_b, up_c = (0, 192), (192, 192), (384, 128)
    assert up_c[0] + up_c[1] == SHARD_SHAPES[w_up_i][0] // 2

    class MeshComm:
        def __init__(self):
            self.w = {i: gathered[i] for i in range(N_BIG)}
            self.g32, self.g16, self.p32, self.p16, self.recv_b = {}, {}, {}, {}, {}
            self.up_complete = False
            self.tokens, self.chips = {}, []

        def weight(self, name):
            i = order.index(name)
            if name == "w_up" and not self.up_complete:
                (arrs, _), = _comm_only("gather_up_last", [_gather_rider(
                    {i: self.w[i]}, [("d2d_diag", i, up_b), ("d2d", i, up_c)], (12, "sibling"))])
                self.w[i], self.up_complete = arrs[0], True
            full = self.w[i]
            return full.reshape(-1, full.shape[-1]) if name in ("w_out", "w_down") else full

        def _gather(self, ws, ops, handshake):
            return _gather_rider({w: self.w[w] for w in ws}, ops, handshake), ("w", ws)

        def _pair(self, ws):
            return _pair_rider(ws, [self.g16[w] for w in ws]), ("recv_a", ws)

        def _chip(self, ws):
            return _chip_rider(ws, [self.p16[w] for w in ws]), ("recv_b", ws)

        def plan(self, call):
            out_all, down_all = _whole_half(w_out_i), _whole_half(w_down_i)
            if call == "proj_pool":
                return [self._gather([w_out_i, w_up_i, w_down_i],
                                     [("ici", w_out_i, out_all), ("nbr", w_down_i, down_all),
                                      ("nbr", w_up_i, up_a)], (9, "chips"))]
            if call == "retention_fwd":
                return [self._gather([w_out_i, w_up_i, w_down_i],
                                     [("d2d", w_out_i, out_all),
                                      ("relay", w_down_i, down_all), ("d2d_nbr", w_down_i, down_all),
                                      ("relay", w_up_i, up_a), ("d2d_nbr", w_up_i, up_a), ("nbr", w_up_i, up_b)],
                                     (10, "both"))]
            if call == "outproj_ln1":
                return [self._gather([w_up_i, w_down_i],
                                     [("d2d_diag", w_down_i, down_all), ("d2d_diag", w_up_i, up_a),
                                      ("relay", w_up_i, up_b), ("d2d_nbr", w_up_i, up_b), ("ici", w_up_i, up_c)],
                                     (11, "both"))]
            return []

        def after(self, call):
            return tuple(self.tokens.pop(call, ()))

        def riders(self, call):
            self.pending = self.plan(call)
            return [r for r, _ in self.pending]

        def _start(self, name, rider, before, handshake):
            state, token = _split_start(name, rider, handshake)
            self.tokens.setdefault(before, []).append(token)
            return state

        def _finish_pair(self, name, state, ws, after):
            _, lands = _split_wait(name, state, after)
            self._finish_sum(ws, lands)

        def landed(self, call, results, outs):
            for (_, (slot, ws)), (inplace, lands) in zip(self.pending, results):
                for w, arr in zip(ws, inplace if len(inplace) else lands):
                    getattr(self, slot)[w] = arr
            if call == "wgrad_out":
                self._finish_pair("pair_exchange_up_wait", self.pair_up, [w_up_i], outs[1])
                self.chips.append(([w_up_i], self._start(
                    "chip_exchange_up_start", self._chip([w_up_i])[0], "wgrad_down", (5, "chips"))))
            if call == "mix_bwd":
                ws = [w_out_i, w_down_i]
                self._finish_pair("pair_exchange_out_down_wait", self.pair_out_down, ws, outs[0])

        def small_gradients(self, small, packed):
            dcw4 = jnp.transpose(small["conv_w"].reshape(3, N_SHARD, DOWN_SH), (1, 0, 2))
            own = [small["w_pool"], packed, dcw4]
            ws = [w_out_i, w_down_i]
            parts = [self._chip(ws)[0], _small_all_rider(own)]
            chip, self.small_all = _split_parts(
                self._start("chip_out_down_small_all_start", _merged_rider(parts), "retention_bwd",
                            (7, "all")), parts)
            self.chips.append((ws, chip))

        def gradient(self, name, g32, g16):
            w = order.index(name)
            shape = (N_SHARD,) + SHARD_SHAPES[w]
            self.g32[w], self.g16[w] = g32.reshape(shape), g16.reshape(shape)
            if name == "w_up":
                self.pair_up = self._start("pair_exchange_up_start", self._pair([w])[0], "wgrad_out",
                                           (1, "sibling"))
            if name == "w_down":
                self.pair_out_down = self._start("pair_exchange_out_down_start",
                                                 self._pair([w_out_i, w_down_i])[0], "mix_bwd", (2, "sibling"))

        def wgrad_in(self, xb, dproj):
            w = w_in_i
            g32, landed = _wgrad_send(xb, dproj, IN_SH, "wgrad_in", 4, after=self.after("wgrad_in"))
            self.g32[w] = g32
            self._finish_sum([w], [landed])
            self.chips.append(([w], self._start("chip_exchange_in_start", self._chip([w])[0], "dx", (8, "chips"))))

        def _finish_sum(self, ws, lands):
            p32s, p16s = _pair_sum(pos, ws, [self.g32[w] for w in ws], lands)
            for w, p32, p16 in zip(ws, p32s, p16s):
                self.p32[w], self.p16[w] = p32, p16

        def finish(self, after):
            for n, (ws, state) in enumerate(self.chips):
                _, lands = _split_wait("chip_exchange_wait_%d" % n, state, after)
                for w, arr in zip(ws, lands):
                    self.recv_b[w] = arr
            own, sent = _split_wait("small_all_wait", self.small_all, after)
            return list(sent) + list(own)

    comm = MeshComm()
    loss, grad_x, small = _local_step(x[0], loss_target[0], cw_full, conv_b, gathered[N_BIG + 1], pool_scale,
                                      ln1_g, ln1_b, ln2_g, ln2_b, comm)

    given = dict(w_pool=w_pool, pool_scale=pool_scale, ln1_g=ln1_g, ln1_b=ln1_b, conv_w=conv_w, conv_b=conv_b,
                 ln2_g=ln2_g, ln2_b=ln2_b)
    given_m = dict(w_pool=m_w_pool, pool_scale=m_pool_scale, ln1_g=m_ln1_g, ln1_b=m_ln1_b, conv_w=m_conv_w,
                   conv_b=m_conv_b, ln2_g=m_ln2_g, ln2_b=m_ln2_b)
    given_v = dict(w_pool=v_w_pool, pool_scale=v_pool_scale, ln1_g=v_ln1_g, ln1_b=v_ln1_b, conv_w=v_conv_w,
                   conv_b=v_conv_b, ln2_g=v_ln2_g, ln2_b=v_ln2_b)
    args = []
    for src in (given, given_m, given_v):
        args += [src["w_pool"][0], taps_first(src["conv_w"]), [src[n] for n in vec_names]]
    small_sums = comm.finish(grad_x)
    loss_tot, small_out = _small_adam(*small_sums, *args)
    every = range(N_BIG)
    mine = _chip_sum([comm.p32[w] for w in every], [comm.recv_b[w] for w in every])
    final_state, _ = _split_start("pair_exchange_f32_start", _final_rider(mine), (3, "sibling"))
    mine = final_state[2][:N_BIG]
    big = ([w_in, w_out, w_up, w_down], [m_w_in, m_w_out, m_w_up, m_w_down], [v_w_in, v_w_out, v_w_up, v_w_down])
    own_half = _adam_half("adam_own_half", pos, mine, *big)
    _, theirs = _split_wait("pair_exchange_f32_wait", final_state, own_half[0][0])
    big_out = _adam_half("adam_other_half", pos, theirs, *big, into=own_half)

    names = ("w_in", "w_pool", "pool_scale", "w_out", "ln1_g", "ln1_b", "w_up", "conv_w", "conv_b", "w_down",
             "ln2_g", "ln2_b")
    small_names = ("w_pool", "conv_w") + vec_names
    result = [loss_tot.reshape(()), grad_x[None]]
    for kind in range(4):
        for n in names:
            if n in order:
                result.append(big_out[kind][order.index(n)])
            else:
                val = small_out[kind][small_names.index(n)]
                if n == "conv_w":
                    val = taps_first(val)
                elif n == "w_pool":
                    val = val[None]
                result.append(val)
    return tuple(result)
```
